```python
import math
import jax, jax.numpy as jnp
from jax import lax
import numpy as np

D_MODEL = 1024
BATCH = 8
SEQ = 4096
DEPTH = 1

PLE_DIM = 256
DN_HEADS = 8
DN_DK = 128
DN_DV = 128
DN_CONV = 4
DN_CHUNK = 64
DN_QK_W = DN_HEADS * DN_DK
DN_V_W = DN_HEADS * DN_DV
DN_CONV_CH = 2 * DN_QK_W + DN_V_W
MLA_HEADS = 8
MLA_Q_LORA = 384
MLA_KV_LORA = 256
MLA_NOPE = 128
MLA_ROPE = 64
MLA_V = 128
MLA_V_W = MLA_HEADS * MLA_V
ROPE_BASE = 10000.0
Q_BLOCK = 128
FFN_HIDDEN = -(-8 * D_MODEL // (3 * 256)) * 256
DEEPNORM_ALPHA = (2.0 * DEPTH) ** 0.25
DEEPNORM_BETA = (8.0 * DEPTH) ** -0.25
IN_SIZES = (DN_CONV_CH, DN_V_W, DN_HEADS, DN_HEADS, MLA_Q_LORA, MLA_KV_LORA, MLA_ROPE, D_MODEL, D_MODEL)
D_IN = sum(IN_SIZES)
SPLIT_IDX = tuple(int(v) for v in np.cumsum(IN_SIZES)[:-1])
NEG_BIG = -1e30

kernel_name = 'hybrid_deltanet_mla_deepnorm_block'


def layer_norm(t, g, b, eps=1e-5):
    tf = t.astype(jnp.float32)
    mu = jnp.mean(tf, axis=-1, keepdims=True)
    var = jnp.mean(jnp.square(tf - mu), axis=-1, keepdims=True)
    return ((tf - mu) * lax.rsqrt(var + eps) * g.astype(jnp.float32) + b.astype(jnp.float32)).astype(t.dtype)


def rms_norm(t, w, eps=1e-6):
    tf = t.astype(jnp.float32)
    return tf * lax.rsqrt(jnp.mean(jnp.square(tf), axis=-1, keepdims=True) + eps) * w.astype(jnp.float32)


def l2_normalize(t, eps=1e-6):
    tf = t.astype(jnp.float32)
    return tf * lax.rsqrt(jnp.sum(jnp.square(tf), axis=-1, keepdims=True) + eps)


def rope_tables(positions):
    inv_freq = ROPE_BASE ** (-jnp.arange(0, MLA_ROPE, 2, dtype=jnp.float32) / MLA_ROPE)
    ang = positions.astype(jnp.float32)[..., None] * inv_freq
    return jnp.cos(ang), jnp.sin(ang)


def apply_rope(t, cos, sin):
    t1, t2 = jnp.split(t.astype(jnp.float32), 2, axis=-1)
    return jnp.concatenate([t1 * cos - t2 * sin, t2 * cos + t1 * sin], axis=-1)


def causal_depthwise_conv(t, w):
    width, ch = w.shape
    return lax.conv_general_dilated(t, w[:, None, :].astype(t.dtype), window_strides=(1,), padding=[(width - 1, 0)], dimension_numbers=('NWC', 'WIO', 'NWC'), feature_group_count=ch)


def gated_delta_rule(q, k, v, beta, g):
    b, s, h, dk = q.shape
    dv = v.shape[-1]
    c = DN_CHUNK
    n = s // c

    def chunk(t):
        return jnp.swapaxes(t.reshape((b, n, c) + t.shape[2:]), 2, 3)

    q, k, v, beta, g = (chunk(t) for t in (q, k, v, beta, g))
    g = jnp.cumsum(g, axis=-1)
    tril = jnp.tril(jnp.ones((c, c), dtype=bool))
    strict = jnp.tril(jnp.ones((c, c), dtype=bool), k=-1)
    diff = g[..., :, None] - g[..., None, :]
    decay = jnp.where(tril, jnp.exp(jnp.where(tril, diff, 0.0)), 0.0)
    k_beta = k * beta[..., None]
    l_mat = jnp.where(strict, jnp.einsum('bnhid,bnhjd->bnhij', k_beta, k) * decay, 0.0)
    eye = jnp.eye(c, dtype=q.dtype)
    t_inv = lax.linalg.triangular_solve(eye + l_mat, jnp.broadcast_to(eye, l_mat.shape), left_side=True, lower=True, unit_diagonal=True)
    u = jnp.einsum('bnhij,bnhje->bnhie', t_inv, v * beta[..., None])
    w = jnp.einsum('bnhij,bnhjd->bnhid', t_inv, k_beta * jnp.exp(g)[..., None])
    intra = jnp.where(tril, jnp.einsum('bnhid,bnhjd->bnhij', q, k) * decay, 0.0)
    q_dec = q * jnp.exp(g)[..., None]
    g_last = g[..., -1]
    k_tail = k * jnp.exp(g_last[..., None] - g)[..., None]

    def step(state, xs):
        w_c, u_c, q_c, a_c, kt_c, gl_c = xs
        v_new = u_c - jnp.einsum('bhcd,bhde->bhce', w_c, state)
        o_c = jnp.einsum('bhcd,bhde->bhce', q_c, state) + jnp.einsum('bhij,bhje->bhie', a_c, v_new)
        state = state * jnp.exp(gl_c)[..., None, None] + jnp.einsum('bhcd,bhce->bhde', kt_c, v_new)
        return state, o_c

    xs = tuple(jnp.moveaxis(t, 1, 0) for t in (w, u, q_dec, intra, k_tail, g_last))
    state0 = jnp.zeros((b, h, dk, dv), q.dtype)
    _, o = lax.scan(step, state0, xs)
    return jnp.transpose(o, (1, 0, 3, 2, 4)).reshape(b, s, h, dv)


def mla_attention(q_lat, q_rope, c_kv, k_rope):
    b, s, h, c = q_lat.shape
    nblk = s // Q_BLOCK
    scale = (MLA_NOPE + MLA_ROPE) ** -0.5
    ckv = c_kv.astype(jnp.float32)
    kr = k_rope.astype(jnp.float32)
    key_idx = jnp.arange(s)

    def blocks(t):
        return jnp.swapaxes(t.reshape((b, nblk, Q_BLOCK) + t.shape[2:]), 0, 1)

    def one_block(args):
        ql, qr, blk = args
        sc = (jnp.einsum('bqhc,bkc->bhqk', ql, ckv) + jnp.einsum('bqhr,bkr->bhqk', qr, kr)) * scale
        q_idx = blk * Q_BLOCK + jnp.arange(Q_BLOCK)
        sc = jnp.where(key_idx[None, :] <= q_idx[:, None], sc, NEG_BIG)
        pr = jax.nn.softmax(sc, axis=-1)
        return jnp.einsum('bhqk,bkc->bqhc', pr, ckv)

    out = lax.map(one_block, (blocks(q_lat.astype(jnp.float32)), blocks(q_rope.astype(jnp.float32)), jnp.arange(nblk)))
    return jnp.swapaxes(out, 0, 1).reshape(b, s, h, c)


def _fwd_setup_inputs(seed: int = 0) -> dict:
    key = jax.random.key(seed)
    ks = jax.random.split(key, 32)

    def nrm(k, shape, scale):
        return jax.random.normal(k, shape, jnp.float32) * scale

    x = nrm(ks[0], (BATCH, SEQ, D_MODEL), 1.0)
    p = nrm(ks[1], (DEPTH, BATCH, SEQ, PLE_DIM), 1.0)
    positions = jax.random.randint(ks[2], (BATCH, 1), 0, 1024, dtype=jnp.int32) + jnp.arange(SEQ, dtype=jnp.int32)[None, :]
    w_in = nrm(ks[3], (DEPTH, D_MODEL, D_IN), D_MODEL ** -0.5)
    conv_w = nrm(ks[4], (DEPTH, DN_CONV, DN_CONV_CH), DN_CONV ** -0.5)
    dn_a_log = jnp.log(jax.random.uniform(ks[5], (DEPTH, DN_HEADS), jnp.float32, 1.0, 16.0))
    dt = jnp.exp(jax.random.uniform(ks[6], (DEPTH, DN_HEADS), jnp.float32, math.log(1e-3), math.log(1e-1)))
    dn_dt_bias = dt + jnp.log(-jnp.expm1(-dt))
    dn_norm_w = 1.0 + nrm(ks[7], (DEPTH, DN_DV), 0.01)
    q_norm_w = 1.0 + nrm(ks[8], (DEPTH, MLA_Q_LORA), 0.01)
    w_uq = nrm(ks[9], (DEPTH, MLA_Q_LORA, MLA_HEADS, MLA_NOPE + MLA_ROPE), MLA_Q_LORA ** -0.5)
    kv_norm_w = 1.0 + nrm(ks[10], (DEPTH, MLA_KV_LORA), 0.01)
    w_uk = nrm(ks[11], (DEPTH, MLA_KV_LORA, MLA_HEADS, MLA_NOPE), MLA_KV_LORA ** -0.5)
    w_uv = nrm(ks[12], (DEPTH, MLA_KV_LORA, MLA_HEADS, MLA_V), MLA_KV_LORA ** -0.5)
    w_br_dn = nrm(ks[13], (DEPTH, DN_V_W, D_MODEL), DN_V_W ** -0.5)
    w_br_mla = nrm(ks[14], (DEPTH, MLA_V_W, D_MODEL), MLA_V_W ** -0.5)
    w_o = nrm(ks[15], (DEPTH, D_MODEL, D_MODEL), D_MODEL ** -0.5 * DEEPNORM_BETA)
    ln1_g = 1.0 + nrm(ks[16], (DEPTH, D_MODEL), 0.01)
    ln1_b = nrm(ks[17], (DEPTH, D_MODEL), 0.01)
    w_ffn_in = nrm(ks[18], (DEPTH, D_MODEL, 2 * FFN_HIDDEN), D_MODEL ** -0.5)
    w_ffn_out = nrm(ks[19], (DEPTH, FFN_HIDDEN, D_MODEL), FFN_HIDDEN ** -0.5 * DEEPNORM_BETA)
    w_ple = nrm(ks[20], (DEPTH, PLE_DIM, D_MODEL), PLE_DIM ** -0.5 * DEEPNORM_BETA)
    w_ple_gate = nrm(ks[21], (DEPTH, D_MODEL, D_MODEL), D_MODEL ** -0.5)
    ln2_g = 1.0 + nrm(ks[22], (DEPTH, D_MODEL), 0.01)
    ln2_b = nrm(ks[23], (DEPTH, D_MODEL), 0.01)
    return {'x': x, 'p': p, 'positions': positions, 'w_in': w_in, 'conv_w': conv_w, 'dn_a_log': dn_a_log, 'dn_dt_bias': dn_dt_bias, 'dn_norm_w': dn_norm_w, 'q_norm_w': q_norm_w, 'w_uq': w_uq, 'kv_norm_w': kv_norm_w, 'w_uk': w_uk, 'w_uv': w_uv, 'w_br_dn': w_br_dn, 'w_br_mla': w_br_mla, 'w_o': w_o, 'ln1_g': ln1_g, 'ln1_b': ln1_b, 'w_ffn_in': w_ffn_in, 'w_ffn_out': w_ffn_out, 'w_ple': w_ple, 'w_ple_gate': w_ple_gate, 'ln2_g': ln2_g, 'ln2_b': ln2_b}


def _fwd_reference(x, p, positions, w_in, conv_w, dn_a_log, dn_dt_bias, dn_norm_w, q_norm_w, w_uq, kv_norm_w, w_uk, w_uv, w_br_dn, w_br_mla, w_o, ln1_g, ln1_b, w_ffn_in, w_ffn_out, w_ple, w_ple_gate, ln2_g, ln2_b):
    b, s, _ = x.shape
    cos, sin = rope_tables(positions)
    h = x
    for i in range(DEPTH):
        proj = h @ w_in[i]
        qkv, z, b_raw, a_raw, cq, ckv, kr, gate_dn, gate_mla = jnp.split(proj, SPLIT_IDX, axis=-1)

        qkv = jax.nn.silu(causal_depthwise_conv(qkv, conv_w[i]))
        dq, dk, dv = jnp.split(qkv, [DN_QK_W, 2 * DN_QK_W], axis=-1)
        dq = l2_normalize(dq.reshape(b, s, DN_HEADS, DN_DK)) * (DN_DK ** -0.5)
        dk = l2_normalize(dk.reshape(b, s, DN_HEADS, DN_DK))
        dv = dv.reshape(b, s, DN_HEADS, DN_DV).astype(jnp.float32)
        beta = jax.nn.sigmoid(b_raw.astype(jnp.float32))
        g = -jnp.exp(dn_a_log[i].astype(jnp.float32)) * jax.nn.softplus(a_raw.astype(jnp.float32) + dn_dt_bias[i].astype(jnp.float32))
        o_dn = gated_delta_rule(dq, dk, dv, beta, g)
        o_dn = rms_norm(o_dn, dn_norm_w[i]) * jax.nn.silu(z.reshape(b, s, DN_HEADS, DN_DV).astype(jnp.float32))
        y_dn = o_dn.reshape(b, s, DN_V_W).astype(h.dtype) @ w_br_dn[i]

        c_q = rms_norm(cq, q_norm_w[i]).astype(h.dtype)
        q_full = jnp.einsum('bsc,chd->bshd', c_q, w_uq[i])
        q_nope, q_rope = jnp.split(q_full, [MLA_NOPE], axis=-1)
        q_rope = apply_rope(q_rope, cos[:, :, None, :], sin[:, :, None, :])
        c_kv = rms_norm(ckv, kv_norm_w[i])
        k_rope = apply_rope(kr, cos, sin)
        q_lat = jnp.einsum('bshd,chd->bshc', q_nope.astype(jnp.float32), w_uk[i].astype(jnp.float32))
        out_lat = mla_attention(q_lat, q_rope, c_kv, k_rope)
        o_mla = jnp.einsum('bshc,chd->bshd', out_lat, w_uv[i].astype(jnp.float32))
        y_mla = o_mla.reshape(b, s, MLA_V_W).astype(h.dtype) @ w_br_mla[i]

        mixed = jax.nn.sigmoid(gate_dn) * y_dn + jax.nn.sigmoid(gate_mla) * y_mla
        h = layer_norm(DEEPNORM_ALPHA * h + mixed @ w_o[i], ln1_g[i], ln1_b[i])

        gt, up = jnp.split(h @ w_ffn_in[i], 2, axis=-1)
        ffn = (jax.nn.silu(gt) * up) @ w_ffn_out[i]
        ple = jax.nn.sigmoid(h @ w_ple_gate[i]) * (p[i] @ w_ple[i])
        h = layer_norm(DEEPNORM_ALPHA * h + ffn + ple, ln2_g[i], ln2_b[i])
    return h


import jax as _jax
import jax.numpy as _jnp

TWIN_FORMAT = 'train_step'
FWD_PARAMS = ['x', 'p', 'positions', 'w_in', 'conv_w', 'dn_a_log', 'dn_dt_bias', 'dn_norm_w', 'q_norm_w', 'w_uq', 'kv_norm_w', 'w_uk', 'w_uv', 'w_br_dn', 'w_br_mla', 'w_o', 'ln1_g', 'ln1_b', 'w_ffn_in', 'w_ffn_out', 'w_ple', 'w_ple_gate', 'ln2_g', 'ln2_b']
TWIN_WEIGHTS = ['w_in', 'conv_w', 'dn_a_log', 'dn_dt_bias', 'dn_norm_w', 'q_norm_w', 'w_uq', 'kv_norm_w', 'w_uk', 'w_uv', 'w_br_dn', 'w_br_mla', 'w_o', 'ln1_g', 'ln1_b', 'w_ffn_in', 'w_ffn_out', 'w_ple', 'w_ple_gate', 'ln2_g', 'ln2_b']
TWIN_DIFF_INPUT = 'x'
TWIN_INPUTS = ['x', 'p', 'positions', 'w_in', 'conv_w', 'dn_a_log', 'dn_dt_bias', 'dn_norm_w', 'q_norm_w', 'w_uq', 'kv_norm_w', 'w_uk', 'w_uv', 'w_br_dn', 'w_br_mla', 'w_o', 'ln1_g', 'ln1_b', 'w_ffn_in', 'w_ffn_out', 'w_ple', 'w_ple_gate', 'ln2_g', 'ln2_b', 'loss_target', 'm_w_in', 'm_conv_w', 'm_dn_a_log', 'm_dn_dt_bias', 'm_dn_norm_w', 'm_q_norm_w', 'm_w_uq', 'm_kv_norm_w', 'm_w_uk', 'm_w_uv', 'm_w_br_dn', 'm_w_br_mla', 'm_w_o', 'm_ln1_g', 'm_ln1_b', 'm_w_ffn_in', 'm_w_ffn_out', 'm_w_ple', 'm_w_ple_gate', 'm_ln2_g', 'm_ln2_b', 'v_w_in', 'v_conv_w', 'v_dn_a_log', 'v_dn_dt_bias', 'v_dn_norm_w', 'v_q_norm_w', 'v_w_uq', 'v_kv_norm_w', 'v_w_uk', 'v_w_uv', 'v_w_br_dn', 'v_w_br_mla', 'v_w_o', 'v_ln1_g', 'v_ln1_b', 'v_w_ffn_in', 'v_w_ffn_out', 'v_w_ple', 'v_w_ple_gate', 'v_ln2_g', 'v_ln2_b']
TWIN_OUTPUTS = ['loss', 'grad_x', 'grad_w_in', 'grad_conv_w', 'grad_dn_a_log', 'grad_dn_dt_bias', 'grad_dn_norm_w', 'grad_q_norm_w', 'grad_w_uq', 'grad_kv_norm_w', 'grad_w_uk', 'grad_w_uv', 'grad_w_br_dn', 'grad_w_br_mla', 'grad_w_o', 'grad_ln1_g', 'grad_ln1_b', 'grad_w_ffn_in', 'grad_w_ffn_out', 'grad_w_ple', 'grad_w_ple_gate', 'grad_ln2_g', 'grad_ln2_b', 'delta_w_in', 'delta_conv_w', 'delta_dn_a_log', 'delta_dn_dt_bias', 'delta_dn_norm_w', 'delta_q_norm_w', 'delta_w_uq', 'delta_kv_norm_w', 'delta_w_uk', 'delta_w_uv', 'delta_w_br_dn', 'delta_w_br_mla', 'delta_w_o', 'delta_ln1_g', 'delta_ln1_b', 'delta_w_ffn_in', 'delta_w_ffn_out', 'delta_w_ple', 'delta_w_ple_gate', 'delta_ln2_g', 'delta_ln2_b', 'new_m_w_in', 'new_m_conv_w', 'new_m_dn_a_log', 'new_m_dn_dt_bias', 'new_m_dn_norm_w', 'new_m_q_norm_w', 'new_m_w_uq', 'new_m_kv_norm_w', 'new_m_w_uk', 'new_m_w_uv', 'new_m_w_br_dn', 'new_m_w_br_mla', 'new_m_w_o', 'new_m_ln1_g', 'new_m_ln1_b', 'new_m_w_ffn_in', 'new_m_w_ffn_out', 'new_m_w_ple', 'new_m_w_ple_gate', 'new_m_ln2_g', 'new_m_ln2_b', 'new_v_w_in', 'new_v_conv_w', 'new_v_dn_a_log', 'new_v_dn_dt_bias', 'new_v_dn_norm_w', 'new_v_q_norm_w', 'new_v_w_uq', 'new_v_kv_norm_w', 'new_v_w_uk', 'new_v_w_uv', 'new_v_w_br_dn', 'new_v_w_br_mla', 'new_v_w_o', 'new_v_ln1_g', 'new_v_ln1_b', 'new_v_w_ffn_in', 'new_v_w_ffn_out', 'new_v_w_ple', 'new_v_w_ple_gate', 'new_v_ln2_g', 'new_v_ln2_b']
TWIN_LEAF_KINDS = {'loss': 'loss', 'grad_x': 'grad_x', 'grad_w_in': 'grad_w', 'grad_conv_w': 'grad_w', 'grad_dn_a_log': 'grad_w', 'grad_dn_dt_bias': 'grad_w', 'grad_dn_norm_w': 'grad_w', 'grad_q_norm_w': 'grad_w', 'grad_w_uq': 'grad_w', 'grad_kv_norm_w': 'grad_w', 'grad_w_uk': 'grad_w', 'grad_w_uv': 'grad_w', 'grad_w_br_dn': 'grad_w', 'grad_w_br_mla': 'grad_w', 'grad_w_o': 'grad_w', 'grad_ln1_g': 'grad_w', 'grad_ln1_b': 'grad_w', 'grad_w_ffn_in': 'grad_w', 'grad_w_ffn_out': 'grad_w', 'grad_w_ple': 'grad_w', 'grad_w_ple_gate': 'grad_w', 'grad_ln2_g': 'grad_w', 'grad_ln2_b': 'grad_w', 'delta_w_in': 'delta_w', 'delta_conv_w': 'delta_w', 'delta_dn_a_log': 'delta_w', 'delta_dn_dt_bias': 'delta_w', 'delta_dn_norm_w': 'delta_w', 'delta_q_norm_w': 'delta_w', 'delta_w_uq': 'delta_w', 'delta_kv_norm_w': 'delta_w', 'delta_w_uk': 'delta_w', 'delta_w_uv': 'delta_w', 'delta_w_br_dn': 'delta_w', 'delta_w_br_mla': 'delta_w', 'delta_w_o': 'delta_w', 'delta_ln1_g': 'delta_w', 'delta_ln1_b': 'delta_w', 'delta_w_ffn_in': 'delta_w', 'delta_w_ffn_out': 'delta_w', 'delta_w_ple': 'delta_w', 'delta_w_ple_gate': 'delta_w', 'delta_ln2_g': 'delta_w', 'delta_ln2_b': 'delta_w', 'new_m_w_in': 'new_m', 'new_m_conv_w': 'new_m', 'new_m_dn_a_log': 'new_m', 'new_m_dn_dt_bias': 'new_m', 'new_m_dn_norm_w': 'new_m', 'new_m_q_norm_w': 'new_m', 'new_m_w_uq': 'new_m', 'new_m_kv_norm_w': 'new_m', 'new_m_w_uk': 'new_m', 'new_m_w_uv': 'new_m', 'new_m_w_br_dn': 'new_m', 'new_m_w_br_mla': 'new_m', 'new_m_w_o': 'new_m', 'new_m_ln1_g': 'new_m', 'new_m_ln1_b': 'new_m', 'new_m_w_ffn_in': 'new_m', 'new_m_w_ffn_out': 'new_m', 'new_m_w_ple': 'new_m', 'new_m_w_ple_gate': 'new_m', 'new_m_ln2_g': 'new_m', 'new_m_ln2_b': 'new_m', 'new_v_w_in': 'new_v', 'new_v_conv_w': 'new_v', 'new_v_dn_a_log': 'new_v', 'new_v_dn_dt_bias': 'new_v', 'new_v_dn_norm_w': 'new_v', 'new_v_q_norm_w': 'new_v', 'new_v_w_uq': 'new_v', 'new_v_kv_norm_w': 'new_v', 'new_v_w_uk': 'new_v', 'new_v_w_uv': 'new_v', 'new_v_w_br_dn': 'new_v', 'new_v_w_br_mla': 'new_v', 'new_v_w_o': 'new_v', 'new_v_ln1_g': 'new_v', 'new_v_ln1_b': 'new_v', 'new_v_w_ffn_in': 'new_v', 'new_v_w_ffn_out': 'new_v', 'new_v_w_ple': 'new_v', 'new_v_w_ple_gate': 'new_v', 'new_v_ln2_g': 'new_v', 'new_v_ln2_b': 'new_v'}


def _forward(args):
    return _fwd_reference(*[args[k] for k in FWD_PARAMS])


def _output_shape():
    def fwd():
        inp = _fwd_setup_inputs(0)
        return _fwd_reference(*[inp[k] for k in FWD_PARAMS])
    out = _jax.eval_shape(fwd)
    return out.shape, out.dtype

N_MICROBATCH = 1
ADAM_LR = 0.001
ADAM_B1 = 0.9
ADAM_B2 = 0.999
ADAM_EPS = 1e-08
ADAM_WD = 0.01
ADAM_STEP = 10
PER_EXAMPLE_BATCH_AXIS = {'x': 0, 'p': 1, 'positions': 0, 'loss_target': 0}
SHARED_INPUTS = []
_WEIGHT_DTYPES = {'w_in': _jnp.float32, 'conv_w': _jnp.float32, 'dn_a_log': _jnp.float32, 'dn_dt_bias': _jnp.float32, 'dn_norm_w': _jnp.float32, 'q_norm_w': _jnp.float32, 'w_uq': _jnp.float32, 'kv_norm_w': _jnp.float32, 'w_uk': _jnp.float32, 'w_uv': _jnp.float32, 'w_br_dn': _jnp.float32, 'w_br_mla': _jnp.float32, 'w_o': _jnp.float32, 'ln1_g': _jnp.float32, 'ln1_b': _jnp.float32, 'w_ffn_in': _jnp.float32, 'w_ffn_out': _jnp.float32, 'w_ple': _jnp.float32, 'w_ple_gate': _jnp.float32, 'ln2_g': _jnp.float32, 'ln2_b': _jnp.float32}
MOMENT_SCALE = {'w_in': 1.979670e-02, 'conv_w': 2.231386e-02, 'dn_a_log': 2.421492e-01, 'dn_dt_bias': 2.346932e-01, 'dn_norm_w': 7.785858e-02, 'q_norm_w': 1.421055e-02, 'w_uq': 7.089751e-03, 'kv_norm_w': 2.715638e-02, 'w_uk': 7.216097e-03, 'w_uv': 1.060810e-02, 'w_br_dn': 2.815893e-02, 'w_br_mla': 1.061743e-02, 'w_o': 4.941029e-02, 'ln1_g': 4.774231e-01, 'ln1_b': 2.872173e-01, 'w_ffn_in': 3.001621e-02, 'w_ffn_out': 8.236101e-02, 'w_ple': 7.467033e-02, 'w_ple_gate': 1.730190e-02, 'ln2_g': 3.199438e+01, 'ln2_b': 8.394686e-01}


def _to_microbatches(a, axis):
    t = _jnp.moveaxis(a, axis, 0)
    t = t.reshape((N_MICROBATCH, t.shape[0] // N_MICROBATCH) + t.shape[1:])
    return _jnp.moveaxis(t, 1, axis + 1)


def setup_inputs(seed: int = 0) -> dict:
    inp = _fwd_setup_inputs(seed)
    key = _jax.random.fold_in(_jax.random.key(seed), 7919)
    shape, _ = _output_shape()
    out = dict(inp)
    out["loss_target"] = _jax.random.normal(_jax.random.fold_in(key, 0), shape, _jnp.float32)
    for i, name in enumerate(TWIN_WEIGHTS):
        w = inp[name].astype(_jnp.float32)
        if MOMENT_SCALE is None:
            s = _jnp.sqrt(_jnp.mean(_jnp.square(w)) + 1e-30)
        else:
            s = MOMENT_SCALE[name]
        km, kv = _jax.random.split(_jax.random.fold_in(key, i + 1))
        out[name] = w
        out["m_" + name] = s * _jax.random.normal(km, w.shape, _jnp.float32)
        out["v_" + name] = (s * s) * _jax.random.uniform(kv, w.shape, _jnp.float32, 0.5, 1.5)
    if N_MICROBATCH > 1:
        for name, axis in PER_EXAMPLE_BATCH_AXIS.items():
            out[name] = _to_microbatches(out[name], axis)
    return {'x': out['x'], 'p': out['p'], 'positions': out['positions'], 'w_in': out['w_in'], 'conv_w': out['conv_w'], 'dn_a_log': out['dn_a_log'], 'dn_dt_bias': out['dn_dt_bias'], 'dn_norm_w': out['dn_norm_w'], 'q_norm_w': out['q_norm_w'], 'w_uq': out['w_uq'], 'kv_norm_w': out['kv_norm_w'], 'w_uk': out['w_uk'], 'w_uv': out['w_uv'], 'w_br_dn': out['w_br_dn'], 'w_br_mla': out['w_br_mla'], 'w_o': out['w_o'], 'ln1_g': out['ln1_g'], 'ln1_b': out['ln1_b'], 'w_ffn_in': out['w_ffn_in'], 'w_ffn_out': out['w_ffn_out'], 'w_ple': out['w_ple'], 'w_ple_gate': out['w_ple_gate'], 'ln2_g': out['ln2_g'], 'ln2_b': out['ln2_b'], 'loss_target': out['loss_target'], 'm_w_in': out['m_w_in'], 'm_conv_w': out['m_conv_w'], 'm_dn_a_log': out['m_dn_a_log'], 'm_dn_dt_bias': out['m_dn_dt_bias'], 'm_dn_norm_w': out['m_dn_norm_w'], 'm_q_norm_w': out['m_q_norm_w'], 'm_w_uq': out['m_w_uq'], 'm_kv_norm_w': out['m_kv_norm_w'], 'm_w_uk': out['m_w_uk'], 'm_w_uv': out['m_w_uv'], 'm_w_br_dn': out['m_w_br_dn'], 'm_w_br_mla': out['m_w_br_mla'], 'm_w_o': out['m_w_o'], 'm_ln1_g': out['m_ln1_g'], 'm_ln1_b': out['m_ln1_b'], 'm_w_ffn_in': out['m_w_ffn_in'], 'm_w_ffn_out': out['m_w_ffn_out'], 'm_w_ple': out['m_w_ple'], 'm_w_ple_gate': out['m_w_ple_gate'], 'm_ln2_g': out['m_ln2_g'], 'm_ln2_b': out['m_ln2_b'], 'v_w_in': out['v_w_in'], 'v_conv_w': out['v_conv_w'], 'v_dn_a_log': out['v_dn_a_log'], 'v_dn_dt_bias': out['v_dn_dt_bias'], 'v_dn_norm_w': out['v_dn_norm_w'], 'v_q_norm_w': out['v_q_norm_w'], 'v_w_uq': out['v_w_uq'], 'v_kv_norm_w': out['v_kv_norm_w'], 'v_w_uk': out['v_w_uk'], 'v_w_uv': out['v_w_uv'], 'v_w_br_dn': out['v_w_br_dn'], 'v_w_br_mla': out['v_w_br_mla'], 'v_w_o': out['v_w_o'], 'v_ln1_g': out['v_ln1_g'], 'v_ln1_b': out['v_ln1_b'], 'v_w_ffn_in': out['v_w_ffn_in'], 'v_w_ffn_out': out['v_w_ffn_out'], 'v_w_ple': out['v_w_ple'], 'v_w_ple_gate': out['v_w_ple_gate'], 'v_ln2_g': out['v_ln2_g'], 'v_ln2_b': out['v_ln2_b']}


def _loss(weights, diff, rest, loss_target):
    with _jax.named_scope("forward"):
        args = {**rest, TWIN_DIFF_INPUT: diff, **{k: w.astype(_WEIGHT_DTYPES[k]) for k, w in weights.items()}}
        y = _forward(args)
    with _jax.named_scope("loss_head"):
        err = _jnp.square(y.astype(_jnp.float32) - loss_target)
        return 0.5 * _jnp.sum(_jnp.mean(err, axis=-1)) if err.ndim else 0.5 * err


def _adamw(w, g, m, v):
    m = ADAM_B1 * m + (1.0 - ADAM_B1) * g
    v = ADAM_B2 * v + (1.0 - ADAM_B2) * _jnp.square(g)
    m_hat = m / (1.0 - ADAM_B1 ** ADAM_STEP)
    v_hat = v / (1.0 - ADAM_B2 ** ADAM_STEP)
    delta = -ADAM_LR * (m_hat / (_jnp.sqrt(v_hat) + ADAM_EPS) + ADAM_WD * w)
    return delta, m, v


def reference(x, p, positions, w_in, conv_w, dn_a_log, dn_dt_bias, dn_norm_w, q_norm_w, w_uq, kv_norm_w, w_uk, w_uv, w_br_dn, w_br_mla, w_o, ln1_g, ln1_b, w_ffn_in, w_ffn_out, w_ple, w_ple_gate, ln2_g, ln2_b, loss_target, m_w_in, m_conv_w, m_dn_a_log, m_dn_dt_bias, m_dn_norm_w, m_q_norm_w, m_w_uq, m_kv_norm_w, m_w_uk, m_w_uv, m_w_br_dn, m_w_br_mla, m_w_o, m_ln1_g, m_ln1_b, m_w_ffn_in, m_w_ffn_out, m_w_ple, m_w_ple_gate, m_ln2_g, m_ln2_b, v_w_in, v_conv_w, v_dn_a_log, v_dn_dt_bias, v_dn_norm_w, v_q_norm_w, v_w_uq, v_kv_norm_w, v_w_uk, v_w_uv, v_w_br_dn, v_w_br_mla, v_w_o, v_ln1_g, v_ln1_b, v_w_ffn_in, v_w_ffn_out, v_w_ple, v_w_ple_gate, v_ln2_g, v_ln2_b):
    given = dict(x=x, p=p, positions=positions, w_in=w_in, conv_w=conv_w, dn_a_log=dn_a_log, dn_dt_bias=dn_dt_bias, dn_norm_w=dn_norm_w, q_norm_w=q_norm_w, w_uq=w_uq, kv_norm_w=kv_norm_w, w_uk=w_uk, w_uv=w_uv, w_br_dn=w_br_dn, w_br_mla=w_br_mla, w_o=w_o, ln1_g=ln1_g, ln1_b=ln1_b, w_ffn_in=w_ffn_in, w_ffn_out=w_ffn_out, w_ple=w_ple, w_ple_gate=w_ple_gate, ln2_g=ln2_g, ln2_b=ln2_b, loss_target=loss_target, m_w_in=m_w_in, m_conv_w=m_conv_w, m_dn_a_log=m_dn_a_log, m_dn_dt_bias=m_dn_dt_bias, m_dn_norm_w=m_dn_norm_w, m_q_norm_w=m_q_norm_w, m_w_uq=m_w_uq, m_kv_norm_w=m_kv_norm_w, m_w_uk=m_w_uk, m_w_uv=m_w_uv, m_w_br_dn=m_w_br_dn, m_w_br_mla=m_w_br_mla, m_w_o=m_w_o, m_ln1_g=m_ln1_g, m_ln1_b=m_ln1_b, m_w_ffn_in=m_w_ffn_in, m_w_ffn_out=m_w_ffn_out, m_w_ple=m_w_ple, m_w_ple_gate=m_w_ple_gate, m_ln2_g=m_ln2_g, m_ln2_b=m_ln2_b, v_w_in=v_w_in, v_conv_w=v_conv_w, v_dn_a_log=v_dn_a_log, v_dn_dt_bias=v_dn_dt_bias, v_dn_norm_w=v_dn_norm_w, v_q_norm_w=v_q_norm_w, v_w_uq=v_w_uq, v_kv_norm_w=v_kv_norm_w, v_w_uk=v_w_uk, v_w_uv=v_w_uv, v_w_br_dn=v_w_br_dn, v_w_br_mla=v_w_br_mla, v_w_o=v_w_o, v_ln1_g=v_ln1_g, v_ln1_b=v_ln1_b, v_w_ffn_in=v_w_ffn_in, v_w_ffn_out=v_w_ffn_out, v_w_ple=v_w_ple, v_w_ple_gate=v_w_ple_gate, v_ln2_g=v_ln2_g, v_ln2_b=v_ln2_b)
    weights = {n: given[n] for n in TWIN_WEIGHTS}
    shared = {n: given[n] for n in SHARED_INPUTS}
    per_example = {n: given[n] for n in ['x', 'p', 'positions']}
    grad_fn = _jax.value_and_grad(_loss, argnums=(0, 1))

    def one_microbatch(ex, loss_target):
        ex = dict(ex)
        diff = ex.pop(TWIN_DIFF_INPUT)
        return grad_fn(weights, diff, {**shared, **ex}, loss_target)

    if N_MICROBATCH == 1:
        loss, (grad_w, grad_x) = one_microbatch(per_example, given["loss_target"])
    else:
        def body(carry, xs):
            loss_sum, grad_sum = carry
            l_k, (gw_k, gx_k) = one_microbatch(xs[0], xs[1])
            with _jax.named_scope("update"):
                return (loss_sum + l_k, _jax.tree.map(_jnp.add, grad_sum, gw_k)), gx_k

        init = (_jnp.zeros((), _jnp.float32), _jax.tree.map(_jnp.zeros_like, weights))
        (loss, grad_w), grad_x = _jax.lax.scan(body, init, (per_example, given["loss_target"]))
    with _jax.named_scope("update"):
        delta_w, new_m, new_v = {}, {}, {}
        for n in TWIN_WEIGHTS:
            delta_w[n], new_m[n], new_v[n] = _adamw(weights[n], grad_w[n], given["m_" + n], given["v_" + n])
    return (loss, grad_x, *[grad_w[n] for n in TWIN_WEIGHTS], *[delta_w[n] for n in TWIN_WEIGHTS],
            *[new_m[n] for n in TWIN_WEIGHTS], *[new_v[n] for n in TWIN_WEIGHTS])
```

```python
import functools
import math

import numpy as np
import jax
import jax.numpy as jnp
from jax import lax
from jax.experimental import pallas as pl
from jax.experimental.pallas import tpu as pltpu

F32 = jnp.float32
BF16 = jnp.bfloat16

D_MODEL = 1024
N_HEADS = 8
HEAD = 128
CHUNK = 64
GROUP = 256
ROPE = 64
Q_LORA = 384
KV_LORA = 256
FFN_HIDDEN = 2816
PLE_DIM = 256
ROPE_BASE = 10000.0
ALPHA = 2.0 ** 0.25
SCALE = float((HEAD + ROPE) ** -0.5)
NEG_BIG = -1e30
EPS_RMS = 1e-6
EPS_LN = 1e-5

ADAM_LR = 0.001
ADAM_B1 = 0.9
ADAM_B2 = 0.999
ADAM_EPS = 1e-08
ADAM_WD = 0.01
ADAM_STEP = 10

N_DEV = 8
LANES = 128
FLAT_COLS = 1024
FLAT_TILE = 128

WB_CQ, WB_CKV, WB_KR, WB_B, WB_A, WB_COLS = 0, 512, 768, 896, 1024, 1152

HIGHEST = lax.Precision.HIGHEST

NN = (((1,), (0,)), ((), ()))
TN = (((0,), (0,)), ((), ()))
NT = (((1,), (1,)), ((), ()))


def _dot(a, b, dims=NN):
    return lax.dot_general(a.astype(BF16), b.astype(BF16), dims, preferred_element_type=F32)


def _dot32(a, b, dims=NN):
    return lax.dot_general(a, b, dims, precision=HIGHEST, preferred_element_type=F32)


def _sig(x):
    return 1.0 / (1.0 + jnp.exp(-x))


def _pick(n, pref, unit):
    if n <= pref:
        return n
    t = (pref // unit) * unit
    while t >= unit:
        if n % t == 0:
            return t
        t -= unit
    return n


def _mm(a, b, *, ta=False, tb=False, add=(), out_dtype=F32, name):
    if ta:
        k_dim, m_dim = a.shape
    else:
        m_dim, k_dim = a.shape
    if tb:
        n_dim, k2 = b.shape
    else:
        k2, n_dim = b.shape
    assert k_dim == k2, (a.shape, b.shape, ta, tb)
    tm = _pick(m_dim, 512, LANES)
    tn = _pick(n_dim, 512, LANES)
    tk = _pick(k_dim, 1024, LANES)
    nk = k_dim // tk
    n_add = len(add)
    dims = TN if ta else (NT if tb else NN)
    assert not (ta and tb)

    def body(a_ref, b_ref, *rest):
        add_refs = rest[:n_add]
        o_ref = rest[n_add]
        acc = rest[n_add + 1]
        k = pl.program_id(2)

        @pl.when(k == 0)
        def _():
            acc[...] = jnp.zeros_like(acc)

        acc[...] += _dot(a_ref[...], b_ref[...], dims)

        @pl.when(k == nk - 1)
        def _():
            r = acc[...]
            for ar in add_refs:
                r = r + ar[...].astype(F32)
            o_ref[...] = r.astype(o_ref.dtype)

    a_spec = pl.BlockSpec((tk, tm), lambda i, j, k: (k, i)) if ta else pl.BlockSpec((tm, tk), lambda i, j, k: (i, k))
    b_spec = pl.BlockSpec((tn, tk), lambda i, j, k: (j, k)) if tb else pl.BlockSpec((tk, tn), lambda i, j, k: (k, j))
    o_spec = pl.BlockSpec((tm, tn), lambda i, j, k: (i, j))
    return pl.pallas_call(
        body,
        out_shape=jax.ShapeDtypeStruct((m_dim, n_dim), out_dtype),
        grid=(m_dim // tm, n_dim // tn, nk),
        in_specs=[a_spec, b_spec] + [o_spec] * n_add,
        out_specs=o_spec,
        scratch_shapes=[pltpu.VMEM((tm, tn), F32)],
        compiler_params=pltpu.CompilerParams(dimension_semantics=("parallel", "parallel", "arbitrary")),
        name=name,
    )(a, b, *add)


def _rowwise(fn, rows, consts, outs, accs=(), *, tm=256, name):
    rows = [r if isinstance(r, tuple) else (r, 0, r.shape[1]) for r in rows]
    s_dim = rows[0][0].shape[0]
    tm = min(tm, s_dim)
    assert s_dim % tm == 0 and all(arr.shape[0] == s_dim for arr, _, _ in rows)
    specs = [pl.BlockSpec((tm, width), functools.partial(lambda i, cb: (i, cb), cb=cb)) for _, cb, width in rows]
    args = [arr for arr, _, _ in rows]
    for c in consts:
        specs.append(pl.BlockSpec(c.shape, lambda i: (0, 0)))
        args.append(c)
    nr, nc, no = len(rows), len(consts), len(outs)
    out_shape = [jax.ShapeDtypeStruct((s_dim, w), dt) for (w, dt) in outs]
    out_specs = [pl.BlockSpec((tm, w), lambda i: (i, 0)) for (w, dt) in outs]
    out_shape += [jax.ShapeDtypeStruct(sh, F32) for sh in accs]
    out_specs += [pl.BlockSpec(sh, lambda i: (0, 0)) for sh in accs]

    def body(*refs):
        r = [x[...] for x in refs[:nr]]
        c = [x[...] for x in refs[nr:nr + nc]]
        o_refs = refs[nr + nc:nr + nc + no]
        a_refs = refs[nr + nc + no:]
        o_vals, a_vals = fn(r, c)
        for ref, v in zip(o_refs, o_vals, strict=True):
            ref[...] = v.astype(ref.dtype)
        if a_refs:
            @pl.when(pl.program_id(0) == 0)
            def _():
                for ref in a_refs:
                    ref[...] = jnp.zeros_like(ref)

            for ref, v in zip(a_refs, a_vals, strict=True):
                ref[...] += v

    res = pl.pallas_call(
        body,
        out_shape=out_shape,
        grid=(s_dim // tm,),
        in_specs=specs,
        out_specs=out_specs,
        compiler_params=pltpu.CompilerParams(dimension_semantics=("arbitrary" if accs else "parallel",)),
        name=name,
    )(*args)
    return res


def _colsum(v):
    return jnp.sum(v, axis=0, keepdims=True)


def _rowsum(v):
    return jnp.sum(v, axis=1, keepdims=True)


def _rowmean(v):
    return jnp.mean(v, axis=1, keepdims=True)


def _silu_grad(x):
    s = _sig(x)
    return s * (1.0 + x * (1.0 - s))


def _conv_taps(x, w, width=4):
    row = lax.broadcasted_iota(jnp.int32, x.shape, 0)
    c = x * w[width - 1:width, :]
    for s in range(1, width):
        c = c + jnp.where(row >= s, pltpu.roll(x, s, 0), 0.0) * w[width - 1 - s:width - s, :]
    return c


def _conv_fwd(proj_a, conv_w):
    s_dim = proj_a.shape[0]
    n_blk = 3 * N_HEADS

    def body(x_ref, w_ref, o_ref):
        j = pl.program_id(0)
        c = _conv_taps(x_ref[...], w_ref[...])
        y = c * _sig(c)
        r = lax.rsqrt(_rowsum(y * y) + EPS_RMS)
        fac = jnp.where(j < N_HEADS, r * (HEAD ** -0.5), jnp.where(j < 2 * N_HEADS, r, 1.0))
        o_ref[...] = y * fac

    return pl.pallas_call(
        body,
        out_shape=jax.ShapeDtypeStruct((s_dim, n_blk * HEAD), F32),
        grid=(n_blk,),
        in_specs=[pl.BlockSpec((s_dim, HEAD), lambda j: (0, j)), pl.BlockSpec((4, HEAD), lambda j: (0, j))],
        out_specs=pl.BlockSpec((s_dim, HEAD), lambda j: (0, j)),
        compiler_params=pltpu.CompilerParams(dimension_semantics=("parallel",)),
        name="conv_fwd",
    )(proj_a, conv_w)


def _conv_bwd(proj_a, conv_w, dq, dk, dv):
    s_dim = proj_a.shape[0]
    n_blk = 3 * N_HEADS

    def body(x_ref, w_ref, dq_ref, dk_ref, dv_ref, dx_ref, dw_ref):
        j = pl.program_id(0)
        x = x_ref[...]
        w = w_ref[...]
        do = jnp.where(j < N_HEADS, dq_ref[...], jnp.where(j < 2 * N_HEADS, dk_ref[...], dv_ref[...]))
        c = _conv_taps(x, w)
        sg = _sig(c)
        y = c * sg
        r = lax.rsqrt(_rowsum(y * y) + EPS_RMS)
        sc = jnp.where(j < N_HEADS, HEAD ** -0.5, 1.0)
        dy_n = sc * (r * do - y * (r * r * r) * _rowsum(do * y))
        dy = jnp.where(j < 2 * N_HEADS, dy_n, do)
        dc = dy * (sg * (1.0 + c * (1.0 - sg)))
        row = lax.broadcasted_iota(jnp.int32, x.shape, 0)
        dx = dc * w[3:4, :]
        dw_ref[3:4, :] = _colsum(dc * x)
        for s in range(1, 4):
            dx = dx + jnp.where(row < s_dim - s, pltpu.roll(dc, s_dim - s, 0), 0.0) * w[3 - s:4 - s, :]
            xs = jnp.where(row >= s, pltpu.roll(x, s, 0), 0.0)
            dw_ref[3 - s:4 - s, :] = _colsum(dc * xs)
        dx_ref[...] = dx

    hd = N_HEADS - 1
    return pl.pallas_call(
        body,
        out_shape=[jax.ShapeDtypeStruct((s_dim, n_blk * HEAD), F32), jax.ShapeDtypeStruct((4, n_blk * HEAD), F32)],
        grid=(n_blk,),
        in_specs=[
            pl.BlockSpec((s_dim, HEAD), lambda j: (0, j)),
            pl.BlockSpec((4, HEAD), lambda j: (0, j)),
            pl.BlockSpec((s_dim, HEAD), lambda j: (0, jnp.minimum(j, hd))),
            pl.BlockSpec((s_dim, HEAD), lambda j: (0, jnp.clip(j - N_HEADS, 0, hd))),
            pl.BlockSpec((s_dim, HEAD), lambda j: (0, jnp.clip(j - 2 * N_HEADS, 0, hd))),
        ],
        out_specs=[pl.BlockSpec((s_dim, HEAD), lambda j: (0, j)), pl.BlockSpec((4, HEAD), lambda j: (0, j))],
        compiler_params=pltpu.CompilerParams(dimension_semantics=("parallel",)),
        name="conv_bwd",
    )(proj_a, conv_w, dq, dk, dv)


def _chunk_tri(n):
    r = np.arange(n)
    m = ((r[:, None] // CHUNK) == (r[None, :] // CHUNK)) & (r[:, None] >= r[None, :])
    m = m.astype(np.float32)
    return jnp.asarray(m), jnp.asarray(m.T)


def _softplus(z):
    return jnp.maximum(z, 0.0) + jnp.log(1.0 + jnp.exp(-jnp.abs(z)))


def _gates_fwd(proj_b, alog, dtb):
    tm = min(GROUP, proj_b.shape[0])
    tri, _ = _chunk_tri(tm)

    def fn(r, c):
        b, a = r
        alog_, dtb_, tri_ = c
        g = -jnp.exp(alog_) * _softplus(a + dtb_)
        return [_sig(b), _dot32(tri_, g)], []

    return _rowwise(fn, [(proj_b, WB_B // LANES, LANES), (proj_b, WB_A // LANES, LANES)], [alog, dtb, tri],
                    [(LANES, F32), (LANES, F32)], tm=tm, name="gates_fwd")


def _gates_bwd(proj_b, alog, dtb, gc, d_beta, d_gc, d_egl_rows):
    tm = min(GROUP, proj_b.shape[0])
    _, tri_t = _chunk_tri(tm)

    def fn(r, c):
        b, a, gc_, d_beta_, d_gc_, d_egl_ = r
        alog_, dtb_, tri_t_ = c
        z = a + dtb_
        ea = jnp.exp(alog_)
        g = -ea * _softplus(z)
        dg = _dot32(tri_t_, d_gc_ + d_egl_ * jnp.exp(gc_))
        d_a = dg * (-ea) * _sig(z)
        beta = _sig(b)
        return [d_beta_ * beta * (1.0 - beta), d_a], [_colsum(dg * g), _colsum(d_a)]

    return _rowwise(fn, [(proj_b, WB_B // LANES, LANES), (proj_b, WB_A // LANES, LANES), gc, d_beta, d_gc, d_egl_rows],
                    [alog, dtb, tri_t], [(LANES, F32), (LANES, F32)], accs=[(1, LANES), (1, LANES)], tm=tm,
                    name="gates_bwd")


def _group_masks(n):
    r = lax.broadcasted_iota(jnp.int32, (n, n), 0)
    c = lax.broadcasted_iota(jnp.int32, (n, n), 1)
    same = (r // CHUNK) == (c // CHUNK)
    tril = jnp.logical_and(same, r >= c)
    strict = jnp.logical_and(same, r > c)
    last = c == (r // CHUNK) * CHUNK + (CHUNK - 1)
    eye = r == c
    return same, tril, strict, last, eye


def _inv_unit_lower(l_mat, eye_f):
    q = -l_mat
    r = eye_f + q
    for _ in range(5):
        q = _dot32(q, q)
        r = r + _dot32(r, q)
    return r


def _head_cols(beta, gc, gc_t, h):
    lane = lax.broadcasted_iota(jnp.int32, beta.shape, 1)
    sub = lax.broadcasted_iota(jnp.int32, gc_t.shape, 0)
    bcol = _rowsum(jnp.where(lane == h, beta, 0.0))
    gcol = _rowsum(jnp.where(lane == h, gc, 0.0))
    grow = _colsum(jnp.where(sub == h, gc_t, 0.0))
    return bcol, gcol, grow


def _prep_common(q, k, bcol, gcol, grow):
    n = q.shape[0]
    same, tril, strict, last, eye = _group_masks(n)
    decay = jnp.where(tril, jnp.exp(jnp.where(tril, gcol - grow, 0.0)), 0.0)
    glast = _rowsum(jnp.where(last, jnp.broadcast_to(grow, (n, n)), 0.0))
    e = jnp.exp(gcol)
    ekt = jnp.exp(glast - gcol)
    kb = k * bcol
    kk = _dot32(kb, k, NT)
    l_mat = jnp.where(strict, kk * decay, 0.0)
    t_mat = _inv_unit_lower(l_mat, eye.astype(F32))
    qk = _dot(q, k, NT)
    return dict(same=same, tril=tril, strict=strict, last=last, eye=eye, decay=decay, e=e, ekt=ekt, kb=kb, kk=kk,
                t=t_mat, qk=qk)


def _fold_blocks(m):
    n = m.shape[0]
    out = m[:, 0:CHUNK]
    for b in range(1, n // CHUNK):
        out = out + m[:, b * CHUNK:(b + 1) * CHUNK]
    return out


def _gdr_prep_fwd(qkvn, beta, gc, gc_t):
    s_dim = qkvn.shape[0]
    tg = min(GROUP, s_dim)

    def body(q_ref, k_ref, v_ref, b_ref, g_ref, gt_ref, u_ref, w_ref, qd_ref, kt_ref, a_ref):
        h = pl.program_id(0)
        q, k, v = q_ref[...], k_ref[...], v_ref[...]
        bcol, gcol, grow = _head_cols(b_ref[...], g_ref[...], gt_ref[...], h)
        p = _prep_common(q, k, bcol, gcol, grow)
        u_ref[...] = _dot32(p["t"], v * bcol)
        w_ref[...] = _dot32(p["t"], p["kb"] * p["e"])
        qd_ref[...] = q * p["e"]
        kt_ref[...] = k * p["ekt"]
        a_ref[...] = _fold_blocks(jnp.where(p["tril"], p["qk"] * p["decay"], 0.0))

    row = lambda off: pl.BlockSpec((tg, HEAD), functools.partial(lambda h, m, off: (m, h + off), off=off))
    full = pl.BlockSpec((tg, LANES), lambda h, m: (m, 0))
    o_spec = pl.BlockSpec((tg, HEAD), lambda h, m: (m, h))
    wide = jax.ShapeDtypeStruct((s_dim, N_HEADS * HEAD), F32)
    return pl.pallas_call(
        body,
        out_shape=[wide, wide, wide, wide, jax.ShapeDtypeStruct((N_HEADS, s_dim, CHUNK), F32)],
        grid=(N_HEADS, s_dim // tg),
        in_specs=[row(0), row(N_HEADS), row(2 * N_HEADS), full, full, pl.BlockSpec((8, tg), lambda h, m: (0, m))],
        out_specs=[o_spec, o_spec, o_spec, o_spec, pl.BlockSpec((None, tg, CHUNK), lambda h, m: (h, m, 0))],
        compiler_params=pltpu.CompilerParams(dimension_semantics=("parallel", "parallel")),
        name="gdr_prep_fwd",
    )(qkvn, qkvn, qkvn, beta, gc, gc_t)


def _gdr_prep_bwd(qkvn, beta, gc, gc_t, du, dw, dqd, dkt, d_a):
    s_dim = qkvn.shape[0]
    tg = min(GROUP, s_dim)

    def body(q_ref, k_ref, v_ref, b_ref, g_ref, gt_ref, du_ref, dw_ref, dqd_ref, dkt_ref, da_ref,
             dq_ref, dk_ref, dv_ref, db_ref, dg_ref):
        h = pl.program_id(0)
        q, k, v = q_ref[...], k_ref[...], v_ref[...]
        bcol, gcol, grow = _head_cols(b_ref[...], g_ref[...], gt_ref[...], h)
        p = _prep_common(q, k, bcol, gcol, grow)
        t_mat, decay, e, ekt, kb = p["t"], p["decay"], p["e"], p["ekt"], p["kb"]
        du_, dw_, dqd_, dkt_ = du_ref[...], dw_ref[...], dqd_ref[...], dkt_ref[...]
        vb = v * bcol
        kbe = kb * e
        d_t = _dot32(du_, vb, NT) + _dot32(dw_, kbe, NT)
        dvb = _dot32(t_mat, du_, TN)
        dkbe = _dot32(t_mat, dw_, TN)
        d_l = -_dot32(_dot32(t_mat, d_t, TN), t_mat, NT)
        m1 = jnp.where(p["strict"], d_l, 0.0)
        d_a_full = jnp.concatenate([da_ref[...]] * (tg // CHUNK), axis=1)
        m2 = jnp.where(p["tril"], d_a_full, 0.0)
        d_kk = m1 * decay
        d_qk = m2 * decay
        d_decay = m1 * p["kk"] + m2 * p["qk"]
        dkb = _dot32(d_kk, k) + dkbe * e
        dk = _dot32(d_kk, kb, TN) + _dot(d_qk, q, TN) + dkt_ * ekt + dkb * bcol
        dq = _dot(d_qk, k) + dqd_ * e
        d_beta = _rowsum(dkb * k) + _rowsum(dvb * v)
        d_e = _rowsum(dkbe * kb) + _rowsum(dqd_ * q)
        d_ekt = _rowsum(dkt_ * k) * ekt
        d_diff = d_decay * decay
        d_grow = -_colsum(d_diff) + _colsum(jnp.where(p["last"], jnp.broadcast_to(d_ekt, (tg, tg)), 0.0))
        d_gcol = d_e * e - d_ekt + _rowsum(d_diff)
        d_gcol = d_gcol + _rowsum(jnp.where(p["eye"], jnp.broadcast_to(d_grow, (tg, tg)), 0.0))
        dq_ref[...] = dq
        dk_ref[...] = dk
        dv_ref[...] = dvb * bcol
        db_ref[...] = jnp.broadcast_to(d_beta, (tg, LANES))
        dg_ref[...] = jnp.broadcast_to(d_gcol, (tg, LANES))

    row = lambda off: pl.BlockSpec((tg, HEAD), functools.partial(lambda h, m, off: (m, h + off), off=off))
    full = pl.BlockSpec((tg, LANES), lambda h, m: (m, 0))
    o_spec = pl.BlockSpec((tg, HEAD), lambda h, m: (m, h))
    a_spec = pl.BlockSpec((None, tg, CHUNK), lambda h, m: (h, m, 0))
    l_spec = pl.BlockSpec((None, tg, LANES), lambda h, m: (h, m, 0))
    wide = jax.ShapeDtypeStruct((s_dim, N_HEADS * HEAD), F32)
    per_head = jax.ShapeDtypeStruct((N_HEADS, s_dim, LANES), F32)
    return pl.pallas_call(
        body,
        out_shape=[wide, wide, wide, per_head, per_head],
        grid=(N_HEADS, s_dim // tg),
        in_specs=[row(0), row(N_HEADS), row(2 * N_HEADS), full, full, pl.BlockSpec((8, tg), lambda h, m: (0, m)),
                  o_spec, o_spec, o_spec, o_spec, a_spec],
        out_specs=[o_spec, o_spec, o_spec, l_spec, l_spec],
        compiler_params=pltpu.CompilerParams(dimension_semantics=("parallel", "parallel")),
        name="gdr_prep_bwd",
    )(qkvn, qkvn, qkvn, beta, gc, gc_t, du, dw, dqd, dkt, d_a)


def _gdr_scan_fwd(u, w, qd, kt, a_mat, gc):
    s_dim = u.shape[0]
    n_chunks = s_dim // CHUNK

    def body(u_ref, w_ref, qd_ref, kt_ref, a_ref, g_ref, o_ref, st_ref, state):
        @pl.when(pl.program_id(0) == 0)
        def _():
            state[...] = jnp.zeros_like(state)

        egl = jnp.exp(g_ref[CHUNK - 1:CHUNK, :])
        for h in range(N_HEADS):
            cs = slice(h * HEAD, (h + 1) * HEAD)
            s_h = state[h]
            st_ref[h] = s_h
            vn = u_ref[:, cs] - _dot(w_ref[:, cs], s_h)
            o_ref[:, cs] = _dot(qd_ref[:, cs], s_h) + _dot(a_ref[h], vn)
            state[h] = s_h * egl[:, h:h + 1] + _dot(kt_ref[:, cs], vn, TN)

    wide = pl.BlockSpec((CHUNK, N_HEADS * HEAD), lambda n: (n, 0))
    return pl.pallas_call(
        body,
        out_shape=[jax.ShapeDtypeStruct((s_dim, N_HEADS * HEAD), F32),
                   jax.ShapeDtypeStruct((n_chunks, N_HEADS, HEAD, HEAD), F32)],
        grid=(n_chunks,),
        in_specs=[wide, wide, wide, wide, pl.BlockSpec((N_HEADS, CHUNK, CHUNK), lambda n: (0, n, 0)),
                  pl.BlockSpec((CHUNK, LANES), lambda n: (n, 0))],
        out_specs=[wide, pl.BlockSpec((None, N_HEADS, HEAD, HEAD), lambda n: (n, 0, 0, 0))],
        scratch_shapes=[pltpu.VMEM((N_HEADS, HEAD, HEAD), F32)],
        compiler_params=pltpu.CompilerParams(dimension_semantics=("arbitrary",)),
        name="gdr_scan_fwd",
    )(u, w, qd, kt, a_mat, gc)


def _gdr_scan_bwd(u, w, qd, kt, a_mat, gc, states, d_o):
    s_dim = u.shape[0]
    n_chunks = s_dim // CHUNK
    last = n_chunks - 1

    def body(u_ref, w_ref, qd_ref, kt_ref, a_ref, g_ref, st_ref, do_ref,
             du_ref, dw_ref, dqd_ref, dkt_ref, da_ref, de_ref, d_state):
        @pl.when(pl.program_id(0) == 0)
        def _():
            d_state[...] = jnp.zeros_like(d_state)

        egl = jnp.exp(g_ref[CHUNK - 1:CHUNK, :])
        for h in range(N_HEADS):
            cs = slice(h * HEAD, (h + 1) * HEAD)
            s_h = st_ref[h]
            ds_n = d_state[h]
            do = do_ref[:, cs]
            w_h = w_ref[:, cs]
            vn = u_ref[:, cs] - _dot(w_h, s_h)
            dvn = _dot(a_ref[h], do, TN) + _dot(kt_ref[:, cs], ds_n)
            dqd_ref[:, cs] = _dot(do, s_h, NT)
            da_ref[h] = _dot(do, vn, NT)
            dkt_ref[:, cs] = _dot(vn, ds_n, NT)
            de = jnp.sum(_rowsum(ds_n * s_h), axis=0, keepdims=True)
            de_ref[h:h + 1, :] = jnp.broadcast_to(de, (1, LANES))
            du_ref[:, cs] = dvn
            dw_ref[:, cs] = -_dot(dvn, s_h, NT)
            d_state[h] = ds_n * egl[:, h:h + 1] + _dot(qd_ref[:, cs], do, TN) - _dot(w_h, dvn, TN)

    wide = pl.BlockSpec((CHUNK, N_HEADS * HEAD), lambda n: (last - n, 0))
    a_spec = pl.BlockSpec((N_HEADS, CHUNK, CHUNK), lambda n: (0, last - n, 0))
    wide_shape = jax.ShapeDtypeStruct((s_dim, N_HEADS * HEAD), F32)
    return pl.pallas_call(
        body,
        out_shape=[wide_shape, wide_shape, wide_shape, wide_shape,
                   jax.ShapeDtypeStruct((N_HEADS, s_dim, CHUNK), F32),
                   jax.ShapeDtypeStruct((n_chunks, N_HEADS, LANES), F32)],
        grid=(n_chunks,),
        in_specs=[wide, wide, wide, wide, a_spec, pl.BlockSpec((CHUNK, LANES), lambda n: (last - n, 0)),
                  pl.BlockSpec((None, N_HEADS, HEAD, HEAD), lambda n: (last - n, 0, 0, 0)), wide],
        out_specs=[wide, wide, wide, wide, a_spec, pl.BlockSpec((None, N_HEADS, LANES), lambda n: (last - n, 0, 0))],
        scratch_shapes=[pltpu.VMEM((N_HEADS, HEAD, HEAD), F32)],
        compiler_params=pltpu.CompilerParams(dimension_semantics=("arbitrary",)),
        name="gdr_scan_bwd",
    )(u, w, qd, kt, a_mat, gc, states, d_o)


def _gdr_out_fwd(o_dn, proj_a, dn_w):
    def fn(r, c):
        o, z = r
        (w_,) = c
        outs = []
        for h in range(N_HEADS):
            cs = slice(h * HEAD, (h + 1) * HEAD)
            oh, zh = o[:, cs], z[:, cs]
            rr = lax.rsqrt(_rowmean(oh * oh) + EPS_RMS)
            outs.append(oh * rr * w_ * (zh * _sig(zh)))
        return [jnp.concatenate(outs, axis=1)], []

    return _rowwise(fn, [o_dn, (proj_a, 3, D_MODEL)], [dn_w], [(D_MODEL, F32)], name="gdr_out_fwd")[0]


def _gdr_out_bwd(o_dn, proj_a, d_og, dn_w):
    def fn(r, c):
        o, z, dg = r
        (w_,) = c
        d_o, d_z = [], []
        d_w = jnp.zeros((1, HEAD), F32)
        for h in range(N_HEADS):
            cs = slice(h * HEAD, (h + 1) * HEAD)
            oh, zh, dgh = o[:, cs], z[:, cs], dg[:, cs]
            rr = lax.rsqrt(_rowmean(oh * oh) + EPS_RMS)
            sz = zh * _sig(zh)
            d_n = dgh * sz
            d_z.append(dgh * (oh * rr * w_) * _silu_grad(zh))
            d_w = d_w + _colsum(d_n * oh * rr)
            gw = d_n * w_
            d_o.append(rr * gw - oh * (rr * rr * rr) * _rowmean(gw * oh))
        return [jnp.concatenate(d_o, axis=1), jnp.concatenate(d_z, axis=1)], [d_w]

    return _rowwise(fn, [o_dn, (proj_a, 3, D_MODEL), d_og], [dn_w], [(D_MODEL, F32), (D_MODEL, F32)],
                    accs=[(1, HEAD)], name="gdr_out_bwd")


def _rms_fwd(x, w):
    r = lax.rsqrt(_rowmean(x * x) + EPS_RMS)
    return x * r * w


def _rms_bwd(x, w, dy):
    r = lax.rsqrt(_rowmean(x * x) + EPS_RMS)
    gw = dy * w
    return r * gw - x * (r * r * r) * _rowmean(gw * x), _colsum(dy * x * r)


def _mla_norm_fwd(proj_b, qn_w, kvn_w):
    def fn(r, c):
        return [_rms_fwd(r[0], c[0]), _rms_fwd(r[1], c[1])], []

    return _rowwise(fn, [(proj_b, WB_CQ // Q_LORA, Q_LORA), (proj_b, WB_CKV // KV_LORA, KV_LORA)], [qn_w, kvn_w],
                    [(Q_LORA, F32), (KV_LORA, F32)], name="mla_norm_fwd")


def _mla_norm_bwd(proj_b, qn_w, kvn_w, d_cq, d_ckv):
    def fn(r, c):
        dx1, dw1 = _rms_bwd(r[0], c[0], r[2])
        dx2, dw2 = _rms_bwd(r[1], c[1], r[3])
        return [dx1, dx2], [dw1, dw2]

    return _rowwise(fn, [(proj_b, WB_CQ // Q_LORA, Q_LORA), (proj_b, WB_CKV // KV_LORA, KV_LORA), d_cq, d_ckv],
                    [qn_w, kvn_w], [(Q_LORA, F32), (KV_LORA, F32)], accs=[(1, Q_LORA), (1, KV_LORA)],
                    name="mla_norm_bwd")


def _rope_consts():
    inv = ROPE_BASE ** (-np.arange(0, ROPE, 2, dtype=np.float32) / ROPE)
    t = np.zeros((4, LANES), np.float32)
    t[0, :32] = inv
    t[0, 32:64] = inv
    t[1, :64] = 1.0
    t[2, 32:64] = 1.0
    t[3, :32] = -1.0
    return jnp.asarray(t)


def _rope_tables(pos, consts, width):
    ang = pos * consts[0:1, :]
    cosv, sinv = jnp.cos(ang), jnp.sin(ang)
    reps = width // LANES
    tile = (lambda t: jnp.concatenate([t] * reps, axis=1)) if reps > 1 else (lambda t: t)
    return tile(cosv * consts[1:2, :]), tile(sinv * consts[2:3, :]), tile(sinv * consts[3:4, :])


def _rope_apply(t, tabs):
    cos_t, sin_a, sin_b = tabs
    width = t.shape[1]
    return t * cos_t + pltpu.roll(t, 32, 1) * sin_a + pltpu.roll(t, width - 32, 1) * sin_b


def _rope_transpose(d, tabs):
    cos_t, sin_a, sin_b = tabs
    width = d.shape[1]
    return d * cos_t + pltpu.roll(d * sin_a, width - 32, 1) + pltpu.roll(d * sin_b, 32, 1)


def _mla_qk_fwd(q_full, k_nope, proj_b, pos):
    consts = _rope_consts()

    def fn(r, c):
        qf, kn, kr, pos_ = r
        qn, qr = qf[:, :D_MODEL], qf[:, D_MODEL:]
        return [qn * SCALE, _rope_apply(qr, _rope_tables(pos_, c[0], D_MODEL)) * SCALE, kn,
                _rope_apply(kr, _rope_tables(pos_, c[0], LANES))], []

    return _rowwise(fn, [q_full, k_nope, (proj_b, WB_KR // LANES, LANES), pos], [consts],
                    [(D_MODEL, BF16), (D_MODEL, BF16), (D_MODEL, BF16), (LANES, BF16)], name="mla_qk_fwd")


def _mla_qk_bwd(d_qn, d_qr, d_kr_heads, pos):
    consts = _rope_consts()

    def fn(r, c):
        dqn, dqr, dkr, pos_ = r
        d_qr_raw = _rope_transpose(dqr, _rope_tables(pos_, c[0], D_MODEL)) * SCALE
        dk = dkr[:, 0:LANES]
        for h in range(1, N_HEADS):
            dk = dk + dkr[:, h * LANES:(h + 1) * LANES]
        return [jnp.concatenate([dqn * SCALE, d_qr_raw], axis=1), _rope_transpose(dk, _rope_tables(pos_, c[0], LANES))], []

    return _rowwise(fn, [d_qn, d_qr, d_kr_heads, pos], [consts], [(2 * D_MODEL, F32), (LANES, F32)],
                    name="mla_qk_bwd")


def _causal_scores(qn, qr, kn, kr, qi, ki, tq, tk):
    s = _dot(qn, kn, NT) + _dot(qr, kr, NT)
    row = lax.broadcasted_iota(jnp.int32, (tq, tk), 0) + qi * tq
    col = lax.broadcasted_iota(jnp.int32, (tq, tk), 1) + ki * tk
    return jnp.where(col <= row, s, NEG_BIG)


def _attn_fwd(qn, qr, kn, kr, v):
    s_dim = qn.shape[0]
    tq = tk = min(512, s_dim)
    nq, nk = s_dim // tq, s_dim // tk

    def body(qn_ref, qr_ref, kn_ref, kr_ref, v_ref, o_ref, lse_ref, m_s, l_s, acc):
        qi, ki = pl.program_id(1), pl.program_id(2)

        @pl.when(ki == 0)
        def _():
            m_s[...] = jnp.full_like(m_s, NEG_BIG)
            l_s[...] = jnp.zeros_like(l_s)
            acc[...] = jnp.zeros_like(acc)

        @pl.when(ki <= qi)
        def _():
            s = _causal_scores(qn_ref[...], qr_ref[...], kn_ref[...], kr_ref[...], qi, ki, tq, tk)
            m_prev = m_s[:, 0:1]
            m_new = jnp.maximum(m_prev, jnp.max(s, axis=1, keepdims=True))
            alpha = jnp.exp(m_prev - m_new)
            p = jnp.exp(s - m_new)
            l_s[...] = jnp.broadcast_to(alpha * l_s[:, 0:1] + _rowsum(p), l_s.shape)
            m_s[...] = jnp.broadcast_to(m_new, m_s.shape)
            acc[...] = acc[...] * alpha + _dot(p, v_ref[...])

        @pl.when(ki == nk - 1)
        def _():
            l = l_s[...]
            o_ref[...] = acc[...] / l
            lse_ref[...] = m_s[...] + jnp.log(l)

    q_spec = pl.BlockSpec((tq, HEAD), lambda h, qi, ki: (qi, h))
    k_spec = pl.BlockSpec((tk, HEAD), lambda h, qi, ki: (jnp.minimum(ki, qi), h))
    kr_spec = pl.BlockSpec((tk, LANES), lambda h, qi, ki: (jnp.minimum(ki, qi), 0))
    return pl.pallas_call(
        body,
        out_shape=[jax.ShapeDtypeStruct((s_dim, N_HEADS * HEAD), F32)] * 2,
        grid=(N_HEADS, nq, nk),
        in_specs=[q_spec, q_spec, k_spec, kr_spec, k_spec],
        out_specs=[q_spec, q_spec],
        scratch_shapes=[pltpu.VMEM((tq, LANES), F32), pltpu.VMEM((tq, LANES), F32), pltpu.VMEM((tq, HEAD), F32)],
        compiler_params=pltpu.CompilerParams(dimension_semantics=("parallel", "parallel", "arbitrary")),
        name="attn_fwd",
    )(qn, qr, kn, kr, v)


def _attn_delta(o, d_o):
    def fn(r, c):
        o_, do_ = r
        outs = []
        for h in range(N_HEADS):
            cs = slice(h * HEAD, (h + 1) * HEAD)
            outs.append(jnp.broadcast_to(_rowsum(o_[:, cs] * do_[:, cs]), (o_.shape[0], HEAD)))
        return [jnp.concatenate(outs, axis=1)], []

    return _rowwise(fn, [o, d_o], [], [(D_MODEL, F32)], name="attn_delta")[0]


def _attn_bwd_kv(qn, qr, kn, kr, v, d_o, lse, delta):
    s_dim = qn.shape[0]
    tq = tk = min(512, s_dim)
    nq, nk = s_dim // tq, s_dim // tk

    def body(qn_ref, qr_ref, kn_ref, kr_ref, v_ref, do_ref, lse_ref, dl_ref, dkn_ref, dkr_ref, dv_ref, a_kn, a_kr, a_v):
        ki, qi = pl.program_id(1), pl.program_id(2)

        @pl.when(qi == 0)
        def _():
            a_kn[...] = jnp.zeros_like(a_kn)
            a_kr[...] = jnp.zeros_like(a_kr)
            a_v[...] = jnp.zeros_like(a_v)

        @pl.when(qi >= ki)
        def _():
            qn_, qr_, do = qn_ref[...], qr_ref[...], do_ref[...]
            s = _causal_scores(qn_, qr_, kn_ref[...], kr_ref[...], qi, ki, tq, tk)
            p = jnp.exp(s - lse_ref[:, 0:1])
            a_v[...] += _dot(p, do, TN)
            ds = p * (_dot(do, v_ref[...], NT) - dl_ref[:, 0:1])
            a_kn[...] += _dot(ds, qn_, TN)
            a_kr[...] += _dot(ds, qr_, TN)

        @pl.when(qi == nq - 1)
        def _():
            dkn_ref[...] = a_kn[...]
            dkr_ref[...] = a_kr[...]
            dv_ref[...] = a_v[...]

    q_spec = pl.BlockSpec((tq, HEAD), lambda h, ki, qi: (jnp.maximum(qi, ki), h))
    k_spec = pl.BlockSpec((tk, HEAD), lambda h, ki, qi: (ki, h))
    kr_spec = pl.BlockSpec((tk, LANES), lambda h, ki, qi: (ki, 0))
    wide = jax.ShapeDtypeStruct((s_dim, N_HEADS * HEAD), F32)
    return pl.pallas_call(
        body,
        out_shape=[wide, wide, wide],
        grid=(N_HEADS, nk, nq),
        in_specs=[q_spec, q_spec, k_spec, kr_spec, k_spec, q_spec, q_spec, q_spec],
        out_specs=[k_spec, k_spec, k_spec],
        scratch_shapes=[pltpu.VMEM((tk, HEAD), F32)] * 3,
        compiler_params=pltpu.CompilerParams(dimension_semantics=("parallel", "parallel", "arbitrary")),
        name="attn_bwd_kv",
    )(qn, qr, kn, kr, v, d_o, lse, delta)


def _attn_bwd_q(qn, qr, kn, kr, v, d_o, lse, delta):
    s_dim = qn.shape[0]
    tq = tk = min(512, s_dim)
    nq, nk = s_dim // tq, s_dim // tk

    def body(qn_ref, qr_ref, kn_ref, kr_ref, v_ref, do_ref, lse_ref, dl_ref, dqn_ref, dqr_ref, a_n, a_r):
        qi, ki = pl.program_id(1), pl.program_id(2)

        @pl.when(ki == 0)
        def _():
            a_n[...] = jnp.zeros_like(a_n)
            a_r[...] = jnp.zeros_like(a_r)

        @pl.when(ki <= qi)
        def _():
            kn_, kr_ = kn_ref[...], kr_ref[...]
            s = _causal_scores(qn_ref[...], qr_ref[...], kn_, kr_, qi, ki, tq, tk)
            p = jnp.exp(s - lse_ref[:, 0:1])
            ds = p * (_dot(do_ref[...], v_ref[...], NT) - dl_ref[:, 0:1])
            a_n[...] += _dot(ds, kn_)
            a_r[...] += _dot(ds, kr_)

        @pl.when(ki == nk - 1)
        def _():
            dqn_ref[...] = a_n[...]
            dqr_ref[...] = a_r[...]

    q_spec = pl.BlockSpec((tq, HEAD), lambda h, qi, ki: (qi, h))
    k_spec = pl.BlockSpec((tk, HEAD), lambda h, qi, ki: (jnp.minimum(ki, qi), h))
    kr_spec = pl.BlockSpec((tk, LANES), lambda h, qi, ki: (jnp.minimum(ki, qi), 0))
    wide = jax.ShapeDtypeStruct((s_dim, N_HEADS * HEAD), F32)
    return pl.pallas_call(
        body,
        out_shape=[wide, wide],
        grid=(N_HEADS, nq, nk),
        in_specs=[q_spec, q_spec, k_spec, kr_spec, k_spec, q_spec, q_spec, q_spec],
        out_specs=[q_spec, q_spec],
        scratch_shapes=[pltpu.VMEM((tq, HEAD), F32)] * 2,
        compiler_params=pltpu.CompilerParams(dimension_semantics=("parallel", "parallel", "arbitrary")),
        name="attn_bwd_q",
    )(qn, qr, kn, kr, v, d_o, lse, delta)


def _merge_fwd(y_dn, y_mla, proj_g):
    def fn(r, c):
        yd, ym, g = r
        return [_sig(g[:, :D_MODEL]) * yd + _sig(g[:, D_MODEL:]) * ym], []

    return _rowwise(fn, [y_dn, y_mla, proj_g], [], [(D_MODEL, F32)], name="merge_fwd")[0]


def _merge_bwd(y_dn, y_mla, proj_g, d_mixed):
    def fn(r, c):
        yd, ym, g, dm = r
        sd, sm = _sig(g[:, :D_MODEL]), _sig(g[:, D_MODEL:])
        d_g = jnp.concatenate([dm * yd * sd * (1.0 - sd), dm * ym * sm * (1.0 - sm)], axis=1)
        return [d_g, dm * sd, dm * sm], []

    return _rowwise(fn, [y_dn, y_mla, proj_g, d_mixed], [], [(2 * D_MODEL, F32), (D_MODEL, F32), (D_MODEL, F32)],
                    name="merge_bwd")


def _ln_stats(z):
    mu = _rowmean(z)
    zc = z - mu
    r = lax.rsqrt(_rowmean(zc * zc) + EPS_LN)
    return zc * r, r


def _ln_bwd(dy, xh, r, g):
    dxh = dy * g
    return r * (dxh - _rowmean(dxh) - xh * _rowmean(dxh * xh))


def _ln1_fwd(x, a1, g, b):
    def fn(r, c):
        xh, _ = _ln_stats(ALPHA * r[0] + r[1])
        return [xh * c[0] + c[1]], []

    return _rowwise(fn, [x, a1], [g, b], [(D_MODEL, F32)], name="ln1_fwd")[0]


def _ln1_bwd(x, a1, d_h1, g):
    def fn(r, c):
        xh, rr = _ln_stats(ALPHA * r[0] + r[1])
        dy = r[2]
        dz = _ln_bwd(dy, xh, rr, c[0])
        return [dz, ALPHA * dz], [_colsum(dy * xh), _colsum(dy)]

    return _rowwise(fn, [x, a1, d_h1], [g], [(D_MODEL, F32), (D_MODEL, F32)], accs=[(1, D_MODEL), (1, D_MODEL)],
                    name="ln1_bwd")


def _act_fwd(gu):
    def fn(r, c):
        gt, up = r[0][:, :FFN_HIDDEN], r[0][:, FFN_HIDDEN:]
        return [gt * _sig(gt) * up], []

    return _rowwise(fn, [gu], [], [(FFN_HIDDEN, F32)], name="act_fwd")[0]


def _act_bwd(gu, d_act):
    def fn(r, c):
        gt, up = r[0][:, :FFN_HIDDEN], r[0][:, FFN_HIDDEN:]
        da = r[1]
        return [jnp.concatenate([da * up * _silu_grad(gt), da * gt * _sig(gt)], axis=1)], []

    return _rowwise(fn, [gu, d_act], [], [(2 * FFN_HIDDEN, F32)], name="act_bwd")[0]


def _tail(h1, ffn, pg, pp, tgt, g, b):
    def fn(r, c):
        h1_, ffn_, pg_, pp_, t_ = r
        sp = _sig(pg_)
        xh, rr = _ln_stats(ALPHA * h1_ + ffn_ + sp * pp_)
        y = xh * c[0] + c[1]
        err = y - t_
        dy = err * (1.0 / D_MODEL)
        dz = _ln_bwd(dy, xh, rr, c[0])
        loss = jnp.sum(0.5 * _rowmean(err * err), axis=0, keepdims=True)
        return ([dz, dz * pp_ * sp * (1.0 - sp), dz * sp, ALPHA * dz],
                [_colsum(dy * xh), _colsum(dy), jnp.broadcast_to(loss, (1, LANES))])

    return _rowwise(fn, [h1, ffn, pg, pp, tgt], [g, b], [(D_MODEL, F32)] * 4,
                    accs=[(1, D_MODEL), (1, D_MODEL), (1, LANES)], name="tail")


def _pad_heads_to_lanes(per_head_lane0, s_dim):
    t = jnp.transpose(per_head_lane0[:, :, 0])
    return jnp.pad(t, ((0, 0), (0, LANES - N_HEADS)))


def _local_step(x, p, pos, tgt, w):
    s_dim = x.shape[0]
    proj_a = _mm(x, w["wa"], name="f_proj_a")
    proj_g = _mm(x, w["wg"], name="f_proj_g")
    proj_b = _mm(x, w["wb"], name="f_proj_b")
    qkvn = _conv_fwd(proj_a, w["conv"])
    beta, gc = _gates_fwd(proj_b, w["alog"], w["dtb"])
    gc_t = jnp.transpose(gc[:, :N_HEADS])
    u, w_, qd, kt, a_mat = _gdr_prep_fwd(qkvn, beta, gc, gc_t)
    o_dn, states = _gdr_scan_fwd(u, w_, qd, kt, a_mat, gc)
    og = _gdr_out_fwd(o_dn, proj_a, w["dnw"])
    y_dn = _mm(og, w["br_dn"], name="f_y_dn")
    c_q, c_kv = _mla_norm_fwd(proj_b, w["qnw"], w["kvnw"])
    q_full = _mm(c_q, w["uq"], name="f_q_full")
    k_nope = _mm(c_kv, w["uk"], name="f_k_nope")
    vv = _mm(c_kv, w["uv"], out_dtype=BF16, name="f_v")
    qn, qr, kn, kr = _mla_qk_fwd(q_full, k_nope, proj_b, pos)
    o_mla, lse = _attn_fwd(qn, qr, kn, kr, vv)
    y_mla = _mm(o_mla, w["br_mla"], name="f_y_mla")
    mixed = _merge_fwd(y_dn, y_mla, proj_g)
    a1 = _mm(mixed, w["wo"], name="f_a1")
    h1 = _ln1_fwd(x, a1, w["ln1g"], w["ln1b"])
    gu = _mm(h1, w["ffn_in"], name="f_gu")
    act = _act_fwd(gu)
    ffn = _mm(act, w["ffn_out"], name="f_ffn")
    pg = _mm(h1, w["ple_gate"], name="f_pg")
    pp = _mm(p, w["ple"], name="f_pp")
    g = {}
    dz2, d_pg, d_pp, dh1a, g["ln2g"], g["ln2b"], loss = _tail(h1, ffn, pg, pp, tgt, w["ln2g"], w["ln2b"])
    g["ple"] = _mm(p, d_pp, ta=True, name="b_w_ple")
    g["ple_gate"] = _mm(h1, d_pg, ta=True, name="b_w_ple_gate")
    g["ffn_out"] = _mm(act, dz2, ta=True, name="b_w_ffn_out")
    d_act = _mm(dz2, w["ffn_out"], tb=True, name="b_act")
    d_gu = _act_bwd(gu, d_act)
    g["ffn_in"] = _mm(h1, d_gu, ta=True, name="b_w_ffn_in")
    d_h1 = _mm(d_gu, w["ffn_in"], tb=True, add=(dh1a,), name="b_h1_ffn")
    d_h1 = _mm(d_pg, w["ple_gate"], tb=True, add=(d_h1,), name="b_h1_ple")
    dz1, dxa, g["ln1g"], g["ln1b"] = _ln1_bwd(x, a1, d_h1, w["ln1g"])
    g["wo"] = _mm(mixed, dz1, ta=True, name="b_w_o")
    d_mixed = _mm(dz1, w["wo"], tb=True, name="b_mixed")
    d_proj_g, d_y_dn, d_y_mla = _merge_bwd(y_dn, y_mla, proj_g, d_mixed)
    g["br_mla"] = _mm(o_mla, d_y_mla, ta=True, name="b_w_br_mla")
    d_o_mla = _mm(d_y_mla, w["br_mla"], tb=True, name="b_o_mla")
    delta = _attn_delta(o_mla, d_o_mla)
    d_kn, d_kr_heads, d_v = _attn_bwd_kv(qn, qr, kn, kr, vv, d_o_mla, lse, delta)
    d_qn, d_qr = _attn_bwd_q(qn, qr, kn, kr, vv, d_o_mla, lse, delta)
    d_q_full, d_kr = _mla_qk_bwd(d_qn, d_qr, d_kr_heads, pos)
    g["uq"] = _mm(c_q, d_q_full, ta=True, name="b_w_uq")
    d_c_q = _mm(d_q_full, w["uq"], tb=True, name="b_c_q")
    g["uk"] = _mm(c_kv, d_kn, ta=True, name="b_w_uk")
    g["uv"] = _mm(c_kv, d_v, ta=True, name="b_w_uv")
    d_c_kv = _mm(d_kn, w["uk"], tb=True, name="b_c_kv_k")
    d_c_kv = _mm(d_v, w["uv"], tb=True, add=(d_c_kv,), name="b_c_kv_v")
    d_cq, d_ckv, g["qnw"], g["kvnw"] = _mla_norm_bwd(proj_b, w["qnw"], w["kvnw"], d_c_q, d_c_kv)
    g["br_dn"] = _mm(og, d_y_dn, ta=True, name="b_w_br_dn")
    d_og = _mm(d_y_dn, w["br_dn"], tb=True, name="b_og")
    d_o_dn, d_z, g["dnw"] = _gdr_out_bwd(o_dn, proj_a, d_og, w["dnw"])
    du, dw, dqd, dkt, d_a, d_egl = _gdr_scan_bwd(u, w_, qd, kt, a_mat, gc, states, d_o_dn)
    dq, dk, dv, d_beta_h, d_gc_h = _gdr_prep_bwd(qkvn, beta, gc, gc_t, du, dw, dqd, dkt, d_a)
    d_beta = _pad_heads_to_lanes(d_beta_h, s_dim)
    d_gc = _pad_heads_to_lanes(d_gc_h, s_dim)
    d_egl_rows = jnp.pad(d_egl[:, None, :, 0], ((0, 0), (CHUNK - 1, 0), (0, LANES - N_HEADS))).reshape(s_dim, LANES)
    d_b, d_a_raw, g["alog"], g["dtb"] = _gates_bwd(proj_b, w["alog"], w["dtb"], gc, d_beta, d_gc, d_egl_rows)
    d_qkv, g["conv"] = _conv_bwd(proj_a, w["conv"], dq, dk, dv)
    zeros = jnp.zeros((s_dim, WB_CKV - Q_LORA), F32)
    d_proj_b = jnp.concatenate([d_cq, zeros, d_ckv, d_kr, d_b, d_a_raw], axis=1)
    g["wa_qkv"] = _mm(x, d_qkv, ta=True, name="b_w_qkv")
    g["wa_z"] = _mm(x, d_z, ta=True, name="b_w_z")
    g["wg"] = _mm(x, d_proj_g, ta=True, name="b_w_g")
    g["wb"] = _mm(x, d_proj_b, ta=True, name="b_w_b")
    dx = _mm(d_qkv, w["wa_qkv"], tb=True, add=(dxa,), name="b_x_qkv")
    dx = _mm(d_z, w["wa_z"], tb=True, add=(dx,), name="b_x_z")
    dx = _mm(d_proj_g, w["wg"], tb=True, add=(dx,), name="b_x_g")
    dx = _mm(d_proj_b, w["wb"], tb=True, add=(dx,), name="b_x_b")
    return loss, dx, g


_BIG = (("w_in", 1), ("conv_w", 1), ("w_uq", 0), ("w_uk", 0), ("w_uv", 0), ("w_br_dn", 0), ("w_br_mla", 0),
        ("w_o", 0), ("w_ffn_in", 1), ("w_ffn_out", 0), ("w_ple", 1), ("w_ple_gate", 0))
_SMALL = ("ln1_g", "ln1_b", "ln2_g", "ln2_b", "q_norm_w", "kv_norm_w", "dn_norm_w", "dn_a_log", "dn_dt_bias")
_ORDER = ("w_in", "conv_w", "dn_a_log", "dn_dt_bias", "dn_norm_w", "q_norm_w", "w_uq", "kv_norm_w", "w_uk", "w_uv",
          "w_br_dn", "w_br_mla", "w_o", "ln1_g", "ln1_b", "w_ffn_in", "w_ffn_out", "w_ple", "w_ple_gate", "ln2_g",
          "ln2_b")


ROW_ALIGN = 16


def _flat_rows(shape):
    rows = -(-int(np.prod(shape)) // FLAT_COLS)
    return -(-rows // ROW_ALIGN) * ROW_ALIGN


def _flat_layout(shard_shapes):
    rows, off = {}, 0
    for name, _ in _BIG:
        n = _flat_rows(shard_shapes[name])
        rows[name] = (off, n)
        off += n
    total = -(-off // FLAT_TILE) * FLAT_TILE
    return rows, total


def _to_rows(a, lead):
    lead_shape = a.shape[:lead]
    flat = a.reshape(lead_shape + (-1,))
    n = flat.shape[-1]
    flat = jnp.pad(flat, [(0, 0)] * lead + [(0, _flat_rows((n,)) * FLAT_COLS - n)])
    return flat.reshape(lead_shape + (-1, FLAT_COLS))


def _pack_shards(shards, total_rows):
    parts = [_to_rows(shards[name], 0) for name, _ in _BIG]
    flat = jnp.concatenate(parts, axis=0)
    return jnp.pad(flat, ((0, total_rows - flat.shape[0]), (0, 0)))


def _unpack_shards(flat, shard_shapes, layout):
    out = {}
    for name, _ in _BIG:
        off, n = layout[name]
        shp = shard_shapes[name]
        out[name] = flat[off:off + n].reshape(-1)[:int(np.prod(shp))].reshape(shp)
    return out


def _unpack_gathered(gathered, shard_shapes, layout):
    out = {}
    for name, axis in _BIG:
        off, n = layout[name]
        shp = shard_shapes[name]
        sh = gathered[:, off:off + n].reshape(N_DEV, -1)[:, :int(np.prod(shp))].reshape((N_DEV,) + shp)
        sh = jnp.moveaxis(sh, 0, axis)
        out[name] = sh.reshape(shp[:axis] + (N_DEV * shp[axis],) + shp[axis + 1:])
    return out


def _pack_grads_for_parity(full_grads, shard_shapes, layout, total_rows, parity):
    parts = []
    for name, axis in _BIG:
        shp = shard_shapes[name]
        gfull = full_grads[name]
        gsh = gfull.reshape(shp[:axis] + (4, 2, shp[axis]) + shp[axis + 1:])
        gsh = lax.dynamic_index_in_dim(gsh, parity, axis=axis + 1, keepdims=False)
        gsh = jnp.moveaxis(gsh, axis, 0)
        parts.append(_to_rows(gsh, 1))
    flat = jnp.concatenate(parts, axis=1)
    return jnp.pad(flat, ((0, 0), (0, total_rows - flat.shape[1]), (0, 0)))


def _full_weights(fw, small):
    w_in = fw["w_in"]
    c = np.cumsum([0, 3072, 1024, 8, 8, Q_LORA, KV_LORA, ROPE, D_MODEL, D_MODEL])
    qkv, z, b, a, cq, ckv, kr, gd, gm = (w_in[:, c[i]:c[i + 1]] for i in range(9))
    zc = lambda n: jnp.zeros((D_MODEL, n), w_in.dtype)
    w = {}
    w["wa_qkv"], w["wa_z"] = qkv, z
    w["wa"] = jnp.concatenate([qkv, z], axis=1)
    w["wg"] = jnp.concatenate([gd, gm], axis=1)
    w["wb"] = jnp.concatenate([cq, zc(WB_CKV - Q_LORA), ckv, kr, zc(LANES - ROPE), b, zc(LANES - N_HEADS), a,
                               zc(LANES - N_HEADS)], axis=1)
    uq = fw["w_uq"]
    uq_r = jnp.pad(uq[:, :, HEAD:], ((0, 0), (0, 0), (0, HEAD - ROPE)))
    w["uq"] = jnp.concatenate([uq[:, :, :HEAD].reshape(Q_LORA, -1), uq_r.reshape(Q_LORA, -1)], axis=1)
    w["uk"] = fw["w_uk"].reshape(KV_LORA, -1)
    w["uv"] = fw["w_uv"].reshape(KV_LORA, -1)
    w["conv"] = fw["conv_w"].astype(F32)
    for k_, n_ in (("br_dn", "w_br_dn"), ("br_mla", "w_br_mla"), ("wo", "w_o"), ("ffn_in", "w_ffn_in"),
                   ("ffn_out", "w_ffn_out"), ("ple", "w_ple"), ("ple_gate", "w_ple_gate")):
        w[k_] = fw[n_]
    pad_l = lambda v: jnp.pad(v, ((0, 0), (0, LANES - v.shape[1])))
    w["alog"], w["dtb"] = pad_l(small["dn_a_log"]), pad_l(small["dn_dt_bias"])
    w["dnw"], w["qnw"], w["kvnw"] = small["dn_norm_w"], small["q_norm_w"], small["kv_norm_w"]
    w["ln1g"], w["ln1b"], w["ln2g"], w["ln2b"] = small["ln1_g"], small["ln1_b"], small["ln2_g"], small["ln2_b"]
    return w


def _full_grads(g):
    wb = g["wb"]
    full = {}
    full["w_in"] = jnp.concatenate([
        g["wa_qkv"], g["wa_z"], wb[:, WB_B:WB_B + N_HEADS], wb[:, WB_A:WB_A + N_HEADS], wb[:, WB_CQ:WB_CQ + Q_LORA],
        wb[:, WB_CKV:WB_CKV + KV_LORA], wb[:, WB_KR:WB_KR + ROPE], g["wg"]], axis=1)
    uq = g["uq"]
    uq_n = uq[:, :D_MODEL].reshape(Q_LORA, N_HEADS, HEAD)
    uq_r = uq[:, D_MODEL:].reshape(Q_LORA, N_HEADS, HEAD)[:, :, :ROPE]
    full["w_uq"] = jnp.concatenate([uq_n, uq_r], axis=2)
    full["w_uk"] = g["uk"].reshape(KV_LORA, N_HEADS, HEAD)
    full["w_uv"] = g["uv"].reshape(KV_LORA, N_HEADS, HEAD)
    full["conv_w"] = g["conv"]
    for k_, n_ in (("br_dn", "w_br_dn"), ("br_mla", "w_br_mla"), ("wo", "w_o"), ("ffn_in", "w_ffn_in"),
                   ("ffn_out", "w_ffn_out"), ("ple", "w_ple"), ("ple_gate", "w_ple_gate")):
        full[n_] = g[k_]
    small = {"ln1_g": g["ln1g"], "ln1_b": g["ln1b"], "ln2_g": g["ln2g"], "ln2_b": g["ln2b"], "q_norm_w": g["qnw"],
             "kv_norm_w": g["kvnw"], "dn_norm_w": g["dnw"], "dn_a_log": g["alog"][:, :N_HEADS],
             "dn_dt_bias": g["dtb"][:, :N_HEADS]}
    return full, small


_SMALL_SLOTS = {"ln1_g": (0, 0, 1024), "ln1_b": (1, 0, 1024), "ln2_g": (2, 0, 1024), "ln2_b": (3, 0, 1024),
                "q_norm_w": (4, 0, 384), "kv_norm_w": (4, 384, 256), "dn_norm_w": (4, 640, 128),
                "dn_a_log": (4, 768, 8), "dn_dt_bias": (4, 776, 8)}
_LOSS_SLOT = (4, 896)


def _pack_small(vals, loss=None):
    blk = jnp.zeros((8, FLAT_COLS), F32)
    for name, (r, c, n) in _SMALL_SLOTS.items():
        blk = lax.dynamic_update_slice(blk, vals[name].reshape(1, n).astype(F32), (r, c))
    if loss is not None:
        blk = lax.dynamic_update_slice(blk, loss[:, :1], _LOSS_SLOT)
    return blk


def _unpack_small(blk, shapes):
    return {name: blk[r:r + 1, c:c + n].reshape(shapes[name]) for name, (r, c, n) in _SMALL_SLOTS.items()}


_MESH_ID = pl.DeviceIdType.MESH
_ANY = pl.BlockSpec(memory_space=pl.ANY)


def _all_gather(block, name):
    def body(x_ref, out_ref, send_sems, recv_sems, local_sem):
        x, y, c = lax.axis_index("x"), lax.axis_index("y"), lax.axis_index("c")
        me, sibling = (x, y, c), (x, y, 1 - c)
        chips = [(1 - x, y), (x, 1 - y), (1 - x, 1 - y)]

        def slot(px, py, pc):
            return out_ref.at[4 * px + 2 * py + pc]

        def copy(k, origin, to, src=None):
            return pltpu.make_async_remote_copy(
                src_ref=slot(*origin) if src is None else src, dst_ref=slot(*origin), send_sem=send_sems.at[k],
                recv_sem=recv_sems.at[k], device_id=to, device_id_type=_MESH_ID)

        mine = pltpu.make_async_copy(x_ref, slot(*me), local_sem)
        mine.start()
        first = [copy(0, me, sibling, src=x_ref)]
        first += [copy(1 + j, me, (*chip, c), src=x_ref) for j, chip in enumerate(chips)]
        for cp in first:
            cp.start()
        passed = [copy(4 + j, (*chip, c), sibling) for j, chip in enumerate(chips)]
        for j, chip in enumerate(chips):
            copy(1 + j, (*chip, c), me).wait_recv()
            passed[j].start()
        copy(0, sibling, me).wait_recv()
        for j, chip in enumerate(chips):
            copy(4 + j, (*chip, 1 - c), me).wait_recv()
        for cp in first + passed:
            cp.wait_send()
        mine.wait()

    return pl.pallas_call(
        body,
        out_shape=jax.ShapeDtypeStruct((N_DEV,) + block.shape, block.dtype),
        in_specs=[_ANY],
        out_specs=_ANY,
        scratch_shapes=[pltpu.SemaphoreType.DMA((7,)), pltpu.SemaphoreType.DMA((7,)), pltpu.SemaphoreType.DMA],
        name=name,
    )(block)


def _exchange_sibling(src, name):
    def body(src_ref, dst_ref, send_sems, recv_sems):
        x, y, c = lax.axis_index("x"), lax.axis_index("y"), lax.axis_index("c")
        copies = [pltpu.make_async_remote_copy(
            src_ref=src_ref.at[q], dst_ref=dst_ref.at[q], send_sem=send_sems.at[q], recv_sem=recv_sems.at[q],
            device_id=(x, y, 1 - c), device_id_type=_MESH_ID) for q in range(4)]
        for cp in copies:
            cp.start()
        for cp in copies:
            cp.wait_recv()
        for cp in copies:
            cp.wait_send()

    return pl.pallas_call(
        body,
        out_shape=jax.ShapeDtypeStruct(src.shape, src.dtype),
        in_specs=[_ANY],
        out_specs=_ANY,
        scratch_shapes=[pltpu.SemaphoreType.DMA((4,)), pltpu.SemaphoreType.DMA((4,))],
        name=name,
    )(src)


def _exchange_chips(src, name):
    def body(src_ref, dst_ref, send_sems, recv_sems):
        x, y, c = lax.axis_index("x"), lax.axis_index("y"), lax.axis_index("c")
        chips = [(1 - x, y), (x, 1 - y), (1 - x, 1 - y)]
        copies = [pltpu.make_async_remote_copy(
            src_ref=src_ref.at[2 * tx + ty], dst_ref=dst_ref.at[j], send_sem=send_sems.at[j],
            recv_sem=recv_sems.at[j], device_id=(tx, ty, c), device_id_type=_MESH_ID)
            for j, (tx, ty) in enumerate(chips)]
        for cp in copies:
            cp.start()
        for cp in copies:
            cp.wait_recv()
        for cp in copies:
            cp.wait_send()

    return pl.pallas_call(
        body,
        out_shape=jax.ShapeDtypeStruct((3,) + src.shape[1:], src.dtype),
        in_specs=[_ANY],
        out_specs=_ANY,
        scratch_shapes=[pltpu.SemaphoreType.DMA((3,)), pltpu.SemaphoreType.DMA((3,))],
        name=name,
    )(src)


def _add_pairs(a, b, name):
    n, r, c = a.shape

    def fn(rows, consts):
        return [rows[0] + rows[1]], []

    out = _rowwise(fn, [a.reshape(n * r, c), b.reshape(n * r, c)], [], [(c, F32)], tm=FLAT_TILE, name=name)[0]
    return out.reshape(n, r, c)


def _adamw_math(w, g, m, v):
    m = ADAM_B1 * m + (1.0 - ADAM_B1) * g
    v = ADAM_B2 * v + (1.0 - ADAM_B2) * (g * g)
    m_hat = m / (1.0 - ADAM_B1 ** ADAM_STEP)
    v_hat = v / (1.0 - ADAM_B2 ** ADAM_STEP)
    delta = -ADAM_LR * (m_hat / (jnp.sqrt(v_hat) + ADAM_EPS) + ADAM_WD * w)
    return delta, m, v


def _adamw_flat(w, m, v, g_parts):
    def fn(rows, consts):
        w_, m_, v_ = rows[:3]
        g = rows[3]
        for part in rows[4:]:
            g = g + part
        delta, m2, v2 = _adamw_math(w_, g, m_, v_)
        return [g, delta, m2, v2], []

    return _rowwise(fn, [w, m, v] + list(g_parts), [], [(FLAT_COLS, F32)] * 4, tm=FLAT_TILE, name="adamw_flat")


def _adamw_small(w, m, v, gathered):
    def body(w_ref, m_ref, v_ref, g_ref, go_ref, d_ref, m2_ref, v2_ref):
        g = g_ref[0]
        for k in range(1, N_DEV):
            g = g + g_ref[k]
        delta, m2, v2 = _adamw_math(w_ref[...], g, m_ref[...], v_ref[...])
        go_ref[...] = g
        d_ref[...] = delta
        m2_ref[...] = m2
        v2_ref[...] = v2

    blk = jax.ShapeDtypeStruct((8, FLAT_COLS), F32)
    return pl.pallas_call(body, out_shape=[blk] * 4, name="adamw_small")(w, m, v, gathered)


def kernel(x, p, positions, w_in, conv_w, dn_a_log, dn_dt_bias, dn_norm_w, q_norm_w, w_uq, kv_norm_w, w_uk, w_uv, w_br_dn, w_br_mla, w_o, ln1_g, ln1_b, w_ffn_in, w_ffn_out, w_ple, w_ple_gate, ln2_g, ln2_b, loss_target, m_w_in, m_conv_w, m_dn_a_log, m_dn_dt_bias, m_dn_norm_w, m_q_norm_w, m_w_uq, m_kv_norm_w, m_w_uk, m_w_uv, m_w_br_dn, m_w_br_mla, m_w_o, m_ln1_g, m_ln1_b, m_w_ffn_in, m_w_ffn_out, m_w_ple, m_w_ple_gate, m_ln2_g, m_ln2_b, v_w_in, v_conv_w, v_dn_a_log, v_dn_dt_bias, v_dn_norm_w, v_q_norm_w, v_w_uq, v_kv_norm_w, v_w_uk, v_w_uv, v_w_br_dn, v_w_br_mla, v_w_o, v_ln1_g, v_ln1_b, v_w_ffn_in, v_w_ffn_out, v_w_ple, v_w_ple_gate, v_ln2_g, v_ln2_b):
    args = dict(locals())
    wts = {n: args[n] for n in _ORDER}
    mom1 = {n: args["m_" + n] for n in _ORDER}
    mom2 = {n: args["v_" + n] for n in _ORDER}
    big_names = [n for n, _ in _BIG]
    shard_shapes = {n: wts[n].shape[1:] for n in big_names}
    layout, total_rows = _flat_layout(shard_shapes)
    drop = lambda d, names: {n: d[n][0] for n in names}

    w_flat = _pack_shards(drop(wts, big_names), total_rows)
    gathered = _all_gather(w_flat.astype(BF16), "ag_weights")
    full_w = _unpack_gathered(gathered, shard_shapes, layout)
    small_w = {n: wts[n].astype(F32) for n in _SMALL}
    w = _full_weights(full_w, small_w)

    s_dim = x.shape[1]
    loss, dx, g = _local_step(x[0], p[0, 0], positions.reshape(s_dim, 1).astype(F32), loss_target[0], w)
    full_g, small_g = _full_grads(g)

    c_idx = lax.axis_index("c")
    q_idx = 2 * lax.axis_index("x") + lax.axis_index("y")
    g_own = _pack_grads_for_parity(full_g, shard_shapes, layout, total_rows, c_idx)
    g_sib = _pack_grads_for_parity(full_g, shard_shapes, layout, total_rows, 1 - c_idx)
    from_sibling = _exchange_sibling(g_sib, "rs_sibling")
    chip_sum = _add_pairs(g_own, from_sibling, "rs_chip_sum")
    from_chips = _exchange_chips(chip_sum, "rs_chips")
    mine = lax.dynamic_index_in_dim(chip_sum, q_idx, axis=0, keepdims=False)

    m_flat = _pack_shards(drop(mom1, big_names), total_rows)
    v_flat = _pack_shards(drop(mom2, big_names), total_rows)
    g_flat, d_flat, m2_flat, v2_flat = _adamw_flat(w_flat, m_flat, v_flat,
                                                   [mine, from_chips[0], from_chips[1], from_chips[2]])
    out_g = _unpack_shards(g_flat, shard_shapes, layout)
    out_d = _unpack_shards(d_flat, shard_shapes, layout)
    out_m = _unpack_shards(m2_flat, shard_shapes, layout)
    out_v = _unpack_shards(v2_flat, shard_shapes, layout)

    small_shapes = {n: wts[n].shape for n in _SMALL}
    g_small = _all_gather(_pack_small(small_g, loss), "ag_small")
    sg, sd, sm2, sv2 = _adamw_small(_pack_small(small_w), _pack_small({n: mom1[n] for n in _SMALL}),
                                    _pack_small({n: mom2[n] for n in _SMALL}), g_small)
    for blk, dst in ((sg, out_g), (sd, out_d), (sm2, out_m), (sv2, out_v)):
        dst.update(_unpack_small(blk, small_shapes))
    loss_out = sg[_LOSS_SLOT[0], _LOSS_SLOT[1]]

    expand = lambda d, n: d[n] if n in _SMALL else d[n][None]
    return (loss_out, dx[None], *[expand(out_g, n) for n in _ORDER], *[expand(out_d, n) for n in _ORDER],
            *[expand(out_m, n) for n in _ORDER], *[expand(out_v, n) for n in _ORDER])
```

```python
import functools
import math

import numpy as np
import jax
import jax.numpy as jnp
from jax import lax
from jax.experimental import pallas as pl
from jax.experimental.pallas import tpu as pltpu

F32 = jnp.float32
BF16 = jnp.bfloat16

D_MODEL = 1024
N_HEADS = 8
HEAD = 128
CHUNK = 64
GROUP = 256
ROPE = 64
Q_LORA = 384
KV_LORA = 256
FFN_HIDDEN = 2816
PLE_DIM = 256
ROPE_BASE = 10000.0
ALPHA = 2.0 ** 0.25
SCALE = float((HEAD + ROPE) ** -0.5)
NEG_BIG = -1e30
EPS_RMS = 1e-6
EPS_LN = 1e-5

ADAM_LR = 0.001
ADAM_B1 = 0.9
ADAM_B2 = 0.999
ADAM_EPS = 1e-08
ADAM_WD = 0.01
ADAM_STEP = 10

N_DEV = 8
LANES = 128
FLAT_COLS = 1024
FLAT_TILE = 128

WB_CQ, WB_CKV, WB_KR, WB_BA, WB_COLS = 0, 512, 768, 896, 1024

HIGHEST = lax.Precision.HIGHEST

NN = (((1,), (0,)), ((), ()))
TN = (((0,), (0,)), ((), ()))
NT = (((1,), (1,)), ((), ()))


def _dot(a, b, dims=NN):
    return lax.dot_general(a.astype(BF16), b.astype(BF16), dims, preferred_element_type=F32)


def _dot32(a, b, dims=NN):
    return lax.dot_general(a, b, dims, precision=HIGHEST, preferred_element_type=F32)


def _sig(x):
    return 1.0 / (1.0 + jnp.exp(-x))


def _pick(n, pref, unit):
    if n <= pref:
        return n
    t = (pref // unit) * unit
    while t >= unit:
        if n % t == 0:
            return t
        t -= unit
    return n


def _pick_wide(n):
    if n <= 1024:
        return n
    cands = [t for t in range(LANES, 1536 + 1, LANES) if n % t == 0]
    best = max(t for t in cands if t <= 1024)
    if best < 512 and cands[-1] > 1024:
        return cands[-1]
    return best


def _split_bf16(a):
    hi = a.astype(BF16)
    return hi, (a - hi.astype(F32)).astype(BF16)


def _dot3(a, b, dims=NN):
    ah, al = a if isinstance(a, tuple) else _split_bf16(a)
    bh, bl = b if isinstance(b, tuple) else _split_bf16(b)
    d = lambda p, q: lax.dot_general(p, q, dims, preferred_element_type=F32)
    return d(ah, bh) + (d(ah, bl) + d(al, bh))


def _mm(a, b, *, ta=False, tb=False, add=(), out_dtype=F32, name):
    if ta:
        k_dim, m_dim = a.shape
    else:
        m_dim, k_dim = a.shape
    if tb:
        n_dim, k2 = b.shape
    else:
        k2, n_dim = b.shape
    assert k_dim == k2, (a.shape, b.shape, ta, tb)
    tm = _pick_wide(m_dim)
    tn = _pick_wide(n_dim)
    tk = _pick(k_dim, 512, LANES)
    nk = k_dim // tk
    n_add = len(add)
    dims = TN if ta else (NT if tb else NN)
    assert not (ta and tb)

    def body(a_ref, b_ref, *rest):
        add_refs = rest[:n_add]
        o_ref = rest[n_add]
        acc = rest[n_add + 1]
        k = pl.program_id(2)

        @pl.when(k == 0)
        def _():
            acc[...] = jnp.zeros_like(acc)

        acc[...] += _dot(a_ref[...], b_ref[...], dims)

        @pl.when(k == nk - 1)
        def _():
            r = acc[...]
            for ar in add_refs:
                r = r + ar[...].astype(F32)
            o_ref[...] = r.astype(o_ref.dtype)

    a_spec = pl.BlockSpec((tk, tm), lambda i, j, k: (k, i)) if ta else pl.BlockSpec((tm, tk), lambda i, j, k: (i, k))
    b_spec = pl.BlockSpec((tn, tk), lambda i, j, k: (j, k)) if tb else pl.BlockSpec((tk, tn), lambda i, j, k: (k, j))
    o_spec = pl.BlockSpec((tm, tn), lambda i, j, k: (i, j))
    return pl.pallas_call(
        body,
        out_shape=jax.ShapeDtypeStruct((m_dim, n_dim), out_dtype),
        grid=(m_dim // tm, n_dim // tn, nk),
        in_specs=[a_spec, b_spec] + [o_spec] * n_add,
        out_specs=o_spec,
        scratch_shapes=[pltpu.VMEM((tm, tn), F32)],
        compiler_params=pltpu.CompilerParams(dimension_semantics=("parallel", "parallel", "arbitrary")),
        name=name,
    )(a, b, *add)


def _rowwise(fn, rows, consts, outs, accs=(), *, tm=256, name):
    rows = [r if isinstance(r, tuple) else (r, 0, r.shape[1]) for r in rows]
    s_dim = rows[0][0].shape[0]
    tm = min(tm, s_dim)
    assert s_dim % tm == 0 and all(arr.shape[0] == s_dim for arr, _, _ in rows)
    specs = [pl.BlockSpec((tm, width), functools.partial(lambda i, cb: (i, cb), cb=cb)) for _, cb, width in rows]
    args = [arr for arr, _, _ in rows]
    for c in consts:
        specs.append(pl.BlockSpec(c.shape, lambda i: (0, 0)))
        args.append(c)
    nr, nc, no = len(rows), len(consts), len(outs)
    out_shape = [jax.ShapeDtypeStruct((s_dim, w), dt) for (w, dt) in outs]
    out_specs = [pl.BlockSpec((tm, w), lambda i: (i, 0)) for (w, dt) in outs]
    out_shape += [jax.ShapeDtypeStruct(sh, F32) for sh in accs]
    out_specs += [pl.BlockSpec(sh, lambda i: (0, 0)) for sh in accs]

    def body(*refs):
        r = [x[...] for x in refs[:nr]]
        c = [x[...] for x in refs[nr:nr + nc]]
        o_refs = refs[nr + nc:nr + nc + no]
        a_refs = refs[nr + nc + no:]
        o_vals, a_vals = fn(r, c)
        for ref, v in zip(o_refs, o_vals, strict=True):
            ref[...] = v.astype(ref.dtype)
        if a_refs:
            @pl.when(pl.program_id(0) == 0)
            def _():
                for ref in a_refs:
                    ref[...] = jnp.zeros_like(ref)

            for ref, v in zip(a_refs, a_vals, strict=True):
                ref[...] += v

    res = pl.pallas_call(
        body,
        out_shape=out_shape,
        grid=(s_dim // tm,),
        in_specs=specs,
        out_specs=out_specs,
        compiler_params=pltpu.CompilerParams(dimension_semantics=("arbitrary" if accs else "parallel",)),
        name=name,
    )(*args)
    return res


def _colsum(v):
    return jnp.sum(v, axis=0, keepdims=True)


def _rowsum(v):
    return jnp.sum(v, axis=1, keepdims=True)


def _rowmean(v):
    return jnp.mean(v, axis=1, keepdims=True)


def _silu_grad(x):
    s = _sig(x)
    return s * (1.0 + x * (1.0 - s))


def _conv_taps(x, w, width=4):
    row = lax.broadcasted_iota(jnp.int32, x.shape, 0)
    c = x * w[width - 1:width, :]
    for s in range(1, width):
        c = c + jnp.where(row >= s, pltpu.roll(x, s, 0), 0.0) * w[width - 1 - s:width - s, :]
    return c


def _conv_fwd(proj_a, conv_w):
    s_dim = proj_a.shape[0]
    n_blk = 3 * N_HEADS

    def body(x_ref, w_ref, o_ref):
        j = pl.program_id(0)
        c = _conv_taps(x_ref[...], w_ref[...])
        y = c * _sig(c)
        r = lax.rsqrt(_rowsum(y * y) + EPS_RMS)
        fac = jnp.where(j < N_HEADS, r * (HEAD ** -0.5), jnp.where(j < 2 * N_HEADS, r, 1.0))
        o_ref[...] = y * fac

    return pl.pallas_call(
        body,
        out_shape=jax.ShapeDtypeStruct((s_dim, n_blk * HEAD), F32),
        grid=(n_blk,),
        in_specs=[pl.BlockSpec((s_dim, HEAD), lambda j: (0, j)), pl.BlockSpec((4, HEAD), lambda j: (0, j))],
        out_specs=pl.BlockSpec((s_dim, HEAD), lambda j: (0, j)),
        compiler_params=pltpu.CompilerParams(dimension_semantics=("parallel",)),
        name="conv_fwd",
    )(proj_a, conv_w)


def _conv_bwd(proj_a, conv_w, dq, dk, dv):
    s_dim = proj_a.shape[0]
    n_blk = 3 * N_HEADS

    def body(x_ref, w_ref, dq_ref, dk_ref, dv_ref, dx_ref, dw_ref):
        j = pl.program_id(0)
        x = x_ref[...]
        w = w_ref[...]
        do = jnp.where(j < N_HEADS, dq_ref[...], jnp.where(j < 2 * N_HEADS, dk_ref[...], dv_ref[...]))
        c = _conv_taps(x, w)
        sg = _sig(c)
        y = c * sg
        r = lax.rsqrt(_rowsum(y * y) + EPS_RMS)
        sc = jnp.where(j < N_HEADS, HEAD ** -0.5, 1.0)
        dy_n = sc * (r * do - y * (r * r * r) * _rowsum(do * y))
        dy = jnp.where(j < 2 * N_HEADS, dy_n, do)
        dc = dy * (sg * (1.0 + c * (1.0 - sg)))
        row = lax.broadcasted_iota(jnp.int32, x.shape, 0)
        dx = dc * w[3:4, :]
        dw_ref[3:4, :] = _colsum(dc * x)
        for s in range(1, 4):
            dx = dx + jnp.where(row < s_dim - s, pltpu.roll(dc, s_dim - s, 0), 0.0) * w[3 - s:4 - s, :]
            xs = jnp.where(row >= s, pltpu.roll(x, s, 0), 0.0)
            dw_ref[3 - s:4 - s, :] = _colsum(dc * xs)
        dx_ref[...] = dx.astype(dx_ref.dtype)

    hd = N_HEADS - 1
    return pl.pallas_call(
        body,
        out_shape=[jax.ShapeDtypeStruct((s_dim, n_blk * HEAD), BF16), jax.ShapeDtypeStruct((4, n_blk * HEAD), F32)],
        grid=(n_blk,),
        in_specs=[
            pl.BlockSpec((s_dim, HEAD), lambda j: (0, j)),
            pl.BlockSpec((4, HEAD), lambda j: (0, j)),
            pl.BlockSpec((s_dim, HEAD), lambda j: (0, jnp.minimum(j, hd))),
            pl.BlockSpec((s_dim, HEAD), lambda j: (0, jnp.clip(j - N_HEADS, 0, hd))),
            pl.BlockSpec((s_dim, HEAD), lambda j: (0, jnp.clip(j - 2 * N_HEADS, 0, hd))),
        ],
        out_specs=[pl.BlockSpec((s_dim, HEAD), lambda j: (0, j)), pl.BlockSpec((4, HEAD), lambda j: (0, j))],
        compiler_params=pltpu.CompilerParams(dimension_semantics=("parallel",)),
        name="conv_bwd",
    )(proj_a, conv_w, dq, dk, dv)


def _chunk_tri(n):
    r = np.arange(n)
    m = ((r[:, None] // CHUNK) == (r[None, :] // CHUNK)) & (r[:, None] >= r[None, :])
    m = m.astype(np.float32)
    return jnp.asarray(m), jnp.asarray(m.T)


def _softplus(z):
    return jnp.maximum(z, 0.0) + jnp.log(1.0 + jnp.exp(-jnp.abs(z)))


def _gates_fwd(proj_b, alog, dtb):
    tm = min(GROUP, proj_b.shape[0])
    tri, _ = _chunk_tri(tm)

    def fn(r, c):
        b = r[0]
        a = pltpu.roll(b, LANES - N_HEADS, 1)
        alog_, dtb_, tri_ = c
        g = -jnp.exp(alog_) * _softplus(a + dtb_)
        return [_sig(b), _dot32(tri_, g)], []

    return _rowwise(fn, [(proj_b, WB_BA // LANES, LANES)], [alog, dtb, tri],
                    [(LANES, F32), (LANES, F32)], tm=tm, name="gates_fwd")


def _gates_bwd(proj_b, alog, dtb, gc, d_beta, d_gc, d_egl_rows):
    tm = min(GROUP, proj_b.shape[0])
    _, tri_t = _chunk_tri(tm)

    def fn(r, c):
        b, gc_, d_beta_, d_gc_, d_egl_ = r
        a = pltpu.roll(b, LANES - N_HEADS, 1)
        alog_, dtb_, tri_t_ = c
        z = a + dtb_
        ea = jnp.exp(alog_)
        g = -ea * _softplus(z)
        dg = _dot32(tri_t_, d_gc_ + d_egl_ * jnp.exp(gc_))
        d_a = dg * (-ea) * _sig(z)
        beta = _sig(b)
        d_ba = d_beta_ * beta * (1.0 - beta) + pltpu.roll(d_a, N_HEADS, 1)
        return [d_ba], [_colsum(dg * g), _colsum(d_a)]

    return _rowwise(fn, [(proj_b, WB_BA // LANES, LANES), gc, d_beta, d_gc, d_egl_rows],
                    [alog, dtb, tri_t], [(LANES, BF16)], accs=[(1, LANES), (1, LANES)], tm=tm,
                    name="gates_bwd")


def _group_masks(n):
    r = lax.broadcasted_iota(jnp.int32, (n, n), 0)
    c = lax.broadcasted_iota(jnp.int32, (n, n), 1)
    same = (r // CHUNK) == (c // CHUNK)
    tril = jnp.logical_and(same, r >= c)
    strict = jnp.logical_and(same, r > c)
    last = c == (r // CHUNK) * CHUNK + (CHUNK - 1)
    eye = r == c
    return same, tril, strict, last, eye


def _inv_unit_lower(l_mat, eye_f):
    q = -l_mat
    r = eye_f + q
    qs = _split_bf16(q)
    for _ in range(5):
        qs = _split_bf16(_dot3(qs, qs))
        r = r + _dot3(r, qs)
    return r


def _unfold_blocks(folded, mask):
    n = folded.shape[0]
    return jnp.where(mask, jnp.concatenate([folded] * (n // CHUNK), axis=1), 0.0)


def _head_cols(beta, gc, gc_t, h):
    lane = lax.broadcasted_iota(jnp.int32, beta.shape, 1)
    sub = lax.broadcasted_iota(jnp.int32, gc_t.shape, 0)
    bcol = _rowsum(jnp.where(lane == h, beta, 0.0))
    gcol = _rowsum(jnp.where(lane == h, gc, 0.0))
    grow = _colsum(jnp.where(sub == h, gc_t, 0.0))
    return bcol, gcol, grow


def _prep_common(q, k, bcol, gcol, grow, t_folded=None):
    n = q.shape[0]
    same, tril, strict, last, eye = _group_masks(n)
    decay = jnp.where(tril, jnp.exp(jnp.where(tril, gcol - grow, 0.0)), 0.0)
    glast = _rowsum(jnp.where(last, jnp.broadcast_to(grow, (n, n)), 0.0))
    e = jnp.exp(gcol)
    ekt = jnp.exp(glast - gcol)
    kb = k * bcol
    kk = _dot(kb, k, NT)
    if t_folded is None:
        t_mat = _inv_unit_lower(jnp.where(strict, kk * decay, 0.0), eye.astype(F32))
    else:
        t_mat = _unfold_blocks(t_folded, same)
    qk = _dot(q, k, NT)
    return dict(same=same, tril=tril, strict=strict, last=last, eye=eye, decay=decay, e=e, ekt=ekt, kb=kb, kk=kk,
                t=t_mat, qk=qk)


def _fold_blocks(m):
    n = m.shape[0]
    out = m[:, 0:CHUNK]
    for b in range(1, n // CHUNK):
        out = out + m[:, b * CHUNK:(b + 1) * CHUNK]
    return out


def _gdr_prep_fwd(qkvn, beta, gc, gc_t):
    s_dim = qkvn.shape[0]
    tg = min(GROUP, s_dim)

    def body(q_ref, k_ref, v_ref, b_ref, g_ref, gt_ref, u_ref, w_ref, qd_ref, kt_ref, a_ref, t_ref):
        h = pl.program_id(0)
        q, k, v = q_ref[...], k_ref[...], v_ref[...]
        bcol, gcol, grow = _head_cols(b_ref[...], g_ref[...], gt_ref[...], h)
        p = _prep_common(q, k, bcol, gcol, grow)
        u_ref[...] = _dot(p["t"], v * bcol)
        w_ref[...] = _dot(p["t"], p["kb"] * p["e"])
        qd_ref[...] = q * p["e"]
        kt_ref[...] = k * p["ekt"]
        a_ref[...] = _fold_blocks(jnp.where(p["tril"], p["qk"] * p["decay"], 0.0))
        t_ref[...] = _fold_blocks(p["t"])

    row = lambda off: pl.BlockSpec((tg, HEAD), functools.partial(lambda h, m, off: (m, h + off), off=off))
    full = pl.BlockSpec((tg, LANES), lambda h, m: (m, 0))
    o_spec = pl.BlockSpec((tg, HEAD), lambda h, m: (m, h))
    a_spec = pl.BlockSpec((None, tg, CHUNK), lambda h, m: (h, m, 0))
    wide = jax.ShapeDtypeStruct((s_dim, N_HEADS * HEAD), F32)
    folded = jax.ShapeDtypeStruct((N_HEADS, s_dim, CHUNK), F32)
    return pl.pallas_call(
        body,
        out_shape=[wide, wide, wide, wide, folded, folded],
        grid=(N_HEADS, s_dim // tg),
        in_specs=[row(0), row(N_HEADS), row(2 * N_HEADS), full, full, pl.BlockSpec((8, tg), lambda h, m: (0, m))],
        out_specs=[o_spec, o_spec, o_spec, o_spec, a_spec, a_spec],
        compiler_params=pltpu.CompilerParams(dimension_semantics=("parallel", "parallel")),
        name="gdr_prep_fwd",
    )(qkvn, qkvn, qkvn, beta, gc, gc_t)


def _gdr_prep_bwd(qkvn, beta, gc, gc_t, t_fold, du, dw, dqd, dkt, d_a):
    s_dim = qkvn.shape[0]
    tg = min(GROUP, s_dim)

    def body(q_ref, k_ref, v_ref, b_ref, g_ref, gt_ref, t_ref, du_ref, dw_ref, dqd_ref, dkt_ref, da_ref,
             dq_ref, dk_ref, dv_ref, db_ref, dg_ref):
        h = pl.program_id(1)
        q, k, v = q_ref[...], k_ref[...], v_ref[...]
        bcol, gcol, grow = _head_cols(b_ref[...], g_ref[...], gt_ref[...], h)
        p = _prep_common(q, k, bcol, gcol, grow, t_ref[...])
        t_mat, decay, e, ekt, kb = p["t"], p["decay"], p["e"], p["ekt"], p["kb"]
        du_, dw_, dqd_, dkt_ = du_ref[...], dw_ref[...], dqd_ref[...], dkt_ref[...]
        vb = v * bcol
        kbe = kb * e
        d_t = _dot(du_, vb, NT) + _dot(dw_, kbe, NT)
        dvb = _dot(t_mat, du_, TN)
        dkbe = _dot(t_mat, dw_, TN)
        ts = _split_bf16(t_mat)
        d_l = -_dot3(_dot3(ts, d_t, TN), ts, NT)
        m1 = jnp.where(p["strict"], d_l, 0.0)
        m2 = _unfold_blocks(da_ref[...], p["tril"])
        d_kk = m1 * decay
        d_qk = m2 * decay
        d_decay = m1 * p["kk"] + m2 * p["qk"]
        dkb = _dot(d_kk, k) + dkbe * e
        dk = _dot(d_kk, kb, TN) + _dot(d_qk, q, TN) + dkt_ * ekt + dkb * bcol
        dq = _dot(d_qk, k) + dqd_ * e
        d_beta = _rowsum(dkb * k) + _rowsum(dvb * v)
        d_e = _rowsum(dkbe * kb) + _rowsum(dqd_ * q)
        d_ekt = _rowsum(dkt_ * k) * ekt
        d_diff = d_decay * decay
        d_grow = -_colsum(d_diff) + _colsum(jnp.where(p["last"], jnp.broadcast_to(d_ekt, (tg, tg)), 0.0))
        d_gcol = d_e * e - d_ekt + _rowsum(d_diff)
        d_gcol = d_gcol + _rowsum(jnp.where(p["eye"], jnp.broadcast_to(d_grow, (tg, tg)), 0.0))
        dq_ref[...] = dq
        dk_ref[...] = dk
        dv_ref[...] = dvb * bcol

        @pl.when(h == 0)
        def _():
            db_ref[...] = jnp.zeros_like(db_ref)
            dg_ref[...] = jnp.zeros_like(dg_ref)

        lane = lax.broadcasted_iota(jnp.int32, (tg, LANES), 1)
        db_ref[...] = jnp.where(lane == h, d_beta, db_ref[...])
        dg_ref[...] = jnp.where(lane == h, d_gcol, dg_ref[...])

    row = lambda off: pl.BlockSpec((tg, HEAD), functools.partial(lambda m, h, off: (m, h + off), off=off))
    full = pl.BlockSpec((tg, LANES), lambda m, h: (m, 0))
    o_spec = pl.BlockSpec((tg, HEAD), lambda m, h: (m, h))
    a_spec = pl.BlockSpec((None, tg, CHUNK), lambda m, h: (h, m, 0))
    wide = jax.ShapeDtypeStruct((s_dim, N_HEADS * HEAD), F32)
    lanes = jax.ShapeDtypeStruct((s_dim, LANES), F32)
    return pl.pallas_call(
        body,
        out_shape=[wide, wide, wide, lanes, lanes],
        grid=(s_dim // tg, N_HEADS),
        in_specs=[row(0), row(N_HEADS), row(2 * N_HEADS), full, full, pl.BlockSpec((8, tg), lambda m, h: (0, m)),
                  a_spec, o_spec, o_spec, o_spec, o_spec, a_spec],
        out_specs=[o_spec, o_spec, o_spec, full, full],
        compiler_params=pltpu.CompilerParams(dimension_semantics=("parallel", "arbitrary")),
        name="gdr_prep_bwd",
    )(qkvn, qkvn, qkvn, beta, gc, gc_t, t_fold, du, dw, dqd, dkt, d_a)


def _gdr_scan_fwd(u, w, qd, kt, a_mat, gc):
    s_dim = u.shape[0]
    n_chunks = s_dim // CHUNK

    def body(u_ref, w_ref, qd_ref, kt_ref, a_ref, g_ref, o_ref, st_ref, state):
        @pl.when(pl.program_id(0) == 0)
        def _():
            state[...] = jnp.zeros_like(state)

        egl = jnp.exp(g_ref[CHUNK - 1:CHUNK, :])
        for h in range(N_HEADS):
            cs = slice(h * HEAD, (h + 1) * HEAD)
            s_h = state[h]
            st_ref[h] = s_h
            vn = u_ref[:, cs] - _dot(w_ref[:, cs], s_h)
            o_ref[:, cs] = _dot(qd_ref[:, cs], s_h) + _dot(a_ref[h], vn)
            state[h] = s_h * egl[:, h:h + 1] + _dot(kt_ref[:, cs], vn, TN)

    wide = pl.BlockSpec((CHUNK, N_HEADS * HEAD), lambda n: (n, 0))
    return pl.pallas_call(
        body,
        out_shape=[jax.ShapeDtypeStruct((s_dim, N_HEADS * HEAD), F32),
                   jax.ShapeDtypeStruct((n_chunks, N_HEADS, HEAD, HEAD), F32)],
        grid=(n_chunks,),
        in_specs=[wide, wide, wide, wide, pl.BlockSpec((N_HEADS, CHUNK, CHUNK), lambda n: (0, n, 0)),
                  pl.BlockSpec((CHUNK, LANES), lambda n: (n, 0))],
        out_specs=[wide, pl.BlockSpec((None, N_HEADS, HEAD, HEAD), lambda n: (n, 0, 0, 0))],
        scratch_shapes=[pltpu.VMEM((N_HEADS, HEAD, HEAD), F32)],
        compiler_params=pltpu.CompilerParams(dimension_semantics=("arbitrary",)),
        name="gdr_scan_fwd",
    )(u, w, qd, kt, a_mat, gc)


def _gdr_scan_bwd(u, w, qd, kt, a_mat, gc, states, d_o):
    s_dim = u.shape[0]
    n_chunks = s_dim // CHUNK
    last = n_chunks - 1

    def body(u_ref, w_ref, qd_ref, kt_ref, a_ref, g_ref, st_ref, do_ref,
             du_ref, dw_ref, dqd_ref, dkt_ref, da_ref, de_ref, d_state):
        @pl.when(pl.program_id(0) == 0)
        def _():
            d_state[...] = jnp.zeros_like(d_state)

        egl = jnp.exp(g_ref[CHUNK - 1:CHUNK, :])
        for h in range(N_HEADS):
            cs = slice(h * HEAD, (h + 1) * HEAD)
            s_h = st_ref[h]
            ds_n = d_state[h]
            do = do_ref[:, cs]
            w_h = w_ref[:, cs]
            vn = u_ref[:, cs] - _dot(w_h, s_h)
            dvn = _dot(a_ref[h], do, TN) + _dot(kt_ref[:, cs], ds_n)
            dqd_ref[:, cs] = _dot(do, s_h, NT)
            da_ref[h] = _dot(do, vn, NT)
            dkt_ref[:, cs] = _dot(vn, ds_n, NT)
            de = jnp.sum(_rowsum(ds_n * s_h), axis=0, keepdims=True)
            de_ref[h:h + 1, :] = jnp.broadcast_to(de, (1, LANES))
            du_ref[:, cs] = dvn
            dw_ref[:, cs] = -_dot(dvn, s_h, NT)
            d_state[h] = ds_n * egl[:, h:h + 1] + _dot(qd_ref[:, cs], do, TN) - _dot(w_h, dvn, TN)

    wide = pl.BlockSpec((CHUNK, N_HEADS * HEAD), lambda n: (last - n, 0))
    a_spec = pl.BlockSpec((N_HEADS, CHUNK, CHUNK), lambda n: (0, last - n, 0))
    wide_shape = jax.ShapeDtypeStruct((s_dim, N_HEADS * HEAD), F32)
    return pl.pallas_call(
        body,
        out_shape=[wide_shape, wide_shape, wide_shape, wide_shape,
                   jax.ShapeDtypeStruct((N_HEADS, s_dim, CHUNK), F32),
                   jax.ShapeDtypeStruct((n_chunks, N_HEADS, LANES), F32)],
        grid=(n_chunks,),
        in_specs=[wide, wide, wide, wide, a_spec, pl.BlockSpec((CHUNK, LANES), lambda n: (last - n, 0)),
                  pl.BlockSpec((None, N_HEADS, HEAD, HEAD), lambda n: (last - n, 0, 0, 0)), wide],
        out_specs=[wide, wide, wide, wide, a_spec, pl.BlockSpec((None, N_HEADS, LANES), lambda n: (last - n, 0, 0))],
        scratch_shapes=[pltpu.VMEM((N_HEADS, HEAD, HEAD), F32)],
        compiler_params=pltpu.CompilerParams(dimension_semantics=("arbitrary",)),
        name="gdr_scan_bwd",
    )(u, w, qd, kt, a_mat, gc, states, d_o)


def _gdr_out_fwd(o_dn, proj_a, dn_w):
    def fn(r, c):
        o, z = r
        (w_,) = c
        outs = []
        for h in range(N_HEADS):
            cs = slice(h * HEAD, (h + 1) * HEAD)
            oh, zh = o[:, cs], z[:, cs]
            rr = lax.rsqrt(_rowmean(oh * oh) + EPS_RMS)
            outs.append(oh * rr * w_ * (zh * _sig(zh)))
        return [jnp.concatenate(outs, axis=1)], []

    return _rowwise(fn, [o_dn, (proj_a, 3, D_MODEL)], [dn_w], [(D_MODEL, BF16)], name="gdr_out_fwd")[0]


def _gdr_out_bwd(o_dn, proj_a, d_og, dn_w):
    def fn(r, c):
        o, z, dg = r
        (w_,) = c
        d_o, d_z = [], []
        d_w = jnp.zeros((1, HEAD), F32)
        for h in range(N_HEADS):
            cs = slice(h * HEAD, (h + 1) * HEAD)
            oh, zh, dgh = o[:, cs], z[:, cs], dg[:, cs]
            rr = lax.rsqrt(_rowmean(oh * oh) + EPS_RMS)
            sz = zh * _sig(zh)
            d_n = dgh * sz
            d_z.append(dgh * (oh * rr * w_) * _silu_grad(zh))
            d_w = d_w + _colsum(d_n * oh * rr)
            gw = d_n * w_
            d_o.append(rr * gw - oh * (rr * rr * rr) * _rowmean(gw * oh))
        return [jnp.concatenate(d_o, axis=1), jnp.concatenate(d_z, axis=1)], [d_w]

    return _rowwise(fn, [o_dn, (proj_a, 3, D_MODEL), d_og], [dn_w], [(D_MODEL, F32), (D_MODEL, BF16)],
                    accs=[(1, HEAD)], name="gdr_out_bwd")


def _rms_fwd(x, w):
    r = lax.rsqrt(_rowmean(x * x) + EPS_RMS)
    return x * r * w


def _rms_bwd(x, w, dy):
    r = lax.rsqrt(_rowmean(x * x) + EPS_RMS)
    gw = dy * w
    return r * gw - x * (r * r * r) * _rowmean(gw * x), _colsum(dy * x * r)


def _mla_norm_fwd(proj_b, qn_w, kvn_w):
    def fn(r, c):
        return [_rms_fwd(r[0], c[0]), _rms_fwd(r[1], c[1])], []

    return _rowwise(fn, [(proj_b, WB_CQ // Q_LORA, Q_LORA), (proj_b, WB_CKV // KV_LORA, KV_LORA)], [qn_w, kvn_w],
                    [(Q_LORA, BF16), (KV_LORA, BF16)], name="mla_norm_fwd")


def _mla_norm_bwd(proj_b, qn_w, kvn_w, d_cq, d_ckv):
    def fn(r, c):
        dx1, dw1 = _rms_bwd(r[0], c[0], r[2])
        dx2, dw2 = _rms_bwd(r[1], c[1], r[3])
        return [dx1, dx2], [dw1, dw2]

    return _rowwise(fn, [(proj_b, WB_CQ // Q_LORA, Q_LORA), (proj_b, WB_CKV // KV_LORA, KV_LORA), d_cq, d_ckv],
                    [qn_w, kvn_w], [(Q_LORA, BF16), (KV_LORA, BF16)], accs=[(1, Q_LORA), (1, KV_LORA)],
                    name="mla_norm_bwd")


def _rope_consts():
    inv = ROPE_BASE ** (-np.arange(0, ROPE, 2, dtype=np.float32) / ROPE)
    t = np.zeros((4, LANES), np.float32)
    t[0, :32] = inv
    t[0, 32:64] = inv
    t[1, :64] = 1.0
    t[2, 32:64] = 1.0
    t[3, :32] = -1.0
    return jnp.asarray(t)


def _rope_tables(pos, consts, width):
    ang = pos * consts[0:1, :]
    cosv, sinv = jnp.cos(ang), jnp.sin(ang)
    reps = width // LANES
    tile = (lambda t: jnp.concatenate([t] * reps, axis=1)) if reps > 1 else (lambda t: t)
    return tile(cosv * consts[1:2, :]), tile(sinv * consts[2:3, :]), tile(sinv * consts[3:4, :])


def _rope_apply(t, tabs):
    cos_t, sin_a, sin_b = tabs
    width = t.shape[1]
    return t * cos_t + pltpu.roll(t, 32, 1) * sin_a + pltpu.roll(t, width - 32, 1) * sin_b


def _rope_transpose(d, tabs):
    cos_t, sin_a, sin_b = tabs
    width = d.shape[1]
    return d * cos_t + pltpu.roll(d * sin_a, width - 32, 1) + pltpu.roll(d * sin_b, 32, 1)


def _mla_qk_fwd(q_full, k_nope, proj_b, pos):
    consts = _rope_consts()

    def fn(r, c):
        qf, kn, kr, pos_ = r
        qn, qr = qf[:, :D_MODEL], qf[:, D_MODEL:]
        return [qn * SCALE, _rope_apply(qr, _rope_tables(pos_, c[0], D_MODEL)) * SCALE, kn,
                _rope_apply(kr, _rope_tables(pos_, c[0], LANES))], []

    return _rowwise(fn, [q_full, k_nope, (proj_b, WB_KR // LANES, LANES), pos], [consts],
                    [(D_MODEL, BF16), (D_MODEL, BF16), (D_MODEL, BF16), (LANES, BF16)], name="mla_qk_fwd")


def _mla_qk_bwd(d_qn, d_qr, d_kr_heads, pos):
    consts = _rope_consts()

    def fn(r, c):
        dqn, dqr, dkr, pos_ = r
        d_qr_raw = _rope_transpose(dqr, _rope_tables(pos_, c[0], D_MODEL)) * SCALE
        dk = dkr[:, 0:LANES]
        for h in range(1, N_HEADS):
            dk = dk + dkr[:, h * LANES:(h + 1) * LANES]
        return [jnp.concatenate([dqn * SCALE, d_qr_raw], axis=1), _rope_transpose(dk, _rope_tables(pos_, c[0], LANES))], []

    return _rowwise(fn, [d_qn, d_qr, d_kr_heads, pos], [consts], [(2 * D_MODEL, BF16), (LANES, BF16)],
                    name="mla_qk_bwd")


def _causal_scores(qn, qr, kn, kr, qi, ki, tq, tk):
    s = _dot(qn, kn, NT) + _dot(qr, kr, NT)
    row = lax.broadcasted_iota(jnp.int32, (tq, tk), 0) + qi * tq
    col = lax.broadcasted_iota(jnp.int32, (tq, tk), 1) + ki * tk
    return jnp.where(col <= row, s, NEG_BIG)


def _attn_fwd(qn, qr, kn, kr, v):
    s_dim = qn.shape[0]
    tq = tk = min(512, s_dim)
    nq, nk = s_dim // tq, s_dim // tk

    def body(qn_ref, qr_ref, kn_ref, kr_ref, v_ref, o_ref, lse_ref, m_s, l_s, acc):
        qi, ki = pl.program_id(1), pl.program_id(2)

        @pl.when(ki == 0)
        def _():
            m_s[...] = jnp.full_like(m_s, NEG_BIG)
            l_s[...] = jnp.zeros_like(l_s)
            acc[...] = jnp.zeros_like(acc)

        @pl.when(ki <= qi)
        def _():
            s = _causal_scores(qn_ref[...], qr_ref[...], kn_ref[...], kr_ref[...], qi, ki, tq, tk)
            m_prev = m_s[:, 0:1]
            m_new = jnp.maximum(m_prev, jnp.max(s, axis=1, keepdims=True))
            alpha = jnp.exp(m_prev - m_new)
            p = jnp.exp(s - m_new)
            l_s[...] = jnp.broadcast_to(alpha * l_s[:, 0:1] + _rowsum(p), l_s.shape)
            m_s[...] = jnp.broadcast_to(m_new, m_s.shape)
            acc[...] = acc[...] * alpha + _dot(p, v_ref[...])

        @pl.when(ki == nk - 1)
        def _():
            l = l_s[...]
            o_ref[...] = acc[...] / l
            lse_ref[...] = m_s[...] + jnp.log(l)

    q_spec = pl.BlockSpec((tq, HEAD), lambda h, qi, ki: (qi, h))
    k_spec = pl.BlockSpec((tk, HEAD), lambda h, qi, ki: (jnp.minimum(ki, qi), h))
    kr_spec = pl.BlockSpec((tk, LANES), lambda h, qi, ki: (jnp.minimum(ki, qi), 0))
    return pl.pallas_call(
        body,
        out_shape=[jax.ShapeDtypeStruct((s_dim, N_HEADS * HEAD), F32)] * 2,
        grid=(N_HEADS, nq, nk),
        in_specs=[q_spec, q_spec, k_spec, kr_spec, k_spec],
        out_specs=[q_spec, q_spec],
        scratch_shapes=[pltpu.VMEM((tq, LANES), F32), pltpu.VMEM((tq, LANES), F32), pltpu.VMEM((tq, HEAD), F32)],
        compiler_params=pltpu.CompilerParams(dimension_semantics=("parallel", "parallel", "arbitrary")),
        name="attn_fwd",
    )(qn, qr, kn, kr, v)


def _attn_delta(o, d_o):
    def fn(r, c):
        o_, do_ = r
        outs = []
        for h in range(N_HEADS):
            cs = slice(h * HEAD, (h + 1) * HEAD)
            outs.append(jnp.broadcast_to(_rowsum(o_[:, cs] * do_[:, cs]), (o_.shape[0], HEAD)))
        return [jnp.concatenate(outs, axis=1)], []

    return _rowwise(fn, [o, d_o], [], [(D_MODEL, F32)], name="attn_delta")[0]


def _attn_bwd_kv(qn, qr, kn, kr, v, d_o, lse, delta):
    s_dim = qn.shape[0]
    tq = tk = min(512, s_dim)
    nq, nk = s_dim // tq, s_dim // tk

    def body(qn_ref, qr_ref, kn_ref, kr_ref, v_ref, do_ref, lse_ref, dl_ref, dkn_ref, dkr_ref, dv_ref, a_kn, a_kr, a_v):
        ki, qi = pl.program_id(1), pl.program_id(2)

        @pl.when(qi == 0)
        def _():
            a_kn[...] = jnp.zeros_like(a_kn)
            a_kr[...] = jnp.zeros_like(a_kr)
            a_v[...] = jnp.zeros_like(a_v)

        @pl.when(qi >= ki)
        def _():
            qn_, qr_, do = qn_ref[...], qr_ref[...], do_ref[...]
            s = _causal_scores(qn_, qr_, kn_ref[...], kr_ref[...], qi, ki, tq, tk)
            p = jnp.exp(s - lse_ref[:, 0:1])
            a_v[...] += _dot(p, do, TN)
            ds = p * (_dot(do, v_ref[...], NT) - dl_ref[:, 0:1])
            a_kn[...] += _dot(ds, qn_, TN)
            a_kr[...] += _dot(ds, qr_, TN)

        @pl.when(qi == nq - 1)
        def _():
            dkn_ref[...] = a_kn[...].astype(dkn_ref.dtype)
            dkr_ref[...] = a_kr[...]
            dv_ref[...] = a_v[...].astype(dv_ref.dtype)

    q_spec = pl.BlockSpec((tq, HEAD), lambda h, ki, qi: (jnp.maximum(qi, ki), h))
    k_spec = pl.BlockSpec((tk, HEAD), lambda h, ki, qi: (ki, h))
    kr_spec = pl.BlockSpec((tk, LANES), lambda h, ki, qi: (ki, 0))
    wide = jax.ShapeDtypeStruct((s_dim, N_HEADS * HEAD), F32)
    wide_bf = jax.ShapeDtypeStruct((s_dim, N_HEADS * HEAD), BF16)
    return pl.pallas_call(
        body,
        out_shape=[wide_bf, wide, wide_bf],
        grid=(N_HEADS, nk, nq),
        in_specs=[q_spec, q_spec, k_spec, kr_spec, k_spec, q_spec, q_spec, q_spec],
        out_specs=[k_spec, k_spec, k_spec],
        scratch_shapes=[pltpu.VMEM((tk, HEAD), F32)] * 3,
        compiler_params=pltpu.CompilerParams(dimension_semantics=("parallel", "parallel", "arbitrary")),
        name="attn_bwd_kv",
    )(qn, qr, kn, kr, v, d_o, lse, delta)


def _attn_bwd_q(qn, qr, kn, kr, v, d_o, lse, delta):
    s_dim = qn.shape[0]
    tq = tk = min(512, s_dim)
    nq, nk = s_dim // tq, s_dim // tk

    def body(qn_ref, qr_ref, kn_ref, kr_ref, v_ref, do_ref, lse_ref, dl_ref, dqn_ref, dqr_ref, a_n, a_r):
        qi, ki = pl.program_id(1), pl.program_id(2)

        @pl.when(ki == 0)
        def _():
            a_n[...] = jnp.zeros_like(a_n)
            a_r[...] = jnp.zeros_like(a_r)

        @pl.when(ki <= qi)
        def _():
            kn_, kr_ = kn_ref[...], kr_ref[...]
            s = _causal_scores(qn_ref[...], qr_ref[...], kn_, kr_, qi, ki, tq, tk)
            p = jnp.exp(s - lse_ref[:, 0:1])
            ds = p * (_dot(do_ref[...], v_ref[...], NT) - dl_ref[:, 0:1])
            a_n[...] += _dot(ds, kn_)
            a_r[...] += _dot(ds, kr_)

        @pl.when(ki == nk - 1)
        def _():
            dqn_ref[...] = a_n[...]
            dqr_ref[...] = a_r[...]

    q_spec = pl.BlockSpec((tq, HEAD), lambda h, qi, ki: (qi, h))
    k_spec = pl.BlockSpec((tk, HEAD), lambda h, qi, ki: (jnp.minimum(ki, qi), h))
    kr_spec = pl.BlockSpec((tk, LANES), lambda h, qi, ki: (jnp.minimum(ki, qi), 0))
    wide = jax.ShapeDtypeStruct((s_dim, N_HEADS * HEAD), F32)
    return pl.pallas_call(
        body,
        out_shape=[wide, wide],
        grid=(N_HEADS, nq, nk),
        in_specs=[q_spec, q_spec, k_spec, kr_spec, k_spec, q_spec, q_spec, q_spec],
        out_specs=[q_spec, q_spec],
        scratch_shapes=[pltpu.VMEM((tq, HEAD), F32)] * 2,
        compiler_params=pltpu.CompilerParams(dimension_semantics=("parallel", "parallel", "arbitrary")),
        name="attn_bwd_q",
    )(qn, qr, kn, kr, v, d_o, lse, delta)


def _merge_fwd(y_dn, y_mla, proj_g):
    def fn(r, c):
        yd, ym, g = r
        return [_sig(g[:, :D_MODEL]) * yd + _sig(g[:, D_MODEL:]) * ym], []

    return _rowwise(fn, [y_dn, y_mla, proj_g], [], [(D_MODEL, BF16)], name="merge_fwd")[0]


def _merge_bwd(y_dn, y_mla, proj_g, d_mixed):
    def fn(r, c):
        yd, ym, g, dm = r
        sd, sm = _sig(g[:, :D_MODEL]), _sig(g[:, D_MODEL:])
        d_g = jnp.concatenate([dm * yd * sd * (1.0 - sd), dm * ym * sm * (1.0 - sm)], axis=1)
        return [d_g, dm * sd, dm * sm], []

    return _rowwise(fn, [y_dn, y_mla, proj_g, d_mixed], [], [(2 * D_MODEL, BF16), (D_MODEL, BF16), (D_MODEL, BF16)],
                    name="merge_bwd")


def _ln_stats(z):
    mu = _rowmean(z)
    zc = z - mu
    r = lax.rsqrt(_rowmean(zc * zc) + EPS_LN)
    return zc * r, r


def _ln_bwd(dy, xh, r, g):
    dxh = dy * g
    return r * (dxh - _rowmean(dxh) - xh * _rowmean(dxh * xh))


def _ln1_fwd(x, a1, g, b):
    def fn(r, c):
        xh, _ = _ln_stats(ALPHA * r[0] + r[1])
        y = xh * c[0] + c[1]
        return [y, y], []

    return _rowwise(fn, [x, a1], [g, b], [(D_MODEL, F32), (D_MODEL, BF16)], name="ln1_fwd")


def _ln1_bwd(x, a1, d_h1, g):
    def fn(r, c):
        xh, rr = _ln_stats(ALPHA * r[0] + r[1])
        dy = r[2]
        dz = _ln_bwd(dy, xh, rr, c[0])
        return [dz, ALPHA * dz], [_colsum(dy * xh), _colsum(dy)]

    return _rowwise(fn, [x, a1, d_h1], [g], [(D_MODEL, BF16), (D_MODEL, F32)], accs=[(1, D_MODEL), (1, D_MODEL)],
                    name="ln1_bwd")


def _act_fwd(gu):
    def fn(r, c):
        gt, up = r[0][:, :FFN_HIDDEN], r[0][:, FFN_HIDDEN:]
        return [gt * _sig(gt) * up], []

    return _rowwise(fn, [gu], [], [(FFN_HIDDEN, BF16)], name="act_fwd")[0]


def _act_bwd(gu, d_act):
    def fn(r, c):
        gt, up = r[0][:, :FFN_HIDDEN], r[0][:, FFN_HIDDEN:]
        da = r[1]
        return [jnp.concatenate([da * up * _silu_grad(gt), da * gt * _sig(gt)], axis=1)], []

    return _rowwise(fn, [gu, d_act], [], [(2 * FFN_HIDDEN, BF16)], name="act_bwd")[0]


def _tail(h1, ffn, pg, pp, tgt, g, b):
    def fn(r, c):
        h1_, ffn_, pg_, pp_, t_ = r
        sp = _sig(pg_)
        xh, rr = _ln_stats(ALPHA * h1_ + ffn_ + sp * pp_)
        y = xh * c[0] + c[1]
        err = y - t_
        dy = err * (1.0 / D_MODEL)
        dz = _ln_bwd(dy, xh, rr, c[0])
        loss = jnp.sum(0.5 * _rowmean(err * err), axis=0, keepdims=True)
        return ([dz, dz * pp_ * sp * (1.0 - sp), dz * sp, ALPHA * dz],
                [_colsum(dy * xh), _colsum(dy), jnp.broadcast_to(loss, (1, LANES))])

    return _rowwise(fn, [h1, ffn, pg, pp, tgt], [g, b], [(D_MODEL, BF16)] * 3 + [(D_MODEL, F32)],
                    accs=[(1, D_MODEL), (1, D_MODEL), (1, LANES)], name="tail")


def _pad_heads_to_lanes(per_head_lane0, s_dim):
    t = jnp.transpose(per_head_lane0[:, :, 0])
    return jnp.pad(t, ((0, 0), (0, LANES - N_HEADS)))


def _local_step(x, p, pos, tgt, w):
    s_dim = x.shape[0]
    xb, pb = x.astype(BF16), p.astype(BF16)
    proj_a = _mm(xb, w["wa_t"], tb=True, name="f_proj_a")
    proj_g = _mm(xb, w["wg_t"], tb=True, name="f_proj_g")
    proj_b = _mm(xb, w["wb_t"], tb=True, name="f_proj_b")
    qkvn = _conv_fwd(proj_a, w["conv"])
    beta, gc = _gates_fwd(proj_b, w["alog"], w["dtb"])
    gc_t = jnp.transpose(gc[:, :N_HEADS])
    u, w_, qd, kt, a_mat, t_fold = _gdr_prep_fwd(qkvn, beta, gc, gc_t)
    o_dn, states = _gdr_scan_fwd(u, w_, qd, kt, a_mat, gc)
    og = _gdr_out_fwd(o_dn, proj_a, w["dnw"])
    y_dn = _mm(og, w["br_dn"], name="f_y_dn")
    c_q, c_kv = _mla_norm_fwd(proj_b, w["qnw"], w["kvnw"])
    q_full = _mm(c_q, w["uq"], name="f_q_full")
    k_nope = _mm(c_kv, w["uk"], name="f_k_nope")
    vv = _mm(c_kv, w["uv"], out_dtype=BF16, name="f_v")
    qn, qr, kn, kr = _mla_qk_fwd(q_full, k_nope, proj_b, pos)
    o_mla, lse = _attn_fwd(qn, qr, kn, kr, vv)
    y_mla = _mm(o_mla, w["br_mla"], name="f_y_mla")
    mixed = _merge_fwd(y_dn, y_mla, proj_g)
    a1 = _mm(mixed, w["wo"], name="f_a1")
    h1, h1b = _ln1_fwd(x, a1, w["ln1g"], w["ln1b"])
    gu = _mm(h1b, w["ffn_in_t"], tb=True, name="f_gu")
    act = _act_fwd(gu)
    ffn = _mm(act, w["ffn_out"], name="f_ffn")
    pg = _mm(h1b, w["ple_gate"], name="f_pg")
    pp = _mm(pb, w["ple_t"], tb=True, name="f_pp")
    g = {}
    dz2, d_pg, d_pp, dh1a, g["ln2g"], g["ln2b"], loss = _tail(h1, ffn, pg, pp, tgt, w["ln2g"], w["ln2b"])
    g["ple_t"] = _mm(d_pp, pb, ta=True, name="b_w_ple")
    g["ple_gate"] = _mm(h1b, d_pg, ta=True, name="b_w_ple_gate")
    g["ffn_out"] = _mm(act, dz2, ta=True, name="b_w_ffn_out")
    d_act = _mm(dz2, w["ffn_out"], tb=True, name="b_act")
    d_gu = _act_bwd(gu, d_act)
    g["ffn_in_t"] = _mm(d_gu, h1b, ta=True, name="b_w_ffn_in")
    d_h1 = _mm(d_gu, w["ffn_in_t"], add=(dh1a,), name="b_h1_ffn")
    d_h1 = _mm(d_pg, w["ple_gate"], tb=True, add=(d_h1,), name="b_h1_ple")
    dz1, dxa, g["ln1g"], g["ln1b"] = _ln1_bwd(x, a1, d_h1, w["ln1g"])
    g["wo"] = _mm(mixed, dz1, ta=True, name="b_w_o")
    d_mixed = _mm(dz1, w["wo"], tb=True, name="b_mixed")
    d_proj_g, d_y_dn, d_y_mla = _merge_bwd(y_dn, y_mla, proj_g, d_mixed)
    g["br_mla"] = _mm(o_mla, d_y_mla, ta=True, name="b_w_br_mla")
    d_o_mla = _mm(d_y_mla, w["br_mla"], tb=True, name="b_o_mla")
    delta = _attn_delta(o_mla, d_o_mla)
    d_kn, d_kr_heads, d_v = _attn_bwd_kv(qn, qr, kn, kr, vv, d_o_mla, lse, delta)
    d_qn, d_qr = _attn_bwd_q(qn, qr, kn, kr, vv, d_o_mla, lse, delta)
    d_q_full, d_kr = _mla_qk_bwd(d_qn, d_qr, d_kr_heads, pos)
    g["uq"] = _mm(c_q, d_q_full, ta=True, name="b_w_uq")
    d_c_q = _mm(d_q_full, w["uq"], tb=True, name="b_c_q")
    g["uk"] = _mm(c_kv, d_kn, ta=True, name="b_w_uk")
    g["uv"] = _mm(c_kv, d_v, ta=True, name="b_w_uv")
    d_c_kv = _mm(d_kn, w["uk"], tb=True, name="b_c_kv_k")
    d_c_kv = _mm(d_v, w["uv"], tb=True, add=(d_c_kv,), name="b_c_kv_v")
    d_cq, d_ckv, g["qnw"], g["kvnw"] = _mla_norm_bwd(proj_b, w["qnw"], w["kvnw"], d_c_q, d_c_kv)
    g["br_dn"] = _mm(og, d_y_dn, ta=True, name="b_w_br_dn")
    d_og = _mm(d_y_dn, w["br_dn"], tb=True, name="b_og")
    d_o_dn, d_z, g["dnw"] = _gdr_out_bwd(o_dn, proj_a, d_og, w["dnw"])
    du, dw, dqd, dkt, d_a, d_egl = _gdr_scan_bwd(u, w_, qd, kt, a_mat, gc, states, d_o_dn)
    dq, dk, dv, d_beta, d_gc = _gdr_prep_bwd(qkvn, beta, gc, gc_t, t_fold, du, dw, dqd, dkt, d_a)
    d_egl_rows = jnp.pad(d_egl[:, None, :, 0], ((0, 0), (CHUNK - 1, 0), (0, LANES - N_HEADS))).reshape(s_dim, LANES)
    d_ba, g["alog"], g["dtb"] = _gates_bwd(proj_b, w["alog"], w["dtb"], gc, d_beta, d_gc, d_egl_rows)
    d_qkv, g["conv"] = _conv_bwd(proj_a, w["conv"], dq, dk, dv)
    zeros = jnp.zeros((s_dim, WB_CKV - Q_LORA), BF16)
    d_proj_b = jnp.concatenate([d_cq, zeros, d_ckv, d_kr, d_ba], axis=1)
    g["wa_qkv_t"] = _mm(d_qkv, xb, ta=True, name="b_w_qkv")
    g["wa_z_t"] = _mm(d_z, xb, ta=True, name="b_w_z")
    g["wg_t"] = _mm(d_proj_g, xb, ta=True, name="b_w_g")
    g["wb_t"] = _mm(d_proj_b, xb, ta=True, name="b_w_b")
    dx = _mm(d_qkv, w["wa_qkv_t"], add=(dxa,), name="b_x_qkv")
    dx = _mm(d_z, w["wa_z_t"], add=(dx,), name="b_x_z")
    dx = _mm(d_proj_g, w["wg_t"], add=(dx,), name="b_x_g")
    dx = _mm(d_proj_b, w["wb_t"], add=(dx,), name="b_x_b")
    return loss, dx, g


_BIG = (("w_in", 1), ("conv_w", 1), ("w_uq", 0), ("w_uk", 0), ("w_uv", 0), ("w_br_dn", 0), ("w_br_mla", 0),
        ("w_o", 0), ("w_ffn_in", 1), ("w_ffn_out", 0), ("w_ple", 1), ("w_ple_gate", 0))
_BIG_AXIS = dict(_BIG)
_SMALL = ("ln1_g", "ln1_b", "ln2_g", "ln2_b", "q_norm_w", "kv_norm_w", "dn_norm_w", "dn_a_log", "dn_dt_bias")
_ORDER = ("w_in", "conv_w", "dn_a_log", "dn_dt_bias", "dn_norm_w", "q_norm_w", "w_uq", "kv_norm_w", "w_uk", "w_uv",
          "w_br_dn", "w_br_mla", "w_o", "ln1_g", "ln1_b", "w_ffn_in", "w_ffn_out", "w_ple", "w_ple_gate", "ln2_g",
          "ln2_b")


ROW_ALIGN = 16


def _flat_rows(shape):
    rows = -(-int(np.prod(shape)) // FLAT_COLS)
    return -(-rows // ROW_ALIGN) * ROW_ALIGN


def _stored_shape(name, shard_shape):
    axis = _BIG_AXIS[name]
    lead = shard_shape[axis]
    return lead, int(np.prod(shard_shape)) // lead


def _to_stored(name, shard):
    return jnp.moveaxis(shard, _BIG_AXIS[name], 0).reshape(_stored_shape(name, shard.shape))


def _from_stored(name, stored, shard_shape):
    axis = _BIG_AXIS[name]
    moved = (shard_shape[axis],) + shard_shape[:axis] + shard_shape[axis + 1:]
    return jnp.moveaxis(stored.reshape(moved), 0, axis)


def _flat_layout(shard_shapes):
    rows, off = {}, 0
    for name, _ in _BIG:
        n = _flat_rows(shard_shapes[name])
        rows[name] = (off, n)
        off += n
    total = -(-off // FLAT_TILE) * FLAT_TILE
    return rows, total


def _to_rows(a, lead, n_rows):
    lead_shape = a.shape[:lead]
    if a.shape[-1] == FLAT_COLS:
        return jnp.pad(a, [(0, 0)] * lead + [(0, n_rows - a.shape[lead]), (0, 0)])
    flat = a.reshape(lead_shape + (-1,))
    flat = jnp.pad(flat, [(0, 0)] * lead + [(0, n_rows * FLAT_COLS - flat.shape[-1])])
    return flat.reshape(lead_shape + (n_rows, FLAT_COLS))


def _from_rows(rows, stored_shape):
    r, c = stored_shape
    lead_shape = rows.shape[:-2]
    if c == FLAT_COLS:
        return rows[..., :r, :]
    return rows.reshape(lead_shape + (-1,))[..., :r * c].reshape(lead_shape + (r, c))


def _pack_shards(shards, layout, total_rows):
    parts = [_to_rows(_to_stored(name, shards[name]), 0, layout[name][1]) for name, _ in _BIG]
    flat = jnp.concatenate(parts, axis=0)
    return jnp.pad(flat, ((0, total_rows - flat.shape[0]), (0, 0)))


def _unpack_shards(flat, shard_shapes, layout):
    out = {}
    for name, _ in _BIG:
        off, n = layout[name]
        stored = _from_rows(flat[off:off + n], _stored_shape(name, shard_shapes[name]))
        out[name] = _from_stored(name, stored, shard_shapes[name])
    return out


def _unpack_gathered(gathered, shard_shapes, layout):
    out = {}
    for name, _ in _BIG:
        off, n = layout[name]
        r, c = _stored_shape(name, shard_shapes[name])
        out[name] = _from_rows(gathered[:, off:off + n], (r, c)).reshape(N_DEV * r, c)
    return out


def _pack_grads_for_parity(stored_grads, shard_shapes, layout, total_rows, parity):
    parts = []
    for name, _ in _BIG:
        r, c = _stored_shape(name, shard_shapes[name])
        gsh = stored_grads[name].reshape(4, 2, r, c)
        gsh = lax.dynamic_index_in_dim(gsh, parity, axis=1, keepdims=False)
        parts.append(_to_rows(gsh, 1, layout[name][1]))
    flat = jnp.concatenate(parts, axis=1)
    return jnp.pad(flat, ((0, 0), (0, total_rows - flat.shape[1]), (0, 0)))


_W_IN_ROWS = np.cumsum([0, 3072, 1024, 8, 8, Q_LORA, KV_LORA, ROPE, D_MODEL, D_MODEL])


def _full_weights(fw, small):
    w_in_t = fw["w_in"]
    r = _W_IN_ROWS
    zr = lambda n: jnp.zeros((n, D_MODEL), w_in_t.dtype)
    w = {}
    w["wa_t"] = w_in_t[r[0]:r[2]]
    w["wa_qkv_t"], w["wa_z_t"] = w_in_t[r[0]:r[1]], w_in_t[r[1]:r[2]]
    w["wg_t"] = w_in_t[r[7]:r[9]]
    w["wb_t"] = jnp.concatenate([w_in_t[r[4]:r[5]], zr(WB_CKV - Q_LORA), w_in_t[r[5]:r[7]], zr(LANES - ROPE),
                                 w_in_t[r[2]:r[4]], zr(LANES - 2 * N_HEADS)], axis=0)
    uq = fw["w_uq"].reshape(Q_LORA, N_HEADS, HEAD + ROPE)
    uq_r = jnp.pad(uq[:, :, HEAD:], ((0, 0), (0, 0), (0, HEAD - ROPE)))
    w["uq"] = jnp.concatenate([uq[:, :, :HEAD].reshape(Q_LORA, -1), uq_r.reshape(Q_LORA, -1)], axis=1)
    w["uk"], w["uv"] = fw["w_uk"], fw["w_uv"]
    w["conv"] = jnp.transpose(fw["conv_w"]).astype(F32)
    w["br_dn"], w["br_mla"], w["wo"] = fw["w_br_dn"], fw["w_br_mla"], fw["w_o"]
    w["ffn_in_t"], w["ffn_out"] = fw["w_ffn_in"], fw["w_ffn_out"]
    w["ple_t"], w["ple_gate"] = fw["w_ple"], fw["w_ple_gate"]
    pad_l = lambda v: jnp.pad(v, ((0, 0), (0, LANES - v.shape[1])))
    w["alog"], w["dtb"] = pad_l(small["dn_a_log"]), pad_l(small["dn_dt_bias"])
    w["dnw"], w["qnw"], w["kvnw"] = small["dn_norm_w"], small["q_norm_w"], small["kv_norm_w"]
    w["ln1g"], w["ln1b"], w["ln2g"], w["ln2b"] = small["ln1_g"], small["ln1_b"], small["ln2_g"], small["ln2_b"]
    return w


def _full_grads(g):
    wb = g["wb_t"]
    full = {}
    full["w_in"] = jnp.concatenate([
        g["wa_qkv_t"], g["wa_z_t"], wb[WB_BA:WB_BA + 2 * N_HEADS], wb[WB_CQ:WB_CQ + Q_LORA],
        wb[WB_CKV:WB_CKV + KV_LORA], wb[WB_KR:WB_KR + ROPE], g["wg_t"]], axis=0)
    uq = g["uq"]
    uq_n = uq[:, :D_MODEL].reshape(Q_LORA, N_HEADS, HEAD)
    uq_r = uq[:, D_MODEL:].reshape(Q_LORA, N_HEADS, HEAD)[:, :, :ROPE]
    full["w_uq"] = jnp.concatenate([uq_n, uq_r], axis=2).reshape(Q_LORA, -1)
    full["w_uk"], full["w_uv"] = g["uk"], g["uv"]
    full["conv_w"] = jnp.transpose(g["conv"])
    full["w_br_dn"], full["w_br_mla"], full["w_o"] = g["br_dn"], g["br_mla"], g["wo"]
    full["w_ffn_in"], full["w_ffn_out"] = g["ffn_in_t"], g["ffn_out"]
    full["w_ple"], full["w_ple_gate"] = g["ple_t"], g["ple_gate"]
    small = {"ln1_g": g["ln1g"], "ln1_b": g["ln1b"], "ln2_g": g["ln2g"], "ln2_b": g["ln2b"], "q_norm_w": g["qnw"],
             "kv_norm_w": g["kvnw"], "dn_norm_w": g["dnw"], "dn_a_log": g["alog"][:, :N_HEADS],
             "dn_dt_bias": g["dtb"][:, :N_HEADS]}
    return full, small


_SMALL_SLOTS = {"ln1_g": (0, 0, 1024), "ln1_b": (1, 0, 1024), "ln2_g": (2, 0, 1024), "ln2_b": (3, 0, 1024),
                "q_norm_w": (4, 0, 384), "kv_norm_w": (4, 384, 256), "dn_norm_w": (4, 640, 128),
                "dn_a_log": (4, 768, 8), "dn_dt_bias": (4, 776, 8)}
_LOSS_SLOT = (4, 896)


def _pack_small(vals, loss=None):
    blk = jnp.zeros((8, FLAT_COLS), F32)
    for name, (r, c, n) in _SMALL_SLOTS.items():
        blk = lax.dynamic_update_slice(blk, vals[name].reshape(1, n).astype(F32), (r, c))
    if loss is not None:
        blk = lax.dynamic_update_slice(blk, loss[:, :1], _LOSS_SLOT)
    return blk


def _unpack_small(blk, shapes):
    return {name: blk[r:r + 1, c:c + n].reshape(shapes[name]) for name, (r, c, n) in _SMALL_SLOTS.items()}


_MESH_ID = pl.DeviceIdType.MESH
_ANY = pl.BlockSpec(memory_space=pl.ANY)


def _all_gather(block, name):
    def body(x_ref, out_ref, send_sems, recv_sems, local_sem):
        x, y, c = lax.axis_index("x"), lax.axis_index("y"), lax.axis_index("c")
        me, sibling = (x, y, c), (x, y, 1 - c)
        chips = [(1 - x, y), (x, 1 - y), (1 - x, 1 - y)]

        def slot(px, py, pc):
            return out_ref.at[4 * px + 2 * py + pc]

        def copy(k, origin, to, src=None):
            return pltpu.make_async_remote_copy(
                src_ref=slot(*origin) if src is None else src, dst_ref=slot(*origin), send_sem=send_sems.at[k],
                recv_sem=recv_sems.at[k], device_id=to, device_id_type=_MESH_ID)

        mine = pltpu.make_async_copy(x_ref, slot(*me), local_sem)
        mine.start()
        first = [copy(0, me, sibling, src=x_ref)]
        first += [copy(1 + j, me, (*chip, c), src=x_ref) for j, chip in enumerate(chips)]
        for cp in first:
            cp.start()
        passed = [copy(4 + j, (*chip, c), sibling) for j, chip in enumerate(chips)]
        for j, chip in enumerate(chips):
            copy(1 + j, (*chip, c), me).wait_recv()
            passed[j].start()
        copy(0, sibling, me).wait_recv()
        for j, chip in enumerate(chips):
            copy(4 + j, (*chip, 1 - c), me).wait_recv()
        for cp in first + passed:
            cp.wait_send()
        mine.wait()

    return pl.pallas_call(
        body,
        out_shape=jax.ShapeDtypeStruct((N_DEV,) + block.shape, block.dtype),
        in_specs=[_ANY],
        out_specs=_ANY,
        scratch_shapes=[pltpu.SemaphoreType.DMA((7,)), pltpu.SemaphoreType.DMA((7,)), pltpu.SemaphoreType.DMA],
        name=name,
    )(block)


def _exchange_sibling(src, name):
    def body(src_ref, dst_ref, send_sems, recv_sems):
        x, y, c = lax.axis_index("x"), lax.axis_index("y"), lax.axis_index("c")
        copies = [pltpu.make_async_remote_copy(
            src_ref=src_ref.at[q], dst_ref=dst_ref.at[q], send_sem=send_sems.at[q], recv_sem=recv_sems.at[q],
            device_id=(x, y, 1 - c), device_id_type=_MESH_ID) for q in range(4)]
        for cp in copies:
            cp.start()
        for cp in copies:
            cp.wait_recv()
        for cp in copies:
            cp.wait_send()

    return pl.pallas_call(
        body,
        out_shape=jax.ShapeDtypeStruct(src.shape, src.dtype),
        in_specs=[_ANY],
        out_specs=_ANY,
        scratch_shapes=[pltpu.SemaphoreType.DMA((4,)), pltpu.SemaphoreType.DMA((4,))],
        name=name,
    )(src)


def _exchange_chips(src, name):
    def body(src_ref, dst_ref, send_sems, recv_sems):
        x, y, c = lax.axis_index("x"), lax.axis_index("y"), lax.axis_index("c")
        chips = [(1 - x, y), (x, 1 - y), (1 - x, 1 - y)]
        copies = [pltpu.make_async_remote_copy(
            src_ref=src_ref.at[2 * tx + ty], dst_ref=dst_ref.at[j], send_sem=send_sems.at[j],
            recv_sem=recv_sems.at[j], device_id=(tx, ty, c), device_id_type=_MESH_ID)
            for j, (tx, ty) in enumerate(chips)]
        for cp in copies:
            cp.start()
        for cp in copies:
            cp.wait_recv()
        for cp in copies:
            cp.wait_send()

    return pl.pallas_call(
        body,
        out_shape=jax.ShapeDtypeStruct((3,) + src.shape[1:], src.dtype),
        in_specs=[_ANY],
        out_specs=_ANY,
        scratch_shapes=[pltpu.SemaphoreType.DMA((3,)), pltpu.SemaphoreType.DMA((3,))],
        name=name,
    )(src)


def _add_pairs(a, b, name):
    n, r, c = a.shape

    def fn(rows, consts):
        return [rows[0] + rows[1]], []

    out = _rowwise(fn, [a.reshape(n * r, c), b.reshape(n * r, c)], [], [(c, F32)], tm=FLAT_TILE, name=name)[0]
    return out.reshape(n, r, c)


def _adamw_math(w, g, m, v):
    m = ADAM_B1 * m + (1.0 - ADAM_B1) * g
    v = ADAM_B2 * v + (1.0 - ADAM_B2) * (g * g)
    m_hat = m / (1.0 - ADAM_B1 ** ADAM_STEP)
    v_hat = v / (1.0 - ADAM_B2 ** ADAM_STEP)
    delta = -ADAM_LR * (m_hat / (jnp.sqrt(v_hat) + ADAM_EPS) + ADAM_WD * w)
    return delta, m, v


def _adamw_flat(w, m, v, g_parts):
    def fn(rows, consts):
        w_, m_, v_ = rows[:3]
        g = rows[3]
        for part in rows[4:]:
            g = g + part
        delta, m2, v2 = _adamw_math(w_, g, m_, v_)
        return [g, delta, m2, v2], []

    return _rowwise(fn, [w, m, v] + list(g_parts), [], [(FLAT_COLS, F32)] * 4, tm=FLAT_TILE, name="adamw_flat")


def _adamw_small(w, m, v, gathered):
    def body(w_ref, m_ref, v_ref, g_ref, go_ref, d_ref, m2_ref, v2_ref):
        g = g_ref[0]
        for k in range(1, N_DEV):
            g = g + g_ref[k]
        delta, m2, v2 = _adamw_math(w_ref[...], g, m_ref[...], v_ref[...])
        go_ref[...] = g
        d_ref[...] = delta
        m2_ref[...] = m2
        v2_ref[...] = v2

    blk = jax.ShapeDtypeStruct((8, FLAT_COLS), F32)
    return pl.pallas_call(body, out_shape=[blk] * 4, name="adamw_small")(w, m, v, gathered)


def kernel(x, p, positions, w_in, conv_w, dn_a_log, dn_dt_bias, dn_norm_w, q_norm_w, w_uq, kv_norm_w, w_uk, w_uv, w_br_dn, w_br_mla, w_o, ln1_g, ln1_b, w_ffn_in, w_ffn_out, w_ple, w_ple_gate, ln2_g, ln2_b, loss_target, m_w_in, m_conv_w, m_dn_a_log, m_dn_dt_bias, m_dn_norm_w, m_q_norm_w, m_w_uq, m_kv_norm_w, m_w_uk, m_w_uv, m_w_br_dn, m_w_br_mla, m_w_o, m_ln1_g, m_ln1_b, m_w_ffn_in, m_w_ffn_out, m_w_ple, m_w_ple_gate, m_ln2_g, m_ln2_b, v_w_in, v_conv_w, v_dn_a_log, v_dn_dt_bias, v_dn_norm_w, v_q_norm_w, v_w_uq, v_kv_norm_w, v_w_uk, v_w_uv, v_w_br_dn, v_w_br_mla, v_w_o, v_ln1_g, v_ln1_b, v_w_ffn_in, v_w_ffn_out, v_w_ple, v_w_ple_gate, v_ln2_g, v_ln2_b):
    args = dict(locals())
    wts = {n: args[n] for n in _ORDER}
    mom1 = {n: args["m_" + n] for n in _ORDER}
    mom2 = {n: args["v_" + n] for n in _ORDER}
    big_names = [n for n, _ in _BIG]
    shard_shapes = {n: wts[n].shape[1:] for n in big_names}
    layout, total_rows = _flat_layout(shard_shapes)
    drop = lambda d, names: {n: d[n][0] for n in names}

    w_flat = _pack_shards(drop(wts, big_names), layout, total_rows)
    gathered = _all_gather(w_flat.astype(BF16), "ag_weights")
    full_w = _unpack_gathered(gathered, shard_shapes, layout)
    small_w = {n: wts[n].astype(F32) for n in _SMALL}
    w = _full_weights(full_w, small_w)

    s_dim = x.shape[1]
    loss, dx, g = _local_step(x[0], p[0, 0], positions.reshape(s_dim, 1).astype(F32), loss_target[0], w)
    full_g, small_g = _full_grads(g)

    c_idx = lax.axis_index("c")
    q_idx = 2 * lax.axis_index("x") + lax.axis_index("y")
    g_own = _pack_grads_for_parity(full_g, shard_shapes, layout, total_rows, c_idx)
    g_sib = _pack_grads_for_parity(full_g, shard_shapes, layout, total_rows, 1 - c_idx)
    from_sibling = _exchange_sibling(g_sib, "rs_sibling")
    chip_sum = _add_pairs(g_own, from_sibling, "rs_chip_sum")
    from_chips = _exchange_chips(chip_sum, "rs_chips")
    mine = lax.dynamic_index_in_dim(chip_sum, q_idx, axis=0, keepdims=False)

    m_flat = _pack_shards(drop(mom1, big_names), layout, total_rows)
    v_flat = _pack_shards(drop(mom2, big_names), layout, total_rows)
    g_flat, d_flat, m2_flat, v2_flat = _adamw_flat(w_flat, m_flat, v_flat,
                                                   [mine, from_chips[0], from_chips[1], from_chips[2]])
    out_g = _unpack_shards(g_flat, shard_shapes, layout)
    out_d = _unpack_shards(d_flat, shard_shapes, layout)
    out_m = _unpack_shards(m2_flat, shard_shapes, layout)
    out_v = _unpack_shards(v2_flat, shard_shapes, layout)

    small_shapes = {n: wts[n].shape for n in _SMALL}
    g_small = _all_gather(_pack_small(small_g, loss), "ag_small")
    sg, sd, sm2, sv2 = _adamw_small(_pack_small(small_w), _pack_small({n: mom1[n] for n in _SMALL}),
                                    _pack_small({n: mom2[n] for n in _SMALL}), g_small)
    for blk, dst in ((sg, out_g), (sd, out_d), (sm2, out_m), (sv2, out_v)):
        dst.update(_unpack_small(blk, small_shapes))
    loss_out = sg[_LOSS_SLOT[0], _LOSS_SLOT[1]]

    expand = lambda d, n: d[n] if n in _SMALL else d[n][None]
    return (loss_out, dx[None], *[expand(out_g, n) for n in _ORDER], *[expand(out_d, n) for n in _ORDER],
            *[expand(out_m, n) for n in _ORDER], *[expand(out_v, n) for n in _ORDER])
```

```python
import functools
import math

import numpy as np
import jax
import jax.numpy as jnp
from jax import lax
from jax.experimental import pallas as pl
from jax.experimental.pallas import tpu as pltpu

F32 = jnp.float32
BF16 = jnp.bfloat16

D_MODEL = 1024
N_HEADS = 8
HEAD = 128
CHUNK = 64
GROUP = 256
ROPE = 64
Q_LORA = 384
KV_LORA = 256
FFN_HIDDEN = 2816
PLE_DIM = 256
ROPE_BASE = 10000.0
ALPHA = 2.0 ** 0.25
SCALE = float((HEAD + ROPE) ** -0.5)
NEG_BIG = -1e30
EPS_RMS = 1e-6
EPS_LN = 1e-5

ADAM_LR = 0.001
ADAM_B1 = 0.9
ADAM_B2 = 0.999
ADAM_EPS = 1e-08
ADAM_WD = 0.01
ADAM_STEP = 10

N_DEV = 8
LANES = 128
FLAT_COLS = 1024
FLAT_TILE = 128

WB_CQ, WB_CKV, WB_KR, WB_BA, WB_COLS = 0, 512, 768, 896, 1024

HIGHEST = lax.Precision.HIGHEST

NN = (((1,), (0,)), ((), ()))
TN = (((0,), (0,)), ((), ()))
NT = (((1,), (1,)), ((), ()))


def _dot(a, b, dims=NN):
    return lax.dot_general(a.astype(BF16), b.astype(BF16), dims, preferred_element_type=F32)


def _dot32(a, b, dims=NN):
    return lax.dot_general(a, b, dims, precision=HIGHEST, preferred_element_type=F32)


def _sig(x):
    return 1.0 / (1.0 + jnp.exp(-x))


def _pick(n, pref, unit):
    if n <= pref:
        return n
    t = (pref // unit) * unit
    while t >= unit:
        if n % t == 0:
            return t
        t -= unit
    return n


def _pick_wide(n):
    if n <= 1024:
        return n
    cands = [t for t in range(LANES, 1536 + 1, LANES) if n % t == 0]
    best = max(t for t in cands if t <= 1024)
    if best < 512 and cands[-1] > 1024:
        return cands[-1]
    return best


def _split_bf16(a):
    hi = a.astype(BF16)
    return hi, (a - hi.astype(F32)).astype(BF16)


def _dot3(a, b, dims=NN):
    ah, al = a if isinstance(a, tuple) else _split_bf16(a)
    bh, bl = b if isinstance(b, tuple) else _split_bf16(b)
    d = lambda p, q: lax.dot_general(p, q, dims, preferred_element_type=F32)
    return d(ah, bh) + (d(ah, bl) + d(al, bh))


def _mm(a, b, *, ta=False, tb=False, add=(), out_dtype=F32, name):
    if ta:
        k_dim, m_dim = a.shape
    else:
        m_dim, k_dim = a.shape
    if tb:
        n_dim, k2 = b.shape
    else:
        k2, n_dim = b.shape
    assert k_dim == k2, (a.shape, b.shape, ta, tb)
    tm = _pick_wide(m_dim)
    tn = _pick_wide(n_dim)
    tk = _pick(k_dim, 512, LANES)
    nk = k_dim // tk
    n_add = len(add)
    dims = TN if ta else (NT if tb else NN)
    assert not (ta and tb)

    def body(a_ref, b_ref, *rest):
        add_refs = rest[:n_add]
        o_ref = rest[n_add]
        acc = rest[n_add + 1]
        k = pl.program_id(2)

        @pl.when(k == 0)
        def _():
            acc[...] = jnp.zeros_like(acc)

        acc[...] += _dot(a_ref[...], b_ref[...], dims)

        @pl.when(k == nk - 1)
        def _():
            r = acc[...]
            for ar in add_refs:
                r = r + ar[...].astype(F32)
            o_ref[...] = r.astype(o_ref.dtype)

    a_spec = pl.BlockSpec((tk, tm), lambda i, j, k: (k, i)) if ta else pl.BlockSpec((tm, tk), lambda i, j, k: (i, k))
    b_spec = pl.BlockSpec((tn, tk), lambda i, j, k: (j, k)) if tb else pl.BlockSpec((tk, tn), lambda i, j, k: (k, j))
    o_spec = pl.BlockSpec((tm, tn), lambda i, j, k: (i, j))
    return pl.pallas_call(
        body,
        out_shape=jax.ShapeDtypeStruct((m_dim, n_dim), out_dtype),
        grid=(m_dim // tm, n_dim // tn, nk),
        in_specs=[a_spec, b_spec] + [o_spec] * n_add,
        out_specs=o_spec,
        scratch_shapes=[pltpu.VMEM((tm, tn), F32)],
        compiler_params=pltpu.CompilerParams(dimension_semantics=("parallel", "parallel", "arbitrary")),
        name=name,
    )(a, b, *add)


def _rowwise(fn, rows, consts, outs, accs=(), *, tm=256, name):
    rows = [r if isinstance(r, tuple) else (r, 0, r.shape[1]) for r in rows]
    s_dim = rows[0][0].shape[0]
    tm = min(tm, s_dim)
    assert s_dim % tm == 0 and all(arr.shape[0] == s_dim for arr, _, _ in rows)
    specs = [pl.BlockSpec((tm, width), functools.partial(lambda i, cb: (i, cb), cb=cb)) for _, cb, width in rows]
    args = [arr for arr, _, _ in rows]
    for c in consts:
        specs.append(pl.BlockSpec(c.shape, lambda i: (0, 0)))
        args.append(c)
    nr, nc, no = len(rows), len(consts), len(outs)
    out_shape = [jax.ShapeDtypeStruct((s_dim, w), dt) for (w, dt) in outs]
    out_specs = [pl.BlockSpec((tm, w), lambda i: (i, 0)) for (w, dt) in outs]
    out_shape += [jax.ShapeDtypeStruct(sh, F32) for sh in accs]
    out_specs += [pl.BlockSpec(sh, lambda i: (0, 0)) for sh in accs]

    def body(*refs):
        r = [x[...] for x in refs[:nr]]
        c = [x[...] for x in refs[nr:nr + nc]]
        o_refs = refs[nr + nc:nr + nc + no]
        a_refs = refs[nr + nc + no:]
        o_vals, a_vals = fn(r, c)
        for ref, v in zip(o_refs, o_vals, strict=True):
            ref[...] = v.astype(ref.dtype)
        if a_refs:
            @pl.when(pl.program_id(0) == 0)
            def _():
                for ref in a_refs:
                    ref[...] = jnp.zeros_like(ref)

            for ref, v in zip(a_refs, a_vals, strict=True):
                ref[...] += v

    res = pl.pallas_call(
        body,
        out_shape=out_shape,
        grid=(s_dim // tm,),
        in_specs=specs,
        out_specs=out_specs,
        compiler_params=pltpu.CompilerParams(dimension_semantics=("arbitrary" if accs else "parallel",)),
        name=name,
    )(*args)
    return res


def _colsum(v):
    return jnp.sum(v, axis=0, keepdims=True)


def _rowsum(v):
    return jnp.sum(v, axis=1, keepdims=True)


def _rowmean(v):
    return jnp.mean(v, axis=1, keepdims=True)


def _silu_grad(x):
    s = _sig(x)
    return s * (1.0 + x * (1.0 - s))


def _conv_taps(x, w, width=4):
    row = lax.broadcasted_iota(jnp.int32, x.shape, 0)
    c = x * w[width - 1:width, :]
    for s in range(1, width):
        c = c + jnp.where(row >= s, pltpu.roll(x, s, 0), 0.0) * w[width - 1 - s:width - s, :]
    return c


def _conv_fwd(proj_a, conv_w):
    s_dim = proj_a.shape[0]
    n_blk = 3 * N_HEADS

    def body(x_ref, w_ref, o_ref):
        j = pl.program_id(0)
        c = _conv_taps(x_ref[...], w_ref[...])
        y = c * _sig(c)
        r = lax.rsqrt(_rowsum(y * y) + EPS_RMS)
        fac = jnp.where(j < N_HEADS, r * (HEAD ** -0.5), jnp.where(j < 2 * N_HEADS, r, 1.0))
        o_ref[...] = y * fac

    return pl.pallas_call(
        body,
        out_shape=jax.ShapeDtypeStruct((s_dim, n_blk * HEAD), F32),
        grid=(n_blk,),
        in_specs=[pl.BlockSpec((s_dim, HEAD), lambda j: (0, j)), pl.BlockSpec((4, HEAD), lambda j: (0, j))],
        out_specs=pl.BlockSpec((s_dim, HEAD), lambda j: (0, j)),
        compiler_params=pltpu.CompilerParams(dimension_semantics=("parallel",)),
        name="conv_fwd",
    )(proj_a, conv_w)


def _conv_bwd(proj_a, conv_w, dq, dk, dv):
    s_dim = proj_a.shape[0]
    n_blk = 3 * N_HEADS

    def body(x_ref, w_ref, dq_ref, dk_ref, dv_ref, dx_ref, dw_ref):
        j = pl.program_id(0)
        x = x_ref[...]
        w = w_ref[...]
        do = jnp.where(j < N_HEADS, dq_ref[...], jnp.where(j < 2 * N_HEADS, dk_ref[...], dv_ref[...]))
        c = _conv_taps(x, w)
        sg = _sig(c)
        y = c * sg
        r = lax.rsqrt(_rowsum(y * y) + EPS_RMS)
        sc = jnp.where(j < N_HEADS, HEAD ** -0.5, 1.0)
        dy_n = sc * (r * do - y * (r * r * r) * _rowsum(do * y))
        dy = jnp.where(j < 2 * N_HEADS, dy_n, do)
        dc = dy * (sg * (1.0 + c * (1.0 - sg)))
        row = lax.broadcasted_iota(jnp.int32, x.shape, 0)
        dx = dc * w[3:4, :]
        dw_ref[3:4, :] = _colsum(dc * x)
        for s in range(1, 4):
            dx = dx + jnp.where(row < s_dim - s, pltpu.roll(dc, s_dim - s, 0), 0.0) * w[3 - s:4 - s, :]
            xs = jnp.where(row >= s, pltpu.roll(x, s, 0), 0.0)
            dw_ref[3 - s:4 - s, :] = _colsum(dc * xs)
        dx_ref[...] = dx.astype(dx_ref.dtype)

    hd = N_HEADS - 1
    return pl.pallas_call(
        body,
        out_shape=[jax.ShapeDtypeStruct((s_dim, n_blk * HEAD), BF16), jax.ShapeDtypeStruct((4, n_blk * HEAD), F32)],
        grid=(n_blk,),
        in_specs=[
            pl.BlockSpec((s_dim, HEAD), lambda j: (0, j)),
            pl.BlockSpec((4, HEAD), lambda j: (0, j)),
            pl.BlockSpec((s_dim, HEAD), lambda j: (0, jnp.minimum(j, hd))),
            pl.BlockSpec((s_dim, HEAD), lambda j: (0, jnp.clip(j - N_HEADS, 0, hd))),
            pl.BlockSpec((s_dim, HEAD), lambda j: (0, jnp.clip(j - 2 * N_HEADS, 0, hd))),
        ],
        out_specs=[pl.BlockSpec((s_dim, HEAD), lambda j: (0, j)), pl.BlockSpec((4, HEAD), lambda j: (0, j))],
        compiler_params=pltpu.CompilerParams(dimension_semantics=("parallel",)),
        name="conv_bwd",
    )(proj_a, conv_w, dq, dk, dv)


def _chunk_tri(n):
    r = np.arange(n)
    m = ((r[:, None] // CHUNK) == (r[None, :] // CHUNK)) & (r[:, None] >= r[None, :])
    m = m.astype(np.float32)
    return jnp.asarray(m), jnp.asarray(m.T)


def _softplus(z):
    return jnp.maximum(z, 0.0) + jnp.log(1.0 + jnp.exp(-jnp.abs(z)))


def _gates_fwd(proj_b, alog, dtb):
    tm = min(GROUP, proj_b.shape[0])
    tri, _ = _chunk_tri(tm)

    def fn(r, c):
        b = r[0]
        a = pltpu.roll(b, LANES - N_HEADS, 1)
        alog_, dtb_, tri_ = c
        g = -jnp.exp(alog_) * _softplus(a + dtb_)
        return [_sig(b), _dot32(tri_, g)], []

    return _rowwise(fn, [(proj_b, WB_BA // LANES, LANES)], [alog, dtb, tri],
                    [(LANES, F32), (LANES, F32)], tm=tm, name="gates_fwd")


def _gates_bwd(proj_b, alog, dtb, gc, d_beta, d_gc, d_egl_rows):
    tm = min(GROUP, proj_b.shape[0])
    _, tri_t = _chunk_tri(tm)

    def fn(r, c):
        b, gc_, d_beta_, d_gc_, d_egl_ = r
        a = pltpu.roll(b, LANES - N_HEADS, 1)
        alog_, dtb_, tri_t_ = c
        z = a + dtb_
        ea = jnp.exp(alog_)
        g = -ea * _softplus(z)
        dg = _dot32(tri_t_, d_gc_ + d_egl_ * jnp.exp(gc_))
        d_a = dg * (-ea) * _sig(z)
        beta = _sig(b)
        d_ba = d_beta_ * beta * (1.0 - beta) + pltpu.roll(d_a, N_HEADS, 1)
        return [d_ba], [_colsum(dg * g), _colsum(d_a)]

    return _rowwise(fn, [(proj_b, WB_BA // LANES, LANES), gc, d_beta, d_gc, d_egl_rows],
                    [alog, dtb, tri_t], [(LANES, BF16)], accs=[(1, LANES), (1, LANES)], tm=tm,
                    name="gates_bwd")


def _group_masks(n):
    r = lax.broadcasted_iota(jnp.int32, (n, n), 0)
    c = lax.broadcasted_iota(jnp.int32, (n, n), 1)
    same = (r // CHUNK) == (c // CHUNK)
    tril = jnp.logical_and(same, r >= c)
    strict = jnp.logical_and(same, r > c)
    last = c == (r // CHUNK) * CHUNK + (CHUNK - 1)
    eye = r == c
    return same, tril, strict, last, eye


def _inv_unit_lower(l_mat, eye_f):
    q = -l_mat
    r = eye_f + q
    qs = _split_bf16(q)
    for _ in range(5):
        qs = _split_bf16(_dot3(qs, qs))
        r = r + _dot3(r, qs)
    return r


def _unfold_blocks(folded, mask):
    n = folded.shape[0]
    return jnp.where(mask, jnp.concatenate([folded] * (n // CHUNK), axis=1), 0.0)


def _head_cols(beta, gc, gc_t, h):
    lane = lax.broadcasted_iota(jnp.int32, beta.shape, 1)
    sub = lax.broadcasted_iota(jnp.int32, gc_t.shape, 0)
    bcol = _rowsum(jnp.where(lane == h, beta, 0.0))
    gcol = _rowsum(jnp.where(lane == h, gc, 0.0))
    grow = _colsum(jnp.where(sub == h, gc_t, 0.0))
    return bcol, gcol, grow


def _prep_common(q, k, bcol, gcol, grow, t_folded=None):
    n = q.shape[0]
    same, tril, strict, last, eye = _group_masks(n)
    decay = jnp.where(tril, jnp.exp(jnp.where(tril, gcol - grow, 0.0)), 0.0)
    glast = _rowsum(jnp.where(last, jnp.broadcast_to(grow, (n, n)), 0.0))
    e = jnp.exp(gcol)
    ekt = jnp.exp(glast - gcol)
    kb = k * bcol
    kk = _dot(kb, k, NT)
    if t_folded is None:
        t_mat = _inv_unit_lower(jnp.where(strict, kk * decay, 0.0), eye.astype(F32))
    else:
        t_mat = _unfold_blocks(t_folded, same)
    qk = _dot(q, k, NT)
    return dict(same=same, tril=tril, strict=strict, last=last, eye=eye, decay=decay, e=e, ekt=ekt, kb=kb, kk=kk,
                t=t_mat, qk=qk)


def _fold_blocks(m):
    n = m.shape[0]
    out = m[:, 0:CHUNK]
    for b in range(1, n // CHUNK):
        out = out + m[:, b * CHUNK:(b + 1) * CHUNK]
    return out


def _gdr_prep_fwd(qkvn, beta, gc, gc_t):
    s_dim = qkvn.shape[0]
    tg = min(GROUP, s_dim)

    def body(q_ref, k_ref, v_ref, b_ref, g_ref, gt_ref, u_ref, w_ref, qd_ref, kt_ref, a_ref, t_ref):
        h = pl.program_id(0)
        q, k, v = q_ref[...], k_ref[...], v_ref[...]
        bcol, gcol, grow = _head_cols(b_ref[...], g_ref[...], gt_ref[...], h)
        p = _prep_common(q, k, bcol, gcol, grow)
        u_ref[...] = _dot(p["t"], v * bcol)
        w_ref[...] = _dot(p["t"], p["kb"] * p["e"])
        qd_ref[...] = q * p["e"]
        kt_ref[...] = k * p["ekt"]
        a_ref[...] = _fold_blocks(jnp.where(p["tril"], p["qk"] * p["decay"], 0.0))
        t_ref[...] = _fold_blocks(p["t"])

    row = lambda off: pl.BlockSpec((tg, HEAD), functools.partial(lambda h, m, off: (m, h + off), off=off))
    full = pl.BlockSpec((tg, LANES), lambda h, m: (m, 0))
    o_spec = pl.BlockSpec((tg, HEAD), lambda h, m: (m, h))
    a_spec = pl.BlockSpec((None, tg, CHUNK), lambda h, m: (h, m, 0))
    wide = jax.ShapeDtypeStruct((s_dim, N_HEADS * HEAD), F32)
    folded = jax.ShapeDtypeStruct((N_HEADS, s_dim, CHUNK), F32)
    return pl.pallas_call(
        body,
        out_shape=[wide, wide, wide, wide, folded, folded],
        grid=(N_HEADS, s_dim // tg),
        in_specs=[row(0), row(N_HEADS), row(2 * N_HEADS), full, full, pl.BlockSpec((8, tg), lambda h, m: (0, m))],
        out_specs=[o_spec, o_spec, o_spec, o_spec, a_spec, a_spec],
        compiler_params=pltpu.CompilerParams(dimension_semantics=("parallel", "parallel")),
        name="gdr_prep_fwd",
    )(qkvn, qkvn, qkvn, beta, gc, gc_t)


def _gdr_prep_bwd(qkvn, beta, gc, gc_t, t_fold, du, dw, dqd, dkt, d_a):
    s_dim = qkvn.shape[0]
    tg = min(GROUP, s_dim)

    def body(q_ref, k_ref, v_ref, b_ref, g_ref, gt_ref, t_ref, du_ref, dw_ref, dqd_ref, dkt_ref, da_ref,
             dq_ref, dk_ref, dv_ref, db_ref, dg_ref):
        h = pl.program_id(1)
        q, k, v = q_ref[...], k_ref[...], v_ref[...]
        bcol, gcol, grow = _head_cols(b_ref[...], g_ref[...], gt_ref[...], h)
        p = _prep_common(q, k, bcol, gcol, grow, t_ref[...])
        t_mat, decay, e, ekt, kb = p["t"], p["decay"], p["e"], p["ekt"], p["kb"]
        du_, dw_, dqd_, dkt_ = du_ref[...], dw_ref[...], dqd_ref[...], dkt_ref[...]
        vb = v * bcol
        kbe = kb * e
        d_t = _dot(du_, vb, NT) + _dot(dw_, kbe, NT)
        dvb = _dot(t_mat, du_, TN)
        dkbe = _dot(t_mat, dw_, TN)
        ts = _split_bf16(t_mat)
        d_l = -_dot3(_dot3(ts, d_t, TN), ts, NT)
        m1 = jnp.where(p["strict"], d_l, 0.0)
        m2 = _unfold_blocks(da_ref[...], p["tril"])
        d_kk = m1 * decay
        d_qk = m2 * decay
        d_decay = m1 * p["kk"] + m2 * p["qk"]
        dkb = _dot(d_kk, k) + dkbe * e
        dk = _dot(d_kk, kb, TN) + _dot(d_qk, q, TN) + dkt_ * ekt + dkb * bcol
        dq = _dot(d_qk, k) + dqd_ * e
        d_beta = _rowsum(dkb * k) + _rowsum(dvb * v)
        d_e = _rowsum(dkbe * kb) + _rowsum(dqd_ * q)
        d_ekt = _rowsum(dkt_ * k) * ekt
        d_diff = d_decay * decay
        d_grow = -_colsum(d_diff) + _colsum(jnp.where(p["last"], jnp.broadcast_to(d_ekt, (tg, tg)), 0.0))
        d_gcol = d_e * e - d_ekt + _rowsum(d_diff)
        d_gcol = d_gcol + _rowsum(jnp.where(p["eye"], jnp.broadcast_to(d_grow, (tg, tg)), 0.0))
        dq_ref[...] = dq
        dk_ref[...] = dk
        dv_ref[...] = dvb * bcol

        @pl.when(h == 0)
        def _():
            db_ref[...] = jnp.zeros_like(db_ref)
            dg_ref[...] = jnp.zeros_like(dg_ref)

        lane = lax.broadcasted_iota(jnp.int32, (tg, LANES), 1)
        db_ref[...] = jnp.where(lane == h, d_beta, db_ref[...])
        dg_ref[...] = jnp.where(lane == h, d_gcol, dg_ref[...])

    row = lambda off: pl.BlockSpec((tg, HEAD), functools.partial(lambda m, h, off: (m, h + off), off=off))
    full = pl.BlockSpec((tg, LANES), lambda m, h: (m, 0))
    o_spec = pl.BlockSpec((tg, HEAD), lambda m, h: (m, h))
    a_spec = pl.BlockSpec((None, tg, CHUNK), lambda m, h: (h, m, 0))
    wide = jax.ShapeDtypeStruct((s_dim, N_HEADS * HEAD), F32)
    lanes = jax.ShapeDtypeStruct((s_dim, LANES), F32)
    return pl.pallas_call(
        body,
        out_shape=[wide, wide, wide, lanes, lanes],
        grid=(s_dim // tg, N_HEADS),
        in_specs=[row(0), row(N_HEADS), row(2 * N_HEADS), full, full, pl.BlockSpec((8, tg), lambda m, h: (0, m)),
                  a_spec, o_spec, o_spec, o_spec, o_spec, a_spec],
        out_specs=[o_spec, o_spec, o_spec, full, full],
        compiler_params=pltpu.CompilerParams(dimension_semantics=("parallel", "arbitrary")),
        name="gdr_prep_bwd",
    )(qkvn, qkvn, qkvn, beta, gc, gc_t, t_fold, du, dw, dqd, dkt, d_a)


def _gdr_scan_fwd(u, w, qd, kt, a_mat, gc):
    s_dim = u.shape[0]
    n_chunks = s_dim // CHUNK

    def body(u_ref, w_ref, qd_ref, kt_ref, a_ref, g_ref, o_ref, st_ref, state):
        @pl.when(pl.program_id(0) == 0)
        def _():
            state[...] = jnp.zeros_like(state)

        egl = jnp.exp(g_ref[CHUNK - 1:CHUNK, :])
        for h in range(N_HEADS):
            cs = slice(h * HEAD, (h + 1) * HEAD)
            s_h = state[h]
            st_ref[h] = s_h
            vn = u_ref[:, cs] - _dot(w_ref[:, cs], s_h)
            o_ref[:, cs] = _dot(qd_ref[:, cs], s_h) + _dot(a_ref[h], vn)
            state[h] = s_h * egl[:, h:h + 1] + _dot(kt_ref[:, cs], vn, TN)

    wide = pl.BlockSpec((CHUNK, N_HEADS * HEAD), lambda n: (n, 0))
    return pl.pallas_call(
        body,
        out_shape=[jax.ShapeDtypeStruct((s_dim, N_HEADS * HEAD), F32),
                   jax.ShapeDtypeStruct((n_chunks, N_HEADS, HEAD, HEAD), F32)],
        grid=(n_chunks,),
        in_specs=[wide, wide, wide, wide, pl.BlockSpec((N_HEADS, CHUNK, CHUNK), lambda n: (0, n, 0)),
                  pl.BlockSpec((CHUNK, LANES), lambda n: (n, 0))],
        out_specs=[wide, pl.BlockSpec((None, N_HEADS, HEAD, HEAD), lambda n: (n, 0, 0, 0))],
        scratch_shapes=[pltpu.VMEM((N_HEADS, HEAD, HEAD), F32)],
        compiler_params=pltpu.CompilerParams(dimension_semantics=("arbitrary",)),
        name="gdr_scan_fwd",
    )(u, w, qd, kt, a_mat, gc)


def _gdr_scan_bwd(u, w, qd, kt, a_mat, gc, states, d_o):
    s_dim = u.shape[0]
    n_chunks = s_dim // CHUNK
    last = n_chunks - 1

    def body(u_ref, w_ref, qd_ref, kt_ref, a_ref, g_ref, st_ref, do_ref,
             du_ref, dw_ref, dqd_ref, dkt_ref, da_ref, de_ref, d_state):
        @pl.when(pl.program_id(0) == 0)
        def _():
            d_state[...] = jnp.zeros_like(d_state)

        egl = jnp.exp(g_ref[CHUNK - 1:CHUNK, :])
        for h in range(N_HEADS):
            cs = slice(h * HEAD, (h + 1) * HEAD)
            s_h = st_ref[h]
            ds_n = d_state[h]
            do = do_ref[:, cs]
            w_h = w_ref[:, cs]
            vn = u_ref[:, cs] - _dot(w_h, s_h)
            dvn = _dot(a_ref[h], do, TN) + _dot(kt_ref[:, cs], ds_n)
            dqd_ref[:, cs] = _dot(do, s_h, NT)
            da_ref[h] = _dot(do, vn, NT)
            dkt_ref[:, cs] = _dot(vn, ds_n, NT)
            de = jnp.sum(_rowsum(ds_n * s_h), axis=0, keepdims=True)
            de_ref[h:h + 1, :] = jnp.broadcast_to(de, (1, LANES))
            du_ref[:, cs] = dvn
            dw_ref[:, cs] = -_dot(dvn, s_h, NT)
            d_state[h] = ds_n * egl[:, h:h + 1] + _dot(qd_ref[:, cs], do, TN) - _dot(w_h, dvn, TN)

    wide = pl.BlockSpec((CHUNK, N_HEADS * HEAD), lambda n: (last - n, 0))
    a_spec = pl.BlockSpec((N_HEADS, CHUNK, CHUNK), lambda n: (0, last - n, 0))
    wide_shape = jax.ShapeDtypeStruct((s_dim, N_HEADS * HEAD), F32)
    return pl.pallas_call(
        body,
        out_shape=[wide_shape, wide_shape, wide_shape, wide_shape,
                   jax.ShapeDtypeStruct((N_HEADS, s_dim, CHUNK), F32),
                   jax.ShapeDtypeStruct((n_chunks, N_HEADS, LANES), F32)],
        grid=(n_chunks,),
        in_specs=[wide, wide, wide, wide, a_spec, pl.BlockSpec((CHUNK, LANES), lambda n: (last - n, 0)),
                  pl.BlockSpec((None, N_HEADS, HEAD, HEAD), lambda n: (last - n, 0, 0, 0)), wide],
        out_specs=[wide, wide, wide, wide, a_spec, pl.BlockSpec((None, N_HEADS, LANES), lambda n: (last - n, 0, 0))],
        scratch_shapes=[pltpu.VMEM((N_HEADS, HEAD, HEAD), F32)],
        compiler_params=pltpu.CompilerParams(dimension_semantics=("arbitrary",)),
        name="gdr_scan_bwd",
    )(u, w, qd, kt, a_mat, gc, states, d_o)


def _gdr_out_fwd(o_dn, proj_a, dn_w):
    def fn(r, c):
        o, z = r
        (w_,) = c
        outs = []
        for h in range(N_HEADS):
            cs = slice(h * HEAD, (h + 1) * HEAD)
            oh, zh = o[:, cs], z[:, cs]
            rr = lax.rsqrt(_rowmean(oh * oh) + EPS_RMS)
            outs.append(oh * rr * w_ * (zh * _sig(zh)))
        return [jnp.concatenate(outs, axis=1)], []

    return _rowwise(fn, [o_dn, (proj_a, 3, D_MODEL)], [dn_w], [(D_MODEL, BF16)], name="gdr_out_fwd")[0]


def _gdr_out_bwd(o_dn, proj_a, d_og, dn_w):
    def fn(r, c):
        o, z, dg = r
        (w_,) = c
        d_o, d_z = [], []
        d_w = jnp.zeros((1, HEAD), F32)
        for h in range(N_HEADS):
            cs = slice(h * HEAD, (h + 1) * HEAD)
            oh, zh, dgh = o[:, cs], z[:, cs], dg[:, cs]
            rr = lax.rsqrt(_rowmean(oh * oh) + EPS_RMS)
            sz = zh * _sig(zh)
            d_n = dgh * sz
            d_z.append(dgh * (oh * rr * w_) * _silu_grad(zh))
            d_w = d_w + _colsum(d_n * oh * rr)
            gw = d_n * w_
            d_o.append(rr * gw - oh * (rr * rr * rr) * _rowmean(gw * oh))
        return [jnp.concatenate(d_o, axis=1), jnp.concatenate(d_z, axis=1)], [d_w]

    return _rowwise(fn, [o_dn, (proj_a, 3, D_MODEL), d_og], [dn_w], [(D_MODEL, F32), (D_MODEL, BF16)],
                    accs=[(1, HEAD)], name="gdr_out_bwd")


def _rms_fwd(x, w):
    r = lax.rsqrt(_rowmean(x * x) + EPS_RMS)
    return x * r * w


def _rms_bwd(x, w, dy):
    r = lax.rsqrt(_rowmean(x * x) + EPS_RMS)
    gw = dy * w
    return r * gw - x * (r * r * r) * _rowmean(gw * x), _colsum(dy * x * r)


def _mla_norm_fwd(proj_b, qn_w, kvn_w):
    def fn(r, c):
        return [_rms_fwd(r[0], c[0]), _rms_fwd(r[1], c[1])], []

    return _rowwise(fn, [(proj_b, WB_CQ // Q_LORA, Q_LORA), (proj_b, WB_CKV // KV_LORA, KV_LORA)], [qn_w, kvn_w],
                    [(Q_LORA, BF16), (KV_LORA, BF16)], name="mla_norm_fwd")


def _mla_norm_bwd(proj_b, qn_w, kvn_w, d_cq, d_ckv):
    def fn(r, c):
        dx1, dw1 = _rms_bwd(r[0], c[0], r[2])
        dx2, dw2 = _rms_bwd(r[1], c[1], r[3])
        return [dx1, dx2], [dw1, dw2]

    return _rowwise(fn, [(proj_b, WB_CQ // Q_LORA, Q_LORA), (proj_b, WB_CKV // KV_LORA, KV_LORA), d_cq, d_ckv],
                    [qn_w, kvn_w], [(Q_LORA, BF16), (KV_LORA, BF16)], accs=[(1, Q_LORA), (1, KV_LORA)],
                    name="mla_norm_bwd")


def _rope_consts():
    inv = ROPE_BASE ** (-np.arange(0, ROPE, 2, dtype=np.float32) / ROPE)
    t = np.zeros((4, LANES), np.float32)
    t[0, :32] = inv
    t[0, 32:64] = inv
    t[1, :64] = 1.0
    t[2, 32:64] = 1.0
    t[3, :32] = -1.0
    return jnp.asarray(t)


def _rope_tables(pos, consts, width):
    ang = pos * consts[0:1, :]
    cosv, sinv = jnp.cos(ang), jnp.sin(ang)
    reps = width // LANES
    tile = (lambda t: jnp.concatenate([t] * reps, axis=1)) if reps > 1 else (lambda t: t)
    return tile(cosv * consts[1:2, :]), tile(sinv * consts[2:3, :]), tile(sinv * consts[3:4, :])


def _rope_apply(t, tabs):
    cos_t, sin_a, sin_b = tabs
    width = t.shape[1]
    return t * cos_t + pltpu.roll(t, 32, 1) * sin_a + pltpu.roll(t, width - 32, 1) * sin_b


def _rope_transpose(d, tabs):
    cos_t, sin_a, sin_b = tabs
    width = d.shape[1]
    return d * cos_t + pltpu.roll(d * sin_a, width - 32, 1) + pltpu.roll(d * sin_b, 32, 1)


QK_HEAD = 2 * HEAD


def _interleave_heads(a, b):
    parts = []
    for h in range(N_HEADS):
        parts.append(a[:, h * HEAD:(h + 1) * HEAD])
        parts.append(b if b.shape[1] == LANES else b[:, h * LANES:(h + 1) * LANES])
    return jnp.concatenate(parts, axis=1)


def _mla_qk_fwd(q_full, k_nope, proj_b, pos):
    consts = _rope_consts()

    def fn(r, c):
        qf, kn, kr, pos_ = r
        qn, qr = qf[:, :D_MODEL], qf[:, D_MODEL:]
        qr = _rope_apply(qr, _rope_tables(pos_, c[0], D_MODEL))
        kr = _rope_apply(kr, _rope_tables(pos_, c[0], LANES))
        return [_interleave_heads(qn, qr) * SCALE, _interleave_heads(kn, kr)], []

    return _rowwise(fn, [q_full, k_nope, (proj_b, WB_KR // LANES, LANES), pos], [consts],
                    [(N_HEADS * QK_HEAD, BF16), (N_HEADS * QK_HEAD, BF16)], name="mla_qk_fwd")


def _mla_qk_bwd(d_qc, d_kc, pos):
    consts = _rope_consts()

    def fn(r, c):
        dq, dk, pos_ = r
        even = lambda t: jnp.concatenate([t[:, (2 * h) * LANES:(2 * h + 1) * LANES] for h in range(N_HEADS)], axis=1)
        odd = lambda t: jnp.concatenate([t[:, (2 * h + 1) * LANES:(2 * h + 2) * LANES] for h in range(N_HEADS)], axis=1)
        d_qr_raw = _rope_transpose(odd(dq), _rope_tables(pos_, c[0], D_MODEL)) * SCALE
        dkr = dk[:, LANES:2 * LANES]
        for h in range(1, N_HEADS):
            dkr = dkr + dk[:, (2 * h + 1) * LANES:(2 * h + 2) * LANES]
        return [jnp.concatenate([even(dq) * SCALE, d_qr_raw], axis=1), even(dk),
                _rope_transpose(dkr, _rope_tables(pos_, c[0], LANES))], []

    return _rowwise(fn, [d_qc, d_kc, pos], [consts], [(2 * D_MODEL, BF16), (D_MODEL, BF16), (LANES, BF16)],
                    name="mla_qk_bwd")


def _causal_mask(s, row0, col0):
    row = lax.broadcasted_iota(jnp.int32, s.shape, 0) + row0
    col = lax.broadcasted_iota(jnp.int32, s.shape, 1) + col0
    return jnp.where(col <= row, s, NEG_BIG)


def _attn_tiles(s_dim):
    tq = min(512, s_dim)
    n_chains = 2 if s_dim >= 2 * tq else 1
    return tq, n_chains, min(512, s_dim)


def _attn_fwd(qc, kc, v):
    s_dim = qc.shape[0]
    tq, n_chains, tk = _attn_tiles(s_dim)
    tqs = tq * n_chains

    def body(q_ref, k_ref, v_ref, o_ref, lse_ref, m_s, l_s, acc):
        qi = pl.program_id(1)
        m_s[...] = jnp.full_like(m_s, NEG_BIG)
        l_s[...] = jnp.zeros_like(l_s)
        acc[...] = jnp.zeros_like(acc)

        def step(j, carry):
            ks = pl.multiple_of(j * tk, tk)
            kb, vb = k_ref[pl.ds(ks, tk), :], v_ref[pl.ds(ks, tk), :]
            for c in range(n_chains):
                rows = slice(c * tq, (c + 1) * tq)
                s = _causal_mask(_dot(q_ref[rows, :], kb, NT), qi * tqs + c * tq, j * tk)
                m_prev = m_s[rows, 0:1]
                m_new = jnp.maximum(m_prev, jnp.max(s, axis=1, keepdims=True))
                alpha = jnp.exp(m_prev - m_new)
                p = jnp.exp(s - m_new)
                l_s[rows, :] = jnp.broadcast_to(alpha * l_s[rows, 0:1] + _rowsum(p), (tq, LANES))
                m_s[rows, :] = jnp.broadcast_to(m_new, (tq, LANES))
                acc[rows, :] = acc[rows, :] * alpha + _dot(p, vb)
            return carry

        lax.fori_loop(0, (qi + 1) * (tqs // tk), step, 0)
        l = l_s[...]
        o_ref[...] = acc[...] / l
        lse_ref[...] = m_s[...] + jnp.log(l)

    q_spec = pl.BlockSpec((tqs, QK_HEAD), lambda h, qi: (qi, h))
    o_spec = pl.BlockSpec((tqs, HEAD), lambda h, qi: (qi, h))
    return pl.pallas_call(
        body,
        out_shape=[jax.ShapeDtypeStruct((s_dim, N_HEADS * HEAD), F32)] * 2,
        grid=(N_HEADS, s_dim // tqs),
        in_specs=[q_spec, pl.BlockSpec((s_dim, QK_HEAD), lambda h, qi: (0, h)),
                  pl.BlockSpec((s_dim, HEAD), lambda h, qi: (0, h))],
        out_specs=[o_spec, o_spec],
        scratch_shapes=[pltpu.VMEM((tqs, LANES), F32), pltpu.VMEM((tqs, LANES), F32), pltpu.VMEM((tqs, HEAD), F32)],
        compiler_params=pltpu.CompilerParams(dimension_semantics=("parallel", "parallel")),
        name="attn_fwd",
    )(qc, kc, v)


def _attn_delta(o, d_o):
    def fn(r, c):
        o_, do_ = r
        outs = []
        for h in range(N_HEADS):
            cs = slice(h * HEAD, (h + 1) * HEAD)
            outs.append(jnp.broadcast_to(_rowsum(o_[:, cs] * do_[:, cs]), (o_.shape[0], HEAD)))
        return [jnp.concatenate(outs, axis=1)], []

    return _rowwise(fn, [o, d_o], [], [(D_MODEL, F32)], name="attn_delta")[0]


def _attn_bwd(qc, kc, v, d_o, lse, delta):
    s_dim = qc.shape[0]
    tq, n_chains, tk = _attn_tiles(s_dim)
    tqs = tq * n_chains

    def body(q_ref, k_ref, v_ref, do_ref, lse_ref, dl_ref, dq_ref, dk_ref, dv_ref, dq_acc, dv_acc):
        qi = pl.program_id(1)

        @pl.when(qi == 0)
        def _():
            dk_ref[...] = jnp.zeros_like(dk_ref)
            dv_acc[...] = jnp.zeros_like(dv_acc)

        dq_acc[...] = jnp.zeros_like(dq_acc)
        do_all = do_ref[...].astype(BF16)
        q_all = q_ref[...]

        def step(j, carry):
            ks = pl.multiple_of(j * tk, tk)
            kb, vb = k_ref[pl.ds(ks, tk), :], v_ref[pl.ds(ks, tk), :]
            ps, dss = [], []
            for c in range(n_chains):
                rows = slice(c * tq, (c + 1) * tq)
                s = _causal_mask(_dot(q_all[rows, :], kb, NT), qi * tqs + c * tq, j * tk)
                p = jnp.exp(s - lse_ref[rows, 0:1])
                ds = p * (_dot(do_all[rows, :], vb, NT) - dl_ref[rows, 0:1])
                dq_acc[rows, :] += _dot(ds, kb)
                ps.append(p.astype(BF16))
                dss.append(ds.astype(BF16))
            p_all = jnp.concatenate(ps, axis=0) if n_chains > 1 else ps[0]
            ds_all = jnp.concatenate(dss, axis=0) if n_chains > 1 else dss[0]
            dk_ref[pl.ds(ks, tk), :] += _dot(ds_all, q_all, TN)
            dv_acc[pl.ds(ks, tk), :] += _dot(p_all, do_all, TN)
            return carry

        lax.fori_loop(0, (qi + 1) * (tqs // tk), step, 0)
        dq_ref[...] = dq_acc[...]

        @pl.when(qi == s_dim // tqs - 1)
        def _():
            dv_ref[...] = dv_acc[...].astype(dv_ref.dtype)

    q_spec = pl.BlockSpec((tqs, QK_HEAD), lambda h, qi: (qi, h))
    o_spec = pl.BlockSpec((tqs, HEAD), lambda h, qi: (qi, h))
    k_spec = pl.BlockSpec((s_dim, QK_HEAD), lambda h, qi: (0, h))
    v_spec = pl.BlockSpec((s_dim, HEAD), lambda h, qi: (0, h))
    wide2 = jax.ShapeDtypeStruct((s_dim, N_HEADS * QK_HEAD), F32)
    return pl.pallas_call(
        body,
        out_shape=[wide2, wide2, jax.ShapeDtypeStruct((s_dim, N_HEADS * HEAD), BF16)],
        grid=(N_HEADS, s_dim // tqs),
        in_specs=[q_spec, k_spec, v_spec, o_spec, o_spec, o_spec],
        out_specs=[q_spec, k_spec, v_spec],
        scratch_shapes=[pltpu.VMEM((tqs, QK_HEAD), F32), pltpu.VMEM((s_dim, HEAD), F32)],
        compiler_params=pltpu.CompilerParams(dimension_semantics=("parallel", "arbitrary")),
        name="attn_bwd",
    )(qc, kc, v, d_o, lse, delta)


def _merge_fwd(y_dn, y_mla, proj_g):
    def fn(r, c):
        yd, ym, g = r
        return [_sig(g[:, :D_MODEL]) * yd + _sig(g[:, D_MODEL:]) * ym], []

    return _rowwise(fn, [y_dn, y_mla, proj_g], [], [(D_MODEL, BF16)], name="merge_fwd")[0]


def _merge_bwd(y_dn, y_mla, proj_g, d_mixed):
    def fn(r, c):
        yd, ym, g, dm = r
        sd, sm = _sig(g[:, :D_MODEL]), _sig(g[:, D_MODEL:])
        d_g = jnp.concatenate([dm * yd * sd * (1.0 - sd), dm * ym * sm * (1.0 - sm)], axis=1)
        return [d_g, dm * sd, dm * sm], []

    return _rowwise(fn, [y_dn, y_mla, proj_g, d_mixed], [], [(2 * D_MODEL, BF16), (D_MODEL, BF16), (D_MODEL, BF16)],
                    name="merge_bwd")


def _ln_stats(z):
    mu = _rowmean(z)
    zc = z - mu
    r = lax.rsqrt(_rowmean(zc * zc) + EPS_LN)
    return zc * r, r


def _ln_bwd(dy, xh, r, g):
    dxh = dy * g
    return r * (dxh - _rowmean(dxh) - xh * _rowmean(dxh * xh))


def _ln1_fwd(x, a1, g, b):
    def fn(r, c):
        xh, _ = _ln_stats(ALPHA * r[0] + r[1])
        y = xh * c[0] + c[1]
        return [y, y], []

    return _rowwise(fn, [x, a1], [g, b], [(D_MODEL, F32), (D_MODEL, BF16)], name="ln1_fwd")


def _ln1_bwd(x, a1, d_h1, g):
    def fn(r, c):
        xh, rr = _ln_stats(ALPHA * r[0] + r[1])
        dy = r[2]
        dz = _ln_bwd(dy, xh, rr, c[0])
        return [dz, ALPHA * dz], [_colsum(dy * xh), _colsum(dy)]

    return _rowwise(fn, [x, a1, d_h1], [g], [(D_MODEL, BF16), (D_MODEL, F32)], accs=[(1, D_MODEL), (1, D_MODEL)],
                    name="ln1_bwd")


def _act_fwd(gu):
    def fn(r, c):
        gt, up = r[0][:, :FFN_HIDDEN], r[0][:, FFN_HIDDEN:]
        return [gt * _sig(gt) * up], []

    return _rowwise(fn, [gu], [], [(FFN_HIDDEN, BF16)], name="act_fwd")[0]


def _act_bwd(gu, d_act):
    def fn(r, c):
        gt, up = r[0][:, :FFN_HIDDEN], r[0][:, FFN_HIDDEN:]
        da = r[1]
        return [jnp.concatenate([da * up * _silu_grad(gt), da * gt * _sig(gt)], axis=1)], []

    return _rowwise(fn, [gu, d_act], [], [(2 * FFN_HIDDEN, BF16)], name="act_bwd")[0]


def _tail(h1, ffn, pg, pp, tgt, g, b):
    def fn(r, c):
        h1_, ffn_, pg_, pp_, t_ = r
        sp = _sig(pg_)
        xh, rr = _ln_stats(ALPHA * h1_ + ffn_ + sp * pp_)
        y = xh * c[0] + c[1]
        err = y - t_
        dy = err * (1.0 / D_MODEL)
        dz = _ln_bwd(dy, xh, rr, c[0])
        loss = jnp.sum(0.5 * _rowmean(err * err), axis=0, keepdims=True)
        return ([dz, dz * pp_ * sp * (1.0 - sp), dz * sp, ALPHA * dz],
                [_colsum(dy * xh), _colsum(dy), jnp.broadcast_to(loss, (1, LANES))])

    return _rowwise(fn, [h1, ffn, pg, pp, tgt], [g, b], [(D_MODEL, BF16)] * 3 + [(D_MODEL, F32)],
                    accs=[(1, D_MODEL), (1, D_MODEL), (1, LANES)], name="tail")


def _pad_heads_to_lanes(per_head_lane0, s_dim):
    t = jnp.transpose(per_head_lane0[:, :, 0])
    return jnp.pad(t, ((0, 0), (0, LANES - N_HEADS)))


def _local_step(x, p, pos, tgt, w):
    s_dim = x.shape[0]
    xb, pb = x.astype(BF16), p.astype(BF16)
    proj_a = _mm(xb, w["wa_t"], tb=True, name="f_proj_a")
    proj_g = _mm(xb, w["wg_t"], tb=True, name="f_proj_g")
    proj_b = _mm(xb, w["wb_t"], tb=True, name="f_proj_b")
    qkvn = _conv_fwd(proj_a, w["conv"])
    beta, gc = _gates_fwd(proj_b, w["alog"], w["dtb"])
    gc_t = jnp.transpose(gc[:, :N_HEADS])
    u, w_, qd, kt, a_mat, t_fold = _gdr_prep_fwd(qkvn, beta, gc, gc_t)
    o_dn, states = _gdr_scan_fwd(u, w_, qd, kt, a_mat, gc)
    og = _gdr_out_fwd(o_dn, proj_a, w["dnw"])
    y_dn = _mm(og, w["br_dn"], name="f_y_dn")
    c_q, c_kv = _mla_norm_fwd(proj_b, w["qnw"], w["kvnw"])
    q_full = _mm(c_q, w["uq"], name="f_q_full")
    k_nope = _mm(c_kv, w["uk"], name="f_k_nope")
    vv = _mm(c_kv, w["uv"], out_dtype=BF16, name="f_v")
    qc, kc = _mla_qk_fwd(q_full, k_nope, proj_b, pos)
    o_mla, lse = _attn_fwd(qc, kc, vv)
    y_mla = _mm(o_mla, w["br_mla"], name="f_y_mla")
    mixed = _merge_fwd(y_dn, y_mla, proj_g)
    a1 = _mm(mixed, w["wo"], name="f_a1")
    h1, h1b = _ln1_fwd(x, a1, w["ln1g"], w["ln1b"])
    gu = _mm(h1b, w["ffn_in_t"], tb=True, name="f_gu")
    act = _act_fwd(gu)
    ffn = _mm(act, w["ffn_out"], name="f_ffn")
    pg = _mm(h1b, w["ple_gate"], name="f_pg")
    pp = _mm(pb, w["ple_t"], tb=True, name="f_pp")
    g = {}
    dz2, d_pg, d_pp, dh1a, g["ln2g"], g["ln2b"], loss = _tail(h1, ffn, pg, pp, tgt, w["ln2g"], w["ln2b"])
    g["ple_t"] = _mm(d_pp, pb, ta=True, name="b_w_ple")
    g["ple_gate"] = _mm(h1b, d_pg, ta=True, name="b_w_ple_gate")
    g["ffn_out"] = _mm(act, dz2, ta=True, name="b_w_ffn_out")
    d_act = _mm(dz2, w["ffn_out"], tb=True, name="b_act")
    d_gu = _act_bwd(gu, d_act)
    g["ffn_in_t"] = _mm(d_gu, h1b, ta=True, name="b_w_ffn_in")
    d_h1 = _mm(d_gu, w["ffn_in_t"], add=(dh1a,), name="b_h1_ffn")
    d_h1 = _mm(d_pg, w["ple_gate"], tb=True, add=(d_h1,), name="b_h1_ple")
    dz1, dxa, g["ln1g"], g["ln1b"] = _ln1_bwd(x, a1, d_h1, w["ln1g"])
    g["wo"] = _mm(mixed, dz1, ta=True, name="b_w_o")
    d_mixed = _mm(dz1, w["wo"], tb=True, name="b_mixed")
    d_proj_g, d_y_dn, d_y_mla = _merge_bwd(y_dn, y_mla, proj_g, d_mixed)
    g["br_mla"] = _mm(o_mla, d_y_mla, ta=True, name="b_w_br_mla")
    d_o_mla = _mm(d_y_mla, w["br_mla"], tb=True, name="b_o_mla")
    delta = _attn_delta(o_mla, d_o_mla)
    d_qc, d_kc, d_v = _attn_bwd(qc, kc, vv, d_o_mla, lse, delta)
    d_q_full, d_kn, d_kr = _mla_qk_bwd(d_qc, d_kc, pos)
    g["uq"] = _mm(c_q, d_q_full, ta=True, name="b_w_uq")
    d_c_q = _mm(d_q_full, w["uq"], tb=True, name="b_c_q")
    g["uk"] = _mm(c_kv, d_kn, ta=True, name="b_w_uk")
    g["uv"] = _mm(c_kv, d_v, ta=True, name="b_w_uv")
    d_c_kv = _mm(d_kn, w["uk"], tb=True, name="b_c_kv_k")
    d_c_kv = _mm(d_v, w["uv"], tb=True, add=(d_c_kv,), name="b_c_kv_v")
    d_cq, d_ckv, g["qnw"], g["kvnw"] = _mla_norm_bwd(proj_b, w["qnw"], w["kvnw"], d_c_q, d_c_kv)
    g["br_dn"] = _mm(og, d_y_dn, ta=True, name="b_w_br_dn")
    d_og = _mm(d_y_dn, w["br_dn"], tb=True, name="b_og")
    d_o_dn, d_z, g["dnw"] = _gdr_out_bwd(o_dn, proj_a, d_og, w["dnw"])
    du, dw, dqd, dkt, d_a, d_egl = _gdr_scan_bwd(u, w_, qd, kt, a_mat, gc, states, d_o_dn)
    dq, dk, dv, d_beta, d_gc = _gdr_prep_bwd(qkvn, beta, gc, gc_t, t_fold, du, dw, dqd, dkt, d_a)
    d_egl_rows = jnp.pad(d_egl[:, None, :, 0], ((0, 0), (CHUNK - 1, 0), (0, LANES - N_HEADS))).reshape(s_dim, LANES)
    d_ba, g["alog"], g["dtb"] = _gates_bwd(proj_b, w["alog"], w["dtb"], gc, d_beta, d_gc, d_egl_rows)
    d_qkv, g["conv"] = _conv_bwd(proj_a, w["conv"], dq, dk, dv)
    zeros = jnp.zeros((s_dim, WB_CKV - Q_LORA), BF16)
    d_proj_b = jnp.concatenate([d_cq, zeros, d_ckv, d_kr, d_ba], axis=1)
    g["wa_qkv_t"] = _mm(d_qkv, xb, ta=True, name="b_w_qkv")
    g["wa_z_t"] = _mm(d_z, xb, ta=True, name="b_w_z")
    g["wg_t"] = _mm(d_proj_g, xb, ta=True, name="b_w_g")
    g["wb_t"] = _mm(d_proj_b, xb, ta=True, name="b_w_b")
    dx = _mm(d_qkv, w["wa_qkv_t"], add=(dxa,), name="b_x_qkv")
    dx = _mm(d_z, w["wa_z_t"], add=(dx,), name="b_x_z")
    dx = _mm(d_proj_g, w["wg_t"], add=(dx,), name="b_x_g")
    dx = _mm(d_proj_b, w["wb_t"], add=(dx,), name="b_x_b")
    return loss, dx, g


_BIG = (("w_in", 1), ("conv_w", 1), ("w_uq", 0), ("w_uk", 0), ("w_uv", 0), ("w_br_dn", 0), ("w_br_mla", 0),
        ("w_o", 0), ("w_ffn_in", 1), ("w_ffn_out", 0), ("w_ple", 1), ("w_ple_gate", 0))
_BIG_AXIS = dict(_BIG)
_SMALL = ("ln1_g", "ln1_b", "ln2_g", "ln2_b", "q_norm_w", "kv_norm_w", "dn_norm_w", "dn_a_log", "dn_dt_bias")
_ORDER = ("w_in", "conv_w", "dn_a_log", "dn_dt_bias", "dn_norm_w", "q_norm_w", "w_uq", "kv_norm_w", "w_uk", "w_uv",
          "w_br_dn", "w_br_mla", "w_o", "ln1_g", "ln1_b", "w_ffn_in", "w_ffn_out", "w_ple", "w_ple_gate", "ln2_g",
          "ln2_b")


ROW_ALIGN = 16


def _flat_rows(shape):
    rows = -(-int(np.prod(shape)) // FLAT_COLS)
    return -(-rows // ROW_ALIGN) * ROW_ALIGN


def _stored_shape(name, shard_shape):
    axis = _BIG_AXIS[name]
    lead = shard_shape[axis]
    return lead, int(np.prod(shard_shape)) // lead


def _to_stored(name, shard):
    return jnp.moveaxis(shard, _BIG_AXIS[name], 0).reshape(_stored_shape(name, shard.shape))


def _from_stored(name, stored, shard_shape):
    axis = _BIG_AXIS[name]
    moved = (shard_shape[axis],) + shard_shape[:axis] + shard_shape[axis + 1:]
    return jnp.moveaxis(stored.reshape(moved), 0, axis)


def _flat_layout(shard_shapes):
    rows, off = {}, 0
    for name, _ in _BIG:
        n = _flat_rows(shard_shapes[name])
        rows[name] = (off, n)
        off += n
    total = -(-off // FLAT_TILE) * FLAT_TILE
    return rows, total


def _to_rows(a, lead, n_rows):
    lead_shape = a.shape[:lead]
    if a.shape[-1] == FLAT_COLS:
        return jnp.pad(a, [(0, 0)] * lead + [(0, n_rows - a.shape[lead]), (0, 0)])
    flat = a.reshape(lead_shape + (-1,))
    flat = jnp.pad(flat, [(0, 0)] * lead + [(0, n_rows * FLAT_COLS - flat.shape[-1])])
    return flat.reshape(lead_shape + (n_rows, FLAT_COLS))


def _from_rows(rows, stored_shape):
    r, c = stored_shape
    lead_shape = rows.shape[:-2]
    if c == FLAT_COLS:
        return rows[..., :r, :]
    return rows.reshape(lead_shape + (-1,))[..., :r * c].reshape(lead_shape + (r, c))


def _pack_shards(shards, layout, total_rows):
    parts = [_to_rows(_to_stored(name, shards[name]), 0, layout[name][1]) for name, _ in _BIG]
    flat = jnp.concatenate(parts, axis=0)
    return jnp.pad(flat, ((0, total_rows - flat.shape[0]), (0, 0)))


def _unpack_shards(flat, shard_shapes, layout):
    out = {}
    for name, _ in _BIG:
        off, n = layout[name]
        stored = _from_rows(flat[off:off + n], _stored_shape(name, shard_shapes[name]))
        out[name] = _from_stored(name, stored, shard_shapes[name])
    return out


def _unpack_gathered(gathered, shard_shapes, layout):
    out = {}
    for name, _ in _BIG:
        off, n = layout[name]
        r, c = _stored_shape(name, shard_shapes[name])
        out[name] = _from_rows(gathered[:, off:off + n], (r, c)).reshape(N_DEV * r, c)
    return out


def _pack_grads_for_parity(stored_grads, shard_shapes, layout, total_rows, parity):
    parts = []
    for name, _ in _BIG:
        r, c = _stored_shape(name, shard_shapes[name])
        gsh = stored_grads[name].reshape(4, 2, r, c)
        gsh = lax.dynamic_index_in_dim(gsh, parity, axis=1, keepdims=False)
        parts.append(_to_rows(gsh, 1, layout[name][1]))
    flat = jnp.concatenate(parts, axis=1)
    return jnp.pad(flat, ((0, 0), (0, total_rows - flat.shape[1]), (0, 0)))


_W_IN_ROWS = np.cumsum([0, 3072, 1024, 8, 8, Q_LORA, KV_LORA, ROPE, D_MODEL, D_MODEL])


def _full_weights(fw, small):
    w_in_t = fw["w_in"]
    r = _W_IN_ROWS
    zr = lambda n: jnp.zeros((n, D_MODEL), w_in_t.dtype)
    w = {}
    w["wa_t"] = w_in_t[r[0]:r[2]]
    w["wa_qkv_t"], w["wa_z_t"] = w_in_t[r[0]:r[1]], w_in_t[r[1]:r[2]]
    w["wg_t"] = w_in_t[r[7]:r[9]]
    w["wb_t"] = jnp.concatenate([w_in_t[r[4]:r[5]], zr(WB_CKV - Q_LORA), w_in_t[r[5]:r[7]], zr(LANES - ROPE),
                                 w_in_t[r[2]:r[4]], zr(LANES - 2 * N_HEADS)], axis=0)
    uq = fw["w_uq"].reshape(Q_LORA, N_HEADS, HEAD + ROPE)
    uq_r = jnp.pad(uq[:, :, HEAD:], ((0, 0), (0, 0), (0, HEAD - ROPE)))
    w["uq"] = jnp.concatenate([uq[:, :, :HEAD].reshape(Q_LORA, -1), uq_r.reshape(Q_LORA, -1)], axis=1)
    w["uk"], w["uv"] = fw["w_uk"], fw["w_uv"]
    w["conv"] = jnp.transpose(fw["conv_w"]).astype(F32)
    w["br_dn"], w["br_mla"], w["wo"] = fw["w_br_dn"], fw["w_br_mla"], fw["w_o"]
    w["ffn_in_t"], w["ffn_out"] = fw["w_ffn_in"], fw["w_ffn_out"]
    w["ple_t"], w["ple_gate"] = fw["w_ple"], fw["w_ple_gate"]
    pad_l = lambda v: jnp.pad(v, ((0, 0), (0, LANES - v.shape[1])))
    w["alog"], w["dtb"] = pad_l(small["dn_a_log"]), pad_l(small["dn_dt_bias"])
    w["dnw"], w["qnw"], w["kvnw"] = small["dn_norm_w"], small["q_norm_w"], small["kv_norm_w"]
    w["ln1g"], w["ln1b"], w["ln2g"], w["ln2b"] = small["ln1_g"], small["ln1_b"], small["ln2_g"], small["ln2_b"]
    return w


def _full_grads(g):
    wb = g["wb_t"]
    full = {}
    full["w_in"] = jnp.concatenate([
        g["wa_qkv_t"], g["wa_z_t"], wb[WB_BA:WB_BA + 2 * N_HEADS], wb[WB_CQ:WB_CQ + Q_LORA],
        wb[WB_CKV:WB_CKV + KV_LORA], wb[WB_KR:WB_KR + ROPE], g["wg_t"]], axis=0)
    uq = g["uq"]
    uq_n = uq[:, :D_MODEL].reshape(Q_LORA, N_HEADS, HEAD)
    uq_r = uq[:, D_MODEL:].reshape(Q_LORA, N_HEADS, HEAD)[:, :, :ROPE]
    full["w_uq"] = jnp.concatenate([uq_n, uq_r], axis=2).reshape(Q_LORA, -1)
    full["w_uk"], full["w_uv"] = g["uk"], g["uv"]
    full["conv_w"] = jnp.transpose(g["conv"])
    full["w_br_dn"], full["w_br_mla"], full["w_o"] = g["br_dn"], g["br_mla"], g["wo"]
    full["w_ffn_in"], full["w_ffn_out"] = g["ffn_in_t"], g["ffn_out"]
    full["w_ple"], full["w_ple_gate"] = g["ple_t"], g["ple_gate"]
    small = {"ln1_g": g["ln1g"], "ln1_b": g["ln1b"], "ln2_g": g["ln2g"], "ln2_b": g["ln2b"], "q_norm_w": g["qnw"],
             "kv_norm_w": g["kvnw"], "dn_norm_w": g["dnw"], "dn_a_log": g["alog"][:, :N_HEADS],
             "dn_dt_bias": g["dtb"][:, :N_HEADS]}
    return full, small


_SMALL_SLOTS = {"ln1_g": (0, 0, 1024), "ln1_b": (1, 0, 1024), "ln2_g": (2, 0, 1024), "ln2_b": (3, 0, 1024),
                "q_norm_w": (4, 0, 384), "kv_norm_w": (4, 384, 256), "dn_norm_w": (4, 640, 128),
                "dn_a_log": (4, 768, 8), "dn_dt_bias": (4, 776, 8)}
_LOSS_SLOT = (4, 896)


def _pack_small(vals, loss=None):
    blk = jnp.zeros((8, FLAT_COLS), F32)
    for name, (r, c, n) in _SMALL_SLOTS.items():
        blk = lax.dynamic_update_slice(blk, vals[name].reshape(1, n).astype(F32), (r, c))
    if loss is not None:
        blk = lax.dynamic_update_slice(blk, loss[:, :1], _LOSS_SLOT)
    return blk


def _unpack_small(blk, shapes):
    return {name: blk[r:r + 1, c:c + n].reshape(shapes[name]) for name, (r, c, n) in _SMALL_SLOTS.items()}


_MESH_ID = pl.DeviceIdType.MESH
_ANY = pl.BlockSpec(memory_space=pl.ANY)


def _all_gather(block, name):
    def body(x_ref, out_ref, send_sems, recv_sems, local_sem):
        x, y, c = lax.axis_index("x"), lax.axis_index("y"), lax.axis_index("c")
        me, sibling = (x, y, c), (x, y, 1 - c)
        chips = [(1 - x, y), (x, 1 - y), (1 - x, 1 - y)]

        def slot(px, py, pc):
            return out_ref.at[4 * px + 2 * py + pc]

        def copy(k, origin, to, src=None):
            return pltpu.make_async_remote_copy(
                src_ref=slot(*origin) if src is None else src, dst_ref=slot(*origin), send_sem=send_sems.at[k],
                recv_sem=recv_sems.at[k], device_id=to, device_id_type=_MESH_ID)

        mine = pltpu.make_async_copy(x_ref, slot(*me), local_sem)
        mine.start()
        first = [copy(0, me, sibling, src=x_ref)]
        first += [copy(1 + j, me, (*chip, c), src=x_ref) for j, chip in enumerate(chips)]
        for cp in first:
            cp.start()
        passed = [copy(4 + j, (*chip, c), sibling) for j, chip in enumerate(chips)]
        for j, chip in enumerate(chips):
            copy(1 + j, (*chip, c), me).wait_recv()
            passed[j].start()
        copy(0, sibling, me).wait_recv()
        for j, chip in enumerate(chips):
            copy(4 + j, (*chip, 1 - c), me).wait_recv()
        for cp in first + passed:
            cp.wait_send()
        mine.wait()

    return pl.pallas_call(
        body,
        out_shape=jax.ShapeDtypeStruct((N_DEV,) + block.shape, block.dtype),
        in_specs=[_ANY],
        out_specs=_ANY,
        scratch_shapes=[pltpu.SemaphoreType.DMA((7,)), pltpu.SemaphoreType.DMA((7,)), pltpu.SemaphoreType.DMA],
        name=name,
    )(block)


def _exchange_sibling(src, name):
    def body(src_ref, dst_ref, send_sems, recv_sems):
        x, y, c = lax.axis_index("x"), lax.axis_index("y"), lax.axis_index("c")
        copies = [pltpu.make_async_remote_copy(
            src_ref=src_ref.at[q], dst_ref=dst_ref.at[q], send_sem=send_sems.at[q], recv_sem=recv_sems.at[q],
            device_id=(x, y, 1 - c), device_id_type=_MESH_ID) for q in range(4)]
        for cp in copies:
            cp.start()
        for cp in copies:
            cp.wait_recv()
        for cp in copies:
            cp.wait_send()

    return pl.pallas_call(
        body,
        out_shape=jax.ShapeDtypeStruct(src.shape, src.dtype),
        in_specs=[_ANY],
        out_specs=_ANY,
        scratch_shapes=[pltpu.SemaphoreType.DMA((4,)), pltpu.SemaphoreType.DMA((4,))],
        name=name,
    )(src)


def _exchange_chips(src, name):
    def body(src_ref, dst_ref, send_sems, recv_sems):
        x, y, c = lax.axis_index("x"), lax.axis_index("y"), lax.axis_index("c")
        chips = [(1 - x, y), (x, 1 - y), (1 - x, 1 - y)]
        copies = [pltpu.make_async_remote_copy(
            src_ref=src_ref.at[2 * tx + ty], dst_ref=dst_ref.at[j], send_sem=send_sems.at[j],
            recv_sem=recv_sems.at[j], device_id=(tx, ty, c), device_id_type=_MESH_ID)
            for j, (tx, ty) in enumerate(chips)]
        for cp in copies:
            cp.start()
        for cp in copies:
            cp.wait_recv()
        for cp in copies:
            cp.wait_send()

    return pl.pallas_call(
        body,
        out_shape=jax.ShapeDtypeStruct((3,) + src.shape[1:], src.dtype),
        in_specs=[_ANY],
        out_specs=_ANY,
        scratch_shapes=[pltpu.SemaphoreType.DMA((3,)), pltpu.SemaphoreType.DMA((3,))],
        name=name,
    )(src)


def _add_pairs(a, b, name):
    n, r, c = a.shape

    def fn(rows, consts):
        s = rows[0] + rows[1]
        return [s, s], []

    out, out_bf = _rowwise(fn, [a.reshape(n * r, c), b.reshape(n * r, c)], [], [(c, F32), (c, BF16)], tm=FLAT_TILE,
                           name=name)
    return out.reshape(n, r, c), out_bf.reshape(n, r, c)


def _adamw_math(w, g, m, v):
    m = ADAM_B1 * m + (1.0 - ADAM_B1) * g
    v = ADAM_B2 * v + (1.0 - ADAM_B2) * (g * g)
    m_hat = m / (1.0 - ADAM_B1 ** ADAM_STEP)
    v_hat = v / (1.0 - ADAM_B2 ** ADAM_STEP)
    delta = -ADAM_LR * (m_hat / (jnp.sqrt(v_hat) + ADAM_EPS) + ADAM_WD * w)
    return delta, m, v


def _adamw_flat(w, m, v, g_parts):
    def fn(rows, consts):
        w_, m_, v_ = rows[:3]
        g = rows[3]
        for part in rows[4:]:
            g = g + part
        delta, m2, v2 = _adamw_math(w_, g, m_, v_)
        return [g, delta, m2, v2], []

    return _rowwise(fn, [w, m, v] + list(g_parts), [], [(FLAT_COLS, F32)] * 4, tm=FLAT_TILE, name="adamw_flat")


def _adamw_small(w, m, v, gathered):
    def body(w_ref, m_ref, v_ref, g_ref, go_ref, d_ref, m2_ref, v2_ref):
        g = g_ref[0]
        for k in range(1, N_DEV):
            g = g + g_ref[k]
        delta, m2, v2 = _adamw_math(w_ref[...], g, m_ref[...], v_ref[...])
        go_ref[...] = g
        d_ref[...] = delta
        m2_ref[...] = m2
        v2_ref[...] = v2

    blk = jax.ShapeDtypeStruct((8, FLAT_COLS), F32)
    return pl.pallas_call(body, out_shape=[blk] * 4, name="adamw_small")(w, m, v, gathered)


def kernel(x, p, positions, w_in, conv_w, dn_a_log, dn_dt_bias, dn_norm_w, q_norm_w, w_uq, kv_norm_w, w_uk, w_uv, w_br_dn, w_br_mla, w_o, ln1_g, ln1_b, w_ffn_in, w_ffn_out, w_ple, w_ple_gate, ln2_g, ln2_b, loss_target, m_w_in, m_conv_w, m_dn_a_log, m_dn_dt_bias, m_dn_norm_w, m_q_norm_w, m_w_uq, m_kv_norm_w, m_w_uk, m_w_uv, m_w_br_dn, m_w_br_mla, m_w_o, m_ln1_g, m_ln1_b, m_w_ffn_in, m_w_ffn_out, m_w_ple, m_w_ple_gate, m_ln2_g, m_ln2_b, v_w_in, v_conv_w, v_dn_a_log, v_dn_dt_bias, v_dn_norm_w, v_q_norm_w, v_w_uq, v_kv_norm_w, v_w_uk, v_w_uv, v_w_br_dn, v_w_br_mla, v_w_o, v_ln1_g, v_ln1_b, v_w_ffn_in, v_w_ffn_out, v_w_ple, v_w_ple_gate, v_ln2_g, v_ln2_b):
    args = dict(locals())
    wts = {n: args[n] for n in _ORDER}
    mom1 = {n: args["m_" + n] for n in _ORDER}
    mom2 = {n: args["v_" + n] for n in _ORDER}
    big_names = [n for n, _ in _BIG]
    shard_shapes = {n: wts[n].shape[1:] for n in big_names}
    layout, total_rows = _flat_layout(shard_shapes)
    drop = lambda d, names: {n: d[n][0] for n in names}

    w_flat = _pack_shards(drop(wts, big_names), layout, total_rows)
    gathered = _all_gather(w_flat.astype(BF16), "ag_weights")
    full_w = _unpack_gathered(gathered, shard_shapes, layout)
    small_w = {n: wts[n].astype(F32) for n in _SMALL}
    w = _full_weights(full_w, small_w)

    s_dim = x.shape[1]
    loss, dx, g = _local_step(x[0], p[0, 0], positions.reshape(s_dim, 1).astype(F32), loss_target[0], w)
    full_g, small_g = _full_grads(g)

    c_idx = lax.axis_index("c")
    q_idx = 2 * lax.axis_index("x") + lax.axis_index("y")
    g_own = _pack_grads_for_parity(full_g, shard_shapes, layout, total_rows, c_idx)
    g_sib = _pack_grads_for_parity(full_g, shard_shapes, layout, total_rows, 1 - c_idx)
    from_sibling = _exchange_sibling(g_sib, "rs_sibling")
    chip_sum, chip_sum_bf = _add_pairs(g_own, from_sibling, "rs_chip_sum")
    from_chips = _exchange_chips(chip_sum_bf, "rs_chips")
    mine = lax.dynamic_index_in_dim(chip_sum, q_idx, axis=0, keepdims=False)

    m_flat = _pack_shards(drop(mom1, big_names), layout, total_rows)
    v_flat = _pack_shards(drop(mom2, big_names), layout, total_rows)
    g_flat, d_flat, m2_flat, v2_flat = _adamw_flat(w_flat, m_flat, v_flat,
                                                   [mine, from_chips[0], from_chips[1], from_chips[2]])
    out_g = _unpack_shards(g_flat, shard_shapes, layout)
    out_d = _unpack_shards(d_flat, shard_shapes, layout)
    out_m = _unpack_shards(m2_flat, shard_shapes, layout)
    out_v = _unpack_shards(v2_flat, shard_shapes, layout)

    small_shapes = {n: wts[n].shape for n in _SMALL}
    g_small = _all_gather(_pack_small(small_g, loss), "ag_small")
    sg, sd, sm2, sv2 = _adamw_small(_pack_small(small_w), _pack_small({n: mom1[n] for n in _SMALL}),
                                    _pack_small({n: mom2[n] for n in _SMALL}), g_small)
    for blk, dst in ((sg, out_g), (sd, out_d), (sm2, out_m), (sv2, out_v)):
        dst.update(_unpack_small(blk, small_shapes))
    loss_out = sg[_LOSS_SLOT[0], _LOSS_SLOT[1]]

    expand = lambda d, n: d[n] if n in _SMALL else d[n][None]
    return (loss_out, dx[None], *[expand(out_g, n) for n in _ORDER], *[expand(out_d, n) for n in _ORDER],
            *[expand(out_m, n) for n in _ORDER], *[expand(out_v, n) for n in _ORDER])
```

```python
import functools
import math

import numpy as np
import jax
import jax.numpy as jnp
from jax import lax
from jax.experimental import pallas as pl
from jax.experimental.pallas import tpu as pltpu

F32 = jnp.float32
BF16 = jnp.bfloat16

D_MODEL = 1024
N_HEADS = 8
HEAD = 128
CHUNK = 64
GROUP = 256
ROPE = 64
Q_LORA = 384
KV_LORA = 256
FFN_HIDDEN = 2816
PLE_DIM = 256
ROPE_BASE = 10000.0
ALPHA = 2.0 ** 0.25
SCALE = float((HEAD + ROPE) ** -0.5)
NEG_BIG = -1e30
EPS_RMS = 1e-6
EPS_LN = 1e-5

ADAM_LR = 0.001
ADAM_B1 = 0.9
ADAM_B2 = 0.999
ADAM_EPS = 1e-08
ADAM_WD = 0.01
ADAM_STEP = 10

N_DEV = 8
LANES = 128
FLAT_COLS = 1024
FLAT_TILE = 128

WB_CQ, WB_CKV, WB_KR, WB_BA, WB_COLS = 0, 512, 768, 896, 1024

HIGHEST = lax.Precision.HIGHEST

NN = (((1,), (0,)), ((), ()))
TN = (((0,), (0,)), ((), ()))
NT = (((1,), (1,)), ((), ()))


def _dot(a, b, dims=NN):
    return lax.dot_general(a.astype(BF16), b.astype(BF16), dims, preferred_element_type=F32)


def _dot32(a, b, dims=NN):
    return lax.dot_general(a, b, dims, precision=HIGHEST, preferred_element_type=F32)


def _sig(x):
    return 1.0 / (1.0 + jnp.exp(-x))


def _pick(n, pref, unit):
    if n <= pref:
        return n
    t = (pref // unit) * unit
    while t >= unit:
        if n % t == 0:
            return t
        t -= unit
    return n


def _pick_wide(n):
    if n <= 1024:
        return n
    cands = [t for t in range(LANES, 1536 + 1, LANES) if n % t == 0]
    best = max(t for t in cands if t <= 1024)
    if best < 512 and cands[-1] > 1024:
        return cands[-1]
    return best


def _split_bf16(a):
    hi = a.astype(BF16)
    return hi, (a - hi.astype(F32)).astype(BF16)


def _dot3(a, b, dims=NN):
    ah, al = a if isinstance(a, tuple) else _split_bf16(a)
    bh, bl = b if isinstance(b, tuple) else _split_bf16(b)
    d = lambda p, q: lax.dot_general(p, q, dims, preferred_element_type=F32)
    return d(ah, bh) + (d(ah, bl) + d(al, bh))


def _mm(a, b, *, ta=False, tb=False, add=(), out_dtype=F32, name):
    if ta:
        k_dim, m_dim = a.shape
    else:
        m_dim, k_dim = a.shape
    if tb:
        n_dim, k2 = b.shape
    else:
        k2, n_dim = b.shape
    assert k_dim == k2, (a.shape, b.shape, ta, tb)
    tm = _pick_wide(m_dim)
    tn = _pick_wide(n_dim)
    tk = _pick(k_dim, 512, LANES)
    nk = k_dim // tk
    n_add = len(add)
    dims = TN if ta else (NT if tb else NN)
    assert not (ta and tb)

    def body(a_ref, b_ref, *rest):
        add_refs = rest[:n_add]
        o_ref = rest[n_add]
        acc = rest[n_add + 1]
        k = pl.program_id(2)

        @pl.when(k == 0)
        def _():
            acc[...] = jnp.zeros_like(acc)

        acc[...] += _dot(a_ref[...], b_ref[...], dims)

        @pl.when(k == nk - 1)
        def _():
            r = acc[...]
            for ar in add_refs:
                r = r + ar[...].astype(F32)
            o_ref[...] = r.astype(o_ref.dtype)

    a_spec = pl.BlockSpec((tk, tm), lambda i, j, k: (k, i)) if ta else pl.BlockSpec((tm, tk), lambda i, j, k: (i, k))
    b_spec = pl.BlockSpec((tn, tk), lambda i, j, k: (j, k)) if tb else pl.BlockSpec((tk, tn), lambda i, j, k: (k, j))
    o_spec = pl.BlockSpec((tm, tn), lambda i, j, k: (i, j))
    return pl.pallas_call(
        body,
        out_shape=jax.ShapeDtypeStruct((m_dim, n_dim), out_dtype),
        grid=(m_dim // tm, n_dim // tn, nk),
        in_specs=[a_spec, b_spec] + [o_spec] * n_add,
        out_specs=o_spec,
        scratch_shapes=[pltpu.VMEM((tm, tn), F32)],
        compiler_params=pltpu.CompilerParams(dimension_semantics=("parallel", "parallel", "arbitrary")),
        name=name,
    )(a, b, *add)


def _rowwise(fn, rows, consts, outs, accs=(), *, tm=256, name):
    rows = [r if isinstance(r, tuple) else (r, 0, r.shape[1]) for r in rows]
    s_dim = rows[0][0].shape[0]
    tm = min(tm, s_dim)
    assert s_dim % tm == 0 and all(arr.shape[0] == s_dim for arr, _, _ in rows)
    specs = [pl.BlockSpec((tm, width), functools.partial(lambda i, cb: (i, cb), cb=cb)) for _, cb, width in rows]
    args = [arr for arr, _, _ in rows]
    for c in consts:
        specs.append(pl.BlockSpec(c.shape, lambda i: (0, 0)))
        args.append(c)
    nr, nc, no = len(rows), len(consts), len(outs)
    out_shape = [jax.ShapeDtypeStruct((s_dim, w), dt) for (w, dt) in outs]
    out_specs = [pl.BlockSpec((tm, w), lambda i: (i, 0)) for (w, dt) in outs]
    out_shape += [jax.ShapeDtypeStruct(sh, F32) for sh in accs]
    out_specs += [pl.BlockSpec(sh, lambda i: (0, 0)) for sh in accs]

    def body(*refs):
        r = [x[...] for x in refs[:nr]]
        c = [x[...] for x in refs[nr:nr + nc]]
        o_refs = refs[nr + nc:nr + nc + no]
        a_refs = refs[nr + nc + no:]
        o_vals, a_vals = fn(r, c)
        for ref, v in zip(o_refs, o_vals, strict=True):
            ref[...] = v.astype(ref.dtype)
        if a_refs:
            @pl.when(pl.program_id(0) == 0)
            def _():
                for ref in a_refs:
                    ref[...] = jnp.zeros_like(ref)

            for ref, v in zip(a_refs, a_vals, strict=True):
                ref[...] += v

    res = pl.pallas_call(
        body,
        out_shape=out_shape,
        grid=(s_dim // tm,),
        in_specs=specs,
        out_specs=out_specs,
        compiler_params=pltpu.CompilerParams(dimension_semantics=("arbitrary" if accs else "parallel",)),
        name=name,
    )(*args)
    return res


def _colsum(v):
    return jnp.sum(v, axis=0, keepdims=True)


def _rowsum(v):
    return jnp.sum(v, axis=1, keepdims=True)


def _rowmean(v):
    return jnp.mean(v, axis=1, keepdims=True)


def _silu_grad(x):
    s = _sig(x)
    return s * (1.0 + x * (1.0 - s))


def _conv_taps(x, w, width=4):
    row = lax.broadcasted_iota(jnp.int32, x.shape, 0)
    c = x * w[width - 1:width, :]
    for s in range(1, width):
        c = c + jnp.where(row >= s, pltpu.roll(x, s, 0), 0.0) * w[width - 1 - s:width - s, :]
    return c


def _conv_fwd(proj_a, conv_w):
    s_dim = proj_a.shape[0]
    n_blk = 3 * N_HEADS

    def body(x_ref, w_ref, o_ref):
        j = pl.program_id(0)
        c = _conv_taps(x_ref[...], w_ref[...])
        y = c * _sig(c)
        r = lax.rsqrt(_rowsum(y * y) + EPS_RMS)
        fac = jnp.where(j < N_HEADS, r * (HEAD ** -0.5), jnp.where(j < 2 * N_HEADS, r, 1.0))
        o_ref[...] = y * fac

    return pl.pallas_call(
        body,
        out_shape=jax.ShapeDtypeStruct((s_dim, n_blk * HEAD), F32),
        grid=(n_blk,),
        in_specs=[pl.BlockSpec((s_dim, HEAD), lambda j: (0, j)), pl.BlockSpec((4, HEAD), lambda j: (0, j))],
        out_specs=pl.BlockSpec((s_dim, HEAD), lambda j: (0, j)),
        compiler_params=pltpu.CompilerParams(dimension_semantics=("parallel",)),
        name="conv_fwd",
    )(proj_a, conv_w)


def _conv_bwd(proj_a, conv_w, dq, dk, dv):
    s_dim = proj_a.shape[0]
    n_blk = 3 * N_HEADS

    def body(x_ref, w_ref, dq_ref, dk_ref, dv_ref, dx_ref, dw_ref):
        j = pl.program_id(0)
        x = x_ref[...]
        w = w_ref[...]
        do = jnp.where(j < N_HEADS, dq_ref[...], jnp.where(j < 2 * N_HEADS, dk_ref[...], dv_ref[...]))
        c = _conv_taps(x, w)
        sg = _sig(c)
        y = c * sg
        r = lax.rsqrt(_rowsum(y * y) + EPS_RMS)
        sc = jnp.where(j < N_HEADS, HEAD ** -0.5, 1.0)
        dy_n = sc * (r * do - y * (r * r * r) * _rowsum(do * y))
        dy = jnp.where(j < 2 * N_HEADS, dy_n, do)
        dc = dy * (sg * (1.0 + c * (1.0 - sg)))
        row = lax.broadcasted_iota(jnp.int32, x.shape, 0)
        dx = dc * w[3:4, :]
        dw_ref[3:4, :] = _colsum(dc * x)
        for s in range(1, 4):
            dx = dx + jnp.where(row < s_dim - s, pltpu.roll(dc, s_dim - s, 0), 0.0) * w[3 - s:4 - s, :]
            xs = jnp.where(row >= s, pltpu.roll(x, s, 0), 0.0)
            dw_ref[3 - s:4 - s, :] = _colsum(dc * xs)
        dx_ref[...] = dx.astype(dx_ref.dtype)

    hd = N_HEADS - 1
    return pl.pallas_call(
        body,
        out_shape=[jax.ShapeDtypeStruct((s_dim, n_blk * HEAD), BF16), jax.ShapeDtypeStruct((4, n_blk * HEAD), F32)],
        grid=(n_blk,),
        in_specs=[
            pl.BlockSpec((s_dim, HEAD), lambda j: (0, j)),
            pl.BlockSpec((4, HEAD), lambda j: (0, j)),
            pl.BlockSpec((s_dim, HEAD), lambda j: (0, jnp.minimum(j, hd))),
            pl.BlockSpec((s_dim, HEAD), lambda j: (0, jnp.clip(j - N_HEADS, 0, hd))),
            pl.BlockSpec((s_dim, HEAD), lambda j: (0, jnp.clip(j - 2 * N_HEADS, 0, hd))),
        ],
        out_specs=[pl.BlockSpec((s_dim, HEAD), lambda j: (0, j)), pl.BlockSpec((4, HEAD), lambda j: (0, j))],
        compiler_params=pltpu.CompilerParams(dimension_semantics=("parallel",)),
        name="conv_bwd",
    )(proj_a, conv_w, dq, dk, dv)


def _chunk_tri(n):
    r = np.arange(n)
    m = ((r[:, None] // CHUNK) == (r[None, :] // CHUNK)) & (r[:, None] >= r[None, :])
    m = m.astype(np.float32)
    return jnp.asarray(m), jnp.asarray(m.T)


def _softplus(z):
    return jnp.maximum(z, 0.0) + jnp.log(1.0 + jnp.exp(-jnp.abs(z)))


def _gates_fwd(proj_b, alog, dtb):
    tm = min(GROUP, proj_b.shape[0])
    tri, _ = _chunk_tri(tm)

    def fn(r, c):
        b = r[0]
        a = pltpu.roll(b, LANES - N_HEADS, 1)
        alog_, dtb_, tri_ = c
        g = -jnp.exp(alog_) * _softplus(a + dtb_)
        return [_sig(b), _dot32(tri_, g)], []

    return _rowwise(fn, [(proj_b, WB_BA // LANES, LANES)], [alog, dtb, tri],
                    [(LANES, F32), (LANES, F32)], tm=tm, name="gates_fwd")


def _gates_bwd(proj_b, alog, dtb, gc, d_beta, d_gc, d_egl_rows):
    tm = min(GROUP, proj_b.shape[0])
    _, tri_t = _chunk_tri(tm)

    def fn(r, c):
        b, gc_, d_beta_, d_gc_, d_egl_ = r
        a = pltpu.roll(b, LANES - N_HEADS, 1)
        alog_, dtb_, tri_t_ = c
        z = a + dtb_
        ea = jnp.exp(alog_)
        g = -ea * _softplus(z)
        dg = _dot32(tri_t_, d_gc_ + d_egl_ * jnp.exp(gc_))
        d_a = dg * (-ea) * _sig(z)
        beta = _sig(b)
        d_ba = d_beta_ * beta * (1.0 - beta) + pltpu.roll(d_a, N_HEADS, 1)
        return [d_ba], [_colsum(dg * g), _colsum(d_a)]

    return _rowwise(fn, [(proj_b, WB_BA // LANES, LANES), gc, d_beta, d_gc, d_egl_rows],
                    [alog, dtb, tri_t], [(LANES, BF16)], accs=[(1, LANES), (1, LANES)], tm=tm,
                    name="gates_bwd")


def _group_masks(n):
    r = lax.broadcasted_iota(jnp.int32, (n, n), 0)
    c = lax.broadcasted_iota(jnp.int32, (n, n), 1)
    same = (r // CHUNK) == (c // CHUNK)
    tril = jnp.logical_and(same, r >= c)
    strict = jnp.logical_and(same, r > c)
    last = c == (r // CHUNK) * CHUNK + (CHUNK - 1)
    eye = r == c
    return same, tril, strict, last, eye


def _inv_unit_lower(l_mat, eye_f):
    q = -l_mat
    r = eye_f + q
    qs = _split_bf16(q)
    for _ in range(5):
        qs = _split_bf16(_dot3(qs, qs))
        r = r + _dot3(r, qs)
    return r


def _unfold_blocks(folded, mask):
    n = folded.shape[0]
    return jnp.where(mask, jnp.concatenate([folded] * (n // CHUNK), axis=1), 0.0)


def _head_cols(beta, gc, gc_t, h):
    lane = lax.broadcasted_iota(jnp.int32, beta.shape, 1)
    sub = lax.broadcasted_iota(jnp.int32, gc_t.shape, 0)
    bcol = _rowsum(jnp.where(lane == h, beta, 0.0))
    gcol = _rowsum(jnp.where(lane == h, gc, 0.0))
    grow = _colsum(jnp.where(sub == h, gc_t, 0.0))
    return bcol, gcol, grow


def _prep_common(q, k, bcol, gcol, grow, t_folded=None):
    n = q.shape[0]
    same, tril, strict, last, eye = _group_masks(n)
    decay = jnp.where(tril, jnp.exp(jnp.where(tril, gcol - grow, 0.0)), 0.0)
    glast = _rowsum(jnp.where(last, jnp.broadcast_to(grow, (n, n)), 0.0))
    e = jnp.exp(gcol)
    ekt = jnp.exp(glast - gcol)
    kb = k * bcol
    kk = _dot(kb, k, NT)
    if t_folded is None:
        t_mat = _inv_unit_lower(jnp.where(strict, kk * decay, 0.0), eye.astype(F32))
    else:
        t_mat = _unfold_blocks(t_folded, same)
    qk = _dot(q, k, NT)
    return dict(same=same, tril=tril, strict=strict, last=last, eye=eye, decay=decay, e=e, ekt=ekt, kb=kb, kk=kk,
                t=t_mat, qk=qk)


def _fold_blocks(m):
    n = m.shape[0]
    out = m[:, 0:CHUNK]
    for b in range(1, n // CHUNK):
        out = out + m[:, b * CHUNK:(b + 1) * CHUNK]
    return out


def _gdr_prep_fwd(qkvn, beta, gc, gc_t):
    s_dim = qkvn.shape[0]
    tg = min(GROUP, s_dim)

    def body(q_ref, k_ref, v_ref, b_ref, g_ref, gt_ref, u_ref, w_ref, qd_ref, kt_ref, a_ref, t_ref):
        h = pl.program_id(0)
        q, k, v = q_ref[...], k_ref[...], v_ref[...]
        bcol, gcol, grow = _head_cols(b_ref[...], g_ref[...], gt_ref[...], h)
        p = _prep_common(q, k, bcol, gcol, grow)
        u_ref[...] = _dot(p["t"], v * bcol)
        w_ref[...] = _dot(p["t"], p["kb"] * p["e"])
        qd_ref[...] = q * p["e"]
        kt_ref[...] = k * p["ekt"]
        a_ref[...] = _fold_blocks(jnp.where(p["tril"], p["qk"] * p["decay"], 0.0))
        t_ref[...] = _fold_blocks(p["t"])

    row = lambda off: pl.BlockSpec((tg, HEAD), functools.partial(lambda h, m, off: (m, h + off), off=off))
    full = pl.BlockSpec((tg, LANES), lambda h, m: (m, 0))
    o_spec = pl.BlockSpec((tg, HEAD), lambda h, m: (m, h))
    a_spec = pl.BlockSpec((None, tg, CHUNK), lambda h, m: (h, m, 0))
    wide = jax.ShapeDtypeStruct((s_dim, N_HEADS * HEAD), F32)
    folded = jax.ShapeDtypeStruct((N_HEADS, s_dim, CHUNK), F32)
    return pl.pallas_call(
        body,
        out_shape=[wide, wide, wide, wide, folded, folded],
        grid=(N_HEADS, s_dim // tg),
        in_specs=[row(0), row(N_HEADS), row(2 * N_HEADS), full, full, pl.BlockSpec((8, tg), lambda h, m: (0, m))],
        out_specs=[o_spec, o_spec, o_spec, o_spec, a_spec, a_spec],
        compiler_params=pltpu.CompilerParams(dimension_semantics=("parallel", "parallel")),
        name="gdr_prep_fwd",
    )(qkvn, qkvn, qkvn, beta, gc, gc_t)


def _gdr_prep_bwd(qkvn, beta, gc, gc_t, t_fold, du, dw, dqd, dkt, d_a):
    s_dim = qkvn.shape[0]
    tg = min(GROUP, s_dim)

    def body(q_ref, k_ref, v_ref, b_ref, g_ref, gt_ref, t_ref, du_ref, dw_ref, dqd_ref, dkt_ref, da_ref,
             dq_ref, dk_ref, dv_ref, db_ref, dg_ref):
        h = pl.program_id(1)
        q, k, v = q_ref[...], k_ref[...], v_ref[...]
        bcol, gcol, grow = _head_cols(b_ref[...], g_ref[...], gt_ref[...], h)
        p = _prep_common(q, k, bcol, gcol, grow, t_ref[...])
        t_mat, decay, e, ekt, kb = p["t"], p["decay"], p["e"], p["ekt"], p["kb"]
        du_, dw_, dqd_, dkt_ = du_ref[...], dw_ref[...], dqd_ref[...], dkt_ref[...]
        vb = v * bcol
        kbe = kb * e
        d_t = _dot(du_, vb, NT) + _dot(dw_, kbe, NT)
        dvb = _dot(t_mat, du_, TN)
        dkbe = _dot(t_mat, dw_, TN)
        ts = _split_bf16(t_mat)
        d_l = -_dot3(_dot3(ts, d_t, TN), ts, NT)
        m1 = jnp.where(p["strict"], d_l, 0.0)
        m2 = _unfold_blocks(da_ref[...], p["tril"])
        d_kk = m1 * decay
        d_qk = m2 * decay
        d_decay = m1 * p["kk"] + m2 * p["qk"]
        dkb = _dot(d_kk, k) + dkbe * e
        dk = _dot(d_kk, kb, TN) + _dot(d_qk, q, TN) + dkt_ * ekt + dkb * bcol
        dq = _dot(d_qk, k) + dqd_ * e
        d_beta = _rowsum(dkb * k) + _rowsum(dvb * v)
        d_e = _rowsum(dkbe * kb) + _rowsum(dqd_ * q)
        d_ekt = _rowsum(dkt_ * k) * ekt
        d_diff = d_decay * decay
        d_grow = -_colsum(d_diff) + _colsum(jnp.where(p["last"], jnp.broadcast_to(d_ekt, (tg, tg)), 0.0))
        d_gcol = d_e * e - d_ekt + _rowsum(d_diff)
        d_gcol = d_gcol + _rowsum(jnp.where(p["eye"], jnp.broadcast_to(d_grow, (tg, tg)), 0.0))
        dq_ref[...] = dq
        dk_ref[...] = dk
        dv_ref[...] = dvb * bcol

        @pl.when(h == 0)
        def _():
            db_ref[...] = jnp.zeros_like(db_ref)
            dg_ref[...] = jnp.zeros_like(dg_ref)

        lane = lax.broadcasted_iota(jnp.int32, (tg, LANES), 1)
        db_ref[...] = jnp.where(lane == h, d_beta, db_ref[...])
        dg_ref[...] = jnp.where(lane == h, d_gcol, dg_ref[...])

    row = lambda off: pl.BlockSpec((tg, HEAD), functools.partial(lambda m, h, off: (m, h + off), off=off))
    full = pl.BlockSpec((tg, LANES), lambda m, h: (m, 0))
    o_spec = pl.BlockSpec((tg, HEAD), lambda m, h: (m, h))
    a_spec = pl.BlockSpec((None, tg, CHUNK), lambda m, h: (h, m, 0))
    wide = jax.ShapeDtypeStruct((s_dim, N_HEADS * HEAD), F32)
    lanes = jax.ShapeDtypeStruct((s_dim, LANES), F32)
    return pl.pallas_call(
        body,
        out_shape=[wide, wide, wide, lanes, lanes],
        grid=(s_dim // tg, N_HEADS),
        in_specs=[row(0), row(N_HEADS), row(2 * N_HEADS), full, full, pl.BlockSpec((8, tg), lambda m, h: (0, m)),
                  a_spec, o_spec, o_spec, o_spec, o_spec, a_spec],
        out_specs=[o_spec, o_spec, o_spec, full, full],
        compiler_params=pltpu.CompilerParams(dimension_semantics=("parallel", "arbitrary")),
        name="gdr_prep_bwd",
    )(qkvn, qkvn, qkvn, beta, gc, gc_t, t_fold, du, dw, dqd, dkt, d_a)


def _gdr_scan_fwd(u, w, qd, kt, a_mat, gc):
    s_dim = u.shape[0]
    n_chunks = s_dim // CHUNK

    def body(u_ref, w_ref, qd_ref, kt_ref, a_ref, g_ref, o_ref, st_ref, state):
        @pl.when(pl.program_id(0) == 0)
        def _():
            state[...] = jnp.zeros_like(state)

        egl = jnp.exp(g_ref[CHUNK - 1:CHUNK, :])
        for h in range(N_HEADS):
            cs = slice(h * HEAD, (h + 1) * HEAD)
            s_h = state[h]
            st_ref[h] = s_h
            vn = u_ref[:, cs] - _dot(w_ref[:, cs], s_h)
            o_ref[:, cs] = _dot(qd_ref[:, cs], s_h) + _dot(a_ref[h], vn)
            state[h] = s_h * egl[:, h:h + 1] + _dot(kt_ref[:, cs], vn, TN)

    wide = pl.BlockSpec((CHUNK, N_HEADS * HEAD), lambda n: (n, 0))
    return pl.pallas_call(
        body,
        out_shape=[jax.ShapeDtypeStruct((s_dim, N_HEADS * HEAD), F32),
                   jax.ShapeDtypeStruct((n_chunks, N_HEADS, HEAD, HEAD), F32)],
        grid=(n_chunks,),
        in_specs=[wide, wide, wide, wide, pl.BlockSpec((N_HEADS, CHUNK, CHUNK), lambda n: (0, n, 0)),
                  pl.BlockSpec((CHUNK, LANES), lambda n: (n, 0))],
        out_specs=[wide, pl.BlockSpec((None, N_HEADS, HEAD, HEAD), lambda n: (n, 0, 0, 0))],
        scratch_shapes=[pltpu.VMEM((N_HEADS, HEAD, HEAD), F32)],
        compiler_params=pltpu.CompilerParams(dimension_semantics=("arbitrary",)),
        name="gdr_scan_fwd",
    )(u, w, qd, kt, a_mat, gc)


def _gdr_scan_bwd(u, w, qd, kt, a_mat, gc, states, d_o):
    s_dim = u.shape[0]
    n_chunks = s_dim // CHUNK
    last = n_chunks - 1

    def body(u_ref, w_ref, qd_ref, kt_ref, a_ref, g_ref, st_ref, do_ref,
             du_ref, dw_ref, dqd_ref, dkt_ref, da_ref, de_ref, d_state):
        @pl.when(pl.program_id(0) == 0)
        def _():
            d_state[...] = jnp.zeros_like(d_state)

        egl = jnp.exp(g_ref[CHUNK - 1:CHUNK, :])
        for h in range(N_HEADS):
            cs = slice(h * HEAD, (h + 1) * HEAD)
            s_h = st_ref[h]
            ds_n = d_state[h]
            do = do_ref[:, cs]
            w_h = w_ref[:, cs]
            vn = u_ref[:, cs] - _dot(w_h, s_h)
            dvn = _dot(a_ref[h], do, TN) + _dot(kt_ref[:, cs], ds_n)
            dqd_ref[:, cs] = _dot(do, s_h, NT)
            da_ref[h] = _dot(do, vn, NT)
            dkt_ref[:, cs] = _dot(vn, ds_n, NT)
            de = jnp.sum(_rowsum(ds_n * s_h), axis=0, keepdims=True)
            de_ref[h:h + 1, :] = jnp.broadcast_to(de, (1, LANES))
            du_ref[:, cs] = dvn
            dw_ref[:, cs] = -_dot(dvn, s_h, NT)
            d_state[h] = ds_n * egl[:, h:h + 1] + _dot(qd_ref[:, cs], do, TN) - _dot(w_h, dvn, TN)

    wide = pl.BlockSpec((CHUNK, N_HEADS * HEAD), lambda n: (last - n, 0))
    a_spec = pl.BlockSpec((N_HEADS, CHUNK, CHUNK), lambda n: (0, last - n, 0))
    wide_shape = jax.ShapeDtypeStruct((s_dim, N_HEADS * HEAD), F32)
    return pl.pallas_call(
        body,
        out_shape=[wide_shape, wide_shape, wide_shape, wide_shape,
                   jax.ShapeDtypeStruct((N_HEADS, s_dim, CHUNK), F32),
                   jax.ShapeDtypeStruct((n_chunks, N_HEADS, LANES), F32)],
        grid=(n_chunks,),
        in_specs=[wide, wide, wide, wide, a_spec, pl.BlockSpec((CHUNK, LANES), lambda n: (last - n, 0)),
                  pl.BlockSpec((None, N_HEADS, HEAD, HEAD), lambda n: (last - n, 0, 0, 0)), wide],
        out_specs=[wide, wide, wide, wide, a_spec, pl.BlockSpec((None, N_HEADS, LANES), lambda n: (last - n, 0, 0))],
        scratch_shapes=[pltpu.VMEM((N_HEADS, HEAD, HEAD), F32)],
        compiler_params=pltpu.CompilerParams(dimension_semantics=("arbitrary",)),
        name="gdr_scan_bwd",
    )(u, w, qd, kt, a_mat, gc, states, d_o)


def _gdr_out_fwd(o_dn, proj_a, dn_w):
    def fn(r, c):
        o, z = r
        (w_,) = c
        outs = []
        for h in range(N_HEADS):
            cs = slice(h * HEAD, (h + 1) * HEAD)
            oh, zh = o[:, cs], z[:, cs]
            rr = lax.rsqrt(_rowmean(oh * oh) + EPS_RMS)
            outs.append(oh * rr * w_ * (zh * _sig(zh)))
        return [jnp.concatenate(outs, axis=1)], []

    return _rowwise(fn, [o_dn, (proj_a, 3, D_MODEL)], [dn_w], [(D_MODEL, BF16)], name="gdr_out_fwd")[0]


def _gdr_out_bwd(o_dn, proj_a, d_og, dn_w):
    def fn(r, c):
        o, z, dg = r
        (w_,) = c
        d_o, d_z = [], []
        d_w = jnp.zeros((1, HEAD), F32)
        for h in range(N_HEADS):
            cs = slice(h * HEAD, (h + 1) * HEAD)
            oh, zh, dgh = o[:, cs], z[:, cs], dg[:, cs]
            rr = lax.rsqrt(_rowmean(oh * oh) + EPS_RMS)
            sz = zh * _sig(zh)
            d_n = dgh * sz
            d_z.append(dgh * (oh * rr * w_) * _silu_grad(zh))
            d_w = d_w + _colsum(d_n * oh * rr)
            gw = d_n * w_
            d_o.append(rr * gw - oh * (rr * rr * rr) * _rowmean(gw * oh))
        return [jnp.concatenate(d_o, axis=1), jnp.concatenate(d_z, axis=1)], [d_w]

    return _rowwise(fn, [o_dn, (proj_a, 3, D_MODEL), d_og], [dn_w], [(D_MODEL, F32), (D_MODEL, BF16)],
                    accs=[(1, HEAD)], name="gdr_out_bwd")


def _rms_fwd(x, w):
    r = lax.rsqrt(_rowmean(x * x) + EPS_RMS)
    return x * r * w


def _rms_bwd(x, w, dy):
    r = lax.rsqrt(_rowmean(x * x) + EPS_RMS)
    gw = dy * w
    return r * gw - x * (r * r * r) * _rowmean(gw * x), _colsum(dy * x * r)


def _mla_norm_fwd(proj_b, qn_w, kvn_w):
    def fn(r, c):
        return [_rms_fwd(r[0], c[0]), _rms_fwd(r[1], c[1])], []

    return _rowwise(fn, [(proj_b, WB_CQ // Q_LORA, Q_LORA), (proj_b, WB_CKV // KV_LORA, KV_LORA)], [qn_w, kvn_w],
                    [(Q_LORA, BF16), (KV_LORA, BF16)], name="mla_norm_fwd")


def _mla_norm_bwd(proj_b, qn_w, kvn_w, d_cq, d_ckv):
    def fn(r, c):
        dx1, dw1 = _rms_bwd(r[0], c[0], r[2])
        dx2, dw2 = _rms_bwd(r[1], c[1], r[3])
        return [dx1, dx2], [dw1, dw2]

    return _rowwise(fn, [(proj_b, WB_CQ // Q_LORA, Q_LORA), (proj_b, WB_CKV // KV_LORA, KV_LORA), d_cq, d_ckv],
                    [qn_w, kvn_w], [(Q_LORA, BF16), (KV_LORA, BF16)], accs=[(1, Q_LORA), (1, KV_LORA)],
                    name="mla_norm_bwd")


def _rope_consts():
    inv = ROPE_BASE ** (-np.arange(0, ROPE, 2, dtype=np.float32) / ROPE)
    t = np.zeros((4, LANES), np.float32)
    t[0, :32] = inv
    t[0, 32:64] = inv
    t[1, :64] = 1.0
    t[2, 32:64] = 1.0
    t[3, :32] = -1.0
    return jnp.asarray(t)


def _rope_tables(pos, consts, width):
    ang = pos * consts[0:1, :]
    cosv, sinv = jnp.cos(ang), jnp.sin(ang)
    reps = width // LANES
    tile = (lambda t: jnp.concatenate([t] * reps, axis=1)) if reps > 1 else (lambda t: t)
    return tile(cosv * consts[1:2, :]), tile(sinv * consts[2:3, :]), tile(sinv * consts[3:4, :])


def _rope_apply(t, tabs):
    cos_t, sin_a, sin_b = tabs
    width = t.shape[1]
    return t * cos_t + pltpu.roll(t, 32, 1) * sin_a + pltpu.roll(t, width - 32, 1) * sin_b


def _rope_transpose(d, tabs):
    cos_t, sin_a, sin_b = tabs
    width = d.shape[1]
    return d * cos_t + pltpu.roll(d * sin_a, width - 32, 1) + pltpu.roll(d * sin_b, 32, 1)


QK_HEAD = 2 * HEAD


def _interleave_heads(a, b):
    parts = []
    for h in range(N_HEADS):
        parts.append(a[:, h * HEAD:(h + 1) * HEAD])
        parts.append(b if b.shape[1] == LANES else b[:, h * LANES:(h + 1) * LANES])
    return jnp.concatenate(parts, axis=1)


def _mla_qk_fwd(q_full, k_nope, proj_b, pos):
    consts = _rope_consts()

    def fn(r, c):
        qf, kn, kr, pos_ = r
        qn, qr = qf[:, :D_MODEL], qf[:, D_MODEL:]
        qr = _rope_apply(qr, _rope_tables(pos_, c[0], D_MODEL))
        kr = _rope_apply(kr, _rope_tables(pos_, c[0], LANES))
        return [_interleave_heads(qn, qr) * SCALE, _interleave_heads(kn, kr)], []

    return _rowwise(fn, [q_full, k_nope, (proj_b, WB_KR // LANES, LANES), pos], [consts],
                    [(N_HEADS * QK_HEAD, BF16), (N_HEADS * QK_HEAD, BF16)], name="mla_qk_fwd")


def _mla_qk_bwd(d_qc, d_kc, pos):
    consts = _rope_consts()

    def fn(r, c):
        dq, dk, pos_ = r
        even = lambda t: jnp.concatenate([t[:, (2 * h) * LANES:(2 * h + 1) * LANES] for h in range(N_HEADS)], axis=1)
        odd = lambda t: jnp.concatenate([t[:, (2 * h + 1) * LANES:(2 * h + 2) * LANES] for h in range(N_HEADS)], axis=1)
        d_qr_raw = _rope_transpose(odd(dq), _rope_tables(pos_, c[0], D_MODEL)) * SCALE
        dkr = dk[:, LANES:2 * LANES]
        for h in range(1, N_HEADS):
            dkr = dkr + dk[:, (2 * h + 1) * LANES:(2 * h + 2) * LANES]
        return [jnp.concatenate([even(dq) * SCALE, d_qr_raw], axis=1), even(dk),
                _rope_transpose(dkr, _rope_tables(pos_, c[0], LANES))], []

    return _rowwise(fn, [d_qc, d_kc, pos], [consts], [(2 * D_MODEL, BF16), (D_MODEL, BF16), (LANES, BF16)],
                    name="mla_qk_bwd")


def _causal_mask_t(st, key0, query0):
    key = lax.broadcasted_iota(jnp.int32, st.shape, 0) + key0
    query = lax.broadcasted_iota(jnp.int32, st.shape, 1) + query0
    return jnp.where(key <= query, st, NEG_BIG)


def _attn_tiles(s_dim):
    tq = min(512, s_dim)
    n_chains = 2 if s_dim >= 2 * tq else 1
    return tq, n_chains, min(512, s_dim)


def _attn_fwd(qc, kc, vt):
    s_dim = qc.shape[0]
    tq, n_chains, tk = _attn_tiles(s_dim)
    tqs = tq * n_chains

    def body(q_ref, k_ref, vt_ref, o_ref, lse_ref, m_s, l_s, acc):
        qi = pl.program_id(1)
        m_s[...] = jnp.full_like(m_s, NEG_BIG)
        l_s[...] = jnp.zeros_like(l_s)
        acc[...] = jnp.zeros_like(acc)

        def step(j, carry):
            ks = pl.multiple_of(j * tk, tk)
            kb, vtb = k_ref[pl.ds(ks, tk), :], vt_ref[:, pl.ds(ks, tk)]
            for c in range(n_chains):
                cols = slice(c * tq, (c + 1) * tq)
                st = _causal_mask_t(_dot(kb, q_ref[cols, :], NT), j * tk, qi * tqs + c * tq)
                m_prev = m_s[:, cols]
                m_new = jnp.maximum(m_prev, jnp.max(st, axis=0, keepdims=True))
                alpha = jnp.exp(m_prev - m_new)
                pt = jnp.exp(st - m_new)
                l_s[:, cols] = alpha * l_s[:, cols] + _colsum(pt)
                m_s[:, cols] = m_new
                acc[:, cols] = acc[:, cols] * alpha + _dot(vtb, pt)
            return carry

        lax.fori_loop(0, (qi + 1) * (tqs // tk), step, 0)
        l = l_s[...]
        o_ref[...] = jnp.transpose(acc[...] / l)
        lse_ref[...] = m_s[...] + jnp.log(l)

    return pl.pallas_call(
        body,
        out_shape=[jax.ShapeDtypeStruct((s_dim, N_HEADS * HEAD), F32), jax.ShapeDtypeStruct((N_HEADS, 1, s_dim), F32)],
        grid=(N_HEADS, s_dim // tqs),
        in_specs=[pl.BlockSpec((tqs, QK_HEAD), lambda h, qi: (qi, h)),
                  pl.BlockSpec((s_dim, QK_HEAD), lambda h, qi: (0, h)),
                  pl.BlockSpec((HEAD, s_dim), lambda h, qi: (h, 0))],
        out_specs=[pl.BlockSpec((tqs, HEAD), lambda h, qi: (qi, h)),
                   pl.BlockSpec((None, 1, tqs), lambda h, qi: (h, 0, qi))],
        scratch_shapes=[pltpu.VMEM((1, tqs), F32), pltpu.VMEM((1, tqs), F32), pltpu.VMEM((HEAD, tqs), F32)],
        compiler_params=pltpu.CompilerParams(dimension_semantics=("parallel", "parallel")),
        name="attn_fwd",
    )(qc, kc, vt)


def _attn_bwd(qc, kc, kct, v, o, d_o, lse):
    s_dim = qc.shape[0]
    tq, n_chains, tk = _attn_tiles(s_dim)
    tqs = tq * n_chains

    def body(q_ref, k_ref, kt_ref, v_ref, o_ref, do_ref, lse_ref, dq_ref, dk_ref, dv_ref, dqt_acc, dv_acc):
        qi = pl.program_id(1)

        @pl.when(qi == 0)
        def _():
            dk_ref[...] = jnp.zeros_like(dk_ref)
            dv_acc[...] = jnp.zeros_like(dv_acc)

        dqt_acc[...] = jnp.zeros_like(dqt_acc)
        do_f = do_ref[...]
        do_all = do_f.astype(BF16)
        q_all = q_ref[...]
        lse_row = lse_ref[...]
        delta_row = _dot3(jnp.ones((8, HEAD), F32), o_ref[...] * do_f, NT)[0:1, :]

        def step(j, carry):
            ks = pl.multiple_of(j * tk, tk)
            kb, vb, ktb = k_ref[pl.ds(ks, tk), :], v_ref[pl.ds(ks, tk), :], kt_ref[:, pl.ds(ks, tk)]
            pts, dsts = [], []
            for c in range(n_chains):
                cols = slice(c * tq, (c + 1) * tq)
                st = _causal_mask_t(_dot(kb, q_all[cols, :], NT), j * tk, qi * tqs + c * tq)
                pt = jnp.exp(st - lse_row[:, cols])
                dst = pt * (_dot(vb, do_all[cols, :], NT) - delta_row[:, cols])
                dst_b = dst.astype(BF16)
                dqt_acc[:, cols] += _dot(ktb, dst_b)
                pts.append(pt.astype(BF16))
                dsts.append(dst_b)
            pt_all = jnp.concatenate(pts, axis=1) if n_chains > 1 else pts[0]
            dst_all = jnp.concatenate(dsts, axis=1) if n_chains > 1 else dsts[0]
            dk_ref[pl.ds(ks, tk), :] += _dot(dst_all, q_all)
            dv_acc[pl.ds(ks, tk), :] += _dot(pt_all, do_all)
            return carry

        lax.fori_loop(0, (qi + 1) * (tqs // tk), step, 0)
        dq_ref[...] = jnp.transpose(dqt_acc[...])

        @pl.when(qi == s_dim // tqs - 1)
        def _():
            dv_ref[...] = dv_acc[...].astype(dv_ref.dtype)

    q_spec = pl.BlockSpec((tqs, QK_HEAD), lambda h, qi: (qi, h))
    o_spec = pl.BlockSpec((tqs, HEAD), lambda h, qi: (qi, h))
    k_spec = pl.BlockSpec((s_dim, QK_HEAD), lambda h, qi: (0, h))
    v_spec = pl.BlockSpec((s_dim, HEAD), lambda h, qi: (0, h))
    wide2 = jax.ShapeDtypeStruct((s_dim, N_HEADS * QK_HEAD), F32)
    return pl.pallas_call(
        body,
        out_shape=[wide2, wide2, jax.ShapeDtypeStruct((s_dim, N_HEADS * HEAD), BF16)],
        grid=(N_HEADS, s_dim // tqs),
        in_specs=[q_spec, k_spec, pl.BlockSpec((QK_HEAD, s_dim), lambda h, qi: (h, 0)), v_spec, o_spec, o_spec,
                  pl.BlockSpec((None, 1, tqs), lambda h, qi: (h, 0, qi))],
        out_specs=[q_spec, k_spec, v_spec],
        scratch_shapes=[pltpu.VMEM((QK_HEAD, tqs), F32), pltpu.VMEM((s_dim, HEAD), F32)],
        compiler_params=pltpu.CompilerParams(dimension_semantics=("parallel", "arbitrary")),
        name="attn_bwd",
    )(qc, kc, kct, v, o, d_o, lse)


def _merge_fwd(y_dn, y_mla, proj_g):
    def fn(r, c):
        yd, ym, g = r
        return [_sig(g[:, :D_MODEL]) * yd + _sig(g[:, D_MODEL:]) * ym], []

    return _rowwise(fn, [y_dn, y_mla, proj_g], [], [(D_MODEL, BF16)], name="merge_fwd")[0]


def _merge_bwd(y_dn, y_mla, proj_g, d_mixed):
    def fn(r, c):
        yd, ym, g, dm = r
        sd, sm = _sig(g[:, :D_MODEL]), _sig(g[:, D_MODEL:])
        d_g = jnp.concatenate([dm * yd * sd * (1.0 - sd), dm * ym * sm * (1.0 - sm)], axis=1)
        return [d_g, dm * sd, dm * sm], []

    return _rowwise(fn, [y_dn, y_mla, proj_g, d_mixed], [], [(2 * D_MODEL, BF16), (D_MODEL, BF16), (D_MODEL, BF16)],
                    name="merge_bwd")


def _ln_stats(z):
    mu = _rowmean(z)
    zc = z - mu
    r = lax.rsqrt(_rowmean(zc * zc) + EPS_LN)
    return zc * r, r


def _ln_bwd(dy, xh, r, g):
    dxh = dy * g
    return r * (dxh - _rowmean(dxh) - xh * _rowmean(dxh * xh))


def _ln1_fwd(x, a1, g, b):
    def fn(r, c):
        xh, _ = _ln_stats(ALPHA * r[0] + r[1])
        y = xh * c[0] + c[1]
        return [y, y], []

    return _rowwise(fn, [x, a1], [g, b], [(D_MODEL, F32), (D_MODEL, BF16)], name="ln1_fwd")


def _ln1_bwd(x, a1, d_h1, g):
    def fn(r, c):
        xh, rr = _ln_stats(ALPHA * r[0] + r[1])
        dy = r[2]
        dz = _ln_bwd(dy, xh, rr, c[0])
        return [dz, ALPHA * dz], [_colsum(dy * xh), _colsum(dy)]

    return _rowwise(fn, [x, a1, d_h1], [g], [(D_MODEL, BF16), (D_MODEL, F32)], accs=[(1, D_MODEL), (1, D_MODEL)],
                    name="ln1_bwd")


def _act_fwd(gu):
    def fn(r, c):
        gt, up = r[0][:, :FFN_HIDDEN], r[0][:, FFN_HIDDEN:]
        return [gt * _sig(gt) * up], []

    return _rowwise(fn, [gu], [], [(FFN_HIDDEN, BF16)], name="act_fwd")[0]


def _act_bwd(gu, d_act):
    def fn(r, c):
        gt, up = r[0][:, :FFN_HIDDEN], r[0][:, FFN_HIDDEN:]
        da = r[1]
        return [jnp.concatenate([da * up * _silu_grad(gt), da * gt * _sig(gt)], axis=1)], []

    return _rowwise(fn, [gu, d_act], [], [(2 * FFN_HIDDEN, BF16)], name="act_bwd")[0]


def _tail(h1, ffn, pg, pp, tgt, g, b):
    def fn(r, c):
        h1_, ffn_, pg_, pp_, t_ = r
        sp = _sig(pg_)
        xh, rr = _ln_stats(ALPHA * h1_ + ffn_ + sp * pp_)
        y = xh * c[0] + c[1]
        err = y - t_
        dy = err * (1.0 / D_MODEL)
        dz = _ln_bwd(dy, xh, rr, c[0])
        loss = jnp.sum(0.5 * _rowmean(err * err), axis=0, keepdims=True)
        return ([dz, dz * pp_ * sp * (1.0 - sp), dz * sp, ALPHA * dz],
                [_colsum(dy * xh), _colsum(dy), jnp.broadcast_to(loss, (1, LANES))])

    return _rowwise(fn, [h1, ffn, pg, pp, tgt], [g, b], [(D_MODEL, BF16)] * 3 + [(D_MODEL, F32)],
                    accs=[(1, D_MODEL), (1, D_MODEL), (1, LANES)], name="tail")


def _pad_heads_to_lanes(per_head_lane0, s_dim):
    t = jnp.transpose(per_head_lane0[:, :, 0])
    return jnp.pad(t, ((0, 0), (0, LANES - N_HEADS)))


def _local_step(x, p, pos, tgt, w):
    s_dim = x.shape[0]
    xb, pb = x.astype(BF16), p.astype(BF16)
    proj_a = _mm(xb, w["wa"], name="f_proj_a")
    proj_g = _mm(xb, w["wg"], name="f_proj_g")
    proj_b = _mm(xb, w["wb"], name="f_proj_b")
    qkvn = _conv_fwd(proj_a, w["conv"])
    beta, gc = _gates_fwd(proj_b, w["alog"], w["dtb"])
    gc_t = jnp.transpose(gc[:, :N_HEADS])
    u, w_, qd, kt, a_mat, t_fold = _gdr_prep_fwd(qkvn, beta, gc, gc_t)
    o_dn, states = _gdr_scan_fwd(u, w_, qd, kt, a_mat, gc)
    og = _gdr_out_fwd(o_dn, proj_a, w["dnw"])
    y_dn = _mm(og, w["br_dn"], name="f_y_dn")
    c_q, c_kv = _mla_norm_fwd(proj_b, w["qnw"], w["kvnw"])
    q_full = _mm(c_q, w["uq"], name="f_q_full")
    k_nope = _mm(c_kv, w["uk"], name="f_k_nope")
    vv = _mm(c_kv, w["uv"], out_dtype=BF16, name="f_v")
    qc, kc = _mla_qk_fwd(q_full, k_nope, proj_b, pos)
    o_mla, lse = _attn_fwd(qc, kc, jnp.transpose(vv))
    y_mla = _mm(o_mla, w["br_mla"], name="f_y_mla")
    mixed = _merge_fwd(y_dn, y_mla, proj_g)
    a1 = _mm(mixed, w["wo"], name="f_a1")
    h1, h1b = _ln1_fwd(x, a1, w["ln1g"], w["ln1b"])
    gu = _mm(h1b, w["ffn_in"], name="f_gu")
    act = _act_fwd(gu)
    ffn = _mm(act, w["ffn_out"], name="f_ffn")
    pg = _mm(h1b, w["ple_gate"], name="f_pg")
    pp = _mm(pb, w["ple"], name="f_pp")
    g = {}
    dz2, d_pg, d_pp, dh1a, g["ln2g"], g["ln2b"], loss = _tail(h1, ffn, pg, pp, tgt, w["ln2g"], w["ln2b"])
    g["ple_t"] = _mm(d_pp, pb, ta=True, name="b_w_ple")
    g["ple_gate"] = _mm(h1b, d_pg, ta=True, name="b_w_ple_gate")
    g["ffn_out"] = _mm(act, dz2, ta=True, name="b_w_ffn_out")
    d_act = _mm(dz2, w["ffn_out"], tb=True, name="b_act")
    d_gu = _act_bwd(gu, d_act)
    g["ffn_in_t"] = _mm(d_gu, h1b, ta=True, name="b_w_ffn_in")
    d_h1 = _mm(d_gu, w["ffn_in_t"], add=(dh1a,), name="b_h1_ffn")
    d_h1 = _mm(d_pg, w["ple_gate"], tb=True, add=(d_h1,), name="b_h1_ple")
    dz1, dxa, g["ln1g"], g["ln1b"] = _ln1_bwd(x, a1, d_h1, w["ln1g"])
    g["wo"] = _mm(mixed, dz1, ta=True, name="b_w_o")
    d_mixed = _mm(dz1, w["wo"], tb=True, name="b_mixed")
    d_proj_g, d_y_dn, d_y_mla = _merge_bwd(y_dn, y_mla, proj_g, d_mixed)
    g["br_mla"] = _mm(o_mla, d_y_mla, ta=True, name="b_w_br_mla")
    d_o_mla = _mm(d_y_mla, w["br_mla"], tb=True, name="b_o_mla")
    d_qc, d_kc, d_v = _attn_bwd(qc, kc, jnp.transpose(kc), vv, o_mla, d_o_mla, lse)
    d_q_full, d_kn, d_kr = _mla_qk_bwd(d_qc, d_kc, pos)
    g["uq"] = _mm(c_q, d_q_full, ta=True, name="b_w_uq")
    d_c_q = _mm(d_q_full, w["uq"], tb=True, name="b_c_q")
    g["uk"] = _mm(c_kv, d_kn, ta=True, name="b_w_uk")
    g["uv"] = _mm(c_kv, d_v, ta=True, name="b_w_uv")
    d_c_kv = _mm(d_kn, w["uk"], tb=True, name="b_c_kv_k")
    d_c_kv = _mm(d_v, w["uv"], tb=True, add=(d_c_kv,), name="b_c_kv_v")
    d_cq, d_ckv, g["qnw"], g["kvnw"] = _mla_norm_bwd(proj_b, w["qnw"], w["kvnw"], d_c_q, d_c_kv)
    g["br_dn"] = _mm(og, d_y_dn, ta=True, name="b_w_br_dn")
    d_og = _mm(d_y_dn, w["br_dn"], tb=True, name="b_og")
    d_o_dn, d_z, g["dnw"] = _gdr_out_bwd(o_dn, proj_a, d_og, w["dnw"])
    du, dw, dqd, dkt, d_a, d_egl = _gdr_scan_bwd(u, w_, qd, kt, a_mat, gc, states, d_o_dn)
    dq, dk, dv, d_beta, d_gc = _gdr_prep_bwd(qkvn, beta, gc, gc_t, t_fold, du, dw, dqd, dkt, d_a)
    d_egl_rows = jnp.pad(d_egl[:, None, :, 0], ((0, 0), (CHUNK - 1, 0), (0, LANES - N_HEADS))).reshape(s_dim, LANES)
    d_ba, g["alog"], g["dtb"] = _gates_bwd(proj_b, w["alog"], w["dtb"], gc, d_beta, d_gc, d_egl_rows)
    d_qkv, g["conv"] = _conv_bwd(proj_a, w["conv"], dq, dk, dv)
    zeros = jnp.zeros((s_dim, WB_CKV - Q_LORA), BF16)
    d_proj_b = jnp.concatenate([d_cq, zeros, d_ckv, d_kr, d_ba], axis=1)
    g["wa_qkv_t"] = _mm(d_qkv, xb, ta=True, name="b_w_qkv")
    g["wa_z_t"] = _mm(d_z, xb, ta=True, name="b_w_z")
    g["wg_t"] = _mm(d_proj_g, xb, ta=True, name="b_w_g")
    g["wb_t"] = _mm(d_proj_b, xb, ta=True, name="b_w_b")
    dx = _mm(d_qkv, w["wa_qkv_t"], add=(dxa,), name="b_x_qkv")
    dx = _mm(d_z, w["wa_z_t"], add=(dx,), name="b_x_z")
    dx = _mm(d_proj_g, w["wg_t"], add=(dx,), name="b_x_g")
    dx = _mm(d_proj_b, w["wb_t"], add=(dx,), name="b_x_b")
    return loss, dx, g


_BIG = (("w_in", 1), ("conv_w", 1), ("w_uq", 0), ("w_uk", 0), ("w_uv", 0), ("w_br_dn", 0), ("w_br_mla", 0),
        ("w_o", 0), ("w_ffn_in", 1), ("w_ffn_out", 0), ("w_ple", 1), ("w_ple_gate", 0))
_BIG_AXIS = dict(_BIG)
_SMALL = ("ln1_g", "ln1_b", "ln2_g", "ln2_b", "q_norm_w", "kv_norm_w", "dn_norm_w", "dn_a_log", "dn_dt_bias")
_ORDER = ("w_in", "conv_w", "dn_a_log", "dn_dt_bias", "dn_norm_w", "q_norm_w", "w_uq", "kv_norm_w", "w_uk", "w_uv",
          "w_br_dn", "w_br_mla", "w_o", "ln1_g", "ln1_b", "w_ffn_in", "w_ffn_out", "w_ple", "w_ple_gate", "ln2_g",
          "ln2_b")


ROW_ALIGN = 16


def _flat_rows(shape):
    rows = -(-int(np.prod(shape)) // FLAT_COLS)
    return -(-rows // ROW_ALIGN) * ROW_ALIGN


def _stored_shape(name, shard_shape):
    axis = _BIG_AXIS[name]
    lead = shard_shape[axis]
    return lead, int(np.prod(shard_shape)) // lead


def _to_stored(name, shard):
    return jnp.moveaxis(shard, _BIG_AXIS[name], 0).reshape(_stored_shape(name, shard.shape))


def _from_stored(name, stored, shard_shape):
    axis = _BIG_AXIS[name]
    moved = (shard_shape[axis],) + shard_shape[:axis] + shard_shape[axis + 1:]
    return jnp.moveaxis(stored.reshape(moved), 0, axis)


def _flat_layout(shard_shapes):
    rows, off = {}, 0
    for name, _ in _BIG:
        n = _flat_rows(shard_shapes[name])
        rows[name] = (off, n)
        off += n
    total = -(-off // FLAT_TILE) * FLAT_TILE
    return rows, total


def _to_rows(a, lead, n_rows):
    lead_shape = a.shape[:lead]
    if a.shape[-1] == FLAT_COLS:
        return jnp.pad(a, [(0, 0)] * lead + [(0, n_rows - a.shape[lead]), (0, 0)])
    flat = a.reshape(lead_shape + (-1,))
    flat = jnp.pad(flat, [(0, 0)] * lead + [(0, n_rows * FLAT_COLS - flat.shape[-1])])
    return flat.reshape(lead_shape + (n_rows, FLAT_COLS))


def _from_rows(rows, stored_shape):
    r, c = stored_shape
    lead_shape = rows.shape[:-2]
    if c == FLAT_COLS:
        return rows[..., :r, :]
    return rows.reshape(lead_shape + (-1,))[..., :r * c].reshape(lead_shape + (r, c))


def _pack_shards(shards, layout, total_rows):
    parts = [_to_rows(_to_stored(name, shards[name]), 0, layout[name][1]) for name, _ in _BIG]
    flat = jnp.concatenate(parts, axis=0)
    return jnp.pad(flat, ((0, total_rows - flat.shape[0]), (0, 0)))


def _unpack_shards(flat, shard_shapes, layout):
    out = {}
    for name, _ in _BIG:
        off, n = layout[name]
        stored = _from_rows(flat[off:off + n], _stored_shape(name, shard_shapes[name]))
        out[name] = _from_stored(name, stored, shard_shapes[name])
    return out


def _unpack_gathered(gathered, shard_shapes, layout):
    out = {}
    for name, _ in _BIG:
        off, n = layout[name]
        r, c = _stored_shape(name, shard_shapes[name])
        out[name] = _from_rows(gathered[:, off:off + n], (r, c)).reshape(N_DEV * r, c)
    return out


def _pack_grads_for_parity(stored_grads, shard_shapes, layout, total_rows, parity):
    parts = []
    for name, _ in _BIG:
        r, c = _stored_shape(name, shard_shapes[name])
        gsh = stored_grads[name].reshape(4, 2, r, c)
        gsh = lax.dynamic_index_in_dim(gsh, parity, axis=1, keepdims=False)
        parts.append(_to_rows(gsh, 1, layout[name][1]))
    flat = jnp.concatenate(parts, axis=1)
    return jnp.pad(flat, ((0, 0), (0, total_rows - flat.shape[1]), (0, 0)))


_W_IN_ROWS = np.cumsum([0, 3072, 1024, 8, 8, Q_LORA, KV_LORA, ROPE, D_MODEL, D_MODEL])


def _full_weights(fw, small):
    w_in_t = fw["w_in"]
    r = _W_IN_ROWS
    zr = lambda n: jnp.zeros((n, D_MODEL), w_in_t.dtype)
    w = {}
    w["wa_t"] = w_in_t[r[0]:r[2]]
    w["wa_qkv_t"], w["wa_z_t"] = w_in_t[r[0]:r[1]], w_in_t[r[1]:r[2]]
    w["wg_t"] = w_in_t[r[7]:r[9]]
    w["wb_t"] = jnp.concatenate([w_in_t[r[4]:r[5]], zr(WB_CKV - Q_LORA), w_in_t[r[5]:r[7]], zr(LANES - ROPE),
                                 w_in_t[r[2]:r[4]], zr(LANES - 2 * N_HEADS)], axis=0)
    uq = fw["w_uq"].reshape(Q_LORA, N_HEADS, HEAD + ROPE)
    uq_r = jnp.pad(uq[:, :, HEAD:], ((0, 0), (0, 0), (0, HEAD - ROPE)))
    w["uq"] = jnp.concatenate([uq[:, :, :HEAD].reshape(Q_LORA, -1), uq_r.reshape(Q_LORA, -1)], axis=1)
    w["uk"], w["uv"] = fw["w_uk"], fw["w_uv"]
    w["conv"] = jnp.transpose(fw["conv_w"]).astype(F32)
    w["br_dn"], w["br_mla"], w["wo"] = fw["w_br_dn"], fw["w_br_mla"], fw["w_o"]
    w["ffn_in_t"], w["ffn_out"] = fw["w_ffn_in"], fw["w_ffn_out"]
    w["ple_t"], w["ple_gate"] = fw["w_ple"], fw["w_ple_gate"]
    for k_ in ("wa", "wg", "wb", "ffn_in", "ple"):
        w[k_] = jnp.transpose(w[k_ + "_t"])
    pad_l = lambda v: jnp.pad(v, ((0, 0), (0, LANES - v.shape[1])))
    w["alog"], w["dtb"] = pad_l(small["dn_a_log"]), pad_l(small["dn_dt_bias"])
    w["dnw"], w["qnw"], w["kvnw"] = small["dn_norm_w"], small["q_norm_w"], small["kv_norm_w"]
    w["ln1g"], w["ln1b"], w["ln2g"], w["ln2b"] = small["ln1_g"], small["ln1_b"], small["ln2_g"], small["ln2_b"]
    return w


def _full_grads(g):
    wb = g["wb_t"]
    full = {}
    full["w_in"] = jnp.concatenate([
        g["wa_qkv_t"], g["wa_z_t"], wb[WB_BA:WB_BA + 2 * N_HEADS], wb[WB_CQ:WB_CQ + Q_LORA],
        wb[WB_CKV:WB_CKV + KV_LORA], wb[WB_KR:WB_KR + ROPE], g["wg_t"]], axis=0)
    uq = g["uq"]
    uq_n = uq[:, :D_MODEL].reshape(Q_LORA, N_HEADS, HEAD)
    uq_r = uq[:, D_MODEL:].reshape(Q_LORA, N_HEADS, HEAD)[:, :, :ROPE]
    full["w_uq"] = jnp.concatenate([uq_n, uq_r], axis=2).reshape(Q_LORA, -1)
    full["w_uk"], full["w_uv"] = g["uk"], g["uv"]
    full["conv_w"] = jnp.transpose(g["conv"])
    full["w_br_dn"], full["w_br_mla"], full["w_o"] = g["br_dn"], g["br_mla"], g["wo"]
    full["w_ffn_in"], full["w_ffn_out"] = g["ffn_in_t"], g["ffn_out"]
    full["w_ple"], full["w_ple_gate"] = g["ple_t"], g["ple_gate"]
    small = {"ln1_g": g["ln1g"], "ln1_b": g["ln1b"], "ln2_g": g["ln2g"], "ln2_b": g["ln2b"], "q_norm_w": g["qnw"],
             "kv_norm_w": g["kvnw"], "dn_norm_w": g["dnw"], "dn_a_log": g["alog"][:, :N_HEADS],
             "dn_dt_bias": g["dtb"][:, :N_HEADS]}
    return full, small


_SMALL_SLOTS = {"ln1_g": (0, 0, 1024), "ln1_b": (1, 0, 1024), "ln2_g": (2, 0, 1024), "ln2_b": (3, 0, 1024),
                "q_norm_w": (4, 0, 384), "kv_norm_w": (4, 384, 256), "dn_norm_w": (4, 640, 128),
                "dn_a_log": (4, 768, 8), "dn_dt_bias": (4, 776, 8)}
_LOSS_SLOT = (4, 896)


def _pack_small(vals, loss=None):
    blk = jnp.zeros((8, FLAT_COLS), F32)
    for name, (r, c, n) in _SMALL_SLOTS.items():
        blk = lax.dynamic_update_slice(blk, vals[name].reshape(1, n).astype(F32), (r, c))
    if loss is not None:
        blk = lax.dynamic_update_slice(blk, loss[:, :1], _LOSS_SLOT)
    return blk


def _unpack_small(blk, shapes):
    return {name: blk[r:r + 1, c:c + n].reshape(shapes[name]) for name, (r, c, n) in _SMALL_SLOTS.items()}


_MESH_ID = pl.DeviceIdType.MESH
_ANY = pl.BlockSpec(memory_space=pl.ANY)


def _all_gather(block, name):
    def body(x_ref, out_ref, send_sems, recv_sems, local_sem):
        x, y, c = lax.axis_index("x"), lax.axis_index("y"), lax.axis_index("c")
        me, sibling = (x, y, c), (x, y, 1 - c)
        chips = [(1 - x, y), (x, 1 - y), (1 - x, 1 - y)]

        def slot(px, py, pc):
            return out_ref.at[4 * px + 2 * py + pc]

        def copy(k, origin, to, src=None):
            return pltpu.make_async_remote_copy(
                src_ref=slot(*origin) if src is None else src, dst_ref=slot(*origin), send_sem=send_sems.at[k],
                recv_sem=recv_sems.at[k], device_id=to, device_id_type=_MESH_ID)

        mine = pltpu.make_async_copy(x_ref, slot(*me), local_sem)
        mine.start()
        first = [copy(0, me, sibling, src=x_ref)]
        first += [copy(1 + j, me, (*chip, c), src=x_ref) for j, chip in enumerate(chips)]
        for cp in first:
            cp.start()
        passed = [copy(4 + j, (*chip, c), sibling) for j, chip in enumerate(chips)]
        for j, chip in enumerate(chips):
            copy(1 + j, (*chip, c), me).wait_recv()
            passed[j].start()
        copy(0, sibling, me).wait_recv()
        for j, chip in enumerate(chips):
            copy(4 + j, (*chip, 1 - c), me).wait_recv()
        for cp in first + passed:
            cp.wait_send()
        mine.wait()

    return pl.pallas_call(
        body,
        out_shape=jax.ShapeDtypeStruct((N_DEV,) + block.shape, block.dtype),
        in_specs=[_ANY],
        out_specs=_ANY,
        scratch_shapes=[pltpu.SemaphoreType.DMA((7,)), pltpu.SemaphoreType.DMA((7,)), pltpu.SemaphoreType.DMA],
        name=name,
    )(block)


def _exchange_sibling(src, name):
    def body(src_ref, dst_ref, send_sems, recv_sems):
        x, y, c = lax.axis_index("x"), lax.axis_index("y"), lax.axis_index("c")
        copies = [pltpu.make_async_remote_copy(
            src_ref=src_ref.at[q], dst_ref=dst_ref.at[q], send_sem=send_sems.at[q], recv_sem=recv_sems.at[q],
            device_id=(x, y, 1 - c), device_id_type=_MESH_ID) for q in range(4)]
        for cp in copies:
            cp.start()
        for cp in copies:
            cp.wait_recv()
        for cp in copies:
            cp.wait_send()

    return pl.pallas_call(
        body,
        out_shape=jax.ShapeDtypeStruct(src.shape, src.dtype),
        in_specs=[_ANY],
        out_specs=_ANY,
        scratch_shapes=[pltpu.SemaphoreType.DMA((4,)), pltpu.SemaphoreType.DMA((4,))],
        name=name,
    )(src)


def _exchange_chips(src, name):
    def body(src_ref, dst_ref, send_sems, recv_sems):
        x, y, c = lax.axis_index("x"), lax.axis_index("y"), lax.axis_index("c")
        chips = [(1 - x, y), (x, 1 - y), (1 - x, 1 - y)]
        copies = [pltpu.make_async_remote_copy(
            src_ref=src_ref.at[2 * tx + ty], dst_ref=dst_ref.at[j], send_sem=send_sems.at[j],
            recv_sem=recv_sems.at[j], device_id=(tx, ty, c), device_id_type=_MESH_ID)
            for j, (tx, ty) in enumerate(chips)]
        for cp in copies:
            cp.start()
        for cp in copies:
            cp.wait_recv()
        for cp in copies:
            cp.wait_send()

    return pl.pallas_call(
        body,
        out_shape=jax.ShapeDtypeStruct((3,) + src.shape[1:], src.dtype),
        in_specs=[_ANY],
        out_specs=_ANY,
        scratch_shapes=[pltpu.SemaphoreType.DMA((3,)), pltpu.SemaphoreType.DMA((3,))],
        name=name,
    )(src)


def _add_pairs(a, b, name):
    n, r, c = a.shape

    def fn(rows, consts):
        s = rows[0] + rows[1]
        return [s, s], []

    out, out_bf = _rowwise(fn, [a.reshape(n * r, c), b.reshape(n * r, c)], [], [(c, F32), (c, BF16)], tm=FLAT_TILE,
                           name=name)
    return out.reshape(n, r, c), out_bf.reshape(n, r, c)


def _adamw_math(w, g, m, v):
    m = ADAM_B1 * m + (1.0 - ADAM_B1) * g
    v = ADAM_B2 * v + (1.0 - ADAM_B2) * (g * g)
    m_hat = m / (1.0 - ADAM_B1 ** ADAM_STEP)
    v_hat = v / (1.0 - ADAM_B2 ** ADAM_STEP)
    delta = -ADAM_LR * (m_hat / (jnp.sqrt(v_hat) + ADAM_EPS) + ADAM_WD * w)
    return delta, m, v


def _adamw_flat(w, m, v, g_parts):
    def fn(rows, consts):
        w_, m_, v_ = rows[:3]
        g = rows[3]
        for part in rows[4:]:
            g = g + part
        delta, m2, v2 = _adamw_math(w_, g, m_, v_)
        return [g, delta, m2, v2], []

    return _rowwise(fn, [w, m, v] + list(g_parts), [], [(FLAT_COLS, F32)] * 4, tm=FLAT_TILE, name="adamw_flat")


def _adamw_small(w, m, v, gathered):
    def body(w_ref, m_ref, v_ref, g_ref, go_ref, d_ref, m2_ref, v2_ref):
        g = g_ref[0]
        for k in range(1, N_DEV):
            g = g + g_ref[k]
        delta, m2, v2 = _adamw_math(w_ref[...], g, m_ref[...], v_ref[...])
        go_ref[...] = g
        d_ref[...] = delta
        m2_ref[...] = m2
        v2_ref[...] = v2

    blk = jax.ShapeDtypeStruct((8, FLAT_COLS), F32)
    return pl.pallas_call(body, out_shape=[blk] * 4, name="adamw_small")(w, m, v, gathered)


def kernel(x, p, positions, w_in, conv_w, dn_a_log, dn_dt_bias, dn_norm_w, q_norm_w, w_uq, kv_norm_w, w_uk, w_uv, w_br_dn, w_br_mla, w_o, ln1_g, ln1_b, w_ffn_in, w_ffn_out, w_ple, w_ple_gate, ln2_g, ln2_b, loss_target, m_w_in, m_conv_w, m_dn_a_log, m_dn_dt_bias, m_dn_norm_w, m_q_norm_w, m_w_uq, m_kv_norm_w, m_w_uk, m_w_uv, m_w_br_dn, m_w_br_mla, m_w_o, m_ln1_g, m_ln1_b, m_w_ffn_in, m_w_ffn_out, m_w_ple, m_w_ple_gate, m_ln2_g, m_ln2_b, v_w_in, v_conv_w, v_dn_a_log, v_dn_dt_bias, v_dn_norm_w, v_q_norm_w, v_w_uq, v_kv_norm_w, v_w_uk, v_w_uv, v_w_br_dn, v_w_br_mla, v_w_o, v_ln1_g, v_ln1_b, v_w_ffn_in, v_w_ffn_out, v_w_ple, v_w_ple_gate, v_ln2_g, v_ln2_b):
    args = dict(locals())
    wts = {n: args[n] for n in _ORDER}
    mom1 = {n: args["m_" + n] for n in _ORDER}
    mom2 = {n: args["v_" + n] for n in _ORDER}
    big_names = [n for n, _ in _BIG]
    shard_shapes = {n: wts[n].shape[1:] for n in big_names}
    layout, total_rows = _flat_layout(shard_shapes)
    drop = lambda d, names: {n: d[n][0] for n in names}

    w_flat = _pack_shards(drop(wts, big_names), layout, total_rows)
    gathered = _all_gather(w_flat.astype(BF16), "ag_weights")
    full_w = _unpack_gathered(gathered, shard_shapes, layout)
    small_w = {n: wts[n].astype(F32) for n in _SMALL}
    w = _full_weights(full_w, small_w)

    s_dim = x.shape[1]
    loss, dx, g = _local_step(x[0], p[0, 0], positions.reshape(s_dim, 1).astype(F32), loss_target[0], w)
    full_g, small_g = _full_grads(g)

    c_idx = lax.axis_index("c")
    q_idx = 2 * lax.axis_index("x") + lax.axis_index("y")
    g_own = _pack_grads_for_parity(full_g, shard_shapes, layout, total_rows, c_idx)
    g_sib = _pack_grads_for_parity(full_g, shard_shapes, layout, total_rows, 1 - c_idx)
    from_sibling = _exchange_sibling(g_sib, "rs_sibling")
    chip_sum, chip_sum_bf = _add_pairs(g_own, from_sibling, "rs_chip_sum")
    from_chips = _exchange_chips(chip_sum_bf, "rs_chips")
    mine = lax.dynamic_index_in_dim(chip_sum, q_idx, axis=0, keepdims=False)

    m_flat = _pack_shards(drop(mom1, big_names), layout, total_rows)
    v_flat = _pack_shards(drop(mom2, big_names), layout, total_rows)
    g_flat, d_flat, m2_flat, v2_flat = _adamw_flat(w_flat, m_flat, v_flat,
                                                   [mine, from_chips[0], from_chips[1], from_chips[2]])
    out_g = _unpack_shards(g_flat, shard_shapes, layout)
    out_d = _unpack_shards(d_flat, shard_shapes, layout)
    out_m = _unpack_shards(m2_flat, shard_shapes, layout)
    out_v = _unpack_shards(v2_flat, shard_shapes, layout)

    small_shapes = {n: wts[n].shape for n in _SMALL}
    g_small = _all_gather(_pack_small(small_g, loss), "ag_small")
    sg, sd, sm2, sv2 = _adamw_small(_pack_small(small_w), _pack_small({n: mom1[n] for n in _SMALL}),
                                    _pack_small({n: mom2[n] for n in _SMALL}), g_small)
    for blk, dst in ((sg, out_g), (sd, out_d), (sm2, out_m), (sv2, out_v)):
        dst.update(_unpack_small(blk, small_shapes))
    loss_out = sg[_LOSS_SLOT[0], _LOSS_SLOT[1]]

    expand = lambda d, n: d[n] if n in _SMALL else d[n][None]
    return (loss_out, dx[None], *[expand(out_g, n) for n in _ORDER], *[expand(out_d, n) for n in _ORDER],
            *[expand(out_m, n) for n in _ORDER], *[expand(out_v, n) for n in _ORDER])
```

```python
import functools

import numpy as np
import jax
import jax.numpy as jnp
from jax import lax
from jax.experimental import pallas as pl
from jax.experimental.pallas import tpu as pltpu

F32 = jnp.float32
BF16 = jnp.bfloat16

D_MODEL = 1024
N_HEADS = 8
HEAD = 128
CHUNK = 64
GROUP = 256
ROPE = 64
Q_LORA = 384
KV_LORA = 256
FFN_HIDDEN = 2816
PLE_DIM = 256
ROPE_BASE = 10000.0
ALPHA = 2.0 ** 0.25
SCALE = float((HEAD + ROPE) ** -0.5)
NEG_BIG = -1e30
EPS_RMS = 1e-6
EPS_LN = 1e-5

ADAM_LR = 0.001
ADAM_B1 = 0.9
ADAM_B2 = 0.999
ADAM_EPS = 1e-08
ADAM_WD = 0.01
ADAM_STEP = 10

N_DEV = 8
LANES = 128
FLAT_COLS = 1024

WB_CQ, WB_CKV, WB_KR, WB_BA, WB_COLS = 0, 512, 768, 896, 1024

HIGHEST = lax.Precision.HIGHEST

NN = (((1,), (0,)), ((), ()))
TN = (((0,), (0,)), ((), ()))
NT = (((1,), (1,)), ((), ()))


def _dot(a, b, dims=NN):
    return lax.dot_general(a.astype(BF16), b.astype(BF16), dims, preferred_element_type=F32)


def _dot32(a, b, dims=NN):
    return lax.dot_general(a, b, dims, precision=HIGHEST, preferred_element_type=F32)


def _sig(x):
    return 1.0 / (1.0 + jnp.exp(-x))


MM_TILE = 1536


def _pick_wide(n):
    if n <= MM_TILE:
        return n
    return max(t for t in range(LANES, MM_TILE + 1, LANES) if n % t == 0)


def _split_bf16(a):
    hi = a.astype(BF16)
    return hi, (a - hi.astype(F32)).astype(BF16)


def _dot3(a, b, dims=NN):
    ah, al = a if isinstance(a, tuple) else _split_bf16(a)
    bh, bl = b if isinstance(b, tuple) else _split_bf16(b)
    d = lambda p, q: lax.dot_general(p, q, dims, preferred_element_type=F32)
    return d(ah, bh) + (d(ah, bl) + d(al, bh))


def _mm(a, b, *, ta=False, tb=False, add=(), out_dtype=F32, name):
    if ta:
        k_dim, m_dim = a.shape
    else:
        m_dim, k_dim = a.shape
    if tb:
        n_dim, k2 = b.shape
    else:
        k2, n_dim = b.shape
    assert k_dim == k2, (a.shape, b.shape, ta, tb)
    tm = _pick_wide(m_dim)
    tn = _pick_wide(n_dim)
    tk = _pick_wide(k_dim)
    nk = k_dim // tk
    n_add = len(add)
    dims = TN if ta else (NT if tb else NN)
    assert not (ta and tb)

    def body(a_ref, b_ref, *rest):
        add_refs = rest[:n_add]
        o_ref = rest[n_add]
        acc = rest[n_add + 1]
        k = pl.program_id(2)

        @pl.when(k == 0)
        def _():
            acc[...] = jnp.zeros_like(acc)

        acc[...] += _dot(a_ref[...], b_ref[...], dims)

        @pl.when(k == nk - 1)
        def _():
            r = acc[...]
            for ar in add_refs:
                r = r + ar[...].astype(F32)
            o_ref[...] = r.astype(o_ref.dtype)

    a_spec = pl.BlockSpec((tk, tm), lambda i, j, k: (k, i)) if ta else pl.BlockSpec((tm, tk), lambda i, j, k: (i, k))
    b_spec = pl.BlockSpec((tn, tk), lambda i, j, k: (j, k)) if tb else pl.BlockSpec((tk, tn), lambda i, j, k: (k, j))
    o_spec = pl.BlockSpec((tm, tn), lambda i, j, k: (i, j))
    return pl.pallas_call(
        body,
        out_shape=jax.ShapeDtypeStruct((m_dim, n_dim), out_dtype),
        grid=(m_dim // tm, n_dim // tn, nk),
        in_specs=[a_spec, b_spec] + [o_spec] * n_add,
        out_specs=o_spec,
        scratch_shapes=[pltpu.VMEM((tm, tn), F32)],
        compiler_params=pltpu.CompilerParams(dimension_semantics=("parallel", "parallel", "arbitrary")),
        name=name,
    )(a, b, *add)


def _rowwise(fn, rows, consts, outs, accs=(), *, tm=256, name):
    rows = [r if isinstance(r, tuple) else (r, 0, r.shape[1]) for r in rows]
    s_dim = rows[0][0].shape[0]
    tm = min(tm, s_dim)
    assert s_dim % tm == 0 and all(arr.shape[0] == s_dim for arr, _, _ in rows)
    specs = [pl.BlockSpec((tm, width), functools.partial(lambda i, cb: (i, cb), cb=cb)) for _, cb, width in rows]
    args = [arr for arr, _, _ in rows]
    for c in consts:
        specs.append(pl.BlockSpec(c.shape, lambda i: (0, 0)))
        args.append(c)
    nr, nc, no = len(rows), len(consts), len(outs)
    out_shape = [jax.ShapeDtypeStruct((s_dim, w), dt) for (w, dt) in outs]
    out_specs = [pl.BlockSpec((tm, w), lambda i: (i, 0)) for (w, dt) in outs]
    out_shape += [jax.ShapeDtypeStruct(sh, F32) for sh in accs]
    out_specs += [pl.BlockSpec(sh, lambda i: (0, 0)) for sh in accs]

    def body(*refs):
        r = [x[...] for x in refs[:nr]]
        c = [x[...] for x in refs[nr:nr + nc]]
        o_refs = refs[nr + nc:nr + nc + no]
        a_refs = refs[nr + nc + no:]
        o_vals, a_vals = fn(r, c)
        for ref, v in zip(o_refs, o_vals, strict=True):
            ref[...] = v.astype(ref.dtype)
        if a_refs:
            @pl.when(pl.program_id(0) == 0)
            def _():
                for ref in a_refs:
                    ref[...] = jnp.zeros_like(ref)

            for ref, v in zip(a_refs, a_vals, strict=True):
                ref[...] += v

    res = pl.pallas_call(
        body,
        out_shape=out_shape,
        grid=(s_dim // tm,),
        in_specs=specs,
        out_specs=out_specs,
        compiler_params=pltpu.CompilerParams(dimension_semantics=("arbitrary" if accs else "parallel",)),
        name=name,
    )(*args)
    return res


def _colsum(v):
    return jnp.sum(v, axis=0, keepdims=True)


def _rowsum(v):
    return jnp.sum(v, axis=1, keepdims=True)


def _rowmean(v):
    return jnp.mean(v, axis=1, keepdims=True)


def _silu_grad(x):
    s = _sig(x)
    return s * (1.0 + x * (1.0 - s))


def _conv_taps(x, w, width=4):
    row = lax.broadcasted_iota(jnp.int32, x.shape, 0)
    c = x * w[width - 1:width, :]
    for s in range(1, width):
        c = c + jnp.where(row >= s, pltpu.roll(x, s, 0), 0.0) * w[width - 1 - s:width - s, :]
    return c


def _conv_fwd(proj_a, conv_w):
    s_dim = proj_a.shape[0]
    n_blk = 3 * N_HEADS

    def body(x_ref, w_ref, o_ref):
        j = pl.program_id(0)
        c = _conv_taps(x_ref[...], w_ref[...])
        y = c * _sig(c)
        r = lax.rsqrt(_rowsum(y * y) + EPS_RMS)
        fac = jnp.where(j < N_HEADS, r * (HEAD ** -0.5), jnp.where(j < 2 * N_HEADS, r, 1.0))
        o_ref[...] = y * fac

    return pl.pallas_call(
        body,
        out_shape=jax.ShapeDtypeStruct((s_dim, n_blk * HEAD), F32),
        grid=(n_blk,),
        in_specs=[pl.BlockSpec((s_dim, HEAD), lambda j: (0, j)), pl.BlockSpec((4, HEAD), lambda j: (0, j))],
        out_specs=pl.BlockSpec((s_dim, HEAD), lambda j: (0, j)),
        compiler_params=pltpu.CompilerParams(dimension_semantics=("parallel",)),
        name="conv_fwd",
    )(proj_a, conv_w)


def _conv_bwd(proj_a, conv_w, dq, dk, dv):
    s_dim = proj_a.shape[0]
    n_blk = 3 * N_HEADS

    def body(x_ref, w_ref, dq_ref, dk_ref, dv_ref, dx_ref, dw_ref):
        j = pl.program_id(0)
        x = x_ref[...]
        w = w_ref[...]
        do = jnp.where(j < N_HEADS, dq_ref[...], jnp.where(j < 2 * N_HEADS, dk_ref[...], dv_ref[...]))
        c = _conv_taps(x, w)
        sg = _sig(c)
        y = c * sg
        r = lax.rsqrt(_rowsum(y * y) + EPS_RMS)
        sc = jnp.where(j < N_HEADS, HEAD ** -0.5, 1.0)
        dy_n = sc * (r * do - y * (r * r * r) * _rowsum(do * y))
        dy = jnp.where(j < 2 * N_HEADS, dy_n, do)
        dc = dy * (sg * (1.0 + c * (1.0 - sg)))
        row = lax.broadcasted_iota(jnp.int32, x.shape, 0)
        dx = dc * w[3:4, :]
        dw_ref[3:4, :] = _colsum(dc * x)
        for s in range(1, 4):
            dx = dx + jnp.where(row < s_dim - s, pltpu.roll(dc, s_dim - s, 0), 0.0) * w[3 - s:4 - s, :]
            xs = jnp.where(row >= s, pltpu.roll(x, s, 0), 0.0)
            dw_ref[3 - s:4 - s, :] = _colsum(dc * xs)
        dx_ref[...] = dx.astype(dx_ref.dtype)

    hd = N_HEADS - 1
    return pl.pallas_call(
        body,
        out_shape=[jax.ShapeDtypeStruct((s_dim, n_blk * HEAD), BF16), jax.ShapeDtypeStruct((4, n_blk * HEAD), F32)],
        grid=(n_blk,),
        in_specs=[
            pl.BlockSpec((s_dim, HEAD), lambda j: (0, j)),
            pl.BlockSpec((4, HEAD), lambda j: (0, j)),
            pl.BlockSpec((s_dim, HEAD), lambda j: (0, jnp.minimum(j, hd))),
            pl.BlockSpec((s_dim, HEAD), lambda j: (0, jnp.clip(j - N_HEADS, 0, hd))),
            pl.BlockSpec((s_dim, HEAD), lambda j: (0, jnp.clip(j - 2 * N_HEADS, 0, hd))),
        ],
        out_specs=[pl.BlockSpec((s_dim, HEAD), lambda j: (0, j)), pl.BlockSpec((4, HEAD), lambda j: (0, j))],
        compiler_params=pltpu.CompilerParams(dimension_semantics=("parallel",)),
        name="conv_bwd",
    )(proj_a, conv_w, dq, dk, dv)


def _chunk_tri(n):
    r = np.arange(n)
    m = ((r[:, None] // CHUNK) == (r[None, :] // CHUNK)) & (r[:, None] >= r[None, :])
    m = m.astype(np.float32)
    return jnp.asarray(m), jnp.asarray(m.T)


def _softplus(z):
    return jnp.maximum(z, 0.0) + jnp.log(1.0 + jnp.exp(-jnp.abs(z)))


def _gates_fwd(proj_b, alog, dtb):
    tm = min(GROUP, proj_b.shape[0])
    tri, _ = _chunk_tri(tm)

    def fn(r, c):
        b = r[0]
        a = pltpu.roll(b, LANES - N_HEADS, 1)
        alog_, dtb_, tri_ = c
        g = -jnp.exp(alog_) * _softplus(a + dtb_)
        return [_sig(b), _dot32(tri_, g)], []

    return _rowwise(fn, [(proj_b, WB_BA // LANES, LANES)], [alog, dtb, tri],
                    [(LANES, F32), (LANES, F32)], tm=tm, name="gates_fwd")


def _gates_bwd(proj_b, alog, dtb, gc, d_beta, d_gc, d_egl_rows):
    tm = min(GROUP, proj_b.shape[0])
    _, tri_t = _chunk_tri(tm)

    def fn(r, c):
        b, gc_, d_beta_, d_gc_, d_egl_ = r
        a = pltpu.roll(b, LANES - N_HEADS, 1)
        alog_, dtb_, tri_t_ = c
        z = a + dtb_
        ea = jnp.exp(alog_)
        g = -ea * _softplus(z)
        dg = _dot32(tri_t_, d_gc_ + d_egl_ * jnp.exp(gc_))
        d_a = dg * (-ea) * _sig(z)
        beta = _sig(b)
        d_ba = d_beta_ * beta * (1.0 - beta) + pltpu.roll(d_a, N_HEADS, 1)
        return [d_ba], [_colsum(dg * g), _colsum(d_a)]

    return _rowwise(fn, [(proj_b, WB_BA // LANES, LANES), gc, d_beta, d_gc, d_egl_rows],
                    [alog, dtb, tri_t], [(LANES, BF16)], accs=[(1, LANES), (1, LANES)], tm=tm,
                    name="gates_bwd")


def _group_masks(n):
    r = lax.broadcasted_iota(jnp.int32, (n, n), 0)
    c = lax.broadcasted_iota(jnp.int32, (n, n), 1)
    same = (r // CHUNK) == (c // CHUNK)
    tril = jnp.logical_and(same, r >= c)
    strict = jnp.logical_and(same, r > c)
    last = c == (r // CHUNK) * CHUNK + (CHUNK - 1)
    eye = r == c
    return same, tril, strict, last, eye


def _inv_unit_lower(l_mat, eye_f):
    q = -l_mat
    r = eye_f + q
    qs = _split_bf16(q)
    for _ in range(5):
        qs = _split_bf16(_dot3(qs, qs))
        r = r + _dot3(r, qs)
    return r


def _unfold_blocks(folded, mask):
    n = folded.shape[0]
    return jnp.where(mask, jnp.concatenate([folded] * (n // CHUNK), axis=1), 0.0)


def _head_cols(beta, gc, gc_t, h):
    lane = lax.broadcasted_iota(jnp.int32, beta.shape, 1)
    sub = lax.broadcasted_iota(jnp.int32, gc_t.shape, 0)
    bcol = _rowsum(jnp.where(lane == h, beta, 0.0))
    gcol = _rowsum(jnp.where(lane == h, gc, 0.0))
    grow = _colsum(jnp.where(sub == h, gc_t, 0.0))
    return bcol, gcol, grow


def _prep_common(q, k, bcol, gcol, grow, t_folded=None):
    n = q.shape[0]
    same, tril, strict, last, eye = _group_masks(n)
    decay = jnp.where(tril, jnp.exp(jnp.where(tril, gcol - grow, 0.0)), 0.0)
    glast = _rowsum(jnp.where(last, jnp.broadcast_to(grow, (n, n)), 0.0))
    e = jnp.exp(gcol)
    ekt = jnp.exp(glast - gcol)
    kb = k * bcol
    kk = _dot(kb, k, NT)
    if t_folded is None:
        t_mat = _inv_unit_lower(jnp.where(strict, kk * decay, 0.0), eye.astype(F32))
    else:
        t_mat = _unfold_blocks(t_folded, same)
    qk = _dot(q, k, NT)
    return dict(same=same, tril=tril, strict=strict, last=last, eye=eye, decay=decay, e=e, ekt=ekt, kb=kb, kk=kk,
                t=t_mat, qk=qk)


def _fold_blocks(m):
    n = m.shape[0]
    out = m[:, 0:CHUNK]
    for b in range(1, n // CHUNK):
        out = out + m[:, b * CHUNK:(b + 1) * CHUNK]
    return out


def _gdr_prep_fwd(qkvn, beta, gc, gc_t):
    s_dim = qkvn.shape[0]
    tg = min(GROUP, s_dim)

    def body(q_ref, k_ref, v_ref, b_ref, g_ref, gt_ref, u_ref, w_ref, qd_ref, kt_ref, a_ref, t_ref):
        h = pl.program_id(0)
        q, k, v = q_ref[...], k_ref[...], v_ref[...]
        bcol, gcol, grow = _head_cols(b_ref[...], g_ref[...], gt_ref[...], h)
        p = _prep_common(q, k, bcol, gcol, grow)
        u_ref[...] = _dot(p["t"], v * bcol)
        w_ref[...] = _dot(p["t"], p["kb"] * p["e"])
        qd_ref[...] = q * p["e"]
        kt_ref[...] = k * p["ekt"]
        a_ref[...] = _fold_blocks(jnp.where(p["tril"], p["qk"] * p["decay"], 0.0))
        t_ref[...] = _fold_blocks(p["t"])

    row = lambda off: pl.BlockSpec((tg, HEAD), functools.partial(lambda h, m, off: (m, h + off), off=off))
    full = pl.BlockSpec((tg, LANES), lambda h, m: (m, 0))
    o_spec = pl.BlockSpec((tg, HEAD), lambda h, m: (m, h))
    a_spec = pl.BlockSpec((None, tg, CHUNK), lambda h, m: (h, m, 0))
    wide = jax.ShapeDtypeStruct((s_dim, N_HEADS * HEAD), F32)
    folded = jax.ShapeDtypeStruct((N_HEADS, s_dim, CHUNK), F32)
    return pl.pallas_call(
        body,
        out_shape=[wide, wide, wide, wide, folded, folded],
        grid=(N_HEADS, s_dim // tg),
        in_specs=[row(0), row(N_HEADS), row(2 * N_HEADS), full, full, pl.BlockSpec((8, tg), lambda h, m: (0, m))],
        out_specs=[o_spec, o_spec, o_spec, o_spec, a_spec, a_spec],
        compiler_params=pltpu.CompilerParams(dimension_semantics=("parallel", "parallel")),
        name="gdr_prep_fwd",
    )(qkvn, qkvn, qkvn, beta, gc, gc_t)


def _gdr_prep_bwd(qkvn, beta, gc, gc_t, t_fold, du, dw, dqd, dkt, d_a):
    s_dim = qkvn.shape[0]
    tg = min(GROUP, s_dim)

    def body(q_ref, k_ref, v_ref, b_ref, g_ref, gt_ref, t_ref, du_ref, dw_ref, dqd_ref, dkt_ref, da_ref,
             dq_ref, dk_ref, dv_ref, db_ref, dg_ref):
        h = pl.program_id(1)
        q, k, v = q_ref[...], k_ref[...], v_ref[...]
        bcol, gcol, grow = _head_cols(b_ref[...], g_ref[...], gt_ref[...], h)
        p = _prep_common(q, k, bcol, gcol, grow, t_ref[...])
        t_mat, decay, e, ekt, kb = p["t"], p["decay"], p["e"], p["ekt"], p["kb"]
        du_, dw_, dqd_, dkt_ = du_ref[...], dw_ref[...], dqd_ref[...], dkt_ref[...]
        vb = v * bcol
        kbe = kb * e
        d_t = _dot(du_, vb, NT) + _dot(dw_, kbe, NT)
        dvb = _dot(t_mat, du_, TN)
        dkbe = _dot(t_mat, dw_, TN)
        ts = _split_bf16(t_mat)
        d_l = -_dot3(_dot3(ts, d_t, TN), ts, NT)
        m1 = jnp.where(p["strict"], d_l, 0.0)
        m2 = _unfold_blocks(da_ref[...], p["tril"])
        d_kk = m1 * decay
        d_qk = m2 * decay
        d_decay = m1 * p["kk"] + m2 * p["qk"]
        dkb = _dot(d_kk, k) + dkbe * e
        dk = _dot(d_kk, kb, TN) + _dot(d_qk, q, TN) + dkt_ * ekt + dkb * bcol
        dq = _dot(d_qk, k) + dqd_ * e
        d_beta = _rowsum(dkb * k) + _rowsum(dvb * v)
        d_e = _rowsum(dkbe * kb) + _rowsum(dqd_ * q)
        d_ekt = _rowsum(dkt_ * k) * ekt
        d_diff = d_decay * decay
        d_grow = -_colsum(d_diff) + _colsum(jnp.where(p["last"], jnp.broadcast_to(d_ekt, (tg, tg)), 0.0))
        d_gcol = d_e * e - d_ekt + _rowsum(d_diff)
        d_gcol = d_gcol + _rowsum(jnp.where(p["eye"], jnp.broadcast_to(d_grow, (tg, tg)), 0.0))
        dq_ref[...] = dq
        dk_ref[...] = dk
        dv_ref[...] = dvb * bcol

        @pl.when(h == 0)
        def _():
            db_ref[...] = jnp.zeros_like(db_ref)
            dg_ref[...] = jnp.zeros_like(dg_ref)

        lane = lax.broadcasted_iota(jnp.int32, (tg, LANES), 1)
        db_ref[...] = jnp.where(lane == h, d_beta, db_ref[...])
        dg_ref[...] = jnp.where(lane == h, d_gcol, dg_ref[...])

    row = lambda off: pl.BlockSpec((tg, HEAD), functools.partial(lambda m, h, off: (m, h + off), off=off))
    full = pl.BlockSpec((tg, LANES), lambda m, h: (m, 0))
    o_spec = pl.BlockSpec((tg, HEAD), lambda m, h: (m, h))
    a_spec = pl.BlockSpec((None, tg, CHUNK), lambda m, h: (h, m, 0))
    wide = jax.ShapeDtypeStruct((s_dim, N_HEADS * HEAD), F32)
    lanes = jax.ShapeDtypeStruct((s_dim, LANES), F32)
    return pl.pallas_call(
        body,
        out_shape=[wide, wide, wide, lanes, lanes],
        grid=(s_dim // tg, N_HEADS),
        in_specs=[row(0), row(N_HEADS), row(2 * N_HEADS), full, full, pl.BlockSpec((8, tg), lambda m, h: (0, m)),
                  a_spec, o_spec, o_spec, o_spec, o_spec, a_spec],
        out_specs=[o_spec, o_spec, o_spec, full, full],
        compiler_params=pltpu.CompilerParams(dimension_semantics=("parallel", "arbitrary")),
        name="gdr_prep_bwd",
    )(qkvn, qkvn, qkvn, beta, gc, gc_t, t_fold, du, dw, dqd, dkt, d_a)


def _gdr_scan_fwd(u, w, qd, kt, a_mat, gc):
    s_dim = u.shape[0]
    n_chunks = s_dim // CHUNK

    def body(u_ref, w_ref, qd_ref, kt_ref, a_ref, g_ref, o_ref, st_ref, state):
        @pl.when(pl.program_id(0) == 0)
        def _():
            state[...] = jnp.zeros_like(state)

        egl = jnp.exp(g_ref[CHUNK - 1:CHUNK, :])
        for h in range(N_HEADS):
            cs = slice(h * HEAD, (h + 1) * HEAD)
            s_h = state[h]
            st_ref[h] = s_h
            vn = u_ref[:, cs] - _dot(w_ref[:, cs], s_h)
            o_ref[:, cs] = _dot(qd_ref[:, cs], s_h) + _dot(a_ref[h], vn)
            state[h] = s_h * egl[:, h:h + 1] + _dot(kt_ref[:, cs], vn, TN)

    wide = pl.BlockSpec((CHUNK, N_HEADS * HEAD), lambda n: (n, 0))
    return pl.pallas_call(
        body,
        out_shape=[jax.ShapeDtypeStruct((s_dim, N_HEADS * HEAD), F32),
                   jax.ShapeDtypeStruct((n_chunks, N_HEADS, HEAD, HEAD), F32)],
        grid=(n_chunks,),
        in_specs=[wide, wide, wide, wide, pl.BlockSpec((N_HEADS, CHUNK, CHUNK), lambda n: (0, n, 0)),
                  pl.BlockSpec((CHUNK, LANES), lambda n: (n, 0))],
        out_specs=[wide, pl.BlockSpec((None, N_HEADS, HEAD, HEAD), lambda n: (n, 0, 0, 0))],
        scratch_shapes=[pltpu.VMEM((N_HEADS, HEAD, HEAD), F32)],
        compiler_params=pltpu.CompilerParams(dimension_semantics=("arbitrary",)),
        name="gdr_scan_fwd",
    )(u, w, qd, kt, a_mat, gc)


def _gdr_scan_bwd(u, w, qd, kt, a_mat, gc, states, d_o):
    s_dim = u.shape[0]
    n_chunks = s_dim // CHUNK
    last = n_chunks - 1

    def body(u_ref, w_ref, qd_ref, kt_ref, a_ref, g_ref, st_ref, do_ref,
             du_ref, dw_ref, dqd_ref, dkt_ref, da_ref, de_ref, d_state):
        @pl.when(pl.program_id(0) == 0)
        def _():
            d_state[...] = jnp.zeros_like(d_state)

        egl = jnp.exp(g_ref[CHUNK - 1:CHUNK, :])
        for h in range(N_HEADS):
            cs = slice(h * HEAD, (h + 1) * HEAD)
            s_h = st_ref[h]
            ds_n = d_state[h]
            do = do_ref[:, cs]
            w_h = w_ref[:, cs]
            vn = u_ref[:, cs] - _dot(w_h, s_h)
            dvn = _dot(a_ref[h], do, TN) + _dot(kt_ref[:, cs], ds_n)
            dqd_ref[:, cs] = _dot(do, s_h, NT)
            da_ref[h] = _dot(do, vn, NT)
            dkt_ref[:, cs] = _dot(vn, ds_n, NT)
            de = jnp.sum(_rowsum(ds_n * s_h), axis=0, keepdims=True)
            de_ref[h:h + 1, :] = jnp.broadcast_to(de, (1, LANES))
            du_ref[:, cs] = dvn
            dw_ref[:, cs] = -_dot(dvn, s_h, NT)
            d_state[h] = ds_n * egl[:, h:h + 1] + _dot(qd_ref[:, cs], do, TN) - _dot(w_h, dvn, TN)

    wide = pl.BlockSpec((CHUNK, N_HEADS * HEAD), lambda n: (last - n, 0))
    a_spec = pl.BlockSpec((N_HEADS, CHUNK, CHUNK), lambda n: (0, last - n, 0))
    wide_shape = jax.ShapeDtypeStruct((s_dim, N_HEADS * HEAD), F32)
    return pl.pallas_call(
        body,
        out_shape=[wide_shape, wide_shape, wide_shape, wide_shape,
                   jax.ShapeDtypeStruct((N_HEADS, s_dim, CHUNK), F32),
                   jax.ShapeDtypeStruct((n_chunks, N_HEADS, LANES), F32)],
        grid=(n_chunks,),
        in_specs=[wide, wide, wide, wide, a_spec, pl.BlockSpec((CHUNK, LANES), lambda n: (last - n, 0)),
                  pl.BlockSpec((None, N_HEADS, HEAD, HEAD), lambda n: (last - n, 0, 0, 0)), wide],
        out_specs=[wide, wide, wide, wide, a_spec, pl.BlockSpec((None, N_HEADS, LANES), lambda n: (last - n, 0, 0))],
        scratch_shapes=[pltpu.VMEM((N_HEADS, HEAD, HEAD), F32)],
        compiler_params=pltpu.CompilerParams(dimension_semantics=("arbitrary",)),
        name="gdr_scan_bwd",
    )(u, w, qd, kt, a_mat, gc, states, d_o)


def _gdr_out_fwd(o_dn, proj_a, dn_w):
    def fn(r, c):
        o, z = r
        (w_,) = c
        outs = []
        for h in range(N_HEADS):
            cs = slice(h * HEAD, (h + 1) * HEAD)
            oh, zh = o[:, cs], z[:, cs]
            rr = lax.rsqrt(_rowmean(oh * oh) + EPS_RMS)
            outs.append(oh * rr * w_ * (zh * _sig(zh)))
        return [jnp.concatenate(outs, axis=1)], []

    return _rowwise(fn, [o_dn, (proj_a, 3, D_MODEL)], [dn_w], [(D_MODEL, BF16)], name="gdr_out_fwd")[0]


def _gdr_out_bwd(o_dn, proj_a, d_og, dn_w):
    def fn(r, c):
        o, z, dg = r
        (w_,) = c
        d_o, d_z = [], []
        d_w = jnp.zeros((1, HEAD), F32)
        for h in range(N_HEADS):
            cs = slice(h * HEAD, (h + 1) * HEAD)
            oh, zh, dgh = o[:, cs], z[:, cs], dg[:, cs]
            rr = lax.rsqrt(_rowmean(oh * oh) + EPS_RMS)
            sz = zh * _sig(zh)
            d_n = dgh * sz
            d_z.append(dgh * (oh * rr * w_) * _silu_grad(zh))
            d_w = d_w + _colsum(d_n * oh * rr)
            gw = d_n * w_
            d_o.append(rr * gw - oh * (rr * rr * rr) * _rowmean(gw * oh))
        return [jnp.concatenate(d_o, axis=1), jnp.concatenate(d_z, axis=1)], [d_w]

    return _rowwise(fn, [o_dn, (proj_a, 3, D_MODEL), d_og], [dn_w], [(D_MODEL, F32), (D_MODEL, BF16)],
                    accs=[(1, HEAD)], name="gdr_out_bwd")


def _rms_fwd(x, w):
    r = lax.rsqrt(_rowmean(x * x) + EPS_RMS)
    return x * r * w


def _rms_bwd(x, w, dy):
    r = lax.rsqrt(_rowmean(x * x) + EPS_RMS)
    gw = dy * w
    return r * gw - x * (r * r * r) * _rowmean(gw * x), _colsum(dy * x * r)


def _mla_norm_fwd(proj_b, qn_w, kvn_w):
    def fn(r, c):
        return [_rms_fwd(r[0], c[0]), _rms_fwd(r[1], c[1])], []

    return _rowwise(fn, [(proj_b, WB_CQ // Q_LORA, Q_LORA), (proj_b, WB_CKV // KV_LORA, KV_LORA)], [qn_w, kvn_w],
                    [(Q_LORA, BF16), (KV_LORA, BF16)], name="mla_norm_fwd")


def _mla_norm_bwd(proj_b, qn_w, kvn_w, d_cq, d_ckv):
    def fn(r, c):
        dx1, dw1 = _rms_bwd(r[0], c[0], r[2])
        dx2, dw2 = _rms_bwd(r[1], c[1], r[3])
        return [dx1, dx2], [dw1, dw2]

    return _rowwise(fn, [(proj_b, WB_CQ // Q_LORA, Q_LORA), (proj_b, WB_CKV // KV_LORA, KV_LORA), d_cq, d_ckv],
                    [qn_w, kvn_w], [(Q_LORA, BF16), (KV_LORA, BF16)], accs=[(1, Q_LORA), (1, KV_LORA)],
                    name="mla_norm_bwd")


def _rope_consts():
    inv = ROPE_BASE ** (-np.arange(0, ROPE, 2, dtype=np.float32) / ROPE)
    t = np.zeros((4, LANES), np.float32)
    t[0, :32] = inv
    t[0, 32:64] = inv
    t[1, :64] = 1.0
    t[2, 32:64] = 1.0
    t[3, :32] = -1.0
    return jnp.asarray(t)


def _rope_tables(pos, consts, width):
    ang = pos * consts[0:1, :]
    cosv, sinv = jnp.cos(ang), jnp.sin(ang)
    reps = width // LANES
    tile = (lambda t: jnp.concatenate([t] * reps, axis=1)) if reps > 1 else (lambda t: t)
    return tile(cosv * consts[1:2, :]), tile(sinv * consts[2:3, :]), tile(sinv * consts[3:4, :])


def _rope_apply(t, tabs):
    cos_t, sin_a, sin_b = tabs
    width = t.shape[1]
    return t * cos_t + pltpu.roll(t, 32, 1) * sin_a + pltpu.roll(t, width - 32, 1) * sin_b


def _rope_transpose(d, tabs):
    cos_t, sin_a, sin_b = tabs
    width = d.shape[1]
    return d * cos_t + pltpu.roll(d * sin_a, width - 32, 1) + pltpu.roll(d * sin_b, 32, 1)


QK_HEAD = 2 * HEAD


def _interleave_heads(a, b):
    parts = []
    for h in range(N_HEADS):
        parts.append(a[:, h * HEAD:(h + 1) * HEAD])
        parts.append(b if b.shape[1] == LANES else b[:, h * LANES:(h + 1) * LANES])
    return jnp.concatenate(parts, axis=1)


def _mla_qk_fwd(q_full, k_nope, proj_b, pos):
    consts = _rope_consts()

    def fn(r, c):
        qf, kn, kr, pos_ = r
        qn, qr = qf[:, :D_MODEL], qf[:, D_MODEL:]
        qr = _rope_apply(qr, _rope_tables(pos_, c[0], D_MODEL))
        kr = _rope_apply(kr, _rope_tables(pos_, c[0], LANES))
        return [_interleave_heads(qn, qr) * SCALE, _interleave_heads(kn, kr)], []

    return _rowwise(fn, [q_full, k_nope, (proj_b, WB_KR // LANES, LANES), pos], [consts],
                    [(N_HEADS * QK_HEAD, BF16), (N_HEADS * QK_HEAD, BF16)], name="mla_qk_fwd")


def _mla_qk_bwd(d_qc, d_kc, pos):
    consts = _rope_consts()

    def fn(r, c):
        dq, dk, pos_ = r
        even = lambda t: jnp.concatenate([t[:, (2 * h) * LANES:(2 * h + 1) * LANES] for h in range(N_HEADS)], axis=1)
        odd = lambda t: jnp.concatenate([t[:, (2 * h + 1) * LANES:(2 * h + 2) * LANES] for h in range(N_HEADS)], axis=1)
        d_qr_raw = _rope_transpose(odd(dq), _rope_tables(pos_, c[0], D_MODEL)) * SCALE
        dkr = dk[:, LANES:2 * LANES]
        for h in range(1, N_HEADS):
            dkr = dkr + dk[:, (2 * h + 1) * LANES:(2 * h + 2) * LANES]
        return [jnp.concatenate([even(dq) * SCALE, d_qr_raw], axis=1), even(dk),
                _rope_transpose(dkr, _rope_tables(pos_, c[0], LANES))], []

    return _rowwise(fn, [d_qc, d_kc, pos], [consts], [(2 * D_MODEL, BF16), (D_MODEL, BF16), (LANES, BF16)],
                    name="mla_qk_bwd")


def _causal_mask_t(st, key0, query0):
    key = lax.broadcasted_iota(jnp.int32, st.shape, 0) + key0
    query = lax.broadcasted_iota(jnp.int32, st.shape, 1) + query0
    return jnp.where(key <= query, st, NEG_BIG)


def _attn_tiles(s_dim):
    tq = min(512, s_dim)
    n_chains = 2 if s_dim >= 2 * tq else 1
    return tq, n_chains, min(512, s_dim)


def _attn_fwd(qc, kc, vt):
    s_dim = qc.shape[0]
    tq, n_chains, tk = _attn_tiles(s_dim)
    tqs = tq * n_chains

    def body(q_ref, k_ref, vt_ref, o_ref, lse_ref, m_s, l_s, acc):
        qi = pl.program_id(1)
        m_s[...] = jnp.full_like(m_s, NEG_BIG)
        l_s[...] = jnp.zeros_like(l_s)
        acc[...] = jnp.zeros_like(acc)

        def step(j, carry):
            ks = pl.multiple_of(j * tk, tk)
            kb, vtb = k_ref[pl.ds(ks, tk), :], vt_ref[:, pl.ds(ks, tk)]
            for c in range(n_chains):
                cols = slice(c * tq, (c + 1) * tq)
                st = _causal_mask_t(_dot(kb, q_ref[cols, :], NT), j * tk, qi * tqs + c * tq)
                m_prev = m_s[:, cols]
                m_new = jnp.maximum(m_prev, jnp.max(st, axis=0, keepdims=True))
                alpha = jnp.exp(m_prev - m_new)
                pt = jnp.exp(st - m_new)
                l_s[:, cols] = alpha * l_s[:, cols] + _colsum(pt)
                m_s[:, cols] = m_new
                acc[:, cols] = acc[:, cols] * alpha + _dot(vtb, pt)
            return carry

        lax.fori_loop(0, (qi + 1) * (tqs // tk), step, 0)
        l = l_s[...]
        o_ref[...] = jnp.transpose(acc[...] / l)
        lse_ref[...] = m_s[...] + jnp.log(l)

    return pl.pallas_call(
        body,
        out_shape=[jax.ShapeDtypeStruct((s_dim, N_HEADS * HEAD), F32), jax.ShapeDtypeStruct((N_HEADS, 1, s_dim), F32)],
        grid=(N_HEADS, s_dim // tqs),
        in_specs=[pl.BlockSpec((tqs, QK_HEAD), lambda h, qi: (qi, h)),
                  pl.BlockSpec((s_dim, QK_HEAD), lambda h, qi: (0, h)),
                  pl.BlockSpec((HEAD, s_dim), lambda h, qi: (h, 0))],
        out_specs=[pl.BlockSpec((tqs, HEAD), lambda h, qi: (qi, h)),
                   pl.BlockSpec((None, 1, tqs), lambda h, qi: (h, 0, qi))],
        scratch_shapes=[pltpu.VMEM((1, tqs), F32), pltpu.VMEM((1, tqs), F32), pltpu.VMEM((HEAD, tqs), F32)],
        compiler_params=pltpu.CompilerParams(dimension_semantics=("parallel", "parallel")),
        name="attn_fwd",
    )(qc, kc, vt)


def _attn_bwd(qc, kc, kct, v, o, d_o, lse):
    s_dim = qc.shape[0]
    tq, n_chains, tk = _attn_tiles(s_dim)
    tqs = tq * n_chains

    def body(q_ref, k_ref, kt_ref, v_ref, o_ref, do_ref, lse_ref, dq_ref, dk_ref, dv_ref, dqt_acc, dv_acc):
        qi = pl.program_id(1)

        @pl.when(qi == 0)
        def _():
            dk_ref[...] = jnp.zeros_like(dk_ref)
            dv_acc[...] = jnp.zeros_like(dv_acc)

        dqt_acc[...] = jnp.zeros_like(dqt_acc)
        do_f = do_ref[...]
        do_all = do_f.astype(BF16)
        q_all = q_ref[...]
        lse_row = lse_ref[...]
        delta_row = _dot3(jnp.ones((8, HEAD), F32), o_ref[...] * do_f, NT)[0:1, :]

        def step(j, carry):
            ks = pl.multiple_of(j * tk, tk)
            kb, vb, ktb = k_ref[pl.ds(ks, tk), :], v_ref[pl.ds(ks, tk), :], kt_ref[:, pl.ds(ks, tk)]
            pts, dsts = [], []
            for c in range(n_chains):
                cols = slice(c * tq, (c + 1) * tq)
                st = _causal_mask_t(_dot(kb, q_all[cols, :], NT), j * tk, qi * tqs + c * tq)
                pt = jnp.exp(st - lse_row[:, cols])
                dst = pt * (_dot(vb, do_all[cols, :], NT) - delta_row[:, cols])
                dst_b = dst.astype(BF16)
                dqt_acc[:, cols] += _dot(ktb, dst_b)
                pts.append(pt.astype(BF16))
                dsts.append(dst_b)
            pt_all = jnp.concatenate(pts, axis=1) if n_chains > 1 else pts[0]
            dst_all = jnp.concatenate(dsts, axis=1) if n_chains > 1 else dsts[0]
            dk_ref[pl.ds(ks, tk), :] += _dot(dst_all, q_all)
            dv_acc[pl.ds(ks, tk), :] += _dot(pt_all, do_all)
            return carry

        lax.fori_loop(0, (qi + 1) * (tqs // tk), step, 0)
        dq_ref[...] = jnp.transpose(dqt_acc[...])

        @pl.when(qi == s_dim // tqs - 1)
        def _():
            dv_ref[...] = dv_acc[...].astype(dv_ref.dtype)

    q_spec = pl.BlockSpec((tqs, QK_HEAD), lambda h, qi: (qi, h))
    o_spec = pl.BlockSpec((tqs, HEAD), lambda h, qi: (qi, h))
    k_spec = pl.BlockSpec((s_dim, QK_HEAD), lambda h, qi: (0, h))
    v_spec = pl.BlockSpec((s_dim, HEAD), lambda h, qi: (0, h))
    wide2 = jax.ShapeDtypeStruct((s_dim, N_HEADS * QK_HEAD), F32)
    return pl.pallas_call(
        body,
        out_shape=[wide2, wide2, jax.ShapeDtypeStruct((s_dim, N_HEADS * HEAD), BF16)],
        grid=(N_HEADS, s_dim // tqs),
        in_specs=[q_spec, k_spec, pl.BlockSpec((QK_HEAD, s_dim), lambda h, qi: (h, 0)), v_spec, o_spec, o_spec,
                  pl.BlockSpec((None, 1, tqs), lambda h, qi: (h, 0, qi))],
        out_specs=[q_spec, k_spec, v_spec],
        scratch_shapes=[pltpu.VMEM((QK_HEAD, tqs), F32), pltpu.VMEM((s_dim, HEAD), F32)],
        compiler_params=pltpu.CompilerParams(dimension_semantics=("parallel", "arbitrary")),
        name="attn_bwd",
    )(qc, kc, kct, v, o, d_o, lse)


def _merge_fwd(y_dn, y_mla, proj_g):
    def fn(r, c):
        yd, ym, g = r
        return [_sig(g[:, :D_MODEL]) * yd + _sig(g[:, D_MODEL:]) * ym], []

    return _rowwise(fn, [y_dn, y_mla, proj_g], [], [(D_MODEL, BF16)], name="merge_fwd")[0]


def _merge_bwd(y_dn, y_mla, proj_g, d_mixed):
    def fn(r, c):
        yd, ym, g, dm = r
        sd, sm = _sig(g[:, :D_MODEL]), _sig(g[:, D_MODEL:])
        d_g = jnp.concatenate([dm * yd * sd * (1.0 - sd), dm * ym * sm * (1.0 - sm)], axis=1)
        return [d_g, dm * sd, dm * sm], []

    return _rowwise(fn, [y_dn, y_mla, proj_g, d_mixed], [], [(2 * D_MODEL, BF16), (D_MODEL, BF16), (D_MODEL, BF16)],
                    name="merge_bwd")


def _ln_stats(z):
    mu = _rowmean(z)
    zc = z - mu
    r = lax.rsqrt(_rowmean(zc * zc) + EPS_LN)
    return zc * r, r


def _ln_bwd(dy, xh, r, g):
    dxh = dy * g
    return r * (dxh - _rowmean(dxh) - xh * _rowmean(dxh * xh))


def _ln1_fwd(x, a1, g, b):
    def fn(r, c):
        xh, _ = _ln_stats(ALPHA * r[0] + r[1])
        y = xh * c[0] + c[1]
        return [y, y], []

    return _rowwise(fn, [x, a1], [g, b], [(D_MODEL, F32), (D_MODEL, BF16)], name="ln1_fwd")


def _ln1_bwd(x, a1, d_h1, g):
    def fn(r, c):
        xh, rr = _ln_stats(ALPHA * r[0] + r[1])
        dy = r[2]
        dz = _ln_bwd(dy, xh, rr, c[0])
        return [dz, ALPHA * dz], [_colsum(dy * xh), _colsum(dy)]

    return _rowwise(fn, [x, a1, d_h1], [g], [(D_MODEL, BF16), (D_MODEL, F32)], accs=[(1, D_MODEL), (1, D_MODEL)],
                    name="ln1_bwd")


def _act_fwd(gu):
    def fn(r, c):
        gt, up = r[0][:, :FFN_HIDDEN], r[0][:, FFN_HIDDEN:]
        return [gt * _sig(gt) * up], []

    return _rowwise(fn, [gu], [], [(FFN_HIDDEN, BF16)], name="act_fwd")[0]


def _act_bwd(gu, d_act):
    def fn(r, c):
        gt, up = r[0][:, :FFN_HIDDEN], r[0][:, FFN_HIDDEN:]
        da = r[1]
        return [jnp.concatenate([da * up * _silu_grad(gt), da * gt * _sig(gt)], axis=1)], []

    return _rowwise(fn, [gu, d_act], [], [(2 * FFN_HIDDEN, BF16)], name="act_bwd")[0]


def _tail(h1, ffn, pg, pp, tgt, g, b):
    def fn(r, c):
        h1_, ffn_, pg_, pp_, t_ = r
        sp = _sig(pg_)
        xh, rr = _ln_stats(ALPHA * h1_ + ffn_ + sp * pp_)
        y = xh * c[0] + c[1]
        err = y - t_
        dy = err * (1.0 / D_MODEL)
        dz = _ln_bwd(dy, xh, rr, c[0])
        loss = jnp.sum(0.5 * _rowmean(err * err), axis=0, keepdims=True)
        return ([dz, dz * pp_ * sp * (1.0 - sp), dz * sp, ALPHA * dz],
                [_colsum(dy * xh), _colsum(dy), jnp.broadcast_to(loss, (1, LANES))])

    return _rowwise(fn, [h1, ffn, pg, pp, tgt], [g, b], [(D_MODEL, BF16)] * 3 + [(D_MODEL, F32)],
                    accs=[(1, D_MODEL), (1, D_MODEL), (1, LANES)], name="tail")


def _local_step(x, p, pos, tgt, w):
    s_dim = x.shape[0]
    xb, pb = x.astype(BF16), p.astype(BF16)
    proj_a = _mm(xb, w["wa"], name="f_proj_a")
    proj_g = _mm(xb, w["wg"], name="f_proj_g")
    proj_b = _mm(xb, w["wb"], name="f_proj_b")
    qkvn = _conv_fwd(proj_a, w["conv"])
    beta, gc = _gates_fwd(proj_b, w["alog"], w["dtb"])
    gc_t = jnp.transpose(gc[:, :N_HEADS])
    u, w_, qd, kt, a_mat, t_fold = _gdr_prep_fwd(qkvn, beta, gc, gc_t)
    o_dn, states = _gdr_scan_fwd(u, w_, qd, kt, a_mat, gc)
    og = _gdr_out_fwd(o_dn, proj_a, w["dnw"])
    y_dn = _mm(og, w["br_dn"], name="f_y_dn")
    c_q, c_kv = _mla_norm_fwd(proj_b, w["qnw"], w["kvnw"])
    q_full = _mm(c_q, w["uq"], name="f_q_full")
    k_nope = _mm(c_kv, w["uk"], name="f_k_nope")
    vv = _mm(c_kv, w["uv"], out_dtype=BF16, name="f_v")
    qc, kc = _mla_qk_fwd(q_full, k_nope, proj_b, pos)
    o_mla, lse = _attn_fwd(qc, kc, jnp.transpose(vv))
    y_mla = _mm(o_mla, w["br_mla"], name="f_y_mla")
    mixed = _merge_fwd(y_dn, y_mla, proj_g)
    a1 = _mm(mixed, w["wo"], name="f_a1")
    h1, h1b = _ln1_fwd(x, a1, w["ln1g"], w["ln1b"])
    gu = _mm(h1b, w["ffn_in"], name="f_gu")
    act = _act_fwd(gu)
    ffn = _mm(act, w["ffn_out"], name="f_ffn")
    pg = _mm(h1b, w["ple_gate"], name="f_pg")
    pp = _mm(pb, w["ple"], name="f_pp")
    g = {}
    dz2, d_pg, d_pp, dh1a, g["ln2g"], g["ln2b"], loss = _tail(h1, ffn, pg, pp, tgt, w["ln2g"], w["ln2b"])
    g["ple_t"] = _mm(d_pp, pb, ta=True, name="b_w_ple")
    g["ple_gate"] = _mm(h1b, d_pg, ta=True, name="b_w_ple_gate")
    g["ffn_out"] = _mm(act, dz2, ta=True, name="b_w_ffn_out")
    d_act = _mm(dz2, w["ffn_out"], tb=True, name="b_act")
    d_gu = _act_bwd(gu, d_act)
    g["ffn_in_t"] = _mm(d_gu, h1b, ta=True, name="b_w_ffn_in")
    d_h1 = _mm(d_gu, w["ffn_in_t"], add=(dh1a,), name="b_h1_ffn")
    d_h1 = _mm(d_pg, w["ple_gate"], tb=True, add=(d_h1,), name="b_h1_ple")
    dz1, dxa, g["ln1g"], g["ln1b"] = _ln1_bwd(x, a1, d_h1, w["ln1g"])
    g["wo"] = _mm(mixed, dz1, ta=True, name="b_w_o")
    d_mixed = _mm(dz1, w["wo"], tb=True, name="b_mixed")
    d_proj_g, d_y_dn, d_y_mla = _merge_bwd(y_dn, y_mla, proj_g, d_mixed)
    g["br_mla"] = _mm(o_mla, d_y_mla, ta=True, name="b_w_br_mla")
    d_o_mla = _mm(d_y_mla, w["br_mla"], tb=True, name="b_o_mla")
    d_qc, d_kc, d_v = _attn_bwd(qc, kc, jnp.transpose(kc), vv, o_mla, d_o_mla, lse)
    d_q_full, d_kn, d_kr = _mla_qk_bwd(d_qc, d_kc, pos)
    g["uq"] = _mm(c_q, d_q_full, ta=True, name="b_w_uq")
    d_c_q = _mm(d_q_full, w["uq"], tb=True, name="b_c_q")
    g["uk"] = _mm(c_kv, d_kn, ta=True, name="b_w_uk")
    g["uv"] = _mm(c_kv, d_v, ta=True, name="b_w_uv")
    d_c_kv = _mm(d_kn, w["uk"], tb=True, name="b_c_kv_k")
    d_c_kv = _mm(d_v, w["uv"], tb=True, add=(d_c_kv,), name="b_c_kv_v")
    d_cq, d_ckv, g["qnw"], g["kvnw"] = _mla_norm_bwd(proj_b, w["qnw"], w["kvnw"], d_c_q, d_c_kv)
    g["br_dn"] = _mm(og, d_y_dn, ta=True, name="b_w_br_dn")
    d_og = _mm(d_y_dn, w["br_dn"], tb=True, name="b_og")
    d_o_dn, d_z, g["dnw"] = _gdr_out_bwd(o_dn, proj_a, d_og, w["dnw"])
    du, dw, dqd, dkt, d_a, d_egl = _gdr_scan_bwd(u, w_, qd, kt, a_mat, gc, states, d_o_dn)
    dq, dk, dv, d_beta, d_gc = _gdr_prep_bwd(qkvn, beta, gc, gc_t, t_fold, du, dw, dqd, dkt, d_a)
    d_egl_rows = jnp.pad(d_egl[:, None, :, 0], ((0, 0), (CHUNK - 1, 0), (0, LANES - N_HEADS))).reshape(s_dim, LANES)
    d_ba, g["alog"], g["dtb"] = _gates_bwd(proj_b, w["alog"], w["dtb"], gc, d_beta, d_gc, d_egl_rows)
    d_qkv, g["conv"] = _conv_bwd(proj_a, w["conv"], dq, dk, dv)
    zeros = jnp.zeros((s_dim, WB_CKV - Q_LORA), BF16)
    d_proj_b = jnp.concatenate([d_cq, zeros, d_ckv, d_kr, d_ba], axis=1)
    g["wa_qkv_t"] = _mm(d_qkv, xb, ta=True, name="b_w_qkv")
    g["wa_z_t"] = _mm(d_z, xb, ta=True, name="b_w_z")
    g["wg_t"] = _mm(d_proj_g, xb, ta=True, name="b_w_g")
    g["wb_t"] = _mm(d_proj_b, xb, ta=True, name="b_w_b")
    dx = _mm(d_qkv, w["wa_qkv_t"], add=(dxa,), name="b_x_qkv")
    dx = _mm(d_z, w["wa_z_t"], add=(dx,), name="b_x_z")
    dx = _mm(d_proj_g, w["wg_t"], add=(dx,), name="b_x_g")
    dx = _mm(d_proj_b, w["wb_t"], add=(dx,), name="b_x_b")
    return loss, dx, g


_BIG = (("w_in", 1), ("w_uq", 0), ("w_uk", 0), ("w_uv", 0), ("w_br_dn", 0), ("w_br_mla", 0),
        ("w_o", 0), ("w_ffn_in", 1), ("w_ffn_out", 0), ("w_ple", 1), ("w_ple_gate", 0))
_BIG_AXIS = dict(_BIG)
_SMALL = ("ln1_g", "ln1_b", "ln2_g", "ln2_b", "q_norm_w", "kv_norm_w", "dn_norm_w", "dn_a_log", "dn_dt_bias")
_ORDER = ("w_in", "conv_w", "dn_a_log", "dn_dt_bias", "dn_norm_w", "q_norm_w", "w_uq", "kv_norm_w", "w_uk", "w_uv",
          "w_br_dn", "w_br_mla", "w_o", "ln1_g", "ln1_b", "w_ffn_in", "w_ffn_out", "w_ple", "w_ple_gate", "ln2_g",
          "ln2_b")


def _stored_shape(name, shard_shape):
    axis = _BIG_AXIS[name]
    lead = shard_shape[axis]
    return lead, int(np.prod(shard_shape)) // lead


def _to_stored(name, shard):
    return jnp.moveaxis(shard, _BIG_AXIS[name], 0).reshape(_stored_shape(name, shard.shape))


def _from_stored(name, stored, shard_shape):
    axis = _BIG_AXIS[name]
    moved = (shard_shape[axis],) + shard_shape[:axis] + shard_shape[axis + 1:]
    return jnp.moveaxis(stored.reshape(moved), 0, axis)


_W_IN_ROWS = np.cumsum([0, 3072, 1024, 8, 8, Q_LORA, KV_LORA, ROPE, D_MODEL, D_MODEL])


def _full_weights(fw, conv_full, small):
    w_in_t = fw["w_in"]
    r = _W_IN_ROWS
    zr = lambda n: jnp.zeros((n, D_MODEL), w_in_t.dtype)
    w = {}
    w["wa_t"] = w_in_t[r[0]:r[2]]
    w["wa_qkv_t"], w["wa_z_t"] = w_in_t[r[0]:r[1]], w_in_t[r[1]:r[2]]
    w["wg_t"] = w_in_t[r[7]:r[9]]
    w["wb_t"] = jnp.concatenate([w_in_t[r[4]:r[5]], zr(WB_CKV - Q_LORA), w_in_t[r[5]:r[7]], zr(LANES - ROPE),
                                 w_in_t[r[2]:r[4]], zr(LANES - 2 * N_HEADS)], axis=0)
    uq = fw["w_uq"].reshape(Q_LORA, N_HEADS, HEAD + ROPE)
    uq_r = jnp.pad(uq[:, :, HEAD:], ((0, 0), (0, 0), (0, HEAD - ROPE)))
    w["uq"] = jnp.concatenate([uq[:, :, :HEAD].reshape(Q_LORA, -1), uq_r.reshape(Q_LORA, -1)], axis=1)
    w["uk"], w["uv"] = fw["w_uk"], fw["w_uv"]
    w["conv"] = conv_full
    w["br_dn"], w["br_mla"], w["wo"] = fw["w_br_dn"], fw["w_br_mla"], fw["w_o"]
    w["ffn_in_t"], w["ffn_out"] = fw["w_ffn_in"], fw["w_ffn_out"]
    w["ple_t"], w["ple_gate"] = fw["w_ple"], fw["w_ple_gate"]
    for k_ in ("wa", "wg", "wb", "ffn_in", "ple"):
        w[k_] = jnp.transpose(w[k_ + "_t"])
    pad_l = lambda v: jnp.pad(v, ((0, 0), (0, LANES - v.shape[1])))
    w["alog"], w["dtb"] = pad_l(small["dn_a_log"]), pad_l(small["dn_dt_bias"])
    w["dnw"], w["qnw"], w["kvnw"] = small["dn_norm_w"], small["q_norm_w"], small["kv_norm_w"]
    w["ln1g"], w["ln1b"], w["ln2g"], w["ln2b"] = small["ln1_g"], small["ln1_b"], small["ln2_g"], small["ln2_b"]
    return w


def _full_grads(g):
    wb = g["wb_t"]
    full = {}
    full["w_in"] = jnp.concatenate([
        g["wa_qkv_t"], g["wa_z_t"], wb[WB_BA:WB_BA + 2 * N_HEADS], wb[WB_CQ:WB_CQ + Q_LORA],
        wb[WB_CKV:WB_CKV + KV_LORA], wb[WB_KR:WB_KR + ROPE], g["wg_t"]], axis=0)
    uq = g["uq"]
    uq_n = uq[:, :D_MODEL].reshape(Q_LORA, N_HEADS, HEAD)
    uq_r = uq[:, D_MODEL:].reshape(Q_LORA, N_HEADS, HEAD)[:, :, :ROPE]
    full["w_uq"] = jnp.concatenate([uq_n, uq_r], axis=2).reshape(Q_LORA, -1)
    full["w_uk"], full["w_uv"] = g["uk"], g["uv"]
    full["w_br_dn"], full["w_br_mla"], full["w_o"] = g["br_dn"], g["br_mla"], g["wo"]
    full["w_ffn_in"], full["w_ffn_out"] = g["ffn_in_t"], g["ffn_out"]
    full["w_ple"], full["w_ple_gate"] = g["ple_t"], g["ple_gate"]
    small = {"ln1_g": g["ln1g"], "ln1_b": g["ln1b"], "ln2_g": g["ln2g"], "ln2_b": g["ln2b"], "q_norm_w": g["qnw"],
             "kv_norm_w": g["kvnw"], "dn_norm_w": g["dnw"], "dn_a_log": g["alog"], "dn_dt_bias": g["dtb"],
             "conv_w": g["conv"]}
    return full, small


_SMALL_SLOTS = {"ln1_g": (0, 0, 1024), "ln1_b": (1, 0, 1024), "ln2_g": (2, 0, 1024), "ln2_b": (3, 0, 1024),
                "q_norm_w": (4, 0, 384), "kv_norm_w": (4, 384, 256), "dn_norm_w": (4, 640, 128),
                "dn_a_log": (4, 768, 8), "dn_dt_bias": (4, 896, 8)}
_SMALL_ROWS, _LOSS_ROW, _CONV_ROW0, _CONV_ROWS = 24, 5, 8, 12


def _pack_small_grads(small_g, loss):
    zeros = lambda r, c: jnp.zeros((r, c), F32)
    row4 = jnp.concatenate([small_g["q_norm_w"], small_g["kv_norm_w"], small_g["dn_norm_w"], small_g["dn_a_log"],
                            small_g["dn_dt_bias"]], axis=1)
    row5 = jnp.concatenate([loss, zeros(1, FLAT_COLS - LANES)], axis=1)
    head = jnp.concatenate([small_g["ln1_g"], small_g["ln1_b"], small_g["ln2_g"], small_g["ln2_b"], row4, row5,
                            zeros(2, FLAT_COLS)], axis=0)
    conv = small_g["conv_w"].reshape(_CONV_ROWS, FLAT_COLS)
    return jnp.concatenate([head, conv, zeros(_SMALL_ROWS - _CONV_ROW0 - _CONV_ROWS, FLAT_COLS)], axis=0)


_MESH_ID = pl.DeviceIdType.MESH
_ANY = pl.BlockSpec(memory_space=pl.ANY)


def _all_gather(blocks, name):
    n = len(blocks)

    def body(*refs):
        x_refs, out_refs = refs[:n], refs[n:2 * n]
        send_sems, recv_sems, local_sems = refs[2 * n:]
        x, y, c = lax.axis_index("x"), lax.axis_index("y"), lax.axis_index("c")
        me, sibling = (x, y, c), (x, y, 1 - c)
        chips = [(1 - x, y), (x, 1 - y), (1 - x, 1 - y)]

        def slot(i, px, py, pc):
            return out_refs[i].at[4 * px + 2 * py + pc]

        def copy(i, k, origin, to, src=None):
            return pltpu.make_async_remote_copy(
                src_ref=slot(i, *origin) if src is None else src, dst_ref=slot(i, *origin),
                send_sem=send_sems.at[7 * i + k], recv_sem=recv_sems.at[7 * i + k], device_id=to,
                device_id_type=_MESH_ID)

        mine = [pltpu.make_async_copy(x_refs[i], slot(i, *me), local_sems.at[i]) for i in range(n)]
        first, passed = [], []
        for i in range(n):
            mine[i].start()
            first.append(copy(i, 0, me, sibling, src=x_refs[i]))
            first += [copy(i, 1 + j, me, (*chip, c), src=x_refs[i]) for j, chip in enumerate(chips)]
        for cp in first:
            cp.start()
        for i in range(n):
            for j, chip in enumerate(chips):
                copy(i, 1 + j, (*chip, c), me).wait_recv()
                passed.append(copy(i, 4 + j, (*chip, c), sibling))
                passed[-1].start()
        for i in range(n):
            copy(i, 0, sibling, me).wait_recv()
            for j, chip in enumerate(chips):
                copy(i, 4 + j, (*chip, 1 - c), me).wait_recv()
        for cp in first + passed:
            cp.wait_send()
        for cp in mine:
            cp.wait()

    return pl.pallas_call(
        body,
        out_shape=[jax.ShapeDtypeStruct((N_DEV,) + b.shape, b.dtype) for b in blocks],
        in_specs=[_ANY] * n,
        out_specs=[_ANY] * n,
        scratch_shapes=[pltpu.SemaphoreType.DMA((7 * n,)), pltpu.SemaphoreType.DMA((7 * n,)),
                        pltpu.SemaphoreType.DMA((n,))],
        name=name,
    )(*blocks)


def _exchange_sibling(srcs, name):
    n = len(srcs)

    def body(*refs):
        src_refs, dst_refs = refs[:n], refs[n:2 * n]
        send_sems, recv_sems = refs[2 * n:]
        x, y, c = lax.axis_index("x"), lax.axis_index("y"), lax.axis_index("c")
        copies = [pltpu.make_async_remote_copy(
            src_ref=src_refs[i].at[2 * q + (1 - c)], dst_ref=dst_refs[i].at[q], send_sem=send_sems.at[4 * i + q],
            recv_sem=recv_sems.at[4 * i + q], device_id=(x, y, 1 - c), device_id_type=_MESH_ID)
            for i in range(n) for q in range(4)]
        for cp in copies:
            cp.start()
        for cp in copies:
            cp.wait_recv()
        for cp in copies:
            cp.wait_send()

    return pl.pallas_call(
        body,
        out_shape=[jax.ShapeDtypeStruct((4,) + s.shape[1:], s.dtype) for s in srcs],
        in_specs=[_ANY] * n,
        out_specs=[_ANY] * n,
        scratch_shapes=[pltpu.SemaphoreType.DMA((4 * n,)), pltpu.SemaphoreType.DMA((4 * n,))],
        name=name,
    )(*srcs)


def _exchange_chips(srcs, name):
    n = len(srcs)

    def body(*refs):
        src_refs, dst_refs = refs[:n], refs[n:2 * n]
        send_sems, recv_sems = refs[2 * n:]
        x, y, c = lax.axis_index("x"), lax.axis_index("y"), lax.axis_index("c")
        chips = [(1 - x, y), (x, 1 - y), (1 - x, 1 - y)]
        copies = [pltpu.make_async_remote_copy(
            src_ref=src_refs[i].at[2 * tx + ty], dst_ref=dst_refs[i].at[j], send_sem=send_sems.at[3 * i + j],
            recv_sem=recv_sems.at[3 * i + j], device_id=(tx, ty, c), device_id_type=_MESH_ID)
            for i in range(n) for j, (tx, ty) in enumerate(chips)]
        for cp in copies:
            cp.start()
        for cp in copies:
            cp.wait_recv()
        for cp in copies:
            cp.wait_send()

    return pl.pallas_call(
        body,
        out_shape=[jax.ShapeDtypeStruct((3,) + s.shape[1:], s.dtype) for s in srcs],
        in_specs=[_ANY] * n,
        out_specs=[_ANY] * n,
        scratch_shapes=[pltpu.SemaphoreType.DMA((3 * n,)), pltpu.SemaphoreType.DMA((3 * n,))],
        name=name,
    )(*srcs)


def _col_tile(c):
    return c if c <= 256 else 256


def _chip_sum(src, recv, parity, name):
    _, r, c = src.shape
    tc = _col_tile(c)

    def body(par_ref, a_ref, b_ref, o_ref, ob_ref):
        s = a_ref[...] + b_ref[...]
        o_ref[...] = s
        ob_ref[...] = s.astype(BF16)

    blk = lambda f: pl.BlockSpec((None, r, tc), f)
    return pl.pallas_call(
        body,
        out_shape=[jax.ShapeDtypeStruct((4, r, c), F32), jax.ShapeDtypeStruct((4, r, c), BF16)],
        grid_spec=pltpu.PrefetchScalarGridSpec(
            num_scalar_prefetch=1, grid=(4, c // tc),
            in_specs=[blk(lambda q, j, par: (2 * q + par[0], 0, j)), blk(lambda q, j, par: (q, 0, j))],
            out_specs=[blk(lambda q, j, par: (q, 0, j)), blk(lambda q, j, par: (q, 0, j))]),
        compiler_params=pltpu.CompilerParams(dimension_semantics=("parallel", "parallel")),
        name=name,
    )(parity, src, recv)


def _sum_parts(own, others, chip, name):
    _, r, c = own.shape
    tc = _col_tile(c)

    def body(q_ref, a_ref, b_ref, o_ref):
        o_ref[...] = ((a_ref[...] + b_ref[0].astype(F32)) + b_ref[1].astype(F32)) + b_ref[2].astype(F32)

    return pl.pallas_call(
        body,
        out_shape=jax.ShapeDtypeStruct((r, c), F32),
        grid_spec=pltpu.PrefetchScalarGridSpec(
            num_scalar_prefetch=1, grid=(c // tc,),
            in_specs=[pl.BlockSpec((None, r, tc), lambda j, q: (q[0], 0, j)),
                      pl.BlockSpec((3, r, tc), lambda j, q: (0, 0, j))],
            out_specs=pl.BlockSpec((r, tc), lambda j, q: (0, j))),
        compiler_params=pltpu.CompilerParams(dimension_semantics=("parallel",)),
        name=name,
    )(chip, own, others)


def _adamw_math(w, g, m, v):
    m = ADAM_B1 * m + (1.0 - ADAM_B1) * g
    v = ADAM_B2 * v + (1.0 - ADAM_B2) * (g * g)
    m_hat = m / (1.0 - ADAM_B1 ** ADAM_STEP)
    v_hat = v / (1.0 - ADAM_B2 ** ADAM_STEP)
    delta = -ADAM_LR * (m_hat / (jnp.sqrt(v_hat) + ADAM_EPS) + ADAM_WD * w)
    return delta, m, v


def _adamw(w, m, v, g, name):
    r, c = w.shape

    def fn(rows, consts):
        return list(_adamw_math(*rows)), []

    return _rowwise(fn, [w, g, m, v], [], [(c, F32)] * 3, tm=r if r <= 512 else 256, name=name)


def _adamw_small(gathered, params):
    ns = len(_SMALL)

    def body(*refs):
        g_ref, p_refs, o_refs = refs[0], refs[1:1 + 3 * ns], refs[1 + 3 * ns:]
        tot = g_ref[0]
        for k in range(1, N_DEV):
            tot = tot + g_ref[k]
        for i, name in enumerate(_SMALL):
            row, lane0, lanes = _SMALL_SLOTS[name]
            g = tot[row:row + 1, lane0:lane0 + lanes]
            w_, m_, v_ = (p_refs[3 * i + j][...] for j in range(3))
            delta, m2, v2 = _adamw_math(w_, g, m_, v_)
            for j, val in enumerate((g, delta, m2, v2)):
                o_refs[4 * i + j][...] = val
        o_refs[4 * ns][...] = tot[_LOSS_ROW:_LOSS_ROW + 1, 0:LANES]
        o_refs[4 * ns + 1][...] = tot[_CONV_ROW0:_CONV_ROW0 + _CONV_ROWS, :]

    out_shape = [jax.ShapeDtypeStruct(w.shape, F32) for (w, _, _) in params for _ in range(4)]
    out_shape += [jax.ShapeDtypeStruct((1, LANES), F32), jax.ShapeDtypeStruct((_CONV_ROWS, FLAT_COLS), F32)]
    flat = [a for wmv in params for a in wmv]
    return pl.pallas_call(body, out_shape=out_shape, name="adamw_small")(gathered, *flat)


def kernel(x, p, positions, w_in, conv_w, dn_a_log, dn_dt_bias, dn_norm_w, q_norm_w, w_uq, kv_norm_w, w_uk, w_uv, w_br_dn, w_br_mla, w_o, ln1_g, ln1_b, w_ffn_in, w_ffn_out, w_ple, w_ple_gate, ln2_g, ln2_b, loss_target, m_w_in, m_conv_w, m_dn_a_log, m_dn_dt_bias, m_dn_norm_w, m_q_norm_w, m_w_uq, m_kv_norm_w, m_w_uk, m_w_uv, m_w_br_dn, m_w_br_mla, m_w_o, m_ln1_g, m_ln1_b, m_w_ffn_in, m_w_ffn_out, m_w_ple, m_w_ple_gate, m_ln2_g, m_ln2_b, v_w_in, v_conv_w, v_dn_a_log, v_dn_dt_bias, v_dn_norm_w, v_q_norm_w, v_w_uq, v_kv_norm_w, v_w_uk, v_w_uv, v_w_br_dn, v_w_br_mla, v_w_o, v_ln1_g, v_ln1_b, v_w_ffn_in, v_w_ffn_out, v_w_ple, v_w_ple_gate, v_ln2_g, v_ln2_b):
    args = dict(locals())
    wts = {n: args[n] for n in _ORDER}
    mom1 = {n: args["m_" + n] for n in _ORDER}
    mom2 = {n: args["v_" + n] for n in _ORDER}
    big_names = [n for n, _ in _BIG]
    shard_shapes = {n: wts[n].shape[1:] for n in big_names}
    c_idx = lax.axis_index("c")
    q_idx = 2 * lax.axis_index("x") + lax.axis_index("y")
    parity, chip = c_idx.reshape(1).astype(jnp.int32), q_idx.reshape(1).astype(jnp.int32)

    stored = [_to_stored(n, wts[n][0]).astype(BF16) for n in big_names]
    gathered = _all_gather(stored + [conv_w[0]], "ag_weights")
    full_w = {n: t.reshape(-1, t.shape[-1]) for n, t in zip(big_names, gathered)}
    conv_full = jnp.moveaxis(gathered[-1], 0, 1).reshape(conv_w.shape[1], -1)
    small_w = {n: wts[n].astype(F32) for n in _SMALL}
    w = _full_weights(full_w, conv_full, small_w)

    s_dim = x.shape[1]
    loss, dx, g = _local_step(x[0], p[0, 0], positions.reshape(s_dim, 1).astype(F32), loss_target[0], w)
    full_g, small_g = _full_grads(g)

    srcs = [full_g[n].reshape((N_DEV,) + _stored_shape(n, shard_shapes[n])) for n in big_names]
    from_sibling = _exchange_sibling(srcs, "rs_sibling")
    sums = [_chip_sum(s, r, parity, "rs_sum_" + n) for n, s, r in zip(big_names, srcs, from_sibling)]
    from_chips = _exchange_chips([sb for _, sb in sums], "rs_chips")

    out_g, out_d, out_m, out_v = {}, {}, {}, {}

    def update(n, grad, shp):
        flat2 = (shp[0], int(np.prod(shp[1:])))
        d, m2, v2 = _adamw(wts[n][0].reshape(flat2), mom1[n][0].reshape(flat2), mom2[n][0].reshape(flat2),
                           grad.reshape(flat2), "adamw_" + n)
        out_g[n], out_d[n], out_m[n], out_v[n] = grad, d.reshape(shp), m2.reshape(shp), v2.reshape(shp)

    for n, (own, _), others in zip(big_names, sums, from_chips):
        total = _sum_parts(own, others, chip, "rs_total_" + n)
        update(n, _from_stored(n, total, shard_shapes[n]), shard_shapes[n])

    g_small = _all_gather([_pack_small_grads(small_g, loss)], "ag_small")[0]
    res = _adamw_small(g_small, [(wts[n], mom1[n], mom2[n]) for n in _SMALL])
    for i, n in enumerate(_SMALL):
        out_g[n], out_d[n], out_m[n], out_v[n] = res[4 * i:4 * i + 4]
    loss_out = res[4 * len(_SMALL)][0, 0]
    conv_shape = conv_w.shape[1:]
    conv_g = lax.dynamic_slice(res[-1].reshape(conv_shape[0], -1), (0, (2 * q_idx + c_idx) * conv_shape[1]),
                               conv_shape)
    update("conv_w", conv_g, conv_shape)

    expand = lambda d, n: d[n] if n in _SMALL else d[n][None]
    return (loss_out, dx[None], *[expand(out_g, n) for n in _ORDER], *[expand(out_d, n) for n in _ORDER],
            *[expand(out_m, n) for n in _ORDER], *[expand(out_v, n) for n in _ORDER])
```

```python
import functools

import numpy as np
import jax
import jax.numpy as jnp
from jax import lax
from jax.experimental import pallas as pl
from jax.experimental.pallas import tpu as pltpu

F32 = jnp.float32
BF16 = jnp.bfloat16

D_MODEL = 1024
N_HEADS = 8
HEAD = 128
CHUNK = 64
GROUP = 256
ROPE = 64
Q_LORA = 384
KV_LORA = 256
FFN_HIDDEN = 2816
PLE_DIM = 256
ROPE_BASE = 10000.0
ALPHA = 2.0 ** 0.25
SCALE = float((HEAD + ROPE) ** -0.5)
NEG_BIG = -1e30
EPS_RMS = 1e-6
EPS_LN = 1e-5

ADAM_LR = 0.001
ADAM_B1 = 0.9
ADAM_B2 = 0.999
ADAM_EPS = 1e-08
ADAM_WD = 0.01
ADAM_STEP = 10

N_DEV = 8
LANES = 128
FLAT_COLS = 1024

WB_CQ, WB_CKV, WB_KR, WB_BA, WB_COLS = 0, 512, 768, 896, 1024

HIGHEST = lax.Precision.HIGHEST

NN = (((1,), (0,)), ((), ()))
TN = (((0,), (0,)), ((), ()))
NT = (((1,), (1,)), ((), ()))


def _dot(a, b, dims=NN):
    return lax.dot_general(a.astype(BF16), b.astype(BF16), dims, preferred_element_type=F32)


def _dot32(a, b, dims=NN):
    return lax.dot_general(a, b, dims, precision=HIGHEST, preferred_element_type=F32)


def _sig(x):
    return 1.0 / (1.0 + jnp.exp(-x))


MM_TILE = 1536


def _pick_wide(n):
    if n <= MM_TILE:
        return n
    return max(t for t in range(LANES, MM_TILE + 1, LANES) if n % t == 0)


def _split_bf16(a):
    hi = a.astype(BF16)
    return hi, (a - hi.astype(F32)).astype(BF16)


def _dot3(a, b, dims=NN):
    ah, al = a if isinstance(a, tuple) else _split_bf16(a)
    bh, bl = b if isinstance(b, tuple) else _split_bf16(b)
    d = lambda p, q: lax.dot_general(p, q, dims, preferred_element_type=F32)
    return d(ah, bh) + (d(ah, bl) + d(al, bh))


def _mm(a, b, *, ta=False, tb=False, add=(), out_dtype=F32, name):
    if ta:
        k_dim, m_dim = a.shape
    else:
        m_dim, k_dim = a.shape
    if tb:
        n_dim, k2 = b.shape
    else:
        k2, n_dim = b.shape
    assert k_dim == k2, (a.shape, b.shape, ta, tb)
    tm = _pick_wide(m_dim)
    tn = _pick_wide(n_dim)
    tk = _pick_wide(k_dim)
    nk = k_dim // tk
    n_add = len(add)
    dims = TN if ta else (NT if tb else NN)
    assert not (ta and tb)

    def body(a_ref, b_ref, *rest):
        add_refs = rest[:n_add]
        o_ref = rest[n_add]
        acc = rest[n_add + 1]
        k = pl.program_id(2)

        @pl.when(k == 0)
        def _():
            acc[...] = jnp.zeros_like(acc)

        acc[...] += _dot(a_ref[...], b_ref[...], dims)

        @pl.when(k == nk - 1)
        def _():
            r = acc[...]
            for ar in add_refs:
                r = r + ar[...].astype(F32)
            o_ref[...] = r.astype(o_ref.dtype)

    a_spec = pl.BlockSpec((tk, tm), lambda i, j, k: (k, i)) if ta else pl.BlockSpec((tm, tk), lambda i, j, k: (i, k))
    b_spec = pl.BlockSpec((tn, tk), lambda i, j, k: (j, k)) if tb else pl.BlockSpec((tk, tn), lambda i, j, k: (k, j))
    o_spec = pl.BlockSpec((tm, tn), lambda i, j, k: (i, j))
    return pl.pallas_call(
        body,
        out_shape=jax.ShapeDtypeStruct((m_dim, n_dim), out_dtype),
        grid=(m_dim // tm, n_dim // tn, nk),
        in_specs=[a_spec, b_spec] + [o_spec] * n_add,
        out_specs=o_spec,
        scratch_shapes=[pltpu.VMEM((tm, tn), F32)],
        compiler_params=pltpu.CompilerParams(dimension_semantics=("parallel", "parallel", "arbitrary")),
        name=name,
    )(a, b, *add)


def _rowwise(fn, rows, consts, outs, accs=(), *, tm=256, name):
    rows = [r if isinstance(r, tuple) else (r, 0, r.shape[1]) for r in rows]
    s_dim = rows[0][0].shape[0]
    tm = min(tm, s_dim)
    assert s_dim % tm == 0 and all(arr.shape[0] == s_dim for arr, _, _ in rows)
    specs = [pl.BlockSpec((tm, width), functools.partial(lambda i, cb: (i, cb), cb=cb)) for _, cb, width in rows]
    args = [arr for arr, _, _ in rows]
    for c in consts:
        specs.append(pl.BlockSpec(c.shape, lambda i: (0, 0)))
        args.append(c)
    nr, nc, no = len(rows), len(consts), len(outs)
    out_shape = [jax.ShapeDtypeStruct((s_dim, w), dt) for (w, dt) in outs]
    out_specs = [pl.BlockSpec((tm, w), lambda i: (i, 0)) for (w, dt) in outs]
    out_shape += [jax.ShapeDtypeStruct(sh, F32) for sh in accs]
    out_specs += [pl.BlockSpec(sh, lambda i: (0, 0)) for sh in accs]

    def body(*refs):
        r = [x[...] for x in refs[:nr]]
        c = [x[...] for x in refs[nr:nr + nc]]
        o_refs = refs[nr + nc:nr + nc + no]
        a_refs = refs[nr + nc + no:]
        o_vals, a_vals = fn(r, c)
        for ref, v in zip(o_refs, o_vals, strict=True):
            ref[...] = v.astype(ref.dtype)
        if a_refs:
            @pl.when(pl.program_id(0) == 0)
            def _():
                for ref in a_refs:
                    ref[...] = jnp.zeros_like(ref)

            for ref, v in zip(a_refs, a_vals, strict=True):
                ref[...] += v

    res = pl.pallas_call(
        body,
        out_shape=out_shape,
        grid=(s_dim // tm,),
        in_specs=specs,
        out_specs=out_specs,
        compiler_params=pltpu.CompilerParams(dimension_semantics=("arbitrary" if accs else "parallel",)),
        name=name,
    )(*args)
    return res


def _colsum(v):
    return jnp.sum(v, axis=0, keepdims=True)


def _rowsum(v):
    return jnp.sum(v, axis=1, keepdims=True)


def _rowmean(v):
    return jnp.mean(v, axis=1, keepdims=True)


def _silu_grad(x):
    s = _sig(x)
    return s * (1.0 + x * (1.0 - s))


def _conv_taps(x, w, width=4):
    row = lax.broadcasted_iota(jnp.int32, x.shape, 0)
    c = x * w[width - 1:width, :]
    for s in range(1, width):
        c = c + jnp.where(row >= s, pltpu.roll(x, s, 0), 0.0) * w[width - 1 - s:width - s, :]
    return c


def _conv_fwd(proj_a, conv_w):
    s_dim = proj_a.shape[0]
    n_blk = 3 * N_HEADS

    def body(x_ref, w_ref, o_ref):
        j = pl.program_id(0)
        c = _conv_taps(x_ref[...], w_ref[...])
        y = c * _sig(c)
        r = lax.rsqrt(_rowsum(y * y) + EPS_RMS)
        fac = jnp.where(j < N_HEADS, r * (HEAD ** -0.5), jnp.where(j < 2 * N_HEADS, r, 1.0))
        o_ref[...] = y * fac

    return pl.pallas_call(
        body,
        out_shape=jax.ShapeDtypeStruct((s_dim, n_blk * HEAD), F32),
        grid=(n_blk,),
        in_specs=[pl.BlockSpec((s_dim, HEAD), lambda j: (0, j)), pl.BlockSpec((4, HEAD), lambda j: (0, j))],
        out_specs=pl.BlockSpec((s_dim, HEAD), lambda j: (0, j)),
        compiler_params=pltpu.CompilerParams(dimension_semantics=("parallel",)),
        name="conv_fwd",
    )(proj_a, conv_w)


def _conv_bwd(proj_a, conv_w, dq, dk, dv):
    s_dim = proj_a.shape[0]
    n_blk = 3 * N_HEADS

    def body(x_ref, w_ref, dq_ref, dk_ref, dv_ref, dx_ref, dw_ref):
        j = pl.program_id(0)
        x = x_ref[...]
        w = w_ref[...]
        do = jnp.where(j < N_HEADS, dq_ref[...], jnp.where(j < 2 * N_HEADS, dk_ref[...], dv_ref[...]))
        c = _conv_taps(x, w)
        sg = _sig(c)
        y = c * sg
        r = lax.rsqrt(_rowsum(y * y) + EPS_RMS)
        sc = jnp.where(j < N_HEADS, HEAD ** -0.5, 1.0)
        dy_n = sc * (r * do - y * (r * r * r) * _rowsum(do * y))
        dy = jnp.where(j < 2 * N_HEADS, dy_n, do)
        dc = dy * (sg * (1.0 + c * (1.0 - sg)))
        row = lax.broadcasted_iota(jnp.int32, x.shape, 0)
        dx = dc * w[3:4, :]
        dw_ref[3:4, :] = _colsum(dc * x)
        for s in range(1, 4):
            dx = dx + jnp.where(row < s_dim - s, pltpu.roll(dc, s_dim - s, 0), 0.0) * w[3 - s:4 - s, :]
            xs = jnp.where(row >= s, pltpu.roll(x, s, 0), 0.0)
            dw_ref[3 - s:4 - s, :] = _colsum(dc * xs)
        dx_ref[...] = dx.astype(dx_ref.dtype)

    hd = N_HEADS - 1
    return pl.pallas_call(
        body,
        out_shape=[jax.ShapeDtypeStruct((s_dim, n_blk * HEAD), BF16), jax.ShapeDtypeStruct((4, n_blk * HEAD), F32)],
        grid=(n_blk,),
        in_specs=[
            pl.BlockSpec((s_dim, HEAD), lambda j: (0, j)),
            pl.BlockSpec((4, HEAD), lambda j: (0, j)),
            pl.BlockSpec((s_dim, HEAD), lambda j: (0, jnp.minimum(j, hd))),
            pl.BlockSpec((s_dim, HEAD), lambda j: (0, jnp.clip(j - N_HEADS, 0, hd))),
            pl.BlockSpec((s_dim, HEAD), lambda j: (0, jnp.clip(j - 2 * N_HEADS, 0, hd))),
        ],
        out_specs=[pl.BlockSpec((s_dim, HEAD), lambda j: (0, j)), pl.BlockSpec((4, HEAD), lambda j: (0, j))],
        compiler_params=pltpu.CompilerParams(dimension_semantics=("parallel",)),
        name="conv_bwd",
    )(proj_a, conv_w, dq, dk, dv)


def _chunk_tri(n):
    r = np.arange(n)
    m = ((r[:, None] // CHUNK) == (r[None, :] // CHUNK)) & (r[:, None] >= r[None, :])
    m = m.astype(np.float32)
    return jnp.asarray(m), jnp.asarray(m.T)


def _softplus(z):
    return jnp.maximum(z, 0.0) + jnp.log(1.0 + jnp.exp(-jnp.abs(z)))


def _gates_fwd(proj_b, alog, dtb):
    tm = min(GROUP, proj_b.shape[0])
    tri, _ = _chunk_tri(tm)

    def fn(r, c):
        b = r[0]
        a = pltpu.roll(b, LANES - N_HEADS, 1)
        alog_, dtb_, tri_ = c
        g = -jnp.exp(alog_) * _softplus(a + dtb_)
        return [_sig(b), _dot32(tri_, g)], []

    return _rowwise(fn, [(proj_b, WB_BA // LANES, LANES)], [alog, dtb, tri],
                    [(LANES, F32), (LANES, F32)], tm=tm, name="gates_fwd")


def _gates_bwd(proj_b, alog, dtb, gc, d_beta, d_gc, d_egl_rows):
    tm = min(GROUP, proj_b.shape[0])
    _, tri_t = _chunk_tri(tm)

    def fn(r, c):
        b, gc_, d_beta_, d_gc_, d_egl_ = r
        a = pltpu.roll(b, LANES - N_HEADS, 1)
        alog_, dtb_, tri_t_ = c
        z = a + dtb_
        ea = jnp.exp(alog_)
        g = -ea * _softplus(z)
        dg = _dot32(tri_t_, d_gc_ + d_egl_ * jnp.exp(gc_))
        d_a = dg * (-ea) * _sig(z)
        beta = _sig(b)
        d_ba = d_beta_ * beta * (1.0 - beta) + pltpu.roll(d_a, N_HEADS, 1)
        return [d_ba], [_colsum(dg * g), _colsum(d_a)]

    return _rowwise(fn, [(proj_b, WB_BA // LANES, LANES), gc, d_beta, d_gc, d_egl_rows],
                    [alog, dtb, tri_t], [(LANES, BF16)], accs=[(1, LANES), (1, LANES)], tm=tm,
                    name="gates_bwd")


def _group_masks(n):
    r = lax.broadcasted_iota(jnp.int32, (n, n), 0)
    c = lax.broadcasted_iota(jnp.int32, (n, n), 1)
    same = (r // CHUNK) == (c // CHUNK)
    tril = jnp.logical_and(same, r >= c)
    strict = jnp.logical_and(same, r > c)
    last = c == (r // CHUNK) * CHUNK + (CHUNK - 1)
    eye = r == c
    return same, tril, strict, last, eye


def _inv_unit_lower(l_mat, eye_f):
    q = -l_mat
    r = eye_f + q
    qs = _split_bf16(q)
    for _ in range(5):
        qs = _split_bf16(_dot3(qs, qs))
        r = r + _dot3(r, qs)
    return r


def _unfold_blocks(folded, mask):
    n = folded.shape[0]
    return jnp.where(mask, jnp.concatenate([folded] * (n // CHUNK), axis=1), 0.0)


def _head_cols(beta, gc, gc_t, h):
    lane = lax.broadcasted_iota(jnp.int32, beta.shape, 1)
    sub = lax.broadcasted_iota(jnp.int32, gc_t.shape, 0)
    bcol = _rowsum(jnp.where(lane == h, beta, 0.0))
    gcol = _rowsum(jnp.where(lane == h, gc, 0.0))
    grow = _colsum(jnp.where(sub == h, gc_t, 0.0))
    return bcol, gcol, grow


def _prep_common(q, k, bcol, gcol, grow, t_folded=None):
    n = q.shape[0]
    same, tril, strict, last, eye = _group_masks(n)
    decay = jnp.where(tril, jnp.exp(jnp.where(tril, gcol - grow, 0.0)), 0.0)
    glast = _rowsum(jnp.where(last, jnp.broadcast_to(grow, (n, n)), 0.0))
    e = jnp.exp(gcol)
    ekt = jnp.exp(glast - gcol)
    kb = k * bcol
    kk = _dot(kb, k, NT)
    if t_folded is None:
        t_mat = _inv_unit_lower(jnp.where(strict, kk * decay, 0.0), eye.astype(F32))
    else:
        t_mat = _unfold_blocks(t_folded, same)
    qk = _dot(q, k, NT)
    return dict(same=same, tril=tril, strict=strict, last=last, eye=eye, decay=decay, e=e, ekt=ekt, kb=kb, kk=kk,
                t=t_mat, qk=qk)


def _fold_blocks(m):
    n = m.shape[0]
    out = m[:, 0:CHUNK]
    for b in range(1, n // CHUNK):
        out = out + m[:, b * CHUNK:(b + 1) * CHUNK]
    return out


def _gdr_prep_fwd(qkvn, beta, gc, gc_t):
    s_dim = qkvn.shape[0]
    tg = min(GROUP, s_dim)

    def body(q_ref, k_ref, v_ref, b_ref, g_ref, gt_ref, u_ref, w_ref, qd_ref, kt_ref, a_ref, t_ref):
        h = pl.program_id(0)
        q, k, v = q_ref[...], k_ref[...], v_ref[...]
        bcol, gcol, grow = _head_cols(b_ref[...], g_ref[...], gt_ref[...], h)
        p = _prep_common(q, k, bcol, gcol, grow)
        u_ref[...] = _dot(p["t"], v * bcol)
        w_ref[...] = _dot(p["t"], p["kb"] * p["e"])
        qd_ref[...] = q * p["e"]
        kt_ref[...] = k * p["ekt"]
        a_ref[...] = _fold_blocks(jnp.where(p["tril"], p["qk"] * p["decay"], 0.0))
        t_ref[...] = _fold_blocks(p["t"])

    row = lambda off: pl.BlockSpec((tg, HEAD), functools.partial(lambda h, m, off: (m, h + off), off=off))
    full = pl.BlockSpec((tg, LANES), lambda h, m: (m, 0))
    o_spec = pl.BlockSpec((tg, HEAD), lambda h, m: (m, h))
    a_spec = pl.BlockSpec((None, tg, CHUNK), lambda h, m: (h, m, 0))
    wide = jax.ShapeDtypeStruct((s_dim, N_HEADS * HEAD), F32)
    folded = jax.ShapeDtypeStruct((N_HEADS, s_dim, CHUNK), F32)
    return pl.pallas_call(
        body,
        out_shape=[wide, wide, wide, wide, folded, folded],
        grid=(N_HEADS, s_dim // tg),
        in_specs=[row(0), row(N_HEADS), row(2 * N_HEADS), full, full, pl.BlockSpec((8, tg), lambda h, m: (0, m))],
        out_specs=[o_spec, o_spec, o_spec, o_spec, a_spec, a_spec],
        compiler_params=pltpu.CompilerParams(dimension_semantics=("parallel", "parallel")),
        name="gdr_prep_fwd",
    )(qkvn, qkvn, qkvn, beta, gc, gc_t)


def _gdr_prep_bwd(qkvn, beta, gc, gc_t, t_fold, du, dw, dqd, dkt, d_a):
    s_dim = qkvn.shape[0]
    tg = min(GROUP, s_dim)

    def body(q_ref, k_ref, v_ref, b_ref, g_ref, gt_ref, t_ref, du_ref, dw_ref, dqd_ref, dkt_ref, da_ref,
             dq_ref, dk_ref, dv_ref, db_ref, dg_ref):
        h = pl.program_id(1)
        q, k, v = q_ref[...], k_ref[...], v_ref[...]
        bcol, gcol, grow = _head_cols(b_ref[...], g_ref[...], gt_ref[...], h)
        p = _prep_common(q, k, bcol, gcol, grow, t_ref[...])
        t_mat, decay, e, ekt, kb = p["t"], p["decay"], p["e"], p["ekt"], p["kb"]
        du_, dw_, dqd_, dkt_ = du_ref[...], dw_ref[...], dqd_ref[...], dkt_ref[...]
        vb = v * bcol
        kbe = kb * e
        d_t = _dot(du_, vb, NT) + _dot(dw_, kbe, NT)
        dvb = _dot(t_mat, du_, TN)
        dkbe = _dot(t_mat, dw_, TN)
        ts = _split_bf16(t_mat)
        d_l = -_dot3(_dot3(ts, d_t, TN), ts, NT)
        m1 = jnp.where(p["strict"], d_l, 0.0)
        m2 = _unfold_blocks(da_ref[...], p["tril"])
        d_kk = m1 * decay
        d_qk = m2 * decay
        d_decay = m1 * p["kk"] + m2 * p["qk"]
        dkb = _dot(d_kk, k) + dkbe * e
        dk = _dot(d_kk, kb, TN) + _dot(d_qk, q, TN) + dkt_ * ekt + dkb * bcol
        dq = _dot(d_qk, k) + dqd_ * e
        d_beta = _rowsum(dkb * k) + _rowsum(dvb * v)
        d_e = _rowsum(dkbe * kb) + _rowsum(dqd_ * q)
        d_ekt = _rowsum(dkt_ * k) * ekt
        d_diff = d_decay * decay
        d_grow = -_colsum(d_diff) + _colsum(jnp.where(p["last"], jnp.broadcast_to(d_ekt, (tg, tg)), 0.0))
        d_gcol = d_e * e - d_ekt + _rowsum(d_diff)
        d_gcol = d_gcol + _rowsum(jnp.where(p["eye"], jnp.broadcast_to(d_grow, (tg, tg)), 0.0))
        dq_ref[...] = dq
        dk_ref[...] = dk
        dv_ref[...] = dvb * bcol

        @pl.when(h == 0)
        def _():
            db_ref[...] = jnp.zeros_like(db_ref)
            dg_ref[...] = jnp.zeros_like(dg_ref)

        lane = lax.broadcasted_iota(jnp.int32, (tg, LANES), 1)
        db_ref[...] = jnp.where(lane == h, d_beta, db_ref[...])
        dg_ref[...] = jnp.where(lane == h, d_gcol, dg_ref[...])

    row = lambda off: pl.BlockSpec((tg, HEAD), functools.partial(lambda m, h, off: (m, h + off), off=off))
    full = pl.BlockSpec((tg, LANES), lambda m, h: (m, 0))
    o_spec = pl.BlockSpec((tg, HEAD), lambda m, h: (m, h))
    a_spec = pl.BlockSpec((None, tg, CHUNK), lambda m, h: (h, m, 0))
    wide = jax.ShapeDtypeStruct((s_dim, N_HEADS * HEAD), F32)
    lanes = jax.ShapeDtypeStruct((s_dim, LANES), F32)
    return pl.pallas_call(
        body,
        out_shape=[wide, wide, wide, lanes, lanes],
        grid=(s_dim // tg, N_HEADS),
        in_specs=[row(0), row(N_HEADS), row(2 * N_HEADS), full, full, pl.BlockSpec((8, tg), lambda m, h: (0, m)),
                  a_spec, o_spec, o_spec, o_spec, o_spec, a_spec],
        out_specs=[o_spec, o_spec, o_spec, full, full],
        compiler_params=pltpu.CompilerParams(dimension_semantics=("parallel", "arbitrary")),
        name="gdr_prep_bwd",
    )(qkvn, qkvn, qkvn, beta, gc, gc_t, t_fold, du, dw, dqd, dkt, d_a)


def _gdr_scan_fwd(u, w, qd, kt, a_mat, gc):
    s_dim = u.shape[0]
    n_chunks = s_dim // CHUNK

    def body(u_ref, w_ref, qd_ref, kt_ref, a_ref, g_ref, o_ref, st_ref, state):
        @pl.when(pl.program_id(0) == 0)
        def _():
            state[...] = jnp.zeros_like(state)

        egl = jnp.exp(g_ref[CHUNK - 1:CHUNK, :])
        for h in range(N_HEADS):
            cs = slice(h * HEAD, (h + 1) * HEAD)
            s_h = state[h]
            st_ref[h] = s_h
            vn = u_ref[:, cs] - _dot(w_ref[:, cs], s_h)
            o_ref[:, cs] = _dot(qd_ref[:, cs], s_h) + _dot(a_ref[h], vn)
            state[h] = s_h * egl[:, h:h + 1] + _dot(kt_ref[:, cs], vn, TN)

    wide = pl.BlockSpec((CHUNK, N_HEADS * HEAD), lambda n: (n, 0))
    return pl.pallas_call(
        body,
        out_shape=[jax.ShapeDtypeStruct((s_dim, N_HEADS * HEAD), F32),
                   jax.ShapeDtypeStruct((n_chunks, N_HEADS, HEAD, HEAD), F32)],
        grid=(n_chunks,),
        in_specs=[wide, wide, wide, wide, pl.BlockSpec((N_HEADS, CHUNK, CHUNK), lambda n: (0, n, 0)),
                  pl.BlockSpec((CHUNK, LANES), lambda n: (n, 0))],
        out_specs=[wide, pl.BlockSpec((None, N_HEADS, HEAD, HEAD), lambda n: (n, 0, 0, 0))],
        scratch_shapes=[pltpu.VMEM((N_HEADS, HEAD, HEAD), F32)],
        compiler_params=pltpu.CompilerParams(dimension_semantics=("arbitrary",)),
        name="gdr_scan_fwd",
    )(u, w, qd, kt, a_mat, gc)


def _gdr_scan_bwd(u, w, qd, kt, a_mat, gc, states, d_o):
    s_dim = u.shape[0]
    n_chunks = s_dim // CHUNK
    last = n_chunks - 1

    def body(u_ref, w_ref, qd_ref, kt_ref, a_ref, g_ref, st_ref, do_ref,
             du_ref, dw_ref, dqd_ref, dkt_ref, da_ref, de_ref, d_state):
        @pl.when(pl.program_id(0) == 0)
        def _():
            d_state[...] = jnp.zeros_like(d_state)

        egl = jnp.exp(g_ref[CHUNK - 1:CHUNK, :])
        for h in range(N_HEADS):
            cs = slice(h * HEAD, (h + 1) * HEAD)
            s_h = st_ref[h]
            ds_n = d_state[h]
            do = do_ref[:, cs]
            w_h = w_ref[:, cs]
            vn = u_ref[:, cs] - _dot(w_h, s_h)
            dvn = _dot(a_ref[h], do, TN) + _dot(kt_ref[:, cs], ds_n)
            dqd_ref[:, cs] = _dot(do, s_h, NT)
            da_ref[h] = _dot(do, vn, NT)
            dkt_ref[:, cs] = _dot(vn, ds_n, NT)
            de = jnp.sum(_rowsum(ds_n * s_h), axis=0, keepdims=True)
            de_ref[h:h + 1, :] = jnp.broadcast_to(de, (1, LANES))
            du_ref[:, cs] = dvn
            dw_ref[:, cs] = -_dot(dvn, s_h, NT)
            d_state[h] = ds_n * egl[:, h:h + 1] + _dot(qd_ref[:, cs], do, TN) - _dot(w_h, dvn, TN)

    wide = pl.BlockSpec((CHUNK, N_HEADS * HEAD), lambda n: (last - n, 0))
    a_spec = pl.BlockSpec((N_HEADS, CHUNK, CHUNK), lambda n: (0, last - n, 0))
    wide_shape = jax.ShapeDtypeStruct((s_dim, N_HEADS * HEAD), F32)
    return pl.pallas_call(
        body,
        out_shape=[wide_shape, wide_shape, wide_shape, wide_shape,
                   jax.ShapeDtypeStruct((N_HEADS, s_dim, CHUNK), F32),
                   jax.ShapeDtypeStruct((n_chunks, N_HEADS, LANES), F32)],
        grid=(n_chunks,),
        in_specs=[wide, wide, wide, wide, a_spec, pl.BlockSpec((CHUNK, LANES), lambda n: (last - n, 0)),
                  pl.BlockSpec((None, N_HEADS, HEAD, HEAD), lambda n: (last - n, 0, 0, 0)), wide],
        out_specs=[wide, wide, wide, wide, a_spec, pl.BlockSpec((None, N_HEADS, LANES), lambda n: (last - n, 0, 0))],
        scratch_shapes=[pltpu.VMEM((N_HEADS, HEAD, HEAD), F32)],
        compiler_params=pltpu.CompilerParams(dimension_semantics=("arbitrary",)),
        name="gdr_scan_bwd",
    )(u, w, qd, kt, a_mat, gc, states, d_o)


def _gdr_out_fwd(o_dn, proj_a, dn_w):
    def fn(r, c):
        o, z = r
        (w_,) = c
        outs = []
        for h in range(N_HEADS):
            cs = slice(h * HEAD, (h + 1) * HEAD)
            oh, zh = o[:, cs], z[:, cs]
            rr = lax.rsqrt(_rowmean(oh * oh) + EPS_RMS)
            outs.append(oh * rr * w_ * (zh * _sig(zh)))
        return [jnp.concatenate(outs, axis=1)], []

    return _rowwise(fn, [o_dn, (proj_a, 3, D_MODEL)], [dn_w], [(D_MODEL, BF16)], name="gdr_out_fwd")[0]


def _gdr_out_bwd(o_dn, proj_a, d_og, dn_w):
    def fn(r, c):
        o, z, dg = r
        (w_,) = c
        d_o, d_z = [], []
        d_w = jnp.zeros((1, HEAD), F32)
        for h in range(N_HEADS):
            cs = slice(h * HEAD, (h + 1) * HEAD)
            oh, zh, dgh = o[:, cs], z[:, cs], dg[:, cs]
            rr = lax.rsqrt(_rowmean(oh * oh) + EPS_RMS)
            sz = zh * _sig(zh)
            d_n = dgh * sz
            d_z.append(dgh * (oh * rr * w_) * _silu_grad(zh))
            d_w = d_w + _colsum(d_n * oh * rr)
            gw = d_n * w_
            d_o.append(rr * gw - oh * (rr * rr * rr) * _rowmean(gw * oh))
        return [jnp.concatenate(d_o, axis=1), jnp.concatenate(d_z, axis=1)], [d_w]

    return _rowwise(fn, [o_dn, (proj_a, 3, D_MODEL), d_og], [dn_w], [(D_MODEL, F32), (D_MODEL, BF16)],
                    accs=[(1, HEAD)], name="gdr_out_bwd")


def _rms_fwd(x, w):
    r = lax.rsqrt(_rowmean(x * x) + EPS_RMS)
    return x * r * w


def _rms_bwd(x, w, dy):
    r = lax.rsqrt(_rowmean(x * x) + EPS_RMS)
    gw = dy * w
    return r * gw - x * (r * r * r) * _rowmean(gw * x), _colsum(dy * x * r)


def _mla_norm_fwd(proj_b, qn_w, kvn_w):
    def fn(r, c):
        return [_rms_fwd(r[0], c[0]), _rms_fwd(r[1], c[1])], []

    return _rowwise(fn, [(proj_b, WB_CQ // Q_LORA, Q_LORA), (proj_b, WB_CKV // KV_LORA, KV_LORA)], [qn_w, kvn_w],
                    [(Q_LORA, BF16), (KV_LORA, BF16)], name="mla_norm_fwd")


def _mla_norm_bwd(proj_b, qn_w, kvn_w, d_cq, d_ckv):
    def fn(r, c):
        dx1, dw1 = _rms_bwd(r[0], c[0], r[2])
        dx2, dw2 = _rms_bwd(r[1], c[1], r[3])
        return [dx1, dx2], [dw1, dw2]

    return _rowwise(fn, [(proj_b, WB_CQ // Q_LORA, Q_LORA), (proj_b, WB_CKV // KV_LORA, KV_LORA), d_cq, d_ckv],
                    [qn_w, kvn_w], [(Q_LORA, BF16), (KV_LORA, BF16)], accs=[(1, Q_LORA), (1, KV_LORA)],
                    name="mla_norm_bwd")


def _rope_consts():
    inv = ROPE_BASE ** (-np.arange(0, ROPE, 2, dtype=np.float32) / ROPE)
    t = np.zeros((4, LANES), np.float32)
    t[0, :32] = inv
    t[0, 32:64] = inv
    t[1, :64] = 1.0
    t[2, 32:64] = 1.0
    t[3, :32] = -1.0
    return jnp.asarray(t)


def _rope_tables(pos, consts, width):
    ang = pos * consts[0:1, :]
    cosv, sinv = jnp.cos(ang), jnp.sin(ang)
    reps = width // LANES
    tile = (lambda t: jnp.concatenate([t] * reps, axis=1)) if reps > 1 else (lambda t: t)
    return tile(cosv * consts[1:2, :]), tile(sinv * consts[2:3, :]), tile(sinv * consts[3:4, :])


def _rope_apply(t, tabs):
    cos_t, sin_a, sin_b = tabs
    width = t.shape[1]
    return t * cos_t + pltpu.roll(t, 32, 1) * sin_a + pltpu.roll(t, width - 32, 1) * sin_b


def _rope_transpose(d, tabs):
    cos_t, sin_a, sin_b = tabs
    width = d.shape[1]
    return d * cos_t + pltpu.roll(d * sin_a, width - 32, 1) + pltpu.roll(d * sin_b, 32, 1)


QK_HEAD = 2 * HEAD


def _interleave_heads(a, b):
    parts = []
    for h in range(N_HEADS):
        parts.append(a[:, h * HEAD:(h + 1) * HEAD])
        parts.append(b if b.shape[1] == LANES else b[:, h * LANES:(h + 1) * LANES])
    return jnp.concatenate(parts, axis=1)


def _mla_qk_fwd(q_full, k_nope, proj_b, pos):
    consts = _rope_consts()

    def fn(r, c):
        qf, kn, kr, pos_ = r
        qn, qr = qf[:, :D_MODEL], qf[:, D_MODEL:]
        qr = _rope_apply(qr, _rope_tables(pos_, c[0], D_MODEL))
        kr = _rope_apply(kr, _rope_tables(pos_, c[0], LANES))
        return [_interleave_heads(qn, qr) * SCALE, _interleave_heads(kn, kr)], []

    return _rowwise(fn, [q_full, k_nope, (proj_b, WB_KR // LANES, LANES), pos], [consts],
                    [(N_HEADS * QK_HEAD, BF16), (N_HEADS * QK_HEAD, BF16)], name="mla_qk_fwd")


def _mla_qk_bwd(d_qc, d_kc, pos):
    consts = _rope_consts()

    def fn(r, c):
        dq, dk, pos_ = r
        even = lambda t: jnp.concatenate([t[:, (2 * h) * LANES:(2 * h + 1) * LANES] for h in range(N_HEADS)], axis=1)
        odd = lambda t: jnp.concatenate([t[:, (2 * h + 1) * LANES:(2 * h + 2) * LANES] for h in range(N_HEADS)], axis=1)
        d_qr_raw = _rope_transpose(odd(dq), _rope_tables(pos_, c[0], D_MODEL)) * SCALE
        dkr = dk[:, LANES:2 * LANES]
        for h in range(1, N_HEADS):
            dkr = dkr + dk[:, (2 * h + 1) * LANES:(2 * h + 2) * LANES]
        return [jnp.concatenate([even(dq) * SCALE, d_qr_raw], axis=1), even(dk),
                _rope_transpose(dkr, _rope_tables(pos_, c[0], LANES))], []

    return _rowwise(fn, [d_qc, d_kc, pos], [consts], [(2 * D_MODEL, BF16), (D_MODEL, BF16), (LANES, BF16)],
                    name="mla_qk_bwd")


def _causal_mask_t(st, key0, query0):
    key = lax.broadcasted_iota(jnp.int32, st.shape, 0) + key0
    query = lax.broadcasted_iota(jnp.int32, st.shape, 1) + query0
    return jnp.where(key <= query, st, NEG_BIG)


def _attn_tiles(s_dim):
    tq = min(512, s_dim)
    n_chains = 2 if s_dim >= 2 * tq else 1
    return tq, n_chains, min(512, s_dim)


def _attn_fwd(qc, kc, vt):
    s_dim = qc.shape[0]
    tq, n_chains, tk = _attn_tiles(s_dim)
    tqs = tq * n_chains

    def body(q_ref, k_ref, vt_ref, o_ref, lse_ref, m_s, l_s, acc):
        qi = pl.program_id(1)
        m_s[...] = jnp.full_like(m_s, NEG_BIG)
        l_s[...] = jnp.zeros_like(l_s)
        acc[...] = jnp.zeros_like(acc)

        def step(j, carry):
            ks = pl.multiple_of(j * tk, tk)
            kb, vtb = k_ref[pl.ds(ks, tk), :], vt_ref[:, pl.ds(ks, tk)]
            for c in range(n_chains):
                cols = slice(c * tq, (c + 1) * tq)
                st = _causal_mask_t(_dot(kb, q_ref[cols, :], NT), j * tk, qi * tqs + c * tq)
                m_prev = m_s[:, cols]
                m_new = jnp.maximum(m_prev, jnp.max(st, axis=0, keepdims=True))
                alpha = jnp.exp(m_prev - m_new)
                pt = jnp.exp(st - m_new)
                l_s[:, cols] = alpha * l_s[:, cols] + _colsum(pt)
                m_s[:, cols] = m_new
                acc[:, cols] = acc[:, cols] * alpha + _dot(vtb, pt)
            return carry

        lax.fori_loop(0, (qi + 1) * (tqs // tk), step, 0)
        l = l_s[...]
        o_ref[...] = jnp.transpose(acc[...] / l)
        lse_ref[...] = m_s[...] + jnp.log(l)

    return pl.pallas_call(
        body,
        out_shape=[jax.ShapeDtypeStruct((s_dim, N_HEADS * HEAD), F32), jax.ShapeDtypeStruct((N_HEADS, 1, s_dim), F32)],
        grid=(N_HEADS, s_dim // tqs),
        in_specs=[pl.BlockSpec((tqs, QK_HEAD), lambda h, qi: (qi, h)),
                  pl.BlockSpec((s_dim, QK_HEAD), lambda h, qi: (0, h)),
                  pl.BlockSpec((HEAD, s_dim), lambda h, qi: (h, 0))],
        out_specs=[pl.BlockSpec((tqs, HEAD), lambda h, qi: (qi, h)),
                   pl.BlockSpec((None, 1, tqs), lambda h, qi: (h, 0, qi))],
        scratch_shapes=[pltpu.VMEM((1, tqs), F32), pltpu.VMEM((1, tqs), F32), pltpu.VMEM((HEAD, tqs), F32)],
        compiler_params=pltpu.CompilerParams(dimension_semantics=("parallel", "parallel")),
        name="attn_fwd",
    )(qc, kc, vt)


def _attn_bwd(qc, kc, kct, v, o, d_o, lse):
    s_dim = qc.shape[0]
    tq, n_chains, tk = _attn_tiles(s_dim)
    tqs = tq * n_chains

    def body(q_ref, k_ref, kt_ref, v_ref, o_ref, do_ref, lse_ref, dq_ref, dk_ref, dv_ref, dqt_acc, dv_acc):
        qi = pl.program_id(1)

        @pl.when(qi == 0)
        def _():
            dk_ref[...] = jnp.zeros_like(dk_ref)
            dv_acc[...] = jnp.zeros_like(dv_acc)

        dqt_acc[...] = jnp.zeros_like(dqt_acc)
        do_f = do_ref[...]
        do_all = do_f.astype(BF16)
        q_all = q_ref[...]
        lse_row = lse_ref[...]
        delta_row = _dot3(jnp.ones((8, HEAD), F32), o_ref[...] * do_f, NT)[0:1, :]

        def step(j, carry):
            ks = pl.multiple_of(j * tk, tk)
            kb, vb, ktb = k_ref[pl.ds(ks, tk), :], v_ref[pl.ds(ks, tk), :], kt_ref[:, pl.ds(ks, tk)]
            pts, dsts = [], []
            for c in range(n_chains):
                cols = slice(c * tq, (c + 1) * tq)
                st = _causal_mask_t(_dot(kb, q_all[cols, :], NT), j * tk, qi * tqs + c * tq)
                pt = jnp.exp(st - lse_row[:, cols])
                dst = pt * (_dot(vb, do_all[cols, :], NT) - delta_row[:, cols])
                dst_b = dst.astype(BF16)
                dqt_acc[:, cols] += _dot(ktb, dst_b)
                pts.append(pt.astype(BF16))
                dsts.append(dst_b)
            pt_all = jnp.concatenate(pts, axis=1) if n_chains > 1 else pts[0]
            dst_all = jnp.concatenate(dsts, axis=1) if n_chains > 1 else dsts[0]
            dk_ref[pl.ds(ks, tk), :] += _dot(dst_all, q_all)
            dv_acc[pl.ds(ks, tk), :] += _dot(pt_all, do_all)
            return carry

        lax.fori_loop(0, (qi + 1) * (tqs // tk), step, 0)
        dq_ref[...] = jnp.transpose(dqt_acc[...])

        @pl.when(qi == s_dim // tqs - 1)
        def _():
            dv_ref[...] = dv_acc[...].astype(dv_ref.dtype)

    q_spec = pl.BlockSpec((tqs, QK_HEAD), lambda h, qi: (qi, h))
    o_spec = pl.BlockSpec((tqs, HEAD), lambda h, qi: (qi, h))
    k_spec = pl.BlockSpec((s_dim, QK_HEAD), lambda h, qi: (0, h))
    v_spec = pl.BlockSpec((s_dim, HEAD), lambda h, qi: (0, h))
    wide2 = jax.ShapeDtypeStruct((s_dim, N_HEADS * QK_HEAD), F32)
    return pl.pallas_call(
        body,
        out_shape=[wide2, wide2, jax.ShapeDtypeStruct((s_dim, N_HEADS * HEAD), BF16)],
        grid=(N_HEADS, s_dim // tqs),
        in_specs=[q_spec, k_spec, pl.BlockSpec((QK_HEAD, s_dim), lambda h, qi: (h, 0)), v_spec, o_spec, o_spec,
                  pl.BlockSpec((None, 1, tqs), lambda h, qi: (h, 0, qi))],
        out_specs=[q_spec, k_spec, v_spec],
        scratch_shapes=[pltpu.VMEM((QK_HEAD, tqs), F32), pltpu.VMEM((s_dim, HEAD), F32)],
        compiler_params=pltpu.CompilerParams(dimension_semantics=("parallel", "arbitrary")),
        name="attn_bwd",
    )(qc, kc, kct, v, o, d_o, lse)


def _merge_fwd(y_dn, y_mla, proj_g):
    def fn(r, c):
        yd, ym, g = r
        return [_sig(g[:, :D_MODEL]) * yd + _sig(g[:, D_MODEL:]) * ym], []

    return _rowwise(fn, [y_dn, y_mla, proj_g], [], [(D_MODEL, BF16)], name="merge_fwd")[0]


def _merge_bwd(y_dn, y_mla, proj_g, d_mixed):
    def fn(r, c):
        yd, ym, g, dm = r
        sd, sm = _sig(g[:, :D_MODEL]), _sig(g[:, D_MODEL:])
        d_g = jnp.concatenate([dm * yd * sd * (1.0 - sd), dm * ym * sm * (1.0 - sm)], axis=1)
        return [d_g, dm * sd, dm * sm], []

    return _rowwise(fn, [y_dn, y_mla, proj_g, d_mixed], [], [(2 * D_MODEL, BF16), (D_MODEL, BF16), (D_MODEL, BF16)],
                    name="merge_bwd")


def _ln_stats(z):
    mu = _rowmean(z)
    zc = z - mu
    r = lax.rsqrt(_rowmean(zc * zc) + EPS_LN)
    return zc * r, r


def _ln_bwd(dy, xh, r, g):
    dxh = dy * g
    return r * (dxh - _rowmean(dxh) - xh * _rowmean(dxh * xh))


def _ln1_fwd(x, a1, g, b):
    def fn(r, c):
        xh, _ = _ln_stats(ALPHA * r[0] + r[1])
        y = xh * c[0] + c[1]
        return [y, y], []

    return _rowwise(fn, [x, a1], [g, b], [(D_MODEL, F32), (D_MODEL, BF16)], name="ln1_fwd")


def _ln1_bwd(x, a1, d_h1, g):
    def fn(r, c):
        xh, rr = _ln_stats(ALPHA * r[0] + r[1])
        dy = r[2]
        dz = _ln_bwd(dy, xh, rr, c[0])
        return [dz, ALPHA * dz], [_colsum(dy * xh), _colsum(dy)]

    return _rowwise(fn, [x, a1, d_h1], [g], [(D_MODEL, BF16), (D_MODEL, F32)], accs=[(1, D_MODEL), (1, D_MODEL)],
                    name="ln1_bwd")


def _act_fwd(gu):
    def fn(r, c):
        gt, up = r[0][:, :FFN_HIDDEN], r[0][:, FFN_HIDDEN:]
        return [gt * _sig(gt) * up], []

    return _rowwise(fn, [gu], [], [(FFN_HIDDEN, BF16)], name="act_fwd")[0]


def _act_bwd(gu, d_act):
    def fn(r, c):
        gt, up = r[0][:, :FFN_HIDDEN], r[0][:, FFN_HIDDEN:]
        da = r[1]
        return [jnp.concatenate([da * up * _silu_grad(gt), da * gt * _sig(gt)], axis=1)], []

    return _rowwise(fn, [gu, d_act], [], [(2 * FFN_HIDDEN, BF16)], name="act_bwd")[0]


def _tail(h1, ffn, pg, pp, tgt, g, b):
    def fn(r, c):
        h1_, ffn_, pg_, pp_, t_ = r
        sp = _sig(pg_)
        xh, rr = _ln_stats(ALPHA * h1_ + ffn_ + sp * pp_)
        y = xh * c[0] + c[1]
        err = y - t_
        dy = err * (1.0 / D_MODEL)
        dz = _ln_bwd(dy, xh, rr, c[0])
        loss = jnp.sum(0.5 * _rowmean(err * err), axis=0, keepdims=True)
        return ([dz, dz * pp_ * sp * (1.0 - sp), dz * sp, ALPHA * dz],
                [_colsum(dy * xh), _colsum(dy), jnp.broadcast_to(loss, (1, LANES))])

    return _rowwise(fn, [h1, ffn, pg, pp, tgt], [g, b], [(D_MODEL, BF16)] * 3 + [(D_MODEL, F32)],
                    accs=[(1, D_MODEL), (1, D_MODEL), (1, LANES)], name="tail")


def _local_step(x, p, pos, tgt, w, late_weights, emit):
    w = dict(w)
    s_dim = x.shape[0]
    xb, pb = x.astype(BF16), p.astype(BF16)
    proj_a = _mm(xb, w["wa"], name="f_proj_a")
    proj_g = _mm(xb, w["wg"], name="f_proj_g")
    proj_b = _mm(xb, w["wb"], name="f_proj_b")
    qkvn = _conv_fwd(proj_a, w["conv"])
    beta, gc = _gates_fwd(proj_b, w["alog"], w["dtb"])
    gc_t = jnp.transpose(gc[:, :N_HEADS])
    u, w_, qd, kt, a_mat, t_fold = _gdr_prep_fwd(qkvn, beta, gc, gc_t)
    o_dn, states = _gdr_scan_fwd(u, w_, qd, kt, a_mat, gc)
    og = _gdr_out_fwd(o_dn, proj_a, w["dnw"])
    w.update(late_weights(og))
    y_dn = _mm(og, w["br_dn"], name="f_y_dn")
    c_q, c_kv = _mla_norm_fwd(proj_b, w["qnw"], w["kvnw"])
    q_full = _mm(c_q, w["uq"], name="f_q_full")
    k_nope = _mm(c_kv, w["uk"], name="f_k_nope")
    vv = _mm(c_kv, w["uv"], out_dtype=BF16, name="f_v")
    qc, kc = _mla_qk_fwd(q_full, k_nope, proj_b, pos)
    o_mla, lse = _attn_fwd(qc, kc, jnp.transpose(vv))
    y_mla = _mm(o_mla, w["br_mla"], name="f_y_mla")
    mixed = _merge_fwd(y_dn, y_mla, proj_g)
    a1 = _mm(mixed, w["wo"], name="f_a1")
    h1, h1b = _ln1_fwd(x, a1, w["ln1g"], w["ln1b"])
    gu = _mm(h1b, w["ffn_in"], name="f_gu")
    act = _act_fwd(gu)
    ffn = _mm(act, w["ffn_out"], name="f_ffn")
    pg = _mm(h1b, w["ple_gate"], name="f_pg")
    pp = _mm(pb, w["ple"], name="f_pp")
    g = {}
    dz2, d_pg, d_pp, dh1a, g["ln2g"], g["ln2b"], loss = _tail(h1, ffn, pg, pp, tgt, w["ln2g"], w["ln2b"])
    g["ple_t"] = _mm(d_pp, pb, ta=True, name="b_w_ple")
    g["ple_gate"] = _mm(h1b, d_pg, ta=True, name="b_w_ple_gate")
    g["ffn_out"] = _mm(act, dz2, ta=True, name="b_w_ffn_out")
    d_act = _mm(dz2, w["ffn_out"], tb=True, name="b_act")
    d_gu = _act_bwd(gu, d_act)
    g["ffn_in_t"] = _mm(d_gu, h1b, ta=True, name="b_w_ffn_in")
    d_gu = emit("ffn", g, d_gu)
    d_h1 = _mm(d_gu, w["ffn_in_t"], add=(dh1a,), name="b_h1_ffn")
    d_h1 = _mm(d_pg, w["ple_gate"], tb=True, add=(d_h1,), name="b_h1_ple")
    dz1, dxa, g["ln1g"], g["ln1b"] = _ln1_bwd(x, a1, d_h1, w["ln1g"])
    g["wo"] = _mm(mixed, dz1, ta=True, name="b_w_o")
    d_mixed = _mm(dz1, w["wo"], tb=True, name="b_mixed")
    d_proj_g, d_y_dn, d_y_mla = _merge_bwd(y_dn, y_mla, proj_g, d_mixed)
    g["br_mla"] = _mm(o_mla, d_y_mla, ta=True, name="b_w_br_mla")
    d_o_mla = _mm(d_y_mla, w["br_mla"], tb=True, name="b_o_mla")
    d_qc, d_kc, d_v = _attn_bwd(qc, kc, jnp.transpose(kc), vv, o_mla, d_o_mla, lse)
    d_q_full, d_kn, d_kr = _mla_qk_bwd(d_qc, d_kc, pos)
    g["uq"] = _mm(c_q, d_q_full, ta=True, name="b_w_uq")
    d_c_q = _mm(d_q_full, w["uq"], tb=True, name="b_c_q")
    g["uk"] = _mm(c_kv, d_kn, ta=True, name="b_w_uk")
    g["uv"] = _mm(c_kv, d_v, ta=True, name="b_w_uv")
    d_c_kv = _mm(d_kn, w["uk"], tb=True, name="b_c_kv_k")
    d_c_kv = _mm(d_v, w["uv"], tb=True, add=(d_c_kv,), name="b_c_kv_v")
    d_cq, d_ckv, g["qnw"], g["kvnw"] = _mla_norm_bwd(proj_b, w["qnw"], w["kvnw"], d_c_q, d_c_kv)
    g["br_dn"] = _mm(og, d_y_dn, ta=True, name="b_w_br_dn")
    d_og = emit("mix", g, _mm(d_y_dn, w["br_dn"], tb=True, name="b_og"))
    d_o_dn, d_z, g["dnw"] = _gdr_out_bwd(o_dn, proj_a, d_og, w["dnw"])
    du, dw, dqd, dkt, d_a, d_egl = _gdr_scan_bwd(u, w_, qd, kt, a_mat, gc, states, d_o_dn)
    dq, dk, dv, d_beta, d_gc = _gdr_prep_bwd(qkvn, beta, gc, gc_t, t_fold, du, dw, dqd, dkt, d_a)
    d_egl_rows = jnp.pad(d_egl[:, None, :, 0], ((0, 0), (CHUNK - 1, 0), (0, LANES - N_HEADS))).reshape(s_dim, LANES)
    d_ba, g["alog"], g["dtb"] = _gates_bwd(proj_b, w["alog"], w["dtb"], gc, d_beta, d_gc, d_egl_rows)
    d_qkv, g["conv"] = _conv_bwd(proj_a, w["conv"], dq, dk, dv)
    zeros = jnp.zeros((s_dim, WB_CKV - Q_LORA), BF16)
    d_proj_b = jnp.concatenate([d_cq, zeros, d_ckv, d_kr, d_ba], axis=1)
    g["wa_qkv_t"] = _mm(d_qkv, xb, ta=True, name="b_w_qkv")
    g["wa_z_t"] = _mm(d_z, xb, ta=True, name="b_w_z")
    g["wg_t"] = _mm(d_proj_g, xb, ta=True, name="b_w_g")
    g["wb_t"] = _mm(d_proj_b, xb, ta=True, name="b_w_b")
    dx = _mm(d_qkv, w["wa_qkv_t"], add=(dxa,), name="b_x_qkv")
    dx = _mm(d_z, w["wa_z_t"], add=(dx,), name="b_x_z")
    dx = _mm(d_proj_g, w["wg_t"], add=(dx,), name="b_x_g")
    dx = _mm(d_proj_b, w["wb_t"], add=(dx,), name="b_x_b")
    return loss, dx, g


_BIG = (("w_in", 1), ("w_uq", 0), ("w_uk", 0), ("w_uv", 0), ("w_br_dn", 0), ("w_br_mla", 0),
        ("w_o", 0), ("w_ffn_in", 1), ("w_ffn_out", 0), ("w_ple", 1), ("w_ple_gate", 0))
_BIG_AXIS = dict(_BIG)
_SMALL = ("ln1_g", "ln1_b", "ln2_g", "ln2_b", "q_norm_w", "kv_norm_w", "dn_norm_w", "dn_a_log", "dn_dt_bias")
_ORDER = ("w_in", "conv_w", "dn_a_log", "dn_dt_bias", "dn_norm_w", "q_norm_w", "w_uq", "kv_norm_w", "w_uk", "w_uv",
          "w_br_dn", "w_br_mla", "w_o", "ln1_g", "ln1_b", "w_ffn_in", "w_ffn_out", "w_ple", "w_ple_gate", "ln2_g",
          "ln2_b")


def _stored_shape(name, shard_shape):
    axis = _BIG_AXIS[name]
    lead = shard_shape[axis]
    return lead, int(np.prod(shard_shape)) // lead


def _to_stored(name, shard):
    return jnp.moveaxis(shard, _BIG_AXIS[name], 0).reshape(_stored_shape(name, shard.shape))


def _from_stored(name, stored, shard_shape):
    axis = _BIG_AXIS[name]
    moved = (shard_shape[axis],) + shard_shape[:axis] + shard_shape[axis + 1:]
    return jnp.moveaxis(stored.reshape(moved), 0, axis)


_W_IN_ROWS = np.cumsum([0, 3072, 1024, 8, 8, Q_LORA, KV_LORA, ROPE, D_MODEL, D_MODEL])


def _first_weights(w_in_t, conv_full, small):
    r = _W_IN_ROWS
    zr = lambda n: jnp.zeros((n, D_MODEL), w_in_t.dtype)
    w = {}
    w["wa_t"] = w_in_t[r[0]:r[2]]
    w["wa_qkv_t"], w["wa_z_t"] = w_in_t[r[0]:r[1]], w_in_t[r[1]:r[2]]
    w["wg_t"] = w_in_t[r[7]:r[9]]
    w["wb_t"] = jnp.concatenate([w_in_t[r[4]:r[5]], zr(WB_CKV - Q_LORA), w_in_t[r[5]:r[7]], zr(LANES - ROPE),
                                 w_in_t[r[2]:r[4]], zr(LANES - 2 * N_HEADS)], axis=0)
    for k_ in ("wa", "wg", "wb"):
        w[k_] = jnp.transpose(w[k_ + "_t"])
    w["conv"] = conv_full
    pad_l = lambda v: jnp.pad(v, ((0, 0), (0, LANES - v.shape[1])))
    w["alog"], w["dtb"] = pad_l(small["dn_a_log"]), pad_l(small["dn_dt_bias"])
    w["dnw"], w["qnw"], w["kvnw"] = small["dn_norm_w"], small["q_norm_w"], small["kv_norm_w"]
    w["ln1g"], w["ln1b"], w["ln2g"], w["ln2b"] = small["ln1_g"], small["ln1_b"], small["ln2_g"], small["ln2_b"]
    return w


def _late_weights(fw):
    w = {}
    uq = fw["w_uq"].reshape(Q_LORA, N_HEADS, HEAD + ROPE)
    uq_r = jnp.pad(uq[:, :, HEAD:], ((0, 0), (0, 0), (0, HEAD - ROPE)))
    w["uq"] = jnp.concatenate([uq[:, :, :HEAD].reshape(Q_LORA, -1), uq_r.reshape(Q_LORA, -1)], axis=1)
    w["uk"], w["uv"] = fw["w_uk"], fw["w_uv"]
    w["br_dn"], w["br_mla"], w["wo"] = fw["w_br_dn"], fw["w_br_mla"], fw["w_o"]
    w["ffn_in_t"], w["ffn_out"] = fw["w_ffn_in"], fw["w_ffn_out"]
    w["ple_t"], w["ple_gate"] = fw["w_ple"], fw["w_ple_gate"]
    for k_ in ("ffn_in", "ple"):
        w[k_] = jnp.transpose(w[k_ + "_t"])
    return w


_GROUP_GRADS = {"ffn": (("w_ple", "ple_t"), ("w_ple_gate", "ple_gate"), ("w_ffn_out", "ffn_out"),
                        ("w_ffn_in", "ffn_in_t")),
                "mix": (("w_o", "wo"), ("w_br_mla", "br_mla"), ("w_uq", "uq"), ("w_uk", "uk"), ("w_uv", "uv"),
                        ("w_br_dn", "br_dn"))}


def _group_grads(group, g):
    out = {}
    for name, key in _GROUP_GRADS[group]:
        t = g[key]
        if name == "w_uq":
            uq_n = t[:, :D_MODEL].reshape(Q_LORA, N_HEADS, HEAD)
            uq_r = t[:, D_MODEL:].reshape(Q_LORA, N_HEADS, HEAD)[:, :, :ROPE]
            t = jnp.concatenate([uq_n, uq_r], axis=2).reshape(Q_LORA, -1)
        out[name] = t
    return out


def _last_grads(g):
    wb = g["wb_t"]
    w_in = jnp.concatenate([
        g["wa_qkv_t"], g["wa_z_t"], wb[WB_BA:WB_BA + 2 * N_HEADS], wb[WB_CQ:WB_CQ + Q_LORA],
        wb[WB_CKV:WB_CKV + KV_LORA], wb[WB_KR:WB_KR + ROPE], g["wg_t"]], axis=0)
    small = {"ln1_g": g["ln1g"], "ln1_b": g["ln1b"], "ln2_g": g["ln2g"], "ln2_b": g["ln2b"], "q_norm_w": g["qnw"],
             "kv_norm_w": g["kvnw"], "dn_norm_w": g["dnw"], "dn_a_log": g["alog"], "dn_dt_bias": g["dtb"],
             "conv_w": g["conv"]}
    return w_in, small


_SMALL_SLOTS = {"ln1_g": (0, 0, 1024), "ln1_b": (1, 0, 1024), "ln2_g": (2, 0, 1024), "ln2_b": (3, 0, 1024),
                "q_norm_w": (4, 0, 384), "kv_norm_w": (4, 384, 256), "dn_norm_w": (4, 640, 128),
                "dn_a_log": (4, 768, 8), "dn_dt_bias": (4, 896, 8)}
_SMALL_ROWS, _LOSS_ROW, _CONV_ROW0, _CONV_ROWS = 24, 5, 8, 12


def _pack_small_grads(small_g, loss):
    zeros = lambda r, c: jnp.zeros((r, c), F32)
    row4 = jnp.concatenate([small_g["q_norm_w"], small_g["kv_norm_w"], small_g["dn_norm_w"], small_g["dn_a_log"],
                            small_g["dn_dt_bias"]], axis=1)
    row5 = jnp.concatenate([loss, zeros(1, FLAT_COLS - LANES)], axis=1)
    head = jnp.concatenate([small_g["ln1_g"], small_g["ln1_b"], small_g["ln2_g"], small_g["ln2_b"], row4, row5,
                            zeros(2, FLAT_COLS)], axis=0)
    conv = small_g["conv_w"].reshape(_CONV_ROWS, FLAT_COLS)
    return jnp.concatenate([head, conv, zeros(_SMALL_ROWS - _CONV_ROW0 - _CONV_ROWS, FLAT_COLS)], axis=0)


_MESH_ID = pl.DeviceIdType.MESH
_ANY = pl.BlockSpec(memory_space=pl.ANY)


def _all_gather(blocks, name):
    n = len(blocks)

    def body(*refs):
        x_refs, out_refs = refs[:n], refs[n:2 * n]
        send_sems, recv_sems, local_sems = refs[2 * n:]
        x, y, c = lax.axis_index("x"), lax.axis_index("y"), lax.axis_index("c")
        me, sibling = (x, y, c), (x, y, 1 - c)
        chips = [(1 - x, y), (x, 1 - y), (1 - x, 1 - y)]

        def slot(i, px, py, pc):
            return out_refs[i].at[4 * px + 2 * py + pc]

        def copy(i, k, origin, to, src=None):
            return pltpu.make_async_remote_copy(
                src_ref=slot(i, *origin) if src is None else src, dst_ref=slot(i, *origin),
                send_sem=send_sems.at[7 * i + k], recv_sem=recv_sems.at[7 * i + k], device_id=to,
                device_id_type=_MESH_ID)

        mine = [pltpu.make_async_copy(x_refs[i], slot(i, *me), local_sems.at[i]) for i in range(n)]
        first, passed = [], []
        for i in range(n):
            mine[i].start()
            first.append(copy(i, 0, me, sibling, src=x_refs[i]))
            first += [copy(i, 1 + j, me, (*chip, c), src=x_refs[i]) for j, chip in enumerate(chips)]
        for cp in first:
            cp.start()
        for i in range(n):
            for j, chip in enumerate(chips):
                copy(i, 1 + j, (*chip, c), me).wait_recv()
                passed.append(copy(i, 4 + j, (*chip, c), sibling))
                passed[-1].start()
        for i in range(n):
            copy(i, 0, sibling, me).wait_recv()
            for j, chip in enumerate(chips):
                copy(i, 4 + j, (*chip, 1 - c), me).wait_recv()
        for cp in first + passed:
            cp.wait_send()
        for cp in mine:
            cp.wait()

    return pl.pallas_call(
        body,
        out_shape=[jax.ShapeDtypeStruct((N_DEV,) + b.shape, b.dtype) for b in blocks],
        in_specs=[_ANY] * n,
        out_specs=[_ANY] * n,
        scratch_shapes=[pltpu.SemaphoreType.DMA((7 * n,)), pltpu.SemaphoreType.DMA((7 * n,)),
                        pltpu.SemaphoreType.DMA((n,))],
        name=name,
    )(*blocks)


def _exchange_sibling(srcs, name):
    n = len(srcs)

    def body(*refs):
        src_refs, dst_refs = refs[:n], refs[n:2 * n]
        send_sems, recv_sems = refs[2 * n:]
        x, y, c = lax.axis_index("x"), lax.axis_index("y"), lax.axis_index("c")
        copies = [pltpu.make_async_remote_copy(
            src_ref=src_refs[i].at[2 * q + (1 - c)], dst_ref=dst_refs[i].at[q], send_sem=send_sems.at[4 * i + q],
            recv_sem=recv_sems.at[4 * i + q], device_id=(x, y, 1 - c), device_id_type=_MESH_ID)
            for i in range(n) for q in range(4)]
        for cp in copies:
            cp.start()
        for cp in copies:
            cp.wait_recv()
        for cp in copies:
            cp.wait_send()

    return pl.pallas_call(
        body,
        out_shape=[jax.ShapeDtypeStruct((4,) + s.shape[1:], s.dtype) for s in srcs],
        in_specs=[_ANY] * n,
        out_specs=[_ANY] * n,
        scratch_shapes=[pltpu.SemaphoreType.DMA((4 * n,)), pltpu.SemaphoreType.DMA((4 * n,))],
        name=name,
    )(*srcs)


def _exchange_chips(srcs, name):
    n = len(srcs)

    def body(*refs):
        src_refs, dst_refs = refs[:n], refs[n:2 * n]
        send_sems, recv_sems = refs[2 * n:]
        x, y, c = lax.axis_index("x"), lax.axis_index("y"), lax.axis_index("c")
        chips = [(1 - x, y), (x, 1 - y), (1 - x, 1 - y)]
        copies = [pltpu.make_async_remote_copy(
            src_ref=src_refs[i].at[2 * tx + ty], dst_ref=dst_refs[i].at[j], send_sem=send_sems.at[3 * i + j],
            recv_sem=recv_sems.at[3 * i + j], device_id=(tx, ty, c), device_id_type=_MESH_ID)
            for i in range(n) for j, (tx, ty) in enumerate(chips)]
        for cp in copies:
            cp.start()
        for cp in copies:
            cp.wait_recv()
        for cp in copies:
            cp.wait_send()

    return pl.pallas_call(
        body,
        out_shape=[jax.ShapeDtypeStruct((3,) + s.shape[1:], s.dtype) for s in srcs],
        in_specs=[_ANY] * n,
        out_specs=[_ANY] * n,
        scratch_shapes=[pltpu.SemaphoreType.DMA((3 * n,)), pltpu.SemaphoreType.DMA((3 * n,))],
        name=name,
    )(*srcs)


def _col_tile(c):
    return c if c <= 256 else 256


def _chip_sum(src, recv, parity, name):
    _, r, c = src.shape
    tc = _col_tile(c)

    def body(par_ref, a_ref, b_ref, o_ref, ob_ref):
        s = a_ref[...] + b_ref[...]
        o_ref[...] = s
        ob_ref[...] = s.astype(BF16)

    blk = lambda f: pl.BlockSpec((None, r, tc), f)
    return pl.pallas_call(
        body,
        out_shape=[jax.ShapeDtypeStruct((4, r, c), F32), jax.ShapeDtypeStruct((4, r, c), BF16)],
        grid_spec=pltpu.PrefetchScalarGridSpec(
            num_scalar_prefetch=1, grid=(4, c // tc),
            in_specs=[blk(lambda q, j, par: (2 * q + par[0], 0, j)), blk(lambda q, j, par: (q, 0, j))],
            out_specs=[blk(lambda q, j, par: (q, 0, j)), blk(lambda q, j, par: (q, 0, j))]),
        compiler_params=pltpu.CompilerParams(dimension_semantics=("parallel", "parallel")),
        name=name,
    )(parity, src, recv)


def _sum_parts(own, others, chip, name):
    _, r, c = own.shape
    tc = _col_tile(c)

    def body(q_ref, a_ref, b_ref, o_ref):
        o_ref[...] = ((a_ref[...] + b_ref[0].astype(F32)) + b_ref[1].astype(F32)) + b_ref[2].astype(F32)

    return pl.pallas_call(
        body,
        out_shape=jax.ShapeDtypeStruct((r, c), F32),
        grid_spec=pltpu.PrefetchScalarGridSpec(
            num_scalar_prefetch=1, grid=(c // tc,),
            in_specs=[pl.BlockSpec((None, r, tc), lambda j, q: (q[0], 0, j)),
                      pl.BlockSpec((3, r, tc), lambda j, q: (0, 0, j))],
            out_specs=pl.BlockSpec((r, tc), lambda j, q: (0, j))),
        compiler_params=pltpu.CompilerParams(dimension_semantics=("parallel",)),
        name=name,
    )(chip, own, others)


_HBM = pl.BlockSpec(memory_space=pltpu.HBM)
_SEM = pl.BlockSpec(memory_space=pltpu.SEMAPHORE)
_DATAFLOW = pltpu.SideEffectType.DATAFLOW_SIDE_EFFECTING
N_PEERS = N_DEV - 1


def _ring_peer(j):
    me = 4 * lax.axis_index("x") + 2 * lax.axis_index("y") + lax.axis_index("c")
    k = (me + j) % N_DEV
    return me, k, (k // 4, (k // 2) % 2, k % 2)


def _spread_copy(i, j, src_refs, land_refs, send_sems, recv_sems, scatter):
    me, k, peer = _ring_peer(j)
    return pltpu.make_async_remote_copy(
        src_ref=src_refs[i].at[k] if scatter else src_refs[i], dst_ref=land_refs[i].at[me],
        send_sem=send_sems.at[N_PEERS * i + j - 1], recv_sem=recv_sems.at[N_PEERS * i + j - 1], device_id=peer,
        device_id_type=_MESH_ID)


def _spread_start(srcs, carry, scatter, name):
    n = len(srcs)
    lands = [lax.empty(((N_DEV,) + s.shape[-2:]), s.dtype) for s in srcs]

    def body(*refs):
        src_refs, land_refs = refs[:n], refs[n:2 * n]
        send_sems, recv_sems = refs[2 * n + 1], refs[2 * n + 2]
        for i in range(n):
            for j in range(1, N_DEV):
                _spread_copy(i, j, src_refs, land_refs, send_sems, recv_sems, scatter).start()

    hbm = lambda a: pltpu.HBM(a.shape, a.dtype)
    sems = pltpu.SemaphoreType.DMA((N_PEERS * n,))
    pinned = [pltpu.with_memory_space_constraint(a, pltpu.HBM) for a in list(srcs) + lands + [carry]]
    res = pl.pallas_call(
        body, name=name,
        out_shape=(sems, sems, *[hbm(a) for a in pinned]),
        in_specs=[_HBM] * (2 * n + 1),
        out_specs=(_SEM, _SEM, *[_HBM] * (2 * n + 1)),
        input_output_aliases={i: 2 + i for i in range(2 * n + 1)},
        compiler_params=pltpu.CompilerParams(has_side_effects=_DATAFLOW),
    )(*pinned)
    return res[0], res[1], list(res[2:2 + n]), list(res[2 + n:2 + 2 * n]), res[2 + 2 * n]


def _spread_wait(started, after, scatter, name):
    send_sems, recv_sems, srcs, lands, _ = started
    n = len(srcs)

    def body(*refs):
        src_refs, land_refs = refs[:n], refs[n:2 * n]
        send_s, recv_s = refs[2 * n], refs[2 * n + 1]
        local_sems = refs[-1]
        me = _ring_peer(0)[0]
        own = [pltpu.make_async_copy(src_refs[i].at[me] if scatter else src_refs[i], land_refs[i].at[me],
                                     local_sems.at[i]) for i in range(n)]
        for cp in own:
            cp.start()
        for i in range(n):
            for j in range(1, N_DEV):
                cp = _spread_copy(i, j, src_refs, land_refs, send_s, recv_s, scatter)
                cp.wait_send()
                cp.wait_recv()
        for cp in own:
            cp.wait()

    hbm = lambda a: pltpu.HBM(a.shape, a.dtype)
    res = pl.pallas_call(
        body, name=name,
        out_shape=tuple(hbm(a) for a in srcs + lands),
        in_specs=[_HBM] * (2 * n) + [_SEM, _SEM, pl.BlockSpec(memory_space=pl.ANY)],
        out_specs=tuple([_HBM] * (2 * n)),
        input_output_aliases={i: i for i in range(2 * n)},
        scratch_shapes=[pltpu.SemaphoreType.DMA((n,))],
        compiler_params=pltpu.CompilerParams(has_side_effects=_DATAFLOW),
    )(*srcs, *lands, send_sems, recv_sems, after)
    return list(res[n:])


def _sum8(landing, name):
    _, r, c = landing.shape
    tc = _col_tile(c)

    def body(a_ref, o_ref):
        tot = a_ref[0]
        for k in range(1, N_DEV):
            tot = tot + a_ref[k]
        o_ref[...] = tot

    return pl.pallas_call(
        body,
        out_shape=jax.ShapeDtypeStruct((r, c), F32),
        grid=(c // tc,),
        in_specs=[pl.BlockSpec((N_DEV, r, tc), lambda j: (0, 0, j))],
        out_specs=pl.BlockSpec((r, tc), lambda j: (0, j)),
        compiler_params=pltpu.CompilerParams(dimension_semantics=("parallel",)),
        name=name,
    )(landing)


def _adamw_math(w, g, m, v):
    m = ADAM_B1 * m + (1.0 - ADAM_B1) * g
    v = ADAM_B2 * v + (1.0 - ADAM_B2) * (g * g)
    m_hat = m / (1.0 - ADAM_B1 ** ADAM_STEP)
    v_hat = v / (1.0 - ADAM_B2 ** ADAM_STEP)
    delta = -ADAM_LR * (m_hat / (jnp.sqrt(v_hat) + ADAM_EPS) + ADAM_WD * w)
    return delta, m, v


def _adamw(w, m, v, g, name):
    r, c = w.shape

    def fn(rows, consts):
        return list(_adamw_math(*rows)), []

    return _rowwise(fn, [w, g, m, v], [], [(c, F32)] * 3, tm=r if r <= 512 else 256, name=name)


def _adamw_small(gathered, params):
    ns = len(_SMALL)

    def body(*refs):
        g_ref, p_refs, o_refs = refs[0], refs[1:1 + 3 * ns], refs[1 + 3 * ns:]
        tot = g_ref[0]
        for k in range(1, N_DEV):
            tot = tot + g_ref[k]
        for i, name in enumerate(_SMALL):
            row, lane0, lanes = _SMALL_SLOTS[name]
            g = tot[row:row + 1, lane0:lane0 + lanes]
            w_, m_, v_ = (p_refs[3 * i + j][...] for j in range(3))
            delta, m2, v2 = _adamw_math(w_, g, m_, v_)
            for j, val in enumerate((g, delta, m2, v2)):
                o_refs[4 * i + j][...] = val
        o_refs[4 * ns][...] = tot[_LOSS_ROW:_LOSS_ROW + 1, 0:LANES]
        o_refs[4 * ns + 1][...] = tot[_CONV_ROW0:_CONV_ROW0 + _CONV_ROWS, :]

    out_shape = [jax.ShapeDtypeStruct(w.shape, F32) for (w, _, _) in params for _ in range(4)]
    out_shape += [jax.ShapeDtypeStruct((1, LANES), F32), jax.ShapeDtypeStruct((_CONV_ROWS, FLAT_COLS), F32)]
    flat = [a for wmv in params for a in wmv]
    return pl.pallas_call(body, out_shape=out_shape, name="adamw_small")(gathered, *flat)


def kernel(x, p, positions, w_in, conv_w, dn_a_log, dn_dt_bias, dn_norm_w, q_norm_w, w_uq, kv_norm_w, w_uk, w_uv, w_br_dn, w_br_mla, w_o, ln1_g, ln1_b, w_ffn_in, w_ffn_out, w_ple, w_ple_gate, ln2_g, ln2_b, loss_target, m_w_in, m_conv_w, m_dn_a_log, m_dn_dt_bias, m_dn_norm_w, m_q_norm_w, m_w_uq, m_kv_norm_w, m_w_uk, m_w_uv, m_w_br_dn, m_w_br_mla, m_w_o, m_ln1_g, m_ln1_b, m_w_ffn_in, m_w_ffn_out, m_w_ple, m_w_ple_gate, m_ln2_g, m_ln2_b, v_w_in, v_conv_w, v_dn_a_log, v_dn_dt_bias, v_dn_norm_w, v_q_norm_w, v_w_uq, v_kv_norm_w, v_w_uk, v_w_uv, v_w_br_dn, v_w_br_mla, v_w_o, v_ln1_g, v_ln1_b, v_w_ffn_in, v_w_ffn_out, v_w_ple, v_w_ple_gate, v_ln2_g, v_ln2_b):
    args = dict(locals())
    wts = {n: args[n] for n in _ORDER}
    mom1 = {n: args["m_" + n] for n in _ORDER}
    mom2 = {n: args["v_" + n] for n in _ORDER}
    big_names = [n for n, _ in _BIG]
    shard_shapes = {n: wts[n].shape[1:] for n in big_names}
    c_idx = lax.axis_index("c")
    q_idx = 2 * lax.axis_index("x") + lax.axis_index("y")
    parity, chip = c_idx.reshape(1).astype(jnp.int32), q_idx.reshape(1).astype(jnp.int32)

    stored = {n: _to_stored(n, wts[n][0]).astype(BF16) for n in big_names}
    first = _all_gather([stored["w_in"], conv_w[0]], "ag_first")
    late_names = big_names[1:]
    ag = _spread_start([stored[n] for n in late_names], first[0], False, "ag_start")
    conv_full = jnp.moveaxis(first[1], 0, 1).reshape(conv_w.shape[1], -1)
    small_w = {n: wts[n].astype(F32) for n in _SMALL}
    w = _first_weights(ag[4].reshape(-1, D_MODEL), conv_full, small_w)

    def late_weights(after):
        got = _spread_wait(ag, after, False, "ag_wait")
        return _late_weights({n: t.reshape(-1, t.shape[-1]) for n, t in zip(late_names, got)})

    started = {}

    def emit(group, g, carry):
        grads = _group_grads(group, g)
        srcs = [grads[n].reshape((N_DEV,) + _stored_shape(n, shard_shapes[n])) for n in grads]
        started[group] = (list(grads), _spread_start(srcs, carry, True, "rs_start_" + group))
        return started[group][1][4]

    s_dim = x.shape[1]
    loss, dx, g = _local_step(x[0], p[0, 0], positions.reshape(s_dim, 1).astype(F32), loss_target[0], w,
                              late_weights, emit)
    g_w_in, small_g = _last_grads(g)

    src = g_w_in.reshape((N_DEV,) + _stored_shape("w_in", shard_shapes["w_in"]))
    from_sibling = _exchange_sibling([src], "rs_sibling")[0]
    own, own_bf = _chip_sum(src, from_sibling, parity, "rs_sum_w_in")
    from_chips = _exchange_chips([own_bf], "rs_chips")[0]

    out_g, out_d, out_m, out_v = {}, {}, {}, {}

    def update(n, grad, shp):
        flat2 = (shp[0], int(np.prod(shp[1:])))
        d, m2, v2 = _adamw(wts[n][0].reshape(flat2), mom1[n][0].reshape(flat2), mom2[n][0].reshape(flat2),
                           grad.reshape(flat2), "adamw_" + n)
        out_g[n], out_d[n], out_m[n], out_v[n] = grad, d.reshape(shp), m2.reshape(shp), v2.reshape(shp)

    total = _sum_parts(own, from_chips, chip, "rs_total_w_in")
    update("w_in", _from_stored("w_in", total, shard_shapes["w_in"]), shard_shapes["w_in"])
    for group, (names, st) in started.items():
        for n, landing in zip(names, _spread_wait(st, dx, True, "rs_wait_" + group)):
            update(n, _from_stored(n, _sum8(landing, "rs_total_" + n), shard_shapes[n]), shard_shapes[n])

    g_small = _all_gather([_pack_small_grads(small_g, loss)], "ag_small")[0]
    res = _adamw_small(g_small, [(wts[n], mom1[n], mom2[n]) for n in _SMALL])
    for i, n in enumerate(_SMALL):
        out_g[n], out_d[n], out_m[n], out_v[n] = res[4 * i:4 * i + 4]
    loss_out = res[4 * len(_SMALL)][0, 0]
    conv_shape = conv_w.shape[1:]
    conv_g = lax.dynamic_slice(res[-1].reshape(conv_shape[0], -1), (0, (2 * q_idx + c_idx) * conv_shape[1]),
                               conv_shape)
    update("conv_w", conv_g, conv_shape)

    expand = lambda d, n: d[n] if n in _SMALL else d[n][None]
    return (loss_out, dx[None], *[expand(out_g, n) for n in _ORDER], *[expand(out_d, n) for n in _ORDER],
            *[expand(out_m, n) for n in _ORDER], *[expand(out_v, n) for n in _ORDER])
```

```python
import functools

import numpy as np
import jax
import jax.numpy as jnp
from jax import lax
from jax.experimental import pallas as pl
from jax.experimental.pallas import tpu as pltpu

F32 = jnp.float32
BF16 = jnp.bfloat16

D_MODEL = 1024
N_HEADS = 8
HEAD = 128
CHUNK = 64
GROUP = 256
ROPE = 64
Q_LORA = 384
KV_LORA = 256
FFN_HIDDEN = 2816
PLE_DIM = 256
ROPE_BASE = 10000.0
ALPHA = 2.0 ** 0.25
SCALE = float((HEAD + ROPE) ** -0.5)
NEG_BIG = -1e30
EPS_RMS = 1e-6
EPS_LN = 1e-5

ADAM_LR = 0.001
ADAM_B1 = 0.9
ADAM_B2 = 0.999
ADAM_EPS = 1e-08
ADAM_WD = 0.01
ADAM_STEP = 10

N_DEV = 8
LANES = 128
FLAT_COLS = 1024

WB_CQ, WB_CKV, WB_KR, WB_BA, WB_COLS = 0, 512, 768, 896, 1024

HIGHEST = lax.Precision.HIGHEST

NN = (((1,), (0,)), ((), ()))
TN = (((0,), (0,)), ((), ()))
NT = (((1,), (1,)), ((), ()))


def _dot(a, b, dims=NN):
    return lax.dot_general(a.astype(BF16), b.astype(BF16), dims, preferred_element_type=F32)


def _dot32(a, b, dims=NN):
    return lax.dot_general(a, b, dims, precision=HIGHEST, preferred_element_type=F32)


def _sig(x):
    return 1.0 / (1.0 + jnp.exp(-x))


MM_TILE = 1536


def _pick_wide(n):
    if n <= MM_TILE:
        return n
    return max(t for t in range(LANES, MM_TILE + 1, LANES) if n % t == 0)


def _split_bf16(a):
    hi = a.astype(BF16)
    return hi, (a - hi.astype(F32)).astype(BF16)


def _dot3(a, b, dims=NN):
    ah, al = a if isinstance(a, tuple) else _split_bf16(a)
    bh, bl = b if isinstance(b, tuple) else _split_bf16(b)
    d = lambda p, q: lax.dot_general(p, q, dims, preferred_element_type=F32)
    return d(ah, bh) + (d(ah, bl) + d(al, bh))


def _mm(a, b, *, ta=False, tb=False, add=(), out_dtype=F32, name):
    if ta:
        k_dim, m_dim = a.shape
    else:
        m_dim, k_dim = a.shape
    if tb:
        n_dim, k2 = b.shape
    else:
        k2, n_dim = b.shape
    assert k_dim == k2, (a.shape, b.shape, ta, tb)
    tm = _pick_wide(m_dim)
    tn = _pick_wide(n_dim)
    tk = _pick_wide(k_dim)
    nk = k_dim // tk
    n_add = len(add)
    dims = TN if ta else (NT if tb else NN)
    assert not (ta and tb)

    def body(a_ref, b_ref, *rest):
        add_refs = rest[:n_add]
        o_ref = rest[n_add]
        acc = rest[n_add + 1]
        k = pl.program_id(2)

        @pl.when(k == 0)
        def _():
            acc[...] = jnp.zeros_like(acc)

        acc[...] += _dot(a_ref[...], b_ref[...], dims)

        @pl.when(k == nk - 1)
        def _():
            r = acc[...]
            for ar in add_refs:
                r = r + ar[...].astype(F32)
            o_ref[...] = r.astype(o_ref.dtype)

    a_spec = pl.BlockSpec((tk, tm), lambda i, j, k: (k, i)) if ta else pl.BlockSpec((tm, tk), lambda i, j, k: (i, k))
    b_spec = pl.BlockSpec((tn, tk), lambda i, j, k: (j, k)) if tb else pl.BlockSpec((tk, tn), lambda i, j, k: (k, j))
    o_spec = pl.BlockSpec((tm, tn), lambda i, j, k: (i, j))
    return pl.pallas_call(
        body,
        out_shape=jax.ShapeDtypeStruct((m_dim, n_dim), out_dtype),
        grid=(m_dim // tm, n_dim // tn, nk),
        in_specs=[a_spec, b_spec] + [o_spec] * n_add,
        out_specs=o_spec,
        scratch_shapes=[pltpu.VMEM((tm, tn), F32)],
        compiler_params=pltpu.CompilerParams(dimension_semantics=("parallel", "parallel", "arbitrary")),
        name=name,
    )(a, b, *add)


def _rowwise(fn, rows, consts, outs, accs=(), *, tm=256, name):
    rows = [r if isinstance(r, tuple) else (r, 0, r.shape[1]) for r in rows]
    s_dim = rows[0][0].shape[0]
    tm = min(tm, s_dim)
    assert s_dim % tm == 0 and all(arr.shape[0] == s_dim for arr, _, _ in rows)
    specs = [pl.BlockSpec((tm, width), functools.partial(lambda i, cb: (i, cb), cb=cb)) for _, cb, width in rows]
    args = [arr for arr, _, _ in rows]
    for c in consts:
        specs.append(pl.BlockSpec(c.shape, lambda i: (0, 0)))
        args.append(c)
    nr, nc, no = len(rows), len(consts), len(outs)
    out_shape = [jax.ShapeDtypeStruct((s_dim, w), dt) for (w, dt) in outs]
    out_specs = [pl.BlockSpec((tm, w), lambda i: (i, 0)) for (w, dt) in outs]
    out_shape += [jax.ShapeDtypeStruct(sh, F32) for sh in accs]
    out_specs += [pl.BlockSpec(sh, lambda i: (0, 0)) for sh in accs]

    def body(*refs):
        r = [x[...] for x in refs[:nr]]
        c = [x[...] for x in refs[nr:nr + nc]]
        o_refs = refs[nr + nc:nr + nc + no]
        a_refs = refs[nr + nc + no:]
        o_vals, a_vals = fn(r, c)
        for ref, v in zip(o_refs, o_vals, strict=True):
            ref[...] = v.astype(ref.dtype)
        if a_refs:
            @pl.when(pl.program_id(0) == 0)
            def _():
                for ref in a_refs:
                    ref[...] = jnp.zeros_like(ref)

            for ref, v in zip(a_refs, a_vals, strict=True):
                ref[...] += v

    res = pl.pallas_call(
        body,
        out_shape=out_shape,
        grid=(s_dim // tm,),
        in_specs=specs,
        out_specs=out_specs,
        compiler_params=pltpu.CompilerParams(dimension_semantics=("arbitrary" if accs else "parallel",)),
        name=name,
    )(*args)
    return res


def _colsum(v):
    return jnp.sum(v, axis=0, keepdims=True)


def _rowsum(v):
    return jnp.sum(v, axis=1, keepdims=True)


def _rowmean(v):
    return jnp.mean(v, axis=1, keepdims=True)


def _silu_grad(x):
    s = _sig(x)
    return s * (1.0 + x * (1.0 - s))


def _conv_taps(x, w, width=4):
    row = lax.broadcasted_iota(jnp.int32, x.shape, 0)
    c = x * w[width - 1:width, :]
    for s in range(1, width):
        c = c + jnp.where(row >= s, pltpu.roll(x, s, 0), 0.0) * w[width - 1 - s:width - s, :]
    return c


def _conv_fwd(proj_a, conv_w):
    s_dim = proj_a.shape[0]
    n_blk = 3 * N_HEADS

    def body(x_ref, w_ref, o_ref):
        j = pl.program_id(0)
        c = _conv_taps(x_ref[...], w_ref[...])
        y = c * _sig(c)
        r = lax.rsqrt(_rowsum(y * y) + EPS_RMS)
        fac = jnp.where(j < N_HEADS, r * (HEAD ** -0.5), jnp.where(j < 2 * N_HEADS, r, 1.0))
        o_ref[...] = y * fac

    return pl.pallas_call(
        body,
        out_shape=jax.ShapeDtypeStruct((s_dim, n_blk * HEAD), F32),
        grid=(n_blk,),
        in_specs=[pl.BlockSpec((s_dim, HEAD), lambda j: (0, j)), pl.BlockSpec((4, HEAD), lambda j: (0, j))],
        out_specs=pl.BlockSpec((s_dim, HEAD), lambda j: (0, j)),
        compiler_params=pltpu.CompilerParams(dimension_semantics=("parallel",)),
        name="conv_fwd",
    )(proj_a, conv_w)


def _conv_bwd(proj_a, conv_w, dq, dk, dv):
    s_dim = proj_a.shape[0]
    n_blk = 3 * N_HEADS

    def body(x_ref, w_ref, dq_ref, dk_ref, dv_ref, dx_ref, dw_ref):
        j = pl.program_id(0)
        x = x_ref[...]
        w = w_ref[...]
        do = jnp.where(j < N_HEADS, dq_ref[...], jnp.where(j < 2 * N_HEADS, dk_ref[...], dv_ref[...]))
        c = _conv_taps(x, w)
        sg = _sig(c)
        y = c * sg
        r = lax.rsqrt(_rowsum(y * y) + EPS_RMS)
        sc = jnp.where(j < N_HEADS, HEAD ** -0.5, 1.0)
        dy_n = sc * (r * do - y * (r * r * r) * _rowsum(do * y))
        dy = jnp.where(j < 2 * N_HEADS, dy_n, do)
        dc = dy * (sg * (1.0 + c * (1.0 - sg)))
        row = lax.broadcasted_iota(jnp.int32, x.shape, 0)
        dx = dc * w[3:4, :]
        dw_ref[3:4, :] = _colsum(dc * x)
        for s in range(1, 4):
            dx = dx + jnp.where(row < s_dim - s, pltpu.roll(dc, s_dim - s, 0), 0.0) * w[3 - s:4 - s, :]
            xs = jnp.where(row >= s, pltpu.roll(x, s, 0), 0.0)
            dw_ref[3 - s:4 - s, :] = _colsum(dc * xs)
        dx_ref[...] = dx.astype(dx_ref.dtype)

    hd = N_HEADS - 1
    return pl.pallas_call(
        body,
        out_shape=[jax.ShapeDtypeStruct((s_dim, n_blk * HEAD), BF16), jax.ShapeDtypeStruct((4, n_blk * HEAD), F32)],
        grid=(n_blk,),
        in_specs=[
            pl.BlockSpec((s_dim, HEAD), lambda j: (0, j)),
            pl.BlockSpec((4, HEAD), lambda j: (0, j)),
            pl.BlockSpec((s_dim, HEAD), lambda j: (0, jnp.minimum(j, hd))),
            pl.BlockSpec((s_dim, HEAD), lambda j: (0, jnp.clip(j - N_HEADS, 0, hd))),
            pl.BlockSpec((s_dim, HEAD), lambda j: (0, jnp.clip(j - 2 * N_HEADS, 0, hd))),
        ],
        out_specs=[pl.BlockSpec((s_dim, HEAD), lambda j: (0, j)), pl.BlockSpec((4, HEAD), lambda j: (0, j))],
        compiler_params=pltpu.CompilerParams(dimension_semantics=("parallel",)),
        name="conv_bwd",
    )(proj_a, conv_w, dq, dk, dv)


def _chunk_tri(n):
    r = np.arange(n)
    m = ((r[:, None] // CHUNK) == (r[None, :] // CHUNK)) & (r[:, None] >= r[None, :])
    m = m.astype(np.float32)
    return jnp.asarray(m), jnp.asarray(m.T)


def _softplus(z):
    return jnp.maximum(z, 0.0) + jnp.log(1.0 + jnp.exp(-jnp.abs(z)))


def _gates_fwd(proj_b, alog, dtb):
    tm = min(GROUP, proj_b.shape[0])
    tri, _ = _chunk_tri(tm)

    def fn(r, c):
        b = r[0]
        a = pltpu.roll(b, LANES - N_HEADS, 1)
        alog_, dtb_, tri_ = c
        g = -jnp.exp(alog_) * _softplus(a + dtb_)
        return [_sig(b), _dot32(tri_, g)], []

    return _rowwise(fn, [(proj_b, WB_BA // LANES, LANES)], [alog, dtb, tri],
                    [(LANES, F32), (LANES, F32)], tm=tm, name="gates_fwd")


def _gates_bwd(proj_b, alog, dtb, gc, d_beta, d_gc, d_egl_rows):
    tm = min(GROUP, proj_b.shape[0])
    _, tri_t = _chunk_tri(tm)

    def fn(r, c):
        b, gc_, d_beta_, d_gc_, d_egl_ = r
        a = pltpu.roll(b, LANES - N_HEADS, 1)
        alog_, dtb_, tri_t_ = c
        z = a + dtb_
        ea = jnp.exp(alog_)
        g = -ea * _softplus(z)
        dg = _dot32(tri_t_, d_gc_ + d_egl_ * jnp.exp(gc_))
        d_a = dg * (-ea) * _sig(z)
        beta = _sig(b)
        d_ba = d_beta_ * beta * (1.0 - beta) + pltpu.roll(d_a, N_HEADS, 1)
        return [d_ba], [_colsum(dg * g), _colsum(d_a)]

    return _rowwise(fn, [(proj_b, WB_BA // LANES, LANES), gc, d_beta, d_gc, d_egl_rows],
                    [alog, dtb, tri_t], [(LANES, BF16)], accs=[(1, LANES), (1, LANES)], tm=tm,
                    name="gates_bwd")


def _group_masks(n):
    r = lax.broadcasted_iota(jnp.int32, (n, n), 0)
    c = lax.broadcasted_iota(jnp.int32, (n, n), 1)
    same = (r // CHUNK) == (c // CHUNK)
    tril = jnp.logical_and(same, r >= c)
    strict = jnp.logical_and(same, r > c)
    last = c == (r // CHUNK) * CHUNK + (CHUNK - 1)
    eye = r == c
    return same, tril, strict, last, eye


def _inv_unit_lower(l_mat, eye_f):
    q = -l_mat
    r = eye_f + q
    qs = _split_bf16(q)
    for _ in range(5):
        qs = _split_bf16(_dot3(qs, qs))
        r = r + _dot3(r, qs)
    return r


def _unfold_blocks(folded, mask):
    n = folded.shape[0]
    return jnp.where(mask, jnp.concatenate([folded] * (n // CHUNK), axis=1), 0.0)


def _head_cols(beta, gc, gc_t, h):
    lane = lax.broadcasted_iota(jnp.int32, beta.shape, 1)
    sub = lax.broadcasted_iota(jnp.int32, gc_t.shape, 0)
    bcol = _rowsum(jnp.where(lane == h, beta, 0.0))
    gcol = _rowsum(jnp.where(lane == h, gc, 0.0))
    grow = _colsum(jnp.where(sub == h, gc_t, 0.0))
    return bcol, gcol, grow


def _prep_common(q, k, bcol, gcol, grow, t_folded=None):
    n = q.shape[0]
    same, tril, strict, last, eye = _group_masks(n)
    decay = jnp.where(tril, jnp.exp(jnp.where(tril, gcol - grow, 0.0)), 0.0)
    glast = _rowsum(jnp.where(last, jnp.broadcast_to(grow, (n, n)), 0.0))
    e = jnp.exp(gcol)
    ekt = jnp.exp(glast - gcol)
    kb = k * bcol
    kk = _dot(kb, k, NT)
    if t_folded is None:
        t_mat = _inv_unit_lower(jnp.where(strict, kk * decay, 0.0), eye.astype(F32))
    else:
        t_mat = _unfold_blocks(t_folded, same)
    qk = _dot(q, k, NT)
    return dict(same=same, tril=tril, strict=strict, last=last, eye=eye, decay=decay, e=e, ekt=ekt, kb=kb, kk=kk,
                t=t_mat, qk=qk)


def _fold_blocks(m):
    n = m.shape[0]
    out = m[:, 0:CHUNK]
    for b in range(1, n // CHUNK):
        out = out + m[:, b * CHUNK:(b + 1) * CHUNK]
    return out


def _gdr_prep_fwd(qkvn, beta, gc, gc_t):
    s_dim = qkvn.shape[0]
    tg = min(GROUP, s_dim)

    def body(q_ref, k_ref, v_ref, b_ref, g_ref, gt_ref, u_ref, w_ref, qd_ref, kt_ref, a_ref, t_ref):
        h = pl.program_id(0)
        q, k, v = q_ref[...], k_ref[...], v_ref[...]
        bcol, gcol, grow = _head_cols(b_ref[...], g_ref[...], gt_ref[...], h)
        p = _prep_common(q, k, bcol, gcol, grow)
        u_ref[...] = _dot(p["t"], v * bcol)
        w_ref[...] = _dot(p["t"], p["kb"] * p["e"])
        qd_ref[...] = q * p["e"]
        kt_ref[...] = k * p["ekt"]
        a_ref[...] = _fold_blocks(jnp.where(p["tril"], p["qk"] * p["decay"], 0.0))
        t_ref[...] = _fold_blocks(p["t"])

    row = lambda off: pl.BlockSpec((tg, HEAD), functools.partial(lambda h, m, off: (m, h + off), off=off))
    full = pl.BlockSpec((tg, LANES), lambda h, m: (m, 0))
    o_spec = pl.BlockSpec((tg, HEAD), lambda h, m: (m, h))
    a_spec = pl.BlockSpec((None, tg, CHUNK), lambda h, m: (h, m, 0))
    wide = jax.ShapeDtypeStruct((s_dim, N_HEADS * HEAD), F32)
    folded = jax.ShapeDtypeStruct((N_HEADS, s_dim, CHUNK), F32)
    return pl.pallas_call(
        body,
        out_shape=[wide, wide, wide, wide, folded, folded],
        grid=(N_HEADS, s_dim // tg),
        in_specs=[row(0), row(N_HEADS), row(2 * N_HEADS), full, full, pl.BlockSpec((8, tg), lambda h, m: (0, m))],
        out_specs=[o_spec, o_spec, o_spec, o_spec, a_spec, a_spec],
        compiler_params=pltpu.CompilerParams(dimension_semantics=("parallel", "parallel")),
        name="gdr_prep_fwd",
    )(qkvn, qkvn, qkvn, beta, gc, gc_t)


def _gdr_prep_bwd(qkvn, beta, gc, gc_t, t_fold, du, dw, dqd, dkt, d_a):
    s_dim = qkvn.shape[0]
    tg = min(GROUP, s_dim)

    def body(q_ref, k_ref, v_ref, b_ref, g_ref, gt_ref, t_ref, du_ref, dw_ref, dqd_ref, dkt_ref, da_ref,
             dq_ref, dk_ref, dv_ref, db_ref, dg_ref):
        h = pl.program_id(1)
        q, k, v = q_ref[...], k_ref[...], v_ref[...]
        bcol, gcol, grow = _head_cols(b_ref[...], g_ref[...], gt_ref[...], h)
        p = _prep_common(q, k, bcol, gcol, grow, t_ref[...])
        t_mat, decay, e, ekt, kb = p["t"], p["decay"], p["e"], p["ekt"], p["kb"]
        du_, dw_, dqd_, dkt_ = du_ref[...], dw_ref[...], dqd_ref[...], dkt_ref[...]
        vb = v * bcol
        kbe = kb * e
        d_t = _dot(du_, vb, NT) + _dot(dw_, kbe, NT)
        dvb = _dot(t_mat, du_, TN)
        dkbe = _dot(t_mat, dw_, TN)
        ts = _split_bf16(t_mat)
        d_l = -_dot3(_dot3(ts, d_t, TN), ts, NT)
        m1 = jnp.where(p["strict"], d_l, 0.0)
        m2 = _unfold_blocks(da_ref[...], p["tril"])
        d_kk = m1 * decay
        d_qk = m2 * decay
        d_decay = m1 * p["kk"] + m2 * p["qk"]
        dkb = _dot(d_kk, k) + dkbe * e
        dk = _dot(d_kk, kb, TN) + _dot(d_qk, q, TN) + dkt_ * ekt + dkb * bcol
        dq = _dot(d_qk, k) + dqd_ * e
        d_beta = _rowsum(dkb * k) + _rowsum(dvb * v)
        d_e = _rowsum(dkbe * kb) + _rowsum(dqd_ * q)
        d_ekt = _rowsum(dkt_ * k) * ekt
        d_diff = d_decay * decay
        d_grow = -_colsum(d_diff) + _colsum(jnp.where(p["last"], jnp.broadcast_to(d_ekt, (tg, tg)), 0.0))
        d_gcol = d_e * e - d_ekt + _rowsum(d_diff)
        d_gcol = d_gcol + _rowsum(jnp.where(p["eye"], jnp.broadcast_to(d_grow, (tg, tg)), 0.0))
        dq_ref[...] = dq
        dk_ref[...] = dk
        dv_ref[...] = dvb * bcol

        @pl.when(h == 0)
        def _():
            db_ref[...] = jnp.zeros_like(db_ref)
            dg_ref[...] = jnp.zeros_like(dg_ref)

        lane = lax.broadcasted_iota(jnp.int32, (tg, LANES), 1)
        db_ref[...] = jnp.where(lane == h, d_beta, db_ref[...])
        dg_ref[...] = jnp.where(lane == h, d_gcol, dg_ref[...])

    row = lambda off: pl.BlockSpec((tg, HEAD), functools.partial(lambda m, h, off: (m, h + off), off=off))
    full = pl.BlockSpec((tg, LANES), lambda m, h: (m, 0))
    o_spec = pl.BlockSpec((tg, HEAD), lambda m, h: (m, h))
    a_spec = pl.BlockSpec((None, tg, CHUNK), lambda m, h: (h, m, 0))
    wide = jax.ShapeDtypeStruct((s_dim, N_HEADS * HEAD), F32)
    lanes = jax.ShapeDtypeStruct((s_dim, LANES), F32)
    return pl.pallas_call(
        body,
        out_shape=[wide, wide, wide, lanes, lanes],
        grid=(s_dim // tg, N_HEADS),
        in_specs=[row(0), row(N_HEADS), row(2 * N_HEADS), full, full, pl.BlockSpec((8, tg), lambda m, h: (0, m)),
                  a_spec, o_spec, o_spec, o_spec, o_spec, a_spec],
        out_specs=[o_spec, o_spec, o_spec, full, full],
        compiler_params=pltpu.CompilerParams(dimension_semantics=("parallel", "arbitrary")),
        name="gdr_prep_bwd",
    )(qkvn, qkvn, qkvn, beta, gc, gc_t, t_fold, du, dw, dqd, dkt, d_a)


def _gdr_scan_fwd(u, w, qd, kt, a_mat, gc):
    s_dim = u.shape[0]
    n_chunks = s_dim // CHUNK

    def body(u_ref, w_ref, qd_ref, kt_ref, a_ref, g_ref, o_ref, st_ref, state):
        @pl.when(pl.program_id(0) == 0)
        def _():
            state[...] = jnp.zeros_like(state)

        egl = jnp.exp(g_ref[CHUNK - 1:CHUNK, :])
        for h in range(N_HEADS):
            cs = slice(h * HEAD, (h + 1) * HEAD)
            s_h = state[h]
            st_ref[h] = s_h
            vn = u_ref[:, cs] - _dot(w_ref[:, cs], s_h)
            o_ref[:, cs] = _dot(qd_ref[:, cs], s_h) + _dot(a_ref[h], vn)
            state[h] = s_h * egl[:, h:h + 1] + _dot(kt_ref[:, cs], vn, TN)

    wide = pl.BlockSpec((CHUNK, N_HEADS * HEAD), lambda n: (n, 0))
    return pl.pallas_call(
        body,
        out_shape=[jax.ShapeDtypeStruct((s_dim, N_HEADS * HEAD), F32),
                   jax.ShapeDtypeStruct((n_chunks, N_HEADS, HEAD, HEAD), F32)],
        grid=(n_chunks,),
        in_specs=[wide, wide, wide, wide, pl.BlockSpec((N_HEADS, CHUNK, CHUNK), lambda n: (0, n, 0)),
                  pl.BlockSpec((CHUNK, LANES), lambda n: (n, 0))],
        out_specs=[wide, pl.BlockSpec((None, N_HEADS, HEAD, HEAD), lambda n: (n, 0, 0, 0))],
        scratch_shapes=[pltpu.VMEM((N_HEADS, HEAD, HEAD), F32)],
        compiler_params=pltpu.CompilerParams(dimension_semantics=("arbitrary",)),
        name="gdr_scan_fwd",
    )(u, w, qd, kt, a_mat, gc)


def _gdr_scan_bwd(u, w, qd, kt, a_mat, gc, states, d_o):
    s_dim = u.shape[0]
    n_chunks = s_dim // CHUNK
    last = n_chunks - 1

    def body(u_ref, w_ref, qd_ref, kt_ref, a_ref, g_ref, st_ref, do_ref,
             du_ref, dw_ref, dqd_ref, dkt_ref, da_ref, de_ref, d_state):
        @pl.when(pl.program_id(0) == 0)
        def _():
            d_state[...] = jnp.zeros_like(d_state)

        egl = jnp.exp(g_ref[CHUNK - 1:CHUNK, :])
        for h in range(N_HEADS):
            cs = slice(h * HEAD, (h + 1) * HEAD)
            s_h = st_ref[h]
            ds_n = d_state[h]
            do = do_ref[:, cs]
            w_h = w_ref[:, cs]
            vn = u_ref[:, cs] - _dot(w_h, s_h)
            dvn = _dot(a_ref[h], do, TN) + _dot(kt_ref[:, cs], ds_n)
            dqd_ref[:, cs] = _dot(do, s_h, NT)
            da_ref[h] = _dot(do, vn, NT)
            dkt_ref[:, cs] = _dot(vn, ds_n, NT)
            de = jnp.sum(_rowsum(ds_n * s_h), axis=0, keepdims=True)
            de_ref[h:h + 1, :] = jnp.broadcast_to(de, (1, LANES))
            du_ref[:, cs] = dvn
            dw_ref[:, cs] = -_dot(dvn, s_h, NT)
            d_state[h] = ds_n * egl[:, h:h + 1] + _dot(qd_ref[:, cs], do, TN) - _dot(w_h, dvn, TN)

    wide = pl.BlockSpec((CHUNK, N_HEADS * HEAD), lambda n: (last - n, 0))
    a_spec = pl.BlockSpec((N_HEADS, CHUNK, CHUNK), lambda n: (0, last - n, 0))
    wide_shape = jax.ShapeDtypeStruct((s_dim, N_HEADS * HEAD), F32)
    return pl.pallas_call(
        body,
        out_shape=[wide_shape, wide_shape, wide_shape, wide_shape,
                   jax.ShapeDtypeStruct((N_HEADS, s_dim, CHUNK), F32),
                   jax.ShapeDtypeStruct((n_chunks, N_HEADS, LANES), F32)],
        grid=(n_chunks,),
        in_specs=[wide, wide, wide, wide, a_spec, pl.BlockSpec((CHUNK, LANES), lambda n: (last - n, 0)),
                  pl.BlockSpec((None, N_HEADS, HEAD, HEAD), lambda n: (last - n, 0, 0, 0)), wide],
        out_specs=[wide, wide, wide, wide, a_spec, pl.BlockSpec((None, N_HEADS, LANES), lambda n: (last - n, 0, 0))],
        scratch_shapes=[pltpu.VMEM((N_HEADS, HEAD, HEAD), F32)],
        compiler_params=pltpu.CompilerParams(dimension_semantics=("arbitrary",)),
        name="gdr_scan_bwd",
    )(u, w, qd, kt, a_mat, gc, states, d_o)


def _gdr_out_fwd(o_dn, proj_a, dn_w):
    def fn(r, c):
        o, z = r
        (w_,) = c
        outs = []
        for h in range(N_HEADS):
            cs = slice(h * HEAD, (h + 1) * HEAD)
            oh, zh = o[:, cs], z[:, cs]
            rr = lax.rsqrt(_rowmean(oh * oh) + EPS_RMS)
            outs.append(oh * rr * w_ * (zh * _sig(zh)))
        return [jnp.concatenate(outs, axis=1)], []

    return _rowwise(fn, [o_dn, (proj_a, 3, D_MODEL)], [dn_w], [(D_MODEL, BF16)], name="gdr_out_fwd")[0]


def _gdr_out_bwd(o_dn, proj_a, d_og, dn_w):
    def fn(r, c):
        o, z, dg = r
        (w_,) = c
        d_o, d_z = [], []
        d_w = jnp.zeros((1, HEAD), F32)
        for h in range(N_HEADS):
            cs = slice(h * HEAD, (h + 1) * HEAD)
            oh, zh, dgh = o[:, cs], z[:, cs], dg[:, cs]
            rr = lax.rsqrt(_rowmean(oh * oh) + EPS_RMS)
            sz = zh * _sig(zh)
            d_n = dgh * sz
            d_z.append(dgh * (oh * rr * w_) * _silu_grad(zh))
            d_w = d_w + _colsum(d_n * oh * rr)
            gw = d_n * w_
            d_o.append(rr * gw - oh * (rr * rr * rr) * _rowmean(gw * oh))
        return [jnp.concatenate(d_o, axis=1), jnp.concatenate(d_z, axis=1)], [d_w]

    return _rowwise(fn, [o_dn, (proj_a, 3, D_MODEL), d_og], [dn_w], [(D_MODEL, F32), (D_MODEL, BF16)],
                    accs=[(1, HEAD)], name="gdr_out_bwd")


def _rms_fwd(x, w):
    r = lax.rsqrt(_rowmean(x * x) + EPS_RMS)
    return x * r * w


def _rms_bwd(x, w, dy):
    r = lax.rsqrt(_rowmean(x * x) + EPS_RMS)
    gw = dy * w
    return r * gw - x * (r * r * r) * _rowmean(gw * x), _colsum(dy * x * r)


def _mla_norm_fwd(proj_b, qn_w, kvn_w):
    def fn(r, c):
        return [_rms_fwd(r[0], c[0]), _rms_fwd(r[1], c[1])], []

    return _rowwise(fn, [(proj_b, WB_CQ // Q_LORA, Q_LORA), (proj_b, WB_CKV // KV_LORA, KV_LORA)], [qn_w, kvn_w],
                    [(Q_LORA, BF16), (KV_LORA, BF16)], name="mla_norm_fwd")


def _mla_norm_bwd(proj_b, qn_w, kvn_w, d_cq, d_ckv):
    def fn(r, c):
        dx1, dw1 = _rms_bwd(r[0], c[0], r[2])
        dx2, dw2 = _rms_bwd(r[1], c[1], r[3])
        return [dx1, dx2], [dw1, dw2]

    return _rowwise(fn, [(proj_b, WB_CQ // Q_LORA, Q_LORA), (proj_b, WB_CKV // KV_LORA, KV_LORA), d_cq, d_ckv],
                    [qn_w, kvn_w], [(Q_LORA, BF16), (KV_LORA, BF16)], accs=[(1, Q_LORA), (1, KV_LORA)],
                    name="mla_norm_bwd")


def _rope_consts():
    inv = ROPE_BASE ** (-np.arange(0, ROPE, 2, dtype=np.float32) / ROPE)
    t = np.zeros((4, LANES), np.float32)
    t[0, :32] = inv
    t[0, 32:64] = inv
    t[1, :64] = 1.0
    t[2, 32:64] = 1.0
    t[3, :32] = -1.0
    return jnp.asarray(t)


def _rope_tables(pos, consts, width):
    ang = pos * consts[0:1, :]
    cosv, sinv = jnp.cos(ang), jnp.sin(ang)
    reps = width // LANES
    tile = (lambda t: jnp.concatenate([t] * reps, axis=1)) if reps > 1 else (lambda t: t)
    return tile(cosv * consts[1:2, :]), tile(sinv * consts[2:3, :]), tile(sinv * consts[3:4, :])


def _rope_apply(t, tabs):
    cos_t, sin_a, sin_b = tabs
    width = t.shape[1]
    return t * cos_t + pltpu.roll(t, 32, 1) * sin_a + pltpu.roll(t, width - 32, 1) * sin_b


def _rope_transpose(d, tabs):
    cos_t, sin_a, sin_b = tabs
    width = d.shape[1]
    return d * cos_t + pltpu.roll(d * sin_a, width - 32, 1) + pltpu.roll(d * sin_b, 32, 1)


QK_HEAD = 2 * HEAD


def _interleave_heads(a, b):
    parts = []
    for h in range(N_HEADS):
        parts.append(a[:, h * HEAD:(h + 1) * HEAD])
        parts.append(b if b.shape[1] == LANES else b[:, h * LANES:(h + 1) * LANES])
    return jnp.concatenate(parts, axis=1)


def _mla_qk_fwd(q_full, k_nope, proj_b, pos):
    consts = _rope_consts()

    def fn(r, c):
        qf, kn, kr, pos_ = r
        qn, qr = qf[:, :D_MODEL], qf[:, D_MODEL:]
        qr = _rope_apply(qr, _rope_tables(pos_, c[0], D_MODEL))
        kr = _rope_apply(kr, _rope_tables(pos_, c[0], LANES))
        return [_interleave_heads(qn, qr) * SCALE, _interleave_heads(kn, kr)], []

    return _rowwise(fn, [q_full, k_nope, (proj_b, WB_KR // LANES, LANES), pos], [consts],
                    [(N_HEADS * QK_HEAD, BF16), (N_HEADS * QK_HEAD, BF16)], name="mla_qk_fwd")


def _mla_qk_bwd(d_qc, d_kc, pos):
    consts = _rope_consts()

    def fn(r, c):
        dq, dk, pos_ = r
        even = lambda t: jnp.concatenate([t[:, (2 * h) * LANES:(2 * h + 1) * LANES] for h in range(N_HEADS)], axis=1)
        odd = lambda t: jnp.concatenate([t[:, (2 * h + 1) * LANES:(2 * h + 2) * LANES] for h in range(N_HEADS)], axis=1)
        d_qr_raw = _rope_transpose(odd(dq), _rope_tables(pos_, c[0], D_MODEL)) * SCALE
        dkr = dk[:, LANES:2 * LANES]
        for h in range(1, N_HEADS):
            dkr = dkr + dk[:, (2 * h + 1) * LANES:(2 * h + 2) * LANES]
        return [jnp.concatenate([even(dq) * SCALE, d_qr_raw], axis=1), even(dk),
                _rope_transpose(dkr, _rope_tables(pos_, c[0], LANES))], []

    return _rowwise(fn, [d_qc, d_kc, pos], [consts], [(2 * D_MODEL, BF16), (D_MODEL, BF16), (LANES, BF16)],
                    name="mla_qk_bwd")


def _causal_mask_t(st, key0, query0):
    key = lax.broadcasted_iota(jnp.int32, st.shape, 0) + key0
    query = lax.broadcasted_iota(jnp.int32, st.shape, 1) + query0
    return jnp.where(key <= query, st, NEG_BIG)


def _attn_tiles(s_dim):
    tq = min(512, s_dim)
    n_chains = 2 if s_dim >= 2 * tq else 1
    return tq, n_chains, min(512, s_dim)


def _attn_fwd(qc, kc, vt):
    s_dim = qc.shape[0]
    tq, n_chains, tk = _attn_tiles(s_dim)
    tqs = tq * n_chains

    def body(q_ref, k_ref, vt_ref, o_ref, lse_ref, m_s, l_s, acc):
        qi = pl.program_id(1)
        m_s[...] = jnp.full_like(m_s, NEG_BIG)
        l_s[...] = jnp.zeros_like(l_s)
        acc[...] = jnp.zeros_like(acc)

        def step(j, carry):
            ks = pl.multiple_of(j * tk, tk)
            kb, vtb = k_ref[pl.ds(ks, tk), :], vt_ref[:, pl.ds(ks, tk)]
            for c in range(n_chains):
                cols = slice(c * tq, (c + 1) * tq)
                st = _causal_mask_t(_dot(kb, q_ref[cols, :], NT), j * tk, qi * tqs + c * tq)
                m_prev = m_s[:, cols]
                m_new = jnp.maximum(m_prev, jnp.max(st, axis=0, keepdims=True))
                alpha = jnp.exp(m_prev - m_new)
                pt = jnp.exp(st - m_new)
                l_s[:, cols] = alpha * l_s[:, cols] + _colsum(pt)
                m_s[:, cols] = m_new
                acc[:, cols] = acc[:, cols] * alpha + _dot(vtb, pt)
            return carry

        lax.fori_loop(0, (qi + 1) * (tqs // tk), step, 0)
        l = l_s[...]
        o_ref[...] = jnp.transpose(acc[...] / l)
        lse_ref[...] = m_s[...] + jnp.log(l)

    return pl.pallas_call(
        body,
        out_shape=[jax.ShapeDtypeStruct((s_dim, N_HEADS * HEAD), F32), jax.ShapeDtypeStruct((N_HEADS, 1, s_dim), F32)],
        grid=(N_HEADS, s_dim // tqs),
        in_specs=[pl.BlockSpec((tqs, QK_HEAD), lambda h, qi: (qi, h)),
                  pl.BlockSpec((s_dim, QK_HEAD), lambda h, qi: (0, h)),
                  pl.BlockSpec((HEAD, s_dim), lambda h, qi: (h, 0))],
        out_specs=[pl.BlockSpec((tqs, HEAD), lambda h, qi: (qi, h)),
                   pl.BlockSpec((None, 1, tqs), lambda h, qi: (h, 0, qi))],
        scratch_shapes=[pltpu.VMEM((1, tqs), F32), pltpu.VMEM((1, tqs), F32), pltpu.VMEM((HEAD, tqs), F32)],
        compiler_params=pltpu.CompilerParams(dimension_semantics=("parallel", "parallel")),
        name="attn_fwd",
    )(qc, kc, vt)


def _attn_bwd(qc, kc, kct, v, o, d_o, lse):
    s_dim = qc.shape[0]
    tq, n_chains, tk = _attn_tiles(s_dim)
    tqs = tq * n_chains

    def body(q_ref, k_ref, kt_ref, v_ref, o_ref, do_ref, lse_ref, dq_ref, dk_ref, dv_ref, dqt_acc, dv_acc):
        qi = pl.program_id(1)

        @pl.when(qi == 0)
        def _():
            dk_ref[...] = jnp.zeros_like(dk_ref)
            dv_acc[...] = jnp.zeros_like(dv_acc)

        dqt_acc[...] = jnp.zeros_like(dqt_acc)
        do_f = do_ref[...]
        do_all = do_f.astype(BF16)
        q_all = q_ref[...]
        lse_row = lse_ref[...]
        delta_row = _dot3(jnp.ones((8, HEAD), F32), o_ref[...] * do_f, NT)[0:1, :]

        def step(j, carry):
            ks = pl.multiple_of(j * tk, tk)
            kb, vb, ktb = k_ref[pl.ds(ks, tk), :], v_ref[pl.ds(ks, tk), :], kt_ref[:, pl.ds(ks, tk)]
            pts, dsts = [], []
            for c in range(n_chains):
                cols = slice(c * tq, (c + 1) * tq)
                st = _causal_mask_t(_dot(kb, q_all[cols, :], NT), j * tk, qi * tqs + c * tq)
                pt = jnp.exp(st - lse_row[:, cols])
                dst = pt * (_dot(vb, do_all[cols, :], NT) - delta_row[:, cols])
                dst_b = dst.astype(BF16)
                dqt_acc[:, cols] += _dot(ktb, dst_b)
                pts.append(pt.astype(BF16))
                dsts.append(dst_b)
            pt_all = jnp.concatenate(pts, axis=1) if n_chains > 1 else pts[0]
            dst_all = jnp.concatenate(dsts, axis=1) if n_chains > 1 else dsts[0]
            dk_ref[pl.ds(ks, tk), :] += _dot(dst_all, q_all)
            dv_acc[pl.ds(ks, tk), :] += _dot(pt_all, do_all)
            return carry

        lax.fori_loop(0, (qi + 1) * (tqs // tk), step, 0)
        dq_ref[...] = jnp.transpose(dqt_acc[...])

        @pl.when(qi == s_dim // tqs - 1)
        def _():
            dv_ref[...] = dv_acc[...].astype(dv_ref.dtype)

    q_spec = pl.BlockSpec((tqs, QK_HEAD), lambda h, qi: (qi, h))
    o_spec = pl.BlockSpec((tqs, HEAD), lambda h, qi: (qi, h))
    k_spec = pl.BlockSpec((s_dim, QK_HEAD), lambda h, qi: (0, h))
    v_spec = pl.BlockSpec((s_dim, HEAD), lambda h, qi: (0, h))
    wide2 = jax.ShapeDtypeStruct((s_dim, N_HEADS * QK_HEAD), F32)
    return pl.pallas_call(
        body,
        out_shape=[wide2, wide2, jax.ShapeDtypeStruct((s_dim, N_HEADS * HEAD), BF16)],
        grid=(N_HEADS, s_dim // tqs),
        in_specs=[q_spec, k_spec, pl.BlockSpec((QK_HEAD, s_dim), lambda h, qi: (h, 0)), v_spec, o_spec, o_spec,
                  pl.BlockSpec((None, 1, tqs), lambda h, qi: (h, 0, qi))],
        out_specs=[q_spec, k_spec, v_spec],
        scratch_shapes=[pltpu.VMEM((QK_HEAD, tqs), F32), pltpu.VMEM((s_dim, HEAD), F32)],
        compiler_params=pltpu.CompilerParams(dimension_semantics=("parallel", "arbitrary")),
        name="attn_bwd",
    )(qc, kc, kct, v, o, d_o, lse)


def _merge_fwd(y_dn, y_mla, proj_g):
    def fn(r, c):
        yd, ym, g = r
        return [_sig(g[:, :D_MODEL]) * yd + _sig(g[:, D_MODEL:]) * ym], []

    return _rowwise(fn, [y_dn, y_mla, proj_g], [], [(D_MODEL, BF16)], name="merge_fwd")[0]


def _merge_bwd(y_dn, y_mla, proj_g, d_mixed):
    def fn(r, c):
        yd, ym, g, dm = r
        sd, sm = _sig(g[:, :D_MODEL]), _sig(g[:, D_MODEL:])
        d_g = jnp.concatenate([dm * yd * sd * (1.0 - sd), dm * ym * sm * (1.0 - sm)], axis=1)
        return [d_g, dm * sd, dm * sm], []

    return _rowwise(fn, [y_dn, y_mla, proj_g, d_mixed], [], [(2 * D_MODEL, BF16), (D_MODEL, BF16), (D_MODEL, BF16)],
                    name="merge_bwd")


def _ln_stats(z):
    mu = _rowmean(z)
    zc = z - mu
    r = lax.rsqrt(_rowmean(zc * zc) + EPS_LN)
    return zc * r, r


def _ln_bwd(dy, xh, r, g):
    dxh = dy * g
    return r * (dxh - _rowmean(dxh) - xh * _rowmean(dxh * xh))


def _ln1_fwd(x, a1, g, b):
    def fn(r, c):
        xh, _ = _ln_stats(ALPHA * r[0] + r[1])
        y = xh * c[0] + c[1]
        return [y, y], []

    return _rowwise(fn, [x, a1], [g, b], [(D_MODEL, F32), (D_MODEL, BF16)], name="ln1_fwd")


def _ln1_bwd(x, a1, d_h1, g):
    def fn(r, c):
        xh, rr = _ln_stats(ALPHA * r[0] + r[1])
        dy = r[2]
        dz = _ln_bwd(dy, xh, rr, c[0])
        return [dz, ALPHA * dz], [_colsum(dy * xh), _colsum(dy)]

    return _rowwise(fn, [x, a1, d_h1], [g], [(D_MODEL, BF16), (D_MODEL, F32)], accs=[(1, D_MODEL), (1, D_MODEL)],
                    name="ln1_bwd")


def _act_fwd(gu):
    def fn(r, c):
        gt, up = r[0][:, :FFN_HIDDEN], r[0][:, FFN_HIDDEN:]
        return [gt * _sig(gt) * up], []

    return _rowwise(fn, [gu], [], [(FFN_HIDDEN, BF16)], name="act_fwd")[0]


def _act_bwd(gu, d_act):
    def fn(r, c):
        gt, up = r[0][:, :FFN_HIDDEN], r[0][:, FFN_HIDDEN:]
        da = r[1]
        return [jnp.concatenate([da * up * _silu_grad(gt), da * gt * _sig(gt)], axis=1)], []

    return _rowwise(fn, [gu, d_act], [], [(2 * FFN_HIDDEN, BF16)], name="act_bwd")[0]


def _tail(h1, ffn, pg, pp, tgt, g, b):
    def fn(r, c):
        h1_, ffn_, pg_, pp_, t_ = r
        sp = _sig(pg_)
        xh, rr = _ln_stats(ALPHA * h1_ + ffn_ + sp * pp_)
        y = xh * c[0] + c[1]
        err = y - t_
        dy = err * (1.0 / D_MODEL)
        dz = _ln_bwd(dy, xh, rr, c[0])
        loss = jnp.sum(0.5 * _rowmean(err * err), axis=0, keepdims=True)
        return ([dz, dz * pp_ * sp * (1.0 - sp), dz * sp, ALPHA * dz],
                [_colsum(dy * xh), _colsum(dy), jnp.broadcast_to(loss, (1, LANES))])

    return _rowwise(fn, [h1, ffn, pg, pp, tgt], [g, b], [(D_MODEL, BF16)] * 3 + [(D_MODEL, F32)],
                    accs=[(1, D_MODEL), (1, D_MODEL), (1, LANES)], name="tail")


def _local_step(x, p, pos, tgt, w, late_weights, emit):
    w = dict(w)
    s_dim = x.shape[0]
    xb, pb = x.astype(BF16), p.astype(BF16)
    proj_a = _mm(xb, w["wa"], name="f_proj_a")
    proj_g = _mm(xb, w["wg"], name="f_proj_g")
    proj_b = _mm(xb, w["wb"], name="f_proj_b")
    qkvn = _conv_fwd(proj_a, w["conv"])
    beta, gc = _gates_fwd(proj_b, w["alog"], w["dtb"])
    gc_t = jnp.transpose(gc[:, :N_HEADS])
    u, w_, qd, kt, a_mat, t_fold = _gdr_prep_fwd(qkvn, beta, gc, gc_t)
    o_dn, states = _gdr_scan_fwd(u, w_, qd, kt, a_mat, gc)
    og = _gdr_out_fwd(o_dn, proj_a, w["dnw"])
    w.update(late_weights("mix", og))
    y_dn = _mm(og, w["br_dn"], name="f_y_dn")
    c_q, c_kv = _mla_norm_fwd(proj_b, w["qnw"], w["kvnw"])
    q_full = _mm(c_q, w["uq"], name="f_q_full")
    k_nope = _mm(c_kv, w["uk"], name="f_k_nope")
    vv = _mm(c_kv, w["uv"], out_dtype=BF16, name="f_v")
    qc, kc = _mla_qk_fwd(q_full, k_nope, proj_b, pos)
    o_mla, lse = _attn_fwd(qc, kc, jnp.transpose(vv))
    y_mla = _mm(o_mla, w["br_mla"], name="f_y_mla")
    mixed = _merge_fwd(y_dn, y_mla, proj_g)
    a1 = _mm(mixed, w["wo"], name="f_a1")
    w.update(late_weights("ffn", a1))
    h1, h1b = _ln1_fwd(x, a1, w["ln1g"], w["ln1b"])
    gu = _mm(h1b, w["ffn_in"], name="f_gu")
    act = _act_fwd(gu)
    ffn = _mm(act, w["ffn_out"], name="f_ffn")
    pg = _mm(h1b, w["ple_gate"], name="f_pg")
    pp = _mm(pb, w["ple"], name="f_pp")
    g = {}
    dz2, d_pg, d_pp, dh1a, g["ln2g"], g["ln2b"], loss = _tail(h1, ffn, pg, pp, tgt, w["ln2g"], w["ln2b"])
    g["ple_t"] = _mm(d_pp, pb, ta=True, out_dtype=BF16, name="b_w_ple")
    g["ple_gate"] = _mm(h1b, d_pg, ta=True, out_dtype=BF16, name="b_w_ple_gate")
    g["ffn_out"] = _mm(act, dz2, ta=True, out_dtype=BF16, name="b_w_ffn_out")
    d_act = _mm(dz2, w["ffn_out"], tb=True, name="b_act")
    d_gu = _act_bwd(gu, d_act)
    g["ffn_in_t"] = _mm(d_gu, h1b, ta=True, out_dtype=BF16, name="b_w_ffn_in")
    d_gu = emit("ffn", g, d_gu)
    d_h1 = _mm(d_gu, w["ffn_in_t"], add=(dh1a,), name="b_h1_ffn")
    d_h1 = _mm(d_pg, w["ple_gate"], tb=True, add=(d_h1,), name="b_h1_ple")
    dz1, dxa, g["ln1g"], g["ln1b"] = _ln1_bwd(x, a1, d_h1, w["ln1g"])
    g["wo"] = _mm(mixed, dz1, ta=True, out_dtype=BF16, name="b_w_o")
    d_mixed = _mm(dz1, w["wo"], tb=True, name="b_mixed")
    d_proj_g, d_y_dn, d_y_mla = _merge_bwd(y_dn, y_mla, proj_g, d_mixed)
    g["br_mla"] = _mm(o_mla, d_y_mla, ta=True, out_dtype=BF16, name="b_w_br_mla")
    d_o_mla = _mm(d_y_mla, w["br_mla"], tb=True, name="b_o_mla")
    d_qc, d_kc, d_v = _attn_bwd(qc, kc, jnp.transpose(kc), vv, o_mla, d_o_mla, lse)
    d_q_full, d_kn, d_kr = _mla_qk_bwd(d_qc, d_kc, pos)
    g["uq"] = _mm(c_q, d_q_full, ta=True, out_dtype=BF16, name="b_w_uq")
    d_c_q = _mm(d_q_full, w["uq"], tb=True, name="b_c_q")
    g["uk"] = _mm(c_kv, d_kn, ta=True, out_dtype=BF16, name="b_w_uk")
    g["uv"] = _mm(c_kv, d_v, ta=True, out_dtype=BF16, name="b_w_uv")
    d_c_kv = _mm(d_kn, w["uk"], tb=True, name="b_c_kv_k")
    d_c_kv = _mm(d_v, w["uv"], tb=True, add=(d_c_kv,), name="b_c_kv_v")
    d_cq, d_ckv, g["qnw"], g["kvnw"] = _mla_norm_bwd(proj_b, w["qnw"], w["kvnw"], d_c_q, d_c_kv)
    g["br_dn"] = _mm(og, d_y_dn, ta=True, out_dtype=BF16, name="b_w_br_dn")
    d_og = emit("mix", g, _mm(d_y_dn, w["br_dn"], tb=True, name="b_og"))
    d_o_dn, d_z, g["dnw"] = _gdr_out_bwd(o_dn, proj_a, d_og, w["dnw"])
    du, dw, dqd, dkt, d_a, d_egl = _gdr_scan_bwd(u, w_, qd, kt, a_mat, gc, states, d_o_dn)
    dq, dk, dv, d_beta, d_gc = _gdr_prep_bwd(qkvn, beta, gc, gc_t, t_fold, du, dw, dqd, dkt, d_a)
    d_egl_rows = jnp.pad(d_egl[:, None, :, 0], ((0, 0), (CHUNK - 1, 0), (0, LANES - N_HEADS))).reshape(s_dim, LANES)
    d_ba, g["alog"], g["dtb"] = _gates_bwd(proj_b, w["alog"], w["dtb"], gc, d_beta, d_gc, d_egl_rows)
    d_qkv, g["conv"] = _conv_bwd(proj_a, w["conv"], dq, dk, dv)
    zeros = jnp.zeros((s_dim, WB_CKV - Q_LORA), BF16)
    d_proj_b = jnp.concatenate([d_cq, zeros, d_ckv, d_kr, d_ba], axis=1)
    g["wa_qkv_t"] = _mm(d_qkv, xb, ta=True, name="b_w_qkv")
    g["wa_z_t"] = _mm(d_z, xb, ta=True, name="b_w_z")
    g["wg_t"] = _mm(d_proj_g, xb, ta=True, name="b_w_g")
    g["wb_t"] = _mm(d_proj_b, xb, ta=True, name="b_w_b")
    dx = _mm(d_qkv, w["wa_qkv_t"], add=(dxa,), name="b_x_qkv")
    dx = _mm(d_z, w["wa_z_t"], add=(dx,), name="b_x_z")
    dx = _mm(d_proj_g, w["wg_t"], add=(dx,), name="b_x_g")
    dx = _mm(d_proj_b, w["wb_t"], add=(dx,), name="b_x_b")
    return loss, dx, g


_BIG = (("w_in", 1), ("w_uq", 0), ("w_uk", 0), ("w_uv", 0), ("w_br_dn", 0), ("w_br_mla", 0),
        ("w_o", 0), ("w_ffn_in", 1), ("w_ffn_out", 0), ("w_ple", 1), ("w_ple_gate", 0))
_BIG_AXIS = dict(_BIG)
_SMALL = ("ln1_g", "ln1_b", "ln2_g", "ln2_b", "q_norm_w", "kv_norm_w", "dn_norm_w", "dn_a_log", "dn_dt_bias")
_ORDER = ("w_in", "conv_w", "dn_a_log", "dn_dt_bias", "dn_norm_w", "q_norm_w", "w_uq", "kv_norm_w", "w_uk", "w_uv",
          "w_br_dn", "w_br_mla", "w_o", "ln1_g", "ln1_b", "w_ffn_in", "w_ffn_out", "w_ple", "w_ple_gate", "ln2_g",
          "ln2_b")


def _stored_shape(name, shard_shape):
    axis = _BIG_AXIS[name]
    lead = shard_shape[axis]
    return lead, int(np.prod(shard_shape)) // lead


def _to_stored(name, shard):
    return jnp.moveaxis(shard, _BIG_AXIS[name], 0).reshape(_stored_shape(name, shard.shape))


def _from_stored(name, stored, shard_shape):
    axis = _BIG_AXIS[name]
    moved = (shard_shape[axis],) + shard_shape[:axis] + shard_shape[axis + 1:]
    return jnp.moveaxis(stored.reshape(moved), 0, axis)


_W_IN_ROWS = np.cumsum([0, 3072, 1024, 8, 8, Q_LORA, KV_LORA, ROPE, D_MODEL, D_MODEL])


def _first_weights(w_in_t, conv_full, small):
    r = _W_IN_ROWS
    zr = lambda n: jnp.zeros((n, D_MODEL), w_in_t.dtype)
    w = {}
    w["wa_t"] = w_in_t[r[0]:r[2]]
    w["wa_qkv_t"], w["wa_z_t"] = w_in_t[r[0]:r[1]], w_in_t[r[1]:r[2]]
    w["wg_t"] = w_in_t[r[7]:r[9]]
    w["wb_t"] = jnp.concatenate([w_in_t[r[4]:r[5]], zr(WB_CKV - Q_LORA), w_in_t[r[5]:r[7]], zr(LANES - ROPE),
                                 w_in_t[r[2]:r[4]], zr(LANES - 2 * N_HEADS)], axis=0)
    for k_ in ("wa", "wg", "wb"):
        w[k_] = jnp.transpose(w[k_ + "_t"])
    w["conv"] = conv_full
    pad_l = lambda v: jnp.pad(v, ((0, 0), (0, LANES - v.shape[1])))
    w["alog"], w["dtb"] = pad_l(small["dn_a_log"]), pad_l(small["dn_dt_bias"])
    w["dnw"], w["qnw"], w["kvnw"] = small["dn_norm_w"], small["q_norm_w"], small["kv_norm_w"]
    w["ln1g"], w["ln1b"], w["ln2g"], w["ln2b"] = small["ln1_g"], small["ln1_b"], small["ln2_g"], small["ln2_b"]
    return w


def _late_weights(group, fw):
    w = {}
    if group == "mix":
        uq = fw["w_uq"].reshape(Q_LORA, N_HEADS, HEAD + ROPE)
        uq_r = jnp.pad(uq[:, :, HEAD:], ((0, 0), (0, 0), (0, HEAD - ROPE)))
        w["uq"] = jnp.concatenate([uq[:, :, :HEAD].reshape(Q_LORA, -1), uq_r.reshape(Q_LORA, -1)], axis=1)
        w["uk"], w["uv"] = fw["w_uk"], fw["w_uv"]
        w["br_dn"], w["br_mla"], w["wo"] = fw["w_br_dn"], fw["w_br_mla"], fw["w_o"]
    else:
        w["ffn_in_t"], w["ffn_out"] = fw["w_ffn_in"], fw["w_ffn_out"]
        w["ple_t"], w["ple_gate"] = fw["w_ple"], fw["w_ple_gate"]
        for k_ in ("ffn_in", "ple"):
            w[k_] = jnp.transpose(w[k_ + "_t"])
    return w


_GROUP_GRADS = {"ffn": (("w_ple", "ple_t"), ("w_ple_gate", "ple_gate"), ("w_ffn_out", "ffn_out"),
                        ("w_ffn_in", "ffn_in_t")),
                "mix": (("w_o", "wo"), ("w_br_mla", "br_mla"), ("w_uq", "uq"), ("w_uk", "uk"), ("w_uv", "uv"),
                        ("w_br_dn", "br_dn"))}


def _group_grads(group, g):
    out = {}
    for name, key in _GROUP_GRADS[group]:
        t = g[key]
        if name == "w_uq":
            uq_n = t[:, :D_MODEL].reshape(Q_LORA, N_HEADS, HEAD)
            uq_r = t[:, D_MODEL:].reshape(Q_LORA, N_HEADS, HEAD)[:, :, :ROPE]
            t = jnp.concatenate([uq_n, uq_r], axis=2).reshape(Q_LORA, -1)
        out[name] = t
    return out


def _last_grads(g):
    wb = g["wb_t"]
    w_in = jnp.concatenate([
        g["wa_qkv_t"], g["wa_z_t"], wb[WB_BA:WB_BA + 2 * N_HEADS], wb[WB_CQ:WB_CQ + Q_LORA],
        wb[WB_CKV:WB_CKV + KV_LORA], wb[WB_KR:WB_KR + ROPE], g["wg_t"]], axis=0)
    small = {"ln1_g": g["ln1g"], "ln1_b": g["ln1b"], "ln2_g": g["ln2g"], "ln2_b": g["ln2b"], "q_norm_w": g["qnw"],
             "kv_norm_w": g["kvnw"], "dn_norm_w": g["dnw"], "dn_a_log": g["alog"], "dn_dt_bias": g["dtb"],
             "conv_w": g["conv"]}
    return w_in, small


_SMALL_SLOTS = {"ln1_g": (0, 0, 1024), "ln1_b": (1, 0, 1024), "ln2_g": (2, 0, 1024), "ln2_b": (3, 0, 1024),
                "q_norm_w": (4, 0, 384), "kv_norm_w": (4, 384, 256), "dn_norm_w": (4, 640, 128),
                "dn_a_log": (4, 768, 8), "dn_dt_bias": (4, 896, 8)}
_SMALL_ROWS, _LOSS_ROW, _CONV_ROW0, _CONV_ROWS = 24, 5, 8, 12


def _pack_small_grads(small_g, loss):
    zeros = lambda r, c: jnp.zeros((r, c), F32)
    row4 = jnp.concatenate([small_g["q_norm_w"], small_g["kv_norm_w"], small_g["dn_norm_w"], small_g["dn_a_log"],
                            small_g["dn_dt_bias"]], axis=1)
    row5 = jnp.concatenate([loss, zeros(1, FLAT_COLS - LANES)], axis=1)
    head = jnp.concatenate([small_g["ln1_g"], small_g["ln1_b"], small_g["ln2_g"], small_g["ln2_b"], row4, row5,
                            zeros(2, FLAT_COLS)], axis=0)
    conv = small_g["conv_w"].reshape(_CONV_ROWS, FLAT_COLS)
    return jnp.concatenate([head, conv, zeros(_SMALL_ROWS - _CONV_ROW0 - _CONV_ROWS, FLAT_COLS)], axis=0)


_MESH_ID = pl.DeviceIdType.MESH
_ANY = pl.BlockSpec(memory_space=pl.ANY)


def _all_gather(blocks, name):
    n = len(blocks)

    def body(*refs):
        x_refs, out_refs = refs[:n], refs[n:2 * n]
        send_sems, recv_sems, local_sems = refs[2 * n:]
        x, y, c = lax.axis_index("x"), lax.axis_index("y"), lax.axis_index("c")
        me, sibling = (x, y, c), (x, y, 1 - c)
        chips = [(1 - x, y), (x, 1 - y), (1 - x, 1 - y)]

        def slot(i, px, py, pc):
            return out_refs[i].at[4 * px + 2 * py + pc]

        def copy(i, k, origin, to, src=None):
            return pltpu.make_async_remote_copy(
                src_ref=slot(i, *origin) if src is None else src, dst_ref=slot(i, *origin),
                send_sem=send_sems.at[7 * i + k], recv_sem=recv_sems.at[7 * i + k], device_id=to,
                device_id_type=_MESH_ID)

        mine = [pltpu.make_async_copy(x_refs[i], slot(i, *me), local_sems.at[i]) for i in range(n)]
        first, passed = [], []
        for i in range(n):
            mine[i].start()
            first.append(copy(i, 0, me, sibling, src=x_refs[i]))
            first += [copy(i, 1 + j, me, (*chip, c), src=x_refs[i]) for j, chip in enumerate(chips)]
        for cp in first:
            cp.start()
        for i in range(n):
            for j, chip in enumerate(chips):
                copy(i, 1 + j, (*chip, c), me).wait_recv()
                passed.append(copy(i, 4 + j, (*chip, c), sibling))
                passed[-1].start()
        for i in range(n):
            copy(i, 0, sibling, me).wait_recv()
            for j, chip in enumerate(chips):
                copy(i, 4 + j, (*chip, 1 - c), me).wait_recv()
        for cp in first + passed:
            cp.wait_send()
        for cp in mine:
            cp.wait()

    return pl.pallas_call(
        body,
        out_shape=[jax.ShapeDtypeStruct((N_DEV,) + b.shape, b.dtype) for b in blocks],
        in_specs=[_ANY] * n,
        out_specs=[_ANY] * n,
        scratch_shapes=[pltpu.SemaphoreType.DMA((7 * n,)), pltpu.SemaphoreType.DMA((7 * n,)),
                        pltpu.SemaphoreType.DMA((n,))],
        name=name,
    )(*blocks)


def _exchange_sibling(srcs, name):
    n = len(srcs)

    def body(*refs):
        src_refs, dst_refs = refs[:n], refs[n:2 * n]
        send_sems, recv_sems = refs[2 * n:]
        x, y, c = lax.axis_index("x"), lax.axis_index("y"), lax.axis_index("c")
        copies = [pltpu.make_async_remote_copy(
            src_ref=src_refs[i].at[2 * q + (1 - c)], dst_ref=dst_refs[i].at[q], send_sem=send_sems.at[4 * i + q],
            recv_sem=recv_sems.at[4 * i + q], device_id=(x, y, 1 - c), device_id_type=_MESH_ID)
            for i in range(n) for q in range(4)]
        for cp in copies:
            cp.start()
        for cp in copies:
            cp.wait_recv()
        for cp in copies:
            cp.wait_send()

    return pl.pallas_call(
        body,
        out_shape=[jax.ShapeDtypeStruct((4,) + s.shape[1:], s.dtype) for s in srcs],
        in_specs=[_ANY] * n,
        out_specs=[_ANY] * n,
        scratch_shapes=[pltpu.SemaphoreType.DMA((4 * n,)), pltpu.SemaphoreType.DMA((4 * n,))],
        name=name,
    )(*srcs)


def _exchange_chips(srcs, name):
    n = len(srcs)

    def body(*refs):
        src_refs, dst_refs = refs[:n], refs[n:2 * n]
        send_sems, recv_sems = refs[2 * n:]
        x, y, c = lax.axis_index("x"), lax.axis_index("y"), lax.axis_index("c")
        chips = [(1 - x, y), (x, 1 - y), (1 - x, 1 - y)]
        copies = [pltpu.make_async_remote_copy(
            src_ref=src_refs[i].at[2 * tx + ty], dst_ref=dst_refs[i].at[j], send_sem=send_sems.at[3 * i + j],
            recv_sem=recv_sems.at[3 * i + j], device_id=(tx, ty, c), device_id_type=_MESH_ID)
            for i in range(n) for j, (tx, ty) in enumerate(chips)]
        for cp in copies:
            cp.start()
        for cp in copies:
            cp.wait_recv()
        for cp in copies:
            cp.wait_send()

    return pl.pallas_call(
        body,
        out_shape=[jax.ShapeDtypeStruct((3,) + s.shape[1:], s.dtype) for s in srcs],
        in_specs=[_ANY] * n,
        out_specs=[_ANY] * n,
        scratch_shapes=[pltpu.SemaphoreType.DMA((3 * n,)), pltpu.SemaphoreType.DMA((3 * n,))],
        name=name,
    )(*srcs)


def _col_tile(c):
    return c if c <= 256 else 256


def _chip_sum(src, recv, parity, name):
    _, r, c = src.shape
    tc = _col_tile(c)

    def body(par_ref, a_ref, b_ref, o_ref, ob_ref):
        s = a_ref[...] + b_ref[...]
        o_ref[...] = s
        ob_ref[...] = s.astype(BF16)

    blk = lambda f: pl.BlockSpec((None, r, tc), f)
    return pl.pallas_call(
        body,
        out_shape=[jax.ShapeDtypeStruct((4, r, c), F32), jax.ShapeDtypeStruct((4, r, c), BF16)],
        grid_spec=pltpu.PrefetchScalarGridSpec(
            num_scalar_prefetch=1, grid=(4, c // tc),
            in_specs=[blk(lambda q, j, par: (2 * q + par[0], 0, j)), blk(lambda q, j, par: (q, 0, j))],
            out_specs=[blk(lambda q, j, par: (q, 0, j)), blk(lambda q, j, par: (q, 0, j))]),
        compiler_params=pltpu.CompilerParams(dimension_semantics=("parallel", "parallel")),
        name=name,
    )(parity, src, recv)


def _sum_parts(own, others, chip, name):
    _, r, c = own.shape
    tc = _col_tile(c)

    def body(q_ref, a_ref, b_ref, o_ref):
        o_ref[...] = ((a_ref[...] + b_ref[0].astype(F32)) + b_ref[1].astype(F32)) + b_ref[2].astype(F32)

    return pl.pallas_call(
        body,
        out_shape=jax.ShapeDtypeStruct((r, c), F32),
        grid_spec=pltpu.PrefetchScalarGridSpec(
            num_scalar_prefetch=1, grid=(c // tc,),
            in_specs=[pl.BlockSpec((None, r, tc), lambda j, q: (q[0], 0, j)),
                      pl.BlockSpec((3, r, tc), lambda j, q: (0, 0, j))],
            out_specs=pl.BlockSpec((r, tc), lambda j, q: (0, j))),
        compiler_params=pltpu.CompilerParams(dimension_semantics=("parallel",)),
        name=name,
    )(chip, own, others)


_HBM = pl.BlockSpec(memory_space=pltpu.HBM)
_SEM = pl.BlockSpec(memory_space=pltpu.SEMAPHORE)
_DATAFLOW = pltpu.SideEffectType.DATAFLOW_SIDE_EFFECTING
N_PEERS = N_DEV - 1


def _ring_peer(j):
    me = 4 * lax.axis_index("x") + 2 * lax.axis_index("y") + lax.axis_index("c")
    k = (me + j) % N_DEV
    return me, k, (k // 4, (k // 2) % 2, k % 2)


def _spread_copy(i, j, src_refs, land_refs, send_sems, recv_sems, scatter):
    me, k, peer = _ring_peer(j)
    return pltpu.make_async_remote_copy(
        src_ref=src_refs[i].at[k] if scatter else src_refs[i], dst_ref=land_refs[i].at[me],
        send_sem=send_sems.at[N_PEERS * i + j - 1], recv_sem=recv_sems.at[N_PEERS * i + j - 1], device_id=peer,
        device_id_type=_MESH_ID)


def _spread_start(srcs, carry, scatter, name):
    n = len(srcs)
    lands = [lax.empty(((N_DEV,) + s.shape[-2:]), s.dtype) for s in srcs]

    def body(*refs):
        src_refs, land_refs = refs[:n], refs[n:2 * n]
        send_sems, recv_sems = refs[2 * n + 1], refs[2 * n + 2]
        for i in range(n):
            for j in range(1, N_DEV):
                _spread_copy(i, j, src_refs, land_refs, send_sems, recv_sems, scatter).start()

    hbm = lambda a: pltpu.HBM(a.shape, a.dtype)
    sems = pltpu.SemaphoreType.DMA((N_PEERS * n,))
    pinned = [pltpu.with_memory_space_constraint(a, pltpu.HBM) for a in list(srcs) + lands + [carry]]
    res = pl.pallas_call(
        body, name=name,
        out_shape=(sems, sems, *[hbm(a) for a in pinned]),
        in_specs=[_HBM] * (2 * n + 1),
        out_specs=(_SEM, _SEM, *[_HBM] * (2 * n + 1)),
        input_output_aliases={i: 2 + i for i in range(2 * n + 1)},
        compiler_params=pltpu.CompilerParams(has_side_effects=_DATAFLOW),
    )(*pinned)
    return res[0], res[1], list(res[2:2 + n]), list(res[2 + n:2 + 2 * n]), res[2 + 2 * n]


def _spread_wait(started, after, scatter, name):
    send_sems, recv_sems, srcs, lands, _ = started
    n = len(srcs)

    def body(*refs):
        src_refs, land_refs = refs[:n], refs[n:2 * n]
        send_s, recv_s = refs[2 * n], refs[2 * n + 1]
        local_sems = refs[-1]
        me = _ring_peer(0)[0]
        own = [pltpu.make_async_copy(src_refs[i].at[me] if scatter else src_refs[i], land_refs[i].at[me],
                                     local_sems.at[i]) for i in range(n)]
        for cp in own:
            cp.start()
        for i in range(n):
            for j in range(1, N_DEV):
                cp = _spread_copy(i, j, src_refs, land_refs, send_s, recv_s, scatter)
                cp.wait_send()
                cp.wait_recv()
        for cp in own:
            cp.wait()

    hbm = lambda a: pltpu.HBM(a.shape, a.dtype)
    res = pl.pallas_call(
        body, name=name,
        out_shape=tuple(hbm(a) for a in srcs + lands),
        in_specs=[_HBM] * (2 * n) + [_SEM, _SEM, pl.BlockSpec(memory_space=pl.ANY)],
        out_specs=tuple([_HBM] * (2 * n)),
        input_output_aliases={i: i for i in range(2 * n)},
        scratch_shapes=[pltpu.SemaphoreType.DMA((n,))],
        compiler_params=pltpu.CompilerParams(has_side_effects=_DATAFLOW),
    )(*srcs, *lands, send_sems, recv_sems, after)
    return list(res[n:])


def _sum8(landing, name):
    _, r, c = landing.shape
    tc = _col_tile(c)

    def body(a_ref, o_ref):
        tot = a_ref[0].astype(F32)
        for k in range(1, N_DEV):
            tot = tot + a_ref[k].astype(F32)
        o_ref[...] = tot

    return pl.pallas_call(
        body,
        out_shape=jax.ShapeDtypeStruct((r, c), F32),
        grid=(c // tc,),
        in_specs=[pl.BlockSpec((N_DEV, r, tc), lambda j: (0, 0, j))],
        out_specs=pl.BlockSpec((r, tc), lambda j: (0, j)),
        compiler_params=pltpu.CompilerParams(dimension_semantics=("parallel",)),
        name=name,
    )(landing)


def _adamw_math(w, g, m, v):
    m = ADAM_B1 * m + (1.0 - ADAM_B1) * g
    v = ADAM_B2 * v + (1.0 - ADAM_B2) * (g * g)
    m_hat = m / (1.0 - ADAM_B1 ** ADAM_STEP)
    v_hat = v / (1.0 - ADAM_B2 ** ADAM_STEP)
    delta = -ADAM_LR * (m_hat / (jnp.sqrt(v_hat) + ADAM_EPS) + ADAM_WD * w)
    return delta, m, v


def _adamw(w, m, v, g, name):
    r, c = w.shape

    def fn(rows, consts):
        return list(_adamw_math(*rows)), []

    return _rowwise(fn, [w, g, m, v], [], [(c, F32)] * 3, tm=r if r <= 512 else 256, name=name)


def _adamw_small(gathered, params):
    ns = len(_SMALL)

    def body(*refs):
        g_ref, p_refs, o_refs = refs[0], refs[1:1 + 3 * ns], refs[1 + 3 * ns:]
        tot = g_ref[0]
        for k in range(1, N_DEV):
            tot = tot + g_ref[k]
        for i, name in enumerate(_SMALL):
            row, lane0, lanes = _SMALL_SLOTS[name]
            g = tot[row:row + 1, lane0:lane0 + lanes]
            w_, m_, v_ = (p_refs[3 * i + j][...] for j in range(3))
            delta, m2, v2 = _adamw_math(w_, g, m_, v_)
            for j, val in enumerate((g, delta, m2, v2)):
                o_refs[4 * i + j][...] = val
        o_refs[4 * ns][...] = tot[_LOSS_ROW:_LOSS_ROW + 1, 0:LANES]
        o_refs[4 * ns + 1][...] = tot[_CONV_ROW0:_CONV_ROW0 + _CONV_ROWS, :]

    out_shape = [jax.ShapeDtypeStruct(w.shape, F32) for (w, _, _) in params for _ in range(4)]
    out_shape += [jax.ShapeDtypeStruct((1, LANES), F32), jax.ShapeDtypeStruct((_CONV_ROWS, FLAT_COLS), F32)]
    flat = [a for wmv in params for a in wmv]
    return pl.pallas_call(body, out_shape=out_shape, name="adamw_small")(gathered, *flat)


def kernel(x, p, positions, w_in, conv_w, dn_a_log, dn_dt_bias, dn_norm_w, q_norm_w, w_uq, kv_norm_w, w_uk, w_uv, w_br_dn, w_br_mla, w_o, ln1_g, ln1_b, w_ffn_in, w_ffn_out, w_ple, w_ple_gate, ln2_g, ln2_b, loss_target, m_w_in, m_conv_w, m_dn_a_log, m_dn_dt_bias, m_dn_norm_w, m_q_norm_w, m_w_uq, m_kv_norm_w, m_w_uk, m_w_uv, m_w_br_dn, m_w_br_mla, m_w_o, m_ln1_g, m_ln1_b, m_w_ffn_in, m_w_ffn_out, m_w_ple, m_w_ple_gate, m_ln2_g, m_ln2_b, v_w_in, v_conv_w, v_dn_a_log, v_dn_dt_bias, v_dn_norm_w, v_q_norm_w, v_w_uq, v_kv_norm_w, v_w_uk, v_w_uv, v_w_br_dn, v_w_br_mla, v_w_o, v_ln1_g, v_ln1_b, v_w_ffn_in, v_w_ffn_out, v_w_ple, v_w_ple_gate, v_ln2_g, v_ln2_b):
    args = dict(locals())
    wts = {n: args[n] for n in _ORDER}
    mom1 = {n: args["m_" + n] for n in _ORDER}
    mom2 = {n: args["v_" + n] for n in _ORDER}
    big_names = [n for n, _ in _BIG]
    shard_shapes = {n: wts[n].shape[1:] for n in big_names}
    c_idx = lax.axis_index("c")
    q_idx = 2 * lax.axis_index("x") + lax.axis_index("y")
    parity, chip = c_idx.reshape(1).astype(jnp.int32), q_idx.reshape(1).astype(jnp.int32)

    stored = {n: _to_stored(n, wts[n][0]).astype(BF16) for n in big_names}
    first = _all_gather([stored["w_in"], conv_w[0]], "ag_first")
    group_names = {grp: [n for n, _ in pairs] for grp, pairs in _GROUP_GRADS.items()}
    carry, gathers = first[0], {}
    for grp in ("mix", "ffn"):
        gathers[grp] = _spread_start([stored[n] for n in group_names[grp]], carry, False, "ag_start_" + grp)
        carry = gathers[grp][4]
    conv_full = jnp.moveaxis(first[1], 0, 1).reshape(conv_w.shape[1], -1)
    small_w = {n: wts[n].astype(F32) for n in _SMALL}
    w = _first_weights(carry.reshape(-1, D_MODEL), conv_full, small_w)

    def late_weights(grp, after):
        got = _spread_wait(gathers[grp], after, False, "ag_wait_" + grp)
        return _late_weights(grp, {n: t.reshape(-1, t.shape[-1]) for n, t in zip(group_names[grp], got)})

    started = {}

    def emit(group, g, carry):
        grads = _group_grads(group, g)
        srcs = [grads[n].reshape((N_DEV,) + _stored_shape(n, shard_shapes[n])) for n in grads]
        started[group] = (list(grads), _spread_start(srcs, carry, True, "rs_start_" + group))
        return started[group][1][4]

    s_dim = x.shape[1]
    loss, dx, g = _local_step(x[0], p[0, 0], positions.reshape(s_dim, 1).astype(F32), loss_target[0], w,
                              late_weights, emit)
    g_w_in, small_g = _last_grads(g)

    src = g_w_in.reshape((N_DEV,) + _stored_shape("w_in", shard_shapes["w_in"]))
    from_sibling = _exchange_sibling([src], "rs_sibling")[0]
    own, own_bf = _chip_sum(src, from_sibling, parity, "rs_sum_w_in")
    from_chips = _exchange_chips([own_bf], "rs_chips")[0]

    out_g, out_d, out_m, out_v = {}, {}, {}, {}

    def update(n, grad, shp):
        flat2 = (shp[0], int(np.prod(shp[1:])))
        d, m2, v2 = _adamw(wts[n][0].reshape(flat2), mom1[n][0].reshape(flat2), mom2[n][0].reshape(flat2),
                           grad.reshape(flat2), "adamw_" + n)
        out_g[n], out_d[n], out_m[n], out_v[n] = grad, d.reshape(shp), m2.reshape(shp), v2.reshape(shp)

    total = _sum_parts(own, from_chips, chip, "rs_total_w_in")
    update("w_in", _from_stored("w_in", total, shard_shapes["w_in"]), shard_shapes["w_in"])
    for group, (names, st) in started.items():
        for n, landing in zip(names, _spread_wait(st, dx, True, "rs_wait_" + group)):
            update(n, _from_stored(n, _sum8(landing, "rs_total_" + n), shard_shapes[n]), shard_shapes[n])

    g_small = _all_gather([_pack_small_grads(small_g, loss)], "ag_small")[0]
    res = _adamw_small(g_small, [(wts[n], mom1[n], mom2[n]) for n in _SMALL])
    for i, n in enumerate(_SMALL):
        out_g[n], out_d[n], out_m[n], out_v[n] = res[4 * i:4 * i + 4]
    loss_out = res[4 * len(_SMALL)][0, 0]
    conv_shape = conv_w.shape[1:]
    conv_g = lax.dynamic_slice(res[-1].reshape(conv_shape[0], -1), (0, (2 * q_idx + c_idx) * conv_shape[1]),
                               conv_shape)
    update("conv_w", conv_g, conv_shape)

    expand = lambda d, n: d[n] if n in _SMALL else d[n][None]
    return (loss_out, dx[None], *[expand(out_g, n) for n in _ORDER], *[expand(out_d, n) for n in _ORDER],
            *[expand(out_m, n) for n in _ORDER], *[expand(out_v, n) for n in _ORDER])
```

```python
import functools

import numpy as np
import jax
import jax.numpy as jnp
from jax import lax
from jax.experimental import pallas as pl
from jax.experimental.pallas import tpu as pltpu

F32 = jnp.float32
BF16 = jnp.bfloat16

D_MODEL = 1024
N_HEADS = 8
HEAD = 128
CHUNK = 64
GROUP = 256
ROPE = 64
Q_LORA = 384
KV_LORA = 256
FFN_HIDDEN = 2816
PLE_DIM = 256
ROPE_BASE = 10000.0
ALPHA = 2.0 ** 0.25
SCALE = float((HEAD + ROPE) ** -0.5)
NEG_BIG = -1e30
EPS_RMS = 1e-6
EPS_LN = 1e-5

ADAM_LR = 0.001
ADAM_B1 = 0.9
ADAM_B2 = 0.999
ADAM_EPS = 1e-08
ADAM_WD = 0.01
ADAM_STEP = 10

N_DEV = 8
LANES = 128
FLAT_COLS = 1024

WB_CQ, WB_CKV, WB_KR, WB_BA, WB_COLS = 0, 512, 768, 896, 1024

HIGHEST = lax.Precision.HIGHEST

NN = (((1,), (0,)), ((), ()))
TN = (((0,), (0,)), ((), ()))
NT = (((1,), (1,)), ((), ()))


def _dot(a, b, dims=NN):
    return lax.dot_general(a.astype(BF16), b.astype(BF16), dims, preferred_element_type=F32)


def _dot32(a, b, dims=NN):
    return lax.dot_general(a, b, dims, precision=HIGHEST, preferred_element_type=F32)


def _sig(x):
    return 1.0 / (1.0 + jnp.exp(-x))


MM_TILE = 1536


def _pick_wide(n):
    if n <= MM_TILE:
        return n
    return max(t for t in range(LANES, MM_TILE + 1, LANES) if n % t == 0)


def _split_bf16(a):
    hi = a.astype(BF16)
    return hi, (a - hi.astype(F32)).astype(BF16)


def _dot3(a, b, dims=NN):
    ah, al = a if isinstance(a, tuple) else _split_bf16(a)
    bh, bl = b if isinstance(b, tuple) else _split_bf16(b)
    d = lambda p, q: lax.dot_general(p, q, dims, preferred_element_type=F32)
    return d(ah, bh) + (d(ah, bl) + d(al, bh))


def _mm(a, b, *, ta=False, tb=False, add=(), out_dtype=F32, name):
    if ta:
        k_dim, m_dim = a.shape
    else:
        m_dim, k_dim = a.shape
    if tb:
        n_dim, k2 = b.shape
    else:
        k2, n_dim = b.shape
    assert k_dim == k2, (a.shape, b.shape, ta, tb)
    tm = _pick_wide(m_dim)
    tn = _pick_wide(n_dim)
    tk = _pick_wide(k_dim)
    nk = k_dim // tk
    n_add = len(add)
    dims = TN if ta else (NT if tb else NN)
    assert not (ta and tb)

    def body(a_ref, b_ref, *rest):
        add_refs = rest[:n_add]
        o_ref = rest[n_add]
        acc = rest[n_add + 1]
        k = pl.program_id(2)

        @pl.when(k == 0)
        def _():
            acc[...] = jnp.zeros_like(acc)

        acc[...] += _dot(a_ref[...], b_ref[...], dims)

        @pl.when(k == nk - 1)
        def _():
            r = acc[...]
            for ar in add_refs:
                r = r + ar[...].astype(F32)
            o_ref[...] = r.astype(o_ref.dtype)

    a_spec = pl.BlockSpec((tk, tm), lambda i, j, k: (k, i)) if ta else pl.BlockSpec((tm, tk), lambda i, j, k: (i, k))
    b_spec = pl.BlockSpec((tn, tk), lambda i, j, k: (j, k)) if tb else pl.BlockSpec((tk, tn), lambda i, j, k: (k, j))
    o_spec = pl.BlockSpec((tm, tn), lambda i, j, k: (i, j))
    return pl.pallas_call(
        body,
        out_shape=jax.ShapeDtypeStruct((m_dim, n_dim), out_dtype),
        grid=(m_dim // tm, n_dim // tn, nk),
        in_specs=[a_spec, b_spec] + [o_spec] * n_add,
        out_specs=o_spec,
        scratch_shapes=[pltpu.VMEM((tm, tn), F32)],
        compiler_params=pltpu.CompilerParams(dimension_semantics=("parallel", "parallel", "arbitrary")),
        name=name,
    )(a, b, *add)


def _rowwise(fn, rows, consts, outs, accs=(), *, tm=256, name):
    rows = [r if isinstance(r, tuple) else (r, 0, r.shape[1]) for r in rows]
    s_dim = rows[0][0].shape[0]
    tm = min(tm, s_dim)
    assert s_dim % tm == 0 and all(arr.shape[0] == s_dim for arr, _, _ in rows)
    specs = [pl.BlockSpec((tm, width), functools.partial(lambda i, cb: (i, cb), cb=cb)) for _, cb, width in rows]
    args = [arr for arr, _, _ in rows]
    for c in consts:
        specs.append(pl.BlockSpec(c.shape, lambda i: (0, 0)))
        args.append(c)
    nr, nc, no = len(rows), len(consts), len(outs)
    out_shape = [jax.ShapeDtypeStruct((s_dim, w), dt) for (w, dt) in outs]
    out_specs = [pl.BlockSpec((tm, w), lambda i: (i, 0)) for (w, dt) in outs]
    out_shape += [jax.ShapeDtypeStruct(sh, F32) for sh in accs]
    out_specs += [pl.BlockSpec(sh, lambda i: (0, 0)) for sh in accs]

    def body(*refs):
        r = [x[...] for x in refs[:nr]]
        c = [x[...] for x in refs[nr:nr + nc]]
        o_refs = refs[nr + nc:nr + nc + no]
        a_refs = refs[nr + nc + no:]
        o_vals, a_vals = fn(r, c)
        for ref, v in zip(o_refs, o_vals, strict=True):
            ref[...] = v.astype(ref.dtype)
        if a_refs:
            @pl.when(pl.program_id(0) == 0)
            def _():
                for ref in a_refs:
                    ref[...] = jnp.zeros_like(ref)

            for ref, v in zip(a_refs, a_vals, strict=True):
                ref[...] += v

    res = pl.pallas_call(
        body,
        out_shape=out_shape,
        grid=(s_dim // tm,),
        in_specs=specs,
        out_specs=out_specs,
        compiler_params=pltpu.CompilerParams(dimension_semantics=("arbitrary" if accs else "parallel",)),
        name=name,
    )(*args)
    return res


def _colsum(v):
    return jnp.sum(v, axis=0, keepdims=True)


def _rowsum(v):
    return jnp.sum(v, axis=1, keepdims=True)


def _rowmean(v):
    return jnp.mean(v, axis=1, keepdims=True)


def _silu_grad(x):
    s = _sig(x)
    return s * (1.0 + x * (1.0 - s))


def _conv_taps(x, w, width=4):
    row = lax.broadcasted_iota(jnp.int32, x.shape, 0)
    c = x * w[width - 1:width, :]
    for s in range(1, width):
        c = c + jnp.where(row >= s, pltpu.roll(x, s, 0), 0.0) * w[width - 1 - s:width - s, :]
    return c


def _conv_fwd(proj_a, conv_w):
    s_dim = proj_a.shape[0]
    n_blk = 3 * N_HEADS

    def body(x_ref, w_ref, o_ref):
        j = pl.program_id(0)
        c = _conv_taps(x_ref[...], w_ref[...])
        y = c * _sig(c)
        r = lax.rsqrt(_rowsum(y * y) + EPS_RMS)
        fac = jnp.where(j < N_HEADS, r * (HEAD ** -0.5), jnp.where(j < 2 * N_HEADS, r, 1.0))
        o_ref[...] = y * fac

    return pl.pallas_call(
        body,
        out_shape=jax.ShapeDtypeStruct((s_dim, n_blk * HEAD), F32),
        grid=(n_blk,),
        in_specs=[pl.BlockSpec((s_dim, HEAD), lambda j: (0, j)), pl.BlockSpec((4, HEAD), lambda j: (0, j))],
        out_specs=pl.BlockSpec((s_dim, HEAD), lambda j: (0, j)),
        compiler_params=pltpu.CompilerParams(dimension_semantics=("parallel",)),
        name="conv_fwd",
    )(proj_a, conv_w)


def _conv_bwd(proj_a, conv_w, dq, dk, dv):
    s_dim = proj_a.shape[0]
    n_blk = 3 * N_HEADS

    def body(x_ref, w_ref, dq_ref, dk_ref, dv_ref, dx_ref, dw_ref):
        j = pl.program_id(0)
        x = x_ref[...]
        w = w_ref[...]
        do = jnp.where(j < N_HEADS, dq_ref[...], jnp.where(j < 2 * N_HEADS, dk_ref[...], dv_ref[...]))
        c = _conv_taps(x, w)
        sg = _sig(c)
        y = c * sg
        r = lax.rsqrt(_rowsum(y * y) + EPS_RMS)
        sc = jnp.where(j < N_HEADS, HEAD ** -0.5, 1.0)
        dy_n = sc * (r * do - y * (r * r * r) * _rowsum(do * y))
        dy = jnp.where(j < 2 * N_HEADS, dy_n, do)
        dc = dy * (sg * (1.0 + c * (1.0 - sg)))
        row = lax.broadcasted_iota(jnp.int32, x.shape, 0)
        dx = dc * w[3:4, :]
        dw_ref[3:4, :] = _colsum(dc * x)
        for s in range(1, 4):
            dx = dx + jnp.where(row < s_dim - s, pltpu.roll(dc, s_dim - s, 0), 0.0) * w[3 - s:4 - s, :]
            xs = jnp.where(row >= s, pltpu.roll(x, s, 0), 0.0)
            dw_ref[3 - s:4 - s, :] = _colsum(dc * xs)
        dx_ref[...] = dx.astype(dx_ref.dtype)

    hd = N_HEADS - 1
    return pl.pallas_call(
        body,
        out_shape=[jax.ShapeDtypeStruct((s_dim, n_blk * HEAD), BF16), jax.ShapeDtypeStruct((4, n_blk * HEAD), F32)],
        grid=(n_blk,),
        in_specs=[
            pl.BlockSpec((s_dim, HEAD), lambda j: (0, j)),
            pl.BlockSpec((4, HEAD), lambda j: (0, j)),
            pl.BlockSpec((s_dim, HEAD), lambda j: (0, jnp.minimum(j, hd))),
            pl.BlockSpec((s_dim, HEAD), lambda j: (0, jnp.clip(j - N_HEADS, 0, hd))),
            pl.BlockSpec((s_dim, HEAD), lambda j: (0, jnp.clip(j - 2 * N_HEADS, 0, hd))),
        ],
        out_specs=[pl.BlockSpec((s_dim, HEAD), lambda j: (0, j)), pl.BlockSpec((4, HEAD), lambda j: (0, j))],
        compiler_params=pltpu.CompilerParams(dimension_semantics=("parallel",)),
        name="conv_bwd",
    )(proj_a, conv_w, dq, dk, dv)


def _chunk_tri(n):
    r = np.arange(n)
    m = ((r[:, None] // CHUNK) == (r[None, :] // CHUNK)) & (r[:, None] >= r[None, :])
    m = m.astype(np.float32)
    return jnp.asarray(m), jnp.asarray(m.T)


def _softplus(z):
    return jnp.maximum(z, 0.0) + jnp.log(1.0 + jnp.exp(-jnp.abs(z)))


def _gates_fwd(proj_b, alog, dtb):
    tm = min(GROUP, proj_b.shape[0])
    tri, _ = _chunk_tri(tm)

    def fn(r, c):
        b = r[0]
        a = pltpu.roll(b, LANES - N_HEADS, 1)
        alog_, dtb_, tri_ = c
        g = -jnp.exp(alog_) * _softplus(a + dtb_)
        return [_sig(b), _dot32(tri_, g)], []

    return _rowwise(fn, [(proj_b, WB_BA // LANES, LANES)], [alog, dtb, tri],
                    [(LANES, F32), (LANES, F32)], tm=tm, name="gates_fwd")


def _gates_bwd(proj_b, alog, dtb, gc, d_beta, d_gc, d_egl_rows):
    tm = min(GROUP, proj_b.shape[0])
    _, tri_t = _chunk_tri(tm)

    def fn(r, c):
        b, gc_, d_beta_, d_gc_, d_egl_ = r
        a = pltpu.roll(b, LANES - N_HEADS, 1)
        alog_, dtb_, tri_t_ = c
        z = a + dtb_
        ea = jnp.exp(alog_)
        g = -ea * _softplus(z)
        dg = _dot32(tri_t_, d_gc_ + d_egl_ * jnp.exp(gc_))
        d_a = dg * (-ea) * _sig(z)
        beta = _sig(b)
        d_ba = d_beta_ * beta * (1.0 - beta) + pltpu.roll(d_a, N_HEADS, 1)
        return [d_ba], [_colsum(dg * g), _colsum(d_a)]

    return _rowwise(fn, [(proj_b, WB_BA // LANES, LANES), gc, d_beta, d_gc, d_egl_rows],
                    [alog, dtb, tri_t], [(LANES, BF16)], accs=[(1, LANES), (1, LANES)], tm=tm,
                    name="gates_bwd")


def _group_masks(n):
    r = lax.broadcasted_iota(jnp.int32, (n, n), 0)
    c = lax.broadcasted_iota(jnp.int32, (n, n), 1)
    same = (r // CHUNK) == (c // CHUNK)
    tril = jnp.logical_and(same, r >= c)
    strict = jnp.logical_and(same, r > c)
    last = c == (r // CHUNK) * CHUNK + (CHUNK - 1)
    eye = r == c
    return same, tril, strict, last, eye


def _inv_unit_lower(l_mat, eye_f):
    q = -l_mat
    r = eye_f + q
    qs = _split_bf16(q)
    for _ in range(5):
        qs = _split_bf16(_dot3(qs, qs))
        r = r + _dot3(r, qs)
    return r


def _unfold_blocks(folded, mask):
    n = folded.shape[0]
    return jnp.where(mask, jnp.concatenate([folded] * (n // CHUNK), axis=1), 0.0)


def _head_cols(beta, gc, gc_t, h):
    lane = lax.broadcasted_iota(jnp.int32, beta.shape, 1)
    sub = lax.broadcasted_iota(jnp.int32, gc_t.shape, 0)
    bcol = _rowsum(jnp.where(lane == h, beta, 0.0))
    gcol = _rowsum(jnp.where(lane == h, gc, 0.0))
    grow = _colsum(jnp.where(sub == h, gc_t, 0.0))
    return bcol, gcol, grow


def _prep_common(q, k, bcol, gcol, grow, t_folded=None):
    n = q.shape[0]
    same, tril, strict, last, eye = _group_masks(n)
    decay = jnp.where(tril, jnp.exp(jnp.where(tril, gcol - grow, 0.0)), 0.0)
    glast = _rowsum(jnp.where(last, jnp.broadcast_to(grow, (n, n)), 0.0))
    e = jnp.exp(gcol)
    ekt = jnp.exp(glast - gcol)
    kb = k * bcol
    kk = _dot(kb, k, NT)
    if t_folded is None:
        t_mat = _inv_unit_lower(jnp.where(strict, kk * decay, 0.0), eye.astype(F32))
    else:
        t_mat = _unfold_blocks(t_folded, same)
    qk = _dot(q, k, NT)
    return dict(same=same, tril=tril, strict=strict, last=last, eye=eye, decay=decay, e=e, ekt=ekt, kb=kb, kk=kk,
                t=t_mat, qk=qk)


def _fold_blocks(m):
    n = m.shape[0]
    out = m[:, 0:CHUNK]
    for b in range(1, n // CHUNK):
        out = out + m[:, b * CHUNK:(b + 1) * CHUNK]
    return out


def _gdr_prep_fwd(qkvn, beta, gc, gc_t):
    s_dim = qkvn.shape[0]
    tg = min(GROUP, s_dim)

    def body(q_ref, k_ref, v_ref, b_ref, g_ref, gt_ref, u_ref, w_ref, qd_ref, kt_ref, a_ref, t_ref):
        h = pl.program_id(0)
        q, k, v = q_ref[...], k_ref[...], v_ref[...]
        bcol, gcol, grow = _head_cols(b_ref[...], g_ref[...], gt_ref[...], h)
        p = _prep_common(q, k, bcol, gcol, grow)
        u_ref[...] = _dot(p["t"], v * bcol)
        w_ref[...] = _dot(p["t"], p["kb"] * p["e"])
        qd_ref[...] = q * p["e"]
        kt_ref[...] = k * p["ekt"]
        a_ref[...] = _fold_blocks(jnp.where(p["tril"], p["qk"] * p["decay"], 0.0))
        t_ref[...] = _fold_blocks(p["t"])

    row = lambda off: pl.BlockSpec((tg, HEAD), functools.partial(lambda h, m, off: (m, h + off), off=off))
    full = pl.BlockSpec((tg, LANES), lambda h, m: (m, 0))
    o_spec = pl.BlockSpec((tg, HEAD), lambda h, m: (m, h))
    a_spec = pl.BlockSpec((None, tg, CHUNK), lambda h, m: (h, m, 0))
    wide = jax.ShapeDtypeStruct((s_dim, N_HEADS * HEAD), F32)
    folded = jax.ShapeDtypeStruct((N_HEADS, s_dim, CHUNK), F32)
    return pl.pallas_call(
        body,
        out_shape=[wide, wide, wide, wide, folded, folded],
        grid=(N_HEADS, s_dim // tg),
        in_specs=[row(0), row(N_HEADS), row(2 * N_HEADS), full, full, pl.BlockSpec((8, tg), lambda h, m: (0, m))],
        out_specs=[o_spec, o_spec, o_spec, o_spec, a_spec, a_spec],
        compiler_params=pltpu.CompilerParams(dimension_semantics=("parallel", "parallel")),
        name="gdr_prep_fwd",
    )(qkvn, qkvn, qkvn, beta, gc, gc_t)


def _gdr_prep_bwd(qkvn, beta, gc, gc_t, t_fold, du, dw, dqd, dkt, d_a):
    s_dim = qkvn.shape[0]
    tg = min(GROUP, s_dim)

    def body(q_ref, k_ref, v_ref, b_ref, g_ref, gt_ref, t_ref, du_ref, dw_ref, dqd_ref, dkt_ref, da_ref,
             dq_ref, dk_ref, dv_ref, db_ref, dg_ref):
        h = pl.program_id(1)
        q, k, v = q_ref[...], k_ref[...], v_ref[...]
        bcol, gcol, grow = _head_cols(b_ref[...], g_ref[...], gt_ref[...], h)
        p = _prep_common(q, k, bcol, gcol, grow, t_ref[...])
        t_mat, decay, e, ekt, kb = p["t"], p["decay"], p["e"], p["ekt"], p["kb"]
        du_, dw_, dqd_, dkt_ = du_ref[...], dw_ref[...], dqd_ref[...], dkt_ref[...]
        vb = v * bcol
        kbe = kb * e
        d_t = _dot(du_, vb, NT) + _dot(dw_, kbe, NT)
        dvb = _dot(t_mat, du_, TN)
        dkbe = _dot(t_mat, dw_, TN)
        ts = _split_bf16(t_mat)
        d_l = -_dot3(_dot3(ts, d_t, TN), ts, NT)
        m1 = jnp.where(p["strict"], d_l, 0.0)
        m2 = _unfold_blocks(da_ref[...], p["tril"])
        d_kk = m1 * decay
        d_qk = m2 * decay
        d_decay = m1 * p["kk"] + m2 * p["qk"]
        dkb = _dot(d_kk, k) + dkbe * e
        dk = _dot(d_kk, kb, TN) + _dot(d_qk, q, TN) + dkt_ * ekt + dkb * bcol
        dq = _dot(d_qk, k) + dqd_ * e
        d_beta = _rowsum(dkb * k) + _rowsum(dvb * v)
        d_e = _rowsum(dkbe * kb) + _rowsum(dqd_ * q)
        d_ekt = _rowsum(dkt_ * k) * ekt
        d_diff = d_decay * decay
        d_grow = -_colsum(d_diff) + _colsum(jnp.where(p["last"], jnp.broadcast_to(d_ekt, (tg, tg)), 0.0))
        d_gcol = d_e * e - d_ekt + _rowsum(d_diff)
        d_gcol = d_gcol + _rowsum(jnp.where(p["eye"], jnp.broadcast_to(d_grow, (tg, tg)), 0.0))
        dq_ref[...] = dq
        dk_ref[...] = dk
        dv_ref[...] = dvb * bcol

        @pl.when(h == 0)
        def _():
            db_ref[...] = jnp.zeros_like(db_ref)
            dg_ref[...] = jnp.zeros_like(dg_ref)

        lane = lax.broadcasted_iota(jnp.int32, (tg, LANES), 1)
        db_ref[...] = jnp.where(lane == h, d_beta, db_ref[...])
        dg_ref[...] = jnp.where(lane == h, d_gcol, dg_ref[...])

    row = lambda off: pl.BlockSpec((tg, HEAD), functools.partial(lambda m, h, off: (m, h + off), off=off))
    full = pl.BlockSpec((tg, LANES), lambda m, h: (m, 0))
    o_spec = pl.BlockSpec((tg, HEAD), lambda m, h: (m, h))
    a_spec = pl.BlockSpec((None, tg, CHUNK), lambda m, h: (h, m, 0))
    wide = jax.ShapeDtypeStruct((s_dim, N_HEADS * HEAD), F32)
    lanes = jax.ShapeDtypeStruct((s_dim, LANES), F32)
    return pl.pallas_call(
        body,
        out_shape=[wide, wide, wide, lanes, lanes],
        grid=(s_dim // tg, N_HEADS),
        in_specs=[row(0), row(N_HEADS), row(2 * N_HEADS), full, full, pl.BlockSpec((8, tg), lambda m, h: (0, m)),
                  a_spec, o_spec, o_spec, o_spec, o_spec, a_spec],
        out_specs=[o_spec, o_spec, o_spec, full, full],
        compiler_params=pltpu.CompilerParams(dimension_semantics=("parallel", "arbitrary")),
        name="gdr_prep_bwd",
    )(qkvn, qkvn, qkvn, beta, gc, gc_t, t_fold, du, dw, dqd, dkt, d_a)


def _gdr_scan_fwd(u, w, qd, kt, a_mat, gc):
    s_dim = u.shape[0]
    n_chunks = s_dim // CHUNK

    def body(u_ref, w_ref, qd_ref, kt_ref, a_ref, g_ref, o_ref, st_ref, state):
        @pl.when(pl.program_id(0) == 0)
        def _():
            state[...] = jnp.zeros_like(state)

        egl = jnp.exp(g_ref[CHUNK - 1:CHUNK, :])
        for h in range(N_HEADS):
            cs = slice(h * HEAD, (h + 1) * HEAD)
            s_h = state[h]
            st_ref[h] = s_h
            vn = u_ref[:, cs] - _dot(w_ref[:, cs], s_h)
            o_ref[:, cs] = _dot(qd_ref[:, cs], s_h) + _dot(a_ref[h], vn)
            state[h] = s_h * egl[:, h:h + 1] + _dot(kt_ref[:, cs], vn, TN)

    wide = pl.BlockSpec((CHUNK, N_HEADS * HEAD), lambda n: (n, 0))
    return pl.pallas_call(
        body,
        out_shape=[jax.ShapeDtypeStruct((s_dim, N_HEADS * HEAD), F32),
                   jax.ShapeDtypeStruct((n_chunks, N_HEADS, HEAD, HEAD), F32)],
        grid=(n_chunks,),
        in_specs=[wide, wide, wide, wide, pl.BlockSpec((N_HEADS, CHUNK, CHUNK), lambda n: (0, n, 0)),
                  pl.BlockSpec((CHUNK, LANES), lambda n: (n, 0))],
        out_specs=[wide, pl.BlockSpec((None, N_HEADS, HEAD, HEAD), lambda n: (n, 0, 0, 0))],
        scratch_shapes=[pltpu.VMEM((N_HEADS, HEAD, HEAD), F32)],
        compiler_params=pltpu.CompilerParams(dimension_semantics=("arbitrary",)),
        name="gdr_scan_fwd",
    )(u, w, qd, kt, a_mat, gc)


def _gdr_scan_bwd(u, w, qd, kt, a_mat, gc, states, d_o):
    s_dim = u.shape[0]
    n_chunks = s_dim // CHUNK
    last = n_chunks - 1

    def body(u_ref, w_ref, qd_ref, kt_ref, a_ref, g_ref, st_ref, do_ref,
             du_ref, dw_ref, dqd_ref, dkt_ref, da_ref, de_ref, d_state):
        @pl.when(pl.program_id(0) == 0)
        def _():
            d_state[...] = jnp.zeros_like(d_state)

        egl = jnp.exp(g_ref[CHUNK - 1:CHUNK, :])
        for h in range(N_HEADS):
            cs = slice(h * HEAD, (h + 1) * HEAD)
            s_h = st_ref[h]
            ds_n = d_state[h]
            do = do_ref[:, cs]
            w_h = w_ref[:, cs]
            vn = u_ref[:, cs] - _dot(w_h, s_h)
            dvn = _dot(a_ref[h], do, TN) + _dot(kt_ref[:, cs], ds_n)
            dqd_ref[:, cs] = _dot(do, s_h, NT)
            da_ref[h] = _dot(do, vn, NT)
            dkt_ref[:, cs] = _dot(vn, ds_n, NT)
            de = jnp.sum(_rowsum(ds_n * s_h), axis=0, keepdims=True)
            de_ref[h:h + 1, :] = jnp.broadcast_to(de, (1, LANES))
            du_ref[:, cs] = dvn
            dw_ref[:, cs] = -_dot(dvn, s_h, NT)
            d_state[h] = ds_n * egl[:, h:h + 1] + _dot(qd_ref[:, cs], do, TN) - _dot(w_h, dvn, TN)

    wide = pl.BlockSpec((CHUNK, N_HEADS * HEAD), lambda n: (last - n, 0))
    a_spec = pl.BlockSpec((N_HEADS, CHUNK, CHUNK), lambda n: (0, last - n, 0))
    wide_shape = jax.ShapeDtypeStruct((s_dim, N_HEADS * HEAD), F32)
    return pl.pallas_call(
        body,
        out_shape=[wide_shape, wide_shape, wide_shape, wide_shape,
                   jax.ShapeDtypeStruct((N_HEADS, s_dim, CHUNK), F32),
                   jax.ShapeDtypeStruct((n_chunks, N_HEADS, LANES), F32)],
        grid=(n_chunks,),
        in_specs=[wide, wide, wide, wide, a_spec, pl.BlockSpec((CHUNK, LANES), lambda n: (last - n, 0)),
                  pl.BlockSpec((None, N_HEADS, HEAD, HEAD), lambda n: (last - n, 0, 0, 0)), wide],
        out_specs=[wide, wide, wide, wide, a_spec, pl.BlockSpec((None, N_HEADS, LANES), lambda n: (last - n, 0, 0))],
        scratch_shapes=[pltpu.VMEM((N_HEADS, HEAD, HEAD), F32)],
        compiler_params=pltpu.CompilerParams(dimension_semantics=("arbitrary",)),
        name="gdr_scan_bwd",
    )(u, w, qd, kt, a_mat, gc, states, d_o)


def _gdr_out_fwd(o_dn, proj_a, dn_w):
    def fn(r, c):
        o, z = r
        (w_,) = c
        outs = []
        for h in range(N_HEADS):
            cs = slice(h * HEAD, (h + 1) * HEAD)
            oh, zh = o[:, cs], z[:, cs]
            rr = lax.rsqrt(_rowmean(oh * oh) + EPS_RMS)
            outs.append(oh * rr * w_ * (zh * _sig(zh)))
        return [jnp.concatenate(outs, axis=1)], []

    return _rowwise(fn, [o_dn, (proj_a, 3, D_MODEL)], [dn_w], [(D_MODEL, BF16)], name="gdr_out_fwd")[0]


def _gdr_out_bwd(o_dn, proj_a, d_og, dn_w):
    def fn(r, c):
        o, z, dg = r
        (w_,) = c
        d_o, d_z = [], []
        d_w = jnp.zeros((1, HEAD), F32)
        for h in range(N_HEADS):
            cs = slice(h * HEAD, (h + 1) * HEAD)
            oh, zh, dgh = o[:, cs], z[:, cs], dg[:, cs]
            rr = lax.rsqrt(_rowmean(oh * oh) + EPS_RMS)
            sz = zh * _sig(zh)
            d_n = dgh * sz
            d_z.append(dgh * (oh * rr * w_) * _silu_grad(zh))
            d_w = d_w + _colsum(d_n * oh * rr)
            gw = d_n * w_
            d_o.append(rr * gw - oh * (rr * rr * rr) * _rowmean(gw * oh))
        return [jnp.concatenate(d_o, axis=1), jnp.concatenate(d_z, axis=1)], [d_w]

    return _rowwise(fn, [o_dn, (proj_a, 3, D_MODEL), d_og], [dn_w], [(D_MODEL, F32), (D_MODEL, BF16)],
                    accs=[(1, HEAD)], name="gdr_out_bwd")


def _rms_fwd(x, w):
    r = lax.rsqrt(_rowmean(x * x) + EPS_RMS)
    return x * r * w


def _rms_bwd(x, w, dy):
    r = lax.rsqrt(_rowmean(x * x) + EPS_RMS)
    gw = dy * w
    return r * gw - x * (r * r * r) * _rowmean(gw * x), _colsum(dy * x * r)


def _mla_norm_fwd(proj_b, qn_w, kvn_w):
    def fn(r, c):
        return [_rms_fwd(r[0], c[0]), _rms_fwd(r[1], c[1])], []

    return _rowwise(fn, [(proj_b, WB_CQ // Q_LORA, Q_LORA), (proj_b, WB_CKV // KV_LORA, KV_LORA)], [qn_w, kvn_w],
                    [(Q_LORA, BF16), (KV_LORA, BF16)], name="mla_norm_fwd")


def _mla_norm_bwd(proj_b, qn_w, kvn_w, d_cq, d_ckv):
    def fn(r, c):
        dx1, dw1 = _rms_bwd(r[0], c[0], r[2])
        dx2, dw2 = _rms_bwd(r[1], c[1], r[3])
        return [dx1, dx2], [dw1, dw2]

    return _rowwise(fn, [(proj_b, WB_CQ // Q_LORA, Q_LORA), (proj_b, WB_CKV // KV_LORA, KV_LORA), d_cq, d_ckv],
                    [qn_w, kvn_w], [(Q_LORA, BF16), (KV_LORA, BF16)], accs=[(1, Q_LORA), (1, KV_LORA)],
                    name="mla_norm_bwd")


def _rope_consts():
    inv = ROPE_BASE ** (-np.arange(0, ROPE, 2, dtype=np.float32) / ROPE)
    t = np.zeros((4, LANES), np.float32)
    t[0, :32] = inv
    t[0, 32:64] = inv
    t[1, :64] = 1.0
    t[2, 32:64] = 1.0
    t[3, :32] = -1.0
    return jnp.asarray(t)


def _rope_tables(pos, consts, width):
    ang = pos * consts[0:1, :]
    cosv, sinv = jnp.cos(ang), jnp.sin(ang)
    reps = width // LANES
    tile = (lambda t: jnp.concatenate([t] * reps, axis=1)) if reps > 1 else (lambda t: t)
    return tile(cosv * consts[1:2, :]), tile(sinv * consts[2:3, :]), tile(sinv * consts[3:4, :])


def _rope_apply(t, tabs):
    cos_t, sin_a, sin_b = tabs
    width = t.shape[1]
    return t * cos_t + pltpu.roll(t, 32, 1) * sin_a + pltpu.roll(t, width - 32, 1) * sin_b


def _rope_transpose(d, tabs):
    cos_t, sin_a, sin_b = tabs
    width = d.shape[1]
    return d * cos_t + pltpu.roll(d * sin_a, width - 32, 1) + pltpu.roll(d * sin_b, 32, 1)


QK_HEAD = 2 * HEAD


def _interleave_heads(a, b):
    parts = []
    for h in range(N_HEADS):
        parts.append(a[:, h * HEAD:(h + 1) * HEAD])
        parts.append(b if b.shape[1] == LANES else b[:, h * LANES:(h + 1) * LANES])
    return jnp.concatenate(parts, axis=1)


def _mla_qk_fwd(q_full, k_nope, proj_b, pos):
    consts = _rope_consts()

    def fn(r, c):
        qf, kn, kr, pos_ = r
        qn, qr = qf[:, :D_MODEL], qf[:, D_MODEL:]
        qr = _rope_apply(qr, _rope_tables(pos_, c[0], D_MODEL))
        kr = _rope_apply(kr, _rope_tables(pos_, c[0], LANES))
        return [_interleave_heads(qn, qr) * SCALE, _interleave_heads(kn, kr)], []

    return _rowwise(fn, [q_full, k_nope, (proj_b, WB_KR // LANES, LANES), pos], [consts],
                    [(N_HEADS * QK_HEAD, BF16), (N_HEADS * QK_HEAD, BF16)], name="mla_qk_fwd")


def _mla_qk_bwd(d_qc, d_kc, pos):
    consts = _rope_consts()

    def fn(r, c):
        dq, dk, pos_ = r
        even = lambda t: jnp.concatenate([t[:, (2 * h) * LANES:(2 * h + 1) * LANES] for h in range(N_HEADS)], axis=1)
        odd = lambda t: jnp.concatenate([t[:, (2 * h + 1) * LANES:(2 * h + 2) * LANES] for h in range(N_HEADS)], axis=1)
        d_qr_raw = _rope_transpose(odd(dq), _rope_tables(pos_, c[0], D_MODEL)) * SCALE
        dkr = dk[:, LANES:2 * LANES]
        for h in range(1, N_HEADS):
            dkr = dkr + dk[:, (2 * h + 1) * LANES:(2 * h + 2) * LANES]
        return [jnp.concatenate([even(dq) * SCALE, d_qr_raw], axis=1), even(dk),
                _rope_transpose(dkr, _rope_tables(pos_, c[0], LANES))], []

    return _rowwise(fn, [d_qc, d_kc, pos], [consts], [(2 * D_MODEL, BF16), (D_MODEL, BF16), (LANES, BF16)],
                    name="mla_qk_bwd")


def _causal_mask_t(st, key0, query0):
    key = lax.broadcasted_iota(jnp.int32, st.shape, 0) + key0
    query = lax.broadcasted_iota(jnp.int32, st.shape, 1) + query0
    return jnp.where(key <= query, st, NEG_BIG)


def _attn_tiles(s_dim):
    tq = min(512, s_dim)
    n_chains = 2 if s_dim >= 2 * tq else 1
    return tq, n_chains, min(512, s_dim)


def _diagonal_chains(t, tq, n_chains, tk):
    return [(c, (t + 1) * tk - 1 > c * tq) for c in range(n_chains) if t * tk < (c + 1) * tq]


def _attn_fwd(qc, kc, vt):
    s_dim = qc.shape[0]
    tq, n_chains, tk = _attn_tiles(s_dim)
    tqs = tq * n_chains

    def body(q_ref, k_ref, vt_ref, o_ref, lse_ref, m_s, l_s, acc):
        qi = pl.program_id(1)
        m_s[...] = jnp.full_like(m_s, NEG_BIG)
        l_s[...] = jnp.zeros_like(l_s)
        acc[...] = jnp.zeros_like(acc)

        def make_step(chains):
            def step(j, carry):
                ks = pl.multiple_of(j * tk, tk)
                kb, vtb = k_ref[pl.ds(ks, tk), :], vt_ref[:, pl.ds(ks, tk)]
                for c, masked in chains:
                    cols = slice(c * tq, (c + 1) * tq)
                    st = _dot(kb, q_ref[cols, :], NT)
                    if masked:
                        st = _causal_mask_t(st, j * tk, qi * tqs + c * tq)
                    m_prev = m_s[:, cols]
                    m_new = jnp.maximum(m_prev, jnp.max(st, axis=0, keepdims=True))
                    alpha = jnp.exp(m_prev - m_new)
                    pt = jnp.exp(st - m_new)
                    l_s[:, cols] = alpha * l_s[:, cols] + _colsum(pt)
                    m_s[:, cols] = m_new
                    acc[:, cols] = acc[:, cols] * alpha + _dot(vtb, pt)
                return carry
            return step

        below = qi * (tqs // tk)
        lax.fori_loop(0, below, make_step([(c, False) for c in range(n_chains)]), 0)
        for t in range(tqs // tk):
            make_step(_diagonal_chains(t, tq, n_chains, tk))(below + t, 0)
        l = l_s[...]
        o_ref[...] = jnp.transpose(acc[...] / l)
        lse_ref[...] = m_s[...] + jnp.log(l)

    return pl.pallas_call(
        body,
        out_shape=[jax.ShapeDtypeStruct((s_dim, N_HEADS * HEAD), F32), jax.ShapeDtypeStruct((N_HEADS, 1, s_dim), F32)],
        grid=(N_HEADS, s_dim // tqs),
        in_specs=[pl.BlockSpec((tqs, QK_HEAD), lambda h, qi: (qi, h)),
                  pl.BlockSpec((s_dim, QK_HEAD), lambda h, qi: (0, h)),
                  pl.BlockSpec((HEAD, s_dim), lambda h, qi: (h, 0))],
        out_specs=[pl.BlockSpec((tqs, HEAD), lambda h, qi: (qi, h)),
                   pl.BlockSpec((None, 1, tqs), lambda h, qi: (h, 0, qi))],
        scratch_shapes=[pltpu.VMEM((1, tqs), F32), pltpu.VMEM((1, tqs), F32), pltpu.VMEM((HEAD, tqs), F32)],
        compiler_params=pltpu.CompilerParams(dimension_semantics=("parallel", "parallel")),
        name="attn_fwd",
    )(qc, kc, vt)


def _attn_bwd(qc, kc, kct, v, o, d_o, lse):
    s_dim = qc.shape[0]
    tq, n_chains, tk = _attn_tiles(s_dim)
    tqs = tq * n_chains

    def body(q_ref, k_ref, kt_ref, v_ref, o_ref, do_ref, lse_ref, dq_ref, dk_ref, dv_ref, dqt_acc, dv_acc):
        qi = pl.program_id(1)

        @pl.when(qi == 0)
        def _():
            dk_ref[...] = jnp.zeros_like(dk_ref)
            dv_acc[...] = jnp.zeros_like(dv_acc)

        dqt_acc[...] = jnp.zeros_like(dqt_acc)
        do_f = do_ref[...]
        do_all = do_f.astype(BF16)
        q_all = q_ref[...]
        lse_row = lse_ref[...]
        delta_row = _dot3(jnp.ones((8, HEAD), F32), o_ref[...] * do_f, NT)[0:1, :]

        def make_step(chains):
            rows = slice(chains[0][0] * tq, (chains[-1][0] + 1) * tq)

            def step(j, carry):
                ks = pl.multiple_of(j * tk, tk)
                kb, vb, ktb = k_ref[pl.ds(ks, tk), :], v_ref[pl.ds(ks, tk), :], kt_ref[:, pl.ds(ks, tk)]
                pts, dsts = [], []
                for c, masked in chains:
                    cols = slice(c * tq, (c + 1) * tq)
                    st = _dot(kb, q_all[cols, :], NT)
                    if masked:
                        st = _causal_mask_t(st, j * tk, qi * tqs + c * tq)
                    pt = jnp.exp(st - lse_row[:, cols])
                    dst = pt * (_dot(vb, do_all[cols, :], NT) - delta_row[:, cols])
                    dst_b = dst.astype(BF16)
                    dqt_acc[:, cols] += _dot(ktb, dst_b)
                    pts.append(pt.astype(BF16))
                    dsts.append(dst_b)
                pt_all = jnp.concatenate(pts, axis=1) if len(chains) > 1 else pts[0]
                dst_all = jnp.concatenate(dsts, axis=1) if len(chains) > 1 else dsts[0]
                dk_ref[pl.ds(ks, tk), :] += _dot(dst_all, q_all[rows, :])
                dv_acc[pl.ds(ks, tk), :] += _dot(pt_all, do_all[rows, :])
                return carry
            return step

        below = qi * (tqs // tk)
        lax.fori_loop(0, below, make_step([(c, False) for c in range(n_chains)]), 0)
        for t in range(tqs // tk):
            make_step(_diagonal_chains(t, tq, n_chains, tk))(below + t, 0)
        dq_ref[...] = jnp.transpose(dqt_acc[...])

        @pl.when(qi == s_dim // tqs - 1)
        def _():
            dv_ref[...] = dv_acc[...].astype(dv_ref.dtype)

    q_spec = pl.BlockSpec((tqs, QK_HEAD), lambda h, qi: (qi, h))
    o_spec = pl.BlockSpec((tqs, HEAD), lambda h, qi: (qi, h))
    k_spec = pl.BlockSpec((s_dim, QK_HEAD), lambda h, qi: (0, h))
    v_spec = pl.BlockSpec((s_dim, HEAD), lambda h, qi: (0, h))
    wide2 = jax.ShapeDtypeStruct((s_dim, N_HEADS * QK_HEAD), F32)
    return pl.pallas_call(
        body,
        out_shape=[wide2, wide2, jax.ShapeDtypeStruct((s_dim, N_HEADS * HEAD), BF16)],
        grid=(N_HEADS, s_dim // tqs),
        in_specs=[q_spec, k_spec, pl.BlockSpec((QK_HEAD, s_dim), lambda h, qi: (h, 0)), v_spec, o_spec, o_spec,
                  pl.BlockSpec((None, 1, tqs), lambda h, qi: (h, 0, qi))],
        out_specs=[q_spec, k_spec, v_spec],
        scratch_shapes=[pltpu.VMEM((QK_HEAD, tqs), F32), pltpu.VMEM((s_dim, HEAD), F32)],
        compiler_params=pltpu.CompilerParams(dimension_semantics=("parallel", "arbitrary")),
        name="attn_bwd",
    )(qc, kc, kct, v, o, d_o, lse)


def _merge_fwd(y_dn, y_mla, proj_g):
    def fn(r, c):
        yd, ym, g = r
        return [_sig(g[:, :D_MODEL]) * yd + _sig(g[:, D_MODEL:]) * ym], []

    return _rowwise(fn, [y_dn, y_mla, proj_g], [], [(D_MODEL, BF16)], name="merge_fwd")[0]


def _merge_bwd(y_dn, y_mla, proj_g, d_mixed):
    def fn(r, c):
        yd, ym, g, dm = r
        sd, sm = _sig(g[:, :D_MODEL]), _sig(g[:, D_MODEL:])
        d_g = jnp.concatenate([dm * yd * sd * (1.0 - sd), dm * ym * sm * (1.0 - sm)], axis=1)
        return [d_g, dm * sd, dm * sm], []

    return _rowwise(fn, [y_dn, y_mla, proj_g, d_mixed], [], [(2 * D_MODEL, BF16), (D_MODEL, BF16), (D_MODEL, BF16)],
                    name="merge_bwd")


def _ln_stats(z):
    mu = _rowmean(z)
    zc = z - mu
    r = lax.rsqrt(_rowmean(zc * zc) + EPS_LN)
    return zc * r, r


def _ln_bwd(dy, xh, r, g):
    dxh = dy * g
    return r * (dxh - _rowmean(dxh) - xh * _rowmean(dxh * xh))


def _ln1_fwd(x, a1, g, b):
    def fn(r, c):
        xh, _ = _ln_stats(ALPHA * r[0] + r[1])
        y = xh * c[0] + c[1]
        return [y, y], []

    return _rowwise(fn, [x, a1], [g, b], [(D_MODEL, F32), (D_MODEL, BF16)], name="ln1_fwd")


def _ln1_bwd(x, a1, d_h1, g):
    def fn(r, c):
        xh, rr = _ln_stats(ALPHA * r[0] + r[1])
        dy = r[2]
        dz = _ln_bwd(dy, xh, rr, c[0])
        return [dz, ALPHA * dz], [_colsum(dy * xh), _colsum(dy)]

    return _rowwise(fn, [x, a1, d_h1], [g], [(D_MODEL, BF16), (D_MODEL, F32)], accs=[(1, D_MODEL), (1, D_MODEL)],
                    name="ln1_bwd")


def _act_fwd(gu):
    def fn(r, c):
        gt, up = r[0][:, :FFN_HIDDEN], r[0][:, FFN_HIDDEN:]
        return [gt * _sig(gt) * up], []

    return _rowwise(fn, [gu], [], [(FFN_HIDDEN, BF16)], name="act_fwd")[0]


def _act_bwd(gu, d_act):
    def fn(r, c):
        gt, up = r[0][:, :FFN_HIDDEN], r[0][:, FFN_HIDDEN:]
        da = r[1]
        return [jnp.concatenate([da * up * _silu_grad(gt), da * gt * _sig(gt)], axis=1)], []

    return _rowwise(fn, [gu, d_act], [], [(2 * FFN_HIDDEN, BF16)], name="act_bwd")[0]


def _tail(h1, ffn, pg, pp, tgt, g, b):
    def fn(r, c):
        h1_, ffn_, pg_, pp_, t_ = r
        sp = _sig(pg_)
        xh, rr = _ln_stats(ALPHA * h1_ + ffn_ + sp * pp_)
        y = xh * c[0] + c[1]
        err = y - t_
        dy = err * (1.0 / D_MODEL)
        dz = _ln_bwd(dy, xh, rr, c[0])
        loss = jnp.sum(0.5 * _rowmean(err * err), axis=0, keepdims=True)
        return ([dz, dz * pp_ * sp * (1.0 - sp), dz * sp, ALPHA * dz],
                [_colsum(dy * xh), _colsum(dy), jnp.broadcast_to(loss, (1, LANES))])

    return _rowwise(fn, [h1, ffn, pg, pp, tgt], [g, b], [(D_MODEL, BF16)] * 3 + [(D_MODEL, F32)],
                    accs=[(1, D_MODEL), (1, D_MODEL), (1, LANES)], name="tail")


def _local_step(x, p, pos, tgt, w, late_weights, emit):
    w = dict(w)
    s_dim = x.shape[0]
    xb, pb = x.astype(BF16), p.astype(BF16)
    proj_a = _mm(xb, w["wa"], name="f_proj_a")
    proj_g = _mm(xb, w["wg"], name="f_proj_g")
    proj_b = _mm(xb, w["wb"], name="f_proj_b")
    qkvn = _conv_fwd(proj_a, w["conv"])
    beta, gc = _gates_fwd(proj_b, w["alog"], w["dtb"])
    gc_t = jnp.transpose(gc[:, :N_HEADS])
    u, w_, qd, kt, a_mat, t_fold = _gdr_prep_fwd(qkvn, beta, gc, gc_t)
    o_dn, states = _gdr_scan_fwd(u, w_, qd, kt, a_mat, gc)
    og = _gdr_out_fwd(o_dn, proj_a, w["dnw"])
    w.update(late_weights("mix", og))
    y_dn = _mm(og, w["br_dn"], name="f_y_dn")
    c_q, c_kv = _mla_norm_fwd(proj_b, w["qnw"], w["kvnw"])
    q_full = _mm(c_q, w["uq"], name="f_q_full")
    k_nope = _mm(c_kv, w["uk"], name="f_k_nope")
    vv = _mm(c_kv, w["uv"], out_dtype=BF16, name="f_v")
    qc, kc = _mla_qk_fwd(q_full, k_nope, proj_b, pos)
    o_mla, lse = _attn_fwd(qc, kc, jnp.transpose(vv))
    y_mla = _mm(o_mla, w["br_mla"], name="f_y_mla")
    mixed = _merge_fwd(y_dn, y_mla, proj_g)
    a1 = _mm(mixed, w["wo"], name="f_a1")
    w.update(late_weights("ffn", a1))
    h1, h1b = _ln1_fwd(x, a1, w["ln1g"], w["ln1b"])
    gu = _mm(h1b, w["ffn_in"], name="f_gu")
    act = _act_fwd(gu)
    ffn = _mm(act, w["ffn_out"], name="f_ffn")
    pg = _mm(h1b, w["ple_gate"], name="f_pg")
    pp = _mm(pb, w["ple"], name="f_pp")
    g = {}
    dz2, d_pg, d_pp, dh1a, g["ln2g"], g["ln2b"], loss = _tail(h1, ffn, pg, pp, tgt, w["ln2g"], w["ln2b"])
    g["ple_t"] = _mm(d_pp, pb, ta=True, out_dtype=BF16, name="b_w_ple")
    g["ple_gate"] = _mm(h1b, d_pg, ta=True, out_dtype=BF16, name="b_w_ple_gate")
    g["ffn_out"] = _mm(act, dz2, ta=True, out_dtype=BF16, name="b_w_ffn_out")
    d_act = _mm(dz2, w["ffn_out"], tb=True, name="b_act")
    d_gu = _act_bwd(gu, d_act)
    g["ffn_in_t"] = _mm(d_gu, h1b, ta=True, out_dtype=BF16, name="b_w_ffn_in")
    d_gu = emit("ffn", g, d_gu)
    d_h1 = _mm(d_gu, w["ffn_in_t"], add=(dh1a,), name="b_h1_ffn")
    d_h1 = _mm(d_pg, w["ple_gate"], tb=True, add=(d_h1,), name="b_h1_ple")
    dz1, dxa, g["ln1g"], g["ln1b"] = _ln1_bwd(x, a1, d_h1, w["ln1g"])
    g["wo"] = _mm(mixed, dz1, ta=True, out_dtype=BF16, name="b_w_o")
    d_mixed = _mm(dz1, w["wo"], tb=True, name="b_mixed")
    d_proj_g, d_y_dn, d_y_mla = _merge_bwd(y_dn, y_mla, proj_g, d_mixed)
    g["br_mla"] = _mm(o_mla, d_y_mla, ta=True, out_dtype=BF16, name="b_w_br_mla")
    d_o_mla = _mm(d_y_mla, w["br_mla"], tb=True, name="b_o_mla")
    d_qc, d_kc, d_v = _attn_bwd(qc, kc, jnp.transpose(kc), vv, o_mla, d_o_mla, lse)
    d_q_full, d_kn, d_kr = _mla_qk_bwd(d_qc, d_kc, pos)
    g["uq"] = _mm(c_q, d_q_full, ta=True, out_dtype=BF16, name="b_w_uq")
    d_c_q = _mm(d_q_full, w["uq"], tb=True, name="b_c_q")
    g["uk"] = _mm(c_kv, d_kn, ta=True, out_dtype=BF16, name="b_w_uk")
    g["uv"] = _mm(c_kv, d_v, ta=True, out_dtype=BF16, name="b_w_uv")
    d_c_kv = _mm(d_kn, w["uk"], tb=True, name="b_c_kv_k")
    d_c_kv = _mm(d_v, w["uv"], tb=True, add=(d_c_kv,), name="b_c_kv_v")
    d_cq, d_ckv, g["qnw"], g["kvnw"] = _mla_norm_bwd(proj_b, w["qnw"], w["kvnw"], d_c_q, d_c_kv)
    g["br_dn"] = _mm(og, d_y_dn, ta=True, out_dtype=BF16, name="b_w_br_dn")
    d_og = emit("mix", g, _mm(d_y_dn, w["br_dn"], tb=True, name="b_og"))
    d_o_dn, d_z, g["dnw"] = _gdr_out_bwd(o_dn, proj_a, d_og, w["dnw"])
    du, dw, dqd, dkt, d_a, d_egl = _gdr_scan_bwd(u, w_, qd, kt, a_mat, gc, states, d_o_dn)
    dq, dk, dv, d_beta, d_gc = _gdr_prep_bwd(qkvn, beta, gc, gc_t, t_fold, du, dw, dqd, dkt, d_a)
    d_egl_rows = jnp.pad(d_egl[:, None, :, 0], ((0, 0), (CHUNK - 1, 0), (0, LANES - N_HEADS))).reshape(s_dim, LANES)
    d_ba, g["alog"], g["dtb"] = _gates_bwd(proj_b, w["alog"], w["dtb"], gc, d_beta, d_gc, d_egl_rows)
    d_qkv, g["conv"] = _conv_bwd(proj_a, w["conv"], dq, dk, dv)
    zeros = jnp.zeros((s_dim, WB_CKV - Q_LORA), BF16)
    d_proj_b = jnp.concatenate([d_cq, zeros, d_ckv, d_kr, d_ba], axis=1)
    g["wa_qkv_t"] = _mm(d_qkv, xb, ta=True, name="b_w_qkv")
    g["wa_z_t"] = _mm(d_z, xb, ta=True, name="b_w_z")
    g["wg_t"] = _mm(d_proj_g, xb, ta=True, name="b_w_g")
    g["wb_t"] = _mm(d_proj_b, xb, ta=True, name="b_w_b")
    dx = _mm(d_qkv, w["wa_qkv_t"], add=(dxa,), name="b_x_qkv")
    dx = _mm(d_z, w["wa_z_t"], add=(dx,), name="b_x_z")
    dx = _mm(d_proj_g, w["wg_t"], add=(dx,), name="b_x_g")
    dx = _mm(d_proj_b, w["wb_t"], add=(dx,), name="b_x_b")
    return loss, dx, g


_BIG = (("w_in", 1), ("w_uq", 0), ("w_uk", 0), ("w_uv", 0), ("w_br_dn", 0), ("w_br_mla", 0),
        ("w_o", 0), ("w_ffn_in", 1), ("w_ffn_out", 0), ("w_ple", 1), ("w_ple_gate", 0))
_BIG_AXIS = dict(_BIG)
_SMALL = ("ln1_g", "ln1_b", "ln2_g", "ln2_b", "q_norm_w", "kv_norm_w", "dn_norm_w", "dn_a_log", "dn_dt_bias")
_ORDER = ("w_in", "conv_w", "dn_a_log", "dn_dt_bias", "dn_norm_w", "q_norm_w", "w_uq", "kv_norm_w", "w_uk", "w_uv",
          "w_br_dn", "w_br_mla", "w_o", "ln1_g", "ln1_b", "w_ffn_in", "w_ffn_out", "w_ple", "w_ple_gate", "ln2_g",
          "ln2_b")


def _stored_shape(name, shard_shape):
    axis = _BIG_AXIS[name]
    lead = shard_shape[axis]
    return lead, int(np.prod(shard_shape)) // lead


def _to_stored(name, shard):
    return jnp.moveaxis(shard, _BIG_AXIS[name], 0).reshape(_stored_shape(name, shard.shape))


def _from_stored(name, stored, shard_shape):
    axis = _BIG_AXIS[name]
    moved = (shard_shape[axis],) + shard_shape[:axis] + shard_shape[axis + 1:]
    return jnp.moveaxis(stored.reshape(moved), 0, axis)


_W_IN_ROWS = np.cumsum([0, 3072, 1024, 8, 8, Q_LORA, KV_LORA, ROPE, D_MODEL, D_MODEL])


def _first_weights(w_in_t, conv_full, small):
    r = _W_IN_ROWS
    zr = lambda n: jnp.zeros((n, D_MODEL), w_in_t.dtype)
    w = {}
    w["wa_t"] = w_in_t[r[0]:r[2]]
    w["wa_qkv_t"], w["wa_z_t"] = w_in_t[r[0]:r[1]], w_in_t[r[1]:r[2]]
    w["wg_t"] = w_in_t[r[7]:r[9]]
    w["wb_t"] = jnp.concatenate([w_in_t[r[4]:r[5]], zr(WB_CKV - Q_LORA), w_in_t[r[5]:r[7]], zr(LANES - ROPE),
                                 w_in_t[r[2]:r[4]], zr(LANES - 2 * N_HEADS)], axis=0)
    for k_ in ("wa", "wg", "wb"):
        w[k_] = jnp.transpose(w[k_ + "_t"])
    w["conv"] = conv_full
    pad_l = lambda v: jnp.pad(v, ((0, 0), (0, LANES - v.shape[1])))
    w["alog"], w["dtb"] = pad_l(small["dn_a_log"]), pad_l(small["dn_dt_bias"])
    w["dnw"], w["qnw"], w["kvnw"] = small["dn_norm_w"], small["q_norm_w"], small["kv_norm_w"]
    w["ln1g"], w["ln1b"], w["ln2g"], w["ln2b"] = small["ln1_g"], small["ln1_b"], small["ln2_g"], small["ln2_b"]
    return w


def _late_weights(group, fw):
    w = {}
    if group == "mix":
        uq = fw["w_uq"].reshape(Q_LORA, N_HEADS, HEAD + ROPE)
        uq_r = jnp.pad(uq[:, :, HEAD:], ((0, 0), (0, 0), (0, HEAD - ROPE)))
        w["uq"] = jnp.concatenate([uq[:, :, :HEAD].reshape(Q_LORA, -1), uq_r.reshape(Q_LORA, -1)], axis=1)
        w["uk"], w["uv"] = fw["w_uk"], fw["w_uv"]
        w["br_dn"], w["br_mla"], w["wo"] = fw["w_br_dn"], fw["w_br_mla"], fw["w_o"]
    else:
        w["ffn_in_t"], w["ffn_out"] = fw["w_ffn_in"], fw["w_ffn_out"]
        w["ple_t"], w["ple_gate"] = fw["w_ple"], fw["w_ple_gate"]
        for k_ in ("ffn_in", "ple"):
            w[k_] = jnp.transpose(w[k_ + "_t"])
    return w


_GROUP_GRADS = {"ffn": (("w_ple", "ple_t"), ("w_ple_gate", "ple_gate"), ("w_ffn_out", "ffn_out"),
                        ("w_ffn_in", "ffn_in_t")),
                "mix": (("w_o", "wo"), ("w_br_mla", "br_mla"), ("w_uq", "uq"), ("w_uk", "uk"), ("w_uv", "uv"),
                        ("w_br_dn", "br_dn"))}


def _group_grads(group, g):
    out = {}
    for name, key in _GROUP_GRADS[group]:
        t = g[key]
        if name == "w_uq":
            uq_n = t[:, :D_MODEL].reshape(Q_LORA, N_HEADS, HEAD)
            uq_r = t[:, D_MODEL:].reshape(Q_LORA, N_HEADS, HEAD)[:, :, :ROPE]
            t = jnp.concatenate([uq_n, uq_r], axis=2).reshape(Q_LORA, -1)
        out[name] = t
    return out


def _last_grads(g):
    wb = g["wb_t"]
    w_in = jnp.concatenate([
        g["wa_qkv_t"], g["wa_z_t"], wb[WB_BA:WB_BA + 2 * N_HEADS], wb[WB_CQ:WB_CQ + Q_LORA],
        wb[WB_CKV:WB_CKV + KV_LORA], wb[WB_KR:WB_KR + ROPE], g["wg_t"]], axis=0)
    small = {"ln1_g": g["ln1g"], "ln1_b": g["ln1b"], "ln2_g": g["ln2g"], "ln2_b": g["ln2b"], "q_norm_w": g["qnw"],
             "kv_norm_w": g["kvnw"], "dn_norm_w": g["dnw"], "dn_a_log": g["alog"], "dn_dt_bias": g["dtb"],
             "conv_w": g["conv"]}
    return w_in, small


_SMALL_SLOTS = {"ln1_g": (0, 0, 1024), "ln1_b": (1, 0, 1024), "ln2_g": (2, 0, 1024), "ln2_b": (3, 0, 1024),
                "q_norm_w": (4, 0, 384), "kv_norm_w": (4, 384, 256), "dn_norm_w": (4, 640, 128),
                "dn_a_log": (4, 768, 8), "dn_dt_bias": (4, 896, 8)}
_SMALL_ROWS, _LOSS_ROW, _CONV_ROW0, _CONV_ROWS = 24, 5, 8, 12


def _pack_small_grads(small_g, loss):
    zeros = lambda r, c: jnp.zeros((r, c), F32)
    row4 = jnp.concatenate([small_g["q_norm_w"], small_g["kv_norm_w"], small_g["dn_norm_w"], small_g["dn_a_log"],
                            small_g["dn_dt_bias"]], axis=1)
    row5 = jnp.concatenate([loss, zeros(1, FLAT_COLS - LANES)], axis=1)
    head = jnp.concatenate([small_g["ln1_g"], small_g["ln1_b"], small_g["ln2_g"], small_g["ln2_b"], row4, row5,
                            zeros(2, FLAT_COLS)], axis=0)
    conv = small_g["conv_w"].reshape(_CONV_ROWS, FLAT_COLS)
    return jnp.concatenate([head, conv, zeros(_SMALL_ROWS - _CONV_ROW0 - _CONV_ROWS, FLAT_COLS)], axis=0)


_MESH_ID = pl.DeviceIdType.MESH
_ANY = pl.BlockSpec(memory_space=pl.ANY)


def _all_gather(blocks, name):
    n = len(blocks)

    def body(*refs):
        x_refs, out_refs = refs[:n], refs[n:2 * n]
        send_sems, recv_sems, local_sems = refs[2 * n:]
        x, y, c = lax.axis_index("x"), lax.axis_index("y"), lax.axis_index("c")
        me, sibling = (x, y, c), (x, y, 1 - c)
        chips = [(1 - x, y), (x, 1 - y), (1 - x, 1 - y)]

        def slot(i, px, py, pc):
            return out_refs[i].at[4 * px + 2 * py + pc]

        def copy(i, k, origin, to, src=None):
            return pltpu.make_async_remote_copy(
                src_ref=slot(i, *origin) if src is None else src, dst_ref=slot(i, *origin),
                send_sem=send_sems.at[7 * i + k], recv_sem=recv_sems.at[7 * i + k], device_id=to,
                device_id_type=_MESH_ID)

        mine = [pltpu.make_async_copy(x_refs[i], slot(i, *me), local_sems.at[i]) for i in range(n)]
        first, passed = [], []
        for i in range(n):
            mine[i].start()
            first.append(copy(i, 0, me, sibling, src=x_refs[i]))
            first += [copy(i, 1 + j, me, (*chip, c), src=x_refs[i]) for j, chip in enumerate(chips)]
        for cp in first:
            cp.start()
        for i in range(n):
            for j, chip in enumerate(chips):
                copy(i, 1 + j, (*chip, c), me).wait_recv()
                passed.append(copy(i, 4 + j, (*chip, c), sibling))
                passed[-1].start()
        for i in range(n):
            copy(i, 0, sibling, me).wait_recv()
            for j, chip in enumerate(chips):
                copy(i, 4 + j, (*chip, 1 - c), me).wait_recv()
        for cp in first + passed:
            cp.wait_send()
        for cp in mine:
            cp.wait()

    return pl.pallas_call(
        body,
        out_shape=[jax.ShapeDtypeStruct((N_DEV,) + b.shape, b.dtype) for b in blocks],
        in_specs=[_ANY] * n,
        out_specs=[_ANY] * n,
        scratch_shapes=[pltpu.SemaphoreType.DMA((7 * n,)), pltpu.SemaphoreType.DMA((7 * n,)),
                        pltpu.SemaphoreType.DMA((n,))],
        name=name,
    )(*blocks)


def _exchange_sibling(srcs, name):
    n = len(srcs)

    def body(*refs):
        src_refs, dst_refs = refs[:n], refs[n:2 * n]
        send_sems, recv_sems = refs[2 * n:]
        x, y, c = lax.axis_index("x"), lax.axis_index("y"), lax.axis_index("c")
        copies = [pltpu.make_async_remote_copy(
            src_ref=src_refs[i].at[2 * q + (1 - c)], dst_ref=dst_refs[i].at[q], send_sem=send_sems.at[4 * i + q],
            recv_sem=recv_sems.at[4 * i + q], device_id=(x, y, 1 - c), device_id_type=_MESH_ID)
            for i in range(n) for q in range(4)]
        for cp in copies:
            cp.start()
        for cp in copies:
            cp.wait_recv()
        for cp in copies:
            cp.wait_send()

    return pl.pallas_call(
        body,
        out_shape=[jax.ShapeDtypeStruct((4,) + s.shape[1:], s.dtype) for s in srcs],
        in_specs=[_ANY] * n,
        out_specs=[_ANY] * n,
        scratch_shapes=[pltpu.SemaphoreType.DMA((4 * n,)), pltpu.SemaphoreType.DMA((4 * n,))],
        name=name,
    )(*srcs)


def _exchange_chips(srcs, name):
    n = len(srcs)

    def body(*refs):
        src_refs, dst_refs = refs[:n], refs[n:2 * n]
        send_sems, recv_sems = refs[2 * n:]
        x, y, c = lax.axis_index("x"), lax.axis_index("y"), lax.axis_index("c")
        chips = [(1 - x, y), (x, 1 - y), (1 - x, 1 - y)]
        copies = [pltpu.make_async_remote_copy(
            src_ref=src_refs[i].at[2 * tx + ty], dst_ref=dst_refs[i].at[j], send_sem=send_sems.at[3 * i + j],
            recv_sem=recv_sems.at[3 * i + j], device_id=(tx, ty, c), device_id_type=_MESH_ID)
            for i in range(n) for j, (tx, ty) in enumerate(chips)]
        for cp in copies:
            cp.start()
        for cp in copies:
            cp.wait_recv()
        for cp in copies:
            cp.wait_send()

    return pl.pallas_call(
        body,
        out_shape=[jax.ShapeDtypeStruct((3,) + s.shape[1:], s.dtype) for s in srcs],
        in_specs=[_ANY] * n,
        out_specs=[_ANY] * n,
        scratch_shapes=[pltpu.SemaphoreType.DMA((3 * n,)), pltpu.SemaphoreType.DMA((3 * n,))],
        name=name,
    )(*srcs)


def _col_tile(c):
    return c if c <= 256 else 256


def _chip_sum(src, recv, parity, name):
    _, r, c = src.shape
    tc = _col_tile(c)

    def body(par_ref, a_ref, b_ref, o_ref, ob_ref):
        s = a_ref[...] + b_ref[...]
        o_ref[...] = s
        ob_ref[...] = s.astype(BF16)

    blk = lambda f: pl.BlockSpec((None, r, tc), f)
    return pl.pallas_call(
        body,
        out_shape=[jax.ShapeDtypeStruct((4, r, c), F32), jax.ShapeDtypeStruct((4, r, c), BF16)],
        grid_spec=pltpu.PrefetchScalarGridSpec(
            num_scalar_prefetch=1, grid=(4, c // tc),
            in_specs=[blk(lambda q, j, par: (2 * q + par[0], 0, j)), blk(lambda q, j, par: (q, 0, j))],
            out_specs=[blk(lambda q, j, par: (q, 0, j)), blk(lambda q, j, par: (q, 0, j))]),
        compiler_params=pltpu.CompilerParams(dimension_semantics=("parallel", "parallel")),
        name=name,
    )(parity, src, recv)


def _sum_parts(own, others, chip, name):
    _, r, c = own.shape
    tc = _col_tile(c)

    def body(q_ref, a_ref, b_ref, o_ref):
        o_ref[...] = ((a_ref[...] + b_ref[0].astype(F32)) + b_ref[1].astype(F32)) + b_ref[2].astype(F32)

    return pl.pallas_call(
        body,
        out_shape=jax.ShapeDtypeStruct((r, c), F32),
        grid_spec=pltpu.PrefetchScalarGridSpec(
            num_scalar_prefetch=1, grid=(c // tc,),
            in_specs=[pl.BlockSpec((None, r, tc), lambda j, q: (q[0], 0, j)),
                      pl.BlockSpec((3, r, tc), lambda j, q: (0, 0, j))],
            out_specs=pl.BlockSpec((r, tc), lambda j, q: (0, j))),
        compiler_params=pltpu.CompilerParams(dimension_semantics=("parallel",)),
        name=name,
    )(chip, own, others)


_HBM = pl.BlockSpec(memory_space=pltpu.HBM)
_SEM = pl.BlockSpec(memory_space=pltpu.SEMAPHORE)
_DATAFLOW = pltpu.SideEffectType.DATAFLOW_SIDE_EFFECTING
N_PEERS = N_DEV - 1


def _ring_peer(j):
    me = 4 * lax.axis_index("x") + 2 * lax.axis_index("y") + lax.axis_index("c")
    k = (me + j) % N_DEV
    return me, k, (k // 4, (k // 2) % 2, k % 2)


def _spread_copy(i, j, src_refs, land_refs, send_sems, recv_sems, scatter):
    me, k, peer = _ring_peer(j)
    return pltpu.make_async_remote_copy(
        src_ref=src_refs[i].at[k] if scatter else src_refs[i], dst_ref=land_refs[i].at[me],
        send_sem=send_sems.at[N_PEERS * i + j - 1], recv_sem=recv_sems.at[N_PEERS * i + j - 1], device_id=peer,
        device_id_type=_MESH_ID)


def _spread_start(srcs, carry, scatter, name):
    n = len(srcs)
    lands = [lax.empty(((N_DEV,) + s.shape[-2:]), s.dtype) for s in srcs]

    def body(*refs):
        src_refs, land_refs = refs[:n], refs[n:2 * n]
        send_sems, recv_sems, local_sems = refs[2 * n + 1:2 * n + 4]
        for i in range(n):
            for j in range(1, N_DEV):
                _spread_copy(i, j, src_refs, land_refs, send_sems, recv_sems, scatter).start()
        for i in range(n):
            _own_copy(i, src_refs, land_refs, local_sems, scatter).start()

    hbm = lambda a: pltpu.HBM(a.shape, a.dtype)
    sems = pltpu.SemaphoreType.DMA((N_PEERS * n,))
    pinned = [pltpu.with_memory_space_constraint(a, pltpu.HBM) for a in list(srcs) + lands + [carry]]
    res = pl.pallas_call(
        body, name=name,
        out_shape=(sems, sems, pltpu.SemaphoreType.DMA((n,)), *[hbm(a) for a in pinned]),
        in_specs=[_HBM] * (2 * n + 1),
        out_specs=(_SEM, _SEM, _SEM, *[_HBM] * (2 * n + 1)),
        input_output_aliases={i: 3 + i for i in range(2 * n + 1)},
        compiler_params=pltpu.CompilerParams(has_side_effects=_DATAFLOW),
    )(*pinned)
    return res[:3], list(res[3:3 + n]), list(res[3 + n:3 + 2 * n]), res[3 + 2 * n]


def _own_copy(i, src_refs, land_refs, local_sems, scatter):
    me = _ring_peer(0)[0]
    return pltpu.make_async_copy(src_refs[i].at[me] if scatter else src_refs[i], land_refs[i].at[me],
                                 local_sems.at[i])


def _spread_wait(started, after, scatter, name):
    sems, srcs, lands, _ = started
    n = len(srcs)

    def body(*refs):
        src_refs, land_refs = refs[:n], refs[n:2 * n]
        send_s, recv_s, local_s = refs[2 * n:2 * n + 3]
        for i in range(n):
            for j in range(1, N_DEV):
                cp = _spread_copy(i, j, src_refs, land_refs, send_s, recv_s, scatter)
                cp.wait_send()
                cp.wait_recv()
        for i in range(n):
            _own_copy(i, src_refs, land_refs, local_s, scatter).wait()

    hbm = lambda a: pltpu.HBM(a.shape, a.dtype)
    res = pl.pallas_call(
        body, name=name,
        out_shape=tuple(hbm(a) for a in srcs + lands),
        in_specs=[_HBM] * (2 * n) + [_SEM, _SEM, _SEM, pl.BlockSpec(memory_space=pl.ANY)],
        out_specs=tuple([_HBM] * (2 * n)),
        input_output_aliases={i: i for i in range(2 * n)},
        compiler_params=pltpu.CompilerParams(has_side_effects=_DATAFLOW),
    )(*srcs, *lands, *sems, after)
    return list(res[n:])


def _sum8(landing, name):
    _, r, c = landing.shape
    tc = _col_tile(c)

    def body(a_ref, o_ref):
        tot = a_ref[0].astype(F32)
        for k in range(1, N_DEV):
            tot = tot + a_ref[k].astype(F32)
        o_ref[...] = tot

    return pl.pallas_call(
        body,
        out_shape=jax.ShapeDtypeStruct((r, c), F32),
        grid=(c // tc,),
        in_specs=[pl.BlockSpec((N_DEV, r, tc), lambda j: (0, 0, j))],
        out_specs=pl.BlockSpec((r, tc), lambda j: (0, j)),
        compiler_params=pltpu.CompilerParams(dimension_semantics=("parallel",)),
        name=name,
    )(landing)


def _adamw_math(w, g, m, v):
    m = ADAM_B1 * m + (1.0 - ADAM_B1) * g
    v = ADAM_B2 * v + (1.0 - ADAM_B2) * (g * g)
    m_hat = m / (1.0 - ADAM_B1 ** ADAM_STEP)
    v_hat = v / (1.0 - ADAM_B2 ** ADAM_STEP)
    delta = -ADAM_LR * (m_hat / (jnp.sqrt(v_hat) + ADAM_EPS) + ADAM_WD * w)
    return delta, m, v


def _adamw(w, m, v, g, name):
    r, c = w.shape

    def fn(rows, consts):
        return list(_adamw_math(*rows)), []

    return _rowwise(fn, [w, g, m, v], [], [(c, F32)] * 3, tm=r if r <= 512 else 256, name=name)


def _adamw_small(gathered, params):
    ns = len(_SMALL)

    def body(*refs):
        g_ref, p_refs, o_refs = refs[0], refs[1:1 + 3 * ns], refs[1 + 3 * ns:]
        tot = g_ref[0]
        for k in range(1, N_DEV):
            tot = tot + g_ref[k]
        for i, name in enumerate(_SMALL):
            row, lane0, lanes = _SMALL_SLOTS[name]
            g = tot[row:row + 1, lane0:lane0 + lanes]
            w_, m_, v_ = (p_refs[3 * i + j][...] for j in range(3))
            delta, m2, v2 = _adamw_math(w_, g, m_, v_)
            for j, val in enumerate((g, delta, m2, v2)):
                o_refs[4 * i + j][...] = val
        o_refs[4 * ns][...] = tot[_LOSS_ROW:_LOSS_ROW + 1, 0:LANES]
        o_refs[4 * ns + 1][...] = tot[_CONV_ROW0:_CONV_ROW0 + _CONV_ROWS, :]

    out_shape = [jax.ShapeDtypeStruct(w.shape, F32) for (w, _, _) in params for _ in range(4)]
    out_shape += [jax.ShapeDtypeStruct((1, LANES), F32), jax.ShapeDtypeStruct((_CONV_ROWS, FLAT_COLS), F32)]
    flat = [a for wmv in params for a in wmv]
    return pl.pallas_call(body, out_shape=out_shape, name="adamw_small")(gathered, *flat)


def kernel(x, p, positions, w_in, conv_w, dn_a_log, dn_dt_bias, dn_norm_w, q_norm_w, w_uq, kv_norm_w, w_uk, w_uv, w_br_dn, w_br_mla, w_o, ln1_g, ln1_b, w_ffn_in, w_ffn_out, w_ple, w_ple_gate, ln2_g, ln2_b, loss_target, m_w_in, m_conv_w, m_dn_a_log, m_dn_dt_bias, m_dn_norm_w, m_q_norm_w, m_w_uq, m_kv_norm_w, m_w_uk, m_w_uv, m_w_br_dn, m_w_br_mla, m_w_o, m_ln1_g, m_ln1_b, m_w_ffn_in, m_w_ffn_out, m_w_ple, m_w_ple_gate, m_ln2_g, m_ln2_b, v_w_in, v_conv_w, v_dn_a_log, v_dn_dt_bias, v_dn_norm_w, v_q_norm_w, v_w_uq, v_kv_norm_w, v_w_uk, v_w_uv, v_w_br_dn, v_w_br_mla, v_w_o, v_ln1_g, v_ln1_b, v_w_ffn_in, v_w_ffn_out, v_w_ple, v_w_ple_gate, v_ln2_g, v_ln2_b):
    args = dict(locals())
    wts = {n: args[n] for n in _ORDER}
    mom1 = {n: args["m_" + n] for n in _ORDER}
    mom2 = {n: args["v_" + n] for n in _ORDER}
    big_names = [n for n, _ in _BIG]
    shard_shapes = {n: wts[n].shape[1:] for n in big_names}
    c_idx = lax.axis_index("c")
    q_idx = 2 * lax.axis_index("x") + lax.axis_index("y")
    parity, chip = c_idx.reshape(1).astype(jnp.int32), q_idx.reshape(1).astype(jnp.int32)

    stored = {n: _to_stored(n, wts[n][0]).astype(BF16) for n in big_names}
    first = _all_gather([stored["w_in"], conv_w[0]], "ag_first")
    group_names = {grp: [n for n, _ in pairs] for grp, pairs in _GROUP_GRADS.items()}
    carry, gathers = first[0], {}
    for grp in ("mix", "ffn"):
        gathers[grp] = _spread_start([stored[n] for n in group_names[grp]], carry, False, "ag_start_" + grp)
        carry = gathers[grp][3]
    conv_full = jnp.moveaxis(first[1], 0, 1).reshape(conv_w.shape[1], -1)
    small_w = {n: wts[n].astype(F32) for n in _SMALL}
    w = _first_weights(carry.reshape(-1, D_MODEL), conv_full, small_w)

    def late_weights(grp, after):
        got = _spread_wait(gathers[grp], after, False, "ag_wait_" + grp)
        return _late_weights(grp, {n: t.reshape(-1, t.shape[-1]) for n, t in zip(group_names[grp], got)})

    started = {}

    def emit(group, g, carry):
        grads = _group_grads(group, g)
        srcs = [grads[n].reshape((N_DEV,) + _stored_shape(n, shard_shapes[n])) for n in grads]
        started[group] = (list(grads), _spread_start(srcs, carry, True, "rs_start_" + group))
        return started[group][1][3]

    s_dim = x.shape[1]
    loss, dx, g = _local_step(x[0], p[0, 0], positions.reshape(s_dim, 1).astype(F32), loss_target[0], w,
                              late_weights, emit)
    g_w_in, small_g = _last_grads(g)

    src = g_w_in.reshape((N_DEV,) + _stored_shape("w_in", shard_shapes["w_in"]))
    from_sibling = _exchange_sibling([src], "rs_sibling")[0]
    own, own_bf = _chip_sum(src, from_sibling, parity, "rs_sum_w_in")
    from_chips = _exchange_chips([own_bf], "rs_chips")[0]

    out_g, out_d, out_m, out_v = {}, {}, {}, {}

    def update(n, grad, shp):
        flat2 = (shp[0], int(np.prod(shp[1:])))
        d, m2, v2 = _adamw(wts[n][0].reshape(flat2), mom1[n][0].reshape(flat2), mom2[n][0].reshape(flat2),
                           grad.reshape(flat2), "adamw_" + n)
        out_g[n], out_d[n], out_m[n], out_v[n] = grad, d.reshape(shp), m2.reshape(shp), v2.reshape(shp)

    total = _sum_parts(own, from_chips, chip, "rs_total_w_in")
    update("w_in", _from_stored("w_in", total, shard_shapes["w_in"]), shard_shapes["w_in"])
    for group, (names, st) in started.items():
        for n, landing in zip(names, _spread_wait(st, dx, True, "rs_wait_" + group)):
            update(n, _from_stored(n, _sum8(landing, "rs_total_" + n), shard_shapes[n]), shard_shapes[n])

    g_small = _all_gather([_pack_small_grads(small_g, loss)], "ag_small")[0]
    res = _adamw_small(g_small, [(wts[n], mom1[n], mom2[n]) for n in _SMALL])
    for i, n in enumerate(_SMALL):
        out_g[n], out_d[n], out_m[n], out_v[n] = res[4 * i:4 * i + 4]
    loss_out = res[4 * len(_SMALL)][0, 0]
    conv_shape = conv_w.shape[1:]
    conv_g = lax.dynamic_slice(res[-1].reshape(conv_shape[0], -1), (0, (2 * q_idx + c_idx) * conv_shape[1]),
                               conv_shape)
    update("conv_w", conv_g, conv_shape)

    expand = lambda d, n: d[n] if n in _SMALL else d[n][None]
    return (loss_out, dx[None], *[expand(out_g, n) for n in _ORDER], *[expand(out_d, n) for n in _ORDER],
            *[expand(out_m, n) for n in _ORDER], *[expand(out_v, n) for n in _ORDER])
```

```python
import functools

import numpy as np
import jax
import jax.numpy as jnp
from jax import lax
from jax.experimental import pallas as pl
from jax.experimental.pallas import tpu as pltpu

F32 = jnp.float32
BF16 = jnp.bfloat16

D_MODEL = 1024
N_HEADS = 8
HEAD = 128
CHUNK = 64
GROUP = 256
ROPE = 64
Q_LORA = 384
KV_LORA = 256
FFN_HIDDEN = 2816
PLE_DIM = 256
ROPE_BASE = 10000.0
ALPHA = 2.0 ** 0.25
SCALE = float((HEAD + ROPE) ** -0.5)
NEG_BIG = -1e30
EPS_RMS = 1e-6
EPS_LN = 1e-5

ADAM_LR = 0.001
ADAM_B1 = 0.9
ADAM_B2 = 0.999
ADAM_EPS = 1e-08
ADAM_WD = 0.01
ADAM_STEP = 10

N_DEV = 8
LANES = 128
FLAT_COLS = 1024

WB_CQ, WB_CKV, WB_KR, WB_BA, WB_COLS = 0, 512, 768, 896, 1024

HIGHEST = lax.Precision.HIGHEST

NN = (((1,), (0,)), ((), ()))
TN = (((0,), (0,)), ((), ()))
NT = (((1,), (1,)), ((), ()))


def _dot(a, b, dims=NN):
    return lax.dot_general(a.astype(BF16), b.astype(BF16), dims, preferred_element_type=F32)


def _dot32(a, b, dims=NN):
    return lax.dot_general(a, b, dims, precision=HIGHEST, preferred_element_type=F32)


def _sig(x):
    return 1.0 / (1.0 + jnp.exp(-x))


MM_TILE = 1536


def _pick_wide(n):
    if n <= MM_TILE:
        return n
    return max(t for t in range(LANES, MM_TILE + 1, LANES) if n % t == 0)


def _split_bf16(a):
    hi = a.astype(BF16)
    return hi, (a - hi.astype(F32)).astype(BF16)


def _dot3(a, b, dims=NN):
    ah, al = a if isinstance(a, tuple) else _split_bf16(a)
    bh, bl = b if isinstance(b, tuple) else _split_bf16(b)
    d = lambda p, q: lax.dot_general(p, q, dims, preferred_element_type=F32)
    return d(ah, bh) + (d(ah, bl) + d(al, bh))


def _mm(a, b, *, ta=False, tb=False, add=(), out_dtype=F32, name):
    if ta:
        k_dim, m_dim = a.shape
    else:
        m_dim, k_dim = a.shape
    if tb:
        n_dim, k2 = b.shape
    else:
        k2, n_dim = b.shape
    assert k_dim == k2, (a.shape, b.shape, ta, tb)
    tm = _pick_wide(m_dim)
    tn = _pick_wide(n_dim)
    tk = _pick_wide(k_dim)
    nk = k_dim // tk
    n_add = len(add)
    dims = TN if ta else (NT if tb else NN)
    assert not (ta and tb)

    def body(a_ref, b_ref, *rest):
        add_refs = rest[:n_add]
        o_ref = rest[n_add]
        acc = rest[n_add + 1]
        k = pl.program_id(2)

        @pl.when(k == 0)
        def _():
            acc[...] = jnp.zeros_like(acc)

        acc[...] += _dot(a_ref[...], b_ref[...], dims)

        @pl.when(k == nk - 1)
        def _():
            r = acc[...]
            for ar in add_refs:
                r = r + ar[...].astype(F32)
            o_ref[...] = r.astype(o_ref.dtype)

    a_spec = pl.BlockSpec((tk, tm), lambda i, j, k: (k, i)) if ta else pl.BlockSpec((tm, tk), lambda i, j, k: (i, k))
    b_spec = pl.BlockSpec((tn, tk), lambda i, j, k: (j, k)) if tb else pl.BlockSpec((tk, tn), lambda i, j, k: (k, j))
    o_spec = pl.BlockSpec((tm, tn), lambda i, j, k: (i, j))
    return pl.pallas_call(
        body,
        out_shape=jax.ShapeDtypeStruct((m_dim, n_dim), out_dtype),
        grid=(m_dim // tm, n_dim // tn, nk),
        in_specs=[a_spec, b_spec] + [o_spec] * n_add,
        out_specs=o_spec,
        scratch_shapes=[pltpu.VMEM((tm, tn), F32)],
        compiler_params=pltpu.CompilerParams(dimension_semantics=("parallel", "parallel", "arbitrary")),
        name=name,
    )(a, b, *add)


def _rowwise(fn, rows, consts, outs, accs=(), *, tm=256, name):
    rows = [r if isinstance(r, tuple) else (r, 0, r.shape[1]) for r in rows]
    s_dim = rows[0][0].shape[0]
    tm = min(tm, s_dim)
    assert s_dim % tm == 0 and all(arr.shape[0] == s_dim for arr, _, _ in rows)
    specs = [pl.BlockSpec((tm, width), functools.partial(lambda i, cb: (i, cb), cb=cb)) for _, cb, width in rows]
    args = [arr for arr, _, _ in rows]
    for c in consts:
        specs.append(pl.BlockSpec(c.shape, lambda i: (0, 0)))
        args.append(c)
    nr, nc, no = len(rows), len(consts), len(outs)
    out_shape = [jax.ShapeDtypeStruct((s_dim, w), dt) for (w, dt) in outs]
    out_specs = [pl.BlockSpec((tm, w), lambda i: (i, 0)) for (w, dt) in outs]
    out_shape += [jax.ShapeDtypeStruct(sh, F32) for sh in accs]
    out_specs += [pl.BlockSpec(sh, lambda i: (0, 0)) for sh in accs]

    def body(*refs):
        r = [x[...] for x in refs[:nr]]
        c = [x[...] for x in refs[nr:nr + nc]]
        o_refs = refs[nr + nc:nr + nc + no]
        a_refs = refs[nr + nc + no:]
        o_vals, a_vals = fn(r, c)
        for ref, v in zip(o_refs, o_vals, strict=True):
            ref[...] = v.astype(ref.dtype)
        if a_refs:
            @pl.when(pl.program_id(0) == 0)
            def _():
                for ref in a_refs:
                    ref[...] = jnp.zeros_like(ref)

            for ref, v in zip(a_refs, a_vals, strict=True):
                ref[...] += v

    res = pl.pallas_call(
        body,
        out_shape=out_shape,
        grid=(s_dim // tm,),
        in_specs=specs,
        out_specs=out_specs,
        compiler_params=pltpu.CompilerParams(dimension_semantics=("arbitrary" if accs else "parallel",)),
        name=name,
    )(*args)
    return res


def _colsum(v):
    return jnp.sum(v, axis=0, keepdims=True)


def _rowsum(v):
    return jnp.sum(v, axis=1, keepdims=True)


def _rowmean(v):
    return jnp.mean(v, axis=1, keepdims=True)


def _silu_grad(x):
    s = _sig(x)
    return s * (1.0 + x * (1.0 - s))


def _conv_taps(x, w, width=4):
    row = lax.broadcasted_iota(jnp.int32, x.shape, 0)
    c = x * w[width - 1:width, :]
    for s in range(1, width):
        c = c + jnp.where(row >= s, pltpu.roll(x, s, 0), 0.0) * w[width - 1 - s:width - s, :]
    return c


def _conv_fwd(proj_a, conv_w):
    s_dim = proj_a.shape[0]
    n_blk = 3 * N_HEADS

    def body(x_ref, w_ref, o_ref):
        j = pl.program_id(0)
        c = _conv_taps(x_ref[...], w_ref[...])
        y = c * _sig(c)
        r = lax.rsqrt(_rowsum(y * y) + EPS_RMS)
        fac = jnp.where(j < N_HEADS, r * (HEAD ** -0.5), jnp.where(j < 2 * N_HEADS, r, 1.0))
        o_ref[...] = y * fac

    return pl.pallas_call(
        body,
        out_shape=jax.ShapeDtypeStruct((s_dim, n_blk * HEAD), F32),
        grid=(n_blk,),
        in_specs=[pl.BlockSpec((s_dim, HEAD), lambda j: (0, j)), pl.BlockSpec((4, HEAD), lambda j: (0, j))],
        out_specs=pl.BlockSpec((s_dim, HEAD), lambda j: (0, j)),
        compiler_params=pltpu.CompilerParams(dimension_semantics=("parallel",)),
        name="conv_fwd",
    )(proj_a, conv_w)


def _conv_bwd(proj_a, conv_w, dq, dk, dv):
    s_dim = proj_a.shape[0]
    n_blk = 3 * N_HEADS

    def body(x_ref, w_ref, dq_ref, dk_ref, dv_ref, dx_ref, dw_ref):
        j = pl.program_id(0)
        x = x_ref[...]
        w = w_ref[...]
        do = jnp.where(j < N_HEADS, dq_ref[...], jnp.where(j < 2 * N_HEADS, dk_ref[...], dv_ref[...]))
        c = _conv_taps(x, w)
        sg = _sig(c)
        y = c * sg
        r = lax.rsqrt(_rowsum(y * y) + EPS_RMS)
        sc = jnp.where(j < N_HEADS, HEAD ** -0.5, 1.0)
        dy_n = sc * (r * do - y * (r * r * r) * _rowsum(do * y))
        dy = jnp.where(j < 2 * N_HEADS, dy_n, do)
        dc = dy * (sg * (1.0 + c * (1.0 - sg)))
        row = lax.broadcasted_iota(jnp.int32, x.shape, 0)
        dx = dc * w[3:4, :]
        dw_ref[3:4, :] = _colsum(dc * x)
        for s in range(1, 4):
            dx = dx + jnp.where(row < s_dim - s, pltpu.roll(dc, s_dim - s, 0), 0.0) * w[3 - s:4 - s, :]
            xs = jnp.where(row >= s, pltpu.roll(x, s, 0), 0.0)
            dw_ref[3 - s:4 - s, :] = _colsum(dc * xs)
        dx_ref[...] = dx.astype(dx_ref.dtype)

    hd = N_HEADS - 1
    return pl.pallas_call(
        body,
        out_shape=[jax.ShapeDtypeStruct((s_dim, n_blk * HEAD), BF16), jax.ShapeDtypeStruct((4, n_blk * HEAD), F32)],
        grid=(n_blk,),
        in_specs=[
            pl.BlockSpec((s_dim, HEAD), lambda j: (0, j)),
            pl.BlockSpec((4, HEAD), lambda j: (0, j)),
            pl.BlockSpec((s_dim, HEAD), lambda j: (0, jnp.minimum(j, hd))),
            pl.BlockSpec((s_dim, HEAD), lambda j: (0, jnp.clip(j - N_HEADS, 0, hd))),
            pl.BlockSpec((s_dim, HEAD), lambda j: (0, jnp.clip(j - 2 * N_HEADS, 0, hd))),
        ],
        out_specs=[pl.BlockSpec((s_dim, HEAD), lambda j: (0, j)), pl.BlockSpec((4, HEAD), lambda j: (0, j))],
        compiler_params=pltpu.CompilerParams(dimension_semantics=("parallel",)),
        name="conv_bwd",
    )(proj_a, conv_w, dq, dk, dv)


def _chunk_tri(n):
    r = np.arange(n)
    m = ((r[:, None] // CHUNK) == (r[None, :] // CHUNK)) & (r[:, None] >= r[None, :])
    m = m.astype(np.float32)
    return jnp.asarray(m), jnp.asarray(m.T)


def _softplus(z):
    return jnp.maximum(z, 0.0) + jnp.log(1.0 + jnp.exp(-jnp.abs(z)))


def _gates_fwd(proj_b, alog, dtb):
    tm = min(GROUP, proj_b.shape[0])
    tri, _ = _chunk_tri(tm)

    def fn(r, c):
        b = r[0]
        a = pltpu.roll(b, LANES - N_HEADS, 1)
        alog_, dtb_, tri_ = c
        g = -jnp.exp(alog_) * _softplus(a + dtb_)
        return [_sig(b), _dot32(tri_, g)], []

    return _rowwise(fn, [(proj_b, WB_BA // LANES, LANES)], [alog, dtb, tri],
                    [(LANES, F32), (LANES, F32)], tm=tm, name="gates_fwd")


def _gates_bwd(proj_b, alog, dtb, gc, d_beta, d_gc, d_egl_rows):
    tm = min(GROUP, proj_b.shape[0])
    _, tri_t = _chunk_tri(tm)

    def fn(r, c):
        b, gc_, d_beta_, d_gc_, d_egl_ = r
        a = pltpu.roll(b, LANES - N_HEADS, 1)
        alog_, dtb_, tri_t_ = c
        z = a + dtb_
        ea = jnp.exp(alog_)
        g = -ea * _softplus(z)
        dg = _dot32(tri_t_, d_gc_ + d_egl_ * jnp.exp(gc_))
        d_a = dg * (-ea) * _sig(z)
        beta = _sig(b)
        d_ba = d_beta_ * beta * (1.0 - beta) + pltpu.roll(d_a, N_HEADS, 1)
        return [d_ba], [_colsum(dg * g), _colsum(d_a)]

    return _rowwise(fn, [(proj_b, WB_BA // LANES, LANES), gc, d_beta, d_gc, d_egl_rows],
                    [alog, dtb, tri_t], [(LANES, BF16)], accs=[(1, LANES), (1, LANES)], tm=tm,
                    name="gates_bwd")


def _group_masks(n):
    r = lax.broadcasted_iota(jnp.int32, (n, n), 0)
    c = lax.broadcasted_iota(jnp.int32, (n, n), 1)
    same = (r // CHUNK) == (c // CHUNK)
    tril = jnp.logical_and(same, r >= c)
    strict = jnp.logical_and(same, r > c)
    last = c == (r // CHUNK) * CHUNK + (CHUNK - 1)
    eye = r == c
    return same, tril, strict, last, eye


def _inv_unit_lower(l_mat, eye_f):
    n = l_mat.shape[0]
    r = lax.broadcasted_iota(jnp.int32, (n, n), 0)
    c = lax.broadcasted_iota(jnp.int32, (n, n), 1)
    t = eye_f - jnp.where((r // 2) == (c // 2), l_mat, 0.0)
    s = 2
    while s < CHUNK:
        below = jnp.logical_and((r // (2 * s)) == (c // (2 * s)), jnp.logical_and((r // s) % 2 == 1, (c // s) % 2 == 0))
        ts = _split_bf16(t)
        t = t - _dot3(_dot3(ts, jnp.where(below, l_mat, 0.0)), ts)
        s *= 2
    return t


def _unfold_blocks(folded, mask):
    n = folded.shape[0]
    return jnp.where(mask, jnp.concatenate([folded] * (n // CHUNK), axis=1), 0.0)


def _head_cols(beta, gc, gc_t, h):
    lane = lax.broadcasted_iota(jnp.int32, beta.shape, 1)
    sub = lax.broadcasted_iota(jnp.int32, gc_t.shape, 0)
    bcol = _rowsum(jnp.where(lane == h, beta, 0.0))
    gcol = _rowsum(jnp.where(lane == h, gc, 0.0))
    grow = _colsum(jnp.where(sub == h, gc_t, 0.0))
    return bcol, gcol, grow


def _prep_common(q, k, bcol, gcol, grow, t_folded=None):
    n = q.shape[0]
    same, tril, strict, last, eye = _group_masks(n)
    decay = jnp.where(tril, jnp.exp(jnp.where(tril, gcol - grow, 0.0)), 0.0)
    glast = _rowsum(jnp.where(last, jnp.broadcast_to(grow, (n, n)), 0.0))
    e = jnp.exp(gcol)
    ekt = jnp.exp(glast - gcol)
    kb = k * bcol
    kk = _dot(kb, k, NT)
    if t_folded is None:
        t_mat = _inv_unit_lower(jnp.where(strict, kk * decay, 0.0), eye.astype(F32))
    else:
        t_mat = _unfold_blocks(t_folded, same)
    qk = _dot(q, k, NT)
    return dict(same=same, tril=tril, strict=strict, last=last, eye=eye, decay=decay, e=e, ekt=ekt, kb=kb, kk=kk,
                t=t_mat, qk=qk)


def _fold_blocks(m):
    n = m.shape[0]
    out = m[:, 0:CHUNK]
    for b in range(1, n // CHUNK):
        out = out + m[:, b * CHUNK:(b + 1) * CHUNK]
    return out


def _gdr_prep_fwd(qkvn, beta, gc, gc_t):
    s_dim = qkvn.shape[0]
    tg = min(GROUP, s_dim)

    def body(q_ref, k_ref, v_ref, b_ref, g_ref, gt_ref, u_ref, w_ref, qd_ref, kt_ref, a_ref, t_ref):
        h = pl.program_id(0)
        q, k, v = q_ref[...], k_ref[...], v_ref[...]
        bcol, gcol, grow = _head_cols(b_ref[...], g_ref[...], gt_ref[...], h)
        p = _prep_common(q, k, bcol, gcol, grow)
        u_ref[...] = _dot(p["t"], v * bcol)
        w_ref[...] = _dot(p["t"], p["kb"] * p["e"])
        qd_ref[...] = q * p["e"]
        kt_ref[...] = k * p["ekt"]
        a_ref[...] = _fold_blocks(jnp.where(p["tril"], p["qk"] * p["decay"], 0.0))
        t_ref[...] = _fold_blocks(p["t"])

    row = lambda off: pl.BlockSpec((tg, HEAD), functools.partial(lambda h, m, off: (m, h + off), off=off))
    full = pl.BlockSpec((tg, LANES), lambda h, m: (m, 0))
    o_spec = pl.BlockSpec((tg, HEAD), lambda h, m: (m, h))
    a_spec = pl.BlockSpec((None, tg, CHUNK), lambda h, m: (h, m, 0))
    wide = jax.ShapeDtypeStruct((s_dim, N_HEADS * HEAD), F32)
    folded = jax.ShapeDtypeStruct((N_HEADS, s_dim, CHUNK), F32)
    return pl.pallas_call(
        body,
        out_shape=[wide, wide, wide, wide, folded, folded],
        grid=(N_HEADS, s_dim // tg),
        in_specs=[row(0), row(N_HEADS), row(2 * N_HEADS), full, full, pl.BlockSpec((8, tg), lambda h, m: (0, m))],
        out_specs=[o_spec, o_spec, o_spec, o_spec, a_spec, a_spec],
        compiler_params=pltpu.CompilerParams(dimension_semantics=("parallel", "parallel")),
        name="gdr_prep_fwd",
    )(qkvn, qkvn, qkvn, beta, gc, gc_t)


def _gdr_prep_bwd(qkvn, beta, gc, gc_t, t_fold, u, w, du, dw, dqd, dkt, d_a):
    s_dim = qkvn.shape[0]
    tg = min(GROUP, s_dim)

    def body(q_ref, k_ref, v_ref, b_ref, g_ref, gt_ref, t_ref, u_ref, w_ref, du_ref, dw_ref, dqd_ref, dkt_ref,
             da_ref, dq_ref, dk_ref, dv_ref, db_ref, dg_ref):
        h = pl.program_id(1)
        q, k, v = q_ref[...], k_ref[...], v_ref[...]
        bcol, gcol, grow = _head_cols(b_ref[...], g_ref[...], gt_ref[...], h)
        p = _prep_common(q, k, bcol, gcol, grow, t_ref[...])
        t_mat, decay, e, ekt, kb = p["t"], p["decay"], p["e"], p["ekt"], p["kb"]
        du_, dw_, dqd_, dkt_ = du_ref[...], dw_ref[...], dqd_ref[...], dkt_ref[...]
        dvb = _dot(t_mat, du_, TN)
        dkbe = _dot(t_mat, dw_, TN)
        d_l = -(_dot(dvb, u_ref[...], NT) + _dot(dkbe, w_ref[...], NT))
        m1 = jnp.where(p["strict"], d_l, 0.0)
        m2 = _unfold_blocks(da_ref[...], p["tril"])
        d_kk = m1 * decay
        d_qk = m2 * decay
        d_decay = m1 * p["kk"] + m2 * p["qk"]
        dkb = _dot(d_kk, k) + dkbe * e
        dk = _dot(d_kk, kb, TN) + _dot(d_qk, q, TN) + dkt_ * ekt + dkb * bcol
        dq = _dot(d_qk, k) + dqd_ * e
        d_beta = _rowsum(dkb * k) + _rowsum(dvb * v)
        d_e = _rowsum(dkbe * kb) + _rowsum(dqd_ * q)
        d_ekt = _rowsum(dkt_ * k) * ekt
        d_diff = d_decay * decay
        d_grow = -_colsum(d_diff) + _colsum(jnp.where(p["last"], jnp.broadcast_to(d_ekt, (tg, tg)), 0.0))
        d_gcol = d_e * e - d_ekt + _rowsum(d_diff)
        d_gcol = d_gcol + _rowsum(jnp.where(p["eye"], jnp.broadcast_to(d_grow, (tg, tg)), 0.0))
        dq_ref[...] = dq
        dk_ref[...] = dk
        dv_ref[...] = dvb * bcol

        @pl.when(h == 0)
        def _():
            db_ref[...] = jnp.zeros_like(db_ref)
            dg_ref[...] = jnp.zeros_like(dg_ref)

        lane = lax.broadcasted_iota(jnp.int32, (tg, LANES), 1)
        db_ref[...] = jnp.where(lane == h, d_beta, db_ref[...])
        dg_ref[...] = jnp.where(lane == h, d_gcol, dg_ref[...])

    row = lambda off: pl.BlockSpec((tg, HEAD), functools.partial(lambda m, h, off: (m, h + off), off=off))
    full = pl.BlockSpec((tg, LANES), lambda m, h: (m, 0))
    o_spec = pl.BlockSpec((tg, HEAD), lambda m, h: (m, h))
    a_spec = pl.BlockSpec((None, tg, CHUNK), lambda m, h: (h, m, 0))
    wide = jax.ShapeDtypeStruct((s_dim, N_HEADS * HEAD), F32)
    lanes = jax.ShapeDtypeStruct((s_dim, LANES), F32)
    return pl.pallas_call(
        body,
        out_shape=[wide, wide, wide, lanes, lanes],
        grid=(s_dim // tg, N_HEADS),
        in_specs=[row(0), row(N_HEADS), row(2 * N_HEADS), full, full, pl.BlockSpec((8, tg), lambda m, h: (0, m)),
                  a_spec, o_spec, o_spec, o_spec, o_spec, o_spec, o_spec, a_spec],
        out_specs=[o_spec, o_spec, o_spec, full, full],
        compiler_params=pltpu.CompilerParams(dimension_semantics=("parallel", "arbitrary")),
        name="gdr_prep_bwd",
    )(qkvn, qkvn, qkvn, beta, gc, gc_t, t_fold, u, w, du, dw, dqd, dkt, d_a)


def _gdr_scan_fwd(u, w, qd, kt, a_mat, gc):
    s_dim = u.shape[0]
    n_chunks = s_dim // CHUNK

    def body(u_ref, w_ref, qd_ref, kt_ref, a_ref, g_ref, o_ref, st_ref, state):
        @pl.when(pl.program_id(0) == 0)
        def _():
            state[...] = jnp.zeros_like(state)

        egl = jnp.exp(g_ref[CHUNK - 1:CHUNK, :])
        for h in range(N_HEADS):
            cs = slice(h * HEAD, (h + 1) * HEAD)
            s_h = state[h]
            st_ref[h] = s_h
            vn = u_ref[:, cs] - _dot(w_ref[:, cs], s_h)
            o_ref[:, cs] = _dot(qd_ref[:, cs], s_h) + _dot(a_ref[h], vn)
            state[h] = s_h * egl[:, h:h + 1] + _dot(kt_ref[:, cs], vn, TN)

    wide = pl.BlockSpec((CHUNK, N_HEADS * HEAD), lambda n: (n, 0))
    return pl.pallas_call(
        body,
        out_shape=[jax.ShapeDtypeStruct((s_dim, N_HEADS * HEAD), F32),
                   jax.ShapeDtypeStruct((n_chunks, N_HEADS, HEAD, HEAD), F32)],
        grid=(n_chunks,),
        in_specs=[wide, wide, wide, wide, pl.BlockSpec((N_HEADS, CHUNK, CHUNK), lambda n: (0, n, 0)),
                  pl.BlockSpec((CHUNK, LANES), lambda n: (n, 0))],
        out_specs=[wide, pl.BlockSpec((None, N_HEADS, HEAD, HEAD), lambda n: (n, 0, 0, 0))],
        scratch_shapes=[pltpu.VMEM((N_HEADS, HEAD, HEAD), F32)],
        compiler_params=pltpu.CompilerParams(dimension_semantics=("arbitrary",)),
        name="gdr_scan_fwd",
    )(u, w, qd, kt, a_mat, gc)


def _gdr_scan_bwd(u, w, qd, kt, a_mat, gc, states, d_o):
    s_dim = u.shape[0]
    n_chunks = s_dim // CHUNK
    last = n_chunks - 1

    def body(u_ref, w_ref, qd_ref, kt_ref, a_ref, g_ref, st_ref, do_ref,
             du_ref, dw_ref, dqd_ref, dkt_ref, da_ref, de_ref, d_state):
        @pl.when(pl.program_id(0) == 0)
        def _():
            d_state[...] = jnp.zeros_like(d_state)

        egl = jnp.exp(g_ref[CHUNK - 1:CHUNK, :])
        for h in range(N_HEADS):
            cs = slice(h * HEAD, (h + 1) * HEAD)
            s_h = st_ref[h]
            ds_n = d_state[h]
            do = do_ref[:, cs]
            w_h = w_ref[:, cs]
            vn = u_ref[:, cs] - _dot(w_h, s_h)
            dvn = _dot(a_ref[h], do, TN) + _dot(kt_ref[:, cs], ds_n)
            dqd_ref[:, cs] = _dot(do, s_h, NT)
            da_ref[h] = _dot(do, vn, NT)
            dkt_ref[:, cs] = _dot(vn, ds_n, NT)
            de = jnp.sum(_rowsum(ds_n * s_h), axis=0, keepdims=True)
            de_ref[h:h + 1, :] = jnp.broadcast_to(de, (1, LANES))
            du_ref[:, cs] = dvn
            dw_ref[:, cs] = -_dot(dvn, s_h, NT)
            d_state[h] = ds_n * egl[:, h:h + 1] + _dot(qd_ref[:, cs], do, TN) - _dot(w_h, dvn, TN)

    wide = pl.BlockSpec((CHUNK, N_HEADS * HEAD), lambda n: (last - n, 0))
    a_spec = pl.BlockSpec((N_HEADS, CHUNK, CHUNK), lambda n: (0, last - n, 0))
    wide_shape = jax.ShapeDtypeStruct((s_dim, N_HEADS * HEAD), F32)
    return pl.pallas_call(
        body,
        out_shape=[wide_shape, wide_shape, wide_shape, wide_shape,
                   jax.ShapeDtypeStruct((N_HEADS, s_dim, CHUNK), F32),
                   jax.ShapeDtypeStruct((n_chunks, N_HEADS, LANES), F32)],
        grid=(n_chunks,),
        in_specs=[wide, wide, wide, wide, a_spec, pl.BlockSpec((CHUNK, LANES), lambda n: (last - n, 0)),
                  pl.BlockSpec((None, N_HEADS, HEAD, HEAD), lambda n: (last - n, 0, 0, 0)), wide],
        out_specs=[wide, wide, wide, wide, a_spec, pl.BlockSpec((None, N_HEADS, LANES), lambda n: (last - n, 0, 0))],
        scratch_shapes=[pltpu.VMEM((N_HEADS, HEAD, HEAD), F32)],
        compiler_params=pltpu.CompilerParams(dimension_semantics=("arbitrary",)),
        name="gdr_scan_bwd",
    )(u, w, qd, kt, a_mat, gc, states, d_o)


def _gdr_out_fwd(o_dn, proj_a, dn_w):
    def fn(r, c):
        o, z = r
        (w_,) = c
        outs = []
        for h in range(N_HEADS):
            cs = slice(h * HEAD, (h + 1) * HEAD)
            oh, zh = o[:, cs], z[:, cs]
            rr = lax.rsqrt(_rowmean(oh * oh) + EPS_RMS)
            outs.append(oh * rr * w_ * (zh * _sig(zh)))
        return [jnp.concatenate(outs, axis=1)], []

    return _rowwise(fn, [o_dn, (proj_a, 3, D_MODEL)], [dn_w], [(D_MODEL, BF16)], name="gdr_out_fwd")[0]


def _gdr_out_bwd(o_dn, proj_a, d_og, dn_w):
    def fn(r, c):
        o, z, dg = r
        (w_,) = c
        d_o, d_z = [], []
        d_w = jnp.zeros((1, HEAD), F32)
        for h in range(N_HEADS):
            cs = slice(h * HEAD, (h + 1) * HEAD)
            oh, zh, dgh = o[:, cs], z[:, cs], dg[:, cs]
            rr = lax.rsqrt(_rowmean(oh * oh) + EPS_RMS)
            sz = zh * _sig(zh)
            d_n = dgh * sz
            d_z.append(dgh * (oh * rr * w_) * _silu_grad(zh))
            d_w = d_w + _colsum(d_n * oh * rr)
            gw = d_n * w_
            d_o.append(rr * gw - oh * (rr * rr * rr) * _rowmean(gw * oh))
        return [jnp.concatenate(d_o, axis=1), jnp.concatenate(d_z, axis=1)], [d_w]

    return _rowwise(fn, [o_dn, (proj_a, 3, D_MODEL), d_og], [dn_w], [(D_MODEL, F32), (D_MODEL, BF16)],
                    accs=[(1, HEAD)], name="gdr_out_bwd")


def _rms_fwd(x, w):
    r = lax.rsqrt(_rowmean(x * x) + EPS_RMS)
    return x * r * w


def _rms_bwd(x, w, dy):
    r = lax.rsqrt(_rowmean(x * x) + EPS_RMS)
    gw = dy * w
    return r * gw - x * (r * r * r) * _rowmean(gw * x), _colsum(dy * x * r)


def _mla_norm_fwd(proj_b, qn_w, kvn_w):
    def fn(r, c):
        return [_rms_fwd(r[0], c[0]), _rms_fwd(r[1], c[1])], []

    return _rowwise(fn, [(proj_b, WB_CQ // Q_LORA, Q_LORA), (proj_b, WB_CKV // KV_LORA, KV_LORA)], [qn_w, kvn_w],
                    [(Q_LORA, BF16), (KV_LORA, BF16)], name="mla_norm_fwd")


def _mla_norm_bwd(proj_b, qn_w, kvn_w, d_cq, d_ckv):
    def fn(r, c):
        dx1, dw1 = _rms_bwd(r[0], c[0], r[2])
        dx2, dw2 = _rms_bwd(r[1], c[1], r[3])
        return [dx1, dx2], [dw1, dw2]

    return _rowwise(fn, [(proj_b, WB_CQ // Q_LORA, Q_LORA), (proj_b, WB_CKV // KV_LORA, KV_LORA), d_cq, d_ckv],
                    [qn_w, kvn_w], [(Q_LORA, BF16), (KV_LORA, BF16)], accs=[(1, Q_LORA), (1, KV_LORA)],
                    name="mla_norm_bwd")


def _rope_consts():
    inv = ROPE_BASE ** (-np.arange(0, ROPE, 2, dtype=np.float32) / ROPE)
    t = np.zeros((4, LANES), np.float32)
    t[0, :32] = inv
    t[0, 32:64] = inv
    t[1, :64] = 1.0
    t[2, 32:64] = 1.0
    t[3, :32] = -1.0
    return jnp.asarray(t)


def _rope_tables(pos, consts, width):
    ang = pos * consts[0:1, :]
    cosv, sinv = jnp.cos(ang), jnp.sin(ang)
    reps = width // LANES
    tile = (lambda t: jnp.concatenate([t] * reps, axis=1)) if reps > 1 else (lambda t: t)
    return tile(cosv * consts[1:2, :]), tile(sinv * consts[2:3, :]), tile(sinv * consts[3:4, :])


def _rope_apply(t, tabs):
    cos_t, sin_a, sin_b = tabs
    width = t.shape[1]
    return t * cos_t + pltpu.roll(t, 32, 1) * sin_a + pltpu.roll(t, width - 32, 1) * sin_b


def _rope_transpose(d, tabs):
    cos_t, sin_a, sin_b = tabs
    width = d.shape[1]
    return d * cos_t + pltpu.roll(d * sin_a, width - 32, 1) + pltpu.roll(d * sin_b, 32, 1)


QK_HEAD = 2 * HEAD


def _interleave_heads(a, b):
    parts = []
    for h in range(N_HEADS):
        parts.append(a[:, h * HEAD:(h + 1) * HEAD])
        parts.append(b if b.shape[1] == LANES else b[:, h * LANES:(h + 1) * LANES])
    return jnp.concatenate(parts, axis=1)


def _mla_qk_fwd(q_full, k_nope, proj_b, pos):
    consts = _rope_consts()

    def fn(r, c):
        qf, kn, kr, pos_ = r
        qn, qr = qf[:, :D_MODEL], qf[:, D_MODEL:]
        qr = _rope_apply(qr, _rope_tables(pos_, c[0], D_MODEL))
        kr = _rope_apply(kr, _rope_tables(pos_, c[0], LANES))
        return [_interleave_heads(qn, qr) * SCALE, _interleave_heads(kn, kr)], []

    return _rowwise(fn, [q_full, k_nope, (proj_b, WB_KR // LANES, LANES), pos], [consts],
                    [(N_HEADS * QK_HEAD, BF16), (N_HEADS * QK_HEAD, BF16)], name="mla_qk_fwd")


def _mla_qk_bwd(d_qc, d_kc, pos):
    consts = _rope_consts()

    def fn(r, c):
        dq, dk, pos_ = r
        even = lambda t: jnp.concatenate([t[:, (2 * h) * LANES:(2 * h + 1) * LANES] for h in range(N_HEADS)], axis=1)
        odd = lambda t: jnp.concatenate([t[:, (2 * h + 1) * LANES:(2 * h + 2) * LANES] for h in range(N_HEADS)], axis=1)
        d_qr_raw = _rope_transpose(odd(dq), _rope_tables(pos_, c[0], D_MODEL)) * SCALE
        dkr = dk[:, LANES:2 * LANES]
        for h in range(1, N_HEADS):
            dkr = dkr + dk[:, (2 * h + 1) * LANES:(2 * h + 2) * LANES]
        return [jnp.concatenate([even(dq) * SCALE, d_qr_raw], axis=1), even(dk),
                _rope_transpose(dkr, _rope_tables(pos_, c[0], LANES))], []

    return _rowwise(fn, [d_qc, d_kc, pos], [consts], [(2 * D_MODEL, BF16), (D_MODEL, BF16), (LANES, BF16)],
                    name="mla_qk_bwd")


def _causal_mask_t(st, key0, query0):
    key = lax.broadcasted_iota(jnp.int32, st.shape, 0) + key0
    query = lax.broadcasted_iota(jnp.int32, st.shape, 1) + query0
    return jnp.where(key <= query, st, NEG_BIG)


def _attn_tiles(s_dim):
    tq = min(512, s_dim)
    n_chains = 2 if s_dim >= 2 * tq else 1
    return tq, n_chains, min(512, s_dim)


def _diagonal_chains(t, tq, n_chains, tk):
    return [(c, (t + 1) * tk - 1 > c * tq) for c in range(n_chains) if t * tk < (c + 1) * tq]


def _attn_fwd(qc, kc, vt):
    s_dim = qc.shape[0]
    tq, n_chains, tk = _attn_tiles(s_dim)
    tqs = tq * n_chains

    def body(q_ref, k_ref, vt_ref, o_ref, lse_ref, m_s, l_s, acc):
        qi = pl.program_id(1)
        m_s[...] = jnp.full_like(m_s, NEG_BIG)
        l_s[...] = jnp.zeros_like(l_s)
        acc[...] = jnp.zeros_like(acc)

        def make_step(chains):
            def step(j, carry):
                ks = pl.multiple_of(j * tk, tk)
                kb, vtb = k_ref[pl.ds(ks, tk), :], vt_ref[:, pl.ds(ks, tk)]
                for c, masked in chains:
                    cols = slice(c * tq, (c + 1) * tq)
                    st = _dot(kb, q_ref[cols, :], NT)
                    if masked:
                        st = _causal_mask_t(st, j * tk, qi * tqs + c * tq)
                    m_prev = m_s[:, cols]
                    m_new = jnp.maximum(m_prev, jnp.max(st, axis=0, keepdims=True))
                    alpha = jnp.exp(m_prev - m_new)
                    pt = jnp.exp(st - m_new)
                    l_s[:, cols] = alpha * l_s[:, cols] + _colsum(pt)
                    m_s[:, cols] = m_new
                    acc[:, cols] = acc[:, cols] * alpha + _dot(vtb, pt)
                return carry
            return step

        below = qi * (tqs // tk)
        lax.fori_loop(0, below, make_step([(c, False) for c in range(n_chains)]), 0)
        for t in range(tqs // tk):
            make_step(_diagonal_chains(t, tq, n_chains, tk))(below + t, 0)
        l = l_s[...]
        o_ref[...] = jnp.transpose(acc[...] / l)
        lse_ref[...] = m_s[...] + jnp.log(l)

    return pl.pallas_call(
        body,
        out_shape=[jax.ShapeDtypeStruct((s_dim, N_HEADS * HEAD), F32), jax.ShapeDtypeStruct((N_HEADS, 1, s_dim), F32)],
        grid=(N_HEADS, s_dim // tqs),
        in_specs=[pl.BlockSpec((tqs, QK_HEAD), lambda h, qi: (qi, h)),
                  pl.BlockSpec((s_dim, QK_HEAD), lambda h, qi: (0, h)),
                  pl.BlockSpec((HEAD, s_dim), lambda h, qi: (h, 0))],
        out_specs=[pl.BlockSpec((tqs, HEAD), lambda h, qi: (qi, h)),
                   pl.BlockSpec((None, 1, tqs), lambda h, qi: (h, 0, qi))],
        scratch_shapes=[pltpu.VMEM((1, tqs), F32), pltpu.VMEM((1, tqs), F32), pltpu.VMEM((HEAD, tqs), F32)],
        compiler_params=pltpu.CompilerParams(dimension_semantics=("parallel", "parallel")),
        name="attn_fwd",
    )(qc, kc, vt)


def _attn_bwd(qc, kc, kct, v, o, d_o, lse):
    s_dim = qc.shape[0]
    tq, n_chains, tk = _attn_tiles(s_dim)
    tqs = tq * n_chains

    def body(q_ref, k_ref, kt_ref, v_ref, o_ref, do_ref, lse_ref, dq_ref, dk_ref, dv_ref, dqt_acc, dv_acc):
        qi = pl.program_id(1)

        @pl.when(qi == 0)
        def _():
            dk_ref[...] = jnp.zeros_like(dk_ref)
            dv_acc[...] = jnp.zeros_like(dv_acc)

        dqt_acc[...] = jnp.zeros_like(dqt_acc)
        do_f = do_ref[...]
        do_all = do_f.astype(BF16)
        q_all = q_ref[...]
        lse_row = lse_ref[...]
        delta_row = _dot3(jnp.ones((8, HEAD), F32), o_ref[...] * do_f, NT)[0:1, :]

        def make_step(chains):
            rows = slice(chains[0][0] * tq, (chains[-1][0] + 1) * tq)

            def step(j, carry):
                ks = pl.multiple_of(j * tk, tk)
                kb, vb, ktb = k_ref[pl.ds(ks, tk), :], v_ref[pl.ds(ks, tk), :], kt_ref[:, pl.ds(ks, tk)]
                pts, dsts = [], []
                for c, masked in chains:
                    cols = slice(c * tq, (c + 1) * tq)
                    st = _dot(kb, q_all[cols, :], NT)
                    if masked:
                        st = _causal_mask_t(st, j * tk, qi * tqs + c * tq)
                    pt = jnp.exp(st - lse_row[:, cols])
                    dst = pt * (_dot(vb, do_all[cols, :], NT) - delta_row[:, cols])
                    dst_b = dst.astype(BF16)
                    dqt_acc[:, cols] += _dot(ktb, dst_b)
                    pts.append(pt.astype(BF16))
                    dsts.append(dst_b)
                pt_all = jnp.concatenate(pts, axis=1) if len(chains) > 1 else pts[0]
                dst_all = jnp.concatenate(dsts, axis=1) if len(chains) > 1 else dsts[0]
                dk_ref[pl.ds(ks, tk), :] += _dot(dst_all, q_all[rows, :])
                dv_acc[pl.ds(ks, tk), :] += _dot(pt_all, do_all[rows, :])
                return carry
            return step

        below = qi * (tqs // tk)
        lax.fori_loop(0, below, make_step([(c, False) for c in range(n_chains)]), 0)
        for t in range(tqs // tk):
            make_step(_diagonal_chains(t, tq, n_chains, tk))(below + t, 0)
        dq_ref[...] = jnp.transpose(dqt_acc[...])

        @pl.when(qi == s_dim // tqs - 1)
        def _():
            dv_ref[...] = dv_acc[...].astype(dv_ref.dtype)

    q_spec = pl.BlockSpec((tqs, QK_HEAD), lambda h, qi: (qi, h))
    o_spec = pl.BlockSpec((tqs, HEAD), lambda h, qi: (qi, h))
    k_spec = pl.BlockSpec((s_dim, QK_HEAD), lambda h, qi: (0, h))
    v_spec = pl.BlockSpec((s_dim, HEAD), lambda h, qi: (0, h))
    wide2 = jax.ShapeDtypeStruct((s_dim, N_HEADS * QK_HEAD), F32)
    return pl.pallas_call(
        body,
        out_shape=[wide2, wide2, jax.ShapeDtypeStruct((s_dim, N_HEADS * HEAD), BF16)],
        grid=(N_HEADS, s_dim // tqs),
        in_specs=[q_spec, k_spec, pl.BlockSpec((QK_HEAD, s_dim), lambda h, qi: (h, 0)), v_spec, o_spec, o_spec,
                  pl.BlockSpec((None, 1, tqs), lambda h, qi: (h, 0, qi))],
        out_specs=[q_spec, k_spec, v_spec],
        scratch_shapes=[pltpu.VMEM((QK_HEAD, tqs), F32), pltpu.VMEM((s_dim, HEAD), F32)],
        compiler_params=pltpu.CompilerParams(dimension_semantics=("parallel", "arbitrary")),
        name="attn_bwd",
    )(qc, kc, kct, v, o, d_o, lse)


def _merge_fwd(y_dn, y_mla, proj_g):
    def fn(r, c):
        yd, ym, g = r
        return [_sig(g[:, :D_MODEL]) * yd + _sig(g[:, D_MODEL:]) * ym], []

    return _rowwise(fn, [y_dn, y_mla, proj_g], [], [(D_MODEL, BF16)], name="merge_fwd")[0]


def _merge_bwd(y_dn, y_mla, proj_g, d_mixed):
    def fn(r, c):
        yd, ym, g, dm = r
        sd, sm = _sig(g[:, :D_MODEL]), _sig(g[:, D_MODEL:])
        d_g = jnp.concatenate([dm * yd * sd * (1.0 - sd), dm * ym * sm * (1.0 - sm)], axis=1)
        return [d_g, dm * sd, dm * sm], []

    return _rowwise(fn, [y_dn, y_mla, proj_g, d_mixed], [], [(2 * D_MODEL, BF16), (D_MODEL, BF16), (D_MODEL, BF16)],
                    name="merge_bwd")


def _ln_stats(z):
    mu = _rowmean(z)
    zc = z - mu
    r = lax.rsqrt(_rowmean(zc * zc) + EPS_LN)
    return zc * r, r


def _ln_bwd(dy, xh, r, g):
    dxh = dy * g
    return r * (dxh - _rowmean(dxh) - xh * _rowmean(dxh * xh))


def _ln1_fwd(x, a1, g, b):
    def fn(r, c):
        xh, _ = _ln_stats(ALPHA * r[0] + r[1])
        y = xh * c[0] + c[1]
        return [y, y], []

    return _rowwise(fn, [x, a1], [g, b], [(D_MODEL, F32), (D_MODEL, BF16)], name="ln1_fwd")


def _ln1_bwd(x, a1, d_h1, g):
    def fn(r, c):
        xh, rr = _ln_stats(ALPHA * r[0] + r[1])
        dy = r[2]
        dz = _ln_bwd(dy, xh, rr, c[0])
        return [dz, ALPHA * dz], [_colsum(dy * xh), _colsum(dy)]

    return _rowwise(fn, [x, a1, d_h1], [g], [(D_MODEL, BF16), (D_MODEL, F32)], accs=[(1, D_MODEL), (1, D_MODEL)],
                    name="ln1_bwd")


def _act_fwd(gu):
    def fn(r, c):
        gt, up = r[0][:, :FFN_HIDDEN], r[0][:, FFN_HIDDEN:]
        return [gt * _sig(gt) * up], []

    return _rowwise(fn, [gu], [], [(FFN_HIDDEN, BF16)], name="act_fwd")[0]


def _act_bwd(gu, d_act):
    def fn(r, c):
        gt, up = r[0][:, :FFN_HIDDEN], r[0][:, FFN_HIDDEN:]
        da = r[1]
        return [jnp.concatenate([da * up * _silu_grad(gt), da * gt * _sig(gt)], axis=1)], []

    return _rowwise(fn, [gu, d_act], [], [(2 * FFN_HIDDEN, BF16)], name="act_bwd")[0]


def _tail(h1, ffn, pg, pp, tgt, g, b):
    def fn(r, c):
        h1_, ffn_, pg_, pp_, t_ = r
        sp = _sig(pg_)
        xh, rr = _ln_stats(ALPHA * h1_ + ffn_ + sp * pp_)
        y = xh * c[0] + c[1]
        err = y - t_
        dy = err * (1.0 / D_MODEL)
        dz = _ln_bwd(dy, xh, rr, c[0])
        loss = jnp.sum(0.5 * _rowmean(err * err), axis=0, keepdims=True)
        return ([dz, dz * pp_ * sp * (1.0 - sp), dz * sp, ALPHA * dz],
                [_colsum(dy * xh), _colsum(dy), jnp.broadcast_to(loss, (1, LANES))])

    return _rowwise(fn, [h1, ffn, pg, pp, tgt], [g, b], [(D_MODEL, BF16)] * 3 + [(D_MODEL, F32)],
                    accs=[(1, D_MODEL), (1, D_MODEL), (1, LANES)], name="tail")


def _local_step(x, p, pos, tgt, w, late_weights, emit):
    w = dict(w)
    s_dim = x.shape[0]
    xb, pb = x.astype(BF16), p.astype(BF16)
    proj_a = _mm(xb, w["wa"], name="f_proj_a")
    proj_g = _mm(xb, w["wg"], name="f_proj_g")
    proj_b = _mm(xb, w["wb"], name="f_proj_b")
    qkvn = _conv_fwd(proj_a, w["conv"])
    beta, gc = _gates_fwd(proj_b, w["alog"], w["dtb"])
    gc_t = jnp.transpose(gc[:, :N_HEADS])
    u, w_, qd, kt, a_mat, t_fold = _gdr_prep_fwd(qkvn, beta, gc, gc_t)
    o_dn, states = _gdr_scan_fwd(u, w_, qd, kt, a_mat, gc)
    og = _gdr_out_fwd(o_dn, proj_a, w["dnw"])
    w.update(late_weights("mix", og))
    y_dn = _mm(og, w["br_dn"], name="f_y_dn")
    c_q, c_kv = _mla_norm_fwd(proj_b, w["qnw"], w["kvnw"])
    q_full = _mm(c_q, w["uq"], name="f_q_full")
    k_nope = _mm(c_kv, w["uk"], name="f_k_nope")
    vv = _mm(c_kv, w["uv"], out_dtype=BF16, name="f_v")
    qc, kc = _mla_qk_fwd(q_full, k_nope, proj_b, pos)
    o_mla, lse = _attn_fwd(qc, kc, jnp.transpose(vv))
    y_mla = _mm(o_mla, w["br_mla"], name="f_y_mla")
    mixed = _merge_fwd(y_dn, y_mla, proj_g)
    a1 = _mm(mixed, w["wo"], name="f_a1")
    w.update(late_weights("ffn", a1))
    h1, h1b = _ln1_fwd(x, a1, w["ln1g"], w["ln1b"])
    gu = _mm(h1b, w["ffn_in"], name="f_gu")
    act = _act_fwd(gu)
    ffn = _mm(act, w["ffn_out"], name="f_ffn")
    pg = _mm(h1b, w["ple_gate"], name="f_pg")
    pp = _mm(pb, w["ple"], name="f_pp")
    g = {}
    dz2, d_pg, d_pp, dh1a, g["ln2g"], g["ln2b"], loss = _tail(h1, ffn, pg, pp, tgt, w["ln2g"], w["ln2b"])
    g["ple_t"] = _mm(d_pp, pb, ta=True, out_dtype=BF16, name="b_w_ple")
    g["ple_gate"] = _mm(h1b, d_pg, ta=True, out_dtype=BF16, name="b_w_ple_gate")
    g["ffn_out"] = _mm(act, dz2, ta=True, out_dtype=BF16, name="b_w_ffn_out")
    d_act = _mm(dz2, w["ffn_out"], tb=True, name="b_act")
    d_gu = _act_bwd(gu, d_act)
    g["ffn_in_t"] = _mm(d_gu, h1b, ta=True, out_dtype=BF16, name="b_w_ffn_in")
    d_gu = emit("ffn", g, d_gu)
    d_h1 = _mm(d_gu, w["ffn_in_t"], add=(dh1a,), name="b_h1_ffn")
    d_h1 = _mm(d_pg, w["ple_gate"], tb=True, add=(d_h1,), name="b_h1_ple")
    dz1, dxa, g["ln1g"], g["ln1b"] = _ln1_bwd(x, a1, d_h1, w["ln1g"])
    g["wo"] = _mm(mixed, dz1, ta=True, out_dtype=BF16, name="b_w_o")
    d_mixed = _mm(dz1, w["wo"], tb=True, name="b_mixed")
    d_proj_g, d_y_dn, d_y_mla = _merge_bwd(y_dn, y_mla, proj_g, d_mixed)
    g["br_mla"] = _mm(o_mla, d_y_mla, ta=True, out_dtype=BF16, name="b_w_br_mla")
    d_o_mla = _mm(d_y_mla, w["br_mla"], tb=True, name="b_o_mla")
    d_qc, d_kc, d_v = _attn_bwd(qc, kc, jnp.transpose(kc), vv, o_mla, d_o_mla, lse)
    d_q_full, d_kn, d_kr = _mla_qk_bwd(d_qc, d_kc, pos)
    g["uq"] = _mm(c_q, d_q_full, ta=True, out_dtype=BF16, name="b_w_uq")
    d_c_q = _mm(d_q_full, w["uq"], tb=True, name="b_c_q")
    g["uk"] = _mm(c_kv, d_kn, ta=True, out_dtype=BF16, name="b_w_uk")
    g["uv"] = _mm(c_kv, d_v, ta=True, out_dtype=BF16, name="b_w_uv")
    d_c_kv = _mm(d_kn, w["uk"], tb=True, name="b_c_kv_k")
    d_c_kv = _mm(d_v, w["uv"], tb=True, add=(d_c_kv,), name="b_c_kv_v")
    d_cq, d_ckv, g["qnw"], g["kvnw"] = _mla_norm_bwd(proj_b, w["qnw"], w["kvnw"], d_c_q, d_c_kv)
    g["br_dn"] = _mm(og, d_y_dn, ta=True, out_dtype=BF16, name="b_w_br_dn")
    d_og = emit("mix", g, _mm(d_y_dn, w["br_dn"], tb=True, name="b_og"))
    d_o_dn, d_z, g["dnw"] = _gdr_out_bwd(o_dn, proj_a, d_og, w["dnw"])
    du, dw, dqd, dkt, d_a, d_egl = _gdr_scan_bwd(u, w_, qd, kt, a_mat, gc, states, d_o_dn)
    dq, dk, dv, d_beta, d_gc = _gdr_prep_bwd(qkvn, beta, gc, gc_t, t_fold, u, w_, du, dw, dqd, dkt, d_a)
    d_egl_rows = jnp.pad(d_egl[:, None, :, 0], ((0, 0), (CHUNK - 1, 0), (0, LANES - N_HEADS))).reshape(s_dim, LANES)
    d_ba, g["alog"], g["dtb"] = _gates_bwd(proj_b, w["alog"], w["dtb"], gc, d_beta, d_gc, d_egl_rows)
    d_qkv, g["conv"] = _conv_bwd(proj_a, w["conv"], dq, dk, dv)
    zeros = jnp.zeros((s_dim, WB_CKV - Q_LORA), BF16)
    d_proj_b = jnp.concatenate([d_cq, zeros, d_ckv, d_kr, d_ba], axis=1)
    g["wa_qkv_t"] = _mm(d_qkv, xb, ta=True, name="b_w_qkv")
    g["wa_z_t"] = _mm(d_z, xb, ta=True, name="b_w_z")
    g["wg_t"] = _mm(d_proj_g, xb, ta=True, name="b_w_g")
    g["wb_t"] = _mm(d_proj_b, xb, ta=True, name="b_w_b")
    dx = _mm(d_qkv, w["wa_qkv_t"], add=(dxa,), name="b_x_qkv")
    dx = _mm(d_z, w["wa_z_t"], add=(dx,), name="b_x_z")
    dx = _mm(d_proj_g, w["wg_t"], add=(dx,), name="b_x_g")
    dx = _mm(d_proj_b, w["wb_t"], add=(dx,), name="b_x_b")
    return loss, dx, g


_BIG = (("w_in", 1), ("w_uq", 0), ("w_uk", 0), ("w_uv", 0), ("w_br_dn", 0), ("w_br_mla", 0),
        ("w_o", 0), ("w_ffn_in", 1), ("w_ffn_out", 0), ("w_ple", 1), ("w_ple_gate", 0))
_BIG_AXIS = dict(_BIG)
_SMALL = ("ln1_g", "ln1_b", "ln2_g", "ln2_b", "q_norm_w", "kv_norm_w", "dn_norm_w", "dn_a_log", "dn_dt_bias")
_ORDER = ("w_in", "conv_w", "dn_a_log", "dn_dt_bias", "dn_norm_w", "q_norm_w", "w_uq", "kv_norm_w", "w_uk", "w_uv",
          "w_br_dn", "w_br_mla", "w_o", "ln1_g", "ln1_b", "w_ffn_in", "w_ffn_out", "w_ple", "w_ple_gate", "ln2_g",
          "ln2_b")


def _stored_shape(name, shard_shape):
    axis = _BIG_AXIS[name]
    lead = shard_shape[axis]
    return lead, int(np.prod(shard_shape)) // lead


def _to_stored(name, shard):
    return jnp.moveaxis(shard, _BIG_AXIS[name], 0).reshape(_stored_shape(name, shard.shape))


def _from_stored(name, stored, shard_shape):
    axis = _BIG_AXIS[name]
    moved = (shard_shape[axis],) + shard_shape[:axis] + shard_shape[axis + 1:]
    return jnp.moveaxis(stored.reshape(moved), 0, axis)


_W_IN_ROWS = np.cumsum([0, 3072, 1024, 8, 8, Q_LORA, KV_LORA, ROPE, D_MODEL, D_MODEL])


def _first_weights(w_in_t, conv_full, small):
    r = _W_IN_ROWS
    zr = lambda n: jnp.zeros((n, D_MODEL), w_in_t.dtype)
    w = {}
    w["wa_t"] = w_in_t[r[0]:r[2]]
    w["wa_qkv_t"], w["wa_z_t"] = w_in_t[r[0]:r[1]], w_in_t[r[1]:r[2]]
    w["wg_t"] = w_in_t[r[7]:r[9]]
    w["wb_t"] = jnp.concatenate([w_in_t[r[4]:r[5]], zr(WB_CKV - Q_LORA), w_in_t[r[5]:r[7]], zr(LANES - ROPE),
                                 w_in_t[r[2]:r[4]], zr(LANES - 2 * N_HEADS)], axis=0)
    for k_ in ("wa", "wg", "wb"):
        w[k_] = jnp.transpose(w[k_ + "_t"])
    w["conv"] = conv_full
    pad_l = lambda v: jnp.pad(v, ((0, 0), (0, LANES - v.shape[1])))
    w["alog"], w["dtb"] = pad_l(small["dn_a_log"]), pad_l(small["dn_dt_bias"])
    w["dnw"], w["qnw"], w["kvnw"] = small["dn_norm_w"], small["q_norm_w"], small["kv_norm_w"]
    w["ln1g"], w["ln1b"], w["ln2g"], w["ln2b"] = small["ln1_g"], small["ln1_b"], small["ln2_g"], small["ln2_b"]
    return w


def _late_weights(group, fw):
    w = {}
    if group == "mix":
        uq = fw["w_uq"].reshape(Q_LORA, N_HEADS, HEAD + ROPE)
        uq_r = jnp.pad(uq[:, :, HEAD:], ((0, 0), (0, 0), (0, HEAD - ROPE)))
        w["uq"] = jnp.concatenate([uq[:, :, :HEAD].reshape(Q_LORA, -1), uq_r.reshape(Q_LORA, -1)], axis=1)
        w["uk"], w["uv"] = fw["w_uk"], fw["w_uv"]
        w["br_dn"], w["br_mla"], w["wo"] = fw["w_br_dn"], fw["w_br_mla"], fw["w_o"]
    else:
        w["ffn_in_t"], w["ffn_out"] = fw["w_ffn_in"], fw["w_ffn_out"]
        w["ple_t"], w["ple_gate"] = fw["w_ple"], fw["w_ple_gate"]
        for k_ in ("ffn_in", "ple"):
            w[k_] = jnp.transpose(w[k_ + "_t"])
    return w


_GROUP_GRADS = {"ffn": (("w_ple", "ple_t"), ("w_ple_gate", "ple_gate"), ("w_ffn_out", "ffn_out"),
                        ("w_ffn_in", "ffn_in_t")),
                "mix": (("w_o", "wo"), ("w_br_mla", "br_mla"), ("w_uq", "uq"), ("w_uk", "uk"), ("w_uv", "uv"),
                        ("w_br_dn", "br_dn"))}


def _group_grads(group, g):
    out = {}
    for name, key in _GROUP_GRADS[group]:
        t = g[key]
        if name == "w_uq":
            uq_n = t[:, :D_MODEL].reshape(Q_LORA, N_HEADS, HEAD)
            uq_r = t[:, D_MODEL:].reshape(Q_LORA, N_HEADS, HEAD)[:, :, :ROPE]
            t = jnp.concatenate([uq_n, uq_r], axis=2).reshape(Q_LORA, -1)
        out[name] = t
    return out


def _last_grads(g):
    wb = g["wb_t"]
    w_in = jnp.concatenate([
        g["wa_qkv_t"], g["wa_z_t"], wb[WB_BA:WB_BA + 2 * N_HEADS], wb[WB_CQ:WB_CQ + Q_LORA],
        wb[WB_CKV:WB_CKV + KV_LORA], wb[WB_KR:WB_KR + ROPE], g["wg_t"]], axis=0)
    small = {"ln1_g": g["ln1g"], "ln1_b": g["ln1b"], "ln2_g": g["ln2g"], "ln2_b": g["ln2b"], "q_norm_w": g["qnw"],
             "kv_norm_w": g["kvnw"], "dn_norm_w": g["dnw"], "dn_a_log": g["alog"], "dn_dt_bias": g["dtb"],
             "conv_w": g["conv"]}
    return w_in, small


_SMALL_SLOTS = {"ln1_g": (0, 0, 1024), "ln1_b": (1, 0, 1024), "ln2_g": (2, 0, 1024), "ln2_b": (3, 0, 1024),
                "q_norm_w": (4, 0, 384), "kv_norm_w": (4, 384, 256), "dn_norm_w": (4, 640, 128),
                "dn_a_log": (4, 768, 8), "dn_dt_bias": (4, 896, 8)}
_SMALL_ROWS, _LOSS_ROW, _CONV_ROW0, _CONV_ROWS = 24, 5, 8, 12


def _pack_small_grads(small_g, loss):
    zeros = lambda r, c: jnp.zeros((r, c), F32)
    row4 = jnp.concatenate([small_g["q_norm_w"], small_g["kv_norm_w"], small_g["dn_norm_w"], small_g["dn_a_log"],
                            small_g["dn_dt_bias"]], axis=1)
    row5 = jnp.concatenate([loss, zeros(1, FLAT_COLS - LANES)], axis=1)
    head = jnp.concatenate([small_g["ln1_g"], small_g["ln1_b"], small_g["ln2_g"], small_g["ln2_b"], row4, row5,
                            zeros(2, FLAT_COLS)], axis=0)
    conv = small_g["conv_w"].reshape(_CONV_ROWS, FLAT_COLS)
    return jnp.concatenate([head, conv, zeros(_SMALL_ROWS - _CONV_ROW0 - _CONV_ROWS, FLAT_COLS)], axis=0)


_MESH_ID = pl.DeviceIdType.MESH
_ANY = pl.BlockSpec(memory_space=pl.ANY)


def _all_gather(blocks, name):
    n = len(blocks)

    def body(*refs):
        x_refs, out_refs = refs[:n], refs[n:2 * n]
        send_sems, recv_sems, local_sems = refs[2 * n:]
        x, y, c = lax.axis_index("x"), lax.axis_index("y"), lax.axis_index("c")
        me, sibling = (x, y, c), (x, y, 1 - c)
        chips = [(1 - x, y), (x, 1 - y), (1 - x, 1 - y)]

        def slot(i, px, py, pc):
            return out_refs[i].at[4 * px + 2 * py + pc]

        def copy(i, k, origin, to, src=None):
            return pltpu.make_async_remote_copy(
                src_ref=slot(i, *origin) if src is None else src, dst_ref=slot(i, *origin),
                send_sem=send_sems.at[7 * i + k], recv_sem=recv_sems.at[7 * i + k], device_id=to,
                device_id_type=_MESH_ID)

        mine = [pltpu.make_async_copy(x_refs[i], slot(i, *me), local_sems.at[i]) for i in range(n)]
        first, passed = [], []
        for i in range(n):
            mine[i].start()
            first.append(copy(i, 0, me, sibling, src=x_refs[i]))
            first += [copy(i, 1 + j, me, (*chip, c), src=x_refs[i]) for j, chip in enumerate(chips)]
        for cp in first:
            cp.start()
        for i in range(n):
            for j, chip in enumerate(chips):
                copy(i, 1 + j, (*chip, c), me).wait_recv()
                passed.append(copy(i, 4 + j, (*chip, c), sibling))
                passed[-1].start()
        for i in range(n):
            copy(i, 0, sibling, me).wait_recv()
            for j, chip in enumerate(chips):
                copy(i, 4 + j, (*chip, 1 - c), me).wait_recv()
        for cp in first + passed:
            cp.wait_send()
        for cp in mine:
            cp.wait()

    return pl.pallas_call(
        body,
        out_shape=[jax.ShapeDtypeStruct((N_DEV,) + b.shape, b.dtype) for b in blocks],
        in_specs=[_ANY] * n,
        out_specs=[_ANY] * n,
        scratch_shapes=[pltpu.SemaphoreType.DMA((7 * n,)), pltpu.SemaphoreType.DMA((7 * n,)),
                        pltpu.SemaphoreType.DMA((n,))],
        name=name,
    )(*blocks)


def _exchange_sibling(srcs, name):
    n = len(srcs)

    def body(*refs):
        src_refs, dst_refs = refs[:n], refs[n:2 * n]
        send_sems, recv_sems = refs[2 * n:]
        x, y, c = lax.axis_index("x"), lax.axis_index("y"), lax.axis_index("c")
        copies = [pltpu.make_async_remote_copy(
            src_ref=src_refs[i].at[2 * q + (1 - c)], dst_ref=dst_refs[i].at[q], send_sem=send_sems.at[4 * i + q],
            recv_sem=recv_sems.at[4 * i + q], device_id=(x, y, 1 - c), device_id_type=_MESH_ID)
            for i in range(n) for q in range(4)]
        for cp in copies:
            cp.start()
        for cp in copies:
            cp.wait_recv()
        for cp in copies:
            cp.wait_send()

    return pl.pallas_call(
        body,
        out_shape=[jax.ShapeDtypeStruct((4,) + s.shape[1:], s.dtype) for s in srcs],
        in_specs=[_ANY] * n,
        out_specs=[_ANY] * n,
        scratch_shapes=[pltpu.SemaphoreType.DMA((4 * n,)), pltpu.SemaphoreType.DMA((4 * n,))],
        name=name,
    )(*srcs)


def _exchange_chips(srcs, name):
    n = len(srcs)

    def body(*refs):
        src_refs, dst_refs = refs[:n], refs[n:2 * n]
        send_sems, recv_sems = refs[2 * n:]
        x, y, c = lax.axis_index("x"), lax.axis_index("y"), lax.axis_index("c")
        chips = [(1 - x, y), (x, 1 - y), (1 - x, 1 - y)]
        copies = [pltpu.make_async_remote_copy(
            src_ref=src_refs[i].at[2 * tx + ty], dst_ref=dst_refs[i].at[j], send_sem=send_sems.at[3 * i + j],
            recv_sem=recv_sems.at[3 * i + j], device_id=(tx, ty, c), device_id_type=_MESH_ID)
            for i in range(n) for j, (tx, ty) in enumerate(chips)]
        for cp in copies:
            cp.start()
        for cp in copies:
            cp.wait_recv()
        for cp in copies:
            cp.wait_send()

    return pl.pallas_call(
        body,
        out_shape=[jax.ShapeDtypeStruct((3,) + s.shape[1:], s.dtype) for s in srcs],
        in_specs=[_ANY] * n,
        out_specs=[_ANY] * n,
        scratch_shapes=[pltpu.SemaphoreType.DMA((3 * n,)), pltpu.SemaphoreType.DMA((3 * n,))],
        name=name,
    )(*srcs)


def _col_tile(c):
    return c if c <= 256 else 256


def _chip_sum(src, recv, parity, name):
    _, r, c = src.shape
    tc = _col_tile(c)

    def body(par_ref, a_ref, b_ref, o_ref, ob_ref):
        s = a_ref[...] + b_ref[...]
        o_ref[...] = s
        ob_ref[...] = s.astype(BF16)

    blk = lambda f: pl.BlockSpec((None, r, tc), f)
    return pl.pallas_call(
        body,
        out_shape=[jax.ShapeDtypeStruct((4, r, c), F32), jax.ShapeDtypeStruct((4, r, c), BF16)],
        grid_spec=pltpu.PrefetchScalarGridSpec(
            num_scalar_prefetch=1, grid=(4, c // tc),
            in_specs=[blk(lambda q, j, par: (2 * q + par[0], 0, j)), blk(lambda q, j, par: (q, 0, j))],
            out_specs=[blk(lambda q, j, par: (q, 0, j)), blk(lambda q, j, par: (q, 0, j))]),
        compiler_params=pltpu.CompilerParams(dimension_semantics=("parallel", "parallel")),
        name=name,
    )(parity, src, recv)


def _sum_parts(own, others, chip, name):
    _, r, c = own.shape
    tc = _col_tile(c)

    def body(q_ref, a_ref, b_ref, o_ref):
        o_ref[...] = ((a_ref[...] + b_ref[0].astype(F32)) + b_ref[1].astype(F32)) + b_ref[2].astype(F32)

    return pl.pallas_call(
        body,
        out_shape=jax.ShapeDtypeStruct((r, c), F32),
        grid_spec=pltpu.PrefetchScalarGridSpec(
            num_scalar_prefetch=1, grid=(c // tc,),
            in_specs=[pl.BlockSpec((None, r, tc), lambda j, q: (q[0], 0, j)),
                      pl.BlockSpec((3, r, tc), lambda j, q: (0, 0, j))],
            out_specs=pl.BlockSpec((r, tc), lambda j, q: (0, j))),
        compiler_params=pltpu.CompilerParams(dimension_semantics=("parallel",)),
        name=name,
    )(chip, own, others)


_HBM = pl.BlockSpec(memory_space=pltpu.HBM)
_SEM = pl.BlockSpec(memory_space=pltpu.SEMAPHORE)
_DATAFLOW = pltpu.SideEffectType.DATAFLOW_SIDE_EFFECTING
N_PEERS = N_DEV - 1


def _ring_peer(j):
    me = 4 * lax.axis_index("x") + 2 * lax.axis_index("y") + lax.axis_index("c")
    k = (me + j) % N_DEV
    return me, k, (k // 4, (k // 2) % 2, k % 2)


def _spread_copy(i, j, src_refs, land_refs, send_sems, recv_sems, scatter):
    me, k, peer = _ring_peer(j)
    return pltpu.make_async_remote_copy(
        src_ref=src_refs[i].at[k] if scatter else src_refs[i], dst_ref=land_refs[i].at[me],
        send_sem=send_sems.at[N_PEERS * i + j - 1], recv_sem=recv_sems.at[N_PEERS * i + j - 1], device_id=peer,
        device_id_type=_MESH_ID)


def _spread_start(srcs, carry, scatter, name):
    n = len(srcs)
    lands = [lax.empty(((N_DEV,) + s.shape[-2:]), s.dtype) for s in srcs]

    def body(*refs):
        src_refs, land_refs = refs[:n], refs[n:2 * n]
        send_sems, recv_sems, local_sems = refs[2 * n + 1:2 * n + 4]
        for i in range(n):
            for j in range(1, N_DEV):
                _spread_copy(i, j, src_refs, land_refs, send_sems, recv_sems, scatter).start()
        for i in range(n):
            _own_copy(i, src_refs, land_refs, local_sems, scatter).start()

    hbm = lambda a: pltpu.HBM(a.shape, a.dtype)
    sems = pltpu.SemaphoreType.DMA((N_PEERS * n,))
    pinned = [pltpu.with_memory_space_constraint(a, pltpu.HBM) for a in list(srcs) + lands + [carry]]
    res = pl.pallas_call(
        body, name=name,
        out_shape=(sems, sems, pltpu.SemaphoreType.DMA((n,)), *[hbm(a) for a in pinned]),
        in_specs=[_HBM] * (2 * n + 1),
        out_specs=(_SEM, _SEM, _SEM, *[_HBM] * (2 * n + 1)),
        input_output_aliases={i: 3 + i for i in range(2 * n + 1)},
        compiler_params=pltpu.CompilerParams(has_side_effects=_DATAFLOW),
    )(*pinned)
    return res[:3], list(res[3:3 + n]), list(res[3 + n:3 + 2 * n]), res[3 + 2 * n]


def _own_copy(i, src_refs, land_refs, local_sems, scatter):
    me = _ring_peer(0)[0]
    return pltpu.make_async_copy(src_refs[i].at[me] if scatter else src_refs[i], land_refs[i].at[me],
                                 local_sems.at[i])


def _spread_wait(started, after, scatter, name):
    sems, srcs, lands, _ = started
    n = len(srcs)

    def body(*refs):
        src_refs, land_refs = refs[:n], refs[n:2 * n]
        send_s, recv_s, local_s = refs[2 * n:2 * n + 3]
        for i in range(n):
            for j in range(1, N_DEV):
                cp = _spread_copy(i, j, src_refs, land_refs, send_s, recv_s, scatter)
                cp.wait_send()
                cp.wait_recv()
        for i in range(n):
            _own_copy(i, src_refs, land_refs, local_s, scatter).wait()

    hbm = lambda a: pltpu.HBM(a.shape, a.dtype)
    res = pl.pallas_call(
        body, name=name,
        out_shape=tuple(hbm(a) for a in srcs + lands),
        in_specs=[_HBM] * (2 * n) + [_SEM, _SEM, _SEM, pl.BlockSpec(memory_space=pl.ANY)],
        out_specs=tuple([_HBM] * (2 * n)),
        input_output_aliases={i: i for i in range(2 * n)},
        compiler_params=pltpu.CompilerParams(has_side_effects=_DATAFLOW),
    )(*srcs, *lands, *sems, after)
    return list(res[n:])


def _sum8(landing, name):
    _, r, c = landing.shape
    tc = _col_tile(c)

    def body(a_ref, o_ref):
        tot = a_ref[0].astype(F32)
        for k in range(1, N_DEV):
            tot = tot + a_ref[k].astype(F32)
        o_ref[...] = tot

    return pl.pallas_call(
        body,
        out_shape=jax.ShapeDtypeStruct((r, c), F32),
        grid=(c // tc,),
        in_specs=[pl.BlockSpec((N_DEV, r, tc), lambda j: (0, 0, j))],
        out_specs=pl.BlockSpec((r, tc), lambda j: (0, j)),
        compiler_params=pltpu.CompilerParams(dimension_semantics=("parallel",)),
        name=name,
    )(landing)


def _adamw_math(w, g, m, v):
    m = ADAM_B1 * m + (1.0 - ADAM_B1) * g
    v = ADAM_B2 * v + (1.0 - ADAM_B2) * (g * g)
    m_hat = m / (1.0 - ADAM_B1 ** ADAM_STEP)
    v_hat = v / (1.0 - ADAM_B2 ** ADAM_STEP)
    delta = -ADAM_LR * (m_hat / (jnp.sqrt(v_hat) + ADAM_EPS) + ADAM_WD * w)
    return delta, m, v


def _adamw(w, m, v, g, name):
    r, c = w.shape

    def fn(rows, consts):
        return list(_adamw_math(*rows)), []

    return _rowwise(fn, [w, g, m, v], [], [(c, F32)] * 3, tm=r if r <= 512 else 256, name=name)


def _adamw_small(gathered, params):
    ns = len(_SMALL)

    def body(*refs):
        g_ref, p_refs, o_refs = refs[0], refs[1:1 + 3 * ns], refs[1 + 3 * ns:]
        tot = g_ref[0]
        for k in range(1, N_DEV):
            tot = tot + g_ref[k]
        for i, name in enumerate(_SMALL):
            row, lane0, lanes = _SMALL_SLOTS[name]
            g = tot[row:row + 1, lane0:lane0 + lanes]
            w_, m_, v_ = (p_refs[3 * i + j][...] for j in range(3))
            delta, m2, v2 = _adamw_math(w_, g, m_, v_)
            for j, val in enumerate((g, delta, m2, v2)):
                o_refs[4 * i + j][...] = val
        o_refs[4 * ns][...] = tot[_LOSS_ROW:_LOSS_ROW + 1, 0:LANES]
        o_refs[4 * ns + 1][...] = tot[_CONV_ROW0:_CONV_ROW0 + _CONV_ROWS, :]

    out_shape = [jax.ShapeDtypeStruct(w.shape, F32) for (w, _, _) in params for _ in range(4)]
    out_shape += [jax.ShapeDtypeStruct((1, LANES), F32), jax.ShapeDtypeStruct((_CONV_ROWS, FLAT_COLS), F32)]
    flat = [a for wmv in params for a in wmv]
    return pl.pallas_call(body, out_shape=out_shape, name="adamw_small")(gathered, *flat)


def kernel(x, p, positions, w_in, conv_w, dn_a_log, dn_dt_bias, dn_norm_w, q_norm_w, w_uq, kv_norm_w, w_uk, w_uv, w_br_dn, w_br_mla, w_o, ln1_g, ln1_b, w_ffn_in, w_ffn_out, w_ple, w_ple_gate, ln2_g, ln2_b, loss_target, m_w_in, m_conv_w, m_dn_a_log, m_dn_dt_bias, m_dn_norm_w, m_q_norm_w, m_w_uq, m_kv_norm_w, m_w_uk, m_w_uv, m_w_br_dn, m_w_br_mla, m_w_o, m_ln1_g, m_ln1_b, m_w_ffn_in, m_w_ffn_out, m_w_ple, m_w_ple_gate, m_ln2_g, m_ln2_b, v_w_in, v_conv_w, v_dn_a_log, v_dn_dt_bias, v_dn_norm_w, v_q_norm_w, v_w_uq, v_kv_norm_w, v_w_uk, v_w_uv, v_w_br_dn, v_w_br_mla, v_w_o, v_ln1_g, v_ln1_b, v_w_ffn_in, v_w_ffn_out, v_w_ple, v_w_ple_gate, v_ln2_g, v_ln2_b):
    args = dict(locals())
    wts = {n: args[n] for n in _ORDER}
    mom1 = {n: args["m_" + n] for n in _ORDER}
    mom2 = {n: args["v_" + n] for n in _ORDER}
    big_names = [n for n, _ in _BIG]
    shard_shapes = {n: wts[n].shape[1:] for n in big_names}
    c_idx = lax.axis_index("c")
    q_idx = 2 * lax.axis_index("x") + lax.axis_index("y")
    parity, chip = c_idx.reshape(1).astype(jnp.int32), q_idx.reshape(1).astype(jnp.int32)

    stored = {n: _to_stored(n, wts[n][0]).astype(BF16) for n in big_names}
    first = _all_gather([stored["w_in"], conv_w[0]], "ag_first")
    group_names = {grp: [n for n, _ in pairs] for grp, pairs in _GROUP_GRADS.items()}
    carry, gathers = first[0], {}
    for grp in ("mix", "ffn"):
        gathers[grp] = _spread_start([stored[n] for n in group_names[grp]], carry, False, "ag_start_" + grp)
        carry = gathers[grp][3]
    conv_full = jnp.moveaxis(first[1], 0, 1).reshape(conv_w.shape[1], -1)
    small_w = {n: wts[n].astype(F32) for n in _SMALL}
    w = _first_weights(carry.reshape(-1, D_MODEL), conv_full, small_w)

    def late_weights(grp, after):
        got = _spread_wait(gathers[grp], after, False, "ag_wait_" + grp)
        return _late_weights(grp, {n: t.reshape(-1, t.shape[-1]) for n, t in zip(group_names[grp], got)})

    started = {}

    def emit(group, g, carry):
        grads = _group_grads(group, g)
        srcs = [grads[n].reshape((N_DEV,) + _stored_shape(n, shard_shapes[n])) for n in grads]
        started[group] = (list(grads), _spread_start(srcs, carry, True, "rs_start_" + group))
        return started[group][1][3]

    s_dim = x.shape[1]
    loss, dx, g = _local_step(x[0], p[0, 0], positions.reshape(s_dim, 1).astype(F32), loss_target[0], w,
                              late_weights, emit)
    g_w_in, small_g = _last_grads(g)

    src = g_w_in.reshape((N_DEV,) + _stored_shape("w_in", shard_shapes["w_in"]))
    from_sibling = _exchange_sibling([src], "rs_sibling")[0]
    own, own_bf = _chip_sum(src, from_sibling, parity, "rs_sum_w_in")
    from_chips = _exchange_chips([own_bf], "rs_chips")[0]

    out_g, out_d, out_m, out_v = {}, {}, {}, {}

    def update(n, grad, shp):
        flat2 = (shp[0], int(np.prod(shp[1:])))
        d, m2, v2 = _adamw(wts[n][0].reshape(flat2), mom1[n][0].reshape(flat2), mom2[n][0].reshape(flat2),
                           grad.reshape(flat2), "adamw_" + n)
        out_g[n], out_d[n], out_m[n], out_v[n] = grad, d.reshape(shp), m2.reshape(shp), v2.reshape(shp)

    total = _sum_parts(own, from_chips, chip, "rs_total_w_in")
    update("w_in", _from_stored("w_in", total, shard_shapes["w_in"]), shard_shapes["w_in"])
    for group, (names, st) in started.items():
        for n, landing in zip(names, _spread_wait(st, dx, True, "rs_wait_" + group)):
            update(n, _from_stored(n, _sum8(landing, "rs_total_" + n), shard_shapes[n]), shard_shapes[n])

    g_small = _all_gather([_pack_small_grads(small_g, loss)], "ag_small")[0]
    res = _adamw_small(g_small, [(wts[n], mom1[n], mom2[n]) for n in _SMALL])
    for i, n in enumerate(_SMALL):
        out_g[n], out_d[n], out_m[n], out_v[n] = res[4 * i:4 * i + 4]
    loss_out = res[4 * len(_SMALL)][0, 0]
    conv_shape = conv_w.shape[1:]
    conv_g = lax.dynamic_slice(res[-1].reshape(conv_shape[0], -1), (0, (2 * q_idx + c_idx) * conv_shape[1]),
                               conv_shape)
    update("conv_w", conv_g, conv_shape)

    expand = lambda d, n: d[n] if n in _SMALL else d[n][None]
    return (loss_out, dx[None], *[expand(out_g, n) for n in _ORDER], *[expand(out_d, n) for n in _ORDER],
            *[expand(out_m, n) for n in _ORDER], *[expand(out_v, n) for n in _ORDER])
```

```python
import functools

import numpy as np
import jax
import jax.numpy as jnp
from jax import lax
from jax.experimental import pallas as pl
from jax.experimental.pallas import tpu as pltpu

F32 = jnp.float32
BF16 = jnp.bfloat16

D_MODEL = 1024
N_HEADS = 8
HEAD = 128
CHUNK = 64
GROUP = 256
ROPE = 64
Q_LORA = 384
KV_LORA = 256
FFN_HIDDEN = 2816
PLE_DIM = 256
ROPE_BASE = 10000.0
ALPHA = 2.0 ** 0.25
SCALE = float((HEAD + ROPE) ** -0.5)
NEG_BIG = -1e30
EPS_RMS = 1e-6
EPS_LN = 1e-5

ADAM_LR = 0.001
ADAM_B1 = 0.9
ADAM_B2 = 0.999
ADAM_EPS = 1e-08
ADAM_WD = 0.01
ADAM_STEP = 10

N_DEV = 8
LANES = 128
FLAT_COLS = 1024

WB_CQ, WB_CKV, WB_KR, WB_BA, WB_COLS = 0, 512, 768, 896, 1024

HIGHEST = lax.Precision.HIGHEST

NN = (((1,), (0,)), ((), ()))
TN = (((0,), (0,)), ((), ()))
NT = (((1,), (1,)), ((), ()))


def _dot(a, b, dims=NN):
    return lax.dot_general(a.astype(BF16), b.astype(BF16), dims, preferred_element_type=F32)


def _dot32(a, b, dims=NN):
    return lax.dot_general(a, b, dims, precision=HIGHEST, preferred_element_type=F32)


def _sig(x):
    return 1.0 / (1.0 + jnp.exp(-x))


MM_TILE = 1536


def _pick_wide(n):
    if n <= MM_TILE:
        return n
    return max(t for t in range(LANES, MM_TILE + 1, LANES) if n % t == 0)


def _split_bf16(a):
    hi = a.astype(BF16)
    return hi, (a - hi.astype(F32)).astype(BF16)


def _dot3(a, b, dims=NN):
    ah, al = a if isinstance(a, tuple) else _split_bf16(a)
    bh, bl = b if isinstance(b, tuple) else _split_bf16(b)
    d = lambda p, q: lax.dot_general(p, q, dims, preferred_element_type=F32)
    return d(ah, bh) + (d(ah, bl) + d(al, bh))


def _mm(a, b, *, ta=False, tb=False, add=(), out_dtype=F32, name):
    if ta:
        k_dim, m_dim = a.shape
    else:
        m_dim, k_dim = a.shape
    if tb:
        n_dim, k2 = b.shape
    else:
        k2, n_dim = b.shape
    assert k_dim == k2, (a.shape, b.shape, ta, tb)
    tm = _pick_wide(m_dim)
    tn = _pick_wide(n_dim)
    tk = _pick_wide(k_dim)
    nk = k_dim // tk
    n_add = len(add)
    dims = TN if ta else (NT if tb else NN)
    assert not (ta and tb)

    def body(a_ref, b_ref, *rest):
        add_refs = rest[:n_add]
        o_ref = rest[n_add]
        acc = rest[n_add + 1]
        k = pl.program_id(2)

        @pl.when(k == 0)
        def _():
            acc[...] = jnp.zeros_like(acc)

        acc[...] += _dot(a_ref[...], b_ref[...], dims)

        @pl.when(k == nk - 1)
        def _():
            r = acc[...]
            for ar in add_refs:
                r = r + ar[...].astype(F32)
            o_ref[...] = r.astype(o_ref.dtype)

    a_spec = pl.BlockSpec((tk, tm), lambda i, j, k: (k, i)) if ta else pl.BlockSpec((tm, tk), lambda i, j, k: (i, k))
    b_spec = pl.BlockSpec((tn, tk), lambda i, j, k: (j, k)) if tb else pl.BlockSpec((tk, tn), lambda i, j, k: (k, j))
    o_spec = pl.BlockSpec((tm, tn), lambda i, j, k: (i, j))
    return pl.pallas_call(
        body,
        out_shape=jax.ShapeDtypeStruct((m_dim, n_dim), out_dtype),
        grid=(m_dim // tm, n_dim // tn, nk),
        in_specs=[a_spec, b_spec] + [o_spec] * n_add,
        out_specs=o_spec,
        scratch_shapes=[pltpu.VMEM((tm, tn), F32)],
        compiler_params=pltpu.CompilerParams(dimension_semantics=("parallel", "parallel", "arbitrary")),
        name=name,
    )(a, b, *add)


def _rowwise(fn, rows, consts, outs, accs=(), *, tm=256, name):
    rows = [r if isinstance(r, tuple) else (r, 0, r.shape[1]) for r in rows]
    s_dim = rows[0][0].shape[0]
    tm = min(tm, s_dim)
    assert s_dim % tm == 0 and all(arr.shape[0] == s_dim for arr, _, _ in rows)
    specs = [pl.BlockSpec((tm, width), functools.partial(lambda i, cb: (i, cb), cb=cb)) for _, cb, width in rows]
    args = [arr for arr, _, _ in rows]
    for c in consts:
        specs.append(pl.BlockSpec(c.shape, lambda i: (0, 0)))
        args.append(c)
    nr, nc, no = len(rows), len(consts), len(outs)
    out_shape = [jax.ShapeDtypeStruct((s_dim, w), dt) for (w, dt) in outs]
    out_specs = [pl.BlockSpec((tm, w), lambda i: (i, 0)) for (w, dt) in outs]
    out_shape += [jax.ShapeDtypeStruct(sh, F32) for sh in accs]
    out_specs += [pl.BlockSpec(sh, lambda i: (0, 0)) for sh in accs]

    def body(*refs):
        r = [x[...] for x in refs[:nr]]
        c = [x[...] for x in refs[nr:nr + nc]]
        o_refs = refs[nr + nc:nr + nc + no]
        a_refs = refs[nr + nc + no:]
        o_vals, a_vals = fn(r, c)
        for ref, v in zip(o_refs, o_vals, strict=True):
            ref[...] = v.astype(ref.dtype)
        if a_refs:
            @pl.when(pl.program_id(0) == 0)
            def _():
                for ref in a_refs:
                    ref[...] = jnp.zeros_like(ref)

            for ref, v in zip(a_refs, a_vals, strict=True):
                ref[...] += v

    res = pl.pallas_call(
        body,
        out_shape=out_shape,
        grid=(s_dim // tm,),
        in_specs=specs,
        out_specs=out_specs,
        compiler_params=pltpu.CompilerParams(dimension_semantics=("arbitrary" if accs else "parallel",)),
        name=name,
    )(*args)
    return res


def _colsum(v):
    return jnp.sum(v, axis=0, keepdims=True)


def _rowsum(v):
    return jnp.sum(v, axis=1, keepdims=True)


def _rowmean(v):
    return jnp.mean(v, axis=1, keepdims=True)


def _silu_grad(x):
    s = _sig(x)
    return s * (1.0 + x * (1.0 - s))


def _conv_taps(x, w, width=4):
    row = lax.broadcasted_iota(jnp.int32, x.shape, 0)
    c = x * w[width - 1:width, :]
    for s in range(1, width):
        c = c + jnp.where(row >= s, pltpu.roll(x, s, 0), 0.0) * w[width - 1 - s:width - s, :]
    return c


def _conv_fwd(proj_a, conv_w):
    s_dim = proj_a.shape[0]
    n_blk = 3 * N_HEADS

    def body(x_ref, w_ref, o_ref):
        j = pl.program_id(0)
        c = _conv_taps(x_ref[...], w_ref[...])
        y = c * _sig(c)
        r = lax.rsqrt(_rowsum(y * y) + EPS_RMS)
        fac = jnp.where(j < N_HEADS, r * (HEAD ** -0.5), jnp.where(j < 2 * N_HEADS, r, 1.0))
        o_ref[...] = y * fac

    return pl.pallas_call(
        body,
        out_shape=jax.ShapeDtypeStruct((s_dim, n_blk * HEAD), F32),
        grid=(n_blk,),
        in_specs=[pl.BlockSpec((s_dim, HEAD), lambda j: (0, j)), pl.BlockSpec((4, HEAD), lambda j: (0, j))],
        out_specs=pl.BlockSpec((s_dim, HEAD), lambda j: (0, j)),
        compiler_params=pltpu.CompilerParams(dimension_semantics=("parallel",)),
        name="conv_fwd",
    )(proj_a, conv_w)


def _conv_bwd(proj_a, conv_w, dq, dk, dv):
    s_dim = proj_a.shape[0]
    n_blk = 3 * N_HEADS

    def body(x_ref, w_ref, dq_ref, dk_ref, dv_ref, dx_ref, dw_ref):
        j = pl.program_id(0)
        x = x_ref[...]
        w = w_ref[...]
        do = jnp.where(j < N_HEADS, dq_ref[...], jnp.where(j < 2 * N_HEADS, dk_ref[...], dv_ref[...]))
        c = _conv_taps(x, w)
        sg = _sig(c)
        y = c * sg
        r = lax.rsqrt(_rowsum(y * y) + EPS_RMS)
        sc = jnp.where(j < N_HEADS, HEAD ** -0.5, 1.0)
        dy_n = sc * (r * do - y * (r * r * r) * _rowsum(do * y))
        dy = jnp.where(j < 2 * N_HEADS, dy_n, do)
        dc = dy * (sg * (1.0 + c * (1.0 - sg)))
        row = lax.broadcasted_iota(jnp.int32, x.shape, 0)
        dx = dc * w[3:4, :]
        dw_ref[3:4, :] = _colsum(dc * x)
        for s in range(1, 4):
            dx = dx + jnp.where(row < s_dim - s, pltpu.roll(dc, s_dim - s, 0), 0.0) * w[3 - s:4 - s, :]
            xs = jnp.where(row >= s, pltpu.roll(x, s, 0), 0.0)
            dw_ref[3 - s:4 - s, :] = _colsum(dc * xs)
        dx_ref[...] = dx.astype(dx_ref.dtype)

    hd = N_HEADS - 1
    return pl.pallas_call(
        body,
        out_shape=[jax.ShapeDtypeStruct((s_dim, n_blk * HEAD), BF16), jax.ShapeDtypeStruct((4, n_blk * HEAD), F32)],
        grid=(n_blk,),
        in_specs=[
            pl.BlockSpec((s_dim, HEAD), lambda j: (0, j)),
            pl.BlockSpec((4, HEAD), lambda j: (0, j)),
            pl.BlockSpec((s_dim, HEAD), lambda j: (0, jnp.minimum(j, hd))),
            pl.BlockSpec((s_dim, HEAD), lambda j: (0, jnp.clip(j - N_HEADS, 0, hd))),
            pl.BlockSpec((s_dim, HEAD), lambda j: (0, jnp.clip(j - 2 * N_HEADS, 0, hd))),
        ],
        out_specs=[pl.BlockSpec((s_dim, HEAD), lambda j: (0, j)), pl.BlockSpec((4, HEAD), lambda j: (0, j))],
        compiler_params=pltpu.CompilerParams(dimension_semantics=("parallel",)),
        name="conv_bwd",
    )(proj_a, conv_w, dq, dk, dv)


def _chunk_tri(n):
    r = np.arange(n)
    m = ((r[:, None] // CHUNK) == (r[None, :] // CHUNK)) & (r[:, None] >= r[None, :])
    m = m.astype(np.float32)
    return jnp.asarray(m), jnp.asarray(m.T)


def _softplus(z):
    return jnp.maximum(z, 0.0) + jnp.log(1.0 + jnp.exp(-jnp.abs(z)))


def _gates_fwd(proj_b, alog, dtb):
    tm = min(GROUP, proj_b.shape[0])
    tri, _ = _chunk_tri(tm)

    def fn(r, c):
        b = r[0]
        a = pltpu.roll(b, LANES - N_HEADS, 1)
        alog_, dtb_, tri_ = c
        g = -jnp.exp(alog_) * _softplus(a + dtb_)
        return [_sig(b), _dot32(tri_, g)], []

    return _rowwise(fn, [(proj_b, WB_BA // LANES, LANES)], [alog, dtb, tri],
                    [(LANES, F32), (LANES, F32)], tm=tm, name="gates_fwd")


def _gates_bwd(proj_b, alog, dtb, gc, d_beta, d_gc, d_egl_rows):
    tm = min(GROUP, proj_b.shape[0])
    _, tri_t = _chunk_tri(tm)

    def fn(r, c):
        b, gc_, d_beta_, d_gc_, d_egl_ = r
        a = pltpu.roll(b, LANES - N_HEADS, 1)
        alog_, dtb_, tri_t_ = c
        z = a + dtb_
        ea = jnp.exp(alog_)
        g = -ea * _softplus(z)
        dg = _dot32(tri_t_, d_gc_ + d_egl_ * jnp.exp(gc_))
        d_a = dg * (-ea) * _sig(z)
        beta = _sig(b)
        d_ba = d_beta_ * beta * (1.0 - beta) + pltpu.roll(d_a, N_HEADS, 1)
        return [d_ba], [_colsum(dg * g), _colsum(d_a)]

    return _rowwise(fn, [(proj_b, WB_BA // LANES, LANES), gc, d_beta, d_gc, d_egl_rows],
                    [alog, dtb, tri_t], [(LANES, BF16)], accs=[(1, LANES), (1, LANES)], tm=tm,
                    name="gates_bwd")


def _group_masks(n):
    r = lax.broadcasted_iota(jnp.int32, (n, n), 0)
    c = lax.broadcasted_iota(jnp.int32, (n, n), 1)
    same = (r // CHUNK) == (c // CHUNK)
    below, s = [], 2
    while s < CHUNK:
        below.append(jnp.logical_and((r // (2 * s)) == (c // (2 * s)),
                                     jnp.logical_and((r // s) % 2 == 1, (c // s) % 2 == 0)))
        s *= 2
    return dict(same=same, tril=jnp.logical_and(same, r >= c), strict=jnp.logical_and(same, r > c),
                last=c == (r // CHUNK) * CHUNK + (CHUNK - 1), eye=r == c, pair=(r // 2) == (c // 2), below=below)


def _inv_unit_lower(l_mats, mk):
    eye_f = mk["eye"].astype(F32)
    ts = [eye_f - jnp.where(mk["pair"], l_mat, 0.0) for l_mat in l_mats]
    for below in mk["below"]:
        halves = [_split_bf16(t) for t in ts]
        mids = [_dot3(h, jnp.where(below, l_mat, 0.0)) for h, l_mat in zip(halves, l_mats)]
        ts = [t - _dot3(m, h) for t, m, h in zip(ts, mids, halves)]
    return ts


def _unfold_blocks(folded, mask):
    n = folded.shape[0]
    return jnp.where(mask, jnp.concatenate([folded] * (n // CHUNK), axis=1), 0.0)


def _head_cols(beta, gc, gc_t, h):
    lane = lax.broadcasted_iota(jnp.int32, beta.shape, 1)
    sub = lax.broadcasted_iota(jnp.int32, gc_t.shape, 0)
    bcol = _rowsum(jnp.where(lane == h, beta, 0.0))
    gcol = _rowsum(jnp.where(lane == h, gc, 0.0))
    grow = _colsum(jnp.where(sub == h, gc_t, 0.0))
    return bcol, gcol, grow


def _prep_common(q, k, bcol, gcol, grow, mk, t_folded=None):
    n = q.shape[0]
    tril = mk["tril"]
    decay = jnp.where(tril, jnp.exp(jnp.where(tril, gcol - grow, 0.0)), 0.0)
    glast = _rowsum(jnp.where(mk["last"], jnp.broadcast_to(grow, (n, n)), 0.0))
    e = jnp.exp(gcol)
    ekt = jnp.exp(glast - gcol)
    kb = k * bcol
    kk = _dot(kb, k, NT)
    qk = _dot(q, k, NT)
    p = dict(decay=decay, e=e, ekt=ekt, kb=kb, kk=kk, qk=qk)
    if t_folded is not None:
        p["t"] = _unfold_blocks(t_folded, mk["same"])
    return p


GROUPS_PER_STEP = 4


def _fold_blocks(m):
    n = m.shape[0]
    out = m[:, 0:CHUNK]
    for b in range(1, n // CHUNK):
        out = out + m[:, b * CHUNK:(b + 1) * CHUNK]
    return out


def _gdr_prep_fwd(qkvn, beta, gc, gc_t):
    s_dim = qkvn.shape[0]
    tg = min(GROUP, s_dim)
    n_sub = min(GROUPS_PER_STEP, s_dim // tg)
    tb = tg * n_sub

    def body(q_ref, k_ref, v_ref, b_ref, g_ref, gt_ref, u_ref, w_ref, qd_ref, kt_ref, a_ref, t_ref):
        h = pl.program_id(0)
        mk = _group_masks(tg)
        parts = []
        for s in range(n_sub):
            rows = slice(s * tg, (s + 1) * tg)
            q, k, v = q_ref[rows, :], k_ref[rows, :], v_ref[rows, :]
            bcol, gcol, grow = _head_cols(b_ref[rows, :], g_ref[rows, :], gt_ref[:, rows], h)
            p = _prep_common(q, k, bcol, gcol, grow, mk)
            qd_ref[rows, :] = q * p["e"]
            kt_ref[rows, :] = k * p["ekt"]
            a_ref[rows, :] = _fold_blocks(jnp.where(mk["tril"], p["qk"] * p["decay"], 0.0))
            parts.append((rows, v * bcol, p["kb"] * p["e"], jnp.where(mk["strict"], p["kk"] * p["decay"], 0.0)))
        t_mats = _inv_unit_lower([part[3] for part in parts], mk)
        for (rows, vb, kbe, _), t_mat in zip(parts, t_mats):
            u_ref[rows, :] = _dot(t_mat, vb)
            w_ref[rows, :] = _dot(t_mat, kbe)
            t_ref[rows, :] = _fold_blocks(t_mat)

    row = lambda off: pl.BlockSpec((tb, HEAD), functools.partial(lambda h, m, off: (m, h + off), off=off))
    full = pl.BlockSpec((tb, LANES), lambda h, m: (m, 0))
    o_spec = pl.BlockSpec((tb, HEAD), lambda h, m: (m, h))
    a_spec = pl.BlockSpec((None, tb, CHUNK), lambda h, m: (h, m, 0))
    wide = jax.ShapeDtypeStruct((s_dim, N_HEADS * HEAD), F32)
    folded = jax.ShapeDtypeStruct((N_HEADS, s_dim, CHUNK), F32)
    return pl.pallas_call(
        body,
        out_shape=[wide, wide, wide, wide, folded, folded],
        grid=(N_HEADS, s_dim // tb),
        in_specs=[row(0), row(N_HEADS), row(2 * N_HEADS), full, full, pl.BlockSpec((8, tb), lambda h, m: (0, m))],
        out_specs=[o_spec, o_spec, o_spec, o_spec, a_spec, a_spec],
        compiler_params=pltpu.CompilerParams(dimension_semantics=("parallel", "parallel")),
        name="gdr_prep_fwd",
    )(qkvn, qkvn, qkvn, beta, gc, gc_t)


def _gdr_prep_bwd(qkvn, beta, gc, gc_t, t_fold, u, w, du, dw, dqd, dkt, d_a):
    s_dim = qkvn.shape[0]
    tg = min(GROUP, s_dim)
    n_sub = min(GROUPS_PER_STEP, s_dim // tg)
    tb = tg * n_sub

    def body(q_ref, k_ref, v_ref, b_ref, g_ref, gt_ref, t_ref, u_ref, w_ref, du_ref, dw_ref, dqd_ref, dkt_ref,
             da_ref, dq_ref, dk_ref, dv_ref, db_ref, dg_ref):
        h = pl.program_id(1)

        @pl.when(h == 0)
        def _():
            db_ref[...] = jnp.zeros_like(db_ref)
            dg_ref[...] = jnp.zeros_like(dg_ref)

        mk = _group_masks(tg)
        lane = lax.broadcasted_iota(jnp.int32, (tg, LANES), 1)
        for s in range(n_sub):
            rows = slice(s * tg, (s + 1) * tg)
            q, k, v = q_ref[rows, :], k_ref[rows, :], v_ref[rows, :]
            bcol, gcol, grow = _head_cols(b_ref[rows, :], g_ref[rows, :], gt_ref[:, rows], h)
            p = _prep_common(q, k, bcol, gcol, grow, mk, t_ref[rows, :])
            t_mat, decay, e, ekt, kb = p["t"], p["decay"], p["e"], p["ekt"], p["kb"]
            du_, dw_, dqd_, dkt_ = du_ref[rows, :], dw_ref[rows, :], dqd_ref[rows, :], dkt_ref[rows, :]
            dvb = _dot(t_mat, du_, TN)
            dkbe = _dot(t_mat, dw_, TN)
            d_l = -(_dot(dvb, u_ref[rows, :], NT) + _dot(dkbe, w_ref[rows, :], NT))
            m1 = jnp.where(mk["strict"], d_l, 0.0)
            m2 = _unfold_blocks(da_ref[rows, :], mk["tril"])
            d_kk = m1 * decay
            d_qk = m2 * decay
            d_decay = m1 * p["kk"] + m2 * p["qk"]
            dkb = _dot(d_kk, k) + dkbe * e
            dk = _dot(d_kk, kb, TN) + _dot(d_qk, q, TN) + dkt_ * ekt + dkb * bcol
            dq = _dot(d_qk, k) + dqd_ * e
            d_beta = _rowsum(dkb * k) + _rowsum(dvb * v)
            d_e = _rowsum(dkbe * kb) + _rowsum(dqd_ * q)
            d_ekt = _rowsum(dkt_ * k) * ekt
            d_diff = d_decay * decay
            d_grow = -_colsum(d_diff) + _colsum(jnp.where(mk["last"], jnp.broadcast_to(d_ekt, (tg, tg)), 0.0))
            d_gcol = d_e * e - d_ekt + _rowsum(d_diff)
            d_gcol = d_gcol + _rowsum(jnp.where(mk["eye"], jnp.broadcast_to(d_grow, (tg, tg)), 0.0))
            dq_ref[rows, :] = dq
            dk_ref[rows, :] = dk
            dv_ref[rows, :] = dvb * bcol
            db_ref[rows, :] = jnp.where(lane == h, d_beta, db_ref[rows, :])
            dg_ref[rows, :] = jnp.where(lane == h, d_gcol, dg_ref[rows, :])

    row = lambda off: pl.BlockSpec((tb, HEAD), functools.partial(lambda m, h, off: (m, h + off), off=off))
    full = pl.BlockSpec((tb, LANES), lambda m, h: (m, 0))
    o_spec = pl.BlockSpec((tb, HEAD), lambda m, h: (m, h))
    a_spec = pl.BlockSpec((None, tb, CHUNK), lambda m, h: (h, m, 0))
    wide = jax.ShapeDtypeStruct((s_dim, N_HEADS * HEAD), F32)
    lanes = jax.ShapeDtypeStruct((s_dim, LANES), F32)
    return pl.pallas_call(
        body,
        out_shape=[wide, wide, wide, lanes, lanes],
        grid=(s_dim // tb, N_HEADS),
        in_specs=[row(0), row(N_HEADS), row(2 * N_HEADS), full, full, pl.BlockSpec((8, tb), lambda m, h: (0, m)),
                  a_spec, o_spec, o_spec, o_spec, o_spec, o_spec, o_spec, a_spec],
        out_specs=[o_spec, o_spec, o_spec, full, full],
        compiler_params=pltpu.CompilerParams(dimension_semantics=("parallel", "arbitrary")),
        name="gdr_prep_bwd",
    )(qkvn, qkvn, qkvn, beta, gc, gc_t, t_fold, u, w, du, dw, dqd, dkt, d_a)


def _gdr_scan_fwd(u, w, qd, kt, a_mat, gc):
    s_dim = u.shape[0]
    n_chunks = s_dim // CHUNK

    def body(u_ref, w_ref, qd_ref, kt_ref, a_ref, g_ref, o_ref, st_ref, state):
        @pl.when(pl.program_id(0) == 0)
        def _():
            state[...] = jnp.zeros_like(state)

        egl = jnp.exp(g_ref[CHUNK - 1:CHUNK, :])
        for h in range(N_HEADS):
            cs = slice(h * HEAD, (h + 1) * HEAD)
            s_h = state[h]
            st_ref[h] = s_h
            vn = u_ref[:, cs] - _dot(w_ref[:, cs], s_h)
            o_ref[:, cs] = _dot(qd_ref[:, cs], s_h) + _dot(a_ref[h], vn)
            state[h] = s_h * egl[:, h:h + 1] + _dot(kt_ref[:, cs], vn, TN)

    wide = pl.BlockSpec((CHUNK, N_HEADS * HEAD), lambda n: (n, 0))
    return pl.pallas_call(
        body,
        out_shape=[jax.ShapeDtypeStruct((s_dim, N_HEADS * HEAD), F32),
                   jax.ShapeDtypeStruct((n_chunks, N_HEADS, HEAD, HEAD), F32)],
        grid=(n_chunks,),
        in_specs=[wide, wide, wide, wide, pl.BlockSpec((N_HEADS, CHUNK, CHUNK), lambda n: (0, n, 0)),
                  pl.BlockSpec((CHUNK, LANES), lambda n: (n, 0))],
        out_specs=[wide, pl.BlockSpec((None, N_HEADS, HEAD, HEAD), lambda n: (n, 0, 0, 0))],
        scratch_shapes=[pltpu.VMEM((N_HEADS, HEAD, HEAD), F32)],
        compiler_params=pltpu.CompilerParams(dimension_semantics=("arbitrary",)),
        name="gdr_scan_fwd",
    )(u, w, qd, kt, a_mat, gc)


def _gdr_scan_bwd(u, w, qd, kt, a_mat, gc, states, d_o):
    s_dim = u.shape[0]
    n_chunks = s_dim // CHUNK
    last = n_chunks - 1

    def body(u_ref, w_ref, qd_ref, kt_ref, a_ref, g_ref, st_ref, do_ref,
             du_ref, dw_ref, dqd_ref, dkt_ref, da_ref, de_ref, d_state):
        @pl.when(pl.program_id(0) == 0)
        def _():
            d_state[...] = jnp.zeros_like(d_state)

        egl = jnp.exp(g_ref[CHUNK - 1:CHUNK, :])
        for h in range(N_HEADS):
            cs = slice(h * HEAD, (h + 1) * HEAD)
            s_h = st_ref[h]
            ds_n = d_state[h]
            do = do_ref[:, cs]
            w_h = w_ref[:, cs]
            vn = u_ref[:, cs] - _dot(w_h, s_h)
            dvn = _dot(a_ref[h], do, TN) + _dot(kt_ref[:, cs], ds_n)
            dqd_ref[:, cs] = _dot(do, s_h, NT)
            da_ref[h] = _dot(do, vn, NT)
            dkt_ref[:, cs] = _dot(vn, ds_n, NT)
            de = jnp.sum(_rowsum(ds_n * s_h), axis=0, keepdims=True)
            de_ref[h:h + 1, :] = jnp.broadcast_to(de, (1, LANES))
            du_ref[:, cs] = dvn
            dw_ref[:, cs] = -_dot(dvn, s_h, NT)
            d_state[h] = ds_n * egl[:, h:h + 1] + _dot(qd_ref[:, cs], do, TN) - _dot(w_h, dvn, TN)

    wide = pl.BlockSpec((CHUNK, N_HEADS * HEAD), lambda n: (last - n, 0))
    a_spec = pl.BlockSpec((N_HEADS, CHUNK, CHUNK), lambda n: (0, last - n, 0))
    wide_shape = jax.ShapeDtypeStruct((s_dim, N_HEADS * HEAD), F32)
    return pl.pallas_call(
        body,
        out_shape=[wide_shape, wide_shape, wide_shape, wide_shape,
                   jax.ShapeDtypeStruct((N_HEADS, s_dim, CHUNK), F32),
                   jax.ShapeDtypeStruct((n_chunks, N_HEADS, LANES), F32)],
        grid=(n_chunks,),
        in_specs=[wide, wide, wide, wide, a_spec, pl.BlockSpec((CHUNK, LANES), lambda n: (last - n, 0)),
                  pl.BlockSpec((None, N_HEADS, HEAD, HEAD), lambda n: (last - n, 0, 0, 0)), wide],
        out_specs=[wide, wide, wide, wide, a_spec, pl.BlockSpec((None, N_HEADS, LANES), lambda n: (last - n, 0, 0))],
        scratch_shapes=[pltpu.VMEM((N_HEADS, HEAD, HEAD), F32)],
        compiler_params=pltpu.CompilerParams(dimension_semantics=("arbitrary",)),
        name="gdr_scan_bwd",
    )(u, w, qd, kt, a_mat, gc, states, d_o)


def _gdr_out_fwd(o_dn, proj_a, dn_w):
    def fn(r, c):
        o, z = r
        (w_,) = c
        outs = []
        for h in range(N_HEADS):
            cs = slice(h * HEAD, (h + 1) * HEAD)
            oh, zh = o[:, cs], z[:, cs]
            rr = lax.rsqrt(_rowmean(oh * oh) + EPS_RMS)
            outs.append(oh * rr * w_ * (zh * _sig(zh)))
        return [jnp.concatenate(outs, axis=1)], []

    return _rowwise(fn, [o_dn, (proj_a, 3, D_MODEL)], [dn_w], [(D_MODEL, BF16)], name="gdr_out_fwd")[0]


def _gdr_out_bwd(o_dn, proj_a, d_og, dn_w):
    def fn(r, c):
        o, z, dg = r
        (w_,) = c
        d_o, d_z = [], []
        d_w = jnp.zeros((1, HEAD), F32)
        for h in range(N_HEADS):
            cs = slice(h * HEAD, (h + 1) * HEAD)
            oh, zh, dgh = o[:, cs], z[:, cs], dg[:, cs]
            rr = lax.rsqrt(_rowmean(oh * oh) + EPS_RMS)
            sz = zh * _sig(zh)
            d_n = dgh * sz
            d_z.append(dgh * (oh * rr * w_) * _silu_grad(zh))
            d_w = d_w + _colsum(d_n * oh * rr)
            gw = d_n * w_
            d_o.append(rr * gw - oh * (rr * rr * rr) * _rowmean(gw * oh))
        return [jnp.concatenate(d_o, axis=1), jnp.concatenate(d_z, axis=1)], [d_w]

    return _rowwise(fn, [o_dn, (proj_a, 3, D_MODEL), d_og], [dn_w], [(D_MODEL, F32), (D_MODEL, BF16)],
                    accs=[(1, HEAD)], name="gdr_out_bwd")


def _rms_fwd(x, w):
    r = lax.rsqrt(_rowmean(x * x) + EPS_RMS)
    return x * r * w


def _rms_bwd(x, w, dy):
    r = lax.rsqrt(_rowmean(x * x) + EPS_RMS)
    gw = dy * w
    return r * gw - x * (r * r * r) * _rowmean(gw * x), _colsum(dy * x * r)


def _mla_norm_fwd(proj_b, qn_w, kvn_w):
    def fn(r, c):
        return [_rms_fwd(r[0], c[0]), _rms_fwd(r[1], c[1])], []

    return _rowwise(fn, [(proj_b, WB_CQ // Q_LORA, Q_LORA), (proj_b, WB_CKV // KV_LORA, KV_LORA)], [qn_w, kvn_w],
                    [(Q_LORA, BF16), (KV_LORA, BF16)], name="mla_norm_fwd")


def _mla_norm_bwd(proj_b, qn_w, kvn_w, d_cq, d_ckv):
    def fn(r, c):
        dx1, dw1 = _rms_bwd(r[0], c[0], r[2])
        dx2, dw2 = _rms_bwd(r[1], c[1], r[3])
        return [dx1, dx2], [dw1, dw2]

    return _rowwise(fn, [(proj_b, WB_CQ // Q_LORA, Q_LORA), (proj_b, WB_CKV // KV_LORA, KV_LORA), d_cq, d_ckv],
                    [qn_w, kvn_w], [(Q_LORA, BF16), (KV_LORA, BF16)], accs=[(1, Q_LORA), (1, KV_LORA)],
                    name="mla_norm_bwd")


def _rope_consts():
    inv = ROPE_BASE ** (-np.arange(0, ROPE, 2, dtype=np.float32) / ROPE)
    t = np.zeros((4, LANES), np.float32)
    t[0, :32] = inv
    t[0, 32:64] = inv
    t[1, :64] = 1.0
    t[2, 32:64] = 1.0
    t[3, :32] = -1.0
    return jnp.asarray(t)


def _rope_tables(pos, consts, width):
    ang = pos * consts[0:1, :]
    cosv, sinv = jnp.cos(ang), jnp.sin(ang)
    reps = width // LANES
    tile = (lambda t: jnp.concatenate([t] * reps, axis=1)) if reps > 1 else (lambda t: t)
    return tile(cosv * consts[1:2, :]), tile(sinv * consts[2:3, :]), tile(sinv * consts[3:4, :])


def _rope_apply(t, tabs):
    cos_t, sin_a, sin_b = tabs
    width = t.shape[1]
    return t * cos_t + pltpu.roll(t, 32, 1) * sin_a + pltpu.roll(t, width - 32, 1) * sin_b


def _rope_transpose(d, tabs):
    cos_t, sin_a, sin_b = tabs
    width = d.shape[1]
    return d * cos_t + pltpu.roll(d * sin_a, width - 32, 1) + pltpu.roll(d * sin_b, 32, 1)


QK_HEAD = 2 * HEAD


def _interleave_heads(a, b):
    parts = []
    for h in range(N_HEADS):
        parts.append(a[:, h * HEAD:(h + 1) * HEAD])
        parts.append(b if b.shape[1] == LANES else b[:, h * LANES:(h + 1) * LANES])
    return jnp.concatenate(parts, axis=1)


def _mla_qk_fwd(q_full, k_nope, proj_b, pos):
    consts = _rope_consts()

    def fn(r, c):
        qf, kn, kr, pos_ = r
        qn, qr = qf[:, :D_MODEL], qf[:, D_MODEL:]
        qr = _rope_apply(qr, _rope_tables(pos_, c[0], D_MODEL))
        kr = _rope_apply(kr, _rope_tables(pos_, c[0], LANES))
        return [_interleave_heads(qn, qr) * SCALE, _interleave_heads(kn, kr)], []

    return _rowwise(fn, [q_full, k_nope, (proj_b, WB_KR // LANES, LANES), pos], [consts],
                    [(N_HEADS * QK_HEAD, BF16), (N_HEADS * QK_HEAD, BF16)], name="mla_qk_fwd")


def _mla_qk_bwd(d_qc, d_kc, pos):
    consts = _rope_consts()

    def fn(r, c):
        dq, dk, pos_ = r
        even = lambda t: jnp.concatenate([t[:, (2 * h) * LANES:(2 * h + 1) * LANES] for h in range(N_HEADS)], axis=1)
        odd = lambda t: jnp.concatenate([t[:, (2 * h + 1) * LANES:(2 * h + 2) * LANES] for h in range(N_HEADS)], axis=1)
        d_qr_raw = _rope_transpose(odd(dq), _rope_tables(pos_, c[0], D_MODEL)) * SCALE
        dkr = dk[:, LANES:2 * LANES]
        for h in range(1, N_HEADS):
            dkr = dkr + dk[:, (2 * h + 1) * LANES:(2 * h + 2) * LANES]
        return [jnp.concatenate([even(dq) * SCALE, d_qr_raw], axis=1), even(dk),
                _rope_transpose(dkr, _rope_tables(pos_, c[0], LANES))], []

    return _rowwise(fn, [d_qc, d_kc, pos], [consts], [(2 * D_MODEL, BF16), (D_MODEL, BF16), (LANES, BF16)],
                    name="mla_qk_bwd")


def _causal_mask_t(st, key0, query0):
    key = lax.broadcasted_iota(jnp.int32, st.shape, 0) + key0
    query = lax.broadcasted_iota(jnp.int32, st.shape, 1) + query0
    return jnp.where(key <= query, st, NEG_BIG)


def _attn_tiles(s_dim):
    tq = min(512, s_dim)
    n_chains = 2 if s_dim >= 2 * tq else 1
    return tq, n_chains, min(512, s_dim)


def _diagonal_chains(t, tq, n_chains, tk):
    return [(c, (t + 1) * tk - 1 > c * tq) for c in range(n_chains) if t * tk < (c + 1) * tq]


def _attn_fwd(qc, kc, vt):
    s_dim = qc.shape[0]
    tq, n_chains, tk = _attn_tiles(s_dim)
    tqs = tq * n_chains

    def body(q_ref, k_ref, vt_ref, o_ref, lse_ref, m_s, l_s, acc):
        qi = pl.program_id(1)
        m_s[...] = jnp.full_like(m_s, NEG_BIG)
        l_s[...] = jnp.zeros_like(l_s)
        acc[...] = jnp.zeros_like(acc)

        def make_step(chains):
            def step(j, carry):
                ks = pl.multiple_of(j * tk, tk)
                kb, vtb = k_ref[pl.ds(ks, tk), :], vt_ref[:, pl.ds(ks, tk)]
                for c, masked in chains:
                    cols = slice(c * tq, (c + 1) * tq)
                    st = _dot(kb, q_ref[cols, :], NT)
                    if masked:
                        st = _causal_mask_t(st, j * tk, qi * tqs + c * tq)
                    m_prev = m_s[:, cols]
                    m_new = jnp.maximum(m_prev, jnp.max(st, axis=0, keepdims=True))
                    alpha = jnp.exp(m_prev - m_new)
                    pt = jnp.exp(st - m_new)
                    l_s[:, cols] = alpha * l_s[:, cols] + _colsum(pt)
                    m_s[:, cols] = m_new
                    acc[:, cols] = acc[:, cols] * alpha + _dot(vtb, pt)
                return carry
            return step

        below = qi * (tqs // tk)
        lax.fori_loop(0, below, make_step([(c, False) for c in range(n_chains)]), 0)
        for t in range(tqs // tk):
            make_step(_diagonal_chains(t, tq, n_chains, tk))(below + t, 0)
        l = l_s[...]
        o_ref[...] = jnp.transpose(acc[...] / l)
        lse_ref[...] = m_s[...] + jnp.log(l)

    return pl.pallas_call(
        body,
        out_shape=[jax.ShapeDtypeStruct((s_dim, N_HEADS * HEAD), F32), jax.ShapeDtypeStruct((N_HEADS, 1, s_dim), F32)],
        grid=(N_HEADS, s_dim // tqs),
        in_specs=[pl.BlockSpec((tqs, QK_HEAD), lambda h, qi: (qi, h)),
                  pl.BlockSpec((s_dim, QK_HEAD), lambda h, qi: (0, h)),
                  pl.BlockSpec((HEAD, s_dim), lambda h, qi: (h, 0))],
        out_specs=[pl.BlockSpec((tqs, HEAD), lambda h, qi: (qi, h)),
                   pl.BlockSpec((None, 1, tqs), lambda h, qi: (h, 0, qi))],
        scratch_shapes=[pltpu.VMEM((1, tqs), F32), pltpu.VMEM((1, tqs), F32), pltpu.VMEM((HEAD, tqs), F32)],
        compiler_params=pltpu.CompilerParams(dimension_semantics=("parallel", "parallel")),
        name="attn_fwd",
    )(qc, kc, vt)


def _attn_bwd(qc, kc, kct, v, o, d_o, lse):
    s_dim = qc.shape[0]
    tq, n_chains, tk = _attn_tiles(s_dim)
    tqs = tq * n_chains

    def body(q_ref, k_ref, kt_ref, v_ref, o_ref, do_ref, lse_ref, dq_ref, dk_ref, dv_ref, dqt_acc, dv_acc):
        qi = pl.program_id(1)

        @pl.when(qi == 0)
        def _():
            dk_ref[...] = jnp.zeros_like(dk_ref)
            dv_acc[...] = jnp.zeros_like(dv_acc)

        dqt_acc[...] = jnp.zeros_like(dqt_acc)
        do_f = do_ref[...]
        do_all = do_f.astype(BF16)
        q_all = q_ref[...]
        lse_row = lse_ref[...]
        delta_row = _dot3(jnp.ones((8, HEAD), F32), o_ref[...] * do_f, NT)[0:1, :]

        def make_step(chains):
            rows = slice(chains[0][0] * tq, (chains[-1][0] + 1) * tq)

            def step(j, carry):
                ks = pl.multiple_of(j * tk, tk)
                kb, vb, ktb = k_ref[pl.ds(ks, tk), :], v_ref[pl.ds(ks, tk), :], kt_ref[:, pl.ds(ks, tk)]
                pts, dsts = [], []
                for c, masked in chains:
                    cols = slice(c * tq, (c + 1) * tq)
                    st = _dot(kb, q_all[cols, :], NT)
                    if masked:
                        st = _causal_mask_t(st, j * tk, qi * tqs + c * tq)
                    pt = jnp.exp(st - lse_row[:, cols])
                    dst = pt * (_dot(vb, do_all[cols, :], NT) - delta_row[:, cols])
                    dst_b = dst.astype(BF16)
                    dqt_acc[:, cols] += _dot(ktb, dst_b)
                    pts.append(pt.astype(BF16))
                    dsts.append(dst_b)
                pt_all = jnp.concatenate(pts, axis=1) if len(chains) > 1 else pts[0]
                dst_all = jnp.concatenate(dsts, axis=1) if len(chains) > 1 else dsts[0]
                dk_ref[pl.ds(ks, tk), :] += _dot(dst_all, q_all[rows, :])
                dv_acc[pl.ds(ks, tk), :] += _dot(pt_all, do_all[rows, :])
                return carry
            return step

        below = qi * (tqs // tk)
        lax.fori_loop(0, below, make_step([(c, False) for c in range(n_chains)]), 0)
        for t in range(tqs // tk):
            make_step(_diagonal_chains(t, tq, n_chains, tk))(below + t, 0)
        dq_ref[...] = jnp.transpose(dqt_acc[...])

        @pl.when(qi == s_dim // tqs - 1)
        def _():
            dv_ref[...] = dv_acc[...].astype(dv_ref.dtype)

    q_spec = pl.BlockSpec((tqs, QK_HEAD), lambda h, qi: (qi, h))
    o_spec = pl.BlockSpec((tqs, HEAD), lambda h, qi: (qi, h))
    k_spec = pl.BlockSpec((s_dim, QK_HEAD), lambda h, qi: (0, h))
    v_spec = pl.BlockSpec((s_dim, HEAD), lambda h, qi: (0, h))
    wide2 = jax.ShapeDtypeStruct((s_dim, N_HEADS * QK_HEAD), F32)
    return pl.pallas_call(
        body,
        out_shape=[wide2, wide2, jax.ShapeDtypeStruct((s_dim, N_HEADS * HEAD), BF16)],
        grid=(N_HEADS, s_dim // tqs),
        in_specs=[q_spec, k_spec, pl.BlockSpec((QK_HEAD, s_dim), lambda h, qi: (h, 0)), v_spec, o_spec, o_spec,
                  pl.BlockSpec((None, 1, tqs), lambda h, qi: (h, 0, qi))],
        out_specs=[q_spec, k_spec, v_spec],
        scratch_shapes=[pltpu.VMEM((QK_HEAD, tqs), F32), pltpu.VMEM((s_dim, HEAD), F32)],
        compiler_params=pltpu.CompilerParams(dimension_semantics=("parallel", "arbitrary")),
        name="attn_bwd",
    )(qc, kc, kct, v, o, d_o, lse)


def _merge_fwd(y_dn, y_mla, proj_g):
    def fn(r, c):
        yd, ym, g = r
        return [_sig(g[:, :D_MODEL]) * yd + _sig(g[:, D_MODEL:]) * ym], []

    return _rowwise(fn, [y_dn, y_mla, proj_g], [], [(D_MODEL, BF16)], name="merge_fwd")[0]


def _merge_bwd(y_dn, y_mla, proj_g, d_mixed):
    def fn(r, c):
        yd, ym, g, dm = r
        sd, sm = _sig(g[:, :D_MODEL]), _sig(g[:, D_MODEL:])
        d_g = jnp.concatenate([dm * yd * sd * (1.0 - sd), dm * ym * sm * (1.0 - sm)], axis=1)
        return [d_g, dm * sd, dm * sm], []

    return _rowwise(fn, [y_dn, y_mla, proj_g, d_mixed], [], [(2 * D_MODEL, BF16), (D_MODEL, BF16), (D_MODEL, BF16)],
                    name="merge_bwd")


def _ln_stats(z):
    mu = _rowmean(z)
    zc = z - mu
    r = lax.rsqrt(_rowmean(zc * zc) + EPS_LN)
    return zc * r, r


def _ln_bwd(dy, xh, r, g):
    dxh = dy * g
    return r * (dxh - _rowmean(dxh) - xh * _rowmean(dxh * xh))


def _ln1_fwd(x, a1, g, b):
    def fn(r, c):
        xh, _ = _ln_stats(ALPHA * r[0] + r[1])
        y = xh * c[0] + c[1]
        return [y, y], []

    return _rowwise(fn, [x, a1], [g, b], [(D_MODEL, F32), (D_MODEL, BF16)], name="ln1_fwd")


def _ln1_bwd(x, a1, d_h1, g):
    def fn(r, c):
        xh, rr = _ln_stats(ALPHA * r[0] + r[1])
        dy = r[2]
        dz = _ln_bwd(dy, xh, rr, c[0])
        return [dz, ALPHA * dz], [_colsum(dy * xh), _colsum(dy)]

    return _rowwise(fn, [x, a1, d_h1], [g], [(D_MODEL, BF16), (D_MODEL, F32)], accs=[(1, D_MODEL), (1, D_MODEL)],
                    name="ln1_bwd")


def _act_fwd(gu):
    def fn(r, c):
        gt, up = r[0][:, :FFN_HIDDEN], r[0][:, FFN_HIDDEN:]
        return [gt * _sig(gt) * up], []

    return _rowwise(fn, [gu], [], [(FFN_HIDDEN, BF16)], name="act_fwd")[0]


def _act_bwd(gu, d_act):
    def fn(r, c):
        gt, up = r[0][:, :FFN_HIDDEN], r[0][:, FFN_HIDDEN:]
        da = r[1]
        return [jnp.concatenate([da * up * _silu_grad(gt), da * gt * _sig(gt)], axis=1)], []

    return _rowwise(fn, [gu, d_act], [], [(2 * FFN_HIDDEN, BF16)], name="act_bwd")[0]


def _tail(h1, ffn, pg, pp, tgt, g, b):
    def fn(r, c):
        h1_, ffn_, pg_, pp_, t_ = r
        sp = _sig(pg_)
        xh, rr = _ln_stats(ALPHA * h1_ + ffn_ + sp * pp_)
        y = xh * c[0] + c[1]
        err = y - t_
        dy = err * (1.0 / D_MODEL)
        dz = _ln_bwd(dy, xh, rr, c[0])
        loss = jnp.sum(0.5 * _rowmean(err * err), axis=0, keepdims=True)
        return ([dz, dz * pp_ * sp * (1.0 - sp), dz * sp, ALPHA * dz],
                [_colsum(dy * xh), _colsum(dy), jnp.broadcast_to(loss, (1, LANES))])

    return _rowwise(fn, [h1, ffn, pg, pp, tgt], [g, b], [(D_MODEL, BF16)] * 3 + [(D_MODEL, F32)],
                    accs=[(1, D_MODEL), (1, D_MODEL), (1, LANES)], name="tail")


def _local_step(x, p, pos, tgt, w, late_weights, emit):
    w = dict(w)
    s_dim = x.shape[0]
    xb, pb = x.astype(BF16), p.astype(BF16)
    proj_a = _mm(xb, w["wa"], name="f_proj_a")
    proj_g = _mm(xb, w["wg"], name="f_proj_g")
    proj_b = _mm(xb, w["wb"], name="f_proj_b")
    qkvn = _conv_fwd(proj_a, w["conv"])
    beta, gc = _gates_fwd(proj_b, w["alog"], w["dtb"])
    gc_t = jnp.transpose(gc[:, :N_HEADS])
    u, w_, qd, kt, a_mat, t_fold = _gdr_prep_fwd(qkvn, beta, gc, gc_t)
    o_dn, states = _gdr_scan_fwd(u, w_, qd, kt, a_mat, gc)
    og = _gdr_out_fwd(o_dn, proj_a, w["dnw"])
    w.update(late_weights("mix", og))
    y_dn = _mm(og, w["br_dn"], name="f_y_dn")
    c_q, c_kv = _mla_norm_fwd(proj_b, w["qnw"], w["kvnw"])
    q_full = _mm(c_q, w["uq"], name="f_q_full")
    k_nope = _mm(c_kv, w["uk"], name="f_k_nope")
    vv = _mm(c_kv, w["uv"], out_dtype=BF16, name="f_v")
    qc, kc = _mla_qk_fwd(q_full, k_nope, proj_b, pos)
    o_mla, lse = _attn_fwd(qc, kc, jnp.transpose(vv))
    y_mla = _mm(o_mla, w["br_mla"], name="f_y_mla")
    mixed = _merge_fwd(y_dn, y_mla, proj_g)
    a1 = _mm(mixed, w["wo"], name="f_a1")
    w.update(late_weights("ffn", a1))
    h1, h1b = _ln1_fwd(x, a1, w["ln1g"], w["ln1b"])
    gu = _mm(h1b, w["ffn_in"], name="f_gu")
    act = _act_fwd(gu)
    ffn = _mm(act, w["ffn_out"], name="f_ffn")
    pg = _mm(h1b, w["ple_gate"], name="f_pg")
    pp = _mm(pb, w["ple"], name="f_pp")
    g = {}
    dz2, d_pg, d_pp, dh1a, g["ln2g"], g["ln2b"], loss = _tail(h1, ffn, pg, pp, tgt, w["ln2g"], w["ln2b"])
    g["ple_t"] = _mm(d_pp, pb, ta=True, out_dtype=BF16, name="b_w_ple")
    g["ple_gate"] = _mm(h1b, d_pg, ta=True, out_dtype=BF16, name="b_w_ple_gate")
    g["ffn_out"] = _mm(act, dz2, ta=True, out_dtype=BF16, name="b_w_ffn_out")
    d_act = _mm(dz2, w["ffn_out"], tb=True, name="b_act")
    d_gu = _act_bwd(gu, d_act)
    g["ffn_in_t"] = _mm(d_gu, h1b, ta=True, out_dtype=BF16, name="b_w_ffn_in")
    d_gu = emit("ffn", g, d_gu)
    d_h1 = _mm(d_gu, w["ffn_in_t"], add=(dh1a,), name="b_h1_ffn")
    d_h1 = _mm(d_pg, w["ple_gate"], tb=True, add=(d_h1,), name="b_h1_ple")
    dz1, dxa, g["ln1g"], g["ln1b"] = _ln1_bwd(x, a1, d_h1, w["ln1g"])
    g["wo"] = _mm(mixed, dz1, ta=True, out_dtype=BF16, name="b_w_o")
    d_mixed = _mm(dz1, w["wo"], tb=True, name="b_mixed")
    d_proj_g, d_y_dn, d_y_mla = _merge_bwd(y_dn, y_mla, proj_g, d_mixed)
    g["br_mla"] = _mm(o_mla, d_y_mla, ta=True, out_dtype=BF16, name="b_w_br_mla")
    d_o_mla = _mm(d_y_mla, w["br_mla"], tb=True, name="b_o_mla")
    d_qc, d_kc, d_v = _attn_bwd(qc, kc, jnp.transpose(kc), vv, o_mla, d_o_mla, lse)
    d_q_full, d_kn, d_kr = _mla_qk_bwd(d_qc, d_kc, pos)
    g["uq"] = _mm(c_q, d_q_full, ta=True, out_dtype=BF16, name="b_w_uq")
    d_c_q = _mm(d_q_full, w["uq"], tb=True, name="b_c_q")
    g["uk"] = _mm(c_kv, d_kn, ta=True, out_dtype=BF16, name="b_w_uk")
    g["uv"] = _mm(c_kv, d_v, ta=True, out_dtype=BF16, name="b_w_uv")
    d_c_kv = _mm(d_kn, w["uk"], tb=True, name="b_c_kv_k")
    d_c_kv = _mm(d_v, w["uv"], tb=True, add=(d_c_kv,), name="b_c_kv_v")
    d_cq, d_ckv, g["qnw"], g["kvnw"] = _mla_norm_bwd(proj_b, w["qnw"], w["kvnw"], d_c_q, d_c_kv)
    g["br_dn"] = _mm(og, d_y_dn, ta=True, out_dtype=BF16, name="b_w_br_dn")
    d_og = emit("mix", g, _mm(d_y_dn, w["br_dn"], tb=True, name="b_og"))
    d_o_dn, d_z, g["dnw"] = _gdr_out_bwd(o_dn, proj_a, d_og, w["dnw"])
    du, dw, dqd, dkt, d_a, d_egl = _gdr_scan_bwd(u, w_, qd, kt, a_mat, gc, states, d_o_dn)
    dq, dk, dv, d_beta, d_gc = _gdr_prep_bwd(qkvn, beta, gc, gc_t, t_fold, u, w_, du, dw, dqd, dkt, d_a)
    d_egl_rows = jnp.pad(d_egl[:, None, :, 0], ((0, 0), (CHUNK - 1, 0), (0, LANES - N_HEADS))).reshape(s_dim, LANES)
    d_ba, g["alog"], g["dtb"] = _gates_bwd(proj_b, w["alog"], w["dtb"], gc, d_beta, d_gc, d_egl_rows)
    d_qkv, g["conv"] = _conv_bwd(proj_a, w["conv"], dq, dk, dv)
    zeros = jnp.zeros((s_dim, WB_CKV - Q_LORA), BF16)
    d_proj_b = jnp.concatenate([d_cq, zeros, d_ckv, d_kr, d_ba], axis=1)
    g["wa_qkv_t"] = _mm(d_qkv, xb, ta=True, name="b_w_qkv")
    g["wa_z_t"] = _mm(d_z, xb, ta=True, name="b_w_z")
    g["wg_t"] = _mm(d_proj_g, xb, ta=True, name="b_w_g")
    g["wb_t"] = _mm(d_proj_b, xb, ta=True, name="b_w_b")
    dx = _mm(d_qkv, w["wa_qkv_t"], add=(dxa,), name="b_x_qkv")
    dx = _mm(d_z, w["wa_z_t"], add=(dx,), name="b_x_z")
    dx = _mm(d_proj_g, w["wg_t"], add=(dx,), name="b_x_g")
    dx = _mm(d_proj_b, w["wb_t"], add=(dx,), name="b_x_b")
    return loss, dx, g


_BIG = (("w_in", 1), ("w_uq", 0), ("w_uk", 0), ("w_uv", 0), ("w_br_dn", 0), ("w_br_mla", 0),
        ("w_o", 0), ("w_ffn_in", 1), ("w_ffn_out", 0), ("w_ple", 1), ("w_ple_gate", 0))
_BIG_AXIS = dict(_BIG)
_SMALL = ("ln1_g", "ln1_b", "ln2_g", "ln2_b", "q_norm_w", "kv_norm_w", "dn_norm_w", "dn_a_log", "dn_dt_bias")
_ORDER = ("w_in", "conv_w", "dn_a_log", "dn_dt_bias", "dn_norm_w", "q_norm_w", "w_uq", "kv_norm_w", "w_uk", "w_uv",
          "w_br_dn", "w_br_mla", "w_o", "ln1_g", "ln1_b", "w_ffn_in", "w_ffn_out", "w_ple", "w_ple_gate", "ln2_g",
          "ln2_b")


def _stored_shape(name, shard_shape):
    axis = _BIG_AXIS[name]
    lead = shard_shape[axis]
    return lead, int(np.prod(shard_shape)) // lead


def _to_stored(name, shard):
    return jnp.moveaxis(shard, _BIG_AXIS[name], 0).reshape(_stored_shape(name, shard.shape))


def _from_stored(name, stored, shard_shape):
    axis = _BIG_AXIS[name]
    moved = (shard_shape[axis],) + shard_shape[:axis] + shard_shape[axis + 1:]
    return jnp.moveaxis(stored.reshape(moved), 0, axis)


_W_IN_ROWS = np.cumsum([0, 3072, 1024, 8, 8, Q_LORA, KV_LORA, ROPE, D_MODEL, D_MODEL])


def _first_weights(w_in_t, conv_full, small):
    r = _W_IN_ROWS
    zr = lambda n: jnp.zeros((n, D_MODEL), w_in_t.dtype)
    w = {}
    w["wa_t"] = w_in_t[r[0]:r[2]]
    w["wa_qkv_t"], w["wa_z_t"] = w_in_t[r[0]:r[1]], w_in_t[r[1]:r[2]]
    w["wg_t"] = w_in_t[r[7]:r[9]]
    w["wb_t"] = jnp.concatenate([w_in_t[r[4]:r[5]], zr(WB_CKV - Q_LORA), w_in_t[r[5]:r[7]], zr(LANES - ROPE),
                                 w_in_t[r[2]:r[4]], zr(LANES - 2 * N_HEADS)], axis=0)
    for k_ in ("wa", "wg", "wb"):
        w[k_] = jnp.transpose(w[k_ + "_t"])
    w["conv"] = conv_full
    pad_l = lambda v: jnp.pad(v, ((0, 0), (0, LANES - v.shape[1])))
    w["alog"], w["dtb"] = pad_l(small["dn_a_log"]), pad_l(small["dn_dt_bias"])
    w["dnw"], w["qnw"], w["kvnw"] = small["dn_norm_w"], small["q_norm_w"], small["kv_norm_w"]
    w["ln1g"], w["ln1b"], w["ln2g"], w["ln2b"] = small["ln1_g"], small["ln1_b"], small["ln2_g"], small["ln2_b"]
    return w


def _late_weights(group, fw):
    w = {}
    if group == "mix":
        uq = fw["w_uq"].reshape(Q_LORA, N_HEADS, HEAD + ROPE)
        uq_r = jnp.pad(uq[:, :, HEAD:], ((0, 0), (0, 0), (0, HEAD - ROPE)))
        w["uq"] = jnp.concatenate([uq[:, :, :HEAD].reshape(Q_LORA, -1), uq_r.reshape(Q_LORA, -1)], axis=1)
        w["uk"], w["uv"] = fw["w_uk"], fw["w_uv"]
        w["br_dn"], w["br_mla"], w["wo"] = fw["w_br_dn"], fw["w_br_mla"], fw["w_o"]
    else:
        w["ffn_in_t"], w["ffn_out"] = fw["w_ffn_in"], fw["w_ffn_out"]
        w["ple_t"], w["ple_gate"] = fw["w_ple"], fw["w_ple_gate"]
        for k_ in ("ffn_in", "ple"):
            w[k_] = jnp.transpose(w[k_ + "_t"])
    return w


_GROUP_GRADS = {"ffn": (("w_ple", "ple_t"), ("w_ple_gate", "ple_gate"), ("w_ffn_out", "ffn_out"),
                        ("w_ffn_in", "ffn_in_t")),
                "mix": (("w_o", "wo"), ("w_br_mla", "br_mla"), ("w_uq", "uq"), ("w_uk", "uk"), ("w_uv", "uv"),
                        ("w_br_dn", "br_dn"))}


def _group_grads(group, g):
    out = {}
    for name, key in _GROUP_GRADS[group]:
        t = g[key]
        if name == "w_uq":
            uq_n = t[:, :D_MODEL].reshape(Q_LORA, N_HEADS, HEAD)
            uq_r = t[:, D_MODEL:].reshape(Q_LORA, N_HEADS, HEAD)[:, :, :ROPE]
            t = jnp.concatenate([uq_n, uq_r], axis=2).reshape(Q_LORA, -1)
        out[name] = t
    return out


def _last_grads(g):
    wb = g["wb_t"]
    w_in = jnp.concatenate([
        g["wa_qkv_t"], g["wa_z_t"], wb[WB_BA:WB_BA + 2 * N_HEADS], wb[WB_CQ:WB_CQ + Q_LORA],
        wb[WB_CKV:WB_CKV + KV_LORA], wb[WB_KR:WB_KR + ROPE], g["wg_t"]], axis=0)
    small = {"ln1_g": g["ln1g"], "ln1_b": g["ln1b"], "ln2_g": g["ln2g"], "ln2_b": g["ln2b"], "q_norm_w": g["qnw"],
             "kv_norm_w": g["kvnw"], "dn_norm_w": g["dnw"], "dn_a_log": g["alog"], "dn_dt_bias": g["dtb"],
             "conv_w": g["conv"]}
    return w_in, small


_SMALL_SLOTS = {"ln1_g": (0, 0, 1024), "ln1_b": (1, 0, 1024), "ln2_g": (2, 0, 1024), "ln2_b": (3, 0, 1024),
                "q_norm_w": (4, 0, 384), "kv_norm_w": (4, 384, 256), "dn_norm_w": (4, 640, 128),
                "dn_a_log": (4, 768, 8), "dn_dt_bias": (4, 896, 8)}
_SMALL_ROWS, _LOSS_ROW, _CONV_ROW0, _CONV_ROWS = 24, 5, 8, 12


def _pack_small_grads(small_g, loss):
    zeros = lambda r, c: jnp.zeros((r, c), F32)
    row4 = jnp.concatenate([small_g["q_norm_w"], small_g["kv_norm_w"], small_g["dn_norm_w"], small_g["dn_a_log"],
                            small_g["dn_dt_bias"]], axis=1)
    row5 = jnp.concatenate([loss, zeros(1, FLAT_COLS - LANES)], axis=1)
    head = jnp.concatenate([small_g["ln1_g"], small_g["ln1_b"], small_g["ln2_g"], small_g["ln2_b"], row4, row5,
                            zeros(2, FLAT_COLS)], axis=0)
    conv = small_g["conv_w"].reshape(_CONV_ROWS, FLAT_COLS)
    return jnp.concatenate([head, conv, zeros(_SMALL_ROWS - _CONV_ROW0 - _CONV_ROWS, FLAT_COLS)], axis=0)


_MESH_ID = pl.DeviceIdType.MESH
_ANY = pl.BlockSpec(memory_space=pl.ANY)


def _all_gather(blocks, name):
    n = len(blocks)

    def body(*refs):
        x_refs, out_refs = refs[:n], refs[n:2 * n]
        send_sems, recv_sems, local_sems = refs[2 * n:]
        x, y, c = lax.axis_index("x"), lax.axis_index("y"), lax.axis_index("c")
        me, sibling = (x, y, c), (x, y, 1 - c)
        chips = [(1 - x, y), (x, 1 - y), (1 - x, 1 - y)]

        def slot(i, px, py, pc):
            return out_refs[i].at[4 * px + 2 * py + pc]

        def copy(i, k, origin, to, src=None):
            return pltpu.make_async_remote_copy(
                src_ref=slot(i, *origin) if src is None else src, dst_ref=slot(i, *origin),
                send_sem=send_sems.at[7 * i + k], recv_sem=recv_sems.at[7 * i + k], device_id=to,
                device_id_type=_MESH_ID)

        mine = [pltpu.make_async_copy(x_refs[i], slot(i, *me), local_sems.at[i]) for i in range(n)]
        first, passed = [], []
        for i in range(n):
            mine[i].start()
            first.append(copy(i, 0, me, sibling, src=x_refs[i]))
            first += [copy(i, 1 + j, me, (*chip, c), src=x_refs[i]) for j, chip in enumerate(chips)]
        for cp in first:
            cp.start()
        for i in range(n):
            for j, chip in enumerate(chips):
                copy(i, 1 + j, (*chip, c), me).wait_recv()
                passed.append(copy(i, 4 + j, (*chip, c), sibling))
                passed[-1].start()
        for i in range(n):
            copy(i, 0, sibling, me).wait_recv()
            for j, chip in enumerate(chips):
                copy(i, 4 + j, (*chip, 1 - c), me).wait_recv()
        for cp in first + passed:
            cp.wait_send()
        for cp in mine:
            cp.wait()

    return pl.pallas_call(
        body,
        out_shape=[jax.ShapeDtypeStruct((N_DEV,) + b.shape, b.dtype) for b in blocks],
        in_specs=[_ANY] * n,
        out_specs=[_ANY] * n,
        scratch_shapes=[pltpu.SemaphoreType.DMA((7 * n,)), pltpu.SemaphoreType.DMA((7 * n,)),
                        pltpu.SemaphoreType.DMA((n,))],
        name=name,
    )(*blocks)


def _exchange_sibling(srcs, name):
    n = len(srcs)

    def body(*refs):
        src_refs, dst_refs = refs[:n], refs[n:2 * n]
        send_sems, recv_sems = refs[2 * n:]
        x, y, c = lax.axis_index("x"), lax.axis_index("y"), lax.axis_index("c")
        copies = [pltpu.make_async_remote_copy(
            src_ref=src_refs[i].at[2 * q + (1 - c)], dst_ref=dst_refs[i].at[q], send_sem=send_sems.at[4 * i + q],
            recv_sem=recv_sems.at[4 * i + q], device_id=(x, y, 1 - c), device_id_type=_MESH_ID)
            for i in range(n) for q in range(4)]
        for cp in copies:
            cp.start()
        for cp in copies:
            cp.wait_recv()
        for cp in copies:
            cp.wait_send()

    return pl.pallas_call(
        body,
        out_shape=[jax.ShapeDtypeStruct((4,) + s.shape[1:], s.dtype) for s in srcs],
        in_specs=[_ANY] * n,
        out_specs=[_ANY] * n,
        scratch_shapes=[pltpu.SemaphoreType.DMA((4 * n,)), pltpu.SemaphoreType.DMA((4 * n,))],
        name=name,
    )(*srcs)


def _exchange_chips(srcs, name):
    n = len(srcs)

    def body(*refs):
        src_refs, dst_refs = refs[:n], refs[n:2 * n]
        send_sems, recv_sems = refs[2 * n:]
        x, y, c = lax.axis_index("x"), lax.axis_index("y"), lax.axis_index("c")
        chips = [(1 - x, y), (x, 1 - y), (1 - x, 1 - y)]
        copies = [pltpu.make_async_remote_copy(
            src_ref=src_refs[i].at[2 * tx + ty], dst_ref=dst_refs[i].at[j], send_sem=send_sems.at[3 * i + j],
            recv_sem=recv_sems.at[3 * i + j], device_id=(tx, ty, c), device_id_type=_MESH_ID)
            for i in range(n) for j, (tx, ty) in enumerate(chips)]
        for cp in copies:
            cp.start()
        for cp in copies:
            cp.wait_recv()
        for cp in copies:
            cp.wait_send()

    return pl.pallas_call(
        body,
        out_shape=[jax.ShapeDtypeStruct((3,) + s.shape[1:], s.dtype) for s in srcs],
        in_specs=[_ANY] * n,
        out_specs=[_ANY] * n,
        scratch_shapes=[pltpu.SemaphoreType.DMA((3 * n,)), pltpu.SemaphoreType.DMA((3 * n,))],
        name=name,
    )(*srcs)


def _col_tile(c):
    return c if c <= 256 else 256


def _chip_sum(src, recv, parity, name):
    _, r, c = src.shape
    tc = _col_tile(c)

    def body(par_ref, a_ref, b_ref, o_ref, ob_ref):
        s = a_ref[...] + b_ref[...]
        o_ref[...] = s
        ob_ref[...] = s.astype(BF16)

    blk = lambda f: pl.BlockSpec((None, r, tc), f)
    return pl.pallas_call(
        body,
        out_shape=[jax.ShapeDtypeStruct((4, r, c), F32), jax.ShapeDtypeStruct((4, r, c), BF16)],
        grid_spec=pltpu.PrefetchScalarGridSpec(
            num_scalar_prefetch=1, grid=(4, c // tc),
            in_specs=[blk(lambda q, j, par: (2 * q + par[0], 0, j)), blk(lambda q, j, par: (q, 0, j))],
            out_specs=[blk(lambda q, j, par: (q, 0, j)), blk(lambda q, j, par: (q, 0, j))]),
        compiler_params=pltpu.CompilerParams(dimension_semantics=("parallel", "parallel")),
        name=name,
    )(parity, src, recv)


def _sum_parts(own, others, chip, name):
    _, r, c = own.shape
    tc = _col_tile(c)

    def body(q_ref, a_ref, b_ref, o_ref):
        o_ref[...] = ((a_ref[...] + b_ref[0].astype(F32)) + b_ref[1].astype(F32)) + b_ref[2].astype(F32)

    return pl.pallas_call(
        body,
        out_shape=jax.ShapeDtypeStruct((r, c), F32),
        grid_spec=pltpu.PrefetchScalarGridSpec(
            num_scalar_prefetch=1, grid=(c // tc,),
            in_specs=[pl.BlockSpec((None, r, tc), lambda j, q: (q[0], 0, j)),
                      pl.BlockSpec((3, r, tc), lambda j, q: (0, 0, j))],
            out_specs=pl.BlockSpec((r, tc), lambda j, q: (0, j))),
        compiler_params=pltpu.CompilerParams(dimension_semantics=("parallel",)),
        name=name,
    )(chip, own, others)


_HBM = pl.BlockSpec(memory_space=pltpu.HBM)
_SEM = pl.BlockSpec(memory_space=pltpu.SEMAPHORE)
_DATAFLOW = pltpu.SideEffectType.DATAFLOW_SIDE_EFFECTING
N_PEERS = N_DEV - 1


def _ring_peer(j):
    me = 4 * lax.axis_index("x") + 2 * lax.axis_index("y") + lax.axis_index("c")
    k = (me + j) % N_DEV
    return me, k, (k // 4, (k // 2) % 2, k % 2)


def _spread_copy(i, j, src_refs, land_refs, send_sems, recv_sems, scatter):
    me, k, peer = _ring_peer(j)
    return pltpu.make_async_remote_copy(
        src_ref=src_refs[i].at[k] if scatter else src_refs[i], dst_ref=land_refs[i].at[me],
        send_sem=send_sems.at[N_PEERS * i + j - 1], recv_sem=recv_sems.at[N_PEERS * i + j - 1], device_id=peer,
        device_id_type=_MESH_ID)


def _spread_start(srcs, carry, scatter, name):
    n = len(srcs)
    lands = [lax.empty(((N_DEV,) + s.shape[-2:]), s.dtype) for s in srcs]

    def body(*refs):
        src_refs, land_refs = refs[:n], refs[n:2 * n]
        send_sems, recv_sems, local_sems = refs[2 * n + 1:2 * n + 4]
        for i in range(n):
            for j in range(1, N_DEV):
                _spread_copy(i, j, src_refs, land_refs, send_sems, recv_sems, scatter).start()
        for i in range(n):
            _own_copy(i, src_refs, land_refs, local_sems, scatter).start()

    hbm = lambda a: pltpu.HBM(a.shape, a.dtype)
    sems = pltpu.SemaphoreType.DMA((N_PEERS * n,))
    pinned = [pltpu.with_memory_space_constraint(a, pltpu.HBM) for a in list(srcs) + lands + [carry]]
    res = pl.pallas_call(
        body, name=name,
        out_shape=(sems, sems, pltpu.SemaphoreType.DMA((n,)), *[hbm(a) for a in pinned]),
        in_specs=[_HBM] * (2 * n + 1),
        out_specs=(_SEM, _SEM, _SEM, *[_HBM] * (2 * n + 1)),
        input_output_aliases={i: 3 + i for i in range(2 * n + 1)},
        compiler_params=pltpu.CompilerParams(has_side_effects=_DATAFLOW),
    )(*pinned)
    return res[:3], list(res[3:3 + n]), list(res[3 + n:3 + 2 * n]), res[3 + 2 * n]


def _own_copy(i, src_refs, land_refs, local_sems, scatter):
    me = _ring_peer(0)[0]
    return pltpu.make_async_copy(src_refs[i].at[me] if scatter else src_refs[i], land_refs[i].at[me],
                                 local_sems.at[i])


def _spread_wait(started, after, scatter, name):
    sems, srcs, lands, _ = started
    n = len(srcs)

    def body(*refs):
        src_refs, land_refs = refs[:n], refs[n:2 * n]
        send_s, recv_s, local_s = refs[2 * n:2 * n + 3]
        for i in range(n):
            for j in range(1, N_DEV):
                cp = _spread_copy(i, j, src_refs, land_refs, send_s, recv_s, scatter)
                cp.wait_send()
                cp.wait_recv()
        for i in range(n):
            _own_copy(i, src_refs, land_refs, local_s, scatter).wait()

    hbm = lambda a: pltpu.HBM(a.shape, a.dtype)
    res = pl.pallas_call(
        body, name=name,
        out_shape=tuple(hbm(a) for a in srcs + lands),
        in_specs=[_HBM] * (2 * n) + [_SEM, _SEM, _SEM, pl.BlockSpec(memory_space=pl.ANY)],
        out_specs=tuple([_HBM] * (2 * n)),
        input_output_aliases={i: i for i in range(2 * n)},
        compiler_params=pltpu.CompilerParams(has_side_effects=_DATAFLOW),
    )(*srcs, *lands, *sems, after)
    return list(res[n:])


def _sum8(landing, name):
    _, r, c = landing.shape
    tc = _col_tile(c)

    def body(a_ref, o_ref):
        tot = a_ref[0].astype(F32)
        for k in range(1, N_DEV):
            tot = tot + a_ref[k].astype(F32)
        o_ref[...] = tot

    return pl.pallas_call(
        body,
        out_shape=jax.ShapeDtypeStruct((r, c), F32),
        grid=(c // tc,),
        in_specs=[pl.BlockSpec((N_DEV, r, tc), lambda j: (0, 0, j))],
        out_specs=pl.BlockSpec((r, tc), lambda j: (0, j)),
        compiler_params=pltpu.CompilerParams(dimension_semantics=("parallel",)),
        name=name,
    )(landing)


def _adamw_math(w, g, m, v):
    m = ADAM_B1 * m + (1.0 - ADAM_B1) * g
    v = ADAM_B2 * v + (1.0 - ADAM_B2) * (g * g)
    m_hat = m / (1.0 - ADAM_B1 ** ADAM_STEP)
    v_hat = v / (1.0 - ADAM_B2 ** ADAM_STEP)
    delta = -ADAM_LR * (m_hat / (jnp.sqrt(v_hat) + ADAM_EPS) + ADAM_WD * w)
    return delta, m, v


def _adamw(w, m, v, g, name):
    r, c = w.shape

    def fn(rows, consts):
        return list(_adamw_math(*rows)), []

    return _rowwise(fn, [w, g, m, v], [], [(c, F32)] * 3, tm=r if r <= 512 else 256, name=name)


def _adamw_small(gathered, params):
    ns = len(_SMALL)

    def body(*refs):
        g_ref, p_refs, o_refs = refs[0], refs[1:1 + 3 * ns], refs[1 + 3 * ns:]
        tot = g_ref[0]
        for k in range(1, N_DEV):
            tot = tot + g_ref[k]
        for i, name in enumerate(_SMALL):
            row, lane0, lanes = _SMALL_SLOTS[name]
            g = tot[row:row + 1, lane0:lane0 + lanes]
            w_, m_, v_ = (p_refs[3 * i + j][...] for j in range(3))
            delta, m2, v2 = _adamw_math(w_, g, m_, v_)
            for j, val in enumerate((g, delta, m2, v2)):
                o_refs[4 * i + j][...] = val
        o_refs[4 * ns][...] = tot[_LOSS_ROW:_LOSS_ROW + 1, 0:LANES]
        o_refs[4 * ns + 1][...] = tot[_CONV_ROW0:_CONV_ROW0 + _CONV_ROWS, :]

    out_shape = [jax.ShapeDtypeStruct(w.shape, F32) for (w, _, _) in params for _ in range(4)]
    out_shape += [jax.ShapeDtypeStruct((1, LANES), F32), jax.ShapeDtypeStruct((_CONV_ROWS, FLAT_COLS), F32)]
    flat = [a for wmv in params for a in wmv]
    return pl.pallas_call(body, out_shape=out_shape, name="adamw_small")(gathered, *flat)


def kernel(x, p, positions, w_in, conv_w, dn_a_log, dn_dt_bias, dn_norm_w, q_norm_w, w_uq, kv_norm_w, w_uk, w_uv, w_br_dn, w_br_mla, w_o, ln1_g, ln1_b, w_ffn_in, w_ffn_out, w_ple, w_ple_gate, ln2_g, ln2_b, loss_target, m_w_in, m_conv_w, m_dn_a_log, m_dn_dt_bias, m_dn_norm_w, m_q_norm_w, m_w_uq, m_kv_norm_w, m_w_uk, m_w_uv, m_w_br_dn, m_w_br_mla, m_w_o, m_ln1_g, m_ln1_b, m_w_ffn_in, m_w_ffn_out, m_w_ple, m_w_ple_gate, m_ln2_g, m_ln2_b, v_w_in, v_conv_w, v_dn_a_log, v_dn_dt_bias, v_dn_norm_w, v_q_norm_w, v_w_uq, v_kv_norm_w, v_w_uk, v_w_uv, v_w_br_dn, v_w_br_mla, v_w_o, v_ln1_g, v_ln1_b, v_w_ffn_in, v_w_ffn_out, v_w_ple, v_w_ple_gate, v_ln2_g, v_ln2_b):
    args = dict(locals())
    wts = {n: args[n] for n in _ORDER}
    mom1 = {n: args["m_" + n] for n in _ORDER}
    mom2 = {n: args["v_" + n] for n in _ORDER}
    big_names = [n for n, _ in _BIG]
    shard_shapes = {n: wts[n].shape[1:] for n in big_names}
    c_idx = lax.axis_index("c")
    q_idx = 2 * lax.axis_index("x") + lax.axis_index("y")
    parity, chip = c_idx.reshape(1).astype(jnp.int32), q_idx.reshape(1).astype(jnp.int32)

    stored = {n: _to_stored(n, wts[n][0]).astype(BF16) for n in big_names}
    first = _all_gather([stored["w_in"], conv_w[0]], "ag_first")
    group_names = {grp: [n for n, _ in pairs] for grp, pairs in _GROUP_GRADS.items()}
    carry, gathers = first[0], {}
    for grp in ("mix", "ffn"):
        gathers[grp] = _spread_start([stored[n] for n in group_names[grp]], carry, False, "ag_start_" + grp)
        carry = gathers[grp][3]
    conv_full = jnp.moveaxis(first[1], 0, 1).reshape(conv_w.shape[1], -1)
    small_w = {n: wts[n].astype(F32) for n in _SMALL}
    w = _first_weights(carry.reshape(-1, D_MODEL), conv_full, small_w)

    def late_weights(grp, after):
        got = _spread_wait(gathers[grp], after, False, "ag_wait_" + grp)
        return _late_weights(grp, {n: t.reshape(-1, t.shape[-1]) for n, t in zip(group_names[grp], got)})

    started = {}

    def emit(group, g, carry):
        grads = _group_grads(group, g)
        srcs = [grads[n].reshape((N_DEV,) + _stored_shape(n, shard_shapes[n])) for n in grads]
        started[group] = (list(grads), _spread_start(srcs, carry, True, "rs_start_" + group))
        return started[group][1][3]

    s_dim = x.shape[1]
    loss, dx, g = _local_step(x[0], p[0, 0], positions.reshape(s_dim, 1).astype(F32), loss_target[0], w,
                              late_weights, emit)
    g_w_in, small_g = _last_grads(g)

    src = g_w_in.reshape((N_DEV,) + _stored_shape("w_in", shard_shapes["w_in"]))
    from_sibling = _exchange_sibling([src], "rs_sibling")[0]
    own, own_bf = _chip_sum(src, from_sibling, parity, "rs_sum_w_in")
    from_chips = _exchange_chips([own_bf], "rs_chips")[0]

    out_g, out_d, out_m, out_v = {}, {}, {}, {}

    def update(n, grad, shp):
        flat2 = (shp[0], int(np.prod(shp[1:])))
        d, m2, v2 = _adamw(wts[n][0].reshape(flat2), mom1[n][0].reshape(flat2), mom2[n][0].reshape(flat2),
                           grad.reshape(flat2), "adamw_" + n)
        out_g[n], out_d[n], out_m[n], out_v[n] = grad, d.reshape(shp), m2.reshape(shp), v2.reshape(shp)

    total = _sum_parts(own, from_chips, chip, "rs_total_w_in")
    update("w_in", _from_stored("w_in", total, shard_shapes["w_in"]), shard_shapes["w_in"])
    for group, (names, st) in started.items():
        for n, landing in zip(names, _spread_wait(st, dx, True, "rs_wait_" + group)):
            update(n, _from_stored(n, _sum8(landing, "rs_total_" + n), shard_shapes[n]), shard_shapes[n])

    g_small = _all_gather([_pack_small_grads(small_g, loss)], "ag_small")[0]
    res = _adamw_small(g_small, [(wts[n], mom1[n], mom2[n]) for n in _SMALL])
    for i, n in enumerate(_SMALL):
        out_g[n], out_d[n], out_m[n], out_v[n] = res[4 * i:4 * i + 4]
    loss_out = res[4 * len(_SMALL)][0, 0]
    conv_shape = conv_w.shape[1:]
    conv_g = lax.dynamic_slice(res[-1].reshape(conv_shape[0], -1), (0, (2 * q_idx + c_idx) * conv_shape[1]),
                               conv_shape)
    update("conv_w", conv_g, conv_shape)

    expand = lambda d, n: d[n] if n in _SMALL else d[n][None]
    return (loss_out, dx[None], *[expand(out_g, n) for n in _ORDER], *[expand(out_d, n) for n in _ORDER],
            *[expand(out_m, n) for n in _ORDER], *[expand(out_v, n) for n in _ORDER])
```

```python
import functools

import numpy as np
import jax
import jax.numpy as jnp
from jax import lax
from jax.experimental import pallas as pl
from jax.experimental.pallas import tpu as pltpu

F32 = jnp.float32
BF16 = jnp.bfloat16

D_MODEL = 1024
N_HEADS = 8
HEAD = 128
CHUNK = 64
GROUP = 256
ROPE = 64
Q_LORA = 384
KV_LORA = 256
FFN_HIDDEN = 2816
PLE_DIM = 256
ROPE_BASE = 10000.0
ALPHA = 2.0 ** 0.25
SCALE = float((HEAD + ROPE) ** -0.5)
NEG_BIG = -1e30
EPS_RMS = 1e-6
EPS_LN = 1e-5

ADAM_LR = 0.001
ADAM_B1 = 0.9
ADAM_B2 = 0.999
ADAM_EPS = 1e-08
ADAM_WD = 0.01
ADAM_STEP = 10

N_DEV = 8
LANES = 128
FLAT_COLS = 1024

WB_CQ, WB_CKV, WB_KR, WB_BA, WB_COLS = 0, 512, 768, 896, 1024

HIGHEST = lax.Precision.HIGHEST

NN = (((1,), (0,)), ((), ()))
TN = (((0,), (0,)), ((), ()))
NT = (((1,), (1,)), ((), ()))


def _dot(a, b, dims=NN):
    return lax.dot_general(a.astype(BF16), b.astype(BF16), dims, preferred_element_type=F32)


def _dot32(a, b, dims=NN):
    return lax.dot_general(a, b, dims, precision=HIGHEST, preferred_element_type=F32)


def _sig(x):
    return 1.0 / (1.0 + jnp.exp(-x))


MM_TILE = 1536


def _pick_wide(n):
    if n <= MM_TILE:
        return n
    return max(t for t in range(LANES, MM_TILE + 1, LANES) if n % t == 0)


def _split_bf16(a):
    hi = a.astype(BF16)
    return hi, (a - hi.astype(F32)).astype(BF16)


def _dot3(a, b, dims=NN):
    ah, al = a if isinstance(a, tuple) else _split_bf16(a)
    bh, bl = b if isinstance(b, tuple) else _split_bf16(b)
    d = lambda p, q: lax.dot_general(p, q, dims, preferred_element_type=F32)
    return d(ah, bh) + (d(ah, bl) + d(al, bh))


def _mm(a, b, *, ta=False, tb=False, add=(), out_dtype=F32, name):
    if ta:
        k_dim, m_dim = a.shape
    else:
        m_dim, k_dim = a.shape
    if tb:
        n_dim, k2 = b.shape
    else:
        k2, n_dim = b.shape
    assert k_dim == k2, (a.shape, b.shape, ta, tb)
    tm = _pick_wide(m_dim)
    tn = _pick_wide(n_dim)
    tk = _pick_wide(k_dim)
    nk = k_dim // tk
    n_add = len(add)
    dims = TN if ta else (NT if tb else NN)
    assert not (ta and tb)

    def body(a_ref, b_ref, *rest):
        add_refs = rest[:n_add]
        o_ref = rest[n_add]
        acc = rest[n_add + 1]
        k = pl.program_id(2)

        @pl.when(k == 0)
        def _():
            acc[...] = jnp.zeros_like(acc)

        acc[...] += _dot(a_ref[...], b_ref[...], dims)

        @pl.when(k == nk - 1)
        def _():
            r = acc[...]
            for ar in add_refs:
                r = r + ar[...].astype(F32)
            o_ref[...] = r.astype(o_ref.dtype)

    a_spec = pl.BlockSpec((tk, tm), lambda i, j, k: (k, i)) if ta else pl.BlockSpec((tm, tk), lambda i, j, k: (i, k))
    b_spec = pl.BlockSpec((tn, tk), lambda i, j, k: (j, k)) if tb else pl.BlockSpec((tk, tn), lambda i, j, k: (k, j))
    o_spec = pl.BlockSpec((tm, tn), lambda i, j, k: (i, j))
    return pl.pallas_call(
        body,
        out_shape=jax.ShapeDtypeStruct((m_dim, n_dim), out_dtype),
        grid=(m_dim // tm, n_dim // tn, nk),
        in_specs=[a_spec, b_spec] + [o_spec] * n_add,
        out_specs=o_spec,
        scratch_shapes=[pltpu.VMEM((tm, tn), F32)],
        compiler_params=pltpu.CompilerParams(dimension_semantics=("parallel", "parallel", "arbitrary")),
        name=name,
    )(a, b, *add)


def _rowwise(fn, rows, consts, outs, accs=(), *, tm=256, name):
    rows = [r if isinstance(r, tuple) else (r, 0, r.shape[1]) for r in rows]
    s_dim = rows[0][0].shape[0]
    tm = min(tm, s_dim)
    assert s_dim % tm == 0 and all(arr.shape[0] == s_dim for arr, _, _ in rows)
    specs = [pl.BlockSpec((tm, width), functools.partial(lambda i, cb: (i, cb), cb=cb)) for _, cb, width in rows]
    args = [arr for arr, _, _ in rows]
    for c in consts:
        specs.append(pl.BlockSpec(c.shape, lambda i: (0, 0)))
        args.append(c)
    nr, nc, no = len(rows), len(consts), len(outs)
    out_shape = [jax.ShapeDtypeStruct((s_dim, w), dt) for (w, dt) in outs]
    out_specs = [pl.BlockSpec((tm, w), lambda i: (i, 0)) for (w, dt) in outs]
    out_shape += [jax.ShapeDtypeStruct(sh, F32) for sh in accs]
    out_specs += [pl.BlockSpec(sh, lambda i: (0, 0)) for sh in accs]

    def body(*refs):
        r = [x[...] for x in refs[:nr]]
        c = [x[...] for x in refs[nr:nr + nc]]
        o_refs = refs[nr + nc:nr + nc + no]
        a_refs = refs[nr + nc + no:]
        o_vals, a_vals = fn(r, c)
        for ref, v in zip(o_refs, o_vals, strict=True):
            ref[...] = v.astype(ref.dtype)
        if a_refs:
            @pl.when(pl.program_id(0) == 0)
            def _():
                for ref in a_refs:
                    ref[...] = jnp.zeros_like(ref)

            for ref, v in zip(a_refs, a_vals, strict=True):
                ref[...] += v

    res = pl.pallas_call(
        body,
        out_shape=out_shape,
        grid=(s_dim // tm,),
        in_specs=specs,
        out_specs=out_specs,
        compiler_params=pltpu.CompilerParams(dimension_semantics=("arbitrary" if accs else "parallel",)),
        name=name,
    )(*args)
    return res


def _colsum(v):
    return jnp.sum(v, axis=0, keepdims=True)


def _rowsum(v):
    return jnp.sum(v, axis=1, keepdims=True)


def _rowmean(v):
    return jnp.mean(v, axis=1, keepdims=True)


def _silu_grad(x):
    s = _sig(x)
    return s * (1.0 + x * (1.0 - s))


def _conv_taps(x, w, width=4):
    row = lax.broadcasted_iota(jnp.int32, x.shape, 0)
    c = x * w[width - 1:width, :]
    for s in range(1, width):
        c = c + jnp.where(row >= s, pltpu.roll(x, s, 0), 0.0) * w[width - 1 - s:width - s, :]
    return c


def _conv_fwd(proj_a, conv_w):
    s_dim = proj_a.shape[0]
    n_blk = 3 * N_HEADS

    def body(x_ref, w_ref, o_ref):
        j = pl.program_id(0)
        c = _conv_taps(x_ref[...], w_ref[...])
        y = c * _sig(c)
        r = lax.rsqrt(_rowsum(y * y) + EPS_RMS)
        fac = jnp.where(j < N_HEADS, r * (HEAD ** -0.5), jnp.where(j < 2 * N_HEADS, r, 1.0))
        o_ref[...] = y * fac

    return pl.pallas_call(
        body,
        out_shape=jax.ShapeDtypeStruct((s_dim, n_blk * HEAD), F32),
        grid=(n_blk,),
        in_specs=[pl.BlockSpec((s_dim, HEAD), lambda j: (0, j)), pl.BlockSpec((4, HEAD), lambda j: (0, j))],
        out_specs=pl.BlockSpec((s_dim, HEAD), lambda j: (0, j)),
        compiler_params=pltpu.CompilerParams(dimension_semantics=("parallel",)),
        name="conv_fwd",
    )(proj_a, conv_w)


def _conv_bwd(proj_a, conv_w, dq, dk, dv):
    s_dim = proj_a.shape[0]
    n_blk = 3 * N_HEADS

    def body(x_ref, w_ref, dq_ref, dk_ref, dv_ref, dx_ref, dw_ref):
        j = pl.program_id(0)
        x = x_ref[...]
        w = w_ref[...]
        do = jnp.where(j < N_HEADS, dq_ref[...], jnp.where(j < 2 * N_HEADS, dk_ref[...], dv_ref[...]))
        c = _conv_taps(x, w)
        sg = _sig(c)
        y = c * sg
        r = lax.rsqrt(_rowsum(y * y) + EPS_RMS)
        sc = jnp.where(j < N_HEADS, HEAD ** -0.5, 1.0)
        dy_n = sc * (r * do - y * (r * r * r) * _rowsum(do * y))
        dy = jnp.where(j < 2 * N_HEADS, dy_n, do)
        dc = dy * (sg * (1.0 + c * (1.0 - sg)))
        row = lax.broadcasted_iota(jnp.int32, x.shape, 0)
        dx = dc * w[3:4, :]
        dw_ref[3:4, :] = _colsum(dc * x)
        for s in range(1, 4):
            dx = dx + jnp.where(row < s_dim - s, pltpu.roll(dc, s_dim - s, 0), 0.0) * w[3 - s:4 - s, :]
            xs = jnp.where(row >= s, pltpu.roll(x, s, 0), 0.0)
            dw_ref[3 - s:4 - s, :] = _colsum(dc * xs)
        dx_ref[...] = dx.astype(dx_ref.dtype)

    hd = N_HEADS - 1
    return pl.pallas_call(
        body,
        out_shape=[jax.ShapeDtypeStruct((s_dim, n_blk * HEAD), BF16), jax.ShapeDtypeStruct((4, n_blk * HEAD), F32)],
        grid=(n_blk,),
        in_specs=[
            pl.BlockSpec((s_dim, HEAD), lambda j: (0, j)),
            pl.BlockSpec((4, HEAD), lambda j: (0, j)),
            pl.BlockSpec((s_dim, HEAD), lambda j: (0, jnp.minimum(j, hd))),
            pl.BlockSpec((s_dim, HEAD), lambda j: (0, jnp.clip(j - N_HEADS, 0, hd))),
            pl.BlockSpec((s_dim, HEAD), lambda j: (0, jnp.clip(j - 2 * N_HEADS, 0, hd))),
        ],
        out_specs=[pl.BlockSpec((s_dim, HEAD), lambda j: (0, j)), pl.BlockSpec((4, HEAD), lambda j: (0, j))],
        compiler_params=pltpu.CompilerParams(dimension_semantics=("parallel",)),
        name="conv_bwd",
    )(proj_a, conv_w, dq, dk, dv)


def _chunk_tri(n):
    r = np.arange(n)
    m = ((r[:, None] // CHUNK) == (r[None, :] // CHUNK)) & (r[:, None] >= r[None, :])
    m = m.astype(np.float32)
    return jnp.asarray(m), jnp.asarray(m.T)


def _softplus(z):
    return jnp.maximum(z, 0.0) + jnp.log(1.0 + jnp.exp(-jnp.abs(z)))


def _gates_fwd(proj_b, alog, dtb):
    tm = min(GROUP, proj_b.shape[0])
    tri, _ = _chunk_tri(tm)

    def fn(r, c):
        b = r[0]
        a = pltpu.roll(b, LANES - N_HEADS, 1)
        alog_, dtb_, tri_ = c
        g = -jnp.exp(alog_) * _softplus(a + dtb_)
        return [_sig(b), _dot32(tri_, g)], []

    return _rowwise(fn, [(proj_b, WB_BA // LANES, LANES)], [alog, dtb, tri],
                    [(LANES, F32), (LANES, F32)], tm=tm, name="gates_fwd")


def _gates_bwd(proj_b, alog, dtb, gc, d_beta, d_gc, d_egl_rows):
    tm = min(GROUP, proj_b.shape[0])
    _, tri_t = _chunk_tri(tm)

    def fn(r, c):
        b, gc_, d_beta_, d_gc_, d_egl_ = r
        a = pltpu.roll(b, LANES - N_HEADS, 1)
        alog_, dtb_, tri_t_ = c
        z = a + dtb_
        ea = jnp.exp(alog_)
        g = -ea * _softplus(z)
        dg = _dot32(tri_t_, d_gc_ + d_egl_ * jnp.exp(gc_))
        d_a = dg * (-ea) * _sig(z)
        beta = _sig(b)
        d_ba = d_beta_ * beta * (1.0 - beta) + pltpu.roll(d_a, N_HEADS, 1)
        return [d_ba], [_colsum(dg * g), _colsum(d_a)]

    return _rowwise(fn, [(proj_b, WB_BA // LANES, LANES), gc, d_beta, d_gc, d_egl_rows],
                    [alog, dtb, tri_t], [(LANES, BF16)], accs=[(1, LANES), (1, LANES)], tm=tm,
                    name="gates_bwd")


def _group_masks(n):
    r = lax.broadcasted_iota(jnp.int32, (n, n), 0)
    c = lax.broadcasted_iota(jnp.int32, (n, n), 1)
    same = (r // CHUNK) == (c // CHUNK)
    below, s = [], 2
    while s < CHUNK:
        below.append(jnp.logical_and((r // (2 * s)) == (c // (2 * s)),
                                     jnp.logical_and((r // s) % 2 == 1, (c // s) % 2 == 0)))
        s *= 2
    return dict(same=same, tril=jnp.logical_and(same, r >= c), strict=jnp.logical_and(same, r > c),
                last=c == (r // CHUNK) * CHUNK + (CHUNK - 1), eye=r == c, pair=(r // 2) == (c // 2), below=below)


def _inv_unit_lower(l_mats, mk):
    eye_f = mk["eye"].astype(F32)
    ts = [eye_f - jnp.where(mk["pair"], l_mat, 0.0) for l_mat in l_mats]
    for below in mk["below"]:
        halves = [_split_bf16(t) for t in ts]
        mids = [_dot3(h, jnp.where(below, l_mat, 0.0)) for h, l_mat in zip(halves, l_mats)]
        ts = [t - _dot3(m, h) for t, m, h in zip(ts, mids, halves)]
    return ts


def _unfold_blocks(folded, mask):
    n = folded.shape[0]
    return jnp.where(mask, jnp.concatenate([folded] * (n // CHUNK), axis=1), 0.0)


def _head_cols(beta, gc, gc_t, h):
    lane = lax.broadcasted_iota(jnp.int32, beta.shape, 1)
    sub = lax.broadcasted_iota(jnp.int32, gc_t.shape, 0)
    bcol = _rowsum(jnp.where(lane == h, beta, 0.0))
    gcol = _rowsum(jnp.where(lane == h, gc, 0.0))
    grow = _colsum(jnp.where(sub == h, gc_t, 0.0))
    return bcol, gcol, grow


def _prep_common(q, k, bcol, gcol, grow, mk, t_folded=None):
    n = q.shape[0]
    tril = mk["tril"]
    decay = jnp.where(tril, jnp.exp(jnp.where(tril, gcol - grow, 0.0)), 0.0)
    glast = _rowsum(jnp.where(mk["last"], jnp.broadcast_to(grow, (n, n)), 0.0))
    e = jnp.exp(gcol)
    ekt = jnp.exp(glast - gcol)
    kb = k * bcol
    kk = _dot(kb, k, NT)
    qk = _dot(q, k, NT)
    p = dict(decay=decay, e=e, ekt=ekt, kb=kb, kk=kk, qk=qk)
    if t_folded is not None:
        p["t"] = _unfold_blocks(t_folded, mk["same"])
    return p


GROUPS_PER_STEP = 4


def _fold_blocks(m):
    n = m.shape[0]
    out = m[:, 0:CHUNK]
    for b in range(1, n // CHUNK):
        out = out + m[:, b * CHUNK:(b + 1) * CHUNK]
    return out


def _gdr_prep_fwd(qkvn, beta, gc, gc_t):
    s_dim = qkvn.shape[0]
    tg = min(GROUP, s_dim)
    n_sub = min(GROUPS_PER_STEP, s_dim // tg)
    tb = tg * n_sub

    def body(q_ref, k_ref, v_ref, b_ref, g_ref, gt_ref, u_ref, w_ref, qd_ref, kt_ref, a_ref, t_ref):
        h = pl.program_id(0)
        mk = _group_masks(tg)
        parts = []
        for s in range(n_sub):
            rows = slice(s * tg, (s + 1) * tg)
            q, k, v = q_ref[rows, :], k_ref[rows, :], v_ref[rows, :]
            bcol, gcol, grow = _head_cols(b_ref[rows, :], g_ref[rows, :], gt_ref[:, rows], h)
            p = _prep_common(q, k, bcol, gcol, grow, mk)
            qd_ref[rows, :] = q * p["e"]
            kt_ref[rows, :] = k * p["ekt"]
            a_ref[rows, :] = _fold_blocks(jnp.where(mk["tril"], p["qk"] * p["decay"], 0.0))
            parts.append((rows, v * bcol, p["kb"] * p["e"], jnp.where(mk["strict"], p["kk"] * p["decay"], 0.0)))
        t_mats = _inv_unit_lower([part[3] for part in parts], mk)
        for (rows, vb, kbe, _), t_mat in zip(parts, t_mats):
            u_ref[rows, :] = _dot(t_mat, vb)
            w_ref[rows, :] = _dot(t_mat, kbe)
            t_ref[rows, :] = _fold_blocks(t_mat)

    row = lambda off: pl.BlockSpec((tb, HEAD), functools.partial(lambda h, m, off: (m, h + off), off=off))
    full = pl.BlockSpec((tb, LANES), lambda h, m: (m, 0))
    o_spec = pl.BlockSpec((tb, HEAD), lambda h, m: (m, h))
    a_spec = pl.BlockSpec((None, tb, CHUNK), lambda h, m: (h, m, 0))
    wide = jax.ShapeDtypeStruct((s_dim, N_HEADS * HEAD), F32)
    folded = jax.ShapeDtypeStruct((N_HEADS, s_dim, CHUNK), F32)
    return pl.pallas_call(
        body,
        out_shape=[wide, wide, wide, wide, folded, folded],
        grid=(N_HEADS, s_dim // tb),
        in_specs=[row(0), row(N_HEADS), row(2 * N_HEADS), full, full, pl.BlockSpec((8, tb), lambda h, m: (0, m))],
        out_specs=[o_spec, o_spec, o_spec, o_spec, a_spec, a_spec],
        compiler_params=pltpu.CompilerParams(dimension_semantics=("parallel", "parallel")),
        name="gdr_prep_fwd",
    )(qkvn, qkvn, qkvn, beta, gc, gc_t)


def _gdr_prep_bwd(qkvn, beta, gc, gc_t, t_fold, u, w, du, dw, dqd, dkt, d_a):
    s_dim = qkvn.shape[0]
    tg = min(GROUP, s_dim)
    n_sub = min(GROUPS_PER_STEP, s_dim // tg)
    tb = tg * n_sub

    def body(q_ref, k_ref, v_ref, b_ref, g_ref, gt_ref, t_ref, u_ref, w_ref, du_ref, dw_ref, dqd_ref, dkt_ref,
             da_ref, dq_ref, dk_ref, dv_ref, db_ref, dg_ref):
        h = pl.program_id(1)

        @pl.when(h == 0)
        def _():
            db_ref[...] = jnp.zeros_like(db_ref)
            dg_ref[...] = jnp.zeros_like(dg_ref)

        mk = _group_masks(tg)
        lane = lax.broadcasted_iota(jnp.int32, (tg, LANES), 1)
        for s in range(n_sub):
            rows = slice(s * tg, (s + 1) * tg)
            q, k, v = q_ref[rows, :], k_ref[rows, :], v_ref[rows, :]
            bcol, gcol, grow = _head_cols(b_ref[rows, :], g_ref[rows, :], gt_ref[:, rows], h)
            p = _prep_common(q, k, bcol, gcol, grow, mk, t_ref[rows, :])
            t_mat, decay, e, ekt, kb = p["t"], p["decay"], p["e"], p["ekt"], p["kb"]
            du_, dw_, dqd_, dkt_ = du_ref[rows, :], dw_ref[rows, :], dqd_ref[rows, :], dkt_ref[rows, :]
            dvb = _dot(t_mat, du_, TN)
            dkbe = _dot(t_mat, dw_, TN)
            d_l = -(_dot(dvb, u_ref[rows, :], NT) + _dot(dkbe, w_ref[rows, :], NT))
            m1 = jnp.where(mk["strict"], d_l, 0.0)
            m2 = _unfold_blocks(da_ref[rows, :], mk["tril"])
            d_kk = m1 * decay
            d_qk = m2 * decay
            d_decay = m1 * p["kk"] + m2 * p["qk"]
            dkb = _dot(d_kk, k) + dkbe * e
            dk = _dot(d_kk, kb, TN) + _dot(d_qk, q, TN) + dkt_ * ekt + dkb * bcol
            dq = _dot(d_qk, k) + dqd_ * e
            d_beta = _rowsum(dkb * k) + _rowsum(dvb * v)
            d_e = _rowsum(dkbe * kb) + _rowsum(dqd_ * q)
            d_ekt = _rowsum(dkt_ * k) * ekt
            d_diff = d_decay * decay
            d_grow = -_colsum(d_diff) + _colsum(jnp.where(mk["last"], jnp.broadcast_to(d_ekt, (tg, tg)), 0.0))
            d_gcol = d_e * e - d_ekt + _rowsum(d_diff)
            d_gcol = d_gcol + _rowsum(jnp.where(mk["eye"], jnp.broadcast_to(d_grow, (tg, tg)), 0.0))
            dq_ref[rows, :] = dq
            dk_ref[rows, :] = dk
            dv_ref[rows, :] = dvb * bcol
            db_ref[rows, :] = jnp.where(lane == h, d_beta, db_ref[rows, :])
            dg_ref[rows, :] = jnp.where(lane == h, d_gcol, dg_ref[rows, :])

    row = lambda off: pl.BlockSpec((tb, HEAD), functools.partial(lambda m, h, off: (m, h + off), off=off))
    full = pl.BlockSpec((tb, LANES), lambda m, h: (m, 0))
    o_spec = pl.BlockSpec((tb, HEAD), lambda m, h: (m, h))
    a_spec = pl.BlockSpec((None, tb, CHUNK), lambda m, h: (h, m, 0))
    wide = jax.ShapeDtypeStruct((s_dim, N_HEADS * HEAD), F32)
    lanes = jax.ShapeDtypeStruct((s_dim, LANES), F32)
    return pl.pallas_call(
        body,
        out_shape=[wide, wide, wide, lanes, lanes],
        grid=(s_dim // tb, N_HEADS),
        in_specs=[row(0), row(N_HEADS), row(2 * N_HEADS), full, full, pl.BlockSpec((8, tb), lambda m, h: (0, m)),
                  a_spec, o_spec, o_spec, o_spec, o_spec, o_spec, o_spec, a_spec],
        out_specs=[o_spec, o_spec, o_spec, full, full],
        compiler_params=pltpu.CompilerParams(dimension_semantics=("parallel", "arbitrary")),
        name="gdr_prep_bwd",
    )(qkvn, qkvn, qkvn, beta, gc, gc_t, t_fold, u, w, du, dw, dqd, dkt, d_a)


def _gdr_scan_fwd(u, w, qd, kt, a_mat, gc):
    s_dim = u.shape[0]
    n_chunks = s_dim // CHUNK

    def body(u_ref, w_ref, qd_ref, kt_ref, a_ref, g_ref, o_ref, st_ref, state):
        @pl.when(pl.program_id(0) == 0)
        def _():
            state[...] = jnp.zeros_like(state)

        egl = jnp.exp(g_ref[CHUNK - 1:CHUNK, :])
        heads = range(N_HEADS)
        cols = [slice(h * HEAD, (h + 1) * HEAD) for h in heads]
        s_b = [state[h].astype(BF16) for h in heads]
        for h in heads:
            st_ref[h] = state[h]
        ws = [_dot(w_ref[:, cs], s) for cs, s in zip(cols, s_b)]
        qs = [_dot(qd_ref[:, cs], s) for cs, s in zip(cols, s_b)]
        vns = [(u_ref[:, cs] - ws_h).astype(BF16) for cs, ws_h in zip(cols, ws)]
        avs = [_dot(a_ref[h], vn) for h, vn in zip(heads, vns)]
        kvs = [_dot(kt_ref[:, cs], vn, TN) for cs, vn in zip(cols, vns)]
        for h, cs in zip(heads, cols):
            o_ref[:, cs] = qs[h] + avs[h]
            state[h] = state[h] * egl[:, h:h + 1] + kvs[h]

    wide = pl.BlockSpec((CHUNK, N_HEADS * HEAD), lambda n: (n, 0))
    return pl.pallas_call(
        body,
        out_shape=[jax.ShapeDtypeStruct((s_dim, N_HEADS * HEAD), F32),
                   jax.ShapeDtypeStruct((n_chunks, N_HEADS, HEAD, HEAD), F32)],
        grid=(n_chunks,),
        in_specs=[wide, wide, wide, wide, pl.BlockSpec((N_HEADS, CHUNK, CHUNK), lambda n: (0, n, 0)),
                  pl.BlockSpec((CHUNK, LANES), lambda n: (n, 0))],
        out_specs=[wide, pl.BlockSpec((None, N_HEADS, HEAD, HEAD), lambda n: (n, 0, 0, 0))],
        scratch_shapes=[pltpu.VMEM((N_HEADS, HEAD, HEAD), F32)],
        compiler_params=pltpu.CompilerParams(dimension_semantics=("arbitrary",)),
        name="gdr_scan_fwd",
    )(u, w, qd, kt, a_mat, gc)


def _gdr_scan_bwd(u, w, qd, kt, a_mat, gc, states, d_o):
    s_dim = u.shape[0]
    n_chunks = s_dim // CHUNK
    last = n_chunks - 1

    def body(u_ref, w_ref, qd_ref, kt_ref, a_ref, g_ref, st_ref, do_ref,
             du_ref, dw_ref, dqd_ref, dkt_ref, da_ref, de_ref, d_state):
        @pl.when(pl.program_id(0) == 0)
        def _():
            d_state[...] = jnp.zeros_like(d_state)

        egl = jnp.exp(g_ref[CHUNK - 1:CHUNK, :])
        heads = range(N_HEADS)
        cols = [slice(h * HEAD, (h + 1) * HEAD) for h in heads]
        s_b = [st_ref[h].astype(BF16) for h in heads]
        ds_b = [d_state[h].astype(BF16) for h in heads]
        dos = [do_ref[:, cs].astype(BF16) for cs in cols]
        w_b = [w_ref[:, cs].astype(BF16) for cs in cols]
        ws = [_dot(w_h, s) for w_h, s in zip(w_b, s_b)]
        ados = [_dot(a_ref[h], do, TN) for h, do in zip(heads, dos)]
        kds = [_dot(kt_ref[:, cs], ds) for cs, ds in zip(cols, ds_b)]
        dqds = [_dot(do, s, NT) for do, s in zip(dos, s_b)]
        qdos = [_dot(qd_ref[:, cs], do, TN) for cs, do in zip(cols, dos)]
        vns = [(u_ref[:, cs] - ws_h).astype(BF16) for cs, ws_h in zip(cols, ws)]
        dvns = [a + k_ for a, k_ in zip(ados, kds)]
        dvn_b = [d.astype(BF16) for d in dvns]
        das = [_dot(do, vn, NT) for do, vn in zip(dos, vns)]
        dkts = [_dot(vn, ds, NT) for vn, ds in zip(vns, ds_b)]
        dws = [_dot(d, s, NT) for d, s in zip(dvn_b, s_b)]
        wds = [_dot(w_h, d, TN) for w_h, d in zip(w_b, dvn_b)]
        for h, cs in zip(heads, cols):
            ds_n = d_state[h]
            de = jnp.sum(_rowsum(ds_n * st_ref[h]), axis=0, keepdims=True)
            de_ref[h:h + 1, :] = jnp.broadcast_to(de, (1, LANES))
            dqd_ref[:, cs] = dqds[h]
            da_ref[h] = das[h]
            dkt_ref[:, cs] = dkts[h]
            du_ref[:, cs] = dvns[h]
            dw_ref[:, cs] = -dws[h]
            d_state[h] = ds_n * egl[:, h:h + 1] + qdos[h] - wds[h]

    wide = pl.BlockSpec((CHUNK, N_HEADS * HEAD), lambda n: (last - n, 0))
    a_spec = pl.BlockSpec((N_HEADS, CHUNK, CHUNK), lambda n: (0, last - n, 0))
    wide_shape = jax.ShapeDtypeStruct((s_dim, N_HEADS * HEAD), F32)
    return pl.pallas_call(
        body,
        out_shape=[wide_shape, wide_shape, wide_shape, wide_shape,
                   jax.ShapeDtypeStruct((N_HEADS, s_dim, CHUNK), F32),
                   jax.ShapeDtypeStruct((n_chunks, N_HEADS, LANES), F32)],
        grid=(n_chunks,),
        in_specs=[wide, wide, wide, wide, a_spec, pl.BlockSpec((CHUNK, LANES), lambda n: (last - n, 0)),
                  pl.BlockSpec((None, N_HEADS, HEAD, HEAD), lambda n: (last - n, 0, 0, 0)), wide],
        out_specs=[wide, wide, wide, wide, a_spec, pl.BlockSpec((None, N_HEADS, LANES), lambda n: (last - n, 0, 0))],
        scratch_shapes=[pltpu.VMEM((N_HEADS, HEAD, HEAD), F32)],
        compiler_params=pltpu.CompilerParams(dimension_semantics=("arbitrary",)),
        name="gdr_scan_bwd",
    )(u, w, qd, kt, a_mat, gc, states, d_o)


def _gdr_out_fwd(o_dn, proj_a, dn_w):
    def fn(r, c):
        o, z = r
        (w_,) = c
        outs = []
        for h in range(N_HEADS):
            cs = slice(h * HEAD, (h + 1) * HEAD)
            oh, zh = o[:, cs], z[:, cs]
            rr = lax.rsqrt(_rowmean(oh * oh) + EPS_RMS)
            outs.append(oh * rr * w_ * (zh * _sig(zh)))
        return [jnp.concatenate(outs, axis=1)], []

    return _rowwise(fn, [o_dn, (proj_a, 3, D_MODEL)], [dn_w], [(D_MODEL, BF16)], name="gdr_out_fwd")[0]


def _gdr_out_bwd(o_dn, proj_a, d_og, dn_w):
    def fn(r, c):
        o, z, dg = r
        (w_,) = c
        d_o, d_z = [], []
        d_w = jnp.zeros((1, HEAD), F32)
        for h in range(N_HEADS):
            cs = slice(h * HEAD, (h + 1) * HEAD)
            oh, zh, dgh = o[:, cs], z[:, cs], dg[:, cs]
            rr = lax.rsqrt(_rowmean(oh * oh) + EPS_RMS)
            sz = zh * _sig(zh)
            d_n = dgh * sz
            d_z.append(dgh * (oh * rr * w_) * _silu_grad(zh))
            d_w = d_w + _colsum(d_n * oh * rr)
            gw = d_n * w_
            d_o.append(rr * gw - oh * (rr * rr * rr) * _rowmean(gw * oh))
        return [jnp.concatenate(d_o, axis=1), jnp.concatenate(d_z, axis=1)], [d_w]

    return _rowwise(fn, [o_dn, (proj_a, 3, D_MODEL), d_og], [dn_w], [(D_MODEL, F32), (D_MODEL, BF16)],
                    accs=[(1, HEAD)], name="gdr_out_bwd")


def _rms_fwd(x, w):
    r = lax.rsqrt(_rowmean(x * x) + EPS_RMS)
    return x * r * w


def _rms_bwd(x, w, dy):
    r = lax.rsqrt(_rowmean(x * x) + EPS_RMS)
    gw = dy * w
    return r * gw - x * (r * r * r) * _rowmean(gw * x), _colsum(dy * x * r)


def _mla_norm_fwd(proj_b, qn_w, kvn_w):
    def fn(r, c):
        return [_rms_fwd(r[0], c[0]), _rms_fwd(r[1], c[1])], []

    return _rowwise(fn, [(proj_b, WB_CQ // Q_LORA, Q_LORA), (proj_b, WB_CKV // KV_LORA, KV_LORA)], [qn_w, kvn_w],
                    [(Q_LORA, BF16), (KV_LORA, BF16)], name="mla_norm_fwd")


def _mla_norm_bwd(proj_b, qn_w, kvn_w, d_cq, d_ckv):
    def fn(r, c):
        dx1, dw1 = _rms_bwd(r[0], c[0], r[2])
        dx2, dw2 = _rms_bwd(r[1], c[1], r[3])
        return [dx1, dx2], [dw1, dw2]

    return _rowwise(fn, [(proj_b, WB_CQ // Q_LORA, Q_LORA), (proj_b, WB_CKV // KV_LORA, KV_LORA), d_cq, d_ckv],
                    [qn_w, kvn_w], [(Q_LORA, BF16), (KV_LORA, BF16)], accs=[(1, Q_LORA), (1, KV_LORA)],
                    name="mla_norm_bwd")


def _rope_consts():
    inv = ROPE_BASE ** (-np.arange(0, ROPE, 2, dtype=np.float32) / ROPE)
    t = np.zeros((4, LANES), np.float32)
    t[0, :32] = inv
    t[0, 32:64] = inv
    t[1, :64] = 1.0
    t[2, 32:64] = 1.0
    t[3, :32] = -1.0
    return jnp.asarray(t)


def _rope_tables(pos, consts, width):
    ang = pos * consts[0:1, :]
    cosv, sinv = jnp.cos(ang), jnp.sin(ang)
    reps = width // LANES
    tile = (lambda t: jnp.concatenate([t] * reps, axis=1)) if reps > 1 else (lambda t: t)
    return tile(cosv * consts[1:2, :]), tile(sinv * consts[2:3, :]), tile(sinv * consts[3:4, :])


def _rope_apply(t, tabs):
    cos_t, sin_a, sin_b = tabs
    width = t.shape[1]
    return t * cos_t + pltpu.roll(t, 32, 1) * sin_a + pltpu.roll(t, width - 32, 1) * sin_b


def _rope_transpose(d, tabs):
    cos_t, sin_a, sin_b = tabs
    width = d.shape[1]
    return d * cos_t + pltpu.roll(d * sin_a, width - 32, 1) + pltpu.roll(d * sin_b, 32, 1)


QK_HEAD = 2 * HEAD


def _interleave_heads(a, b):
    parts = []
    for h in range(N_HEADS):
        parts.append(a[:, h * HEAD:(h + 1) * HEAD])
        parts.append(b if b.shape[1] == LANES else b[:, h * LANES:(h + 1) * LANES])
    return jnp.concatenate(parts, axis=1)


def _mla_qk_fwd(q_full, k_nope, proj_b, pos):
    consts = _rope_consts()

    def fn(r, c):
        qf, kn, kr, pos_ = r
        qn, qr = qf[:, :D_MODEL], qf[:, D_MODEL:]
        qr = _rope_apply(qr, _rope_tables(pos_, c[0], D_MODEL))
        kr = _rope_apply(kr, _rope_tables(pos_, c[0], LANES))
        return [_interleave_heads(qn, qr) * SCALE, _interleave_heads(kn, kr)], []

    return _rowwise(fn, [q_full, k_nope, (proj_b, WB_KR // LANES, LANES), pos], [consts],
                    [(N_HEADS * QK_HEAD, BF16), (N_HEADS * QK_HEAD, BF16)], name="mla_qk_fwd")


def _mla_qk_bwd(d_qc, d_kc, pos):
    consts = _rope_consts()

    def fn(r, c):
        dq, dk, pos_ = r
        even = lambda t: jnp.concatenate([t[:, (2 * h) * LANES:(2 * h + 1) * LANES] for h in range(N_HEADS)], axis=1)
        odd = lambda t: jnp.concatenate([t[:, (2 * h + 1) * LANES:(2 * h + 2) * LANES] for h in range(N_HEADS)], axis=1)
        d_qr_raw = _rope_transpose(odd(dq), _rope_tables(pos_, c[0], D_MODEL)) * SCALE
        dkr = dk[:, LANES:2 * LANES]
        for h in range(1, N_HEADS):
            dkr = dkr + dk[:, (2 * h + 1) * LANES:(2 * h + 2) * LANES]
        return [jnp.concatenate([even(dq) * SCALE, d_qr_raw], axis=1), even(dk),
                _rope_transpose(dkr, _rope_tables(pos_, c[0], LANES))], []

    return _rowwise(fn, [d_qc, d_kc, pos], [consts], [(2 * D_MODEL, BF16), (D_MODEL, BF16), (LANES, BF16)],
                    name="mla_qk_bwd")


def _causal_mask_t(st, key0, query0):
    key = lax.broadcasted_iota(jnp.int32, st.shape, 0) + key0
    query = lax.broadcasted_iota(jnp.int32, st.shape, 1) + query0
    return jnp.where(key <= query, st, NEG_BIG)


def _attn_tiles(s_dim):
    tq = min(512, s_dim)
    n_chains = 2 if s_dim >= 2 * tq else 1
    return tq, n_chains, min(512, s_dim)


def _diagonal_chains(t, tq, n_chains, tk):
    return [(c, (t + 1) * tk - 1 > c * tq) for c in range(n_chains) if t * tk < (c + 1) * tq]


def _attn_fwd(qc, kc, vt):
    s_dim = qc.shape[0]
    tq, n_chains, tk = _attn_tiles(s_dim)
    tqs = tq * n_chains

    def body(q_ref, k_ref, vt_ref, o_ref, lse_ref, m_s, l_s, acc):
        qi = pl.program_id(1)
        m_s[...] = jnp.full_like(m_s, NEG_BIG)
        l_s[...] = jnp.zeros_like(l_s)
        acc[...] = jnp.zeros_like(acc)

        def make_step(chains):
            def step(j, carry):
                ks = pl.multiple_of(j * tk, tk)
                kb, vtb = k_ref[pl.ds(ks, tk), :], vt_ref[:, pl.ds(ks, tk)]
                cols = [slice(c * tq, (c + 1) * tq) for c, _ in chains]
                sts = [_dot(kb, q_ref[cs, :], NT) for cs in cols]
                sts = [_causal_mask_t(st, j * tk, qi * tqs + c * tq) if masked else st
                       for st, (c, masked) in zip(sts, chains)]
                m_prevs = [m_s[:, cs] for cs in cols]
                m_news = [jnp.maximum(mp, jnp.max(st, axis=0, keepdims=True)) for mp, st in zip(m_prevs, sts)]
                alphas = [jnp.exp(mp - mn) for mp, mn in zip(m_prevs, m_news)]
                pts = [jnp.exp(st - mn) for st, mn in zip(sts, m_news)]
                pvs = [_dot(vtb, pt) for pt in pts]
                for cs, mn, al, pt, pv in zip(cols, m_news, alphas, pts, pvs):
                    l_s[:, cs] = al * l_s[:, cs] + _colsum(pt)
                    m_s[:, cs] = mn
                    acc[:, cs] = acc[:, cs] * al + pv
                return carry
            return step

        below = qi * (tqs // tk)
        lax.fori_loop(0, below, make_step([(c, False) for c in range(n_chains)]), 0)
        for t in range(tqs // tk):
            make_step(_diagonal_chains(t, tq, n_chains, tk))(below + t, 0)
        l = l_s[...]
        o_ref[...] = jnp.transpose(acc[...] / l)
        lse_ref[...] = m_s[...] + jnp.log(l)

    return pl.pallas_call(
        body,
        out_shape=[jax.ShapeDtypeStruct((s_dim, N_HEADS * HEAD), F32), jax.ShapeDtypeStruct((N_HEADS, 1, s_dim), F32)],
        grid=(N_HEADS, s_dim // tqs),
        in_specs=[pl.BlockSpec((tqs, QK_HEAD), lambda h, qi: (qi, h)),
                  pl.BlockSpec((s_dim, QK_HEAD), lambda h, qi: (0, h)),
                  pl.BlockSpec((HEAD, s_dim), lambda h, qi: (h, 0))],
        out_specs=[pl.BlockSpec((tqs, HEAD), lambda h, qi: (qi, h)),
                   pl.BlockSpec((None, 1, tqs), lambda h, qi: (h, 0, qi))],
        scratch_shapes=[pltpu.VMEM((1, tqs), F32), pltpu.VMEM((1, tqs), F32), pltpu.VMEM((HEAD, tqs), F32)],
        compiler_params=pltpu.CompilerParams(dimension_semantics=("parallel", "parallel")),
        name="attn_fwd",
    )(qc, kc, vt)


def _attn_bwd(qc, kc, kct, v, o, d_o, lse):
    s_dim = qc.shape[0]
    tq, n_chains, tk = _attn_tiles(s_dim)
    tqs = tq * n_chains

    def body(q_ref, k_ref, kt_ref, v_ref, o_ref, do_ref, lse_ref, dq_ref, dk_ref, dv_ref, dqt_acc, dv_acc):
        qi = pl.program_id(1)

        @pl.when(qi == 0)
        def _():
            dk_ref[...] = jnp.zeros_like(dk_ref)
            dv_acc[...] = jnp.zeros_like(dv_acc)

        dqt_acc[...] = jnp.zeros_like(dqt_acc)
        do_f = do_ref[...]
        do_all = do_f.astype(BF16)
        q_all = q_ref[...]
        lse_row = lse_ref[...]
        delta_row = _dot3(jnp.ones((8, HEAD), F32), o_ref[...] * do_f, NT)[0:1, :]

        def make_step(chains):
            rows = slice(chains[0][0] * tq, (chains[-1][0] + 1) * tq)

            def step(j, carry):
                ks = pl.multiple_of(j * tk, tk)
                kb, vb, ktb = k_ref[pl.ds(ks, tk), :], v_ref[pl.ds(ks, tk), :], kt_ref[:, pl.ds(ks, tk)]
                cols = [slice(c * tq, (c + 1) * tq) for c, _ in chains]
                sts = [_dot(kb, q_all[cs, :], NT) for cs in cols]
                sts = [_causal_mask_t(st, j * tk, qi * tqs + c * tq) if masked else st
                       for st, (c, masked) in zip(sts, chains)]
                dpts = [_dot(vb, do_all[cs, :], NT) for cs in cols]
                pts = [jnp.exp(st - lse_row[:, cs]) for st, cs in zip(sts, cols)]
                dsts = [(pt * (dpt - delta_row[:, cs])).astype(BF16) for pt, dpt, cs in zip(pts, dpts, cols)]
                pts = [pt.astype(BF16) for pt in pts]
                dqs = [_dot(ktb, dst) for dst in dsts]
                for cs, dq in zip(cols, dqs):
                    dqt_acc[:, cs] += dq
                pt_all = jnp.concatenate(pts, axis=1) if len(chains) > 1 else pts[0]
                dst_all = jnp.concatenate(dsts, axis=1) if len(chains) > 1 else dsts[0]
                dk_ref[pl.ds(ks, tk), :] += _dot(dst_all, q_all[rows, :])
                dv_acc[pl.ds(ks, tk), :] += _dot(pt_all, do_all[rows, :])
                return carry
            return step

        below = qi * (tqs // tk)
        lax.fori_loop(0, below, make_step([(c, False) for c in range(n_chains)]), 0)
        for t in range(tqs // tk):
            make_step(_diagonal_chains(t, tq, n_chains, tk))(below + t, 0)
        dq_ref[...] = jnp.transpose(dqt_acc[...])

        @pl.when(qi == s_dim // tqs - 1)
        def _():
            dv_ref[...] = dv_acc[...].astype(dv_ref.dtype)

    q_spec = pl.BlockSpec((tqs, QK_HEAD), lambda h, qi: (qi, h))
    o_spec = pl.BlockSpec((tqs, HEAD), lambda h, qi: (qi, h))
    k_spec = pl.BlockSpec((s_dim, QK_HEAD), lambda h, qi: (0, h))
    v_spec = pl.BlockSpec((s_dim, HEAD), lambda h, qi: (0, h))
    wide2 = jax.ShapeDtypeStruct((s_dim, N_HEADS * QK_HEAD), F32)
    return pl.pallas_call(
        body,
        out_shape=[wide2, wide2, jax.ShapeDtypeStruct((s_dim, N_HEADS * HEAD), BF16)],
        grid=(N_HEADS, s_dim // tqs),
        in_specs=[q_spec, k_spec, pl.BlockSpec((QK_HEAD, s_dim), lambda h, qi: (h, 0)), v_spec, o_spec, o_spec,
                  pl.BlockSpec((None, 1, tqs), lambda h, qi: (h, 0, qi))],
        out_specs=[q_spec, k_spec, v_spec],
        scratch_shapes=[pltpu.VMEM((QK_HEAD, tqs), F32), pltpu.VMEM((s_dim, HEAD), F32)],
        compiler_params=pltpu.CompilerParams(dimension_semantics=("parallel", "arbitrary")),
        name="attn_bwd",
    )(qc, kc, kct, v, o, d_o, lse)


def _merge_fwd(y_dn, y_mla, proj_g):
    def fn(r, c):
        yd, ym, g = r
        return [_sig(g[:, :D_MODEL]) * yd + _sig(g[:, D_MODEL:]) * ym], []

    return _rowwise(fn, [y_dn, y_mla, proj_g], [], [(D_MODEL, BF16)], name="merge_fwd")[0]


def _merge_bwd(y_dn, y_mla, proj_g, d_mixed):
    def fn(r, c):
        yd, ym, g, dm = r
        sd, sm = _sig(g[:, :D_MODEL]), _sig(g[:, D_MODEL:])
        d_g = jnp.concatenate([dm * yd * sd * (1.0 - sd), dm * ym * sm * (1.0 - sm)], axis=1)
        return [d_g, dm * sd, dm * sm], []

    return _rowwise(fn, [y_dn, y_mla, proj_g, d_mixed], [], [(2 * D_MODEL, BF16), (D_MODEL, BF16), (D_MODEL, BF16)],
                    name="merge_bwd")


def _ln_stats(z):
    mu = _rowmean(z)
    zc = z - mu
    r = lax.rsqrt(_rowmean(zc * zc) + EPS_LN)
    return zc * r, r


def _ln_bwd(dy, xh, r, g):
    dxh = dy * g
    return r * (dxh - _rowmean(dxh) - xh * _rowmean(dxh * xh))


def _ln1_fwd(x, a1, g, b):
    def fn(r, c):
        xh, _ = _ln_stats(ALPHA * r[0] + r[1])
        y = xh * c[0] + c[1]
        return [y, y], []

    return _rowwise(fn, [x, a1], [g, b], [(D_MODEL, F32), (D_MODEL, BF16)], name="ln1_fwd")


def _ln1_bwd(x, a1, d_h1, g):
    def fn(r, c):
        xh, rr = _ln_stats(ALPHA * r[0] + r[1])
        dy = r[2]
        dz = _ln_bwd(dy, xh, rr, c[0])
        return [dz, ALPHA * dz], [_colsum(dy * xh), _colsum(dy)]

    return _rowwise(fn, [x, a1, d_h1], [g], [(D_MODEL, BF16), (D_MODEL, F32)], accs=[(1, D_MODEL), (1, D_MODEL)],
                    name="ln1_bwd")


def _act_fwd(gu):
    def fn(r, c):
        gt, up = r[0][:, :FFN_HIDDEN], r[0][:, FFN_HIDDEN:]
        return [gt * _sig(gt) * up], []

    return _rowwise(fn, [gu], [], [(FFN_HIDDEN, BF16)], name="act_fwd")[0]


def _act_bwd(gu, d_act):
    def fn(r, c):
        gt, up = r[0][:, :FFN_HIDDEN], r[0][:, FFN_HIDDEN:]
        da = r[1]
        return [jnp.concatenate([da * up * _silu_grad(gt), da * gt * _sig(gt)], axis=1)], []

    return _rowwise(fn, [gu, d_act], [], [(2 * FFN_HIDDEN, BF16)], name="act_bwd")[0]


def _tail(h1, ffn, pg, pp, tgt, g, b):
    def fn(r, c):
        h1_, ffn_, pg_, pp_, t_ = r
        sp = _sig(pg_)
        xh, rr = _ln_stats(ALPHA * h1_ + ffn_ + sp * pp_)
        y = xh * c[0] + c[1]
        err = y - t_
        dy = err * (1.0 / D_MODEL)
        dz = _ln_bwd(dy, xh, rr, c[0])
        loss = jnp.sum(0.5 * _rowmean(err * err), axis=0, keepdims=True)
        return ([dz, dz * pp_ * sp * (1.0 - sp), dz * sp, ALPHA * dz],
                [_colsum(dy * xh), _colsum(dy), jnp.broadcast_to(loss, (1, LANES))])

    return _rowwise(fn, [h1, ffn, pg, pp, tgt], [g, b], [(D_MODEL, BF16)] * 3 + [(D_MODEL, F32)],
                    accs=[(1, D_MODEL), (1, D_MODEL), (1, LANES)], name="tail")


def _local_step(x, p, pos, tgt, w, late_weights, emit):
    w = dict(w)
    s_dim = x.shape[0]
    xb, pb = x.astype(BF16), p.astype(BF16)
    proj_a = _mm(xb, w["wa"], name="f_proj_a")
    proj_g = _mm(xb, w["wg"], name="f_proj_g")
    proj_b = _mm(xb, w["wb"], name="f_proj_b")
    qkvn = _conv_fwd(proj_a, w["conv"])
    beta, gc = _gates_fwd(proj_b, w["alog"], w["dtb"])
    gc_t = jnp.transpose(gc[:, :N_HEADS])
    u, w_, qd, kt, a_mat, t_fold = _gdr_prep_fwd(qkvn, beta, gc, gc_t)
    o_dn, states = _gdr_scan_fwd(u, w_, qd, kt, a_mat, gc)
    og = _gdr_out_fwd(o_dn, proj_a, w["dnw"])
    w.update(late_weights("mix", og))
    y_dn = _mm(og, w["br_dn"], name="f_y_dn")
    c_q, c_kv = _mla_norm_fwd(proj_b, w["qnw"], w["kvnw"])
    q_full = _mm(c_q, w["uq"], name="f_q_full")
    k_nope = _mm(c_kv, w["uk"], name="f_k_nope")
    vv = _mm(c_kv, w["uv"], out_dtype=BF16, name="f_v")
    qc, kc = _mla_qk_fwd(q_full, k_nope, proj_b, pos)
    o_mla, lse = _attn_fwd(qc, kc, jnp.transpose(vv))
    y_mla = _mm(o_mla, w["br_mla"], name="f_y_mla")
    mixed = _merge_fwd(y_dn, y_mla, proj_g)
    a1 = _mm(mixed, w["wo"], name="f_a1")
    w.update(late_weights("ffn", a1))
    h1, h1b = _ln1_fwd(x, a1, w["ln1g"], w["ln1b"])
    gu = _mm(h1b, w["ffn_in"], name="f_gu")
    act = _act_fwd(gu)
    ffn = _mm(act, w["ffn_out"], name="f_ffn")
    pg = _mm(h1b, w["ple_gate"], name="f_pg")
    pp = _mm(pb, w["ple"], name="f_pp")
    g = {}
    dz2, d_pg, d_pp, dh1a, g["ln2g"], g["ln2b"], loss = _tail(h1, ffn, pg, pp, tgt, w["ln2g"], w["ln2b"])
    g["ple_t"] = _mm(d_pp, pb, ta=True, out_dtype=BF16, name="b_w_ple")
    g["ple_gate"] = _mm(h1b, d_pg, ta=True, out_dtype=BF16, name="b_w_ple_gate")
    g["ffn_out"] = _mm(act, dz2, ta=True, out_dtype=BF16, name="b_w_ffn_out")
    d_act = _mm(dz2, w["ffn_out"], tb=True, name="b_act")
    d_gu = _act_bwd(gu, d_act)
    g["ffn_in_t"] = _mm(d_gu, h1b, ta=True, out_dtype=BF16, name="b_w_ffn_in")
    d_gu = emit("ffn", g, d_gu)
    d_h1 = _mm(d_gu, w["ffn_in_t"], add=(dh1a,), name="b_h1_ffn")
    d_h1 = _mm(d_pg, w["ple_gate"], tb=True, add=(d_h1,), name="b_h1_ple")
    dz1, dxa, g["ln1g"], g["ln1b"] = _ln1_bwd(x, a1, d_h1, w["ln1g"])
    g["wo"] = _mm(mixed, dz1, ta=True, out_dtype=BF16, name="b_w_o")
    d_mixed = _mm(dz1, w["wo"], tb=True, name="b_mixed")
    d_proj_g, d_y_dn, d_y_mla = _merge_bwd(y_dn, y_mla, proj_g, d_mixed)
    g["br_mla"] = _mm(o_mla, d_y_mla, ta=True, out_dtype=BF16, name="b_w_br_mla")
    d_o_mla = _mm(d_y_mla, w["br_mla"], tb=True, name="b_o_mla")
    d_qc, d_kc, d_v = _attn_bwd(qc, kc, jnp.transpose(kc), vv, o_mla, d_o_mla, lse)
    d_q_full, d_kn, d_kr = _mla_qk_bwd(d_qc, d_kc, pos)
    g["uq"] = _mm(c_q, d_q_full, ta=True, out_dtype=BF16, name="b_w_uq")
    d_c_q = _mm(d_q_full, w["uq"], tb=True, name="b_c_q")
    g["uk"] = _mm(c_kv, d_kn, ta=True, out_dtype=BF16, name="b_w_uk")
    g["uv"] = _mm(c_kv, d_v, ta=True, out_dtype=BF16, name="b_w_uv")
    d_c_kv = _mm(d_kn, w["uk"], tb=True, name="b_c_kv_k")
    d_c_kv = _mm(d_v, w["uv"], tb=True, add=(d_c_kv,), name="b_c_kv_v")
    d_cq, d_ckv, g["qnw"], g["kvnw"] = _mla_norm_bwd(proj_b, w["qnw"], w["kvnw"], d_c_q, d_c_kv)
    g["br_dn"] = _mm(og, d_y_dn, ta=True, out_dtype=BF16, name="b_w_br_dn")
    d_og = emit("mix", g, _mm(d_y_dn, w["br_dn"], tb=True, name="b_og"))
    d_o_dn, d_z, g["dnw"] = _gdr_out_bwd(o_dn, proj_a, d_og, w["dnw"])
    du, dw, dqd, dkt, d_a, d_egl = _gdr_scan_bwd(u, w_, qd, kt, a_mat, gc, states, d_o_dn)
    dq, dk, dv, d_beta, d_gc = _gdr_prep_bwd(qkvn, beta, gc, gc_t, t_fold, u, w_, du, dw, dqd, dkt, d_a)
    d_egl_rows = jnp.pad(d_egl[:, None, :, 0], ((0, 0), (CHUNK - 1, 0), (0, LANES - N_HEADS))).reshape(s_dim, LANES)
    d_ba, g["alog"], g["dtb"] = _gates_bwd(proj_b, w["alog"], w["dtb"], gc, d_beta, d_gc, d_egl_rows)
    d_qkv, g["conv"] = _conv_bwd(proj_a, w["conv"], dq, dk, dv)
    zeros = jnp.zeros((s_dim, WB_CKV - Q_LORA), BF16)
    d_proj_b = jnp.concatenate([d_cq, zeros, d_ckv, d_kr, d_ba], axis=1)
    g["wa_qkv_t"] = _mm(d_qkv, xb, ta=True, name="b_w_qkv")
    g["wa_z_t"] = _mm(d_z, xb, ta=True, name="b_w_z")
    g["wg_t"] = _mm(d_proj_g, xb, ta=True, name="b_w_g")
    g["wb_t"] = _mm(d_proj_b, xb, ta=True, name="b_w_b")
    dx = _mm(d_qkv, w["wa_qkv_t"], add=(dxa,), name="b_x_qkv")
    dx = _mm(d_z, w["wa_z_t"], add=(dx,), name="b_x_z")
    dx = _mm(d_proj_g, w["wg_t"], add=(dx,), name="b_x_g")
    dx = _mm(d_proj_b, w["wb_t"], add=(dx,), name="b_x_b")
    return loss, dx, g


_BIG = (("w_in", 1), ("w_uq", 0), ("w_uk", 0), ("w_uv", 0), ("w_br_dn", 0), ("w_br_mla", 0),
        ("w_o", 0), ("w_ffn_in", 1), ("w_ffn_out", 0), ("w_ple", 1), ("w_ple_gate", 0))
_BIG_AXIS = dict(_BIG)
_SMALL = ("ln1_g", "ln1_b", "ln2_g", "ln2_b", "q_norm_w", "kv_norm_w", "dn_norm_w", "dn_a_log", "dn_dt_bias")
_ORDER = ("w_in", "conv_w", "dn_a_log", "dn_dt_bias", "dn_norm_w", "q_norm_w", "w_uq", "kv_norm_w", "w_uk", "w_uv",
          "w_br_dn", "w_br_mla", "w_o", "ln1_g", "ln1_b", "w_ffn_in", "w_ffn_out", "w_ple", "w_ple_gate", "ln2_g",
          "ln2_b")


def _stored_shape(name, shard_shape):
    axis = _BIG_AXIS[name]
    lead = shard_shape[axis]
    return lead, int(np.prod(shard_shape)) // lead


def _to_stored(name, shard):
    return jnp.moveaxis(shard, _BIG_AXIS[name], 0).reshape(_stored_shape(name, shard.shape))


def _from_stored(name, stored, shard_shape):
    axis = _BIG_AXIS[name]
    moved = (shard_shape[axis],) + shard_shape[:axis] + shard_shape[axis + 1:]
    return jnp.moveaxis(stored.reshape(moved), 0, axis)


_W_IN_ROWS = np.cumsum([0, 3072, 1024, 8, 8, Q_LORA, KV_LORA, ROPE, D_MODEL, D_MODEL])


def _first_weights(w_in_t, conv_full, small):
    r = _W_IN_ROWS
    zr = lambda n: jnp.zeros((n, D_MODEL), w_in_t.dtype)
    w = {}
    w["wa_t"] = w_in_t[r[0]:r[2]]
    w["wa_qkv_t"], w["wa_z_t"] = w_in_t[r[0]:r[1]], w_in_t[r[1]:r[2]]
    w["wg_t"] = w_in_t[r[7]:r[9]]
    w["wb_t"] = jnp.concatenate([w_in_t[r[4]:r[5]], zr(WB_CKV - Q_LORA), w_in_t[r[5]:r[7]], zr(LANES - ROPE),
                                 w_in_t[r[2]:r[4]], zr(LANES - 2 * N_HEADS)], axis=0)
    for k_ in ("wa", "wg", "wb"):
        w[k_] = jnp.transpose(w[k_ + "_t"])
    w["conv"] = conv_full
    pad_l = lambda v: jnp.pad(v, ((0, 0), (0, LANES - v.shape[1])))
    w["alog"], w["dtb"] = pad_l(small["dn_a_log"]), pad_l(small["dn_dt_bias"])
    w["dnw"], w["qnw"], w["kvnw"] = small["dn_norm_w"], small["q_norm_w"], small["kv_norm_w"]
    w["ln1g"], w["ln1b"], w["ln2g"], w["ln2b"] = small["ln1_g"], small["ln1_b"], small["ln2_g"], small["ln2_b"]
    return w


def _late_weights(group, fw):
    w = {}
    if group == "mix":
        uq = fw["w_uq"].reshape(Q_LORA, N_HEADS, HEAD + ROPE)
        uq_r = jnp.pad(uq[:, :, HEAD:], ((0, 0), (0, 0), (0, HEAD - ROPE)))
        w["uq"] = jnp.concatenate([uq[:, :, :HEAD].reshape(Q_LORA, -1), uq_r.reshape(Q_LORA, -1)], axis=1)
        w["uk"], w["uv"] = fw["w_uk"], fw["w_uv"]
        w["br_dn"], w["br_mla"], w["wo"] = fw["w_br_dn"], fw["w_br_mla"], fw["w_o"]
    else:
        w["ffn_in_t"], w["ffn_out"] = fw["w_ffn_in"], fw["w_ffn_out"]
        w["ple_t"], w["ple_gate"] = fw["w_ple"], fw["w_ple_gate"]
        for k_ in ("ffn_in", "ple"):
            w[k_] = jnp.transpose(w[k_ + "_t"])
    return w


_GROUP_GRADS = {"ffn": (("w_ple", "ple_t"), ("w_ple_gate", "ple_gate"), ("w_ffn_out", "ffn_out"),
                        ("w_ffn_in", "ffn_in_t")),
                "mix": (("w_o", "wo"), ("w_br_mla", "br_mla"), ("w_uq", "uq"), ("w_uk", "uk"), ("w_uv", "uv"),
                        ("w_br_dn", "br_dn"))}


def _group_grads(group, g):
    out = {}
    for name, key in _GROUP_GRADS[group]:
        t = g[key]
        if name == "w_uq":
            uq_n = t[:, :D_MODEL].reshape(Q_LORA, N_HEADS, HEAD)
            uq_r = t[:, D_MODEL:].reshape(Q_LORA, N_HEADS, HEAD)[:, :, :ROPE]
            t = jnp.concatenate([uq_n, uq_r], axis=2).reshape(Q_LORA, -1)
        out[name] = t
    return out


def _last_grads(g):
    wb = g["wb_t"]
    w_in = jnp.concatenate([
        g["wa_qkv_t"], g["wa_z_t"], wb[WB_BA:WB_BA + 2 * N_HEADS], wb[WB_CQ:WB_CQ + Q_LORA],
        wb[WB_CKV:WB_CKV + KV_LORA], wb[WB_KR:WB_KR + ROPE], g["wg_t"]], axis=0)
    small = {"ln1_g": g["ln1g"], "ln1_b": g["ln1b"], "ln2_g": g["ln2g"], "ln2_b": g["ln2b"], "q_norm_w": g["qnw"],
             "kv_norm_w": g["kvnw"], "dn_norm_w": g["dnw"], "dn_a_log": g["alog"], "dn_dt_bias": g["dtb"],
             "conv_w": g["conv"]}
    return w_in, small


_SMALL_SLOTS = {"ln1_g": (0, 0, 1024), "ln1_b": (1, 0, 1024), "ln2_g": (2, 0, 1024), "ln2_b": (3, 0, 1024),
                "q_norm_w": (4, 0, 384), "kv_norm_w": (4, 384, 256), "dn_norm_w": (4, 640, 128),
                "dn_a_log": (4, 768, 8), "dn_dt_bias": (4, 896, 8)}
_SMALL_ROWS, _LOSS_ROW, _CONV_ROW0, _CONV_ROWS = 24, 5, 8, 12


def _pack_small_grads(small_g, loss):
    zeros = lambda r, c: jnp.zeros((r, c), F32)
    row4 = jnp.concatenate([small_g["q_norm_w"], small_g["kv_norm_w"], small_g["dn_norm_w"], small_g["dn_a_log"],
                            small_g["dn_dt_bias"]], axis=1)
    row5 = jnp.concatenate([loss, zeros(1, FLAT_COLS - LANES)], axis=1)
    head = jnp.concatenate([small_g["ln1_g"], small_g["ln1_b"], small_g["ln2_g"], small_g["ln2_b"], row4, row5,
                            zeros(2, FLAT_COLS)], axis=0)
    conv = small_g["conv_w"].reshape(_CONV_ROWS, FLAT_COLS)
    return jnp.concatenate([head, conv, zeros(_SMALL_ROWS - _CONV_ROW0 - _CONV_ROWS, FLAT_COLS)], axis=0)


_MESH_ID = pl.DeviceIdType.MESH
_ANY = pl.BlockSpec(memory_space=pl.ANY)


def _all_gather(blocks, name):
    n = len(blocks)

    def body(*refs):
        x_refs, out_refs = refs[:n], refs[n:2 * n]
        send_sems, recv_sems, local_sems = refs[2 * n:]
        x, y, c = lax.axis_index("x"), lax.axis_index("y"), lax.axis_index("c")
        me, sibling = (x, y, c), (x, y, 1 - c)
        chips = [(1 - x, y), (x, 1 - y), (1 - x, 1 - y)]

        def slot(i, px, py, pc):
            return out_refs[i].at[4 * px + 2 * py + pc]

        def copy(i, k, origin, to, src=None):
            return pltpu.make_async_remote_copy(
                src_ref=slot(i, *origin) if src is None else src, dst_ref=slot(i, *origin),
                send_sem=send_sems.at[7 * i + k], recv_sem=recv_sems.at[7 * i + k], device_id=to,
                device_id_type=_MESH_ID)

        mine = [pltpu.make_async_copy(x_refs[i], slot(i, *me), local_sems.at[i]) for i in range(n)]
        first, passed = [], []
        for i in range(n):
            mine[i].start()
            first.append(copy(i, 0, me, sibling, src=x_refs[i]))
            first += [copy(i, 1 + j, me, (*chip, c), src=x_refs[i]) for j, chip in enumerate(chips)]
        for cp in first:
            cp.start()
        for i in range(n):
            for j, chip in enumerate(chips):
                copy(i, 1 + j, (*chip, c), me).wait_recv()
                passed.append(copy(i, 4 + j, (*chip, c), sibling))
                passed[-1].start()
        for i in range(n):
            copy(i, 0, sibling, me).wait_recv()
            for j, chip in enumerate(chips):
                copy(i, 4 + j, (*chip, 1 - c), me).wait_recv()
        for cp in first + passed:
            cp.wait_send()
        for cp in mine:
            cp.wait()

    return pl.pallas_call(
        body,
        out_shape=[jax.ShapeDtypeStruct((N_DEV,) + b.shape, b.dtype) for b in blocks],
        in_specs=[_ANY] * n,
        out_specs=[_ANY] * n,
        scratch_shapes=[pltpu.SemaphoreType.DMA((7 * n,)), pltpu.SemaphoreType.DMA((7 * n,)),
                        pltpu.SemaphoreType.DMA((n,))],
        name=name,
    )(*blocks)


def _exchange_sibling(srcs, name):
    n = len(srcs)

    def body(*refs):
        src_refs, dst_refs = refs[:n], refs[n:2 * n]
        send_sems, recv_sems = refs[2 * n:]
        x, y, c = lax.axis_index("x"), lax.axis_index("y"), lax.axis_index("c")
        copies = [pltpu.make_async_remote_copy(
            src_ref=src_refs[i].at[2 * q + (1 - c)], dst_ref=dst_refs[i].at[q], send_sem=send_sems.at[4 * i + q],
            recv_sem=recv_sems.at[4 * i + q], device_id=(x, y, 1 - c), device_id_type=_MESH_ID)
            for i in range(n) for q in range(4)]
        for cp in copies:
            cp.start()
        for cp in copies:
            cp.wait_recv()
        for cp in copies:
            cp.wait_send()

    return pl.pallas_call(
        body,
        out_shape=[jax.ShapeDtypeStruct((4,) + s.shape[1:], s.dtype) for s in srcs],
        in_specs=[_ANY] * n,
        out_specs=[_ANY] * n,
        scratch_shapes=[pltpu.SemaphoreType.DMA((4 * n,)), pltpu.SemaphoreType.DMA((4 * n,))],
        name=name,
    )(*srcs)


def _exchange_chips(srcs, name):
    n = len(srcs)

    def body(*refs):
        src_refs, dst_refs = refs[:n], refs[n:2 * n]
        send_sems, recv_sems = refs[2 * n:]
        x, y, c = lax.axis_index("x"), lax.axis_index("y"), lax.axis_index("c")
        chips = [(1 - x, y), (x, 1 - y), (1 - x, 1 - y)]
        copies = [pltpu.make_async_remote_copy(
            src_ref=src_refs[i].at[2 * tx + ty], dst_ref=dst_refs[i].at[j], send_sem=send_sems.at[3 * i + j],
            recv_sem=recv_sems.at[3 * i + j], device_id=(tx, ty, c), device_id_type=_MESH_ID)
            for i in range(n) for j, (tx, ty) in enumerate(chips)]
        for cp in copies:
            cp.start()
        for cp in copies:
            cp.wait_recv()
        for cp in copies:
            cp.wait_send()

    return pl.pallas_call(
        body,
        out_shape=[jax.ShapeDtypeStruct((3,) + s.shape[1:], s.dtype) for s in srcs],
        in_specs=[_ANY] * n,
        out_specs=[_ANY] * n,
        scratch_shapes=[pltpu.SemaphoreType.DMA((3 * n,)), pltpu.SemaphoreType.DMA((3 * n,))],
        name=name,
    )(*srcs)


def _col_tile(c):
    return c if c <= 256 else 256


def _chip_sum(src, recv, parity, name):
    _, r, c = src.shape
    tc = _col_tile(c)

    def body(par_ref, a_ref, b_ref, o_ref, ob_ref):
        s = a_ref[...] + b_ref[...]
        o_ref[...] = s
        ob_ref[...] = s.astype(BF16)

    blk = lambda f: pl.BlockSpec((None, r, tc), f)
    return pl.pallas_call(
        body,
        out_shape=[jax.ShapeDtypeStruct((4, r, c), F32), jax.ShapeDtypeStruct((4, r, c), BF16)],
        grid_spec=pltpu.PrefetchScalarGridSpec(
            num_scalar_prefetch=1, grid=(4, c // tc),
            in_specs=[blk(lambda q, j, par: (2 * q + par[0], 0, j)), blk(lambda q, j, par: (q, 0, j))],
            out_specs=[blk(lambda q, j, par: (q, 0, j)), blk(lambda q, j, par: (q, 0, j))]),
        compiler_params=pltpu.CompilerParams(dimension_semantics=("parallel", "parallel")),
        name=name,
    )(parity, src, recv)


def _sum_parts(own, others, chip, name):
    _, r, c = own.shape
    tc = _col_tile(c)

    def body(q_ref, a_ref, b_ref, o_ref):
        o_ref[...] = ((a_ref[...] + b_ref[0].astype(F32)) + b_ref[1].astype(F32)) + b_ref[2].astype(F32)

    return pl.pallas_call(
        body,
        out_shape=jax.ShapeDtypeStruct((r, c), F32),
        grid_spec=pltpu.PrefetchScalarGridSpec(
            num_scalar_prefetch=1, grid=(c // tc,),
            in_specs=[pl.BlockSpec((None, r, tc), lambda j, q: (q[0], 0, j)),
                      pl.BlockSpec((3, r, tc), lambda j, q: (0, 0, j))],
            out_specs=pl.BlockSpec((r, tc), lambda j, q: (0, j))),
        compiler_params=pltpu.CompilerParams(dimension_semantics=("parallel",)),
        name=name,
    )(chip, own, others)


_HBM = pl.BlockSpec(memory_space=pltpu.HBM)
_SEM = pl.BlockSpec(memory_space=pltpu.SEMAPHORE)
_DATAFLOW = pltpu.SideEffectType.DATAFLOW_SIDE_EFFECTING
N_PEERS = N_DEV - 1


def _ring_peer(j):
    me = 4 * lax.axis_index("x") + 2 * lax.axis_index("y") + lax.axis_index("c")
    k = (me + j) % N_DEV
    return me, k, (k // 4, (k // 2) % 2, k % 2)


def _spread_copy(i, j, src_refs, land_refs, send_sems, recv_sems, scatter):
    me, k, peer = _ring_peer(j)
    return pltpu.make_async_remote_copy(
        src_ref=src_refs[i].at[k] if scatter else src_refs[i], dst_ref=land_refs[i].at[me],
        send_sem=send_sems.at[N_PEERS * i + j - 1], recv_sem=recv_sems.at[N_PEERS * i + j - 1], device_id=peer,
        device_id_type=_MESH_ID)


def _spread_start(srcs, carry, scatter, name):
    n = len(srcs)
    lands = [lax.empty(((N_DEV,) + s.shape[-2:]), s.dtype) for s in srcs]

    def body(*refs):
        src_refs, land_refs = refs[:n], refs[n:2 * n]
        send_sems, recv_sems, local_sems = refs[2 * n + 1:2 * n + 4]
        for i in range(n):
            for j in range(1, N_DEV):
                _spread_copy(i, j, src_refs, land_refs, send_sems, recv_sems, scatter).start()
        for i in range(n):
            _own_copy(i, src_refs, land_refs, local_sems, scatter).start()

    hbm = lambda a: pltpu.HBM(a.shape, a.dtype)
    sems = pltpu.SemaphoreType.DMA((N_PEERS * n,))
    pinned = [pltpu.with_memory_space_constraint(a, pltpu.HBM) for a in list(srcs) + lands + [carry]]
    res = pl.pallas_call(
        body, name=name,
        out_shape=(sems, sems, pltpu.SemaphoreType.DMA((n,)), *[hbm(a) for a in pinned]),
        in_specs=[_HBM] * (2 * n + 1),
        out_specs=(_SEM, _SEM, _SEM, *[_HBM] * (2 * n + 1)),
        input_output_aliases={i: 3 + i for i in range(2 * n + 1)},
        compiler_params=pltpu.CompilerParams(has_side_effects=_DATAFLOW),
    )(*pinned)
    return res[:3], list(res[3:3 + n]), list(res[3 + n:3 + 2 * n]), res[3 + 2 * n]


def _own_copy(i, src_refs, land_refs, local_sems, scatter):
    me = _ring_peer(0)[0]
    return pltpu.make_async_copy(src_refs[i].at[me] if scatter else src_refs[i], land_refs[i].at[me],
                                 local_sems.at[i])


def _spread_wait(started, after, scatter, name):
    sems, srcs, lands, _ = started
    n = len(srcs)

    def body(*refs):
        src_refs, land_refs = refs[:n], refs[n:2 * n]
        send_s, recv_s, local_s = refs[2 * n:2 * n + 3]
        for i in range(n):
            for j in range(1, N_DEV):
                cp = _spread_copy(i, j, src_refs, land_refs, send_s, recv_s, scatter)
                cp.wait_send()
                cp.wait_recv()
        for i in range(n):
            _own_copy(i, src_refs, land_refs, local_s, scatter).wait()

    hbm = lambda a: pltpu.HBM(a.shape, a.dtype)
    res = pl.pallas_call(
        body, name=name,
        out_shape=tuple(hbm(a) for a in srcs + lands),
        in_specs=[_HBM] * (2 * n) + [_SEM, _SEM, _SEM, pl.BlockSpec(memory_space=pl.ANY)],
        out_specs=tuple([_HBM] * (2 * n)),
        input_output_aliases={i: i for i in range(2 * n)},
        compiler_params=pltpu.CompilerParams(has_side_effects=_DATAFLOW),
    )(*srcs, *lands, *sems, after)
    return list(res[n:])


def _sum8(landing, name):
    _, r, c = landing.shape
    tc = _col_tile(c)

    def body(a_ref, o_ref):
        tot = a_ref[0].astype(F32)
        for k in range(1, N_DEV):
            tot = tot + a_ref[k].astype(F32)
        o_ref[...] = tot

    return pl.pallas_call(
        body,
        out_shape=jax.ShapeDtypeStruct((r, c), F32),
        grid=(c // tc,),
        in_specs=[pl.BlockSpec((N_DEV, r, tc), lambda j: (0, 0, j))],
        out_specs=pl.BlockSpec((r, tc), lambda j: (0, j)),
        compiler_params=pltpu.CompilerParams(dimension_semantics=("parallel",)),
        name=name,
    )(landing)


def _adamw_math(w, g, m, v):
    m = ADAM_B1 * m + (1.0 - ADAM_B1) * g
    v = ADAM_B2 * v + (1.0 - ADAM_B2) * (g * g)
    m_hat = m / (1.0 - ADAM_B1 ** ADAM_STEP)
    v_hat = v / (1.0 - ADAM_B2 ** ADAM_STEP)
    delta = -ADAM_LR * (m_hat / (jnp.sqrt(v_hat) + ADAM_EPS) + ADAM_WD * w)
    return delta, m, v


def _adamw(w, m, v, g, name):
    r, c = w.shape

    def fn(rows, consts):
        return list(_adamw_math(*rows)), []

    return _rowwise(fn, [w, g, m, v], [], [(c, F32)] * 3, tm=r if r <= 512 else 256, name=name)


def _adamw_small(gathered, params):
    ns = len(_SMALL)

    def body(*refs):
        g_ref, p_refs, o_refs = refs[0], refs[1:1 + 3 * ns], refs[1 + 3 * ns:]
        tot = g_ref[0]
        for k in range(1, N_DEV):
            tot = tot + g_ref[k]
        for i, name in enumerate(_SMALL):
            row, lane0, lanes = _SMALL_SLOTS[name]
            g = tot[row:row + 1, lane0:lane0 + lanes]
            w_, m_, v_ = (p_refs[3 * i + j][...] for j in range(3))
            delta, m2, v2 = _adamw_math(w_, g, m_, v_)
            for j, val in enumerate((g, delta, m2, v2)):
                o_refs[4 * i + j][...] = val
        o_refs[4 * ns][...] = tot[_LOSS_ROW:_LOSS_ROW + 1, 0:LANES]
        o_refs[4 * ns + 1][...] = tot[_CONV_ROW0:_CONV_ROW0 + _CONV_ROWS, :]

    out_shape = [jax.ShapeDtypeStruct(w.shape, F32) for (w, _, _) in params for _ in range(4)]
    out_shape += [jax.ShapeDtypeStruct((1, LANES), F32), jax.ShapeDtypeStruct((_CONV_ROWS, FLAT_COLS), F32)]
    flat = [a for wmv in params for a in wmv]
    return pl.pallas_call(body, out_shape=out_shape, name="adamw_small")(gathered, *flat)


def kernel(x, p, positions, w_in, conv_w, dn_a_log, dn_dt_bias, dn_norm_w, q_norm_w, w_uq, kv_norm_w, w_uk, w_uv, w_br_dn, w_br_mla, w_o, ln1_g, ln1_b, w_ffn_in, w_ffn_out, w_ple, w_ple_gate, ln2_g, ln2_b, loss_target, m_w_in, m_conv_w, m_dn_a_log, m_dn_dt_bias, m_dn_norm_w, m_q_norm_w, m_w_uq, m_kv_norm_w, m_w_uk, m_w_uv, m_w_br_dn, m_w_br_mla, m_w_o, m_ln1_g, m_ln1_b, m_w_ffn_in, m_w_ffn_out, m_w_ple, m_w_ple_gate, m_ln2_g, m_ln2_b, v_w_in, v_conv_w, v_dn_a_log, v_dn_dt_bias, v_dn_norm_w, v_q_norm_w, v_w_uq, v_kv_norm_w, v_w_uk, v_w_uv, v_w_br_dn, v_w_br_mla, v_w_o, v_ln1_g, v_ln1_b, v_w_ffn_in, v_w_ffn_out, v_w_ple, v_w_ple_gate, v_ln2_g, v_ln2_b):
    args = dict(locals())
    wts = {n: args[n] for n in _ORDER}
    mom1 = {n: args["m_" + n] for n in _ORDER}
    mom2 = {n: args["v_" + n] for n in _ORDER}
    big_names = [n for n, _ in _BIG]
    shard_shapes = {n: wts[n].shape[1:] for n in big_names}
    c_idx = lax.axis_index("c")
    q_idx = 2 * lax.axis_index("x") + lax.axis_index("y")
    parity, chip = c_idx.reshape(1).astype(jnp.int32), q_idx.reshape(1).astype(jnp.int32)

    stored = {n: _to_stored(n, wts[n][0]).astype(BF16) for n in big_names}
    first = _all_gather([stored["w_in"], conv_w[0]], "ag_first")
    group_names = {grp: [n for n, _ in pairs] for grp, pairs in _GROUP_GRADS.items()}
    carry, gathers = first[0], {}
    for grp in ("mix", "ffn"):
        gathers[grp] = _spread_start([stored[n] for n in group_names[grp]], carry, False, "ag_start_" + grp)
        carry = gathers[grp][3]
    conv_full = jnp.moveaxis(first[1], 0, 1).reshape(conv_w.shape[1], -1)
    small_w = {n: wts[n].astype(F32) for n in _SMALL}
    w = _first_weights(carry.reshape(-1, D_MODEL), conv_full, small_w)

    def late_weights(grp, after):
        got = _spread_wait(gathers[grp], after, False, "ag_wait_" + grp)
        return _late_weights(grp, {n: t.reshape(-1, t.shape[-1]) for n, t in zip(group_names[grp], got)})

    started = {}

    def emit(group, g, carry):
        grads = _group_grads(group, g)
        srcs = [grads[n].reshape((N_DEV,) + _stored_shape(n, shard_shapes[n])) for n in grads]
        started[group] = (list(grads), _spread_start(srcs, carry, True, "rs_start_" + group))
        return started[group][1][3]

    s_dim = x.shape[1]
    loss, dx, g = _local_step(x[0], p[0, 0], positions.reshape(s_dim, 1).astype(F32), loss_target[0], w,
                              late_weights, emit)
    g_w_in, small_g = _last_grads(g)

    src = g_w_in.reshape((N_DEV,) + _stored_shape("w_in", shard_shapes["w_in"]))
    from_sibling = _exchange_sibling([src], "rs_sibling")[0]
    own, own_bf = _chip_sum(src, from_sibling, parity, "rs_sum_w_in")
    from_chips = _exchange_chips([own_bf], "rs_chips")[0]

    out_g, out_d, out_m, out_v = {}, {}, {}, {}

    def update(n, grad, shp):
        flat2 = (shp[0], int(np.prod(shp[1:])))
        d, m2, v2 = _adamw(wts[n][0].reshape(flat2), mom1[n][0].reshape(flat2), mom2[n][0].reshape(flat2),
                           grad.reshape(flat2), "adamw_" + n)
        out_g[n], out_d[n], out_m[n], out_v[n] = grad, d.reshape(shp), m2.reshape(shp), v2.reshape(shp)

    total = _sum_parts(own, from_chips, chip, "rs_total_w_in")
    update("w_in", _from_stored("w_in", total, shard_shapes["w_in"]), shard_shapes["w_in"])
    for group, (names, st) in started.items():
        for n, landing in zip(names, _spread_wait(st, dx, True, "rs_wait_" + group)):
            update(n, _from_stored(n, _sum8(landing, "rs_total_" + n), shard_shapes[n]), shard_shapes[n])

    g_small = _all_gather([_pack_small_grads(small_g, loss)], "ag_small")[0]
    res = _adamw_small(g_small, [(wts[n], mom1[n], mom2[n]) for n in _SMALL])
    for i, n in enumerate(_SMALL):
        out_g[n], out_d[n], out_m[n], out_v[n] = res[4 * i:4 * i + 4]
    loss_out = res[4 * len(_SMALL)][0, 0]
    conv_shape = conv_w.shape[1:]
    conv_g = lax.dynamic_slice(res[-1].reshape(conv_shape[0], -1), (0, (2 * q_idx + c_idx) * conv_shape[1]),
                               conv_shape)
    update("conv_w", conv_g, conv_shape)

    expand = lambda d, n: d[n] if n in _SMALL else d[n][None]
    return (loss_out, dx[None], *[expand(out_g, n) for n in _ORDER], *[expand(out_d, n) for n in _ORDER],
            *[expand(out_m, n) for n in _ORDER], *[expand(out_v, n) for n in _ORDER])
```

```python
import functools

import numpy as np
import jax
import jax.numpy as jnp
from jax import lax
from jax.experimental import pallas as pl
from jax.experimental.pallas import tpu as pltpu

F32 = jnp.float32
BF16 = jnp.bfloat16

D_MODEL = 1024
N_HEADS = 8
HEAD = 128
CHUNK = 64
GROUP = 256
ROPE = 64
Q_LORA = 384
KV_LORA = 256
FFN_HIDDEN = 2816
PLE_DIM = 256
ROPE_BASE = 10000.0
ALPHA = 2.0 ** 0.25
SCALE = float((HEAD + ROPE) ** -0.5)
NEG_BIG = -1e30
EPS_RMS = 1e-6
EPS_LN = 1e-5

ADAM_LR = 0.001
ADAM_B1 = 0.9
ADAM_B2 = 0.999
ADAM_EPS = 1e-08
ADAM_WD = 0.01
ADAM_STEP = 10

N_DEV = 8
LANES = 128
FLAT_COLS = 1024

WB_CQ, WB_CKV, WB_KR, WB_BA, WB_COLS = 0, 512, 768, 896, 1024

HIGHEST = lax.Precision.HIGHEST

NN = (((1,), (0,)), ((), ()))
TN = (((0,), (0,)), ((), ()))
NT = (((1,), (1,)), ((), ()))


def _dot(a, b, dims=NN):
    return lax.dot_general(a.astype(BF16), b.astype(BF16), dims, preferred_element_type=F32)


def _dot32(a, b, dims=NN):
    return lax.dot_general(a, b, dims, precision=HIGHEST, preferred_element_type=F32)


def _sig(x):
    return 1.0 / (1.0 + jnp.exp(-x))


MM_TILE = 1536


def _pick_wide(n):
    if n <= MM_TILE:
        return n
    return max(t for t in range(LANES, MM_TILE + 1, LANES) if n % t == 0)


def _split_bf16(a):
    hi = a.astype(BF16)
    return hi, (a - hi.astype(F32)).astype(BF16)


def _dot3(a, b, dims=NN):
    ah, al = a if isinstance(a, tuple) else _split_bf16(a)
    bh, bl = b if isinstance(b, tuple) else _split_bf16(b)
    d = lambda p, q: lax.dot_general(p, q, dims, preferred_element_type=F32)
    return d(ah, bh) + (d(ah, bl) + d(al, bh))


def _mm(a, b, *, ta=False, tb=False, add=(), out_dtype=F32, name):
    if ta:
        k_dim, m_dim = a.shape
    else:
        m_dim, k_dim = a.shape
    if tb:
        n_dim, k2 = b.shape
    else:
        k2, n_dim = b.shape
    assert k_dim == k2, (a.shape, b.shape, ta, tb)
    tm = _pick_wide(m_dim)
    tn = _pick_wide(n_dim)
    tk = _pick_wide(k_dim)
    nk = k_dim // tk
    n_add = len(add)
    dims = TN if ta else (NT if tb else NN)
    assert not (ta and tb)

    def body(a_ref, b_ref, *rest):
        add_refs = rest[:n_add]
        o_ref = rest[n_add]
        acc = rest[n_add + 1]
        k = pl.program_id(2)

        @pl.when(k == 0)
        def _():
            acc[...] = jnp.zeros_like(acc)

        acc[...] += _dot(a_ref[...], b_ref[...], dims)

        @pl.when(k == nk - 1)
        def _():
            r = acc[...]
            for ar in add_refs:
                r = r + ar[...].astype(F32)
            o_ref[...] = r.astype(o_ref.dtype)

    a_spec = pl.BlockSpec((tk, tm), lambda i, j, k: (k, i)) if ta else pl.BlockSpec((tm, tk), lambda i, j, k: (i, k))
    b_spec = pl.BlockSpec((tn, tk), lambda i, j, k: (j, k)) if tb else pl.BlockSpec((tk, tn), lambda i, j, k: (k, j))
    o_spec = pl.BlockSpec((tm, tn), lambda i, j, k: (i, j))
    return pl.pallas_call(
        body,
        out_shape=jax.ShapeDtypeStruct((m_dim, n_dim), out_dtype),
        grid=(m_dim // tm, n_dim // tn, nk),
        in_specs=[a_spec, b_spec] + [o_spec] * n_add,
        out_specs=o_spec,
        scratch_shapes=[pltpu.VMEM((tm, tn), F32)],
        compiler_params=pltpu.CompilerParams(dimension_semantics=("parallel", "parallel", "arbitrary")),
        name=name,
    )(a, b, *add)


def _rowwise(fn, rows, consts, outs, accs=(), *, tm=256, name):
    rows = [r if isinstance(r, tuple) else (r, 0, r.shape[1]) for r in rows]
    s_dim = rows[0][0].shape[0]
    tm = min(tm, s_dim)
    assert s_dim % tm == 0 and all(arr.shape[0] == s_dim for arr, _, _ in rows)
    specs = [pl.BlockSpec((tm, width), functools.partial(lambda i, cb: (i, cb), cb=cb)) for _, cb, width in rows]
    args = [arr for arr, _, _ in rows]
    for c in consts:
        specs.append(pl.BlockSpec(c.shape, lambda i: (0, 0)))
        args.append(c)
    nr, nc, no = len(rows), len(consts), len(outs)
    out_shape = [jax.ShapeDtypeStruct((s_dim, w), dt) for (w, dt) in outs]
    out_specs = [pl.BlockSpec((tm, w), lambda i: (i, 0)) for (w, dt) in outs]
    out_shape += [jax.ShapeDtypeStruct(sh, F32) for sh in accs]
    out_specs += [pl.BlockSpec(sh, lambda i: (0, 0)) for sh in accs]

    def body(*refs):
        r = [x[...].astype(F32) if x.dtype == BF16 else x[...] for x in refs[:nr]]
        c = [x[...] for x in refs[nr:nr + nc]]
        o_refs = refs[nr + nc:nr + nc + no]
        a_refs = refs[nr + nc + no:]
        o_vals, a_vals = fn(r, c)
        for ref, v in zip(o_refs, o_vals, strict=True):
            ref[...] = v.astype(ref.dtype)
        if a_refs:
            @pl.when(pl.program_id(0) == 0)
            def _():
                for ref in a_refs:
                    ref[...] = jnp.zeros_like(ref)

            for ref, v in zip(a_refs, a_vals, strict=True):
                ref[...] += v

    res = pl.pallas_call(
        body,
        out_shape=out_shape,
        grid=(s_dim // tm,),
        in_specs=specs,
        out_specs=out_specs,
        compiler_params=pltpu.CompilerParams(dimension_semantics=("arbitrary" if accs else "parallel",)),
        name=name,
    )(*args)
    return res


def _colsum(v):
    return jnp.sum(v, axis=0, keepdims=True)


def _rowsum(v):
    return jnp.sum(v, axis=1, keepdims=True)


def _rowmean(v):
    return jnp.mean(v, axis=1, keepdims=True)


def _silu_grad(x):
    s = _sig(x)
    return s * (1.0 + x * (1.0 - s))


def _conv_taps(x, w, width=4):
    row = lax.broadcasted_iota(jnp.int32, x.shape, 0)
    c = x * w[width - 1:width, :]
    for s in range(1, width):
        c = c + jnp.where(row >= s, pltpu.roll(x, s, 0), 0.0) * w[width - 1 - s:width - s, :]
    return c


def _conv_fwd(proj_a, conv_w):
    s_dim = proj_a.shape[0]
    n_blk = 3 * N_HEADS

    def body(x_ref, w_ref, o_ref):
        j = pl.program_id(0)
        c = _conv_taps(x_ref[...], w_ref[...])
        y = c * _sig(c)
        r = lax.rsqrt(_rowsum(y * y) + EPS_RMS)
        fac = jnp.where(j < N_HEADS, r * (HEAD ** -0.5), jnp.where(j < 2 * N_HEADS, r, 1.0))
        o_ref[...] = y * fac

    return pl.pallas_call(
        body,
        out_shape=jax.ShapeDtypeStruct((s_dim, n_blk * HEAD), F32),
        grid=(n_blk,),
        in_specs=[pl.BlockSpec((s_dim, HEAD), lambda j: (0, j)), pl.BlockSpec((4, HEAD), lambda j: (0, j))],
        out_specs=pl.BlockSpec((s_dim, HEAD), lambda j: (0, j)),
        compiler_params=pltpu.CompilerParams(dimension_semantics=("parallel",)),
        name="conv_fwd",
    )(proj_a, conv_w)


def _conv_bwd(proj_a, conv_w, dq, dk, dv):
    s_dim = proj_a.shape[0]
    n_blk = 3 * N_HEADS

    def body(x_ref, w_ref, dq_ref, dk_ref, dv_ref, dx_ref, dw_ref):
        j = pl.program_id(0)
        x = x_ref[...]
        w = w_ref[...]
        do = jnp.where(j < N_HEADS, dq_ref[...], jnp.where(j < 2 * N_HEADS, dk_ref[...], dv_ref[...]))
        c = _conv_taps(x, w)
        sg = _sig(c)
        y = c * sg
        r = lax.rsqrt(_rowsum(y * y) + EPS_RMS)
        sc = jnp.where(j < N_HEADS, HEAD ** -0.5, 1.0)
        dy_n = sc * (r * do - y * (r * r * r) * _rowsum(do * y))
        dy = jnp.where(j < 2 * N_HEADS, dy_n, do)
        dc = dy * (sg * (1.0 + c * (1.0 - sg)))
        row = lax.broadcasted_iota(jnp.int32, x.shape, 0)
        dx = dc * w[3:4, :]
        dw_ref[3:4, :] = _colsum(dc * x)
        for s in range(1, 4):
            dx = dx + jnp.where(row < s_dim - s, pltpu.roll(dc, s_dim - s, 0), 0.0) * w[3 - s:4 - s, :]
            xs = jnp.where(row >= s, pltpu.roll(x, s, 0), 0.0)
            dw_ref[3 - s:4 - s, :] = _colsum(dc * xs)
        dx_ref[...] = dx.astype(dx_ref.dtype)

    hd = N_HEADS - 1
    return pl.pallas_call(
        body,
        out_shape=[jax.ShapeDtypeStruct((s_dim, n_blk * HEAD), BF16), jax.ShapeDtypeStruct((4, n_blk * HEAD), F32)],
        grid=(n_blk,),
        in_specs=[
            pl.BlockSpec((s_dim, HEAD), lambda j: (0, j)),
            pl.BlockSpec((4, HEAD), lambda j: (0, j)),
            pl.BlockSpec((s_dim, HEAD), lambda j: (0, jnp.minimum(j, hd))),
            pl.BlockSpec((s_dim, HEAD), lambda j: (0, jnp.clip(j - N_HEADS, 0, hd))),
            pl.BlockSpec((s_dim, HEAD), lambda j: (0, jnp.clip(j - 2 * N_HEADS, 0, hd))),
        ],
        out_specs=[pl.BlockSpec((s_dim, HEAD), lambda j: (0, j)), pl.BlockSpec((4, HEAD), lambda j: (0, j))],
        compiler_params=pltpu.CompilerParams(dimension_semantics=("parallel",)),
        name="conv_bwd",
    )(proj_a, conv_w, dq, dk, dv)


def _chunk_tri(n):
    r = np.arange(n)
    m = ((r[:, None] // CHUNK) == (r[None, :] // CHUNK)) & (r[:, None] >= r[None, :])
    m = m.astype(np.float32)
    return jnp.asarray(m), jnp.asarray(m.T)


def _softplus(z):
    return jnp.maximum(z, 0.0) + jnp.log(1.0 + jnp.exp(-jnp.abs(z)))


def _gates_fwd(proj_b, alog, dtb):
    tm = min(GROUP, proj_b.shape[0])
    tri, _ = _chunk_tri(tm)

    def fn(r, c):
        b = r[0]
        a = pltpu.roll(b, LANES - N_HEADS, 1)
        alog_, dtb_, tri_ = c
        g = -jnp.exp(alog_) * _softplus(a + dtb_)
        return [_sig(b), _dot32(tri_, g)], []

    return _rowwise(fn, [(proj_b, WB_BA // LANES, LANES)], [alog, dtb, tri],
                    [(LANES, F32), (LANES, F32)], tm=tm, name="gates_fwd")


def _gates_bwd(proj_b, alog, dtb, gc, d_beta, d_gc, d_egl_rows):
    tm = min(GROUP, proj_b.shape[0])
    _, tri_t = _chunk_tri(tm)

    def fn(r, c):
        b, gc_, d_beta_, d_gc_, d_egl_ = r
        a = pltpu.roll(b, LANES - N_HEADS, 1)
        alog_, dtb_, tri_t_ = c
        z = a + dtb_
        ea = jnp.exp(alog_)
        g = -ea * _softplus(z)
        dg = _dot32(tri_t_, d_gc_ + d_egl_ * jnp.exp(gc_))
        d_a = dg * (-ea) * _sig(z)
        beta = _sig(b)
        d_ba = d_beta_ * beta * (1.0 - beta) + pltpu.roll(d_a, N_HEADS, 1)
        return [d_ba], [_colsum(dg * g), _colsum(d_a)]

    return _rowwise(fn, [(proj_b, WB_BA // LANES, LANES), gc, d_beta, d_gc, d_egl_rows],
                    [alog, dtb, tri_t], [(LANES, BF16)], accs=[(1, LANES), (1, LANES)], tm=tm,
                    name="gates_bwd")


def _group_masks(n):
    r = lax.broadcasted_iota(jnp.int32, (n, n), 0)
    c = lax.broadcasted_iota(jnp.int32, (n, n), 1)
    same = (r // CHUNK) == (c // CHUNK)
    below, s = [], 2
    while s < CHUNK:
        below.append(jnp.logical_and((r // (2 * s)) == (c // (2 * s)),
                                     jnp.logical_and((r // s) % 2 == 1, (c // s) % 2 == 0)))
        s *= 2
    return dict(same=same, tril=jnp.logical_and(same, r >= c), strict=jnp.logical_and(same, r > c),
                last=c == (r // CHUNK) * CHUNK + (CHUNK - 1), eye=r == c, pair=(r // 2) == (c // 2), below=below)


def _inv_unit_lower(l_mats, mk):
    eye_f = mk["eye"].astype(F32)
    ts = [eye_f - jnp.where(mk["pair"], l_mat, 0.0) for l_mat in l_mats]
    for below in mk["below"]:
        halves = [_split_bf16(t) for t in ts]
        mids = [_dot3(h, jnp.where(below, l_mat, 0.0)) for h, l_mat in zip(halves, l_mats)]
        ts = [t - _dot3(m, h) for t, m, h in zip(ts, mids, halves)]
    return ts


def _unfold_blocks(folded, mask):
    n = folded.shape[0]
    return jnp.where(mask, jnp.concatenate([folded] * (n // CHUNK), axis=1), 0.0)


def _head_cols(beta, gc, gc_t, h):
    lane = lax.broadcasted_iota(jnp.int32, beta.shape, 1)
    sub = lax.broadcasted_iota(jnp.int32, gc_t.shape, 0)
    bcol = _rowsum(jnp.where(lane == h, beta, 0.0))
    gcol = _rowsum(jnp.where(lane == h, gc, 0.0))
    grow = _colsum(jnp.where(sub == h, gc_t, 0.0))
    return bcol, gcol, grow


def _prep_common(q, k, bcol, gcol, grow, mk, t_folded=None):
    n = q.shape[0]
    tril = mk["tril"]
    decay = jnp.where(tril, jnp.exp(jnp.where(tril, gcol - grow, 0.0)), 0.0)
    glast = _rowsum(jnp.where(mk["last"], jnp.broadcast_to(grow, (n, n)), 0.0))
    e = jnp.exp(gcol)
    ekt = jnp.exp(glast - gcol)
    kb = k * bcol
    kk = _dot(kb, k, NT)
    qk = _dot(q, k, NT)
    p = dict(decay=decay, e=e, ekt=ekt, kb=kb, kk=kk, qk=qk)
    if t_folded is not None:
        p["t"] = _unfold_blocks(t_folded, mk["same"])
    return p


GROUPS_PER_STEP = 4
SCAN_CHUNKS_PER_STEP = 4


def _fold_blocks(m):
    n = m.shape[0]
    out = m[:, 0:CHUNK]
    for b in range(1, n // CHUNK):
        out = out + m[:, b * CHUNK:(b + 1) * CHUNK]
    return out


def _gdr_prep_fwd(qkvn, beta, gc, gc_t):
    s_dim = qkvn.shape[0]
    tg = min(GROUP, s_dim)
    n_sub = min(GROUPS_PER_STEP, s_dim // tg)
    tb = tg * n_sub

    def body(q_ref, k_ref, v_ref, b_ref, g_ref, gt_ref, u_ref, w_ref, qd_ref, kt_ref, a_ref, t_ref):
        h = pl.program_id(0)
        mk = _group_masks(tg)
        parts = []
        for s in range(n_sub):
            rows = slice(s * tg, (s + 1) * tg)
            q, k, v = q_ref[rows, :], k_ref[rows, :], v_ref[rows, :]
            bcol, gcol, grow = _head_cols(b_ref[rows, :], g_ref[rows, :], gt_ref[:, rows], h)
            p = _prep_common(q, k, bcol, gcol, grow, mk)
            qd_ref[rows, :] = q * p["e"]
            kt_ref[rows, :] = k * p["ekt"]
            a_ref[rows, :] = _fold_blocks(jnp.where(mk["tril"], p["qk"] * p["decay"], 0.0))
            parts.append((rows, v * bcol, p["kb"] * p["e"], jnp.where(mk["strict"], p["kk"] * p["decay"], 0.0)))
        t_mats = _inv_unit_lower([part[3] for part in parts], mk)
        for (rows, vb, kbe, _), t_mat in zip(parts, t_mats):
            u_ref[rows, :] = _dot(t_mat, vb)
            w_ref[rows, :] = _dot(t_mat, kbe)
            t_ref[rows, :] = _fold_blocks(t_mat)

    row = lambda off: pl.BlockSpec((tb, HEAD), functools.partial(lambda h, m, off: (m, h + off), off=off))
    full = pl.BlockSpec((tb, LANES), lambda h, m: (m, 0))
    o_spec = pl.BlockSpec((tb, HEAD), lambda h, m: (m, h))
    a_spec = pl.BlockSpec((None, tb, CHUNK), lambda h, m: (h, m, 0))
    wide = jax.ShapeDtypeStruct((s_dim, N_HEADS * HEAD), F32)
    folded = jax.ShapeDtypeStruct((N_HEADS, s_dim, CHUNK), F32)
    return pl.pallas_call(
        body,
        out_shape=[wide, wide, wide, wide, folded, folded],
        grid=(N_HEADS, s_dim // tb),
        in_specs=[row(0), row(N_HEADS), row(2 * N_HEADS), full, full, pl.BlockSpec((8, tb), lambda h, m: (0, m))],
        out_specs=[o_spec, o_spec, o_spec, o_spec, a_spec, a_spec],
        compiler_params=pltpu.CompilerParams(dimension_semantics=("parallel", "parallel")),
        name="gdr_prep_fwd",
    )(qkvn, qkvn, qkvn, beta, gc, gc_t)


def _gdr_prep_bwd(qkvn, beta, gc, gc_t, t_fold, u, w, du, dw, dqd, dkt, d_a):
    s_dim = qkvn.shape[0]
    tg = min(GROUP, s_dim)
    n_sub = min(GROUPS_PER_STEP, s_dim // tg)
    tb = tg * n_sub

    def body(q_ref, k_ref, v_ref, b_ref, g_ref, gt_ref, t_ref, u_ref, w_ref, du_ref, dw_ref, dqd_ref, dkt_ref,
             da_ref, dq_ref, dk_ref, dv_ref, db_ref, dg_ref):
        h = pl.program_id(1)

        @pl.when(h == 0)
        def _():
            db_ref[...] = jnp.zeros_like(db_ref)
            dg_ref[...] = jnp.zeros_like(dg_ref)

        mk = _group_masks(tg)
        lane = lax.broadcasted_iota(jnp.int32, (tg, LANES), 1)
        for s in range(n_sub):
            rows = slice(s * tg, (s + 1) * tg)
            q, k, v = q_ref[rows, :], k_ref[rows, :], v_ref[rows, :]
            bcol, gcol, grow = _head_cols(b_ref[rows, :], g_ref[rows, :], gt_ref[:, rows], h)
            p = _prep_common(q, k, bcol, gcol, grow, mk, t_ref[rows, :])
            t_mat, decay, e, ekt, kb = p["t"], p["decay"], p["e"], p["ekt"], p["kb"]
            du_, dw_, dqd_, dkt_ = du_ref[rows, :], dw_ref[rows, :], dqd_ref[rows, :], dkt_ref[rows, :]
            dvb = _dot(t_mat, du_, TN)
            dkbe = _dot(t_mat, dw_, TN)
            d_l = -(_dot(dvb, u_ref[rows, :], NT) + _dot(dkbe, w_ref[rows, :], NT))
            m1 = jnp.where(mk["strict"], d_l, 0.0)
            m2 = _unfold_blocks(da_ref[rows, :], mk["tril"])
            d_kk = m1 * decay
            d_qk = m2 * decay
            d_decay = m1 * p["kk"] + m2 * p["qk"]
            dkb = _dot(d_kk, k) + dkbe * e
            dk = _dot(d_kk, kb, TN) + _dot(d_qk, q, TN) + dkt_ * ekt + dkb * bcol
            dq = _dot(d_qk, k) + dqd_ * e
            d_beta = _rowsum(dkb * k) + _rowsum(dvb * v)
            d_e = _rowsum(dkbe * kb) + _rowsum(dqd_ * q)
            d_ekt = _rowsum(dkt_ * k) * ekt
            d_diff = d_decay * decay
            d_grow = -_colsum(d_diff) + _colsum(jnp.where(mk["last"], jnp.broadcast_to(d_ekt, (tg, tg)), 0.0))
            d_gcol = d_e * e - d_ekt + _rowsum(d_diff)
            d_gcol = d_gcol + _rowsum(jnp.where(mk["eye"], jnp.broadcast_to(d_grow, (tg, tg)), 0.0))
            dq_ref[rows, :] = dq
            dk_ref[rows, :] = dk
            dv_ref[rows, :] = dvb * bcol
            db_ref[rows, :] = jnp.where(lane == h, d_beta, db_ref[rows, :])
            dg_ref[rows, :] = jnp.where(lane == h, d_gcol, dg_ref[rows, :])

    row = lambda off: pl.BlockSpec((tb, HEAD), functools.partial(lambda m, h, off: (m, h + off), off=off))
    full = pl.BlockSpec((tb, LANES), lambda m, h: (m, 0))
    o_spec = pl.BlockSpec((tb, HEAD), lambda m, h: (m, h))
    a_spec = pl.BlockSpec((None, tb, CHUNK), lambda m, h: (h, m, 0))
    wide = jax.ShapeDtypeStruct((s_dim, N_HEADS * HEAD), F32)
    lanes = jax.ShapeDtypeStruct((s_dim, LANES), F32)
    return pl.pallas_call(
        body,
        out_shape=[wide, wide, wide, lanes, lanes],
        grid=(s_dim // tb, N_HEADS),
        in_specs=[row(0), row(N_HEADS), row(2 * N_HEADS), full, full, pl.BlockSpec((8, tb), lambda m, h: (0, m)),
                  a_spec, o_spec, o_spec, o_spec, o_spec, o_spec, o_spec, a_spec],
        out_specs=[o_spec, o_spec, o_spec, full, full],
        compiler_params=pltpu.CompilerParams(dimension_semantics=("parallel", "arbitrary")),
        name="gdr_prep_bwd",
    )(qkvn, qkvn, qkvn, beta, gc, gc_t, t_fold, u, w, du, dw, dqd, dkt, d_a)


def _gdr_scan_fwd(u, w, qd, kt, a_mat, gc):
    s_dim = u.shape[0]
    n_chunks = s_dim // CHUNK
    per = min(SCAN_CHUNKS_PER_STEP, n_chunks)
    tb = per * CHUNK

    def body(u_ref, w_ref, qd_ref, kt_ref, a_ref, g_ref, o_ref, st_ref, state):
        @pl.when(pl.program_id(0) == 0)
        def _():
            state[...] = jnp.zeros_like(state)

        heads = range(N_HEADS)
        cols = [slice(h * HEAD, (h + 1) * HEAD) for h in heads]
        for i in range(per):
            rows = slice(i * CHUNK, (i + 1) * CHUNK)
            egl = jnp.exp(g_ref[(i + 1) * CHUNK - 1:(i + 1) * CHUNK, :])
            s_b = [state[h].astype(BF16) for h in heads]
            for h in heads:
                st_ref[i, h] = state[h]
            ws = [_dot(w_ref[rows, cs], s) for cs, s in zip(cols, s_b)]
            qs = [_dot(qd_ref[rows, cs], s) for cs, s in zip(cols, s_b)]
            vns = [(u_ref[rows, cs] - ws_h).astype(BF16) for cs, ws_h in zip(cols, ws)]
            avs = [_dot(a_ref[h, rows, :], vn) for h, vn in zip(heads, vns)]
            kvs = [_dot(kt_ref[rows, cs], vn, TN) for cs, vn in zip(cols, vns)]
            for h, cs in zip(heads, cols):
                o_ref[rows, cs] = qs[h] + avs[h]
                state[h] = state[h] * egl[:, h:h + 1] + kvs[h]

    wide = pl.BlockSpec((tb, N_HEADS * HEAD), lambda n: (n, 0))
    return pl.pallas_call(
        body,
        out_shape=[jax.ShapeDtypeStruct((s_dim, N_HEADS * HEAD), F32),
                   jax.ShapeDtypeStruct((n_chunks, N_HEADS, HEAD, HEAD), F32)],
        grid=(n_chunks // per,),
        in_specs=[wide, wide, wide, wide, pl.BlockSpec((N_HEADS, tb, CHUNK), lambda n: (0, n, 0)),
                  pl.BlockSpec((tb, LANES), lambda n: (n, 0))],
        out_specs=[wide, pl.BlockSpec((per, N_HEADS, HEAD, HEAD), lambda n: (n, 0, 0, 0))],
        scratch_shapes=[pltpu.VMEM((N_HEADS, HEAD, HEAD), F32)],
        compiler_params=pltpu.CompilerParams(dimension_semantics=("arbitrary",)),
        name="gdr_scan_fwd",
    )(u, w, qd, kt, a_mat, gc)


def _gdr_scan_bwd(u, w, qd, kt, a_mat, gc, states, d_o):
    s_dim = u.shape[0]
    n_chunks = s_dim // CHUNK
    per = min(SCAN_CHUNKS_PER_STEP, n_chunks)
    tb = per * CHUNK
    last = n_chunks // per - 1

    def body(u_ref, w_ref, qd_ref, kt_ref, a_ref, g_ref, st_ref, do_ref,
             du_ref, dw_ref, dqd_ref, dkt_ref, da_ref, de_ref, d_state):
        @pl.when(pl.program_id(0) == 0)
        def _():
            d_state[...] = jnp.zeros_like(d_state)

        heads = range(N_HEADS)
        cols = [slice(h * HEAD, (h + 1) * HEAD) for h in heads]
        for i in reversed(range(per)):
            rows = slice(i * CHUNK, (i + 1) * CHUNK)
            egl = jnp.exp(g_ref[(i + 1) * CHUNK - 1:(i + 1) * CHUNK, :])
            s_b = [st_ref[i, h].astype(BF16) for h in heads]
            ds_b = [d_state[h].astype(BF16) for h in heads]
            dos = [do_ref[rows, cs].astype(BF16) for cs in cols]
            w_b = [w_ref[rows, cs].astype(BF16) for cs in cols]
            ws = [_dot(w_h, s) for w_h, s in zip(w_b, s_b)]
            ados = [_dot(a_ref[h, rows, :], do, TN) for h, do in zip(heads, dos)]
            kds = [_dot(kt_ref[rows, cs], ds) for cs, ds in zip(cols, ds_b)]
            dqds = [_dot(do, s, NT) for do, s in zip(dos, s_b)]
            qdos = [_dot(qd_ref[rows, cs], do, TN) for cs, do in zip(cols, dos)]
            vns = [(u_ref[rows, cs] - ws_h).astype(BF16) for cs, ws_h in zip(cols, ws)]
            dvns = [a + k_ for a, k_ in zip(ados, kds)]
            dvn_b = [d.astype(BF16) for d in dvns]
            das = [_dot(do, vn, NT) for do, vn in zip(dos, vns)]
            dkts = [_dot(vn, ds, NT) for vn, ds in zip(vns, ds_b)]
            dws = [_dot(d, s, NT) for d, s in zip(dvn_b, s_b)]
            wds = [_dot(w_h, d, TN) for w_h, d in zip(w_b, dvn_b)]
            for h, cs in zip(heads, cols):
                ds_n = d_state[h]
                de = jnp.sum(_rowsum(ds_n * st_ref[i, h]), axis=0, keepdims=True)
                de_ref[i, h:h + 1, :] = jnp.broadcast_to(de, (1, LANES))
                dqd_ref[rows, cs] = dqds[h]
                da_ref[h, rows, :] = das[h]
                dkt_ref[rows, cs] = dkts[h]
                du_ref[rows, cs] = dvns[h]
                dw_ref[rows, cs] = -dws[h]
                d_state[h] = ds_n * egl[:, h:h + 1] + qdos[h] - wds[h]

    wide = pl.BlockSpec((tb, N_HEADS * HEAD), lambda n: (last - n, 0))
    a_spec = pl.BlockSpec((N_HEADS, tb, CHUNK), lambda n: (0, last - n, 0))
    wide_shape = jax.ShapeDtypeStruct((s_dim, N_HEADS * HEAD), F32)
    return pl.pallas_call(
        body,
        out_shape=[wide_shape, wide_shape, wide_shape, wide_shape,
                   jax.ShapeDtypeStruct((N_HEADS, s_dim, CHUNK), F32),
                   jax.ShapeDtypeStruct((n_chunks, N_HEADS, LANES), F32)],
        grid=(n_chunks // per,),
        in_specs=[wide, wide, wide, wide, a_spec, pl.BlockSpec((tb, LANES), lambda n: (last - n, 0)),
                  pl.BlockSpec((per, N_HEADS, HEAD, HEAD), lambda n: (last - n, 0, 0, 0)), wide],
        out_specs=[wide, wide, wide, wide, a_spec, pl.BlockSpec((per, N_HEADS, LANES), lambda n: (last - n, 0, 0))],
        scratch_shapes=[pltpu.VMEM((N_HEADS, HEAD, HEAD), F32)],
        compiler_params=pltpu.CompilerParams(dimension_semantics=("arbitrary",)),
        name="gdr_scan_bwd",
    )(u, w, qd, kt, a_mat, gc, states, d_o)


def _gdr_out_fwd(o_dn, proj_a, dn_w):
    def fn(r, c):
        o, z = r
        (w_,) = c
        outs = []
        for h in range(N_HEADS):
            cs = slice(h * HEAD, (h + 1) * HEAD)
            oh, zh = o[:, cs], z[:, cs]
            rr = lax.rsqrt(_rowmean(oh * oh) + EPS_RMS)
            outs.append(oh * rr * w_ * (zh * _sig(zh)))
        return [jnp.concatenate(outs, axis=1)], []

    return _rowwise(fn, [o_dn, (proj_a, 3, D_MODEL)], [dn_w], [(D_MODEL, BF16)], name="gdr_out_fwd")[0]


def _gdr_out_bwd(o_dn, proj_a, d_og, dn_w):
    def fn(r, c):
        o, z, dg = r
        (w_,) = c
        d_o, d_z = [], []
        d_w = jnp.zeros((1, HEAD), F32)
        for h in range(N_HEADS):
            cs = slice(h * HEAD, (h + 1) * HEAD)
            oh, zh, dgh = o[:, cs], z[:, cs], dg[:, cs]
            rr = lax.rsqrt(_rowmean(oh * oh) + EPS_RMS)
            sz = zh * _sig(zh)
            d_n = dgh * sz
            d_z.append(dgh * (oh * rr * w_) * _silu_grad(zh))
            d_w = d_w + _colsum(d_n * oh * rr)
            gw = d_n * w_
            d_o.append(rr * gw - oh * (rr * rr * rr) * _rowmean(gw * oh))
        return [jnp.concatenate(d_o, axis=1), jnp.concatenate(d_z, axis=1)], [d_w]

    return _rowwise(fn, [o_dn, (proj_a, 3, D_MODEL), d_og], [dn_w], [(D_MODEL, F32), (D_MODEL, BF16)],
                    accs=[(1, HEAD)], name="gdr_out_bwd")


def _rms_fwd(x, w):
    r = lax.rsqrt(_rowmean(x * x) + EPS_RMS)
    return x * r * w


def _rms_bwd(x, w, dy):
    r = lax.rsqrt(_rowmean(x * x) + EPS_RMS)
    gw = dy * w
    return r * gw - x * (r * r * r) * _rowmean(gw * x), _colsum(dy * x * r)


def _mla_norm_fwd(proj_b, qn_w, kvn_w):
    def fn(r, c):
        return [_rms_fwd(r[0], c[0]), _rms_fwd(r[1], c[1])], []

    return _rowwise(fn, [(proj_b, WB_CQ // Q_LORA, Q_LORA), (proj_b, WB_CKV // KV_LORA, KV_LORA)], [qn_w, kvn_w],
                    [(Q_LORA, BF16), (KV_LORA, BF16)], name="mla_norm_fwd")


def _mla_norm_bwd(proj_b, qn_w, kvn_w, d_cq, d_ckv):
    def fn(r, c):
        dx1, dw1 = _rms_bwd(r[0], c[0], r[2])
        dx2, dw2 = _rms_bwd(r[1], c[1], r[3])
        return [dx1, dx2], [dw1, dw2]

    return _rowwise(fn, [(proj_b, WB_CQ // Q_LORA, Q_LORA), (proj_b, WB_CKV // KV_LORA, KV_LORA), d_cq, d_ckv],
                    [qn_w, kvn_w], [(Q_LORA, BF16), (KV_LORA, BF16)], accs=[(1, Q_LORA), (1, KV_LORA)],
                    name="mla_norm_bwd")


def _rope_consts():
    inv = ROPE_BASE ** (-np.arange(0, ROPE, 2, dtype=np.float32) / ROPE)
    t = np.zeros((4, LANES), np.float32)
    t[0, :32] = inv
    t[0, 32:64] = inv
    t[1, :64] = 1.0
    t[2, 32:64] = 1.0
    t[3, :32] = -1.0
    return jnp.asarray(t)


def _rope_tables(pos, consts, width):
    ang = pos * consts[0:1, :]
    cosv, sinv = jnp.cos(ang), jnp.sin(ang)
    reps = width // LANES
    tile = (lambda t: jnp.concatenate([t] * reps, axis=1)) if reps > 1 else (lambda t: t)
    return tile(cosv * consts[1:2, :]), tile(sinv * consts[2:3, :]), tile(sinv * consts[3:4, :])


def _rope_apply(t, tabs):
    cos_t, sin_a, sin_b = tabs
    width = t.shape[1]
    return t * cos_t + pltpu.roll(t, 32, 1) * sin_a + pltpu.roll(t, width - 32, 1) * sin_b


def _rope_transpose(d, tabs):
    cos_t, sin_a, sin_b = tabs
    width = d.shape[1]
    return d * cos_t + pltpu.roll(d * sin_a, width - 32, 1) + pltpu.roll(d * sin_b, 32, 1)


QK_HEAD = 2 * HEAD


def _interleave_heads(a, b):
    parts = []
    for h in range(N_HEADS):
        parts.append(a[:, h * HEAD:(h + 1) * HEAD])
        parts.append(b if b.shape[1] == LANES else b[:, h * LANES:(h + 1) * LANES])
    return jnp.concatenate(parts, axis=1)


def _mla_qk_fwd(q_full, k_nope, proj_b, pos):
    consts = _rope_consts()

    def fn(r, c):
        qf, kn, kr, pos_ = r
        qn, qr = qf[:, :D_MODEL], qf[:, D_MODEL:]
        qr = _rope_apply(qr, _rope_tables(pos_, c[0], D_MODEL))
        kr = _rope_apply(kr, _rope_tables(pos_, c[0], LANES))
        return [_interleave_heads(qn, qr) * SCALE, _interleave_heads(kn, kr)], []

    return _rowwise(fn, [q_full, k_nope, (proj_b, WB_KR // LANES, LANES), pos], [consts],
                    [(N_HEADS * QK_HEAD, BF16), (N_HEADS * QK_HEAD, BF16)], name="mla_qk_fwd")


def _mla_qk_bwd(d_qc, d_kc, pos):
    consts = _rope_consts()

    def fn(r, c):
        dq, dk, pos_ = r
        even = lambda t: jnp.concatenate([t[:, (2 * h) * LANES:(2 * h + 1) * LANES] for h in range(N_HEADS)], axis=1)
        odd = lambda t: jnp.concatenate([t[:, (2 * h + 1) * LANES:(2 * h + 2) * LANES] for h in range(N_HEADS)], axis=1)
        d_qr_raw = _rope_transpose(odd(dq), _rope_tables(pos_, c[0], D_MODEL)) * SCALE
        dkr = dk[:, LANES:2 * LANES]
        for h in range(1, N_HEADS):
            dkr = dkr + dk[:, (2 * h + 1) * LANES:(2 * h + 2) * LANES]
        return [jnp.concatenate([even(dq) * SCALE, d_qr_raw], axis=1), even(dk),
                _rope_transpose(dkr, _rope_tables(pos_, c[0], LANES))], []

    return _rowwise(fn, [d_qc, d_kc, pos], [consts], [(2 * D_MODEL, BF16), (D_MODEL, BF16), (LANES, BF16)],
                    name="mla_qk_bwd")


def _causal_mask_t(st, key0, query0):
    key = lax.broadcasted_iota(jnp.int32, st.shape, 0) + key0
    query = lax.broadcasted_iota(jnp.int32, st.shape, 1) + query0
    return jnp.where(key <= query, st, NEG_BIG)


def _attn_tiles(s_dim):
    tq = min(512, s_dim)
    n_chains = 2 if s_dim >= 2 * tq else 1
    return tq, n_chains, min(512, s_dim)


def _diagonal_chains(t, tq, n_chains, tk):
    return [(c, (t + 1) * tk - 1 > c * tq) for c in range(n_chains) if t * tk < (c + 1) * tq]


def _attn_fwd(qc, kc, vt):
    s_dim = qc.shape[0]
    tq, n_chains, tk = _attn_tiles(s_dim)
    tqs = tq * n_chains

    def body(q_ref, k_ref, vt_ref, o_ref, lse_ref, m_s, l_s, acc):
        qi = pl.program_id(1)
        m_s[...] = jnp.full_like(m_s, NEG_BIG)
        l_s[...] = jnp.zeros_like(l_s)
        acc[...] = jnp.zeros_like(acc)

        def make_step(chains):
            def step(j, carry):
                ks = pl.multiple_of(j * tk, tk)
                kb, vtb = k_ref[pl.ds(ks, tk), :], vt_ref[:, pl.ds(ks, tk)]
                cols = [slice(c * tq, (c + 1) * tq) for c, _ in chains]
                sts = [_dot(kb, q_ref[cs, :], NT) for cs in cols]
                sts = [_causal_mask_t(st, j * tk, qi * tqs + c * tq) if masked else st
                       for st, (c, masked) in zip(sts, chains)]
                m_prevs = [m_s[:, cs] for cs in cols]
                m_news = [jnp.maximum(mp, jnp.max(st, axis=0, keepdims=True)) for mp, st in zip(m_prevs, sts)]
                alphas = [jnp.exp(mp - mn) for mp, mn in zip(m_prevs, m_news)]
                pts = [jnp.exp(st - mn) for st, mn in zip(sts, m_news)]
                pvs = [_dot(vtb, pt) for pt in pts]
                for cs, mn, al, pt, pv in zip(cols, m_news, alphas, pts, pvs):
                    l_s[:, cs] = al * l_s[:, cs] + _colsum(pt)
                    m_s[:, cs] = mn
                    acc[:, cs] = acc[:, cs] * al + pv
                return carry
            return step

        below = qi * (tqs // tk)
        lax.fori_loop(0, below, make_step([(c, False) for c in range(n_chains)]), 0)
        for t in range(tqs // tk):
            make_step(_diagonal_chains(t, tq, n_chains, tk))(below + t, 0)
        l = l_s[...]
        o_ref[...] = jnp.transpose(acc[...] / l)
        lse_ref[...] = m_s[...] + jnp.log(l)

    return pl.pallas_call(
        body,
        out_shape=[jax.ShapeDtypeStruct((s_dim, N_HEADS * HEAD), F32), jax.ShapeDtypeStruct((N_HEADS, 1, s_dim), F32)],
        grid=(N_HEADS, s_dim // tqs),
        in_specs=[pl.BlockSpec((tqs, QK_HEAD), lambda h, qi: (qi, h)),
                  pl.BlockSpec((s_dim, QK_HEAD), lambda h, qi: (0, h)),
                  pl.BlockSpec((HEAD, s_dim), lambda h, qi: (h, 0))],
        out_specs=[pl.BlockSpec((tqs, HEAD), lambda h, qi: (qi, h)),
                   pl.BlockSpec((None, 1, tqs), lambda h, qi: (h, 0, qi))],
        scratch_shapes=[pltpu.VMEM((1, tqs), F32), pltpu.VMEM((1, tqs), F32), pltpu.VMEM((HEAD, tqs), F32)],
        compiler_params=pltpu.CompilerParams(dimension_semantics=("parallel", "parallel")),
        name="attn_fwd",
    )(qc, kc, vt)


def _attn_bwd(qc, kc, kct, v, o, d_o, lse):
    s_dim = qc.shape[0]
    tq, n_chains, tk = _attn_tiles(s_dim)
    tqs = tq * n_chains

    def body(q_ref, k_ref, kt_ref, v_ref, o_ref, do_ref, lse_ref, dq_ref, dk_ref, dv_ref, dqt_acc, dv_acc):
        qi = pl.program_id(1)

        @pl.when(qi == 0)
        def _():
            dk_ref[...] = jnp.zeros_like(dk_ref)
            dv_acc[...] = jnp.zeros_like(dv_acc)

        dqt_acc[...] = jnp.zeros_like(dqt_acc)
        do_f = do_ref[...]
        do_all = do_f.astype(BF16)
        q_all = q_ref[...]
        lse_row = lse_ref[...]
        delta_row = _dot3(jnp.ones((8, HEAD), F32), o_ref[...] * do_f, NT)[0:1, :]

        def make_step(chains):
            rows = slice(chains[0][0] * tq, (chains[-1][0] + 1) * tq)

            def step(j, carry):
                ks = pl.multiple_of(j * tk, tk)
                kb, vb, ktb = k_ref[pl.ds(ks, tk), :], v_ref[pl.ds(ks, tk), :], kt_ref[:, pl.ds(ks, tk)]
                cols = [slice(c * tq, (c + 1) * tq) for c, _ in chains]
                sts = [_dot(kb, q_all[cs, :], NT) for cs in cols]
                sts = [_causal_mask_t(st, j * tk, qi * tqs + c * tq) if masked else st
                       for st, (c, masked) in zip(sts, chains)]
                dpts = [_dot(vb, do_all[cs, :], NT) for cs in cols]
                pts = [jnp.exp(st - lse_row[:, cs]) for st, cs in zip(sts, cols)]
                dsts = [(pt * (dpt - delta_row[:, cs])).astype(BF16) for pt, dpt, cs in zip(pts, dpts, cols)]
                pts = [pt.astype(BF16) for pt in pts]
                dqs = [_dot(ktb, dst) for dst in dsts]
                for cs, dq in zip(cols, dqs):
                    dqt_acc[:, cs] += dq
                pt_all = jnp.concatenate(pts, axis=1) if len(chains) > 1 else pts[0]
                dst_all = jnp.concatenate(dsts, axis=1) if len(chains) > 1 else dsts[0]
                dk_ref[pl.ds(ks, tk), :] += _dot(dst_all, q_all[rows, :])
                dv_acc[pl.ds(ks, tk), :] += _dot(pt_all, do_all[rows, :])
                return carry
            return step

        below = qi * (tqs // tk)
        lax.fori_loop(0, below, make_step([(c, False) for c in range(n_chains)]), 0)
        for t in range(tqs // tk):
            make_step(_diagonal_chains(t, tq, n_chains, tk))(below + t, 0)
        dq_ref[...] = jnp.transpose(dqt_acc[...])

        @pl.when(qi == s_dim // tqs - 1)
        def _():
            dv_ref[...] = dv_acc[...].astype(dv_ref.dtype)

    q_spec = pl.BlockSpec((tqs, QK_HEAD), lambda h, qi: (qi, h))
    o_spec = pl.BlockSpec((tqs, HEAD), lambda h, qi: (qi, h))
    k_spec = pl.BlockSpec((s_dim, QK_HEAD), lambda h, qi: (0, h))
    v_spec = pl.BlockSpec((s_dim, HEAD), lambda h, qi: (0, h))
    wide2 = jax.ShapeDtypeStruct((s_dim, N_HEADS * QK_HEAD), F32)
    return pl.pallas_call(
        body,
        out_shape=[wide2, wide2, jax.ShapeDtypeStruct((s_dim, N_HEADS * HEAD), BF16)],
        grid=(N_HEADS, s_dim // tqs),
        in_specs=[q_spec, k_spec, pl.BlockSpec((QK_HEAD, s_dim), lambda h, qi: (h, 0)), v_spec, o_spec, o_spec,
                  pl.BlockSpec((None, 1, tqs), lambda h, qi: (h, 0, qi))],
        out_specs=[q_spec, k_spec, v_spec],
        scratch_shapes=[pltpu.VMEM((QK_HEAD, tqs), F32), pltpu.VMEM((s_dim, HEAD), F32)],
        compiler_params=pltpu.CompilerParams(dimension_semantics=("parallel", "arbitrary")),
        name="attn_bwd",
    )(qc, kc, kct, v, o, d_o, lse)


def _merge_fwd(y_dn, y_mla, proj_g):
    def fn(r, c):
        yd, ym, g = r
        return [_sig(g[:, :D_MODEL]) * yd + _sig(g[:, D_MODEL:]) * ym], []

    return _rowwise(fn, [y_dn, y_mla, proj_g], [], [(D_MODEL, BF16)], name="merge_fwd")[0]


def _merge_bwd(y_dn, y_mla, proj_g, d_mixed):
    def fn(r, c):
        yd, ym, g, dm = r
        sd, sm = _sig(g[:, :D_MODEL]), _sig(g[:, D_MODEL:])
        d_g = jnp.concatenate([dm * yd * sd * (1.0 - sd), dm * ym * sm * (1.0 - sm)], axis=1)
        return [d_g, dm * sd, dm * sm], []

    return _rowwise(fn, [y_dn, y_mla, proj_g, d_mixed], [], [(2 * D_MODEL, BF16), (D_MODEL, BF16), (D_MODEL, BF16)],
                    name="merge_bwd")


def _ln_stats(z):
    mu = _rowmean(z)
    zc = z - mu
    r = lax.rsqrt(_rowmean(zc * zc) + EPS_LN)
    return zc * r, r


def _ln_bwd(dy, xh, r, g):
    dxh = dy * g
    return r * (dxh - _rowmean(dxh) - xh * _rowmean(dxh * xh))


def _ln1_fwd(x, a1, g, b):
    def fn(r, c):
        xh, _ = _ln_stats(ALPHA * r[0] + r[1])
        y = xh * c[0] + c[1]
        return [y, y], []

    return _rowwise(fn, [x, a1], [g, b], [(D_MODEL, F32), (D_MODEL, BF16)], name="ln1_fwd")


def _ln1_bwd(x, a1, d_h1, g):
    def fn(r, c):
        xh, rr = _ln_stats(ALPHA * r[0] + r[1])
        dy = r[2]
        dz = _ln_bwd(dy, xh, rr, c[0])
        return [dz, ALPHA * dz], [_colsum(dy * xh), _colsum(dy)]

    return _rowwise(fn, [x, a1, d_h1], [g], [(D_MODEL, BF16), (D_MODEL, F32)], accs=[(1, D_MODEL), (1, D_MODEL)],
                    name="ln1_bwd")


def _act_fwd(gu):
    def fn(r, c):
        gt, up = r[0][:, :FFN_HIDDEN], r[0][:, FFN_HIDDEN:]
        return [gt * _sig(gt) * up], []

    return _rowwise(fn, [gu], [], [(FFN_HIDDEN, BF16)], name="act_fwd")[0]


def _act_bwd(gu, d_act):
    def fn(r, c):
        gt, up = r[0][:, :FFN_HIDDEN], r[0][:, FFN_HIDDEN:]
        da = r[1]
        return [jnp.concatenate([da * up * _silu_grad(gt), da * gt * _sig(gt)], axis=1)], []

    return _rowwise(fn, [gu, d_act], [], [(2 * FFN_HIDDEN, BF16)], name="act_bwd")[0]


def _tail(h1, ffn, pg, pp, tgt, g, b):
    def fn(r, c):
        h1_, ffn_, pg_, pp_, t_ = r
        sp = _sig(pg_)
        xh, rr = _ln_stats(ALPHA * h1_ + ffn_ + sp * pp_)
        y = xh * c[0] + c[1]
        err = y - t_
        dy = err * (1.0 / D_MODEL)
        dz = _ln_bwd(dy, xh, rr, c[0])
        loss = jnp.sum(0.5 * _rowmean(err * err), axis=0, keepdims=True)
        return ([dz, dz * pp_ * sp * (1.0 - sp), dz * sp, ALPHA * dz],
                [_colsum(dy * xh), _colsum(dy), jnp.broadcast_to(loss, (1, LANES))])

    return _rowwise(fn, [h1, ffn, pg, pp, tgt], [g, b], [(D_MODEL, BF16)] * 3 + [(D_MODEL, F32)],
                    accs=[(1, D_MODEL), (1, D_MODEL), (1, LANES)], name="tail")


def _local_step(x, p, pos, tgt, w, late_weights, emit):
    w = dict(w)
    s_dim = x.shape[0]
    xb, pb = x.astype(BF16), p.astype(BF16)
    proj_a = _mm(xb, w["wa"], name="f_proj_a")
    proj_g = _mm(xb, w["wg"], out_dtype=BF16, name="f_proj_g")
    proj_b = _mm(xb, w["wb"], name="f_proj_b")
    qkvn = _conv_fwd(proj_a, w["conv"])
    beta, gc = _gates_fwd(proj_b, w["alog"], w["dtb"])
    gc_t = jnp.transpose(gc[:, :N_HEADS])
    u, w_, qd, kt, a_mat, t_fold = _gdr_prep_fwd(qkvn, beta, gc, gc_t)
    o_dn, states = _gdr_scan_fwd(u, w_, qd, kt, a_mat, gc)
    og = _gdr_out_fwd(o_dn, proj_a, w["dnw"])
    w.update(late_weights("mix", og))
    y_dn = _mm(og, w["br_dn"], out_dtype=BF16, name="f_y_dn")
    c_q, c_kv = _mla_norm_fwd(proj_b, w["qnw"], w["kvnw"])
    q_full = _mm(c_q, w["uq"], out_dtype=BF16, name="f_q_full")
    k_nope = _mm(c_kv, w["uk"], out_dtype=BF16, name="f_k_nope")
    vv = _mm(c_kv, w["uv"], out_dtype=BF16, name="f_v")
    qc, kc = _mla_qk_fwd(q_full, k_nope, proj_b, pos)
    o_mla, lse = _attn_fwd(qc, kc, jnp.transpose(vv))
    y_mla = _mm(o_mla, w["br_mla"], out_dtype=BF16, name="f_y_mla")
    mixed = _merge_fwd(y_dn, y_mla, proj_g)
    a1 = _mm(mixed, w["wo"], name="f_a1")
    w.update(late_weights("ffn", a1))
    h1, h1b = _ln1_fwd(x, a1, w["ln1g"], w["ln1b"])
    gu = _mm(h1b, w["ffn_in"], out_dtype=BF16, name="f_gu")
    act = _act_fwd(gu)
    ffn = _mm(act, w["ffn_out"], name="f_ffn")
    pg = _mm(h1b, w["ple_gate"], name="f_pg")
    pp = _mm(pb, w["ple"], name="f_pp")
    g = {}
    dz2, d_pg, d_pp, dh1a, g["ln2g"], g["ln2b"], loss = _tail(h1, ffn, pg, pp, tgt, w["ln2g"], w["ln2b"])
    g["ple_t"] = _mm(d_pp, pb, ta=True, out_dtype=BF16, name="b_w_ple")
    g["ple_gate"] = _mm(h1b, d_pg, ta=True, out_dtype=BF16, name="b_w_ple_gate")
    g["ffn_out"] = _mm(act, dz2, ta=True, out_dtype=BF16, name="b_w_ffn_out")
    d_act = _mm(dz2, w["ffn_out"], tb=True, out_dtype=BF16, name="b_act")
    d_gu = _act_bwd(gu, d_act)
    g["ffn_in_t"] = _mm(d_gu, h1b, ta=True, out_dtype=BF16, name="b_w_ffn_in")
    d_gu = emit("ffn", g, d_gu)
    d_h1 = _mm(d_gu, w["ffn_in_t"], add=(dh1a,), name="b_h1_ffn")
    d_h1 = _mm(d_pg, w["ple_gate"], tb=True, add=(d_h1,), name="b_h1_ple")
    dz1, dxa, g["ln1g"], g["ln1b"] = _ln1_bwd(x, a1, d_h1, w["ln1g"])
    g["wo"] = _mm(mixed, dz1, ta=True, out_dtype=BF16, name="b_w_o")
    d_mixed = _mm(dz1, w["wo"], tb=True, out_dtype=BF16, name="b_mixed")
    d_proj_g, d_y_dn, d_y_mla = _merge_bwd(y_dn, y_mla, proj_g, d_mixed)
    g["br_mla"] = _mm(o_mla, d_y_mla, ta=True, out_dtype=BF16, name="b_w_br_mla")
    d_o_mla = _mm(d_y_mla, w["br_mla"], tb=True, out_dtype=BF16, name="b_o_mla")
    d_qc, d_kc, d_v = _attn_bwd(qc, kc, jnp.transpose(kc), vv, o_mla, d_o_mla, lse)
    d_q_full, d_kn, d_kr = _mla_qk_bwd(d_qc, d_kc, pos)
    g["uq"] = _mm(c_q, d_q_full, ta=True, out_dtype=BF16, name="b_w_uq")
    d_c_q = _mm(d_q_full, w["uq"], tb=True, out_dtype=BF16, name="b_c_q")
    g["uk"] = _mm(c_kv, d_kn, ta=True, out_dtype=BF16, name="b_w_uk")
    g["uv"] = _mm(c_kv, d_v, ta=True, out_dtype=BF16, name="b_w_uv")
    d_c_kv = _mm(d_kn, w["uk"], tb=True, name="b_c_kv_k")
    d_c_kv = _mm(d_v, w["uv"], tb=True, add=(d_c_kv,), out_dtype=BF16, name="b_c_kv_v")
    d_cq, d_ckv, g["qnw"], g["kvnw"] = _mla_norm_bwd(proj_b, w["qnw"], w["kvnw"], d_c_q, d_c_kv)
    g["br_dn"] = _mm(og, d_y_dn, ta=True, out_dtype=BF16, name="b_w_br_dn")
    d_og = emit("mix", g, _mm(d_y_dn, w["br_dn"], tb=True, out_dtype=BF16, name="b_og"))
    d_o_dn, d_z, g["dnw"] = _gdr_out_bwd(o_dn, proj_a, d_og, w["dnw"])
    du, dw, dqd, dkt, d_a, d_egl = _gdr_scan_bwd(u, w_, qd, kt, a_mat, gc, states, d_o_dn)
    dq, dk, dv, d_beta, d_gc = _gdr_prep_bwd(qkvn, beta, gc, gc_t, t_fold, u, w_, du, dw, dqd, dkt, d_a)
    d_egl_rows = jnp.pad(d_egl[:, None, :, 0], ((0, 0), (CHUNK - 1, 0), (0, LANES - N_HEADS))).reshape(s_dim, LANES)
    d_ba, g["alog"], g["dtb"] = _gates_bwd(proj_b, w["alog"], w["dtb"], gc, d_beta, d_gc, d_egl_rows)
    d_qkv, g["conv"] = _conv_bwd(proj_a, w["conv"], dq, dk, dv)
    zeros = jnp.zeros((s_dim, WB_CKV - Q_LORA), BF16)
    d_proj_b = jnp.concatenate([d_cq, zeros, d_ckv, d_kr, d_ba], axis=1)
    g["wa_qkv_t"] = _mm(d_qkv, xb, ta=True, name="b_w_qkv")
    g["wa_z_t"] = _mm(d_z, xb, ta=True, name="b_w_z")
    g["wg_t"] = _mm(d_proj_g, xb, ta=True, name="b_w_g")
    g["wb_t"] = _mm(d_proj_b, xb, ta=True, name="b_w_b")
    dx = _mm(d_qkv, w["wa_qkv_t"], add=(dxa,), name="b_x_qkv")
    dx = _mm(d_z, w["wa_z_t"], add=(dx,), name="b_x_z")
    dx = _mm(d_proj_g, w["wg_t"], add=(dx,), name="b_x_g")
    dx = _mm(d_proj_b, w["wb_t"], add=(dx,), name="b_x_b")
    return loss, dx, g


_BIG = (("w_in", 1), ("w_uq", 0), ("w_uk", 0), ("w_uv", 0), ("w_br_dn", 0), ("w_br_mla", 0),
        ("w_o", 0), ("w_ffn_in", 1), ("w_ffn_out", 0), ("w_ple", 1), ("w_ple_gate", 0))
_BIG_AXIS = dict(_BIG)
_SMALL = ("ln1_g", "ln1_b", "ln2_g", "ln2_b", "q_norm_w", "kv_norm_w", "dn_norm_w", "dn_a_log", "dn_dt_bias")
_ORDER = ("w_in", "conv_w", "dn_a_log", "dn_dt_bias", "dn_norm_w", "q_norm_w", "w_uq", "kv_norm_w", "w_uk", "w_uv",
          "w_br_dn", "w_br_mla", "w_o", "ln1_g", "ln1_b", "w_ffn_in", "w_ffn_out", "w_ple", "w_ple_gate", "ln2_g",
          "ln2_b")


def _stored_shape(name, shard_shape):
    axis = _BIG_AXIS[name]
    lead = shard_shape[axis]
    return lead, int(np.prod(shard_shape)) // lead


def _to_stored(name, shard):
    return jnp.moveaxis(shard, _BIG_AXIS[name], 0).reshape(_stored_shape(name, shard.shape))


def _from_stored(name, stored, shard_shape):
    axis = _BIG_AXIS[name]
    moved = (shard_shape[axis],) + shard_shape[:axis] + shard_shape[axis + 1:]
    return jnp.moveaxis(stored.reshape(moved), 0, axis)


_W_IN_ROWS = np.cumsum([0, 3072, 1024, 8, 8, Q_LORA, KV_LORA, ROPE, D_MODEL, D_MODEL])


def _first_weights(w_in_t, conv_full, small):
    r = _W_IN_ROWS
    zr = lambda n: jnp.zeros((n, D_MODEL), w_in_t.dtype)
    w = {}
    w["wa_t"] = w_in_t[r[0]:r[2]]
    w["wa_qkv_t"], w["wa_z_t"] = w_in_t[r[0]:r[1]], w_in_t[r[1]:r[2]]
    w["wg_t"] = w_in_t[r[7]:r[9]]
    w["wb_t"] = jnp.concatenate([w_in_t[r[4]:r[5]], zr(WB_CKV - Q_LORA), w_in_t[r[5]:r[7]], zr(LANES - ROPE),
                                 w_in_t[r[2]:r[4]], zr(LANES - 2 * N_HEADS)], axis=0)
    for k_ in ("wa", "wg", "wb"):
        w[k_] = jnp.transpose(w[k_ + "_t"])
    w["conv"] = conv_full
    pad_l = lambda v: jnp.pad(v, ((0, 0), (0, LANES - v.shape[1])))
    w["alog"], w["dtb"] = pad_l(small["dn_a_log"]), pad_l(small["dn_dt_bias"])
    w["dnw"], w["qnw"], w["kvnw"] = small["dn_norm_w"], small["q_norm_w"], small["kv_norm_w"]
    w["ln1g"], w["ln1b"], w["ln2g"], w["ln2b"] = small["ln1_g"], small["ln1_b"], small["ln2_g"], small["ln2_b"]
    return w


def _late_weights(group, fw):
    w = {}
    if group == "mix":
        uq = fw["w_uq"].reshape(Q_LORA, N_HEADS, HEAD + ROPE)
        uq_r = jnp.pad(uq[:, :, HEAD:], ((0, 0), (0, 0), (0, HEAD - ROPE)))
        w["uq"] = jnp.concatenate([uq[:, :, :HEAD].reshape(Q_LORA, -1), uq_r.reshape(Q_LORA, -1)], axis=1)
        w["uk"], w["uv"] = fw["w_uk"], fw["w_uv"]
        w["br_dn"], w["br_mla"], w["wo"] = fw["w_br_dn"], fw["w_br_mla"], fw["w_o"]
    else:
        w["ffn_in_t"], w["ffn_out"] = fw["w_ffn_in"], fw["w_ffn_out"]
        w["ple_t"], w["ple_gate"] = fw["w_ple"], fw["w_ple_gate"]
        for k_ in ("ffn_in", "ple"):
            w[k_] = jnp.transpose(w[k_ + "_t"])
    return w


_GROUP_GRADS = {"ffn": (("w_ple", "ple_t"), ("w_ple_gate", "ple_gate"), ("w_ffn_out", "ffn_out"),
                        ("w_ffn_in", "ffn_in_t")),
                "mix": (("w_o", "wo"), ("w_br_mla", "br_mla"), ("w_uq", "uq"), ("w_uk", "uk"), ("w_uv", "uv"),
                        ("w_br_dn", "br_dn"))}


def _group_grads(group, g):
    out = {}
    for name, key in _GROUP_GRADS[group]:
        t = g[key]
        if name == "w_uq":
            uq_n = t[:, :D_MODEL].reshape(Q_LORA, N_HEADS, HEAD)
            uq_r = t[:, D_MODEL:].reshape(Q_LORA, N_HEADS, HEAD)[:, :, :ROPE]
            t = jnp.concatenate([uq_n, uq_r], axis=2).reshape(Q_LORA, -1)
        out[name] = t
    return out


def _last_grads(g):
    wb = g["wb_t"]
    w_in = jnp.concatenate([
        g["wa_qkv_t"], g["wa_z_t"], wb[WB_BA:WB_BA + 2 * N_HEADS], wb[WB_CQ:WB_CQ + Q_LORA],
        wb[WB_CKV:WB_CKV + KV_LORA], wb[WB_KR:WB_KR + ROPE], g["wg_t"]], axis=0)
    small = {"ln1_g": g["ln1g"], "ln1_b": g["ln1b"], "ln2_g": g["ln2g"], "ln2_b": g["ln2b"], "q_norm_w": g["qnw"],
             "kv_norm_w": g["kvnw"], "dn_norm_w": g["dnw"], "dn_a_log": g["alog"], "dn_dt_bias": g["dtb"],
             "conv_w": g["conv"]}
    return w_in, small


_SMALL_SLOTS = {"ln1_g": (0, 0, 1024), "ln1_b": (1, 0, 1024), "ln2_g": (2, 0, 1024), "ln2_b": (3, 0, 1024),
                "q_norm_w": (4, 0, 384), "kv_norm_w": (4, 384, 256), "dn_norm_w": (4, 640, 128),
                "dn_a_log": (4, 768, 8), "dn_dt_bias": (4, 896, 8)}
_SMALL_ROWS, _LOSS_ROW, _CONV_ROW0, _CONV_ROWS = 24, 5, 8, 12


def _pack_small_grads(small_g, loss):
    zeros = lambda r, c: jnp.zeros((r, c), F32)
    row4 = jnp.concatenate([small_g["q_norm_w"], small_g["kv_norm_w"], small_g["dn_norm_w"], small_g["dn_a_log"],
                            small_g["dn_dt_bias"]], axis=1)
    row5 = jnp.concatenate([loss, zeros(1, FLAT_COLS - LANES)], axis=1)
    head = jnp.concatenate([small_g["ln1_g"], small_g["ln1_b"], small_g["ln2_g"], small_g["ln2_b"], row4, row5,
                            zeros(2, FLAT_COLS)], axis=0)
    conv = small_g["conv_w"].reshape(_CONV_ROWS, FLAT_COLS)
    return jnp.concatenate([head, conv, zeros(_SMALL_ROWS - _CONV_ROW0 - _CONV_ROWS, FLAT_COLS)], axis=0)


_MESH_ID = pl.DeviceIdType.MESH
_ANY = pl.BlockSpec(memory_space=pl.ANY)


def _all_gather(blocks, name):
    n = len(blocks)

    def body(*refs):
        x_refs, out_refs = refs[:n], refs[n:2 * n]
        send_sems, recv_sems, local_sems = refs[2 * n:]
        x, y, c = lax.axis_index("x"), lax.axis_index("y"), lax.axis_index("c")
        me, sibling = (x, y, c), (x, y, 1 - c)
        chips = [(1 - x, y), (x, 1 - y), (1 - x, 1 - y)]

        def slot(i, px, py, pc):
            return out_refs[i].at[4 * px + 2 * py + pc]

        def copy(i, k, origin, to, src=None):
            return pltpu.make_async_remote_copy(
                src_ref=slot(i, *origin) if src is None else src, dst_ref=slot(i, *origin),
                send_sem=send_sems.at[7 * i + k], recv_sem=recv_sems.at[7 * i + k], device_id=to,
                device_id_type=_MESH_ID)

        mine = [pltpu.make_async_copy(x_refs[i], slot(i, *me), local_sems.at[i]) for i in range(n)]
        first, passed = [], []
        for i in range(n):
            mine[i].start()
            first.append(copy(i, 0, me, sibling, src=x_refs[i]))
            first += [copy(i, 1 + j, me, (*chip, c), src=x_refs[i]) for j, chip in enumerate(chips)]
        for cp in first:
            cp.start()
        for i in range(n):
            for j, chip in enumerate(chips):
                copy(i, 1 + j, (*chip, c), me).wait_recv()
                passed.append(copy(i, 4 + j, (*chip, c), sibling))
                passed[-1].start()
        for i in range(n):
            copy(i, 0, sibling, me).wait_recv()
            for j, chip in enumerate(chips):
                copy(i, 4 + j, (*chip, 1 - c), me).wait_recv()
        for cp in first + passed:
            cp.wait_send()
        for cp in mine:
            cp.wait()

    return pl.pallas_call(
        body,
        out_shape=[jax.ShapeDtypeStruct((N_DEV,) + b.shape, b.dtype) for b in blocks],
        in_specs=[_ANY] * n,
        out_specs=[_ANY] * n,
        scratch_shapes=[pltpu.SemaphoreType.DMA((7 * n,)), pltpu.SemaphoreType.DMA((7 * n,)),
                        pltpu.SemaphoreType.DMA((n,))],
        name=name,
    )(*blocks)


def _exchange_sibling(srcs, name):
    n = len(srcs)

    def body(*refs):
        src_refs, dst_refs = refs[:n], refs[n:2 * n]
        send_sems, recv_sems = refs[2 * n:]
        x, y, c = lax.axis_index("x"), lax.axis_index("y"), lax.axis_index("c")
        copies = [pltpu.make_async_remote_copy(
            src_ref=src_refs[i].at[2 * q + (1 - c)], dst_ref=dst_refs[i].at[q], send_sem=send_sems.at[4 * i + q],
            recv_sem=recv_sems.at[4 * i + q], device_id=(x, y, 1 - c), device_id_type=_MESH_ID)
            for i in range(n) for q in range(4)]
        for cp in copies:
            cp.start()
        for cp in copies:
            cp.wait_recv()
        for cp in copies:
            cp.wait_send()

    return pl.pallas_call(
        body,
        out_shape=[jax.ShapeDtypeStruct((4,) + s.shape[1:], s.dtype) for s in srcs],
        in_specs=[_ANY] * n,
        out_specs=[_ANY] * n,
        scratch_shapes=[pltpu.SemaphoreType.DMA((4 * n,)), pltpu.SemaphoreType.DMA((4 * n,))],
        name=name,
    )(*srcs)


def _exchange_chips(srcs, name):
    n = len(srcs)

    def body(*refs):
        src_refs, dst_refs = refs[:n], refs[n:2 * n]
        send_sems, recv_sems = refs[2 * n:]
        x, y, c = lax.axis_index("x"), lax.axis_index("y"), lax.axis_index("c")
        chips = [(1 - x, y), (x, 1 - y), (1 - x, 1 - y)]
        copies = [pltpu.make_async_remote_copy(
            src_ref=src_refs[i].at[2 * tx + ty], dst_ref=dst_refs[i].at[j], send_sem=send_sems.at[3 * i + j],
            recv_sem=recv_sems.at[3 * i + j], device_id=(tx, ty, c), device_id_type=_MESH_ID)
            for i in range(n) for j, (tx, ty) in enumerate(chips)]
        for cp in copies:
            cp.start()
        for cp in copies:
            cp.wait_recv()
        for cp in copies:
            cp.wait_send()

    return pl.pallas_call(
        body,
        out_shape=[jax.ShapeDtypeStruct((3,) + s.shape[1:], s.dtype) for s in srcs],
        in_specs=[_ANY] * n,
        out_specs=[_ANY] * n,
        scratch_shapes=[pltpu.SemaphoreType.DMA((3 * n,)), pltpu.SemaphoreType.DMA((3 * n,))],
        name=name,
    )(*srcs)


def _col_tile(c):
    return c if c <= 256 else 256


def _chip_sum(src, recv, parity, name):
    _, r, c = src.shape
    tc = _col_tile(c)

    def body(par_ref, a_ref, b_ref, o_ref, ob_ref):
        s = a_ref[...] + b_ref[...]
        o_ref[...] = s
        ob_ref[...] = s.astype(BF16)

    blk = lambda f: pl.BlockSpec((None, r, tc), f)
    return pl.pallas_call(
        body,
        out_shape=[jax.ShapeDtypeStruct((4, r, c), F32), jax.ShapeDtypeStruct((4, r, c), BF16)],
        grid_spec=pltpu.PrefetchScalarGridSpec(
            num_scalar_prefetch=1, grid=(4, c // tc),
            in_specs=[blk(lambda q, j, par: (2 * q + par[0], 0, j)), blk(lambda q, j, par: (q, 0, j))],
            out_specs=[blk(lambda q, j, par: (q, 0, j)), blk(lambda q, j, par: (q, 0, j))]),
        compiler_params=pltpu.CompilerParams(dimension_semantics=("parallel", "parallel")),
        name=name,
    )(parity, src, recv)


def _sum_parts(own, others, chip, name):
    _, r, c = own.shape
    tc = _col_tile(c)

    def body(q_ref, a_ref, b_ref, o_ref):
        o_ref[...] = ((a_ref[...] + b_ref[0].astype(F32)) + b_ref[1].astype(F32)) + b_ref[2].astype(F32)

    return pl.pallas_call(
        body,
        out_shape=jax.ShapeDtypeStruct((r, c), F32),
        grid_spec=pltpu.PrefetchScalarGridSpec(
            num_scalar_prefetch=1, grid=(c // tc,),
            in_specs=[pl.BlockSpec((None, r, tc), lambda j, q: (q[0], 0, j)),
                      pl.BlockSpec((3, r, tc), lambda j, q: (0, 0, j))],
            out_specs=pl.BlockSpec((r, tc), lambda j, q: (0, j))),
        compiler_params=pltpu.CompilerParams(dimension_semantics=("parallel",)),
        name=name,
    )(chip, own, others)


_HBM = pl.BlockSpec(memory_space=pltpu.HBM)
_SEM = pl.BlockSpec(memory_space=pltpu.SEMAPHORE)
_DATAFLOW = pltpu.SideEffectType.DATAFLOW_SIDE_EFFECTING
N_PEERS = N_DEV - 1


def _ring_peer(j):
    me = 4 * lax.axis_index("x") + 2 * lax.axis_index("y") + lax.axis_index("c")
    k = (me + j) % N_DEV
    return me, k, (k // 4, (k // 2) % 2, k % 2)


def _spread_copy(i, j, src_refs, land_refs, send_sems, recv_sems, scatter):
    me, k, peer = _ring_peer(j)
    return pltpu.make_async_remote_copy(
        src_ref=src_refs[i].at[k] if scatter else src_refs[i], dst_ref=land_refs[i].at[me],
        send_sem=send_sems.at[N_PEERS * i + j - 1], recv_sem=recv_sems.at[N_PEERS * i + j - 1], device_id=peer,
        device_id_type=_MESH_ID)


def _spread_start(srcs, carry, scatter, name):
    n = len(srcs)
    lands = [lax.empty(((N_DEV,) + s.shape[-2:]), s.dtype) for s in srcs]

    def body(*refs):
        src_refs, land_refs = refs[:n], refs[n:2 * n]
        send_sems, recv_sems, local_sems = refs[2 * n + 1:2 * n + 4]
        for i in range(n):
            for j in range(1, N_DEV):
                _spread_copy(i, j, src_refs, land_refs, send_sems, recv_sems, scatter).start()
        for i in range(n):
            _own_copy(i, src_refs, land_refs, local_sems, scatter).start()

    hbm = lambda a: pltpu.HBM(a.shape, a.dtype)
    sems = pltpu.SemaphoreType.DMA((N_PEERS * n,))
    pinned = [pltpu.with_memory_space_constraint(a, pltpu.HBM) for a in list(srcs) + lands + [carry]]
    res = pl.pallas_call(
        body, name=name,
        out_shape=(sems, sems, pltpu.SemaphoreType.DMA((n,)), *[hbm(a) for a in pinned]),
        in_specs=[_HBM] * (2 * n + 1),
        out_specs=(_SEM, _SEM, _SEM, *[_HBM] * (2 * n + 1)),
        input_output_aliases={i: 3 + i for i in range(2 * n + 1)},
        compiler_params=pltpu.CompilerParams(has_side_effects=_DATAFLOW),
    )(*pinned)
    return res[:3], list(res[3:3 + n]), list(res[3 + n:3 + 2 * n]), res[3 + 2 * n]


def _own_copy(i, src_refs, land_refs, local_sems, scatter):
    me = _ring_peer(0)[0]
    return pltpu.make_async_copy(src_refs[i].at[me] if scatter else src_refs[i], land_refs[i].at[me],
                                 local_sems.at[i])


def _spread_wait(started, after, scatter, name):
    sems, srcs, lands, _ = started
    n = len(srcs)

    def body(*refs):
        src_refs, land_refs = refs[:n], refs[n:2 * n]
        send_s, recv_s, local_s = refs[2 * n:2 * n + 3]
        for i in range(n):
            for j in range(1, N_DEV):
                cp = _spread_copy(i, j, src_refs, land_refs, send_s, recv_s, scatter)
                cp.wait_send()
                cp.wait_recv()
        for i in range(n):
            _own_copy(i, src_refs, land_refs, local_s, scatter).wait()

    hbm = lambda a: pltpu.HBM(a.shape, a.dtype)
    res = pl.pallas_call(
        body, name=name,
        out_shape=tuple(hbm(a) for a in srcs + lands),
        in_specs=[_HBM] * (2 * n) + [_SEM, _SEM, _SEM, pl.BlockSpec(memory_space=pl.ANY)],
        out_specs=tuple([_HBM] * (2 * n)),
        input_output_aliases={i: i for i in range(2 * n)},
        compiler_params=pltpu.CompilerParams(has_side_effects=_DATAFLOW),
    )(*srcs, *lands, *sems, after)
    return list(res[n:])


def _sum8(landing, name):
    _, r, c = landing.shape
    tc = _col_tile(c)

    def body(a_ref, o_ref):
        tot = a_ref[0].astype(F32)
        for k in range(1, N_DEV):
            tot = tot + a_ref[k].astype(F32)
        o_ref[...] = tot

    return pl.pallas_call(
        body,
        out_shape=jax.ShapeDtypeStruct((r, c), F32),
        grid=(c // tc,),
        in_specs=[pl.BlockSpec((N_DEV, r, tc), lambda j: (0, 0, j))],
        out_specs=pl.BlockSpec((r, tc), lambda j: (0, j)),
        compiler_params=pltpu.CompilerParams(dimension_semantics=("parallel",)),
        name=name,
    )(landing)


def _adamw_math(w, g, m, v):
    m = ADAM_B1 * m + (1.0 - ADAM_B1) * g
    v = ADAM_B2 * v + (1.0 - ADAM_B2) * (g * g)
    m_hat = m / (1.0 - ADAM_B1 ** ADAM_STEP)
    v_hat = v / (1.0 - ADAM_B2 ** ADAM_STEP)
    delta = -ADAM_LR * (m_hat / (jnp.sqrt(v_hat) + ADAM_EPS) + ADAM_WD * w)
    return delta, m, v


def _adamw(w, m, v, g, name):
    r, c = w.shape

    def fn(rows, consts):
        return list(_adamw_math(*rows)), []

    return _rowwise(fn, [w, g, m, v], [], [(c, F32)] * 3, tm=r if r <= 512 else 256, name=name)


def _adamw_small(gathered, params):
    ns = len(_SMALL)

    def body(*refs):
        g_ref, p_refs, o_refs = refs[0], refs[1:1 + 3 * ns], refs[1 + 3 * ns:]
        tot = g_ref[0]
        for k in range(1, N_DEV):
            tot = tot + g_ref[k]
        for i, name in enumerate(_SMALL):
            row, lane0, lanes = _SMALL_SLOTS[name]
            g = tot[row:row + 1, lane0:lane0 + lanes]
            w_, m_, v_ = (p_refs[3 * i + j][...] for j in range(3))
            delta, m2, v2 = _adamw_math(w_, g, m_, v_)
            for j, val in enumerate((g, delta, m2, v2)):
                o_refs[4 * i + j][...] = val
        o_refs[4 * ns][...] = tot[_LOSS_ROW:_LOSS_ROW + 1, 0:LANES]
        o_refs[4 * ns + 1][...] = tot[_CONV_ROW0:_CONV_ROW0 + _CONV_ROWS, :]

    out_shape = [jax.ShapeDtypeStruct(w.shape, F32) for (w, _, _) in params for _ in range(4)]
    out_shape += [jax.ShapeDtypeStruct((1, LANES), F32), jax.ShapeDtypeStruct((_CONV_ROWS, FLAT_COLS), F32)]
    flat = [a for wmv in params for a in wmv]
    return pl.pallas_call(body, out_shape=out_shape, name="adamw_small")(gathered, *flat)


def kernel(x, p, positions, w_in, conv_w, dn_a_log, dn_dt_bias, dn_norm_w, q_norm_w, w_uq, kv_norm_w, w_uk, w_uv, w_br_dn, w_br_mla, w_o, ln1_g, ln1_b, w_ffn_in, w_ffn_out, w_ple, w_ple_gate, ln2_g, ln2_b, loss_target, m_w_in, m_conv_w, m_dn_a_log, m_dn_dt_bias, m_dn_norm_w, m_q_norm_w, m_w_uq, m_kv_norm_w, m_w_uk, m_w_uv, m_w_br_dn, m_w_br_mla, m_w_o, m_ln1_g, m_ln1_b, m_w_ffn_in, m_w_ffn_out, m_w_ple, m_w_ple_gate, m_ln2_g, m_ln2_b, v_w_in, v_conv_w, v_dn_a_log, v_dn_dt_bias, v_dn_norm_w, v_q_norm_w, v_w_uq, v_kv_norm_w, v_w_uk, v_w_uv, v_w_br_dn, v_w_br_mla, v_w_o, v_ln1_g, v_ln1_b, v_w_ffn_in, v_w_ffn_out, v_w_ple, v_w_ple_gate, v_ln2_g, v_ln2_b):
    args = dict(locals())
    wts = {n: args[n] for n in _ORDER}
    mom1 = {n: args["m_" + n] for n in _ORDER}
    mom2 = {n: args["v_" + n] for n in _ORDER}
    big_names = [n for n, _ in _BIG]
    shard_shapes = {n: wts[n].shape[1:] for n in big_names}
    c_idx = lax.axis_index("c")
    q_idx = 2 * lax.axis_index("x") + lax.axis_index("y")
    parity, chip = c_idx.reshape(1).astype(jnp.int32), q_idx.reshape(1).astype(jnp.int32)

    stored = {n: _to_stored(n, wts[n][0]).astype(BF16) for n in big_names}
    first = _all_gather([stored["w_in"], conv_w[0]], "ag_first")
    group_names = {grp: [n for n, _ in pairs] for grp, pairs in _GROUP_GRADS.items()}
    carry, gathers = first[0], {}
    for grp in ("mix", "ffn"):
        gathers[grp] = _spread_start([stored[n] for n in group_names[grp]], carry, False, "ag_start_" + grp)
        carry = gathers[grp][3]
    conv_full = jnp.moveaxis(first[1], 0, 1).reshape(conv_w.shape[1], -1)
    small_w = {n: wts[n].astype(F32) for n in _SMALL}
    w = _first_weights(carry.reshape(-1, D_MODEL), conv_full, small_w)

    def late_weights(grp, after):
        got = _spread_wait(gathers[grp], after, False, "ag_wait_" + grp)
        return _late_weights(grp, {n: t.reshape(-1, t.shape[-1]) for n, t in zip(group_names[grp], got)})

    started = {}

    def emit(group, g, carry):
        grads = _group_grads(group, g)
        srcs = [grads[n].reshape((N_DEV,) + _stored_shape(n, shard_shapes[n])) for n in grads]
        started[group] = (list(grads), _spread_start(srcs, carry, True, "rs_start_" + group))
        return started[group][1][3]

    s_dim = x.shape[1]
    loss, dx, g = _local_step(x[0], p[0, 0], positions.reshape(s_dim, 1).astype(F32), loss_target[0], w,
                              late_weights, emit)
    g_w_in, small_g = _last_grads(g)

    src = g_w_in.reshape((N_DEV,) + _stored_shape("w_in", shard_shapes["w_in"]))
    from_sibling = _exchange_sibling([src], "rs_sibling")[0]
    own, own_bf = _chip_sum(src, from_sibling, parity, "rs_sum_w_in")
    from_chips = _exchange_chips([own_bf], "rs_chips")[0]

    out_g, out_d, out_m, out_v = {}, {}, {}, {}

    def update(n, grad, shp):
        flat2 = (shp[0], int(np.prod(shp[1:])))
        d, m2, v2 = _adamw(wts[n][0].reshape(flat2), mom1[n][0].reshape(flat2), mom2[n][0].reshape(flat2),
                           grad.reshape(flat2), "adamw_" + n)
        out_g[n], out_d[n], out_m[n], out_v[n] = grad, d.reshape(shp), m2.reshape(shp), v2.reshape(shp)

    total = _sum_parts(own, from_chips, chip, "rs_total_w_in")
    update("w_in", _from_stored("w_in", total, shard_shapes["w_in"]), shard_shapes["w_in"])
    for group, (names, st) in started.items():
        for n, landing in zip(names, _spread_wait(st, dx, True, "rs_wait_" + group)):
            update(n, _from_stored(n, _sum8(landing, "rs_total_" + n), shard_shapes[n]), shard_shapes[n])

    g_small = _all_gather([_pack_small_grads(small_g, loss)], "ag_small")[0]
    res = _adamw_small(g_small, [(wts[n], mom1[n], mom2[n]) for n in _SMALL])
    for i, n in enumerate(_SMALL):
        out_g[n], out_d[n], out_m[n], out_v[n] = res[4 * i:4 * i + 4]
    loss_out = res[4 * len(_SMALL)][0, 0]
    conv_shape = conv_w.shape[1:]
    conv_g = lax.dynamic_slice(res[-1].reshape(conv_shape[0], -1), (0, (2 * q_idx + c_idx) * conv_shape[1]),
                               conv_shape)
    update("conv_w", conv_g, conv_shape)

    expand = lambda d, n: d[n] if n in _SMALL else d[n][None]
    return (loss_out, dx[None], *[expand(out_g, n) for n in _ORDER], *[expand(out_d, n) for n in _ORDER],
            *[expand(out_m, n) for n in _ORDER], *[expand(out_v, n) for n in _ORDER])
```

```python
import functools

import numpy as np
import jax
import jax.numpy as jnp
from jax import lax
from jax.experimental import pallas as pl
from jax.experimental.pallas import tpu as pltpu

F32 = jnp.float32
BF16 = jnp.bfloat16

D_MODEL = 1024
N_HEADS = 8
HEAD = 128
CHUNK = 64
GROUP = 256
ROPE = 64
Q_LORA = 384
KV_LORA = 256
FFN_HIDDEN = 2816
PLE_DIM = 256
ROPE_BASE = 10000.0
ALPHA = 2.0 ** 0.25
SCALE = float((HEAD + ROPE) ** -0.5)
NEG_BIG = -1e30
EPS_RMS = 1e-6
EPS_LN = 1e-5

ADAM_LR = 0.001
ADAM_B1 = 0.9
ADAM_B2 = 0.999
ADAM_EPS = 1e-08
ADAM_WD = 0.01
ADAM_STEP = 10

N_DEV = 8
LANES = 128
FLAT_COLS = 1024

WB_CQ, WB_CKV, WB_KR, WB_BA, WB_COLS = 0, 512, 768, 896, 1024

HIGHEST = lax.Precision.HIGHEST

NN = (((1,), (0,)), ((), ()))
TN = (((0,), (0,)), ((), ()))
NT = (((1,), (1,)), ((), ()))


def _dot(a, b, dims=NN):
    return lax.dot_general(a.astype(BF16), b.astype(BF16), dims, preferred_element_type=F32)


def _dot32(a, b, dims=NN):
    return lax.dot_general(a, b, dims, precision=HIGHEST, preferred_element_type=F32)


def _sig(x):
    return 1.0 / (1.0 + jnp.exp(-x))


MM_TILE = 1536


def _pick_wide(n):
    if n <= MM_TILE:
        return n
    return max(t for t in range(LANES, MM_TILE + 1, LANES) if n % t == 0)


def _split_bf16(a):
    hi = a.astype(BF16)
    return hi, (a - hi.astype(F32)).astype(BF16)


def _dot3(a, b, dims=NN):
    ah, al = a if isinstance(a, tuple) else _split_bf16(a)
    bh, bl = b if isinstance(b, tuple) else _split_bf16(b)
    d = lambda p, q: lax.dot_general(p, q, dims, preferred_element_type=F32)
    return d(ah, bh) + (d(ah, bl) + d(al, bh))


def _mm(a, b, *, ta=False, tb=False, add=(), out_dtype=F32, name):
    if ta:
        k_dim, m_dim = a.shape
    else:
        m_dim, k_dim = a.shape
    if tb:
        n_dim, k2 = b.shape
    else:
        k2, n_dim = b.shape
    assert k_dim == k2, (a.shape, b.shape, ta, tb)
    tm = _pick_wide(m_dim)
    tn = _pick_wide(n_dim)
    tk = _pick_wide(k_dim)
    nk = k_dim // tk
    n_add = len(add)
    dims = TN if ta else (NT if tb else NN)
    assert not (ta and tb)

    def body(a_ref, b_ref, *rest):
        add_refs = rest[:n_add]
        o_ref = rest[n_add]
        acc = rest[n_add + 1]
        k = pl.program_id(2)

        @pl.when(k == 0)
        def _():
            acc[...] = jnp.zeros_like(acc)

        acc[...] += _dot(a_ref[...], b_ref[...], dims)

        @pl.when(k == nk - 1)
        def _():
            r = acc[...]
            for ar in add_refs:
                r = r + ar[...].astype(F32)
            o_ref[...] = r.astype(o_ref.dtype)

    a_spec = pl.BlockSpec((tk, tm), lambda i, j, k: (k, i)) if ta else pl.BlockSpec((tm, tk), lambda i, j, k: (i, k))
    b_spec = pl.BlockSpec((tn, tk), lambda i, j, k: (j, k)) if tb else pl.BlockSpec((tk, tn), lambda i, j, k: (k, j))
    o_spec = pl.BlockSpec((tm, tn), lambda i, j, k: (i, j))
    return pl.pallas_call(
        body,
        out_shape=jax.ShapeDtypeStruct((m_dim, n_dim), out_dtype),
        grid=(m_dim // tm, n_dim // tn, nk),
        in_specs=[a_spec, b_spec] + [o_spec] * n_add,
        out_specs=o_spec,
        scratch_shapes=[pltpu.VMEM((tm, tn), F32)],
        compiler_params=pltpu.CompilerParams(dimension_semantics=("parallel", "parallel", "arbitrary")),
        name=name,
    )(a, b, *add)


def _rowwise(fn, rows, consts, outs, accs=(), *, tm=256, name):
    rows = [r if isinstance(r, tuple) else (r, 0, r.shape[1]) for r in rows]
    s_dim = rows[0][0].shape[0]
    tm = min(tm, s_dim)
    assert s_dim % tm == 0 and all(arr.shape[0] == s_dim for arr, _, _ in rows)
    specs = [pl.BlockSpec((tm, width), functools.partial(lambda i, cb: (i, cb), cb=cb)) for _, cb, width in rows]
    args = [arr for arr, _, _ in rows]
    for c in consts:
        specs.append(pl.BlockSpec(c.shape, lambda i: (0, 0)))
        args.append(c)
    nr, nc, no = len(rows), len(consts), len(outs)
    out_shape = [jax.ShapeDtypeStruct((s_dim, w), dt) for (w, dt) in outs]
    out_specs = [pl.BlockSpec((tm, w), lambda i: (i, 0)) for (w, dt) in outs]
    out_shape += [jax.ShapeDtypeStruct(sh, F32) for sh in accs]
    out_specs += [pl.BlockSpec(sh, lambda i: (0, 0)) for sh in accs]

    def body(*refs):
        r = [x[...].astype(F32) if x.dtype == BF16 else x[...] for x in refs[:nr]]
        c = [x[...] for x in refs[nr:nr + nc]]
        o_refs = refs[nr + nc:nr + nc + no]
        a_refs = refs[nr + nc + no:]
        o_vals, a_vals = fn(r, c)
        for ref, v in zip(o_refs, o_vals, strict=True):
            ref[...] = v.astype(ref.dtype)
        if a_refs:
            @pl.when(pl.program_id(0) == 0)
            def _():
                for ref in a_refs:
                    ref[...] = jnp.zeros_like(ref)

            for ref, v in zip(a_refs, a_vals, strict=True):
                ref[...] += v

    res = pl.pallas_call(
        body,
        out_shape=out_shape,
        grid=(s_dim // tm,),
        in_specs=specs,
        out_specs=out_specs,
        compiler_params=pltpu.CompilerParams(dimension_semantics=("arbitrary" if accs else "parallel",)),
        name=name,
    )(*args)
    return res


def _colsum(v):
    return jnp.sum(v, axis=0, keepdims=True)


def _rowsum(v):
    return jnp.sum(v, axis=1, keepdims=True)


def _rowmean(v):
    return jnp.mean(v, axis=1, keepdims=True)


def _silu_grad(x):
    s = _sig(x)
    return s * (1.0 + x * (1.0 - s))


def _conv_taps(x, w, width=4):
    row = lax.broadcasted_iota(jnp.int32, x.shape, 0)
    c = x * w[width - 1:width, :]
    for s in range(1, width):
        c = c + jnp.where(row >= s, pltpu.roll(x, s, 0), 0.0) * w[width - 1 - s:width - s, :]
    return c


def _conv_fwd(proj_a, conv_w):
    s_dim = proj_a.shape[0]
    n_blk = 3 * N_HEADS

    def body(x_ref, w_ref, o_ref):
        j = pl.program_id(0)
        c = _conv_taps(x_ref[...], w_ref[...])
        y = c * _sig(c)
        r = lax.rsqrt(_rowsum(y * y) + EPS_RMS)
        fac = jnp.where(j < N_HEADS, r * (HEAD ** -0.5), jnp.where(j < 2 * N_HEADS, r, 1.0))
        o_ref[...] = y * fac

    return pl.pallas_call(
        body,
        out_shape=jax.ShapeDtypeStruct((s_dim, n_blk * HEAD), F32),
        grid=(n_blk,),
        in_specs=[pl.BlockSpec((s_dim, HEAD), lambda j: (0, j)), pl.BlockSpec((4, HEAD), lambda j: (0, j))],
        out_specs=pl.BlockSpec((s_dim, HEAD), lambda j: (0, j)),
        compiler_params=pltpu.CompilerParams(dimension_semantics=("parallel",)),
        name="conv_fwd",
    )(proj_a, conv_w)


def _conv_bwd(proj_a, conv_w, dq, dk, dv):
    s_dim = proj_a.shape[0]
    n_blk = 3 * N_HEADS

    def body(x_ref, w_ref, dq_ref, dk_ref, dv_ref, dx_ref, dw_ref):
        j = pl.program_id(0)
        x = x_ref[...]
        w = w_ref[...]
        do = jnp.where(j < N_HEADS, dq_ref[...], jnp.where(j < 2 * N_HEADS, dk_ref[...], dv_ref[...]))
        c = _conv_taps(x, w)
        sg = _sig(c)
        y = c * sg
        r = lax.rsqrt(_rowsum(y * y) + EPS_RMS)
        sc = jnp.where(j < N_HEADS, HEAD ** -0.5, 1.0)
        dy_n = sc * (r * do - y * (r * r * r) * _rowsum(do * y))
        dy = jnp.where(j < 2 * N_HEADS, dy_n, do)
        dc = dy * (sg * (1.0 + c * (1.0 - sg)))
        row = lax.broadcasted_iota(jnp.int32, x.shape, 0)
        dx = dc * w[3:4, :]
        dw_ref[3:4, :] = _colsum(dc * x)
        for s in range(1, 4):
            dx = dx + jnp.where(row < s_dim - s, pltpu.roll(dc, s_dim - s, 0), 0.0) * w[3 - s:4 - s, :]
            xs = jnp.where(row >= s, pltpu.roll(x, s, 0), 0.0)
            dw_ref[3 - s:4 - s, :] = _colsum(dc * xs)
        dx_ref[...] = dx.astype(dx_ref.dtype)

    hd = N_HEADS - 1
    return pl.pallas_call(
        body,
        out_shape=[jax.ShapeDtypeStruct((s_dim, n_blk * HEAD), BF16), jax.ShapeDtypeStruct((4, n_blk * HEAD), F32)],
        grid=(n_blk,),
        in_specs=[
            pl.BlockSpec((s_dim, HEAD), lambda j: (0, j)),
            pl.BlockSpec((4, HEAD), lambda j: (0, j)),
            pl.BlockSpec((s_dim, HEAD), lambda j: (0, jnp.minimum(j, hd))),
            pl.BlockSpec((s_dim, HEAD), lambda j: (0, jnp.clip(j - N_HEADS, 0, hd))),
            pl.BlockSpec((s_dim, HEAD), lambda j: (0, jnp.clip(j - 2 * N_HEADS, 0, hd))),
        ],
        out_specs=[pl.BlockSpec((s_dim, HEAD), lambda j: (0, j)), pl.BlockSpec((4, HEAD), lambda j: (0, j))],
        compiler_params=pltpu.CompilerParams(dimension_semantics=("parallel",)),
        name="conv_bwd",
    )(proj_a, conv_w, dq, dk, dv)


def _chunk_tri(n):
    r = np.arange(n)
    m = ((r[:, None] // CHUNK) == (r[None, :] // CHUNK)) & (r[:, None] >= r[None, :])
    m = m.astype(np.float32)
    return jnp.asarray(m), jnp.asarray(m.T)


def _softplus(z):
    return jnp.maximum(z, 0.0) + jnp.log(1.0 + jnp.exp(-jnp.abs(z)))


def _gates_fwd(proj_b, alog, dtb):
    tm = min(GROUP, proj_b.shape[0])
    tri, _ = _chunk_tri(tm)

    def fn(r, c):
        b = r[0]
        a = pltpu.roll(b, LANES - N_HEADS, 1)
        alog_, dtb_, tri_ = c
        g = -jnp.exp(alog_) * _softplus(a + dtb_)
        return [_sig(b), _dot32(tri_, g)], []

    return _rowwise(fn, [(proj_b, WB_BA // LANES, LANES)], [alog, dtb, tri],
                    [(LANES, F32), (LANES, F32)], tm=tm, name="gates_fwd")


def _gates_bwd(proj_b, alog, dtb, gc, d_beta, d_gc, d_egl_rows):
    tm = min(GROUP, proj_b.shape[0])
    _, tri_t = _chunk_tri(tm)

    def fn(r, c):
        b, gc_, d_beta_, d_gc_, d_egl_ = r
        a = pltpu.roll(b, LANES - N_HEADS, 1)
        alog_, dtb_, tri_t_ = c
        z = a + dtb_
        ea = jnp.exp(alog_)
        g = -ea * _softplus(z)
        dg = _dot32(tri_t_, d_gc_ + d_egl_ * jnp.exp(gc_))
        d_a = dg * (-ea) * _sig(z)
        beta = _sig(b)
        d_ba = d_beta_ * beta * (1.0 - beta) + pltpu.roll(d_a, N_HEADS, 1)
        return [d_ba], [_colsum(dg * g), _colsum(d_a)]

    return _rowwise(fn, [(proj_b, WB_BA // LANES, LANES), gc, d_beta, d_gc, d_egl_rows],
                    [alog, dtb, tri_t], [(LANES, BF16)], accs=[(1, LANES), (1, LANES)], tm=tm,
                    name="gates_bwd")


def _group_masks(n):
    r = lax.broadcasted_iota(jnp.int32, (n, n), 0)
    c = lax.broadcasted_iota(jnp.int32, (n, n), 1)
    same = (r // CHUNK) == (c // CHUNK)
    below, s = [], 2
    while s < CHUNK:
        below.append(jnp.logical_and((r // (2 * s)) == (c // (2 * s)),
                                     jnp.logical_and((r // s) % 2 == 1, (c // s) % 2 == 0)))
        s *= 2
    return dict(same=same, tril=jnp.logical_and(same, r >= c), strict=jnp.logical_and(same, r > c),
                last=c == (r // CHUNK) * CHUNK + (CHUNK - 1), eye=r == c, pair=(r // 2) == (c // 2), below=below)


def _inv_unit_lower(l_mats, mk):
    eye_f = mk["eye"].astype(F32)
    ts = [eye_f - jnp.where(mk["pair"], l_mat, 0.0) for l_mat in l_mats]
    for below in mk["below"]:
        halves = [_split_bf16(t) for t in ts]
        mids = [_dot3(h, jnp.where(below, l_mat, 0.0)) for h, l_mat in zip(halves, l_mats)]
        ts = [t - _dot3(m, h) for t, m, h in zip(ts, mids, halves)]
    return ts


def _unfold_blocks(folded, mask):
    n = folded.shape[0]
    return jnp.where(mask, jnp.concatenate([folded] * (n // CHUNK), axis=1), 0.0)


def _head_cols(beta, gc, gc_t, h):
    lane = lax.broadcasted_iota(jnp.int32, beta.shape, 1)
    sub = lax.broadcasted_iota(jnp.int32, gc_t.shape, 0)
    bcol = _rowsum(jnp.where(lane == h, beta, 0.0))
    gcol = _rowsum(jnp.where(lane == h, gc, 0.0))
    grow = _colsum(jnp.where(sub == h, gc_t, 0.0))
    return bcol, gcol, grow


def _prep_common(q, k, bcol, gcol, grow, mk, t_folded=None):
    n = q.shape[0]
    tril = mk["tril"]
    decay = jnp.where(tril, jnp.exp(jnp.where(tril, gcol - grow, 0.0)), 0.0)
    glast = _rowsum(jnp.where(mk["last"], jnp.broadcast_to(grow, (n, n)), 0.0))
    e = jnp.exp(gcol)
    ekt = jnp.exp(glast - gcol)
    kb = k * bcol
    kk = _dot(kb, k, NT)
    qk = _dot(q, k, NT)
    p = dict(decay=decay, e=e, ekt=ekt, kb=kb, kk=kk, qk=qk)
    if t_folded is not None:
        p["t"] = _unfold_blocks(t_folded, mk["same"])
    return p


GROUPS_PER_STEP = 4
SCAN_CHUNKS_PER_STEP = 4


def _fold_blocks(m):
    n = m.shape[0]
    out = m[:, 0:CHUNK]
    for b in range(1, n // CHUNK):
        out = out + m[:, b * CHUNK:(b + 1) * CHUNK]
    return out


def _gdr_prep_fwd(qkvn, beta, gc, gc_t):
    s_dim = qkvn.shape[0]
    tg = min(GROUP, s_dim)
    n_sub = min(GROUPS_PER_STEP, s_dim // tg)
    tb = tg * n_sub

    def body(q_ref, k_ref, v_ref, b_ref, g_ref, gt_ref, u_ref, w_ref, qd_ref, kt_ref, a_ref, t_ref):
        h = pl.program_id(0)
        mk = _group_masks(tg)
        parts = []
        for s in range(n_sub):
            rows = slice(s * tg, (s + 1) * tg)
            q, k, v = q_ref[rows, :], k_ref[rows, :], v_ref[rows, :]
            bcol, gcol, grow = _head_cols(b_ref[rows, :], g_ref[rows, :], gt_ref[:, rows], h)
            p = _prep_common(q, k, bcol, gcol, grow, mk)
            qd_ref[rows, :] = q * p["e"]
            kt_ref[rows, :] = k * p["ekt"]
            a_ref[rows, :] = _fold_blocks(jnp.where(mk["tril"], p["qk"] * p["decay"], 0.0))
            parts.append((rows, v * bcol, p["kb"] * p["e"], jnp.where(mk["strict"], p["kk"] * p["decay"], 0.0)))
        t_mats = _inv_unit_lower([part[3] for part in parts], mk)
        for (rows, vb, kbe, _), t_mat in zip(parts, t_mats):
            u_ref[rows, :] = _dot(t_mat, vb)
            w_ref[rows, :] = _dot(t_mat, kbe)
            t_ref[rows, :] = _fold_blocks(t_mat)

    row = lambda off: pl.BlockSpec((tb, HEAD), functools.partial(lambda h, m, off: (m, h + off), off=off))
    full = pl.BlockSpec((tb, LANES), lambda h, m: (m, 0))
    o_spec = pl.BlockSpec((tb, HEAD), lambda h, m: (m, h))
    a_spec = pl.BlockSpec((None, tb, CHUNK), lambda h, m: (h, m, 0))
    wide = jax.ShapeDtypeStruct((s_dim, N_HEADS * HEAD), F32)
    folded = jax.ShapeDtypeStruct((N_HEADS, s_dim, CHUNK), F32)
    return pl.pallas_call(
        body,
        out_shape=[wide, wide, wide, wide, folded, folded],
        grid=(N_HEADS, s_dim // tb),
        in_specs=[row(0), row(N_HEADS), row(2 * N_HEADS), full, full, pl.BlockSpec((8, tb), lambda h, m: (0, m))],
        out_specs=[o_spec, o_spec, o_spec, o_spec, a_spec, a_spec],
        compiler_params=pltpu.CompilerParams(dimension_semantics=("parallel", "parallel")),
        name="gdr_prep_fwd",
    )(qkvn, qkvn, qkvn, beta, gc, gc_t)


def _gdr_prep_bwd(qkvn, beta, gc, gc_t, t_fold, u, w, du, dw, dqd, dkt, d_a):
    s_dim = qkvn.shape[0]
    tg = min(GROUP, s_dim)
    n_sub = min(GROUPS_PER_STEP, s_dim // tg)
    tb = tg * n_sub

    def body(q_ref, k_ref, v_ref, b_ref, g_ref, gt_ref, t_ref, u_ref, w_ref, du_ref, dw_ref, dqd_ref, dkt_ref,
             da_ref, dq_ref, dk_ref, dv_ref, db_ref, dg_ref):
        h = pl.program_id(1)

        @pl.when(h == 0)
        def _():
            db_ref[...] = jnp.zeros_like(db_ref)
            dg_ref[...] = jnp.zeros_like(dg_ref)

        mk = _group_masks(tg)
        lane = lax.broadcasted_iota(jnp.int32, (tg, LANES), 1)
        for s in range(n_sub):
            rows = slice(s * tg, (s + 1) * tg)
            q, k, v = q_ref[rows, :], k_ref[rows, :], v_ref[rows, :]
            bcol, gcol, grow = _head_cols(b_ref[rows, :], g_ref[rows, :], gt_ref[:, rows], h)
            p = _prep_common(q, k, bcol, gcol, grow, mk, t_ref[rows, :])
            t_mat, decay, e, ekt, kb = p["t"], p["decay"], p["e"], p["ekt"], p["kb"]
            du_, dw_, dqd_, dkt_ = du_ref[rows, :], dw_ref[rows, :], dqd_ref[rows, :], dkt_ref[rows, :]
            dvb = _dot(t_mat, du_, TN)
            dkbe = _dot(t_mat, dw_, TN)
            d_l = -(_dot(dvb, u_ref[rows, :], NT) + _dot(dkbe, w_ref[rows, :], NT))
            m1 = jnp.where(mk["strict"], d_l, 0.0)
            m2 = _unfold_blocks(da_ref[rows, :], mk["tril"])
            d_kk = m1 * decay
            d_qk = m2 * decay
            d_decay = m1 * p["kk"] + m2 * p["qk"]
            dkb = _dot(d_kk, k) + dkbe * e
            dk = _dot(d_kk, kb, TN) + _dot(d_qk, q, TN) + dkt_ * ekt + dkb * bcol
            dq = _dot(d_qk, k) + dqd_ * e
            d_beta = _rowsum(dkb * k) + _rowsum(dvb * v)
            d_e = _rowsum(dkbe * kb) + _rowsum(dqd_ * q)
            d_ekt = _rowsum(dkt_ * k) * ekt
            d_diff = d_decay * decay
            d_grow = -_colsum(d_diff) + _colsum(jnp.where(mk["last"], jnp.broadcast_to(d_ekt, (tg, tg)), 0.0))
            d_gcol = d_e * e - d_ekt + _rowsum(d_diff)
            d_gcol = d_gcol + _rowsum(jnp.where(mk["eye"], jnp.broadcast_to(d_grow, (tg, tg)), 0.0))
            dq_ref[rows, :] = dq
            dk_ref[rows, :] = dk
            dv_ref[rows, :] = dvb * bcol
            db_ref[rows, :] = jnp.where(lane == h, d_beta, db_ref[rows, :])
            dg_ref[rows, :] = jnp.where(lane == h, d_gcol, dg_ref[rows, :])

    row = lambda off: pl.BlockSpec((tb, HEAD), functools.partial(lambda m, h, off: (m, h + off), off=off))
    full = pl.BlockSpec((tb, LANES), lambda m, h: (m, 0))
    o_spec = pl.BlockSpec((tb, HEAD), lambda m, h: (m, h))
    a_spec = pl.BlockSpec((None, tb, CHUNK), lambda m, h: (h, m, 0))
    wide = jax.ShapeDtypeStruct((s_dim, N_HEADS * HEAD), F32)
    lanes = jax.ShapeDtypeStruct((s_dim, LANES), F32)
    return pl.pallas_call(
        body,
        out_shape=[wide, wide, wide, lanes, lanes],
        grid=(s_dim // tb, N_HEADS),
        in_specs=[row(0), row(N_HEADS), row(2 * N_HEADS), full, full, pl.BlockSpec((8, tb), lambda m, h: (0, m)),
                  a_spec, o_spec, o_spec, o_spec, o_spec, o_spec, o_spec, a_spec],
        out_specs=[o_spec, o_spec, o_spec, full, full],
        compiler_params=pltpu.CompilerParams(dimension_semantics=("parallel", "arbitrary")),
        name="gdr_prep_bwd",
    )(qkvn, qkvn, qkvn, beta, gc, gc_t, t_fold, u, w, du, dw, dqd, dkt, d_a)


def _gdr_scan_fwd(u, w, qd, kt, a_mat, gc):
    s_dim = u.shape[0]
    n_chunks = s_dim // CHUNK
    per = min(SCAN_CHUNKS_PER_STEP, n_chunks)
    tb = per * CHUNK

    def body(u_ref, w_ref, qd_ref, kt_ref, a_ref, g_ref, o_ref, st_ref, state):
        @pl.when(pl.program_id(0) == 0)
        def _():
            state[...] = jnp.zeros_like(state)

        heads = range(N_HEADS)
        cols = [slice(h * HEAD, (h + 1) * HEAD) for h in heads]
        for i in range(per):
            rows = slice(i * CHUNK, (i + 1) * CHUNK)
            egl = jnp.exp(g_ref[(i + 1) * CHUNK - 1:(i + 1) * CHUNK, :])
            s_b = [state[h].astype(BF16) for h in heads]
            for h in heads:
                st_ref[i, h] = state[h]
            ws = [_dot(w_ref[rows, cs], s) for cs, s in zip(cols, s_b)]
            qs = [_dot(qd_ref[rows, cs], s) for cs, s in zip(cols, s_b)]
            vns = [(u_ref[rows, cs] - ws_h).astype(BF16) for cs, ws_h in zip(cols, ws)]
            avs = [_dot(a_ref[h, rows, :], vn) for h, vn in zip(heads, vns)]
            kvs = [_dot(kt_ref[rows, cs], vn, TN) for cs, vn in zip(cols, vns)]
            for h, cs in zip(heads, cols):
                o_ref[rows, cs] = qs[h] + avs[h]
                state[h] = state[h] * egl[:, h:h + 1] + kvs[h]

    wide = pl.BlockSpec((tb, N_HEADS * HEAD), lambda n: (n, 0))
    return pl.pallas_call(
        body,
        out_shape=[jax.ShapeDtypeStruct((s_dim, N_HEADS * HEAD), F32),
                   jax.ShapeDtypeStruct((n_chunks, N_HEADS, HEAD, HEAD), F32)],
        grid=(n_chunks // per,),
        in_specs=[wide, wide, wide, wide, pl.BlockSpec((N_HEADS, tb, CHUNK), lambda n: (0, n, 0)),
                  pl.BlockSpec((tb, LANES), lambda n: (n, 0))],
        out_specs=[wide, pl.BlockSpec((per, N_HEADS, HEAD, HEAD), lambda n: (n, 0, 0, 0))],
        scratch_shapes=[pltpu.VMEM((N_HEADS, HEAD, HEAD), F32)],
        compiler_params=pltpu.CompilerParams(dimension_semantics=("arbitrary",)),
        name="gdr_scan_fwd",
    )(u, w, qd, kt, a_mat, gc)


def _gdr_scan_bwd(u, w, qd, kt, a_mat, gc, states, d_o):
    s_dim = u.shape[0]
    n_chunks = s_dim // CHUNK
    per = min(SCAN_CHUNKS_PER_STEP, n_chunks)
    tb = per * CHUNK
    last = n_chunks // per - 1

    def body(u_ref, w_ref, qd_ref, kt_ref, a_ref, g_ref, st_ref, do_ref,
             du_ref, dw_ref, dqd_ref, dkt_ref, da_ref, de_ref, d_state):
        @pl.when(pl.program_id(0) == 0)
        def _():
            d_state[...] = jnp.zeros_like(d_state)

        heads = range(N_HEADS)
        cols = [slice(h * HEAD, (h + 1) * HEAD) for h in heads]
        for i in reversed(range(per)):
            rows = slice(i * CHUNK, (i + 1) * CHUNK)
            egl = jnp.exp(g_ref[(i + 1) * CHUNK - 1:(i + 1) * CHUNK, :])
            s_b = [st_ref[i, h].astype(BF16) for h in heads]
            ds_b = [d_state[h].astype(BF16) for h in heads]
            dos = [do_ref[rows, cs].astype(BF16) for cs in cols]
            w_b = [w_ref[rows, cs].astype(BF16) for cs in cols]
            ws = [_dot(w_h, s) for w_h, s in zip(w_b, s_b)]
            ados = [_dot(a_ref[h, rows, :], do, TN) for h, do in zip(heads, dos)]
            kds = [_dot(kt_ref[rows, cs], ds) for cs, ds in zip(cols, ds_b)]
            dqds = [_dot(do, s, NT) for do, s in zip(dos, s_b)]
            qdos = [_dot(qd_ref[rows, cs], do, TN) for cs, do in zip(cols, dos)]
            vns = [(u_ref[rows, cs] - ws_h).astype(BF16) for cs, ws_h in zip(cols, ws)]
            dvns = [a + k_ for a, k_ in zip(ados, kds)]
            dvn_b = [d.astype(BF16) for d in dvns]
            das = [_dot(do, vn, NT) for do, vn in zip(dos, vns)]
            dkts = [_dot(vn, ds, NT) for vn, ds in zip(vns, ds_b)]
            dws = [_dot(d, s, NT) for d, s in zip(dvn_b, s_b)]
            wds = [_dot(w_h, d, TN) for w_h, d in zip(w_b, dvn_b)]
            for h, cs in zip(heads, cols):
                ds_n = d_state[h]
                de = jnp.sum(_rowsum(ds_n * st_ref[i, h]), axis=0, keepdims=True)
                de_ref[i, h:h + 1, :] = jnp.broadcast_to(de, (1, LANES))
                dqd_ref[rows, cs] = dqds[h]
                da_ref[h, rows, :] = das[h]
                dkt_ref[rows, cs] = dkts[h]
                du_ref[rows, cs] = dvns[h]
                dw_ref[rows, cs] = -dws[h]
                d_state[h] = ds_n * egl[:, h:h + 1] + qdos[h] - wds[h]

    wide = pl.BlockSpec((tb, N_HEADS * HEAD), lambda n: (last - n, 0))
    a_spec = pl.BlockSpec((N_HEADS, tb, CHUNK), lambda n: (0, last - n, 0))
    wide_shape = jax.ShapeDtypeStruct((s_dim, N_HEADS * HEAD), F32)
    return pl.pallas_call(
        body,
        out_shape=[wide_shape, wide_shape, wide_shape, wide_shape,
                   jax.ShapeDtypeStruct((N_HEADS, s_dim, CHUNK), F32),
                   jax.ShapeDtypeStruct((n_chunks, N_HEADS, LANES), F32)],
        grid=(n_chunks // per,),
        in_specs=[wide, wide, wide, wide, a_spec, pl.BlockSpec((tb, LANES), lambda n: (last - n, 0)),
                  pl.BlockSpec((per, N_HEADS, HEAD, HEAD), lambda n: (last - n, 0, 0, 0)), wide],
        out_specs=[wide, wide, wide, wide, a_spec, pl.BlockSpec((per, N_HEADS, LANES), lambda n: (last - n, 0, 0))],
        scratch_shapes=[pltpu.VMEM((N_HEADS, HEAD, HEAD), F32)],
        compiler_params=pltpu.CompilerParams(dimension_semantics=("arbitrary",)),
        name="gdr_scan_bwd",
    )(u, w, qd, kt, a_mat, gc, states, d_o)


def _gdr_out_fwd(o_dn, proj_a, dn_w):
    def fn(r, c):
        o, z = r
        (w_,) = c
        outs = []
        for h in range(N_HEADS):
            cs = slice(h * HEAD, (h + 1) * HEAD)
            oh, zh = o[:, cs], z[:, cs]
            rr = lax.rsqrt(_rowmean(oh * oh) + EPS_RMS)
            outs.append(oh * rr * w_ * (zh * _sig(zh)))
        return [jnp.concatenate(outs, axis=1)], []

    return _rowwise(fn, [o_dn, (proj_a, 3, D_MODEL)], [dn_w], [(D_MODEL, BF16)], name="gdr_out_fwd")[0]


def _gdr_out_bwd(o_dn, proj_a, d_og, dn_w):
    def fn(r, c):
        o, z, dg = r
        (w_,) = c
        d_o, d_z = [], []
        d_w = jnp.zeros((1, HEAD), F32)
        for h in range(N_HEADS):
            cs = slice(h * HEAD, (h + 1) * HEAD)
            oh, zh, dgh = o[:, cs], z[:, cs], dg[:, cs]
            rr = lax.rsqrt(_rowmean(oh * oh) + EPS_RMS)
            sz = zh * _sig(zh)
            d_n = dgh * sz
            d_z.append(dgh * (oh * rr * w_) * _silu_grad(zh))
            d_w = d_w + _colsum(d_n * oh * rr)
            gw = d_n * w_
            d_o.append(rr * gw - oh * (rr * rr * rr) * _rowmean(gw * oh))
        return [jnp.concatenate(d_o, axis=1), jnp.concatenate(d_z, axis=1)], [d_w]

    return _rowwise(fn, [o_dn, (proj_a, 3, D_MODEL), d_og], [dn_w], [(D_MODEL, F32), (D_MODEL, BF16)],
                    accs=[(1, HEAD)], name="gdr_out_bwd")


def _rms_fwd(x, w):
    r = lax.rsqrt(_rowmean(x * x) + EPS_RMS)
    return x * r * w


def _rms_bwd(x, w, dy):
    r = lax.rsqrt(_rowmean(x * x) + EPS_RMS)
    gw = dy * w
    return r * gw - x * (r * r * r) * _rowmean(gw * x), _colsum(dy * x * r)


def _mla_norm_fwd(proj_b, qn_w, kvn_w):
    def fn(r, c):
        return [_rms_fwd(r[0], c[0]), _rms_fwd(r[1], c[1])], []

    return _rowwise(fn, [(proj_b, WB_CQ // Q_LORA, Q_LORA), (proj_b, WB_CKV // KV_LORA, KV_LORA)], [qn_w, kvn_w],
                    [(Q_LORA, BF16), (KV_LORA, BF16)], name="mla_norm_fwd")


def _mla_norm_bwd(proj_b, qn_w, kvn_w, d_cq, d_ckv):
    def fn(r, c):
        dx1, dw1 = _rms_bwd(r[0], c[0], r[2])
        dx2, dw2 = _rms_bwd(r[1], c[1], r[3])
        return [dx1, dx2], [dw1, dw2]

    return _rowwise(fn, [(proj_b, WB_CQ // Q_LORA, Q_LORA), (proj_b, WB_CKV // KV_LORA, KV_LORA), d_cq, d_ckv],
                    [qn_w, kvn_w], [(Q_LORA, BF16), (KV_LORA, BF16)], accs=[(1, Q_LORA), (1, KV_LORA)],
                    name="mla_norm_bwd")


def _rope_consts():
    inv = ROPE_BASE ** (-np.arange(0, ROPE, 2, dtype=np.float32) / ROPE)
    t = np.zeros((4, LANES), np.float32)
    t[0, :32] = inv
    t[0, 32:64] = inv
    t[1, :64] = 1.0
    t[2, 32:64] = 1.0
    t[3, :32] = -1.0
    return jnp.asarray(t)


def _rope_tables(pos, consts, width):
    ang = pos * consts[0:1, :]
    cosv, sinv = jnp.cos(ang), jnp.sin(ang)
    reps = width // LANES
    tile = (lambda t: jnp.concatenate([t] * reps, axis=1)) if reps > 1 else (lambda t: t)
    return tile(cosv * consts[1:2, :]), tile(sinv * consts[2:3, :]), tile(sinv * consts[3:4, :])


def _rope_apply(t, tabs):
    cos_t, sin_a, sin_b = tabs
    width = t.shape[1]
    return t * cos_t + pltpu.roll(t, 32, 1) * sin_a + pltpu.roll(t, width - 32, 1) * sin_b


def _rope_transpose(d, tabs):
    cos_t, sin_a, sin_b = tabs
    width = d.shape[1]
    return d * cos_t + pltpu.roll(d * sin_a, width - 32, 1) + pltpu.roll(d * sin_b, 32, 1)


QK_HEAD = 2 * HEAD


def _interleave_heads(a, b):
    parts = []
    for h in range(N_HEADS):
        parts.append(a[:, h * HEAD:(h + 1) * HEAD])
        parts.append(b if b.shape[1] == LANES else b[:, h * LANES:(h + 1) * LANES])
    return jnp.concatenate(parts, axis=1)


def _mla_qk_fwd(q_full, k_nope, proj_b, pos):
    consts = _rope_consts()

    def fn(r, c):
        qf, kn, kr, pos_ = r
        qn, qr = qf[:, :D_MODEL], qf[:, D_MODEL:]
        qr = _rope_apply(qr, _rope_tables(pos_, c[0], D_MODEL))
        kr = _rope_apply(kr, _rope_tables(pos_, c[0], LANES))
        return [_interleave_heads(qn, qr) * SCALE, _interleave_heads(kn, kr)], []

    return _rowwise(fn, [q_full, k_nope, (proj_b, WB_KR // LANES, LANES), pos], [consts],
                    [(N_HEADS * QK_HEAD, BF16), (N_HEADS * QK_HEAD, BF16)], name="mla_qk_fwd")


def _mla_qk_bwd(d_qc, d_kc, pos):
    consts = _rope_consts()

    def fn(r, c):
        dq, dk, pos_ = r
        even = lambda t: jnp.concatenate([t[:, (2 * h) * LANES:(2 * h + 1) * LANES] for h in range(N_HEADS)], axis=1)
        odd = lambda t: jnp.concatenate([t[:, (2 * h + 1) * LANES:(2 * h + 2) * LANES] for h in range(N_HEADS)], axis=1)
        d_qr_raw = _rope_transpose(odd(dq), _rope_tables(pos_, c[0], D_MODEL)) * SCALE
        dkr = dk[:, LANES:2 * LANES]
        for h in range(1, N_HEADS):
            dkr = dkr + dk[:, (2 * h + 1) * LANES:(2 * h + 2) * LANES]
        return [jnp.concatenate([even(dq) * SCALE, d_qr_raw], axis=1), even(dk),
                _rope_transpose(dkr, _rope_tables(pos_, c[0], LANES))], []

    return _rowwise(fn, [d_qc, d_kc, pos], [consts], [(2 * D_MODEL, BF16), (D_MODEL, BF16), (LANES, BF16)],
                    name="mla_qk_bwd")


def _causal_mask_t(st, key0, query0):
    key = lax.broadcasted_iota(jnp.int32, st.shape, 0) + key0
    query = lax.broadcasted_iota(jnp.int32, st.shape, 1) + query0
    return jnp.where(key <= query, st, NEG_BIG)


def _attn_tiles(s_dim):
    tq = min(512, s_dim)
    n_chains = 2 if s_dim >= 2 * tq else 1
    return tq, n_chains, min(512, s_dim)


def _diagonal_chains(t, tq, n_chains, tk):
    return [(c, (t + 1) * tk - 1 > c * tq) for c in range(n_chains) if t * tk < (c + 1) * tq]


def _attn_fwd(qc, kc, vt):
    s_dim = qc.shape[0]
    tq, n_chains, tk = _attn_tiles(s_dim)
    tqs = tq * n_chains

    def body(q_ref, k_ref, vt_ref, o_ref, lse_ref, m_s, l_s, acc):
        qi = pl.program_id(1)
        m_s[...] = jnp.full_like(m_s, NEG_BIG)
        l_s[...] = jnp.zeros_like(l_s)
        acc[...] = jnp.zeros_like(acc)

        def make_step(chains):
            def step(j, carry):
                ks = pl.multiple_of(j * tk, tk)
                kb, vtb = k_ref[pl.ds(ks, tk), :], vt_ref[:, pl.ds(ks, tk)]
                cols = [slice(c * tq, (c + 1) * tq) for c, _ in chains]
                sts = [_dot(kb, q_ref[cs, :], NT) for cs in cols]
                sts = [_causal_mask_t(st, j * tk, qi * tqs + c * tq) if masked else st
                       for st, (c, masked) in zip(sts, chains)]
                m_prevs = [m_s[:, cs] for cs in cols]
                m_news = [jnp.maximum(mp, jnp.max(st, axis=0, keepdims=True)) for mp, st in zip(m_prevs, sts)]
                alphas = [jnp.exp(mp - mn) for mp, mn in zip(m_prevs, m_news)]
                pts = [jnp.exp(st - mn) for st, mn in zip(sts, m_news)]
                pvs = [_dot(vtb, pt) for pt in pts]
                for cs, mn, al, pt, pv in zip(cols, m_news, alphas, pts, pvs):
                    l_s[:, cs] = al * l_s[:, cs] + _colsum(pt)
                    m_s[:, cs] = mn
                    acc[:, cs] = acc[:, cs] * al + pv
                return carry
            return step

        below = qi * (tqs // tk)
        lax.fori_loop(0, below, make_step([(c, False) for c in range(n_chains)]), 0)
        for t in range(tqs // tk):
            make_step(_diagonal_chains(t, tq, n_chains, tk))(below + t, 0)
        l = l_s[...]
        o_ref[...] = jnp.transpose(acc[...] / l)
        lse_ref[...] = m_s[...] + jnp.log(l)

    return pl.pallas_call(
        body,
        out_shape=[jax.ShapeDtypeStruct((s_dim, N_HEADS * HEAD), F32), jax.ShapeDtypeStruct((N_HEADS, 1, s_dim), F32)],
        grid=(N_HEADS, s_dim // tqs),
        in_specs=[pl.BlockSpec((tqs, QK_HEAD), lambda h, qi: (qi, h)),
                  pl.BlockSpec((s_dim, QK_HEAD), lambda h, qi: (0, h)),
                  pl.BlockSpec((HEAD, s_dim), lambda h, qi: (h, 0))],
        out_specs=[pl.BlockSpec((tqs, HEAD), lambda h, qi: (qi, h)),
                   pl.BlockSpec((None, 1, tqs), lambda h, qi: (h, 0, qi))],
        scratch_shapes=[pltpu.VMEM((1, tqs), F32), pltpu.VMEM((1, tqs), F32), pltpu.VMEM((HEAD, tqs), F32)],
        compiler_params=pltpu.CompilerParams(dimension_semantics=("parallel", "parallel")),
        name="attn_fwd",
    )(qc, kc, vt)


def _attn_bwd(qc, kc, kct, v, o, d_o, lse):
    s_dim = qc.shape[0]
    tq, n_chains, tk = _attn_tiles(s_dim)
    tqs = tq * n_chains

    def body(q_ref, k_ref, kt_ref, v_ref, o_ref, do_ref, lse_ref, dq_ref, dk_ref, dv_ref, dqt_acc, dv_acc):
        qi = pl.program_id(1)

        @pl.when(qi == 0)
        def _():
            dk_ref[...] = jnp.zeros_like(dk_ref)
            dv_acc[...] = jnp.zeros_like(dv_acc)

        dqt_acc[...] = jnp.zeros_like(dqt_acc)
        do_f = do_ref[...]
        do_all = do_f.astype(BF16)
        q_all = q_ref[...]
        lse_row = lse_ref[...]
        delta_row = _dot3(jnp.ones((8, HEAD), F32), o_ref[...] * do_f, NT)[0:1, :]

        def make_step(chains):
            rows = slice(chains[0][0] * tq, (chains[-1][0] + 1) * tq)

            def step(j, carry):
                ks = pl.multiple_of(j * tk, tk)
                kb, vb, ktb = k_ref[pl.ds(ks, tk), :], v_ref[pl.ds(ks, tk), :], kt_ref[:, pl.ds(ks, tk)]
                cols = [slice(c * tq, (c + 1) * tq) for c, _ in chains]
                sts = [_dot(kb, q_all[cs, :], NT) for cs in cols]
                sts = [_causal_mask_t(st, j * tk, qi * tqs + c * tq) if masked else st
                       for st, (c, masked) in zip(sts, chains)]
                dpts = [_dot(vb, do_all[cs, :], NT) for cs in cols]
                pts = [jnp.exp(st - lse_row[:, cs]) for st, cs in zip(sts, cols)]
                dsts = [(pt * (dpt - delta_row[:, cs])).astype(BF16) for pt, dpt, cs in zip(pts, dpts, cols)]
                pts = [pt.astype(BF16) for pt in pts]
                dqs = [_dot(ktb, dst) for dst in dsts]
                for cs, dq in zip(cols, dqs):
                    dqt_acc[:, cs] += dq
                pt_all = jnp.concatenate(pts, axis=1) if len(chains) > 1 else pts[0]
                dst_all = jnp.concatenate(dsts, axis=1) if len(chains) > 1 else dsts[0]
                dk_ref[pl.ds(ks, tk), :] += _dot(dst_all, q_all[rows, :])
                dv_acc[pl.ds(ks, tk), :] += _dot(pt_all, do_all[rows, :])
                return carry
            return step

        below = qi * (tqs // tk)
        lax.fori_loop(0, below, make_step([(c, False) for c in range(n_chains)]), 0)
        for t in range(tqs // tk):
            make_step(_diagonal_chains(t, tq, n_chains, tk))(below + t, 0)
        dq_ref[...] = jnp.transpose(dqt_acc[...])

        @pl.when(qi == s_dim // tqs - 1)
        def _():
            dv_ref[...] = dv_acc[...].astype(dv_ref.dtype)

    q_spec = pl.BlockSpec((tqs, QK_HEAD), lambda h, qi: (qi, h))
    o_spec = pl.BlockSpec((tqs, HEAD), lambda h, qi: (qi, h))
    k_spec = pl.BlockSpec((s_dim, QK_HEAD), lambda h, qi: (0, h))
    v_spec = pl.BlockSpec((s_dim, HEAD), lambda h, qi: (0, h))
    wide2 = jax.ShapeDtypeStruct((s_dim, N_HEADS * QK_HEAD), F32)
    return pl.pallas_call(
        body,
        out_shape=[wide2, wide2, jax.ShapeDtypeStruct((s_dim, N_HEADS * HEAD), BF16)],
        grid=(N_HEADS, s_dim // tqs),
        in_specs=[q_spec, k_spec, pl.BlockSpec((QK_HEAD, s_dim), lambda h, qi: (h, 0)), v_spec, o_spec, o_spec,
                  pl.BlockSpec((None, 1, tqs), lambda h, qi: (h, 0, qi))],
        out_specs=[q_spec, k_spec, v_spec],
        scratch_shapes=[pltpu.VMEM((QK_HEAD, tqs), F32), pltpu.VMEM((s_dim, HEAD), F32)],
        compiler_params=pltpu.CompilerParams(dimension_semantics=("parallel", "arbitrary")),
        name="attn_bwd",
    )(qc, kc, kct, v, o, d_o, lse)


def _mix_proj_ln1(y_dn, y_mla, proj_g, x, w_o, g, b):
    s_dim = x.shape[0]
    tm = min(512, s_dim)

    def body(yd_ref, ym_ref, g_ref, x_ref, w_ref, lg_ref, lb_ref, mixed_ref, a1_ref, h1_ref, h1b_ref):
        gates = g_ref[...].astype(F32)
        mixed = (_sig(gates[:, :D_MODEL]) * yd_ref[...].astype(F32)
                 + _sig(gates[:, D_MODEL:]) * ym_ref[...].astype(F32)).astype(BF16)
        a1 = _dot(mixed, w_ref[...])
        xh, _ = _ln_stats(ALPHA * x_ref[...] + a1)
        y = xh * lg_ref[...] + lb_ref[...]
        mixed_ref[...] = mixed
        a1_ref[...] = a1
        h1_ref[...] = y
        h1b_ref[...] = y.astype(BF16)

    row = lambda width: pl.BlockSpec((tm, width), lambda i: (i, 0))
    whole = lambda a: pl.BlockSpec(a.shape, lambda i: (0, 0))
    sds = lambda dt: jax.ShapeDtypeStruct((s_dim, D_MODEL), dt)
    return pl.pallas_call(
        body,
        out_shape=[sds(BF16), sds(F32), sds(F32), sds(BF16)],
        grid=(s_dim // tm,),
        in_specs=[row(D_MODEL), row(D_MODEL), row(2 * D_MODEL), row(D_MODEL), whole(w_o), whole(g), whole(b)],
        out_specs=[row(D_MODEL)] * 4,
        compiler_params=pltpu.CompilerParams(dimension_semantics=("parallel",)),
        name="mix_proj_ln1",
    )(y_dn, y_mla, proj_g, x, w_o, g, b)


def _merge_bwd(y_dn, y_mla, proj_g, d_mixed):
    def fn(r, c):
        yd, ym, g, dm = r
        sd, sm = _sig(g[:, :D_MODEL]), _sig(g[:, D_MODEL:])
        d_g = jnp.concatenate([dm * yd * sd * (1.0 - sd), dm * ym * sm * (1.0 - sm)], axis=1)
        return [d_g, dm * sd, dm * sm], []

    return _rowwise(fn, [y_dn, y_mla, proj_g, d_mixed], [], [(2 * D_MODEL, BF16), (D_MODEL, BF16), (D_MODEL, BF16)],
                    name="merge_bwd")


def _ln_stats(z):
    mu = _rowmean(z)
    zc = z - mu
    r = lax.rsqrt(_rowmean(zc * zc) + EPS_LN)
    return zc * r, r


def _ln_bwd(dy, xh, r, g):
    dxh = dy * g
    return r * (dxh - _rowmean(dxh) - xh * _rowmean(dxh * xh))


def _ln1_bwd(x, a1, d_h1, g):
    def fn(r, c):
        xh, rr = _ln_stats(ALPHA * r[0] + r[1])
        dy = r[2]
        dz = _ln_bwd(dy, xh, rr, c[0])
        return [dz, ALPHA * dz], [_colsum(dy * xh), _colsum(dy)]

    return _rowwise(fn, [x, a1, d_h1], [g], [(D_MODEL, BF16), (D_MODEL, F32)], accs=[(1, D_MODEL), (1, D_MODEL)],
                    name="ln1_bwd")


def _ffn_in_act(h1b, w_t):
    s_dim, k_dim = h1b.shape
    hidden = w_t.shape[0] // 2
    tm, tn = min(512, s_dim), _pick_wide(hidden)
    nt = hidden // tn

    def body(a_ref, bg_ref, bu_ref, gt_ref, up_ref, act_ref):
        a = a_ref[...]
        gt, up = _dot(a, bg_ref[...], NT), _dot(a, bu_ref[...], NT)
        gt_ref[...] = gt.astype(BF16)
        up_ref[...] = up.astype(BF16)
        act_ref[...] = (gt * _sig(gt) * up).astype(BF16)

    o_spec = pl.BlockSpec((tm, tn), lambda j, i: (i, j))
    sds = jax.ShapeDtypeStruct((s_dim, hidden), BF16)
    return pl.pallas_call(
        body,
        out_shape=[sds, sds, sds],
        grid=(nt, s_dim // tm),
        in_specs=[pl.BlockSpec((tm, k_dim), lambda j, i: (i, 0)), pl.BlockSpec((tn, k_dim), lambda j, i: (j, 0)),
                  pl.BlockSpec((tn, k_dim), lambda j, i: (j + nt, 0))],
        out_specs=[o_spec, o_spec, o_spec],
        compiler_params=pltpu.CompilerParams(dimension_semantics=("parallel", "parallel")),
        name="ffn_in_act",
    )(h1b, w_t, w_t)


def _act_bwd(gt, up, d_act):
    def fn(r, c):
        gt_, up_, da = r
        return [jnp.concatenate([da * up_ * _silu_grad(gt_), da * gt_ * _sig(gt_)], axis=1)], []

    return _rowwise(fn, [gt, up, d_act], [], [(2 * FFN_HIDDEN, BF16)], name="act_bwd")[0]


def _tail(h1, ffn, pg, pp, tgt, g, b):
    def fn(r, c):
        h1_, ffn_, pg_, pp_, t_ = r
        sp = _sig(pg_)
        xh, rr = _ln_stats(ALPHA * h1_ + ffn_ + sp * pp_)
        y = xh * c[0] + c[1]
        err = y - t_
        dy = err * (1.0 / D_MODEL)
        dz = _ln_bwd(dy, xh, rr, c[0])
        loss = jnp.sum(0.5 * _rowmean(err * err), axis=0, keepdims=True)
        return ([dz, dz * pp_ * sp * (1.0 - sp), dz * sp, ALPHA * dz],
                [_colsum(dy * xh), _colsum(dy), jnp.broadcast_to(loss, (1, LANES))])

    return _rowwise(fn, [h1, ffn, pg, pp, tgt], [g, b], [(D_MODEL, BF16)] * 3 + [(D_MODEL, F32)],
                    accs=[(1, D_MODEL), (1, D_MODEL), (1, LANES)], name="tail")


def _local_step(x, p, pos, tgt, w, late_weights, emit):
    w = dict(w)
    s_dim = x.shape[0]
    xb, pb = x.astype(BF16), p.astype(BF16)
    proj_a = _mm(xb, w["wa_t"], tb=True, name="f_proj_a")
    proj_g = _mm(xb, w["wg_t"], tb=True, out_dtype=BF16, name="f_proj_g")
    proj_b = _mm(xb, w["wb_t"], tb=True, name="f_proj_b")
    qkvn = _conv_fwd(proj_a, w["conv"])
    beta, gc = _gates_fwd(proj_b, w["alog"], w["dtb"])
    gc_t = jnp.transpose(gc[:, :N_HEADS])
    u, w_, qd, kt, a_mat, t_fold = _gdr_prep_fwd(qkvn, beta, gc, gc_t)
    o_dn, states = _gdr_scan_fwd(u, w_, qd, kt, a_mat, gc)
    og = _gdr_out_fwd(o_dn, proj_a, w["dnw"])
    w.update(late_weights("mix", og))
    y_dn = _mm(og, w["br_dn"], out_dtype=BF16, name="f_y_dn")
    c_q, c_kv = _mla_norm_fwd(proj_b, w["qnw"], w["kvnw"])
    q_full = _mm(c_q, w["uq"], out_dtype=BF16, name="f_q_full")
    k_nope = _mm(c_kv, w["uk"], out_dtype=BF16, name="f_k_nope")
    vv = _mm(c_kv, w["uv"], out_dtype=BF16, name="f_v")
    qc, kc = _mla_qk_fwd(q_full, k_nope, proj_b, pos)
    o_mla, lse = _attn_fwd(qc, kc, jnp.transpose(vv))
    y_mla = _mm(o_mla, w["br_mla"], out_dtype=BF16, name="f_y_mla")
    mixed, a1, h1, h1b = _mix_proj_ln1(y_dn, y_mla, proj_g, x, w["wo"], w["ln1g"], w["ln1b"])
    w.update(late_weights("ffn", a1))
    gt, up, act = _ffn_in_act(h1b, w["ffn_in_t"])
    ffn = _mm(act, w["ffn_out"], name="f_ffn")
    pg = _mm(h1b, w["ple_gate"], name="f_pg")
    pp = _mm(pb, w["ple_t"], tb=True, name="f_pp")
    g = {}
    dz2, d_pg, d_pp, dh1a, g["ln2g"], g["ln2b"], loss = _tail(h1, ffn, pg, pp, tgt, w["ln2g"], w["ln2b"])
    g["ple_t"] = _mm(d_pp, pb, ta=True, out_dtype=BF16, name="b_w_ple")
    g["ple_gate"] = _mm(h1b, d_pg, ta=True, out_dtype=BF16, name="b_w_ple_gate")
    g["ffn_out"] = _mm(act, dz2, ta=True, out_dtype=BF16, name="b_w_ffn_out")
    d_act = _mm(dz2, w["ffn_out"], tb=True, out_dtype=BF16, name="b_act")
    d_gu = _act_bwd(gt, up, d_act)
    g["ffn_in_t"] = _mm(d_gu, h1b, ta=True, out_dtype=BF16, name="b_w_ffn_in")
    d_gu = emit("ffn", g, d_gu)
    d_h1 = _mm(d_gu, w["ffn_in_t"], add=(dh1a,), name="b_h1_ffn")
    d_h1 = _mm(d_pg, w["ple_gate"], tb=True, add=(d_h1,), name="b_h1_ple")
    dz1, dxa, g["ln1g"], g["ln1b"] = _ln1_bwd(x, a1, d_h1, w["ln1g"])
    g["wo"] = _mm(mixed, dz1, ta=True, out_dtype=BF16, name="b_w_o")
    d_mixed = _mm(dz1, w["wo"], tb=True, out_dtype=BF16, name="b_mixed")
    d_proj_g, d_y_dn, d_y_mla = _merge_bwd(y_dn, y_mla, proj_g, d_mixed)
    g["br_mla"] = _mm(o_mla, d_y_mla, ta=True, out_dtype=BF16, name="b_w_br_mla")
    d_o_mla = _mm(d_y_mla, w["br_mla"], tb=True, out_dtype=BF16, name="b_o_mla")
    d_qc, d_kc, d_v = _attn_bwd(qc, kc, jnp.transpose(kc), vv, o_mla, d_o_mla, lse)
    d_q_full, d_kn, d_kr = _mla_qk_bwd(d_qc, d_kc, pos)
    g["uq"] = _mm(c_q, d_q_full, ta=True, out_dtype=BF16, name="b_w_uq")
    d_c_q = _mm(d_q_full, w["uq"], tb=True, out_dtype=BF16, name="b_c_q")
    g["uk"] = _mm(c_kv, d_kn, ta=True, out_dtype=BF16, name="b_w_uk")
    g["uv"] = _mm(c_kv, d_v, ta=True, out_dtype=BF16, name="b_w_uv")
    d_c_kv = _mm(d_kn, w["uk"], tb=True, name="b_c_kv_k")
    d_c_kv = _mm(d_v, w["uv"], tb=True, add=(d_c_kv,), out_dtype=BF16, name="b_c_kv_v")
    d_cq, d_ckv, g["qnw"], g["kvnw"] = _mla_norm_bwd(proj_b, w["qnw"], w["kvnw"], d_c_q, d_c_kv)
    g["br_dn"] = _mm(og, d_y_dn, ta=True, out_dtype=BF16, name="b_w_br_dn")
    d_og = emit("mix", g, _mm(d_y_dn, w["br_dn"], tb=True, out_dtype=BF16, name="b_og"))
    d_o_dn, d_z, g["dnw"] = _gdr_out_bwd(o_dn, proj_a, d_og, w["dnw"])
    du, dw, dqd, dkt, d_a, d_egl = _gdr_scan_bwd(u, w_, qd, kt, a_mat, gc, states, d_o_dn)
    dq, dk, dv, d_beta, d_gc = _gdr_prep_bwd(qkvn, beta, gc, gc_t, t_fold, u, w_, du, dw, dqd, dkt, d_a)
    d_egl_rows = jnp.pad(d_egl[:, None, :, 0], ((0, 0), (CHUNK - 1, 0), (0, LANES - N_HEADS))).reshape(s_dim, LANES)
    d_ba, g["alog"], g["dtb"] = _gates_bwd(proj_b, w["alog"], w["dtb"], gc, d_beta, d_gc, d_egl_rows)
    d_qkv, g["conv"] = _conv_bwd(proj_a, w["conv"], dq, dk, dv)
    zeros = jnp.zeros((s_dim, WB_CKV - Q_LORA), BF16)
    d_proj_b = jnp.concatenate([d_cq, zeros, d_ckv, d_kr, d_ba], axis=1)
    g["wa_qkv_t"] = _mm(d_qkv, xb, ta=True, name="b_w_qkv")
    g["wa_z_t"] = _mm(d_z, xb, ta=True, name="b_w_z")
    g["wg_t"] = _mm(d_proj_g, xb, ta=True, name="b_w_g")
    g["wb_t"] = _mm(d_proj_b, xb, ta=True, name="b_w_b")
    dx = _mm(d_qkv, w["wa_qkv_t"], add=(dxa,), name="b_x_qkv")
    dx = _mm(d_z, w["wa_z_t"], add=(dx,), name="b_x_z")
    dx = _mm(d_proj_g, w["wg_t"], add=(dx,), name="b_x_g")
    dx = _mm(d_proj_b, w["wb_t"], add=(dx,), name="b_x_b")
    return loss, dx, g


_BIG = (("w_in", 1), ("w_uq", 0), ("w_uk", 0), ("w_uv", 0), ("w_br_dn", 0), ("w_br_mla", 0),
        ("w_o", 0), ("w_ffn_in", 1), ("w_ffn_out", 0), ("w_ple", 1), ("w_ple_gate", 0))
_BIG_AXIS = dict(_BIG)
_SMALL = ("ln1_g", "ln1_b", "ln2_g", "ln2_b", "q_norm_w", "kv_norm_w", "dn_norm_w", "dn_a_log", "dn_dt_bias")
_ORDER = ("w_in", "conv_w", "dn_a_log", "dn_dt_bias", "dn_norm_w", "q_norm_w", "w_uq", "kv_norm_w", "w_uk", "w_uv",
          "w_br_dn", "w_br_mla", "w_o", "ln1_g", "ln1_b", "w_ffn_in", "w_ffn_out", "w_ple", "w_ple_gate", "ln2_g",
          "ln2_b")


def _stored_shape(name, shard_shape):
    axis = _BIG_AXIS[name]
    lead = shard_shape[axis]
    return lead, int(np.prod(shard_shape)) // lead


def _to_stored(name, shard):
    return jnp.moveaxis(shard, _BIG_AXIS[name], 0).reshape(_stored_shape(name, shard.shape))


def _from_stored(name, stored, shard_shape):
    axis = _BIG_AXIS[name]
    moved = (shard_shape[axis],) + shard_shape[:axis] + shard_shape[axis + 1:]
    return jnp.moveaxis(stored.reshape(moved), 0, axis)


_W_IN_ROWS = np.cumsum([0, 3072, 1024, 8, 8, Q_LORA, KV_LORA, ROPE, D_MODEL, D_MODEL])


def _first_weights(w_in_t, conv_full, small):
    r = _W_IN_ROWS
    zr = lambda n: jnp.zeros((n, D_MODEL), w_in_t.dtype)
    w = {}
    w["wa_t"] = w_in_t[r[0]:r[2]]
    w["wa_qkv_t"], w["wa_z_t"] = w_in_t[r[0]:r[1]], w_in_t[r[1]:r[2]]
    w["wg_t"] = w_in_t[r[7]:r[9]]
    w["wb_t"] = jnp.concatenate([w_in_t[r[4]:r[5]], zr(WB_CKV - Q_LORA), w_in_t[r[5]:r[7]], zr(LANES - ROPE),
                                 w_in_t[r[2]:r[4]], zr(LANES - 2 * N_HEADS)], axis=0)
    w["conv"] = conv_full
    pad_l = lambda v: jnp.pad(v, ((0, 0), (0, LANES - v.shape[1])))
    w["alog"], w["dtb"] = pad_l(small["dn_a_log"]), pad_l(small["dn_dt_bias"])
    w["dnw"], w["qnw"], w["kvnw"] = small["dn_norm_w"], small["q_norm_w"], small["kv_norm_w"]
    w["ln1g"], w["ln1b"], w["ln2g"], w["ln2b"] = small["ln1_g"], small["ln1_b"], small["ln2_g"], small["ln2_b"]
    return w


def _late_weights(group, fw):
    w = {}
    if group == "mix":
        uq = fw["w_uq"].reshape(Q_LORA, N_HEADS, HEAD + ROPE)
        uq_r = jnp.pad(uq[:, :, HEAD:], ((0, 0), (0, 0), (0, HEAD - ROPE)))
        w["uq"] = jnp.concatenate([uq[:, :, :HEAD].reshape(Q_LORA, -1), uq_r.reshape(Q_LORA, -1)], axis=1)
        w["uk"], w["uv"] = fw["w_uk"], fw["w_uv"]
        w["br_dn"], w["br_mla"], w["wo"] = fw["w_br_dn"], fw["w_br_mla"], fw["w_o"]
    else:
        w["ffn_in_t"], w["ffn_out"] = fw["w_ffn_in"], fw["w_ffn_out"]
        w["ple_t"], w["ple_gate"] = fw["w_ple"], fw["w_ple_gate"]
    return w


_GROUP_GRADS = {"ffn": (("w_ple", "ple_t"), ("w_ple_gate", "ple_gate"), ("w_ffn_out", "ffn_out"),
                        ("w_ffn_in", "ffn_in_t")),
                "mix": (("w_o", "wo"), ("w_br_mla", "br_mla"), ("w_uq", "uq"), ("w_uk", "uk"), ("w_uv", "uv"),
                        ("w_br_dn", "br_dn"))}


def _group_grads(group, g):
    out = {}
    for name, key in _GROUP_GRADS[group]:
        t = g[key]
        if name == "w_uq":
            uq_n = t[:, :D_MODEL].reshape(Q_LORA, N_HEADS, HEAD)
            uq_r = t[:, D_MODEL:].reshape(Q_LORA, N_HEADS, HEAD)[:, :, :ROPE]
            t = jnp.concatenate([uq_n, uq_r], axis=2).reshape(Q_LORA, -1)
        out[name] = t
    return out


def _last_grads(g):
    wb = g["wb_t"]
    w_in = jnp.concatenate([
        g["wa_qkv_t"], g["wa_z_t"], wb[WB_BA:WB_BA + 2 * N_HEADS], wb[WB_CQ:WB_CQ + Q_LORA],
        wb[WB_CKV:WB_CKV + KV_LORA], wb[WB_KR:WB_KR + ROPE], g["wg_t"]], axis=0)
    small = {"ln1_g": g["ln1g"], "ln1_b": g["ln1b"], "ln2_g": g["ln2g"], "ln2_b": g["ln2b"], "q_norm_w": g["qnw"],
             "kv_norm_w": g["kvnw"], "dn_norm_w": g["dnw"], "dn_a_log": g["alog"], "dn_dt_bias": g["dtb"],
             "conv_w": g["conv"]}
    return w_in, small


_SMALL_SLOTS = {"ln1_g": (0, 0, 1024), "ln1_b": (1, 0, 1024), "ln2_g": (2, 0, 1024), "ln2_b": (3, 0, 1024),
                "q_norm_w": (4, 0, 384), "kv_norm_w": (4, 384, 256), "dn_norm_w": (4, 640, 128),
                "dn_a_log": (4, 768, 8), "dn_dt_bias": (4, 896, 8)}
_SMALL_ROWS, _LOSS_ROW, _CONV_ROW0, _CONV_ROWS = 24, 5, 8, 12


def _pack_small_grads(small_g, loss):
    zeros = lambda r, c: jnp.zeros((r, c), F32)
    row4 = jnp.concatenate([small_g["q_norm_w"], small_g["kv_norm_w"], small_g["dn_norm_w"], small_g["dn_a_log"],
                            small_g["dn_dt_bias"]], axis=1)
    row5 = jnp.concatenate([loss, zeros(1, FLAT_COLS - LANES)], axis=1)
    head = jnp.concatenate([small_g["ln1_g"], small_g["ln1_b"], small_g["ln2_g"], small_g["ln2_b"], row4, row5,
                            zeros(2, FLAT_COLS)], axis=0)
    conv = small_g["conv_w"].reshape(_CONV_ROWS, FLAT_COLS)
    return jnp.concatenate([head, conv, zeros(_SMALL_ROWS - _CONV_ROW0 - _CONV_ROWS, FLAT_COLS)], axis=0)


_MESH_ID = pl.DeviceIdType.MESH
_ANY = pl.BlockSpec(memory_space=pl.ANY)


def _all_gather(blocks, name):
    n = len(blocks)

    def body(*refs):
        x_refs, out_refs = refs[:n], refs[n:2 * n]
        send_sems, recv_sems, local_sems = refs[2 * n:]
        x, y, c = lax.axis_index("x"), lax.axis_index("y"), lax.axis_index("c")
        me, sibling = (x, y, c), (x, y, 1 - c)
        chips = [(1 - x, y), (x, 1 - y), (1 - x, 1 - y)]

        def slot(i, px, py, pc):
            return out_refs[i].at[4 * px + 2 * py + pc]

        def copy(i, k, origin, to, src=None):
            return pltpu.make_async_remote_copy(
                src_ref=slot(i, *origin) if src is None else src, dst_ref=slot(i, *origin),
                send_sem=send_sems.at[7 * i + k], recv_sem=recv_sems.at[7 * i + k], device_id=to,
                device_id_type=_MESH_ID)

        mine = [pltpu.make_async_copy(x_refs[i], slot(i, *me), local_sems.at[i]) for i in range(n)]
        first, passed = [], []
        for i in range(n):
            mine[i].start()
            first.append(copy(i, 0, me, sibling, src=x_refs[i]))
            first += [copy(i, 1 + j, me, (*chip, c), src=x_refs[i]) for j, chip in enumerate(chips)]
        for cp in first:
            cp.start()
        for i in range(n):
            for j, chip in enumerate(chips):
                copy(i, 1 + j, (*chip, c), me).wait_recv()
                passed.append(copy(i, 4 + j, (*chip, c), sibling))
                passed[-1].start()
        for i in range(n):
            copy(i, 0, sibling, me).wait_recv()
            for j, chip in enumerate(chips):
                copy(i, 4 + j, (*chip, 1 - c), me).wait_recv()
        for cp in first + passed:
            cp.wait_send()
        for cp in mine:
            cp.wait()

    return pl.pallas_call(
        body,
        out_shape=[jax.ShapeDtypeStruct((N_DEV,) + b.shape, b.dtype) for b in blocks],
        in_specs=[_ANY] * n,
        out_specs=[_ANY] * n,
        scratch_shapes=[pltpu.SemaphoreType.DMA((7 * n,)), pltpu.SemaphoreType.DMA((7 * n,)),
                        pltpu.SemaphoreType.DMA((n,))],
        name=name,
    )(*blocks)


def _exchange_sibling(srcs, name):
    n = len(srcs)

    def body(*refs):
        src_refs, dst_refs = refs[:n], refs[n:2 * n]
        send_sems, recv_sems = refs[2 * n:]
        x, y, c = lax.axis_index("x"), lax.axis_index("y"), lax.axis_index("c")
        copies = [pltpu.make_async_remote_copy(
            src_ref=src_refs[i].at[2 * q + (1 - c)], dst_ref=dst_refs[i].at[q], send_sem=send_sems.at[4 * i + q],
            recv_sem=recv_sems.at[4 * i + q], device_id=(x, y, 1 - c), device_id_type=_MESH_ID)
            for i in range(n) for q in range(4)]
        for cp in copies:
            cp.start()
        for cp in copies:
            cp.wait_recv()
        for cp in copies:
            cp.wait_send()

    return pl.pallas_call(
        body,
        out_shape=[jax.ShapeDtypeStruct((4,) + s.shape[1:], s.dtype) for s in srcs],
        in_specs=[_ANY] * n,
        out_specs=[_ANY] * n,
        scratch_shapes=[pltpu.SemaphoreType.DMA((4 * n,)), pltpu.SemaphoreType.DMA((4 * n,))],
        name=name,
    )(*srcs)


def _exchange_chips(srcs, name):
    n = len(srcs)

    def body(*refs):
        src_refs, dst_refs = refs[:n], refs[n:2 * n]
        send_sems, recv_sems = refs[2 * n:]
        x, y, c = lax.axis_index("x"), lax.axis_index("y"), lax.axis_index("c")
        chips = [(1 - x, y), (x, 1 - y), (1 - x, 1 - y)]
        copies = [pltpu.make_async_remote_copy(
            src_ref=src_refs[i].at[2 * tx + ty], dst_ref=dst_refs[i].at[j], send_sem=send_sems.at[3 * i + j],
            recv_sem=recv_sems.at[3 * i + j], device_id=(tx, ty, c), device_id_type=_MESH_ID)
            for i in range(n) for j, (tx, ty) in enumerate(chips)]
        for cp in copies:
            cp.start()
        for cp in copies:
            cp.wait_recv()
        for cp in copies:
            cp.wait_send()

    return pl.pallas_call(
        body,
        out_shape=[jax.ShapeDtypeStruct((3,) + s.shape[1:], s.dtype) for s in srcs],
        in_specs=[_ANY] * n,
        out_specs=[_ANY] * n,
        scratch_shapes=[pltpu.SemaphoreType.DMA((3 * n,)), pltpu.SemaphoreType.DMA((3 * n,))],
        name=name,
    )(*srcs)


def _col_tile(c):
    return c if c <= 256 else 256


def _chip_sum(src, recv, parity, name):
    _, r, c = src.shape
    tc = _col_tile(c)

    def body(par_ref, a_ref, b_ref, o_ref, ob_ref):
        s = a_ref[...] + b_ref[...]
        o_ref[...] = s
        ob_ref[...] = s.astype(BF16)

    blk = lambda f: pl.BlockSpec((None, r, tc), f)
    return pl.pallas_call(
        body,
        out_shape=[jax.ShapeDtypeStruct((4, r, c), F32), jax.ShapeDtypeStruct((4, r, c), BF16)],
        grid_spec=pltpu.PrefetchScalarGridSpec(
            num_scalar_prefetch=1, grid=(4, c // tc),
            in_specs=[blk(lambda q, j, par: (2 * q + par[0], 0, j)), blk(lambda q, j, par: (q, 0, j))],
            out_specs=[blk(lambda q, j, par: (q, 0, j)), blk(lambda q, j, par: (q, 0, j))]),
        compiler_params=pltpu.CompilerParams(dimension_semantics=("parallel", "parallel")),
        name=name,
    )(parity, src, recv)


def _sum_parts(own, others, chip, name):
    _, r, c = own.shape
    tc = _col_tile(c)

    def body(q_ref, a_ref, b_ref, o_ref):
        o_ref[...] = ((a_ref[...] + b_ref[0].astype(F32)) + b_ref[1].astype(F32)) + b_ref[2].astype(F32)

    return pl.pallas_call(
        body,
        out_shape=jax.ShapeDtypeStruct((r, c), F32),
        grid_spec=pltpu.PrefetchScalarGridSpec(
            num_scalar_prefetch=1, grid=(c // tc,),
            in_specs=[pl.BlockSpec((None, r, tc), lambda j, q: (q[0], 0, j)),
                      pl.BlockSpec((3, r, tc), lambda j, q: (0, 0, j))],
            out_specs=pl.BlockSpec((r, tc), lambda j, q: (0, j))),
        compiler_params=pltpu.CompilerParams(dimension_semantics=("parallel",)),
        name=name,
    )(chip, own, others)


_HBM = pl.BlockSpec(memory_space=pltpu.HBM)
_SEM = pl.BlockSpec(memory_space=pltpu.SEMAPHORE)
_DATAFLOW = pltpu.SideEffectType.DATAFLOW_SIDE_EFFECTING
N_PEERS = N_DEV - 1


def _ring_peer(j):
    me = 4 * lax.axis_index("x") + 2 * lax.axis_index("y") + lax.axis_index("c")
    k = (me + j) % N_DEV
    return me, k, (k // 4, (k // 2) % 2, k % 2)


def _spread_copy(i, j, src_refs, land_refs, send_sems, recv_sems, scatter):
    me, k, peer = _ring_peer(j)
    return pltpu.make_async_remote_copy(
        src_ref=src_refs[i].at[k] if scatter else src_refs[i], dst_ref=land_refs[i].at[me],
        send_sem=send_sems.at[N_PEERS * i + j - 1], recv_sem=recv_sems.at[N_PEERS * i + j - 1], device_id=peer,
        device_id_type=_MESH_ID)


def _spread_start(srcs, carry, scatter, name):
    n = len(srcs)
    lands = [lax.empty(((N_DEV,) + s.shape[-2:]), s.dtype) for s in srcs]

    def body(*refs):
        src_refs, land_refs = refs[:n], refs[n:2 * n]
        send_sems, recv_sems, local_sems = refs[2 * n + 1:2 * n + 4]
        for i in range(n):
            for j in range(1, N_DEV):
                _spread_copy(i, j, src_refs, land_refs, send_sems, recv_sems, scatter).start()
        for i in range(n):
            _own_copy(i, src_refs, land_refs, local_sems, scatter).start()

    hbm = lambda a: pltpu.HBM(a.shape, a.dtype)
    sems = pltpu.SemaphoreType.DMA((N_PEERS * n,))
    pinned = [pltpu.with_memory_space_constraint(a, pltpu.HBM) for a in list(srcs) + lands + [carry]]
    res = pl.pallas_call(
        body, name=name,
        out_shape=(sems, sems, pltpu.SemaphoreType.DMA((n,)), *[hbm(a) for a in pinned]),
        in_specs=[_HBM] * (2 * n + 1),
        out_specs=(_SEM, _SEM, _SEM, *[_HBM] * (2 * n + 1)),
        input_output_aliases={i: 3 + i for i in range(2 * n + 1)},
        compiler_params=pltpu.CompilerParams(has_side_effects=_DATAFLOW),
    )(*pinned)
    return res[:3], list(res[3:3 + n]), list(res[3 + n:3 + 2 * n]), res[3 + 2 * n]


def _own_copy(i, src_refs, land_refs, local_sems, scatter):
    me = _ring_peer(0)[0]
    return pltpu.make_async_copy(src_refs[i].at[me] if scatter else src_refs[i], land_refs[i].at[me],
                                 local_sems.at[i])


def _spread_wait(started, after, scatter, name):
    sems, srcs, lands, _ = started
    n = len(srcs)

    def body(*refs):
        src_refs, land_refs = refs[:n], refs[n:2 * n]
        send_s, recv_s, local_s = refs[2 * n:2 * n + 3]
        for i in range(n):
            for j in range(1, N_DEV):
                cp = _spread_copy(i, j, src_refs, land_refs, send_s, recv_s, scatter)
                cp.wait_send()
                cp.wait_recv()
        for i in range(n):
            _own_copy(i, src_refs, land_refs, local_s, scatter).wait()

    hbm = lambda a: pltpu.HBM(a.shape, a.dtype)
    res = pl.pallas_call(
        body, name=name,
        out_shape=tuple(hbm(a) for a in srcs + lands),
        in_specs=[_HBM] * (2 * n) + [_SEM, _SEM, _SEM, pl.BlockSpec(memory_space=pl.ANY)],
        out_specs=tuple([_HBM] * (2 * n)),
        input_output_aliases={i: i for i in range(2 * n)},
        compiler_params=pltpu.CompilerParams(has_side_effects=_DATAFLOW),
    )(*srcs, *lands, *sems, after)
    return list(res[n:])


def _sum8(landing, name):
    _, r, c = landing.shape
    tc = _col_tile(c)

    def body(a_ref, o_ref):
        tot = a_ref[0].astype(F32)
        for k in range(1, N_DEV):
            tot = tot + a_ref[k].astype(F32)
        o_ref[...] = tot

    return pl.pallas_call(
        body,
        out_shape=jax.ShapeDtypeStruct((r, c), F32),
        grid=(c // tc,),
        in_specs=[pl.BlockSpec((N_DEV, r, tc), lambda j: (0, 0, j))],
        out_specs=pl.BlockSpec((r, tc), lambda j: (0, j)),
        compiler_params=pltpu.CompilerParams(dimension_semantics=("parallel",)),
        name=name,
    )(landing)


def _adamw_math(w, g, m, v):
    m = ADAM_B1 * m + (1.0 - ADAM_B1) * g
    v = ADAM_B2 * v + (1.0 - ADAM_B2) * (g * g)
    m_hat = m / (1.0 - ADAM_B1 ** ADAM_STEP)
    v_hat = v / (1.0 - ADAM_B2 ** ADAM_STEP)
    delta = -ADAM_LR * (m_hat / (jnp.sqrt(v_hat) + ADAM_EPS) + ADAM_WD * w)
    return delta, m, v


def _adamw(w, m, v, g, name):
    r, c = w.shape

    def fn(rows, consts):
        return list(_adamw_math(*rows)), []

    return _rowwise(fn, [w, g, m, v], [], [(c, F32)] * 3, tm=r if r <= 512 else 256, name=name)


def _adamw_sum8(w, m, v, landing, name):
    r, c = w.shape
    tc = _col_tile(c)

    def body(w_ref, m_ref, v_ref, a_ref, g_ref, d_ref, m2_ref, v2_ref):
        g = a_ref[0].astype(F32)
        for k in range(1, N_DEV):
            g = g + a_ref[k].astype(F32)
        delta, m2, v2 = _adamw_math(w_ref[...], g, m_ref[...], v_ref[...])
        g_ref[...] = g
        d_ref[...] = delta
        m2_ref[...] = m2
        v2_ref[...] = v2

    blk = pl.BlockSpec((r, tc), lambda j: (0, j))
    return pl.pallas_call(
        body,
        out_shape=[jax.ShapeDtypeStruct((r, c), F32)] * 4,
        grid=(c // tc,),
        in_specs=[blk, blk, blk, pl.BlockSpec((N_DEV, r, tc), lambda j: (0, 0, j))],
        out_specs=[blk] * 4,
        compiler_params=pltpu.CompilerParams(dimension_semantics=("parallel",)),
        name=name,
    )(w, m, v, landing)


def _adamw_small(gathered, params):
    ns = len(_SMALL)

    def body(*refs):
        g_ref, p_refs, o_refs = refs[0], refs[1:1 + 3 * ns], refs[1 + 3 * ns:]
        tot = g_ref[0]
        for k in range(1, N_DEV):
            tot = tot + g_ref[k]
        for i, name in enumerate(_SMALL):
            row, lane0, lanes = _SMALL_SLOTS[name]
            g = tot[row:row + 1, lane0:lane0 + lanes]
            w_, m_, v_ = (p_refs[3 * i + j][...] for j in range(3))
            delta, m2, v2 = _adamw_math(w_, g, m_, v_)
            for j, val in enumerate((g, delta, m2, v2)):
                o_refs[4 * i + j][...] = val
        o_refs[4 * ns][...] = tot[_LOSS_ROW:_LOSS_ROW + 1, 0:LANES]
        o_refs[4 * ns + 1][...] = tot[_CONV_ROW0:_CONV_ROW0 + _CONV_ROWS, :]

    out_shape = [jax.ShapeDtypeStruct(w.shape, F32) for (w, _, _) in params for _ in range(4)]
    out_shape += [jax.ShapeDtypeStruct((1, LANES), F32), jax.ShapeDtypeStruct((_CONV_ROWS, FLAT_COLS), F32)]
    flat = [a for wmv in params for a in wmv]
    return pl.pallas_call(body, out_shape=out_shape, name="adamw_small")(gathered, *flat)


def kernel(x, p, positions, w_in, conv_w, dn_a_log, dn_dt_bias, dn_norm_w, q_norm_w, w_uq, kv_norm_w, w_uk, w_uv, w_br_dn, w_br_mla, w_o, ln1_g, ln1_b, w_ffn_in, w_ffn_out, w_ple, w_ple_gate, ln2_g, ln2_b, loss_target, m_w_in, m_conv_w, m_dn_a_log, m_dn_dt_bias, m_dn_norm_w, m_q_norm_w, m_w_uq, m_kv_norm_w, m_w_uk, m_w_uv, m_w_br_dn, m_w_br_mla, m_w_o, m_ln1_g, m_ln1_b, m_w_ffn_in, m_w_ffn_out, m_w_ple, m_w_ple_gate, m_ln2_g, m_ln2_b, v_w_in, v_conv_w, v_dn_a_log, v_dn_dt_bias, v_dn_norm_w, v_q_norm_w, v_w_uq, v_kv_norm_w, v_w_uk, v_w_uv, v_w_br_dn, v_w_br_mla, v_w_o, v_ln1_g, v_ln1_b, v_w_ffn_in, v_w_ffn_out, v_w_ple, v_w_ple_gate, v_ln2_g, v_ln2_b):
    args = dict(locals())
    wts = {n: args[n] for n in _ORDER}
    mom1 = {n: args["m_" + n] for n in _ORDER}
    mom2 = {n: args["v_" + n] for n in _ORDER}
    big_names = [n for n, _ in _BIG]
    shard_shapes = {n: wts[n].shape[1:] for n in big_names}
    c_idx = lax.axis_index("c")
    q_idx = 2 * lax.axis_index("x") + lax.axis_index("y")
    parity, chip = c_idx.reshape(1).astype(jnp.int32), q_idx.reshape(1).astype(jnp.int32)

    stored = {n: _to_stored(n, wts[n][0]).astype(BF16) for n in big_names}
    first = _all_gather([stored["w_in"], conv_w[0]], "ag_first")
    group_names = {grp: [n for n, _ in pairs] for grp, pairs in _GROUP_GRADS.items()}
    carry, gathers = first[0], {}
    for grp in ("mix", "ffn"):
        gathers[grp] = _spread_start([stored[n] for n in group_names[grp]], carry, False, "ag_start_" + grp)
        carry = gathers[grp][3]
    conv_full = jnp.moveaxis(first[1], 0, 1).reshape(conv_w.shape[1], -1)
    small_w = {n: wts[n].astype(F32) for n in _SMALL}
    w = _first_weights(carry.reshape(-1, D_MODEL), conv_full, small_w)

    def late_weights(grp, after):
        got = _spread_wait(gathers[grp], after, False, "ag_wait_" + grp)
        return _late_weights(grp, {n: t.reshape(-1, t.shape[-1]) for n, t in zip(group_names[grp], got)})

    started = {}

    def emit(group, g, carry):
        grads = _group_grads(group, g)
        srcs = [grads[n].reshape((N_DEV,) + _stored_shape(n, shard_shapes[n])) for n in grads]
        started[group] = (list(grads), _spread_start(srcs, carry, True, "rs_start_" + group))
        return started[group][1][3]

    s_dim = x.shape[1]
    loss, dx, g = _local_step(x[0], p[0, 0], positions.reshape(s_dim, 1).astype(F32), loss_target[0], w,
                              late_weights, emit)
    g_w_in, small_g = _last_grads(g)

    src = g_w_in.reshape((N_DEV,) + _stored_shape("w_in", shard_shapes["w_in"]))
    from_sibling = _exchange_sibling([src], "rs_sibling")[0]
    own, own_bf = _chip_sum(src, from_sibling, parity, "rs_sum_w_in")
    from_chips = _exchange_chips([own_bf], "rs_chips")[0]

    out_g, out_d, out_m, out_v = {}, {}, {}, {}

    def update(n, grad, shp):
        flat2 = (shp[0], int(np.prod(shp[1:])))
        d, m2, v2 = _adamw(wts[n][0].reshape(flat2), mom1[n][0].reshape(flat2), mom2[n][0].reshape(flat2),
                           grad.reshape(flat2), "adamw_" + n)
        out_g[n], out_d[n], out_m[n], out_v[n] = grad, d.reshape(shp), m2.reshape(shp), v2.reshape(shp)

    total = _sum_parts(own, from_chips, chip, "rs_total_w_in")
    update("w_in", _from_stored("w_in", total, shard_shapes["w_in"]), shard_shapes["w_in"])
    for group, (names, st) in started.items():
        for n, landing in zip(names, _spread_wait(st, dx, True, "rs_wait_" + group)):
            shp = shard_shapes[n]
            if _BIG_AXIS[n] == 0:
                flat2 = _stored_shape(n, shp)
                res = _adamw_sum8(wts[n][0].reshape(flat2), mom1[n][0].reshape(flat2), mom2[n][0].reshape(flat2),
                                  landing, "adamw_" + n)
                out_g[n], out_d[n], out_m[n], out_v[n] = (t.reshape(shp) for t in res)
            else:
                update(n, _from_stored(n, _sum8(landing, "rs_total_" + n), shp), shp)

    g_small = _all_gather([_pack_small_grads(small_g, loss)], "ag_small")[0]
    res = _adamw_small(g_small, [(wts[n], mom1[n], mom2[n]) for n in _SMALL])
    for i, n in enumerate(_SMALL):
        out_g[n], out_d[n], out_m[n], out_v[n] = res[4 * i:4 * i + 4]
    loss_out = res[4 * len(_SMALL)][0, 0]
    conv_shape = conv_w.shape[1:]
    conv_g = lax.dynamic_slice(res[-1].reshape(conv_shape[0], -1), (0, (2 * q_idx + c_idx) * conv_shape[1]),
                               conv_shape)
    update("conv_w", conv_g, conv_shape)

    expand = lambda d, n: d[n] if n in _SMALL else d[n][None]
    return (loss_out, dx[None], *[expand(out_g, n) for n in _ORDER], *[expand(out_d, n) for n in _ORDER],
            *[expand(out_m, n) for n in _ORDER], *[expand(out_v, n) for n in _ORDER])
```

```python
import functools

import numpy as np
import jax
import jax.numpy as jnp
from jax import lax
from jax.experimental import pallas as pl
from jax.experimental.pallas import tpu as pltpu

F32 = jnp.float32
BF16 = jnp.bfloat16

D_MODEL = 1024
N_HEADS = 8
HEAD = 128
CHUNK = 64
GROUP = 256
ROPE = 64
Q_LORA = 384
KV_LORA = 256
FFN_HIDDEN = 2816
PLE_DIM = 256
ROPE_BASE = 10000.0
ALPHA = 2.0 ** 0.25
SCALE = float((HEAD + ROPE) ** -0.5)
NEG_BIG = -1e30
EPS_RMS = 1e-6
EPS_LN = 1e-5

ADAM_LR = 0.001
ADAM_B1 = 0.9
ADAM_B2 = 0.999
ADAM_EPS = 1e-08
ADAM_WD = 0.01
ADAM_STEP = 10

N_DEV = 8
LANES = 128
FLAT_COLS = 1024

WB_CQ, WB_CKV, WB_KR, WB_BA, WB_COLS = 0, 512, 768, 896, 1024

HIGHEST = lax.Precision.HIGHEST

NN = (((1,), (0,)), ((), ()))
TN = (((0,), (0,)), ((), ()))
NT = (((1,), (1,)), ((), ()))


def _dot(a, b, dims=NN):
    return lax.dot_general(a.astype(BF16), b.astype(BF16), dims, preferred_element_type=F32)


def _dot32(a, b, dims=NN):
    return lax.dot_general(a, b, dims, precision=HIGHEST, preferred_element_type=F32)


def _sig(x):
    return 1.0 / (1.0 + jnp.exp(-x))


MM_TILE = 1536


def _pick_wide(n):
    if n <= MM_TILE:
        return n
    return max(t for t in range(LANES, MM_TILE + 1, LANES) if n % t == 0)


def _split_bf16(a):
    hi = a.astype(BF16)
    return hi, (a - hi.astype(F32)).astype(BF16)


def _dot3(a, b, dims=NN):
    ah, al = a if isinstance(a, tuple) else _split_bf16(a)
    bh, bl = b if isinstance(b, tuple) else _split_bf16(b)
    d = lambda p, q: lax.dot_general(p, q, dims, preferred_element_type=F32)
    return d(ah, bh) + (d(ah, bl) + d(al, bh))


def _mm(a, b, *, ta=False, tb=False, add=(), out_dtype=F32, name):
    if ta:
        k_dim, m_dim = a.shape
    else:
        m_dim, k_dim = a.shape
    if tb:
        n_dim, k2 = b.shape
    else:
        k2, n_dim = b.shape
    assert k_dim == k2, (a.shape, b.shape, ta, tb)
    tm = _pick_wide(m_dim)
    tn = _pick_wide(n_dim)
    tk = _pick_wide(k_dim)
    nk = k_dim // tk
    n_add = len(add)
    dims = TN if ta else (NT if tb else NN)
    assert not (ta and tb)

    def body(a_ref, b_ref, *rest):
        add_refs = rest[:n_add]
        o_ref = rest[n_add]
        acc = rest[n_add + 1]
        k = pl.program_id(2)

        @pl.when(k == 0)
        def _():
            acc[...] = jnp.zeros_like(acc)

        acc[...] += _dot(a_ref[...], b_ref[...], dims)

        @pl.when(k == nk - 1)
        def _():
            r = acc[...]
            for ar in add_refs:
                r = r + ar[...].astype(F32)
            o_ref[...] = r.astype(o_ref.dtype)

    a_spec = pl.BlockSpec((tk, tm), lambda i, j, k: (k, i)) if ta else pl.BlockSpec((tm, tk), lambda i, j, k: (i, k))
    b_spec = pl.BlockSpec((tn, tk), lambda i, j, k: (j, k)) if tb else pl.BlockSpec((tk, tn), lambda i, j, k: (k, j))
    o_spec = pl.BlockSpec((tm, tn), lambda i, j, k: (i, j))
    return pl.pallas_call(
        body,
        out_shape=jax.ShapeDtypeStruct((m_dim, n_dim), out_dtype),
        grid=(m_dim // tm, n_dim // tn, nk),
        in_specs=[a_spec, b_spec] + [o_spec] * n_add,
        out_specs=o_spec,
        scratch_shapes=[pltpu.VMEM((tm, tn), F32)],
        compiler_params=pltpu.CompilerParams(dimension_semantics=("parallel", "parallel", "arbitrary")),
        name=name,
    )(a, b, *add)


def _rowwise(fn, rows, consts, outs, accs=(), *, tm=256, name):
    rows = [r if isinstance(r, tuple) else (r, 0, r.shape[1]) for r in rows]
    s_dim = rows[0][0].shape[0]
    tm = min(tm, s_dim)
    assert s_dim % tm == 0 and all(arr.shape[0] == s_dim for arr, _, _ in rows)
    specs = [pl.BlockSpec((tm, width), functools.partial(lambda i, cb: (i, cb), cb=cb)) for _, cb, width in rows]
    args = [arr for arr, _, _ in rows]
    for c in consts:
        specs.append(pl.BlockSpec(c.shape, lambda i: (0, 0)))
        args.append(c)
    nr, nc, no = len(rows), len(consts), len(outs)
    out_shape = [jax.ShapeDtypeStruct((s_dim, w), dt) for (w, dt) in outs]
    out_specs = [pl.BlockSpec((tm, w), lambda i: (i, 0)) for (w, dt) in outs]
    out_shape += [jax.ShapeDtypeStruct(sh, F32) for sh in accs]
    out_specs += [pl.BlockSpec(sh, lambda i: (0, 0)) for sh in accs]

    def body(*refs):
        r = [x[...].astype(F32) if x.dtype == BF16 else x[...] for x in refs[:nr]]
        c = [x[...] for x in refs[nr:nr + nc]]
        o_refs = refs[nr + nc:nr + nc + no]
        a_refs = refs[nr + nc + no:]
        o_vals, a_vals = fn(r, c)
        for ref, v in zip(o_refs, o_vals, strict=True):
            ref[...] = v.astype(ref.dtype)
        if a_refs:
            @pl.when(pl.program_id(0) == 0)
            def _():
                for ref in a_refs:
                    ref[...] = jnp.zeros_like(ref)

            for ref, v in zip(a_refs, a_vals, strict=True):
                ref[...] += v

    res = pl.pallas_call(
        body,
        out_shape=out_shape,
        grid=(s_dim // tm,),
        in_specs=specs,
        out_specs=out_specs,
        compiler_params=pltpu.CompilerParams(dimension_semantics=("arbitrary" if accs else "parallel",)),
        name=name,
    )(*args)
    return res


def _colsum(v):
    return jnp.sum(v, axis=0, keepdims=True)


def _rowsum(v):
    return jnp.sum(v, axis=1, keepdims=True)


def _rowmean(v):
    return jnp.mean(v, axis=1, keepdims=True)


def _silu_grad(x):
    s = _sig(x)
    return s * (1.0 + x * (1.0 - s))


def _conv_taps(x, w, width=4):
    row = lax.broadcasted_iota(jnp.int32, x.shape, 0)
    c = x * w[width - 1:width, :]
    for s in range(1, width):
        c = c + jnp.where(row >= s, pltpu.roll(x, s, 0), 0.0) * w[width - 1 - s:width - s, :]
    return c


def _conv_fwd(proj_a, conv_w):
    s_dim = proj_a.shape[0]
    n_blk = 3 * N_HEADS

    def body(x_ref, w_ref, o_ref):
        j = pl.program_id(0)
        c = _conv_taps(x_ref[...], w_ref[...])
        y = c * _sig(c)
        r = lax.rsqrt(_rowsum(y * y) + EPS_RMS)
        fac = jnp.where(j < N_HEADS, r * (HEAD ** -0.5), jnp.where(j < 2 * N_HEADS, r, 1.0))
        o_ref[...] = y * fac

    return pl.pallas_call(
        body,
        out_shape=jax.ShapeDtypeStruct((s_dim, n_blk * HEAD), F32),
        grid=(n_blk,),
        in_specs=[pl.BlockSpec((s_dim, HEAD), lambda j: (0, j)), pl.BlockSpec((4, HEAD), lambda j: (0, j))],
        out_specs=pl.BlockSpec((s_dim, HEAD), lambda j: (0, j)),
        compiler_params=pltpu.CompilerParams(dimension_semantics=("parallel",)),
        name="conv_fwd",
    )(proj_a, conv_w)


def _conv_bwd(proj_a, conv_w, dq, dk, dv):
    s_dim = proj_a.shape[0]
    n_blk = 3 * N_HEADS

    def body(x_ref, w_ref, dq_ref, dk_ref, dv_ref, dx_ref, dw_ref):
        j = pl.program_id(0)
        x = x_ref[...]
        w = w_ref[...]
        do = jnp.where(j < N_HEADS, dq_ref[...], jnp.where(j < 2 * N_HEADS, dk_ref[...], dv_ref[...]))
        c = _conv_taps(x, w)
        sg = _sig(c)
        y = c * sg
        r = lax.rsqrt(_rowsum(y * y) + EPS_RMS)
        sc = jnp.where(j < N_HEADS, HEAD ** -0.5, 1.0)
        dy_n = sc * (r * do - y * (r * r * r) * _rowsum(do * y))
        dy = jnp.where(j < 2 * N_HEADS, dy_n, do)
        dc = dy * (sg * (1.0 + c * (1.0 - sg)))
        row = lax.broadcasted_iota(jnp.int32, x.shape, 0)
        dx = dc * w[3:4, :]
        dw_ref[3:4, :] = _colsum(dc * x)
        for s in range(1, 4):
            dx = dx + jnp.where(row < s_dim - s, pltpu.roll(dc, s_dim - s, 0), 0.0) * w[3 - s:4 - s, :]
            xs = jnp.where(row >= s, pltpu.roll(x, s, 0), 0.0)
            dw_ref[3 - s:4 - s, :] = _colsum(dc * xs)
        dx_ref[...] = dx.astype(dx_ref.dtype)

    hd = N_HEADS - 1
    return pl.pallas_call(
        body,
        out_shape=[jax.ShapeDtypeStruct((s_dim, n_blk * HEAD), BF16), jax.ShapeDtypeStruct((4, n_blk * HEAD), F32)],
        grid=(n_blk,),
        in_specs=[
            pl.BlockSpec((s_dim, HEAD), lambda j: (0, j)),
            pl.BlockSpec((4, HEAD), lambda j: (0, j)),
            pl.BlockSpec((s_dim, HEAD), lambda j: (0, jnp.minimum(j, hd))),
            pl.BlockSpec((s_dim, HEAD), lambda j: (0, jnp.clip(j - N_HEADS, 0, hd))),
            pl.BlockSpec((s_dim, HEAD), lambda j: (0, jnp.clip(j - 2 * N_HEADS, 0, hd))),
        ],
        out_specs=[pl.BlockSpec((s_dim, HEAD), lambda j: (0, j)), pl.BlockSpec((4, HEAD), lambda j: (0, j))],
        compiler_params=pltpu.CompilerParams(dimension_semantics=("parallel",)),
        name="conv_bwd",
    )(proj_a, conv_w, dq, dk, dv)


def _chunk_tri(n):
    r = np.arange(n)
    m = ((r[:, None] // CHUNK) == (r[None, :] // CHUNK)) & (r[:, None] >= r[None, :])
    m = m.astype(np.float32)
    return jnp.asarray(m), jnp.asarray(m.T)


def _softplus(z):
    return jnp.maximum(z, 0.0) + jnp.log(1.0 + jnp.exp(-jnp.abs(z)))


def _gates_fwd(proj_b, alog, dtb):
    tm = min(GROUP, proj_b.shape[0])
    tri, _ = _chunk_tri(tm)

    def fn(r, c):
        b = r[0]
        a = pltpu.roll(b, LANES - N_HEADS, 1)
        alog_, dtb_, tri_ = c
        g = -jnp.exp(alog_) * _softplus(a + dtb_)
        return [_sig(b), _dot32(tri_, g)], []

    return _rowwise(fn, [(proj_b, WB_BA // LANES, LANES)], [alog, dtb, tri],
                    [(LANES, F32), (LANES, F32)], tm=tm, name="gates_fwd")


def _gates_bwd(proj_b, alog, dtb, gc, d_beta, d_gc, d_egl_rows):
    tm = min(GROUP, proj_b.shape[0])
    _, tri_t = _chunk_tri(tm)

    def fn(r, c):
        b, gc_, d_beta_, d_gc_, d_egl_ = r
        a = pltpu.roll(b, LANES - N_HEADS, 1)
        alog_, dtb_, tri_t_ = c
        z = a + dtb_
        ea = jnp.exp(alog_)
        g = -ea * _softplus(z)
        dg = _dot32(tri_t_, d_gc_ + d_egl_ * jnp.exp(gc_))
        d_a = dg * (-ea) * _sig(z)
        beta = _sig(b)
        d_ba = d_beta_ * beta * (1.0 - beta) + pltpu.roll(d_a, N_HEADS, 1)
        return [d_ba], [_colsum(dg * g), _colsum(d_a)]

    return _rowwise(fn, [(proj_b, WB_BA // LANES, LANES), gc, d_beta, d_gc, d_egl_rows],
                    [alog, dtb, tri_t], [(LANES, BF16)], accs=[(1, LANES), (1, LANES)], tm=tm,
                    name="gates_bwd")


def _group_masks(n):
    r = lax.broadcasted_iota(jnp.int32, (n, n), 0)
    c = lax.broadcasted_iota(jnp.int32, (n, n), 1)
    same = (r // CHUNK) == (c // CHUNK)
    below, s = [], 2
    while s < CHUNK:
        below.append(jnp.logical_and((r // (2 * s)) == (c // (2 * s)),
                                     jnp.logical_and((r // s) % 2 == 1, (c // s) % 2 == 0)))
        s *= 2
    return dict(same=same, tril=jnp.logical_and(same, r >= c), strict=jnp.logical_and(same, r > c),
                last=c == (r // CHUNK) * CHUNK + (CHUNK - 1), eye=r == c, pair=(r // 2) == (c // 2), below=below)


def _inv_unit_lower(l_mats, mk):
    eye_f = mk["eye"].astype(F32)
    ts = [eye_f - jnp.where(mk["pair"], l_mat, 0.0) for l_mat in l_mats]
    for below in mk["below"]:
        halves = [_split_bf16(t) for t in ts]
        mids = [_dot3(h, jnp.where(below, l_mat, 0.0)) for h, l_mat in zip(halves, l_mats)]
        ts = [t - _dot3(m, h) for t, m, h in zip(ts, mids, halves)]
    return ts


def _unfold_blocks(folded, mask):
    n = folded.shape[0]
    return jnp.where(mask, jnp.concatenate([folded] * (n // CHUNK), axis=1), 0.0)


def _head_cols(beta, gc, gc_t, h):
    lane = lax.broadcasted_iota(jnp.int32, beta.shape, 1)
    sub = lax.broadcasted_iota(jnp.int32, gc_t.shape, 0)
    bcol = _rowsum(jnp.where(lane == h, beta, 0.0))
    gcol = _rowsum(jnp.where(lane == h, gc, 0.0))
    grow = _colsum(jnp.where(sub == h, gc_t, 0.0))
    return bcol, gcol, grow


def _prep_common(q, k, bcol, gcol, grow, mk, t_folded=None):
    n = q.shape[0]
    tril = mk["tril"]
    decay = jnp.where(tril, jnp.exp(jnp.where(tril, gcol - grow, 0.0)), 0.0)
    glast = _rowsum(jnp.where(mk["last"], jnp.broadcast_to(grow, (n, n)), 0.0))
    e = jnp.exp(gcol)
    ekt = jnp.exp(glast - gcol)
    kb = k * bcol
    kk = _dot(kb, k, NT)
    qk = _dot(q, k, NT)
    p = dict(decay=decay, e=e, ekt=ekt, kb=kb, kk=kk, qk=qk)
    if t_folded is not None:
        p["t"] = _unfold_blocks(t_folded, mk["same"])
    return p


GROUPS_PER_STEP = 4
SCAN_CHUNKS_PER_STEP = 4


def _fold_blocks(m):
    n = m.shape[0]
    out = m[:, 0:CHUNK]
    for b in range(1, n // CHUNK):
        out = out + m[:, b * CHUNK:(b + 1) * CHUNK]
    return out


def _gdr_prep_fwd(qkvn, beta, gc, gc_t):
    s_dim = qkvn.shape[0]
    tg = min(GROUP, s_dim)
    n_sub = min(GROUPS_PER_STEP, s_dim // tg)
    tb = tg * n_sub

    def body(q_ref, k_ref, v_ref, b_ref, g_ref, gt_ref, u_ref, w_ref, qd_ref, kt_ref, a_ref, t_ref):
        h = pl.program_id(0)
        mk = _group_masks(tg)
        parts = []
        for s in range(n_sub):
            rows = slice(s * tg, (s + 1) * tg)
            q, k, v = q_ref[rows, :], k_ref[rows, :], v_ref[rows, :]
            bcol, gcol, grow = _head_cols(b_ref[rows, :], g_ref[rows, :], gt_ref[:, rows], h)
            p = _prep_common(q, k, bcol, gcol, grow, mk)
            qd_ref[rows, :] = q * p["e"]
            kt_ref[rows, :] = k * p["ekt"]
            a_ref[rows, :] = _fold_blocks(jnp.where(mk["tril"], p["qk"] * p["decay"], 0.0))
            parts.append((rows, v * bcol, p["kb"] * p["e"], jnp.where(mk["strict"], p["kk"] * p["decay"], 0.0)))
        t_mats = _inv_unit_lower([part[3] for part in parts], mk)
        for (rows, vb, kbe, _), t_mat in zip(parts, t_mats):
            u_ref[rows, :] = _dot(t_mat, vb)
            w_ref[rows, :] = _dot(t_mat, kbe)
            t_ref[rows, :] = _fold_blocks(t_mat)

    row = lambda off: pl.BlockSpec((tb, HEAD), functools.partial(lambda h, m, off: (m, h + off), off=off))
    full = pl.BlockSpec((tb, LANES), lambda h, m: (m, 0))
    o_spec = pl.BlockSpec((tb, HEAD), lambda h, m: (m, h))
    a_spec = pl.BlockSpec((None, tb, CHUNK), lambda h, m: (h, m, 0))
    wide = jax.ShapeDtypeStruct((s_dim, N_HEADS * HEAD), F32)
    folded = jax.ShapeDtypeStruct((N_HEADS, s_dim, CHUNK), F32)
    return pl.pallas_call(
        body,
        out_shape=[wide, wide, wide, wide, folded, folded],
        grid=(N_HEADS, s_dim // tb),
        in_specs=[row(0), row(N_HEADS), row(2 * N_HEADS), full, full, pl.BlockSpec((8, tb), lambda h, m: (0, m))],
        out_specs=[o_spec, o_spec, o_spec, o_spec, a_spec, a_spec],
        compiler_params=pltpu.CompilerParams(dimension_semantics=("parallel", "parallel")),
        name="gdr_prep_fwd",
    )(qkvn, qkvn, qkvn, beta, gc, gc_t)


def _gdr_prep_bwd(qkvn, beta, gc, gc_t, t_fold, u, w, du, dw, dqd, dkt, d_a):
    s_dim = qkvn.shape[0]
    tg = min(GROUP, s_dim)
    n_sub = min(GROUPS_PER_STEP, s_dim // tg)
    tb = tg * n_sub

    def body(q_ref, k_ref, v_ref, b_ref, g_ref, gt_ref, t_ref, u_ref, w_ref, du_ref, dw_ref, dqd_ref, dkt_ref,
             da_ref, dq_ref, dk_ref, dv_ref, db_ref, dg_ref):
        h = pl.program_id(1)

        @pl.when(h == 0)
        def _():
            db_ref[...] = jnp.zeros_like(db_ref)
            dg_ref[...] = jnp.zeros_like(dg_ref)

        mk = _group_masks(tg)
        lane = lax.broadcasted_iota(jnp.int32, (tg, LANES), 1)
        for s in range(n_sub):
            rows = slice(s * tg, (s + 1) * tg)
            q, k, v = q_ref[rows, :], k_ref[rows, :], v_ref[rows, :]
            bcol, gcol, grow = _head_cols(b_ref[rows, :], g_ref[rows, :], gt_ref[:, rows], h)
            p = _prep_common(q, k, bcol, gcol, grow, mk, t_ref[rows, :])
            t_mat, decay, e, ekt, kb = p["t"], p["decay"], p["e"], p["ekt"], p["kb"]
            du_, dw_, dqd_, dkt_ = du_ref[rows, :], dw_ref[rows, :], dqd_ref[rows, :], dkt_ref[rows, :]
            dvb = _dot(t_mat, du_, TN)
            dkbe = _dot(t_mat, dw_, TN)
            d_l = -(_dot(dvb, u_ref[rows, :], NT) + _dot(dkbe, w_ref[rows, :], NT))
            m1 = jnp.where(mk["strict"], d_l, 0.0)
            m2 = _unfold_blocks(da_ref[rows, :], mk["tril"])
            d_kk = m1 * decay
            d_qk = m2 * decay
            d_decay = m1 * p["kk"] + m2 * p["qk"]
            dkb = _dot(d_kk, k) + dkbe * e
            dk = _dot(d_kk, kb, TN) + _dot(d_qk, q, TN) + dkt_ * ekt + dkb * bcol
            dq = _dot(d_qk, k) + dqd_ * e
            d_beta = _rowsum(dkb * k) + _rowsum(dvb * v)
            d_e = _rowsum(dkbe * kb) + _rowsum(dqd_ * q)
            d_ekt = _rowsum(dkt_ * k) * ekt
            d_diff = d_decay * decay
            d_grow = -_colsum(d_diff) + _colsum(jnp.where(mk["last"], jnp.broadcast_to(d_ekt, (tg, tg)), 0.0))
            d_gcol = d_e * e - d_ekt + _rowsum(d_diff)
            d_gcol = d_gcol + _rowsum(jnp.where(mk["eye"], jnp.broadcast_to(d_grow, (tg, tg)), 0.0))
            dq_ref[rows, :] = dq
            dk_ref[rows, :] = dk
            dv_ref[rows, :] = dvb * bcol
            db_ref[rows, :] = jnp.where(lane == h, d_beta, db_ref[rows, :])
            dg_ref[rows, :] = jnp.where(lane == h, d_gcol, dg_ref[rows, :])

    row = lambda off: pl.BlockSpec((tb, HEAD), functools.partial(lambda m, h, off: (m, h + off), off=off))
    full = pl.BlockSpec((tb, LANES), lambda m, h: (m, 0))
    o_spec = pl.BlockSpec((tb, HEAD), lambda m, h: (m, h))
    a_spec = pl.BlockSpec((None, tb, CHUNK), lambda m, h: (h, m, 0))
    wide = jax.ShapeDtypeStruct((s_dim, N_HEADS * HEAD), F32)
    lanes = jax.ShapeDtypeStruct((s_dim, LANES), F32)
    return pl.pallas_call(
        body,
        out_shape=[wide, wide, wide, lanes, lanes],
        grid=(s_dim // tb, N_HEADS),
        in_specs=[row(0), row(N_HEADS), row(2 * N_HEADS), full, full, pl.BlockSpec((8, tb), lambda m, h: (0, m)),
                  a_spec, o_spec, o_spec, o_spec, o_spec, o_spec, o_spec, a_spec],
        out_specs=[o_spec, o_spec, o_spec, full, full],
        compiler_params=pltpu.CompilerParams(dimension_semantics=("parallel", "arbitrary")),
        name="gdr_prep_bwd",
    )(qkvn, qkvn, qkvn, beta, gc, gc_t, t_fold, u, w, du, dw, dqd, dkt, d_a)


def _gdr_scan_fwd(u, w, qd, kt, a_mat, gc):
    s_dim = u.shape[0]
    n_chunks = s_dim // CHUNK
    per = min(SCAN_CHUNKS_PER_STEP, n_chunks)
    tb = per * CHUNK

    def body(u_ref, w_ref, qd_ref, kt_ref, a_ref, g_ref, o_ref, st_ref, state):
        @pl.when(pl.program_id(0) == 0)
        def _():
            state[...] = jnp.zeros_like(state)

        heads = range(N_HEADS)
        cols = [slice(h * HEAD, (h + 1) * HEAD) for h in heads]
        for i in range(per):
            rows = slice(i * CHUNK, (i + 1) * CHUNK)
            egl = jnp.exp(g_ref[(i + 1) * CHUNK - 1:(i + 1) * CHUNK, :])
            s_b = [state[h].astype(BF16) for h in heads]
            for h in heads:
                st_ref[i, h] = state[h]
            ws = [_dot(w_ref[rows, cs], s) for cs, s in zip(cols, s_b)]
            qs = [_dot(qd_ref[rows, cs], s) for cs, s in zip(cols, s_b)]
            vns = [(u_ref[rows, cs] - ws_h).astype(BF16) for cs, ws_h in zip(cols, ws)]
            avs = [_dot(a_ref[h, rows, :], vn) for h, vn in zip(heads, vns)]
            kvs = [_dot(kt_ref[rows, cs], vn, TN) for cs, vn in zip(cols, vns)]
            for h, cs in zip(heads, cols):
                o_ref[rows, cs] = qs[h] + avs[h]
                state[h] = state[h] * egl[:, h:h + 1] + kvs[h]

    wide = pl.BlockSpec((tb, N_HEADS * HEAD), lambda n: (n, 0))
    return pl.pallas_call(
        body,
        out_shape=[jax.ShapeDtypeStruct((s_dim, N_HEADS * HEAD), F32),
                   jax.ShapeDtypeStruct((n_chunks, N_HEADS, HEAD, HEAD), F32)],
        grid=(n_chunks // per,),
        in_specs=[wide, wide, wide, wide, pl.BlockSpec((N_HEADS, tb, CHUNK), lambda n: (0, n, 0)),
                  pl.BlockSpec((tb, LANES), lambda n: (n, 0))],
        out_specs=[wide, pl.BlockSpec((per, N_HEADS, HEAD, HEAD), lambda n: (n, 0, 0, 0))],
        scratch_shapes=[pltpu.VMEM((N_HEADS, HEAD, HEAD), F32)],
        compiler_params=pltpu.CompilerParams(dimension_semantics=("arbitrary",)),
        name="gdr_scan_fwd",
    )(u, w, qd, kt, a_mat, gc)


def _gdr_scan_bwd(u, w, qd, kt, a_mat, gc, states, d_o):
    s_dim = u.shape[0]
    n_chunks = s_dim // CHUNK
    per = min(SCAN_CHUNKS_PER_STEP, n_chunks)
    tb = per * CHUNK
    last = n_chunks // per - 1

    def body(u_ref, w_ref, qd_ref, kt_ref, a_ref, g_ref, st_ref, do_ref,
             du_ref, dw_ref, dqd_ref, dkt_ref, da_ref, de_ref, d_state):
        @pl.when(pl.program_id(0) == 0)
        def _():
            d_state[...] = jnp.zeros_like(d_state)

        heads = range(N_HEADS)
        cols = [slice(h * HEAD, (h + 1) * HEAD) for h in heads]
        for i in reversed(range(per)):
            rows = slice(i * CHUNK, (i + 1) * CHUNK)
            egl = jnp.exp(g_ref[(i + 1) * CHUNK - 1:(i + 1) * CHUNK, :])
            s_b = [st_ref[i, h].astype(BF16) for h in heads]
            ds_b = [d_state[h].astype(BF16) for h in heads]
            dos = [do_ref[rows, cs].astype(BF16) for cs in cols]
            w_b = [w_ref[rows, cs].astype(BF16) for cs in cols]
            ws = [_dot(w_h, s) for w_h, s in zip(w_b, s_b)]
            ados = [_dot(a_ref[h, rows, :], do, TN) for h, do in zip(heads, dos)]
            kds = [_dot(kt_ref[rows, cs], ds) for cs, ds in zip(cols, ds_b)]
            dqds = [_dot(do, s, NT) for do, s in zip(dos, s_b)]
            qdos = [_dot(qd_ref[rows, cs], do, TN) for cs, do in zip(cols, dos)]
            vns = [(u_ref[rows, cs] - ws_h).astype(BF16) for cs, ws_h in zip(cols, ws)]
            dvns = [a + k_ for a, k_ in zip(ados, kds)]
            dvn_b = [d.astype(BF16) for d in dvns]
            das = [_dot(do, vn, NT) for do, vn in zip(dos, vns)]
            dkts = [_dot(vn, ds, NT) for vn, ds in zip(vns, ds_b)]
            dws = [_dot(d, s, NT) for d, s in zip(dvn_b, s_b)]
            wds = [_dot(w_h, d, TN) for w_h, d in zip(w_b, dvn_b)]
            for h, cs in zip(heads, cols):
                ds_n = d_state[h]
                de = jnp.sum(_rowsum(ds_n * st_ref[i, h]), axis=0, keepdims=True)
                de_ref[i, h:h + 1, :] = jnp.broadcast_to(de, (1, LANES))
                dqd_ref[rows, cs] = dqds[h]
                da_ref[h, rows, :] = das[h]
                dkt_ref[rows, cs] = dkts[h]
                du_ref[rows, cs] = dvns[h]
                dw_ref[rows, cs] = -dws[h]
                d_state[h] = ds_n * egl[:, h:h + 1] + qdos[h] - wds[h]

    wide = pl.BlockSpec((tb, N_HEADS * HEAD), lambda n: (last - n, 0))
    a_spec = pl.BlockSpec((N_HEADS, tb, CHUNK), lambda n: (0, last - n, 0))
    wide_shape = jax.ShapeDtypeStruct((s_dim, N_HEADS * HEAD), F32)
    return pl.pallas_call(
        body,
        out_shape=[wide_shape, wide_shape, wide_shape, wide_shape,
                   jax.ShapeDtypeStruct((N_HEADS, s_dim, CHUNK), F32),
                   jax.ShapeDtypeStruct((n_chunks, N_HEADS, LANES), F32)],
        grid=(n_chunks // per,),
        in_specs=[wide, wide, wide, wide, a_spec, pl.BlockSpec((tb, LANES), lambda n: (last - n, 0)),
                  pl.BlockSpec((per, N_HEADS, HEAD, HEAD), lambda n: (last - n, 0, 0, 0)), wide],
        out_specs=[wide, wide, wide, wide, a_spec, pl.BlockSpec((per, N_HEADS, LANES), lambda n: (last - n, 0, 0))],
        scratch_shapes=[pltpu.VMEM((N_HEADS, HEAD, HEAD), F32)],
        compiler_params=pltpu.CompilerParams(dimension_semantics=("arbitrary",)),
        name="gdr_scan_bwd",
    )(u, w, qd, kt, a_mat, gc, states, d_o)


FUSED_ROWS = 512


def _gdr_out_fwd(o_dn, proj_a, dn_w, w_br):
    def fn(r, c):
        o, z = r
        w_, w_br_ = c
        outs = []
        for h in range(N_HEADS):
            cs = slice(h * HEAD, (h + 1) * HEAD)
            oh, zh = o[:, cs], z[:, cs]
            rr = lax.rsqrt(_rowmean(oh * oh) + EPS_RMS)
            outs.append(oh * rr * w_ * (zh * _sig(zh)))
        og = jnp.concatenate(outs, axis=1).astype(BF16)
        return [og, _dot(og, w_br_)], []

    return _rowwise(fn, [o_dn, (proj_a, 3, D_MODEL)], [dn_w, w_br], [(D_MODEL, BF16), (D_MODEL, BF16)],
                    tm=FUSED_ROWS, name="gdr_out_fwd")


def _gdr_out_bwd(o_dn, proj_a, d_y_dn, dn_w, w_br):
    def fn(r, c):
        o, z, dy = r
        w_, w_br_ = c
        dg = _dot(dy, w_br_, NT)
        d_o, d_z = [], []
        d_w = jnp.zeros((1, HEAD), F32)
        for h in range(N_HEADS):
            cs = slice(h * HEAD, (h + 1) * HEAD)
            oh, zh, dgh = o[:, cs], z[:, cs], dg[:, cs]
            rr = lax.rsqrt(_rowmean(oh * oh) + EPS_RMS)
            sz = zh * _sig(zh)
            d_n = dgh * sz
            d_z.append(dgh * (oh * rr * w_) * _silu_grad(zh))
            d_w = d_w + _colsum(d_n * oh * rr)
            gw = d_n * w_
            d_o.append(rr * gw - oh * (rr * rr * rr) * _rowmean(gw * oh))
        return [jnp.concatenate(d_o, axis=1), jnp.concatenate(d_z, axis=1)], [d_w]

    return _rowwise(fn, [o_dn, (proj_a, 3, D_MODEL), d_y_dn], [dn_w, w_br], [(D_MODEL, F32), (D_MODEL, BF16)],
                    accs=[(1, HEAD)], tm=FUSED_ROWS, name="gdr_out_bwd")


def _rms_fwd(x, w):
    r = lax.rsqrt(_rowmean(x * x) + EPS_RMS)
    return x * r * w


def _rms_bwd(x, w, dy):
    r = lax.rsqrt(_rowmean(x * x) + EPS_RMS)
    gw = dy * w
    return r * gw - x * (r * r * r) * _rowmean(gw * x), _colsum(dy * x * r)


def _mla_norm_fwd(proj_b, qn_w, kvn_w):
    def fn(r, c):
        return [_rms_fwd(r[0], c[0]), _rms_fwd(r[1], c[1])], []

    return _rowwise(fn, [(proj_b, WB_CQ // Q_LORA, Q_LORA), (proj_b, WB_CKV // KV_LORA, KV_LORA)], [qn_w, kvn_w],
                    [(Q_LORA, BF16), (KV_LORA, BF16)], name="mla_norm_fwd")


def _mla_norm_bwd(proj_b, qn_w, kvn_w, d_cq, d_ckv):
    def fn(r, c):
        dx1, dw1 = _rms_bwd(r[0], c[0], r[2])
        dx2, dw2 = _rms_bwd(r[1], c[1], r[3])
        return [dx1, dx2], [dw1, dw2]

    return _rowwise(fn, [(proj_b, WB_CQ // Q_LORA, Q_LORA), (proj_b, WB_CKV // KV_LORA, KV_LORA), d_cq, d_ckv],
                    [qn_w, kvn_w], [(Q_LORA, BF16), (KV_LORA, BF16)], accs=[(1, Q_LORA), (1, KV_LORA)],
                    name="mla_norm_bwd")


def _rope_consts():
    inv = ROPE_BASE ** (-np.arange(0, ROPE, 2, dtype=np.float32) / ROPE)
    t = np.zeros((4, LANES), np.float32)
    t[0, :32] = inv
    t[0, 32:64] = inv
    t[1, :64] = 1.0
    t[2, 32:64] = 1.0
    t[3, :32] = -1.0
    return jnp.asarray(t)


def _rope_tables(pos, consts, width):
    ang = pos * consts[0:1, :]
    cosv, sinv = jnp.cos(ang), jnp.sin(ang)
    reps = width // LANES
    tile = (lambda t: jnp.concatenate([t] * reps, axis=1)) if reps > 1 else (lambda t: t)
    return tile(cosv * consts[1:2, :]), tile(sinv * consts[2:3, :]), tile(sinv * consts[3:4, :])


def _rope_apply(t, tabs):
    cos_t, sin_a, sin_b = tabs
    width = t.shape[1]
    return t * cos_t + pltpu.roll(t, 32, 1) * sin_a + pltpu.roll(t, width - 32, 1) * sin_b


def _rope_transpose(d, tabs):
    cos_t, sin_a, sin_b = tabs
    width = d.shape[1]
    return d * cos_t + pltpu.roll(d * sin_a, width - 32, 1) + pltpu.roll(d * sin_b, 32, 1)


QK_HEAD = 2 * HEAD


def _interleave_heads(a, b):
    parts = []
    for h in range(N_HEADS):
        parts.append(a[:, h * HEAD:(h + 1) * HEAD])
        parts.append(b if b.shape[1] == LANES else b[:, h * LANES:(h + 1) * LANES])
    return jnp.concatenate(parts, axis=1)


def _mla_qk_fwd(q_full, k_nope, proj_b, pos):
    consts = _rope_consts()

    def fn(r, c):
        qf, kn, kr, pos_ = r
        qn, qr = qf[:, :D_MODEL], qf[:, D_MODEL:]
        qr = _rope_apply(qr, _rope_tables(pos_, c[0], D_MODEL))
        kr = _rope_apply(kr, _rope_tables(pos_, c[0], LANES))
        return [_interleave_heads(qn, qr) * SCALE, _interleave_heads(kn, kr)], []

    return _rowwise(fn, [q_full, k_nope, (proj_b, WB_KR // LANES, LANES), pos], [consts],
                    [(N_HEADS * QK_HEAD, BF16), (N_HEADS * QK_HEAD, BF16)], name="mla_qk_fwd")


def _mla_qk_bwd(d_qc, d_kc, pos):
    consts = _rope_consts()

    def fn(r, c):
        dq, dk, pos_ = r
        even = lambda t: jnp.concatenate([t[:, (2 * h) * LANES:(2 * h + 1) * LANES] for h in range(N_HEADS)], axis=1)
        odd = lambda t: jnp.concatenate([t[:, (2 * h + 1) * LANES:(2 * h + 2) * LANES] for h in range(N_HEADS)], axis=1)
        d_qr_raw = _rope_transpose(odd(dq), _rope_tables(pos_, c[0], D_MODEL)) * SCALE
        dkr = dk[:, LANES:2 * LANES]
        for h in range(1, N_HEADS):
            dkr = dkr + dk[:, (2 * h + 1) * LANES:(2 * h + 2) * LANES]
        return [jnp.concatenate([even(dq) * SCALE, d_qr_raw], axis=1), even(dk),
                _rope_transpose(dkr, _rope_tables(pos_, c[0], LANES))], []

    return _rowwise(fn, [d_qc, d_kc, pos], [consts], [(2 * D_MODEL, BF16), (D_MODEL, BF16), (LANES, BF16)],
                    name="mla_qk_bwd")


def _causal_mask_t(st, key0, query0):
    key = lax.broadcasted_iota(jnp.int32, st.shape, 0) + key0
    query = lax.broadcasted_iota(jnp.int32, st.shape, 1) + query0
    return jnp.where(key <= query, st, NEG_BIG)


def _attn_tiles(s_dim):
    tq = min(512, s_dim)
    n_chains = 2 if s_dim >= 2 * tq else 1
    return tq, n_chains, min(512, s_dim)


def _diagonal_chains(t, tq, n_chains, tk):
    return [(c, (t + 1) * tk - 1 > c * tq) for c in range(n_chains) if t * tk < (c + 1) * tq]


def _attn_fwd(qc, kc, vt):
    s_dim = qc.shape[0]
    tq, n_chains, tk = _attn_tiles(s_dim)
    tqs = tq * n_chains

    def body(q_ref, k_ref, vt_ref, o_ref, lse_ref, m_s, l_s, acc):
        qi = pl.program_id(1)
        m_s[...] = jnp.full_like(m_s, NEG_BIG)
        l_s[...] = jnp.zeros_like(l_s)
        acc[...] = jnp.zeros_like(acc)

        def make_step(chains):
            def step(j, carry):
                ks = pl.multiple_of(j * tk, tk)
                kb, vtb = k_ref[pl.ds(ks, tk), :], vt_ref[:, pl.ds(ks, tk)]
                cols = [slice(c * tq, (c + 1) * tq) for c, _ in chains]
                sts = [_dot(kb, q_ref[cs, :], NT) for cs in cols]
                sts = [_causal_mask_t(st, j * tk, qi * tqs + c * tq) if masked else st
                       for st, (c, masked) in zip(sts, chains)]
                m_prevs = [m_s[:, cs] for cs in cols]
                m_news = [jnp.maximum(mp, jnp.max(st, axis=0, keepdims=True)) for mp, st in zip(m_prevs, sts)]
                alphas = [jnp.exp(mp - mn) for mp, mn in zip(m_prevs, m_news)]
                pts = [jnp.exp(st - mn) for st, mn in zip(sts, m_news)]
                pvs = [_dot(vtb, pt) for pt in pts]
                for cs, mn, al, pt, pv in zip(cols, m_news, alphas, pts, pvs):
                    l_s[:, cs] = al * l_s[:, cs] + _colsum(pt)
                    m_s[:, cs] = mn
                    acc[:, cs] = acc[:, cs] * al + pv
                return carry
            return step

        below = qi * (tqs // tk)
        lax.fori_loop(0, below, make_step([(c, False) for c in range(n_chains)]), 0)
        for t in range(tqs // tk):
            make_step(_diagonal_chains(t, tq, n_chains, tk))(below + t, 0)
        l = l_s[...]
        o_ref[...] = jnp.transpose(acc[...] / l)
        lse_ref[...] = m_s[...] + jnp.log(l)

    return pl.pallas_call(
        body,
        out_shape=[jax.ShapeDtypeStruct((s_dim, N_HEADS * HEAD), F32), jax.ShapeDtypeStruct((N_HEADS, 1, s_dim), F32)],
        grid=(N_HEADS, s_dim // tqs),
        in_specs=[pl.BlockSpec((tqs, QK_HEAD), lambda h, qi: (qi, h)),
                  pl.BlockSpec((s_dim, QK_HEAD), lambda h, qi: (0, h)),
                  pl.BlockSpec((HEAD, s_dim), lambda h, qi: (h, 0))],
        out_specs=[pl.BlockSpec((tqs, HEAD), lambda h, qi: (qi, h)),
                   pl.BlockSpec((None, 1, tqs), lambda h, qi: (h, 0, qi))],
        scratch_shapes=[pltpu.VMEM((1, tqs), F32), pltpu.VMEM((1, tqs), F32), pltpu.VMEM((HEAD, tqs), F32)],
        compiler_params=pltpu.CompilerParams(dimension_semantics=("parallel", "parallel")),
        name="attn_fwd",
    )(qc, kc, vt)


def _attn_bwd(qc, kc, kct, v, o, d_o, lse):
    s_dim = qc.shape[0]
    tq, n_chains, tk = _attn_tiles(s_dim)
    tqs = tq * n_chains

    def body(q_ref, k_ref, kt_ref, v_ref, o_ref, do_ref, lse_ref, dq_ref, dk_ref, dv_ref, dqt_acc, dv_acc):
        qi = pl.program_id(1)

        @pl.when(qi == 0)
        def _():
            dk_ref[...] = jnp.zeros_like(dk_ref)
            dv_acc[...] = jnp.zeros_like(dv_acc)

        dqt_acc[...] = jnp.zeros_like(dqt_acc)
        do_f = do_ref[...]
        do_all = do_f.astype(BF16)
        q_all = q_ref[...]
        lse_row = lse_ref[...]
        delta_row = _dot3(jnp.ones((8, HEAD), F32), o_ref[...] * do_f, NT)[0:1, :]

        def make_step(chains):
            rows = slice(chains[0][0] * tq, (chains[-1][0] + 1) * tq)

            def step(j, carry):
                ks = pl.multiple_of(j * tk, tk)
                kb, vb, ktb = k_ref[pl.ds(ks, tk), :], v_ref[pl.ds(ks, tk), :], kt_ref[:, pl.ds(ks, tk)]
                cols = [slice(c * tq, (c + 1) * tq) for c, _ in chains]
                sts = [_dot(kb, q_all[cs, :], NT) for cs in cols]
                sts = [_causal_mask_t(st, j * tk, qi * tqs + c * tq) if masked else st
                       for st, (c, masked) in zip(sts, chains)]
                dpts = [_dot(vb, do_all[cs, :], NT) for cs in cols]
                pts = [jnp.exp(st - lse_row[:, cs]) for st, cs in zip(sts, cols)]
                dsts = [(pt * (dpt - delta_row[:, cs])).astype(BF16) for pt, dpt, cs in zip(pts, dpts, cols)]
                pts = [pt.astype(BF16) for pt in pts]
                dqs = [_dot(ktb, dst) for dst in dsts]
                for cs, dq in zip(cols, dqs):
                    dqt_acc[:, cs] += dq
                pt_all = jnp.concatenate(pts, axis=1) if len(chains) > 1 else pts[0]
                dst_all = jnp.concatenate(dsts, axis=1) if len(chains) > 1 else dsts[0]
                dk_ref[pl.ds(ks, tk), :] += _dot(dst_all, q_all[rows, :])
                dv_acc[pl.ds(ks, tk), :] += _dot(pt_all, do_all[rows, :])
                return carry
            return step

        below = qi * (tqs // tk)
        lax.fori_loop(0, below, make_step([(c, False) for c in range(n_chains)]), 0)
        for t in range(tqs // tk):
            make_step(_diagonal_chains(t, tq, n_chains, tk))(below + t, 0)
        dq_ref[...] = jnp.transpose(dqt_acc[...])

        @pl.when(qi == s_dim // tqs - 1)
        def _():
            dv_ref[...] = dv_acc[...].astype(dv_ref.dtype)

    q_spec = pl.BlockSpec((tqs, QK_HEAD), lambda h, qi: (qi, h))
    o_spec = pl.BlockSpec((tqs, HEAD), lambda h, qi: (qi, h))
    k_spec = pl.BlockSpec((s_dim, QK_HEAD), lambda h, qi: (0, h))
    v_spec = pl.BlockSpec((s_dim, HEAD), lambda h, qi: (0, h))
    wide2 = jax.ShapeDtypeStruct((s_dim, N_HEADS * QK_HEAD), F32)
    return pl.pallas_call(
        body,
        out_shape=[wide2, wide2, jax.ShapeDtypeStruct((s_dim, N_HEADS * HEAD), BF16)],
        grid=(N_HEADS, s_dim // tqs),
        in_specs=[q_spec, k_spec, pl.BlockSpec((QK_HEAD, s_dim), lambda h, qi: (h, 0)), v_spec, o_spec, o_spec,
                  pl.BlockSpec((None, 1, tqs), lambda h, qi: (h, 0, qi))],
        out_specs=[q_spec, k_spec, v_spec],
        scratch_shapes=[pltpu.VMEM((QK_HEAD, tqs), F32), pltpu.VMEM((s_dim, HEAD), F32)],
        compiler_params=pltpu.CompilerParams(dimension_semantics=("parallel", "arbitrary")),
        name="attn_bwd",
    )(qc, kc, kct, v, o, d_o, lse)


def _mix_proj_ln1(y_dn, y_mla, proj_g, x, w_o, g, b):
    s_dim = x.shape[0]
    tm = min(512, s_dim)

    def body(yd_ref, ym_ref, g_ref, x_ref, w_ref, lg_ref, lb_ref, mixed_ref, a1_ref, h1_ref, h1b_ref):
        gates = g_ref[...].astype(F32)
        mixed = (_sig(gates[:, :D_MODEL]) * yd_ref[...].astype(F32)
                 + _sig(gates[:, D_MODEL:]) * ym_ref[...].astype(F32)).astype(BF16)
        a1 = _dot(mixed, w_ref[...])
        xh, _ = _ln_stats(ALPHA * x_ref[...] + a1)
        y = xh * lg_ref[...] + lb_ref[...]
        mixed_ref[...] = mixed
        a1_ref[...] = a1
        h1_ref[...] = y
        h1b_ref[...] = y.astype(BF16)

    row = lambda width: pl.BlockSpec((tm, width), lambda i: (i, 0))
    whole = lambda a: pl.BlockSpec(a.shape, lambda i: (0, 0))
    sds = lambda dt: jax.ShapeDtypeStruct((s_dim, D_MODEL), dt)
    return pl.pallas_call(
        body,
        out_shape=[sds(BF16), sds(F32), sds(F32), sds(BF16)],
        grid=(s_dim // tm,),
        in_specs=[row(D_MODEL), row(D_MODEL), row(2 * D_MODEL), row(D_MODEL), whole(w_o), whole(g), whole(b)],
        out_specs=[row(D_MODEL)] * 4,
        compiler_params=pltpu.CompilerParams(dimension_semantics=("parallel",)),
        name="mix_proj_ln1",
    )(y_dn, y_mla, proj_g, x, w_o, g, b)


def _ln1_mix_bwd(x, a1, d_h1, y_dn, y_mla, proj_g, g, w_o):
    def fn(r, c):
        x_, a1_, dy, yd, ym, gates = r
        g_, w_o_ = c
        xh, rr = _ln_stats(ALPHA * x_ + a1_)
        dz = _ln_bwd(dy, xh, rr, g_)
        dz_b = dz.astype(BF16)
        dm = _dot(dz_b, w_o_, NT)
        sd, sm = _sig(gates[:, :D_MODEL]), _sig(gates[:, D_MODEL:])
        d_g = jnp.concatenate([dm * yd * sd * (1.0 - sd), dm * ym * sm * (1.0 - sm)], axis=1)
        return [dz_b, ALPHA * dz, d_g, dm * sd, dm * sm], [_colsum(dy * xh), _colsum(dy)]

    return _rowwise(fn, [x, a1, d_h1, y_dn, y_mla, proj_g], [g, w_o],
                    [(D_MODEL, BF16), (D_MODEL, F32), (2 * D_MODEL, BF16), (D_MODEL, BF16), (D_MODEL, BF16)],
                    accs=[(1, D_MODEL), (1, D_MODEL)], tm=FUSED_ROWS, name="ln1_mix_bwd")


def _ln_stats(z):
    mu = _rowmean(z)
    zc = z - mu
    r = lax.rsqrt(_rowmean(zc * zc) + EPS_LN)
    return zc * r, r


def _ln_bwd(dy, xh, r, g):
    dxh = dy * g
    return r * (dxh - _rowmean(dxh) - xh * _rowmean(dxh * xh))


def _ffn_in_act(h1b, w_t):
    s_dim, k_dim = h1b.shape
    hidden = w_t.shape[0] // 2
    tm, tn = min(512, s_dim), _pick_wide(hidden)
    nt = hidden // tn

    def body(a_ref, bg_ref, bu_ref, gt_ref, up_ref, act_ref):
        a = a_ref[...]
        gt, up = _dot(a, bg_ref[...], NT), _dot(a, bu_ref[...], NT)
        gt_ref[...] = gt.astype(BF16)
        up_ref[...] = up.astype(BF16)
        act_ref[...] = (gt * _sig(gt) * up).astype(BF16)

    o_spec = pl.BlockSpec((tm, tn), lambda j, i: (i, j))
    sds = jax.ShapeDtypeStruct((s_dim, hidden), BF16)
    return pl.pallas_call(
        body,
        out_shape=[sds, sds, sds],
        grid=(nt, s_dim // tm),
        in_specs=[pl.BlockSpec((tm, k_dim), lambda j, i: (i, 0)), pl.BlockSpec((tn, k_dim), lambda j, i: (j, 0)),
                  pl.BlockSpec((tn, k_dim), lambda j, i: (j + nt, 0))],
        out_specs=[o_spec, o_spec, o_spec],
        compiler_params=pltpu.CompilerParams(dimension_semantics=("parallel", "parallel")),
        name="ffn_in_act",
    )(h1b, w_t, w_t)


def _act_bwd(gt, up, d_act):
    def fn(r, c):
        gt_, up_, da = r
        return [jnp.concatenate([da * up_ * _silu_grad(gt_), da * gt_ * _sig(gt_)], axis=1)], []

    return _rowwise(fn, [gt, up, d_act], [], [(2 * FFN_HIDDEN, BF16)], name="act_bwd")[0]


def _tail(h1, ffn, pg, pp, tgt, g, b):
    def fn(r, c):
        h1_, ffn_, pg_, pp_, t_ = r
        sp = _sig(pg_)
        xh, rr = _ln_stats(ALPHA * h1_ + ffn_ + sp * pp_)
        y = xh * c[0] + c[1]
        err = y - t_
        dy = err * (1.0 / D_MODEL)
        dz = _ln_bwd(dy, xh, rr, c[0])
        loss = jnp.sum(0.5 * _rowmean(err * err), axis=0, keepdims=True)
        return ([dz, dz * pp_ * sp * (1.0 - sp), dz * sp, ALPHA * dz],
                [_colsum(dy * xh), _colsum(dy), jnp.broadcast_to(loss, (1, LANES))])

    return _rowwise(fn, [h1, ffn, pg, pp, tgt], [g, b], [(D_MODEL, BF16)] * 3 + [(D_MODEL, F32)],
                    accs=[(1, D_MODEL), (1, D_MODEL), (1, LANES)], name="tail")


def _local_step(x, p, pos, tgt, w, late_weights, emit):
    w = dict(w)
    s_dim = x.shape[0]
    xb, pb = x.astype(BF16), p.astype(BF16)
    proj_a = _mm(xb, w["wa_t"], tb=True, name="f_proj_a")
    proj_g = _mm(xb, w["wg_t"], tb=True, out_dtype=BF16, name="f_proj_g")
    proj_b = _mm(xb, w["wb_t"], tb=True, name="f_proj_b")
    qkvn = _conv_fwd(proj_a, w["conv"])
    beta, gc = _gates_fwd(proj_b, w["alog"], w["dtb"])
    gc_t = jnp.transpose(gc[:, :N_HEADS])
    u, w_, qd, kt, a_mat, t_fold = _gdr_prep_fwd(qkvn, beta, gc, gc_t)
    o_dn, states = _gdr_scan_fwd(u, w_, qd, kt, a_mat, gc)
    w.update(late_weights("mix", o_dn))
    og, y_dn = _gdr_out_fwd(o_dn, proj_a, w["dnw"], w["br_dn"])
    c_q, c_kv = _mla_norm_fwd(proj_b, w["qnw"], w["kvnw"])
    q_full = _mm(c_q, w["uq"], out_dtype=BF16, name="f_q_full")
    k_nope = _mm(c_kv, w["uk"], out_dtype=BF16, name="f_k_nope")
    vv = _mm(c_kv, w["uv"], out_dtype=BF16, name="f_v")
    qc, kc = _mla_qk_fwd(q_full, k_nope, proj_b, pos)
    o_mla, lse = _attn_fwd(qc, kc, jnp.transpose(vv))
    y_mla = _mm(o_mla, w["br_mla"], out_dtype=BF16, name="f_y_mla")
    mixed, a1, h1, h1b = _mix_proj_ln1(y_dn, y_mla, proj_g, x, w["wo"], w["ln1g"], w["ln1b"])
    w.update(late_weights("ffn", a1))
    gt, up, act = _ffn_in_act(h1b, w["ffn_in_t"])
    ffn = _mm(act, w["ffn_out"], name="f_ffn")
    pg = _mm(h1b, w["ple_gate"], name="f_pg")
    pp = _mm(pb, w["ple_t"], tb=True, name="f_pp")
    g = {}
    dz2, d_pg, d_pp, dh1a, g["ln2g"], g["ln2b"], loss = _tail(h1, ffn, pg, pp, tgt, w["ln2g"], w["ln2b"])
    g["ple_t"] = _mm(d_pp, pb, ta=True, out_dtype=BF16, name="b_w_ple")
    g["ple_gate"] = _mm(h1b, d_pg, ta=True, out_dtype=BF16, name="b_w_ple_gate")
    g["ffn_out"] = _mm(act, dz2, ta=True, out_dtype=BF16, name="b_w_ffn_out")
    d_act = _mm(dz2, w["ffn_out"], tb=True, out_dtype=BF16, name="b_act")
    d_gu = _act_bwd(gt, up, d_act)
    g["ffn_in_t"] = _mm(d_gu, h1b, ta=True, out_dtype=BF16, name="b_w_ffn_in")
    d_gu = emit("ffn", g, d_gu)
    d_h1 = _mm(d_gu, w["ffn_in_t"], add=(dh1a,), name="b_h1_ffn")
    d_h1 = _mm(d_pg, w["ple_gate"], tb=True, add=(d_h1,), name="b_h1_ple")
    dz1, dxa, d_proj_g, d_y_dn, d_y_mla, g["ln1g"], g["ln1b"] = _ln1_mix_bwd(
        x, a1, d_h1, y_dn, y_mla, proj_g, w["ln1g"], w["wo"])
    g["wo"] = _mm(mixed, dz1, ta=True, out_dtype=BF16, name="b_w_o")
    g["br_mla"] = _mm(o_mla, d_y_mla, ta=True, out_dtype=BF16, name="b_w_br_mla")
    d_o_mla = _mm(d_y_mla, w["br_mla"], tb=True, out_dtype=BF16, name="b_o_mla")
    d_qc, d_kc, d_v = _attn_bwd(qc, kc, jnp.transpose(kc), vv, o_mla, d_o_mla, lse)
    d_q_full, d_kn, d_kr = _mla_qk_bwd(d_qc, d_kc, pos)
    g["uq"] = _mm(c_q, d_q_full, ta=True, out_dtype=BF16, name="b_w_uq")
    d_c_q = _mm(d_q_full, w["uq"], tb=True, out_dtype=BF16, name="b_c_q")
    g["uk"] = _mm(c_kv, d_kn, ta=True, out_dtype=BF16, name="b_w_uk")
    g["uv"] = _mm(c_kv, d_v, ta=True, out_dtype=BF16, name="b_w_uv")
    d_c_kv = _mm(d_kn, w["uk"], tb=True, name="b_c_kv_k")
    d_c_kv = _mm(d_v, w["uv"], tb=True, add=(d_c_kv,), out_dtype=BF16, name="b_c_kv_v")
    d_cq, d_ckv, g["qnw"], g["kvnw"] = _mla_norm_bwd(proj_b, w["qnw"], w["kvnw"], d_c_q, d_c_kv)
    g["br_dn"] = _mm(og, d_y_dn, ta=True, out_dtype=BF16, name="b_w_br_dn")
    d_y_dn = emit("mix", g, d_y_dn)
    d_o_dn, d_z, g["dnw"] = _gdr_out_bwd(o_dn, proj_a, d_y_dn, w["dnw"], w["br_dn"])
    du, dw, dqd, dkt, d_a, d_egl = _gdr_scan_bwd(u, w_, qd, kt, a_mat, gc, states, d_o_dn)
    dq, dk, dv, d_beta, d_gc = _gdr_prep_bwd(qkvn, beta, gc, gc_t, t_fold, u, w_, du, dw, dqd, dkt, d_a)
    d_egl_rows = jnp.pad(d_egl[:, None, :, 0], ((0, 0), (CHUNK - 1, 0), (0, LANES - N_HEADS))).reshape(s_dim, LANES)
    d_ba, g["alog"], g["dtb"] = _gates_bwd(proj_b, w["alog"], w["dtb"], gc, d_beta, d_gc, d_egl_rows)
    d_qkv, g["conv"] = _conv_bwd(proj_a, w["conv"], dq, dk, dv)
    zeros = jnp.zeros((s_dim, WB_CKV - Q_LORA), BF16)
    d_proj_b = jnp.concatenate([d_cq, zeros, d_ckv, d_kr, d_ba], axis=1)
    g["wa_qkv_t"] = _mm(d_qkv, xb, ta=True, name="b_w_qkv")
    g["wa_z_t"] = _mm(d_z, xb, ta=True, name="b_w_z")
    g["wg_t"] = _mm(d_proj_g, xb, ta=True, name="b_w_g")
    g["wb_t"] = _mm(d_proj_b, xb, ta=True, name="b_w_b")
    dx = _mm(d_qkv, w["wa_qkv_t"], add=(dxa,), name="b_x_qkv")
    dx = _mm(d_z, w["wa_z_t"], add=(dx,), name="b_x_z")
    dx = _mm(d_proj_g, w["wg_t"], add=(dx,), name="b_x_g")
    dx = _mm(d_proj_b, w["wb_t"], add=(dx,), name="b_x_b")
    return loss, dx, g


_BIG = (("w_in", 1), ("w_uq", 0), ("w_uk", 0), ("w_uv", 0), ("w_br_dn", 0), ("w_br_mla", 0),
        ("w_o", 0), ("w_ffn_in", 1), ("w_ffn_out", 0), ("w_ple", 1), ("w_ple_gate", 0))
_BIG_AXIS = dict(_BIG)
_SMALL = ("ln1_g", "ln1_b", "ln2_g", "ln2_b", "q_norm_w", "kv_norm_w", "dn_norm_w", "dn_a_log", "dn_dt_bias")
_ORDER = ("w_in", "conv_w", "dn_a_log", "dn_dt_bias", "dn_norm_w", "q_norm_w", "w_uq", "kv_norm_w", "w_uk", "w_uv",
          "w_br_dn", "w_br_mla", "w_o", "ln1_g", "ln1_b", "w_ffn_in", "w_ffn_out", "w_ple", "w_ple_gate", "ln2_g",
          "ln2_b")


def _stored_shape(name, shard_shape):
    axis = _BIG_AXIS[name]
    lead = shard_shape[axis]
    return lead, int(np.prod(shard_shape)) // lead


def _to_stored(name, shard):
    return jnp.moveaxis(shard, _BIG_AXIS[name], 0).reshape(_stored_shape(name, shard.shape))


def _from_stored(name, stored, shard_shape):
    axis = _BIG_AXIS[name]
    moved = (shard_shape[axis],) + shard_shape[:axis] + shard_shape[axis + 1:]
    return jnp.moveaxis(stored.reshape(moved), 0, axis)


_W_IN_ROWS = np.cumsum([0, 3072, 1024, 8, 8, Q_LORA, KV_LORA, ROPE, D_MODEL, D_MODEL])


def _first_weights(w_in_t, conv_full, small):
    r = _W_IN_ROWS
    zr = lambda n: jnp.zeros((n, D_MODEL), w_in_t.dtype)
    w = {}
    w["wa_t"] = w_in_t[r[0]:r[2]]
    w["wa_qkv_t"], w["wa_z_t"] = w_in_t[r[0]:r[1]], w_in_t[r[1]:r[2]]
    w["wg_t"] = w_in_t[r[7]:r[9]]
    w["wb_t"] = jnp.concatenate([w_in_t[r[4]:r[5]], zr(WB_CKV - Q_LORA), w_in_t[r[5]:r[7]], zr(LANES - ROPE),
                                 w_in_t[r[2]:r[4]], zr(LANES - 2 * N_HEADS)], axis=0)
    w["conv"] = conv_full
    pad_l = lambda v: jnp.pad(v, ((0, 0), (0, LANES - v.shape[1])))
    w["alog"], w["dtb"] = pad_l(small["dn_a_log"]), pad_l(small["dn_dt_bias"])
    w["dnw"], w["qnw"], w["kvnw"] = small["dn_norm_w"], small["q_norm_w"], small["kv_norm_w"]
    w["ln1g"], w["ln1b"], w["ln2g"], w["ln2b"] = small["ln1_g"], small["ln1_b"], small["ln2_g"], small["ln2_b"]
    return w


def _late_weights(group, fw):
    w = {}
    if group == "mix":
        uq = fw["w_uq"].reshape(Q_LORA, N_HEADS, HEAD + ROPE)
        uq_r = jnp.pad(uq[:, :, HEAD:], ((0, 0), (0, 0), (0, HEAD - ROPE)))
        w["uq"] = jnp.concatenate([uq[:, :, :HEAD].reshape(Q_LORA, -1), uq_r.reshape(Q_LORA, -1)], axis=1)
        w["uk"], w["uv"] = fw["w_uk"], fw["w_uv"]
        w["br_dn"], w["br_mla"], w["wo"] = fw["w_br_dn"], fw["w_br_mla"], fw["w_o"]
    else:
        w["ffn_in_t"], w["ffn_out"] = fw["w_ffn_in"], fw["w_ffn_out"]
        w["ple_t"], w["ple_gate"] = fw["w_ple"], fw["w_ple_gate"]
    return w


_GROUP_GRADS = {"ffn": (("w_ple", "ple_t"), ("w_ple_gate", "ple_gate"), ("w_ffn_out", "ffn_out"),
                        ("w_ffn_in", "ffn_in_t")),
                "mix": (("w_o", "wo"), ("w_br_mla", "br_mla"), ("w_uq", "uq"), ("w_uk", "uk"), ("w_uv", "uv"),
                        ("w_br_dn", "br_dn"))}


def _group_grads(group, g):
    out = {}
    for name, key in _GROUP_GRADS[group]:
        t = g[key]
        if name == "w_uq":
            uq_n = t[:, :D_MODEL].reshape(Q_LORA, N_HEADS, HEAD)
            uq_r = t[:, D_MODEL:].reshape(Q_LORA, N_HEADS, HEAD)[:, :, :ROPE]
            t = jnp.concatenate([uq_n, uq_r], axis=2).reshape(Q_LORA, -1)
        out[name] = t
    return out


def _last_grads(g):
    wb = g["wb_t"]
    w_in = jnp.concatenate([
        g["wa_qkv_t"], g["wa_z_t"], wb[WB_BA:WB_BA + 2 * N_HEADS], wb[WB_CQ:WB_CQ + Q_LORA],
        wb[WB_CKV:WB_CKV + KV_LORA], wb[WB_KR:WB_KR + ROPE], g["wg_t"]], axis=0)
    small = {"ln1_g": g["ln1g"], "ln1_b": g["ln1b"], "ln2_g": g["ln2g"], "ln2_b": g["ln2b"], "q_norm_w": g["qnw"],
             "kv_norm_w": g["kvnw"], "dn_norm_w": g["dnw"], "dn_a_log": g["alog"], "dn_dt_bias": g["dtb"],
             "conv_w": g["conv"]}
    return w_in, small


_SMALL_SLOTS = {"ln1_g": (0, 0, 1024), "ln1_b": (1, 0, 1024), "ln2_g": (2, 0, 1024), "ln2_b": (3, 0, 1024),
                "q_norm_w": (4, 0, 384), "kv_norm_w": (4, 384, 256), "dn_norm_w": (4, 640, 128),
                "dn_a_log": (4, 768, 8), "dn_dt_bias": (4, 896, 8)}
_SMALL_ROWS, _LOSS_ROW, _CONV_ROW0, _CONV_ROWS = 24, 5, 8, 12


def _pack_small_grads(small_g, loss):
    zeros = lambda r, c: jnp.zeros((r, c), F32)
    row4 = jnp.concatenate([small_g["q_norm_w"], small_g["kv_norm_w"], small_g["dn_norm_w"], small_g["dn_a_log"],
                            small_g["dn_dt_bias"]], axis=1)
    row5 = jnp.concatenate([loss, zeros(1, FLAT_COLS - LANES)], axis=1)
    head = jnp.concatenate([small_g["ln1_g"], small_g["ln1_b"], small_g["ln2_g"], small_g["ln2_b"], row4, row5,
                            zeros(2, FLAT_COLS)], axis=0)
    conv = small_g["conv_w"].reshape(_CONV_ROWS, FLAT_COLS)
    return jnp.concatenate([head, conv, zeros(_SMALL_ROWS - _CONV_ROW0 - _CONV_ROWS, FLAT_COLS)], axis=0)


_MESH_ID = pl.DeviceIdType.MESH
_ANY = pl.BlockSpec(memory_space=pl.ANY)


def _all_gather(blocks, name):
    n = len(blocks)

    def body(*refs):
        x_refs, out_refs = refs[:n], refs[n:2 * n]
        send_sems, recv_sems, local_sems = refs[2 * n:]
        x, y, c = lax.axis_index("x"), lax.axis_index("y"), lax.axis_index("c")
        me, sibling = (x, y, c), (x, y, 1 - c)
        chips = [(1 - x, y), (x, 1 - y), (1 - x, 1 - y)]

        def slot(i, px, py, pc):
            return out_refs[i].at[4 * px + 2 * py + pc]

        def copy(i, k, origin, to, src=None):
            return pltpu.make_async_remote_copy(
                src_ref=slot(i, *origin) if src is None else src, dst_ref=slot(i, *origin),
                send_sem=send_sems.at[7 * i + k], recv_sem=recv_sems.at[7 * i + k], device_id=to,
                device_id_type=_MESH_ID)

        mine = [pltpu.make_async_copy(x_refs[i], slot(i, *me), local_sems.at[i]) for i in range(n)]
        first, passed = [], []
        for i in range(n):
            mine[i].start()
            first.append(copy(i, 0, me, sibling, src=x_refs[i]))
            first += [copy(i, 1 + j, me, (*chip, c), src=x_refs[i]) for j, chip in enumerate(chips)]
        for cp in first:
            cp.start()
        for i in range(n):
            for j, chip in enumerate(chips):
                copy(i, 1 + j, (*chip, c), me).wait_recv()
                passed.append(copy(i, 4 + j, (*chip, c), sibling))
                passed[-1].start()
        for i in range(n):
            copy(i, 0, sibling, me).wait_recv()
            for j, chip in enumerate(chips):
                copy(i, 4 + j, (*chip, 1 - c), me).wait_recv()
        for cp in first + passed:
            cp.wait_send()
        for cp in mine:
            cp.wait()

    return pl.pallas_call(
        body,
        out_shape=[jax.ShapeDtypeStruct((N_DEV,) + b.shape, b.dtype) for b in blocks],
        in_specs=[_ANY] * n,
        out_specs=[_ANY] * n,
        scratch_shapes=[pltpu.SemaphoreType.DMA((7 * n,)), pltpu.SemaphoreType.DMA((7 * n,)),
                        pltpu.SemaphoreType.DMA((n,))],
        name=name,
    )(*blocks)


def _exchange_sibling(srcs, name):
    n = len(srcs)

    def body(*refs):
        src_refs, dst_refs = refs[:n], refs[n:2 * n]
        send_sems, recv_sems = refs[2 * n:]
        x, y, c = lax.axis_index("x"), lax.axis_index("y"), lax.axis_index("c")
        copies = [pltpu.make_async_remote_copy(
            src_ref=src_refs[i].at[2 * q + (1 - c)], dst_ref=dst_refs[i].at[q], send_sem=send_sems.at[4 * i + q],
            recv_sem=recv_sems.at[4 * i + q], device_id=(x, y, 1 - c), device_id_type=_MESH_ID)
            for i in range(n) for q in range(4)]
        for cp in copies:
            cp.start()
        for cp in copies:
            cp.wait_recv()
        for cp in copies:
            cp.wait_send()

    return pl.pallas_call(
        body,
        out_shape=[jax.ShapeDtypeStruct((4,) + s.shape[1:], s.dtype) for s in srcs],
        in_specs=[_ANY] * n,
        out_specs=[_ANY] * n,
        scratch_shapes=[pltpu.SemaphoreType.DMA((4 * n,)), pltpu.SemaphoreType.DMA((4 * n,))],
        name=name,
    )(*srcs)


def _exchange_chips(srcs, name):
    n = len(srcs)

    def body(*refs):
        src_refs, dst_refs = refs[:n], refs[n:2 * n]
        send_sems, recv_sems = refs[2 * n:]
        x, y, c = lax.axis_index("x"), lax.axis_index("y"), lax.axis_index("c")
        chips = [(1 - x, y), (x, 1 - y), (1 - x, 1 - y)]
        copies = [pltpu.make_async_remote_copy(
            src_ref=src_refs[i].at[2 * tx + ty], dst_ref=dst_refs[i].at[j], send_sem=send_sems.at[3 * i + j],
            recv_sem=recv_sems.at[3 * i + j], device_id=(tx, ty, c), device_id_type=_MESH_ID)
            for i in range(n) for j, (tx, ty) in enumerate(chips)]
        for cp in copies:
            cp.start()
        for cp in copies:
            cp.wait_recv()
        for cp in copies:
            cp.wait_send()

    return pl.pallas_call(
        body,
        out_shape=[jax.ShapeDtypeStruct((3,) + s.shape[1:], s.dtype) for s in srcs],
        in_specs=[_ANY] * n,
        out_specs=[_ANY] * n,
        scratch_shapes=[pltpu.SemaphoreType.DMA((3 * n,)), pltpu.SemaphoreType.DMA((3 * n,))],
        name=name,
    )(*srcs)


def _col_tile(c):
    return c if c <= 256 else 256


def _chip_sum(src, recv, parity, name):
    _, r, c = src.shape
    tc = _col_tile(c)

    def body(par_ref, a_ref, b_ref, o_ref, ob_ref):
        s = a_ref[...] + b_ref[...]
        o_ref[...] = s
        ob_ref[...] = s.astype(BF16)

    blk = lambda f: pl.BlockSpec((None, r, tc), f)
    return pl.pallas_call(
        body,
        out_shape=[jax.ShapeDtypeStruct((4, r, c), F32), jax.ShapeDtypeStruct((4, r, c), BF16)],
        grid_spec=pltpu.PrefetchScalarGridSpec(
            num_scalar_prefetch=1, grid=(4, c // tc),
            in_specs=[blk(lambda q, j, par: (2 * q + par[0], 0, j)), blk(lambda q, j, par: (q, 0, j))],
            out_specs=[blk(lambda q, j, par: (q, 0, j)), blk(lambda q, j, par: (q, 0, j))]),
        compiler_params=pltpu.CompilerParams(dimension_semantics=("parallel", "parallel")),
        name=name,
    )(parity, src, recv)


def _sum_parts(own, others, chip, name):
    _, r, c = own.shape
    tc = _col_tile(c)

    def body(q_ref, a_ref, b_ref, o_ref):
        o_ref[...] = ((a_ref[...] + b_ref[0].astype(F32)) + b_ref[1].astype(F32)) + b_ref[2].astype(F32)

    return pl.pallas_call(
        body,
        out_shape=jax.ShapeDtypeStruct((r, c), F32),
        grid_spec=pltpu.PrefetchScalarGridSpec(
            num_scalar_prefetch=1, grid=(c // tc,),
            in_specs=[pl.BlockSpec((None, r, tc), lambda j, q: (q[0], 0, j)),
                      pl.BlockSpec((3, r, tc), lambda j, q: (0, 0, j))],
            out_specs=pl.BlockSpec((r, tc), lambda j, q: (0, j))),
        compiler_params=pltpu.CompilerParams(dimension_semantics=("parallel",)),
        name=name,
    )(chip, own, others)


_HBM = pl.BlockSpec(memory_space=pltpu.HBM)
_SEM = pl.BlockSpec(memory_space=pltpu.SEMAPHORE)
_DATAFLOW = pltpu.SideEffectType.DATAFLOW_SIDE_EFFECTING
N_PEERS = N_DEV - 1


def _ring_peer(j):
    me = 4 * lax.axis_index("x") + 2 * lax.axis_index("y") + lax.axis_index("c")
    k = (me + j) % N_DEV
    return me, k, (k // 4, (k // 2) % 2, k % 2)


def _spread_copy(i, j, src_refs, land_refs, send_sems, recv_sems, scatter):
    me, k, peer = _ring_peer(j)
    return pltpu.make_async_remote_copy(
        src_ref=src_refs[i].at[k] if scatter else src_refs[i], dst_ref=land_refs[i].at[me],
        send_sem=send_sems.at[N_PEERS * i + j - 1], recv_sem=recv_sems.at[N_PEERS * i + j - 1], device_id=peer,
        device_id_type=_MESH_ID)


def _spread_start(srcs, carry, scatter, name):
    n = len(srcs)
    lands = [lax.empty(((N_DEV,) + s.shape[-2:]), s.dtype) for s in srcs]

    def body(*refs):
        src_refs, land_refs = refs[:n], refs[n:2 * n]
        send_sems, recv_sems, local_sems = refs[2 * n + 1:2 * n + 4]
        for i in range(n):
            for j in range(1, N_DEV):
                _spread_copy(i, j, src_refs, land_refs, send_sems, recv_sems, scatter).start()
        for i in range(n):
            _own_copy(i, src_refs, land_refs, local_sems, scatter).start()

    hbm = lambda a: pltpu.HBM(a.shape, a.dtype)
    sems = pltpu.SemaphoreType.DMA((N_PEERS * n,))
    pinned = [pltpu.with_memory_space_constraint(a, pltpu.HBM) for a in list(srcs) + lands + [carry]]
    res = pl.pallas_call(
        body, name=name,
        out_shape=(sems, sems, pltpu.SemaphoreType.DMA((n,)), *[hbm(a) for a in pinned]),
        in_specs=[_HBM] * (2 * n + 1),
        out_specs=(_SEM, _SEM, _SEM, *[_HBM] * (2 * n + 1)),
        input_output_aliases={i: 3 + i for i in range(2 * n + 1)},
        compiler_params=pltpu.CompilerParams(has_side_effects=_DATAFLOW),
    )(*pinned)
    return res[:3], list(res[3:3 + n]), list(res[3 + n:3 + 2 * n]), res[3 + 2 * n]


def _own_copy(i, src_refs, land_refs, local_sems, scatter):
    me = _ring_peer(0)[0]
    return pltpu.make_async_copy(src_refs[i].at[me] if scatter else src_refs[i], land_refs[i].at[me],
                                 local_sems.at[i])


def _spread_wait(started, after, scatter, name):
    sems, srcs, lands, _ = started
    n = len(srcs)

    def body(*refs):
        src_refs, land_refs = refs[:n], refs[n:2 * n]
        send_s, recv_s, local_s = refs[2 * n:2 * n + 3]
        for i in range(n):
            for j in range(1, N_DEV):
                cp = _spread_copy(i, j, src_refs, land_refs, send_s, recv_s, scatter)
                cp.wait_send()
                cp.wait_recv()
        for i in range(n):
            _own_copy(i, src_refs, land_refs, local_s, scatter).wait()

    hbm = lambda a: pltpu.HBM(a.shape, a.dtype)
    res = pl.pallas_call(
        body, name=name,
        out_shape=tuple(hbm(a) for a in srcs + lands),
        in_specs=[_HBM] * (2 * n) + [_SEM, _SEM, _SEM, pl.BlockSpec(memory_space=pl.ANY)],
        out_specs=tuple([_HBM] * (2 * n)),
        input_output_aliases={i: i for i in range(2 * n)},
        compiler_params=pltpu.CompilerParams(has_side_effects=_DATAFLOW),
    )(*srcs, *lands, *sems, after)
    return list(res[n:])


def _sum8(landing, name):
    _, r, c = landing.shape
    tc = _col_tile(c)

    def body(a_ref, o_ref):
        tot = a_ref[0].astype(F32)
        for k in range(1, N_DEV):
            tot = tot + a_ref[k].astype(F32)
        o_ref[...] = tot

    return pl.pallas_call(
        body,
        out_shape=jax.ShapeDtypeStruct((r, c), F32),
        grid=(c // tc,),
        in_specs=[pl.BlockSpec((N_DEV, r, tc), lambda j: (0, 0, j))],
        out_specs=pl.BlockSpec((r, tc), lambda j: (0, j)),
        compiler_params=pltpu.CompilerParams(dimension_semantics=("parallel",)),
        name=name,
    )(landing)


def _adamw_math(w, g, m, v):
    m = ADAM_B1 * m + (1.0 - ADAM_B1) * g
    v = ADAM_B2 * v + (1.0 - ADAM_B2) * (g * g)
    m_hat = m / (1.0 - ADAM_B1 ** ADAM_STEP)
    v_hat = v / (1.0 - ADAM_B2 ** ADAM_STEP)
    delta = -ADAM_LR * (m_hat / (jnp.sqrt(v_hat) + ADAM_EPS) + ADAM_WD * w)
    return delta, m, v


def _adamw(w, m, v, g, name):
    r, c = w.shape

    def fn(rows, consts):
        return list(_adamw_math(*rows)), []

    return _rowwise(fn, [w, g, m, v], [], [(c, F32)] * 3, tm=r if r <= 512 else 256, name=name)


def _adamw_sum8(w, m, v, landing, name):
    r, c = w.shape
    tc = _col_tile(c)

    def body(w_ref, m_ref, v_ref, a_ref, g_ref, d_ref, m2_ref, v2_ref):
        g = a_ref[0].astype(F32)
        for k in range(1, N_DEV):
            g = g + a_ref[k].astype(F32)
        delta, m2, v2 = _adamw_math(w_ref[...], g, m_ref[...], v_ref[...])
        g_ref[...] = g
        d_ref[...] = delta
        m2_ref[...] = m2
        v2_ref[...] = v2

    blk = pl.BlockSpec((r, tc), lambda j: (0, j))
    return pl.pallas_call(
        body,
        out_shape=[jax.ShapeDtypeStruct((r, c), F32)] * 4,
        grid=(c // tc,),
        in_specs=[blk, blk, blk, pl.BlockSpec((N_DEV, r, tc), lambda j: (0, 0, j))],
        out_specs=[blk] * 4,
        compiler_params=pltpu.CompilerParams(dimension_semantics=("parallel",)),
        name=name,
    )(w, m, v, landing)


def _adamw_small(gathered, params):
    ns = len(_SMALL)

    def body(*refs):
        g_ref, p_refs, o_refs = refs[0], refs[1:1 + 3 * ns], refs[1 + 3 * ns:]
        tot = g_ref[0]
        for k in range(1, N_DEV):
            tot = tot + g_ref[k]
        for i, name in enumerate(_SMALL):
            row, lane0, lanes = _SMALL_SLOTS[name]
            g = tot[row:row + 1, lane0:lane0 + lanes]
            w_, m_, v_ = (p_refs[3 * i + j][...] for j in range(3))
            delta, m2, v2 = _adamw_math(w_, g, m_, v_)
            for j, val in enumerate((g, delta, m2, v2)):
                o_refs[4 * i + j][...] = val
        o_refs[4 * ns][...] = tot[_LOSS_ROW:_LOSS_ROW + 1, 0:LANES]
        o_refs[4 * ns + 1][...] = tot[_CONV_ROW0:_CONV_ROW0 + _CONV_ROWS, :]

    out_shape = [jax.ShapeDtypeStruct(w.shape, F32) for (w, _, _) in params for _ in range(4)]
    out_shape += [jax.ShapeDtypeStruct((1, LANES), F32), jax.ShapeDtypeStruct((_CONV_ROWS, FLAT_COLS), F32)]
    flat = [a for wmv in params for a in wmv]
    return pl.pallas_call(body, out_shape=out_shape, name="adamw_small")(gathered, *flat)


def kernel(x, p, positions, w_in, conv_w, dn_a_log, dn_dt_bias, dn_norm_w, q_norm_w, w_uq, kv_norm_w, w_uk, w_uv, w_br_dn, w_br_mla, w_o, ln1_g, ln1_b, w_ffn_in, w_ffn_out, w_ple, w_ple_gate, ln2_g, ln2_b, loss_target, m_w_in, m_conv_w, m_dn_a_log, m_dn_dt_bias, m_dn_norm_w, m_q_norm_w, m_w_uq, m_kv_norm_w, m_w_uk, m_w_uv, m_w_br_dn, m_w_br_mla, m_w_o, m_ln1_g, m_ln1_b, m_w_ffn_in, m_w_ffn_out, m_w_ple, m_w_ple_gate, m_ln2_g, m_ln2_b, v_w_in, v_conv_w, v_dn_a_log, v_dn_dt_bias, v_dn_norm_w, v_q_norm_w, v_w_uq, v_kv_norm_w, v_w_uk, v_w_uv, v_w_br_dn, v_w_br_mla, v_w_o, v_ln1_g, v_ln1_b, v_w_ffn_in, v_w_ffn_out, v_w_ple, v_w_ple_gate, v_ln2_g, v_ln2_b):
    args = dict(locals())
    wts = {n: args[n] for n in _ORDER}
    mom1 = {n: args["m_" + n] for n in _ORDER}
    mom2 = {n: args["v_" + n] for n in _ORDER}
    big_names = [n for n, _ in _BIG]
    shard_shapes = {n: wts[n].shape[1:] for n in big_names}
    c_idx = lax.axis_index("c")
    q_idx = 2 * lax.axis_index("x") + lax.axis_index("y")
    parity, chip = c_idx.reshape(1).astype(jnp.int32), q_idx.reshape(1).astype(jnp.int32)

    stored = {n: _to_stored(n, wts[n][0]).astype(BF16) for n in big_names}
    first = _all_gather([stored["w_in"], conv_w[0]], "ag_first")
    group_names = {grp: [n for n, _ in pairs] for grp, pairs in _GROUP_GRADS.items()}
    carry, gathers = first[0], {}
    for grp in ("mix", "ffn"):
        gathers[grp] = _spread_start([stored[n] for n in group_names[grp]], carry, False, "ag_start_" + grp)
        carry = gathers[grp][3]
    conv_full = jnp.moveaxis(first[1], 0, 1).reshape(conv_w.shape[1], -1)
    small_w = {n: wts[n].astype(F32) for n in _SMALL}
    w = _first_weights(carry.reshape(-1, D_MODEL), conv_full, small_w)

    def late_weights(grp, after):
        got = _spread_wait(gathers[grp], after, False, "ag_wait_" + grp)
        return _late_weights(grp, {n: t.reshape(-1, t.shape[-1]) for n, t in zip(group_names[grp], got)})

    started = {}

    def emit(group, g, carry):
        grads = _group_grads(group, g)
        srcs = [grads[n].reshape((N_DEV,) + _stored_shape(n, shard_shapes[n])) for n in grads]
        started[group] = (list(grads), _spread_start(srcs, carry, True, "rs_start_" + group))
        return started[group][1][3]

    s_dim = x.shape[1]
    loss, dx, g = _local_step(x[0], p[0, 0], positions.reshape(s_dim, 1).astype(F32), loss_target[0], w,
                              late_weights, emit)
    g_w_in, small_g = _last_grads(g)

    src = g_w_in.reshape((N_DEV,) + _stored_shape("w_in", shard_shapes["w_in"]))
    from_sibling = _exchange_sibling([src], "rs_sibling")[0]
    own, own_bf = _chip_sum(src, from_sibling, parity, "rs_sum_w_in")
    from_chips = _exchange_chips([own_bf], "rs_chips")[0]

    out_g, out_d, out_m, out_v = {}, {}, {}, {}

    def update(n, grad, shp):
        flat2 = (shp[0], int(np.prod(shp[1:])))
        d, m2, v2 = _adamw(wts[n][0].reshape(flat2), mom1[n][0].reshape(flat2), mom2[n][0].reshape(flat2),
                           grad.reshape(flat2), "adamw_" + n)
        out_g[n], out_d[n], out_m[n], out_v[n] = grad, d.reshape(shp), m2.reshape(shp), v2.reshape(shp)

    total = _sum_parts(own, from_chips, chip, "rs_total_w_in")
    update("w_in", _from_stored("w_in", total, shard_shapes["w_in"]), shard_shapes["w_in"])
    for group, (names, st) in started.items():
        for n, landing in zip(names, _spread_wait(st, dx, True, "rs_wait_" + group)):
            shp = shard_shapes[n]
            if _BIG_AXIS[n] == 0:
                flat2 = _stored_shape(n, shp)
                res = _adamw_sum8(wts[n][0].reshape(flat2), mom1[n][0].reshape(flat2), mom2[n][0].reshape(flat2),
                                  landing, "adamw_" + n)
                out_g[n], out_d[n], out_m[n], out_v[n] = (t.reshape(shp) for t in res)
            else:
                update(n, _from_stored(n, _sum8(landing, "rs_total_" + n), shp), shp)

    g_small = _all_gather([_pack_small_grads(small_g, loss)], "ag_small")[0]
    res = _adamw_small(g_small, [(wts[n], mom1[n], mom2[n]) for n in _SMALL])
    for i, n in enumerate(_SMALL):
        out_g[n], out_d[n], out_m[n], out_v[n] = res[4 * i:4 * i + 4]
    loss_out = res[4 * len(_SMALL)][0, 0]
    conv_shape = conv_w.shape[1:]
    conv_g = lax.dynamic_slice(res[-1].reshape(conv_shape[0], -1), (0, (2 * q_idx + c_idx) * conv_shape[1]),
                               conv_shape)
    update("conv_w", conv_g, conv_shape)

    expand = lambda d, n: d[n] if n in _SMALL else d[n][None]
    return (loss_out, dx[None], *[expand(out_g, n) for n in _ORDER], *[expand(out_d, n) for n in _ORDER],
            *[expand(out_m, n) for n in _ORDER], *[expand(out_v, n) for n in _ORDER])
```

```python
import functools

import numpy as np
import jax
import jax.numpy as jnp
from jax import lax
from jax.experimental import pallas as pl
from jax.experimental.pallas import tpu as pltpu

F32 = jnp.float32
BF16 = jnp.bfloat16

D_MODEL = 1024
N_HEADS = 8
HEAD = 128
CHUNK = 64
GROUP = 256
ROPE = 64
Q_LORA = 384
KV_LORA = 256
FFN_HIDDEN = 2816
PLE_DIM = 256
ROPE_BASE = 10000.0
ALPHA = 2.0 ** 0.25
SCALE = float((HEAD + ROPE) ** -0.5)
NEG_BIG = -1e30
EPS_RMS = 1e-6
EPS_LN = 1e-5

ADAM_LR = 0.001
ADAM_B1 = 0.9
ADAM_B2 = 0.999
ADAM_EPS = 1e-08
ADAM_WD = 0.01
ADAM_STEP = 10

N_DEV = 8
LANES = 128
FLAT_COLS = 1024

WB_CQ, WB_CKV, WB_KR, WB_BA, WB_COLS = 0, 512, 768, 896, 1024

HIGHEST = lax.Precision.HIGHEST

NN = (((1,), (0,)), ((), ()))
TN = (((0,), (0,)), ((), ()))
NT = (((1,), (1,)), ((), ()))


def _dot(a, b, dims=NN):
    return lax.dot_general(a.astype(BF16), b.astype(BF16), dims, preferred_element_type=F32)


def _dot32(a, b, dims=NN):
    return lax.dot_general(a, b, dims, precision=HIGHEST, preferred_element_type=F32)


def _sig(x):
    return 1.0 / (1.0 + jnp.exp(-x))


MM_TILE = 1536


def _pick_wide(n):
    if n <= MM_TILE:
        return n
    return max(t for t in range(LANES, MM_TILE + 1, LANES) if n % t == 0)


def _split_bf16(a):
    hi = a.astype(BF16)
    return hi, (a - hi.astype(F32)).astype(BF16)


def _dot3(a, b, dims=NN):
    ah, al = a if isinstance(a, tuple) else _split_bf16(a)
    bh, bl = b if isinstance(b, tuple) else _split_bf16(b)
    d = lambda p, q: lax.dot_general(p, q, dims, preferred_element_type=F32)
    return d(ah, bh) + (d(ah, bl) + d(al, bh))


def _mm(a, b, *, ta=False, tb=False, add=(), out_dtype=F32, name):
    if ta:
        k_dim, m_dim = a.shape
    else:
        m_dim, k_dim = a.shape
    if tb:
        n_dim, k2 = b.shape
    else:
        k2, n_dim = b.shape
    assert k_dim == k2, (a.shape, b.shape, ta, tb)
    tm = _pick_wide(m_dim)
    tn = _pick_wide(n_dim)
    tk = _pick_wide(k_dim)
    nk = k_dim // tk
    n_add = len(add)
    dims = TN if ta else (NT if tb else NN)
    assert not (ta and tb)

    def body(a_ref, b_ref, *rest):
        add_refs = rest[:n_add]
        o_ref = rest[n_add]
        acc = rest[n_add + 1]
        k = pl.program_id(2)

        @pl.when(k == 0)
        def _():
            acc[...] = jnp.zeros_like(acc)

        acc[...] += _dot(a_ref[...], b_ref[...], dims)

        @pl.when(k == nk - 1)
        def _():
            r = acc[...]
            for ar in add_refs:
                r = r + ar[...].astype(F32)
            o_ref[...] = r.astype(o_ref.dtype)

    a_spec = pl.BlockSpec((tk, tm), lambda i, j, k: (k, i)) if ta else pl.BlockSpec((tm, tk), lambda i, j, k: (i, k))
    b_spec = pl.BlockSpec((tn, tk), lambda i, j, k: (j, k)) if tb else pl.BlockSpec((tk, tn), lambda i, j, k: (k, j))
    o_spec = pl.BlockSpec((tm, tn), lambda i, j, k: (i, j))
    return pl.pallas_call(
        body,
        out_shape=jax.ShapeDtypeStruct((m_dim, n_dim), out_dtype),
        grid=(m_dim // tm, n_dim // tn, nk),
        in_specs=[a_spec, b_spec] + [o_spec] * n_add,
        out_specs=o_spec,
        scratch_shapes=[pltpu.VMEM((tm, tn), F32)],
        compiler_params=pltpu.CompilerParams(dimension_semantics=("parallel", "parallel", "arbitrary")),
        name=name,
    )(a, b, *add)


def _rowwise(fn, rows, consts, outs, accs=(), *, tm=256, name):
    rows = [r if isinstance(r, tuple) else (r, 0, r.shape[1]) for r in rows]
    s_dim = rows[0][0].shape[0]
    tm = min(tm, s_dim)
    assert s_dim % tm == 0 and all(arr.shape[0] == s_dim for arr, _, _ in rows)
    specs = [pl.BlockSpec((tm, width), functools.partial(lambda i, cb: (i, cb), cb=cb)) for _, cb, width in rows]
    args = [arr for arr, _, _ in rows]
    for c in consts:
        specs.append(pl.BlockSpec(c.shape, lambda i: (0, 0)))
        args.append(c)
    nr, nc, no = len(rows), len(consts), len(outs)
    out_shape = [jax.ShapeDtypeStruct((s_dim, w), dt) for (w, dt) in outs]
    out_specs = [pl.BlockSpec((tm, w), lambda i: (i, 0)) for (w, dt) in outs]
    out_shape += [jax.ShapeDtypeStruct(sh, F32) for sh in accs]
    out_specs += [pl.BlockSpec(sh, lambda i: (0, 0)) for sh in accs]

    def body(*refs):
        r = [x[...].astype(F32) if x.dtype == BF16 else x[...] for x in refs[:nr]]
        c = [x[...] for x in refs[nr:nr + nc]]
        o_refs = refs[nr + nc:nr + nc + no]
        a_refs = refs[nr + nc + no:]
        o_vals, a_vals = fn(r, c)
        for ref, v in zip(o_refs, o_vals, strict=True):
            ref[...] = v.astype(ref.dtype)
        if a_refs:
            @pl.when(pl.program_id(0) == 0)
            def _():
                for ref in a_refs:
                    ref[...] = jnp.zeros_like(ref)

            for ref, v in zip(a_refs, a_vals, strict=True):
                ref[...] += v

    res = pl.pallas_call(
        body,
        out_shape=out_shape,
        grid=(s_dim // tm,),
        in_specs=specs,
        out_specs=out_specs,
        compiler_params=pltpu.CompilerParams(dimension_semantics=("arbitrary" if accs else "parallel",)),
        name=name,
    )(*args)
    return res


def _colsum(v):
    return jnp.sum(v, axis=0, keepdims=True)


def _rowsum(v):
    return jnp.sum(v, axis=1, keepdims=True)


def _rowmean(v):
    return jnp.mean(v, axis=1, keepdims=True)


def _silu_grad(x):
    s = _sig(x)
    return s * (1.0 + x * (1.0 - s))


def _conv_taps(x, w, width=4):
    row = lax.broadcasted_iota(jnp.int32, x.shape, 0)
    c = x * w[width - 1:width, :]
    for s in range(1, width):
        c = c + jnp.where(row >= s, pltpu.roll(x, s, 0), 0.0) * w[width - 1 - s:width - s, :]
    return c


def _conv_fwd(proj_a, conv_w):
    s_dim = proj_a.shape[0]
    n_blk = 3 * N_HEADS

    def body(x_ref, w_ref, o_ref):
        j = pl.program_id(0)
        c = _conv_taps(x_ref[...], w_ref[...])
        y = c * _sig(c)
        r = lax.rsqrt(_rowsum(y * y) + EPS_RMS)
        fac = jnp.where(j < N_HEADS, r * (HEAD ** -0.5), jnp.where(j < 2 * N_HEADS, r, 1.0))
        o_ref[...] = y * fac

    return pl.pallas_call(
        body,
        out_shape=jax.ShapeDtypeStruct((s_dim, n_blk * HEAD), F32),
        grid=(n_blk,),
        in_specs=[pl.BlockSpec((s_dim, HEAD), lambda j: (0, j)), pl.BlockSpec((4, HEAD), lambda j: (0, j))],
        out_specs=pl.BlockSpec((s_dim, HEAD), lambda j: (0, j)),
        compiler_params=pltpu.CompilerParams(dimension_semantics=("parallel",)),
        name="conv_fwd",
    )(proj_a, conv_w)


def _conv_bwd(proj_a, conv_w, dq, dk, dv):
    s_dim = proj_a.shape[0]
    n_blk = 3 * N_HEADS

    def body(x_ref, w_ref, dq_ref, dk_ref, dv_ref, dx_ref, dw_ref):
        j = pl.program_id(0)
        x = x_ref[...]
        w = w_ref[...]
        do = jnp.where(j < N_HEADS, dq_ref[...], jnp.where(j < 2 * N_HEADS, dk_ref[...], dv_ref[...]))
        c = _conv_taps(x, w)
        sg = _sig(c)
        y = c * sg
        r = lax.rsqrt(_rowsum(y * y) + EPS_RMS)
        sc = jnp.where(j < N_HEADS, HEAD ** -0.5, 1.0)
        dy_n = sc * (r * do - y * (r * r * r) * _rowsum(do * y))
        dy = jnp.where(j < 2 * N_HEADS, dy_n, do)
        dc = dy * (sg * (1.0 + c * (1.0 - sg)))
        row = lax.broadcasted_iota(jnp.int32, x.shape, 0)
        dx = dc * w[3:4, :]
        dw_ref[3:4, :] = _colsum(dc * x)
        for s in range(1, 4):
            dx = dx + jnp.where(row < s_dim - s, pltpu.roll(dc, s_dim - s, 0), 0.0) * w[3 - s:4 - s, :]
            xs = jnp.where(row >= s, pltpu.roll(x, s, 0), 0.0)
            dw_ref[3 - s:4 - s, :] = _colsum(dc * xs)
        dx_ref[...] = dx.astype(dx_ref.dtype)

    hd = N_HEADS - 1
    return pl.pallas_call(
        body,
        out_shape=[jax.ShapeDtypeStruct((s_dim, n_blk * HEAD), BF16), jax.ShapeDtypeStruct((4, n_blk * HEAD), F32)],
        grid=(n_blk,),
        in_specs=[
            pl.BlockSpec((s_dim, HEAD), lambda j: (0, j)),
            pl.BlockSpec((4, HEAD), lambda j: (0, j)),
            pl.BlockSpec((s_dim, HEAD), lambda j: (0, jnp.minimum(j, hd))),
            pl.BlockSpec((s_dim, HEAD), lambda j: (0, jnp.clip(j - N_HEADS, 0, hd))),
            pl.BlockSpec((s_dim, HEAD), lambda j: (0, jnp.clip(j - 2 * N_HEADS, 0, hd))),
        ],
        out_specs=[pl.BlockSpec((s_dim, HEAD), lambda j: (0, j)), pl.BlockSpec((4, HEAD), lambda j: (0, j))],
        compiler_params=pltpu.CompilerParams(dimension_semantics=("parallel",)),
        name="conv_bwd",
    )(proj_a, conv_w, dq, dk, dv)


def _chunk_tri(n):
    r = np.arange(n)
    m = ((r[:, None] // CHUNK) == (r[None, :] // CHUNK)) & (r[:, None] >= r[None, :])
    m = m.astype(np.float32)
    return jnp.asarray(m), jnp.asarray(m.T)


def _softplus(z):
    return jnp.maximum(z, 0.0) + jnp.log(1.0 + jnp.exp(-jnp.abs(z)))


def _gates_fwd(proj_b, alog, dtb):
    tm = min(GROUP, proj_b.shape[0])
    tri, _ = _chunk_tri(tm)

    def fn(r, c):
        b = r[0]
        a = pltpu.roll(b, LANES - N_HEADS, 1)
        alog_, dtb_, tri_ = c
        g = -jnp.exp(alog_) * _softplus(a + dtb_)
        return [_sig(b), _dot32(tri_, g)], []

    return _rowwise(fn, [(proj_b, WB_BA // LANES, LANES)], [alog, dtb, tri],
                    [(LANES, F32), (LANES, F32)], tm=tm, name="gates_fwd")


def _gates_bwd(proj_b, alog, dtb, gc, d_beta, d_gc, d_egl_rows):
    tm = min(GROUP, proj_b.shape[0])
    _, tri_t = _chunk_tri(tm)

    def fn(r, c):
        b, gc_, d_beta_, d_gc_, d_egl_ = r
        a = pltpu.roll(b, LANES - N_HEADS, 1)
        alog_, dtb_, tri_t_ = c
        z = a + dtb_
        ea = jnp.exp(alog_)
        g = -ea * _softplus(z)
        dg = _dot32(tri_t_, d_gc_ + d_egl_ * jnp.exp(gc_))
        d_a = dg * (-ea) * _sig(z)
        beta = _sig(b)
        d_ba = d_beta_ * beta * (1.0 - beta) + pltpu.roll(d_a, N_HEADS, 1)
        return [d_ba], [_colsum(dg * g), _colsum(d_a)]

    return _rowwise(fn, [(proj_b, WB_BA // LANES, LANES), gc, d_beta, d_gc, d_egl_rows],
                    [alog, dtb, tri_t], [(LANES, BF16)], accs=[(1, LANES), (1, LANES)], tm=tm,
                    name="gates_bwd")


def _group_masks(n):
    r = lax.broadcasted_iota(jnp.int32, (n, n), 0)
    c = lax.broadcasted_iota(jnp.int32, (n, n), 1)
    same = (r // CHUNK) == (c // CHUNK)
    below, s = [], 2
    while s < CHUNK:
        below.append(jnp.logical_and((r // (2 * s)) == (c // (2 * s)),
                                     jnp.logical_and((r // s) % 2 == 1, (c // s) % 2 == 0)))
        s *= 2
    return dict(same=same, tril=jnp.logical_and(same, r >= c), strict=jnp.logical_and(same, r > c),
                last=c == (r // CHUNK) * CHUNK + (CHUNK - 1), eye=r == c, pair=(r // 2) == (c // 2), below=below)


def _inv_unit_lower(l_mats, mk):
    eye_f = mk["eye"].astype(F32)
    ts = [eye_f - jnp.where(mk["pair"], l_mat, 0.0) for l_mat in l_mats]
    for below in mk["below"]:
        halves = [_split_bf16(t) for t in ts]
        mids = [_dot3(h, jnp.where(below, l_mat, 0.0)) for h, l_mat in zip(halves, l_mats)]
        ts = [t - _dot3(m, h) for t, m, h in zip(ts, mids, halves)]
    return ts


def _unfold_blocks(folded, mask):
    n = folded.shape[0]
    return jnp.where(mask, jnp.concatenate([folded] * (n // CHUNK), axis=1), 0.0)


def _head_cols(beta, gc, gc_t, h):
    lane = lax.broadcasted_iota(jnp.int32, beta.shape, 1)
    sub = lax.broadcasted_iota(jnp.int32, gc_t.shape, 0)
    bcol = _rowsum(jnp.where(lane == h, beta, 0.0))
    gcol = _rowsum(jnp.where(lane == h, gc, 0.0))
    grow = _colsum(jnp.where(sub == h, gc_t, 0.0))
    return bcol, gcol, grow


def _prep_common(q, k, bcol, gcol, grow, mk, t_folded=None):
    n = q.shape[0]
    tril = mk["tril"]
    decay = jnp.where(tril, jnp.exp(jnp.where(tril, gcol - grow, 0.0)), 0.0)
    glast = _rowsum(jnp.where(mk["last"], jnp.broadcast_to(grow, (n, n)), 0.0))
    e = jnp.exp(gcol)
    ekt = jnp.exp(glast - gcol)
    kb = k * bcol
    kk = _dot(kb, k, NT)
    qk = _dot(q, k, NT)
    p = dict(decay=decay, e=e, ekt=ekt, kb=kb, kk=kk, qk=qk)
    if t_folded is not None:
        p["t"] = _unfold_blocks(t_folded, mk["same"])
    return p


GROUPS_PER_STEP = 4
SCAN_CHUNKS_PER_STEP = 4


def _fold_blocks(m):
    n = m.shape[0]
    out = m[:, 0:CHUNK]
    for b in range(1, n // CHUNK):
        out = out + m[:, b * CHUNK:(b + 1) * CHUNK]
    return out


def _gdr_prep_fwd(qkvn, beta, gc, gc_t):
    s_dim = qkvn.shape[0]
    tg = min(GROUP, s_dim)
    n_sub = min(GROUPS_PER_STEP, s_dim // tg)
    tb = tg * n_sub

    def body(q_ref, k_ref, v_ref, b_ref, g_ref, gt_ref, u_ref, w_ref, qd_ref, kt_ref, a_ref, t_ref):
        h = pl.program_id(0)
        mk = _group_masks(tg)
        parts = []
        for s in range(n_sub):
            rows = slice(s * tg, (s + 1) * tg)
            q, k, v = q_ref[rows, :], k_ref[rows, :], v_ref[rows, :]
            bcol, gcol, grow = _head_cols(b_ref[rows, :], g_ref[rows, :], gt_ref[:, rows], h)
            p = _prep_common(q, k, bcol, gcol, grow, mk)
            qd_ref[rows, :] = q * p["e"]
            kt_ref[rows, :] = k * p["ekt"]
            a_ref[rows, :] = _fold_blocks(jnp.where(mk["tril"], p["qk"] * p["decay"], 0.0))
            parts.append((rows, v * bcol, p["kb"] * p["e"], jnp.where(mk["strict"], p["kk"] * p["decay"], 0.0)))
        t_mats = _inv_unit_lower([part[3] for part in parts], mk)
        for (rows, vb, kbe, _), t_mat in zip(parts, t_mats):
            u_ref[rows, :] = _dot(t_mat, vb)
            w_ref[rows, :] = _dot(t_mat, kbe)
            t_ref[rows, :] = _fold_blocks(t_mat)

    row = lambda off: pl.BlockSpec((tb, HEAD), functools.partial(lambda h, m, off: (m, h + off), off=off))
    full = pl.BlockSpec((tb, LANES), lambda h, m: (m, 0))
    o_spec = pl.BlockSpec((tb, HEAD), lambda h, m: (m, h))
    a_spec = pl.BlockSpec((None, tb, CHUNK), lambda h, m: (h, m, 0))
    wide = jax.ShapeDtypeStruct((s_dim, N_HEADS * HEAD), F32)
    folded = jax.ShapeDtypeStruct((N_HEADS, s_dim, CHUNK), F32)
    return pl.pallas_call(
        body,
        out_shape=[wide, wide, wide, wide, folded, folded],
        grid=(N_HEADS, s_dim // tb),
        in_specs=[row(0), row(N_HEADS), row(2 * N_HEADS), full, full, pl.BlockSpec((8, tb), lambda h, m: (0, m))],
        out_specs=[o_spec, o_spec, o_spec, o_spec, a_spec, a_spec],
        compiler_params=pltpu.CompilerParams(dimension_semantics=("parallel", "parallel")),
        name="gdr_prep_fwd",
    )(qkvn, qkvn, qkvn, beta, gc, gc_t)


def _gdr_prep_bwd(qkvn, beta, gc, gc_t, t_fold, u, w, du, dw, dqd, dkt, d_a):
    s_dim = qkvn.shape[0]
    tg = min(GROUP, s_dim)
    n_sub = min(GROUPS_PER_STEP, s_dim // tg)
    tb = tg * n_sub

    def body(q_ref, k_ref, v_ref, b_ref, g_ref, gt_ref, t_ref, u_ref, w_ref, du_ref, dw_ref, dqd_ref, dkt_ref,
             da_ref, dq_ref, dk_ref, dv_ref, db_ref, dg_ref):
        h = pl.program_id(1)

        @pl.when(h == 0)
        def _():
            db_ref[...] = jnp.zeros_like(db_ref)
            dg_ref[...] = jnp.zeros_like(dg_ref)

        mk = _group_masks(tg)
        lane = lax.broadcasted_iota(jnp.int32, (tg, LANES), 1)
        for s in range(n_sub):
            rows = slice(s * tg, (s + 1) * tg)
            q, k, v = q_ref[rows, :], k_ref[rows, :], v_ref[rows, :]
            bcol, gcol, grow = _head_cols(b_ref[rows, :], g_ref[rows, :], gt_ref[:, rows], h)
            p = _prep_common(q, k, bcol, gcol, grow, mk, t_ref[rows, :])
            t_mat, decay, e, ekt, kb = p["t"], p["decay"], p["e"], p["ekt"], p["kb"]
            du_, dw_, dqd_, dkt_ = du_ref[rows, :], dw_ref[rows, :], dqd_ref[rows, :], dkt_ref[rows, :]
            dvb = _dot(t_mat, du_, TN)
            dkbe = _dot(t_mat, dw_, TN)
            d_l = -(_dot(dvb, u_ref[rows, :], NT) + _dot(dkbe, w_ref[rows, :], NT))
            m1 = jnp.where(mk["strict"], d_l, 0.0)
            m2 = _unfold_blocks(da_ref[rows, :], mk["tril"])
            d_kk = m1 * decay
            d_qk = m2 * decay
            d_decay = m1 * p["kk"] + m2 * p["qk"]
            dkb = _dot(d_kk, k) + dkbe * e
            dk = _dot(d_kk, kb, TN) + _dot(d_qk, q, TN) + dkt_ * ekt + dkb * bcol
            dq = _dot(d_qk, k) + dqd_ * e
            d_beta = _rowsum(dkb * k) + _rowsum(dvb * v)
            d_e = _rowsum(dkbe * kb) + _rowsum(dqd_ * q)
            d_ekt = _rowsum(dkt_ * k) * ekt
            d_diff = d_decay * decay
            d_grow = -_colsum(d_diff) + _colsum(jnp.where(mk["last"], jnp.broadcast_to(d_ekt, (tg, tg)), 0.0))
            d_gcol = d_e * e - d_ekt + _rowsum(d_diff)
            d_gcol = d_gcol + _rowsum(jnp.where(mk["eye"], jnp.broadcast_to(d_grow, (tg, tg)), 0.0))
            dq_ref[rows, :] = dq
            dk_ref[rows, :] = dk
            dv_ref[rows, :] = dvb * bcol
            db_ref[rows, :] = jnp.where(lane == h, d_beta, db_ref[rows, :])
            dg_ref[rows, :] = jnp.where(lane == h, d_gcol, dg_ref[rows, :])

    row = lambda off: pl.BlockSpec((tb, HEAD), functools.partial(lambda m, h, off: (m, h + off), off=off))
    full = pl.BlockSpec((tb, LANES), lambda m, h: (m, 0))
    o_spec = pl.BlockSpec((tb, HEAD), lambda m, h: (m, h))
    a_spec = pl.BlockSpec((None, tb, CHUNK), lambda m, h: (h, m, 0))
    wide = jax.ShapeDtypeStruct((s_dim, N_HEADS * HEAD), F32)
    lanes = jax.ShapeDtypeStruct((s_dim, LANES), F32)
    return pl.pallas_call(
        body,
        out_shape=[wide, wide, wide, lanes, lanes],
        grid=(s_dim // tb, N_HEADS),
        in_specs=[row(0), row(N_HEADS), row(2 * N_HEADS), full, full, pl.BlockSpec((8, tb), lambda m, h: (0, m)),
                  a_spec, o_spec, o_spec, o_spec, o_spec, o_spec, o_spec, a_spec],
        out_specs=[o_spec, o_spec, o_spec, full, full],
        compiler_params=pltpu.CompilerParams(dimension_semantics=("parallel", "arbitrary")),
        name="gdr_prep_bwd",
    )(qkvn, qkvn, qkvn, beta, gc, gc_t, t_fold, u, w, du, dw, dqd, dkt, d_a)


def _gdr_scan_fwd(u, w, qd, kt, a_mat, gc):
    s_dim = u.shape[0]
    n_chunks = s_dim // CHUNK
    per = min(SCAN_CHUNKS_PER_STEP, n_chunks)
    tb = per * CHUNK

    def body(u_ref, w_ref, qd_ref, kt_ref, a_ref, g_ref, o_ref, st_ref, state):
        @pl.when(pl.program_id(0) == 0)
        def _():
            state[...] = jnp.zeros_like(state)

        heads = range(N_HEADS)
        cols = [slice(h * HEAD, (h + 1) * HEAD) for h in heads]
        for i in range(per):
            rows = slice(i * CHUNK, (i + 1) * CHUNK)
            egl = jnp.exp(g_ref[(i + 1) * CHUNK - 1:(i + 1) * CHUNK, :])
            s_b = [state[h].astype(BF16) for h in heads]
            for h in heads:
                st_ref[i, h] = state[h]
            ws = [_dot(w_ref[rows, cs], s) for cs, s in zip(cols, s_b)]
            qs = [_dot(qd_ref[rows, cs], s) for cs, s in zip(cols, s_b)]
            vns = [(u_ref[rows, cs] - ws_h).astype(BF16) for cs, ws_h in zip(cols, ws)]
            avs = [_dot(a_ref[h, rows, :], vn) for h, vn in zip(heads, vns)]
            kvs = [_dot(kt_ref[rows, cs], vn, TN) for cs, vn in zip(cols, vns)]
            for h, cs in zip(heads, cols):
                o_ref[rows, cs] = qs[h] + avs[h]
                state[h] = state[h] * egl[:, h:h + 1] + kvs[h]

    wide = pl.BlockSpec((tb, N_HEADS * HEAD), lambda n: (n, 0))
    return pl.pallas_call(
        body,
        out_shape=[jax.ShapeDtypeStruct((s_dim, N_HEADS * HEAD), F32),
                   jax.ShapeDtypeStruct((n_chunks, N_HEADS, HEAD, HEAD), F32)],
        grid=(n_chunks // per,),
        in_specs=[wide, wide, wide, wide, pl.BlockSpec((N_HEADS, tb, CHUNK), lambda n: (0, n, 0)),
                  pl.BlockSpec((tb, LANES), lambda n: (n, 0))],
        out_specs=[wide, pl.BlockSpec((per, N_HEADS, HEAD, HEAD), lambda n: (n, 0, 0, 0))],
        scratch_shapes=[pltpu.VMEM((N_HEADS, HEAD, HEAD), F32)],
        compiler_params=pltpu.CompilerParams(dimension_semantics=("arbitrary",)),
        name="gdr_scan_fwd",
    )(u, w, qd, kt, a_mat, gc)


def _gdr_scan_bwd(u, w, qd, kt, a_mat, gc, states, d_o):
    s_dim = u.shape[0]
    n_chunks = s_dim // CHUNK
    per = min(SCAN_CHUNKS_PER_STEP, n_chunks)
    tb = per * CHUNK
    last = n_chunks // per - 1

    def body(u_ref, w_ref, qd_ref, kt_ref, a_ref, g_ref, st_ref, do_ref,
             du_ref, dw_ref, dqd_ref, dkt_ref, da_ref, de_ref, d_state):
        @pl.when(pl.program_id(0) == 0)
        def _():
            d_state[...] = jnp.zeros_like(d_state)

        heads = range(N_HEADS)
        cols = [slice(h * HEAD, (h + 1) * HEAD) for h in heads]
        for i in reversed(range(per)):
            rows = slice(i * CHUNK, (i + 1) * CHUNK)
            egl = jnp.exp(g_ref[(i + 1) * CHUNK - 1:(i + 1) * CHUNK, :])
            s_b = [st_ref[i, h].astype(BF16) for h in heads]
            ds_b = [d_state[h].astype(BF16) for h in heads]
            dos = [do_ref[rows, cs].astype(BF16) for cs in cols]
            w_b = [w_ref[rows, cs].astype(BF16) for cs in cols]
            ws = [_dot(w_h, s) for w_h, s in zip(w_b, s_b)]
            ados = [_dot(a_ref[h, rows, :], do, TN) for h, do in zip(heads, dos)]
            kds = [_dot(kt_ref[rows, cs], ds) for cs, ds in zip(cols, ds_b)]
            dqds = [_dot(do, s, NT) for do, s in zip(dos, s_b)]
            qdos = [_dot(qd_ref[rows, cs], do, TN) for cs, do in zip(cols, dos)]
            vns = [(u_ref[rows, cs] - ws_h).astype(BF16) for cs, ws_h in zip(cols, ws)]
            dvns = [a + k_ for a, k_ in zip(ados, kds)]
            dvn_b = [d.astype(BF16) for d in dvns]
            das = [_dot(do, vn, NT) for do, vn in zip(dos, vns)]
            dkts = [_dot(vn, ds, NT) for vn, ds in zip(vns, ds_b)]
            dws = [_dot(d, s, NT) for d, s in zip(dvn_b, s_b)]
            wds = [_dot(w_h, d, TN) for w_h, d in zip(w_b, dvn_b)]
            for h, cs in zip(heads, cols):
                ds_n = d_state[h]
                de = jnp.sum(_rowsum(ds_n * st_ref[i, h]), axis=0, keepdims=True)
                de_ref[i, h:h + 1, :] = jnp.broadcast_to(de, (1, LANES))
                dqd_ref[rows, cs] = dqds[h]
                da_ref[h, rows, :] = das[h]
                dkt_ref[rows, cs] = dkts[h]
                du_ref[rows, cs] = dvns[h]
                dw_ref[rows, cs] = -dws[h]
                d_state[h] = ds_n * egl[:, h:h + 1] + qdos[h] - wds[h]

    wide = pl.BlockSpec((tb, N_HEADS * HEAD), lambda n: (last - n, 0))
    a_spec = pl.BlockSpec((N_HEADS, tb, CHUNK), lambda n: (0, last - n, 0))
    wide_shape = jax.ShapeDtypeStruct((s_dim, N_HEADS * HEAD), F32)
    return pl.pallas_call(
        body,
        out_shape=[wide_shape, wide_shape, wide_shape, wide_shape,
                   jax.ShapeDtypeStruct((N_HEADS, s_dim, CHUNK), F32),
                   jax.ShapeDtypeStruct((n_chunks, N_HEADS, LANES), F32)],
        grid=(n_chunks // per,),
        in_specs=[wide, wide, wide, wide, a_spec, pl.BlockSpec((tb, LANES), lambda n: (last - n, 0)),
                  pl.BlockSpec((per, N_HEADS, HEAD, HEAD), lambda n: (last - n, 0, 0, 0)), wide],
        out_specs=[wide, wide, wide, wide, a_spec, pl.BlockSpec((per, N_HEADS, LANES), lambda n: (last - n, 0, 0))],
        scratch_shapes=[pltpu.VMEM((N_HEADS, HEAD, HEAD), F32)],
        compiler_params=pltpu.CompilerParams(dimension_semantics=("arbitrary",)),
        name="gdr_scan_bwd",
    )(u, w, qd, kt, a_mat, gc, states, d_o)


FUSED_ROWS = 512


def _gdr_out_fwd(o_dn, proj_a, dn_w, w_br):
    def fn(r, c):
        o, z = r
        w_, w_br_ = c
        outs = []
        for h in range(N_HEADS):
            cs = slice(h * HEAD, (h + 1) * HEAD)
            oh, zh = o[:, cs], z[:, cs]
            rr = lax.rsqrt(_rowmean(oh * oh) + EPS_RMS)
            outs.append(oh * rr * w_ * (zh * _sig(zh)))
        og = jnp.concatenate(outs, axis=1).astype(BF16)
        return [og, _dot(og, w_br_)], []

    return _rowwise(fn, [o_dn, (proj_a, 3, D_MODEL)], [dn_w, w_br], [(D_MODEL, BF16), (D_MODEL, BF16)],
                    tm=FUSED_ROWS, name="gdr_out_fwd")


def _gdr_out_bwd(o_dn, proj_a, d_y_dn, dn_w, w_br):
    def fn(r, c):
        o, z, dy = r
        w_, w_br_ = c
        dg = _dot(dy, w_br_, NT)
        d_o, d_z = [], []
        d_w = jnp.zeros((1, HEAD), F32)
        for h in range(N_HEADS):
            cs = slice(h * HEAD, (h + 1) * HEAD)
            oh, zh, dgh = o[:, cs], z[:, cs], dg[:, cs]
            rr = lax.rsqrt(_rowmean(oh * oh) + EPS_RMS)
            sz = zh * _sig(zh)
            d_n = dgh * sz
            d_z.append(dgh * (oh * rr * w_) * _silu_grad(zh))
            d_w = d_w + _colsum(d_n * oh * rr)
            gw = d_n * w_
            d_o.append(rr * gw - oh * (rr * rr * rr) * _rowmean(gw * oh))
        return [jnp.concatenate(d_o, axis=1), jnp.concatenate(d_z, axis=1)], [d_w]

    return _rowwise(fn, [o_dn, (proj_a, 3, D_MODEL), d_y_dn], [dn_w, w_br], [(D_MODEL, F32), (D_MODEL, BF16)],
                    accs=[(1, HEAD)], tm=FUSED_ROWS, name="gdr_out_bwd")


def _rms_fwd(x, w):
    r = lax.rsqrt(_rowmean(x * x) + EPS_RMS)
    return x * r * w


def _rms_bwd(x, w, dy):
    r = lax.rsqrt(_rowmean(x * x) + EPS_RMS)
    gw = dy * w
    return r * gw - x * (r * r * r) * _rowmean(gw * x), _colsum(dy * x * r)


def _rope_consts():
    inv = ROPE_BASE ** (-np.arange(0, ROPE, 2, dtype=np.float32) / ROPE)
    t = np.zeros((4, LANES), np.float32)
    t[0, :32] = inv
    t[0, 32:64] = inv
    t[1, :64] = 1.0
    t[2, 32:64] = 1.0
    t[3, :32] = -1.0
    return jnp.asarray(t)


def _rope_tables(pos, consts, width):
    ang = pos * consts[0:1, :]
    cosv, sinv = jnp.cos(ang), jnp.sin(ang)
    reps = width // LANES
    tile = (lambda t: jnp.concatenate([t] * reps, axis=1)) if reps > 1 else (lambda t: t)
    return tile(cosv * consts[1:2, :]), tile(sinv * consts[2:3, :]), tile(sinv * consts[3:4, :])


def _rope_apply(t, tabs):
    cos_t, sin_a, sin_b = tabs
    width = t.shape[1]
    return t * cos_t + pltpu.roll(t, 32, 1) * sin_a + pltpu.roll(t, width - 32, 1) * sin_b


def _rope_transpose(d, tabs):
    cos_t, sin_a, sin_b = tabs
    width = d.shape[1]
    return d * cos_t + pltpu.roll(d * sin_a, width - 32, 1) + pltpu.roll(d * sin_b, 32, 1)


QK_HEAD = 2 * HEAD


def _interleave_heads(a, b):
    parts = []
    for h in range(N_HEADS):
        parts.append(a[:, h * HEAD:(h + 1) * HEAD])
        parts.append(b if b.shape[1] == LANES else b[:, h * LANES:(h + 1) * LANES])
    return jnp.concatenate(parts, axis=1)


def _mla_rows(proj_b):
    return [(proj_b, WB_CQ // Q_LORA, Q_LORA), (proj_b, WB_CKV // KV_LORA, KV_LORA), (proj_b, WB_KR // LANES, LANES)]


def _mla_prep_fwd(proj_b, pos, qn_w, kvn_w, uq, uk, uv):
    def fn(r, c):
        cq, ckv, kr, pos_ = r
        qn_w_, kvn_w_, uq_, uk_, uv_, rope = c
        c_q = _rms_fwd(cq, qn_w_).astype(BF16)
        c_kv = _rms_fwd(ckv, kvn_w_).astype(BF16)
        qf = _dot(c_q, uq_)
        qr = _rope_apply(qf[:, D_MODEL:], _rope_tables(pos_, rope, D_MODEL))
        kr = _rope_apply(kr, _rope_tables(pos_, rope, LANES))
        return [c_q, c_kv, _interleave_heads(qf[:, :D_MODEL], qr) * SCALE, _interleave_heads(_dot(c_kv, uk_), kr),
                _dot(c_kv, uv_)], []

    return _rowwise(fn, _mla_rows(proj_b) + [pos], [qn_w, kvn_w, uq, uk, uv, _rope_consts()],
                    [(Q_LORA, BF16), (KV_LORA, BF16), (N_HEADS * QK_HEAD, BF16), (N_HEADS * QK_HEAD, BF16),
                     (D_MODEL, BF16)], tm=FUSED_ROWS, name="mla_prep_fwd")


def _mla_prep_bwd(proj_b, pos, d_qc, d_kc, d_v, qn_w, kvn_w, uq, uk, uv):
    def fn(r, c):
        cq, ckv, _, pos_, dq, dk, dv = r
        qn_w_, kvn_w_, uq_, uk_, uv_, rope = c
        even = lambda t: jnp.concatenate([t[:, (2 * h) * LANES:(2 * h + 1) * LANES] for h in range(N_HEADS)], axis=1)
        odd = lambda t: jnp.concatenate([t[:, (2 * h + 1) * LANES:(2 * h + 2) * LANES] for h in range(N_HEADS)], axis=1)
        d_qr_raw = _rope_transpose(odd(dq), _rope_tables(pos_, rope, D_MODEL)) * SCALE
        d_qf = jnp.concatenate([even(dq) * SCALE, d_qr_raw], axis=1).astype(BF16)
        d_kn = even(dk).astype(BF16)
        dkr = dk[:, LANES:2 * LANES]
        for h in range(1, N_HEADS):
            dkr = dkr + dk[:, (2 * h + 1) * LANES:(2 * h + 2) * LANES]
        d_cq, d_qnw = _rms_bwd(cq, qn_w_, _dot(d_qf, uq_, NT))
        d_ckv, d_kvnw = _rms_bwd(ckv, kvn_w_, _dot(d_kn, uk_, NT) + _dot(dv, uv_, NT))
        return [d_qf, d_kn, d_cq, d_ckv, _rope_transpose(dkr, _rope_tables(pos_, rope, LANES))], [d_qnw, d_kvnw]

    return _rowwise(fn, _mla_rows(proj_b) + [pos, d_qc, d_kc, d_v], [qn_w, kvn_w, uq, uk, uv, _rope_consts()],
                    [(2 * D_MODEL, BF16), (D_MODEL, BF16), (Q_LORA, BF16), (KV_LORA, BF16), (LANES, BF16)],
                    accs=[(1, Q_LORA), (1, KV_LORA)], tm=FUSED_ROWS, name="mla_prep_bwd")


def _causal_mask_t(st, key0, query0):
    key = lax.broadcasted_iota(jnp.int32, st.shape, 0) + key0
    query = lax.broadcasted_iota(jnp.int32, st.shape, 1) + query0
    return jnp.where(key <= query, st, NEG_BIG)


def _attn_tiles(s_dim):
    tq = min(512, s_dim)
    n_chains = 2 if s_dim >= 2 * tq else 1
    return tq, n_chains, min(512, s_dim)


def _diagonal_chains(t, tq, n_chains, tk):
    return [(c, (t + 1) * tk - 1 > c * tq) for c in range(n_chains) if t * tk < (c + 1) * tq]


def _attn_fwd(qc, kc, vt):
    s_dim = qc.shape[0]
    tq, n_chains, tk = _attn_tiles(s_dim)
    tqs = tq * n_chains

    def body(q_ref, k_ref, vt_ref, o_ref, lse_ref, m_s, l_s, acc):
        qi = pl.program_id(1)
        m_s[...] = jnp.full_like(m_s, NEG_BIG)
        l_s[...] = jnp.zeros_like(l_s)
        acc[...] = jnp.zeros_like(acc)

        def make_step(chains):
            def step(j, carry):
                ks = pl.multiple_of(j * tk, tk)
                kb, vtb = k_ref[pl.ds(ks, tk), :], vt_ref[:, pl.ds(ks, tk)]
                cols = [slice(c * tq, (c + 1) * tq) for c, _ in chains]
                sts = [_dot(kb, q_ref[cs, :], NT) for cs in cols]
                sts = [_causal_mask_t(st, j * tk, qi * tqs + c * tq) if masked else st
                       for st, (c, masked) in zip(sts, chains)]
                m_prevs = [m_s[:, cs] for cs in cols]
                m_news = [jnp.maximum(mp, jnp.max(st, axis=0, keepdims=True)) for mp, st in zip(m_prevs, sts)]
                alphas = [jnp.exp(mp - mn) for mp, mn in zip(m_prevs, m_news)]
                pts = [jnp.exp(st - mn) for st, mn in zip(sts, m_news)]
                pvs = [_dot(vtb, pt) for pt in pts]
                for cs, mn, al, pt, pv in zip(cols, m_news, alphas, pts, pvs):
                    l_s[:, cs] = al * l_s[:, cs] + _colsum(pt)
                    m_s[:, cs] = mn
                    acc[:, cs] = acc[:, cs] * al + pv
                return carry
            return step

        below = qi * (tqs // tk)
        lax.fori_loop(0, below, make_step([(c, False) for c in range(n_chains)]), 0)
        for t in range(tqs // tk):
            make_step(_diagonal_chains(t, tq, n_chains, tk))(below + t, 0)
        l = l_s[...]
        o_ref[...] = jnp.transpose(acc[...] / l)
        lse_ref[...] = m_s[...] + jnp.log(l)

    return pl.pallas_call(
        body,
        out_shape=[jax.ShapeDtypeStruct((s_dim, N_HEADS * HEAD), F32), jax.ShapeDtypeStruct((N_HEADS, 1, s_dim), F32)],
        grid=(N_HEADS, s_dim // tqs),
        in_specs=[pl.BlockSpec((tqs, QK_HEAD), lambda h, qi: (qi, h)),
                  pl.BlockSpec((s_dim, QK_HEAD), lambda h, qi: (0, h)),
                  pl.BlockSpec((HEAD, s_dim), lambda h, qi: (h, 0))],
        out_specs=[pl.BlockSpec((tqs, HEAD), lambda h, qi: (qi, h)),
                   pl.BlockSpec((None, 1, tqs), lambda h, qi: (h, 0, qi))],
        scratch_shapes=[pltpu.VMEM((1, tqs), F32), pltpu.VMEM((1, tqs), F32), pltpu.VMEM((HEAD, tqs), F32)],
        compiler_params=pltpu.CompilerParams(dimension_semantics=("parallel", "parallel")),
        name="attn_fwd",
    )(qc, kc, vt)


def _attn_bwd(qc, kc, kct, v, o, d_o, lse):
    s_dim = qc.shape[0]
    tq, n_chains, tk = _attn_tiles(s_dim)
    tqs = tq * n_chains

    def body(q_ref, k_ref, kt_ref, v_ref, o_ref, do_ref, lse_ref, dq_ref, dk_ref, dv_ref, dqt_acc, dv_acc):
        qi = pl.program_id(1)

        @pl.when(qi == 0)
        def _():
            dk_ref[...] = jnp.zeros_like(dk_ref)
            dv_acc[...] = jnp.zeros_like(dv_acc)

        dqt_acc[...] = jnp.zeros_like(dqt_acc)
        do_f = do_ref[...]
        do_all = do_f.astype(BF16)
        q_all = q_ref[...]
        lse_row = lse_ref[...]
        delta_row = _dot3(jnp.ones((8, HEAD), F32), o_ref[...] * do_f, NT)[0:1, :]

        def make_step(chains):
            rows = slice(chains[0][0] * tq, (chains[-1][0] + 1) * tq)

            def step(j, carry):
                ks = pl.multiple_of(j * tk, tk)
                kb, vb, ktb = k_ref[pl.ds(ks, tk), :], v_ref[pl.ds(ks, tk), :], kt_ref[:, pl.ds(ks, tk)]
                cols = [slice(c * tq, (c + 1) * tq) for c, _ in chains]
                sts = [_dot(kb, q_all[cs, :], NT) for cs in cols]
                sts = [_causal_mask_t(st, j * tk, qi * tqs + c * tq) if masked else st
                       for st, (c, masked) in zip(sts, chains)]
                dpts = [_dot(vb, do_all[cs, :], NT) for cs in cols]
                pts = [jnp.exp(st - lse_row[:, cs]) for st, cs in zip(sts, cols)]
                dsts = [(pt * (dpt - delta_row[:, cs])).astype(BF16) for pt, dpt, cs in zip(pts, dpts, cols)]
                pts = [pt.astype(BF16) for pt in pts]
                dqs = [_dot(ktb, dst) for dst in dsts]
                for cs, dq in zip(cols, dqs):
                    dqt_acc[:, cs] += dq
                pt_all = jnp.concatenate(pts, axis=1) if len(chains) > 1 else pts[0]
                dst_all = jnp.concatenate(dsts, axis=1) if len(chains) > 1 else dsts[0]
                dk_ref[pl.ds(ks, tk), :] += _dot(dst_all, q_all[rows, :])
                dv_acc[pl.ds(ks, tk), :] += _dot(pt_all, do_all[rows, :])
                return carry
            return step

        below = qi * (tqs // tk)
        lax.fori_loop(0, below, make_step([(c, False) for c in range(n_chains)]), 0)
        for t in range(tqs // tk):
            make_step(_diagonal_chains(t, tq, n_chains, tk))(below + t, 0)
        dq_ref[...] = jnp.transpose(dqt_acc[...])

        @pl.when(qi == s_dim // tqs - 1)
        def _():
            dv_ref[...] = dv_acc[...].astype(dv_ref.dtype)

    q_spec = pl.BlockSpec((tqs, QK_HEAD), lambda h, qi: (qi, h))
    o_spec = pl.BlockSpec((tqs, HEAD), lambda h, qi: (qi, h))
    k_spec = pl.BlockSpec((s_dim, QK_HEAD), lambda h, qi: (0, h))
    v_spec = pl.BlockSpec((s_dim, HEAD), lambda h, qi: (0, h))
    wide2 = jax.ShapeDtypeStruct((s_dim, N_HEADS * QK_HEAD), F32)
    return pl.pallas_call(
        body,
        out_shape=[wide2, wide2, jax.ShapeDtypeStruct((s_dim, N_HEADS * HEAD), BF16)],
        grid=(N_HEADS, s_dim // tqs),
        in_specs=[q_spec, k_spec, pl.BlockSpec((QK_HEAD, s_dim), lambda h, qi: (h, 0)), v_spec, o_spec, o_spec,
                  pl.BlockSpec((None, 1, tqs), lambda h, qi: (h, 0, qi))],
        out_specs=[q_spec, k_spec, v_spec],
        scratch_shapes=[pltpu.VMEM((QK_HEAD, tqs), F32), pltpu.VMEM((s_dim, HEAD), F32)],
        compiler_params=pltpu.CompilerParams(dimension_semantics=("parallel", "arbitrary")),
        name="attn_bwd",
    )(qc, kc, kct, v, o, d_o, lse)


def _mix_proj_ln1(y_dn, y_mla, proj_g, x, w_o, g, b):
    s_dim = x.shape[0]
    tm = min(512, s_dim)

    def body(yd_ref, ym_ref, g_ref, x_ref, w_ref, lg_ref, lb_ref, mixed_ref, a1_ref, h1_ref, h1b_ref):
        gates = g_ref[...].astype(F32)
        mixed = (_sig(gates[:, :D_MODEL]) * yd_ref[...].astype(F32)
                 + _sig(gates[:, D_MODEL:]) * ym_ref[...].astype(F32)).astype(BF16)
        a1 = _dot(mixed, w_ref[...])
        xh, _ = _ln_stats(ALPHA * x_ref[...] + a1)
        y = xh * lg_ref[...] + lb_ref[...]
        mixed_ref[...] = mixed
        a1_ref[...] = a1
        h1_ref[...] = y
        h1b_ref[...] = y.astype(BF16)

    row = lambda width: pl.BlockSpec((tm, width), lambda i: (i, 0))
    whole = lambda a: pl.BlockSpec(a.shape, lambda i: (0, 0))
    sds = lambda dt: jax.ShapeDtypeStruct((s_dim, D_MODEL), dt)
    return pl.pallas_call(
        body,
        out_shape=[sds(BF16), sds(F32), sds(F32), sds(BF16)],
        grid=(s_dim // tm,),
        in_specs=[row(D_MODEL), row(D_MODEL), row(2 * D_MODEL), row(D_MODEL), whole(w_o), whole(g), whole(b)],
        out_specs=[row(D_MODEL)] * 4,
        compiler_params=pltpu.CompilerParams(dimension_semantics=("parallel",)),
        name="mix_proj_ln1",
    )(y_dn, y_mla, proj_g, x, w_o, g, b)


def _ln1_mix_bwd(x, a1, d_h1, d_pg, y_dn, y_mla, proj_g, g, w_o, w_pg):
    def fn(r, c):
        x_, a1_, dy, dpg, yd, ym, gates = r
        g_, w_o_, w_pg_ = c
        dy = dy + _dot(dpg, w_pg_, NT)
        xh, rr = _ln_stats(ALPHA * x_ + a1_)
        dz = _ln_bwd(dy, xh, rr, g_)
        dz_b = dz.astype(BF16)
        dm = _dot(dz_b, w_o_, NT)
        sd, sm = _sig(gates[:, :D_MODEL]), _sig(gates[:, D_MODEL:])
        d_g = jnp.concatenate([dm * yd * sd * (1.0 - sd), dm * ym * sm * (1.0 - sm)], axis=1)
        return [dz_b, ALPHA * dz, d_g, dm * sd, dm * sm], [_colsum(dy * xh), _colsum(dy)]

    return _rowwise(fn, [x, a1, d_h1, d_pg, y_dn, y_mla, proj_g], [g, w_o, w_pg],
                    [(D_MODEL, BF16), (D_MODEL, F32), (2 * D_MODEL, BF16), (D_MODEL, BF16), (D_MODEL, BF16)],
                    accs=[(1, D_MODEL), (1, D_MODEL)], tm=FUSED_ROWS, name="ln1_mix_bwd")


def _ln_stats(z):
    mu = _rowmean(z)
    zc = z - mu
    r = lax.rsqrt(_rowmean(zc * zc) + EPS_LN)
    return zc * r, r


def _ln_bwd(dy, xh, r, g):
    dxh = dy * g
    return r * (dxh - _rowmean(dxh) - xh * _rowmean(dxh * xh))


def _ffn_in_act(h1b, w_t):
    s_dim, k_dim = h1b.shape
    hidden = w_t.shape[0] // 2
    tm, tn = min(512, s_dim), _pick_wide(hidden)
    nt = hidden // tn

    def body(a_ref, bg_ref, bu_ref, gt_ref, up_ref, act_ref):
        a = a_ref[...]
        gt, up = _dot(a, bg_ref[...], NT), _dot(a, bu_ref[...], NT)
        gt_ref[...] = gt.astype(BF16)
        up_ref[...] = up.astype(BF16)
        act_ref[...] = (gt * _sig(gt) * up).astype(BF16)

    o_spec = pl.BlockSpec((tm, tn), lambda j, i: (i, j))
    sds = jax.ShapeDtypeStruct((s_dim, hidden), BF16)
    return pl.pallas_call(
        body,
        out_shape=[sds, sds, sds],
        grid=(nt, s_dim // tm),
        in_specs=[pl.BlockSpec((tm, k_dim), lambda j, i: (i, 0)), pl.BlockSpec((tn, k_dim), lambda j, i: (j, 0)),
                  pl.BlockSpec((tn, k_dim), lambda j, i: (j + nt, 0))],
        out_specs=[o_spec, o_spec, o_spec],
        compiler_params=pltpu.CompilerParams(dimension_semantics=("parallel", "parallel")),
        name="ffn_in_act",
    )(h1b, w_t, w_t)


def _act_bwd(gt, up, d_act):
    def fn(r, c):
        gt_, up_, da = r
        return [jnp.concatenate([da * up_ * _silu_grad(gt_), da * gt_ * _sig(gt_)], axis=1)], []

    return _rowwise(fn, [gt, up, d_act], [], [(2 * FFN_HIDDEN, BF16)], name="act_bwd")[0]


def _tail(h1, ffn, p, tgt, g, b, w_pg, w_ple_t):
    def fn(r, c):
        h1_, ffn_, p_, t_ = r
        pg_ = _dot(h1_, c[2])
        pp_ = _dot(p_, c[3], NT)
        sp = _sig(pg_)
        xh, rr = _ln_stats(ALPHA * h1_ + ffn_ + sp * pp_)
        y = xh * c[0] + c[1]
        err = y - t_
        dy = err * (1.0 / D_MODEL)
        dz = _ln_bwd(dy, xh, rr, c[0])
        loss = jnp.sum(0.5 * _rowmean(err * err), axis=0, keepdims=True)
        return ([dz, dz * pp_ * sp * (1.0 - sp), dz * sp, ALPHA * dz],
                [_colsum(dy * xh), _colsum(dy), jnp.broadcast_to(loss, (1, LANES))])

    return _rowwise(fn, [h1, ffn, p, tgt], [g, b, w_pg, w_ple_t], [(D_MODEL, BF16)] * 3 + [(D_MODEL, F32)],
                    accs=[(1, D_MODEL), (1, D_MODEL), (1, LANES)], tm=FUSED_ROWS, name="tail")


def _local_step(x, p, pos, tgt, w, late_weights, emit):
    w = dict(w)
    s_dim = x.shape[0]
    xb, pb = x.astype(BF16), p.astype(BF16)
    proj_a = _mm(xb, w["wa_t"], tb=True, name="f_proj_a")
    proj_g = _mm(xb, w["wg_t"], tb=True, out_dtype=BF16, name="f_proj_g")
    proj_b = _mm(xb, w["wb_t"], tb=True, name="f_proj_b")
    qkvn = _conv_fwd(proj_a, w["conv"])
    beta, gc = _gates_fwd(proj_b, w["alog"], w["dtb"])
    gc_t = jnp.transpose(gc[:, :N_HEADS])
    u, w_, qd, kt, a_mat, t_fold = _gdr_prep_fwd(qkvn, beta, gc, gc_t)
    o_dn, states = _gdr_scan_fwd(u, w_, qd, kt, a_mat, gc)
    w.update(late_weights("mix", o_dn))
    og, y_dn = _gdr_out_fwd(o_dn, proj_a, w["dnw"], w["br_dn"])
    c_q, c_kv, qc, kc, vv = _mla_prep_fwd(proj_b, pos, w["qnw"], w["kvnw"], w["uq"], w["uk"], w["uv"])
    o_mla, lse = _attn_fwd(qc, kc, jnp.transpose(vv))
    y_mla = _mm(o_mla, w["br_mla"], out_dtype=BF16, name="f_y_mla")
    mixed, a1, h1, h1b = _mix_proj_ln1(y_dn, y_mla, proj_g, x, w["wo"], w["ln1g"], w["ln1b"])
    w.update(late_weights("ffn", a1))
    gt, up, act = _ffn_in_act(h1b, w["ffn_in_t"])
    ffn = _mm(act, w["ffn_out"], name="f_ffn")
    g = {}
    dz2, d_pg, d_pp, dh1a, g["ln2g"], g["ln2b"], loss = _tail(h1, ffn, pb, tgt, w["ln2g"], w["ln2b"],
                                                            w["ple_gate"], w["ple_t"])
    g["ple_t"] = _mm(d_pp, pb, ta=True, out_dtype=BF16, name="b_w_ple")
    g["ple_gate"] = _mm(h1b, d_pg, ta=True, out_dtype=BF16, name="b_w_ple_gate")
    g["ffn_out"] = _mm(act, dz2, ta=True, out_dtype=BF16, name="b_w_ffn_out")
    d_act = _mm(dz2, w["ffn_out"], tb=True, out_dtype=BF16, name="b_act")
    d_gu = _act_bwd(gt, up, d_act)
    g["ffn_in_t"] = _mm(d_gu, h1b, ta=True, out_dtype=BF16, name="b_w_ffn_in")
    d_gu = emit("ffn", g, d_gu)
    d_h1 = _mm(d_gu, w["ffn_in_t"], add=(dh1a,), name="b_h1_ffn")
    dz1, dxa, d_proj_g, d_y_dn, d_y_mla, g["ln1g"], g["ln1b"] = _ln1_mix_bwd(
        x, a1, d_h1, d_pg, y_dn, y_mla, proj_g, w["ln1g"], w["wo"], w["ple_gate"])
    g["wo"] = _mm(mixed, dz1, ta=True, out_dtype=BF16, name="b_w_o")
    g["br_mla"] = _mm(o_mla, d_y_mla, ta=True, out_dtype=BF16, name="b_w_br_mla")
    d_o_mla = _mm(d_y_mla, w["br_mla"], tb=True, out_dtype=BF16, name="b_o_mla")
    d_qc, d_kc, d_v = _attn_bwd(qc, kc, jnp.transpose(kc), vv, o_mla, d_o_mla, lse)
    d_q_full, d_kn, d_cq, d_ckv, d_kr, g["qnw"], g["kvnw"] = _mla_prep_bwd(
        proj_b, pos, d_qc, d_kc, d_v, w["qnw"], w["kvnw"], w["uq"], w["uk"], w["uv"])
    g["uq"] = _mm(c_q, d_q_full, ta=True, out_dtype=BF16, name="b_w_uq")
    g["uk"] = _mm(c_kv, d_kn, ta=True, out_dtype=BF16, name="b_w_uk")
    g["uv"] = _mm(c_kv, d_v, ta=True, out_dtype=BF16, name="b_w_uv")
    g["br_dn"] = _mm(og, d_y_dn, ta=True, out_dtype=BF16, name="b_w_br_dn")
    d_y_dn = emit("mix", g, d_y_dn)
    d_o_dn, d_z, g["dnw"] = _gdr_out_bwd(o_dn, proj_a, d_y_dn, w["dnw"], w["br_dn"])
    du, dw, dqd, dkt, d_a, d_egl = _gdr_scan_bwd(u, w_, qd, kt, a_mat, gc, states, d_o_dn)
    dq, dk, dv, d_beta, d_gc = _gdr_prep_bwd(qkvn, beta, gc, gc_t, t_fold, u, w_, du, dw, dqd, dkt, d_a)
    d_egl_rows = jnp.pad(d_egl[:, None, :, 0], ((0, 0), (CHUNK - 1, 0), (0, LANES - N_HEADS))).reshape(s_dim, LANES)
    d_ba, g["alog"], g["dtb"] = _gates_bwd(proj_b, w["alog"], w["dtb"], gc, d_beta, d_gc, d_egl_rows)
    d_qkv, g["conv"] = _conv_bwd(proj_a, w["conv"], dq, dk, dv)
    zeros = jnp.zeros((s_dim, WB_CKV - Q_LORA), BF16)
    d_proj_b = jnp.concatenate([d_cq, zeros, d_ckv, d_kr, d_ba], axis=1)
    g["wa_qkv_t"] = _mm(d_qkv, xb, ta=True, name="b_w_qkv")
    g["wa_z_t"] = _mm(d_z, xb, ta=True, name="b_w_z")
    g["wg_t"] = _mm(d_proj_g, xb, ta=True, name="b_w_g")
    g["wb_t"] = _mm(d_proj_b, xb, ta=True, name="b_w_b")
    dx = _mm(d_qkv, w["wa_qkv_t"], add=(dxa,), name="b_x_qkv")
    dx = _mm(d_z, w["wa_z_t"], add=(dx,), name="b_x_z")
    dx = _mm(d_proj_g, w["wg_t"], add=(dx,), name="b_x_g")
    dx = _mm(d_proj_b, w["wb_t"], add=(dx,), name="b_x_b")
    return loss, dx, g


_BIG = (("w_in", 1), ("w_uq", 0), ("w_uk", 0), ("w_uv", 0), ("w_br_dn", 0), ("w_br_mla", 0),
        ("w_o", 0), ("w_ffn_in", 1), ("w_ffn_out", 0), ("w_ple", 1), ("w_ple_gate", 0))
_BIG_AXIS = dict(_BIG)
_SMALL = ("ln1_g", "ln1_b", "ln2_g", "ln2_b", "q_norm_w", "kv_norm_w", "dn_norm_w", "dn_a_log", "dn_dt_bias")
_ORDER = ("w_in", "conv_w", "dn_a_log", "dn_dt_bias", "dn_norm_w", "q_norm_w", "w_uq", "kv_norm_w", "w_uk", "w_uv",
          "w_br_dn", "w_br_mla", "w_o", "ln1_g", "ln1_b", "w_ffn_in", "w_ffn_out", "w_ple", "w_ple_gate", "ln2_g",
          "ln2_b")


def _stored_shape(name, shard_shape):
    axis = _BIG_AXIS[name]
    lead = shard_shape[axis]
    return lead, int(np.prod(shard_shape)) // lead


def _to_stored(name, shard):
    return jnp.moveaxis(shard, _BIG_AXIS[name], 0).reshape(_stored_shape(name, shard.shape))


def _from_stored(name, stored, shard_shape):
    axis = _BIG_AXIS[name]
    moved = (shard_shape[axis],) + shard_shape[:axis] + shard_shape[axis + 1:]
    return jnp.moveaxis(stored.reshape(moved), 0, axis)


_W_IN_ROWS = np.cumsum([0, 3072, 1024, 8, 8, Q_LORA, KV_LORA, ROPE, D_MODEL, D_MODEL])


def _first_weights(w_in_t, conv_full, small):
    r = _W_IN_ROWS
    zr = lambda n: jnp.zeros((n, D_MODEL), w_in_t.dtype)
    w = {}
    w["wa_t"] = w_in_t[r[0]:r[2]]
    w["wa_qkv_t"], w["wa_z_t"] = w_in_t[r[0]:r[1]], w_in_t[r[1]:r[2]]
    w["wg_t"] = w_in_t[r[7]:r[9]]
    w["wb_t"] = jnp.concatenate([w_in_t[r[4]:r[5]], zr(WB_CKV - Q_LORA), w_in_t[r[5]:r[7]], zr(LANES - ROPE),
                                 w_in_t[r[2]:r[4]], zr(LANES - 2 * N_HEADS)], axis=0)
    w["conv"] = conv_full
    pad_l = lambda v: jnp.pad(v, ((0, 0), (0, LANES - v.shape[1])))
    w["alog"], w["dtb"] = pad_l(small["dn_a_log"]), pad_l(small["dn_dt_bias"])
    w["dnw"], w["qnw"], w["kvnw"] = small["dn_norm_w"], small["q_norm_w"], small["kv_norm_w"]
    w["ln1g"], w["ln1b"], w["ln2g"], w["ln2b"] = small["ln1_g"], small["ln1_b"], small["ln2_g"], small["ln2_b"]
    return w


def _late_weights(group, fw):
    w = {}
    if group == "mix":
        uq = fw["w_uq"].reshape(Q_LORA, N_HEADS, HEAD + ROPE)
        uq_r = jnp.pad(uq[:, :, HEAD:], ((0, 0), (0, 0), (0, HEAD - ROPE)))
        w["uq"] = jnp.concatenate([uq[:, :, :HEAD].reshape(Q_LORA, -1), uq_r.reshape(Q_LORA, -1)], axis=1)
        w["uk"], w["uv"] = fw["w_uk"], fw["w_uv"]
        w["br_dn"], w["br_mla"], w["wo"] = fw["w_br_dn"], fw["w_br_mla"], fw["w_o"]
    else:
        w["ffn_in_t"], w["ffn_out"] = fw["w_ffn_in"], fw["w_ffn_out"]
        w["ple_t"], w["ple_gate"] = fw["w_ple"], fw["w_ple_gate"]
    return w


_GROUP_GRADS = {"ffn": (("w_ple", "ple_t"), ("w_ple_gate", "ple_gate"), ("w_ffn_out", "ffn_out"),
                        ("w_ffn_in", "ffn_in_t")),
                "mix": (("w_o", "wo"), ("w_br_mla", "br_mla"), ("w_uq", "uq"), ("w_uk", "uk"), ("w_uv", "uv"),
                        ("w_br_dn", "br_dn"))}


def _group_grads(group, g):
    out = {}
    for name, key in _GROUP_GRADS[group]:
        t = g[key]
        if name == "w_uq":
            uq_n = t[:, :D_MODEL].reshape(Q_LORA, N_HEADS, HEAD)
            uq_r = t[:, D_MODEL:].reshape(Q_LORA, N_HEADS, HEAD)[:, :, :ROPE]
            t = jnp.concatenate([uq_n, uq_r], axis=2).reshape(Q_LORA, -1)
        out[name] = t
    return out


def _last_grads(g):
    wb = g["wb_t"]
    w_in = jnp.concatenate([
        g["wa_qkv_t"], g["wa_z_t"], wb[WB_BA:WB_BA + 2 * N_HEADS], wb[WB_CQ:WB_CQ + Q_LORA],
        wb[WB_CKV:WB_CKV + KV_LORA], wb[WB_KR:WB_KR + ROPE], g["wg_t"]], axis=0)
    small = {"ln1_g": g["ln1g"], "ln1_b": g["ln1b"], "ln2_g": g["ln2g"], "ln2_b": g["ln2b"], "q_norm_w": g["qnw"],
             "kv_norm_w": g["kvnw"], "dn_norm_w": g["dnw"], "dn_a_log": g["alog"], "dn_dt_bias": g["dtb"],
             "conv_w": g["conv"]}
    return w_in, small


_SMALL_SLOTS = {"ln1_g": (0, 0, 1024), "ln1_b": (1, 0, 1024), "ln2_g": (2, 0, 1024), "ln2_b": (3, 0, 1024),
                "q_norm_w": (4, 0, 384), "kv_norm_w": (4, 384, 256), "dn_norm_w": (4, 640, 128),
                "dn_a_log": (4, 768, 8), "dn_dt_bias": (4, 896, 8)}
_SMALL_ROWS, _LOSS_ROW, _CONV_ROW0, _CONV_ROWS = 24, 5, 8, 12


def _pack_small_grads(small_g, loss):
    zeros = lambda r, c: jnp.zeros((r, c), F32)
    row4 = jnp.concatenate([small_g["q_norm_w"], small_g["kv_norm_w"], small_g["dn_norm_w"], small_g["dn_a_log"],
                            small_g["dn_dt_bias"]], axis=1)
    row5 = jnp.concatenate([loss, zeros(1, FLAT_COLS - LANES)], axis=1)
    head = jnp.concatenate([small_g["ln1_g"], small_g["ln1_b"], small_g["ln2_g"], small_g["ln2_b"], row4, row5,
                            zeros(2, FLAT_COLS)], axis=0)
    conv = small_g["conv_w"].reshape(_CONV_ROWS, FLAT_COLS)
    return jnp.concatenate([head, conv, zeros(_SMALL_ROWS - _CONV_ROW0 - _CONV_ROWS, FLAT_COLS)], axis=0)


_MESH_ID = pl.DeviceIdType.MESH
_ANY = pl.BlockSpec(memory_space=pl.ANY)


def _all_gather(blocks, name):
    n = len(blocks)

    def body(*refs):
        x_refs, out_refs = refs[:n], refs[n:2 * n]
        send_sems, recv_sems, local_sems = refs[2 * n:]
        x, y, c = lax.axis_index("x"), lax.axis_index("y"), lax.axis_index("c")
        me, sibling = (x, y, c), (x, y, 1 - c)
        chips = [(1 - x, y), (x, 1 - y), (1 - x, 1 - y)]

        def slot(i, px, py, pc):
            return out_refs[i].at[4 * px + 2 * py + pc]

        def copy(i, k, origin, to, src=None):
            return pltpu.make_async_remote_copy(
                src_ref=slot(i, *origin) if src is None else src, dst_ref=slot(i, *origin),
                send_sem=send_sems.at[7 * i + k], recv_sem=recv_sems.at[7 * i + k], device_id=to,
                device_id_type=_MESH_ID)

        mine = [pltpu.make_async_copy(x_refs[i], slot(i, *me), local_sems.at[i]) for i in range(n)]
        first, passed = [], []
        for i in range(n):
            mine[i].start()
            first.append(copy(i, 0, me, sibling, src=x_refs[i]))
            first += [copy(i, 1 + j, me, (*chip, c), src=x_refs[i]) for j, chip in enumerate(chips)]
        for cp in first:
            cp.start()
        for i in range(n):
            for j, chip in enumerate(chips):
                copy(i, 1 + j, (*chip, c), me).wait_recv()
                passed.append(copy(i, 4 + j, (*chip, c), sibling))
                passed[-1].start()
        for i in range(n):
            copy(i, 0, sibling, me).wait_recv()
            for j, chip in enumerate(chips):
                copy(i, 4 + j, (*chip, 1 - c), me).wait_recv()
        for cp in first + passed:
            cp.wait_send()
        for cp in mine:
            cp.wait()

    return pl.pallas_call(
        body,
        out_shape=[jax.ShapeDtypeStruct((N_DEV,) + b.shape, b.dtype) for b in blocks],
        in_specs=[_ANY] * n,
        out_specs=[_ANY] * n,
        scratch_shapes=[pltpu.SemaphoreType.DMA((7 * n,)), pltpu.SemaphoreType.DMA((7 * n,)),
                        pltpu.SemaphoreType.DMA((n,))],
        name=name,
    )(*blocks)


def _exchange_sibling(srcs, name):
    n = len(srcs)

    def body(*refs):
        src_refs, dst_refs = refs[:n], refs[n:2 * n]
        send_sems, recv_sems = refs[2 * n:]
        x, y, c = lax.axis_index("x"), lax.axis_index("y"), lax.axis_index("c")
        copies = [pltpu.make_async_remote_copy(
            src_ref=src_refs[i].at[2 * q + (1 - c)], dst_ref=dst_refs[i].at[q], send_sem=send_sems.at[4 * i + q],
            recv_sem=recv_sems.at[4 * i + q], device_id=(x, y, 1 - c), device_id_type=_MESH_ID)
            for i in range(n) for q in range(4)]
        for cp in copies:
            cp.start()
        for cp in copies:
            cp.wait_recv()
        for cp in copies:
            cp.wait_send()

    return pl.pallas_call(
        body,
        out_shape=[jax.ShapeDtypeStruct((4,) + s.shape[1:], s.dtype) for s in srcs],
        in_specs=[_ANY] * n,
        out_specs=[_ANY] * n,
        scratch_shapes=[pltpu.SemaphoreType.DMA((4 * n,)), pltpu.SemaphoreType.DMA((4 * n,))],
        name=name,
    )(*srcs)


def _exchange_chips(srcs, name):
    n = len(srcs)

    def body(*refs):
        src_refs, dst_refs = refs[:n], refs[n:2 * n]
        send_sems, recv_sems = refs[2 * n:]
        x, y, c = lax.axis_index("x"), lax.axis_index("y"), lax.axis_index("c")
        chips = [(1 - x, y), (x, 1 - y), (1 - x, 1 - y)]
        copies = [pltpu.make_async_remote_copy(
            src_ref=src_refs[i].at[2 * tx + ty], dst_ref=dst_refs[i].at[j], send_sem=send_sems.at[3 * i + j],
            recv_sem=recv_sems.at[3 * i + j], device_id=(tx, ty, c), device_id_type=_MESH_ID)
            for i in range(n) for j, (tx, ty) in enumerate(chips)]
        for cp in copies:
            cp.start()
        for cp in copies:
            cp.wait_recv()
        for cp in copies:
            cp.wait_send()

    return pl.pallas_call(
        body,
        out_shape=[jax.ShapeDtypeStruct((3,) + s.shape[1:], s.dtype) for s in srcs],
        in_specs=[_ANY] * n,
        out_specs=[_ANY] * n,
        scratch_shapes=[pltpu.SemaphoreType.DMA((3 * n,)), pltpu.SemaphoreType.DMA((3 * n,))],
        name=name,
    )(*srcs)


def _col_tile(c):
    return c if c <= 256 else 256


def _chip_sum(src, recv, parity, name):
    _, r, c = src.shape
    tc = _col_tile(c)

    def body(par_ref, a_ref, b_ref, o_ref, ob_ref):
        s = a_ref[...] + b_ref[...]
        o_ref[...] = s
        ob_ref[...] = s.astype(BF16)

    blk = lambda f: pl.BlockSpec((None, r, tc), f)
    return pl.pallas_call(
        body,
        out_shape=[jax.ShapeDtypeStruct((4, r, c), F32), jax.ShapeDtypeStruct((4, r, c), BF16)],
        grid_spec=pltpu.PrefetchScalarGridSpec(
            num_scalar_prefetch=1, grid=(4, c // tc),
            in_specs=[blk(lambda q, j, par: (2 * q + par[0], 0, j)), blk(lambda q, j, par: (q, 0, j))],
            out_specs=[blk(lambda q, j, par: (q, 0, j)), blk(lambda q, j, par: (q, 0, j))]),
        compiler_params=pltpu.CompilerParams(dimension_semantics=("parallel", "parallel")),
        name=name,
    )(parity, src, recv)


def _sum_parts(own, others, chip, name):
    _, r, c = own.shape
    tc = _col_tile(c)

    def body(q_ref, a_ref, b_ref, o_ref):
        o_ref[...] = ((a_ref[...] + b_ref[0].astype(F32)) + b_ref[1].astype(F32)) + b_ref[2].astype(F32)

    return pl.pallas_call(
        body,
        out_shape=jax.ShapeDtypeStruct((r, c), F32),
        grid_spec=pltpu.PrefetchScalarGridSpec(
            num_scalar_prefetch=1, grid=(c // tc,),
            in_specs=[pl.BlockSpec((None, r, tc), lambda j, q: (q[0], 0, j)),
                      pl.BlockSpec((3, r, tc), lambda j, q: (0, 0, j))],
            out_specs=pl.BlockSpec((r, tc), lambda j, q: (0, j))),
        compiler_params=pltpu.CompilerParams(dimension_semantics=("parallel",)),
        name=name,
    )(chip, own, others)


_HBM = pl.BlockSpec(memory_space=pltpu.HBM)
_SEM = pl.BlockSpec(memory_space=pltpu.SEMAPHORE)
_DATAFLOW = pltpu.SideEffectType.DATAFLOW_SIDE_EFFECTING
N_PEERS = N_DEV - 1


def _ring_peer(j):
    me = 4 * lax.axis_index("x") + 2 * lax.axis_index("y") + lax.axis_index("c")
    k = (me + j) % N_DEV
    return me, k, (k // 4, (k // 2) % 2, k % 2)


def _spread_copy(i, j, src_refs, land_refs, send_sems, recv_sems, scatter):
    me, k, peer = _ring_peer(j)
    return pltpu.make_async_remote_copy(
        src_ref=src_refs[i].at[k] if scatter else src_refs[i], dst_ref=land_refs[i].at[me],
        send_sem=send_sems.at[N_PEERS * i + j - 1], recv_sem=recv_sems.at[N_PEERS * i + j - 1], device_id=peer,
        device_id_type=_MESH_ID)


def _spread_start(srcs, carry, scatter, name):
    n = len(srcs)
    lands = [lax.empty(((N_DEV,) + s.shape[-2:]), s.dtype) for s in srcs]

    def body(*refs):
        src_refs, land_refs = refs[:n], refs[n:2 * n]
        send_sems, recv_sems, local_sems = refs[2 * n + 1:2 * n + 4]
        for i in range(n):
            for j in range(1, N_DEV):
                _spread_copy(i, j, src_refs, land_refs, send_sems, recv_sems, scatter).start()
        for i in range(n):
            _own_copy(i, src_refs, land_refs, local_sems, scatter).start()

    hbm = lambda a: pltpu.HBM(a.shape, a.dtype)
    sems = pltpu.SemaphoreType.DMA((N_PEERS * n,))
    pinned = [pltpu.with_memory_space_constraint(a, pltpu.HBM) for a in list(srcs) + lands + [carry]]
    res = pl.pallas_call(
        body, name=name,
        out_shape=(sems, sems, pltpu.SemaphoreType.DMA((n,)), *[hbm(a) for a in pinned]),
        in_specs=[_HBM] * (2 * n + 1),
        out_specs=(_SEM, _SEM, _SEM, *[_HBM] * (2 * n + 1)),
        input_output_aliases={i: 3 + i for i in range(2 * n + 1)},
        compiler_params=pltpu.CompilerParams(has_side_effects=_DATAFLOW),
    )(*pinned)
    return res[:3], list(res[3:3 + n]), list(res[3 + n:3 + 2 * n]), res[3 + 2 * n]


def _own_copy(i, src_refs, land_refs, local_sems, scatter):
    me = _ring_peer(0)[0]
    return pltpu.make_async_copy(src_refs[i].at[me] if scatter else src_refs[i], land_refs[i].at[me],
                                 local_sems.at[i])


def _spread_wait(started, after, scatter, name):
    sems, srcs, lands, _ = started
    n = len(srcs)

    def body(*refs):
        src_refs, land_refs = refs[:n], refs[n:2 * n]
        send_s, recv_s, local_s = refs[2 * n:2 * n + 3]
        for i in range(n):
            for j in range(1, N_DEV):
                cp = _spread_copy(i, j, src_refs, land_refs, send_s, recv_s, scatter)
                cp.wait_send()
                cp.wait_recv()
        for i in range(n):
            _own_copy(i, src_refs, land_refs, local_s, scatter).wait()

    hbm = lambda a: pltpu.HBM(a.shape, a.dtype)
    res = pl.pallas_call(
        body, name=name,
        out_shape=tuple(hbm(a) for a in srcs + lands),
        in_specs=[_HBM] * (2 * n) + [_SEM, _SEM, _SEM, pl.BlockSpec(memory_space=pl.ANY)],
        out_specs=tuple([_HBM] * (2 * n)),
        input_output_aliases={i: i for i in range(2 * n)},
        compiler_params=pltpu.CompilerParams(has_side_effects=_DATAFLOW),
    )(*srcs, *lands, *sems, after)
    return list(res[n:])


def _sum8(landing, name):
    _, r, c = landing.shape
    tc = _col_tile(c)

    def body(a_ref, o_ref):
        tot = a_ref[0].astype(F32)
        for k in range(1, N_DEV):
            tot = tot + a_ref[k].astype(F32)
        o_ref[...] = tot

    return pl.pallas_call(
        body,
        out_shape=jax.ShapeDtypeStruct((r, c), F32),
        grid=(c // tc,),
        in_specs=[pl.BlockSpec((N_DEV, r, tc), lambda j: (0, 0, j))],
        out_specs=pl.BlockSpec((r, tc), lambda j: (0, j)),
        compiler_params=pltpu.CompilerParams(dimension_semantics=("parallel",)),
        name=name,
    )(landing)


def _adamw_math(w, g, m, v):
    m = ADAM_B1 * m + (1.0 - ADAM_B1) * g
    v = ADAM_B2 * v + (1.0 - ADAM_B2) * (g * g)
    m_hat = m / (1.0 - ADAM_B1 ** ADAM_STEP)
    v_hat = v / (1.0 - ADAM_B2 ** ADAM_STEP)
    delta = -ADAM_LR * (m_hat / (jnp.sqrt(v_hat) + ADAM_EPS) + ADAM_WD * w)
    return delta, m, v


def _adamw(w, m, v, g, name):
    r, c = w.shape

    def fn(rows, consts):
        return list(_adamw_math(*rows)), []

    return _rowwise(fn, [w, g, m, v], [], [(c, F32)] * 3, tm=r if r <= 512 else 256, name=name)


def _adamw_sum8(w, m, v, landing, name):
    r, c = w.shape
    tc = _col_tile(c)

    def body(w_ref, m_ref, v_ref, a_ref, g_ref, d_ref, m2_ref, v2_ref):
        g = a_ref[0].astype(F32)
        for k in range(1, N_DEV):
            g = g + a_ref[k].astype(F32)
        delta, m2, v2 = _adamw_math(w_ref[...], g, m_ref[...], v_ref[...])
        g_ref[...] = g
        d_ref[...] = delta
        m2_ref[...] = m2
        v2_ref[...] = v2

    blk = pl.BlockSpec((r, tc), lambda j: (0, j))
    return pl.pallas_call(
        body,
        out_shape=[jax.ShapeDtypeStruct((r, c), F32)] * 4,
        grid=(c // tc,),
        in_specs=[blk, blk, blk, pl.BlockSpec((N_DEV, r, tc), lambda j: (0, 0, j))],
        out_specs=[blk] * 4,
        compiler_params=pltpu.CompilerParams(dimension_semantics=("parallel",)),
        name=name,
    )(w, m, v, landing)


def _adamw_small(gathered, params):
    ns = len(_SMALL)

    def body(*refs):
        g_ref, p_refs, o_refs = refs[0], refs[1:1 + 3 * ns], refs[1 + 3 * ns:]
        tot = g_ref[0]
        for k in range(1, N_DEV):
            tot = tot + g_ref[k]
        for i, name in enumerate(_SMALL):
            row, lane0, lanes = _SMALL_SLOTS[name]
            g = tot[row:row + 1, lane0:lane0 + lanes]
            w_, m_, v_ = (p_refs[3 * i + j][...] for j in range(3))
            delta, m2, v2 = _adamw_math(w_, g, m_, v_)
            for j, val in enumerate((g, delta, m2, v2)):
                o_refs[4 * i + j][...] = val
        o_refs[4 * ns][...] = tot[_LOSS_ROW:_LOSS_ROW + 1, 0:LANES]
        o_refs[4 * ns + 1][...] = tot[_CONV_ROW0:_CONV_ROW0 + _CONV_ROWS, :]

    out_shape = [jax.ShapeDtypeStruct(w.shape, F32) for (w, _, _) in params for _ in range(4)]
    out_shape += [jax.ShapeDtypeStruct((1, LANES), F32), jax.ShapeDtypeStruct((_CONV_ROWS, FLAT_COLS), F32)]
    flat = [a for wmv in params for a in wmv]
    return pl.pallas_call(body, out_shape=out_shape, name="adamw_small")(gathered, *flat)


def kernel(x, p, positions, w_in, conv_w, dn_a_log, dn_dt_bias, dn_norm_w, q_norm_w, w_uq, kv_norm_w, w_uk, w_uv, w_br_dn, w_br_mla, w_o, ln1_g, ln1_b, w_ffn_in, w_ffn_out, w_ple, w_ple_gate, ln2_g, ln2_b, loss_target, m_w_in, m_conv_w, m_dn_a_log, m_dn_dt_bias, m_dn_norm_w, m_q_norm_w, m_w_uq, m_kv_norm_w, m_w_uk, m_w_uv, m_w_br_dn, m_w_br_mla, m_w_o, m_ln1_g, m_ln1_b, m_w_ffn_in, m_w_ffn_out, m_w_ple, m_w_ple_gate, m_ln2_g, m_ln2_b, v_w_in, v_conv_w, v_dn_a_log, v_dn_dt_bias, v_dn_norm_w, v_q_norm_w, v_w_uq, v_kv_norm_w, v_w_uk, v_w_uv, v_w_br_dn, v_w_br_mla, v_w_o, v_ln1_g, v_ln1_b, v_w_ffn_in, v_w_ffn_out, v_w_ple, v_w_ple_gate, v_ln2_g, v_ln2_b):
    args = dict(locals())
    wts = {n: args[n] for n in _ORDER}
    mom1 = {n: args["m_" + n] for n in _ORDER}
    mom2 = {n: args["v_" + n] for n in _ORDER}
    big_names = [n for n, _ in _BIG]
    shard_shapes = {n: wts[n].shape[1:] for n in big_names}
    c_idx = lax.axis_index("c")
    q_idx = 2 * lax.axis_index("x") + lax.axis_index("y")
    parity, chip = c_idx.reshape(1).astype(jnp.int32), q_idx.reshape(1).astype(jnp.int32)

    stored = {n: _to_stored(n, wts[n][0]).astype(BF16) for n in big_names}
    first = _all_gather([stored["w_in"], conv_w[0]], "ag_first")
    group_names = {grp: [n for n, _ in pairs] for grp, pairs in _GROUP_GRADS.items()}
    carry, gathers = first[0], {}
    for grp in ("mix", "ffn"):
        gathers[grp] = _spread_start([stored[n] for n in group_names[grp]], carry, False, "ag_start_" + grp)
        carry = gathers[grp][3]
    conv_full = jnp.moveaxis(first[1], 0, 1).reshape(conv_w.shape[1], -1)
    small_w = {n: wts[n].astype(F32) for n in _SMALL}
    w = _first_weights(carry.reshape(-1, D_MODEL), conv_full, small_w)

    def late_weights(grp, after):
        got = _spread_wait(gathers[grp], after, False, "ag_wait_" + grp)
        return _late_weights(grp, {n: t.reshape(-1, t.shape[-1]) for n, t in zip(group_names[grp], got)})

    started = {}

    def emit(group, g, carry):
        grads = _group_grads(group, g)
        srcs = [grads[n].reshape((N_DEV,) + _stored_shape(n, shard_shapes[n])) for n in grads]
        started[group] = (list(grads), _spread_start(srcs, carry, True, "rs_start_" + group))
        return started[group][1][3]

    s_dim = x.shape[1]
    loss, dx, g = _local_step(x[0], p[0, 0], positions.reshape(s_dim, 1).astype(F32), loss_target[0], w,
                              late_weights, emit)
    g_w_in, small_g = _last_grads(g)

    src = g_w_in.reshape((N_DEV,) + _stored_shape("w_in", shard_shapes["w_in"]))
    from_sibling = _exchange_sibling([src], "rs_sibling")[0]
    own, own_bf = _chip_sum(src, from_sibling, parity, "rs_sum_w_in")
    from_chips = _exchange_chips([own_bf], "rs_chips")[0]

    out_g, out_d, out_m, out_v = {}, {}, {}, {}

    def update(n, grad, shp):
        flat2 = (shp[0], int(np.prod(shp[1:])))
        d, m2, v2 = _adamw(wts[n][0].reshape(flat2), mom1[n][0].reshape(flat2), mom2[n][0].reshape(flat2),
                           grad.reshape(flat2), "adamw_" + n)
        out_g[n], out_d[n], out_m[n], out_v[n] = grad, d.reshape(shp), m2.reshape(shp), v2.reshape(shp)

    total = _sum_parts(own, from_chips, chip, "rs_total_w_in")
    update("w_in", _from_stored("w_in", total, shard_shapes["w_in"]), shard_shapes["w_in"])
    for group, (names, st) in started.items():
        for n, landing in zip(names, _spread_wait(st, dx, True, "rs_wait_" + group)):
            shp = shard_shapes[n]
            if _BIG_AXIS[n] == 0:
                flat2 = _stored_shape(n, shp)
                res = _adamw_sum8(wts[n][0].reshape(flat2), mom1[n][0].reshape(flat2), mom2[n][0].reshape(flat2),
                                  landing, "adamw_" + n)
                out_g[n], out_d[n], out_m[n], out_v[n] = (t.reshape(shp) for t in res)
            else:
                update(n, _from_stored(n, _sum8(landing, "rs_total_" + n), shp), shp)

    g_small = _all_gather([_pack_small_grads(small_g, loss)], "ag_small")[0]
    res = _adamw_small(g_small, [(wts[n], mom1[n], mom2[n]) for n in _SMALL])
    for i, n in enumerate(_SMALL):
        out_g[n], out_d[n], out_m[n], out_v[n] = res[4 * i:4 * i + 4]
    loss_out = res[4 * len(_SMALL)][0, 0]
    conv_shape = conv_w.shape[1:]
    conv_g = lax.dynamic_slice(res[-1].reshape(conv_shape[0], -1), (0, (2 * q_idx + c_idx) * conv_shape[1]),
                               conv_shape)
    update("conv_w", conv_g, conv_shape)

    expand = lambda d, n: d[n] if n in _SMALL else d[n][None]
    return (loss_out, dx[None], *[expand(out_g, n) for n in _ORDER], *[expand(out_d, n) for n in _ORDER],
            *[expand(out_m, n) for n in _ORDER], *[expand(out_v, n) for n in _ORDER])
```

```python
import functools

import numpy as np
import jax
import jax.numpy as jnp
from jax import lax
from jax.experimental import pallas as pl
from jax.experimental.pallas import tpu as pltpu

F32 = jnp.float32
BF16 = jnp.bfloat16

D_MODEL = 1024
N_HEADS = 8
HEAD = 128
CHUNK = 64
GROUP = 256
ROPE = 64
Q_LORA = 384
KV_LORA = 256
FFN_HIDDEN = 2816
PLE_DIM = 256
ROPE_BASE = 10000.0
ALPHA = 2.0 ** 0.25
SCALE = float((HEAD + ROPE) ** -0.5)
NEG_BIG = -1e30
EPS_RMS = 1e-6
EPS_LN = 1e-5

ADAM_LR = 0.001
ADAM_B1 = 0.9
ADAM_B2 = 0.999
ADAM_EPS = 1e-08
ADAM_WD = 0.01
ADAM_STEP = 10

N_DEV = 8
LANES = 128
FLAT_COLS = 1024

WB_CQ, WB_CKV, WB_KR, WB_BA, WB_COLS = 0, 512, 768, 896, 1024

HIGHEST = lax.Precision.HIGHEST

NN = (((1,), (0,)), ((), ()))
TN = (((0,), (0,)), ((), ()))
NT = (((1,), (1,)), ((), ()))


def _dot(a, b, dims=NN):
    return lax.dot_general(a.astype(BF16), b.astype(BF16), dims, preferred_element_type=F32)


def _dot32(a, b, dims=NN):
    return lax.dot_general(a, b, dims, precision=HIGHEST, preferred_element_type=F32)


def _sig(x):
    return 1.0 / (1.0 + jnp.exp(-x))


MM_TILE = 1536


def _pick_wide(n):
    if n <= MM_TILE:
        return n
    return max(t for t in range(LANES, MM_TILE + 1, LANES) if n % t == 0)


def _split_bf16(a):
    hi = a.astype(BF16)
    return hi, (a - hi.astype(F32)).astype(BF16)


def _dot3(a, b, dims=NN):
    ah, al = a if isinstance(a, tuple) else _split_bf16(a)
    bh, bl = b if isinstance(b, tuple) else _split_bf16(b)
    d = lambda p, q: lax.dot_general(p, q, dims, preferred_element_type=F32)
    return d(ah, bh) + (d(ah, bl) + d(al, bh))


def _mm(a, b, *, ta=False, tb=False, add=(), out_dtype=F32, name):
    if ta:
        k_dim, m_dim = a.shape
    else:
        m_dim, k_dim = a.shape
    if tb:
        n_dim, k2 = b.shape
    else:
        k2, n_dim = b.shape
    assert k_dim == k2, (a.shape, b.shape, ta, tb)
    tm = _pick_wide(m_dim)
    tn = _pick_wide(n_dim)
    tk = _pick_wide(k_dim)
    nk = k_dim // tk
    n_add = len(add)
    dims = TN if ta else (NT if tb else NN)
    assert not (ta and tb)

    def body(a_ref, b_ref, *rest):
        add_refs = rest[:n_add]
        o_ref = rest[n_add]
        acc = rest[n_add + 1]
        k = pl.program_id(2)

        @pl.when(k == 0)
        def _():
            acc[...] = jnp.zeros_like(acc)

        acc[...] += _dot(a_ref[...], b_ref[...], dims)

        @pl.when(k == nk - 1)
        def _():
            r = acc[...]
            for ar in add_refs:
                r = r + ar[...].astype(F32)
            o_ref[...] = r.astype(o_ref.dtype)

    a_spec = pl.BlockSpec((tk, tm), lambda i, j, k: (k, i)) if ta else pl.BlockSpec((tm, tk), lambda i, j, k: (i, k))
    b_spec = pl.BlockSpec((tn, tk), lambda i, j, k: (j, k)) if tb else pl.BlockSpec((tk, tn), lambda i, j, k: (k, j))
    o_spec = pl.BlockSpec((tm, tn), lambda i, j, k: (i, j))
    return pl.pallas_call(
        body,
        out_shape=jax.ShapeDtypeStruct((m_dim, n_dim), out_dtype),
        grid=(m_dim // tm, n_dim // tn, nk),
        in_specs=[a_spec, b_spec] + [o_spec] * n_add,
        out_specs=o_spec,
        scratch_shapes=[pltpu.VMEM((tm, tn), F32)],
        compiler_params=pltpu.CompilerParams(dimension_semantics=("parallel", "parallel", "arbitrary")),
        name=name,
    )(a, b, *add)


def _rowwise(fn, rows, consts, outs, accs=(), *, tm=256, name):
    rows = [r if isinstance(r, tuple) else (r, 0, r.shape[1]) for r in rows]
    s_dim = rows[0][0].shape[0]
    tm = min(tm, s_dim)
    assert s_dim % tm == 0 and all(arr.shape[0] == s_dim for arr, _, _ in rows)
    specs = [pl.BlockSpec((tm, width), functools.partial(lambda i, cb: (i, cb), cb=cb)) for _, cb, width in rows]
    args = [arr for arr, _, _ in rows]
    for c in consts:
        specs.append(pl.BlockSpec(c.shape, lambda i: (0, 0)))
        args.append(c)
    nr, nc, no = len(rows), len(consts), len(outs)
    flipped = [len(o) == 3 for o in outs]
    out_shape = [jax.ShapeDtypeStruct((o[0], s_dim) if t else (s_dim, o[0]), o[1]) for o, t in zip(outs, flipped)]
    out_specs = [pl.BlockSpec((o[0], tm), lambda i: (0, i)) if t else pl.BlockSpec((tm, o[0]), lambda i: (i, 0))
                 for o, t in zip(outs, flipped)]
    out_shape += [jax.ShapeDtypeStruct(sh, F32) for sh in accs]
    out_specs += [pl.BlockSpec(sh, lambda i: (0, 0)) for sh in accs]

    def body(*refs):
        r = [x[...].astype(F32) if x.dtype == BF16 else x[...] for x in refs[:nr]]
        c = [x[...] for x in refs[nr:nr + nc]]
        o_refs = refs[nr + nc:nr + nc + no]
        a_refs = refs[nr + nc + no:]
        o_vals, a_vals = fn(r, c)
        for ref, v, t in zip(o_refs, o_vals, flipped, strict=True):
            ref[...] = (jnp.transpose(v.astype(F32)) if t else v).astype(ref.dtype)
        if a_refs:
            @pl.when(pl.program_id(0) == 0)
            def _():
                for ref in a_refs:
                    ref[...] = jnp.zeros_like(ref)

            for ref, v in zip(a_refs, a_vals, strict=True):
                ref[...] += v

    res = pl.pallas_call(
        body,
        out_shape=out_shape,
        grid=(s_dim // tm,),
        in_specs=specs,
        out_specs=out_specs,
        compiler_params=pltpu.CompilerParams(dimension_semantics=("arbitrary" if accs else "parallel",)),
        name=name,
    )(*args)
    return res


def _colsum(v):
    return jnp.sum(v, axis=0, keepdims=True)


def _rowsum(v):
    return jnp.sum(v, axis=1, keepdims=True)


def _rowmean(v):
    return jnp.mean(v, axis=1, keepdims=True)


def _silu_grad(x):
    s = _sig(x)
    return s * (1.0 + x * (1.0 - s))


def _conv_taps(x, w, width=4):
    row = lax.broadcasted_iota(jnp.int32, x.shape, 0)
    c = x * w[width - 1:width, :]
    for s in range(1, width):
        c = c + jnp.where(row >= s, pltpu.roll(x, s, 0), 0.0) * w[width - 1 - s:width - s, :]
    return c


def _conv_fwd(proj_a, conv_w):
    s_dim = proj_a.shape[0]
    n_blk = 3 * N_HEADS

    def body(x_ref, w_ref, o_ref):
        j = pl.program_id(0)
        c = _conv_taps(x_ref[...], w_ref[...])
        y = c * _sig(c)
        r = lax.rsqrt(_rowsum(y * y) + EPS_RMS)
        fac = jnp.where(j < N_HEADS, r * (HEAD ** -0.5), jnp.where(j < 2 * N_HEADS, r, 1.0))
        o_ref[...] = y * fac

    return pl.pallas_call(
        body,
        out_shape=jax.ShapeDtypeStruct((s_dim, n_blk * HEAD), F32),
        grid=(n_blk,),
        in_specs=[pl.BlockSpec((s_dim, HEAD), lambda j: (0, j)), pl.BlockSpec((4, HEAD), lambda j: (0, j))],
        out_specs=pl.BlockSpec((s_dim, HEAD), lambda j: (0, j)),
        compiler_params=pltpu.CompilerParams(dimension_semantics=("parallel",)),
        name="conv_fwd",
    )(proj_a, conv_w)


def _conv_bwd(proj_a, conv_w, dq, dk, dv):
    s_dim = proj_a.shape[0]
    n_blk = 3 * N_HEADS

    def body(x_ref, w_ref, dq_ref, dk_ref, dv_ref, dx_ref, dw_ref):
        j = pl.program_id(0)
        x = x_ref[...]
        w = w_ref[...]
        do = jnp.where(j < N_HEADS, dq_ref[...], jnp.where(j < 2 * N_HEADS, dk_ref[...], dv_ref[...]))
        c = _conv_taps(x, w)
        sg = _sig(c)
        y = c * sg
        r = lax.rsqrt(_rowsum(y * y) + EPS_RMS)
        sc = jnp.where(j < N_HEADS, HEAD ** -0.5, 1.0)
        dy_n = sc * (r * do - y * (r * r * r) * _rowsum(do * y))
        dy = jnp.where(j < 2 * N_HEADS, dy_n, do)
        dc = dy * (sg * (1.0 + c * (1.0 - sg)))
        row = lax.broadcasted_iota(jnp.int32, x.shape, 0)
        dx = dc * w[3:4, :]
        dw_ref[3:4, :] = _colsum(dc * x)
        for s in range(1, 4):
            dx = dx + jnp.where(row < s_dim - s, pltpu.roll(dc, s_dim - s, 0), 0.0) * w[3 - s:4 - s, :]
            xs = jnp.where(row >= s, pltpu.roll(x, s, 0), 0.0)
            dw_ref[3 - s:4 - s, :] = _colsum(dc * xs)
        dx_ref[...] = dx.astype(dx_ref.dtype)

    hd = N_HEADS - 1
    return pl.pallas_call(
        body,
        out_shape=[jax.ShapeDtypeStruct((s_dim, n_blk * HEAD), BF16), jax.ShapeDtypeStruct((4, n_blk * HEAD), F32)],
        grid=(n_blk,),
        in_specs=[
            pl.BlockSpec((s_dim, HEAD), lambda j: (0, j)),
            pl.BlockSpec((4, HEAD), lambda j: (0, j)),
            pl.BlockSpec((s_dim, HEAD), lambda j: (0, jnp.minimum(j, hd))),
            pl.BlockSpec((s_dim, HEAD), lambda j: (0, jnp.clip(j - N_HEADS, 0, hd))),
            pl.BlockSpec((s_dim, HEAD), lambda j: (0, jnp.clip(j - 2 * N_HEADS, 0, hd))),
        ],
        out_specs=[pl.BlockSpec((s_dim, HEAD), lambda j: (0, j)), pl.BlockSpec((4, HEAD), lambda j: (0, j))],
        compiler_params=pltpu.CompilerParams(dimension_semantics=("parallel",)),
        name="conv_bwd",
    )(proj_a, conv_w, dq, dk, dv)


def _chunk_tri(n):
    r = np.arange(n)
    m = ((r[:, None] // CHUNK) == (r[None, :] // CHUNK)) & (r[:, None] >= r[None, :])
    m = m.astype(np.float32)
    return jnp.asarray(m), jnp.asarray(m.T)


def _softplus(z):
    return jnp.maximum(z, 0.0) + jnp.log(1.0 + jnp.exp(-jnp.abs(z)))


def _gates_fwd(proj_b, alog, dtb):
    tm = min(GROUP, proj_b.shape[0])
    tri, _ = _chunk_tri(tm)

    def fn(r, c):
        b = r[0]
        a = pltpu.roll(b, LANES - N_HEADS, 1)
        alog_, dtb_, tri_ = c
        g = -jnp.exp(alog_) * _softplus(a + dtb_)
        return [_sig(b), _dot32(tri_, g)], []

    return _rowwise(fn, [(proj_b, WB_BA // LANES, LANES)], [alog, dtb, tri],
                    [(LANES, F32), (LANES, F32)], tm=tm, name="gates_fwd")


def _gates_bwd(proj_b, alog, dtb, gc, d_beta, d_gc, d_egl_rows):
    tm = min(GROUP, proj_b.shape[0])
    _, tri_t = _chunk_tri(tm)

    def fn(r, c):
        b, gc_, d_beta_, d_gc_, d_egl_ = r
        a = pltpu.roll(b, LANES - N_HEADS, 1)
        alog_, dtb_, tri_t_ = c
        z = a + dtb_
        ea = jnp.exp(alog_)
        g = -ea * _softplus(z)
        dg = _dot32(tri_t_, d_gc_ + d_egl_ * jnp.exp(gc_))
        d_a = dg * (-ea) * _sig(z)
        beta = _sig(b)
        d_ba = d_beta_ * beta * (1.0 - beta) + pltpu.roll(d_a, N_HEADS, 1)
        return [d_ba], [_colsum(dg * g), _colsum(d_a)]

    return _rowwise(fn, [(proj_b, WB_BA // LANES, LANES), gc, d_beta, d_gc, d_egl_rows],
                    [alog, dtb, tri_t], [(LANES, BF16)], accs=[(1, LANES), (1, LANES)], tm=tm,
                    name="gates_bwd")


def _group_masks(n):
    r = lax.broadcasted_iota(jnp.int32, (n, n), 0)
    c = lax.broadcasted_iota(jnp.int32, (n, n), 1)
    same = (r // CHUNK) == (c // CHUNK)
    below, s = [], 2
    while s < CHUNK:
        below.append(jnp.logical_and((r // (2 * s)) == (c // (2 * s)),
                                     jnp.logical_and((r // s) % 2 == 1, (c // s) % 2 == 0)))
        s *= 2
    return dict(same=same, tril=jnp.logical_and(same, r >= c), strict=jnp.logical_and(same, r > c),
                last=c == (r // CHUNK) * CHUNK + (CHUNK - 1), eye=r == c, pair=(r // 2) == (c // 2), below=below)


def _inv_unit_lower(l_mats, mk):
    eye_f = mk["eye"].astype(F32)
    ts = [eye_f - jnp.where(mk["pair"], l_mat, 0.0) for l_mat in l_mats]
    for below in mk["below"]:
        halves = [_split_bf16(t) for t in ts]
        mids = [_dot3(h, jnp.where(below, l_mat, 0.0)) for h, l_mat in zip(halves, l_mats)]
        ts = [t - _dot3(m, h) for t, m, h in zip(ts, mids, halves)]
    return ts


def _unfold_blocks(folded, mask):
    n = folded.shape[0]
    return jnp.where(mask, jnp.concatenate([folded] * (n // CHUNK), axis=1), 0.0)


def _head_cols(beta, gc, gc_t, h):
    lane = lax.broadcasted_iota(jnp.int32, beta.shape, 1)
    sub = lax.broadcasted_iota(jnp.int32, gc_t.shape, 0)
    bcol = _rowsum(jnp.where(lane == h, beta, 0.0))
    gcol = _rowsum(jnp.where(lane == h, gc, 0.0))
    grow = _colsum(jnp.where(sub == h, gc_t, 0.0))
    return bcol, gcol, grow


def _prep_common(q, k, bcol, gcol, grow, mk, t_folded=None):
    n = q.shape[0]
    tril = mk["tril"]
    decay = jnp.where(tril, jnp.exp(jnp.where(tril, gcol - grow, 0.0)), 0.0)
    glast = _rowsum(jnp.where(mk["last"], jnp.broadcast_to(grow, (n, n)), 0.0))
    e = jnp.exp(gcol)
    ekt = jnp.exp(glast - gcol)
    kb = k * bcol
    kk = _dot(kb, k, NT)
    qk = _dot(q, k, NT)
    p = dict(decay=decay, e=e, ekt=ekt, kb=kb, kk=kk, qk=qk)
    if t_folded is not None:
        p["t"] = _unfold_blocks(t_folded, mk["same"])
    return p


GROUPS_PER_STEP = 4
SCAN_CHUNKS_PER_STEP = 4


def _fold_blocks(m):
    n = m.shape[0]
    out = m[:, 0:CHUNK]
    for b in range(1, n // CHUNK):
        out = out + m[:, b * CHUNK:(b + 1) * CHUNK]
    return out


def _gdr_prep_fwd(qkvn, beta, gc, gc_t):
    s_dim = qkvn.shape[0]
    tg = min(GROUP, s_dim)
    n_sub = min(GROUPS_PER_STEP, s_dim // tg)
    tb = tg * n_sub

    def body(q_ref, k_ref, v_ref, b_ref, g_ref, gt_ref, u_ref, w_ref, qd_ref, kt_ref, a_ref, t_ref):
        h = pl.program_id(0)
        mk = _group_masks(tg)
        parts = []
        for s in range(n_sub):
            rows = slice(s * tg, (s + 1) * tg)
            q, k, v = q_ref[rows, :], k_ref[rows, :], v_ref[rows, :]
            bcol, gcol, grow = _head_cols(b_ref[rows, :], g_ref[rows, :], gt_ref[:, rows], h)
            p = _prep_common(q, k, bcol, gcol, grow, mk)
            qd_ref[rows, :] = q * p["e"]
            kt_ref[rows, :] = k * p["ekt"]
            a_ref[rows, :] = _fold_blocks(jnp.where(mk["tril"], p["qk"] * p["decay"], 0.0))
            parts.append((rows, v * bcol, p["kb"] * p["e"], jnp.where(mk["strict"], p["kk"] * p["decay"], 0.0)))
        t_mats = _inv_unit_lower([part[3] for part in parts], mk)
        for (rows, vb, kbe, _), t_mat in zip(parts, t_mats):
            u_ref[rows, :] = _dot(t_mat, vb)
            w_ref[rows, :] = _dot(t_mat, kbe)
            t_ref[rows, :] = _fold_blocks(t_mat)

    row = lambda off: pl.BlockSpec((tb, HEAD), functools.partial(lambda h, m, off: (m, h + off), off=off))
    full = pl.BlockSpec((tb, LANES), lambda h, m: (m, 0))
    o_spec = pl.BlockSpec((tb, HEAD), lambda h, m: (m, h))
    a_spec = pl.BlockSpec((None, tb, CHUNK), lambda h, m: (h, m, 0))
    wide = jax.ShapeDtypeStruct((s_dim, N_HEADS * HEAD), F32)
    folded = jax.ShapeDtypeStruct((N_HEADS, s_dim, CHUNK), F32)
    return pl.pallas_call(
        body,
        out_shape=[wide, wide, wide, wide, folded, folded],
        grid=(N_HEADS, s_dim // tb),
        in_specs=[row(0), row(N_HEADS), row(2 * N_HEADS), full, full, pl.BlockSpec((8, tb), lambda h, m: (0, m))],
        out_specs=[o_spec, o_spec, o_spec, o_spec, a_spec, a_spec],
        compiler_params=pltpu.CompilerParams(dimension_semantics=("parallel", "parallel")),
        name="gdr_prep_fwd",
    )(qkvn, qkvn, qkvn, beta, gc, gc_t)


def _gdr_prep_bwd(qkvn, beta, gc, gc_t, t_fold, u, w, du, dw, dqd, dkt, d_a):
    s_dim = qkvn.shape[0]
    tg = min(GROUP, s_dim)
    n_sub = min(GROUPS_PER_STEP, s_dim // tg)
    tb = tg * n_sub

    def body(q_ref, k_ref, v_ref, b_ref, g_ref, gt_ref, t_ref, u_ref, w_ref, du_ref, dw_ref, dqd_ref, dkt_ref,
             da_ref, dq_ref, dk_ref, dv_ref, db_ref, dg_ref):
        h = pl.program_id(1)

        @pl.when(h == 0)
        def _():
            db_ref[...] = jnp.zeros_like(db_ref)
            dg_ref[...] = jnp.zeros_like(dg_ref)

        mk = _group_masks(tg)
        lane = lax.broadcasted_iota(jnp.int32, (tg, LANES), 1)
        for s in range(n_sub):
            rows = slice(s * tg, (s + 1) * tg)
            q, k, v = q_ref[rows, :], k_ref[rows, :], v_ref[rows, :]
            bcol, gcol, grow = _head_cols(b_ref[rows, :], g_ref[rows, :], gt_ref[:, rows], h)
            p = _prep_common(q, k, bcol, gcol, grow, mk, t_ref[rows, :])
            t_mat, decay, e, ekt, kb = p["t"], p["decay"], p["e"], p["ekt"], p["kb"]
            du_, dw_, dqd_, dkt_ = du_ref[rows, :], dw_ref[rows, :], dqd_ref[rows, :], dkt_ref[rows, :]
            dvb = _dot(t_mat, du_, TN)
            dkbe = _dot(t_mat, dw_, TN)
            d_l = -(_dot(dvb, u_ref[rows, :], NT) + _dot(dkbe, w_ref[rows, :], NT))
            m1 = jnp.where(mk["strict"], d_l, 0.0)
            m2 = _unfold_blocks(da_ref[rows, :], mk["tril"])
            d_kk = m1 * decay
            d_qk = m2 * decay
            d_decay = m1 * p["kk"] + m2 * p["qk"]
            dkb = _dot(d_kk, k) + dkbe * e
            dk = _dot(d_kk, kb, TN) + _dot(d_qk, q, TN) + dkt_ * ekt + dkb * bcol
            dq = _dot(d_qk, k) + dqd_ * e
            d_beta = _rowsum(dkb * k) + _rowsum(dvb * v)
            d_e = _rowsum(dkbe * kb) + _rowsum(dqd_ * q)
            d_ekt = _rowsum(dkt_ * k) * ekt
            d_diff = d_decay * decay
            d_grow = -_colsum(d_diff) + _colsum(jnp.where(mk["last"], jnp.broadcast_to(d_ekt, (tg, tg)), 0.0))
            d_gcol = d_e * e - d_ekt + _rowsum(d_diff)
            d_gcol = d_gcol + _rowsum(jnp.where(mk["eye"], jnp.broadcast_to(d_grow, (tg, tg)), 0.0))
            dq_ref[rows, :] = dq
            dk_ref[rows, :] = dk
            dv_ref[rows, :] = dvb * bcol
            db_ref[rows, :] = jnp.where(lane == h, d_beta, db_ref[rows, :])
            dg_ref[rows, :] = jnp.where(lane == h, d_gcol, dg_ref[rows, :])

    row = lambda off: pl.BlockSpec((tb, HEAD), functools.partial(lambda m, h, off: (m, h + off), off=off))
    full = pl.BlockSpec((tb, LANES), lambda m, h: (m, 0))
    o_spec = pl.BlockSpec((tb, HEAD), lambda m, h: (m, h))
    a_spec = pl.BlockSpec((None, tb, CHUNK), lambda m, h: (h, m, 0))
    wide = jax.ShapeDtypeStruct((s_dim, N_HEADS * HEAD), F32)
    lanes = jax.ShapeDtypeStruct((s_dim, LANES), F32)
    return pl.pallas_call(
        body,
        out_shape=[wide, wide, wide, lanes, lanes],
        grid=(s_dim // tb, N_HEADS),
        in_specs=[row(0), row(N_HEADS), row(2 * N_HEADS), full, full, pl.BlockSpec((8, tb), lambda m, h: (0, m)),
                  a_spec, o_spec, o_spec, o_spec, o_spec, o_spec, o_spec, a_spec],
        out_specs=[o_spec, o_spec, o_spec, full, full],
        compiler_params=pltpu.CompilerParams(dimension_semantics=("parallel", "arbitrary")),
        name="gdr_prep_bwd",
    )(qkvn, qkvn, qkvn, beta, gc, gc_t, t_fold, u, w, du, dw, dqd, dkt, d_a)


def _gdr_scan_fwd(u, w, qd, kt, a_mat, gc):
    s_dim = u.shape[0]
    n_chunks = s_dim // CHUNK
    per = min(SCAN_CHUNKS_PER_STEP, n_chunks)
    tb = per * CHUNK

    def body(u_ref, w_ref, qd_ref, kt_ref, a_ref, g_ref, o_ref, st_ref, state):
        @pl.when(pl.program_id(0) == 0)
        def _():
            state[...] = jnp.zeros_like(state)

        heads = range(N_HEADS)
        cols = [slice(h * HEAD, (h + 1) * HEAD) for h in heads]
        for i in range(per):
            rows = slice(i * CHUNK, (i + 1) * CHUNK)
            egl = jnp.exp(g_ref[(i + 1) * CHUNK - 1:(i + 1) * CHUNK, :])
            s_b = [state[h].astype(BF16) for h in heads]
            for h in heads:
                st_ref[i, h] = state[h]
            ws = [_dot(w_ref[rows, cs], s) for cs, s in zip(cols, s_b)]
            qs = [_dot(qd_ref[rows, cs], s) for cs, s in zip(cols, s_b)]
            vns = [(u_ref[rows, cs] - ws_h).astype(BF16) for cs, ws_h in zip(cols, ws)]
            avs = [_dot(a_ref[h, rows, :], vn) for h, vn in zip(heads, vns)]
            kvs = [_dot(kt_ref[rows, cs], vn, TN) for cs, vn in zip(cols, vns)]
            for h, cs in zip(heads, cols):
                o_ref[rows, cs] = qs[h] + avs[h]
                state[h] = state[h] * egl[:, h:h + 1] + kvs[h]

    wide = pl.BlockSpec((tb, N_HEADS * HEAD), lambda n: (n, 0))
    return pl.pallas_call(
        body,
        out_shape=[jax.ShapeDtypeStruct((s_dim, N_HEADS * HEAD), F32),
                   jax.ShapeDtypeStruct((n_chunks, N_HEADS, HEAD, HEAD), F32)],
        grid=(n_chunks // per,),
        in_specs=[wide, wide, wide, wide, pl.BlockSpec((N_HEADS, tb, CHUNK), lambda n: (0, n, 0)),
                  pl.BlockSpec((tb, LANES), lambda n: (n, 0))],
        out_specs=[wide, pl.BlockSpec((per, N_HEADS, HEAD, HEAD), lambda n: (n, 0, 0, 0))],
        scratch_shapes=[pltpu.VMEM((N_HEADS, HEAD, HEAD), F32)],
        compiler_params=pltpu.CompilerParams(dimension_semantics=("arbitrary",)),
        name="gdr_scan_fwd",
    )(u, w, qd, kt, a_mat, gc)


def _gdr_scan_bwd(u, w, qd, kt, a_mat, gc, states, d_o):
    s_dim = u.shape[0]
    n_chunks = s_dim // CHUNK
    per = min(SCAN_CHUNKS_PER_STEP, n_chunks)
    tb = per * CHUNK
    last = n_chunks // per - 1

    def body(u_ref, w_ref, qd_ref, kt_ref, a_ref, g_ref, st_ref, do_ref,
             du_ref, dw_ref, dqd_ref, dkt_ref, da_ref, de_ref, d_state):
        @pl.when(pl.program_id(0) == 0)
        def _():
            d_state[...] = jnp.zeros_like(d_state)

        heads = range(N_HEADS)
        cols = [slice(h * HEAD, (h + 1) * HEAD) for h in heads]
        for i in reversed(range(per)):
            rows = slice(i * CHUNK, (i + 1) * CHUNK)
            egl = jnp.exp(g_ref[(i + 1) * CHUNK - 1:(i + 1) * CHUNK, :])
            s_b = [st_ref[i, h].astype(BF16) for h in heads]
            ds_b = [d_state[h].astype(BF16) for h in heads]
            dos = [do_ref[rows, cs].astype(BF16) for cs in cols]
            w_b = [w_ref[rows, cs].astype(BF16) for cs in cols]
            ws = [_dot(w_h, s) for w_h, s in zip(w_b, s_b)]
            ados = [_dot(a_ref[h, rows, :], do, TN) for h, do in zip(heads, dos)]
            kds = [_dot(kt_ref[rows, cs], ds) for cs, ds in zip(cols, ds_b)]
            dqds = [_dot(do, s, NT) for do, s in zip(dos, s_b)]
            qdos = [_dot(qd_ref[rows, cs], do, TN) for cs, do in zip(cols, dos)]
            vns = [(u_ref[rows, cs] - ws_h).astype(BF16) for cs, ws_h in zip(cols, ws)]
            dvns = [a + k_ for a, k_ in zip(ados, kds)]
            dvn_b = [d.astype(BF16) for d in dvns]
            das = [_dot(do, vn, NT) for do, vn in zip(dos, vns)]
            dkts = [_dot(vn, ds, NT) for vn, ds in zip(vns, ds_b)]
            dws = [_dot(d, s, NT) for d, s in zip(dvn_b, s_b)]
            wds = [_dot(w_h, d, TN) for w_h, d in zip(w_b, dvn_b)]
            for h, cs in zip(heads, cols):
                ds_n = d_state[h]
                de = jnp.sum(_rowsum(ds_n * st_ref[i, h]), axis=0, keepdims=True)
                de_ref[i, h:h + 1, :] = jnp.broadcast_to(de, (1, LANES))
                dqd_ref[rows, cs] = dqds[h]
                da_ref[h, rows, :] = das[h]
                dkt_ref[rows, cs] = dkts[h]
                du_ref[rows, cs] = dvns[h]
                dw_ref[rows, cs] = -dws[h]
                d_state[h] = ds_n * egl[:, h:h + 1] + qdos[h] - wds[h]

    wide = pl.BlockSpec((tb, N_HEADS * HEAD), lambda n: (last - n, 0))
    a_spec = pl.BlockSpec((N_HEADS, tb, CHUNK), lambda n: (0, last - n, 0))
    wide_shape = jax.ShapeDtypeStruct((s_dim, N_HEADS * HEAD), F32)
    return pl.pallas_call(
        body,
        out_shape=[wide_shape, wide_shape, wide_shape, wide_shape,
                   jax.ShapeDtypeStruct((N_HEADS, s_dim, CHUNK), F32),
                   jax.ShapeDtypeStruct((n_chunks, N_HEADS, LANES), F32)],
        grid=(n_chunks // per,),
        in_specs=[wide, wide, wide, wide, a_spec, pl.BlockSpec((tb, LANES), lambda n: (last - n, 0)),
                  pl.BlockSpec((per, N_HEADS, HEAD, HEAD), lambda n: (last - n, 0, 0, 0)), wide],
        out_specs=[wide, wide, wide, wide, a_spec, pl.BlockSpec((per, N_HEADS, LANES), lambda n: (last - n, 0, 0))],
        scratch_shapes=[pltpu.VMEM((N_HEADS, HEAD, HEAD), F32)],
        compiler_params=pltpu.CompilerParams(dimension_semantics=("arbitrary",)),
        name="gdr_scan_bwd",
    )(u, w, qd, kt, a_mat, gc, states, d_o)


FUSED_ROWS = 512


def _gdr_out_fwd(o_dn, proj_a, dn_w, w_br):
    def fn(r, c):
        o, z = r
        w_, w_br_ = c
        outs = []
        for h in range(N_HEADS):
            cs = slice(h * HEAD, (h + 1) * HEAD)
            oh, zh = o[:, cs], z[:, cs]
            rr = lax.rsqrt(_rowmean(oh * oh) + EPS_RMS)
            outs.append(oh * rr * w_ * (zh * _sig(zh)))
        og = jnp.concatenate(outs, axis=1).astype(BF16)
        return [og, _dot(og, w_br_)], []

    return _rowwise(fn, [o_dn, (proj_a, 3, D_MODEL)], [dn_w, w_br], [(D_MODEL, BF16), (D_MODEL, BF16)],
                    tm=FUSED_ROWS, name="gdr_out_fwd")


def _gdr_out_bwd(o_dn, proj_a, d_y_dn, dn_w, w_br):
    def fn(r, c):
        o, z, dy = r
        w_, w_br_ = c
        dg = _dot(dy, w_br_, NT)
        d_o, d_z = [], []
        d_w = jnp.zeros((1, HEAD), F32)
        for h in range(N_HEADS):
            cs = slice(h * HEAD, (h + 1) * HEAD)
            oh, zh, dgh = o[:, cs], z[:, cs], dg[:, cs]
            rr = lax.rsqrt(_rowmean(oh * oh) + EPS_RMS)
            sz = zh * _sig(zh)
            d_n = dgh * sz
            d_z.append(dgh * (oh * rr * w_) * _silu_grad(zh))
            d_w = d_w + _colsum(d_n * oh * rr)
            gw = d_n * w_
            d_o.append(rr * gw - oh * (rr * rr * rr) * _rowmean(gw * oh))
        return [jnp.concatenate(d_o, axis=1), jnp.concatenate(d_z, axis=1)], [d_w]

    return _rowwise(fn, [o_dn, (proj_a, 3, D_MODEL), d_y_dn], [dn_w, w_br], [(D_MODEL, F32), (D_MODEL, BF16)],
                    accs=[(1, HEAD)], tm=FUSED_ROWS, name="gdr_out_bwd")


def _rms_fwd(x, w):
    r = lax.rsqrt(_rowmean(x * x) + EPS_RMS)
    return x * r * w


def _rms_bwd(x, w, dy):
    r = lax.rsqrt(_rowmean(x * x) + EPS_RMS)
    gw = dy * w
    return r * gw - x * (r * r * r) * _rowmean(gw * x), _colsum(dy * x * r)


def _rope_consts():
    inv = ROPE_BASE ** (-np.arange(0, ROPE, 2, dtype=np.float32) / ROPE)
    t = np.zeros((4, LANES), np.float32)
    t[0, :32] = inv
    t[0, 32:64] = inv
    t[1, :64] = 1.0
    t[2, 32:64] = 1.0
    t[3, :32] = -1.0
    return jnp.asarray(t)


def _rope_tables(pos, consts, width):
    ang = pos * consts[0:1, :]
    cosv, sinv = jnp.cos(ang), jnp.sin(ang)
    reps = width // LANES
    tile = (lambda t: jnp.concatenate([t] * reps, axis=1)) if reps > 1 else (lambda t: t)
    return tile(cosv * consts[1:2, :]), tile(sinv * consts[2:3, :]), tile(sinv * consts[3:4, :])


def _rope_apply(t, tabs):
    cos_t, sin_a, sin_b = tabs
    width = t.shape[1]
    return t * cos_t + pltpu.roll(t, 32, 1) * sin_a + pltpu.roll(t, width - 32, 1) * sin_b


def _rope_transpose(d, tabs):
    cos_t, sin_a, sin_b = tabs
    width = d.shape[1]
    return d * cos_t + pltpu.roll(d * sin_a, width - 32, 1) + pltpu.roll(d * sin_b, 32, 1)


QK_HEAD = 2 * HEAD


def _interleave_heads(a, b):
    parts = []
    for h in range(N_HEADS):
        parts.append(a[:, h * HEAD:(h + 1) * HEAD])
        parts.append(b if b.shape[1] == LANES else b[:, h * LANES:(h + 1) * LANES])
    return jnp.concatenate(parts, axis=1)


def _mla_rows(proj_b):
    return [(proj_b, WB_CQ // Q_LORA, Q_LORA), (proj_b, WB_CKV // KV_LORA, KV_LORA), (proj_b, WB_KR // LANES, LANES)]


def _mla_prep_fwd(proj_b, pos, qn_w, kvn_w, uq, uk, uv):
    def fn(r, c):
        cq, ckv, kr, pos_ = r
        qn_w_, kvn_w_, uq_, uk_, uv_, rope = c
        c_q = _rms_fwd(cq, qn_w_).astype(BF16)
        c_kv = _rms_fwd(ckv, kvn_w_).astype(BF16)
        qf = _dot(c_q, uq_)
        qr = _rope_apply(qf[:, D_MODEL:], _rope_tables(pos_, rope, D_MODEL))
        kr = _rope_apply(kr, _rope_tables(pos_, rope, LANES))
        kc = _interleave_heads(_dot(c_kv, uk_), kr)
        v = _dot(c_kv, uv_)
        return [c_q, c_kv, _interleave_heads(qf[:, :D_MODEL], qr) * SCALE, kc, v, kc, v], []

    wide2 = N_HEADS * QK_HEAD
    return _rowwise(fn, _mla_rows(proj_b) + [pos], [qn_w, kvn_w, uq, uk, uv, _rope_consts()],
                    [(Q_LORA, BF16), (KV_LORA, BF16), (wide2, BF16), (wide2, BF16), (D_MODEL, BF16),
                     (wide2, BF16, "T"), (D_MODEL, BF16, "T")], tm=FUSED_ROWS, name="mla_prep_fwd")


def _mla_prep_bwd(proj_b, pos, d_qc, d_kc, d_v, qn_w, kvn_w, uq, uk, uv):
    def fn(r, c):
        cq, ckv, _, pos_, dq, dk, dv = r
        qn_w_, kvn_w_, uq_, uk_, uv_, rope = c
        even = lambda t: jnp.concatenate([t[:, (2 * h) * LANES:(2 * h + 1) * LANES] for h in range(N_HEADS)], axis=1)
        odd = lambda t: jnp.concatenate([t[:, (2 * h + 1) * LANES:(2 * h + 2) * LANES] for h in range(N_HEADS)], axis=1)
        d_qr_raw = _rope_transpose(odd(dq), _rope_tables(pos_, rope, D_MODEL)) * SCALE
        d_qf = jnp.concatenate([even(dq) * SCALE, d_qr_raw], axis=1).astype(BF16)
        d_kn = even(dk).astype(BF16)
        dkr = dk[:, LANES:2 * LANES]
        for h in range(1, N_HEADS):
            dkr = dkr + dk[:, (2 * h + 1) * LANES:(2 * h + 2) * LANES]
        d_cq, d_qnw = _rms_bwd(cq, qn_w_, _dot(d_qf, uq_, NT))
        d_ckv, d_kvnw = _rms_bwd(ckv, kvn_w_, _dot(d_kn, uk_, NT) + _dot(dv, uv_, NT))
        return [d_qf, d_kn, d_cq, d_ckv, _rope_transpose(dkr, _rope_tables(pos_, rope, LANES))], [d_qnw, d_kvnw]

    return _rowwise(fn, _mla_rows(proj_b) + [pos, d_qc, d_kc, d_v], [qn_w, kvn_w, uq, uk, uv, _rope_consts()],
                    [(2 * D_MODEL, BF16), (D_MODEL, BF16), (Q_LORA, BF16), (KV_LORA, BF16), (LANES, BF16)],
                    accs=[(1, Q_LORA), (1, KV_LORA)], tm=FUSED_ROWS, name="mla_prep_bwd")


def _causal_mask_t(st, key0, query0):
    key = lax.broadcasted_iota(jnp.int32, st.shape, 0) + key0
    query = lax.broadcasted_iota(jnp.int32, st.shape, 1) + query0
    return jnp.where(key <= query, st, NEG_BIG)


def _attn_tiles(s_dim):
    tq = min(512, s_dim)
    n_chains = 2 if s_dim >= 2 * tq else 1
    return tq, n_chains, min(512, s_dim)


def _diagonal_chains(t, tq, n_chains, tk):
    return [(c, (t + 1) * tk - 1 > c * tq) for c in range(n_chains) if t * tk < (c + 1) * tq]


def _attn_fwd(qc, kc, vt):
    s_dim = qc.shape[0]
    tq, n_chains, tk = _attn_tiles(s_dim)
    tqs = tq * n_chains

    def body(q_ref, k_ref, vt_ref, o_ref, lse_ref, m_s, l_s, acc):
        qi = pl.program_id(1)
        m_s[...] = jnp.full_like(m_s, NEG_BIG)
        l_s[...] = jnp.zeros_like(l_s)
        acc[...] = jnp.zeros_like(acc)

        def make_step(chains):
            def step(j, carry):
                ks = pl.multiple_of(j * tk, tk)
                kb, vtb = k_ref[pl.ds(ks, tk), :], vt_ref[:, pl.ds(ks, tk)]
                cols = [slice(c * tq, (c + 1) * tq) for c, _ in chains]
                sts = [_dot(kb, q_ref[cs, :], NT) for cs in cols]
                sts = [_causal_mask_t(st, j * tk, qi * tqs + c * tq) if masked else st
                       for st, (c, masked) in zip(sts, chains)]
                m_prevs = [m_s[:, cs] for cs in cols]
                m_news = [jnp.maximum(mp, jnp.max(st, axis=0, keepdims=True)) for mp, st in zip(m_prevs, sts)]
                alphas = [jnp.exp(mp - mn) for mp, mn in zip(m_prevs, m_news)]
                pts = [jnp.exp(st - mn) for st, mn in zip(sts, m_news)]
                pvs = [_dot(vtb, pt) for pt in pts]
                for cs, mn, al, pt, pv in zip(cols, m_news, alphas, pts, pvs):
                    l_s[:, cs] = al * l_s[:, cs] + _colsum(pt)
                    m_s[:, cs] = mn
                    acc[:, cs] = acc[:, cs] * al + pv
                return carry
            return step

        below = qi * (tqs // tk)
        lax.fori_loop(0, below, make_step([(c, False) for c in range(n_chains)]), 0)
        for t in range(tqs // tk):
            make_step(_diagonal_chains(t, tq, n_chains, tk))(below + t, 0)
        l = l_s[...]
        o_ref[...] = jnp.transpose(acc[...] / l)
        lse_ref[...] = m_s[...] + jnp.log(l)

    return pl.pallas_call(
        body,
        out_shape=[jax.ShapeDtypeStruct((s_dim, N_HEADS * HEAD), F32), jax.ShapeDtypeStruct((N_HEADS, 1, s_dim), F32)],
        grid=(N_HEADS, s_dim // tqs),
        in_specs=[pl.BlockSpec((tqs, QK_HEAD), lambda h, qi: (qi, h)),
                  pl.BlockSpec((s_dim, QK_HEAD), lambda h, qi: (0, h)),
                  pl.BlockSpec((HEAD, s_dim), lambda h, qi: (h, 0))],
        out_specs=[pl.BlockSpec((tqs, HEAD), lambda h, qi: (qi, h)),
                   pl.BlockSpec((None, 1, tqs), lambda h, qi: (h, 0, qi))],
        scratch_shapes=[pltpu.VMEM((1, tqs), F32), pltpu.VMEM((1, tqs), F32), pltpu.VMEM((HEAD, tqs), F32)],
        compiler_params=pltpu.CompilerParams(dimension_semantics=("parallel", "parallel")),
        name="attn_fwd",
    )(qc, kc, vt)


def _attn_bwd(qc, kc, kct, v, o, d_o, lse):
    s_dim = qc.shape[0]
    tq, n_chains, tk = _attn_tiles(s_dim)
    tqs = tq * n_chains

    def body(q_ref, k_ref, kt_ref, v_ref, o_ref, do_ref, lse_ref, dq_ref, dk_ref, dv_ref, dqt_acc, dv_acc):
        qi = pl.program_id(1)

        @pl.when(qi == 0)
        def _():
            dk_ref[...] = jnp.zeros_like(dk_ref)
            dv_acc[...] = jnp.zeros_like(dv_acc)

        dqt_acc[...] = jnp.zeros_like(dqt_acc)
        do_f = do_ref[...]
        do_all = do_f.astype(BF16)
        q_all = q_ref[...]
        lse_row = lse_ref[...]
        delta_row = _dot3(jnp.ones((8, HEAD), F32), o_ref[...] * do_f, NT)[0:1, :]

        def make_step(chains):
            rows = slice(chains[0][0] * tq, (chains[-1][0] + 1) * tq)

            def step(j, carry):
                ks = pl.multiple_of(j * tk, tk)
                kb, vb, ktb = k_ref[pl.ds(ks, tk), :], v_ref[pl.ds(ks, tk), :], kt_ref[:, pl.ds(ks, tk)]
                cols = [slice(c * tq, (c + 1) * tq) for c, _ in chains]
                sts = [_dot(kb, q_all[cs, :], NT) for cs in cols]
                sts = [_causal_mask_t(st, j * tk, qi * tqs + c * tq) if masked else st
                       for st, (c, masked) in zip(sts, chains)]
                dpts = [_dot(vb, do_all[cs, :], NT) for cs in cols]
                pts = [jnp.exp(st - lse_row[:, cs]) for st, cs in zip(sts, cols)]
                dsts = [(pt * (dpt - delta_row[:, cs])).astype(BF16) for pt, dpt, cs in zip(pts, dpts, cols)]
                pts = [pt.astype(BF16) for pt in pts]
                dqs = [_dot(ktb, dst) for dst in dsts]
                for cs, dq in zip(cols, dqs):
                    dqt_acc[:, cs] += dq
                pt_all = jnp.concatenate(pts, axis=1) if len(chains) > 1 else pts[0]
                dst_all = jnp.concatenate(dsts, axis=1) if len(chains) > 1 else dsts[0]
                dk_ref[pl.ds(ks, tk), :] += _dot(dst_all, q_all[rows, :])
                dv_acc[pl.ds(ks, tk), :] += _dot(pt_all, do_all[rows, :])
                return carry
            return step

        below = qi * (tqs // tk)
        lax.fori_loop(0, below, make_step([(c, False) for c in range(n_chains)]), 0)
        for t in range(tqs // tk):
            make_step(_diagonal_chains(t, tq, n_chains, tk))(below + t, 0)
        dq_ref[...] = jnp.transpose(dqt_acc[...])

        @pl.when(qi == s_dim // tqs - 1)
        def _():
            dv_ref[...] = dv_acc[...].astype(dv_ref.dtype)

    q_spec = pl.BlockSpec((tqs, QK_HEAD), lambda h, qi: (qi, h))
    o_spec = pl.BlockSpec((tqs, HEAD), lambda h, qi: (qi, h))
    k_spec = pl.BlockSpec((s_dim, QK_HEAD), lambda h, qi: (0, h))
    v_spec = pl.BlockSpec((s_dim, HEAD), lambda h, qi: (0, h))
    wide2 = jax.ShapeDtypeStruct((s_dim, N_HEADS * QK_HEAD), F32)
    return pl.pallas_call(
        body,
        out_shape=[wide2, wide2, jax.ShapeDtypeStruct((s_dim, N_HEADS * HEAD), BF16)],
        grid=(N_HEADS, s_dim // tqs),
        in_specs=[q_spec, k_spec, pl.BlockSpec((QK_HEAD, s_dim), lambda h, qi: (h, 0)), v_spec, o_spec, o_spec,
                  pl.BlockSpec((None, 1, tqs), lambda h, qi: (h, 0, qi))],
        out_specs=[q_spec, k_spec, v_spec],
        scratch_shapes=[pltpu.VMEM((QK_HEAD, tqs), F32), pltpu.VMEM((s_dim, HEAD), F32)],
        compiler_params=pltpu.CompilerParams(dimension_semantics=("parallel", "arbitrary")),
        name="attn_bwd",
    )(qc, kc, kct, v, o, d_o, lse)


def _mix_proj_ln1(y_dn, y_mla, proj_g, x, w_o, g, b):
    s_dim = x.shape[0]
    tm = min(512, s_dim)

    def body(yd_ref, ym_ref, g_ref, x_ref, w_ref, lg_ref, lb_ref, mixed_ref, a1_ref, h1_ref, h1b_ref):
        gates = g_ref[...].astype(F32)
        mixed = (_sig(gates[:, :D_MODEL]) * yd_ref[...].astype(F32)
                 + _sig(gates[:, D_MODEL:]) * ym_ref[...].astype(F32)).astype(BF16)
        a1 = _dot(mixed, w_ref[...])
        xh, _ = _ln_stats(ALPHA * x_ref[...] + a1)
        y = xh * lg_ref[...] + lb_ref[...]
        mixed_ref[...] = mixed
        a1_ref[...] = a1
        h1_ref[...] = y
        h1b_ref[...] = y.astype(BF16)

    row = lambda width: pl.BlockSpec((tm, width), lambda i: (i, 0))
    whole = lambda a: pl.BlockSpec(a.shape, lambda i: (0, 0))
    sds = lambda dt: jax.ShapeDtypeStruct((s_dim, D_MODEL), dt)
    return pl.pallas_call(
        body,
        out_shape=[sds(BF16), sds(F32), sds(F32), sds(BF16)],
        grid=(s_dim // tm,),
        in_specs=[row(D_MODEL), row(D_MODEL), row(2 * D_MODEL), row(D_MODEL), whole(w_o), whole(g), whole(b)],
        out_specs=[row(D_MODEL)] * 4,
        compiler_params=pltpu.CompilerParams(dimension_semantics=("parallel",)),
        name="mix_proj_ln1",
    )(y_dn, y_mla, proj_g, x, w_o, g, b)


def _ln1_mix_bwd(x, a1, d_h1, d_pg, y_dn, y_mla, proj_g, g, w_o, w_pg):
    def fn(r, c):
        x_, a1_, dy, dpg, yd, ym, gates = r
        g_, w_o_, w_pg_ = c
        dy = dy + _dot(dpg, w_pg_, NT)
        xh, rr = _ln_stats(ALPHA * x_ + a1_)
        dz = _ln_bwd(dy, xh, rr, g_)
        dz_b = dz.astype(BF16)
        dm = _dot(dz_b, w_o_, NT)
        sd, sm = _sig(gates[:, :D_MODEL]), _sig(gates[:, D_MODEL:])
        d_g = jnp.concatenate([dm * yd * sd * (1.0 - sd), dm * ym * sm * (1.0 - sm)], axis=1)
        return [dz_b, ALPHA * dz, d_g, dm * sd, dm * sm], [_colsum(dy * xh), _colsum(dy)]

    return _rowwise(fn, [x, a1, d_h1, d_pg, y_dn, y_mla, proj_g], [g, w_o, w_pg],
                    [(D_MODEL, BF16), (D_MODEL, F32), (2 * D_MODEL, BF16), (D_MODEL, BF16), (D_MODEL, BF16)],
                    accs=[(1, D_MODEL), (1, D_MODEL)], tm=FUSED_ROWS, name="ln1_mix_bwd")


def _ln_stats(z):
    mu = _rowmean(z)
    zc = z - mu
    r = lax.rsqrt(_rowmean(zc * zc) + EPS_LN)
    return zc * r, r


def _ln_bwd(dy, xh, r, g):
    dxh = dy * g
    return r * (dxh - _rowmean(dxh) - xh * _rowmean(dxh * xh))


def _ffn_in_act(h1b, w_t):
    s_dim, k_dim = h1b.shape
    hidden = w_t.shape[0] // 2
    tm, tn = min(512, s_dim), _pick_wide(hidden)
    nt = hidden // tn

    def body(a_ref, bg_ref, bu_ref, gt_ref, up_ref, act_ref):
        a = a_ref[...]
        gt, up = _dot(a, bg_ref[...], NT), _dot(a, bu_ref[...], NT)
        gt_ref[...] = gt.astype(BF16)
        up_ref[...] = up.astype(BF16)
        act_ref[...] = (gt * _sig(gt) * up).astype(BF16)

    o_spec = pl.BlockSpec((tm, tn), lambda j, i: (i, j))
    sds = jax.ShapeDtypeStruct((s_dim, hidden), BF16)
    return pl.pallas_call(
        body,
        out_shape=[sds, sds, sds],
        grid=(nt, s_dim // tm),
        in_specs=[pl.BlockSpec((tm, k_dim), lambda j, i: (i, 0)), pl.BlockSpec((tn, k_dim), lambda j, i: (j, 0)),
                  pl.BlockSpec((tn, k_dim), lambda j, i: (j + nt, 0))],
        out_specs=[o_spec, o_spec, o_spec],
        compiler_params=pltpu.CompilerParams(dimension_semantics=("parallel", "parallel")),
        name="ffn_in_act",
    )(h1b, w_t, w_t)


def _act_bwd(gt, up, d_act):
    def fn(r, c):
        gt_, up_, da = r
        return [jnp.concatenate([da * up_ * _silu_grad(gt_), da * gt_ * _sig(gt_)], axis=1)], []

    return _rowwise(fn, [gt, up, d_act], [], [(2 * FFN_HIDDEN, BF16)], name="act_bwd")[0]


def _tail(h1, ffn, p, tgt, g, b, w_pg, w_ple_t):
    def fn(r, c):
        h1_, ffn_, p_, t_ = r
        pg_ = _dot(h1_, c[2])
        pp_ = _dot(p_, c[3], NT)
        sp = _sig(pg_)
        xh, rr = _ln_stats(ALPHA * h1_ + ffn_ + sp * pp_)
        y = xh * c[0] + c[1]
        err = y - t_
        dy = err * (1.0 / D_MODEL)
        dz = _ln_bwd(dy, xh, rr, c[0])
        loss = jnp.sum(0.5 * _rowmean(err * err), axis=0, keepdims=True)
        return ([dz, dz * pp_ * sp * (1.0 - sp), dz * sp, ALPHA * dz],
                [_colsum(dy * xh), _colsum(dy), jnp.broadcast_to(loss, (1, LANES))])

    return _rowwise(fn, [h1, ffn, p, tgt], [g, b, w_pg, w_ple_t], [(D_MODEL, BF16)] * 3 + [(D_MODEL, F32)],
                    accs=[(1, D_MODEL), (1, D_MODEL), (1, LANES)], tm=FUSED_ROWS, name="tail")


def _local_step(x, p, pos, tgt, w, late_weights, emit):
    w = dict(w)
    s_dim = x.shape[0]
    xb, pb = x.astype(BF16), p.astype(BF16)
    proj_a = _mm(xb, w["wa_t"], tb=True, name="f_proj_a")
    proj_g = _mm(xb, w["wg_t"], tb=True, out_dtype=BF16, name="f_proj_g")
    proj_b = _mm(xb, w["wb_t"], tb=True, name="f_proj_b")
    qkvn = _conv_fwd(proj_a, w["conv"])
    beta, gc = _gates_fwd(proj_b, w["alog"], w["dtb"])
    gc_t = jnp.transpose(gc[:, :N_HEADS])
    u, w_, qd, kt, a_mat, t_fold = _gdr_prep_fwd(qkvn, beta, gc, gc_t)
    o_dn, states = _gdr_scan_fwd(u, w_, qd, kt, a_mat, gc)
    w.update(late_weights("mix", o_dn))
    og, y_dn = _gdr_out_fwd(o_dn, proj_a, w["dnw"], w["br_dn"])
    c_q, c_kv, qc, kc, vv, kct, vt = _mla_prep_fwd(proj_b, pos, w["qnw"], w["kvnw"], w["uq"], w["uk"], w["uv"])
    o_mla, lse = _attn_fwd(qc, kc, vt)
    y_mla = _mm(o_mla, w["br_mla"], out_dtype=BF16, name="f_y_mla")
    mixed, a1, h1, h1b = _mix_proj_ln1(y_dn, y_mla, proj_g, x, w["wo"], w["ln1g"], w["ln1b"])
    w.update(late_weights("ffn", a1))
    gt, up, act = _ffn_in_act(h1b, w["ffn_in_t"])
    ffn = _mm(act, w["ffn_out"], name="f_ffn")
    g = {}
    dz2, d_pg, d_pp, dh1a, g["ln2g"], g["ln2b"], loss = _tail(h1, ffn, pb, tgt, w["ln2g"], w["ln2b"],
                                                            w["ple_gate"], w["ple_t"])
    g["ple_t"] = _mm(d_pp, pb, ta=True, out_dtype=BF16, name="b_w_ple")
    g["ple_gate"] = _mm(h1b, d_pg, ta=True, out_dtype=BF16, name="b_w_ple_gate")
    g["ffn_out"] = _mm(act, dz2, ta=True, out_dtype=BF16, name="b_w_ffn_out")
    d_act = _mm(dz2, w["ffn_out"], tb=True, out_dtype=BF16, name="b_act")
    d_gu = _act_bwd(gt, up, d_act)
    g["ffn_in_t"] = _mm(d_gu, h1b, ta=True, out_dtype=BF16, name="b_w_ffn_in")
    d_gu = emit("ffn", g, d_gu)
    d_h1 = _mm(d_gu, w["ffn_in_t"], add=(dh1a,), name="b_h1_ffn")
    dz1, dxa, d_proj_g, d_y_dn, d_y_mla, g["ln1g"], g["ln1b"] = _ln1_mix_bwd(
        x, a1, d_h1, d_pg, y_dn, y_mla, proj_g, w["ln1g"], w["wo"], w["ple_gate"])
    g["wo"] = _mm(mixed, dz1, ta=True, out_dtype=BF16, name="b_w_o")
    g["br_mla"] = _mm(o_mla, d_y_mla, ta=True, out_dtype=BF16, name="b_w_br_mla")
    d_o_mla = _mm(d_y_mla, w["br_mla"], tb=True, out_dtype=BF16, name="b_o_mla")
    d_qc, d_kc, d_v = _attn_bwd(qc, kc, kct, vv, o_mla, d_o_mla, lse)
    d_q_full, d_kn, d_cq, d_ckv, d_kr, g["qnw"], g["kvnw"] = _mla_prep_bwd(
        proj_b, pos, d_qc, d_kc, d_v, w["qnw"], w["kvnw"], w["uq"], w["uk"], w["uv"])
    g["uq"] = _mm(c_q, d_q_full, ta=True, out_dtype=BF16, name="b_w_uq")
    g["uk"] = _mm(c_kv, d_kn, ta=True, out_dtype=BF16, name="b_w_uk")
    g["uv"] = _mm(c_kv, d_v, ta=True, out_dtype=BF16, name="b_w_uv")
    g["br_dn"] = _mm(og, d_y_dn, ta=True, out_dtype=BF16, name="b_w_br_dn")
    d_y_dn = emit("mix", g, d_y_dn)
    d_o_dn, d_z, g["dnw"] = _gdr_out_bwd(o_dn, proj_a, d_y_dn, w["dnw"], w["br_dn"])
    du, dw, dqd, dkt, d_a, d_egl = _gdr_scan_bwd(u, w_, qd, kt, a_mat, gc, states, d_o_dn)
    dq, dk, dv, d_beta, d_gc = _gdr_prep_bwd(qkvn, beta, gc, gc_t, t_fold, u, w_, du, dw, dqd, dkt, d_a)
    d_egl_rows = jnp.pad(d_egl[:, None, :, 0], ((0, 0), (CHUNK - 1, 0), (0, LANES - N_HEADS))).reshape(s_dim, LANES)
    d_ba, g["alog"], g["dtb"] = _gates_bwd(proj_b, w["alog"], w["dtb"], gc, d_beta, d_gc, d_egl_rows)
    d_qkv, g["conv"] = _conv_bwd(proj_a, w["conv"], dq, dk, dv)
    zeros = jnp.zeros((s_dim, WB_CKV - Q_LORA), BF16)
    d_proj_b = jnp.concatenate([d_cq, zeros, d_ckv, d_kr, d_ba], axis=1)
    g["wa_qkv_t"] = _mm(d_qkv, xb, ta=True, name="b_w_qkv")
    g["wa_z_t"] = _mm(d_z, xb, ta=True, name="b_w_z")
    g["wg_t"] = _mm(d_proj_g, xb, ta=True, name="b_w_g")
    g["wb_t"] = _mm(d_proj_b, xb, ta=True, name="b_w_b")
    dx = _mm(d_qkv, w["wa_qkv_t"], add=(dxa,), name="b_x_qkv")
    dx = _mm(d_z, w["wa_z_t"], add=(dx,), name="b_x_z")
    dx = _mm(d_proj_g, w["wg_t"], add=(dx,), name="b_x_g")
    dx = _mm(d_proj_b, w["wb_t"], add=(dx,), name="b_x_b")
    return loss, dx, g


_BIG = (("w_in", 1), ("w_uq", 0), ("w_uk", 0), ("w_uv", 0), ("w_br_dn", 0), ("w_br_mla", 0),
        ("w_o", 0), ("w_ffn_in", 1), ("w_ffn_out", 0), ("w_ple", 1), ("w_ple_gate", 0))
_BIG_AXIS = dict(_BIG)
_SMALL = ("ln1_g", "ln1_b", "ln2_g", "ln2_b", "q_norm_w", "kv_norm_w", "dn_norm_w", "dn_a_log", "dn_dt_bias")
_ORDER = ("w_in", "conv_w", "dn_a_log", "dn_dt_bias", "dn_norm_w", "q_norm_w", "w_uq", "kv_norm_w", "w_uk", "w_uv",
          "w_br_dn", "w_br_mla", "w_o", "ln1_g", "ln1_b", "w_ffn_in", "w_ffn_out", "w_ple", "w_ple_gate", "ln2_g",
          "ln2_b")


def _stored_shape(name, shard_shape):
    axis = _BIG_AXIS[name]
    lead = shard_shape[axis]
    return lead, int(np.prod(shard_shape)) // lead


def _to_stored(name, shard):
    return jnp.moveaxis(shard, _BIG_AXIS[name], 0).reshape(_stored_shape(name, shard.shape))


def _from_stored(name, stored, shard_shape):
    axis = _BIG_AXIS[name]
    moved = (shard_shape[axis],) + shard_shape[:axis] + shard_shape[axis + 1:]
    return jnp.moveaxis(stored.reshape(moved), 0, axis)


_W_IN_ROWS = np.cumsum([0, 3072, 1024, 8, 8, Q_LORA, KV_LORA, ROPE, D_MODEL, D_MODEL])


def _first_weights(w_in_t, conv_full, small):
    r = _W_IN_ROWS
    zr = lambda n: jnp.zeros((n, D_MODEL), w_in_t.dtype)
    w = {}
    w["wa_t"] = w_in_t[r[0]:r[2]]
    w["wa_qkv_t"], w["wa_z_t"] = w_in_t[r[0]:r[1]], w_in_t[r[1]:r[2]]
    w["wg_t"] = w_in_t[r[7]:r[9]]
    w["wb_t"] = jnp.concatenate([w_in_t[r[4]:r[5]], zr(WB_CKV - Q_LORA), w_in_t[r[5]:r[7]], zr(LANES - ROPE),
                                 w_in_t[r[2]:r[4]], zr(LANES - 2 * N_HEADS)], axis=0)
    w["conv"] = conv_full
    pad_l = lambda v: jnp.pad(v, ((0, 0), (0, LANES - v.shape[1])))
    w["alog"], w["dtb"] = pad_l(small["dn_a_log"]), pad_l(small["dn_dt_bias"])
    w["dnw"], w["qnw"], w["kvnw"] = small["dn_norm_w"], small["q_norm_w"], small["kv_norm_w"]
    w["ln1g"], w["ln1b"], w["ln2g"], w["ln2b"] = small["ln1_g"], small["ln1_b"], small["ln2_g"], small["ln2_b"]
    return w


def _late_weights(group, fw):
    w = {}
    if group == "mix":
        uq = fw["w_uq"].reshape(Q_LORA, N_HEADS, HEAD + ROPE)
        uq_r = jnp.pad(uq[:, :, HEAD:], ((0, 0), (0, 0), (0, HEAD - ROPE)))
        w["uq"] = jnp.concatenate([uq[:, :, :HEAD].reshape(Q_LORA, -1), uq_r.reshape(Q_LORA, -1)], axis=1)
        w["uk"], w["uv"] = fw["w_uk"], fw["w_uv"]
        w["br_dn"], w["br_mla"], w["wo"] = fw["w_br_dn"], fw["w_br_mla"], fw["w_o"]
    else:
        w["ffn_in_t"], w["ffn_out"] = fw["w_ffn_in"], fw["w_ffn_out"]
        w["ple_t"], w["ple_gate"] = fw["w_ple"], fw["w_ple_gate"]
    return w


_GROUP_GRADS = {"ffn": (("w_ple", "ple_t"), ("w_ple_gate", "ple_gate"), ("w_ffn_out", "ffn_out"),
                        ("w_ffn_in", "ffn_in_t")),
                "mix": (("w_o", "wo"), ("w_br_mla", "br_mla"), ("w_uq", "uq"), ("w_uk", "uk"), ("w_uv", "uv"),
                        ("w_br_dn", "br_dn"))}


def _group_grads(group, g):
    out = {}
    for name, key in _GROUP_GRADS[group]:
        t = g[key]
        if name == "w_uq":
            uq_n = t[:, :D_MODEL].reshape(Q_LORA, N_HEADS, HEAD)
            uq_r = t[:, D_MODEL:].reshape(Q_LORA, N_HEADS, HEAD)[:, :, :ROPE]
            t = jnp.concatenate([uq_n, uq_r], axis=2).reshape(Q_LORA, -1)
        out[name] = t
    return out


def _last_grads(g):
    wb = g["wb_t"]
    w_in = jnp.concatenate([
        g["wa_qkv_t"], g["wa_z_t"], wb[WB_BA:WB_BA + 2 * N_HEADS], wb[WB_CQ:WB_CQ + Q_LORA],
        wb[WB_CKV:WB_CKV + KV_LORA], wb[WB_KR:WB_KR + ROPE], g["wg_t"]], axis=0)
    small = {"ln1_g": g["ln1g"], "ln1_b": g["ln1b"], "ln2_g": g["ln2g"], "ln2_b": g["ln2b"], "q_norm_w": g["qnw"],
             "kv_norm_w": g["kvnw"], "dn_norm_w": g["dnw"], "dn_a_log": g["alog"], "dn_dt_bias": g["dtb"],
             "conv_w": g["conv"]}
    return w_in, small


_SMALL_SLOTS = {"ln1_g": (0, 0, 1024), "ln1_b": (1, 0, 1024), "ln2_g": (2, 0, 1024), "ln2_b": (3, 0, 1024),
                "q_norm_w": (4, 0, 384), "kv_norm_w": (4, 384, 256), "dn_norm_w": (4, 640, 128),
                "dn_a_log": (4, 768, 8), "dn_dt_bias": (4, 896, 8)}
_SMALL_ROWS, _LOSS_ROW, _CONV_ROW0, _CONV_ROWS = 24, 5, 8, 12


def _pack_small_grads(small_g, loss):
    zeros = lambda r, c: jnp.zeros((r, c), F32)
    row4 = jnp.concatenate([small_g["q_norm_w"], small_g["kv_norm_w"], small_g["dn_norm_w"], small_g["dn_a_log"],
                            small_g["dn_dt_bias"]], axis=1)
    row5 = jnp.concatenate([loss, zeros(1, FLAT_COLS - LANES)], axis=1)
    head = jnp.concatenate([small_g["ln1_g"], small_g["ln1_b"], small_g["ln2_g"], small_g["ln2_b"], row4, row5,
                            zeros(2, FLAT_COLS)], axis=0)
    conv = small_g["conv_w"].reshape(_CONV_ROWS, FLAT_COLS)
    return jnp.concatenate([head, conv, zeros(_SMALL_ROWS - _CONV_ROW0 - _CONV_ROWS, FLAT_COLS)], axis=0)


_MESH_ID = pl.DeviceIdType.MESH
_ANY = pl.BlockSpec(memory_space=pl.ANY)


def _all_gather(blocks, name):
    n = len(blocks)

    def body(*refs):
        x_refs, out_refs = refs[:n], refs[n:2 * n]
        send_sems, recv_sems, local_sems = refs[2 * n:]
        x, y, c = lax.axis_index("x"), lax.axis_index("y"), lax.axis_index("c")
        me, sibling = (x, y, c), (x, y, 1 - c)
        chips = [(1 - x, y), (x, 1 - y), (1 - x, 1 - y)]

        def slot(i, px, py, pc):
            return out_refs[i].at[4 * px + 2 * py + pc]

        def copy(i, k, origin, to, src=None):
            return pltpu.make_async_remote_copy(
                src_ref=slot(i, *origin) if src is None else src, dst_ref=slot(i, *origin),
                send_sem=send_sems.at[7 * i + k], recv_sem=recv_sems.at[7 * i + k], device_id=to,
                device_id_type=_MESH_ID)

        mine = [pltpu.make_async_copy(x_refs[i], slot(i, *me), local_sems.at[i]) for i in range(n)]
        first, passed = [], []
        for i in range(n):
            mine[i].start()
            first.append(copy(i, 0, me, sibling, src=x_refs[i]))
            first += [copy(i, 1 + j, me, (*chip, c), src=x_refs[i]) for j, chip in enumerate(chips)]
        for cp in first:
            cp.start()
        for i in range(n):
            for j, chip in enumerate(chips):
                copy(i, 1 + j, (*chip, c), me).wait_recv()
                passed.append(copy(i, 4 + j, (*chip, c), sibling))
                passed[-1].start()
        for i in range(n):
            copy(i, 0, sibling, me).wait_recv()
            for j, chip in enumerate(chips):
                copy(i, 4 + j, (*chip, 1 - c), me).wait_recv()
        for cp in first + passed:
            cp.wait_send()
        for cp in mine:
            cp.wait()

    return pl.pallas_call(
        body,
        out_shape=[jax.ShapeDtypeStruct((N_DEV,) + b.shape, b.dtype) for b in blocks],
        in_specs=[_ANY] * n,
        out_specs=[_ANY] * n,
        scratch_shapes=[pltpu.SemaphoreType.DMA((7 * n,)), pltpu.SemaphoreType.DMA((7 * n,)),
                        pltpu.SemaphoreType.DMA((n,))],
        name=name,
    )(*blocks)


def _exchange_sibling(srcs, name):
    n = len(srcs)

    def body(*refs):
        src_refs, dst_refs = refs[:n], refs[n:2 * n]
        send_sems, recv_sems = refs[2 * n:]
        x, y, c = lax.axis_index("x"), lax.axis_index("y"), lax.axis_index("c")
        copies = [pltpu.make_async_remote_copy(
            src_ref=src_refs[i].at[2 * q + (1 - c)], dst_ref=dst_refs[i].at[q], send_sem=send_sems.at[4 * i + q],
            recv_sem=recv_sems.at[4 * i + q], device_id=(x, y, 1 - c), device_id_type=_MESH_ID)
            for i in range(n) for q in range(4)]
        for cp in copies:
            cp.start()
        for cp in copies:
            cp.wait_recv()
        for cp in copies:
            cp.wait_send()

    return pl.pallas_call(
        body,
        out_shape=[jax.ShapeDtypeStruct((4,) + s.shape[1:], s.dtype) for s in srcs],
        in_specs=[_ANY] * n,
        out_specs=[_ANY] * n,
        scratch_shapes=[pltpu.SemaphoreType.DMA((4 * n,)), pltpu.SemaphoreType.DMA((4 * n,))],
        name=name,
    )(*srcs)


def _exchange_chips(srcs, name):
    n = len(srcs)

    def body(*refs):
        src_refs, dst_refs = refs[:n], refs[n:2 * n]
        send_sems, recv_sems = refs[2 * n:]
        x, y, c = lax.axis_index("x"), lax.axis_index("y"), lax.axis_index("c")
        chips = [(1 - x, y), (x, 1 - y), (1 - x, 1 - y)]
        copies = [pltpu.make_async_remote_copy(
            src_ref=src_refs[i].at[2 * tx + ty], dst_ref=dst_refs[i].at[j], send_sem=send_sems.at[3 * i + j],
            recv_sem=recv_sems.at[3 * i + j], device_id=(tx, ty, c), device_id_type=_MESH_ID)
            for i in range(n) for j, (tx, ty) in enumerate(chips)]
        for cp in copies:
            cp.start()
        for cp in copies:
            cp.wait_recv()
        for cp in copies:
            cp.wait_send()

    return pl.pallas_call(
        body,
        out_shape=[jax.ShapeDtypeStruct((3,) + s.shape[1:], s.dtype) for s in srcs],
        in_specs=[_ANY] * n,
        out_specs=[_ANY] * n,
        scratch_shapes=[pltpu.SemaphoreType.DMA((3 * n,)), pltpu.SemaphoreType.DMA((3 * n,))],
        name=name,
    )(*srcs)


def _col_tile(c):
    return c if c <= 256 else 256


def _chip_sum(src, recv, parity, name):
    _, r, c = src.shape
    tc = _col_tile(c)

    def body(par_ref, a_ref, b_ref, o_ref, ob_ref):
        s = a_ref[...] + b_ref[...]
        o_ref[...] = s
        ob_ref[...] = s.astype(BF16)

    blk = lambda f: pl.BlockSpec((None, r, tc), f)
    return pl.pallas_call(
        body,
        out_shape=[jax.ShapeDtypeStruct((4, r, c), F32), jax.ShapeDtypeStruct((4, r, c), BF16)],
        grid_spec=pltpu.PrefetchScalarGridSpec(
            num_scalar_prefetch=1, grid=(4, c // tc),
            in_specs=[blk(lambda q, j, par: (2 * q + par[0], 0, j)), blk(lambda q, j, par: (q, 0, j))],
            out_specs=[blk(lambda q, j, par: (q, 0, j)), blk(lambda q, j, par: (q, 0, j))]),
        compiler_params=pltpu.CompilerParams(dimension_semantics=("parallel", "parallel")),
        name=name,
    )(parity, src, recv)


def _sum_parts(own, others, chip, name):
    _, r, c = own.shape
    tc = _col_tile(c)

    def body(q_ref, a_ref, b_ref, o_ref):
        o_ref[...] = ((a_ref[...] + b_ref[0].astype(F32)) + b_ref[1].astype(F32)) + b_ref[2].astype(F32)

    return pl.pallas_call(
        body,
        out_shape=jax.ShapeDtypeStruct((r, c), F32),
        grid_spec=pltpu.PrefetchScalarGridSpec(
            num_scalar_prefetch=1, grid=(c // tc,),
            in_specs=[pl.BlockSpec((None, r, tc), lambda j, q: (q[0], 0, j)),
                      pl.BlockSpec((3, r, tc), lambda j, q: (0, 0, j))],
            out_specs=pl.BlockSpec((r, tc), lambda j, q: (0, j))),
        compiler_params=pltpu.CompilerParams(dimension_semantics=("parallel",)),
        name=name,
    )(chip, own, others)


_HBM = pl.BlockSpec(memory_space=pltpu.HBM)
_SEM = pl.BlockSpec(memory_space=pltpu.SEMAPHORE)
_DATAFLOW = pltpu.SideEffectType.DATAFLOW_SIDE_EFFECTING
N_PEERS = N_DEV - 1


def _ring_peer(j):
    me = 4 * lax.axis_index("x") + 2 * lax.axis_index("y") + lax.axis_index("c")
    k = (me + j) % N_DEV
    return me, k, (k // 4, (k // 2) % 2, k % 2)


def _spread_copy(i, j, src_refs, land_refs, send_sems, recv_sems, scatter):
    me, k, peer = _ring_peer(j)
    return pltpu.make_async_remote_copy(
        src_ref=src_refs[i].at[k] if scatter else src_refs[i], dst_ref=land_refs[i].at[me],
        send_sem=send_sems.at[N_PEERS * i + j - 1], recv_sem=recv_sems.at[N_PEERS * i + j - 1], device_id=peer,
        device_id_type=_MESH_ID)


def _spread_start(srcs, carry, scatter, name):
    n = len(srcs)
    lands = [lax.empty(((N_DEV,) + s.shape[-2:]), s.dtype) for s in srcs]

    def body(*refs):
        src_refs, land_refs = refs[:n], refs[n:2 * n]
        send_sems, recv_sems, local_sems = refs[2 * n + 1:2 * n + 4]
        for i in range(n):
            for j in range(1, N_DEV):
                _spread_copy(i, j, src_refs, land_refs, send_sems, recv_sems, scatter).start()
        for i in range(n):
            _own_copy(i, src_refs, land_refs, local_sems, scatter).start()

    hbm = lambda a: pltpu.HBM(a.shape, a.dtype)
    sems = pltpu.SemaphoreType.DMA((N_PEERS * n,))
    pinned = [pltpu.with_memory_space_constraint(a, pltpu.HBM) for a in list(srcs) + lands + [carry]]
    res = pl.pallas_call(
        body, name=name,
        out_shape=(sems, sems, pltpu.SemaphoreType.DMA((n,)), *[hbm(a) for a in pinned]),
        in_specs=[_HBM] * (2 * n + 1),
        out_specs=(_SEM, _SEM, _SEM, *[_HBM] * (2 * n + 1)),
        input_output_aliases={i: 3 + i for i in range(2 * n + 1)},
        compiler_params=pltpu.CompilerParams(has_side_effects=_DATAFLOW),
    )(*pinned)
    return res[:3], list(res[3:3 + n]), list(res[3 + n:3 + 2 * n]), res[3 + 2 * n]


def _own_copy(i, src_refs, land_refs, local_sems, scatter):
    me = _ring_peer(0)[0]
    return pltpu.make_async_copy(src_refs[i].at[me] if scatter else src_refs[i], land_refs[i].at[me],
                                 local_sems.at[i])


def _spread_wait(started, after, scatter, name):
    sems, srcs, lands, _ = started
    n = len(srcs)

    def body(*refs):
        src_refs, land_refs = refs[:n], refs[n:2 * n]
        send_s, recv_s, local_s = refs[2 * n:2 * n + 3]
        for i in range(n):
            for j in range(1, N_DEV):
                cp = _spread_copy(i, j, src_refs, land_refs, send_s, recv_s, scatter)
                cp.wait_send()
                cp.wait_recv()
        for i in range(n):
            _own_copy(i, src_refs, land_refs, local_s, scatter).wait()

    hbm = lambda a: pltpu.HBM(a.shape, a.dtype)
    res = pl.pallas_call(
        body, name=name,
        out_shape=tuple(hbm(a) for a in srcs + lands),
        in_specs=[_HBM] * (2 * n) + [_SEM, _SEM, _SEM, pl.BlockSpec(memory_space=pl.ANY)],
        out_specs=tuple([_HBM] * (2 * n)),
        input_output_aliases={i: i for i in range(2 * n)},
        compiler_params=pltpu.CompilerParams(has_side_effects=_DATAFLOW),
    )(*srcs, *lands, *sems, after)
    return list(res[n:])


def _sum8(landing, name):
    _, r, c = landing.shape
    tc = _col_tile(c)

    def body(a_ref, o_ref):
        tot = a_ref[0].astype(F32)
        for k in range(1, N_DEV):
            tot = tot + a_ref[k].astype(F32)
        o_ref[...] = tot

    return pl.pallas_call(
        body,
        out_shape=jax.ShapeDtypeStruct((r, c), F32),
        grid=(c // tc,),
        in_specs=[pl.BlockSpec((N_DEV, r, tc), lambda j: (0, 0, j))],
        out_specs=pl.BlockSpec((r, tc), lambda j: (0, j)),
        compiler_params=pltpu.CompilerParams(dimension_semantics=("parallel",)),
        name=name,
    )(landing)


def _adamw_math(w, g, m, v):
    m = ADAM_B1 * m + (1.0 - ADAM_B1) * g
    v = ADAM_B2 * v + (1.0 - ADAM_B2) * (g * g)
    m_hat = m / (1.0 - ADAM_B1 ** ADAM_STEP)
    v_hat = v / (1.0 - ADAM_B2 ** ADAM_STEP)
    delta = -ADAM_LR * (m_hat / (jnp.sqrt(v_hat) + ADAM_EPS) + ADAM_WD * w)
    return delta, m, v


def _adamw(w, m, v, g, name):
    r, c = w.shape

    def fn(rows, consts):
        return list(_adamw_math(*rows)), []

    return _rowwise(fn, [w, g, m, v], [], [(c, F32)] * 3, tm=r if r <= 512 else 256, name=name)


def _adamw_sum8(w, m, v, landing, name):
    r, c = w.shape
    tc = _col_tile(c)

    def body(w_ref, m_ref, v_ref, a_ref, g_ref, d_ref, m2_ref, v2_ref):
        g = a_ref[0].astype(F32)
        for k in range(1, N_DEV):
            g = g + a_ref[k].astype(F32)
        delta, m2, v2 = _adamw_math(w_ref[...], g, m_ref[...], v_ref[...])
        g_ref[...] = g
        d_ref[...] = delta
        m2_ref[...] = m2
        v2_ref[...] = v2

    blk = pl.BlockSpec((r, tc), lambda j: (0, j))
    return pl.pallas_call(
        body,
        out_shape=[jax.ShapeDtypeStruct((r, c), F32)] * 4,
        grid=(c // tc,),
        in_specs=[blk, blk, blk, pl.BlockSpec((N_DEV, r, tc), lambda j: (0, 0, j))],
        out_specs=[blk] * 4,
        compiler_params=pltpu.CompilerParams(dimension_semantics=("parallel",)),
        name=name,
    )(w, m, v, landing)


def _adamw_small(gathered, params):
    ns = len(_SMALL)

    def body(*refs):
        g_ref, p_refs, o_refs = refs[0], refs[1:1 + 3 * ns], refs[1 + 3 * ns:]
        tot = g_ref[0]
        for k in range(1, N_DEV):
            tot = tot + g_ref[k]
        for i, name in enumerate(_SMALL):
            row, lane0, lanes = _SMALL_SLOTS[name]
            g = tot[row:row + 1, lane0:lane0 + lanes]
            w_, m_, v_ = (p_refs[3 * i + j][...] for j in range(3))
            delta, m2, v2 = _adamw_math(w_, g, m_, v_)
            for j, val in enumerate((g, delta, m2, v2)):
                o_refs[4 * i + j][...] = val
        o_refs[4 * ns][...] = tot[_LOSS_ROW:_LOSS_ROW + 1, 0:LANES]
        o_refs[4 * ns + 1][...] = tot[_CONV_ROW0:_CONV_ROW0 + _CONV_ROWS, :]

    out_shape = [jax.ShapeDtypeStruct(w.shape, F32) for (w, _, _) in params for _ in range(4)]
    out_shape += [jax.ShapeDtypeStruct((1, LANES), F32), jax.ShapeDtypeStruct((_CONV_ROWS, FLAT_COLS), F32)]
    flat = [a for wmv in params for a in wmv]
    return pl.pallas_call(body, out_shape=out_shape, name="adamw_small")(gathered, *flat)


def kernel(x, p, positions, w_in, conv_w, dn_a_log, dn_dt_bias, dn_norm_w, q_norm_w, w_uq, kv_norm_w, w_uk, w_uv, w_br_dn, w_br_mla, w_o, ln1_g, ln1_b, w_ffn_in, w_ffn_out, w_ple, w_ple_gate, ln2_g, ln2_b, loss_target, m_w_in, m_conv_w, m_dn_a_log, m_dn_dt_bias, m_dn_norm_w, m_q_norm_w, m_w_uq, m_kv_norm_w, m_w_uk, m_w_uv, m_w_br_dn, m_w_br_mla, m_w_o, m_ln1_g, m_ln1_b, m_w_ffn_in, m_w_ffn_out, m_w_ple, m_w_ple_gate, m_ln2_g, m_ln2_b, v_w_in, v_conv_w, v_dn_a_log, v_dn_dt_bias, v_dn_norm_w, v_q_norm_w, v_w_uq, v_kv_norm_w, v_w_uk, v_w_uv, v_w_br_dn, v_w_br_mla, v_w_o, v_ln1_g, v_ln1_b, v_w_ffn_in, v_w_ffn_out, v_w_ple, v_w_ple_gate, v_ln2_g, v_ln2_b):
    args = dict(locals())
    wts = {n: args[n] for n in _ORDER}
    mom1 = {n: args["m_" + n] for n in _ORDER}
    mom2 = {n: args["v_" + n] for n in _ORDER}
    big_names = [n for n, _ in _BIG]
    shard_shapes = {n: wts[n].shape[1:] for n in big_names}
    c_idx = lax.axis_index("c")
    q_idx = 2 * lax.axis_index("x") + lax.axis_index("y")
    parity, chip = c_idx.reshape(1).astype(jnp.int32), q_idx.reshape(1).astype(jnp.int32)

    stored = {n: _to_stored(n, wts[n][0]).astype(BF16) for n in big_names}
    first = _all_gather([stored["w_in"], conv_w[0]], "ag_first")
    group_names = {grp: [n for n, _ in pairs] for grp, pairs in _GROUP_GRADS.items()}
    carry, gathers = first[0], {}
    for grp in ("mix", "ffn"):
        gathers[grp] = _spread_start([stored[n] for n in group_names[grp]], carry, False, "ag_start_" + grp)
        carry = gathers[grp][3]
    conv_full = jnp.moveaxis(first[1], 0, 1).reshape(conv_w.shape[1], -1)
    small_w = {n: wts[n].astype(F32) for n in _SMALL}
    w = _first_weights(carry.reshape(-1, D_MODEL), conv_full, small_w)

    def late_weights(grp, after):
        got = _spread_wait(gathers[grp], after, False, "ag_wait_" + grp)
        return _late_weights(grp, {n: t.reshape(-1, t.shape[-1]) for n, t in zip(group_names[grp], got)})

    started = {}

    def emit(group, g, carry):
        grads = _group_grads(group, g)
        srcs = [grads[n].reshape((N_DEV,) + _stored_shape(n, shard_shapes[n])) for n in grads]
        started[group] = (list(grads), _spread_start(srcs, carry, True, "rs_start_" + group))
        return started[group][1][3]

    s_dim = x.shape[1]
    loss, dx, g = _local_step(x[0], p[0, 0], positions.reshape(s_dim, 1).astype(F32), loss_target[0], w,
                              late_weights, emit)
    g_w_in, small_g = _last_grads(g)

    src = g_w_in.reshape((N_DEV,) + _stored_shape("w_in", shard_shapes["w_in"]))
    from_sibling = _exchange_sibling([src], "rs_sibling")[0]
    own, own_bf = _chip_sum(src, from_sibling, parity, "rs_sum_w_in")
    from_chips = _exchange_chips([own_bf], "rs_chips")[0]

    out_g, out_d, out_m, out_v = {}, {}, {}, {}

    def update(n, grad, shp):
        flat2 = (shp[0], int(np.prod(shp[1:])))
        d, m2, v2 = _adamw(wts[n][0].reshape(flat2), mom1[n][0].reshape(flat2), mom2[n][0].reshape(flat2),
                           grad.reshape(flat2), "adamw_" + n)
        out_g[n], out_d[n], out_m[n], out_v[n] = grad, d.reshape(shp), m2.reshape(shp), v2.reshape(shp)

    total = _sum_parts(own, from_chips, chip, "rs_total_w_in")
    update("w_in", _from_stored("w_in", total, shard_shapes["w_in"]), shard_shapes["w_in"])
    for group, (names, st) in started.items():
        for n, landing in zip(names, _spread_wait(st, dx, True, "rs_wait_" + group)):
            shp = shard_shapes[n]
            if _BIG_AXIS[n] == 0:
                flat2 = _stored_shape(n, shp)
                res = _adamw_sum8(wts[n][0].reshape(flat2), mom1[n][0].reshape(flat2), mom2[n][0].reshape(flat2),
                                  landing, "adamw_" + n)
                out_g[n], out_d[n], out_m[n], out_v[n] = (t.reshape(shp) for t in res)
            else:
                update(n, _from_stored(n, _sum8(landing, "rs_total_" + n), shp), shp)

    g_small = _all_gather([_pack_small_grads(small_g, loss)], "ag_small")[0]
    res = _adamw_small(g_small, [(wts[n], mom1[n], mom2[n]) for n in _SMALL])
    for i, n in enumerate(_SMALL):
        out_g[n], out_d[n], out_m[n], out_v[n] = res[4 * i:4 * i + 4]
    loss_out = res[4 * len(_SMALL)][0, 0]
    conv_shape = conv_w.shape[1:]
    conv_g = lax.dynamic_slice(res[-1].reshape(conv_shape[0], -1), (0, (2 * q_idx + c_idx) * conv_shape[1]),
                               conv_shape)
    update("conv_w", conv_g, conv_shape)

    expand = lambda d, n: d[n] if n in _SMALL else d[n][None]
    return (loss_out, dx[None], *[expand(out_g, n) for n in _ORDER], *[expand(out_d, n) for n in _ORDER],
            *[expand(out_m, n) for n in _ORDER], *[expand(out_v, n) for n in _ORDER])
```

```python
import functools

import numpy as np
import jax
import jax.numpy as jnp
from jax import lax
from jax.experimental import pallas as pl
from jax.experimental.pallas import tpu as pltpu

F32 = jnp.float32
BF16 = jnp.bfloat16

D_MODEL = 1024
N_HEADS = 8
HEAD = 128
CHUNK = 64
GROUP = 256
ROPE = 64
Q_LORA = 384
KV_LORA = 256
FFN_HIDDEN = 2816
PLE_DIM = 256
ROPE_BASE = 10000.0
ALPHA = 2.0 ** 0.25
SCALE = float((HEAD + ROPE) ** -0.5)
NEG_BIG = -1e30
EPS_RMS = 1e-6
EPS_LN = 1e-5

ADAM_LR = 0.001
ADAM_B1 = 0.9
ADAM_B2 = 0.999
ADAM_EPS = 1e-08
ADAM_WD = 0.01
ADAM_STEP = 10

N_DEV = 8
LANES = 128
FLAT_COLS = 1024

WB_CQ, WB_CKV, WB_KR, WB_BA, WB_COLS = 0, 512, 768, 896, 1024

HIGHEST = lax.Precision.HIGHEST

NN = (((1,), (0,)), ((), ()))
TN = (((0,), (0,)), ((), ()))
NT = (((1,), (1,)), ((), ()))


def _dot(a, b, dims=NN):
    return lax.dot_general(a.astype(BF16), b.astype(BF16), dims, preferred_element_type=F32)


def _dot32(a, b, dims=NN):
    return lax.dot_general(a, b, dims, precision=HIGHEST, preferred_element_type=F32)


def _sig(x):
    return 1.0 / (1.0 + jnp.exp(-x))


MM_TILE = 1536


def _pick_wide(n):
    if n <= MM_TILE:
        return n
    return max(t for t in range(LANES, MM_TILE + 1, LANES) if n % t == 0)


def _split_bf16(a):
    hi = a.astype(BF16)
    return hi, (a - hi.astype(F32)).astype(BF16)


def _dot3(a, b, dims=NN):
    ah, al = a if isinstance(a, tuple) else _split_bf16(a)
    bh, bl = b if isinstance(b, tuple) else _split_bf16(b)
    d = lambda p, q: lax.dot_general(p, q, dims, preferred_element_type=F32)
    return d(ah, bh) + (d(ah, bl) + d(al, bh))


def _mm(a, b, *, ta=False, tb=False, add=(), out_dtype=F32, name):
    if ta:
        k_dim, m_dim = a.shape
    else:
        m_dim, k_dim = a.shape
    if tb:
        n_dim, k2 = b.shape
    else:
        k2, n_dim = b.shape
    assert k_dim == k2, (a.shape, b.shape, ta, tb)
    tm = _pick_wide(m_dim)
    tn = _pick_wide(n_dim)
    tk = _pick_wide(k_dim)
    nk = k_dim // tk
    n_add = len(add)
    dims = TN if ta else (NT if tb else NN)
    assert not (ta and tb)

    def body(a_ref, b_ref, *rest):
        add_refs = rest[:n_add]
        o_ref = rest[n_add]
        acc = rest[n_add + 1]
        k = pl.program_id(2)

        @pl.when(k == 0)
        def _():
            acc[...] = jnp.zeros_like(acc)

        acc[...] += _dot(a_ref[...], b_ref[...], dims)

        @pl.when(k == nk - 1)
        def _():
            r = acc[...]
            for ar in add_refs:
                r = r + ar[...].astype(F32)
            o_ref[...] = r.astype(o_ref.dtype)

    a_spec = pl.BlockSpec((tk, tm), lambda i, j, k: (k, i)) if ta else pl.BlockSpec((tm, tk), lambda i, j, k: (i, k))
    b_spec = pl.BlockSpec((tn, tk), lambda i, j, k: (j, k)) if tb else pl.BlockSpec((tk, tn), lambda i, j, k: (k, j))
    o_spec = pl.BlockSpec((tm, tn), lambda i, j, k: (i, j))
    return pl.pallas_call(
        body,
        out_shape=jax.ShapeDtypeStruct((m_dim, n_dim), out_dtype),
        grid=(m_dim // tm, n_dim // tn, nk),
        in_specs=[a_spec, b_spec] + [o_spec] * n_add,
        out_specs=o_spec,
        scratch_shapes=[pltpu.VMEM((tm, tn), F32)],
        compiler_params=pltpu.CompilerParams(dimension_semantics=("parallel", "parallel", "arbitrary")),
        name=name,
    )(a, b, *add)


def _rowwise(fn, rows, consts, outs, accs=(), *, tm=256, name):
    rows = [r if isinstance(r, tuple) else (r, 0, r.shape[1]) for r in rows]
    s_dim = rows[0][0].shape[0]
    tm = min(tm, s_dim)
    assert s_dim % tm == 0 and all(arr.shape[0] == s_dim for arr, _, _ in rows)
    specs = [pl.BlockSpec((tm, width), functools.partial(lambda i, cb: (i, cb), cb=cb)) for _, cb, width in rows]
    args = [arr for arr, _, _ in rows]
    for c in consts:
        specs.append(pl.BlockSpec(c.shape, lambda i: (0, 0)))
        args.append(c)
    nr, nc, no = len(rows), len(consts), len(outs)
    flipped = [len(o) == 3 for o in outs]
    out_shape = [jax.ShapeDtypeStruct((o[0], s_dim) if t else (s_dim, o[0]), o[1]) for o, t in zip(outs, flipped)]
    out_specs = [pl.BlockSpec((o[0], tm), lambda i: (0, i)) if t else pl.BlockSpec((tm, o[0]), lambda i: (i, 0))
                 for o, t in zip(outs, flipped)]
    out_shape += [jax.ShapeDtypeStruct(sh, F32) for sh in accs]
    out_specs += [pl.BlockSpec(sh, lambda i: (0, 0)) for sh in accs]

    def body(*refs):
        r = [x[...].astype(F32) if x.dtype == BF16 else x[...] for x in refs[:nr]]
        c = [x[...] for x in refs[nr:nr + nc]]
        o_refs = refs[nr + nc:nr + nc + no]
        a_refs = refs[nr + nc + no:]
        o_vals, a_vals = fn(r, c)
        for ref, v, t in zip(o_refs, o_vals, flipped, strict=True):
            ref[...] = (jnp.transpose(v.astype(F32)) if t else v).astype(ref.dtype)
        if a_refs:
            @pl.when(pl.program_id(0) == 0)
            def _():
                for ref in a_refs:
                    ref[...] = jnp.zeros_like(ref)

            for ref, v in zip(a_refs, a_vals, strict=True):
                ref[...] += v

    res = pl.pallas_call(
        body,
        out_shape=out_shape,
        grid=(s_dim // tm,),
        in_specs=specs,
        out_specs=out_specs,
        compiler_params=pltpu.CompilerParams(dimension_semantics=("arbitrary" if accs else "parallel",)),
        name=name,
    )(*args)
    return res


def _colsum(v):
    return jnp.sum(v, axis=0, keepdims=True)


def _rowsum(v):
    return jnp.sum(v, axis=1, keepdims=True)


def _rowmean(v):
    return jnp.mean(v, axis=1, keepdims=True)


def _silu_grad(x):
    s = _sig(x)
    return s * (1.0 + x * (1.0 - s))


def _conv_taps(x, w, width=4):
    row = lax.broadcasted_iota(jnp.int32, x.shape, 0)
    c = x * w[width - 1:width, :]
    for s in range(1, width):
        c = c + jnp.where(row >= s, pltpu.roll(x, s, 0), 0.0) * w[width - 1 - s:width - s, :]
    return c


def _conv_fwd(proj_a, conv_w):
    s_dim = proj_a.shape[0]
    n_blk = 3 * N_HEADS

    def body(x_ref, w_ref, o_ref):
        j = pl.program_id(0)
        c = _conv_taps(x_ref[...], w_ref[...])
        y = c * _sig(c)
        r = lax.rsqrt(_rowsum(y * y) + EPS_RMS)
        fac = jnp.where(j < N_HEADS, r * (HEAD ** -0.5), jnp.where(j < 2 * N_HEADS, r, 1.0))
        o_ref[...] = y * fac

    return pl.pallas_call(
        body,
        out_shape=jax.ShapeDtypeStruct((s_dim, n_blk * HEAD), F32),
        grid=(n_blk,),
        in_specs=[pl.BlockSpec((s_dim, HEAD), lambda j: (0, j)), pl.BlockSpec((4, HEAD), lambda j: (0, j))],
        out_specs=pl.BlockSpec((s_dim, HEAD), lambda j: (0, j)),
        compiler_params=pltpu.CompilerParams(dimension_semantics=("parallel",)),
        name="conv_fwd",
    )(proj_a, conv_w)


def _conv_bwd(proj_a, conv_w, dq, dk, dv):
    s_dim = proj_a.shape[0]
    n_blk = 3 * N_HEADS

    def body(x_ref, w_ref, dq_ref, dk_ref, dv_ref, dx_ref, dw_ref):
        j = pl.program_id(0)
        x = x_ref[...]
        w = w_ref[...]
        do = jnp.where(j < N_HEADS, dq_ref[...], jnp.where(j < 2 * N_HEADS, dk_ref[...], dv_ref[...]))
        c = _conv_taps(x, w)
        sg = _sig(c)
        y = c * sg
        r = lax.rsqrt(_rowsum(y * y) + EPS_RMS)
        sc = jnp.where(j < N_HEADS, HEAD ** -0.5, 1.0)
        dy_n = sc * (r * do - y * (r * r * r) * _rowsum(do * y))
        dy = jnp.where(j < 2 * N_HEADS, dy_n, do)
        dc = dy * (sg * (1.0 + c * (1.0 - sg)))
        row = lax.broadcasted_iota(jnp.int32, x.shape, 0)
        dx = dc * w[3:4, :]
        dw_ref[3:4, :] = _colsum(dc * x)
        for s in range(1, 4):
            dx = dx + jnp.where(row < s_dim - s, pltpu.roll(dc, s_dim - s, 0), 0.0) * w[3 - s:4 - s, :]
            xs = jnp.where(row >= s, pltpu.roll(x, s, 0), 0.0)
            dw_ref[3 - s:4 - s, :] = _colsum(dc * xs)
        dx_ref[...] = dx.astype(dx_ref.dtype)

    hd = N_HEADS - 1
    return pl.pallas_call(
        body,
        out_shape=[jax.ShapeDtypeStruct((s_dim, n_blk * HEAD), BF16), jax.ShapeDtypeStruct((4, n_blk * HEAD), F32)],
        grid=(n_blk,),
        in_specs=[
            pl.BlockSpec((s_dim, HEAD), lambda j: (0, j)),
            pl.BlockSpec((4, HEAD), lambda j: (0, j)),
            pl.BlockSpec((s_dim, HEAD), lambda j: (0, jnp.minimum(j, hd))),
            pl.BlockSpec((s_dim, HEAD), lambda j: (0, jnp.clip(j - N_HEADS, 0, hd))),
            pl.BlockSpec((s_dim, HEAD), lambda j: (0, jnp.clip(j - 2 * N_HEADS, 0, hd))),
        ],
        out_specs=[pl.BlockSpec((s_dim, HEAD), lambda j: (0, j)), pl.BlockSpec((4, HEAD), lambda j: (0, j))],
        compiler_params=pltpu.CompilerParams(dimension_semantics=("parallel",)),
        name="conv_bwd",
    )(proj_a, conv_w, dq, dk, dv)


def _chunk_tri(n):
    r = np.arange(n)
    m = ((r[:, None] // CHUNK) == (r[None, :] // CHUNK)) & (r[:, None] >= r[None, :])
    m = m.astype(np.float32)
    return jnp.asarray(m), jnp.asarray(m.T)


def _softplus(z):
    return jnp.maximum(z, 0.0) + jnp.log(1.0 + jnp.exp(-jnp.abs(z)))


def _gates_fwd(proj_b, alog, dtb):
    tm = min(GROUP, proj_b.shape[0])
    tri, _ = _chunk_tri(tm)

    def fn(r, c):
        b = r[0]
        a = pltpu.roll(b, LANES - N_HEADS, 1)
        alog_, dtb_, tri_ = c
        g = -jnp.exp(alog_) * _softplus(a + dtb_)
        return [_sig(b), _dot32(tri_, g)], []

    return _rowwise(fn, [(proj_b, WB_BA // LANES, LANES)], [alog, dtb, tri],
                    [(LANES, F32), (LANES, F32)], tm=tm, name="gates_fwd")


def _gates_bwd(proj_b, alog, dtb, gc, d_beta, d_gc, d_egl_rows):
    tm = min(GROUP, proj_b.shape[0])
    _, tri_t = _chunk_tri(tm)

    def fn(r, c):
        b, gc_, d_beta_, d_gc_, d_egl_ = r
        a = pltpu.roll(b, LANES - N_HEADS, 1)
        alog_, dtb_, tri_t_ = c
        z = a + dtb_
        ea = jnp.exp(alog_)
        g = -ea * _softplus(z)
        dg = _dot32(tri_t_, d_gc_ + d_egl_ * jnp.exp(gc_))
        d_a = dg * (-ea) * _sig(z)
        beta = _sig(b)
        d_ba = d_beta_ * beta * (1.0 - beta) + pltpu.roll(d_a, N_HEADS, 1)
        return [d_ba], [_colsum(dg * g), _colsum(d_a)]

    return _rowwise(fn, [(proj_b, WB_BA // LANES, LANES), gc, d_beta, d_gc, d_egl_rows],
                    [alog, dtb, tri_t], [(LANES, BF16)], accs=[(1, LANES), (1, LANES)], tm=tm,
                    name="gates_bwd")


def _group_masks(n):
    r = lax.broadcasted_iota(jnp.int32, (n, n), 0)
    c = lax.broadcasted_iota(jnp.int32, (n, n), 1)
    same = (r // CHUNK) == (c // CHUNK)
    below, s = [], 2
    while s < CHUNK:
        below.append(jnp.logical_and((r // (2 * s)) == (c // (2 * s)),
                                     jnp.logical_and((r // s) % 2 == 1, (c // s) % 2 == 0)))
        s *= 2
    return dict(same=same, tril=jnp.logical_and(same, r >= c), strict=jnp.logical_and(same, r > c),
                last=c == (r // CHUNK) * CHUNK + (CHUNK - 1), eye=r == c, pair=(r // 2) == (c // 2), below=below)


def _inv_unit_lower(l_mats, mk):
    eye_f = mk["eye"].astype(F32)
    ts = [eye_f - jnp.where(mk["pair"], l_mat, 0.0) for l_mat in l_mats]
    for below in mk["below"]:
        halves = [_split_bf16(t) for t in ts]
        mids = [_dot3(h, jnp.where(below, l_mat, 0.0)) for h, l_mat in zip(halves, l_mats)]
        ts = [t - _dot3(m, h) for t, m, h in zip(ts, mids, halves)]
    return ts


def _unfold_blocks(folded, mask):
    n = folded.shape[0]
    return jnp.where(mask, jnp.concatenate([folded] * (n // CHUNK), axis=1), 0.0)


def _head_cols(beta, gc, gc_t, h):
    lane = lax.broadcasted_iota(jnp.int32, beta.shape, 1)
    sub = lax.broadcasted_iota(jnp.int32, gc_t.shape, 0)
    bcol = _rowsum(jnp.where(lane == h, beta, 0.0))
    gcol = _rowsum(jnp.where(lane == h, gc, 0.0))
    grow = _colsum(jnp.where(sub == h, gc_t, 0.0))
    return bcol, gcol, grow


def _prep_common(q, k, bcol, gcol, grow, mk, t_folded=None):
    n = q.shape[0]
    tril = mk["tril"]
    decay = jnp.where(tril, jnp.exp(jnp.where(tril, gcol - grow, 0.0)), 0.0)
    glast = _rowsum(jnp.where(mk["last"], jnp.broadcast_to(grow, (n, n)), 0.0))
    e = jnp.exp(gcol)
    ekt = jnp.exp(glast - gcol)
    kb = k * bcol
    kk = _dot(kb, k, NT)
    qk = _dot(q, k, NT)
    p = dict(decay=decay, e=e, ekt=ekt, kb=kb, kk=kk, qk=qk)
    if t_folded is not None:
        p["t"] = _unfold_blocks(t_folded, mk["same"])
    return p


GROUPS_PER_STEP = 4
SCAN_CHUNKS_PER_STEP = 4


def _fold_blocks(m):
    n = m.shape[0]
    out = m[:, 0:CHUNK]
    for b in range(1, n // CHUNK):
        out = out + m[:, b * CHUNK:(b + 1) * CHUNK]
    return out


def _gdr_prep_fwd(qkvn, beta, gc, gc_t):
    s_dim = qkvn.shape[0]
    tg = min(GROUP, s_dim)
    n_sub = min(GROUPS_PER_STEP, s_dim // tg)
    tb = tg * n_sub

    def body(q_ref, k_ref, v_ref, b_ref, g_ref, gt_ref, u_ref, w_ref, qd_ref, kt_ref, a_ref, t_ref):
        h = pl.program_id(0)
        mk = _group_masks(tg)
        parts = []
        for s in range(n_sub):
            rows = slice(s * tg, (s + 1) * tg)
            q, k, v = q_ref[rows, :], k_ref[rows, :], v_ref[rows, :]
            bcol, gcol, grow = _head_cols(b_ref[rows, :], g_ref[rows, :], gt_ref[:, rows], h)
            p = _prep_common(q, k, bcol, gcol, grow, mk)
            qd_ref[rows, :] = q * p["e"]
            kt_ref[rows, :] = k * p["ekt"]
            a_ref[rows, :] = _fold_blocks(jnp.where(mk["tril"], p["qk"] * p["decay"], 0.0))
            parts.append((rows, v * bcol, p["kb"] * p["e"], jnp.where(mk["strict"], p["kk"] * p["decay"], 0.0)))
        t_mats = _inv_unit_lower([part[3] for part in parts], mk)
        for (rows, vb, kbe, _), t_mat in zip(parts, t_mats):
            u_ref[rows, :] = _dot(t_mat, vb)
            w_ref[rows, :] = _dot(t_mat, kbe)
            t_ref[rows, :] = _fold_blocks(t_mat)

    row = lambda off: pl.BlockSpec((tb, HEAD), functools.partial(lambda h, m, off: (m, h + off), off=off))
    full = pl.BlockSpec((tb, LANES), lambda h, m: (m, 0))
    o_spec = pl.BlockSpec((tb, HEAD), lambda h, m: (m, h))
    a_spec = pl.BlockSpec((None, tb, CHUNK), lambda h, m: (h, m, 0))
    wide = jax.ShapeDtypeStruct((s_dim, N_HEADS * HEAD), F32)
    folded = jax.ShapeDtypeStruct((N_HEADS, s_dim, CHUNK), F32)
    return pl.pallas_call(
        body,
        out_shape=[wide, wide, wide, wide, folded, folded],
        grid=(N_HEADS, s_dim // tb),
        in_specs=[row(0), row(N_HEADS), row(2 * N_HEADS), full, full, pl.BlockSpec((8, tb), lambda h, m: (0, m))],
        out_specs=[o_spec, o_spec, o_spec, o_spec, a_spec, a_spec],
        compiler_params=pltpu.CompilerParams(dimension_semantics=("parallel", "parallel")),
        name="gdr_prep_fwd",
    )(qkvn, qkvn, qkvn, beta, gc, gc_t)


def _gdr_prep_bwd(qkvn, beta, gc, gc_t, t_fold, u, w, du, dw, dqd, dkt, d_a):
    s_dim = qkvn.shape[0]
    tg = min(GROUP, s_dim)
    n_sub = min(GROUPS_PER_STEP, s_dim // tg)
    tb = tg * n_sub

    def body(q_ref, k_ref, v_ref, b_ref, g_ref, gt_ref, t_ref, u_ref, w_ref, du_ref, dw_ref, dqd_ref, dkt_ref,
             da_ref, dq_ref, dk_ref, dv_ref, db_ref, dg_ref):
        h = pl.program_id(1)

        @pl.when(h == 0)
        def _():
            db_ref[...] = jnp.zeros_like(db_ref)
            dg_ref[...] = jnp.zeros_like(dg_ref)

        mk = _group_masks(tg)
        lane = lax.broadcasted_iota(jnp.int32, (tg, LANES), 1)
        for s in range(n_sub):
            rows = slice(s * tg, (s + 1) * tg)
            q, k, v = q_ref[rows, :], k_ref[rows, :], v_ref[rows, :]
            bcol, gcol, grow = _head_cols(b_ref[rows, :], g_ref[rows, :], gt_ref[:, rows], h)
            p = _prep_common(q, k, bcol, gcol, grow, mk, t_ref[rows, :])
            t_mat, decay, e, ekt, kb = p["t"], p["decay"], p["e"], p["ekt"], p["kb"]
            du_, dw_, dqd_, dkt_ = du_ref[rows, :], dw_ref[rows, :], dqd_ref[rows, :], dkt_ref[rows, :]
            dvb = _dot(t_mat, du_, TN)
            dkbe = _dot(t_mat, dw_, TN)
            d_l = -(_dot(dvb, u_ref[rows, :], NT) + _dot(dkbe, w_ref[rows, :], NT))
            m1 = jnp.where(mk["strict"], d_l, 0.0)
            m2 = _unfold_blocks(da_ref[rows, :], mk["tril"])
            d_kk = m1 * decay
            d_qk = m2 * decay
            d_decay = m1 * p["kk"] + m2 * p["qk"]
            dkb = _dot(d_kk, k) + dkbe * e
            dk = _dot(d_kk, kb, TN) + _dot(d_qk, q, TN) + dkt_ * ekt + dkb * bcol
            dq = _dot(d_qk, k) + dqd_ * e
            d_beta = _rowsum(dkb * k) + _rowsum(dvb * v)
            d_e = _rowsum(dkbe * kb) + _rowsum(dqd_ * q)
            d_ekt = _rowsum(dkt_ * k) * ekt
            d_diff = d_decay * decay
            d_grow = -_colsum(d_diff) + _colsum(jnp.where(mk["last"], jnp.broadcast_to(d_ekt, (tg, tg)), 0.0))
            d_gcol = d_e * e - d_ekt + _rowsum(d_diff)
            d_gcol = d_gcol + _rowsum(jnp.where(mk["eye"], jnp.broadcast_to(d_grow, (tg, tg)), 0.0))
            dq_ref[rows, :] = dq
            dk_ref[rows, :] = dk
            dv_ref[rows, :] = dvb * bcol
            db_ref[rows, :] = jnp.where(lane == h, d_beta, db_ref[rows, :])
            dg_ref[rows, :] = jnp.where(lane == h, d_gcol, dg_ref[rows, :])

    row = lambda off: pl.BlockSpec((tb, HEAD), functools.partial(lambda m, h, off: (m, h + off), off=off))
    full = pl.BlockSpec((tb, LANES), lambda m, h: (m, 0))
    o_spec = pl.BlockSpec((tb, HEAD), lambda m, h: (m, h))
    a_spec = pl.BlockSpec((None, tb, CHUNK), lambda m, h: (h, m, 0))
    wide = jax.ShapeDtypeStruct((s_dim, N_HEADS * HEAD), F32)
    lanes = jax.ShapeDtypeStruct((s_dim, LANES), F32)
    return pl.pallas_call(
        body,
        out_shape=[wide, wide, wide, lanes, lanes],
        grid=(s_dim // tb, N_HEADS),
        in_specs=[row(0), row(N_HEADS), row(2 * N_HEADS), full, full, pl.BlockSpec((8, tb), lambda m, h: (0, m)),
                  a_spec, o_spec, o_spec, o_spec, o_spec, o_spec, o_spec, a_spec],
        out_specs=[o_spec, o_spec, o_spec, full, full],
        compiler_params=pltpu.CompilerParams(dimension_semantics=("parallel", "arbitrary")),
        name="gdr_prep_bwd",
    )(qkvn, qkvn, qkvn, beta, gc, gc_t, t_fold, u, w, du, dw, dqd, dkt, d_a)


def _gdr_scan_fwd(u, w, qd, kt, a_mat, gc):
    s_dim = u.shape[0]
    n_chunks = s_dim // CHUNK
    per = min(SCAN_CHUNKS_PER_STEP, n_chunks)
    tb = per * CHUNK

    def body(u_ref, w_ref, qd_ref, kt_ref, a_ref, g_ref, o_ref, st_ref, state):
        @pl.when(pl.program_id(0) == 0)
        def _():
            state[...] = jnp.zeros_like(state)

        heads = range(N_HEADS)
        cols = [slice(h * HEAD, (h + 1) * HEAD) for h in heads]
        for i in range(per):
            rows = slice(i * CHUNK, (i + 1) * CHUNK)
            egl = jnp.exp(g_ref[(i + 1) * CHUNK - 1:(i + 1) * CHUNK, :])
            s_b = [state[h].astype(BF16) for h in heads]
            for h in heads:
                st_ref[i, h] = state[h]
            ws = [_dot(w_ref[rows, cs], s) for cs, s in zip(cols, s_b)]
            qs = [_dot(qd_ref[rows, cs], s) for cs, s in zip(cols, s_b)]
            vns = [(u_ref[rows, cs] - ws_h).astype(BF16) for cs, ws_h in zip(cols, ws)]
            avs = [_dot(a_ref[h, rows, :], vn) for h, vn in zip(heads, vns)]
            kvs = [_dot(kt_ref[rows, cs], vn, TN) for cs, vn in zip(cols, vns)]
            for h, cs in zip(heads, cols):
                o_ref[rows, cs] = qs[h] + avs[h]
                state[h] = state[h] * egl[:, h:h + 1] + kvs[h]

    wide = pl.BlockSpec((tb, N_HEADS * HEAD), lambda n: (n, 0))
    return pl.pallas_call(
        body,
        out_shape=[jax.ShapeDtypeStruct((s_dim, N_HEADS * HEAD), F32),
                   jax.ShapeDtypeStruct((n_chunks, N_HEADS, HEAD, HEAD), F32)],
        grid=(n_chunks // per,),
        in_specs=[wide, wide, wide, wide, pl.BlockSpec((N_HEADS, tb, CHUNK), lambda n: (0, n, 0)),
                  pl.BlockSpec((tb, LANES), lambda n: (n, 0))],
        out_specs=[wide, pl.BlockSpec((per, N_HEADS, HEAD, HEAD), lambda n: (n, 0, 0, 0))],
        scratch_shapes=[pltpu.VMEM((N_HEADS, HEAD, HEAD), F32)],
        compiler_params=pltpu.CompilerParams(dimension_semantics=("arbitrary",)),
        name="gdr_scan_fwd",
    )(u, w, qd, kt, a_mat, gc)


def _gdr_scan_bwd(u, w, qd, kt, a_mat, gc, states, d_o):
    s_dim = u.shape[0]
    n_chunks = s_dim // CHUNK
    per = min(SCAN_CHUNKS_PER_STEP, n_chunks)
    tb = per * CHUNK
    last = n_chunks // per - 1

    def body(u_ref, w_ref, qd_ref, kt_ref, a_ref, g_ref, st_ref, do_ref,
             du_ref, dw_ref, dqd_ref, dkt_ref, da_ref, de_ref, d_state):
        @pl.when(pl.program_id(0) == 0)
        def _():
            d_state[...] = jnp.zeros_like(d_state)

        heads = range(N_HEADS)
        cols = [slice(h * HEAD, (h + 1) * HEAD) for h in heads]
        for i in reversed(range(per)):
            rows = slice(i * CHUNK, (i + 1) * CHUNK)
            egl = jnp.exp(g_ref[(i + 1) * CHUNK - 1:(i + 1) * CHUNK, :])
            s_b = [st_ref[i, h].astype(BF16) for h in heads]
            ds_b = [d_state[h].astype(BF16) for h in heads]
            dos = [do_ref[rows, cs].astype(BF16) for cs in cols]
            w_b = [w_ref[rows, cs].astype(BF16) for cs in cols]
            ws = [_dot(w_h, s) for w_h, s in zip(w_b, s_b)]
            ados = [_dot(a_ref[h, rows, :], do, TN) for h, do in zip(heads, dos)]
            kds = [_dot(kt_ref[rows, cs], ds) for cs, ds in zip(cols, ds_b)]
            dqds = [_dot(do, s, NT) for do, s in zip(dos, s_b)]
            qdos = [_dot(qd_ref[rows, cs], do, TN) for cs, do in zip(cols, dos)]
            vns = [(u_ref[rows, cs] - ws_h).astype(BF16) for cs, ws_h in zip(cols, ws)]
            dvns = [a + k_ for a, k_ in zip(ados, kds)]
            dvn_b = [d.astype(BF16) for d in dvns]
            das = [_dot(do, vn, NT) for do, vn in zip(dos, vns)]
            dkts = [_dot(vn, ds, NT) for vn, ds in zip(vns, ds_b)]
            dws = [_dot(d, s, NT) for d, s in zip(dvn_b, s_b)]
            wds = [_dot(w_h, d, TN) for w_h, d in zip(w_b, dvn_b)]
            for h, cs in zip(heads, cols):
                ds_n = d_state[h]
                de = jnp.sum(_rowsum(ds_n * st_ref[i, h]), axis=0, keepdims=True)
                de_ref[i, h:h + 1, :] = jnp.broadcast_to(de, (1, LANES))
                dqd_ref[rows, cs] = dqds[h]
                da_ref[h, rows, :] = das[h]
                dkt_ref[rows, cs] = dkts[h]
                du_ref[rows, cs] = dvns[h]
                dw_ref[rows, cs] = -dws[h]
                d_state[h] = ds_n * egl[:, h:h + 1] + qdos[h] - wds[h]

    wide = pl.BlockSpec((tb, N_HEADS * HEAD), lambda n: (last - n, 0))
    a_spec = pl.BlockSpec((N_HEADS, tb, CHUNK), lambda n: (0, last - n, 0))
    wide_shape = jax.ShapeDtypeStruct((s_dim, N_HEADS * HEAD), F32)
    return pl.pallas_call(
        body,
        out_shape=[wide_shape, wide_shape, wide_shape, wide_shape,
                   jax.ShapeDtypeStruct((N_HEADS, s_dim, CHUNK), F32),
                   jax.ShapeDtypeStruct((n_chunks, N_HEADS, LANES), F32)],
        grid=(n_chunks // per,),
        in_specs=[wide, wide, wide, wide, a_spec, pl.BlockSpec((tb, LANES), lambda n: (last - n, 0)),
                  pl.BlockSpec((per, N_HEADS, HEAD, HEAD), lambda n: (last - n, 0, 0, 0)), wide],
        out_specs=[wide, wide, wide, wide, a_spec, pl.BlockSpec((per, N_HEADS, LANES), lambda n: (last - n, 0, 0))],
        scratch_shapes=[pltpu.VMEM((N_HEADS, HEAD, HEAD), F32)],
        compiler_params=pltpu.CompilerParams(dimension_semantics=("arbitrary",)),
        name="gdr_scan_bwd",
    )(u, w, qd, kt, a_mat, gc, states, d_o)


FUSED_ROWS = 512


def _gdr_out_fwd(o_dn, proj_a, dn_w, w_br):
    def fn(r, c):
        o, z = r
        w_, w_br_ = c
        outs = []
        for h in range(N_HEADS):
            cs = slice(h * HEAD, (h + 1) * HEAD)
            oh, zh = o[:, cs], z[:, cs]
            rr = lax.rsqrt(_rowmean(oh * oh) + EPS_RMS)
            outs.append(oh * rr * w_ * (zh * _sig(zh)))
        og = jnp.concatenate(outs, axis=1).astype(BF16)
        return [og, _dot(og, w_br_)], []

    return _rowwise(fn, [o_dn, (proj_a, 3, D_MODEL)], [dn_w, w_br], [(D_MODEL, BF16), (D_MODEL, BF16)],
                    tm=FUSED_ROWS, name="gdr_out_fwd")


def _gdr_out_bwd(o_dn, proj_a, d_y_dn, dn_w, w_br):
    def fn(r, c):
        o, z, dy = r
        w_, w_br_ = c
        dg = _dot(dy, w_br_, NT)
        d_o, d_z = [], []
        d_w = jnp.zeros((1, HEAD), F32)
        for h in range(N_HEADS):
            cs = slice(h * HEAD, (h + 1) * HEAD)
            oh, zh, dgh = o[:, cs], z[:, cs], dg[:, cs]
            rr = lax.rsqrt(_rowmean(oh * oh) + EPS_RMS)
            sz = zh * _sig(zh)
            d_n = dgh * sz
            d_z.append(dgh * (oh * rr * w_) * _silu_grad(zh))
            d_w = d_w + _colsum(d_n * oh * rr)
            gw = d_n * w_
            d_o.append(rr * gw - oh * (rr * rr * rr) * _rowmean(gw * oh))
        return [jnp.concatenate(d_o, axis=1), jnp.concatenate(d_z, axis=1)], [d_w]

    return _rowwise(fn, [o_dn, (proj_a, 3, D_MODEL), d_y_dn], [dn_w, w_br], [(D_MODEL, F32), (D_MODEL, BF16)],
                    accs=[(1, HEAD)], tm=FUSED_ROWS, name="gdr_out_bwd")


def _rms_fwd(x, w):
    r = lax.rsqrt(_rowmean(x * x) + EPS_RMS)
    return x * r * w


def _rms_bwd(x, w, dy):
    r = lax.rsqrt(_rowmean(x * x) + EPS_RMS)
    gw = dy * w
    return r * gw - x * (r * r * r) * _rowmean(gw * x), _colsum(dy * x * r)


def _rope_consts():
    inv = ROPE_BASE ** (-np.arange(0, ROPE, 2, dtype=np.float32) / ROPE)
    t = np.zeros((4, LANES), np.float32)
    t[0, :32] = inv
    t[0, 32:64] = inv
    t[1, :64] = 1.0
    t[2, 32:64] = 1.0
    t[3, :32] = -1.0
    return jnp.asarray(t)


def _rope_tables(pos, consts, width):
    ang = pos * consts[0:1, :]
    cosv, sinv = jnp.cos(ang), jnp.sin(ang)
    reps = width // LANES
    tile = (lambda t: jnp.concatenate([t] * reps, axis=1)) if reps > 1 else (lambda t: t)
    return tile(cosv * consts[1:2, :]), tile(sinv * consts[2:3, :]), tile(sinv * consts[3:4, :])


def _rope_apply(t, tabs):
    cos_t, sin_a, sin_b = tabs
    width = t.shape[1]
    return t * cos_t + pltpu.roll(t, 32, 1) * sin_a + pltpu.roll(t, width - 32, 1) * sin_b


def _rope_transpose(d, tabs):
    cos_t, sin_a, sin_b = tabs
    width = d.shape[1]
    return d * cos_t + pltpu.roll(d * sin_a, width - 32, 1) + pltpu.roll(d * sin_b, 32, 1)


QK_HEAD = 2 * HEAD


def _interleave_heads(a, b):
    parts = []
    for h in range(N_HEADS):
        parts.append(a[:, h * HEAD:(h + 1) * HEAD])
        parts.append(b if b.shape[1] == LANES else b[:, h * LANES:(h + 1) * LANES])
    return jnp.concatenate(parts, axis=1)


def _mla_rows(proj_b):
    return [(proj_b, WB_CQ // Q_LORA, Q_LORA), (proj_b, WB_CKV // KV_LORA, KV_LORA), (proj_b, WB_KR // LANES, LANES)]


def _mla_prep_fwd(proj_b, pos, qn_w, kvn_w, uq, uk, uv):
    def fn(r, c):
        cq, ckv, kr, pos_ = r
        qn_w_, kvn_w_, uq_, uk_, uv_, rope = c
        c_q = _rms_fwd(cq, qn_w_).astype(BF16)
        c_kv = _rms_fwd(ckv, kvn_w_).astype(BF16)
        qf = _dot(c_q, uq_)
        qr = _rope_apply(qf[:, D_MODEL:], _rope_tables(pos_, rope, D_MODEL))
        kr = _rope_apply(kr, _rope_tables(pos_, rope, LANES))
        kc = _interleave_heads(_dot(c_kv, uk_), kr)
        v = _dot(c_kv, uv_)
        return [c_q, c_kv, _interleave_heads(qf[:, :D_MODEL], qr) * SCALE, kc, v, kc, v], []

    wide2 = N_HEADS * QK_HEAD
    return _rowwise(fn, _mla_rows(proj_b) + [pos], [qn_w, kvn_w, uq, uk, uv, _rope_consts()],
                    [(Q_LORA, BF16), (KV_LORA, BF16), (wide2, BF16), (wide2, BF16), (D_MODEL, BF16),
                     (wide2, BF16, "T"), (D_MODEL, BF16, "T")], tm=FUSED_ROWS, name="mla_prep_fwd")


def _mla_prep_bwd(proj_b, pos, d_qc, d_kc, d_v, qn_w, kvn_w, uq, uk, uv):
    def fn(r, c):
        cq, ckv, _, pos_, dq, dk, dv = r
        qn_w_, kvn_w_, uq_, uk_, uv_, rope = c
        even = lambda t: jnp.concatenate([t[:, (2 * h) * LANES:(2 * h + 1) * LANES] for h in range(N_HEADS)], axis=1)
        odd = lambda t: jnp.concatenate([t[:, (2 * h + 1) * LANES:(2 * h + 2) * LANES] for h in range(N_HEADS)], axis=1)
        d_qr_raw = _rope_transpose(odd(dq), _rope_tables(pos_, rope, D_MODEL)) * SCALE
        d_qf = jnp.concatenate([even(dq) * SCALE, d_qr_raw], axis=1).astype(BF16)
        d_kn = even(dk).astype(BF16)
        dkr = dk[:, LANES:2 * LANES]
        for h in range(1, N_HEADS):
            dkr = dkr + dk[:, (2 * h + 1) * LANES:(2 * h + 2) * LANES]
        d_cq, d_qnw = _rms_bwd(cq, qn_w_, _dot(d_qf, uq_, NT))
        d_ckv, d_kvnw = _rms_bwd(ckv, kvn_w_, _dot(d_kn, uk_, NT) + _dot(dv, uv_, NT))
        return [d_qf, d_kn, d_cq, d_ckv, _rope_transpose(dkr, _rope_tables(pos_, rope, LANES))], [d_qnw, d_kvnw]

    return _rowwise(fn, _mla_rows(proj_b) + [pos, d_qc, d_kc, d_v], [qn_w, kvn_w, uq, uk, uv, _rope_consts()],
                    [(2 * D_MODEL, BF16), (D_MODEL, BF16), (Q_LORA, BF16), (KV_LORA, BF16), (LANES, BF16)],
                    accs=[(1, Q_LORA), (1, KV_LORA)], tm=FUSED_ROWS, name="mla_prep_bwd")


def _causal_mask_t(st, key0, query0):
    key = lax.broadcasted_iota(jnp.int32, st.shape, 0) + key0
    query = lax.broadcasted_iota(jnp.int32, st.shape, 1) + query0
    return jnp.where(key <= query, st, NEG_BIG)


def _attn_tiles(s_dim):
    tq = min(512, s_dim)
    n_chains = 2 if s_dim >= 2 * tq else 1
    return tq, n_chains, min(512, s_dim)


def _diagonal_chains(t, tq, n_chains, tk):
    return [(c, (t + 1) * tk - 1 > c * tq) for c in range(n_chains) if t * tk < (c + 1) * tq]


def _attn_fwd(qc, kc, vt):
    s_dim = qc.shape[0]
    tq, n_chains, tk = _attn_tiles(s_dim)
    tqs = tq * n_chains

    def body(q_ref, k_ref, vt_ref, o_ref, lse_ref, m_s, l_s, acc):
        qi = pl.program_id(1)
        m_s[...] = jnp.full_like(m_s, NEG_BIG)
        l_s[...] = jnp.zeros_like(l_s)
        acc[...] = jnp.zeros_like(acc)

        def make_step(chains):
            def step(j, carry):
                ks = pl.multiple_of(j * tk, tk)
                kb, vtb = k_ref[pl.ds(ks, tk), :], vt_ref[:, pl.ds(ks, tk)]
                cols = [slice(c * tq, (c + 1) * tq) for c, _ in chains]
                sts = [_dot(kb, q_ref[cs, :], NT) for cs in cols]
                sts = [_causal_mask_t(st, j * tk, qi * tqs + c * tq) if masked else st
                       for st, (c, masked) in zip(sts, chains)]
                m_prevs = [m_s[:, cs] for cs in cols]
                m_news = [jnp.maximum(mp, jnp.max(st, axis=0, keepdims=True)) for mp, st in zip(m_prevs, sts)]
                alphas = [jnp.exp(mp - mn) for mp, mn in zip(m_prevs, m_news)]
                pts = [jnp.exp(st - mn) for st, mn in zip(sts, m_news)]
                pvs = [_dot(vtb, pt) for pt in pts]
                for cs, mn, al, pt, pv in zip(cols, m_news, alphas, pts, pvs):
                    l_s[:, cs] = al * l_s[:, cs] + _colsum(pt)
                    m_s[:, cs] = mn
                    acc[:, cs] = acc[:, cs] * al + pv
                return carry
            return step

        below = qi * (tqs // tk)
        lax.fori_loop(0, below, make_step([(c, False) for c in range(n_chains)]), 0)
        for t in range(tqs // tk):
            make_step(_diagonal_chains(t, tq, n_chains, tk))(below + t, 0)
        l = l_s[...]
        o_ref[...] = jnp.transpose(acc[...] / l)
        lse_ref[...] = m_s[...] + jnp.log(l)

    return pl.pallas_call(
        body,
        out_shape=[jax.ShapeDtypeStruct((s_dim, N_HEADS * HEAD), F32), jax.ShapeDtypeStruct((N_HEADS, 1, s_dim), F32)],
        grid=(N_HEADS, s_dim // tqs),
        in_specs=[pl.BlockSpec((tqs, QK_HEAD), lambda h, qi: (qi, h)),
                  pl.BlockSpec((s_dim, QK_HEAD), lambda h, qi: (0, h)),
                  pl.BlockSpec((HEAD, s_dim), lambda h, qi: (h, 0))],
        out_specs=[pl.BlockSpec((tqs, HEAD), lambda h, qi: (qi, h)),
                   pl.BlockSpec((None, 1, tqs), lambda h, qi: (h, 0, qi))],
        scratch_shapes=[pltpu.VMEM((1, tqs), F32), pltpu.VMEM((1, tqs), F32), pltpu.VMEM((HEAD, tqs), F32)],
        compiler_params=pltpu.CompilerParams(dimension_semantics=("parallel", "parallel")),
        name="attn_fwd",
    )(qc, kc, vt)


def _attn_bwd(qc, kc, kct, v, o, d_o, lse):
    s_dim = qc.shape[0]
    tq, n_chains, tk = _attn_tiles(s_dim)
    tqs = tq * n_chains

    def body(q_ref, k_ref, kt_ref, v_ref, o_ref, do_ref, lse_ref, dq_ref, dk_ref, dv_ref, dqt_acc, dv_acc):
        qi = pl.program_id(1)

        @pl.when(qi == 0)
        def _():
            dk_ref[...] = jnp.zeros_like(dk_ref)
            dv_acc[...] = jnp.zeros_like(dv_acc)

        dqt_acc[...] = jnp.zeros_like(dqt_acc)
        do_f = do_ref[...]
        do_all = do_f.astype(BF16)
        q_all = q_ref[...]
        lse_row = lse_ref[...]
        delta_row = _dot3(jnp.ones((8, HEAD), F32), o_ref[...] * do_f, NT)[0:1, :]

        def make_step(chains):
            rows = slice(chains[0][0] * tq, (chains[-1][0] + 1) * tq)

            def step(j, carry):
                ks = pl.multiple_of(j * tk, tk)
                kb, vb, ktb = k_ref[pl.ds(ks, tk), :], v_ref[pl.ds(ks, tk), :], kt_ref[:, pl.ds(ks, tk)]
                cols = [slice(c * tq, (c + 1) * tq) for c, _ in chains]
                sts = [_dot(kb, q_all[cs, :], NT) for cs in cols]
                sts = [_causal_mask_t(st, j * tk, qi * tqs + c * tq) if masked else st
                       for st, (c, masked) in zip(sts, chains)]
                dpts = [_dot(vb, do_all[cs, :], NT) for cs in cols]
                pts = [jnp.exp(st - lse_row[:, cs]) for st, cs in zip(sts, cols)]
                dsts = [(pt * (dpt - delta_row[:, cs])).astype(BF16) for pt, dpt, cs in zip(pts, dpts, cols)]
                pts = [pt.astype(BF16) for pt in pts]
                dqs = [_dot(ktb, dst) for dst in dsts]
                for cs, dq in zip(cols, dqs):
                    dqt_acc[:, cs] += dq
                pt_all = jnp.concatenate(pts, axis=1) if len(chains) > 1 else pts[0]
                dst_all = jnp.concatenate(dsts, axis=1) if len(chains) > 1 else dsts[0]
                dk_ref[pl.ds(ks, tk), :] += _dot(dst_all, q_all[rows, :])
                dv_acc[pl.ds(ks, tk), :] += _dot(pt_all, do_all[rows, :])
                return carry
            return step

        below = qi * (tqs // tk)
        lax.fori_loop(0, below, make_step([(c, False) for c in range(n_chains)]), 0)
        for t in range(tqs // tk):
            make_step(_diagonal_chains(t, tq, n_chains, tk))(below + t, 0)
        dq_ref[...] = jnp.transpose(dqt_acc[...])

        @pl.when(qi == s_dim // tqs - 1)
        def _():
            dv_ref[...] = dv_acc[...].astype(dv_ref.dtype)

    q_spec = pl.BlockSpec((tqs, QK_HEAD), lambda h, qi: (qi, h))
    o_spec = pl.BlockSpec((tqs, HEAD), lambda h, qi: (qi, h))
    k_spec = pl.BlockSpec((s_dim, QK_HEAD), lambda h, qi: (0, h))
    v_spec = pl.BlockSpec((s_dim, HEAD), lambda h, qi: (0, h))
    wide2 = jax.ShapeDtypeStruct((s_dim, N_HEADS * QK_HEAD), F32)
    return pl.pallas_call(
        body,
        out_shape=[wide2, wide2, jax.ShapeDtypeStruct((s_dim, N_HEADS * HEAD), BF16)],
        grid=(N_HEADS, s_dim // tqs),
        in_specs=[q_spec, k_spec, pl.BlockSpec((QK_HEAD, s_dim), lambda h, qi: (h, 0)), v_spec, o_spec, o_spec,
                  pl.BlockSpec((None, 1, tqs), lambda h, qi: (h, 0, qi))],
        out_specs=[q_spec, k_spec, v_spec],
        scratch_shapes=[pltpu.VMEM((QK_HEAD, tqs), F32), pltpu.VMEM((s_dim, HEAD), F32)],
        compiler_params=pltpu.CompilerParams(dimension_semantics=("parallel", "arbitrary")),
        name="attn_bwd",
    )(qc, kc, kct, v, o, d_o, lse)


def _mix_proj_ln1(y_dn, y_mla, proj_g, x, w_o, g, b):
    s_dim = x.shape[0]
    tm = min(512, s_dim)

    def body(yd_ref, ym_ref, g_ref, x_ref, w_ref, lg_ref, lb_ref, mixed_ref, a1_ref, h1_ref, h1b_ref):
        gates = g_ref[...].astype(F32)
        mixed = (_sig(gates[:, :D_MODEL]) * yd_ref[...].astype(F32)
                 + _sig(gates[:, D_MODEL:]) * ym_ref[...].astype(F32)).astype(BF16)
        a1 = _dot(mixed, w_ref[...])
        xh, _ = _ln_stats(ALPHA * x_ref[...] + a1)
        y = xh * lg_ref[...] + lb_ref[...]
        mixed_ref[...] = mixed
        a1_ref[...] = a1
        h1_ref[...] = y
        h1b_ref[...] = y.astype(BF16)

    row = lambda width: pl.BlockSpec((tm, width), lambda i: (i, 0))
    whole = lambda a: pl.BlockSpec(a.shape, lambda i: (0, 0))
    sds = lambda dt: jax.ShapeDtypeStruct((s_dim, D_MODEL), dt)
    return pl.pallas_call(
        body,
        out_shape=[sds(BF16), sds(F32), sds(F32), sds(BF16)],
        grid=(s_dim // tm,),
        in_specs=[row(D_MODEL), row(D_MODEL), row(2 * D_MODEL), row(D_MODEL), whole(w_o), whole(g), whole(b)],
        out_specs=[row(D_MODEL)] * 4,
        compiler_params=pltpu.CompilerParams(dimension_semantics=("parallel",)),
        name="mix_proj_ln1",
    )(y_dn, y_mla, proj_g, x, w_o, g, b)


def _ln1_mix_bwd(x, a1, d_h1, d_pg, y_dn, y_mla, proj_g, g, w_o, w_pg):
    def fn(r, c):
        x_, a1_, dy, dpg, yd, ym, gates = r
        g_, w_o_, w_pg_ = c
        dy = dy + _dot(dpg, w_pg_, NT)
        xh, rr = _ln_stats(ALPHA * x_ + a1_)
        dz = _ln_bwd(dy, xh, rr, g_)
        dz_b = dz.astype(BF16)
        dm = _dot(dz_b, w_o_, NT)
        sd, sm = _sig(gates[:, :D_MODEL]), _sig(gates[:, D_MODEL:])
        d_g = jnp.concatenate([dm * yd * sd * (1.0 - sd), dm * ym * sm * (1.0 - sm)], axis=1)
        return [dz_b, ALPHA * dz, d_g, dm * sd, dm * sm], [_colsum(dy * xh), _colsum(dy)]

    return _rowwise(fn, [x, a1, d_h1, d_pg, y_dn, y_mla, proj_g], [g, w_o, w_pg],
                    [(D_MODEL, BF16), (D_MODEL, F32), (2 * D_MODEL, BF16), (D_MODEL, BF16), (D_MODEL, BF16)],
                    accs=[(1, D_MODEL), (1, D_MODEL)], tm=FUSED_ROWS, name="ln1_mix_bwd")


def _ln_stats(z):
    mu = _rowmean(z)
    zc = z - mu
    r = lax.rsqrt(_rowmean(zc * zc) + EPS_LN)
    return zc * r, r


def _ln_bwd(dy, xh, r, g):
    dxh = dy * g
    return r * (dxh - _rowmean(dxh) - xh * _rowmean(dxh * xh))


def _ffn_in_act(h1b, w_t):
    s_dim, k_dim = h1b.shape
    hidden = w_t.shape[0] // 2
    tm, tn = min(512, s_dim), _pick_wide(hidden)
    nt = hidden // tn

    def body(a_ref, bg_ref, bu_ref, gt_ref, up_ref, act_ref):
        a = a_ref[...]
        gt, up = _dot(a, bg_ref[...], NT), _dot(a, bu_ref[...], NT)
        gt_ref[...] = gt.astype(BF16)
        up_ref[...] = up.astype(BF16)
        act_ref[...] = (gt * _sig(gt) * up).astype(BF16)

    o_spec = pl.BlockSpec((tm, tn), lambda j, i: (i, j))
    sds = jax.ShapeDtypeStruct((s_dim, hidden), BF16)
    return pl.pallas_call(
        body,
        out_shape=[sds, sds, sds],
        grid=(nt, s_dim // tm),
        in_specs=[pl.BlockSpec((tm, k_dim), lambda j, i: (i, 0)), pl.BlockSpec((tn, k_dim), lambda j, i: (j, 0)),
                  pl.BlockSpec((tn, k_dim), lambda j, i: (j + nt, 0))],
        out_specs=[o_spec, o_spec, o_spec],
        compiler_params=pltpu.CompilerParams(dimension_semantics=("parallel", "parallel")),
        name="ffn_in_act",
    )(h1b, w_t, w_t)


def _act_bwd(gt, up, d_act):
    def fn(r, c):
        gt_, up_, da = r
        return [jnp.concatenate([da * up_ * _silu_grad(gt_), da * gt_ * _sig(gt_)], axis=1)], []

    return _rowwise(fn, [gt, up, d_act], [], [(2 * FFN_HIDDEN, BF16)], name="act_bwd")[0]


def _tail(h1, ffn, p, tgt, g, b, w_pg, w_ple_t):
    def fn(r, c):
        h1_, ffn_, p_, t_ = r
        pg_ = _dot(h1_, c[2])
        pp_ = _dot(p_, c[3], NT)
        sp = _sig(pg_)
        xh, rr = _ln_stats(ALPHA * h1_ + ffn_ + sp * pp_)
        y = xh * c[0] + c[1]
        err = y - t_
        dy = err * (1.0 / D_MODEL)
        dz = _ln_bwd(dy, xh, rr, c[0])
        loss = jnp.sum(0.5 * _rowmean(err * err), axis=0, keepdims=True)
        return ([dz, dz * pp_ * sp * (1.0 - sp), dz * sp, ALPHA * dz],
                [_colsum(dy * xh), _colsum(dy), jnp.broadcast_to(loss, (1, LANES))])

    return _rowwise(fn, [h1, ffn, p, tgt], [g, b, w_pg, w_ple_t], [(D_MODEL, BF16)] * 3 + [(D_MODEL, F32)],
                    accs=[(1, D_MODEL), (1, D_MODEL), (1, LANES)], tm=FUSED_ROWS, name="tail")


def _local_step(x, p, pos, tgt, w, late_weights, emit):
    w = dict(w)
    s_dim = x.shape[0]
    xb, pb = x.astype(BF16), p.astype(BF16)
    proj_a = _mm(xb, w["wa_t"], tb=True, name="f_proj_a")
    proj_g = _mm(xb, w["wg_t"], tb=True, out_dtype=BF16, name="f_proj_g")
    proj_b = _mm(xb, w["wb_t"], tb=True, name="f_proj_b")
    qkvn = _conv_fwd(proj_a, w["conv"])
    beta, gc = _gates_fwd(proj_b, w["alog"], w["dtb"])
    gc_t = jnp.transpose(gc[:, :N_HEADS])
    u, w_, qd, kt, a_mat, t_fold = _gdr_prep_fwd(qkvn, beta, gc, gc_t)
    o_dn, states = _gdr_scan_fwd(u, w_, qd, kt, a_mat, gc)
    w.update(late_weights("mix", o_dn))
    og, y_dn = _gdr_out_fwd(o_dn, proj_a, w["dnw"], w["br_dn"])
    c_q, c_kv, qc, kc, vv, kct, vt = _mla_prep_fwd(proj_b, pos, w["qnw"], w["kvnw"], w["uq"], w["uk"], w["uv"])
    o_mla, lse = _attn_fwd(qc, kc, vt)
    y_mla = _mm(o_mla, w["br_mla"], out_dtype=BF16, name="f_y_mla")
    mixed, a1, h1, h1b = _mix_proj_ln1(y_dn, y_mla, proj_g, x, w["wo"], w["ln1g"], w["ln1b"])
    w.update(late_weights("ffn", a1))
    gt, up, act = _ffn_in_act(h1b, w["ffn_in_t"])
    ffn = _mm(act, w["ffn_out"], name="f_ffn")
    g = {}
    dz2, d_pg, d_pp, dh1a, g["ln2g"], g["ln2b"], loss = _tail(h1, ffn, pb, tgt, w["ln2g"], w["ln2b"],
                                                            w["ple_gate"], w["ple_t"])
    g["ple_t"] = _mm(d_pp, pb, ta=True, out_dtype=BF16, name="b_w_ple")
    g["ple_gate"] = _mm(h1b, d_pg, ta=True, out_dtype=BF16, name="b_w_ple_gate")
    g["ffn_out"] = _mm(act, dz2, ta=True, out_dtype=BF16, name="b_w_ffn_out")
    d_act = _mm(dz2, w["ffn_out"], tb=True, out_dtype=BF16, name="b_act")
    d_gu = _act_bwd(gt, up, d_act)
    g["ffn_in_t"] = _mm(d_gu, h1b, ta=True, out_dtype=BF16, name="b_w_ffn_in")
    d_gu = emit("ffn", g, d_gu)
    d_h1 = _mm(d_gu, w["ffn_in_t"], add=(dh1a,), name="b_h1_ffn")
    dz1, dxa, d_proj_g, d_y_dn, d_y_mla, g["ln1g"], g["ln1b"] = _ln1_mix_bwd(
        x, a1, d_h1, d_pg, y_dn, y_mla, proj_g, w["ln1g"], w["wo"], w["ple_gate"])
    g["wo"] = _mm(mixed, dz1, ta=True, out_dtype=BF16, name="b_w_o")
    g["br_mla"] = _mm(o_mla, d_y_mla, ta=True, out_dtype=BF16, name="b_w_br_mla")
    d_o_mla = _mm(d_y_mla, w["br_mla"], tb=True, out_dtype=BF16, name="b_o_mla")
    d_qc, d_kc, d_v = _attn_bwd(qc, kc, kct, vv, o_mla, d_o_mla, lse)
    d_q_full, d_kn, d_cq, d_ckv, d_kr, g["qnw"], g["kvnw"] = _mla_prep_bwd(
        proj_b, pos, d_qc, d_kc, d_v, w["qnw"], w["kvnw"], w["uq"], w["uk"], w["uv"])
    g["uq"] = _mm(c_q, d_q_full, ta=True, out_dtype=BF16, name="b_w_uq")
    g["uk"] = _mm(c_kv, d_kn, ta=True, out_dtype=BF16, name="b_w_uk")
    g["uv"] = _mm(c_kv, d_v, ta=True, out_dtype=BF16, name="b_w_uv")
    g["br_dn"] = _mm(og, d_y_dn, ta=True, out_dtype=BF16, name="b_w_br_dn")
    d_y_dn = emit("mix", g, d_y_dn)
    d_o_dn, d_z, g["dnw"] = _gdr_out_bwd(o_dn, proj_a, d_y_dn, w["dnw"], w["br_dn"])
    du, dw, dqd, dkt, d_a, d_egl = _gdr_scan_bwd(u, w_, qd, kt, a_mat, gc, states, d_o_dn)
    dq, dk, dv, d_beta, d_gc = _gdr_prep_bwd(qkvn, beta, gc, gc_t, t_fold, u, w_, du, dw, dqd, dkt, d_a)
    d_egl_rows = jnp.pad(d_egl[:, None, :, 0], ((0, 0), (CHUNK - 1, 0), (0, LANES - N_HEADS))).reshape(s_dim, LANES)
    d_ba, g["alog"], g["dtb"] = _gates_bwd(proj_b, w["alog"], w["dtb"], gc, d_beta, d_gc, d_egl_rows)
    d_qkv, g["conv"] = _conv_bwd(proj_a, w["conv"], dq, dk, dv)
    zeros = jnp.zeros((s_dim, WB_CKV - Q_LORA), BF16)
    d_proj_b = jnp.concatenate([d_cq, zeros, d_ckv, d_kr, d_ba], axis=1)
    g["wa_qkv_t"] = _mm(d_qkv, xb, ta=True, name="b_w_qkv")
    g["wa_z_t"] = _mm(d_z, xb, ta=True, name="b_w_z")
    g["wg_t"] = _mm(d_proj_g, xb, ta=True, name="b_w_g")
    g["wb_t"] = _mm(d_proj_b, xb, ta=True, name="b_w_b")
    d_qkv = emit("w_in", g, d_qkv)
    dx = _mm(d_qkv, w["wa_qkv_t"], add=(dxa,), name="b_x_qkv")
    dx = _mm(d_z, w["wa_z_t"], add=(dx,), name="b_x_z")
    dx = _mm(d_proj_g, w["wg_t"], add=(dx,), name="b_x_g")
    dx = _mm(d_proj_b, w["wb_t"], add=(dx,), name="b_x_b")
    return loss, dx, g


_BIG = (("w_in", 1), ("w_uq", 0), ("w_uk", 0), ("w_uv", 0), ("w_br_dn", 0), ("w_br_mla", 0),
        ("w_o", 0), ("w_ffn_in", 1), ("w_ffn_out", 0), ("w_ple", 1), ("w_ple_gate", 0))
_BIG_AXIS = dict(_BIG)
_SMALL = ("ln1_g", "ln1_b", "ln2_g", "ln2_b", "q_norm_w", "kv_norm_w", "dn_norm_w", "dn_a_log", "dn_dt_bias")
_ORDER = ("w_in", "conv_w", "dn_a_log", "dn_dt_bias", "dn_norm_w", "q_norm_w", "w_uq", "kv_norm_w", "w_uk", "w_uv",
          "w_br_dn", "w_br_mla", "w_o", "ln1_g", "ln1_b", "w_ffn_in", "w_ffn_out", "w_ple", "w_ple_gate", "ln2_g",
          "ln2_b")


def _stored_shape(name, shard_shape):
    axis = _BIG_AXIS[name]
    lead = shard_shape[axis]
    return lead, int(np.prod(shard_shape)) // lead


def _to_stored(name, shard):
    return jnp.moveaxis(shard, _BIG_AXIS[name], 0).reshape(_stored_shape(name, shard.shape))


def _from_stored(name, stored, shard_shape):
    axis = _BIG_AXIS[name]
    moved = (shard_shape[axis],) + shard_shape[:axis] + shard_shape[axis + 1:]
    return jnp.moveaxis(stored.reshape(moved), 0, axis)


_W_IN_ROWS = np.cumsum([0, 3072, 1024, 8, 8, Q_LORA, KV_LORA, ROPE, D_MODEL, D_MODEL])


def _first_weights(w_in_t, conv_full, small):
    r = _W_IN_ROWS
    zr = lambda n: jnp.zeros((n, D_MODEL), w_in_t.dtype)
    w = {}
    w["wa_t"] = w_in_t[r[0]:r[2]]
    w["wa_qkv_t"], w["wa_z_t"] = w_in_t[r[0]:r[1]], w_in_t[r[1]:r[2]]
    w["wg_t"] = w_in_t[r[7]:r[9]]
    w["wb_t"] = jnp.concatenate([w_in_t[r[4]:r[5]], zr(WB_CKV - Q_LORA), w_in_t[r[5]:r[7]], zr(LANES - ROPE),
                                 w_in_t[r[2]:r[4]], zr(LANES - 2 * N_HEADS)], axis=0)
    w["conv"] = conv_full
    pad_l = lambda v: jnp.pad(v, ((0, 0), (0, LANES - v.shape[1])))
    w["alog"], w["dtb"] = pad_l(small["dn_a_log"]), pad_l(small["dn_dt_bias"])
    w["dnw"], w["qnw"], w["kvnw"] = small["dn_norm_w"], small["q_norm_w"], small["kv_norm_w"]
    w["ln1g"], w["ln1b"], w["ln2g"], w["ln2b"] = small["ln1_g"], small["ln1_b"], small["ln2_g"], small["ln2_b"]
    return w


def _late_weights(group, fw):
    w = {}
    if group == "mix":
        uq = fw["w_uq"].reshape(Q_LORA, N_HEADS, HEAD + ROPE)
        uq_r = jnp.pad(uq[:, :, HEAD:], ((0, 0), (0, 0), (0, HEAD - ROPE)))
        w["uq"] = jnp.concatenate([uq[:, :, :HEAD].reshape(Q_LORA, -1), uq_r.reshape(Q_LORA, -1)], axis=1)
        w["uk"], w["uv"] = fw["w_uk"], fw["w_uv"]
        w["br_dn"], w["br_mla"], w["wo"] = fw["w_br_dn"], fw["w_br_mla"], fw["w_o"]
    else:
        w["ffn_in_t"], w["ffn_out"] = fw["w_ffn_in"], fw["w_ffn_out"]
        w["ple_t"], w["ple_gate"] = fw["w_ple"], fw["w_ple_gate"]
    return w


_GROUP_GRADS = {"ffn": (("w_ple", "ple_t"), ("w_ple_gate", "ple_gate"), ("w_ffn_out", "ffn_out"),
                        ("w_ffn_in", "ffn_in_t")),
                "mix": (("w_o", "wo"), ("w_br_mla", "br_mla"), ("w_uq", "uq"), ("w_uk", "uk"), ("w_uv", "uv"),
                        ("w_br_dn", "br_dn"))}


def _group_grads(group, g):
    out = {}
    for name, key in _GROUP_GRADS[group]:
        t = g[key]
        if name == "w_uq":
            uq_n = t[:, :D_MODEL].reshape(Q_LORA, N_HEADS, HEAD)
            uq_r = t[:, D_MODEL:].reshape(Q_LORA, N_HEADS, HEAD)[:, :, :ROPE]
            t = jnp.concatenate([uq_n, uq_r], axis=2).reshape(Q_LORA, -1)
        out[name] = t
    return out


def _last_grads(g):
    wb = g["wb_t"]
    w_in = jnp.concatenate([
        g["wa_qkv_t"], g["wa_z_t"], wb[WB_BA:WB_BA + 2 * N_HEADS], wb[WB_CQ:WB_CQ + Q_LORA],
        wb[WB_CKV:WB_CKV + KV_LORA], wb[WB_KR:WB_KR + ROPE], g["wg_t"]], axis=0)
    small = {"ln1_g": g["ln1g"], "ln1_b": g["ln1b"], "ln2_g": g["ln2g"], "ln2_b": g["ln2b"], "q_norm_w": g["qnw"],
             "kv_norm_w": g["kvnw"], "dn_norm_w": g["dnw"], "dn_a_log": g["alog"], "dn_dt_bias": g["dtb"],
             "conv_w": g["conv"]}
    return w_in, small


_SMALL_SLOTS = {"ln1_g": (0, 0, 1024), "ln1_b": (1, 0, 1024), "ln2_g": (2, 0, 1024), "ln2_b": (3, 0, 1024),
                "q_norm_w": (4, 0, 384), "kv_norm_w": (4, 384, 256), "dn_norm_w": (4, 640, 128),
                "dn_a_log": (4, 768, 8), "dn_dt_bias": (4, 896, 8)}
_SMALL_ROWS, _LOSS_ROW, _CONV_ROW0, _CONV_ROWS = 24, 5, 8, 12


def _pack_small_grads(small_g, loss):
    zeros = lambda r, c: jnp.zeros((r, c), F32)
    row4 = jnp.concatenate([small_g["q_norm_w"], small_g["kv_norm_w"], small_g["dn_norm_w"], small_g["dn_a_log"],
                            small_g["dn_dt_bias"]], axis=1)
    row5 = jnp.concatenate([loss, zeros(1, FLAT_COLS - LANES)], axis=1)
    head = jnp.concatenate([small_g["ln1_g"], small_g["ln1_b"], small_g["ln2_g"], small_g["ln2_b"], row4, row5,
                            zeros(2, FLAT_COLS)], axis=0)
    conv = small_g["conv_w"].reshape(_CONV_ROWS, FLAT_COLS)
    return jnp.concatenate([head, conv, zeros(_SMALL_ROWS - _CONV_ROW0 - _CONV_ROWS, FLAT_COLS)], axis=0)


_MESH_ID = pl.DeviceIdType.MESH
_ANY = pl.BlockSpec(memory_space=pl.ANY)


def _all_gather(blocks, name):
    n = len(blocks)

    def body(*refs):
        x_refs, out_refs = refs[:n], refs[n:2 * n]
        send_sems, recv_sems, local_sems = refs[2 * n:]
        x, y, c = lax.axis_index("x"), lax.axis_index("y"), lax.axis_index("c")
        me, sibling = (x, y, c), (x, y, 1 - c)
        chips = [(1 - x, y), (x, 1 - y), (1 - x, 1 - y)]

        def slot(i, px, py, pc):
            return out_refs[i].at[4 * px + 2 * py + pc]

        def copy(i, k, origin, to, src=None):
            return pltpu.make_async_remote_copy(
                src_ref=slot(i, *origin) if src is None else src, dst_ref=slot(i, *origin),
                send_sem=send_sems.at[7 * i + k], recv_sem=recv_sems.at[7 * i + k], device_id=to,
                device_id_type=_MESH_ID)

        mine = [pltpu.make_async_copy(x_refs[i], slot(i, *me), local_sems.at[i]) for i in range(n)]
        first, passed = [], []
        for i in range(n):
            mine[i].start()
            first.append(copy(i, 0, me, sibling, src=x_refs[i]))
            first += [copy(i, 1 + j, me, (*chip, c), src=x_refs[i]) for j, chip in enumerate(chips)]
        for cp in first:
            cp.start()
        for i in range(n):
            for j, chip in enumerate(chips):
                copy(i, 1 + j, (*chip, c), me).wait_recv()
                passed.append(copy(i, 4 + j, (*chip, c), sibling))
                passed[-1].start()
        for i in range(n):
            copy(i, 0, sibling, me).wait_recv()
            for j, chip in enumerate(chips):
                copy(i, 4 + j, (*chip, 1 - c), me).wait_recv()
        for cp in first + passed:
            cp.wait_send()
        for cp in mine:
            cp.wait()

    return pl.pallas_call(
        body,
        out_shape=[jax.ShapeDtypeStruct((N_DEV,) + b.shape, b.dtype) for b in blocks],
        in_specs=[_ANY] * n,
        out_specs=[_ANY] * n,
        scratch_shapes=[pltpu.SemaphoreType.DMA((7 * n,)), pltpu.SemaphoreType.DMA((7 * n,)),
                        pltpu.SemaphoreType.DMA((n,))],
        name=name,
    )(*blocks)


def _exchange_sibling(srcs, name):
    n = len(srcs)

    def body(*refs):
        src_refs, dst_refs = refs[:n], refs[n:2 * n]
        send_sems, recv_sems = refs[2 * n:]
        x, y, c = lax.axis_index("x"), lax.axis_index("y"), lax.axis_index("c")
        copies = [pltpu.make_async_remote_copy(
            src_ref=src_refs[i].at[2 * q + (1 - c)], dst_ref=dst_refs[i].at[q], send_sem=send_sems.at[4 * i + q],
            recv_sem=recv_sems.at[4 * i + q], device_id=(x, y, 1 - c), device_id_type=_MESH_ID)
            for i in range(n) for q in range(4)]
        for cp in copies:
            cp.start()
        for cp in copies:
            cp.wait_recv()
        for cp in copies:
            cp.wait_send()

    return pl.pallas_call(
        body,
        out_shape=[jax.ShapeDtypeStruct((4,) + s.shape[1:], s.dtype) for s in srcs],
        in_specs=[_ANY] * n,
        out_specs=[_ANY] * n,
        scratch_shapes=[pltpu.SemaphoreType.DMA((4 * n,)), pltpu.SemaphoreType.DMA((4 * n,))],
        name=name,
    )(*srcs)


def _col_tile(c):
    return c if c <= 256 else 256


def _chip_sum(src, recv, parity, name):
    _, r, c = src.shape
    tc = _col_tile(c)

    def body(par_ref, a_ref, b_ref, o_ref, ob_ref):
        s = a_ref[...] + b_ref[...]
        o_ref[...] = s
        ob_ref[...] = s.astype(BF16)

    blk = lambda f: pl.BlockSpec((None, r, tc), f)
    return pl.pallas_call(
        body,
        out_shape=[jax.ShapeDtypeStruct((4, r, c), F32), jax.ShapeDtypeStruct((4, r, c), BF16)],
        grid_spec=pltpu.PrefetchScalarGridSpec(
            num_scalar_prefetch=1, grid=(4, c // tc),
            in_specs=[blk(lambda q, j, par: (2 * q + par[0], 0, j)), blk(lambda q, j, par: (q, 0, j))],
            out_specs=[blk(lambda q, j, par: (q, 0, j)), blk(lambda q, j, par: (q, 0, j))]),
        compiler_params=pltpu.CompilerParams(dimension_semantics=("parallel", "parallel")),
        name=name,
    )(parity, src, recv)


def _sum_parts(own, others, chip, name):
    _, r, c = own.shape
    tc = _col_tile(c)

    def body(q_ref, a_ref, b_ref, o_ref):
        o_ref[...] = ((a_ref[...] + b_ref[0].astype(F32)) + b_ref[1].astype(F32)) + b_ref[2].astype(F32)

    return pl.pallas_call(
        body,
        out_shape=jax.ShapeDtypeStruct((r, c), F32),
        grid_spec=pltpu.PrefetchScalarGridSpec(
            num_scalar_prefetch=1, grid=(c // tc,),
            in_specs=[pl.BlockSpec((None, r, tc), lambda j, q: (q[0], 0, j)),
                      pl.BlockSpec((3, r, tc), lambda j, q: (0, 0, j))],
            out_specs=pl.BlockSpec((r, tc), lambda j, q: (0, j))),
        compiler_params=pltpu.CompilerParams(dimension_semantics=("parallel",)),
        name=name,
    )(chip, own, others)


_HBM = pl.BlockSpec(memory_space=pltpu.HBM)
_SEM = pl.BlockSpec(memory_space=pltpu.SEMAPHORE)
_DATAFLOW = pltpu.SideEffectType.DATAFLOW_SIDE_EFFECTING
N_PEERS = N_DEV - 1


def _ring_peer(j):
    me = 4 * lax.axis_index("x") + 2 * lax.axis_index("y") + lax.axis_index("c")
    k = (me + j) % N_DEV
    return me, k, (k // 4, (k // 2) % 2, k % 2)


def _spread_copy(i, j, src_refs, land_refs, send_sems, recv_sems, scatter):
    me, k, peer = _ring_peer(j)
    return pltpu.make_async_remote_copy(
        src_ref=src_refs[i].at[k] if scatter else src_refs[i], dst_ref=land_refs[i].at[me],
        send_sem=send_sems.at[N_PEERS * i + j - 1], recv_sem=recv_sems.at[N_PEERS * i + j - 1], device_id=peer,
        device_id_type=_MESH_ID)


def _spread_start(srcs, carry, scatter, name):
    n = len(srcs)
    lands = [lax.empty(((N_DEV,) + s.shape[-2:]), s.dtype) for s in srcs]

    def body(*refs):
        src_refs, land_refs = refs[:n], refs[n:2 * n]
        send_sems, recv_sems, local_sems = refs[2 * n + 1:2 * n + 4]
        for i in range(n):
            for j in range(1, N_DEV):
                _spread_copy(i, j, src_refs, land_refs, send_sems, recv_sems, scatter).start()
        for i in range(n):
            _own_copy(i, src_refs, land_refs, local_sems, scatter).start()

    hbm = lambda a: pltpu.HBM(a.shape, a.dtype)
    sems = pltpu.SemaphoreType.DMA((N_PEERS * n,))
    pinned = [pltpu.with_memory_space_constraint(a, pltpu.HBM) for a in list(srcs) + lands + [carry]]
    res = pl.pallas_call(
        body, name=name,
        out_shape=(sems, sems, pltpu.SemaphoreType.DMA((n,)), *[hbm(a) for a in pinned]),
        in_specs=[_HBM] * (2 * n + 1),
        out_specs=(_SEM, _SEM, _SEM, *[_HBM] * (2 * n + 1)),
        input_output_aliases={i: 3 + i for i in range(2 * n + 1)},
        compiler_params=pltpu.CompilerParams(has_side_effects=_DATAFLOW),
    )(*pinned)
    return res[:3], list(res[3:3 + n]), list(res[3 + n:3 + 2 * n]), res[3 + 2 * n]


def _own_copy(i, src_refs, land_refs, local_sems, scatter):
    me = _ring_peer(0)[0]
    return pltpu.make_async_copy(src_refs[i].at[me] if scatter else src_refs[i], land_refs[i].at[me],
                                 local_sems.at[i])


def _spread_wait(started, after, scatter, name):
    sems, srcs, lands, _ = started
    n = len(srcs)

    def body(*refs):
        src_refs, land_refs = refs[:n], refs[n:2 * n]
        send_s, recv_s, local_s = refs[2 * n:2 * n + 3]
        for i in range(n):
            for j in range(1, N_DEV):
                cp = _spread_copy(i, j, src_refs, land_refs, send_s, recv_s, scatter)
                cp.wait_send()
                cp.wait_recv()
        for i in range(n):
            _own_copy(i, src_refs, land_refs, local_s, scatter).wait()

    hbm = lambda a: pltpu.HBM(a.shape, a.dtype)
    res = pl.pallas_call(
        body, name=name,
        out_shape=tuple(hbm(a) for a in srcs + lands),
        in_specs=[_HBM] * (2 * n) + [_SEM, _SEM, _SEM, pl.BlockSpec(memory_space=pl.ANY)],
        out_specs=tuple([_HBM] * (2 * n)),
        input_output_aliases={i: i for i in range(2 * n)},
        compiler_params=pltpu.CompilerParams(has_side_effects=_DATAFLOW),
    )(*srcs, *lands, *sems, after)
    return list(res[n:])


def _chips_copy(i, j, src_refs, land_refs, send_sems, recv_sems):
    x, y, c = lax.axis_index("x"), lax.axis_index("y"), lax.axis_index("c")
    tx, ty = [(1 - x, y), (x, 1 - y), (1 - x, 1 - y)][j]
    return pltpu.make_async_remote_copy(
        src_ref=src_refs[i].at[2 * tx + ty], dst_ref=land_refs[i].at[j], send_sem=send_sems.at[3 * i + j],
        recv_sem=recv_sems.at[3 * i + j], device_id=(tx, ty, c), device_id_type=_MESH_ID)


def _chips_start(srcs, carry, name):
    n = len(srcs)
    lands = [lax.empty((3,) + s.shape[1:], s.dtype) for s in srcs]

    def body(*refs):
        src_refs, land_refs = refs[:n], refs[n:2 * n]
        send_sems, recv_sems = refs[2 * n + 1:2 * n + 3]
        for i in range(n):
            for j in range(3):
                _chips_copy(i, j, src_refs, land_refs, send_sems, recv_sems).start()

    hbm = lambda a: pltpu.HBM(a.shape, a.dtype)
    sems = pltpu.SemaphoreType.DMA((3 * n,))
    pinned = [pltpu.with_memory_space_constraint(a, pltpu.HBM) for a in list(srcs) + lands + [carry]]
    res = pl.pallas_call(
        body, name=name,
        out_shape=(sems, sems, *[hbm(a) for a in pinned]),
        in_specs=[_HBM] * (2 * n + 1),
        out_specs=(_SEM, _SEM, *[_HBM] * (2 * n + 1)),
        input_output_aliases={i: 2 + i for i in range(2 * n + 1)},
        compiler_params=pltpu.CompilerParams(has_side_effects=_DATAFLOW),
    )(*pinned)
    return res[:2], list(res[2:2 + n]), list(res[2 + n:2 + 2 * n]), res[2 + 2 * n]


def _chips_wait(started, after, name):
    sems, srcs, lands, _ = started
    n = len(srcs)

    def body(*refs):
        src_refs, land_refs = refs[:n], refs[n:2 * n]
        send_s, recv_s = refs[2 * n:2 * n + 2]
        for i in range(n):
            for j in range(3):
                cp = _chips_copy(i, j, src_refs, land_refs, send_s, recv_s)
                cp.wait_send()
                cp.wait_recv()

    hbm = lambda a: pltpu.HBM(a.shape, a.dtype)
    res = pl.pallas_call(
        body, name=name,
        out_shape=tuple(hbm(a) for a in srcs + lands),
        in_specs=[_HBM] * (2 * n) + [_SEM, _SEM, pl.BlockSpec(memory_space=pl.ANY)],
        out_specs=tuple([_HBM] * (2 * n)),
        input_output_aliases={i: i for i in range(2 * n)},
        compiler_params=pltpu.CompilerParams(has_side_effects=_DATAFLOW),
    )(*srcs, *lands, *sems, after)
    return list(res[n:])


def _sum8(landing, name):
    _, r, c = landing.shape
    tc = _col_tile(c)

    def body(a_ref, o_ref):
        tot = a_ref[0].astype(F32)
        for k in range(1, N_DEV):
            tot = tot + a_ref[k].astype(F32)
        o_ref[...] = tot

    return pl.pallas_call(
        body,
        out_shape=jax.ShapeDtypeStruct((r, c), F32),
        grid=(c // tc,),
        in_specs=[pl.BlockSpec((N_DEV, r, tc), lambda j: (0, 0, j))],
        out_specs=pl.BlockSpec((r, tc), lambda j: (0, j)),
        compiler_params=pltpu.CompilerParams(dimension_semantics=("parallel",)),
        name=name,
    )(landing)


def _adamw_math(w, g, m, v):
    m = ADAM_B1 * m + (1.0 - ADAM_B1) * g
    v = ADAM_B2 * v + (1.0 - ADAM_B2) * (g * g)
    m_hat = m / (1.0 - ADAM_B1 ** ADAM_STEP)
    v_hat = v / (1.0 - ADAM_B2 ** ADAM_STEP)
    delta = -ADAM_LR * (m_hat / (jnp.sqrt(v_hat) + ADAM_EPS) + ADAM_WD * w)
    return delta, m, v


def _adamw(w, m, v, g, name):
    r, c = w.shape

    def fn(rows, consts):
        return list(_adamw_math(*rows)), []

    return _rowwise(fn, [w, g, m, v], [], [(c, F32)] * 3, tm=r if r <= 512 else 256, name=name)


def _adamw_sum8(w, m, v, landing, name):
    r, c = w.shape
    tc = _col_tile(c)

    def body(w_ref, m_ref, v_ref, a_ref, g_ref, d_ref, m2_ref, v2_ref):
        g = a_ref[0].astype(F32)
        for k in range(1, N_DEV):
            g = g + a_ref[k].astype(F32)
        delta, m2, v2 = _adamw_math(w_ref[...], g, m_ref[...], v_ref[...])
        g_ref[...] = g
        d_ref[...] = delta
        m2_ref[...] = m2
        v2_ref[...] = v2

    blk = pl.BlockSpec((r, tc), lambda j: (0, j))
    return pl.pallas_call(
        body,
        out_shape=[jax.ShapeDtypeStruct((r, c), F32)] * 4,
        grid=(c // tc,),
        in_specs=[blk, blk, blk, pl.BlockSpec((N_DEV, r, tc), lambda j: (0, 0, j))],
        out_specs=[blk] * 4,
        compiler_params=pltpu.CompilerParams(dimension_semantics=("parallel",)),
        name=name,
    )(w, m, v, landing)


def _adamw_small(gathered, params):
    ns = len(_SMALL)

    def body(*refs):
        g_ref, p_refs, o_refs = refs[0], refs[1:1 + 3 * ns], refs[1 + 3 * ns:]
        tot = g_ref[0]
        for k in range(1, N_DEV):
            tot = tot + g_ref[k]
        for i, name in enumerate(_SMALL):
            row, lane0, lanes = _SMALL_SLOTS[name]
            g = tot[row:row + 1, lane0:lane0 + lanes]
            w_, m_, v_ = (p_refs[3 * i + j][...] for j in range(3))
            delta, m2, v2 = _adamw_math(w_, g, m_, v_)
            for j, val in enumerate((g, delta, m2, v2)):
                o_refs[4 * i + j][...] = val
        o_refs[4 * ns][...] = tot[_LOSS_ROW:_LOSS_ROW + 1, 0:LANES]
        o_refs[4 * ns + 1][...] = tot[_CONV_ROW0:_CONV_ROW0 + _CONV_ROWS, :]

    out_shape = [jax.ShapeDtypeStruct(w.shape, F32) for (w, _, _) in params for _ in range(4)]
    out_shape += [jax.ShapeDtypeStruct((1, LANES), F32), jax.ShapeDtypeStruct((_CONV_ROWS, FLAT_COLS), F32)]
    flat = [a for wmv in params for a in wmv]
    return pl.pallas_call(body, out_shape=out_shape, name="adamw_small")(gathered, *flat)


def kernel(x, p, positions, w_in, conv_w, dn_a_log, dn_dt_bias, dn_norm_w, q_norm_w, w_uq, kv_norm_w, w_uk, w_uv, w_br_dn, w_br_mla, w_o, ln1_g, ln1_b, w_ffn_in, w_ffn_out, w_ple, w_ple_gate, ln2_g, ln2_b, loss_target, m_w_in, m_conv_w, m_dn_a_log, m_dn_dt_bias, m_dn_norm_w, m_q_norm_w, m_w_uq, m_kv_norm_w, m_w_uk, m_w_uv, m_w_br_dn, m_w_br_mla, m_w_o, m_ln1_g, m_ln1_b, m_w_ffn_in, m_w_ffn_out, m_w_ple, m_w_ple_gate, m_ln2_g, m_ln2_b, v_w_in, v_conv_w, v_dn_a_log, v_dn_dt_bias, v_dn_norm_w, v_q_norm_w, v_w_uq, v_kv_norm_w, v_w_uk, v_w_uv, v_w_br_dn, v_w_br_mla, v_w_o, v_ln1_g, v_ln1_b, v_w_ffn_in, v_w_ffn_out, v_w_ple, v_w_ple_gate, v_ln2_g, v_ln2_b):
    args = dict(locals())
    wts = {n: args[n] for n in _ORDER}
    mom1 = {n: args["m_" + n] for n in _ORDER}
    mom2 = {n: args["v_" + n] for n in _ORDER}
    big_names = [n for n, _ in _BIG]
    shard_shapes = {n: wts[n].shape[1:] for n in big_names}
    c_idx = lax.axis_index("c")
    q_idx = 2 * lax.axis_index("x") + lax.axis_index("y")
    parity, chip = c_idx.reshape(1).astype(jnp.int32), q_idx.reshape(1).astype(jnp.int32)

    stored = {n: _to_stored(n, wts[n][0]).astype(BF16) for n in big_names}
    first = _all_gather([stored["w_in"], conv_w[0]], "ag_first")
    group_names = {grp: [n for n, _ in pairs] for grp, pairs in _GROUP_GRADS.items()}
    carry, gathers = first[0], {}
    for grp in ("mix", "ffn"):
        gathers[grp] = _spread_start([stored[n] for n in group_names[grp]], carry, False, "ag_start_" + grp)
        carry = gathers[grp][3]
    conv_full = jnp.moveaxis(first[1], 0, 1).reshape(conv_w.shape[1], -1)
    small_w = {n: wts[n].astype(F32) for n in _SMALL}
    w = _first_weights(carry.reshape(-1, D_MODEL), conv_full, small_w)

    def late_weights(grp, after):
        got = _spread_wait(gathers[grp], after, False, "ag_wait_" + grp)
        return _late_weights(grp, {n: t.reshape(-1, t.shape[-1]) for n, t in zip(group_names[grp], got)})

    started = {}

    def emit(group, g, carry):
        if group == "w_in":
            src = _last_grads(g)[0].reshape((N_DEV,) + _stored_shape("w_in", shard_shapes["w_in"]))
            from_sibling = _exchange_sibling([src], "rs_sibling")[0]
            own, own_bf = _chip_sum(src, from_sibling, parity, "rs_sum_w_in")
            started["w_in"] = (own, _chips_start([own_bf], carry, "rs_chips_start"))
            return started["w_in"][1][3]
        grads = _group_grads(group, g)
        srcs = [grads[n].reshape((N_DEV,) + _stored_shape(n, shard_shapes[n])) for n in grads]
        started[group] = (list(grads), _spread_start(srcs, carry, True, "rs_start_" + group))
        return started[group][1][3]

    s_dim = x.shape[1]
    loss, dx, g = _local_step(x[0], p[0, 0], positions.reshape(s_dim, 1).astype(F32), loss_target[0], w,
                              late_weights, emit)
    small_g = _last_grads(g)[1]
    own, chips_started = started.pop("w_in")
    from_chips = _chips_wait(chips_started, dx, "rs_chips_wait")[0]

    out_g, out_d, out_m, out_v = {}, {}, {}, {}

    def update(n, grad, shp):
        flat2 = (shp[0], int(np.prod(shp[1:])))
        d, m2, v2 = _adamw(wts[n][0].reshape(flat2), mom1[n][0].reshape(flat2), mom2[n][0].reshape(flat2),
                           grad.reshape(flat2), "adamw_" + n)
        out_g[n], out_d[n], out_m[n], out_v[n] = grad, d.reshape(shp), m2.reshape(shp), v2.reshape(shp)

    total = _sum_parts(own, from_chips, chip, "rs_total_w_in")
    update("w_in", _from_stored("w_in", total, shard_shapes["w_in"]), shard_shapes["w_in"])
    for group, (names, st) in started.items():
        for n, landing in zip(names, _spread_wait(st, dx, True, "rs_wait_" + group)):
            shp = shard_shapes[n]
            if _BIG_AXIS[n] == 0:
                flat2 = _stored_shape(n, shp)
                res = _adamw_sum8(wts[n][0].reshape(flat2), mom1[n][0].reshape(flat2), mom2[n][0].reshape(flat2),
                                  landing, "adamw_" + n)
                out_g[n], out_d[n], out_m[n], out_v[n] = (t.reshape(shp) for t in res)
            else:
                update(n, _from_stored(n, _sum8(landing, "rs_total_" + n), shp), shp)

    g_small = _all_gather([_pack_small_grads(small_g, loss)], "ag_small")[0]
    res = _adamw_small(g_small, [(wts[n], mom1[n], mom2[n]) for n in _SMALL])
    for i, n in enumerate(_SMALL):
        out_g[n], out_d[n], out_m[n], out_v[n] = res[4 * i:4 * i + 4]
    loss_out = res[4 * len(_SMALL)][0, 0]
    conv_shape = conv_w.shape[1:]
    conv_g = lax.dynamic_slice(res[-1].reshape(conv_shape[0], -1), (0, (2 * q_idx + c_idx) * conv_shape[1]),
                               conv_shape)
    update("conv_w", conv_g, conv_shape)

    expand = lambda d, n: d[n] if n in _SMALL else d[n][None]
    return (loss_out, dx[None], *[expand(out_g, n) for n in _ORDER], *[expand(out_d, n) for n in _ORDER],
            *[expand(out_m, n) for n in _ORDER], *[expand(out_v, n) for n in _ORDER])
```

```python
import functools

import numpy as np
import jax
import jax.numpy as jnp
from jax import lax
from jax.experimental import pallas as pl
from jax.experimental.pallas import tpu as pltpu

F32 = jnp.float32
BF16 = jnp.bfloat16

D_MODEL = 1024
N_HEADS = 8
HEAD = 128
CHUNK = 64
GROUP = 256
ROPE = 64
Q_LORA = 384
KV_LORA = 256
FFN_HIDDEN = 2816
PLE_DIM = 256
ROPE_BASE = 10000.0
ALPHA = 2.0 ** 0.25
SCALE = float((HEAD + ROPE) ** -0.5)
NEG_BIG = -1e30
EPS_RMS = 1e-6
EPS_LN = 1e-5

ADAM_LR = 0.001
ADAM_B1 = 0.9
ADAM_B2 = 0.999
ADAM_EPS = 1e-08
ADAM_WD = 0.01
ADAM_STEP = 10

N_DEV = 8
LANES = 128
FLAT_COLS = 1024

WB_CQ, WB_CKV, WB_KR, WB_BA, WB_COLS = 0, 512, 768, 896, 1024

HIGHEST = lax.Precision.HIGHEST

NN = (((1,), (0,)), ((), ()))
TN = (((0,), (0,)), ((), ()))
NT = (((1,), (1,)), ((), ()))


def _dot(a, b, dims=NN):
    return lax.dot_general(a.astype(BF16), b.astype(BF16), dims, preferred_element_type=F32)


def _dot32(a, b, dims=NN):
    return lax.dot_general(a, b, dims, precision=HIGHEST, preferred_element_type=F32)


def _sig(x):
    return 1.0 / (1.0 + jnp.exp(-x))


MM_TILE = 1536


def _pick_wide(n):
    if n <= MM_TILE:
        return n
    return max(t for t in range(LANES, MM_TILE + 1, LANES) if n % t == 0)


def _split_bf16(a):
    hi = a.astype(BF16)
    return hi, (a - hi.astype(F32)).astype(BF16)


def _dot3(a, b, dims=NN):
    ah, al = a if isinstance(a, tuple) else _split_bf16(a)
    bh, bl = b if isinstance(b, tuple) else _split_bf16(b)
    d = lambda p, q: lax.dot_general(p, q, dims, preferred_element_type=F32)
    return d(ah, bh) + (d(ah, bl) + d(al, bh))


def _mm(a, b, *, ta=False, tb=False, add=(), out_dtype=F32, name):
    if ta:
        k_dim, m_dim = a.shape
    else:
        m_dim, k_dim = a.shape
    if tb:
        n_dim, k2 = b.shape
    else:
        k2, n_dim = b.shape
    assert k_dim == k2, (a.shape, b.shape, ta, tb)
    tm = _pick_wide(m_dim)
    tn = _pick_wide(n_dim)
    tk = _pick_wide(k_dim)
    nk = k_dim // tk
    n_add = len(add)
    dims = TN if ta else (NT if tb else NN)
    assert not (ta and tb)

    def body(a_ref, b_ref, *rest):
        add_refs = rest[:n_add]
        o_ref = rest[n_add]
        acc = rest[n_add + 1]
        k = pl.program_id(2)

        @pl.when(k == 0)
        def _():
            acc[...] = jnp.zeros_like(acc)

        acc[...] += _dot(a_ref[...], b_ref[...], dims)

        @pl.when(k == nk - 1)
        def _():
            r = acc[...]
            for ar in add_refs:
                r = r + ar[...].astype(F32)
            o_ref[...] = r.astype(o_ref.dtype)

    a_spec = pl.BlockSpec((tk, tm), lambda i, j, k: (k, i)) if ta else pl.BlockSpec((tm, tk), lambda i, j, k: (i, k))
    b_spec = pl.BlockSpec((tn, tk), lambda i, j, k: (j, k)) if tb else pl.BlockSpec((tk, tn), lambda i, j, k: (k, j))
    o_spec = pl.BlockSpec((tm, tn), lambda i, j, k: (i, j))
    return pl.pallas_call(
        body,
        out_shape=jax.ShapeDtypeStruct((m_dim, n_dim), out_dtype),
        grid=(m_dim // tm, n_dim // tn, nk),
        in_specs=[a_spec, b_spec] + [o_spec] * n_add,
        out_specs=o_spec,
        scratch_shapes=[pltpu.VMEM((tm, tn), F32)],
        compiler_params=pltpu.CompilerParams(dimension_semantics=("parallel", "parallel", "arbitrary")),
        name=name,
    )(a, b, *add)


def _rowwise(fn, rows, consts, outs, accs=(), *, tm=256, name):
    rows = [r if isinstance(r, tuple) else (r, 0, r.shape[1]) for r in rows]
    s_dim = rows[0][0].shape[0]
    tm = min(tm, s_dim)
    assert s_dim % tm == 0 and all(arr.shape[0] == s_dim for arr, _, _ in rows)
    specs = [pl.BlockSpec((tm, width), functools.partial(lambda i, cb: (i, cb), cb=cb)) for _, cb, width in rows]
    args = [arr for arr, _, _ in rows]
    for c in consts:
        specs.append(pl.BlockSpec(c.shape, lambda i: (0, 0)))
        args.append(c)
    nr, nc, no = len(rows), len(consts), len(outs)
    flipped = [len(o) == 3 for o in outs]
    out_shape = [jax.ShapeDtypeStruct((o[0], s_dim) if t else (s_dim, o[0]), o[1]) for o, t in zip(outs, flipped)]
    out_specs = [pl.BlockSpec((o[0], tm), lambda i: (0, i)) if t else pl.BlockSpec((tm, o[0]), lambda i: (i, 0))
                 for o, t in zip(outs, flipped)]
    out_shape += [jax.ShapeDtypeStruct(sh, F32) for sh in accs]
    out_specs += [pl.BlockSpec(sh, lambda i: (0, 0)) for sh in accs]

    def body(*refs):
        r = [x[...].astype(F32) if x.dtype == BF16 else x[...] for x in refs[:nr]]
        c = [x[...] for x in refs[nr:nr + nc]]
        o_refs = refs[nr + nc:nr + nc + no]
        a_refs = refs[nr + nc + no:]
        o_vals, a_vals = fn(r, c)
        for ref, v, t in zip(o_refs, o_vals, flipped, strict=True):
            ref[...] = (jnp.transpose(v.astype(F32)) if t else v).astype(ref.dtype)
        if a_refs:
            @pl.when(pl.program_id(0) == 0)
            def _():
                for ref in a_refs:
                    ref[...] = jnp.zeros_like(ref)

            for ref, v in zip(a_refs, a_vals, strict=True):
                ref[...] += v

    res = pl.pallas_call(
        body,
        out_shape=out_shape,
        grid=(s_dim // tm,),
        in_specs=specs,
        out_specs=out_specs,
        compiler_params=pltpu.CompilerParams(dimension_semantics=("arbitrary" if accs else "parallel",)),
        name=name,
    )(*args)
    return res


def _colsum(v):
    return jnp.sum(v, axis=0, keepdims=True)


def _rowsum(v):
    return jnp.sum(v, axis=1, keepdims=True)


def _rowmean(v):
    return jnp.mean(v, axis=1, keepdims=True)


def _silu_grad(x):
    s = _sig(x)
    return s * (1.0 + x * (1.0 - s))


def _conv_taps(x, w, width=4):
    row = lax.broadcasted_iota(jnp.int32, x.shape, 0)
    c = x * w[width - 1:width, :]
    for s in range(1, width):
        c = c + jnp.where(row >= s, pltpu.roll(x, s, 0), 0.0) * w[width - 1 - s:width - s, :]
    return c


def _conv_fwd(proj_a, conv_w):
    s_dim = proj_a.shape[0]
    n_blk = 3 * N_HEADS

    def body(x_ref, w_ref, o_ref):
        j = pl.program_id(0)
        c = _conv_taps(x_ref[...], w_ref[...])
        y = c * _sig(c)
        r = lax.rsqrt(_rowsum(y * y) + EPS_RMS)
        fac = jnp.where(j < N_HEADS, r * (HEAD ** -0.5), jnp.where(j < 2 * N_HEADS, r, 1.0))
        o_ref[...] = y * fac

    return pl.pallas_call(
        body,
        out_shape=jax.ShapeDtypeStruct((s_dim, n_blk * HEAD), F32),
        grid=(n_blk,),
        in_specs=[pl.BlockSpec((s_dim, HEAD), lambda j: (0, j)), pl.BlockSpec((4, HEAD), lambda j: (0, j))],
        out_specs=pl.BlockSpec((s_dim, HEAD), lambda j: (0, j)),
        compiler_params=pltpu.CompilerParams(dimension_semantics=("parallel",)),
        name="conv_fwd",
    )(proj_a, conv_w)


def _conv_bwd(proj_a, conv_w, dq, dk, dv):
    s_dim = proj_a.shape[0]
    n_blk = 3 * N_HEADS

    def body(x_ref, w_ref, dq_ref, dk_ref, dv_ref, dx_ref, dw_ref):
        j = pl.program_id(0)
        x = x_ref[...]
        w = w_ref[...]
        do = jnp.where(j < N_HEADS, dq_ref[...], jnp.where(j < 2 * N_HEADS, dk_ref[...], dv_ref[...]))
        c = _conv_taps(x, w)
        sg = _sig(c)
        y = c * sg
        r = lax.rsqrt(_rowsum(y * y) + EPS_RMS)
        sc = jnp.where(j < N_HEADS, HEAD ** -0.5, 1.0)
        dy_n = sc * (r * do - y * (r * r * r) * _rowsum(do * y))
        dy = jnp.where(j < 2 * N_HEADS, dy_n, do)
        dc = dy * (sg * (1.0 + c * (1.0 - sg)))
        row = lax.broadcasted_iota(jnp.int32, x.shape, 0)
        dx = dc * w[3:4, :]
        dw_ref[3:4, :] = _colsum(dc * x)
        for s in range(1, 4):
            dx = dx + jnp.where(row < s_dim - s, pltpu.roll(dc, s_dim - s, 0), 0.0) * w[3 - s:4 - s, :]
            xs = jnp.where(row >= s, pltpu.roll(x, s, 0), 0.0)
            dw_ref[3 - s:4 - s, :] = _colsum(dc * xs)
        dx_ref[...] = dx.astype(dx_ref.dtype)

    hd = N_HEADS - 1
    return pl.pallas_call(
        body,
        out_shape=[jax.ShapeDtypeStruct((s_dim, n_blk * HEAD), BF16), jax.ShapeDtypeStruct((4, n_blk * HEAD), F32)],
        grid=(n_blk,),
        in_specs=[
            pl.BlockSpec((s_dim, HEAD), lambda j: (0, j)),
            pl.BlockSpec((4, HEAD), lambda j: (0, j)),
            pl.BlockSpec((s_dim, HEAD), lambda j: (0, jnp.minimum(j, hd))),
            pl.BlockSpec((s_dim, HEAD), lambda j: (0, jnp.clip(j - N_HEADS, 0, hd))),
            pl.BlockSpec((s_dim, HEAD), lambda j: (0, jnp.clip(j - 2 * N_HEADS, 0, hd))),
        ],
        out_specs=[pl.BlockSpec((s_dim, HEAD), lambda j: (0, j)), pl.BlockSpec((4, HEAD), lambda j: (0, j))],
        compiler_params=pltpu.CompilerParams(dimension_semantics=("parallel",)),
        name="conv_bwd",
    )(proj_a, conv_w, dq, dk, dv)


def _chunk_tri(n):
    r = np.arange(n)
    m = ((r[:, None] // CHUNK) == (r[None, :] // CHUNK)) & (r[:, None] >= r[None, :])
    m = m.astype(np.float32)
    return jnp.asarray(m), jnp.asarray(m.T)


def _softplus(z):
    return jnp.maximum(z, 0.0) + jnp.log(1.0 + jnp.exp(-jnp.abs(z)))


def _gates_fwd(proj_b, alog, dtb):
    tm = min(GROUP, proj_b.shape[0])
    tri, _ = _chunk_tri(tm)

    def fn(r, c):
        b = r[0]
        a = pltpu.roll(b, LANES - N_HEADS, 1)
        alog_, dtb_, tri_ = c
        g = -jnp.exp(alog_) * _softplus(a + dtb_)
        return [_sig(b), _dot32(tri_, g)], []

    return _rowwise(fn, [(proj_b, WB_BA // LANES, LANES)], [alog, dtb, tri],
                    [(LANES, F32), (LANES, F32)], tm=tm, name="gates_fwd")


def _gates_bwd(proj_b, alog, dtb, gc, d_beta, d_gc, d_egl_rows):
    tm = min(GROUP, proj_b.shape[0])
    _, tri_t = _chunk_tri(tm)

    def fn(r, c):
        b, gc_, d_beta_, d_gc_, d_egl_ = r
        a = pltpu.roll(b, LANES - N_HEADS, 1)
        alog_, dtb_, tri_t_ = c
        z = a + dtb_
        ea = jnp.exp(alog_)
        g = -ea * _softplus(z)
        dg = _dot32(tri_t_, d_gc_ + d_egl_ * jnp.exp(gc_))
        d_a = dg * (-ea) * _sig(z)
        beta = _sig(b)
        d_ba = d_beta_ * beta * (1.0 - beta) + pltpu.roll(d_a, N_HEADS, 1)
        return [d_ba], [_colsum(dg * g), _colsum(d_a)]

    return _rowwise(fn, [(proj_b, WB_BA // LANES, LANES), gc, d_beta, d_gc, d_egl_rows],
                    [alog, dtb, tri_t], [(LANES, BF16)], accs=[(1, LANES), (1, LANES)], tm=tm,
                    name="gates_bwd")


def _group_masks(n):
    r = lax.broadcasted_iota(jnp.int32, (n, n), 0)
    c = lax.broadcasted_iota(jnp.int32, (n, n), 1)
    same = (r // CHUNK) == (c // CHUNK)
    below, s = [], 2
    while s < CHUNK:
        below.append(jnp.logical_and((r // (2 * s)) == (c // (2 * s)),
                                     jnp.logical_and((r // s) % 2 == 1, (c // s) % 2 == 0)))
        s *= 2
    return dict(same=same, tril=jnp.logical_and(same, r >= c), strict=jnp.logical_and(same, r > c),
                last=c == (r // CHUNK) * CHUNK + (CHUNK - 1), eye=r == c, pair=(r // 2) == (c // 2), below=below)


def _inv_unit_lower(l_mats, mk):
    eye_f = mk["eye"].astype(F32)
    ts = [eye_f - jnp.where(mk["pair"], l_mat, 0.0) for l_mat in l_mats]
    for below in mk["below"]:
        halves = [_split_bf16(t) for t in ts]
        mids = [_dot3(h, jnp.where(below, l_mat, 0.0)) for h, l_mat in zip(halves, l_mats)]
        ts = [t - _dot3(m, h) for t, m, h in zip(ts, mids, halves)]
    return ts


def _unfold_blocks(folded, mask):
    n = folded.shape[0]
    return jnp.where(mask, jnp.concatenate([folded] * (n // CHUNK), axis=1), 0.0)


def _head_cols(beta, gc, gc_t, h):
    lane = lax.broadcasted_iota(jnp.int32, beta.shape, 1)
    sub = lax.broadcasted_iota(jnp.int32, gc_t.shape, 0)
    bcol = _rowsum(jnp.where(lane == h, beta, 0.0))
    gcol = _rowsum(jnp.where(lane == h, gc, 0.0))
    grow = _colsum(jnp.where(sub == h, gc_t, 0.0))
    return bcol, gcol, grow


def _prep_common(q, k, bcol, gcol, grow, mk, t_folded=None):
    n = q.shape[0]
    tril = mk["tril"]
    decay = jnp.where(tril, jnp.exp(jnp.where(tril, gcol - grow, 0.0)), 0.0)
    glast = _rowsum(jnp.where(mk["last"], jnp.broadcast_to(grow, (n, n)), 0.0))
    e = jnp.exp(gcol)
    ekt = jnp.exp(glast - gcol)
    kb = k * bcol
    kk = _dot(kb, k, NT)
    qk = _dot(q, k, NT)
    p = dict(decay=decay, e=e, ekt=ekt, kb=kb, kk=kk, qk=qk)
    if t_folded is not None:
        p["t"] = _unfold_blocks(t_folded, mk["same"])
    return p


GROUPS_PER_STEP = 4
SCAN_CHUNKS_PER_STEP = 4


def _fold_blocks(m):
    n = m.shape[0]
    out = m[:, 0:CHUNK]
    for b in range(1, n // CHUNK):
        out = out + m[:, b * CHUNK:(b + 1) * CHUNK]
    return out


def _gdr_prep_fwd(qkvn, beta, gc, gc_t):
    s_dim = qkvn.shape[0]
    tg = min(GROUP, s_dim)
    n_sub = min(GROUPS_PER_STEP, s_dim // tg)
    tb = tg * n_sub

    def body(q_ref, k_ref, v_ref, b_ref, g_ref, gt_ref, u_ref, w_ref, qd_ref, kt_ref, a_ref, t_ref):
        h = pl.program_id(0)
        mk = _group_masks(tg)
        parts = []
        for s in range(n_sub):
            rows = slice(s * tg, (s + 1) * tg)
            q, k, v = q_ref[rows, :], k_ref[rows, :], v_ref[rows, :]
            bcol, gcol, grow = _head_cols(b_ref[rows, :], g_ref[rows, :], gt_ref[:, rows], h)
            p = _prep_common(q, k, bcol, gcol, grow, mk)
            qd_ref[rows, :] = q * p["e"]
            kt_ref[rows, :] = k * p["ekt"]
            a_ref[rows, :] = _fold_blocks(jnp.where(mk["tril"], p["qk"] * p["decay"], 0.0))
            parts.append((rows, v * bcol, p["kb"] * p["e"], jnp.where(mk["strict"], p["kk"] * p["decay"], 0.0)))
        t_mats = _inv_unit_lower([part[3] for part in parts], mk)
        for (rows, vb, kbe, _), t_mat in zip(parts, t_mats):
            u_ref[rows, :] = _dot(t_mat, vb)
            w_ref[rows, :] = _dot(t_mat, kbe)
            t_ref[rows, :] = _fold_blocks(t_mat)

    row = lambda off: pl.BlockSpec((tb, HEAD), functools.partial(lambda h, m, off: (m, h + off), off=off))
    full = pl.BlockSpec((tb, LANES), lambda h, m: (m, 0))
    o_spec = pl.BlockSpec((tb, HEAD), lambda h, m: (m, h))
    a_spec = pl.BlockSpec((None, tb, CHUNK), lambda h, m: (h, m, 0))
    wide = jax.ShapeDtypeStruct((s_dim, N_HEADS * HEAD), F32)
    folded = jax.ShapeDtypeStruct((N_HEADS, s_dim, CHUNK), F32)
    return pl.pallas_call(
        body,
        out_shape=[wide, wide, wide, wide, folded, folded],
        grid=(N_HEADS, s_dim // tb),
        in_specs=[row(0), row(N_HEADS), row(2 * N_HEADS), full, full, pl.BlockSpec((8, tb), lambda h, m: (0, m))],
        out_specs=[o_spec, o_spec, o_spec, o_spec, a_spec, a_spec],
        compiler_params=pltpu.CompilerParams(dimension_semantics=("parallel", "parallel")),
        name="gdr_prep_fwd",
    )(qkvn, qkvn, qkvn, beta, gc, gc_t)


def _gdr_prep_bwd(qkvn, beta, gc, gc_t, t_fold, u, w, du, dw, dqd, dkt, d_a):
    s_dim = qkvn.shape[0]
    tg = min(GROUP, s_dim)
    n_sub = min(GROUPS_PER_STEP, s_dim // tg)
    tb = tg * n_sub

    def body(q_ref, k_ref, v_ref, b_ref, g_ref, gt_ref, t_ref, u_ref, w_ref, du_ref, dw_ref, dqd_ref, dkt_ref,
             da_ref, dq_ref, dk_ref, dv_ref, db_ref, dg_ref):
        h = pl.program_id(1)

        @pl.when(h == 0)
        def _():
            db_ref[...] = jnp.zeros_like(db_ref)
            dg_ref[...] = jnp.zeros_like(dg_ref)

        mk = _group_masks(tg)
        lane = lax.broadcasted_iota(jnp.int32, (tg, LANES), 1)
        for s in range(n_sub):
            rows = slice(s * tg, (s + 1) * tg)
            q, k, v = q_ref[rows, :], k_ref[rows, :], v_ref[rows, :]
            bcol, gcol, grow = _head_cols(b_ref[rows, :], g_ref[rows, :], gt_ref[:, rows], h)
            p = _prep_common(q, k, bcol, gcol, grow, mk, t_ref[rows, :])
            t_mat, decay, e, ekt, kb = p["t"], p["decay"], p["e"], p["ekt"], p["kb"]
            du_, dw_, dqd_, dkt_ = du_ref[rows, :], dw_ref[rows, :], dqd_ref[rows, :], dkt_ref[rows, :]
            dvb = _dot(t_mat, du_, TN)
            dkbe = _dot(t_mat, dw_, TN)
            d_l = -(_dot(dvb, u_ref[rows, :], NT) + _dot(dkbe, w_ref[rows, :], NT))
            m1 = jnp.where(mk["strict"], d_l, 0.0)
            m2 = _unfold_blocks(da_ref[rows, :], mk["tril"])
            d_kk = m1 * decay
            d_qk = m2 * decay
            d_decay = m1 * p["kk"] + m2 * p["qk"]
            dkb = _dot(d_kk, k) + dkbe * e
            dk = _dot(d_kk, kb, TN) + _dot(d_qk, q, TN) + dkt_ * ekt + dkb * bcol
            dq = _dot(d_qk, k) + dqd_ * e
            d_beta = _rowsum(dkb * k) + _rowsum(dvb * v)
            d_e = _rowsum(dkbe * kb) + _rowsum(dqd_ * q)
            d_ekt = _rowsum(dkt_ * k) * ekt
            d_diff = d_decay * decay
            d_grow = -_colsum(d_diff) + _colsum(jnp.where(mk["last"], jnp.broadcast_to(d_ekt, (tg, tg)), 0.0))
            d_gcol = d_e * e - d_ekt + _rowsum(d_diff)
            d_gcol = d_gcol + _rowsum(jnp.where(mk["eye"], jnp.broadcast_to(d_grow, (tg, tg)), 0.0))
            dq_ref[rows, :] = dq
            dk_ref[rows, :] = dk
            dv_ref[rows, :] = dvb * bcol
            db_ref[rows, :] = jnp.where(lane == h, d_beta, db_ref[rows, :])
            dg_ref[rows, :] = jnp.where(lane == h, d_gcol, dg_ref[rows, :])

    row = lambda off: pl.BlockSpec((tb, HEAD), functools.partial(lambda m, h, off: (m, h + off), off=off))
    full = pl.BlockSpec((tb, LANES), lambda m, h: (m, 0))
    o_spec = pl.BlockSpec((tb, HEAD), lambda m, h: (m, h))
    a_spec = pl.BlockSpec((None, tb, CHUNK), lambda m, h: (h, m, 0))
    wide = jax.ShapeDtypeStruct((s_dim, N_HEADS * HEAD), F32)
    lanes = jax.ShapeDtypeStruct((s_dim, LANES), F32)
    return pl.pallas_call(
        body,
        out_shape=[wide, wide, wide, lanes, lanes],
        grid=(s_dim // tb, N_HEADS),
        in_specs=[row(0), row(N_HEADS), row(2 * N_HEADS), full, full, pl.BlockSpec((8, tb), lambda m, h: (0, m)),
                  a_spec, o_spec, o_spec, o_spec, o_spec, o_spec, o_spec, a_spec],
        out_specs=[o_spec, o_spec, o_spec, full, full],
        compiler_params=pltpu.CompilerParams(dimension_semantics=("parallel", "arbitrary")),
        name="gdr_prep_bwd",
    )(qkvn, qkvn, qkvn, beta, gc, gc_t, t_fold, u, w, du, dw, dqd, dkt, d_a)


def _gdr_scan_fwd(u, w, qd, kt, a_mat, gc):
    s_dim = u.shape[0]
    n_chunks = s_dim // CHUNK
    per = min(SCAN_CHUNKS_PER_STEP, n_chunks)
    tb = per * CHUNK

    def body(u_ref, w_ref, qd_ref, kt_ref, a_ref, g_ref, o_ref, st_ref, state):
        @pl.when(pl.program_id(0) == 0)
        def _():
            state[...] = jnp.zeros_like(state)

        heads = range(N_HEADS)
        cols = [slice(h * HEAD, (h + 1) * HEAD) for h in heads]
        for i in range(per):
            rows = slice(i * CHUNK, (i + 1) * CHUNK)
            egl = jnp.exp(g_ref[(i + 1) * CHUNK - 1:(i + 1) * CHUNK, :])
            s_b = [state[h].astype(BF16) for h in heads]
            for h in heads:
                st_ref[i, h] = state[h]
            ws = [_dot(w_ref[rows, cs], s) for cs, s in zip(cols, s_b)]
            qs = [_dot(qd_ref[rows, cs], s) for cs, s in zip(cols, s_b)]
            vns = [(u_ref[rows, cs] - ws_h).astype(BF16) for cs, ws_h in zip(cols, ws)]
            avs = [_dot(a_ref[h, rows, :], vn) for h, vn in zip(heads, vns)]
            kvs = [_dot(kt_ref[rows, cs], vn, TN) for cs, vn in zip(cols, vns)]
            for h, cs in zip(heads, cols):
                o_ref[rows, cs] = qs[h] + avs[h]
                state[h] = state[h] * egl[:, h:h + 1] + kvs[h]

    wide = pl.BlockSpec((tb, N_HEADS * HEAD), lambda n: (n, 0))
    return pl.pallas_call(
        body,
        out_shape=[jax.ShapeDtypeStruct((s_dim, N_HEADS * HEAD), F32),
                   jax.ShapeDtypeStruct((n_chunks, N_HEADS, HEAD, HEAD), F32)],
        grid=(n_chunks // per,),
        in_specs=[wide, wide, wide, wide, pl.BlockSpec((N_HEADS, tb, CHUNK), lambda n: (0, n, 0)),
                  pl.BlockSpec((tb, LANES), lambda n: (n, 0))],
        out_specs=[wide, pl.BlockSpec((per, N_HEADS, HEAD, HEAD), lambda n: (n, 0, 0, 0))],
        scratch_shapes=[pltpu.VMEM((N_HEADS, HEAD, HEAD), F32)],
        compiler_params=pltpu.CompilerParams(dimension_semantics=("arbitrary",)),
        name="gdr_scan_fwd",
    )(u, w, qd, kt, a_mat, gc)


def _gdr_scan_bwd(u, w, qd, kt, a_mat, gc, states, d_o):
    s_dim = u.shape[0]
    n_chunks = s_dim // CHUNK
    per = min(SCAN_CHUNKS_PER_STEP, n_chunks)
    tb = per * CHUNK
    last = n_chunks // per - 1

    def body(u_ref, w_ref, qd_ref, kt_ref, a_ref, g_ref, st_ref, do_ref,
             du_ref, dw_ref, dqd_ref, dkt_ref, da_ref, de_ref, d_state):
        @pl.when(pl.program_id(0) == 0)
        def _():
            d_state[...] = jnp.zeros_like(d_state)

        heads = range(N_HEADS)
        cols = [slice(h * HEAD, (h + 1) * HEAD) for h in heads]
        for i in reversed(range(per)):
            rows = slice(i * CHUNK, (i + 1) * CHUNK)
            egl = jnp.exp(g_ref[(i + 1) * CHUNK - 1:(i + 1) * CHUNK, :])
            s_b = [st_ref[i, h].astype(BF16) for h in heads]
            ds_b = [d_state[h].astype(BF16) for h in heads]
            dos = [do_ref[rows, cs].astype(BF16) for cs in cols]
            w_b = [w_ref[rows, cs].astype(BF16) for cs in cols]
            ws = [_dot(w_h, s) for w_h, s in zip(w_b, s_b)]
            ados = [_dot(a_ref[h, rows, :], do, TN) for h, do in zip(heads, dos)]
            kds = [_dot(kt_ref[rows, cs], ds) for cs, ds in zip(cols, ds_b)]
            dqds = [_dot(do, s, NT) for do, s in zip(dos, s_b)]
            qdos = [_dot(qd_ref[rows, cs], do, TN) for cs, do in zip(cols, dos)]
            vns = [(u_ref[rows, cs] - ws_h).astype(BF16) for cs, ws_h in zip(cols, ws)]
            dvns = [a + k_ for a, k_ in zip(ados, kds)]
            dvn_b = [d.astype(BF16) for d in dvns]
            das = [_dot(do, vn, NT) for do, vn in zip(dos, vns)]
            dkts = [_dot(vn, ds, NT) for vn, ds in zip(vns, ds_b)]
            dws = [_dot(d, s, NT) for d, s in zip(dvn_b, s_b)]
            wds = [_dot(w_h, d, TN) for w_h, d in zip(w_b, dvn_b)]
            for h, cs in zip(heads, cols):
                ds_n = d_state[h]
                de = jnp.sum(_rowsum(ds_n * st_ref[i, h]), axis=0, keepdims=True)
                de_ref[i, h:h + 1, :] = jnp.broadcast_to(de, (1, LANES))
                dqd_ref[rows, cs] = dqds[h]
                da_ref[h, rows, :] = das[h]
                dkt_ref[rows, cs] = dkts[h]
                du_ref[rows, cs] = dvns[h]
                dw_ref[rows, cs] = -dws[h]
                d_state[h] = ds_n * egl[:, h:h + 1] + qdos[h] - wds[h]

    wide = pl.BlockSpec((tb, N_HEADS * HEAD), lambda n: (last - n, 0))
    a_spec = pl.BlockSpec((N_HEADS, tb, CHUNK), lambda n: (0, last - n, 0))
    wide_shape = jax.ShapeDtypeStruct((s_dim, N_HEADS * HEAD), F32)
    return pl.pallas_call(
        body,
        out_shape=[wide_shape, wide_shape, wide_shape, wide_shape,
                   jax.ShapeDtypeStruct((N_HEADS, s_dim, CHUNK), F32),
                   jax.ShapeDtypeStruct((n_chunks, N_HEADS, LANES), F32)],
        grid=(n_chunks // per,),
        in_specs=[wide, wide, wide, wide, a_spec, pl.BlockSpec((tb, LANES), lambda n: (last - n, 0)),
                  pl.BlockSpec((per, N_HEADS, HEAD, HEAD), lambda n: (last - n, 0, 0, 0)), wide],
        out_specs=[wide, wide, wide, wide, a_spec, pl.BlockSpec((per, N_HEADS, LANES), lambda n: (last - n, 0, 0))],
        scratch_shapes=[pltpu.VMEM((N_HEADS, HEAD, HEAD), F32)],
        compiler_params=pltpu.CompilerParams(dimension_semantics=("arbitrary",)),
        name="gdr_scan_bwd",
    )(u, w, qd, kt, a_mat, gc, states, d_o)


FUSED_ROWS = 512


def _gdr_out_fwd(o_dn, proj_a, dn_w, w_br):
    def fn(r, c):
        o, z = r
        w_, w_br_ = c
        outs = []
        for h in range(N_HEADS):
            cs = slice(h * HEAD, (h + 1) * HEAD)
            oh, zh = o[:, cs], z[:, cs]
            rr = lax.rsqrt(_rowmean(oh * oh) + EPS_RMS)
            outs.append(oh * rr * w_ * (zh * _sig(zh)))
        og = jnp.concatenate(outs, axis=1).astype(BF16)
        return [og, _dot(og, w_br_)], []

    return _rowwise(fn, [o_dn, (proj_a, 3, D_MODEL)], [dn_w, w_br], [(D_MODEL, BF16), (D_MODEL, BF16)],
                    tm=FUSED_ROWS, name="gdr_out_fwd")


def _gdr_out_bwd(o_dn, proj_a, d_y_dn, dn_w, w_br):
    def fn(r, c):
        o, z, dy = r
        w_, w_br_ = c
        dg = _dot(dy, w_br_, NT)
        d_o, d_z = [], []
        d_w = jnp.zeros((1, HEAD), F32)
        for h in range(N_HEADS):
            cs = slice(h * HEAD, (h + 1) * HEAD)
            oh, zh, dgh = o[:, cs], z[:, cs], dg[:, cs]
            rr = lax.rsqrt(_rowmean(oh * oh) + EPS_RMS)
            sz = zh * _sig(zh)
            d_n = dgh * sz
            d_z.append(dgh * (oh * rr * w_) * _silu_grad(zh))
            d_w = d_w + _colsum(d_n * oh * rr)
            gw = d_n * w_
            d_o.append(rr * gw - oh * (rr * rr * rr) * _rowmean(gw * oh))
        return [jnp.concatenate(d_o, axis=1), jnp.concatenate(d_z, axis=1)], [d_w]

    return _rowwise(fn, [o_dn, (proj_a, 3, D_MODEL), d_y_dn], [dn_w, w_br], [(D_MODEL, F32), (D_MODEL, BF16)],
                    accs=[(1, HEAD)], tm=FUSED_ROWS, name="gdr_out_bwd")


def _rms_fwd(x, w):
    r = lax.rsqrt(_rowmean(x * x) + EPS_RMS)
    return x * r * w


def _rms_bwd(x, w, dy):
    r = lax.rsqrt(_rowmean(x * x) + EPS_RMS)
    gw = dy * w
    return r * gw - x * (r * r * r) * _rowmean(gw * x), _colsum(dy * x * r)


def _rope_consts():
    inv = ROPE_BASE ** (-np.arange(0, ROPE, 2, dtype=np.float32) / ROPE)
    t = np.zeros((4, LANES), np.float32)
    t[0, :32] = inv
    t[0, 32:64] = inv
    t[1, :64] = 1.0
    t[2, 32:64] = 1.0
    t[3, :32] = -1.0
    return jnp.asarray(t)


def _rope_tables(pos, consts, width):
    ang = pos * consts[0:1, :]
    cosv, sinv = jnp.cos(ang), jnp.sin(ang)
    reps = width // LANES
    tile = (lambda t: jnp.concatenate([t] * reps, axis=1)) if reps > 1 else (lambda t: t)
    return tile(cosv * consts[1:2, :]), tile(sinv * consts[2:3, :]), tile(sinv * consts[3:4, :])


def _rope_apply(t, tabs):
    cos_t, sin_a, sin_b = tabs
    width = t.shape[1]
    return t * cos_t + pltpu.roll(t, 32, 1) * sin_a + pltpu.roll(t, width - 32, 1) * sin_b


def _rope_transpose(d, tabs):
    cos_t, sin_a, sin_b = tabs
    width = d.shape[1]
    return d * cos_t + pltpu.roll(d * sin_a, width - 32, 1) + pltpu.roll(d * sin_b, 32, 1)


QK_HEAD = 2 * HEAD


def _interleave_heads(a, b):
    parts = []
    for h in range(N_HEADS):
        parts.append(a[:, h * HEAD:(h + 1) * HEAD])
        parts.append(b if b.shape[1] == LANES else b[:, h * LANES:(h + 1) * LANES])
    return jnp.concatenate(parts, axis=1)


def _mla_rows(proj_b):
    return [(proj_b, WB_CQ // Q_LORA, Q_LORA), (proj_b, WB_CKV // KV_LORA, KV_LORA), (proj_b, WB_KR // LANES, LANES)]


def _mla_prep_fwd(proj_b, pos, qn_w, kvn_w, uq, uk, uv):
    def fn(r, c):
        cq, ckv, kr, pos_ = r
        qn_w_, kvn_w_, uq_, uk_, uv_, rope = c
        c_q = _rms_fwd(cq, qn_w_).astype(BF16)
        c_kv = _rms_fwd(ckv, kvn_w_).astype(BF16)
        qf = _dot(c_q, uq_)
        qr = _rope_apply(qf[:, D_MODEL:], _rope_tables(pos_, rope, D_MODEL))
        kr = _rope_apply(kr, _rope_tables(pos_, rope, LANES))
        kc = _interleave_heads(_dot(c_kv, uk_), kr)
        v = _dot(c_kv, uv_)
        return [c_q, c_kv, _interleave_heads(qf[:, :D_MODEL], qr) * SCALE, kc, v, kc, v], []

    wide2 = N_HEADS * QK_HEAD
    return _rowwise(fn, _mla_rows(proj_b) + [pos], [qn_w, kvn_w, uq, uk, uv, _rope_consts()],
                    [(Q_LORA, BF16), (KV_LORA, BF16), (wide2, BF16), (wide2, BF16), (D_MODEL, BF16),
                     (wide2, BF16, "T"), (D_MODEL, BF16, "T")], tm=FUSED_ROWS, name="mla_prep_fwd")


def _mla_prep_bwd(proj_b, pos, d_qc, d_kc, d_v, qn_w, kvn_w, uq, uk, uv):
    def fn(r, c):
        cq, ckv, _, pos_, dq, dk, dv = r
        qn_w_, kvn_w_, uq_, uk_, uv_, rope = c
        even = lambda t: jnp.concatenate([t[:, (2 * h) * LANES:(2 * h + 1) * LANES] for h in range(N_HEADS)], axis=1)
        odd = lambda t: jnp.concatenate([t[:, (2 * h + 1) * LANES:(2 * h + 2) * LANES] for h in range(N_HEADS)], axis=1)
        d_qr_raw = _rope_transpose(odd(dq), _rope_tables(pos_, rope, D_MODEL)) * SCALE
        d_qf = jnp.concatenate([even(dq) * SCALE, d_qr_raw], axis=1).astype(BF16)
        d_kn = even(dk).astype(BF16)
        dkr = dk[:, LANES:2 * LANES]
        for h in range(1, N_HEADS):
            dkr = dkr + dk[:, (2 * h + 1) * LANES:(2 * h + 2) * LANES]
        d_cq, d_qnw = _rms_bwd(cq, qn_w_, _dot(d_qf, uq_, NT))
        d_ckv, d_kvnw = _rms_bwd(ckv, kvn_w_, _dot(d_kn, uk_, NT) + _dot(dv, uv_, NT))
        return [d_qf, d_kn, d_cq, d_ckv, _rope_transpose(dkr, _rope_tables(pos_, rope, LANES))], [d_qnw, d_kvnw]

    return _rowwise(fn, _mla_rows(proj_b) + [pos, d_qc, d_kc, d_v], [qn_w, kvn_w, uq, uk, uv, _rope_consts()],
                    [(2 * D_MODEL, BF16), (D_MODEL, BF16), (Q_LORA, BF16), (KV_LORA, BF16), (LANES, BF16)],
                    accs=[(1, Q_LORA), (1, KV_LORA)], tm=FUSED_ROWS, name="mla_prep_bwd")


def _causal_mask_t(st, key0, query0):
    key = lax.broadcasted_iota(jnp.int32, st.shape, 0) + key0
    query = lax.broadcasted_iota(jnp.int32, st.shape, 1) + query0
    return jnp.where(key <= query, st, NEG_BIG)


def _attn_tiles(s_dim):
    tq = min(512, s_dim)
    n_chains = 2 if s_dim >= 2 * tq else 1
    return tq, n_chains, min(512, s_dim)


def _diagonal_chains(t, tq, n_chains, tk):
    return [(c, (t + 1) * tk - 1 > c * tq) for c in range(n_chains) if t * tk < (c + 1) * tq]


def _attn_fwd(qc, kc, vt):
    s_dim = qc.shape[0]
    tq, n_chains, tk = _attn_tiles(s_dim)
    tqs = tq * n_chains

    def body(q_ref, k_ref, vt_ref, o_ref, lse_ref, m_s, l_s, acc):
        qi = pl.program_id(1)
        m_s[...] = jnp.full_like(m_s, NEG_BIG)
        l_s[...] = jnp.zeros_like(l_s)
        acc[...] = jnp.zeros_like(acc)

        def make_step(chains):
            def step(j, carry):
                ks = pl.multiple_of(j * tk, tk)
                kb, vtb = k_ref[pl.ds(ks, tk), :], vt_ref[:, pl.ds(ks, tk)]
                cols = [slice(c * tq, (c + 1) * tq) for c, _ in chains]
                sts = [_dot(kb, q_ref[cs, :], NT) for cs in cols]
                sts = [_causal_mask_t(st, j * tk, qi * tqs + c * tq) if masked else st
                       for st, (c, masked) in zip(sts, chains)]
                m_prevs = [m_s[:, cs] for cs in cols]
                m_news = [jnp.maximum(mp, jnp.max(st, axis=0, keepdims=True)) for mp, st in zip(m_prevs, sts)]
                alphas = [jnp.exp(mp - mn) for mp, mn in zip(m_prevs, m_news)]
                pts = [jnp.exp(st - mn) for st, mn in zip(sts, m_news)]
                pvs = [_dot(vtb, pt) for pt in pts]
                for cs, mn, al, pt, pv in zip(cols, m_news, alphas, pts, pvs):
                    l_s[:, cs] = al * l_s[:, cs] + _colsum(pt)
                    m_s[:, cs] = mn
                    acc[:, cs] = acc[:, cs] * al + pv
                return carry
            return step

        below = qi * (tqs // tk)
        lax.fori_loop(0, below, make_step([(c, False) for c in range(n_chains)]), 0)
        for t in range(tqs // tk):
            make_step(_diagonal_chains(t, tq, n_chains, tk))(below + t, 0)
        l = l_s[...]
        o_ref[...] = jnp.transpose(acc[...] / l)
        lse_ref[...] = m_s[...] + jnp.log(l)

    return pl.pallas_call(
        body,
        out_shape=[jax.ShapeDtypeStruct((s_dim, N_HEADS * HEAD), F32), jax.ShapeDtypeStruct((N_HEADS, 1, s_dim), F32)],
        grid=(N_HEADS, s_dim // tqs),
        in_specs=[pl.BlockSpec((tqs, QK_HEAD), lambda h, qi: (qi, h)),
                  pl.BlockSpec((s_dim, QK_HEAD), lambda h, qi: (0, h)),
                  pl.BlockSpec((HEAD, s_dim), lambda h, qi: (h, 0))],
        out_specs=[pl.BlockSpec((tqs, HEAD), lambda h, qi: (qi, h)),
                   pl.BlockSpec((None, 1, tqs), lambda h, qi: (h, 0, qi))],
        scratch_shapes=[pltpu.VMEM((1, tqs), F32), pltpu.VMEM((1, tqs), F32), pltpu.VMEM((HEAD, tqs), F32)],
        compiler_params=pltpu.CompilerParams(dimension_semantics=("parallel", "parallel")),
        name="attn_fwd",
    )(qc, kc, vt)


def _attn_bwd(qc, kc, kct, v, o, d_o, lse):
    s_dim = qc.shape[0]
    tq, n_chains, tk = _attn_tiles(s_dim)
    tqs = tq * n_chains

    def body(q_ref, k_ref, kt_ref, v_ref, o_ref, do_ref, lse_ref, dq_ref, dk_ref, dv_ref, dqt_acc, dv_acc):
        qi = pl.program_id(1)

        @pl.when(qi == 0)
        def _():
            dk_ref[...] = jnp.zeros_like(dk_ref)
            dv_acc[...] = jnp.zeros_like(dv_acc)

        dqt_acc[...] = jnp.zeros_like(dqt_acc)
        do_f = do_ref[...]
        do_all = do_f.astype(BF16)
        q_all = q_ref[...]
        lse_row = lse_ref[...]
        delta_row = _dot3(jnp.ones((8, HEAD), F32), o_ref[...] * do_f, NT)[0:1, :]

        def make_step(chains):
            rows = slice(chains[0][0] * tq, (chains[-1][0] + 1) * tq)

            def step(j, carry):
                ks = pl.multiple_of(j * tk, tk)
                kb, vb, ktb = k_ref[pl.ds(ks, tk), :], v_ref[pl.ds(ks, tk), :], kt_ref[:, pl.ds(ks, tk)]
                cols = [slice(c * tq, (c + 1) * tq) for c, _ in chains]
                sts = [_dot(kb, q_all[cs, :], NT) for cs in cols]
                sts = [_causal_mask_t(st, j * tk, qi * tqs + c * tq) if masked else st
                       for st, (c, masked) in zip(sts, chains)]
                dpts = [_dot(vb, do_all[cs, :], NT) for cs in cols]
                pts = [jnp.exp(st - lse_row[:, cs]) for st, cs in zip(sts, cols)]
                dsts = [(pt * (dpt - delta_row[:, cs])).astype(BF16) for pt, dpt, cs in zip(pts, dpts, cols)]
                pts = [pt.astype(BF16) for pt in pts]
                dqs = [_dot(ktb, dst) for dst in dsts]
                for cs, dq in zip(cols, dqs):
                    dqt_acc[:, cs] += dq
                pt_all = jnp.concatenate(pts, axis=1) if len(chains) > 1 else pts[0]
                dst_all = jnp.concatenate(dsts, axis=1) if len(chains) > 1 else dsts[0]
                dk_ref[pl.ds(ks, tk), :] += _dot(dst_all, q_all[rows, :])
                dv_acc[pl.ds(ks, tk), :] += _dot(pt_all, do_all[rows, :])
                return carry
            return step

        below = qi * (tqs // tk)
        lax.fori_loop(0, below, make_step([(c, False) for c in range(n_chains)]), 0)
        for t in range(tqs // tk):
            make_step(_diagonal_chains(t, tq, n_chains, tk))(below + t, 0)
        dq_ref[...] = jnp.transpose(dqt_acc[...])

        @pl.when(qi == s_dim // tqs - 1)
        def _():
            dv_ref[...] = dv_acc[...].astype(dv_ref.dtype)

    q_spec = pl.BlockSpec((tqs, QK_HEAD), lambda h, qi: (qi, h))
    o_spec = pl.BlockSpec((tqs, HEAD), lambda h, qi: (qi, h))
    k_spec = pl.BlockSpec((s_dim, QK_HEAD), lambda h, qi: (0, h))
    v_spec = pl.BlockSpec((s_dim, HEAD), lambda h, qi: (0, h))
    wide2 = jax.ShapeDtypeStruct((s_dim, N_HEADS * QK_HEAD), F32)
    return pl.pallas_call(
        body,
        out_shape=[wide2, wide2, jax.ShapeDtypeStruct((s_dim, N_HEADS * HEAD), BF16)],
        grid=(N_HEADS, s_dim // tqs),
        in_specs=[q_spec, k_spec, pl.BlockSpec((QK_HEAD, s_dim), lambda h, qi: (h, 0)), v_spec, o_spec, o_spec,
                  pl.BlockSpec((None, 1, tqs), lambda h, qi: (h, 0, qi))],
        out_specs=[q_spec, k_spec, v_spec],
        scratch_shapes=[pltpu.VMEM((QK_HEAD, tqs), F32), pltpu.VMEM((s_dim, HEAD), F32)],
        compiler_params=pltpu.CompilerParams(dimension_semantics=("parallel", "arbitrary")),
        name="attn_bwd",
    )(qc, kc, kct, v, o, d_o, lse)


def _mix_proj_ln1(y_dn, y_mla, proj_g, x, w_o, g, b):
    s_dim = x.shape[0]
    tm = min(512, s_dim)

    def body(yd_ref, ym_ref, g_ref, x_ref, w_ref, lg_ref, lb_ref, mixed_ref, a1_ref, h1_ref, h1b_ref):
        gates = g_ref[...].astype(F32)
        mixed = (_sig(gates[:, :D_MODEL]) * yd_ref[...].astype(F32)
                 + _sig(gates[:, D_MODEL:]) * ym_ref[...].astype(F32)).astype(BF16)
        a1 = _dot(mixed, w_ref[...])
        xh, _ = _ln_stats(ALPHA * x_ref[...] + a1)
        y = xh * lg_ref[...] + lb_ref[...]
        mixed_ref[...] = mixed
        a1_ref[...] = a1
        h1_ref[...] = y
        h1b_ref[...] = y.astype(BF16)

    row = lambda width: pl.BlockSpec((tm, width), lambda i: (i, 0))
    whole = lambda a: pl.BlockSpec(a.shape, lambda i: (0, 0))
    sds = lambda dt: jax.ShapeDtypeStruct((s_dim, D_MODEL), dt)
    return pl.pallas_call(
        body,
        out_shape=[sds(BF16), sds(F32), sds(F32), sds(BF16)],
        grid=(s_dim // tm,),
        in_specs=[row(D_MODEL), row(D_MODEL), row(2 * D_MODEL), row(D_MODEL), whole(w_o), whole(g), whole(b)],
        out_specs=[row(D_MODEL)] * 4,
        compiler_params=pltpu.CompilerParams(dimension_semantics=("parallel",)),
        name="mix_proj_ln1",
    )(y_dn, y_mla, proj_g, x, w_o, g, b)


def _ln1_mix_bwd(x, a1, d_h1, d_pg, y_dn, y_mla, proj_g, g, w_o, w_pg):
    def fn(r, c):
        x_, a1_, dy, dpg, yd, ym, gates = r
        g_, w_o_, w_pg_ = c
        dy = dy + _dot(dpg, w_pg_, NT)
        xh, rr = _ln_stats(ALPHA * x_ + a1_)
        dz = _ln_bwd(dy, xh, rr, g_)
        dz_b = dz.astype(BF16)
        dm = _dot(dz_b, w_o_, NT)
        sd, sm = _sig(gates[:, :D_MODEL]), _sig(gates[:, D_MODEL:])
        d_g = jnp.concatenate([dm * yd * sd * (1.0 - sd), dm * ym * sm * (1.0 - sm)], axis=1)
        return [dz_b, ALPHA * dz, d_g, dm * sd, dm * sm], [_colsum(dy * xh), _colsum(dy)]

    return _rowwise(fn, [x, a1, d_h1, d_pg, y_dn, y_mla, proj_g], [g, w_o, w_pg],
                    [(D_MODEL, BF16), (D_MODEL, F32), (2 * D_MODEL, BF16), (D_MODEL, BF16), (D_MODEL, BF16)],
                    accs=[(1, D_MODEL), (1, D_MODEL)], tm=FUSED_ROWS, name="ln1_mix_bwd")


def _ln_stats(z):
    mu = _rowmean(z)
    zc = z - mu
    r = lax.rsqrt(_rowmean(zc * zc) + EPS_LN)
    return zc * r, r


def _ln_bwd(dy, xh, r, g):
    dxh = dy * g
    return r * (dxh - _rowmean(dxh) - xh * _rowmean(dxh * xh))


def _ffn_in_act(h1b, w_t):
    s_dim, k_dim = h1b.shape
    hidden = w_t.shape[0] // 2
    tm, tn = min(512, s_dim), _pick_wide(hidden)
    nt = hidden // tn

    def body(a_ref, bg_ref, bu_ref, gt_ref, up_ref, act_ref):
        a = a_ref[...]
        gt, up = _dot(a, bg_ref[...], NT), _dot(a, bu_ref[...], NT)
        gt_ref[...] = gt.astype(BF16)
        up_ref[...] = up.astype(BF16)
        act_ref[...] = (gt * _sig(gt) * up).astype(BF16)

    o_spec = pl.BlockSpec((tm, tn), lambda j, i: (i, j))
    sds = jax.ShapeDtypeStruct((s_dim, hidden), BF16)
    return pl.pallas_call(
        body,
        out_shape=[sds, sds, sds],
        grid=(nt, s_dim // tm),
        in_specs=[pl.BlockSpec((tm, k_dim), lambda j, i: (i, 0)), pl.BlockSpec((tn, k_dim), lambda j, i: (j, 0)),
                  pl.BlockSpec((tn, k_dim), lambda j, i: (j + nt, 0))],
        out_specs=[o_spec, o_spec, o_spec],
        compiler_params=pltpu.CompilerParams(dimension_semantics=("parallel", "parallel")),
        name="ffn_in_act",
    )(h1b, w_t, w_t)


def _act_bwd(gt, up, d_act):
    def fn(r, c):
        gt_, up_, da = r
        return [jnp.concatenate([da * up_ * _silu_grad(gt_), da * gt_ * _sig(gt_)], axis=1)], []

    return _rowwise(fn, [gt, up, d_act], [], [(2 * FFN_HIDDEN, BF16)], name="act_bwd")[0]


def _tail(h1, ffn, p, tgt, g, b, w_pg, w_ple_t):
    def fn(r, c):
        h1_, ffn_, p_, t_ = r
        pg_ = _dot(h1_, c[2])
        pp_ = _dot(p_, c[3], NT)
        sp = _sig(pg_)
        xh, rr = _ln_stats(ALPHA * h1_ + ffn_ + sp * pp_)
        y = xh * c[0] + c[1]
        err = y - t_
        dy = err * (1.0 / D_MODEL)
        dz = _ln_bwd(dy, xh, rr, c[0])
        loss = jnp.sum(0.5 * _rowmean(err * err), axis=0, keepdims=True)
        return ([dz, dz * pp_ * sp * (1.0 - sp), dz * sp, ALPHA * dz],
                [_colsum(dy * xh), _colsum(dy), jnp.broadcast_to(loss, (1, LANES))])

    return _rowwise(fn, [h1, ffn, p, tgt], [g, b, w_pg, w_ple_t], [(D_MODEL, BF16)] * 3 + [(D_MODEL, F32)],
                    accs=[(1, D_MODEL), (1, D_MODEL), (1, LANES)], tm=FUSED_ROWS, name="tail")


def _local_step(x, p, pos, tgt, w, late_weights, emit):
    w = dict(w)
    s_dim = x.shape[0]
    xb, pb = x.astype(BF16), p.astype(BF16)
    proj_a = _mm(xb, w["wa_t"], tb=True, name="f_proj_a")
    proj_g = _mm(xb, w["wg_t"], tb=True, out_dtype=BF16, name="f_proj_g")
    proj_b = _mm(xb, w["wb_t"], tb=True, name="f_proj_b")
    qkvn = _conv_fwd(proj_a, w["conv"])
    beta, gc = _gates_fwd(proj_b, w["alog"], w["dtb"])
    gc_t = jnp.transpose(gc[:, :N_HEADS])
    u, w_, qd, kt, a_mat, t_fold = _gdr_prep_fwd(qkvn, beta, gc, gc_t)
    o_dn, states = _gdr_scan_fwd(u, w_, qd, kt, a_mat, gc)
    w.update(late_weights("mix", o_dn))
    og, y_dn = _gdr_out_fwd(o_dn, proj_a, w["dnw"], w["br_dn"])
    c_q, c_kv, qc, kc, vv, kct, vt = _mla_prep_fwd(proj_b, pos, w["qnw"], w["kvnw"], w["uq"], w["uk"], w["uv"])
    o_mla, lse = _attn_fwd(qc, kc, vt)
    y_mla = _mm(o_mla, w["br_mla"], out_dtype=BF16, name="f_y_mla")
    mixed, a1, h1, h1b = _mix_proj_ln1(y_dn, y_mla, proj_g, x, w["wo"], w["ln1g"], w["ln1b"])
    w.update(late_weights("ffn", a1))
    gt, up, act = _ffn_in_act(h1b, w["ffn_in_t"])
    ffn = _mm(act, w["ffn_out"], name="f_ffn")
    g = {}
    dz2, d_pg, d_pp, dh1a, g["ln2g"], g["ln2b"], loss = _tail(h1, ffn, pb, tgt, w["ln2g"], w["ln2b"],
                                                            w["ple_gate"], w["ple_t"])
    g["ple_t"] = _mm(d_pp, pb, ta=True, out_dtype=BF16, name="b_w_ple")
    g["ple_gate"] = _mm(h1b, d_pg, ta=True, out_dtype=BF16, name="b_w_ple_gate")
    g["ffn_out"] = _mm(act, dz2, ta=True, out_dtype=BF16, name="b_w_ffn_out")
    d_act = _mm(dz2, w["ffn_out"], tb=True, out_dtype=BF16, name="b_act")
    d_gu = _act_bwd(gt, up, d_act)
    g["ffn_in_t"] = _mm(d_gu, h1b, ta=True, out_dtype=BF16, name="b_w_ffn_in")
    d_gu = emit("ffn", g, d_gu)
    d_h1 = _mm(d_gu, w["ffn_in_t"], add=(dh1a,), name="b_h1_ffn")
    dz1, dxa, d_proj_g, d_y_dn, d_y_mla, g["ln1g"], g["ln1b"] = _ln1_mix_bwd(
        x, a1, d_h1, d_pg, y_dn, y_mla, proj_g, w["ln1g"], w["wo"], w["ple_gate"])
    g["wo"] = _mm(mixed, dz1, ta=True, out_dtype=BF16, name="b_w_o")
    g["br_mla"] = _mm(o_mla, d_y_mla, ta=True, out_dtype=BF16, name="b_w_br_mla")
    d_o_mla = _mm(d_y_mla, w["br_mla"], tb=True, out_dtype=BF16, name="b_o_mla")
    d_qc, d_kc, d_v = _attn_bwd(qc, kc, kct, vv, o_mla, d_o_mla, lse)
    d_q_full, d_kn, d_cq, d_ckv, d_kr, g["qnw"], g["kvnw"] = _mla_prep_bwd(
        proj_b, pos, d_qc, d_kc, d_v, w["qnw"], w["kvnw"], w["uq"], w["uk"], w["uv"])
    g["uq"] = _mm(c_q, d_q_full, ta=True, out_dtype=BF16, name="b_w_uq")
    g["uk"] = _mm(c_kv, d_kn, ta=True, out_dtype=BF16, name="b_w_uk")
    g["uv"] = _mm(c_kv, d_v, ta=True, out_dtype=BF16, name="b_w_uv")
    g["br_dn"] = _mm(og, d_y_dn, ta=True, out_dtype=BF16, name="b_w_br_dn")
    d_y_dn = emit("mix", g, d_y_dn)
    d_o_dn, d_z, g["dnw"] = _gdr_out_bwd(o_dn, proj_a, d_y_dn, w["dnw"], w["br_dn"])
    du, dw, dqd, dkt, d_a, d_egl = _gdr_scan_bwd(u, w_, qd, kt, a_mat, gc, states, d_o_dn)
    dq, dk, dv, d_beta, d_gc = _gdr_prep_bwd(qkvn, beta, gc, gc_t, t_fold, u, w_, du, dw, dqd, dkt, d_a)
    d_egl_rows = jnp.pad(d_egl[:, None, :, 0], ((0, 0), (CHUNK - 1, 0), (0, LANES - N_HEADS))).reshape(s_dim, LANES)
    d_ba, g["alog"], g["dtb"] = _gates_bwd(proj_b, w["alog"], w["dtb"], gc, d_beta, d_gc, d_egl_rows)
    d_qkv, g["conv"] = _conv_bwd(proj_a, w["conv"], dq, dk, dv)
    zeros = jnp.zeros((s_dim, WB_CKV - Q_LORA), BF16)
    d_proj_b = jnp.concatenate([d_cq, zeros, d_ckv, d_kr, d_ba], axis=1)
    g["wa_qkv_t"] = _mm(d_qkv, xb, ta=True, name="b_w_qkv")
    g["wa_z_t"] = _mm(d_z, xb, ta=True, name="b_w_z")
    g["wg_t"] = _mm(d_proj_g, xb, ta=True, name="b_w_g")
    g["wb_t"] = _mm(d_proj_b, xb, ta=True, name="b_w_b")
    d_qkv = emit("w_in", g, d_qkv)
    dx = _mm(d_qkv, w["wa_qkv_t"], add=(dxa,), name="b_x_qkv")
    dx = _mm(d_z, w["wa_z_t"], add=(dx,), name="b_x_z")
    dx = _mm(d_proj_g, w["wg_t"], add=(dx,), name="b_x_g")
    dx = _mm(d_proj_b, w["wb_t"], add=(dx,), name="b_x_b")
    return loss, dx, g


_BIG = (("w_in", 1), ("w_uq", 0), ("w_uk", 0), ("w_uv", 0), ("w_br_dn", 0), ("w_br_mla", 0),
        ("w_o", 0), ("w_ffn_in", 1), ("w_ffn_out", 0), ("w_ple", 1), ("w_ple_gate", 0))
_BIG_AXIS = dict(_BIG)
_SMALL = ("ln1_g", "ln1_b", "ln2_g", "ln2_b", "q_norm_w", "kv_norm_w", "dn_norm_w", "dn_a_log", "dn_dt_bias")
_ORDER = ("w_in", "conv_w", "dn_a_log", "dn_dt_bias", "dn_norm_w", "q_norm_w", "w_uq", "kv_norm_w", "w_uk", "w_uv",
          "w_br_dn", "w_br_mla", "w_o", "ln1_g", "ln1_b", "w_ffn_in", "w_ffn_out", "w_ple", "w_ple_gate", "ln2_g",
          "ln2_b")


def _stored_shape(name, shard_shape):
    axis = _BIG_AXIS[name]
    lead = shard_shape[axis]
    return lead, int(np.prod(shard_shape)) // lead


def _to_stored(name, shard):
    return jnp.moveaxis(shard, _BIG_AXIS[name], 0).reshape(_stored_shape(name, shard.shape))


def _from_stored(name, stored, shard_shape):
    axis = _BIG_AXIS[name]
    moved = (shard_shape[axis],) + shard_shape[:axis] + shard_shape[axis + 1:]
    return jnp.moveaxis(stored.reshape(moved), 0, axis)


_W_IN_ROWS = np.cumsum([0, 3072, 1024, 8, 8, Q_LORA, KV_LORA, ROPE, D_MODEL, D_MODEL])


def _first_weights(w_in_t, conv_full, small):
    r = _W_IN_ROWS
    zr = lambda n: jnp.zeros((n, D_MODEL), w_in_t.dtype)
    w = {}
    w["wa_t"] = w_in_t[r[0]:r[2]]
    w["wa_qkv_t"], w["wa_z_t"] = w_in_t[r[0]:r[1]], w_in_t[r[1]:r[2]]
    w["wg_t"] = w_in_t[r[7]:r[9]]
    w["wb_t"] = jnp.concatenate([w_in_t[r[4]:r[5]], zr(WB_CKV - Q_LORA), w_in_t[r[5]:r[7]], zr(LANES - ROPE),
                                 w_in_t[r[2]:r[4]], zr(LANES - 2 * N_HEADS)], axis=0)
    w["conv"] = conv_full
    pad_l = lambda v: jnp.pad(v, ((0, 0), (0, LANES - v.shape[1])))
    w["alog"], w["dtb"] = pad_l(small["dn_a_log"]), pad_l(small["dn_dt_bias"])
    w["dnw"], w["qnw"], w["kvnw"] = small["dn_norm_w"], small["q_norm_w"], small["kv_norm_w"]
    w["ln1g"], w["ln1b"], w["ln2g"], w["ln2b"] = small["ln1_g"], small["ln1_b"], small["ln2_g"], small["ln2_b"]
    return w


def _late_weights(group, fw):
    w = {}
    if group == "mix":
        uq = fw["w_uq"].reshape(Q_LORA, N_HEADS, HEAD + ROPE)
        uq_r = jnp.pad(uq[:, :, HEAD:], ((0, 0), (0, 0), (0, HEAD - ROPE)))
        w["uq"] = jnp.concatenate([uq[:, :, :HEAD].reshape(Q_LORA, -1), uq_r.reshape(Q_LORA, -1)], axis=1)
        w["uk"], w["uv"] = fw["w_uk"], fw["w_uv"]
        w["br_dn"], w["br_mla"], w["wo"] = fw["w_br_dn"], fw["w_br_mla"], fw["w_o"]
    else:
        w["ffn_in_t"], w["ffn_out"] = fw["w_ffn_in"], fw["w_ffn_out"]
        w["ple_t"], w["ple_gate"] = fw["w_ple"], fw["w_ple_gate"]
    return w


_GROUP_GRADS = {"ffn": (("w_ple", "ple_t"), ("w_ple_gate", "ple_gate"), ("w_ffn_out", "ffn_out"),
                        ("w_ffn_in", "ffn_in_t")),
                "mix": (("w_o", "wo"), ("w_br_mla", "br_mla"), ("w_uq", "uq"), ("w_uk", "uk"), ("w_uv", "uv"),
                        ("w_br_dn", "br_dn"))}


def _group_grads(group, g):
    out = {}
    for name, key in _GROUP_GRADS[group]:
        t = g[key]
        if name == "w_uq":
            uq_n = t[:, :D_MODEL].reshape(Q_LORA, N_HEADS, HEAD)
            uq_r = t[:, D_MODEL:].reshape(Q_LORA, N_HEADS, HEAD)[:, :, :ROPE]
            t = jnp.concatenate([uq_n, uq_r], axis=2).reshape(Q_LORA, -1)
        out[name] = t
    return out


def _last_grads(g):
    wb = g["wb_t"]
    w_in = jnp.concatenate([
        g["wa_qkv_t"], g["wa_z_t"], wb[WB_BA:WB_BA + 2 * N_HEADS], wb[WB_CQ:WB_CQ + Q_LORA],
        wb[WB_CKV:WB_CKV + KV_LORA], wb[WB_KR:WB_KR + ROPE], g["wg_t"]], axis=0)
    small = {"ln1_g": g["ln1g"], "ln1_b": g["ln1b"], "ln2_g": g["ln2g"], "ln2_b": g["ln2b"], "q_norm_w": g["qnw"],
             "kv_norm_w": g["kvnw"], "dn_norm_w": g["dnw"], "dn_a_log": g["alog"], "dn_dt_bias": g["dtb"],
             "conv_w": g["conv"]}
    return w_in, small


_SMALL_SLOTS = {"ln1_g": (0, 0, 1024), "ln1_b": (1, 0, 1024), "ln2_g": (2, 0, 1024), "ln2_b": (3, 0, 1024),
                "q_norm_w": (4, 0, 384), "kv_norm_w": (4, 384, 256), "dn_norm_w": (4, 640, 128),
                "dn_a_log": (4, 768, 8), "dn_dt_bias": (4, 896, 8)}
_SMALL_ROWS, _LOSS_ROW, _CONV_ROW0, _CONV_ROWS = 24, 5, 8, 12


def _pack_small_grads(small_g, loss):
    zeros = lambda r, c: jnp.zeros((r, c), F32)
    row4 = jnp.concatenate([small_g["q_norm_w"], small_g["kv_norm_w"], small_g["dn_norm_w"], small_g["dn_a_log"],
                            small_g["dn_dt_bias"]], axis=1)
    row5 = jnp.concatenate([loss, zeros(1, FLAT_COLS - LANES)], axis=1)
    head = jnp.concatenate([small_g["ln1_g"], small_g["ln1_b"], small_g["ln2_g"], small_g["ln2_b"], row4, row5,
                            zeros(2, FLAT_COLS)], axis=0)
    conv = small_g["conv_w"].reshape(_CONV_ROWS, FLAT_COLS)
    return jnp.concatenate([head, conv, zeros(_SMALL_ROWS - _CONV_ROW0 - _CONV_ROWS, FLAT_COLS)], axis=0)


_MESH_ID = pl.DeviceIdType.MESH
_ANY = pl.BlockSpec(memory_space=pl.ANY)


def _all_gather(blocks, name):
    n = len(blocks)

    def body(*refs):
        x_refs, out_refs = refs[:n], refs[n:2 * n]
        send_sems, recv_sems, local_sems = refs[2 * n:]
        x, y, c = lax.axis_index("x"), lax.axis_index("y"), lax.axis_index("c")
        me, sibling = (x, y, c), (x, y, 1 - c)
        chips = [(1 - x, y), (x, 1 - y), (1 - x, 1 - y)]

        def slot(i, px, py, pc):
            return out_refs[i].at[4 * px + 2 * py + pc]

        def copy(i, k, origin, to, src=None):
            return pltpu.make_async_remote_copy(
                src_ref=slot(i, *origin) if src is None else src, dst_ref=slot(i, *origin),
                send_sem=send_sems.at[7 * i + k], recv_sem=recv_sems.at[7 * i + k], device_id=to,
                device_id_type=_MESH_ID)

        mine = [pltpu.make_async_copy(x_refs[i], slot(i, *me), local_sems.at[i]) for i in range(n)]
        first, passed = [], []
        for i in range(n):
            mine[i].start()
            first.append(copy(i, 0, me, sibling, src=x_refs[i]))
            first += [copy(i, 1 + j, me, (*chip, c), src=x_refs[i]) for j, chip in enumerate(chips)]
        for cp in first:
            cp.start()
        for i in range(n):
            for j, chip in enumerate(chips):
                copy(i, 1 + j, (*chip, c), me).wait_recv()
                passed.append(copy(i, 4 + j, (*chip, c), sibling))
                passed[-1].start()
        for i in range(n):
            copy(i, 0, sibling, me).wait_recv()
            for j, chip in enumerate(chips):
                copy(i, 4 + j, (*chip, 1 - c), me).wait_recv()
        for cp in first + passed:
            cp.wait_send()
        for cp in mine:
            cp.wait()

    return pl.pallas_call(
        body,
        out_shape=[jax.ShapeDtypeStruct((N_DEV,) + b.shape, b.dtype) for b in blocks],
        in_specs=[_ANY] * n,
        out_specs=[_ANY] * n,
        scratch_shapes=[pltpu.SemaphoreType.DMA((7 * n,)), pltpu.SemaphoreType.DMA((7 * n,)),
                        pltpu.SemaphoreType.DMA((n,))],
        name=name,
    )(*blocks)


def _exchange_sibling(srcs, name):
    n = len(srcs)

    def body(*refs):
        src_refs, dst_refs = refs[:n], refs[n:2 * n]
        send_sems, recv_sems = refs[2 * n:]
        x, y, c = lax.axis_index("x"), lax.axis_index("y"), lax.axis_index("c")
        copies = [pltpu.make_async_remote_copy(
            src_ref=src_refs[i].at[2 * q + (1 - c)], dst_ref=dst_refs[i].at[q], send_sem=send_sems.at[4 * i + q],
            recv_sem=recv_sems.at[4 * i + q], device_id=(x, y, 1 - c), device_id_type=_MESH_ID)
            for i in range(n) for q in range(4)]
        for cp in copies:
            cp.start()
        for cp in copies:
            cp.wait_recv()
        for cp in copies:
            cp.wait_send()

    return pl.pallas_call(
        body,
        out_shape=[jax.ShapeDtypeStruct((4,) + s.shape[1:], s.dtype) for s in srcs],
        in_specs=[_ANY] * n,
        out_specs=[_ANY] * n,
        scratch_shapes=[pltpu.SemaphoreType.DMA((4 * n,)), pltpu.SemaphoreType.DMA((4 * n,))],
        name=name,
    )(*srcs)


def _col_tile(c):
    return c if c <= 256 else 256


def _chip_sum(src, recv, parity, name):
    _, r, c = src.shape
    tc = _col_tile(c)

    def body(par_ref, a_ref, b_ref, o_ref, ob_ref):
        s = a_ref[...] + b_ref[...]
        o_ref[...] = s
        ob_ref[...] = s.astype(BF16)

    blk = lambda f: pl.BlockSpec((None, r, tc), f)
    return pl.pallas_call(
        body,
        out_shape=[jax.ShapeDtypeStruct((4, r, c), F32), jax.ShapeDtypeStruct((4, r, c), BF16)],
        grid_spec=pltpu.PrefetchScalarGridSpec(
            num_scalar_prefetch=1, grid=(4, c // tc),
            in_specs=[blk(lambda q, j, par: (2 * q + par[0], 0, j)), blk(lambda q, j, par: (q, 0, j))],
            out_specs=[blk(lambda q, j, par: (q, 0, j)), blk(lambda q, j, par: (q, 0, j))]),
        compiler_params=pltpu.CompilerParams(dimension_semantics=("parallel", "parallel")),
        name=name,
    )(parity, src, recv)


_HBM = pl.BlockSpec(memory_space=pltpu.HBM)
_SEM = pl.BlockSpec(memory_space=pltpu.SEMAPHORE)
_DATAFLOW = pltpu.SideEffectType.DATAFLOW_SIDE_EFFECTING
N_PEERS = N_DEV - 1


def _ring_peer(j):
    me = 4 * lax.axis_index("x") + 2 * lax.axis_index("y") + lax.axis_index("c")
    k = (me + j) % N_DEV
    return me, k, (k // 4, (k // 2) % 2, k % 2)


def _spread_copy(i, j, src_refs, land_refs, send_sems, recv_sems, scatter):
    me, k, peer = _ring_peer(j)
    return pltpu.make_async_remote_copy(
        src_ref=src_refs[i].at[k] if scatter else src_refs[i], dst_ref=land_refs[i].at[me],
        send_sem=send_sems.at[N_PEERS * i + j - 1], recv_sem=recv_sems.at[N_PEERS * i + j - 1], device_id=peer,
        device_id_type=_MESH_ID)


def _spread_start(srcs, carry, scatter, name):
    n = len(srcs)
    lands = [lax.empty(((N_DEV,) + s.shape[-2:]), s.dtype) for s in srcs]

    def body(*refs):
        src_refs, land_refs = refs[:n], refs[n:2 * n]
        send_sems, recv_sems, local_sems = refs[2 * n + 1:2 * n + 4]
        for i in range(n):
            for j in range(1, N_DEV):
                _spread_copy(i, j, src_refs, land_refs, send_sems, recv_sems, scatter).start()
        for i in range(n):
            _own_copy(i, src_refs, land_refs, local_sems, scatter).start()

    hbm = lambda a: pltpu.HBM(a.shape, a.dtype)
    sems = pltpu.SemaphoreType.DMA((N_PEERS * n,))
    pinned = [pltpu.with_memory_space_constraint(a, pltpu.HBM) for a in list(srcs) + lands + [carry]]
    res = pl.pallas_call(
        body, name=name,
        out_shape=(sems, sems, pltpu.SemaphoreType.DMA((n,)), *[hbm(a) for a in pinned]),
        in_specs=[_HBM] * (2 * n + 1),
        out_specs=(_SEM, _SEM, _SEM, *[_HBM] * (2 * n + 1)),
        input_output_aliases={i: 3 + i for i in range(2 * n + 1)},
        compiler_params=pltpu.CompilerParams(has_side_effects=_DATAFLOW),
    )(*pinned)
    return res[:3], list(res[3:3 + n]), list(res[3 + n:3 + 2 * n]), res[3 + 2 * n]


def _own_copy(i, src_refs, land_refs, local_sems, scatter):
    me = _ring_peer(0)[0]
    return pltpu.make_async_copy(src_refs[i].at[me] if scatter else src_refs[i], land_refs[i].at[me],
                                 local_sems.at[i])


def _spread_wait(started, after, scatter, name):
    sems, srcs, lands, _ = started
    n = len(srcs)

    def body(*refs):
        src_refs, land_refs = refs[:n], refs[n:2 * n]
        send_s, recv_s, local_s = refs[2 * n:2 * n + 3]
        for i in range(n):
            for j in range(1, N_DEV):
                cp = _spread_copy(i, j, src_refs, land_refs, send_s, recv_s, scatter)
                cp.wait_send()
                cp.wait_recv()
        for i in range(n):
            _own_copy(i, src_refs, land_refs, local_s, scatter).wait()

    hbm = lambda a: pltpu.HBM(a.shape, a.dtype)
    res = pl.pallas_call(
        body, name=name,
        out_shape=tuple(hbm(a) for a in srcs + lands),
        in_specs=[_HBM] * (2 * n) + [_SEM, _SEM, _SEM, pl.BlockSpec(memory_space=pl.ANY)],
        out_specs=tuple([_HBM] * (2 * n)),
        input_output_aliases={i: i for i in range(2 * n)},
        compiler_params=pltpu.CompilerParams(has_side_effects=_DATAFLOW),
    )(*srcs, *lands, *sems, after)
    return list(res[n:])


def _chips_copy(i, j, src_refs, land_refs, send_sems, recv_sems):
    x, y, c = lax.axis_index("x"), lax.axis_index("y"), lax.axis_index("c")
    tx, ty = [(1 - x, y), (x, 1 - y), (1 - x, 1 - y)][j]
    return pltpu.make_async_remote_copy(
        src_ref=src_refs[i].at[2 * tx + ty], dst_ref=land_refs[i].at[j], send_sem=send_sems.at[3 * i + j],
        recv_sem=recv_sems.at[3 * i + j], device_id=(tx, ty, c), device_id_type=_MESH_ID)


def _chips_start(srcs, carry, name):
    n = len(srcs)
    lands = [lax.empty((3,) + s.shape[1:], s.dtype) for s in srcs]

    def body(*refs):
        src_refs, land_refs = refs[:n], refs[n:2 * n]
        send_sems, recv_sems = refs[2 * n + 1:2 * n + 3]
        for i in range(n):
            for j in range(3):
                _chips_copy(i, j, src_refs, land_refs, send_sems, recv_sems).start()

    hbm = lambda a: pltpu.HBM(a.shape, a.dtype)
    sems = pltpu.SemaphoreType.DMA((3 * n,))
    pinned = [pltpu.with_memory_space_constraint(a, pltpu.HBM) for a in list(srcs) + lands + [carry]]
    res = pl.pallas_call(
        body, name=name,
        out_shape=(sems, sems, *[hbm(a) for a in pinned]),
        in_specs=[_HBM] * (2 * n + 1),
        out_specs=(_SEM, _SEM, *[_HBM] * (2 * n + 1)),
        input_output_aliases={i: 2 + i for i in range(2 * n + 1)},
        compiler_params=pltpu.CompilerParams(has_side_effects=_DATAFLOW),
    )(*pinned)
    return res[:2], list(res[2:2 + n]), list(res[2 + n:2 + 2 * n]), res[2 + 2 * n]


def _chips_wait(started, after, name):
    sems, srcs, lands, _ = started
    n = len(srcs)

    def body(*refs):
        src_refs, land_refs = refs[:n], refs[n:2 * n]
        send_s, recv_s = refs[2 * n:2 * n + 2]
        for i in range(n):
            for j in range(3):
                cp = _chips_copy(i, j, src_refs, land_refs, send_s, recv_s)
                cp.wait_send()
                cp.wait_recv()

    hbm = lambda a: pltpu.HBM(a.shape, a.dtype)
    res = pl.pallas_call(
        body, name=name,
        out_shape=tuple(hbm(a) for a in srcs + lands),
        in_specs=[_HBM] * (2 * n) + [_SEM, _SEM, pl.BlockSpec(memory_space=pl.ANY)],
        out_specs=tuple([_HBM] * (2 * n)),
        input_output_aliases={i: i for i in range(2 * n)},
        compiler_params=pltpu.CompilerParams(has_side_effects=_DATAFLOW),
    )(*srcs, *lands, *sems, after)
    return list(res[n:])


def _sum8(landing, name):
    _, r, c = landing.shape
    tc = _col_tile(c)

    def body(a_ref, o_ref):
        tot = a_ref[0].astype(F32)
        for k in range(1, N_DEV):
            tot = tot + a_ref[k].astype(F32)
        o_ref[...] = tot

    return pl.pallas_call(
        body,
        out_shape=jax.ShapeDtypeStruct((r, c), F32),
        grid=(c // tc,),
        in_specs=[pl.BlockSpec((N_DEV, r, tc), lambda j: (0, 0, j))],
        out_specs=pl.BlockSpec((r, tc), lambda j: (0, j)),
        compiler_params=pltpu.CompilerParams(dimension_semantics=("parallel",)),
        name=name,
    )(landing)


def _adamw_math(w, g, m, v):
    m = ADAM_B1 * m + (1.0 - ADAM_B1) * g
    v = ADAM_B2 * v + (1.0 - ADAM_B2) * (g * g)
    m_hat = m / (1.0 - ADAM_B1 ** ADAM_STEP)
    v_hat = v / (1.0 - ADAM_B2 ** ADAM_STEP)
    delta = -ADAM_LR * (m_hat / (jnp.sqrt(v_hat) + ADAM_EPS) + ADAM_WD * w)
    return delta, m, v


def _adamw(w, m, v, g, name):
    r, c = w.shape

    def fn(rows, consts):
        return list(_adamw_math(*rows)), []

    return _rowwise(fn, [w, g, m, v], [], [(c, F32)] * 3, tm=r if r <= 512 else 256, name=name)


def _adamw_sum8(w, m, v, landing, name):
    r, c = w.shape
    tc = _col_tile(c)

    def body(w_ref, m_ref, v_ref, a_ref, g_ref, d_ref, m2_ref, v2_ref):
        g = a_ref[0].astype(F32)
        for k in range(1, N_DEV):
            g = g + a_ref[k].astype(F32)
        delta, m2, v2 = _adamw_math(w_ref[...], g, m_ref[...], v_ref[...])
        g_ref[...] = g
        d_ref[...] = delta
        m2_ref[...] = m2
        v2_ref[...] = v2

    blk = pl.BlockSpec((r, tc), lambda j: (0, j))
    return pl.pallas_call(
        body,
        out_shape=[jax.ShapeDtypeStruct((r, c), F32)] * 4,
        grid=(c // tc,),
        in_specs=[blk, blk, blk, pl.BlockSpec((N_DEV, r, tc), lambda j: (0, 0, j))],
        out_specs=[blk] * 4,
        compiler_params=pltpu.CompilerParams(dimension_semantics=("parallel",)),
        name=name,
    )(w, m, v, landing)


def _adamw_parts(w, m, v, own, others, chip, name):
    r, _, c = w.shape
    tc = _col_tile(c)

    def body(q_ref, w_ref, m_ref, v_ref, a_ref, b_ref, g_ref, d_ref, m2_ref, v2_ref):
        g = ((a_ref[...] + b_ref[0].astype(F32)) + b_ref[1].astype(F32)) + b_ref[2].astype(F32)
        delta, m2, v2 = _adamw_math(w_ref[...], g, m_ref[...], v_ref[...])
        g_ref[...] = g
        d_ref[...] = delta
        m2_ref[...] = m2
        v2_ref[...] = v2

    row = pl.BlockSpec((r, None, tc), lambda j, q: (0, 0, j))
    return pl.pallas_call(
        body,
        out_shape=[jax.ShapeDtypeStruct((r, 1, c), F32)] * 4,
        grid_spec=pltpu.PrefetchScalarGridSpec(
            num_scalar_prefetch=1, grid=(c // tc,),
            in_specs=[row, row, row, pl.BlockSpec((None, r, tc), lambda j, q: (q[0], 0, j)),
                      pl.BlockSpec((3, r, tc), lambda j, q: (0, 0, j))],
            out_specs=[row] * 4),
        compiler_params=pltpu.CompilerParams(dimension_semantics=("parallel",)),
        name=name,
    )(chip, w, m, v, own, others)


def _adamw_small(gathered, params):
    ns = len(_SMALL)

    def body(*refs):
        g_ref, p_refs, o_refs = refs[0], refs[1:1 + 3 * ns], refs[1 + 3 * ns:]
        tot = g_ref[0]
        for k in range(1, N_DEV):
            tot = tot + g_ref[k]
        for i, name in enumerate(_SMALL):
            row, lane0, lanes = _SMALL_SLOTS[name]
            g = tot[row:row + 1, lane0:lane0 + lanes]
            w_, m_, v_ = (p_refs[3 * i + j][...] for j in range(3))
            delta, m2, v2 = _adamw_math(w_, g, m_, v_)
            for j, val in enumerate((g, delta, m2, v2)):
                o_refs[4 * i + j][...] = val
        o_refs[4 * ns][...] = tot[_LOSS_ROW:_LOSS_ROW + 1, 0:LANES]
        o_refs[4 * ns + 1][...] = tot[_CONV_ROW0:_CONV_ROW0 + _CONV_ROWS, :]

    out_shape = [jax.ShapeDtypeStruct(w.shape, F32) for (w, _, _) in params for _ in range(4)]
    out_shape += [jax.ShapeDtypeStruct((1, LANES), F32), jax.ShapeDtypeStruct((_CONV_ROWS, FLAT_COLS), F32)]
    flat = [a for wmv in params for a in wmv]
    return pl.pallas_call(body, out_shape=out_shape, name="adamw_small")(gathered, *flat)


def kernel(x, p, positions, w_in, conv_w, dn_a_log, dn_dt_bias, dn_norm_w, q_norm_w, w_uq, kv_norm_w, w_uk, w_uv, w_br_dn, w_br_mla, w_o, ln1_g, ln1_b, w_ffn_in, w_ffn_out, w_ple, w_ple_gate, ln2_g, ln2_b, loss_target, m_w_in, m_conv_w, m_dn_a_log, m_dn_dt_bias, m_dn_norm_w, m_q_norm_w, m_w_uq, m_kv_norm_w, m_w_uk, m_w_uv, m_w_br_dn, m_w_br_mla, m_w_o, m_ln1_g, m_ln1_b, m_w_ffn_in, m_w_ffn_out, m_w_ple, m_w_ple_gate, m_ln2_g, m_ln2_b, v_w_in, v_conv_w, v_dn_a_log, v_dn_dt_bias, v_dn_norm_w, v_q_norm_w, v_w_uq, v_kv_norm_w, v_w_uk, v_w_uv, v_w_br_dn, v_w_br_mla, v_w_o, v_ln1_g, v_ln1_b, v_w_ffn_in, v_w_ffn_out, v_w_ple, v_w_ple_gate, v_ln2_g, v_ln2_b):
    args = dict(locals())
    wts = {n: args[n] for n in _ORDER}
    mom1 = {n: args["m_" + n] for n in _ORDER}
    mom2 = {n: args["v_" + n] for n in _ORDER}
    big_names = [n for n, _ in _BIG]
    shard_shapes = {n: wts[n].shape[1:] for n in big_names}
    c_idx = lax.axis_index("c")
    q_idx = 2 * lax.axis_index("x") + lax.axis_index("y")
    parity, chip = c_idx.reshape(1).astype(jnp.int32), q_idx.reshape(1).astype(jnp.int32)

    stored = {n: _to_stored(n, wts[n][0]).astype(BF16) for n in big_names}
    first = _all_gather([stored["w_in"], conv_w[0]], "ag_first")
    group_names = {grp: [n for n, _ in pairs] for grp, pairs in _GROUP_GRADS.items()}
    carry, gathers = first[0], {}
    for grp in ("mix", "ffn"):
        gathers[grp] = _spread_start([stored[n] for n in group_names[grp]], carry, False, "ag_start_" + grp)
        carry = gathers[grp][3]
    conv_full = jnp.moveaxis(first[1], 0, 1).reshape(conv_w.shape[1], -1)
    small_w = {n: wts[n].astype(F32) for n in _SMALL}
    w = _first_weights(carry.reshape(-1, D_MODEL), conv_full, small_w)

    def late_weights(grp, after):
        got = _spread_wait(gathers[grp], after, False, "ag_wait_" + grp)
        return _late_weights(grp, {n: t.reshape(-1, t.shape[-1]) for n, t in zip(group_names[grp], got)})

    started = {}

    def emit(group, g, carry):
        if group == "w_in":
            src = _last_grads(g)[0].reshape((N_DEV,) + _stored_shape("w_in", shard_shapes["w_in"]))
            from_sibling = _exchange_sibling([src], "rs_sibling")[0]
            own, own_bf = _chip_sum(src, from_sibling, parity, "rs_sum_w_in")
            started["w_in"] = (own, _chips_start([own_bf], carry, "rs_chips_start"))
            return started["w_in"][1][3]
        grads = _group_grads(group, g)
        srcs = [grads[n].reshape((N_DEV,) + _stored_shape(n, shard_shapes[n])) for n in grads]
        started[group] = (list(grads), _spread_start(srcs, carry, True, "rs_start_" + group))
        return started[group][1][3]

    s_dim = x.shape[1]
    loss, dx, g = _local_step(x[0], p[0, 0], positions.reshape(s_dim, 1).astype(F32), loss_target[0], w,
                              late_weights, emit)
    small_g = _last_grads(g)[1]
    own, chips_started = started.pop("w_in")
    from_chips = _chips_wait(chips_started, dx, "rs_chips_wait")[0]

    out_g, out_d, out_m, out_v = {}, {}, {}, {}

    def update(n, grad, shp):
        flat2 = (shp[0], int(np.prod(shp[1:])))
        d, m2, v2 = _adamw(wts[n][0].reshape(flat2), mom1[n][0].reshape(flat2), mom2[n][0].reshape(flat2),
                           grad.reshape(flat2), "adamw_" + n)
        out_g[n], out_d[n], out_m[n], out_v[n] = grad, d.reshape(shp), m2.reshape(shp), v2.reshape(shp)

    rows_first = lambda a: jnp.transpose(a, (2, 0, 1))
    res = _adamw_parts(rows_first(wts["w_in"]), rows_first(mom1["w_in"]), rows_first(mom2["w_in"]), own, from_chips,
                       chip, "adamw_w_in")
    out_g["w_in"], out_d["w_in"], out_m["w_in"], out_v["w_in"] = (jnp.transpose(t, (1, 2, 0))[0] for t in res)
    for group, (names, st) in started.items():
        for n, landing in zip(names, _spread_wait(st, dx, True, "rs_wait_" + group)):
            shp = shard_shapes[n]
            if _BIG_AXIS[n] == 0 or shp[-1] % LANES:
                res = _adamw_sum8(_to_stored(n, wts[n][0]), _to_stored(n, mom1[n][0]), _to_stored(n, mom2[n][0]),
                                  landing, "adamw_" + n)
                out_g[n], out_d[n], out_m[n], out_v[n] = (_from_stored(n, t, shp) for t in res)
            else:
                update(n, _from_stored(n, _sum8(landing, "rs_total_" + n), shp), shp)

    g_small = _all_gather([_pack_small_grads(small_g, loss)], "ag_small")[0]
    res = _adamw_small(g_small, [(wts[n], mom1[n], mom2[n]) for n in _SMALL])
    for i, n in enumerate(_SMALL):
        out_g[n], out_d[n], out_m[n], out_v[n] = res[4 * i:4 * i + 4]
    loss_out = res[4 * len(_SMALL)][0, 0]
    conv_shape = conv_w.shape[1:]
    conv_g = lax.dynamic_slice(res[-1].reshape(conv_shape[0], -1), (0, (2 * q_idx + c_idx) * conv_shape[1]),
                               conv_shape)
    update("conv_w", conv_g, conv_shape)

    expand = lambda d, n: d[n] if n in _SMALL else d[n][None]
    return (loss_out, dx[None], *[expand(out_g, n) for n in _ORDER], *[expand(out_d, n) for n in _ORDER],
            *[expand(out_m, n) for n in _ORDER], *[expand(out_v, n) for n in _ORDER])
```

```python
import functools

import numpy as np
import jax
import jax.numpy as jnp
from jax import lax
from jax.experimental import pallas as pl
from jax.experimental.pallas import tpu as pltpu

F32 = jnp.float32
BF16 = jnp.bfloat16

D_MODEL = 1024
N_HEADS = 8
HEAD = 128
CHUNK = 64
GROUP = 256
ROPE = 64
Q_LORA = 384
KV_LORA = 256
FFN_HIDDEN = 2816
PLE_DIM = 256
ROPE_BASE = 10000.0
ALPHA = 2.0 ** 0.25
SCALE = float((HEAD + ROPE) ** -0.5)
NEG_BIG = -1e30
EPS_RMS = 1e-6
EPS_LN = 1e-5

ADAM_LR = 0.001
ADAM_B1 = 0.9
ADAM_B2 = 0.999
ADAM_EPS = 1e-08
ADAM_WD = 0.01
ADAM_STEP = 10

N_DEV = 8
LANES = 128
FLAT_COLS = 1024

WB_CQ, WB_CKV, WB_KR, WB_BA, WB_COLS = 0, 512, 768, 896, 1024

HIGHEST = lax.Precision.HIGHEST

NN = (((1,), (0,)), ((), ()))
TN = (((0,), (0,)), ((), ()))
NT = (((1,), (1,)), ((), ()))


def _dot(a, b, dims=NN):
    return lax.dot_general(a.astype(BF16), b.astype(BF16), dims, preferred_element_type=F32)


def _dot32(a, b, dims=NN):
    return lax.dot_general(a, b, dims, precision=HIGHEST, preferred_element_type=F32)


def _sig(x):
    return 1.0 / (1.0 + jnp.exp(-x))


MM_TILE = 1536


def _pick_wide(n):
    if n <= MM_TILE:
        return n
    return max(t for t in range(LANES, MM_TILE + 1, LANES) if n % t == 0)


def _split_bf16(a):
    hi = a.astype(BF16)
    return hi, (a - hi.astype(F32)).astype(BF16)


def _dot3(a, b, dims=NN):
    ah, al = a if isinstance(a, tuple) else _split_bf16(a)
    bh, bl = b if isinstance(b, tuple) else _split_bf16(b)
    d = lambda p, q: lax.dot_general(p, q, dims, preferred_element_type=F32)
    return d(ah, bh) + (d(ah, bl) + d(al, bh))


def _mm(a, b, *, ta=False, tb=False, add=(), out_dtype=F32, name):
    if ta:
        k_dim, m_dim = a.shape
    else:
        m_dim, k_dim = a.shape
    if tb:
        n_dim, k2 = b.shape
    else:
        k2, n_dim = b.shape
    assert k_dim == k2, (a.shape, b.shape, ta, tb)
    tm = _pick_wide(m_dim)
    tn = _pick_wide(n_dim)
    tk = _pick_wide(k_dim)
    nk = k_dim // tk
    n_add = len(add)
    dims = TN if ta else (NT if tb else NN)
    assert not (ta and tb)

    def body(a_ref, b_ref, *rest):
        add_refs = rest[:n_add]
        o_ref = rest[n_add]
        acc = rest[n_add + 1]
        k = pl.program_id(2)

        @pl.when(k == 0)
        def _():
            acc[...] = jnp.zeros_like(acc)

        acc[...] += _dot(a_ref[...], b_ref[...], dims)

        @pl.when(k == nk - 1)
        def _():
            r = acc[...]
            for ar in add_refs:
                r = r + ar[...].astype(F32)
            o_ref[...] = r.astype(o_ref.dtype)

    a_spec = pl.BlockSpec((tk, tm), lambda i, j, k: (k, i)) if ta else pl.BlockSpec((tm, tk), lambda i, j, k: (i, k))
    b_spec = pl.BlockSpec((tn, tk), lambda i, j, k: (j, k)) if tb else pl.BlockSpec((tk, tn), lambda i, j, k: (k, j))
    o_spec = pl.BlockSpec((tm, tn), lambda i, j, k: (i, j))
    return pl.pallas_call(
        body,
        out_shape=jax.ShapeDtypeStruct((m_dim, n_dim), out_dtype),
        grid=(m_dim // tm, n_dim // tn, nk),
        in_specs=[a_spec, b_spec] + [o_spec] * n_add,
        out_specs=o_spec,
        scratch_shapes=[pltpu.VMEM((tm, tn), F32)],
        compiler_params=pltpu.CompilerParams(dimension_semantics=("parallel", "parallel", "arbitrary")),
        name=name,
    )(a, b, *add)


def _rowwise(fn, rows, consts, outs, accs=(), *, tm=256, name):
    rows = [r if isinstance(r, tuple) else (r, 0, r.shape[1]) for r in rows]
    s_dim = rows[0][0].shape[0]
    tm = min(tm, s_dim)
    assert s_dim % tm == 0 and all(arr.shape[0] == s_dim for arr, _, _ in rows)
    specs = [pl.BlockSpec((tm, width), functools.partial(lambda i, cb: (i, cb), cb=cb)) for _, cb, width in rows]
    args = [arr for arr, _, _ in rows]
    for c in consts:
        specs.append(pl.BlockSpec(c.shape, lambda i: (0, 0)))
        args.append(c)
    nr, nc, no = len(rows), len(consts), len(outs)
    flipped = [len(o) == 3 for o in outs]
    out_shape = [jax.ShapeDtypeStruct((o[0], s_dim) if t else (s_dim, o[0]), o[1]) for o, t in zip(outs, flipped)]
    out_specs = [pl.BlockSpec((o[0], tm), lambda i: (0, i)) if t else pl.BlockSpec((tm, o[0]), lambda i: (i, 0))
                 for o, t in zip(outs, flipped)]
    out_shape += [jax.ShapeDtypeStruct(sh, F32) for sh in accs]
    out_specs += [pl.BlockSpec(sh, lambda i: (0, 0)) for sh in accs]

    def body(*refs):
        r = [x[...].astype(F32) if x.dtype == BF16 else x[...] for x in refs[:nr]]
        c = [x[...] for x in refs[nr:nr + nc]]
        o_refs = refs[nr + nc:nr + nc + no]
        a_refs = refs[nr + nc + no:]
        o_vals, a_vals = fn(r, c)
        for ref, v, t in zip(o_refs, o_vals, flipped, strict=True):
            ref[...] = (jnp.transpose(v.astype(F32)) if t else v).astype(ref.dtype)
        if a_refs:
            @pl.when(pl.program_id(0) == 0)
            def _():
                for ref in a_refs:
                    ref[...] = jnp.zeros_like(ref)

            for ref, v in zip(a_refs, a_vals, strict=True):
                ref[...] += v

    res = pl.pallas_call(
        body,
        out_shape=out_shape,
        grid=(s_dim // tm,),
        in_specs=specs,
        out_specs=out_specs,
        compiler_params=pltpu.CompilerParams(dimension_semantics=("arbitrary" if accs else "parallel",)),
        name=name,
    )(*args)
    return res


def _colsum(v):
    return jnp.sum(v, axis=0, keepdims=True)


def _rowsum(v):
    return jnp.sum(v, axis=1, keepdims=True)


def _rowmean(v):
    return jnp.mean(v, axis=1, keepdims=True)


def _silu_grad(x):
    s = _sig(x)
    return s * (1.0 + x * (1.0 - s))


def _conv_taps(x, w, width=4):
    row = lax.broadcasted_iota(jnp.int32, x.shape, 0)
    c = x * w[width - 1:width, :]
    for s in range(1, width):
        c = c + jnp.where(row >= s, pltpu.roll(x, s, 0), 0.0) * w[width - 1 - s:width - s, :]
    return c


def _conv_fwd(proj_a, conv_w):
    s_dim = proj_a.shape[0]
    n_blk = 3 * N_HEADS

    def body(x_ref, w_ref, o_ref):
        j = pl.program_id(0)
        c = _conv_taps(x_ref[...], w_ref[...])
        y = c * _sig(c)
        r = lax.rsqrt(_rowsum(y * y) + EPS_RMS)
        fac = jnp.where(j < N_HEADS, r * (HEAD ** -0.5), jnp.where(j < 2 * N_HEADS, r, 1.0))
        o_ref[...] = y * fac

    return pl.pallas_call(
        body,
        out_shape=jax.ShapeDtypeStruct((s_dim, n_blk * HEAD), F32),
        grid=(n_blk,),
        in_specs=[pl.BlockSpec((s_dim, HEAD), lambda j: (0, j)), pl.BlockSpec((4, HEAD), lambda j: (0, j))],
        out_specs=pl.BlockSpec((s_dim, HEAD), lambda j: (0, j)),
        compiler_params=pltpu.CompilerParams(dimension_semantics=("parallel",)),
        name="conv_fwd",
    )(proj_a, conv_w)


def _conv_bwd(proj_a, conv_w, dq, dk, dv):
    s_dim = proj_a.shape[0]
    n_blk = 3 * N_HEADS

    def body(x_ref, w_ref, dq_ref, dk_ref, dv_ref, dx_ref, dw_ref):
        j = pl.program_id(0)
        x = x_ref[...]
        w = w_ref[...]
        do = jnp.where(j < N_HEADS, dq_ref[...], jnp.where(j < 2 * N_HEADS, dk_ref[...], dv_ref[...]))
        c = _conv_taps(x, w)
        sg = _sig(c)
        y = c * sg
        r = lax.rsqrt(_rowsum(y * y) + EPS_RMS)
        sc = jnp.where(j < N_HEADS, HEAD ** -0.5, 1.0)
        dy_n = sc * (r * do - y * (r * r * r) * _rowsum(do * y))
        dy = jnp.where(j < 2 * N_HEADS, dy_n, do)
        dc = dy * (sg * (1.0 + c * (1.0 - sg)))
        row = lax.broadcasted_iota(jnp.int32, x.shape, 0)
        dx = dc * w[3:4, :]
        dw_ref[3:4, :] = _colsum(dc * x)
        for s in range(1, 4):
            dx = dx + jnp.where(row < s_dim - s, pltpu.roll(dc, s_dim - s, 0), 0.0) * w[3 - s:4 - s, :]
            xs = jnp.where(row >= s, pltpu.roll(x, s, 0), 0.0)
            dw_ref[3 - s:4 - s, :] = _colsum(dc * xs)
        dx_ref[...] = dx.astype(dx_ref.dtype)

    hd = N_HEADS - 1
    return pl.pallas_call(
        body,
        out_shape=[jax.ShapeDtypeStruct((s_dim, n_blk * HEAD), BF16), jax.ShapeDtypeStruct((4, n_blk * HEAD), F32)],
        grid=(n_blk,),
        in_specs=[
            pl.BlockSpec((s_dim, HEAD), lambda j: (0, j)),
            pl.BlockSpec((4, HEAD), lambda j: (0, j)),
            pl.BlockSpec((s_dim, HEAD), lambda j: (0, jnp.minimum(j, hd))),
            pl.BlockSpec((s_dim, HEAD), lambda j: (0, jnp.clip(j - N_HEADS, 0, hd))),
            pl.BlockSpec((s_dim, HEAD), lambda j: (0, jnp.clip(j - 2 * N_HEADS, 0, hd))),
        ],
        out_specs=[pl.BlockSpec((s_dim, HEAD), lambda j: (0, j)), pl.BlockSpec((4, HEAD), lambda j: (0, j))],
        compiler_params=pltpu.CompilerParams(dimension_semantics=("parallel",)),
        name="conv_bwd",
    )(proj_a, conv_w, dq, dk, dv)


def _chunk_tri(n):
    r = np.arange(n)
    m = ((r[:, None] // CHUNK) == (r[None, :] // CHUNK)) & (r[:, None] >= r[None, :])
    m = m.astype(np.float32)
    return jnp.asarray(m), jnp.asarray(m.T)


def _softplus(z):
    return jnp.maximum(z, 0.0) + jnp.log(1.0 + jnp.exp(-jnp.abs(z)))


def _gates_fwd(proj_b, alog, dtb):
    tm = min(GROUP, proj_b.shape[0])
    tri, _ = _chunk_tri(tm)

    def fn(r, c):
        b = r[0]
        a = pltpu.roll(b, LANES - N_HEADS, 1)
        alog_, dtb_, tri_ = c
        g = -jnp.exp(alog_) * _softplus(a + dtb_)
        return [_sig(b), _dot32(tri_, g)], []

    return _rowwise(fn, [(proj_b, WB_BA // LANES, LANES)], [alog, dtb, tri],
                    [(LANES, F32), (LANES, F32)], tm=tm, name="gates_fwd")


def _gates_bwd(proj_b, alog, dtb, gc, d_beta, d_gc, d_egl_rows):
    tm = min(GROUP, proj_b.shape[0])
    _, tri_t = _chunk_tri(tm)

    def fn(r, c):
        b, gc_, d_beta_, d_gc_, d_egl_ = r
        a = pltpu.roll(b, LANES - N_HEADS, 1)
        alog_, dtb_, tri_t_ = c
        z = a + dtb_
        ea = jnp.exp(alog_)
        g = -ea * _softplus(z)
        dg = _dot32(tri_t_, d_gc_ + d_egl_ * jnp.exp(gc_))
        d_a = dg * (-ea) * _sig(z)
        beta = _sig(b)
        d_ba = d_beta_ * beta * (1.0 - beta) + pltpu.roll(d_a, N_HEADS, 1)
        return [d_ba], [_colsum(dg * g), _colsum(d_a)]

    return _rowwise(fn, [(proj_b, WB_BA // LANES, LANES), gc, d_beta, d_gc, d_egl_rows],
                    [alog, dtb, tri_t], [(LANES, BF16)], accs=[(1, LANES), (1, LANES)], tm=tm,
                    name="gates_bwd")


def _group_masks(n):
    r = lax.broadcasted_iota(jnp.int32, (n, n), 0)
    c = lax.broadcasted_iota(jnp.int32, (n, n), 1)
    same = (r // CHUNK) == (c // CHUNK)
    below, s = [], 2
    while s < CHUNK:
        below.append(jnp.logical_and((r // (2 * s)) == (c // (2 * s)),
                                     jnp.logical_and((r // s) % 2 == 1, (c // s) % 2 == 0)))
        s *= 2
    return dict(same=same, tril=jnp.logical_and(same, r >= c), strict=jnp.logical_and(same, r > c),
                last=c == (r // CHUNK) * CHUNK + (CHUNK - 1), eye=r == c, pair=(r // 2) == (c // 2), below=below)


def _inv_unit_lower(l_mats, mk):
    eye_f = mk["eye"].astype(F32)
    ts = [eye_f - jnp.where(mk["pair"], l_mat, 0.0) for l_mat in l_mats]
    for below in mk["below"]:
        halves = [_split_bf16(t) for t in ts]
        mids = [_dot3(h, jnp.where(below, l_mat, 0.0)) for h, l_mat in zip(halves, l_mats)]
        ts = [t - _dot3(m, h) for t, m, h in zip(ts, mids, halves)]
    return ts


def _unfold_blocks(folded, mask):
    n = folded.shape[0]
    return jnp.where(mask, jnp.concatenate([folded] * (n // CHUNK), axis=1), 0.0)


def _head_cols(beta, gc, gc_t, h):
    lane = lax.broadcasted_iota(jnp.int32, beta.shape, 1)
    sub = lax.broadcasted_iota(jnp.int32, gc_t.shape, 0)
    bcol = _rowsum(jnp.where(lane == h, beta, 0.0))
    gcol = _rowsum(jnp.where(lane == h, gc, 0.0))
    grow = _colsum(jnp.where(sub == h, gc_t, 0.0))
    return bcol, gcol, grow


def _prep_common(q, k, bcol, gcol, grow, mk, t_folded=None):
    n = q.shape[0]
    tril = mk["tril"]
    decay = jnp.where(tril, jnp.exp(jnp.where(tril, gcol - grow, 0.0)), 0.0)
    glast = _rowsum(jnp.where(mk["last"], jnp.broadcast_to(grow, (n, n)), 0.0))
    e = jnp.exp(gcol)
    ekt = jnp.exp(glast - gcol)
    kb = k * bcol
    kk = _dot(kb, k, NT)
    qk = _dot(q, k, NT)
    p = dict(decay=decay, e=e, ekt=ekt, kb=kb, kk=kk, qk=qk)
    if t_folded is not None:
        p["t"] = _unfold_blocks(t_folded, mk["same"])
    return p


GROUPS_PER_STEP = 4
SCAN_CHUNKS_PER_STEP = 4


def _fold_blocks(m):
    n = m.shape[0]
    out = m[:, 0:CHUNK]
    for b in range(1, n // CHUNK):
        out = out + m[:, b * CHUNK:(b + 1) * CHUNK]
    return out


def _gdr_prep_fwd(qkvn, beta, gc, gc_t):
    s_dim = qkvn.shape[0]
    tg = min(GROUP, s_dim)
    n_sub = min(GROUPS_PER_STEP, s_dim // tg)
    tb = tg * n_sub

    def body(q_ref, k_ref, v_ref, b_ref, g_ref, gt_ref, u_ref, w_ref, qd_ref, kt_ref, a_ref, t_ref):
        h = pl.program_id(0)
        mk = _group_masks(tg)
        parts = []
        for s in range(n_sub):
            rows = slice(s * tg, (s + 1) * tg)
            q, k, v = q_ref[rows, :], k_ref[rows, :], v_ref[rows, :]
            bcol, gcol, grow = _head_cols(b_ref[rows, :], g_ref[rows, :], gt_ref[:, rows], h)
            p = _prep_common(q, k, bcol, gcol, grow, mk)
            qd_ref[rows, :] = q * p["e"]
            kt_ref[rows, :] = k * p["ekt"]
            a_ref[rows, :] = _fold_blocks(jnp.where(mk["tril"], p["qk"] * p["decay"], 0.0))
            parts.append((rows, v * bcol, p["kb"] * p["e"], jnp.where(mk["strict"], p["kk"] * p["decay"], 0.0)))
        t_mats = _inv_unit_lower([part[3] for part in parts], mk)
        for (rows, vb, kbe, _), t_mat in zip(parts, t_mats):
            u_ref[rows, :] = _dot(t_mat, vb)
            w_ref[rows, :] = _dot(t_mat, kbe)
            t_ref[rows, :] = _fold_blocks(t_mat)

    row = lambda off: pl.BlockSpec((tb, HEAD), functools.partial(lambda h, m, off: (m, h + off), off=off))
    full = pl.BlockSpec((tb, LANES), lambda h, m: (m, 0))
    o_spec = pl.BlockSpec((tb, HEAD), lambda h, m: (m, h))
    a_spec = pl.BlockSpec((None, tb, CHUNK), lambda h, m: (h, m, 0))
    wide = jax.ShapeDtypeStruct((s_dim, N_HEADS * HEAD), F32)
    folded = jax.ShapeDtypeStruct((N_HEADS, s_dim, CHUNK), F32)
    return pl.pallas_call(
        body,
        out_shape=[wide, wide, wide, wide, folded, folded],
        grid=(N_HEADS, s_dim // tb),
        in_specs=[row(0), row(N_HEADS), row(2 * N_HEADS), full, full, pl.BlockSpec((8, tb), lambda h, m: (0, m))],
        out_specs=[o_spec, o_spec, o_spec, o_spec, a_spec, a_spec],
        compiler_params=pltpu.CompilerParams(dimension_semantics=("parallel", "parallel")),
        name="gdr_prep_fwd",
    )(qkvn, qkvn, qkvn, beta, gc, gc_t)


def _gdr_prep_bwd(qkvn, beta, gc, gc_t, t_fold, u, w, du, dw, dqd, dkt, d_a):
    s_dim = qkvn.shape[0]
    tg = min(GROUP, s_dim)
    n_sub = min(GROUPS_PER_STEP, s_dim // tg)
    tb = tg * n_sub

    def body(q_ref, k_ref, v_ref, b_ref, g_ref, gt_ref, t_ref, u_ref, w_ref, du_ref, dw_ref, dqd_ref, dkt_ref,
             da_ref, dq_ref, dk_ref, dv_ref, db_ref, dg_ref):
        h = pl.program_id(1)

        @pl.when(h == 0)
        def _():
            db_ref[...] = jnp.zeros_like(db_ref)
            dg_ref[...] = jnp.zeros_like(dg_ref)

        mk = _group_masks(tg)
        lane = lax.broadcasted_iota(jnp.int32, (tg, LANES), 1)
        for s in range(n_sub):
            rows = slice(s * tg, (s + 1) * tg)
            q, k, v = q_ref[rows, :], k_ref[rows, :], v_ref[rows, :]
            bcol, gcol, grow = _head_cols(b_ref[rows, :], g_ref[rows, :], gt_ref[:, rows], h)
            p = _prep_common(q, k, bcol, gcol, grow, mk, t_ref[rows, :])
            t_mat, decay, e, ekt, kb = p["t"], p["decay"], p["e"], p["ekt"], p["kb"]
            du_, dw_, dqd_, dkt_ = du_ref[rows, :], dw_ref[rows, :], dqd_ref[rows, :], dkt_ref[rows, :]
            dvb = _dot(t_mat, du_, TN)
            dkbe = _dot(t_mat, dw_, TN)
            d_l = -(_dot(dvb, u_ref[rows, :], NT) + _dot(dkbe, w_ref[rows, :], NT))
            m1 = jnp.where(mk["strict"], d_l, 0.0)
            m2 = _unfold_blocks(da_ref[rows, :], mk["tril"])
            d_kk = m1 * decay
            d_qk = m2 * decay
            d_decay = m1 * p["kk"] + m2 * p["qk"]
            dkb = _dot(d_kk, k) + dkbe * e
            dk = _dot(d_kk, kb, TN) + _dot(d_qk, q, TN) + dkt_ * ekt + dkb * bcol
            dq = _dot(d_qk, k) + dqd_ * e
            d_beta = _rowsum(dkb * k) + _rowsum(dvb * v)
            d_e = _rowsum(dkbe * kb) + _rowsum(dqd_ * q)
            d_ekt = _rowsum(dkt_ * k) * ekt
            d_diff = d_decay * decay
            d_grow = -_colsum(d_diff) + _colsum(jnp.where(mk["last"], jnp.broadcast_to(d_ekt, (tg, tg)), 0.0))
            d_gcol = d_e * e - d_ekt + _rowsum(d_diff)
            d_gcol = d_gcol + _rowsum(jnp.where(mk["eye"], jnp.broadcast_to(d_grow, (tg, tg)), 0.0))
            dq_ref[rows, :] = dq
            dk_ref[rows, :] = dk
            dv_ref[rows, :] = dvb * bcol
            db_ref[rows, :] = jnp.where(lane == h, d_beta, db_ref[rows, :])
            dg_ref[rows, :] = jnp.where(lane == h, d_gcol, dg_ref[rows, :])

    row = lambda off: pl.BlockSpec((tb, HEAD), functools.partial(lambda m, h, off: (m, h + off), off=off))
    full = pl.BlockSpec((tb, LANES), lambda m, h: (m, 0))
    o_spec = pl.BlockSpec((tb, HEAD), lambda m, h: (m, h))
    a_spec = pl.BlockSpec((None, tb, CHUNK), lambda m, h: (h, m, 0))
    wide = jax.ShapeDtypeStruct((s_dim, N_HEADS * HEAD), F32)
    lanes = jax.ShapeDtypeStruct((s_dim, LANES), F32)
    return pl.pallas_call(
        body,
        out_shape=[wide, wide, wide, lanes, lanes],
        grid=(s_dim // tb, N_HEADS),
        in_specs=[row(0), row(N_HEADS), row(2 * N_HEADS), full, full, pl.BlockSpec((8, tb), lambda m, h: (0, m)),
                  a_spec, o_spec, o_spec, o_spec, o_spec, o_spec, o_spec, a_spec],
        out_specs=[o_spec, o_spec, o_spec, full, full],
        compiler_params=pltpu.CompilerParams(dimension_semantics=("parallel", "arbitrary")),
        name="gdr_prep_bwd",
    )(qkvn, qkvn, qkvn, beta, gc, gc_t, t_fold, u, w, du, dw, dqd, dkt, d_a)


def _gdr_scan_fwd(u, w, qd, kt, a_mat, gc):
    s_dim = u.shape[0]
    n_chunks = s_dim // CHUNK
    per = min(SCAN_CHUNKS_PER_STEP, n_chunks)
    tb = per * CHUNK

    def body(u_ref, w_ref, qd_ref, kt_ref, a_ref, g_ref, o_ref, st_ref, state):
        @pl.when(pl.program_id(0) == 0)
        def _():
            state[...] = jnp.zeros_like(state)

        heads = range(N_HEADS)
        cols = [slice(h * HEAD, (h + 1) * HEAD) for h in heads]
        for i in range(per):
            rows = slice(i * CHUNK, (i + 1) * CHUNK)
            egl = jnp.exp(g_ref[(i + 1) * CHUNK - 1:(i + 1) * CHUNK, :])
            s_b = [state[h].astype(BF16) for h in heads]
            for h in heads:
                st_ref[i, h] = state[h]
            ws = [_dot(w_ref[rows, cs], s) for cs, s in zip(cols, s_b)]
            qs = [_dot(qd_ref[rows, cs], s) for cs, s in zip(cols, s_b)]
            vns = [(u_ref[rows, cs] - ws_h).astype(BF16) for cs, ws_h in zip(cols, ws)]
            avs = [_dot(a_ref[h, rows, :], vn) for h, vn in zip(heads, vns)]
            kvs = [_dot(kt_ref[rows, cs], vn, TN) for cs, vn in zip(cols, vns)]
            for h, cs in zip(heads, cols):
                o_ref[rows, cs] = qs[h] + avs[h]
                state[h] = state[h] * egl[:, h:h + 1] + kvs[h]

    wide = pl.BlockSpec((tb, N_HEADS * HEAD), lambda n: (n, 0))
    return pl.pallas_call(
        body,
        out_shape=[jax.ShapeDtypeStruct((s_dim, N_HEADS * HEAD), F32),
                   jax.ShapeDtypeStruct((n_chunks, N_HEADS, HEAD, HEAD), F32)],
        grid=(n_chunks // per,),
        in_specs=[wide, wide, wide, wide, pl.BlockSpec((N_HEADS, tb, CHUNK), lambda n: (0, n, 0)),
                  pl.BlockSpec((tb, LANES), lambda n: (n, 0))],
        out_specs=[wide, pl.BlockSpec((per, N_HEADS, HEAD, HEAD), lambda n: (n, 0, 0, 0))],
        scratch_shapes=[pltpu.VMEM((N_HEADS, HEAD, HEAD), F32)],
        compiler_params=pltpu.CompilerParams(dimension_semantics=("arbitrary",)),
        name="gdr_scan_fwd",
    )(u, w, qd, kt, a_mat, gc)


def _gdr_scan_bwd(u, w, qd, kt, a_mat, gc, states, d_o):
    s_dim = u.shape[0]
    n_chunks = s_dim // CHUNK
    per = min(SCAN_CHUNKS_PER_STEP, n_chunks)
    tb = per * CHUNK
    last = n_chunks // per - 1

    def body(u_ref, w_ref, qd_ref, kt_ref, a_ref, g_ref, st_ref, do_ref,
             du_ref, dw_ref, dqd_ref, dkt_ref, da_ref, de_ref, d_state):
        @pl.when(pl.program_id(0) == 0)
        def _():
            d_state[...] = jnp.zeros_like(d_state)

        heads = range(N_HEADS)
        cols = [slice(h * HEAD, (h + 1) * HEAD) for h in heads]
        for i in reversed(range(per)):
            rows = slice(i * CHUNK, (i + 1) * CHUNK)
            egl = jnp.exp(g_ref[(i + 1) * CHUNK - 1:(i + 1) * CHUNK, :])
            s_b = [st_ref[i, h].astype(BF16) for h in heads]
            ds_b = [d_state[h].astype(BF16) for h in heads]
            dos = [do_ref[rows, cs].astype(BF16) for cs in cols]
            w_b = [w_ref[rows, cs].astype(BF16) for cs in cols]
            ws = [_dot(w_h, s) for w_h, s in zip(w_b, s_b)]
            ados = [_dot(a_ref[h, rows, :], do, TN) for h, do in zip(heads, dos)]
            kds = [_dot(kt_ref[rows, cs], ds) for cs, ds in zip(cols, ds_b)]
            dqds = [_dot(do, s, NT) for do, s in zip(dos, s_b)]
            qdos = [_dot(qd_ref[rows, cs], do, TN) for cs, do in zip(cols, dos)]
            vns = [(u_ref[rows, cs] - ws_h).astype(BF16) for cs, ws_h in zip(cols, ws)]
            dvns = [a + k_ for a, k_ in zip(ados, kds)]
            dvn_b = [d.astype(BF16) for d in dvns]
            das = [_dot(do, vn, NT) for do, vn in zip(dos, vns)]
            dkts = [_dot(vn, ds, NT) for vn, ds in zip(vns, ds_b)]
            dws = [_dot(d, s, NT) for d, s in zip(dvn_b, s_b)]
            wds = [_dot(w_h, d, TN) for w_h, d in zip(w_b, dvn_b)]
            for h, cs in zip(heads, cols):
                ds_n = d_state[h]
                de = jnp.sum(_rowsum(ds_n * st_ref[i, h]), axis=0, keepdims=True)
                de_ref[i, h:h + 1, :] = jnp.broadcast_to(de, (1, LANES))
                dqd_ref[rows, cs] = dqds[h]
                da_ref[h, rows, :] = das[h]
                dkt_ref[rows, cs] = dkts[h]
                du_ref[rows, cs] = dvns[h]
                dw_ref[rows, cs] = -dws[h]
                d_state[h] = ds_n * egl[:, h:h + 1] + qdos[h] - wds[h]

    wide = pl.BlockSpec((tb, N_HEADS * HEAD), lambda n: (last - n, 0))
    a_spec = pl.BlockSpec((N_HEADS, tb, CHUNK), lambda n: (0, last - n, 0))
    wide_shape = jax.ShapeDtypeStruct((s_dim, N_HEADS * HEAD), F32)
    return pl.pallas_call(
        body,
        out_shape=[wide_shape, wide_shape, wide_shape, wide_shape,
                   jax.ShapeDtypeStruct((N_HEADS, s_dim, CHUNK), F32),
                   jax.ShapeDtypeStruct((n_chunks, N_HEADS, LANES), F32)],
        grid=(n_chunks // per,),
        in_specs=[wide, wide, wide, wide, a_spec, pl.BlockSpec((tb, LANES), lambda n: (last - n, 0)),
                  pl.BlockSpec((per, N_HEADS, HEAD, HEAD), lambda n: (last - n, 0, 0, 0)), wide],
        out_specs=[wide, wide, wide, wide, a_spec, pl.BlockSpec((per, N_HEADS, LANES), lambda n: (last - n, 0, 0))],
        scratch_shapes=[pltpu.VMEM((N_HEADS, HEAD, HEAD), F32)],
        compiler_params=pltpu.CompilerParams(dimension_semantics=("arbitrary",)),
        name="gdr_scan_bwd",
    )(u, w, qd, kt, a_mat, gc, states, d_o)


FUSED_ROWS = 512


def _gdr_out_fwd(o_dn, proj_a, dn_w, w_br):
    def fn(r, c):
        o, z = r
        w_, w_br_ = c
        outs = []
        for h in range(N_HEADS):
            cs = slice(h * HEAD, (h + 1) * HEAD)
            oh, zh = o[:, cs], z[:, cs]
            rr = lax.rsqrt(_rowmean(oh * oh) + EPS_RMS)
            outs.append(oh * rr * w_ * (zh * _sig(zh)))
        og = jnp.concatenate(outs, axis=1).astype(BF16)
        return [og, _dot(og, w_br_)], []

    return _rowwise(fn, [o_dn, (proj_a, 3, D_MODEL)], [dn_w, w_br], [(D_MODEL, BF16), (D_MODEL, BF16)],
                    tm=FUSED_ROWS, name="gdr_out_fwd")


def _gdr_out_bwd(o_dn, proj_a, d_y_dn, dn_w, w_br):
    def fn(r, c):
        o, z, dy = r
        w_, w_br_ = c
        dg = _dot(dy, w_br_, NT)
        d_o, d_z = [], []
        d_w = jnp.zeros((1, HEAD), F32)
        for h in range(N_HEADS):
            cs = slice(h * HEAD, (h + 1) * HEAD)
            oh, zh, dgh = o[:, cs], z[:, cs], dg[:, cs]
            rr = lax.rsqrt(_rowmean(oh * oh) + EPS_RMS)
            sz = zh * _sig(zh)
            d_n = dgh * sz
            d_z.append(dgh * (oh * rr * w_) * _silu_grad(zh))
            d_w = d_w + _colsum(d_n * oh * rr)
            gw = d_n * w_
            d_o.append(rr * gw - oh * (rr * rr * rr) * _rowmean(gw * oh))
        return [jnp.concatenate(d_o, axis=1), jnp.concatenate(d_z, axis=1)], [d_w]

    return _rowwise(fn, [o_dn, (proj_a, 3, D_MODEL), d_y_dn], [dn_w, w_br], [(D_MODEL, F32), (D_MODEL, BF16)],
                    accs=[(1, HEAD)], tm=FUSED_ROWS, name="gdr_out_bwd")


def _rms_fwd(x, w):
    r = lax.rsqrt(_rowmean(x * x) + EPS_RMS)
    return x * r * w


def _rms_bwd(x, w, dy):
    r = lax.rsqrt(_rowmean(x * x) + EPS_RMS)
    gw = dy * w
    return r * gw - x * (r * r * r) * _rowmean(gw * x), _colsum(dy * x * r)


def _rope_consts():
    inv = ROPE_BASE ** (-np.arange(0, ROPE, 2, dtype=np.float32) / ROPE)
    t = np.zeros((4, LANES), np.float32)
    t[0, :32] = inv
    t[0, 32:64] = inv
    t[1, :64] = 1.0
    t[2, 32:64] = 1.0
    t[3, :32] = -1.0
    return jnp.asarray(t)


def _rope_tables(pos, consts, width):
    ang = pos * consts[0:1, :]
    cosv, sinv = jnp.cos(ang), jnp.sin(ang)
    reps = width // LANES
    tile = (lambda t: jnp.concatenate([t] * reps, axis=1)) if reps > 1 else (lambda t: t)
    return tile(cosv * consts[1:2, :]), tile(sinv * consts[2:3, :]), tile(sinv * consts[3:4, :])


def _rope_apply(t, tabs):
    cos_t, sin_a, sin_b = tabs
    width = t.shape[1]
    return t * cos_t + pltpu.roll(t, 32, 1) * sin_a + pltpu.roll(t, width - 32, 1) * sin_b


def _rope_transpose(d, tabs):
    cos_t, sin_a, sin_b = tabs
    width = d.shape[1]
    return d * cos_t + pltpu.roll(d * sin_a, width - 32, 1) + pltpu.roll(d * sin_b, 32, 1)


QK_HEAD = 2 * HEAD


def _interleave_heads(a, b):
    parts = []
    for h in range(N_HEADS):
        parts.append(a[:, h * HEAD:(h + 1) * HEAD])
        parts.append(b if b.shape[1] == LANES else b[:, h * LANES:(h + 1) * LANES])
    return jnp.concatenate(parts, axis=1)


def _mla_rows(proj_b):
    return [(proj_b, WB_CQ // Q_LORA, Q_LORA), (proj_b, WB_CKV // KV_LORA, KV_LORA), (proj_b, WB_KR // LANES, LANES)]


def _mla_prep_fwd(proj_b, pos, qn_w, kvn_w, uq, uk, uv):
    def fn(r, c):
        cq, ckv, kr, pos_ = r
        qn_w_, kvn_w_, uq_, uk_, uv_, rope = c
        c_q = _rms_fwd(cq, qn_w_).astype(BF16)
        c_kv = _rms_fwd(ckv, kvn_w_).astype(BF16)
        qf = _dot(c_q, uq_)
        qr = _rope_apply(qf[:, D_MODEL:], _rope_tables(pos_, rope, D_MODEL))
        kr = _rope_apply(kr, _rope_tables(pos_, rope, LANES))
        kc = _interleave_heads(_dot(c_kv, uk_), kr)
        v = _dot(c_kv, uv_)
        return [c_q, c_kv, _interleave_heads(qf[:, :D_MODEL], qr) * SCALE, kc, v, kc, v], []

    wide2 = N_HEADS * QK_HEAD
    return _rowwise(fn, _mla_rows(proj_b) + [pos], [qn_w, kvn_w, uq, uk, uv, _rope_consts()],
                    [(Q_LORA, BF16), (KV_LORA, BF16), (wide2, BF16), (wide2, BF16), (D_MODEL, BF16),
                     (wide2, BF16, "T"), (D_MODEL, BF16, "T")], tm=FUSED_ROWS, name="mla_prep_fwd")


def _mla_prep_bwd(proj_b, pos, d_qc, d_kc, d_v, qn_w, kvn_w, uq, uk, uv):
    def fn(r, c):
        cq, ckv, _, pos_, dq, dk, dv = r
        qn_w_, kvn_w_, uq_, uk_, uv_, rope = c
        even = lambda t: jnp.concatenate([t[:, (2 * h) * LANES:(2 * h + 1) * LANES] for h in range(N_HEADS)], axis=1)
        odd = lambda t: jnp.concatenate([t[:, (2 * h + 1) * LANES:(2 * h + 2) * LANES] for h in range(N_HEADS)], axis=1)
        d_qr_raw = _rope_transpose(odd(dq), _rope_tables(pos_, rope, D_MODEL)) * SCALE
        d_qf = jnp.concatenate([even(dq) * SCALE, d_qr_raw], axis=1).astype(BF16)
        d_kn = even(dk).astype(BF16)
        dkr = dk[:, LANES:2 * LANES]
        for h in range(1, N_HEADS):
            dkr = dkr + dk[:, (2 * h + 1) * LANES:(2 * h + 2) * LANES]
        d_cq, d_qnw = _rms_bwd(cq, qn_w_, _dot(d_qf, uq_, NT))
        d_ckv, d_kvnw = _rms_bwd(ckv, kvn_w_, _dot(d_kn, uk_, NT) + _dot(dv, uv_, NT))
        return [d_qf, d_kn, d_cq, d_ckv, _rope_transpose(dkr, _rope_tables(pos_, rope, LANES))], [d_qnw, d_kvnw]

    return _rowwise(fn, _mla_rows(proj_b) + [pos, d_qc, d_kc, d_v], [qn_w, kvn_w, uq, uk, uv, _rope_consts()],
                    [(2 * D_MODEL, BF16), (D_MODEL, BF16), (Q_LORA, BF16), (KV_LORA, BF16), (LANES, BF16)],
                    accs=[(1, Q_LORA), (1, KV_LORA)], tm=FUSED_ROWS, name="mla_prep_bwd")


def _causal_mask_t(st, key0, query0):
    key = lax.broadcasted_iota(jnp.int32, st.shape, 0) + key0
    query = lax.broadcasted_iota(jnp.int32, st.shape, 1) + query0
    return jnp.where(key <= query, st, NEG_BIG)


def _attn_tiles(s_dim):
    tq = min(512, s_dim)
    n_chains = 2 if s_dim >= 2 * tq else 1
    return tq, n_chains, min(512, s_dim)


def _diagonal_chains(t, tq, n_chains, tk):
    return [(c, (t + 1) * tk - 1 > c * tq) for c in range(n_chains) if t * tk < (c + 1) * tq]


def _attn_fwd(qc, kc, vt):
    s_dim = qc.shape[0]
    tq, n_chains, tk = _attn_tiles(s_dim)
    tqs = tq * n_chains

    def body(q_ref, k_ref, vt_ref, o_ref, lse_ref, m_s, l_s, acc):
        qi = pl.program_id(1)
        m_s[...] = jnp.full_like(m_s, NEG_BIG)
        l_s[...] = jnp.zeros_like(l_s)
        acc[...] = jnp.zeros_like(acc)

        def make_step(chains):
            def step(j, carry):
                ks = pl.multiple_of(j * tk, tk)
                kb, vtb = k_ref[pl.ds(ks, tk), :], vt_ref[:, pl.ds(ks, tk)]
                cols = [slice(c * tq, (c + 1) * tq) for c, _ in chains]
                sts = [_dot(kb, q_ref[cs, :], NT) for cs in cols]
                sts = [_causal_mask_t(st, j * tk, qi * tqs + c * tq) if masked else st
                       for st, (c, masked) in zip(sts, chains)]
                m_prevs = [m_s[:, cs] for cs in cols]
                m_news = [jnp.maximum(mp, jnp.max(st, axis=0, keepdims=True)) for mp, st in zip(m_prevs, sts)]
                alphas = [jnp.exp(mp - mn) for mp, mn in zip(m_prevs, m_news)]
                pts = [jnp.exp(st - mn) for st, mn in zip(sts, m_news)]
                pvs = [_dot(vtb, pt) for pt in pts]
                for cs, mn, al, pt, pv in zip(cols, m_news, alphas, pts, pvs):
                    l_s[:, cs] = al * l_s[:, cs] + _colsum(pt)
                    m_s[:, cs] = mn
                    acc[:, cs] = acc[:, cs] * al + pv
                return carry
            return step

        below = qi * (tqs // tk)
        lax.fori_loop(0, below, make_step([(c, False) for c in range(n_chains)]), 0)
        for t in range(tqs // tk):
            make_step(_diagonal_chains(t, tq, n_chains, tk))(below + t, 0)
        l = l_s[...]
        o_ref[...] = jnp.transpose(acc[...] / l)
        lse_ref[...] = m_s[...] + jnp.log(l)

    return pl.pallas_call(
        body,
        out_shape=[jax.ShapeDtypeStruct((s_dim, N_HEADS * HEAD), F32), jax.ShapeDtypeStruct((N_HEADS, 1, s_dim), F32)],
        grid=(N_HEADS, s_dim // tqs),
        in_specs=[pl.BlockSpec((tqs, QK_HEAD), lambda h, qi: (qi, h)),
                  pl.BlockSpec((s_dim, QK_HEAD), lambda h, qi: (0, h)),
                  pl.BlockSpec((HEAD, s_dim), lambda h, qi: (h, 0))],
        out_specs=[pl.BlockSpec((tqs, HEAD), lambda h, qi: (qi, h)),
                   pl.BlockSpec((None, 1, tqs), lambda h, qi: (h, 0, qi))],
        scratch_shapes=[pltpu.VMEM((1, tqs), F32), pltpu.VMEM((1, tqs), F32), pltpu.VMEM((HEAD, tqs), F32)],
        compiler_params=pltpu.CompilerParams(dimension_semantics=("parallel", "parallel")),
        name="attn_fwd",
    )(qc, kc, vt)


def _attn_bwd(qc, kc, kct, v, o, d_o, lse):
    s_dim = qc.shape[0]
    tq, n_chains, tk = _attn_tiles(s_dim)
    tqs = tq * n_chains

    def body(q_ref, k_ref, kt_ref, v_ref, o_ref, do_ref, lse_ref, dq_ref, dk_ref, dv_ref, dqt_acc, dv_acc):
        qi = pl.program_id(1)

        @pl.when(qi == 0)
        def _():
            dk_ref[...] = jnp.zeros_like(dk_ref)
            dv_acc[...] = jnp.zeros_like(dv_acc)

        dqt_acc[...] = jnp.zeros_like(dqt_acc)
        do_f = do_ref[...]
        do_all = do_f.astype(BF16)
        q_all = q_ref[...]
        lse_row = lse_ref[...]
        delta_row = _dot3(jnp.ones((8, HEAD), F32), o_ref[...] * do_f, NT)[0:1, :]

        def make_step(chains):
            rows = slice(chains[0][0] * tq, (chains[-1][0] + 1) * tq)

            def step(j, carry):
                ks = pl.multiple_of(j * tk, tk)
                kb, vb, ktb = k_ref[pl.ds(ks, tk), :], v_ref[pl.ds(ks, tk), :], kt_ref[:, pl.ds(ks, tk)]
                cols = [slice(c * tq, (c + 1) * tq) for c, _ in chains]
                sts = [_dot(kb, q_all[cs, :], NT) for cs in cols]
                sts = [_causal_mask_t(st, j * tk, qi * tqs + c * tq) if masked else st
                       for st, (c, masked) in zip(sts, chains)]
                dpts = [_dot(vb, do_all[cs, :], NT) for cs in cols]
                pts = [jnp.exp(st - lse_row[:, cs]) for st, cs in zip(sts, cols)]
                dsts = [(pt * (dpt - delta_row[:, cs])).astype(BF16) for pt, dpt, cs in zip(pts, dpts, cols)]
                pts = [pt.astype(BF16) for pt in pts]
                dqs = [_dot(ktb, dst) for dst in dsts]
                for cs, dq in zip(cols, dqs):
                    dqt_acc[:, cs] += dq
                pt_all = jnp.concatenate(pts, axis=1) if len(chains) > 1 else pts[0]
                dst_all = jnp.concatenate(dsts, axis=1) if len(chains) > 1 else dsts[0]
                dk_ref[pl.ds(ks, tk), :] += _dot(dst_all, q_all[rows, :])
                dv_acc[pl.ds(ks, tk), :] += _dot(pt_all, do_all[rows, :])
                return carry
            return step

        below = qi * (tqs // tk)
        lax.fori_loop(0, below, make_step([(c, False) for c in range(n_chains)]), 0)
        for t in range(tqs // tk):
            make_step(_diagonal_chains(t, tq, n_chains, tk))(below + t, 0)
        dq_ref[...] = jnp.transpose(dqt_acc[...])

        @pl.when(qi == s_dim // tqs - 1)
        def _():
            dv_ref[...] = dv_acc[...].astype(dv_ref.dtype)

    q_spec = pl.BlockSpec((tqs, QK_HEAD), lambda h, qi: (qi, h))
    o_spec = pl.BlockSpec((tqs, HEAD), lambda h, qi: (qi, h))
    k_spec = pl.BlockSpec((s_dim, QK_HEAD), lambda h, qi: (0, h))
    v_spec = pl.BlockSpec((s_dim, HEAD), lambda h, qi: (0, h))
    wide2 = jax.ShapeDtypeStruct((s_dim, N_HEADS * QK_HEAD), F32)
    return pl.pallas_call(
        body,
        out_shape=[wide2, wide2, jax.ShapeDtypeStruct((s_dim, N_HEADS * HEAD), BF16)],
        grid=(N_HEADS, s_dim // tqs),
        in_specs=[q_spec, k_spec, pl.BlockSpec((QK_HEAD, s_dim), lambda h, qi: (h, 0)), v_spec, o_spec, o_spec,
                  pl.BlockSpec((None, 1, tqs), lambda h, qi: (h, 0, qi))],
        out_specs=[q_spec, k_spec, v_spec],
        scratch_shapes=[pltpu.VMEM((QK_HEAD, tqs), F32), pltpu.VMEM((s_dim, HEAD), F32)],
        compiler_params=pltpu.CompilerParams(dimension_semantics=("parallel", "arbitrary")),
        name="attn_bwd",
    )(qc, kc, kct, v, o, d_o, lse)


def _mix_proj_ln1(y_dn, y_mla, proj_g, x, w_o, g, b):
    s_dim = x.shape[0]
    tm = min(512, s_dim)

    def body(yd_ref, ym_ref, g_ref, x_ref, w_ref, lg_ref, lb_ref, mixed_ref, a1_ref, h1_ref, h1b_ref):
        gates = g_ref[...].astype(F32)
        mixed = (_sig(gates[:, :D_MODEL]) * yd_ref[...].astype(F32)
                 + _sig(gates[:, D_MODEL:]) * ym_ref[...].astype(F32)).astype(BF16)
        a1 = _dot(mixed, w_ref[...])
        xh, _ = _ln_stats(ALPHA * x_ref[...] + a1)
        y = xh * lg_ref[...] + lb_ref[...]
        mixed_ref[...] = mixed
        a1_ref[...] = a1
        h1_ref[...] = y
        h1b_ref[...] = y.astype(BF16)

    row = lambda width: pl.BlockSpec((tm, width), lambda i: (i, 0))
    whole = lambda a: pl.BlockSpec(a.shape, lambda i: (0, 0))
    sds = lambda dt: jax.ShapeDtypeStruct((s_dim, D_MODEL), dt)
    return pl.pallas_call(
        body,
        out_shape=[sds(BF16), sds(F32), sds(F32), sds(BF16)],
        grid=(s_dim // tm,),
        in_specs=[row(D_MODEL), row(D_MODEL), row(2 * D_MODEL), row(D_MODEL), whole(w_o), whole(g), whole(b)],
        out_specs=[row(D_MODEL)] * 4,
        compiler_params=pltpu.CompilerParams(dimension_semantics=("parallel",)),
        name="mix_proj_ln1",
    )(y_dn, y_mla, proj_g, x, w_o, g, b)


def _ln1_mix_bwd(x, a1, d_h1, d_pg, y_dn, y_mla, proj_g, g, w_o, w_pg):
    def fn(r, c):
        x_, a1_, dy, dpg, yd, ym, gates = r
        g_, w_o_, w_pg_ = c
        dy = dy + _dot(dpg, w_pg_, NT)
        xh, rr = _ln_stats(ALPHA * x_ + a1_)
        dz = _ln_bwd(dy, xh, rr, g_)
        dz_b = dz.astype(BF16)
        dm = _dot(dz_b, w_o_, NT)
        sd, sm = _sig(gates[:, :D_MODEL]), _sig(gates[:, D_MODEL:])
        d_g = jnp.concatenate([dm * yd * sd * (1.0 - sd), dm * ym * sm * (1.0 - sm)], axis=1)
        return [dz_b, ALPHA * dz, d_g, dm * sd, dm * sm], [_colsum(dy * xh), _colsum(dy)]

    return _rowwise(fn, [x, a1, d_h1, d_pg, y_dn, y_mla, proj_g], [g, w_o, w_pg],
                    [(D_MODEL, BF16), (D_MODEL, F32), (2 * D_MODEL, BF16), (D_MODEL, BF16), (D_MODEL, BF16)],
                    accs=[(1, D_MODEL), (1, D_MODEL)], tm=FUSED_ROWS, name="ln1_mix_bwd")


def _ln_stats(z):
    mu = _rowmean(z)
    zc = z - mu
    r = lax.rsqrt(_rowmean(zc * zc) + EPS_LN)
    return zc * r, r


def _ln_bwd(dy, xh, r, g):
    dxh = dy * g
    return r * (dxh - _rowmean(dxh) - xh * _rowmean(dxh * xh))


def _ffn_in_act(h1b, w_t):
    s_dim, k_dim = h1b.shape
    hidden = w_t.shape[0] // 2
    tm, tn = min(512, s_dim), _pick_wide(hidden)
    nt = hidden // tn

    def body(a_ref, bg_ref, bu_ref, gt_ref, up_ref, act_ref):
        a = a_ref[...]
        gt, up = _dot(a, bg_ref[...], NT), _dot(a, bu_ref[...], NT)
        gt_ref[...] = gt.astype(BF16)
        up_ref[...] = up.astype(BF16)
        act_ref[...] = (gt * _sig(gt) * up).astype(BF16)

    o_spec = pl.BlockSpec((tm, tn), lambda j, i: (i, j))
    sds = jax.ShapeDtypeStruct((s_dim, hidden), BF16)
    return pl.pallas_call(
        body,
        out_shape=[sds, sds, sds],
        grid=(nt, s_dim // tm),
        in_specs=[pl.BlockSpec((tm, k_dim), lambda j, i: (i, 0)), pl.BlockSpec((tn, k_dim), lambda j, i: (j, 0)),
                  pl.BlockSpec((tn, k_dim), lambda j, i: (j + nt, 0))],
        out_specs=[o_spec, o_spec, o_spec],
        compiler_params=pltpu.CompilerParams(dimension_semantics=("parallel", "parallel")),
        name="ffn_in_act",
    )(h1b, w_t, w_t)


def _act_bwd(gt, up, d_act):
    def fn(r, c):
        gt_, up_, da = r
        return [jnp.concatenate([da * up_ * _silu_grad(gt_), da * gt_ * _sig(gt_)], axis=1)], []

    return _rowwise(fn, [gt, up, d_act], [], [(2 * FFN_HIDDEN, BF16)], name="act_bwd")[0]


def _tail(h1, ffn, p, tgt, g, b, w_pg, w_ple_t):
    def fn(r, c):
        h1_, ffn_, p_, t_ = r
        pg_ = _dot(h1_, c[2])
        pp_ = _dot(p_, c[3], NT)
        sp = _sig(pg_)
        xh, rr = _ln_stats(ALPHA * h1_ + ffn_ + sp * pp_)
        y = xh * c[0] + c[1]
        err = y - t_
        dy = err * (1.0 / D_MODEL)
        dz = _ln_bwd(dy, xh, rr, c[0])
        loss = jnp.sum(0.5 * _rowmean(err * err), axis=0, keepdims=True)
        return ([dz, dz * pp_ * sp * (1.0 - sp), dz * sp, ALPHA * dz],
                [_colsum(dy * xh), _colsum(dy), jnp.broadcast_to(loss, (1, LANES))])

    return _rowwise(fn, [h1, ffn, p, tgt], [g, b, w_pg, w_ple_t], [(D_MODEL, BF16)] * 3 + [(D_MODEL, F32)],
                    accs=[(1, D_MODEL), (1, D_MODEL), (1, LANES)], tm=FUSED_ROWS, name="tail")


def _local_step(x, p, pos, tgt, w, late_weights, emit):
    w = dict(w)
    s_dim = x.shape[0]
    xb, pb = x.astype(BF16), p.astype(BF16)
    proj_a = _mm(xb, w["wa_t"], tb=True, name="f_proj_a")
    proj_g = _mm(xb, w["wg_t"], tb=True, out_dtype=BF16, name="f_proj_g")
    proj_b = _mm(xb, w["wb_t"], tb=True, name="f_proj_b")
    qkvn = _conv_fwd(proj_a, w["conv"])
    beta, gc = _gates_fwd(proj_b, w["alog"], w["dtb"])
    gc_t = jnp.transpose(gc[:, :N_HEADS])
    u, w_, qd, kt, a_mat, t_fold = _gdr_prep_fwd(qkvn, beta, gc, gc_t)
    o_dn, states = _gdr_scan_fwd(u, w_, qd, kt, a_mat, gc)
    w.update(late_weights("mix", o_dn))
    og, y_dn = _gdr_out_fwd(o_dn, proj_a, w["dnw"], w["br_dn"])
    c_q, c_kv, qc, kc, vv, kct, vt = _mla_prep_fwd(proj_b, pos, w["qnw"], w["kvnw"], w["uq"], w["uk"], w["uv"])
    o_mla, lse = _attn_fwd(qc, kc, vt)
    y_mla = _mm(o_mla, w["br_mla"], out_dtype=BF16, name="f_y_mla")
    mixed, a1, h1, h1b = _mix_proj_ln1(y_dn, y_mla, proj_g, x, w["wo"], w["ln1g"], w["ln1b"])
    w.update(late_weights("ffn", a1))
    gt, up, act = _ffn_in_act(h1b, w["ffn_in_t"])
    ffn = _mm(act, w["ffn_out"], name="f_ffn")
    g = {}
    dz2, d_pg, d_pp, dh1a, g["ln2g"], g["ln2b"], loss = _tail(h1, ffn, pb, tgt, w["ln2g"], w["ln2b"],
                                                            w["ple_gate"], w["ple_t"])
    g["ple_t"] = _mm(d_pp, pb, ta=True, out_dtype=BF16, name="b_w_ple")
    g["ple_gate"] = _mm(h1b, d_pg, ta=True, out_dtype=BF16, name="b_w_ple_gate")
    g["ffn_out"] = _mm(act, dz2, ta=True, out_dtype=BF16, name="b_w_ffn_out")
    d_act = _mm(dz2, w["ffn_out"], tb=True, out_dtype=BF16, name="b_act")
    d_gu = _act_bwd(gt, up, d_act)
    g["ffn_in_t"] = _mm(d_gu, h1b, ta=True, out_dtype=BF16, name="b_w_ffn_in")
    d_gu = emit("ffn", g, d_gu)
    d_h1 = _mm(d_gu, w["ffn_in_t"], add=(dh1a,), name="b_h1_ffn")
    dz1, dxa, d_proj_g, d_y_dn, d_y_mla, g["ln1g"], g["ln1b"] = _ln1_mix_bwd(
        x, a1, d_h1, d_pg, y_dn, y_mla, proj_g, w["ln1g"], w["wo"], w["ple_gate"])
    g["wo"] = _mm(mixed, dz1, ta=True, out_dtype=BF16, name="b_w_o")
    g["br_mla"] = _mm(o_mla, d_y_mla, ta=True, out_dtype=BF16, name="b_w_br_mla")
    d_o_mla = _mm(d_y_mla, w["br_mla"], tb=True, out_dtype=BF16, name="b_o_mla")
    d_qc, d_kc, d_v = _attn_bwd(qc, kc, kct, vv, o_mla, d_o_mla, lse)
    d_q_full, d_kn, d_cq, d_ckv, d_kr, g["qnw"], g["kvnw"] = _mla_prep_bwd(
        proj_b, pos, d_qc, d_kc, d_v, w["qnw"], w["kvnw"], w["uq"], w["uk"], w["uv"])
    g["uq"] = _mm(c_q, d_q_full, ta=True, out_dtype=BF16, name="b_w_uq")
    g["uk"] = _mm(c_kv, d_kn, ta=True, out_dtype=BF16, name="b_w_uk")
    g["uv"] = _mm(c_kv, d_v, ta=True, out_dtype=BF16, name="b_w_uv")
    g["br_dn"] = _mm(og, d_y_dn, ta=True, out_dtype=BF16, name="b_w_br_dn")
    d_y_dn = emit("mix", g, d_y_dn)
    d_o_dn, d_z, g["dnw"] = _gdr_out_bwd(o_dn, proj_a, d_y_dn, w["dnw"], w["br_dn"])
    du, dw, dqd, dkt, d_a, d_egl = _gdr_scan_bwd(u, w_, qd, kt, a_mat, gc, states, d_o_dn)
    dq, dk, dv, d_beta, d_gc = _gdr_prep_bwd(qkvn, beta, gc, gc_t, t_fold, u, w_, du, dw, dqd, dkt, d_a)
    d_egl_rows = jnp.pad(d_egl[:, None, :, 0], ((0, 0), (CHUNK - 1, 0), (0, LANES - N_HEADS))).reshape(s_dim, LANES)
    d_ba, g["alog"], g["dtb"] = _gates_bwd(proj_b, w["alog"], w["dtb"], gc, d_beta, d_gc, d_egl_rows)
    d_qkv, g["conv"] = _conv_bwd(proj_a, w["conv"], dq, dk, dv)
    zeros = jnp.zeros((s_dim, WB_CKV - Q_LORA), BF16)
    d_proj_b = jnp.concatenate([d_cq, zeros, d_ckv, d_kr, d_ba], axis=1)
    g["wa_qkv_t"] = _mm(d_qkv, xb, ta=True, name="b_w_qkv")
    g["wa_z_t"] = _mm(d_z, xb, ta=True, name="b_w_z")
    g["wg_t"] = _mm(d_proj_g, xb, ta=True, name="b_w_g")
    g["wb_t"] = _mm(d_proj_b, xb, ta=True, name="b_w_b")
    d_qkv = emit("w_in", g, d_qkv)
    dx = _mm(d_qkv, w["wa_qkv_t"], add=(dxa,), name="b_x_qkv")
    dx = _mm(d_z, w["wa_z_t"], add=(dx,), name="b_x_z")
    dx = _mm(d_proj_g, w["wg_t"], add=(dx,), name="b_x_g")
    dx = _mm(d_proj_b, w["wb_t"], add=(dx,), name="b_x_b")
    return loss, dx, g


_BIG = (("w_in", 1), ("w_uq", 0), ("w_uk", 0), ("w_uv", 0), ("w_br_dn", 0), ("w_br_mla", 0),
        ("w_o", 0), ("w_ffn_in", 1), ("w_ffn_out", 0), ("w_ple", 1), ("w_ple_gate", 0))
_BIG_AXIS = dict(_BIG)
_SMALL = ("ln1_g", "ln1_b", "ln2_g", "ln2_b", "q_norm_w", "kv_norm_w", "dn_norm_w", "dn_a_log", "dn_dt_bias")
_ORDER = ("w_in", "conv_w", "dn_a_log", "dn_dt_bias", "dn_norm_w", "q_norm_w", "w_uq", "kv_norm_w", "w_uk", "w_uv",
          "w_br_dn", "w_br_mla", "w_o", "ln1_g", "ln1_b", "w_ffn_in", "w_ffn_out", "w_ple", "w_ple_gate", "ln2_g",
          "ln2_b")


def _stored_shape(name, shard_shape):
    axis = _BIG_AXIS[name]
    lead = shard_shape[axis]
    return lead, int(np.prod(shard_shape)) // lead


def _to_stored(name, shard):
    return jnp.moveaxis(shard, _BIG_AXIS[name], 0).reshape(_stored_shape(name, shard.shape))


def _from_stored(name, stored, shard_shape):
    axis = _BIG_AXIS[name]
    moved = (shard_shape[axis],) + shard_shape[:axis] + shard_shape[axis + 1:]
    return jnp.moveaxis(stored.reshape(moved), 0, axis)


_W_IN_ROWS = np.cumsum([0, 3072, 1024, 8, 8, Q_LORA, KV_LORA, ROPE, D_MODEL, D_MODEL])


def _first_weights(w_in_t, conv_full, small):
    r = _W_IN_ROWS
    zr = lambda n: jnp.zeros((n, D_MODEL), w_in_t.dtype)
    w = {}
    w["wa_t"] = w_in_t[r[0]:r[2]]
    w["wa_qkv_t"], w["wa_z_t"] = w_in_t[r[0]:r[1]], w_in_t[r[1]:r[2]]
    w["wg_t"] = w_in_t[r[7]:r[9]]
    w["wb_t"] = jnp.concatenate([w_in_t[r[4]:r[5]], zr(WB_CKV - Q_LORA), w_in_t[r[5]:r[7]], zr(LANES - ROPE),
                                 w_in_t[r[2]:r[4]], zr(LANES - 2 * N_HEADS)], axis=0)
    w["conv"] = conv_full
    pad_l = lambda v: jnp.pad(v, ((0, 0), (0, LANES - v.shape[1])))
    w["alog"], w["dtb"] = pad_l(small["dn_a_log"]), pad_l(small["dn_dt_bias"])
    w["dnw"], w["qnw"], w["kvnw"] = small["dn_norm_w"], small["q_norm_w"], small["kv_norm_w"]
    w["ln1g"], w["ln1b"], w["ln2g"], w["ln2b"] = small["ln1_g"], small["ln1_b"], small["ln2_g"], small["ln2_b"]
    return w


def _late_weights(group, fw):
    w = {}
    if group == "mix":
        uq = fw["w_uq"].reshape(Q_LORA, N_HEADS, HEAD + ROPE)
        uq_r = jnp.pad(uq[:, :, HEAD:], ((0, 0), (0, 0), (0, HEAD - ROPE)))
        w["uq"] = jnp.concatenate([uq[:, :, :HEAD].reshape(Q_LORA, -1), uq_r.reshape(Q_LORA, -1)], axis=1)
        w["uk"], w["uv"] = fw["w_uk"], fw["w_uv"]
        w["br_dn"], w["br_mla"], w["wo"] = fw["w_br_dn"], fw["w_br_mla"], fw["w_o"]
    else:
        w["ffn_in_t"], w["ffn_out"] = fw["w_ffn_in"], fw["w_ffn_out"]
        w["ple_t"], w["ple_gate"] = fw["w_ple"], fw["w_ple_gate"]
    return w


_GROUP_GRADS = {"ffn": (("w_ple", "ple_t"), ("w_ple_gate", "ple_gate"), ("w_ffn_out", "ffn_out"),
                        ("w_ffn_in", "ffn_in_t")),
                "mix": (("w_o", "wo"), ("w_br_mla", "br_mla"), ("w_uq", "uq"), ("w_uk", "uk"), ("w_uv", "uv"),
                        ("w_br_dn", "br_dn"))}


def _group_grads(group, g):
    out = {}
    for name, key in _GROUP_GRADS[group]:
        t = g[key]
        if name == "w_uq":
            uq_n = t[:, :D_MODEL].reshape(Q_LORA, N_HEADS, HEAD)
            uq_r = t[:, D_MODEL:].reshape(Q_LORA, N_HEADS, HEAD)[:, :, :ROPE]
            t = jnp.concatenate([uq_n, uq_r], axis=2).reshape(Q_LORA, -1)
        out[name] = t
    return out


PACK_ROWS = 512
SUBLANES = 8


def _pack_rows(parts, name):
    arrays = []
    for a, _, _ in parts:
        if not any(a is b for b in arrays):
            arrays.append(a)
    index = lambda a: next(i for i, b in enumerate(arrays) if a is b)
    chunks, dst = [], 0
    for a, first, rows in parts:
        assert first % SUBLANES == 0 and rows % SUBLANES == 0
        chunks += [(index(a), first + o, dst + o, min(PACK_ROWS, rows - o)) for o in range(0, rows, PACK_ROWS)]
        dst += rows
    c, n, last = arrays[0].shape[1], len(arrays), len(chunks) - 1

    def body(*refs):
        src_refs, out_ref, buf, sem_in, sem_out = refs[:n], refs[n], refs[n + 1], refs[n + 2], refs[n + 3]

        def load(k):
            i, first, _, rows = chunks[k]
            return pltpu.make_async_copy(src_refs[i].at[pl.ds(first, rows)], buf.at[k % 2, pl.ds(0, rows)],
                                         sem_in.at[k % 2])

        def store(k):
            _, _, first, rows = chunks[k]
            return pltpu.make_async_copy(buf.at[k % 2, pl.ds(0, rows)], out_ref.at[pl.ds(first, rows), 0, :],
                                         sem_out.at[k % 2])

        load(0).start()
        for k in range(last + 1):
            load(k).wait()
            store(k).start()
            if k >= 1:
                store(k - 1).wait()
            if k < last:
                load(k + 1).start()
        store(last).wait()

    return pl.pallas_call(
        body,
        out_shape=jax.ShapeDtypeStruct((dst, 1, c), F32),
        in_specs=[_ANY] * n,
        out_specs=_ANY,
        scratch_shapes=[pltpu.VMEM((2, PACK_ROWS, c), F32), pltpu.SemaphoreType.DMA((2,)),
                        pltpu.SemaphoreType.DMA((2,))],
        name=name,
    )(*arrays)


def _w_in_grad(g):
    wb = g["wb_t"]
    return _pack_rows([
        (g["wa_qkv_t"], 0, 3 * D_MODEL), (g["wa_z_t"], 0, D_MODEL), (wb, WB_BA, 2 * N_HEADS), (wb, WB_CQ, Q_LORA),
        (wb, WB_CKV, KV_LORA), (wb, WB_KR, ROPE), (g["wg_t"], 0, 2 * D_MODEL)], "pack_w_in")


def _small_grads(g):
    return {"ln1_g": g["ln1g"], "ln1_b": g["ln1b"], "ln2_g": g["ln2g"], "ln2_b": g["ln2b"], "q_norm_w": g["qnw"],
            "kv_norm_w": g["kvnw"], "dn_norm_w": g["dnw"], "dn_a_log": g["alog"], "dn_dt_bias": g["dtb"],
            "conv_w": g["conv"]}


_SMALL_SLOTS = {"ln1_g": (0, 0, 1024), "ln1_b": (1, 0, 1024), "ln2_g": (2, 0, 1024), "ln2_b": (3, 0, 1024),
                "q_norm_w": (4, 0, 384), "kv_norm_w": (4, 384, 256), "dn_norm_w": (4, 640, 128),
                "dn_a_log": (4, 768, 8), "dn_dt_bias": (4, 896, 8)}
_SMALL_ROWS, _LOSS_ROW, _CONV_ROW0, _CONV_ROWS = 24, 5, 8, 12


def _pack_small_grads(small_g, loss):
    zeros = lambda r, c: jnp.zeros((r, c), F32)
    row4 = jnp.concatenate([small_g["q_norm_w"], small_g["kv_norm_w"], small_g["dn_norm_w"], small_g["dn_a_log"],
                            small_g["dn_dt_bias"]], axis=1)
    row5 = jnp.concatenate([loss, zeros(1, FLAT_COLS - LANES)], axis=1)
    head = jnp.concatenate([small_g["ln1_g"], small_g["ln1_b"], small_g["ln2_g"], small_g["ln2_b"], row4, row5,
                            zeros(2, FLAT_COLS)], axis=0)
    conv = small_g["conv_w"].reshape(_CONV_ROWS, FLAT_COLS)
    return jnp.concatenate([head, conv, zeros(_SMALL_ROWS - _CONV_ROW0 - _CONV_ROWS, FLAT_COLS)], axis=0)


_MESH_ID = pl.DeviceIdType.MESH
_ANY = pl.BlockSpec(memory_space=pl.ANY)


def _all_gather(blocks, name):
    n = len(blocks)

    def body(*refs):
        x_refs, out_refs = refs[:n], refs[n:2 * n]
        send_sems, recv_sems, local_sems = refs[2 * n:]
        x, y, c = lax.axis_index("x"), lax.axis_index("y"), lax.axis_index("c")
        me, sibling = (x, y, c), (x, y, 1 - c)
        chips = [(1 - x, y), (x, 1 - y), (1 - x, 1 - y)]

        def slot(i, px, py, pc):
            return out_refs[i].at[4 * px + 2 * py + pc]

        def copy(i, k, origin, to, src=None):
            return pltpu.make_async_remote_copy(
                src_ref=slot(i, *origin) if src is None else src, dst_ref=slot(i, *origin),
                send_sem=send_sems.at[7 * i + k], recv_sem=recv_sems.at[7 * i + k], device_id=to,
                device_id_type=_MESH_ID)

        mine = [pltpu.make_async_copy(x_refs[i], slot(i, *me), local_sems.at[i]) for i in range(n)]
        first, passed = [], []
        for i in range(n):
            mine[i].start()
            first.append(copy(i, 0, me, sibling, src=x_refs[i]))
            first += [copy(i, 1 + j, me, (*chip, c), src=x_refs[i]) for j, chip in enumerate(chips)]
        for cp in first:
            cp.start()
        for i in range(n):
            for j, chip in enumerate(chips):
                copy(i, 1 + j, (*chip, c), me).wait_recv()
                passed.append(copy(i, 4 + j, (*chip, c), sibling))
                passed[-1].start()
        for i in range(n):
            copy(i, 0, sibling, me).wait_recv()
            for j, chip in enumerate(chips):
                copy(i, 4 + j, (*chip, 1 - c), me).wait_recv()
        for cp in first + passed:
            cp.wait_send()
        for cp in mine:
            cp.wait()

    return pl.pallas_call(
        body,
        out_shape=[jax.ShapeDtypeStruct((N_DEV,) + b.shape, b.dtype) for b in blocks],
        in_specs=[_ANY] * n,
        out_specs=[_ANY] * n,
        scratch_shapes=[pltpu.SemaphoreType.DMA((7 * n,)), pltpu.SemaphoreType.DMA((7 * n,)),
                        pltpu.SemaphoreType.DMA((n,))],
        name=name,
    )(*blocks)


def _exchange_sibling(srcs, name):
    n = len(srcs)

    def body(*refs):
        src_refs, dst_refs = refs[:n], refs[n:2 * n]
        send_sems, recv_sems = refs[2 * n:]
        x, y, c = lax.axis_index("x"), lax.axis_index("y"), lax.axis_index("c")
        copies = [pltpu.make_async_remote_copy(
            src_ref=src_refs[i].at[2 * q + (1 - c)], dst_ref=dst_refs[i].at[q], send_sem=send_sems.at[4 * i + q],
            recv_sem=recv_sems.at[4 * i + q], device_id=(x, y, 1 - c), device_id_type=_MESH_ID)
            for i in range(n) for q in range(4)]
        for cp in copies:
            cp.start()
        for cp in copies:
            cp.wait_recv()
        for cp in copies:
            cp.wait_send()

    return pl.pallas_call(
        body,
        out_shape=[jax.ShapeDtypeStruct((4,) + s.shape[1:], s.dtype) for s in srcs],
        in_specs=[_ANY] * n,
        out_specs=[_ANY] * n,
        scratch_shapes=[pltpu.SemaphoreType.DMA((4 * n,)), pltpu.SemaphoreType.DMA((4 * n,))],
        name=name,
    )(*srcs)


def _col_tile(c):
    return c if c <= 256 else 256


def _chip_sum(src, recv, parity, name):
    _, r, _, c = src.shape
    tc = _col_tile(c)

    def body(par_ref, a_ref, b_ref, o_ref, ob_ref):
        s = a_ref[...] + b_ref[...]
        o_ref[...] = s
        ob_ref[...] = s.astype(BF16)

    rows = lambda f: pl.BlockSpec((None, r, None, tc), f)
    blk = pl.BlockSpec((None, r, tc), lambda q, j, par: (q, 0, j))
    return pl.pallas_call(
        body,
        out_shape=[jax.ShapeDtypeStruct((4, r, c), F32), jax.ShapeDtypeStruct((4, r, c), BF16)],
        grid_spec=pltpu.PrefetchScalarGridSpec(
            num_scalar_prefetch=1, grid=(4, c // tc),
            in_specs=[rows(lambda q, j, par: (2 * q + par[0], 0, 0, j)), rows(lambda q, j, par: (q, 0, 0, j))],
            out_specs=[blk, blk]),
        compiler_params=pltpu.CompilerParams(dimension_semantics=("parallel", "parallel")),
        name=name,
    )(parity, src, recv)


_HBM = pl.BlockSpec(memory_space=pltpu.HBM)
_SEM = pl.BlockSpec(memory_space=pltpu.SEMAPHORE)
_DATAFLOW = pltpu.SideEffectType.DATAFLOW_SIDE_EFFECTING
N_PEERS = N_DEV - 1


def _ring_peer(j):
    me = 4 * lax.axis_index("x") + 2 * lax.axis_index("y") + lax.axis_index("c")
    k = (me + j) % N_DEV
    return me, k, (k // 4, (k // 2) % 2, k % 2)


def _spread_copy(i, j, src_refs, land_refs, send_sems, recv_sems, scatter):
    me, k, peer = _ring_peer(j)
    return pltpu.make_async_remote_copy(
        src_ref=src_refs[i].at[k] if scatter else src_refs[i], dst_ref=land_refs[i].at[me],
        send_sem=send_sems.at[N_PEERS * i + j - 1], recv_sem=recv_sems.at[N_PEERS * i + j - 1], device_id=peer,
        device_id_type=_MESH_ID)


def _spread_start(srcs, carry, scatter, name):
    n = len(srcs)
    lands = [lax.empty(((N_DEV,) + s.shape[-2:]), s.dtype) for s in srcs]

    def body(*refs):
        src_refs, land_refs = refs[:n], refs[n:2 * n]
        send_sems, recv_sems, local_sems = refs[2 * n + 1:2 * n + 4]
        for i in range(n):
            for j in range(1, N_DEV):
                _spread_copy(i, j, src_refs, land_refs, send_sems, recv_sems, scatter).start()
        for i in range(n):
            _own_copy(i, src_refs, land_refs, local_sems, scatter).start()

    hbm = lambda a: pltpu.HBM(a.shape, a.dtype)
    sems = pltpu.SemaphoreType.DMA((N_PEERS * n,))
    pinned = [pltpu.with_memory_space_constraint(a, pltpu.HBM) for a in list(srcs) + lands + [carry]]
    res = pl.pallas_call(
        body, name=name,
        out_shape=(sems, sems, pltpu.SemaphoreType.DMA((n,)), *[hbm(a) for a in pinned]),
        in_specs=[_HBM] * (2 * n + 1),
        out_specs=(_SEM, _SEM, _SEM, *[_HBM] * (2 * n + 1)),
        input_output_aliases={i: 3 + i for i in range(2 * n + 1)},
        compiler_params=pltpu.CompilerParams(has_side_effects=_DATAFLOW),
    )(*pinned)
    return res[:3], list(res[3:3 + n]), list(res[3 + n:3 + 2 * n]), res[3 + 2 * n]


def _own_copy(i, src_refs, land_refs, local_sems, scatter):
    me = _ring_peer(0)[0]
    return pltpu.make_async_copy(src_refs[i].at[me] if scatter else src_refs[i], land_refs[i].at[me],
                                 local_sems.at[i])


def _spread_wait(started, after, scatter, name):
    sems, srcs, lands, _ = started
    n = len(srcs)

    def body(*refs):
        src_refs, land_refs = refs[:n], refs[n:2 * n]
        send_s, recv_s, local_s = refs[2 * n:2 * n + 3]
        for i in range(n):
            for j in range(1, N_DEV):
                cp = _spread_copy(i, j, src_refs, land_refs, send_s, recv_s, scatter)
                cp.wait_send()
                cp.wait_recv()
        for i in range(n):
            _own_copy(i, src_refs, land_refs, local_s, scatter).wait()

    hbm = lambda a: pltpu.HBM(a.shape, a.dtype)
    res = pl.pallas_call(
        body, name=name,
        out_shape=tuple(hbm(a) for a in srcs + lands),
        in_specs=[_HBM] * (2 * n) + [_SEM, _SEM, _SEM, pl.BlockSpec(memory_space=pl.ANY)],
        out_specs=tuple([_HBM] * (2 * n)),
        input_output_aliases={i: i for i in range(2 * n)},
        compiler_params=pltpu.CompilerParams(has_side_effects=_DATAFLOW),
    )(*srcs, *lands, *sems, after)
    return list(res[n:])


def _chips_copy(i, j, src_refs, land_refs, send_sems, recv_sems):
    x, y, c = lax.axis_index("x"), lax.axis_index("y"), lax.axis_index("c")
    tx, ty = [(1 - x, y), (x, 1 - y), (1 - x, 1 - y)][j]
    return pltpu.make_async_remote_copy(
        src_ref=src_refs[i].at[2 * tx + ty], dst_ref=land_refs[i].at[j], send_sem=send_sems.at[3 * i + j],
        recv_sem=recv_sems.at[3 * i + j], device_id=(tx, ty, c), device_id_type=_MESH_ID)


def _chips_start(srcs, carry, name):
    n = len(srcs)
    lands = [lax.empty((3,) + s.shape[1:], s.dtype) for s in srcs]

    def body(*refs):
        src_refs, land_refs = refs[:n], refs[n:2 * n]
        send_sems, recv_sems = refs[2 * n + 1:2 * n + 3]
        for i in range(n):
            for j in range(3):
                _chips_copy(i, j, src_refs, land_refs, send_sems, recv_sems).start()

    hbm = lambda a: pltpu.HBM(a.shape, a.dtype)
    sems = pltpu.SemaphoreType.DMA((3 * n,))
    pinned = [pltpu.with_memory_space_constraint(a, pltpu.HBM) for a in list(srcs) + lands + [carry]]
    res = pl.pallas_call(
        body, name=name,
        out_shape=(sems, sems, *[hbm(a) for a in pinned]),
        in_specs=[_HBM] * (2 * n + 1),
        out_specs=(_SEM, _SEM, *[_HBM] * (2 * n + 1)),
        input_output_aliases={i: 2 + i for i in range(2 * n + 1)},
        compiler_params=pltpu.CompilerParams(has_side_effects=_DATAFLOW),
    )(*pinned)
    return res[:2], list(res[2:2 + n]), list(res[2 + n:2 + 2 * n]), res[2 + 2 * n]


def _chips_wait(started, after, name):
    sems, srcs, lands, _ = started
    n = len(srcs)

    def body(*refs):
        src_refs, land_refs = refs[:n], refs[n:2 * n]
        send_s, recv_s = refs[2 * n:2 * n + 2]
        for i in range(n):
            for j in range(3):
                cp = _chips_copy(i, j, src_refs, land_refs, send_s, recv_s)
                cp.wait_send()
                cp.wait_recv()

    hbm = lambda a: pltpu.HBM(a.shape, a.dtype)
    res = pl.pallas_call(
        body, name=name,
        out_shape=tuple(hbm(a) for a in srcs + lands),
        in_specs=[_HBM] * (2 * n) + [_SEM, _SEM, pl.BlockSpec(memory_space=pl.ANY)],
        out_specs=tuple([_HBM] * (2 * n)),
        input_output_aliases={i: i for i in range(2 * n)},
        compiler_params=pltpu.CompilerParams(has_side_effects=_DATAFLOW),
    )(*srcs, *lands, *sems, after)
    return list(res[n:])


def _sum8(landing, name):
    _, r, c = landing.shape
    tc = _col_tile(c)

    def body(a_ref, o_ref):
        tot = a_ref[0].astype(F32)
        for k in range(1, N_DEV):
            tot = tot + a_ref[k].astype(F32)
        o_ref[...] = tot

    return pl.pallas_call(
        body,
        out_shape=jax.ShapeDtypeStruct((r, c), F32),
        grid=(c // tc,),
        in_specs=[pl.BlockSpec((N_DEV, r, tc), lambda j: (0, 0, j))],
        out_specs=pl.BlockSpec((r, tc), lambda j: (0, j)),
        compiler_params=pltpu.CompilerParams(dimension_semantics=("parallel",)),
        name=name,
    )(landing)


def _adamw_math(w, g, m, v):
    m = ADAM_B1 * m + (1.0 - ADAM_B1) * g
    v = ADAM_B2 * v + (1.0 - ADAM_B2) * (g * g)
    m_hat = m / (1.0 - ADAM_B1 ** ADAM_STEP)
    v_hat = v / (1.0 - ADAM_B2 ** ADAM_STEP)
    delta = -ADAM_LR * (m_hat / (jnp.sqrt(v_hat) + ADAM_EPS) + ADAM_WD * w)
    return delta, m, v


def _adamw(w, m, v, g, name):
    r, c = w.shape

    def fn(rows, consts):
        return list(_adamw_math(*rows)), []

    return _rowwise(fn, [w, g, m, v], [], [(c, F32)] * 3, tm=r if r <= 512 else 256, name=name)


def _adamw_sum8(w, m, v, landing, name):
    r, c = w.shape
    tc = _col_tile(c)

    def body(w_ref, m_ref, v_ref, a_ref, g_ref, d_ref, m2_ref, v2_ref):
        g = a_ref[0].astype(F32)
        for k in range(1, N_DEV):
            g = g + a_ref[k].astype(F32)
        delta, m2, v2 = _adamw_math(w_ref[...], g, m_ref[...], v_ref[...])
        g_ref[...] = g
        d_ref[...] = delta
        m2_ref[...] = m2
        v2_ref[...] = v2

    blk = pl.BlockSpec((r, tc), lambda j: (0, j))
    return pl.pallas_call(
        body,
        out_shape=[jax.ShapeDtypeStruct((r, c), F32)] * 4,
        grid=(c // tc,),
        in_specs=[blk, blk, blk, pl.BlockSpec((N_DEV, r, tc), lambda j: (0, 0, j))],
        out_specs=[blk] * 4,
        compiler_params=pltpu.CompilerParams(dimension_semantics=("parallel",)),
        name=name,
    )(w, m, v, landing)


def _adamw_parts(w, m, v, own, others, chip, name):
    r, _, c = w.shape
    tc = _col_tile(c)

    def body(q_ref, w_ref, m_ref, v_ref, a_ref, b_ref, g_ref, d_ref, m2_ref, v2_ref):
        g = ((a_ref[...] + b_ref[0].astype(F32)) + b_ref[1].astype(F32)) + b_ref[2].astype(F32)
        delta, m2, v2 = _adamw_math(w_ref[...], g, m_ref[...], v_ref[...])
        g_ref[...] = g
        d_ref[...] = delta
        m2_ref[...] = m2
        v2_ref[...] = v2

    row = pl.BlockSpec((r, None, tc), lambda j, q: (0, 0, j))
    return pl.pallas_call(
        body,
        out_shape=[jax.ShapeDtypeStruct((r, 1, c), F32)] * 4,
        grid_spec=pltpu.PrefetchScalarGridSpec(
            num_scalar_prefetch=1, grid=(c // tc,),
            in_specs=[row, row, row, pl.BlockSpec((None, r, tc), lambda j, q: (q[0], 0, j)),
                      pl.BlockSpec((3, r, tc), lambda j, q: (0, 0, j))],
            out_specs=[row] * 4),
        compiler_params=pltpu.CompilerParams(dimension_semantics=("parallel",)),
        name=name,
    )(chip, w, m, v, own, others)


def _adamw_small(gathered, params):
    ns = len(_SMALL)

    def body(*refs):
        g_ref, p_refs, o_refs = refs[0], refs[1:1 + 3 * ns], refs[1 + 3 * ns:]
        tot = g_ref[0]
        for k in range(1, N_DEV):
            tot = tot + g_ref[k]
        for i, name in enumerate(_SMALL):
            row, lane0, lanes = _SMALL_SLOTS[name]
            g = tot[row:row + 1, lane0:lane0 + lanes]
            w_, m_, v_ = (p_refs[3 * i + j][...] for j in range(3))
            delta, m2, v2 = _adamw_math(w_, g, m_, v_)
            for j, val in enumerate((g, delta, m2, v2)):
                o_refs[4 * i + j][...] = val
        o_refs[4 * ns][...] = tot[_LOSS_ROW:_LOSS_ROW + 1, 0:LANES]
        o_refs[4 * ns + 1][...] = tot[_CONV_ROW0:_CONV_ROW0 + _CONV_ROWS, :]

    out_shape = [jax.ShapeDtypeStruct(w.shape, F32) for (w, _, _) in params for _ in range(4)]
    out_shape += [jax.ShapeDtypeStruct((1, LANES), F32), jax.ShapeDtypeStruct((_CONV_ROWS, FLAT_COLS), F32)]
    flat = [a for wmv in params for a in wmv]
    return pl.pallas_call(body, out_shape=out_shape, name="adamw_small")(gathered, *flat)


def kernel(x, p, positions, w_in, conv_w, dn_a_log, dn_dt_bias, dn_norm_w, q_norm_w, w_uq, kv_norm_w, w_uk, w_uv, w_br_dn, w_br_mla, w_o, ln1_g, ln1_b, w_ffn_in, w_ffn_out, w_ple, w_ple_gate, ln2_g, ln2_b, loss_target, m_w_in, m_conv_w, m_dn_a_log, m_dn_dt_bias, m_dn_norm_w, m_q_norm_w, m_w_uq, m_kv_norm_w, m_w_uk, m_w_uv, m_w_br_dn, m_w_br_mla, m_w_o, m_ln1_g, m_ln1_b, m_w_ffn_in, m_w_ffn_out, m_w_ple, m_w_ple_gate, m_ln2_g, m_ln2_b, v_w_in, v_conv_w, v_dn_a_log, v_dn_dt_bias, v_dn_norm_w, v_q_norm_w, v_w_uq, v_kv_norm_w, v_w_uk, v_w_uv, v_w_br_dn, v_w_br_mla, v_w_o, v_ln1_g, v_ln1_b, v_w_ffn_in, v_w_ffn_out, v_w_ple, v_w_ple_gate, v_ln2_g, v_ln2_b):
    args = dict(locals())
    wts = {n: args[n] for n in _ORDER}
    mom1 = {n: args["m_" + n] for n in _ORDER}
    mom2 = {n: args["v_" + n] for n in _ORDER}
    big_names = [n for n, _ in _BIG]
    shard_shapes = {n: wts[n].shape[1:] for n in big_names}
    c_idx = lax.axis_index("c")
    q_idx = 2 * lax.axis_index("x") + lax.axis_index("y")
    parity, chip = c_idx.reshape(1).astype(jnp.int32), q_idx.reshape(1).astype(jnp.int32)

    stored = {n: _to_stored(n, wts[n][0]).astype(BF16) for n in big_names}
    first = _all_gather([stored["w_in"], conv_w[0]], "ag_first")
    group_names = {grp: [n for n, _ in pairs] for grp, pairs in _GROUP_GRADS.items()}
    carry, gathers = first[0], {}
    for grp in ("mix", "ffn"):
        gathers[grp] = _spread_start([stored[n] for n in group_names[grp]], carry, False, "ag_start_" + grp)
        carry = gathers[grp][3]
    conv_full = jnp.moveaxis(first[1], 0, 1).reshape(conv_w.shape[1], -1)
    small_w = {n: wts[n].astype(F32) for n in _SMALL}
    w = _first_weights(carry.reshape(-1, D_MODEL), conv_full, small_w)

    def late_weights(grp, after):
        got = _spread_wait(gathers[grp], after, False, "ag_wait_" + grp)
        return _late_weights(grp, {n: t.reshape(-1, t.shape[-1]) for n, t in zip(group_names[grp], got)})

    started = {}

    def emit(group, g, carry):
        if group == "w_in":
            rows, cols = _stored_shape("w_in", shard_shapes["w_in"])
            src = _w_in_grad(g).reshape(N_DEV, rows, 1, cols)
            from_sibling = _exchange_sibling([src], "rs_sibling")[0]
            own, own_bf = _chip_sum(src, from_sibling, parity, "rs_sum_w_in")
            started["w_in"] = (own, _chips_start([own_bf], carry, "rs_chips_start"))
            return started["w_in"][1][3]
        grads = _group_grads(group, g)
        srcs = [grads[n].reshape((N_DEV,) + _stored_shape(n, shard_shapes[n])) for n in grads]
        started[group] = (list(grads), _spread_start(srcs, carry, True, "rs_start_" + group))
        return started[group][1][3]

    s_dim = x.shape[1]
    loss, dx, g = _local_step(x[0], p[0, 0], positions.reshape(s_dim, 1).astype(F32), loss_target[0], w,
                              late_weights, emit)
    small_g = _small_grads(g)
    own, chips_started = started.pop("w_in")
    from_chips = _chips_wait(chips_started, dx, "rs_chips_wait")[0]

    out_g, out_d, out_m, out_v = {}, {}, {}, {}

    def update(n, grad, shp):
        flat2 = (shp[0], int(np.prod(shp[1:])))
        d, m2, v2 = _adamw(wts[n][0].reshape(flat2), mom1[n][0].reshape(flat2), mom2[n][0].reshape(flat2),
                           grad.reshape(flat2), "adamw_" + n)
        out_g[n], out_d[n], out_m[n], out_v[n] = grad, d.reshape(shp), m2.reshape(shp), v2.reshape(shp)

    rows_first = lambda a: jnp.transpose(a, (2, 0, 1))
    res = _adamw_parts(rows_first(wts["w_in"]), rows_first(mom1["w_in"]), rows_first(mom2["w_in"]), own, from_chips,
                       chip, "adamw_w_in")
    out_g["w_in"], out_d["w_in"], out_m["w_in"], out_v["w_in"] = (jnp.transpose(t, (1, 2, 0))[0] for t in res)
    for group, (names, st) in started.items():
        for n, landing in zip(names, _spread_wait(st, dx, True, "rs_wait_" + group)):
            shp = shard_shapes[n]
            if _BIG_AXIS[n] == 0 or shp[-1] % LANES:
                res = _adamw_sum8(_to_stored(n, wts[n][0]), _to_stored(n, mom1[n][0]), _to_stored(n, mom2[n][0]),
                                  landing, "adamw_" + n)
                out_g[n], out_d[n], out_m[n], out_v[n] = (_from_stored(n, t, shp) for t in res)
            else:
                update(n, _from_stored(n, _sum8(landing, "rs_total_" + n), shp), shp)

    g_small = _all_gather([_pack_small_grads(small_g, loss)], "ag_small")[0]
    res = _adamw_small(g_small, [(wts[n], mom1[n], mom2[n]) for n in _SMALL])
    for i, n in enumerate(_SMALL):
        out_g[n], out_d[n], out_m[n], out_v[n] = res[4 * i:4 * i + 4]
    loss_out = res[4 * len(_SMALL)][0, 0]
    conv_shape = conv_w.shape[1:]
    conv_g = lax.dynamic_slice(res[-1].reshape(conv_shape[0], -1), (0, (2 * q_idx + c_idx) * conv_shape[1]),
                               conv_shape)
    update("conv_w", conv_g, conv_shape)

    expand = lambda d, n: d[n] if n in _SMALL else d[n][None]
    return (loss_out, dx[None], *[expand(out_g, n) for n in _ORDER], *[expand(out_d, n) for n in _ORDER],
            *[expand(out_m, n) for n in _ORDER], *[expand(out_v, n) for n in _ORDER])
```

```python
import functools

import numpy as np
import jax
import jax.numpy as jnp
from jax import lax
from jax.experimental import pallas as pl
from jax.experimental.pallas import tpu as pltpu

F32 = jnp.float32
BF16 = jnp.bfloat16

D_MODEL = 1024
N_HEADS = 8
HEAD = 128
CHUNK = 64
GROUP = 256
ROPE = 64
Q_LORA = 384
KV_LORA = 256
FFN_HIDDEN = 2816
PLE_DIM = 256
ROPE_BASE = 10000.0
ALPHA = 2.0 ** 0.25
SCALE = float((HEAD + ROPE) ** -0.5)
NEG_BIG = -1e30
EPS_RMS = 1e-6
EPS_LN = 1e-5

ADAM_LR = 0.001
ADAM_B1 = 0.9
ADAM_B2 = 0.999
ADAM_EPS = 1e-08
ADAM_WD = 0.01
ADAM_STEP = 10

N_DEV = 8
LANES = 128
FLAT_COLS = 1024

WB_CQ, WB_CKV, WB_KR, WB_BA, WB_COLS = 0, 512, 768, 896, 1024

HIGHEST = lax.Precision.HIGHEST

NN = (((1,), (0,)), ((), ()))
TN = (((0,), (0,)), ((), ()))
NT = (((1,), (1,)), ((), ()))


def _dot(a, b, dims=NN):
    return lax.dot_general(a.astype(BF16), b.astype(BF16), dims, preferred_element_type=F32)


def _dot32(a, b, dims=NN):
    return lax.dot_general(a, b, dims, precision=HIGHEST, preferred_element_type=F32)


def _sig(x):
    return 1.0 / (1.0 + jnp.exp(-x))


MM_TILE = 1536


def _pick_wide(n):
    if n <= MM_TILE:
        return n
    return max(t for t in range(LANES, MM_TILE + 1, LANES) if n % t == 0)


def _split_bf16(a):
    hi = a.astype(BF16)
    return hi, (a - hi.astype(F32)).astype(BF16)


def _dot3(a, b, dims=NN):
    ah, al = a if isinstance(a, tuple) else _split_bf16(a)
    bh, bl = b if isinstance(b, tuple) else _split_bf16(b)
    d = lambda p, q: lax.dot_general(p, q, dims, preferred_element_type=F32)
    return d(ah, bh) + (d(ah, bl) + d(al, bh))


def _mm(a, b, *, ta=False, tb=False, add=(), out_dtype=F32, b_rows=None, name):
    if ta:
        k_dim, m_dim = a.shape
    else:
        m_dim, k_dim = a.shape
    b_first, b_len = b_rows if b_rows else (0, b.shape[0])
    if tb:
        n_dim, k2 = b_len, b.shape[1]
    else:
        k2, n_dim = b_len, b.shape[1]
    assert k_dim == k2, (a.shape, b.shape, ta, tb)
    tm = _pick_wide(m_dim)
    tn = _pick_wide(n_dim)
    tk = _pick_wide(k_dim)
    nk = k_dim // tk
    n_add = len(add)
    dims = TN if ta else (NT if tb else NN)
    assert not (ta and tb)

    def body(a_ref, b_ref, *rest):
        add_refs = rest[:n_add]
        o_ref = rest[n_add]
        acc = rest[n_add + 1]
        k = pl.program_id(2)

        @pl.when(k == 0)
        def _():
            acc[...] = jnp.zeros_like(acc)

        acc[...] += _dot(a_ref[...], b_ref[...], dims)

        @pl.when(k == nk - 1)
        def _():
            r = acc[...]
            for ar in add_refs:
                r = r + ar[...].astype(F32)
            o_ref[...] = r.astype(o_ref.dtype)

    a_spec = pl.BlockSpec((tk, tm), lambda i, j, k: (k, i)) if ta else pl.BlockSpec((tm, tk), lambda i, j, k: (i, k))
    b_tile = tn if tb else tk
    assert b_first % b_tile == 0
    b0 = b_first // b_tile
    b_spec = (pl.BlockSpec((tn, tk), lambda i, j, k: (b0 + j, k)) if tb
              else pl.BlockSpec((tk, tn), lambda i, j, k: (b0 + k, j)))
    o_spec = pl.BlockSpec((tm, tn), lambda i, j, k: (i, j))
    return pl.pallas_call(
        body,
        out_shape=jax.ShapeDtypeStruct((m_dim, n_dim), out_dtype),
        grid=(m_dim // tm, n_dim // tn, nk),
        in_specs=[a_spec, b_spec] + [o_spec] * n_add,
        out_specs=o_spec,
        scratch_shapes=[pltpu.VMEM((tm, tn), F32)],
        compiler_params=pltpu.CompilerParams(dimension_semantics=("parallel", "parallel", "arbitrary")),
        name=name,
    )(a, b, *add)


def _rowwise(fn, rows, consts, outs, accs=(), *, tm=256, name):
    rows = [r if isinstance(r, tuple) else (r, 0, r.shape[1]) for r in rows]
    s_dim = rows[0][0].shape[0]
    tm = min(tm, s_dim)
    assert s_dim % tm == 0 and all(arr.shape[0] == s_dim for arr, _, _ in rows)
    specs = [pl.BlockSpec((tm, width), functools.partial(lambda i, cb: (i, cb), cb=cb)) for _, cb, width in rows]
    args = [arr for arr, _, _ in rows]
    for c in consts:
        specs.append(pl.BlockSpec(c.shape, lambda i: (0, 0)))
        args.append(c)
    nr, nc, no = len(rows), len(consts), len(outs)
    flipped = [len(o) == 3 for o in outs]
    out_shape = [jax.ShapeDtypeStruct((o[0], s_dim) if t else (s_dim, o[0]), o[1]) for o, t in zip(outs, flipped)]
    out_specs = [pl.BlockSpec((o[0], tm), lambda i: (0, i)) if t else pl.BlockSpec((tm, o[0]), lambda i: (i, 0))
                 for o, t in zip(outs, flipped)]
    out_shape += [jax.ShapeDtypeStruct(sh, F32) for sh in accs]
    out_specs += [pl.BlockSpec(sh, lambda i: (0, 0)) for sh in accs]

    def body(*refs):
        r = [x[...].astype(F32) if x.dtype == BF16 else x[...] for x in refs[:nr]]
        c = [x[...] for x in refs[nr:nr + nc]]
        o_refs = refs[nr + nc:nr + nc + no]
        a_refs = refs[nr + nc + no:]
        o_vals, a_vals = fn(r, c)
        for ref, v, t in zip(o_refs, o_vals, flipped, strict=True):
            ref[...] = (jnp.transpose(v.astype(F32)) if t else v).astype(ref.dtype)
        if a_refs:
            @pl.when(pl.program_id(0) == 0)
            def _():
                for ref in a_refs:
                    ref[...] = jnp.zeros_like(ref)

            for ref, v in zip(a_refs, a_vals, strict=True):
                ref[...] += v

    res = pl.pallas_call(
        body,
        out_shape=out_shape,
        grid=(s_dim // tm,),
        in_specs=specs,
        out_specs=out_specs,
        compiler_params=pltpu.CompilerParams(dimension_semantics=("arbitrary" if accs else "parallel",)),
        name=name,
    )(*args)
    return res


def _colsum(v):
    return jnp.sum(v, axis=0, keepdims=True)


def _rowsum(v):
    return jnp.sum(v, axis=1, keepdims=True)


def _rowmean(v):
    return jnp.mean(v, axis=1, keepdims=True)


def _silu_grad(x):
    s = _sig(x)
    return s * (1.0 + x * (1.0 - s))


def _conv_taps(x, w, width=4):
    row = lax.broadcasted_iota(jnp.int32, x.shape, 0)
    c = x * w[width - 1:width, :]
    for s in range(1, width):
        c = c + jnp.where(row >= s, pltpu.roll(x, s, 0), 0.0) * w[width - 1 - s:width - s, :]
    return c


def _conv_fwd(proj_a, conv_w):
    s_dim = proj_a.shape[0]
    n_blk = 3 * N_HEADS

    def body(x_ref, w_ref, o_ref):
        j = pl.program_id(0)
        c = _conv_taps(x_ref[...], w_ref[...])
        y = c * _sig(c)
        r = lax.rsqrt(_rowsum(y * y) + EPS_RMS)
        fac = jnp.where(j < N_HEADS, r * (HEAD ** -0.5), jnp.where(j < 2 * N_HEADS, r, 1.0))
        o_ref[...] = y * fac

    return pl.pallas_call(
        body,
        out_shape=jax.ShapeDtypeStruct((s_dim, n_blk * HEAD), F32),
        grid=(n_blk,),
        in_specs=[pl.BlockSpec((s_dim, HEAD), lambda j: (0, j)), pl.BlockSpec((4, HEAD), lambda j: (0, j))],
        out_specs=pl.BlockSpec((s_dim, HEAD), lambda j: (0, j)),
        compiler_params=pltpu.CompilerParams(dimension_semantics=("parallel",)),
        name="conv_fwd",
    )(proj_a, conv_w)


def _conv_bwd(proj_a, conv_w, dq, dk, dv):
    s_dim = proj_a.shape[0]
    n_blk = 3 * N_HEADS

    def body(x_ref, w_ref, dq_ref, dk_ref, dv_ref, dx_ref, dw_ref):
        j = pl.program_id(0)
        x = x_ref[...]
        w = w_ref[...]
        do = jnp.where(j < N_HEADS, dq_ref[...], jnp.where(j < 2 * N_HEADS, dk_ref[...], dv_ref[...]))
        c = _conv_taps(x, w)
        sg = _sig(c)
        y = c * sg
        r = lax.rsqrt(_rowsum(y * y) + EPS_RMS)
        sc = jnp.where(j < N_HEADS, HEAD ** -0.5, 1.0)
        dy_n = sc * (r * do - y * (r * r * r) * _rowsum(do * y))
        dy = jnp.where(j < 2 * N_HEADS, dy_n, do)
        dc = dy * (sg * (1.0 + c * (1.0 - sg)))
        row = lax.broadcasted_iota(jnp.int32, x.shape, 0)
        dx = dc * w[3:4, :]
        dw_ref[3:4, :] = _colsum(dc * x)
        for s in range(1, 4):
            dx = dx + jnp.where(row < s_dim - s, pltpu.roll(dc, s_dim - s, 0), 0.0) * w[3 - s:4 - s, :]
            xs = jnp.where(row >= s, pltpu.roll(x, s, 0), 0.0)
            dw_ref[3 - s:4 - s, :] = _colsum(dc * xs)
        dx_ref[...] = dx.astype(dx_ref.dtype)

    hd = N_HEADS - 1
    return pl.pallas_call(
        body,
        out_shape=[jax.ShapeDtypeStruct((s_dim, n_blk * HEAD), BF16), jax.ShapeDtypeStruct((4, n_blk * HEAD), F32)],
        grid=(n_blk,),
        in_specs=[
            pl.BlockSpec((s_dim, HEAD), lambda j: (0, j)),
            pl.BlockSpec((4, HEAD), lambda j: (0, j)),
            pl.BlockSpec((s_dim, HEAD), lambda j: (0, jnp.minimum(j, hd))),
            pl.BlockSpec((s_dim, HEAD), lambda j: (0, jnp.clip(j - N_HEADS, 0, hd))),
            pl.BlockSpec((s_dim, HEAD), lambda j: (0, jnp.clip(j - 2 * N_HEADS, 0, hd))),
        ],
        out_specs=[pl.BlockSpec((s_dim, HEAD), lambda j: (0, j)), pl.BlockSpec((4, HEAD), lambda j: (0, j))],
        compiler_params=pltpu.CompilerParams(dimension_semantics=("parallel",)),
        name="conv_bwd",
    )(proj_a, conv_w, dq, dk, dv)


def _chunk_tri(n):
    r = np.arange(n)
    m = ((r[:, None] // CHUNK) == (r[None, :] // CHUNK)) & (r[:, None] >= r[None, :])
    m = m.astype(np.float32)
    return jnp.asarray(m), jnp.asarray(m.T)


def _softplus(z):
    return jnp.maximum(z, 0.0) + jnp.log(1.0 + jnp.exp(-jnp.abs(z)))


def _gates_fwd(proj_b, alog, dtb):
    tm = min(GROUP, proj_b.shape[0])
    tri, _ = _chunk_tri(tm)

    def fn(r, c):
        b = r[0]
        a = pltpu.roll(b, LANES - N_HEADS, 1)
        alog_, dtb_, tri_ = c
        g = -jnp.exp(alog_) * _softplus(a + dtb_)
        return [_sig(b), _dot32(tri_, g)], []

    return _rowwise(fn, [(proj_b, WB_BA // LANES, LANES)], [alog, dtb, tri],
                    [(LANES, F32), (LANES, F32)], tm=tm, name="gates_fwd")


def _gates_bwd(proj_b, alog, dtb, gc, d_beta, d_gc, d_egl_rows):
    tm = min(GROUP, proj_b.shape[0])
    _, tri_t = _chunk_tri(tm)

    def fn(r, c):
        b, gc_, d_beta_, d_gc_, d_egl_ = r
        a = pltpu.roll(b, LANES - N_HEADS, 1)
        alog_, dtb_, tri_t_ = c
        z = a + dtb_
        ea = jnp.exp(alog_)
        g = -ea * _softplus(z)
        dg = _dot32(tri_t_, d_gc_ + d_egl_ * jnp.exp(gc_))
        d_a = dg * (-ea) * _sig(z)
        beta = _sig(b)
        d_ba = d_beta_ * beta * (1.0 - beta) + pltpu.roll(d_a, N_HEADS, 1)
        return [d_ba], [_colsum(dg * g), _colsum(d_a)]

    return _rowwise(fn, [(proj_b, WB_BA // LANES, LANES), gc, d_beta, d_gc, d_egl_rows],
                    [alog, dtb, tri_t], [(LANES, BF16)], accs=[(1, LANES), (1, LANES)], tm=tm,
                    name="gates_bwd")


def _group_masks(n):
    r = lax.broadcasted_iota(jnp.int32, (n, n), 0)
    c = lax.broadcasted_iota(jnp.int32, (n, n), 1)
    same = (r // CHUNK) == (c // CHUNK)
    below, s = [], 2
    while s < CHUNK:
        below.append(jnp.logical_and((r // (2 * s)) == (c // (2 * s)),
                                     jnp.logical_and((r // s) % 2 == 1, (c // s) % 2 == 0)))
        s *= 2
    return dict(same=same, tril=jnp.logical_and(same, r >= c), strict=jnp.logical_and(same, r > c),
                last=c == (r // CHUNK) * CHUNK + (CHUNK - 1), eye=r == c, pair=(r // 2) == (c // 2), below=below)


def _inv_unit_lower(l_mats, mk):
    eye_f = mk["eye"].astype(F32)
    ts = [eye_f - jnp.where(mk["pair"], l_mat, 0.0) for l_mat in l_mats]
    for below in mk["below"]:
        halves = [_split_bf16(t) for t in ts]
        mids = [_dot3(h, jnp.where(below, l_mat, 0.0)) for h, l_mat in zip(halves, l_mats)]
        ts = [t - _dot3(m, h) for t, m, h in zip(ts, mids, halves)]
    return ts


def _unfold_blocks(folded, mask):
    n = folded.shape[0]
    return jnp.where(mask, jnp.concatenate([folded] * (n // CHUNK), axis=1), 0.0)


def _head_cols(beta, gc, gc_t, h):
    lane = lax.broadcasted_iota(jnp.int32, beta.shape, 1)
    sub = lax.broadcasted_iota(jnp.int32, gc_t.shape, 0)
    bcol = _rowsum(jnp.where(lane == h, beta, 0.0))
    gcol = _rowsum(jnp.where(lane == h, gc, 0.0))
    grow = _colsum(jnp.where(sub == h, gc_t, 0.0))
    return bcol, gcol, grow


def _prep_common(q, k, bcol, gcol, grow, mk, t_folded=None):
    n = q.shape[0]
    tril = mk["tril"]
    decay = jnp.where(tril, jnp.exp(jnp.where(tril, gcol - grow, 0.0)), 0.0)
    glast = _rowsum(jnp.where(mk["last"], jnp.broadcast_to(grow, (n, n)), 0.0))
    e = jnp.exp(gcol)
    ekt = jnp.exp(glast - gcol)
    kb = k * bcol
    kk = _dot(kb, k, NT)
    qk = _dot(q, k, NT)
    p = dict(decay=decay, e=e, ekt=ekt, kb=kb, kk=kk, qk=qk)
    if t_folded is not None:
        p["t"] = _unfold_blocks(t_folded, mk["same"])
    return p


GROUPS_PER_STEP = 4
SCAN_CHUNKS_PER_STEP = 4


def _fold_blocks(m):
    n = m.shape[0]
    out = m[:, 0:CHUNK]
    for b in range(1, n // CHUNK):
        out = out + m[:, b * CHUNK:(b + 1) * CHUNK]
    return out


def _gdr_prep_fwd(qkvn, beta, gc, gc_t):
    s_dim = qkvn.shape[0]
    tg = min(GROUP, s_dim)
    n_sub = min(GROUPS_PER_STEP, s_dim // tg)
    tb = tg * n_sub

    def body(q_ref, k_ref, v_ref, b_ref, g_ref, gt_ref, u_ref, w_ref, qd_ref, kt_ref, a_ref, t_ref):
        h = pl.program_id(0)
        mk = _group_masks(tg)
        parts = []
        for s in range(n_sub):
            rows = slice(s * tg, (s + 1) * tg)
            q, k, v = q_ref[rows, :], k_ref[rows, :], v_ref[rows, :]
            bcol, gcol, grow = _head_cols(b_ref[rows, :], g_ref[rows, :], gt_ref[:, rows], h)
            p = _prep_common(q, k, bcol, gcol, grow, mk)
            qd_ref[rows, :] = q * p["e"]
            kt_ref[rows, :] = k * p["ekt"]
            a_ref[rows, :] = _fold_blocks(jnp.where(mk["tril"], p["qk"] * p["decay"], 0.0))
            parts.append((rows, v * bcol, p["kb"] * p["e"], jnp.where(mk["strict"], p["kk"] * p["decay"], 0.0)))
        t_mats = _inv_unit_lower([part[3] for part in parts], mk)
        for (rows, vb, kbe, _), t_mat in zip(parts, t_mats):
            u_ref[rows, :] = _dot(t_mat, vb)
            w_ref[rows, :] = _dot(t_mat, kbe)
            t_ref[rows, :] = _fold_blocks(t_mat)

    row = lambda off: pl.BlockSpec((tb, HEAD), functools.partial(lambda h, m, off: (m, h + off), off=off))
    full = pl.BlockSpec((tb, LANES), lambda h, m: (m, 0))
    o_spec = pl.BlockSpec((tb, HEAD), lambda h, m: (m, h))
    a_spec = pl.BlockSpec((None, tb, CHUNK), lambda h, m: (h, m, 0))
    wide = jax.ShapeDtypeStruct((s_dim, N_HEADS * HEAD), F32)
    folded = jax.ShapeDtypeStruct((N_HEADS, s_dim, CHUNK), F32)
    return pl.pallas_call(
        body,
        out_shape=[wide, wide, wide, wide, folded, folded],
        grid=(N_HEADS, s_dim // tb),
        in_specs=[row(0), row(N_HEADS), row(2 * N_HEADS), full, full, pl.BlockSpec((8, tb), lambda h, m: (0, m))],
        out_specs=[o_spec, o_spec, o_spec, o_spec, a_spec, a_spec],
        compiler_params=pltpu.CompilerParams(dimension_semantics=("parallel", "parallel")),
        name="gdr_prep_fwd",
    )(qkvn, qkvn, qkvn, beta, gc, gc_t)


def _gdr_prep_bwd(qkvn, beta, gc, gc_t, t_fold, u, w, du, dw, dqd, dkt, d_a):
    s_dim = qkvn.shape[0]
    tg = min(GROUP, s_dim)
    n_sub = min(GROUPS_PER_STEP, s_dim // tg)
    tb = tg * n_sub

    def body(q_ref, k_ref, v_ref, b_ref, g_ref, gt_ref, t_ref, u_ref, w_ref, du_ref, dw_ref, dqd_ref, dkt_ref,
             da_ref, dq_ref, dk_ref, dv_ref, db_ref, dg_ref):
        h = pl.program_id(1)

        @pl.when(h == 0)
        def _():
            db_ref[...] = jnp.zeros_like(db_ref)
            dg_ref[...] = jnp.zeros_like(dg_ref)

        mk = _group_masks(tg)
        lane = lax.broadcasted_iota(jnp.int32, (tg, LANES), 1)
        for s in range(n_sub):
            rows = slice(s * tg, (s + 1) * tg)
            q, k, v = q_ref[rows, :], k_ref[rows, :], v_ref[rows, :]
            bcol, gcol, grow = _head_cols(b_ref[rows, :], g_ref[rows, :], gt_ref[:, rows], h)
            p = _prep_common(q, k, bcol, gcol, grow, mk, t_ref[rows, :])
            t_mat, decay, e, ekt, kb = p["t"], p["decay"], p["e"], p["ekt"], p["kb"]
            du_, dw_, dqd_, dkt_ = du_ref[rows, :], dw_ref[rows, :], dqd_ref[rows, :], dkt_ref[rows, :]
            dvb = _dot(t_mat, du_, TN)
            dkbe = _dot(t_mat, dw_, TN)
            d_l = -(_dot(dvb, u_ref[rows, :], NT) + _dot(dkbe, w_ref[rows, :], NT))
            m1 = jnp.where(mk["strict"], d_l, 0.0)
            m2 = _unfold_blocks(da_ref[rows, :], mk["tril"])
            d_kk = m1 * decay
            d_qk = m2 * decay
            d_decay = m1 * p["kk"] + m2 * p["qk"]
            dkb = _dot(d_kk, k) + dkbe * e
            dk = _dot(d_kk, kb, TN) + _dot(d_qk, q, TN) + dkt_ * ekt + dkb * bcol
            dq = _dot(d_qk, k) + dqd_ * e
            d_beta = _rowsum(dkb * k) + _rowsum(dvb * v)
            d_e = _rowsum(dkbe * kb) + _rowsum(dqd_ * q)
            d_ekt = _rowsum(dkt_ * k) * ekt
            d_diff = d_decay * decay
            d_grow = -_colsum(d_diff) + _colsum(jnp.where(mk["last"], jnp.broadcast_to(d_ekt, (tg, tg)), 0.0))
            d_gcol = d_e * e - d_ekt + _rowsum(d_diff)
            d_gcol = d_gcol + _rowsum(jnp.where(mk["eye"], jnp.broadcast_to(d_grow, (tg, tg)), 0.0))
            dq_ref[rows, :] = dq
            dk_ref[rows, :] = dk
            dv_ref[rows, :] = dvb * bcol
            db_ref[rows, :] = jnp.where(lane == h, d_beta, db_ref[rows, :])
            dg_ref[rows, :] = jnp.where(lane == h, d_gcol, dg_ref[rows, :])

    row = lambda off: pl.BlockSpec((tb, HEAD), functools.partial(lambda m, h, off: (m, h + off), off=off))
    full = pl.BlockSpec((tb, LANES), lambda m, h: (m, 0))
    o_spec = pl.BlockSpec((tb, HEAD), lambda m, h: (m, h))
    a_spec = pl.BlockSpec((None, tb, CHUNK), lambda m, h: (h, m, 0))
    wide = jax.ShapeDtypeStruct((s_dim, N_HEADS * HEAD), F32)
    lanes = jax.ShapeDtypeStruct((s_dim, LANES), F32)
    return pl.pallas_call(
        body,
        out_shape=[wide, wide, wide, lanes, lanes],
        grid=(s_dim // tb, N_HEADS),
        in_specs=[row(0), row(N_HEADS), row(2 * N_HEADS), full, full, pl.BlockSpec((8, tb), lambda m, h: (0, m)),
                  a_spec, o_spec, o_spec, o_spec, o_spec, o_spec, o_spec, a_spec],
        out_specs=[o_spec, o_spec, o_spec, full, full],
        compiler_params=pltpu.CompilerParams(dimension_semantics=("parallel", "arbitrary")),
        name="gdr_prep_bwd",
    )(qkvn, qkvn, qkvn, beta, gc, gc_t, t_fold, u, w, du, dw, dqd, dkt, d_a)


def _gdr_scan_fwd(u, w, qd, kt, a_mat, gc):
    s_dim = u.shape[0]
    n_chunks = s_dim // CHUNK
    per = min(SCAN_CHUNKS_PER_STEP, n_chunks)
    tb = per * CHUNK

    def body(u_ref, w_ref, qd_ref, kt_ref, a_ref, g_ref, o_ref, st_ref, state):
        @pl.when(pl.program_id(0) == 0)
        def _():
            state[...] = jnp.zeros_like(state)

        heads = range(N_HEADS)
        cols = [slice(h * HEAD, (h + 1) * HEAD) for h in heads]
        for i in range(per):
            rows = slice(i * CHUNK, (i + 1) * CHUNK)
            egl = jnp.exp(g_ref[(i + 1) * CHUNK - 1:(i + 1) * CHUNK, :])
            s_b = [state[h].astype(BF16) for h in heads]
            for h in heads:
                st_ref[i, h] = state[h]
            ws = [_dot(w_ref[rows, cs], s) for cs, s in zip(cols, s_b)]
            qs = [_dot(qd_ref[rows, cs], s) for cs, s in zip(cols, s_b)]
            vns = [(u_ref[rows, cs] - ws_h).astype(BF16) for cs, ws_h in zip(cols, ws)]
            avs = [_dot(a_ref[h, rows, :], vn) for h, vn in zip(heads, vns)]
            kvs = [_dot(kt_ref[rows, cs], vn, TN) for cs, vn in zip(cols, vns)]
            for h, cs in zip(heads, cols):
                o_ref[rows, cs] = qs[h] + avs[h]
                state[h] = state[h] * egl[:, h:h + 1] + kvs[h]

    wide = pl.BlockSpec((tb, N_HEADS * HEAD), lambda n: (n, 0))
    return pl.pallas_call(
        body,
        out_shape=[jax.ShapeDtypeStruct((s_dim, N_HEADS * HEAD), F32),
                   jax.ShapeDtypeStruct((n_chunks, N_HEADS, HEAD, HEAD), F32)],
        grid=(n_chunks // per,),
        in_specs=[wide, wide, wide, wide, pl.BlockSpec((N_HEADS, tb, CHUNK), lambda n: (0, n, 0)),
                  pl.BlockSpec((tb, LANES), lambda n: (n, 0))],
        out_specs=[wide, pl.BlockSpec((per, N_HEADS, HEAD, HEAD), lambda n: (n, 0, 0, 0))],
        scratch_shapes=[pltpu.VMEM((N_HEADS, HEAD, HEAD), F32)],
        compiler_params=pltpu.CompilerParams(dimension_semantics=("arbitrary",)),
        name="gdr_scan_fwd",
    )(u, w, qd, kt, a_mat, gc)


def _gdr_scan_bwd(u, w, qd, kt, a_mat, gc, states, d_o):
    s_dim = u.shape[0]
    n_chunks = s_dim // CHUNK
    per = min(SCAN_CHUNKS_PER_STEP, n_chunks)
    tb = per * CHUNK
    last = n_chunks // per - 1

    def body(u_ref, w_ref, qd_ref, kt_ref, a_ref, g_ref, st_ref, do_ref,
             du_ref, dw_ref, dqd_ref, dkt_ref, da_ref, de_ref, d_state):
        @pl.when(pl.program_id(0) == 0)
        def _():
            d_state[...] = jnp.zeros_like(d_state)

        heads = range(N_HEADS)
        cols = [slice(h * HEAD, (h + 1) * HEAD) for h in heads]
        for i in reversed(range(per)):
            rows = slice(i * CHUNK, (i + 1) * CHUNK)
            egl = jnp.exp(g_ref[(i + 1) * CHUNK - 1:(i + 1) * CHUNK, :])
            s_b = [st_ref[i, h].astype(BF16) for h in heads]
            ds_b = [d_state[h].astype(BF16) for h in heads]
            dos = [do_ref[rows, cs].astype(BF16) for cs in cols]
            w_b = [w_ref[rows, cs].astype(BF16) for cs in cols]
            ws = [_dot(w_h, s) for w_h, s in zip(w_b, s_b)]
            ados = [_dot(a_ref[h, rows, :], do, TN) for h, do in zip(heads, dos)]
            kds = [_dot(kt_ref[rows, cs], ds) for cs, ds in zip(cols, ds_b)]
            dqds = [_dot(do, s, NT) for do, s in zip(dos, s_b)]
            qdos = [_dot(qd_ref[rows, cs], do, TN) for cs, do in zip(cols, dos)]
            vns = [(u_ref[rows, cs] - ws_h).astype(BF16) for cs, ws_h in zip(cols, ws)]
            dvns = [a + k_ for a, k_ in zip(ados, kds)]
            dvn_b = [d.astype(BF16) for d in dvns]
            das = [_dot(do, vn, NT) for do, vn in zip(dos, vns)]
            dkts = [_dot(vn, ds, NT) for vn, ds in zip(vns, ds_b)]
            dws = [_dot(d, s, NT) for d, s in zip(dvn_b, s_b)]
            wds = [_dot(w_h, d, TN) for w_h, d in zip(w_b, dvn_b)]
            for h, cs in zip(heads, cols):
                ds_n = d_state[h]
                de = jnp.sum(_rowsum(ds_n * st_ref[i, h]), axis=0, keepdims=True)
                de_ref[i, h:h + 1, :] = jnp.broadcast_to(de, (1, LANES))
                dqd_ref[rows, cs] = dqds[h]
                da_ref[h, rows, :] = das[h]
                dkt_ref[rows, cs] = dkts[h]
                du_ref[rows, cs] = dvns[h]
                dw_ref[rows, cs] = -dws[h]
                d_state[h] = ds_n * egl[:, h:h + 1] + qdos[h] - wds[h]

    wide = pl.BlockSpec((tb, N_HEADS * HEAD), lambda n: (last - n, 0))
    a_spec = pl.BlockSpec((N_HEADS, tb, CHUNK), lambda n: (0, last - n, 0))
    wide_shape = jax.ShapeDtypeStruct((s_dim, N_HEADS * HEAD), F32)
    return pl.pallas_call(
        body,
        out_shape=[wide_shape, wide_shape, wide_shape, wide_shape,
                   jax.ShapeDtypeStruct((N_HEADS, s_dim, CHUNK), F32),
                   jax.ShapeDtypeStruct((n_chunks, N_HEADS, LANES), F32)],
        grid=(n_chunks // per,),
        in_specs=[wide, wide, wide, wide, a_spec, pl.BlockSpec((tb, LANES), lambda n: (last - n, 0)),
                  pl.BlockSpec((per, N_HEADS, HEAD, HEAD), lambda n: (last - n, 0, 0, 0)), wide],
        out_specs=[wide, wide, wide, wide, a_spec, pl.BlockSpec((per, N_HEADS, LANES), lambda n: (last - n, 0, 0))],
        scratch_shapes=[pltpu.VMEM((N_HEADS, HEAD, HEAD), F32)],
        compiler_params=pltpu.CompilerParams(dimension_semantics=("arbitrary",)),
        name="gdr_scan_bwd",
    )(u, w, qd, kt, a_mat, gc, states, d_o)


FUSED_ROWS = 512


def _gdr_out_fwd(o_dn, proj_a, dn_w, w_br):
    def fn(r, c):
        o, z = r
        w_, w_br_ = c
        outs = []
        for h in range(N_HEADS):
            cs = slice(h * HEAD, (h + 1) * HEAD)
            oh, zh = o[:, cs], z[:, cs]
            rr = lax.rsqrt(_rowmean(oh * oh) + EPS_RMS)
            outs.append(oh * rr * w_ * (zh * _sig(zh)))
        og = jnp.concatenate(outs, axis=1).astype(BF16)
        return [og, _dot(og, w_br_)], []

    return _rowwise(fn, [o_dn, (proj_a, 3, D_MODEL)], [dn_w, w_br], [(D_MODEL, BF16), (D_MODEL, BF16)],
                    tm=FUSED_ROWS, name="gdr_out_fwd")


def _gdr_out_bwd(o_dn, proj_a, d_y_dn, dn_w, w_br):
    def fn(r, c):
        o, z, dy = r
        w_, w_br_ = c
        dg = _dot(dy, w_br_, NT)
        d_o, d_z = [], []
        d_w = jnp.zeros((1, HEAD), F32)
        for h in range(N_HEADS):
            cs = slice(h * HEAD, (h + 1) * HEAD)
            oh, zh, dgh = o[:, cs], z[:, cs], dg[:, cs]
            rr = lax.rsqrt(_rowmean(oh * oh) + EPS_RMS)
            sz = zh * _sig(zh)
            d_n = dgh * sz
            d_z.append(dgh * (oh * rr * w_) * _silu_grad(zh))
            d_w = d_w + _colsum(d_n * oh * rr)
            gw = d_n * w_
            d_o.append(rr * gw - oh * (rr * rr * rr) * _rowmean(gw * oh))
        return [jnp.concatenate(d_o, axis=1), jnp.concatenate(d_z, axis=1)], [d_w]

    return _rowwise(fn, [o_dn, (proj_a, 3, D_MODEL), d_y_dn], [dn_w, w_br], [(D_MODEL, F32), (D_MODEL, BF16)],
                    accs=[(1, HEAD)], tm=FUSED_ROWS, name="gdr_out_bwd")


def _rms_fwd(x, w):
    r = lax.rsqrt(_rowmean(x * x) + EPS_RMS)
    return x * r * w


def _rms_bwd(x, w, dy):
    r = lax.rsqrt(_rowmean(x * x) + EPS_RMS)
    gw = dy * w
    return r * gw - x * (r * r * r) * _rowmean(gw * x), _colsum(dy * x * r)


def _rope_consts():
    inv = ROPE_BASE ** (-np.arange(0, ROPE, 2, dtype=np.float32) / ROPE)
    t = np.zeros((4, LANES), np.float32)
    t[0, :32] = inv
    t[0, 32:64] = inv
    t[1, :64] = 1.0
    t[2, 32:64] = 1.0
    t[3, :32] = -1.0
    return jnp.asarray(t)


def _rope_tables(pos, consts, width):
    ang = pos * consts[0:1, :]
    cosv, sinv = jnp.cos(ang), jnp.sin(ang)
    reps = width // LANES
    tile = (lambda t: jnp.concatenate([t] * reps, axis=1)) if reps > 1 else (lambda t: t)
    return tile(cosv * consts[1:2, :]), tile(sinv * consts[2:3, :]), tile(sinv * consts[3:4, :])


def _rope_apply(t, tabs):
    cos_t, sin_a, sin_b = tabs
    width = t.shape[1]
    return t * cos_t + pltpu.roll(t, 32, 1) * sin_a + pltpu.roll(t, width - 32, 1) * sin_b


def _rope_transpose(d, tabs):
    cos_t, sin_a, sin_b = tabs
    width = d.shape[1]
    return d * cos_t + pltpu.roll(d * sin_a, width - 32, 1) + pltpu.roll(d * sin_b, 32, 1)


QK_HEAD = 2 * HEAD


def _interleave_heads(a, b):
    parts = []
    for h in range(N_HEADS):
        parts.append(a[:, h * HEAD:(h + 1) * HEAD])
        parts.append(b if b.shape[1] == LANES else b[:, h * LANES:(h + 1) * LANES])
    return jnp.concatenate(parts, axis=1)


def _mla_rows(proj_b):
    return [(proj_b, WB_CQ // Q_LORA, Q_LORA), (proj_b, WB_CKV // KV_LORA, KV_LORA), (proj_b, WB_KR // LANES, LANES)]


def _mla_prep_fwd(proj_b, pos, qn_w, kvn_w, uq, uk, uv):
    def fn(r, c):
        cq, ckv, kr, pos_ = r
        qn_w_, kvn_w_, uq_, uk_, uv_, rope = c
        c_q = _rms_fwd(cq, qn_w_).astype(BF16)
        c_kv = _rms_fwd(ckv, kvn_w_).astype(BF16)
        qf = _dot(c_q, uq_)
        qr = _rope_apply(qf[:, D_MODEL:], _rope_tables(pos_, rope, D_MODEL))
        kr = _rope_apply(kr, _rope_tables(pos_, rope, LANES))
        kc = _interleave_heads(_dot(c_kv, uk_), kr)
        v = _dot(c_kv, uv_)
        return [c_q, c_kv, _interleave_heads(qf[:, :D_MODEL], qr) * SCALE, kc, v, kc, v], []

    wide2 = N_HEADS * QK_HEAD
    return _rowwise(fn, _mla_rows(proj_b) + [pos], [qn_w, kvn_w, uq, uk, uv, _rope_consts()],
                    [(Q_LORA, BF16), (KV_LORA, BF16), (wide2, BF16), (wide2, BF16), (D_MODEL, BF16),
                     (wide2, BF16, "T"), (D_MODEL, BF16, "T")], tm=FUSED_ROWS, name="mla_prep_fwd")


def _mla_prep_bwd(proj_b, pos, d_qc, d_kc, d_v, qn_w, kvn_w, uq, uk, uv):
    def fn(r, c):
        cq, ckv, _, pos_, dq, dk, dv = r
        qn_w_, kvn_w_, uq_, uk_, uv_, rope = c
        even = lambda t: jnp.concatenate([t[:, (2 * h) * LANES:(2 * h + 1) * LANES] for h in range(N_HEADS)], axis=1)
        odd = lambda t: jnp.concatenate([t[:, (2 * h + 1) * LANES:(2 * h + 2) * LANES] for h in range(N_HEADS)], axis=1)
        d_qr_raw = _rope_transpose(odd(dq), _rope_tables(pos_, rope, D_MODEL)) * SCALE
        d_qf = jnp.concatenate([even(dq) * SCALE, d_qr_raw], axis=1).astype(BF16)
        d_kn = even(dk).astype(BF16)
        dkr = dk[:, LANES:2 * LANES]
        for h in range(1, N_HEADS):
            dkr = dkr + dk[:, (2 * h + 1) * LANES:(2 * h + 2) * LANES]
        d_cq, d_qnw = _rms_bwd(cq, qn_w_, _dot(d_qf, uq_, NT))
        d_ckv, d_kvnw = _rms_bwd(ckv, kvn_w_, _dot(d_kn, uk_, NT) + _dot(dv, uv_, NT))
        return [d_qf, d_kn, d_cq, d_ckv, _rope_transpose(dkr, _rope_tables(pos_, rope, LANES))], [d_qnw, d_kvnw]

    return _rowwise(fn, _mla_rows(proj_b) + [pos, d_qc, d_kc, d_v], [qn_w, kvn_w, uq, uk, uv, _rope_consts()],
                    [(2 * D_MODEL, BF16), (D_MODEL, BF16), (Q_LORA, BF16), (KV_LORA, BF16), (LANES, BF16)],
                    accs=[(1, Q_LORA), (1, KV_LORA)], tm=FUSED_ROWS, name="mla_prep_bwd")


def _causal_mask_t(st, key0, query0):
    key = lax.broadcasted_iota(jnp.int32, st.shape, 0) + key0
    query = lax.broadcasted_iota(jnp.int32, st.shape, 1) + query0
    return jnp.where(key <= query, st, NEG_BIG)


def _attn_tiles(s_dim):
    tq = min(512, s_dim)
    n_chains = 2 if s_dim >= 2 * tq else 1
    return tq, n_chains, min(512, s_dim)


def _diagonal_chains(t, tq, n_chains, tk):
    return [(c, (t + 1) * tk - 1 > c * tq) for c in range(n_chains) if t * tk < (c + 1) * tq]


def _attn_fwd(qc, kc, vt):
    s_dim = qc.shape[0]
    tq, n_chains, tk = _attn_tiles(s_dim)
    tqs = tq * n_chains

    def body(q_ref, k_ref, vt_ref, o_ref, lse_ref, m_s, l_s, acc):
        qi = pl.program_id(1)
        m_s[...] = jnp.full_like(m_s, NEG_BIG)
        l_s[...] = jnp.zeros_like(l_s)
        acc[...] = jnp.zeros_like(acc)

        def make_step(chains):
            def step(j, carry):
                ks = pl.multiple_of(j * tk, tk)
                kb, vtb = k_ref[pl.ds(ks, tk), :], vt_ref[:, pl.ds(ks, tk)]
                cols = [slice(c * tq, (c + 1) * tq) for c, _ in chains]
                sts = [_dot(kb, q_ref[cs, :], NT) for cs in cols]
                sts = [_causal_mask_t(st, j * tk, qi * tqs + c * tq) if masked else st
                       for st, (c, masked) in zip(sts, chains)]
                m_prevs = [m_s[:, cs] for cs in cols]
                m_news = [jnp.maximum(mp, jnp.max(st, axis=0, keepdims=True)) for mp, st in zip(m_prevs, sts)]
                alphas = [jnp.exp(mp - mn) for mp, mn in zip(m_prevs, m_news)]
                pts = [jnp.exp(st - mn) for st, mn in zip(sts, m_news)]
                pvs = [_dot(vtb, pt) for pt in pts]
                for cs, mn, al, pt, pv in zip(cols, m_news, alphas, pts, pvs):
                    l_s[:, cs] = al * l_s[:, cs] + _colsum(pt)
                    m_s[:, cs] = mn
                    acc[:, cs] = acc[:, cs] * al + pv
                return carry
            return step

        below = qi * (tqs // tk)
        lax.fori_loop(0, below, make_step([(c, False) for c in range(n_chains)]), 0)
        for t in range(tqs // tk):
            make_step(_diagonal_chains(t, tq, n_chains, tk))(below + t, 0)
        l = l_s[...]
        o_ref[...] = jnp.transpose(acc[...] / l)
        lse_ref[...] = m_s[...] + jnp.log(l)

    return pl.pallas_call(
        body,
        out_shape=[jax.ShapeDtypeStruct((s_dim, N_HEADS * HEAD), F32), jax.ShapeDtypeStruct((N_HEADS, 1, s_dim), F32)],
        grid=(N_HEADS, s_dim // tqs),
        in_specs=[pl.BlockSpec((tqs, QK_HEAD), lambda h, qi: (qi, h)),
                  pl.BlockSpec((s_dim, QK_HEAD), lambda h, qi: (0, h)),
                  pl.BlockSpec((HEAD, s_dim), lambda h, qi: (h, 0))],
        out_specs=[pl.BlockSpec((tqs, HEAD), lambda h, qi: (qi, h)),
                   pl.BlockSpec((None, 1, tqs), lambda h, qi: (h, 0, qi))],
        scratch_shapes=[pltpu.VMEM((1, tqs), F32), pltpu.VMEM((1, tqs), F32), pltpu.VMEM((HEAD, tqs), F32)],
        compiler_params=pltpu.CompilerParams(dimension_semantics=("parallel", "parallel")),
        name="attn_fwd",
    )(qc, kc, vt)


def _attn_bwd(qc, kc, kct, v, o, d_o, lse):
    s_dim = qc.shape[0]
    tq, n_chains, tk = _attn_tiles(s_dim)
    tqs = tq * n_chains

    def body(q_ref, k_ref, kt_ref, v_ref, o_ref, do_ref, lse_ref, dq_ref, dk_ref, dv_ref, dqt_acc, dv_acc):
        qi = pl.program_id(1)

        @pl.when(qi == 0)
        def _():
            dk_ref[...] = jnp.zeros_like(dk_ref)
            dv_acc[...] = jnp.zeros_like(dv_acc)

        dqt_acc[...] = jnp.zeros_like(dqt_acc)
        do_f = do_ref[...]
        do_all = do_f.astype(BF16)
        q_all = q_ref[...]
        lse_row = lse_ref[...]
        delta_row = _dot3(jnp.ones((8, HEAD), F32), o_ref[...] * do_f, NT)[0:1, :]

        def make_step(chains):
            rows = slice(chains[0][0] * tq, (chains[-1][0] + 1) * tq)

            def step(j, carry):
                ks = pl.multiple_of(j * tk, tk)
                kb, vb, ktb = k_ref[pl.ds(ks, tk), :], v_ref[pl.ds(ks, tk), :], kt_ref[:, pl.ds(ks, tk)]
                cols = [slice(c * tq, (c + 1) * tq) for c, _ in chains]
                sts = [_dot(kb, q_all[cs, :], NT) for cs in cols]
                sts = [_causal_mask_t(st, j * tk, qi * tqs + c * tq) if masked else st
                       for st, (c, masked) in zip(sts, chains)]
                dpts = [_dot(vb, do_all[cs, :], NT) for cs in cols]
                pts = [jnp.exp(st - lse_row[:, cs]) for st, cs in zip(sts, cols)]
                dsts = [(pt * (dpt - delta_row[:, cs])).astype(BF16) for pt, dpt, cs in zip(pts, dpts, cols)]
                pts = [pt.astype(BF16) for pt in pts]
                dqs = [_dot(ktb, dst) for dst in dsts]
                for cs, dq in zip(cols, dqs):
                    dqt_acc[:, cs] += dq
                pt_all = jnp.concatenate(pts, axis=1) if len(chains) > 1 else pts[0]
                dst_all = jnp.concatenate(dsts, axis=1) if len(chains) > 1 else dsts[0]
                dk_ref[pl.ds(ks, tk), :] += _dot(dst_all, q_all[rows, :])
                dv_acc[pl.ds(ks, tk), :] += _dot(pt_all, do_all[rows, :])
                return carry
            return step

        below = qi * (tqs // tk)
        lax.fori_loop(0, below, make_step([(c, False) for c in range(n_chains)]), 0)
        for t in range(tqs // tk):
            make_step(_diagonal_chains(t, tq, n_chains, tk))(below + t, 0)
        dq_ref[...] = jnp.transpose(dqt_acc[...])

        @pl.when(qi == s_dim // tqs - 1)
        def _():
            dv_ref[...] = dv_acc[...].astype(dv_ref.dtype)

    q_spec = pl.BlockSpec((tqs, QK_HEAD), lambda h, qi: (qi, h))
    o_spec = pl.BlockSpec((tqs, HEAD), lambda h, qi: (qi, h))
    k_spec = pl.BlockSpec((s_dim, QK_HEAD), lambda h, qi: (0, h))
    v_spec = pl.BlockSpec((s_dim, HEAD), lambda h, qi: (0, h))
    wide2 = jax.ShapeDtypeStruct((s_dim, N_HEADS * QK_HEAD), F32)
    return pl.pallas_call(
        body,
        out_shape=[wide2, wide2, jax.ShapeDtypeStruct((s_dim, N_HEADS * HEAD), BF16)],
        grid=(N_HEADS, s_dim // tqs),
        in_specs=[q_spec, k_spec, pl.BlockSpec((QK_HEAD, s_dim), lambda h, qi: (h, 0)), v_spec, o_spec, o_spec,
                  pl.BlockSpec((None, 1, tqs), lambda h, qi: (h, 0, qi))],
        out_specs=[q_spec, k_spec, v_spec],
        scratch_shapes=[pltpu.VMEM((QK_HEAD, tqs), F32), pltpu.VMEM((s_dim, HEAD), F32)],
        compiler_params=pltpu.CompilerParams(dimension_semantics=("parallel", "arbitrary")),
        name="attn_bwd",
    )(qc, kc, kct, v, o, d_o, lse)


def _mix_proj_ln1(y_dn, y_mla, proj_g, x, w_o, g, b):
    s_dim = x.shape[0]
    tm = min(512, s_dim)

    def body(yd_ref, ym_ref, g_ref, x_ref, w_ref, lg_ref, lb_ref, mixed_ref, a1_ref, h1_ref, h1b_ref):
        gates = g_ref[...].astype(F32)
        mixed = (_sig(gates[:, :D_MODEL]) * yd_ref[...].astype(F32)
                 + _sig(gates[:, D_MODEL:]) * ym_ref[...].astype(F32)).astype(BF16)
        a1 = _dot(mixed, w_ref[...])
        xh, _ = _ln_stats(ALPHA * x_ref[...] + a1)
        y = xh * lg_ref[...] + lb_ref[...]
        mixed_ref[...] = mixed
        a1_ref[...] = a1
        h1_ref[...] = y
        h1b_ref[...] = y.astype(BF16)

    row = lambda width: pl.BlockSpec((tm, width), lambda i: (i, 0))
    whole = lambda a: pl.BlockSpec(a.shape, lambda i: (0, 0))
    sds = lambda dt: jax.ShapeDtypeStruct((s_dim, D_MODEL), dt)
    return pl.pallas_call(
        body,
        out_shape=[sds(BF16), sds(F32), sds(F32), sds(BF16)],
        grid=(s_dim // tm,),
        in_specs=[row(D_MODEL), row(D_MODEL), row(2 * D_MODEL), row(D_MODEL), whole(w_o), whole(g), whole(b)],
        out_specs=[row(D_MODEL)] * 4,
        compiler_params=pltpu.CompilerParams(dimension_semantics=("parallel",)),
        name="mix_proj_ln1",
    )(y_dn, y_mla, proj_g, x, w_o, g, b)


def _ln1_mix_bwd(x, a1, d_h1, d_pg, y_dn, y_mla, proj_g, g, w_o, w_pg):
    def fn(r, c):
        x_, a1_, dy, dpg, yd, ym, gates = r
        g_, w_o_, w_pg_ = c
        dy = dy + _dot(dpg, w_pg_, NT)
        xh, rr = _ln_stats(ALPHA * x_ + a1_)
        dz = _ln_bwd(dy, xh, rr, g_)
        dz_b = dz.astype(BF16)
        dm = _dot(dz_b, w_o_, NT)
        sd, sm = _sig(gates[:, :D_MODEL]), _sig(gates[:, D_MODEL:])
        d_g = jnp.concatenate([dm * yd * sd * (1.0 - sd), dm * ym * sm * (1.0 - sm)], axis=1)
        return [dz_b, ALPHA * dz, d_g, dm * sd, dm * sm], [_colsum(dy * xh), _colsum(dy)]

    return _rowwise(fn, [x, a1, d_h1, d_pg, y_dn, y_mla, proj_g], [g, w_o, w_pg],
                    [(D_MODEL, BF16), (D_MODEL, F32), (2 * D_MODEL, BF16), (D_MODEL, BF16), (D_MODEL, BF16)],
                    accs=[(1, D_MODEL), (1, D_MODEL)], tm=FUSED_ROWS, name="ln1_mix_bwd")


def _ln_stats(z):
    mu = _rowmean(z)
    zc = z - mu
    r = lax.rsqrt(_rowmean(zc * zc) + EPS_LN)
    return zc * r, r


def _ln_bwd(dy, xh, r, g):
    dxh = dy * g
    return r * (dxh - _rowmean(dxh) - xh * _rowmean(dxh * xh))


def _ffn_in_act(h1b, w_t):
    s_dim, k_dim = h1b.shape
    hidden = w_t.shape[0] // 2
    tm, tn = min(512, s_dim), _pick_wide(hidden)
    nt = hidden // tn

    def body(a_ref, bg_ref, bu_ref, gt_ref, up_ref, act_ref):
        a = a_ref[...]
        gt, up = _dot(a, bg_ref[...], NT), _dot(a, bu_ref[...], NT)
        gt_ref[...] = gt.astype(BF16)
        up_ref[...] = up.astype(BF16)
        act_ref[...] = (gt * _sig(gt) * up).astype(BF16)

    o_spec = pl.BlockSpec((tm, tn), lambda j, i: (i, j))
    sds = jax.ShapeDtypeStruct((s_dim, hidden), BF16)
    return pl.pallas_call(
        body,
        out_shape=[sds, sds, sds],
        grid=(nt, s_dim // tm),
        in_specs=[pl.BlockSpec((tm, k_dim), lambda j, i: (i, 0)), pl.BlockSpec((tn, k_dim), lambda j, i: (j, 0)),
                  pl.BlockSpec((tn, k_dim), lambda j, i: (j + nt, 0))],
        out_specs=[o_spec, o_spec, o_spec],
        compiler_params=pltpu.CompilerParams(dimension_semantics=("parallel", "parallel")),
        name="ffn_in_act",
    )(h1b, w_t, w_t)


def _act_bwd(gt, up, d_act):
    def fn(r, c):
        gt_, up_, da = r
        return [jnp.concatenate([da * up_ * _silu_grad(gt_), da * gt_ * _sig(gt_)], axis=1)], []

    return _rowwise(fn, [gt, up, d_act], [], [(2 * FFN_HIDDEN, BF16)], name="act_bwd")[0]


def _tail(h1, ffn, p, tgt, g, b, w_pg, w_ple_t):
    def fn(r, c):
        h1_, ffn_, p_, t_ = r
        pg_ = _dot(h1_, c[2])
        pp_ = _dot(p_, c[3], NT)
        sp = _sig(pg_)
        xh, rr = _ln_stats(ALPHA * h1_ + ffn_ + sp * pp_)
        y = xh * c[0] + c[1]
        err = y - t_
        dy = err * (1.0 / D_MODEL)
        dz = _ln_bwd(dy, xh, rr, c[0])
        loss = jnp.sum(0.5 * _rowmean(err * err), axis=0, keepdims=True)
        return ([dz, dz * pp_ * sp * (1.0 - sp), dz * sp, ALPHA * dz],
                [_colsum(dy * xh), _colsum(dy), jnp.broadcast_to(loss, (1, LANES))])

    return _rowwise(fn, [h1, ffn, p, tgt], [g, b, w_pg, w_ple_t], [(D_MODEL, BF16)] * 3 + [(D_MODEL, F32)],
                    accs=[(1, D_MODEL), (1, D_MODEL), (1, LANES)], tm=FUSED_ROWS, name="tail")


def _local_step(x, p, pos, tgt, w, late_weights, emit):
    w = dict(w)
    s_dim = x.shape[0]
    xb, pb = x.astype(BF16), p.astype(BF16)
    proj_a = _mm(xb, w["w_in_t"], tb=True, b_rows=(0, 4 * D_MODEL), name="f_proj_a")
    proj_g = _mm(xb, w["wg_t"], tb=True, out_dtype=BF16, name="f_proj_g")
    proj_b = _mm(xb, w["wb_t"], tb=True, name="f_proj_b")
    qkvn = _conv_fwd(proj_a, w["conv"])
    beta, gc = _gates_fwd(proj_b, w["alog"], w["dtb"])
    gc_t = jnp.transpose(gc[:, :N_HEADS])
    u, w_, qd, kt, a_mat, t_fold = _gdr_prep_fwd(qkvn, beta, gc, gc_t)
    o_dn, states = _gdr_scan_fwd(u, w_, qd, kt, a_mat, gc)
    w.update(late_weights("mix", o_dn))
    og, y_dn = _gdr_out_fwd(o_dn, proj_a, w["dnw"], w["br_dn"])
    c_q, c_kv, qc, kc, vv, kct, vt = _mla_prep_fwd(proj_b, pos, w["qnw"], w["kvnw"], w["uq"], w["uk"], w["uv"])
    o_mla, lse = _attn_fwd(qc, kc, vt)
    y_mla = _mm(o_mla, w["br_mla"], out_dtype=BF16, name="f_y_mla")
    mixed, a1, h1, h1b = _mix_proj_ln1(y_dn, y_mla, proj_g, x, w["wo"], w["ln1g"], w["ln1b"])
    w.update(late_weights("ffn", a1))
    gt, up, act = _ffn_in_act(h1b, w["ffn_in_t"])
    ffn = _mm(act, w["ffn_out"], name="f_ffn")
    g = {}
    dz2, d_pg, d_pp, dh1a, g["ln2g"], g["ln2b"], loss = _tail(h1, ffn, pb, tgt, w["ln2g"], w["ln2b"],
                                                            w["ple_gate"], w["ple_t"])
    g["ple_t"] = _mm(d_pp, pb, ta=True, out_dtype=BF16, name="b_w_ple")
    g["ple_gate"] = _mm(h1b, d_pg, ta=True, out_dtype=BF16, name="b_w_ple_gate")
    g["ffn_out"] = _mm(act, dz2, ta=True, out_dtype=BF16, name="b_w_ffn_out")
    d_act = _mm(dz2, w["ffn_out"], tb=True, out_dtype=BF16, name="b_act")
    d_gu = _act_bwd(gt, up, d_act)
    g["ffn_in_t"] = _mm(d_gu, h1b, ta=True, out_dtype=BF16, name="b_w_ffn_in")
    d_gu = emit("ffn", g, d_gu)
    d_h1 = _mm(d_gu, w["ffn_in_t"], add=(dh1a,), name="b_h1_ffn")
    dz1, dxa, d_proj_g, d_y_dn, d_y_mla, g["ln1g"], g["ln1b"] = _ln1_mix_bwd(
        x, a1, d_h1, d_pg, y_dn, y_mla, proj_g, w["ln1g"], w["wo"], w["ple_gate"])
    g["wo"] = _mm(mixed, dz1, ta=True, out_dtype=BF16, name="b_w_o")
    g["br_mla"] = _mm(o_mla, d_y_mla, ta=True, out_dtype=BF16, name="b_w_br_mla")
    d_o_mla = _mm(d_y_mla, w["br_mla"], tb=True, out_dtype=BF16, name="b_o_mla")
    d_qc, d_kc, d_v = _attn_bwd(qc, kc, kct, vv, o_mla, d_o_mla, lse)
    d_q_full, d_kn, d_cq, d_ckv, d_kr, g["qnw"], g["kvnw"] = _mla_prep_bwd(
        proj_b, pos, d_qc, d_kc, d_v, w["qnw"], w["kvnw"], w["uq"], w["uk"], w["uv"])
    g["uq"] = _mm(c_q, d_q_full, ta=True, out_dtype=BF16, name="b_w_uq")
    g["uk"] = _mm(c_kv, d_kn, ta=True, out_dtype=BF16, name="b_w_uk")
    g["uv"] = _mm(c_kv, d_v, ta=True, out_dtype=BF16, name="b_w_uv")
    g["br_dn"] = _mm(og, d_y_dn, ta=True, out_dtype=BF16, name="b_w_br_dn")
    d_y_dn = emit("mix", g, d_y_dn)
    d_o_dn, d_z, g["dnw"] = _gdr_out_bwd(o_dn, proj_a, d_y_dn, w["dnw"], w["br_dn"])
    du, dw, dqd, dkt, d_a, d_egl = _gdr_scan_bwd(u, w_, qd, kt, a_mat, gc, states, d_o_dn)
    dq, dk, dv, d_beta, d_gc = _gdr_prep_bwd(qkvn, beta, gc, gc_t, t_fold, u, w_, du, dw, dqd, dkt, d_a)
    d_egl_rows = jnp.pad(d_egl[:, None, :, 0], ((0, 0), (CHUNK - 1, 0), (0, LANES - N_HEADS))).reshape(s_dim, LANES)
    d_ba, g["alog"], g["dtb"] = _gates_bwd(proj_b, w["alog"], w["dtb"], gc, d_beta, d_gc, d_egl_rows)
    d_qkv, g["conv"] = _conv_bwd(proj_a, w["conv"], dq, dk, dv)
    zeros = jnp.zeros((s_dim, WB_CKV - Q_LORA), BF16)
    d_proj_b = jnp.concatenate([d_cq, zeros, d_ckv, d_kr, d_ba], axis=1)
    g["wa_qkv_t"] = _mm(d_qkv, xb, ta=True, name="b_w_qkv")
    g["wa_z_t"] = _mm(d_z, xb, ta=True, name="b_w_z")
    g["wg_t"] = _mm(d_proj_g, xb, ta=True, name="b_w_g")
    g["wb_t"] = _mm(d_proj_b, xb, ta=True, name="b_w_b")
    d_qkv = emit("small", dict(g, loss=loss), emit("w_in", g, d_qkv))
    dx = _mm(d_qkv, w["w_in_t"], add=(dxa,), b_rows=(0, 3 * D_MODEL), name="b_x_qkv")
    dx = _mm(d_z, w["w_in_t"], add=(dx,), b_rows=(3 * D_MODEL, D_MODEL), name="b_x_z")
    dx = _mm(d_proj_g, w["wg_t"], add=(dx,), name="b_x_g")
    dx = _mm(d_proj_b, w["wb_t"], add=(dx,), name="b_x_b")
    return loss, dx, g


_BIG = (("w_in", 1), ("w_uq", 0), ("w_uk", 0), ("w_uv", 0), ("w_br_dn", 0), ("w_br_mla", 0),
        ("w_o", 0), ("w_ffn_in", 1), ("w_ffn_out", 0), ("w_ple", 1), ("w_ple_gate", 0))
_BIG_AXIS = dict(_BIG)
_SMALL = ("ln1_g", "ln1_b", "ln2_g", "ln2_b", "q_norm_w", "kv_norm_w", "dn_norm_w", "dn_a_log", "dn_dt_bias")
_ORDER = ("w_in", "conv_w", "dn_a_log", "dn_dt_bias", "dn_norm_w", "q_norm_w", "w_uq", "kv_norm_w", "w_uk", "w_uv",
          "w_br_dn", "w_br_mla", "w_o", "ln1_g", "ln1_b", "w_ffn_in", "w_ffn_out", "w_ple", "w_ple_gate", "ln2_g",
          "ln2_b")


def _stored_shape(name, shard_shape):
    axis = _BIG_AXIS[name]
    lead = shard_shape[axis]
    return lead, int(np.prod(shard_shape)) // lead


def _to_stored(name, shard):
    return jnp.moveaxis(shard, _BIG_AXIS[name], 0).reshape(_stored_shape(name, shard.shape))


def _from_stored(name, stored, shard_shape):
    axis = _BIG_AXIS[name]
    moved = (shard_shape[axis],) + shard_shape[:axis] + shard_shape[axis + 1:]
    return jnp.moveaxis(stored.reshape(moved), 0, axis)


_W_IN_ROWS = np.cumsum([0, 3072, 1024, 8, 8, Q_LORA, KV_LORA, ROPE, D_MODEL, D_MODEL])


def _first_weights(w_in_t, conv_full, small):
    r = _W_IN_ROWS
    zr = lambda n: jnp.zeros((n, D_MODEL), w_in_t.dtype)
    w = {}
    w["w_in_t"] = w_in_t
    w["wg_t"] = w_in_t[r[7]:r[9]]
    w["wb_t"] = jnp.concatenate([w_in_t[r[4]:r[5]], zr(WB_CKV - Q_LORA), w_in_t[r[5]:r[7]], zr(LANES - ROPE),
                                 w_in_t[r[2]:r[4]], zr(LANES - 2 * N_HEADS)], axis=0)
    w["conv"] = conv_full
    pad_l = lambda v: jnp.pad(v, ((0, 0), (0, LANES - v.shape[1])))
    w["alog"], w["dtb"] = pad_l(small["dn_a_log"]), pad_l(small["dn_dt_bias"])
    w["dnw"], w["qnw"], w["kvnw"] = small["dn_norm_w"], small["q_norm_w"], small["kv_norm_w"]
    w["ln1g"], w["ln1b"], w["ln2g"], w["ln2b"] = small["ln1_g"], small["ln1_b"], small["ln2_g"], small["ln2_b"]
    return w


def _late_weights(group, fw):
    w = {}
    if group == "mix":
        uq = fw["w_uq"].reshape(Q_LORA, N_HEADS, HEAD + ROPE)
        uq_r = jnp.pad(uq[:, :, HEAD:], ((0, 0), (0, 0), (0, HEAD - ROPE)))
        w["uq"] = jnp.concatenate([uq[:, :, :HEAD].reshape(Q_LORA, -1), uq_r.reshape(Q_LORA, -1)], axis=1)
        w["uk"], w["uv"] = fw["w_uk"], fw["w_uv"]
        w["br_dn"], w["br_mla"], w["wo"] = fw["w_br_dn"], fw["w_br_mla"], fw["w_o"]
    else:
        w["ffn_in_t"], w["ffn_out"] = fw["w_ffn_in"], fw["w_ffn_out"]
        w["ple_t"], w["ple_gate"] = fw["w_ple"], fw["w_ple_gate"]
    return w


_GROUP_GRADS = {"ffn": (("w_ple", "ple_t"), ("w_ple_gate", "ple_gate"), ("w_ffn_out", "ffn_out"),
                        ("w_ffn_in", "ffn_in_t")),
                "mix": (("w_o", "wo"), ("w_br_mla", "br_mla"), ("w_uq", "uq"), ("w_uk", "uk"), ("w_uv", "uv"),
                        ("w_br_dn", "br_dn"))}


def _group_grads(group, g):
    out = {}
    for name, key in _GROUP_GRADS[group]:
        t = g[key]
        if name == "w_uq":
            uq_n = t[:, :D_MODEL].reshape(Q_LORA, N_HEADS, HEAD)
            uq_r = t[:, D_MODEL:].reshape(Q_LORA, N_HEADS, HEAD)[:, :, :ROPE]
            t = jnp.concatenate([uq_n, uq_r], axis=2).reshape(Q_LORA, -1)
        out[name] = t
    return out


PACK_ROWS = 512
SUBLANES = 8


def _pack_rows(parts, name):
    arrays = []
    for a, _, _ in parts:
        if not any(a is b for b in arrays):
            arrays.append(a)
    index = lambda a: next(i for i, b in enumerate(arrays) if a is b)
    chunks, dst = [], 0
    for a, first, rows in parts:
        assert first % SUBLANES == 0 and rows % SUBLANES == 0
        chunks += [(index(a), first + o, dst + o, min(PACK_ROWS, rows - o)) for o in range(0, rows, PACK_ROWS)]
        dst += rows
    c, n, last = arrays[0].shape[1], len(arrays), len(chunks) - 1

    def body(*refs):
        src_refs, out_ref, buf, sem_in, sem_out = refs[:n], refs[n], refs[n + 1], refs[n + 2], refs[n + 3]

        def load(k):
            i, first, _, rows = chunks[k]
            return pltpu.make_async_copy(src_refs[i].at[pl.ds(first, rows)], buf.at[k % 2, pl.ds(0, rows)],
                                         sem_in.at[k % 2])

        def store(k):
            _, _, first, rows = chunks[k]
            return pltpu.make_async_copy(buf.at[k % 2, pl.ds(0, rows)], out_ref.at[pl.ds(first, rows), 0, :],
                                         sem_out.at[k % 2])

        load(0).start()
        for k in range(last + 1):
            load(k).wait()
            store(k).start()
            if k >= 1:
                store(k - 1).wait()
            if k < last:
                load(k + 1).start()
        store(last).wait()

    return pl.pallas_call(
        body,
        out_shape=jax.ShapeDtypeStruct((dst, 1, c), F32),
        in_specs=[_ANY] * n,
        out_specs=_ANY,
        scratch_shapes=[pltpu.VMEM((2, PACK_ROWS, c), F32), pltpu.SemaphoreType.DMA((2,)),
                        pltpu.SemaphoreType.DMA((2,))],
        name=name,
    )(*arrays)


def _w_in_grad(g):
    wb = g["wb_t"]
    return _pack_rows([
        (g["wa_qkv_t"], 0, 3 * D_MODEL), (g["wa_z_t"], 0, D_MODEL), (wb, WB_BA, 2 * N_HEADS), (wb, WB_CQ, Q_LORA),
        (wb, WB_CKV, KV_LORA), (wb, WB_KR, ROPE), (g["wg_t"], 0, 2 * D_MODEL)], "pack_w_in")


def _small_grads(g):
    return {"ln1_g": g["ln1g"], "ln1_b": g["ln1b"], "ln2_g": g["ln2g"], "ln2_b": g["ln2b"], "q_norm_w": g["qnw"],
            "kv_norm_w": g["kvnw"], "dn_norm_w": g["dnw"], "dn_a_log": g["alog"], "dn_dt_bias": g["dtb"],
            "conv_w": g["conv"]}


_SMALL_SLOTS = {"ln1_g": (0, 0, 1024), "ln1_b": (1, 0, 1024), "ln2_g": (2, 0, 1024), "ln2_b": (3, 0, 1024),
                "q_norm_w": (4, 0, 384), "kv_norm_w": (4, 384, 256), "dn_norm_w": (4, 640, 128),
                "dn_a_log": (4, 768, 8), "dn_dt_bias": (4, 896, 8)}
_SMALL_ROWS, _LOSS_ROW, _CONV_ROW0, _CONV_ROWS = 24, 5, 8, 12


def _pack_small_grads(small_g, loss):
    zeros = lambda r, c: jnp.zeros((r, c), F32)
    row4 = jnp.concatenate([small_g["q_norm_w"], small_g["kv_norm_w"], small_g["dn_norm_w"], small_g["dn_a_log"],
                            small_g["dn_dt_bias"]], axis=1)
    row5 = jnp.concatenate([loss, zeros(1, FLAT_COLS - LANES)], axis=1)
    head = jnp.concatenate([small_g["ln1_g"], small_g["ln1_b"], small_g["ln2_g"], small_g["ln2_b"], row4, row5,
                            zeros(2, FLAT_COLS)], axis=0)
    conv = small_g["conv_w"].reshape(_CONV_ROWS, FLAT_COLS)
    return jnp.concatenate([head, conv, zeros(_SMALL_ROWS - _CONV_ROW0 - _CONV_ROWS, FLAT_COLS)], axis=0)


_MESH_ID = pl.DeviceIdType.MESH
_ANY = pl.BlockSpec(memory_space=pl.ANY)


def _all_gather(blocks, name):
    n = len(blocks)

    def body(*refs):
        x_refs, out_refs = refs[:n], refs[n:2 * n]
        send_sems, recv_sems, local_sems = refs[2 * n:]
        x, y, c = lax.axis_index("x"), lax.axis_index("y"), lax.axis_index("c")
        me, sibling = (x, y, c), (x, y, 1 - c)
        chips = [(1 - x, y), (x, 1 - y), (1 - x, 1 - y)]

        def slot(i, px, py, pc):
            return out_refs[i].at[4 * px + 2 * py + pc]

        def copy(i, k, origin, to, src=None):
            return pltpu.make_async_remote_copy(
                src_ref=slot(i, *origin) if src is None else src, dst_ref=slot(i, *origin),
                send_sem=send_sems.at[7 * i + k], recv_sem=recv_sems.at[7 * i + k], device_id=to,
                device_id_type=_MESH_ID)

        mine = [pltpu.make_async_copy(x_refs[i], slot(i, *me), local_sems.at[i]) for i in range(n)]
        first, passed = [], []
        for i in range(n):
            mine[i].start()
            first.append(copy(i, 0, me, sibling, src=x_refs[i]))
            first += [copy(i, 1 + j, me, (*chip, c), src=x_refs[i]) for j, chip in enumerate(chips)]
        for cp in first:
            cp.start()
        for i in range(n):
            for j, chip in enumerate(chips):
                copy(i, 1 + j, (*chip, c), me).wait_recv()
                passed.append(copy(i, 4 + j, (*chip, c), sibling))
                passed[-1].start()
        for i in range(n):
            copy(i, 0, sibling, me).wait_recv()
            for j, chip in enumerate(chips):
                copy(i, 4 + j, (*chip, 1 - c), me).wait_recv()
        for cp in first + passed:
            cp.wait_send()
        for cp in mine:
            cp.wait()

    return pl.pallas_call(
        body,
        out_shape=[jax.ShapeDtypeStruct((N_DEV,) + b.shape, b.dtype) for b in blocks],
        in_specs=[_ANY] * n,
        out_specs=[_ANY] * n,
        scratch_shapes=[pltpu.SemaphoreType.DMA((7 * n,)), pltpu.SemaphoreType.DMA((7 * n,)),
                        pltpu.SemaphoreType.DMA((n,))],
        name=name,
    )(*blocks)


def _exchange_sibling(srcs, name):
    n = len(srcs)

    def body(*refs):
        src_refs, dst_refs = refs[:n], refs[n:2 * n]
        send_sems, recv_sems = refs[2 * n:]
        x, y, c = lax.axis_index("x"), lax.axis_index("y"), lax.axis_index("c")
        copies = [pltpu.make_async_remote_copy(
            src_ref=src_refs[i].at[2 * q + (1 - c)], dst_ref=dst_refs[i].at[q], send_sem=send_sems.at[4 * i + q],
            recv_sem=recv_sems.at[4 * i + q], device_id=(x, y, 1 - c), device_id_type=_MESH_ID)
            for i in range(n) for q in range(4)]
        for cp in copies:
            cp.start()
        for cp in copies:
            cp.wait_recv()
        for cp in copies:
            cp.wait_send()

    return pl.pallas_call(
        body,
        out_shape=[jax.ShapeDtypeStruct((4,) + s.shape[1:], s.dtype) for s in srcs],
        in_specs=[_ANY] * n,
        out_specs=[_ANY] * n,
        scratch_shapes=[pltpu.SemaphoreType.DMA((4 * n,)), pltpu.SemaphoreType.DMA((4 * n,))],
        name=name,
    )(*srcs)


def _col_tile(c):
    return c if c <= 256 else 256


def _chip_sum(src, recv, parity, name):
    _, r, _, c = src.shape
    tc = _col_tile(c)

    def body(par_ref, a_ref, b_ref, o_ref, ob_ref):
        s = a_ref[...] + b_ref[...]
        o_ref[...] = s
        ob_ref[...] = s.astype(BF16)

    rows = lambda f: pl.BlockSpec((None, r, None, tc), f)
    blk = pl.BlockSpec((None, r, tc), lambda q, j, par: (q, 0, j))
    return pl.pallas_call(
        body,
        out_shape=[jax.ShapeDtypeStruct((4, r, c), F32), jax.ShapeDtypeStruct((4, r, c), BF16)],
        grid_spec=pltpu.PrefetchScalarGridSpec(
            num_scalar_prefetch=1, grid=(4, c // tc),
            in_specs=[rows(lambda q, j, par: (2 * q + par[0], 0, 0, j)), rows(lambda q, j, par: (q, 0, 0, j))],
            out_specs=[blk, blk]),
        compiler_params=pltpu.CompilerParams(dimension_semantics=("parallel", "parallel")),
        name=name,
    )(parity, src, recv)


_HBM = pl.BlockSpec(memory_space=pltpu.HBM)
_SEM = pl.BlockSpec(memory_space=pltpu.SEMAPHORE)
_DATAFLOW = pltpu.SideEffectType.DATAFLOW_SIDE_EFFECTING
N_PEERS = N_DEV - 1


def _ring_peer(j):
    me = 4 * lax.axis_index("x") + 2 * lax.axis_index("y") + lax.axis_index("c")
    k = (me + j) % N_DEV
    return me, k, (k // 4, (k // 2) % 2, k % 2)


def _spread_copy(i, j, src_refs, land_refs, send_sems, recv_sems, scatter):
    me, k, peer = _ring_peer(j)
    return pltpu.make_async_remote_copy(
        src_ref=src_refs[i].at[k] if scatter else src_refs[i], dst_ref=land_refs[i].at[me],
        send_sem=send_sems.at[N_PEERS * i + j - 1], recv_sem=recv_sems.at[N_PEERS * i + j - 1], device_id=peer,
        device_id_type=_MESH_ID)


def _spread_start(srcs, carry, scatter, name):
    n = len(srcs)
    lands = [lax.empty(((N_DEV,) + s.shape[-2:]), s.dtype) for s in srcs]

    def body(*refs):
        src_refs, land_refs = refs[:n], refs[n:2 * n]
        send_sems, recv_sems, local_sems = refs[2 * n + 1:2 * n + 4]
        for i in range(n):
            for j in range(1, N_DEV):
                _spread_copy(i, j, src_refs, land_refs, send_sems, recv_sems, scatter).start()
        for i in range(n):
            _own_copy(i, src_refs, land_refs, local_sems, scatter).start()

    hbm = lambda a: pltpu.HBM(a.shape, a.dtype)
    sems = pltpu.SemaphoreType.DMA((N_PEERS * n,))
    pinned = [pltpu.with_memory_space_constraint(a, pltpu.HBM) for a in list(srcs) + lands + [carry]]
    res = pl.pallas_call(
        body, name=name,
        out_shape=(sems, sems, pltpu.SemaphoreType.DMA((n,)), *[hbm(a) for a in pinned]),
        in_specs=[_HBM] * (2 * n + 1),
        out_specs=(_SEM, _SEM, _SEM, *[_HBM] * (2 * n + 1)),
        input_output_aliases={i: 3 + i for i in range(2 * n + 1)},
        compiler_params=pltpu.CompilerParams(has_side_effects=_DATAFLOW),
    )(*pinned)
    return res[:3], list(res[3:3 + n]), list(res[3 + n:3 + 2 * n]), res[3 + 2 * n]


def _own_copy(i, src_refs, land_refs, local_sems, scatter):
    me = _ring_peer(0)[0]
    return pltpu.make_async_copy(src_refs[i].at[me] if scatter else src_refs[i], land_refs[i].at[me],
                                 local_sems.at[i])


def _spread_wait(started, after, scatter, name):
    sems, srcs, lands, _ = started
    n = len(srcs)

    def body(*refs):
        src_refs, land_refs = refs[:n], refs[n:2 * n]
        send_s, recv_s, local_s = refs[2 * n:2 * n + 3]
        for i in range(n):
            for j in range(1, N_DEV):
                cp = _spread_copy(i, j, src_refs, land_refs, send_s, recv_s, scatter)
                cp.wait_send()
                cp.wait_recv()
        for i in range(n):
            _own_copy(i, src_refs, land_refs, local_s, scatter).wait()

    hbm = lambda a: pltpu.HBM(a.shape, a.dtype)
    res = pl.pallas_call(
        body, name=name,
        out_shape=tuple(hbm(a) for a in srcs + lands),
        in_specs=[_HBM] * (2 * n) + [_SEM, _SEM, _SEM, pl.BlockSpec(memory_space=pl.ANY)],
        out_specs=tuple([_HBM] * (2 * n)),
        input_output_aliases={i: i for i in range(2 * n)},
        compiler_params=pltpu.CompilerParams(has_side_effects=_DATAFLOW),
    )(*srcs, *lands, *sems, after)
    return list(res[n:])


def _chips_copy(i, j, src_refs, land_refs, send_sems, recv_sems):
    x, y, c = lax.axis_index("x"), lax.axis_index("y"), lax.axis_index("c")
    tx, ty = [(1 - x, y), (x, 1 - y), (1 - x, 1 - y)][j]
    return pltpu.make_async_remote_copy(
        src_ref=src_refs[i].at[2 * tx + ty], dst_ref=land_refs[i].at[j], send_sem=send_sems.at[3 * i + j],
        recv_sem=recv_sems.at[3 * i + j], device_id=(tx, ty, c), device_id_type=_MESH_ID)


def _chips_start(srcs, carry, name):
    n = len(srcs)
    lands = [lax.empty((3,) + s.shape[1:], s.dtype) for s in srcs]

    def body(*refs):
        src_refs, land_refs = refs[:n], refs[n:2 * n]
        send_sems, recv_sems = refs[2 * n + 1:2 * n + 3]
        for i in range(n):
            for j in range(3):
                _chips_copy(i, j, src_refs, land_refs, send_sems, recv_sems).start()

    hbm = lambda a: pltpu.HBM(a.shape, a.dtype)
    sems = pltpu.SemaphoreType.DMA((3 * n,))
    pinned = [pltpu.with_memory_space_constraint(a, pltpu.HBM) for a in list(srcs) + lands + [carry]]
    res = pl.pallas_call(
        body, name=name,
        out_shape=(sems, sems, *[hbm(a) for a in pinned]),
        in_specs=[_HBM] * (2 * n + 1),
        out_specs=(_SEM, _SEM, *[_HBM] * (2 * n + 1)),
        input_output_aliases={i: 2 + i for i in range(2 * n + 1)},
        compiler_params=pltpu.CompilerParams(has_side_effects=_DATAFLOW),
    )(*pinned)
    return res[:2], list(res[2:2 + n]), list(res[2 + n:2 + 2 * n]), res[2 + 2 * n]


def _chips_wait(started, after, name):
    sems, srcs, lands, _ = started
    n = len(srcs)

    def body(*refs):
        src_refs, land_refs = refs[:n], refs[n:2 * n]
        send_s, recv_s = refs[2 * n:2 * n + 2]
        for i in range(n):
            for j in range(3):
                cp = _chips_copy(i, j, src_refs, land_refs, send_s, recv_s)
                cp.wait_send()
                cp.wait_recv()

    hbm = lambda a: pltpu.HBM(a.shape, a.dtype)
    res = pl.pallas_call(
        body, name=name,
        out_shape=tuple(hbm(a) for a in srcs + lands),
        in_specs=[_HBM] * (2 * n) + [_SEM, _SEM, pl.BlockSpec(memory_space=pl.ANY)],
        out_specs=tuple([_HBM] * (2 * n)),
        input_output_aliases={i: i for i in range(2 * n)},
        compiler_params=pltpu.CompilerParams(has_side_effects=_DATAFLOW),
    )(*srcs, *lands, *sems, after)
    return list(res[n:])


def _sum8(landing, name):
    _, r, c = landing.shape
    tc = _col_tile(c)

    def body(a_ref, o_ref):
        tot = a_ref[0].astype(F32)
        for k in range(1, N_DEV):
            tot = tot + a_ref[k].astype(F32)
        o_ref[...] = tot

    return pl.pallas_call(
        body,
        out_shape=jax.ShapeDtypeStruct((r, c), F32),
        grid=(c // tc,),
        in_specs=[pl.BlockSpec((N_DEV, r, tc), lambda j: (0, 0, j))],
        out_specs=pl.BlockSpec((r, tc), lambda j: (0, j)),
        compiler_params=pltpu.CompilerParams(dimension_semantics=("parallel",)),
        name=name,
    )(landing)


def _adamw_math(w, g, m, v):
    m = ADAM_B1 * m + (1.0 - ADAM_B1) * g
    v = ADAM_B2 * v + (1.0 - ADAM_B2) * (g * g)
    m_hat = m / (1.0 - ADAM_B1 ** ADAM_STEP)
    v_hat = v / (1.0 - ADAM_B2 ** ADAM_STEP)
    delta = -ADAM_LR * (m_hat / (jnp.sqrt(v_hat) + ADAM_EPS) + ADAM_WD * w)
    return delta, m, v


def _adamw(w, m, v, g, name):
    r, c = w.shape

    def fn(rows, consts):
        return list(_adamw_math(*rows)), []

    return _rowwise(fn, [w, g, m, v], [], [(c, F32)] * 3, tm=r if r <= 512 else 256, name=name)


def _adamw_sum8(w, m, v, landing, name):
    r, c = w.shape
    tc = _col_tile(c)

    def body(w_ref, m_ref, v_ref, a_ref, g_ref, d_ref, m2_ref, v2_ref):
        g = a_ref[0].astype(F32)
        for k in range(1, N_DEV):
            g = g + a_ref[k].astype(F32)
        delta, m2, v2 = _adamw_math(w_ref[...], g, m_ref[...], v_ref[...])
        g_ref[...] = g
        d_ref[...] = delta
        m2_ref[...] = m2
        v2_ref[...] = v2

    blk = pl.BlockSpec((r, tc), lambda j: (0, j))
    return pl.pallas_call(
        body,
        out_shape=[jax.ShapeDtypeStruct((r, c), F32)] * 4,
        grid=(c // tc,),
        in_specs=[blk, blk, blk, pl.BlockSpec((N_DEV, r, tc), lambda j: (0, 0, j))],
        out_specs=[blk] * 4,
        compiler_params=pltpu.CompilerParams(dimension_semantics=("parallel",)),
        name=name,
    )(w, m, v, landing)


def _adamw_parts(w, m, v, own, others, chip, name):
    r, _, c = w.shape
    tc = _col_tile(c)

    def body(q_ref, w_ref, m_ref, v_ref, a_ref, b_ref, g_ref, d_ref, m2_ref, v2_ref):
        g = ((a_ref[...] + b_ref[0].astype(F32)) + b_ref[1].astype(F32)) + b_ref[2].astype(F32)
        delta, m2, v2 = _adamw_math(w_ref[...], g, m_ref[...], v_ref[...])
        g_ref[...] = g
        d_ref[...] = delta
        m2_ref[...] = m2
        v2_ref[...] = v2

    row = pl.BlockSpec((r, None, tc), lambda j, q: (0, 0, j))
    return pl.pallas_call(
        body,
        out_shape=[jax.ShapeDtypeStruct((r, 1, c), F32)] * 4,
        grid_spec=pltpu.PrefetchScalarGridSpec(
            num_scalar_prefetch=1, grid=(c // tc,),
            in_specs=[row, row, row, pl.BlockSpec((None, r, tc), lambda j, q: (q[0], 0, j)),
                      pl.BlockSpec((3, r, tc), lambda j, q: (0, 0, j))],
            out_specs=[row] * 4),
        compiler_params=pltpu.CompilerParams(dimension_semantics=("parallel",)),
        name=name,
    )(chip, w, m, v, own, others)


def _adamw_small(gathered, params):
    ns = len(_SMALL)

    def body(*refs):
        g_ref, p_refs, o_refs = refs[0], refs[1:1 + 3 * ns], refs[1 + 3 * ns:]
        tot = g_ref[0]
        for k in range(1, N_DEV):
            tot = tot + g_ref[k]
        for i, name in enumerate(_SMALL):
            row, lane0, lanes = _SMALL_SLOTS[name]
            g = tot[row:row + 1, lane0:lane0 + lanes]
            w_, m_, v_ = (p_refs[3 * i + j][...] for j in range(3))
            delta, m2, v2 = _adamw_math(w_, g, m_, v_)
            for j, val in enumerate((g, delta, m2, v2)):
                o_refs[4 * i + j][...] = val
        o_refs[4 * ns][...] = tot[_LOSS_ROW:_LOSS_ROW + 1, 0:LANES]
        o_refs[4 * ns + 1][...] = tot[_CONV_ROW0:_CONV_ROW0 + _CONV_ROWS, :]

    out_shape = [jax.ShapeDtypeStruct(w.shape, F32) for (w, _, _) in params for _ in range(4)]
    out_shape += [jax.ShapeDtypeStruct((1, LANES), F32), jax.ShapeDtypeStruct((_CONV_ROWS, FLAT_COLS), F32)]
    flat = [a for wmv in params for a in wmv]
    return pl.pallas_call(body, out_shape=out_shape, name="adamw_small")(gathered, *flat)


def kernel(x, p, positions, w_in, conv_w, dn_a_log, dn_dt_bias, dn_norm_w, q_norm_w, w_uq, kv_norm_w, w_uk, w_uv, w_br_dn, w_br_mla, w_o, ln1_g, ln1_b, w_ffn_in, w_ffn_out, w_ple, w_ple_gate, ln2_g, ln2_b, loss_target, m_w_in, m_conv_w, m_dn_a_log, m_dn_dt_bias, m_dn_norm_w, m_q_norm_w, m_w_uq, m_kv_norm_w, m_w_uk, m_w_uv, m_w_br_dn, m_w_br_mla, m_w_o, m_ln1_g, m_ln1_b, m_w_ffn_in, m_w_ffn_out, m_w_ple, m_w_ple_gate, m_ln2_g, m_ln2_b, v_w_in, v_conv_w, v_dn_a_log, v_dn_dt_bias, v_dn_norm_w, v_q_norm_w, v_w_uq, v_kv_norm_w, v_w_uk, v_w_uv, v_w_br_dn, v_w_br_mla, v_w_o, v_ln1_g, v_ln1_b, v_w_ffn_in, v_w_ffn_out, v_w_ple, v_w_ple_gate, v_ln2_g, v_ln2_b):
    args = dict(locals())
    wts = {n: args[n] for n in _ORDER}
    mom1 = {n: args["m_" + n] for n in _ORDER}
    mom2 = {n: args["v_" + n] for n in _ORDER}
    big_names = [n for n, _ in _BIG]
    shard_shapes = {n: wts[n].shape[1:] for n in big_names}
    c_idx = lax.axis_index("c")
    q_idx = 2 * lax.axis_index("x") + lax.axis_index("y")
    parity, chip = c_idx.reshape(1).astype(jnp.int32), q_idx.reshape(1).astype(jnp.int32)

    stored = {n: _to_stored(n, wts[n][0]).astype(BF16) for n in big_names}
    first = _all_gather([stored["w_in"], conv_w[0]], "ag_first")
    group_names = {grp: [n for n, _ in pairs] for grp, pairs in _GROUP_GRADS.items()}
    carry, gathers = first[0], {}
    for grp in ("mix", "ffn"):
        gathers[grp] = _spread_start([stored[n] for n in group_names[grp]], carry, False, "ag_start_" + grp)
        carry = gathers[grp][3]
    conv_full = jnp.moveaxis(first[1], 0, 1).reshape(conv_w.shape[1], -1)
    small_w = {n: wts[n].astype(F32) for n in _SMALL}
    w = _first_weights(carry.reshape(-1, D_MODEL), conv_full, small_w)

    def late_weights(grp, after):
        got = _spread_wait(gathers[grp], after, False, "ag_wait_" + grp)
        return _late_weights(grp, {n: t.reshape(-1, t.shape[-1]) for n, t in zip(group_names[grp], got)})

    started = {}

    def emit(group, g, carry):
        if group == "w_in":
            rows, cols = _stored_shape("w_in", shard_shapes["w_in"])
            src = _w_in_grad(g).reshape(N_DEV, rows, 1, cols)
            from_sibling = _exchange_sibling([src], "rs_sibling")[0]
            own, own_bf = _chip_sum(src, from_sibling, parity, "rs_sum_w_in")
            started["w_in"] = (own, _chips_start([own_bf], carry, "rs_chips_start"))
            return started["w_in"][1][3]
        if group == "small":
            block = _pack_small_grads(_small_grads(g), g["loss"])
            started["small"] = _spread_start([block], carry, False, "ag_start_small")
            return started["small"][3]
        grads = _group_grads(group, g)
        srcs = [grads[n].reshape((N_DEV,) + _stored_shape(n, shard_shapes[n])) for n in grads]
        started[group] = (list(grads), _spread_start(srcs, carry, True, "rs_start_" + group))
        return started[group][1][3]

    s_dim = x.shape[1]
    loss, dx, g = _local_step(x[0], p[0, 0], positions.reshape(s_dim, 1).astype(F32), loss_target[0], w,
                              late_weights, emit)
    own, chips_started = started.pop("w_in")
    from_chips = _chips_wait(chips_started, dx, "rs_chips_wait")[0]
    g_small = _spread_wait(started.pop("small"), dx, False, "ag_wait_small")[0]

    out_g, out_d, out_m, out_v = {}, {}, {}, {}

    def update(n, grad, shp):
        flat2 = (shp[0], int(np.prod(shp[1:])))
        d, m2, v2 = _adamw(wts[n][0].reshape(flat2), mom1[n][0].reshape(flat2), mom2[n][0].reshape(flat2),
                           grad.reshape(flat2), "adamw_" + n)
        out_g[n], out_d[n], out_m[n], out_v[n] = grad, d.reshape(shp), m2.reshape(shp), v2.reshape(shp)

    rows_first = lambda a: jnp.transpose(a, (2, 0, 1))
    res = _adamw_parts(rows_first(wts["w_in"]), rows_first(mom1["w_in"]), rows_first(mom2["w_in"]), own, from_chips,
                       chip, "adamw_w_in")
    out_g["w_in"], out_d["w_in"], out_m["w_in"], out_v["w_in"] = (jnp.transpose(t, (1, 2, 0))[0] for t in res)
    for group, (names, st) in started.items():
        for n, landing in zip(names, _spread_wait(st, dx, True, "rs_wait_" + group)):
            shp = shard_shapes[n]
            if _BIG_AXIS[n] == 0 or shp[-1] % LANES:
                res = _adamw_sum8(_to_stored(n, wts[n][0]), _to_stored(n, mom1[n][0]), _to_stored(n, mom2[n][0]),
                                  landing, "adamw_" + n)
                out_g[n], out_d[n], out_m[n], out_v[n] = (_from_stored(n, t, shp) for t in res)
            else:
                update(n, _from_stored(n, _sum8(landing, "rs_total_" + n), shp), shp)

    res = _adamw_small(g_small, [(wts[n], mom1[n], mom2[n]) for n in _SMALL])
    for i, n in enumerate(_SMALL):
        out_g[n], out_d[n], out_m[n], out_v[n] = res[4 * i:4 * i + 4]
    loss_out = res[4 * len(_SMALL)][0, 0]
    conv_shape = conv_w.shape[1:]
    conv_g = lax.dynamic_slice(res[-1].reshape(conv_shape[0], -1), (0, (2 * q_idx + c_idx) * conv_shape[1]),
                               conv_shape)
    update("conv_w", conv_g, conv_shape)

    expand = lambda d, n: d[n] if n in _SMALL else d[n][None]
    return (loss_out, dx[None], *[expand(out_g, n) for n in _ORDER], *[expand(out_d, n) for n in _ORDER],
            *[expand(out_m, n) for n in _ORDER], *[expand(out_v, n) for n in _ORDER])
```

```python
import functools

import numpy as np
import jax
import jax.numpy as jnp
from jax import lax
from jax.experimental import pallas as pl
from jax.experimental.pallas import tpu as pltpu

F32 = jnp.float32
BF16 = jnp.bfloat16

D_MODEL = 1024
N_HEADS = 8
HEAD = 128
CHUNK = 64
GROUP = 256
ROPE = 64
Q_LORA = 384
KV_LORA = 256
FFN_HIDDEN = 2816
PLE_DIM = 256
ROPE_BASE = 10000.0
ALPHA = 2.0 ** 0.25
SCALE = float((HEAD + ROPE) ** -0.5)
NEG_BIG = -1e30
EPS_RMS = 1e-6
EPS_LN = 1e-5

ADAM_LR = 0.001
ADAM_B1 = 0.9
ADAM_B2 = 0.999
ADAM_EPS = 1e-08
ADAM_WD = 0.01
ADAM_STEP = 10

N_DEV = 8
LANES = 128
FLAT_COLS = 1024

WB_CQ, WB_CKV, WB_KR, WB_BA, WB_COLS = 0, 512, 768, 896, 1024

HIGHEST = lax.Precision.HIGHEST

NN = (((1,), (0,)), ((), ()))
TN = (((0,), (0,)), ((), ()))
NT = (((1,), (1,)), ((), ()))


def _dot(a, b, dims=NN):
    return lax.dot_general(a.astype(BF16), b.astype(BF16), dims, preferred_element_type=F32)


def _dot32(a, b, dims=NN):
    return lax.dot_general(a, b, dims, precision=HIGHEST, preferred_element_type=F32)


def _sig(x):
    return 1.0 / (1.0 + jnp.exp(-x))


MM_TILE = 1536


def _pick_wide(n):
    if n <= MM_TILE:
        return n
    return max(t for t in range(LANES, MM_TILE + 1, LANES) if n % t == 0)


def _split_bf16(a):
    hi = a.astype(BF16)
    return hi, (a - hi.astype(F32)).astype(BF16)


def _dot3(a, b, dims=NN):
    ah, al = a if isinstance(a, tuple) else _split_bf16(a)
    bh, bl = b if isinstance(b, tuple) else _split_bf16(b)
    d = lambda p, q: lax.dot_general(p, q, dims, preferred_element_type=F32)
    return d(ah, bh) + (d(ah, bl) + d(al, bh))


def _mm(a, b, *, ta=False, tb=False, add=(), out_dtype=F32, b_rows=None, name):
    if ta:
        k_dim, m_dim = a.shape
    else:
        m_dim, k_dim = a.shape
    b_first, b_len = b_rows if b_rows else (0, b.shape[0])
    if tb:
        n_dim, k2 = b_len, b.shape[1]
    else:
        k2, n_dim = b_len, b.shape[1]
    assert k_dim == k2, (a.shape, b.shape, ta, tb)
    tm = _pick_wide(m_dim)
    tn = _pick_wide(n_dim)
    tk = _pick_wide(k_dim)
    nk = k_dim // tk
    n_add = len(add)
    dims = TN if ta else (NT if tb else NN)
    assert not (ta and tb)

    def body(a_ref, b_ref, *rest):
        add_refs = rest[:n_add]
        o_ref = rest[n_add]
        acc = rest[n_add + 1]
        k = pl.program_id(2)

        @pl.when(k == 0)
        def _():
            acc[...] = jnp.zeros_like(acc)

        acc[...] += _dot(a_ref[...], b_ref[...], dims)

        @pl.when(k == nk - 1)
        def _():
            r = acc[...]
            for ar in add_refs:
                r = r + ar[...].astype(F32)
            o_ref[...] = r.astype(o_ref.dtype)

    a_spec = pl.BlockSpec((tk, tm), lambda i, j, k: (k, i)) if ta else pl.BlockSpec((tm, tk), lambda i, j, k: (i, k))
    b_tile = tn if tb else tk
    assert b_first % b_tile == 0
    b0 = b_first // b_tile
    b_spec = (pl.BlockSpec((tn, tk), lambda i, j, k: (b0 + j, k)) if tb
              else pl.BlockSpec((tk, tn), lambda i, j, k: (b0 + k, j)))
    o_spec = pl.BlockSpec((tm, tn), lambda i, j, k: (i, j))
    return pl.pallas_call(
        body,
        out_shape=jax.ShapeDtypeStruct((m_dim, n_dim), out_dtype),
        grid=(m_dim // tm, n_dim // tn, nk),
        in_specs=[a_spec, b_spec] + [o_spec] * n_add,
        out_specs=o_spec,
        scratch_shapes=[pltpu.VMEM((tm, tn), F32)],
        compiler_params=pltpu.CompilerParams(dimension_semantics=("parallel", "parallel", "arbitrary")),
        name=name,
    )(a, b, *add)


def _rowwise(fn, rows, consts, outs, accs=(), *, tm=256, name):
    rows = [r if isinstance(r, tuple) else (r, 0, r.shape[1]) for r in rows]
    s_dim = rows[0][0].shape[0]
    tm = min(tm, s_dim)
    assert s_dim % tm == 0 and all(arr.shape[0] == s_dim for arr, _, _ in rows)
    specs = [pl.BlockSpec((tm, width), functools.partial(lambda i, cb: (i, cb), cb=cb)) for _, cb, width in rows]
    args = [arr for arr, _, _ in rows]
    for c in consts:
        specs.append(pl.BlockSpec(c.shape, lambda i: (0, 0)))
        args.append(c)
    nr, nc, no = len(rows), len(consts), len(outs)
    flipped = [len(o) == 3 for o in outs]
    out_shape = [jax.ShapeDtypeStruct((o[0], s_dim) if t else (s_dim, o[0]), o[1]) for o, t in zip(outs, flipped)]
    out_specs = [pl.BlockSpec((o[0], tm), lambda i: (0, i)) if t else pl.BlockSpec((tm, o[0]), lambda i: (i, 0))
                 for o, t in zip(outs, flipped)]
    out_shape += [jax.ShapeDtypeStruct(sh, F32) for sh in accs]
    out_specs += [pl.BlockSpec(sh, lambda i: (0, 0)) for sh in accs]

    def body(*refs):
        r = [x[...].astype(F32) if x.dtype == BF16 else x[...] for x in refs[:nr]]
        c = [x[...] for x in refs[nr:nr + nc]]
        o_refs = refs[nr + nc:nr + nc + no]
        a_refs = refs[nr + nc + no:]
        o_vals, a_vals = fn(r, c)
        for ref, v, t in zip(o_refs, o_vals, flipped, strict=True):
            ref[...] = (jnp.transpose(v.astype(F32)) if t else v).astype(ref.dtype)
        if a_refs:
            @pl.when(pl.program_id(0) == 0)
            def _():
                for ref in a_refs:
                    ref[...] = jnp.zeros_like(ref)

            for ref, v in zip(a_refs, a_vals, strict=True):
                ref[...] += v

    res = pl.pallas_call(
        body,
        out_shape=out_shape,
        grid=(s_dim // tm,),
        in_specs=specs,
        out_specs=out_specs,
        compiler_params=pltpu.CompilerParams(dimension_semantics=("arbitrary" if accs else "parallel",)),
        name=name,
    )(*args)
    return res


def _colsum(v):
    return jnp.sum(v, axis=0, keepdims=True)


def _rowsum(v):
    return jnp.sum(v, axis=1, keepdims=True)


def _rowmean(v):
    return jnp.mean(v, axis=1, keepdims=True)


def _silu_grad(x):
    s = _sig(x)
    return s * (1.0 + x * (1.0 - s))


def _conv_taps(x, w, width=4):
    row = lax.broadcasted_iota(jnp.int32, x.shape, 0)
    c = x * w[width - 1:width, :]
    for s in range(1, width):
        c = c + jnp.where(row >= s, pltpu.roll(x, s, 0), 0.0) * w[width - 1 - s:width - s, :]
    return c


def _conv_fwd(proj_a, conv_w):
    s_dim = proj_a.shape[0]
    n_blk = 3 * N_HEADS

    def body(x_ref, w_ref, o_ref):
        j = pl.program_id(0)
        c = _conv_taps(x_ref[...], w_ref[...])
        y = c * _sig(c)
        r = lax.rsqrt(_rowsum(y * y) + EPS_RMS)
        fac = jnp.where(j < N_HEADS, r * (HEAD ** -0.5), jnp.where(j < 2 * N_HEADS, r, 1.0))
        o_ref[...] = y * fac

    return pl.pallas_call(
        body,
        out_shape=jax.ShapeDtypeStruct((s_dim, n_blk * HEAD), F32),
        grid=(n_blk,),
        in_specs=[pl.BlockSpec((s_dim, HEAD), lambda j: (0, j)), pl.BlockSpec((4, HEAD), lambda j: (0, j))],
        out_specs=pl.BlockSpec((s_dim, HEAD), lambda j: (0, j)),
        compiler_params=pltpu.CompilerParams(dimension_semantics=("parallel",)),
        name="conv_fwd",
    )(proj_a, conv_w)


def _conv_bwd(proj_a, conv_w, dq, dk, dv):
    s_dim = proj_a.shape[0]
    n_blk = 3 * N_HEADS

    def body(x_ref, w_ref, dq_ref, dk_ref, dv_ref, dx_ref, dw_ref):
        j = pl.program_id(0)
        x = x_ref[...]
        w = w_ref[...]
        do = jnp.where(j < N_HEADS, dq_ref[...], jnp.where(j < 2 * N_HEADS, dk_ref[...], dv_ref[...]))
        c = _conv_taps(x, w)
        sg = _sig(c)
        y = c * sg
        r = lax.rsqrt(_rowsum(y * y) + EPS_RMS)
        sc = jnp.where(j < N_HEADS, HEAD ** -0.5, 1.0)
        dy_n = sc * (r * do - y * (r * r * r) * _rowsum(do * y))
        dy = jnp.where(j < 2 * N_HEADS, dy_n, do)
        dc = dy * (sg * (1.0 + c * (1.0 - sg)))
        row = lax.broadcasted_iota(jnp.int32, x.shape, 0)
        dx = dc * w[3:4, :]
        dw_ref[3:4, :] = _colsum(dc * x)
        for s in range(1, 4):
            dx = dx + jnp.where(row < s_dim - s, pltpu.roll(dc, s_dim - s, 0), 0.0) * w[3 - s:4 - s, :]
            xs = jnp.where(row >= s, pltpu.roll(x, s, 0), 0.0)
            dw_ref[3 - s:4 - s, :] = _colsum(dc * xs)
        dx_ref[...] = dx.astype(dx_ref.dtype)

    hd = N_HEADS - 1
    return pl.pallas_call(
        body,
        out_shape=[jax.ShapeDtypeStruct((s_dim, n_blk * HEAD), BF16), jax.ShapeDtypeStruct((4, n_blk * HEAD), F32)],
        grid=(n_blk,),
        in_specs=[
            pl.BlockSpec((s_dim, HEAD), lambda j: (0, j)),
            pl.BlockSpec((4, HEAD), lambda j: (0, j)),
            pl.BlockSpec((s_dim, HEAD), lambda j: (0, jnp.minimum(j, hd))),
            pl.BlockSpec((s_dim, HEAD), lambda j: (0, jnp.clip(j - N_HEADS, 0, hd))),
            pl.BlockSpec((s_dim, HEAD), lambda j: (0, jnp.clip(j - 2 * N_HEADS, 0, hd))),
        ],
        out_specs=[pl.BlockSpec((s_dim, HEAD), lambda j: (0, j)), pl.BlockSpec((4, HEAD), lambda j: (0, j))],
        compiler_params=pltpu.CompilerParams(dimension_semantics=("parallel",)),
        name="conv_bwd",
    )(proj_a, conv_w, dq, dk, dv)


def _chunk_tri(n):
    r = np.arange(n)
    m = ((r[:, None] // CHUNK) == (r[None, :] // CHUNK)) & (r[:, None] >= r[None, :])
    m = m.astype(np.float32)
    return jnp.asarray(m), jnp.asarray(m.T)


def _softplus(z):
    return jnp.maximum(z, 0.0) + jnp.log(1.0 + jnp.exp(-jnp.abs(z)))


def _gates_fwd(proj_b, alog, dtb):
    tm = min(GROUP, proj_b.shape[0])
    tri, _ = _chunk_tri(tm)

    def fn(r, c):
        b = r[0]
        a = pltpu.roll(b, LANES - N_HEADS, 1)
        alog_, dtb_, tri_ = c
        g = -jnp.exp(alog_) * _softplus(a + dtb_)
        return [_sig(b), _dot32(tri_, g)], []

    return _rowwise(fn, [(proj_b, WB_BA // LANES, LANES)], [alog, dtb, tri],
                    [(LANES, F32), (LANES, F32)], tm=tm, name="gates_fwd")


def _gates_bwd(proj_b, alog, dtb, gc, d_beta, d_gc, d_egl_rows):
    tm = min(GROUP, proj_b.shape[0])
    _, tri_t = _chunk_tri(tm)

    def fn(r, c):
        b, gc_, d_beta_, d_gc_, d_egl_ = r
        a = pltpu.roll(b, LANES - N_HEADS, 1)
        alog_, dtb_, tri_t_ = c
        z = a + dtb_
        ea = jnp.exp(alog_)
        g = -ea * _softplus(z)
        dg = _dot32(tri_t_, d_gc_ + d_egl_ * jnp.exp(gc_))
        d_a = dg * (-ea) * _sig(z)
        beta = _sig(b)
        d_ba = d_beta_ * beta * (1.0 - beta) + pltpu.roll(d_a, N_HEADS, 1)
        return [d_ba], [_colsum(dg * g), _colsum(d_a)]

    return _rowwise(fn, [(proj_b, WB_BA // LANES, LANES), gc, d_beta, d_gc, d_egl_rows],
                    [alog, dtb, tri_t], [(LANES, BF16)], accs=[(1, LANES), (1, LANES)], tm=tm,
                    name="gates_bwd")


def _group_masks(n):
    r = lax.broadcasted_iota(jnp.int32, (n, n), 0)
    c = lax.broadcasted_iota(jnp.int32, (n, n), 1)
    same = (r // CHUNK) == (c // CHUNK)
    below, s = [], 2
    while s < CHUNK:
        below.append(jnp.logical_and((r // (2 * s)) == (c // (2 * s)),
                                     jnp.logical_and((r // s) % 2 == 1, (c // s) % 2 == 0)))
        s *= 2
    return dict(same=same, tril=jnp.logical_and(same, r >= c), strict=jnp.logical_and(same, r > c),
                last=c == (r // CHUNK) * CHUNK + (CHUNK - 1), eye=r == c, pair=(r // 2) == (c // 2), below=below)


def _inv_unit_lower(l_mats, mk):
    eye_f = mk["eye"].astype(F32)
    ts = [eye_f - jnp.where(mk["pair"], l_mat, 0.0) for l_mat in l_mats]
    for below in mk["below"]:
        halves = [_split_bf16(t) for t in ts]
        mids = [_dot3(h, jnp.where(below, l_mat, 0.0)) for h, l_mat in zip(halves, l_mats)]
        ts = [t - _dot3(m, h) for t, m, h in zip(ts, mids, halves)]
    return ts


def _unfold_blocks(folded, mask):
    n = folded.shape[0]
    return jnp.where(mask, jnp.concatenate([folded] * (n // CHUNK), axis=1), 0.0)


def _head_cols(beta, gc, gc_t, h):
    lane = lax.broadcasted_iota(jnp.int32, beta.shape, 1)
    sub = lax.broadcasted_iota(jnp.int32, gc_t.shape, 0)
    bcol = _rowsum(jnp.where(lane == h, beta, 0.0))
    gcol = _rowsum(jnp.where(lane == h, gc, 0.0))
    grow = _colsum(jnp.where(sub == h, gc_t, 0.0))
    return bcol, gcol, grow


def _prep_common(q, k, bcol, gcol, grow, mk, t_folded=None):
    n = q.shape[0]
    tril = mk["tril"]
    decay = jnp.where(tril, jnp.exp(jnp.where(tril, gcol - grow, 0.0)), 0.0)
    glast = _rowsum(jnp.where(mk["last"], jnp.broadcast_to(grow, (n, n)), 0.0))
    e = jnp.exp(gcol)
    ekt = jnp.exp(glast - gcol)
    kb = k * bcol
    kk = _dot(kb, k, NT)
    qk = _dot(q, k, NT)
    p = dict(decay=decay, e=e, ekt=ekt, kb=kb, kk=kk, qk=qk)
    if t_folded is not None:
        p["t"] = _unfold_blocks(t_folded, mk["same"])
    return p


GROUPS_PER_STEP = 4
SCAN_CHUNKS_PER_STEP = 4


def _fold_blocks(m):
    n = m.shape[0]
    out = m[:, 0:CHUNK]
    for b in range(1, n // CHUNK):
        out = out + m[:, b * CHUNK:(b + 1) * CHUNK]
    return out


def _gdr_prep_fwd(qkvn, beta, gc, gc_t):
    s_dim = qkvn.shape[0]
    tg = min(GROUP, s_dim)
    n_sub = min(GROUPS_PER_STEP, s_dim // tg)
    tb = tg * n_sub

    def body(q_ref, k_ref, v_ref, b_ref, g_ref, gt_ref, u_ref, w_ref, qd_ref, kt_ref, a_ref, t_ref):
        h = pl.program_id(0)
        mk = _group_masks(tg)
        parts = []
        for s in range(n_sub):
            rows = slice(s * tg, (s + 1) * tg)
            q, k, v = q_ref[rows, :], k_ref[rows, :], v_ref[rows, :]
            bcol, gcol, grow = _head_cols(b_ref[rows, :], g_ref[rows, :], gt_ref[:, rows], h)
            p = _prep_common(q, k, bcol, gcol, grow, mk)
            qd_ref[rows, :] = q * p["e"]
            kt_ref[rows, :] = k * p["ekt"]
            a_ref[rows, :] = _fold_blocks(jnp.where(mk["tril"], p["qk"] * p["decay"], 0.0))
            parts.append((rows, v * bcol, p["kb"] * p["e"], jnp.where(mk["strict"], p["kk"] * p["decay"], 0.0)))
        t_mats = _inv_unit_lower([part[3] for part in parts], mk)
        for (rows, vb, kbe, _), t_mat in zip(parts, t_mats):
            u_ref[rows, :] = _dot(t_mat, vb)
            w_ref[rows, :] = _dot(t_mat, kbe)
            t_ref[rows, :] = _fold_blocks(t_mat)

    row = lambda off: pl.BlockSpec((tb, HEAD), functools.partial(lambda h, m, off: (m, h + off), off=off))
    full = pl.BlockSpec((tb, LANES), lambda h, m: (m, 0))
    o_spec = pl.BlockSpec((tb, HEAD), lambda h, m: (m, h))
    a_spec = pl.BlockSpec((None, tb, CHUNK), lambda h, m: (h, m, 0))
    wide = jax.ShapeDtypeStruct((s_dim, N_HEADS * HEAD), F32)
    folded = jax.ShapeDtypeStruct((N_HEADS, s_dim, CHUNK), F32)
    return pl.pallas_call(
        body,
        out_shape=[wide, wide, wide, wide, folded, folded],
        grid=(N_HEADS, s_dim // tb),
        in_specs=[row(0), row(N_HEADS), row(2 * N_HEADS), full, full, pl.BlockSpec((8, tb), lambda h, m: (0, m))],
        out_specs=[o_spec, o_spec, o_spec, o_spec, a_spec, a_spec],
        compiler_params=pltpu.CompilerParams(dimension_semantics=("parallel", "parallel")),
        name="gdr_prep_fwd",
    )(qkvn, qkvn, qkvn, beta, gc, gc_t)


def _gdr_prep_bwd(qkvn, beta, gc, gc_t, t_fold, u, w, du, dw, dqd, dkt, d_a):
    s_dim = qkvn.shape[0]
    tg = min(GROUP, s_dim)
    n_sub = min(GROUPS_PER_STEP, s_dim // tg)
    tb = tg * n_sub

    def body(q_ref, k_ref, v_ref, b_ref, g_ref, gt_ref, t_ref, u_ref, w_ref, du_ref, dw_ref, dqd_ref, dkt_ref,
             da_ref, dq_ref, dk_ref, dv_ref, db_ref, dg_ref):
        h = pl.program_id(1)

        @pl.when(h == 0)
        def _():
            db_ref[...] = jnp.zeros_like(db_ref)
            dg_ref[...] = jnp.zeros_like(dg_ref)

        mk = _group_masks(tg)
        lane = lax.broadcasted_iota(jnp.int32, (tg, LANES), 1)
        for s in range(n_sub):
            rows = slice(s * tg, (s + 1) * tg)
            q, k, v = q_ref[rows, :], k_ref[rows, :], v_ref[rows, :]
            bcol, gcol, grow = _head_cols(b_ref[rows, :], g_ref[rows, :], gt_ref[:, rows], h)
            p = _prep_common(q, k, bcol, gcol, grow, mk, t_ref[rows, :])
            t_mat, decay, e, ekt, kb = p["t"], p["decay"], p["e"], p["ekt"], p["kb"]
            du_, dw_, dqd_, dkt_ = du_ref[rows, :], dw_ref[rows, :], dqd_ref[rows, :], dkt_ref[rows, :]
            dvb = _dot(t_mat, du_, TN)
            dkbe = _dot(t_mat, dw_, TN)
            d_l = -(_dot(dvb, u_ref[rows, :], NT) + _dot(dkbe, w_ref[rows, :], NT))
            m1 = jnp.where(mk["strict"], d_l, 0.0)
            m2 = _unfold_blocks(da_ref[rows, :], mk["tril"])
            d_kk = m1 * decay
            d_qk = m2 * decay
            d_decay = m1 * p["kk"] + m2 * p["qk"]
            dkb = _dot(d_kk, k) + dkbe * e
            dk = _dot(d_kk, kb, TN) + _dot(d_qk, q, TN) + dkt_ * ekt + dkb * bcol
            dq = _dot(d_qk, k) + dqd_ * e
            d_beta = _rowsum(dkb * k) + _rowsum(dvb * v)
            d_e = _rowsum(dkbe * kb) + _rowsum(dqd_ * q)
            d_ekt = _rowsum(dkt_ * k) * ekt
            d_diff = d_decay * decay
            d_grow = -_colsum(d_diff) + _colsum(jnp.where(mk["last"], jnp.broadcast_to(d_ekt, (tg, tg)), 0.0))
            d_gcol = d_e * e - d_ekt + _rowsum(d_diff)
            d_gcol = d_gcol + _rowsum(jnp.where(mk["eye"], jnp.broadcast_to(d_grow, (tg, tg)), 0.0))
            dq_ref[rows, :] = dq
            dk_ref[rows, :] = dk
            dv_ref[rows, :] = dvb * bcol
            db_ref[rows, :] = jnp.where(lane == h, d_beta, db_ref[rows, :])
            dg_ref[rows, :] = jnp.where(lane == h, d_gcol, dg_ref[rows, :])

    row = lambda off: pl.BlockSpec((tb, HEAD), functools.partial(lambda m, h, off: (m, h + off), off=off))
    full = pl.BlockSpec((tb, LANES), lambda m, h: (m, 0))
    o_spec = pl.BlockSpec((tb, HEAD), lambda m, h: (m, h))
    a_spec = pl.BlockSpec((None, tb, CHUNK), lambda m, h: (h, m, 0))
    wide = jax.ShapeDtypeStruct((s_dim, N_HEADS * HEAD), F32)
    lanes = jax.ShapeDtypeStruct((s_dim, LANES), F32)
    return pl.pallas_call(
        body,
        out_shape=[wide, wide, wide, lanes, lanes],
        grid=(s_dim // tb, N_HEADS),
        in_specs=[row(0), row(N_HEADS), row(2 * N_HEADS), full, full, pl.BlockSpec((8, tb), lambda m, h: (0, m)),
                  a_spec, o_spec, o_spec, o_spec, o_spec, o_spec, o_spec, a_spec],
        out_specs=[o_spec, o_spec, o_spec, full, full],
        compiler_params=pltpu.CompilerParams(dimension_semantics=("parallel", "arbitrary")),
        name="gdr_prep_bwd",
    )(qkvn, qkvn, qkvn, beta, gc, gc_t, t_fold, u, w, du, dw, dqd, dkt, d_a)


def _gdr_scan_fwd(u, w, qd, kt, a_mat, gc):
    s_dim = u.shape[0]
    n_chunks = s_dim // CHUNK
    per = min(SCAN_CHUNKS_PER_STEP, n_chunks)
    tb = per * CHUNK

    def body(u_ref, w_ref, qd_ref, kt_ref, a_ref, g_ref, o_ref, st_ref, state):
        @pl.when(pl.program_id(0) == 0)
        def _():
            state[...] = jnp.zeros_like(state)

        heads = range(N_HEADS)
        cols = [slice(h * HEAD, (h + 1) * HEAD) for h in heads]
        for i in range(per):
            rows = slice(i * CHUNK, (i + 1) * CHUNK)
            egl = jnp.exp(g_ref[(i + 1) * CHUNK - 1:(i + 1) * CHUNK, :])
            s_b = [state[h].astype(BF16) for h in heads]
            for h in heads:
                st_ref[i, h] = state[h]
            ws = [_dot(w_ref[rows, cs], s) for cs, s in zip(cols, s_b)]
            qs = [_dot(qd_ref[rows, cs], s) for cs, s in zip(cols, s_b)]
            vns = [(u_ref[rows, cs] - ws_h).astype(BF16) for cs, ws_h in zip(cols, ws)]
            avs = [_dot(a_ref[h, rows, :], vn) for h, vn in zip(heads, vns)]
            kvs = [_dot(kt_ref[rows, cs], vn, TN) for cs, vn in zip(cols, vns)]
            for h, cs in zip(heads, cols):
                o_ref[rows, cs] = qs[h] + avs[h]
                state[h] = state[h] * egl[:, h:h + 1] + kvs[h]

    wide = pl.BlockSpec((tb, N_HEADS * HEAD), lambda n: (n, 0))
    return pl.pallas_call(
        body,
        out_shape=[jax.ShapeDtypeStruct((s_dim, N_HEADS * HEAD), F32),
                   jax.ShapeDtypeStruct((n_chunks, N_HEADS, HEAD, HEAD), F32)],
        grid=(n_chunks // per,),
        in_specs=[wide, wide, wide, wide, pl.BlockSpec((N_HEADS, tb, CHUNK), lambda n: (0, n, 0)),
                  pl.BlockSpec((tb, LANES), lambda n: (n, 0))],
        out_specs=[wide, pl.BlockSpec((per, N_HEADS, HEAD, HEAD), lambda n: (n, 0, 0, 0))],
        scratch_shapes=[pltpu.VMEM((N_HEADS, HEAD, HEAD), F32)],
        compiler_params=pltpu.CompilerParams(dimension_semantics=("arbitrary",)),
        name="gdr_scan_fwd",
    )(u, w, qd, kt, a_mat, gc)


def _gdr_scan_bwd(u, w, qd, kt, a_mat, gc, states, d_o):
    s_dim = u.shape[0]
    n_chunks = s_dim // CHUNK
    per = min(SCAN_CHUNKS_PER_STEP, n_chunks)
    tb = per * CHUNK
    last = n_chunks // per - 1

    def body(u_ref, w_ref, qd_ref, kt_ref, a_ref, g_ref, st_ref, do_ref,
             du_ref, dw_ref, dqd_ref, dkt_ref, da_ref, de_ref, d_state):
        @pl.when(pl.program_id(0) == 0)
        def _():
            d_state[...] = jnp.zeros_like(d_state)

        heads = range(N_HEADS)
        cols = [slice(h * HEAD, (h + 1) * HEAD) for h in heads]
        for i in reversed(range(per)):
            rows = slice(i * CHUNK, (i + 1) * CHUNK)
            egl = jnp.exp(g_ref[(i + 1) * CHUNK - 1:(i + 1) * CHUNK, :])
            s_b = [st_ref[i, h].astype(BF16) for h in heads]
            ds_b = [d_state[h].astype(BF16) for h in heads]
            dos = [do_ref[rows, cs].astype(BF16) for cs in cols]
            w_b = [w_ref[rows, cs].astype(BF16) for cs in cols]
            ws = [_dot(w_h, s) for w_h, s in zip(w_b, s_b)]
            ados = [_dot(a_ref[h, rows, :], do, TN) for h, do in zip(heads, dos)]
            kds = [_dot(kt_ref[rows, cs], ds) for cs, ds in zip(cols, ds_b)]
            dqds = [_dot(do, s, NT) for do, s in zip(dos, s_b)]
            qdos = [_dot(qd_ref[rows, cs], do, TN) for cs, do in zip(cols, dos)]
            vns = [(u_ref[rows, cs] - ws_h).astype(BF16) for cs, ws_h in zip(cols, ws)]
            dvns = [a + k_ for a, k_ in zip(ados, kds)]
            dvn_b = [d.astype(BF16) for d in dvns]
            das = [_dot(do, vn, NT) for do, vn in zip(dos, vns)]
            dkts = [_dot(vn, ds, NT) for vn, ds in zip(vns, ds_b)]
            dws = [_dot(d, s, NT) for d, s in zip(dvn_b, s_b)]
            wds = [_dot(w_h, d, TN) for w_h, d in zip(w_b, dvn_b)]
            for h, cs in zip(heads, cols):
                ds_n = d_state[h]
                de = jnp.sum(_rowsum(ds_n * st_ref[i, h]), axis=0, keepdims=True)
                de_ref[i, h:h + 1, :] = jnp.broadcast_to(de, (1, LANES))
                dqd_ref[rows, cs] = dqds[h]
                da_ref[h, rows, :] = das[h]
                dkt_ref[rows, cs] = dkts[h]
                du_ref[rows, cs] = dvns[h]
                dw_ref[rows, cs] = -dws[h]
                d_state[h] = ds_n * egl[:, h:h + 1] + qdos[h] - wds[h]

    wide = pl.BlockSpec((tb, N_HEADS * HEAD), lambda n: (last - n, 0))
    a_spec = pl.BlockSpec((N_HEADS, tb, CHUNK), lambda n: (0, last - n, 0))
    wide_shape = jax.ShapeDtypeStruct((s_dim, N_HEADS * HEAD), F32)
    return pl.pallas_call(
        body,
        out_shape=[wide_shape, wide_shape, wide_shape, wide_shape,
                   jax.ShapeDtypeStruct((N_HEADS, s_dim, CHUNK), F32),
                   jax.ShapeDtypeStruct((n_chunks, N_HEADS, LANES), F32)],
        grid=(n_chunks // per,),
        in_specs=[wide, wide, wide, wide, a_spec, pl.BlockSpec((tb, LANES), lambda n: (last - n, 0)),
                  pl.BlockSpec((per, N_HEADS, HEAD, HEAD), lambda n: (last - n, 0, 0, 0)), wide],
        out_specs=[wide, wide, wide, wide, a_spec, pl.BlockSpec((per, N_HEADS, LANES), lambda n: (last - n, 0, 0))],
        scratch_shapes=[pltpu.VMEM((N_HEADS, HEAD, HEAD), F32)],
        compiler_params=pltpu.CompilerParams(dimension_semantics=("arbitrary",)),
        name="gdr_scan_bwd",
    )(u, w, qd, kt, a_mat, gc, states, d_o)


FUSED_ROWS = 512


def _gdr_out_fwd(o_dn, proj_a, dn_w, w_br):
    def fn(r, c):
        o, z = r
        w_, w_br_ = c
        outs = []
        for h in range(N_HEADS):
            cs = slice(h * HEAD, (h + 1) * HEAD)
            oh, zh = o[:, cs], z[:, cs]
            rr = lax.rsqrt(_rowmean(oh * oh) + EPS_RMS)
            outs.append(oh * rr * w_ * (zh * _sig(zh)))
        og = jnp.concatenate(outs, axis=1).astype(BF16)
        return [og, _dot(og, w_br_)], []

    return _rowwise(fn, [o_dn, (proj_a, 3, D_MODEL)], [dn_w, w_br], [(D_MODEL, BF16), (D_MODEL, BF16)],
                    tm=FUSED_ROWS, name="gdr_out_fwd")


def _gdr_out_bwd(o_dn, proj_a, d_y_dn, dn_w, w_br):
    def fn(r, c):
        o, z, dy = r
        w_, w_br_ = c
        dg = _dot(dy, w_br_, NT)
        d_o, d_z = [], []
        d_w = jnp.zeros((1, HEAD), F32)
        for h in range(N_HEADS):
            cs = slice(h * HEAD, (h + 1) * HEAD)
            oh, zh, dgh = o[:, cs], z[:, cs], dg[:, cs]
            rr = lax.rsqrt(_rowmean(oh * oh) + EPS_RMS)
            sz = zh * _sig(zh)
            d_n = dgh * sz
            d_z.append(dgh * (oh * rr * w_) * _silu_grad(zh))
            d_w = d_w + _colsum(d_n * oh * rr)
            gw = d_n * w_
            d_o.append(rr * gw - oh * (rr * rr * rr) * _rowmean(gw * oh))
        return [jnp.concatenate(d_o, axis=1), jnp.concatenate(d_z, axis=1)], [d_w]

    return _rowwise(fn, [o_dn, (proj_a, 3, D_MODEL), d_y_dn], [dn_w, w_br], [(D_MODEL, F32), (D_MODEL, BF16)],
                    accs=[(1, HEAD)], tm=FUSED_ROWS, name="gdr_out_bwd")


def _rms_fwd(x, w):
    r = lax.rsqrt(_rowmean(x * x) + EPS_RMS)
    return x * r * w


def _rms_bwd(x, w, dy):
    r = lax.rsqrt(_rowmean(x * x) + EPS_RMS)
    gw = dy * w
    return r * gw - x * (r * r * r) * _rowmean(gw * x), _colsum(dy * x * r)


def _rope_consts():
    inv = ROPE_BASE ** (-np.arange(0, ROPE, 2, dtype=np.float32) / ROPE)
    t = np.zeros((4, LANES), np.float32)
    t[0, :32] = inv
    t[0, 32:64] = inv
    t[1, :64] = 1.0
    t[2, 32:64] = 1.0
    t[3, :32] = -1.0
    return jnp.asarray(t)


def _rope_tables(pos, consts, width):
    ang = pos * consts[0:1, :]
    cosv, sinv = jnp.cos(ang), jnp.sin(ang)
    reps = width // LANES
    tile = (lambda t: jnp.concatenate([t] * reps, axis=1)) if reps > 1 else (lambda t: t)
    return tile(cosv * consts[1:2, :]), tile(sinv * consts[2:3, :]), tile(sinv * consts[3:4, :])


def _rope_apply(t, tabs):
    cos_t, sin_a, sin_b = tabs
    width = t.shape[1]
    return t * cos_t + pltpu.roll(t, 32, 1) * sin_a + pltpu.roll(t, width - 32, 1) * sin_b


def _rope_transpose(d, tabs):
    cos_t, sin_a, sin_b = tabs
    width = d.shape[1]
    return d * cos_t + pltpu.roll(d * sin_a, width - 32, 1) + pltpu.roll(d * sin_b, 32, 1)


QK_HEAD = 2 * HEAD


def _interleave_heads(a, b):
    parts = []
    for h in range(N_HEADS):
        parts.append(a[:, h * HEAD:(h + 1) * HEAD])
        parts.append(b if b.shape[1] == LANES else b[:, h * LANES:(h + 1) * LANES])
    return jnp.concatenate(parts, axis=1)


def _mla_rows(proj_b):
    return [(proj_b, WB_CQ // Q_LORA, Q_LORA), (proj_b, WB_CKV // KV_LORA, KV_LORA), (proj_b, WB_KR // LANES, LANES)]


def _mla_prep_fwd(proj_b, pos, qn_w, kvn_w, uq, uk, uv):
    def fn(r, c):
        cq, ckv, kr, pos_ = r
        qn_w_, kvn_w_, uq_, uk_, uv_, rope = c
        c_q = _rms_fwd(cq, qn_w_).astype(BF16)
        c_kv = _rms_fwd(ckv, kvn_w_).astype(BF16)
        qf = _dot(c_q, uq_)
        qr = _rope_apply(qf[:, D_MODEL:], _rope_tables(pos_, rope, D_MODEL))
        kr = _rope_apply(kr, _rope_tables(pos_, rope, LANES))
        kc = _interleave_heads(_dot(c_kv, uk_), kr)
        v = _dot(c_kv, uv_)
        return [c_q, c_kv, _interleave_heads(qf[:, :D_MODEL], qr) * SCALE, kc, v, kc, v], []

    wide2 = N_HEADS * QK_HEAD
    return _rowwise(fn, _mla_rows(proj_b) + [pos], [qn_w, kvn_w, uq, uk, uv, _rope_consts()],
                    [(Q_LORA, BF16), (KV_LORA, BF16), (wide2, BF16), (wide2, BF16), (D_MODEL, BF16),
                     (wide2, BF16, "T"), (D_MODEL, BF16, "T")], tm=FUSED_ROWS, name="mla_prep_fwd")


def _mla_prep_bwd(proj_b, pos, d_qc, d_kc, d_v, qn_w, kvn_w, uq, uk, uv):
    def fn(r, c):
        cq, ckv, _, pos_, dq, dk, dv = r
        qn_w_, kvn_w_, uq_, uk_, uv_, rope = c
        even = lambda t: jnp.concatenate([t[:, (2 * h) * LANES:(2 * h + 1) * LANES] for h in range(N_HEADS)], axis=1)
        odd = lambda t: jnp.concatenate([t[:, (2 * h + 1) * LANES:(2 * h + 2) * LANES] for h in range(N_HEADS)], axis=1)
        d_qr_raw = _rope_transpose(odd(dq), _rope_tables(pos_, rope, D_MODEL)) * SCALE
        d_qf = jnp.concatenate([even(dq) * SCALE, d_qr_raw], axis=1).astype(BF16)
        d_kn = even(dk).astype(BF16)
        dkr = dk[:, LANES:2 * LANES]
        for h in range(1, N_HEADS):
            dkr = dkr + dk[:, (2 * h + 1) * LANES:(2 * h + 2) * LANES]
        d_cq, d_qnw = _rms_bwd(cq, qn_w_, _dot(d_qf, uq_, NT))
        d_ckv, d_kvnw = _rms_bwd(ckv, kvn_w_, _dot(d_kn, uk_, NT) + _dot(dv, uv_, NT))
        return [d_qf, d_kn, d_cq, d_ckv, _rope_transpose(dkr, _rope_tables(pos_, rope, LANES))], [d_qnw, d_kvnw]

    return _rowwise(fn, _mla_rows(proj_b) + [pos, d_qc, d_kc, d_v], [qn_w, kvn_w, uq, uk, uv, _rope_consts()],
                    [(2 * D_MODEL, BF16), (D_MODEL, BF16), (Q_LORA, BF16), (KV_LORA, BF16), (LANES, BF16)],
                    accs=[(1, Q_LORA), (1, KV_LORA)], tm=FUSED_ROWS, name="mla_prep_bwd")


def _causal_mask_t(st, key0, query0):
    key = lax.broadcasted_iota(jnp.int32, st.shape, 0) + key0
    query = lax.broadcasted_iota(jnp.int32, st.shape, 1) + query0
    return jnp.where(key <= query, st, NEG_BIG)


def _attn_tiles(s_dim):
    tq = min(512, s_dim)
    n_chains = 2 if s_dim >= 2 * tq else 1
    return tq, n_chains, min(512, s_dim)


def _diagonal_chains(t, tq, n_chains, tk):
    return [(c, (t + 1) * tk - 1 > c * tq) for c in range(n_chains) if t * tk < (c + 1) * tq]


def _attn_fwd(qc, kc, vt):
    s_dim = qc.shape[0]
    tq, n_chains, tk = _attn_tiles(s_dim)
    tqs = tq * n_chains

    def body(q_ref, k_ref, vt_ref, o_ref, lse_ref, m_s, l_s, acc):
        qi = pl.program_id(1)
        m_s[...] = jnp.full_like(m_s, NEG_BIG)
        l_s[...] = jnp.zeros_like(l_s)
        acc[...] = jnp.zeros_like(acc)

        def make_step(chains):
            def step(j, carry):
                ks = pl.multiple_of(j * tk, tk)
                kb, vtb = k_ref[pl.ds(ks, tk), :], vt_ref[:, pl.ds(ks, tk)]
                cols = [slice(c * tq, (c + 1) * tq) for c, _ in chains]
                sts = [_dot(kb, q_ref[cs, :], NT) for cs in cols]
                sts = [_causal_mask_t(st, j * tk, qi * tqs + c * tq) if masked else st
                       for st, (c, masked) in zip(sts, chains)]
                m_prevs = [m_s[:, cs] for cs in cols]
                m_news = [jnp.maximum(mp, jnp.max(st, axis=0, keepdims=True)) for mp, st in zip(m_prevs, sts)]
                alphas = [jnp.exp(mp - mn) for mp, mn in zip(m_prevs, m_news)]
                pts = [jnp.exp(st - mn) for st, mn in zip(sts, m_news)]
                pvs = [_dot(vtb, pt) for pt in pts]
                for cs, mn, al, pt, pv in zip(cols, m_news, alphas, pts, pvs):
                    l_s[:, cs] = al * l_s[:, cs] + _colsum(pt)
                    m_s[:, cs] = mn
                    acc[:, cs] = acc[:, cs] * al + pv
                return carry
            return step

        below = qi * (tqs // tk)
        lax.fori_loop(0, below, make_step([(c, False) for c in range(n_chains)]), 0)
        for t in range(tqs // tk):
            make_step(_diagonal_chains(t, tq, n_chains, tk))(below + t, 0)
        l = l_s[...]
        o_ref[...] = jnp.transpose(acc[...] / l)
        lse_ref[...] = m_s[...] + jnp.log(l)

    return pl.pallas_call(
        body,
        out_shape=[jax.ShapeDtypeStruct((s_dim, N_HEADS * HEAD), F32), jax.ShapeDtypeStruct((N_HEADS, 1, s_dim), F32)],
        grid=(N_HEADS, s_dim // tqs),
        in_specs=[pl.BlockSpec((tqs, QK_HEAD), lambda h, qi: (qi, h)),
                  pl.BlockSpec((s_dim, QK_HEAD), lambda h, qi: (0, h)),
                  pl.BlockSpec((HEAD, s_dim), lambda h, qi: (h, 0))],
        out_specs=[pl.BlockSpec((tqs, HEAD), lambda h, qi: (qi, h)),
                   pl.BlockSpec((None, 1, tqs), lambda h, qi: (h, 0, qi))],
        scratch_shapes=[pltpu.VMEM((1, tqs), F32), pltpu.VMEM((1, tqs), F32), pltpu.VMEM((HEAD, tqs), F32)],
        compiler_params=pltpu.CompilerParams(dimension_semantics=("parallel", "parallel")),
        name="attn_fwd",
    )(qc, kc, vt)


def _attn_bwd(qc, kc, kct, v, o, d_o, lse):
    s_dim = qc.shape[0]
    tq, n_chains, tk = _attn_tiles(s_dim)
    tqs = tq * n_chains

    def body(q_ref, k_ref, kt_ref, v_ref, o_ref, do_ref, lse_ref, dq_ref, dk_ref, dv_ref, dqt_acc, dv_acc):
        qi = pl.program_id(1)

        @pl.when(qi == 0)
        def _():
            dk_ref[...] = jnp.zeros_like(dk_ref)
            dv_acc[...] = jnp.zeros_like(dv_acc)

        dqt_acc[...] = jnp.zeros_like(dqt_acc)
        do_f = do_ref[...]
        do_all = do_f.astype(BF16)
        q_all = q_ref[...]
        lse_row = lse_ref[...]
        delta_row = _dot3(jnp.ones((8, HEAD), F32), o_ref[...] * do_f, NT)[0:1, :]

        def make_step(chains):
            rows = slice(chains[0][0] * tq, (chains[-1][0] + 1) * tq)

            def step(j, carry):
                ks = pl.multiple_of(j * tk, tk)
                kb, vb, ktb = k_ref[pl.ds(ks, tk), :], v_ref[pl.ds(ks, tk), :], kt_ref[:, pl.ds(ks, tk)]
                cols = [slice(c * tq, (c + 1) * tq) for c, _ in chains]
                sts = [_dot(kb, q_all[cs, :], NT) for cs in cols]
                sts = [_causal_mask_t(st, j * tk, qi * tqs + c * tq) if masked else st
                       for st, (c, masked) in zip(sts, chains)]
                dpts = [_dot(vb, do_all[cs, :], NT) for cs in cols]
                pts = [jnp.exp(st - lse_row[:, cs]) for st, cs in zip(sts, cols)]
                dsts = [(pt * (dpt - delta_row[:, cs])).astype(BF16) for pt, dpt, cs in zip(pts, dpts, cols)]
                pts = [pt.astype(BF16) for pt in pts]
                dqs = [_dot(ktb, dst) for dst in dsts]
                for cs, dq in zip(cols, dqs):
                    dqt_acc[:, cs] += dq
                pt_all = jnp.concatenate(pts, axis=1) if len(chains) > 1 else pts[0]
                dst_all = jnp.concatenate(dsts, axis=1) if len(chains) > 1 else dsts[0]
                dk_ref[pl.ds(ks, tk), :] += _dot(dst_all, q_all[rows, :])
                dv_acc[pl.ds(ks, tk), :] += _dot(pt_all, do_all[rows, :])
                return carry
            return step

        below = qi * (tqs // tk)
        lax.fori_loop(0, below, make_step([(c, False) for c in range(n_chains)]), 0)
        for t in range(tqs // tk):
            make_step(_diagonal_chains(t, tq, n_chains, tk))(below + t, 0)
        dq_ref[...] = jnp.transpose(dqt_acc[...])

        @pl.when(qi == s_dim // tqs - 1)
        def _():
            dv_ref[...] = dv_acc[...].astype(dv_ref.dtype)

    q_spec = pl.BlockSpec((tqs, QK_HEAD), lambda h, qi: (qi, h))
    o_spec = pl.BlockSpec((tqs, HEAD), lambda h, qi: (qi, h))
    k_spec = pl.BlockSpec((s_dim, QK_HEAD), lambda h, qi: (0, h))
    v_spec = pl.BlockSpec((s_dim, HEAD), lambda h, qi: (0, h))
    wide2 = jax.ShapeDtypeStruct((s_dim, N_HEADS * QK_HEAD), F32)
    return pl.pallas_call(
        body,
        out_shape=[wide2, wide2, jax.ShapeDtypeStruct((s_dim, N_HEADS * HEAD), BF16)],
        grid=(N_HEADS, s_dim // tqs),
        in_specs=[q_spec, k_spec, pl.BlockSpec((QK_HEAD, s_dim), lambda h, qi: (h, 0)), v_spec, o_spec, o_spec,
                  pl.BlockSpec((None, 1, tqs), lambda h, qi: (h, 0, qi))],
        out_specs=[q_spec, k_spec, v_spec],
        scratch_shapes=[pltpu.VMEM((QK_HEAD, tqs), F32), pltpu.VMEM((s_dim, HEAD), F32)],
        compiler_params=pltpu.CompilerParams(dimension_semantics=("parallel", "arbitrary")),
        name="attn_bwd",
    )(qc, kc, kct, v, o, d_o, lse)


def _mix_proj_ln1(y_dn, y_mla, proj_g, x, w_o, g, b):
    s_dim = x.shape[0]
    tm = min(512, s_dim)

    def body(yd_ref, ym_ref, g_ref, x_ref, w_ref, lg_ref, lb_ref, mixed_ref, a1_ref, h1_ref, h1b_ref):
        gates = g_ref[...].astype(F32)
        mixed = (_sig(gates[:, :D_MODEL]) * yd_ref[...].astype(F32)
                 + _sig(gates[:, D_MODEL:]) * ym_ref[...].astype(F32)).astype(BF16)
        a1 = _dot(mixed, w_ref[...])
        xh, _ = _ln_stats(ALPHA * x_ref[...] + a1)
        y = xh * lg_ref[...] + lb_ref[...]
        mixed_ref[...] = mixed
        a1_ref[...] = a1
        h1_ref[...] = y
        h1b_ref[...] = y.astype(BF16)

    row = lambda width: pl.BlockSpec((tm, width), lambda i: (i, 0))
    whole = lambda a: pl.BlockSpec(a.shape, lambda i: (0, 0))
    sds = lambda dt: jax.ShapeDtypeStruct((s_dim, D_MODEL), dt)
    return pl.pallas_call(
        body,
        out_shape=[sds(BF16), sds(F32), sds(F32), sds(BF16)],
        grid=(s_dim // tm,),
        in_specs=[row(D_MODEL), row(D_MODEL), row(2 * D_MODEL), row(D_MODEL), whole(w_o), whole(g), whole(b)],
        out_specs=[row(D_MODEL)] * 4,
        compiler_params=pltpu.CompilerParams(dimension_semantics=("parallel",)),
        name="mix_proj_ln1",
    )(y_dn, y_mla, proj_g, x, w_o, g, b)


def _ln1_mix_bwd(x, a1, d_h1, d_pg, y_dn, y_mla, proj_g, g, w_o, w_pg):
    def fn(r, c):
        x_, a1_, dy, dpg, yd, ym, gates = r
        g_, w_o_, w_pg_ = c
        dy = dy + _dot(dpg, w_pg_, NT)
        xh, rr = _ln_stats(ALPHA * x_ + a1_)
        dz = _ln_bwd(dy, xh, rr, g_)
        dz_b = dz.astype(BF16)
        dm = _dot(dz_b, w_o_, NT)
        sd, sm = _sig(gates[:, :D_MODEL]), _sig(gates[:, D_MODEL:])
        d_g = jnp.concatenate([dm * yd * sd * (1.0 - sd), dm * ym * sm * (1.0 - sm)], axis=1)
        return [dz_b, ALPHA * dz, d_g, dm * sd, dm * sm], [_colsum(dy * xh), _colsum(dy)]

    return _rowwise(fn, [x, a1, d_h1, d_pg, y_dn, y_mla, proj_g], [g, w_o, w_pg],
                    [(D_MODEL, BF16), (D_MODEL, F32), (2 * D_MODEL, BF16), (D_MODEL, BF16), (D_MODEL, BF16)],
                    accs=[(1, D_MODEL), (1, D_MODEL)], tm=FUSED_ROWS, name="ln1_mix_bwd")


def _ln_stats(z):
    mu = _rowmean(z)
    zc = z - mu
    r = lax.rsqrt(_rowmean(zc * zc) + EPS_LN)
    return zc * r, r


def _ln_bwd(dy, xh, r, g):
    dxh = dy * g
    return r * (dxh - _rowmean(dxh) - xh * _rowmean(dxh * xh))


def _ffn_in_act(h1b, w_t):
    s_dim, k_dim = h1b.shape
    hidden = w_t.shape[0] // 2
    tm, tn = min(512, s_dim), _pick_wide(hidden)
    nt = hidden // tn

    def body(a_ref, bg_ref, bu_ref, gt_ref, up_ref, act_ref):
        a = a_ref[...]
        gt, up = _dot(a, bg_ref[...], NT), _dot(a, bu_ref[...], NT)
        gt_ref[...] = gt.astype(BF16)
        up_ref[...] = up.astype(BF16)
        act_ref[...] = (gt * _sig(gt) * up).astype(BF16)

    o_spec = pl.BlockSpec((tm, tn), lambda j, i: (i, j))
    sds = jax.ShapeDtypeStruct((s_dim, hidden), BF16)
    return pl.pallas_call(
        body,
        out_shape=[sds, sds, sds],
        grid=(nt, s_dim // tm),
        in_specs=[pl.BlockSpec((tm, k_dim), lambda j, i: (i, 0)), pl.BlockSpec((tn, k_dim), lambda j, i: (j, 0)),
                  pl.BlockSpec((tn, k_dim), lambda j, i: (j + nt, 0))],
        out_specs=[o_spec, o_spec, o_spec],
        compiler_params=pltpu.CompilerParams(dimension_semantics=("parallel", "parallel")),
        name="ffn_in_act",
    )(h1b, w_t, w_t)


def _act_bwd(gt, up, d_act):
    def fn(r, c):
        gt_, up_, da = r
        return [jnp.concatenate([da * up_ * _silu_grad(gt_), da * gt_ * _sig(gt_)], axis=1)], []

    return _rowwise(fn, [gt, up, d_act], [], [(2 * FFN_HIDDEN, BF16)], name="act_bwd")[0]


def _tail(h1, ffn, p, tgt, g, b, w_pg, w_ple_t):
    def fn(r, c):
        h1_, ffn_, p_, t_ = r
        pg_ = _dot(h1_, c[2])
        pp_ = _dot(p_, c[3], NT)
        sp = _sig(pg_)
        xh, rr = _ln_stats(ALPHA * h1_ + ffn_ + sp * pp_)
        y = xh * c[0] + c[1]
        err = y - t_
        dy = err * (1.0 / D_MODEL)
        dz = _ln_bwd(dy, xh, rr, c[0])
        loss = jnp.sum(0.5 * _rowmean(err * err), axis=0, keepdims=True)
        return ([dz, dz * pp_ * sp * (1.0 - sp), dz * sp, ALPHA * dz],
                [_colsum(dy * xh), _colsum(dy), jnp.broadcast_to(loss, (1, LANES))])

    return _rowwise(fn, [h1, ffn, p, tgt], [g, b, w_pg, w_ple_t], [(D_MODEL, BF16)] * 3 + [(D_MODEL, F32)],
                    accs=[(1, D_MODEL), (1, D_MODEL), (1, LANES)], tm=FUSED_ROWS, name="tail")


def _local_step(x, p, pos, tgt, w, late_weights, emit):
    w = dict(w)
    s_dim = x.shape[0]
    xb, pb = x.astype(BF16), p.astype(BF16)
    proj_a = _mm(xb, w["w_in_t"], tb=True, b_rows=(0, 4 * D_MODEL), name="f_proj_a")
    proj_g = _mm(xb, w["wg_t"], tb=True, out_dtype=BF16, name="f_proj_g")
    proj_b = _mm(xb, w["wb_t"], tb=True, name="f_proj_b")
    qkvn = _conv_fwd(proj_a, w["conv"])
    beta, gc = _gates_fwd(proj_b, w["alog"], w["dtb"])
    gc_t = jnp.transpose(gc[:, :N_HEADS])
    u, w_, qd, kt, a_mat, t_fold = _gdr_prep_fwd(qkvn, beta, gc, gc_t)
    o_dn, states = _gdr_scan_fwd(u, w_, qd, kt, a_mat, gc)
    w.update(late_weights("mix", o_dn))
    og, y_dn = _gdr_out_fwd(o_dn, proj_a, w["dnw"], w["br_dn"])
    c_q, c_kv, qc, kc, vv, kct, vt = _mla_prep_fwd(proj_b, pos, w["qnw"], w["kvnw"], w["uq"], w["uk"], w["uv"])
    o_mla, lse = _attn_fwd(qc, kc, vt)
    y_mla = _mm(o_mla, w["br_mla"], out_dtype=BF16, name="f_y_mla")
    mixed, a1, h1, h1b = _mix_proj_ln1(y_dn, y_mla, proj_g, x, w["wo"], w["ln1g"], w["ln1b"])
    w.update(late_weights("ffn", a1))
    gt, up, act = _ffn_in_act(h1b, w["ffn_in_t"])
    ffn = _mm(act, w["ffn_out"], name="f_ffn")
    g = {}
    dz2, d_pg, d_pp, dh1a, g["ln2g"], g["ln2b"], loss = _tail(h1, ffn, pb, tgt, w["ln2g"], w["ln2b"],
                                                            w["ple_gate"], w["ple_t"])
    g["ple_t"] = _mm(d_pp, pb, ta=True, out_dtype=BF16, name="b_w_ple")
    g["ple_gate"] = _mm(h1b, d_pg, ta=True, out_dtype=BF16, name="b_w_ple_gate")
    g["ffn_out"] = _mm(act, dz2, ta=True, out_dtype=BF16, name="b_w_ffn_out")
    d_act = _mm(dz2, w["ffn_out"], tb=True, out_dtype=BF16, name="b_act")
    d_gu = _act_bwd(gt, up, d_act)
    g["ffn_in_t"] = _mm(d_gu, h1b, ta=True, out_dtype=BF16, name="b_w_ffn_in")
    d_gu = emit("ffn", g, d_gu)
    d_h1 = _mm(d_gu, w["ffn_in_t"], add=(dh1a,), name="b_h1_ffn")
    dz1, dxa, d_proj_g, d_y_dn, d_y_mla, g["ln1g"], g["ln1b"] = _ln1_mix_bwd(
        x, a1, d_h1, d_pg, y_dn, y_mla, proj_g, w["ln1g"], w["wo"], w["ple_gate"])
    g["wo"] = _mm(mixed, dz1, ta=True, out_dtype=BF16, name="b_w_o")
    g["br_mla"] = _mm(o_mla, d_y_mla, ta=True, out_dtype=BF16, name="b_w_br_mla")
    d_o_mla = _mm(d_y_mla, w["br_mla"], tb=True, out_dtype=BF16, name="b_o_mla")
    d_qc, d_kc, d_v = _attn_bwd(qc, kc, kct, vv, o_mla, d_o_mla, lse)
    d_q_full, d_kn, d_cq, d_ckv, d_kr, g["qnw"], g["kvnw"] = _mla_prep_bwd(
        proj_b, pos, d_qc, d_kc, d_v, w["qnw"], w["kvnw"], w["uq"], w["uk"], w["uv"])
    g["uq"] = _mm(c_q, d_q_full, ta=True, out_dtype=BF16, name="b_w_uq")
    g["uk"] = _mm(c_kv, d_kn, ta=True, out_dtype=BF16, name="b_w_uk")
    g["uv"] = _mm(c_kv, d_v, ta=True, out_dtype=BF16, name="b_w_uv")
    g["br_dn"] = _mm(og, d_y_dn, ta=True, out_dtype=BF16, name="b_w_br_dn")
    d_y_dn = emit("mix", g, d_y_dn)
    d_o_dn, d_z, g["dnw"] = _gdr_out_bwd(o_dn, proj_a, d_y_dn, w["dnw"], w["br_dn"])
    du, dw, dqd, dkt, d_a, d_egl = _gdr_scan_bwd(u, w_, qd, kt, a_mat, gc, states, d_o_dn)
    dq, dk, dv, d_beta, d_gc = _gdr_prep_bwd(qkvn, beta, gc, gc_t, t_fold, u, w_, du, dw, dqd, dkt, d_a)
    d_egl_rows = jnp.pad(d_egl[:, None, :, 0], ((0, 0), (CHUNK - 1, 0), (0, LANES - N_HEADS))).reshape(s_dim, LANES)
    d_ba, g["alog"], g["dtb"] = _gates_bwd(proj_b, w["alog"], w["dtb"], gc, d_beta, d_gc, d_egl_rows)
    d_qkv, g["conv"] = _conv_bwd(proj_a, w["conv"], dq, dk, dv)
    zeros = jnp.zeros((s_dim, WB_CKV - Q_LORA), BF16)
    d_proj_b = jnp.concatenate([d_cq, zeros, d_ckv, d_kr, d_ba], axis=1)
    g["wa_qkv_t"] = _mm(d_qkv, xb, ta=True, name="b_w_qkv")
    g["wa_z_t"] = _mm(d_z, xb, ta=True, name="b_w_z")
    g["wg_t"] = _mm(d_proj_g, xb, ta=True, name="b_w_g")
    g["wb_t"] = _mm(d_proj_b, xb, ta=True, name="b_w_b")
    d_qkv = emit("small", dict(g, loss=loss), emit("w_in", g, d_qkv))
    dx = _mm(d_qkv, w["w_in_t"], add=(dxa,), b_rows=(0, 3 * D_MODEL), name="b_x_qkv")
    dx = _mm(d_z, w["w_in_t"], add=(dx,), b_rows=(3 * D_MODEL, D_MODEL), name="b_x_z")
    dx = _mm(d_proj_g, w["wg_t"], add=(dx,), name="b_x_g")
    dx = _mm(d_proj_b, w["wb_t"], add=(dx,), name="b_x_b")
    return loss, dx, g


_BIG = (("w_in", 1), ("w_uq", 0), ("w_uk", 0), ("w_uv", 0), ("w_br_dn", 0), ("w_br_mla", 0),
        ("w_o", 0), ("w_ffn_in", 1), ("w_ffn_out", 0), ("w_ple", 1), ("w_ple_gate", 0))
_BIG_AXIS = dict(_BIG)
_SMALL = ("ln1_g", "ln1_b", "ln2_g", "ln2_b", "q_norm_w", "kv_norm_w", "dn_norm_w", "dn_a_log", "dn_dt_bias")
_ORDER = ("w_in", "conv_w", "dn_a_log", "dn_dt_bias", "dn_norm_w", "q_norm_w", "w_uq", "kv_norm_w", "w_uk", "w_uv",
          "w_br_dn", "w_br_mla", "w_o", "ln1_g", "ln1_b", "w_ffn_in", "w_ffn_out", "w_ple", "w_ple_gate", "ln2_g",
          "ln2_b")


def _stored_shape(name, shard_shape):
    axis = _BIG_AXIS[name]
    lead = shard_shape[axis]
    return lead, int(np.prod(shard_shape)) // lead


def _to_stored(name, shard):
    return jnp.moveaxis(shard, _BIG_AXIS[name], 0).reshape(_stored_shape(name, shard.shape))


def _from_stored(name, stored, shard_shape):
    axis = _BIG_AXIS[name]
    moved = (shard_shape[axis],) + shard_shape[:axis] + shard_shape[axis + 1:]
    return jnp.moveaxis(stored.reshape(moved), 0, axis)


_W_IN_ROWS = np.cumsum([0, 3072, 1024, 8, 8, Q_LORA, KV_LORA, ROPE, D_MODEL, D_MODEL])


def _first_weights(w_in_t, conv_full, small):
    r = _W_IN_ROWS
    zr = lambda n: jnp.zeros((n, D_MODEL), w_in_t.dtype)
    w = {}
    w["w_in_t"] = w_in_t
    w["wg_t"] = w_in_t[r[7]:r[9]]
    w["wb_t"] = jnp.concatenate([w_in_t[r[4]:r[5]], zr(WB_CKV - Q_LORA), w_in_t[r[5]:r[7]], zr(LANES - ROPE),
                                 w_in_t[r[2]:r[4]], zr(LANES - 2 * N_HEADS)], axis=0)
    w["conv"] = conv_full
    pad_l = lambda v: jnp.pad(v, ((0, 0), (0, LANES - v.shape[1])))
    w["alog"], w["dtb"] = pad_l(small["dn_a_log"]), pad_l(small["dn_dt_bias"])
    w["dnw"], w["qnw"], w["kvnw"] = small["dn_norm_w"], small["q_norm_w"], small["kv_norm_w"]
    w["ln1g"], w["ln1b"], w["ln2g"], w["ln2b"] = small["ln1_g"], small["ln1_b"], small["ln2_g"], small["ln2_b"]
    return w


def _late_weights(group, fw):
    w = {}
    if group == "mix":
        uq = fw["w_uq"].reshape(Q_LORA, N_HEADS, HEAD + ROPE)
        uq_r = jnp.pad(uq[:, :, HEAD:], ((0, 0), (0, 0), (0, HEAD - ROPE)))
        w["uq"] = jnp.concatenate([uq[:, :, :HEAD].reshape(Q_LORA, -1), uq_r.reshape(Q_LORA, -1)], axis=1)
        w["uk"], w["uv"] = fw["w_uk"], fw["w_uv"]
        w["br_dn"], w["br_mla"], w["wo"] = fw["w_br_dn"], fw["w_br_mla"], fw["w_o"]
    else:
        w["ffn_in_t"], w["ffn_out"] = fw["w_ffn_in"], fw["w_ffn_out"]
        w["ple_t"], w["ple_gate"] = fw["w_ple"], fw["w_ple_gate"]
    return w


_GROUP_GRADS = {"ffn": (("w_ple", "ple_t"), ("w_ple_gate", "ple_gate"), ("w_ffn_out", "ffn_out"),
                        ("w_ffn_in", "ffn_in_t")),
                "mix": (("w_o", "wo"), ("w_br_mla", "br_mla"), ("w_uq", "uq"), ("w_uk", "uk"), ("w_uv", "uv"),
                        ("w_br_dn", "br_dn"))}


def _group_grads(group, g):
    out = {}
    for name, key in _GROUP_GRADS[group]:
        t = g[key]
        if name == "w_uq":
            uq_n = t[:, :D_MODEL].reshape(Q_LORA, N_HEADS, HEAD)
            uq_r = t[:, D_MODEL:].reshape(Q_LORA, N_HEADS, HEAD)[:, :, :ROPE]
            t = jnp.concatenate([uq_n, uq_r], axis=2).reshape(Q_LORA, -1)
        out[name] = t
    return out


PACK_ROWS = 512
SUBLANES = 8


def _pack_exchange(parts, name):
    arrays = []
    for a, _, _ in parts:
        if not any(a is b for b in arrays):
            arrays.append(a)
    index = lambda a: next(i for i, b in enumerate(arrays) if a is b)
    chunks, dst = [], 0
    for a, first, rows in parts:
        assert first % SUBLANES == 0 and rows % SUBLANES == 0
        chunks += [(index(a), first + o, dst + o, min(PACK_ROWS, rows - o)) for o in range(0, rows, PACK_ROWS)]
        dst += rows
    c, n, last = arrays[0].shape[1], len(arrays), len(chunks) - 1
    slab = dst // N_DEV
    assert slab * N_DEV == dst

    def body(*refs):
        src_refs, out_ref, recv_ref = refs[:n], refs[n], refs[n + 1]
        buf, sem_in, sem_out, send_sems, recv_sems = refs[n + 2:]
        x, y, core = lax.axis_index("x"), lax.axis_index("y"), lax.axis_index("c")

        def to_sibling(q):
            return pltpu.make_async_remote_copy(
                src_ref=out_ref.at[pl.ds((2 * q + 1 - core) * slab, slab)], dst_ref=recv_ref.at[q],
                send_sem=send_sems.at[q], recv_sem=recv_sems.at[q], device_id=(x, y, 1 - core),
                device_id_type=_MESH_ID)

        sent = [0]

        def send_packed(rows_done):
            while sent[0] < N_DEV // 2 and (2 * sent[0] + 2) * slab <= rows_done:
                to_sibling(sent[0]).start()
                sent[0] += 1

        def load(k):
            i, first, _, rows = chunks[k]
            return pltpu.make_async_copy(src_refs[i].at[pl.ds(first, rows)], buf.at[k % 2, pl.ds(0, rows)],
                                         sem_in.at[k % 2])

        def store(k):
            _, _, first, rows = chunks[k]
            return pltpu.make_async_copy(buf.at[k % 2, pl.ds(0, rows)], out_ref.at[pl.ds(first, rows), 0, :],
                                         sem_out.at[k % 2])

        load(0).start()
        for k in range(last + 1):
            load(k).wait()
            store(k).start()
            if k >= 1:
                store(k - 1).wait()
                send_packed(chunks[k][2])
            if k < last:
                load(k + 1).start()
        store(last).wait()
        send_packed(dst)
        for q in range(N_DEV // 2):
            to_sibling(q).wait_recv()
        for q in range(N_DEV // 2):
            to_sibling(q).wait_send()

    return pl.pallas_call(
        body,
        out_shape=[jax.ShapeDtypeStruct((dst, 1, c), F32), jax.ShapeDtypeStruct((N_DEV // 2, slab, 1, c), F32)],
        in_specs=[_ANY] * n,
        out_specs=[_ANY, _ANY],
        scratch_shapes=[pltpu.VMEM((2, PACK_ROWS, c), F32), pltpu.SemaphoreType.DMA((2,)),
                        pltpu.SemaphoreType.DMA((2,)), pltpu.SemaphoreType.DMA((N_DEV // 2,)),
                        pltpu.SemaphoreType.DMA((N_DEV // 2,))],
        name=name,
    )(*arrays)


def _w_in_grad_parts(g):
    wb = g["wb_t"]
    return [(g["wa_qkv_t"], 0, 3 * D_MODEL), (g["wa_z_t"], 0, D_MODEL), (wb, WB_BA, 2 * N_HEADS),
            (wb, WB_CQ, Q_LORA), (wb, WB_CKV, KV_LORA), (wb, WB_KR, ROPE), (g["wg_t"], 0, 2 * D_MODEL)]


def _small_grads(g):
    return {"ln1_g": g["ln1g"], "ln1_b": g["ln1b"], "ln2_g": g["ln2g"], "ln2_b": g["ln2b"], "q_norm_w": g["qnw"],
            "kv_norm_w": g["kvnw"], "dn_norm_w": g["dnw"], "dn_a_log": g["alog"], "dn_dt_bias": g["dtb"],
            "conv_w": g["conv"]}


_SMALL_SLOTS = {"ln1_g": (0, 0, 1024), "ln1_b": (1, 0, 1024), "ln2_g": (2, 0, 1024), "ln2_b": (3, 0, 1024),
                "q_norm_w": (4, 0, 384), "kv_norm_w": (4, 384, 256), "dn_norm_w": (4, 640, 128),
                "dn_a_log": (4, 768, 8), "dn_dt_bias": (4, 896, 8)}
_SMALL_ROWS, _LOSS_ROW, _CONV_ROW0, _CONV_ROWS = 24, 5, 8, 12


def _pack_small_grads(small_g, loss):
    zeros = lambda r, c: jnp.zeros((r, c), F32)
    row4 = jnp.concatenate([small_g["q_norm_w"], small_g["kv_norm_w"], small_g["dn_norm_w"], small_g["dn_a_log"],
                            small_g["dn_dt_bias"]], axis=1)
    row5 = jnp.concatenate([loss, zeros(1, FLAT_COLS - LANES)], axis=1)
    head = jnp.concatenate([small_g["ln1_g"], small_g["ln1_b"], small_g["ln2_g"], small_g["ln2_b"], row4, row5,
                            zeros(2, FLAT_COLS)], axis=0)
    conv = small_g["conv_w"].reshape(_CONV_ROWS, FLAT_COLS)
    return jnp.concatenate([head, conv, zeros(_SMALL_ROWS - _CONV_ROW0 - _CONV_ROWS, FLAT_COLS)], axis=0)


_MESH_ID = pl.DeviceIdType.MESH
_ANY = pl.BlockSpec(memory_space=pl.ANY)


def _all_gather(blocks, name):
    n = len(blocks)

    def body(*refs):
        x_refs, out_refs = refs[:n], refs[n:2 * n]
        send_sems, recv_sems, local_sems = refs[2 * n:]
        x, y, c = lax.axis_index("x"), lax.axis_index("y"), lax.axis_index("c")
        me, sibling = (x, y, c), (x, y, 1 - c)
        chips = [(1 - x, y), (x, 1 - y), (1 - x, 1 - y)]

        def slot(i, px, py, pc):
            return out_refs[i].at[4 * px + 2 * py + pc]

        def copy(i, k, origin, to, src=None):
            return pltpu.make_async_remote_copy(
                src_ref=slot(i, *origin) if src is None else src, dst_ref=slot(i, *origin),
                send_sem=send_sems.at[7 * i + k], recv_sem=recv_sems.at[7 * i + k], device_id=to,
                device_id_type=_MESH_ID)

        mine = [pltpu.make_async_copy(x_refs[i], slot(i, *me), local_sems.at[i]) for i in range(n)]
        first, passed = [], []
        for i in range(n):
            mine[i].start()
            first.append(copy(i, 0, me, sibling, src=x_refs[i]))
            first += [copy(i, 1 + j, me, (*chip, c), src=x_refs[i]) for j, chip in enumerate(chips)]
        for cp in first:
            cp.start()
        for i in range(n):
            for j, chip in enumerate(chips):
                copy(i, 1 + j, (*chip, c), me).wait_recv()
                passed.append(copy(i, 4 + j, (*chip, c), sibling))
                passed[-1].start()
        for i in range(n):
            copy(i, 0, sibling, me).wait_recv()
            for j, chip in enumerate(chips):
                copy(i, 4 + j, (*chip, 1 - c), me).wait_recv()
        for cp in first + passed:
            cp.wait_send()
        for cp in mine:
            cp.wait()

    return pl.pallas_call(
        body,
        out_shape=[jax.ShapeDtypeStruct((N_DEV,) + b.shape, b.dtype) for b in blocks],
        in_specs=[_ANY] * n,
        out_specs=[_ANY] * n,
        scratch_shapes=[pltpu.SemaphoreType.DMA((7 * n,)), pltpu.SemaphoreType.DMA((7 * n,)),
                        pltpu.SemaphoreType.DMA((n,))],
        name=name,
    )(*blocks)


def _col_tile(c):
    return c if c <= 256 else 256


def _chip_sum(src, recv, parity, name):
    _, r, _, c = src.shape
    tc = _col_tile(c)

    def body(par_ref, a_ref, b_ref, o_ref, ob_ref):
        s = a_ref[...] + b_ref[...]
        o_ref[...] = s
        ob_ref[...] = s.astype(BF16)

    rows = lambda f: pl.BlockSpec((None, r, None, tc), f)
    blk = pl.BlockSpec((None, r, tc), lambda q, j, par: (q, 0, j))
    return pl.pallas_call(
        body,
        out_shape=[jax.ShapeDtypeStruct((4, r, c), F32), jax.ShapeDtypeStruct((4, r, c), BF16)],
        grid_spec=pltpu.PrefetchScalarGridSpec(
            num_scalar_prefetch=1, grid=(4, c // tc),
            in_specs=[rows(lambda q, j, par: (2 * q + par[0], 0, 0, j)), rows(lambda q, j, par: (q, 0, 0, j))],
            out_specs=[blk, blk]),
        compiler_params=pltpu.CompilerParams(dimension_semantics=("parallel", "parallel")),
        name=name,
    )(parity, src, recv)


_HBM = pl.BlockSpec(memory_space=pltpu.HBM)
_SEM = pl.BlockSpec(memory_space=pltpu.SEMAPHORE)
_DATAFLOW = pltpu.SideEffectType.DATAFLOW_SIDE_EFFECTING
N_PEERS = N_DEV - 1


def _ring_peer(j):
    me = 4 * lax.axis_index("x") + 2 * lax.axis_index("y") + lax.axis_index("c")
    k = (me + j) % N_DEV
    return me, k, (k // 4, (k // 2) % 2, k % 2)


def _spread_copy(i, j, src_refs, land_refs, send_sems, recv_sems, scatter):
    me, k, peer = _ring_peer(j)
    return pltpu.make_async_remote_copy(
        src_ref=src_refs[i].at[k] if scatter else src_refs[i], dst_ref=land_refs[i].at[me],
        send_sem=send_sems.at[N_PEERS * i + j - 1], recv_sem=recv_sems.at[N_PEERS * i + j - 1], device_id=peer,
        device_id_type=_MESH_ID)


def _spread_start(srcs, carry, scatter, name):
    n = len(srcs)
    lands = [lax.empty(((N_DEV,) + s.shape[-2:]), s.dtype) for s in srcs]

    def body(*refs):
        src_refs, land_refs = refs[:n], refs[n:2 * n]
        send_sems, recv_sems, local_sems = refs[2 * n + 1:2 * n + 4]
        for i in range(n):
            for j in range(1, N_DEV):
                _spread_copy(i, j, src_refs, land_refs, send_sems, recv_sems, scatter).start()
        for i in range(n):
            _own_copy(i, src_refs, land_refs, local_sems, scatter).start()

    hbm = lambda a: pltpu.HBM(a.shape, a.dtype)
    sems = pltpu.SemaphoreType.DMA((N_PEERS * n,))
    pinned = [pltpu.with_memory_space_constraint(a, pltpu.HBM) for a in list(srcs) + lands + [carry]]
    res = pl.pallas_call(
        body, name=name,
        out_shape=(sems, sems, pltpu.SemaphoreType.DMA((n,)), *[hbm(a) for a in pinned]),
        in_specs=[_HBM] * (2 * n + 1),
        out_specs=(_SEM, _SEM, _SEM, *[_HBM] * (2 * n + 1)),
        input_output_aliases={i: 3 + i for i in range(2 * n + 1)},
        compiler_params=pltpu.CompilerParams(has_side_effects=_DATAFLOW),
    )(*pinned)
    return res[:3], list(res[3:3 + n]), list(res[3 + n:3 + 2 * n]), res[3 + 2 * n]


def _own_copy(i, src_refs, land_refs, local_sems, scatter):
    me = _ring_peer(0)[0]
    return pltpu.make_async_copy(src_refs[i].at[me] if scatter else src_refs[i], land_refs[i].at[me],
                                 local_sems.at[i])


def _spread_wait(started, after, scatter, name):
    sems, srcs, lands, _ = started
    n = len(srcs)

    def body(*refs):
        src_refs, land_refs = refs[:n], refs[n:2 * n]
        send_s, recv_s, local_s = refs[2 * n:2 * n + 3]
        for i in range(n):
            for j in range(1, N_DEV):
                cp = _spread_copy(i, j, src_refs, land_refs, send_s, recv_s, scatter)
                cp.wait_send()
                cp.wait_recv()
        for i in range(n):
            _own_copy(i, src_refs, land_refs, local_s, scatter).wait()

    hbm = lambda a: pltpu.HBM(a.shape, a.dtype)
    res = pl.pallas_call(
        body, name=name,
        out_shape=tuple(hbm(a) for a in srcs + lands),
        in_specs=[_HBM] * (2 * n) + [_SEM, _SEM, _SEM, pl.BlockSpec(memory_space=pl.ANY)],
        out_specs=tuple([_HBM] * (2 * n)),
        input_output_aliases={i: i for i in range(2 * n)},
        compiler_params=pltpu.CompilerParams(has_side_effects=_DATAFLOW),
    )(*srcs, *lands, *sems, after)
    return list(res[n:])


def _chips_copy(i, j, src_refs, land_refs, send_sems, recv_sems):
    x, y, c = lax.axis_index("x"), lax.axis_index("y"), lax.axis_index("c")
    tx, ty = [(1 - x, y), (x, 1 - y), (1 - x, 1 - y)][j]
    return pltpu.make_async_remote_copy(
        src_ref=src_refs[i].at[2 * tx + ty], dst_ref=land_refs[i].at[j], send_sem=send_sems.at[3 * i + j],
        recv_sem=recv_sems.at[3 * i + j], device_id=(tx, ty, c), device_id_type=_MESH_ID)


def _chips_start(srcs, carry, name):
    n = len(srcs)
    lands = [lax.empty((3,) + s.shape[1:], s.dtype) for s in srcs]

    def body(*refs):
        src_refs, land_refs = refs[:n], refs[n:2 * n]
        send_sems, recv_sems = refs[2 * n + 1:2 * n + 3]
        for i in range(n):
            for j in range(3):
                _chips_copy(i, j, src_refs, land_refs, send_sems, recv_sems).start()

    hbm = lambda a: pltpu.HBM(a.shape, a.dtype)
    sems = pltpu.SemaphoreType.DMA((3 * n,))
    pinned = [pltpu.with_memory_space_constraint(a, pltpu.HBM) for a in list(srcs) + lands + [carry]]
    res = pl.pallas_call(
        body, name=name,
        out_shape=(sems, sems, *[hbm(a) for a in pinned]),
        in_specs=[_HBM] * (2 * n + 1),
        out_specs=(_SEM, _SEM, *[_HBM] * (2 * n + 1)),
        input_output_aliases={i: 2 + i for i in range(2 * n + 1)},
        compiler_params=pltpu.CompilerParams(has_side_effects=_DATAFLOW),
    )(*pinned)
    return res[:2], list(res[2:2 + n]), list(res[2 + n:2 + 2 * n]), res[2 + 2 * n]


def _chips_wait(started, after, name):
    sems, srcs, lands, _ = started
    n = len(srcs)

    def body(*refs):
        src_refs, land_refs = refs[:n], refs[n:2 * n]
        send_s, recv_s = refs[2 * n:2 * n + 2]
        for i in range(n):
            for j in range(3):
                cp = _chips_copy(i, j, src_refs, land_refs, send_s, recv_s)
                cp.wait_send()
                cp.wait_recv()

    hbm = lambda a: pltpu.HBM(a.shape, a.dtype)
    res = pl.pallas_call(
        body, name=name,
        out_shape=tuple(hbm(a) for a in srcs + lands),
        in_specs=[_HBM] * (2 * n) + [_SEM, _SEM, pl.BlockSpec(memory_space=pl.ANY)],
        out_specs=tuple([_HBM] * (2 * n)),
        input_output_aliases={i: i for i in range(2 * n)},
        compiler_params=pltpu.CompilerParams(has_side_effects=_DATAFLOW),
    )(*srcs, *lands, *sems, after)
    return list(res[n:])


def _sum8(landing, name):
    _, r, c = landing.shape
    tc = _col_tile(c)

    def body(a_ref, o_ref):
        tot = a_ref[0].astype(F32)
        for k in range(1, N_DEV):
            tot = tot + a_ref[k].astype(F32)
        o_ref[...] = tot

    return pl.pallas_call(
        body,
        out_shape=jax.ShapeDtypeStruct((r, c), F32),
        grid=(c // tc,),
        in_specs=[pl.BlockSpec((N_DEV, r, tc), lambda j: (0, 0, j))],
        out_specs=pl.BlockSpec((r, tc), lambda j: (0, j)),
        compiler_params=pltpu.CompilerParams(dimension_semantics=("parallel",)),
        name=name,
    )(landing)


def _adamw_math(w, g, m, v):
    m = ADAM_B1 * m + (1.0 - ADAM_B1) * g
    v = ADAM_B2 * v + (1.0 - ADAM_B2) * (g * g)
    m_hat = m / (1.0 - ADAM_B1 ** ADAM_STEP)
    v_hat = v / (1.0 - ADAM_B2 ** ADAM_STEP)
    delta = -ADAM_LR * (m_hat / (jnp.sqrt(v_hat) + ADAM_EPS) + ADAM_WD * w)
    return delta, m, v


def _adamw(w, m, v, g, name):
    r, c = w.shape

    def fn(rows, consts):
        return list(_adamw_math(*rows)), []

    return _rowwise(fn, [w, g, m, v], [], [(c, F32)] * 3, tm=r if r <= 512 else 256, name=name)


def _adamw_sum8(w, m, v, landing, name):
    r, c = w.shape
    tc = _col_tile(c)

    def body(w_ref, m_ref, v_ref, a_ref, g_ref, d_ref, m2_ref, v2_ref):
        g = a_ref[0].astype(F32)
        for k in range(1, N_DEV):
            g = g + a_ref[k].astype(F32)
        delta, m2, v2 = _adamw_math(w_ref[...], g, m_ref[...], v_ref[...])
        g_ref[...] = g
        d_ref[...] = delta
        m2_ref[...] = m2
        v2_ref[...] = v2

    blk = pl.BlockSpec((r, tc), lambda j: (0, j))
    return pl.pallas_call(
        body,
        out_shape=[jax.ShapeDtypeStruct((r, c), F32)] * 4,
        grid=(c // tc,),
        in_specs=[blk, blk, blk, pl.BlockSpec((N_DEV, r, tc), lambda j: (0, 0, j))],
        out_specs=[blk] * 4,
        compiler_params=pltpu.CompilerParams(dimension_semantics=("parallel",)),
        name=name,
    )(w, m, v, landing)


def _adamw_parts(w, m, v, own, others, chip, name):
    r, _, c = w.shape
    tc = _col_tile(c)

    def body(q_ref, w_ref, m_ref, v_ref, a_ref, b_ref, g_ref, d_ref, m2_ref, v2_ref):
        g = ((a_ref[...] + b_ref[0].astype(F32)) + b_ref[1].astype(F32)) + b_ref[2].astype(F32)
        delta, m2, v2 = _adamw_math(w_ref[...], g, m_ref[...], v_ref[...])
        g_ref[...] = g
        d_ref[...] = delta
        m2_ref[...] = m2
        v2_ref[...] = v2

    row = pl.BlockSpec((r, None, tc), lambda j, q: (0, 0, j))
    return pl.pallas_call(
        body,
        out_shape=[jax.ShapeDtypeStruct((r, 1, c), F32)] * 4,
        grid_spec=pltpu.PrefetchScalarGridSpec(
            num_scalar_prefetch=1, grid=(c // tc,),
            in_specs=[row, row, row, pl.BlockSpec((None, r, tc), lambda j, q: (q[0], 0, j)),
                      pl.BlockSpec((3, r, tc), lambda j, q: (0, 0, j))],
            out_specs=[row] * 4),
        compiler_params=pltpu.CompilerParams(dimension_semantics=("parallel",)),
        name=name,
    )(chip, w, m, v, own, others)


def _adamw_small(gathered, params):
    ns = len(_SMALL)

    def body(*refs):
        g_ref, p_refs, o_refs = refs[0], refs[1:1 + 3 * ns], refs[1 + 3 * ns:]
        tot = g_ref[0]
        for k in range(1, N_DEV):
            tot = tot + g_ref[k]
        for i, name in enumerate(_SMALL):
            row, lane0, lanes = _SMALL_SLOTS[name]
            g = tot[row:row + 1, lane0:lane0 + lanes]
            w_, m_, v_ = (p_refs[3 * i + j][...] for j in range(3))
            delta, m2, v2 = _adamw_math(w_, g, m_, v_)
            for j, val in enumerate((g, delta, m2, v2)):
                o_refs[4 * i + j][...] = val
        o_refs[4 * ns][...] = tot[_LOSS_ROW:_LOSS_ROW + 1, 0:LANES]
        o_refs[4 * ns + 1][...] = tot[_CONV_ROW0:_CONV_ROW0 + _CONV_ROWS, :]

    out_shape = [jax.ShapeDtypeStruct(w.shape, F32) for (w, _, _) in params for _ in range(4)]
    out_shape += [jax.ShapeDtypeStruct((1, LANES), F32), jax.ShapeDtypeStruct((_CONV_ROWS, FLAT_COLS), F32)]
    flat = [a for wmv in params for a in wmv]
    return pl.pallas_call(body, out_shape=out_shape, name="adamw_small")(gathered, *flat)


def kernel(x, p, positions, w_in, conv_w, dn_a_log, dn_dt_bias, dn_norm_w, q_norm_w, w_uq, kv_norm_w, w_uk, w_uv, w_br_dn, w_br_mla, w_o, ln1_g, ln1_b, w_ffn_in, w_ffn_out, w_ple, w_ple_gate, ln2_g, ln2_b, loss_target, m_w_in, m_conv_w, m_dn_a_log, m_dn_dt_bias, m_dn_norm_w, m_q_norm_w, m_w_uq, m_kv_norm_w, m_w_uk, m_w_uv, m_w_br_dn, m_w_br_mla, m_w_o, m_ln1_g, m_ln1_b, m_w_ffn_in, m_w_ffn_out, m_w_ple, m_w_ple_gate, m_ln2_g, m_ln2_b, v_w_in, v_conv_w, v_dn_a_log, v_dn_dt_bias, v_dn_norm_w, v_q_norm_w, v_w_uq, v_kv_norm_w, v_w_uk, v_w_uv, v_w_br_dn, v_w_br_mla, v_w_o, v_ln1_g, v_ln1_b, v_w_ffn_in, v_w_ffn_out, v_w_ple, v_w_ple_gate, v_ln2_g, v_ln2_b):
    args = dict(locals())
    wts = {n: args[n] for n in _ORDER}
    mom1 = {n: args["m_" + n] for n in _ORDER}
    mom2 = {n: args["v_" + n] for n in _ORDER}
    big_names = [n for n, _ in _BIG]
    shard_shapes = {n: wts[n].shape[1:] for n in big_names}
    c_idx = lax.axis_index("c")
    q_idx = 2 * lax.axis_index("x") + lax.axis_index("y")
    parity, chip = c_idx.reshape(1).astype(jnp.int32), q_idx.reshape(1).astype(jnp.int32)

    stored = {n: _to_stored(n, wts[n][0]).astype(BF16) for n in big_names}
    first = _all_gather([stored["w_in"], conv_w[0]], "ag_first")
    group_names = {grp: [n for n, _ in pairs] for grp, pairs in _GROUP_GRADS.items()}
    carry, gathers = first[0], {}
    for grp in ("mix", "ffn"):
        gathers[grp] = _spread_start([stored[n] for n in group_names[grp]], carry, False, "ag_start_" + grp)
        carry = gathers[grp][3]
    conv_full = jnp.moveaxis(first[1], 0, 1).reshape(conv_w.shape[1], -1)
    small_w = {n: wts[n].astype(F32) for n in _SMALL}
    w = _first_weights(carry.reshape(-1, D_MODEL), conv_full, small_w)

    def late_weights(grp, after):
        got = _spread_wait(gathers[grp], after, False, "ag_wait_" + grp)
        return _late_weights(grp, {n: t.reshape(-1, t.shape[-1]) for n, t in zip(group_names[grp], got)})

    started = {}

    def emit(group, g, carry):
        if group == "w_in":
            rows, cols = _stored_shape("w_in", shard_shapes["w_in"])
            packed, from_sibling = _pack_exchange(_w_in_grad_parts(g), "rs_pack_sibling")
            own, own_bf = _chip_sum(packed.reshape(N_DEV, rows, 1, cols), from_sibling, parity, "rs_sum_w_in")
            started["w_in"] = (own, _chips_start([own_bf], carry, "rs_chips_start"))
            return started["w_in"][1][3]
        if group == "small":
            block = _pack_small_grads(_small_grads(g), g["loss"])
            started["small"] = _spread_start([block], carry, False, "ag_start_small")
            return started["small"][3]
        grads = _group_grads(group, g)
        srcs = [grads[n].reshape((N_DEV,) + _stored_shape(n, shard_shapes[n])) for n in grads]
        started[group] = (list(grads), _spread_start(srcs, carry, True, "rs_start_" + group))
        return started[group][1][3]

    s_dim = x.shape[1]
    loss, dx, g = _local_step(x[0], p[0, 0], positions.reshape(s_dim, 1).astype(F32), loss_target[0], w,
                              late_weights, emit)
    own, chips_started = started.pop("w_in")
    from_chips = _chips_wait(chips_started, dx, "rs_chips_wait")[0]
    g_small = _spread_wait(started.pop("small"), dx, False, "ag_wait_small")[0]

    out_g, out_d, out_m, out_v = {}, {}, {}, {}

    def update(n, grad, shp):
        flat2 = (shp[0], int(np.prod(shp[1:])))
        d, m2, v2 = _adamw(wts[n][0].reshape(flat2), mom1[n][0].reshape(flat2), mom2[n][0].reshape(flat2),
                           grad.reshape(flat2), "adamw_" + n)
        out_g[n], out_d[n], out_m[n], out_v[n] = grad, d.reshape(shp), m2.reshape(shp), v2.reshape(shp)

    rows_first = lambda a: jnp.transpose(a, (2, 0, 1))
    res = _adamw_parts(rows_first(wts["w_in"]), rows_first(mom1["w_in"]), rows_first(mom2["w_in"]), own, from_chips,
                       chip, "adamw_w_in")
    out_g["w_in"], out_d["w_in"], out_m["w_in"], out_v["w_in"] = (jnp.transpose(t, (1, 2, 0))[0] for t in res)
    for group, (names, st) in started.items():
        for n, landing in zip(names, _spread_wait(st, dx, True, "rs_wait_" + group)):
            shp = shard_shapes[n]
            if _BIG_AXIS[n] == 0 or shp[-1] % LANES:
                res = _adamw_sum8(_to_stored(n, wts[n][0]), _to_stored(n, mom1[n][0]), _to_stored(n, mom2[n][0]),
                                  landing, "adamw_" + n)
                out_g[n], out_d[n], out_m[n], out_v[n] = (_from_stored(n, t, shp) for t in res)
            else:
                update(n, _from_stored(n, _sum8(landing, "rs_total_" + n), shp), shp)

    res = _adamw_small(g_small, [(wts[n], mom1[n], mom2[n]) for n in _SMALL])
    for i, n in enumerate(_SMALL):
        out_g[n], out_d[n], out_m[n], out_v[n] = res[4 * i:4 * i + 4]
    loss_out = res[4 * len(_SMALL)][0, 0]
    conv_shape = conv_w.shape[1:]
    conv_g = lax.dynamic_slice(res[-1].reshape(conv_shape[0], -1), (0, (2 * q_idx + c_idx) * conv_shape[1]),
                               conv_shape)
    update("conv_w", conv_g, conv_shape)

    expand = lambda d, n: d[n] if n in _SMALL else d[n][None]
    return (loss_out, dx[None], *[expand(out_g, n) for n in _ORDER], *[expand(out_d, n) for n in _ORDER],
            *[expand(out_m, n) for n in _ORDER], *[expand(out_v, n) for n in _ORDER])
```

```python
import functools

import numpy as np
import jax
import jax.numpy as jnp
from jax import lax
from jax.experimental import pallas as pl
from jax.experimental.pallas import tpu as pltpu

F32 = jnp.float32
BF16 = jnp.bfloat16

D_MODEL = 1024
N_HEADS = 8
HEAD = 128
CHUNK = 64
GROUP = 256
ROPE = 64
Q_LORA = 384
KV_LORA = 256
FFN_HIDDEN = 2816
PLE_DIM = 256
ROPE_BASE = 10000.0
ALPHA = 2.0 ** 0.25
SCALE = float((HEAD + ROPE) ** -0.5)
NEG_BIG = -1e30
EPS_RMS = 1e-6
EPS_LN = 1e-5

ADAM_LR = 0.001
ADAM_B1 = 0.9
ADAM_B2 = 0.999
ADAM_EPS = 1e-08
ADAM_WD = 0.01
ADAM_STEP = 10

N_DEV = 8
LANES = 128
FLAT_COLS = 1024

WB_CQ, WB_CKV, WB_KR, WB_BA, WB_COLS = 0, 512, 768, 896, 1024

HIGHEST = lax.Precision.HIGHEST

NN = (((1,), (0,)), ((), ()))
TN = (((0,), (0,)), ((), ()))
NT = (((1,), (1,)), ((), ()))


def _dot(a, b, dims=NN):
    return lax.dot_general(a.astype(BF16), b.astype(BF16), dims, preferred_element_type=F32)


def _dot32(a, b, dims=NN):
    return lax.dot_general(a, b, dims, precision=HIGHEST, preferred_element_type=F32)


def _sig(x):
    return 1.0 / (1.0 + jnp.exp(-x))


MM_TILE = 1536


def _pick_wide(n):
    if n <= MM_TILE:
        return n
    return max(t for t in range(LANES, MM_TILE + 1, LANES) if n % t == 0)


def _split_bf16(a):
    hi = a.astype(BF16)
    return hi, (a - hi.astype(F32)).astype(BF16)


def _dot3(a, b, dims=NN):
    ah, al = a if isinstance(a, tuple) else _split_bf16(a)
    bh, bl = b if isinstance(b, tuple) else _split_bf16(b)
    d = lambda p, q: lax.dot_general(p, q, dims, preferred_element_type=F32)
    return d(ah, bh) + (d(ah, bl) + d(al, bh))


def _mm(a, b, *, ta=False, tb=False, add=(), out_dtype=F32, b_rows=None, name):
    if ta:
        k_dim, m_dim = a.shape
    else:
        m_dim, k_dim = a.shape
    b_first, b_len = b_rows if b_rows else (0, b.shape[0])
    if tb:
        n_dim, k2 = b_len, b.shape[1]
    else:
        k2, n_dim = b_len, b.shape[1]
    assert k_dim == k2, (a.shape, b.shape, ta, tb)
    tm = _pick_wide(m_dim)
    tn = _pick_wide(n_dim)
    tk = _pick_wide(k_dim)
    nk = k_dim // tk
    n_add = len(add)
    dims = TN if ta else (NT if tb else NN)
    assert not (ta and tb)

    def body(a_ref, b_ref, *rest):
        add_refs = rest[:n_add]
        o_ref = rest[n_add]
        acc = rest[n_add + 1]
        k = pl.program_id(2)

        @pl.when(k == 0)
        def _():
            acc[...] = jnp.zeros_like(acc)

        acc[...] += _dot(a_ref[...], b_ref[...], dims)

        @pl.when(k == nk - 1)
        def _():
            r = acc[...]
            for ar in add_refs:
                r = r + ar[...].astype(F32)
            o_ref[...] = r.astype(o_ref.dtype)

    a_spec = pl.BlockSpec((tk, tm), lambda i, j, k: (k, i)) if ta else pl.BlockSpec((tm, tk), lambda i, j, k: (i, k))
    b_tile = tn if tb else tk
    assert b_first % b_tile == 0
    b0 = b_first // b_tile
    b_spec = (pl.BlockSpec((tn, tk), lambda i, j, k: (b0 + j, k)) if tb
              else pl.BlockSpec((tk, tn), lambda i, j, k: (b0 + k, j)))
    o_spec = pl.BlockSpec((tm, tn), lambda i, j, k: (i, j))
    return pl.pallas_call(
        body,
        out_shape=jax.ShapeDtypeStruct((m_dim, n_dim), out_dtype),
        grid=(m_dim // tm, n_dim // tn, nk),
        in_specs=[a_spec, b_spec] + [o_spec] * n_add,
        out_specs=o_spec,
        scratch_shapes=[pltpu.VMEM((tm, tn), F32)],
        compiler_params=pltpu.CompilerParams(dimension_semantics=("parallel", "parallel", "arbitrary")),
        name=name,
    )(a, b, *add)


def _rowwise(fn, rows, consts, outs, accs=(), *, tm=256, name):
    rows = [r if isinstance(r, tuple) else (r, 0, r.shape[1]) for r in rows]
    s_dim = rows[0][0].shape[0]
    tm = min(tm, s_dim)
    assert s_dim % tm == 0 and all(arr.shape[0] == s_dim for arr, _, _ in rows)
    specs = [pl.BlockSpec((tm, width), functools.partial(lambda i, cb: (i, cb), cb=cb)) for _, cb, width in rows]
    args = [arr for arr, _, _ in rows]
    for c in consts:
        specs.append(pl.BlockSpec(c.shape, lambda i: (0, 0)))
        args.append(c)
    nr, nc, no = len(rows), len(consts), len(outs)
    flipped = [len(o) == 3 for o in outs]
    out_shape = [jax.ShapeDtypeStruct((o[0], s_dim) if t else (s_dim, o[0]), o[1]) for o, t in zip(outs, flipped)]
    out_specs = [pl.BlockSpec((o[0], tm), lambda i: (0, i)) if t else pl.BlockSpec((tm, o[0]), lambda i: (i, 0))
                 for o, t in zip(outs, flipped)]
    out_shape += [jax.ShapeDtypeStruct(sh, F32) for sh in accs]
    out_specs += [pl.BlockSpec(sh, lambda i: (0, 0)) for sh in accs]

    def body(*refs):
        r = [x[...].astype(F32) if x.dtype == BF16 else x[...] for x in refs[:nr]]
        c = [x[...] for x in refs[nr:nr + nc]]
        o_refs = refs[nr + nc:nr + nc + no]
        a_refs = refs[nr + nc + no:]
        o_vals, a_vals = fn(r, c)
        for ref, v, t in zip(o_refs, o_vals, flipped, strict=True):
            ref[...] = (jnp.transpose(v.astype(F32)) if t else v).astype(ref.dtype)
        if a_refs:
            @pl.when(pl.program_id(0) == 0)
            def _():
                for ref in a_refs:
                    ref[...] = jnp.zeros_like(ref)

            for ref, v in zip(a_refs, a_vals, strict=True):
                ref[...] += v

    res = pl.pallas_call(
        body,
        out_shape=out_shape,
        grid=(s_dim // tm,),
        in_specs=specs,
        out_specs=out_specs,
        compiler_params=pltpu.CompilerParams(dimension_semantics=("arbitrary" if accs else "parallel",)),
        name=name,
    )(*args)
    return res


def _colsum(v):
    return jnp.sum(v, axis=0, keepdims=True)


def _rowsum(v):
    return jnp.sum(v, axis=1, keepdims=True)


def _rowmean(v):
    return jnp.mean(v, axis=1, keepdims=True)


def _silu_grad(x):
    s = _sig(x)
    return s * (1.0 + x * (1.0 - s))


def _conv_taps(x, w, width=4):
    row = lax.broadcasted_iota(jnp.int32, x.shape, 0)
    c = x * w[width - 1:width, :]
    for s in range(1, width):
        c = c + jnp.where(row >= s, pltpu.roll(x, s, 0), 0.0) * w[width - 1 - s:width - s, :]
    return c


def _conv_fwd(proj_a, conv_w):
    s_dim = proj_a.shape[0]
    n_blk = 3 * N_HEADS

    def body(x_ref, w_ref, o_ref):
        j = pl.program_id(0)
        c = _conv_taps(x_ref[...], w_ref[...])
        y = c * _sig(c)
        r = lax.rsqrt(_rowsum(y * y) + EPS_RMS)
        fac = jnp.where(j < N_HEADS, r * (HEAD ** -0.5), jnp.where(j < 2 * N_HEADS, r, 1.0))
        o_ref[...] = y * fac

    return pl.pallas_call(
        body,
        out_shape=jax.ShapeDtypeStruct((s_dim, n_blk * HEAD), F32),
        grid=(n_blk,),
        in_specs=[pl.BlockSpec((s_dim, HEAD), lambda j: (0, j)), pl.BlockSpec((4, HEAD), lambda j: (0, j))],
        out_specs=pl.BlockSpec((s_dim, HEAD), lambda j: (0, j)),
        compiler_params=pltpu.CompilerParams(dimension_semantics=("parallel",)),
        name="conv_fwd",
    )(proj_a, conv_w)


def _conv_bwd(proj_a, conv_w, dq, dk, dv):
    s_dim = proj_a.shape[0]
    n_blk = 3 * N_HEADS

    def body(x_ref, w_ref, dq_ref, dk_ref, dv_ref, dx_ref, dw_ref):
        j = pl.program_id(0)
        x = x_ref[...]
        w = w_ref[...]
        do = jnp.where(j < N_HEADS, dq_ref[...], jnp.where(j < 2 * N_HEADS, dk_ref[...], dv_ref[...]))
        c = _conv_taps(x, w)
        sg = _sig(c)
        y = c * sg
        r = lax.rsqrt(_rowsum(y * y) + EPS_RMS)
        sc = jnp.where(j < N_HEADS, HEAD ** -0.5, 1.0)
        dy_n = sc * (r * do - y * (r * r * r) * _rowsum(do * y))
        dy = jnp.where(j < 2 * N_HEADS, dy_n, do)
        dc = dy * (sg * (1.0 + c * (1.0 - sg)))
        row = lax.broadcasted_iota(jnp.int32, x.shape, 0)
        dx = dc * w[3:4, :]
        dw_ref[3:4, :] = _colsum(dc * x)
        for s in range(1, 4):
            dx = dx + jnp.where(row < s_dim - s, pltpu.roll(dc, s_dim - s, 0), 0.0) * w[3 - s:4 - s, :]
            xs = jnp.where(row >= s, pltpu.roll(x, s, 0), 0.0)
            dw_ref[3 - s:4 - s, :] = _colsum(dc * xs)
        dx_ref[...] = dx.astype(dx_ref.dtype)

    hd = N_HEADS - 1
    return pl.pallas_call(
        body,
        out_shape=[jax.ShapeDtypeStruct((s_dim, n_blk * HEAD), BF16), jax.ShapeDtypeStruct((4, n_blk * HEAD), F32)],
        grid=(n_blk,),
        in_specs=[
            pl.BlockSpec((s_dim, HEAD), lambda j: (0, j)),
            pl.BlockSpec((4, HEAD), lambda j: (0, j)),
            pl.BlockSpec((s_dim, HEAD), lambda j: (0, jnp.minimum(j, hd))),
            pl.BlockSpec((s_dim, HEAD), lambda j: (0, jnp.clip(j - N_HEADS, 0, hd))),
            pl.BlockSpec((s_dim, HEAD), lambda j: (0, jnp.clip(j - 2 * N_HEADS, 0, hd))),
        ],
        out_specs=[pl.BlockSpec((s_dim, HEAD), lambda j: (0, j)), pl.BlockSpec((4, HEAD), lambda j: (0, j))],
        compiler_params=pltpu.CompilerParams(dimension_semantics=("parallel",)),
        name="conv_bwd",
    )(proj_a, conv_w, dq, dk, dv)


def _chunk_tri(n):
    r = np.arange(n)
    m = ((r[:, None] // CHUNK) == (r[None, :] // CHUNK)) & (r[:, None] >= r[None, :])
    m = m.astype(np.float32)
    return jnp.asarray(m), jnp.asarray(m.T)


def _softplus(z):
    return jnp.maximum(z, 0.0) + jnp.log(1.0 + jnp.exp(-jnp.abs(z)))


def _gates_fwd(proj_b, alog, dtb):
    tm = min(GROUP, proj_b.shape[0])
    tri, _ = _chunk_tri(tm)

    def fn(r, c):
        b = r[0]
        a = pltpu.roll(b, LANES - N_HEADS, 1)
        alog_, dtb_, tri_ = c
        g = -jnp.exp(alog_) * _softplus(a + dtb_)
        return [_sig(b), _dot32(tri_, g)], []

    return _rowwise(fn, [(proj_b, WB_BA // LANES, LANES)], [alog, dtb, tri],
                    [(LANES, F32), (LANES, F32)], tm=tm, name="gates_fwd")


def _gates_bwd(proj_b, alog, dtb, gc, d_beta, d_gc, d_egl_rows):
    tm = min(GROUP, proj_b.shape[0])
    _, tri_t = _chunk_tri(tm)

    def fn(r, c):
        b, gc_, d_beta_, d_gc_, d_egl_ = r
        a = pltpu.roll(b, LANES - N_HEADS, 1)
        alog_, dtb_, tri_t_ = c
        z = a + dtb_
        ea = jnp.exp(alog_)
        g = -ea * _softplus(z)
        dg = _dot32(tri_t_, d_gc_ + d_egl_ * jnp.exp(gc_))
        d_a = dg * (-ea) * _sig(z)
        beta = _sig(b)
        d_ba = d_beta_ * beta * (1.0 - beta) + pltpu.roll(d_a, N_HEADS, 1)
        return [d_ba], [_colsum(dg * g), _colsum(d_a)]

    return _rowwise(fn, [(proj_b, WB_BA // LANES, LANES), gc, d_beta, d_gc, d_egl_rows],
                    [alog, dtb, tri_t], [(LANES, BF16)], accs=[(1, LANES), (1, LANES)], tm=tm,
                    name="gates_bwd")


def _group_masks(n):
    r = lax.broadcasted_iota(jnp.int32, (n, n), 0)
    c = lax.broadcasted_iota(jnp.int32, (n, n), 1)
    same = (r // CHUNK) == (c // CHUNK)
    below, s = [], 2
    while s < CHUNK:
        below.append(jnp.logical_and((r // (2 * s)) == (c // (2 * s)),
                                     jnp.logical_and((r // s) % 2 == 1, (c // s) % 2 == 0)))
        s *= 2
    return dict(same=same, tril=jnp.logical_and(same, r >= c), strict=jnp.logical_and(same, r > c),
                last=c == (r // CHUNK) * CHUNK + (CHUNK - 1), eye=r == c, pair=(r // 2) == (c // 2), below=below)


def _inv_unit_lower(l_mats, mk):
    eye_f = mk["eye"].astype(F32)
    ts = [eye_f - jnp.where(mk["pair"], l_mat, 0.0) for l_mat in l_mats]
    for below in mk["below"]:
        halves = [_split_bf16(t) for t in ts]
        mids = [_dot3(h, jnp.where(below, l_mat, 0.0)) for h, l_mat in zip(halves, l_mats)]
        ts = [t - _dot3(m, h) for t, m, h in zip(ts, mids, halves)]
    return ts


def _unfold_blocks(folded, mask):
    n = folded.shape[0]
    return jnp.where(mask, jnp.concatenate([folded] * (n // CHUNK), axis=1), 0.0)


def _head_cols(beta, gc, gc_t, h):
    lane = lax.broadcasted_iota(jnp.int32, beta.shape, 1)
    sub = lax.broadcasted_iota(jnp.int32, gc_t.shape, 0)
    bcol = _rowsum(jnp.where(lane == h, beta, 0.0))
    gcol = _rowsum(jnp.where(lane == h, gc, 0.0))
    grow = _colsum(jnp.where(sub == h, gc_t, 0.0))
    return bcol, gcol, grow


def _prep_common(q, k, bcol, gcol, grow, mk, t_folded=None):
    n = q.shape[0]
    tril = mk["tril"]
    decay = jnp.where(tril, jnp.exp(jnp.where(tril, gcol - grow, 0.0)), 0.0)
    glast = _rowsum(jnp.where(mk["last"], jnp.broadcast_to(grow, (n, n)), 0.0))
    e = jnp.exp(gcol)
    ekt = jnp.exp(glast - gcol)
    kb = k * bcol
    kk = _dot(kb, k, NT)
    qk = _dot(q, k, NT)
    p = dict(decay=decay, e=e, ekt=ekt, kb=kb, kk=kk, qk=qk)
    if t_folded is not None:
        p["t"] = _unfold_blocks(t_folded, mk["same"])
    return p


GROUPS_PER_STEP = 4
SCAN_CHUNKS_PER_STEP = 4


def _fold_blocks(m):
    n = m.shape[0]
    out = m[:, 0:CHUNK]
    for b in range(1, n // CHUNK):
        out = out + m[:, b * CHUNK:(b + 1) * CHUNK]
    return out


def _gdr_prep_fwd(qkvn, beta, gc, gc_t):
    s_dim = qkvn.shape[0]
    tg = min(GROUP, s_dim)
    n_sub = min(GROUPS_PER_STEP, s_dim // tg)
    tb = tg * n_sub

    def body(q_ref, k_ref, v_ref, b_ref, g_ref, gt_ref, u_ref, w_ref, qd_ref, kt_ref, a_ref, t_ref):
        h = pl.program_id(0)
        mk = _group_masks(tg)
        parts = []
        for s in range(n_sub):
            rows = slice(s * tg, (s + 1) * tg)
            q, k, v = q_ref[rows, :], k_ref[rows, :], v_ref[rows, :]
            bcol, gcol, grow = _head_cols(b_ref[rows, :], g_ref[rows, :], gt_ref[:, rows], h)
            p = _prep_common(q, k, bcol, gcol, grow, mk)
            qd_ref[rows, :] = q * p["e"]
            kt_ref[rows, :] = k * p["ekt"]
            a_ref[rows, :] = _fold_blocks(jnp.where(mk["tril"], p["qk"] * p["decay"], 0.0))
            parts.append((rows, v * bcol, p["kb"] * p["e"], jnp.where(mk["strict"], p["kk"] * p["decay"], 0.0)))
        t_mats = _inv_unit_lower([part[3] for part in parts], mk)
        for (rows, vb, kbe, _), t_mat in zip(parts, t_mats):
            u_ref[rows, :] = _dot(t_mat, vb)
            w_ref[rows, :] = _dot(t_mat, kbe)
            t_ref[rows, :] = _fold_blocks(t_mat)

    row = lambda off: pl.BlockSpec((tb, HEAD), functools.partial(lambda h, m, off: (m, h + off), off=off))
    full = pl.BlockSpec((tb, LANES), lambda h, m: (m, 0))
    o_spec = pl.BlockSpec((tb, HEAD), lambda h, m: (m, h))
    a_spec = pl.BlockSpec((None, tb, CHUNK), lambda h, m: (h, m, 0))
    wide = jax.ShapeDtypeStruct((s_dim, N_HEADS * HEAD), F32)
    folded = jax.ShapeDtypeStruct((N_HEADS, s_dim, CHUNK), F32)
    return pl.pallas_call(
        body,
        out_shape=[wide, wide, wide, wide, folded, folded],
        grid=(N_HEADS, s_dim // tb),
        in_specs=[row(0), row(N_HEADS), row(2 * N_HEADS), full, full, pl.BlockSpec((8, tb), lambda h, m: (0, m))],
        out_specs=[o_spec, o_spec, o_spec, o_spec, a_spec, a_spec],
        compiler_params=pltpu.CompilerParams(dimension_semantics=("parallel", "parallel")),
        name="gdr_prep_fwd",
    )(qkvn, qkvn, qkvn, beta, gc, gc_t)


def _gdr_prep_bwd(qkvn, beta, gc, gc_t, t_fold, u, w, du, dw, dqd, dkt, d_a):
    s_dim = qkvn.shape[0]
    tg = min(GROUP, s_dim)
    n_sub = min(GROUPS_PER_STEP, s_dim // tg)
    tb = tg * n_sub

    def body(q_ref, k_ref, v_ref, b_ref, g_ref, gt_ref, t_ref, u_ref, w_ref, du_ref, dw_ref, dqd_ref, dkt_ref,
             da_ref, dq_ref, dk_ref, dv_ref, db_ref, dg_ref):
        h = pl.program_id(1)

        @pl.when(h == 0)
        def _():
            db_ref[...] = jnp.zeros_like(db_ref)
            dg_ref[...] = jnp.zeros_like(dg_ref)

        mk = _group_masks(tg)
        lane = lax.broadcasted_iota(jnp.int32, (tg, LANES), 1)
        for s in range(n_sub):
            rows = slice(s * tg, (s + 1) * tg)
            q, k, v = q_ref[rows, :], k_ref[rows, :], v_ref[rows, :]
            bcol, gcol, grow = _head_cols(b_ref[rows, :], g_ref[rows, :], gt_ref[:, rows], h)
            p = _prep_common(q, k, bcol, gcol, grow, mk, t_ref[rows, :])
            t_mat, decay, e, ekt, kb = p["t"], p["decay"], p["e"], p["ekt"], p["kb"]
            du_, dw_, dqd_, dkt_ = du_ref[rows, :], dw_ref[rows, :], dqd_ref[rows, :], dkt_ref[rows, :]
            dvb = _dot(t_mat, du_, TN)
            dkbe = _dot(t_mat, dw_, TN)
            d_l = -(_dot(dvb, u_ref[rows, :], NT) + _dot(dkbe, w_ref[rows, :], NT))
            m1 = jnp.where(mk["strict"], d_l, 0.0)
            m2 = _unfold_blocks(da_ref[rows, :], mk["tril"])
            d_kk = m1 * decay
            d_qk = m2 * decay
            d_decay = m1 * p["kk"] + m2 * p["qk"]
            dkb = _dot(d_kk, k) + dkbe * e
            dk = _dot(d_kk, kb, TN) + _dot(d_qk, q, TN) + dkt_ * ekt + dkb * bcol
            dq = _dot(d_qk, k) + dqd_ * e
            d_beta = _rowsum(dkb * k) + _rowsum(dvb * v)
            d_e = _rowsum(dkbe * kb) + _rowsum(dqd_ * q)
            d_ekt = _rowsum(dkt_ * k) * ekt
            d_diff = d_decay * decay
            d_grow = -_colsum(d_diff) + _colsum(jnp.where(mk["last"], jnp.broadcast_to(d_ekt, (tg, tg)), 0.0))
            d_gcol = d_e * e - d_ekt + _rowsum(d_diff)
            d_gcol = d_gcol + _rowsum(jnp.where(mk["eye"], jnp.broadcast_to(d_grow, (tg, tg)), 0.0))
            dq_ref[rows, :] = dq
            dk_ref[rows, :] = dk
            dv_ref[rows, :] = dvb * bcol
            db_ref[rows, :] = jnp.where(lane == h, d_beta, db_ref[rows, :])
            dg_ref[rows, :] = jnp.where(lane == h, d_gcol, dg_ref[rows, :])

    row = lambda off: pl.BlockSpec((tb, HEAD), functools.partial(lambda m, h, off: (m, h + off), off=off))
    full = pl.BlockSpec((tb, LANES), lambda m, h: (m, 0))
    o_spec = pl.BlockSpec((tb, HEAD), lambda m, h: (m, h))
    a_spec = pl.BlockSpec((None, tb, CHUNK), lambda m, h: (h, m, 0))
    wide = jax.ShapeDtypeStruct((s_dim, N_HEADS * HEAD), F32)
    lanes = jax.ShapeDtypeStruct((s_dim, LANES), F32)
    return pl.pallas_call(
        body,
        out_shape=[wide, wide, wide, lanes, lanes],
        grid=(s_dim // tb, N_HEADS),
        in_specs=[row(0), row(N_HEADS), row(2 * N_HEADS), full, full, pl.BlockSpec((8, tb), lambda m, h: (0, m)),
                  a_spec, o_spec, o_spec, o_spec, o_spec, o_spec, o_spec, a_spec],
        out_specs=[o_spec, o_spec, o_spec, full, full],
        compiler_params=pltpu.CompilerParams(dimension_semantics=("parallel", "arbitrary")),
        name="gdr_prep_bwd",
    )(qkvn, qkvn, qkvn, beta, gc, gc_t, t_fold, u, w, du, dw, dqd, dkt, d_a)


def _gdr_scan_fwd(u, w, qd, kt, a_mat, gc):
    s_dim = u.shape[0]
    n_chunks = s_dim // CHUNK
    per = min(SCAN_CHUNKS_PER_STEP, n_chunks)
    tb = per * CHUNK

    def body(u_ref, w_ref, qd_ref, kt_ref, a_ref, g_ref, o_ref, st_ref, state):
        @pl.when(pl.program_id(0) == 0)
        def _():
            state[...] = jnp.zeros_like(state)

        heads = range(N_HEADS)
        cols = [slice(h * HEAD, (h + 1) * HEAD) for h in heads]
        for i in range(per):
            rows = slice(i * CHUNK, (i + 1) * CHUNK)
            egl = jnp.exp(g_ref[(i + 1) * CHUNK - 1:(i + 1) * CHUNK, :])
            s_b = [state[h].astype(BF16) for h in heads]
            for h in heads:
                st_ref[i, h] = state[h]
            ws = [_dot(w_ref[rows, cs], s) for cs, s in zip(cols, s_b)]
            qs = [_dot(qd_ref[rows, cs], s) for cs, s in zip(cols, s_b)]
            vns = [(u_ref[rows, cs] - ws_h).astype(BF16) for cs, ws_h in zip(cols, ws)]
            avs = [_dot(a_ref[h, rows, :], vn) for h, vn in zip(heads, vns)]
            kvs = [_dot(kt_ref[rows, cs], vn, TN) for cs, vn in zip(cols, vns)]
            for h, cs in zip(heads, cols):
                o_ref[rows, cs] = qs[h] + avs[h]
                state[h] = state[h] * egl[:, h:h + 1] + kvs[h]

    wide = pl.BlockSpec((tb, N_HEADS * HEAD), lambda n: (n, 0))
    return pl.pallas_call(
        body,
        out_shape=[jax.ShapeDtypeStruct((s_dim, N_HEADS * HEAD), F32),
                   jax.ShapeDtypeStruct((n_chunks, N_HEADS, HEAD, HEAD), F32)],
        grid=(n_chunks // per,),
        in_specs=[wide, wide, wide, wide, pl.BlockSpec((N_HEADS, tb, CHUNK), lambda n: (0, n, 0)),
                  pl.BlockSpec((tb, LANES), lambda n: (n, 0))],
        out_specs=[wide, pl.BlockSpec((per, N_HEADS, HEAD, HEAD), lambda n: (n, 0, 0, 0))],
        scratch_shapes=[pltpu.VMEM((N_HEADS, HEAD, HEAD), F32)],
        compiler_params=pltpu.CompilerParams(dimension_semantics=("arbitrary",)),
        name="gdr_scan_fwd",
    )(u, w, qd, kt, a_mat, gc)


def _gdr_scan_bwd(u, w, qd, kt, a_mat, gc, states, d_o):
    s_dim = u.shape[0]
    n_chunks = s_dim // CHUNK
    per = min(SCAN_CHUNKS_PER_STEP, n_chunks)
    tb = per * CHUNK
    last = n_chunks // per - 1

    def body(u_ref, w_ref, qd_ref, kt_ref, a_ref, g_ref, st_ref, do_ref,
             du_ref, dw_ref, dqd_ref, dkt_ref, da_ref, de_ref, d_state):
        @pl.when(pl.program_id(0) == 0)
        def _():
            d_state[...] = jnp.zeros_like(d_state)

        heads = range(N_HEADS)
        cols = [slice(h * HEAD, (h + 1) * HEAD) for h in heads]
        for i in reversed(range(per)):
            rows = slice(i * CHUNK, (i + 1) * CHUNK)
            egl = jnp.exp(g_ref[(i + 1) * CHUNK - 1:(i + 1) * CHUNK, :])
            s_b = [st_ref[i, h].astype(BF16) for h in heads]
            ds_b = [d_state[h].astype(BF16) for h in heads]
            dos = [do_ref[rows, cs].astype(BF16) for cs in cols]
            w_b = [w_ref[rows, cs].astype(BF16) for cs in cols]
            ws = [_dot(w_h, s) for w_h, s in zip(w_b, s_b)]
            ados = [_dot(a_ref[h, rows, :], do, TN) for h, do in zip(heads, dos)]
            kds = [_dot(kt_ref[rows, cs], ds) for cs, ds in zip(cols, ds_b)]
            dqds = [_dot(do, s, NT) for do, s in zip(dos, s_b)]
            qdos = [_dot(qd_ref[rows, cs], do, TN) for cs, do in zip(cols, dos)]
            vns = [(u_ref[rows, cs] - ws_h).astype(BF16) for cs, ws_h in zip(cols, ws)]
            dvns = [a + k_ for a, k_ in zip(ados, kds)]
            dvn_b = [d.astype(BF16) for d in dvns]
            das = [_dot(do, vn, NT) for do, vn in zip(dos, vns)]
            dkts = [_dot(vn, ds, NT) for vn, ds in zip(vns, ds_b)]
            dws = [_dot(d, s, NT) for d, s in zip(dvn_b, s_b)]
            wds = [_dot(w_h, d, TN) for w_h, d in zip(w_b, dvn_b)]
            for h, cs in zip(heads, cols):
                ds_n = d_state[h]
                de = jnp.sum(_rowsum(ds_n * st_ref[i, h]), axis=0, keepdims=True)
                de_ref[i, h:h + 1, :] = jnp.broadcast_to(de, (1, LANES))
                dqd_ref[rows, cs] = dqds[h]
                da_ref[h, rows, :] = das[h]
                dkt_ref[rows, cs] = dkts[h]
                du_ref[rows, cs] = dvns[h]
                dw_ref[rows, cs] = -dws[h]
                d_state[h] = ds_n * egl[:, h:h + 1] + qdos[h] - wds[h]

    wide = pl.BlockSpec((tb, N_HEADS * HEAD), lambda n: (last - n, 0))
    a_spec = pl.BlockSpec((N_HEADS, tb, CHUNK), lambda n: (0, last - n, 0))
    wide_shape = jax.ShapeDtypeStruct((s_dim, N_HEADS * HEAD), F32)
    return pl.pallas_call(
        body,
        out_shape=[wide_shape, wide_shape, wide_shape, wide_shape,
                   jax.ShapeDtypeStruct((N_HEADS, s_dim, CHUNK), F32),
                   jax.ShapeDtypeStruct((n_chunks, N_HEADS, LANES), F32)],
        grid=(n_chunks // per,),
        in_specs=[wide, wide, wide, wide, a_spec, pl.BlockSpec((tb, LANES), lambda n: (last - n, 0)),
                  pl.BlockSpec((per, N_HEADS, HEAD, HEAD), lambda n: (last - n, 0, 0, 0)), wide],
        out_specs=[wide, wide, wide, wide, a_spec, pl.BlockSpec((per, N_HEADS, LANES), lambda n: (last - n, 0, 0))],
        scratch_shapes=[pltpu.VMEM((N_HEADS, HEAD, HEAD), F32)],
        compiler_params=pltpu.CompilerParams(dimension_semantics=("arbitrary",)),
        name="gdr_scan_bwd",
    )(u, w, qd, kt, a_mat, gc, states, d_o)


FUSED_ROWS = 512


def _gdr_out_fwd(o_dn, proj_a, dn_w, w_br):
    def fn(r, c):
        o, z = r
        w_, w_br_ = c
        outs = []
        for h in range(N_HEADS):
            cs = slice(h * HEAD, (h + 1) * HEAD)
            oh, zh = o[:, cs], z[:, cs]
            rr = lax.rsqrt(_rowmean(oh * oh) + EPS_RMS)
            outs.append(oh * rr * w_ * (zh * _sig(zh)))
        og = jnp.concatenate(outs, axis=1).astype(BF16)
        return [og, _dot(og, w_br_)], []

    return _rowwise(fn, [o_dn, (proj_a, 3, D_MODEL)], [dn_w, w_br], [(D_MODEL, BF16), (D_MODEL, BF16)],
                    tm=FUSED_ROWS, name="gdr_out_fwd")


def _gdr_out_bwd(o_dn, proj_a, d_y_dn, dn_w, w_br):
    def fn(r, c):
        o, z, dy = r
        w_, w_br_ = c
        dg = _dot(dy, w_br_, NT)
        d_o, d_z = [], []
        d_w = jnp.zeros((1, HEAD), F32)
        for h in range(N_HEADS):
            cs = slice(h * HEAD, (h + 1) * HEAD)
            oh, zh, dgh = o[:, cs], z[:, cs], dg[:, cs]
            rr = lax.rsqrt(_rowmean(oh * oh) + EPS_RMS)
            sz = zh * _sig(zh)
            d_n = dgh * sz
            d_z.append(dgh * (oh * rr * w_) * _silu_grad(zh))
            d_w = d_w + _colsum(d_n * oh * rr)
            gw = d_n * w_
            d_o.append(rr * gw - oh * (rr * rr * rr) * _rowmean(gw * oh))
        return [jnp.concatenate(d_o, axis=1), jnp.concatenate(d_z, axis=1)], [d_w]

    return _rowwise(fn, [o_dn, (proj_a, 3, D_MODEL), d_y_dn], [dn_w, w_br], [(D_MODEL, F32), (D_MODEL, BF16)],
                    accs=[(1, HEAD)], tm=FUSED_ROWS, name="gdr_out_bwd")


def _rms_fwd(x, w):
    r = lax.rsqrt(_rowmean(x * x) + EPS_RMS)
    return x * r * w


def _rms_bwd(x, w, dy):
    r = lax.rsqrt(_rowmean(x * x) + EPS_RMS)
    gw = dy * w
    return r * gw - x * (r * r * r) * _rowmean(gw * x), _colsum(dy * x * r)


def _rope_consts():
    inv = ROPE_BASE ** (-np.arange(0, ROPE, 2, dtype=np.float32) / ROPE)
    t = np.zeros((4, LANES), np.float32)
    t[0, :32] = inv
    t[0, 32:64] = inv
    t[1, :64] = 1.0
    t[2, 32:64] = 1.0
    t[3, :32] = -1.0
    return jnp.asarray(t)


def _rope_tables(pos, consts, width):
    ang = pos * consts[0:1, :]
    cosv, sinv = jnp.cos(ang), jnp.sin(ang)
    reps = width // LANES
    tile = (lambda t: jnp.concatenate([t] * reps, axis=1)) if reps > 1 else (lambda t: t)
    return tile(cosv * consts[1:2, :]), tile(sinv * consts[2:3, :]), tile(sinv * consts[3:4, :])


def _rope_apply(t, tabs):
    cos_t, sin_a, sin_b = tabs
    width = t.shape[1]
    return t * cos_t + pltpu.roll(t, 32, 1) * sin_a + pltpu.roll(t, width - 32, 1) * sin_b


def _rope_transpose(d, tabs):
    cos_t, sin_a, sin_b = tabs
    width = d.shape[1]
    return d * cos_t + pltpu.roll(d * sin_a, width - 32, 1) + pltpu.roll(d * sin_b, 32, 1)


QK_HEAD = 2 * HEAD


def _interleave_heads(a, b):
    parts = []
    for h in range(N_HEADS):
        parts.append(a[:, h * HEAD:(h + 1) * HEAD])
        parts.append(b if b.shape[1] == LANES else b[:, h * LANES:(h + 1) * LANES])
    return jnp.concatenate(parts, axis=1)


def _mla_rows(proj_b):
    return [(proj_b, WB_CQ // Q_LORA, Q_LORA), (proj_b, WB_CKV // KV_LORA, KV_LORA), (proj_b, WB_KR // LANES, LANES)]


def _mla_prep_fwd(proj_b, pos, qn_w, kvn_w, uq, uk, uv):
    def fn(r, c):
        cq, ckv, kr, pos_ = r
        qn_w_, kvn_w_, uq_, uk_, uv_, rope = c
        c_q = _rms_fwd(cq, qn_w_).astype(BF16)
        c_kv = _rms_fwd(ckv, kvn_w_).astype(BF16)
        qf = _dot(c_q, uq_)
        qr = _rope_apply(qf[:, D_MODEL:], _rope_tables(pos_, rope, D_MODEL))
        kr = _rope_apply(kr, _rope_tables(pos_, rope, LANES))
        kc = _interleave_heads(_dot(c_kv, uk_), kr)
        v = _dot(c_kv, uv_)
        return [c_q, c_kv, _interleave_heads(qf[:, :D_MODEL], qr) * SCALE, kc, v, kc, v], []

    wide2 = N_HEADS * QK_HEAD
    return _rowwise(fn, _mla_rows(proj_b) + [pos], [qn_w, kvn_w, uq, uk, uv, _rope_consts()],
                    [(Q_LORA, BF16), (KV_LORA, BF16), (wide2, BF16), (wide2, BF16), (D_MODEL, BF16),
                     (wide2, BF16, "T"), (D_MODEL, BF16, "T")], tm=FUSED_ROWS, name="mla_prep_fwd")


def _mla_prep_bwd(proj_b, pos, d_qc, d_kc, d_v, qn_w, kvn_w, uq, uk, uv):
    def fn(r, c):
        cq, ckv, _, pos_, dq, dk, dv = r
        qn_w_, kvn_w_, uq_, uk_, uv_, rope = c
        even = lambda t: jnp.concatenate([t[:, (2 * h) * LANES:(2 * h + 1) * LANES] for h in range(N_HEADS)], axis=1)
        odd = lambda t: jnp.concatenate([t[:, (2 * h + 1) * LANES:(2 * h + 2) * LANES] for h in range(N_HEADS)], axis=1)
        d_qr_raw = _rope_transpose(odd(dq), _rope_tables(pos_, rope, D_MODEL)) * SCALE
        d_qf = jnp.concatenate([even(dq) * SCALE, d_qr_raw], axis=1).astype(BF16)
        d_kn = even(dk).astype(BF16)
        dkr = dk[:, LANES:2 * LANES]
        for h in range(1, N_HEADS):
            dkr = dkr + dk[:, (2 * h + 1) * LANES:(2 * h + 2) * LANES]
        d_cq, d_qnw = _rms_bwd(cq, qn_w_, _dot(d_qf, uq_, NT))
        d_ckv, d_kvnw = _rms_bwd(ckv, kvn_w_, _dot(d_kn, uk_, NT) + _dot(dv, uv_, NT))
        return [d_qf, d_kn, d_cq, d_ckv, _rope_transpose(dkr, _rope_tables(pos_, rope, LANES))], [d_qnw, d_kvnw]

    return _rowwise(fn, _mla_rows(proj_b) + [pos, d_qc, d_kc, d_v], [qn_w, kvn_w, uq, uk, uv, _rope_consts()],
                    [(2 * D_MODEL, BF16), (D_MODEL, BF16), (Q_LORA, BF16), (KV_LORA, BF16), (LANES, BF16)],
                    accs=[(1, Q_LORA), (1, KV_LORA)], tm=FUSED_ROWS, name="mla_prep_bwd")


def _causal_mask_t(st, key0, query0):
    key = lax.broadcasted_iota(jnp.int32, st.shape, 0) + key0
    query = lax.broadcasted_iota(jnp.int32, st.shape, 1) + query0
    return jnp.where(key <= query, st, NEG_BIG)


def _attn_tiles(s_dim):
    tq = min(512, s_dim)
    n_chains = 2 if s_dim >= 2 * tq else 1
    return tq, n_chains, min(512, s_dim)


def _diagonal_chains(t, tq, n_chains, tk):
    return [(c, (t + 1) * tk - 1 > c * tq) for c in range(n_chains) if t * tk < (c + 1) * tq]


def _attn_fwd(qc, kc, vt):
    s_dim = qc.shape[0]
    tq, n_chains, tk = _attn_tiles(s_dim)
    tqs = tq * n_chains

    def body(q_ref, k_ref, vt_ref, o_ref, lse_ref, m_s, l_s, acc):
        qi = pl.program_id(1)
        m_s[...] = jnp.full_like(m_s, NEG_BIG)
        l_s[...] = jnp.zeros_like(l_s)
        acc[...] = jnp.zeros_like(acc)

        def make_step(chains):
            def step(j, carry):
                ks = pl.multiple_of(j * tk, tk)
                kb, vtb = k_ref[pl.ds(ks, tk), :], vt_ref[:, pl.ds(ks, tk)]
                cols = [slice(c * tq, (c + 1) * tq) for c, _ in chains]
                sts = [_dot(kb, q_ref[cs, :], NT) for cs in cols]
                sts = [_causal_mask_t(st, j * tk, qi * tqs + c * tq) if masked else st
                       for st, (c, masked) in zip(sts, chains)]
                m_prevs = [m_s[:, cs] for cs in cols]
                m_news = [jnp.maximum(mp, jnp.max(st, axis=0, keepdims=True)) for mp, st in zip(m_prevs, sts)]
                alphas = [jnp.exp(mp - mn) for mp, mn in zip(m_prevs, m_news)]
                pts = [jnp.exp(st - mn) for st, mn in zip(sts, m_news)]
                pvs = [_dot(vtb, pt) for pt in pts]
                for cs, mn, al, pt, pv in zip(cols, m_news, alphas, pts, pvs):
                    l_s[:, cs] = al * l_s[:, cs] + _colsum(pt)
                    m_s[:, cs] = mn
                    acc[:, cs] = acc[:, cs] * al + pv
                return carry
            return step

        below = qi * (tqs // tk)
        lax.fori_loop(0, below, make_step([(c, False) for c in range(n_chains)]), 0)
        for t in range(tqs // tk):
            make_step(_diagonal_chains(t, tq, n_chains, tk))(below + t, 0)
        l = l_s[...]
        o_ref[...] = jnp.transpose(acc[...] / l)
        lse_ref[...] = m_s[...] + jnp.log(l)

    return pl.pallas_call(
        body,
        out_shape=[jax.ShapeDtypeStruct((s_dim, N_HEADS * HEAD), F32), jax.ShapeDtypeStruct((N_HEADS, 1, s_dim), F32)],
        grid=(N_HEADS, s_dim // tqs),
        in_specs=[pl.BlockSpec((tqs, QK_HEAD), lambda h, qi: (qi, h)),
                  pl.BlockSpec((s_dim, QK_HEAD), lambda h, qi: (0, h)),
                  pl.BlockSpec((HEAD, s_dim), lambda h, qi: (h, 0))],
        out_specs=[pl.BlockSpec((tqs, HEAD), lambda h, qi: (qi, h)),
                   pl.BlockSpec((None, 1, tqs), lambda h, qi: (h, 0, qi))],
        scratch_shapes=[pltpu.VMEM((1, tqs), F32), pltpu.VMEM((1, tqs), F32), pltpu.VMEM((HEAD, tqs), F32)],
        compiler_params=pltpu.CompilerParams(dimension_semantics=("parallel", "parallel")),
        name="attn_fwd",
    )(qc, kc, vt)


def _attn_bwd(qc, kc, kct, v, o, d_o, lse):
    s_dim = qc.shape[0]
    tq, n_chains, tk = _attn_tiles(s_dim)
    tqs = tq * n_chains

    def body(q_ref, k_ref, kt_ref, v_ref, o_ref, do_ref, lse_ref, dq_ref, dk_ref, dv_ref, dqt_acc, dv_acc):
        qi = pl.program_id(1)

        @pl.when(qi == 0)
        def _():
            dk_ref[...] = jnp.zeros_like(dk_ref)
            dv_acc[...] = jnp.zeros_like(dv_acc)

        dqt_acc[...] = jnp.zeros_like(dqt_acc)
        do_f = do_ref[...]
        do_all = do_f.astype(BF16)
        q_all = q_ref[...]
        lse_row = lse_ref[...]
        delta_row = _dot3(jnp.ones((8, HEAD), F32), o_ref[...] * do_f, NT)[0:1, :]

        def make_step(chains):
            rows = slice(chains[0][0] * tq, (chains[-1][0] + 1) * tq)

            def step(j, carry):
                ks = pl.multiple_of(j * tk, tk)
                kb, vb, ktb = k_ref[pl.ds(ks, tk), :], v_ref[pl.ds(ks, tk), :], kt_ref[:, pl.ds(ks, tk)]
                cols = [slice(c * tq, (c + 1) * tq) for c, _ in chains]
                sts = [_dot(kb, q_all[cs, :], NT) for cs in cols]
                sts = [_causal_mask_t(st, j * tk, qi * tqs + c * tq) if masked else st
                       for st, (c, masked) in zip(sts, chains)]
                dpts = [_dot(vb, do_all[cs, :], NT) for cs in cols]
                pts = [jnp.exp(st - lse_row[:, cs]) for st, cs in zip(sts, cols)]
                dsts = [(pt * (dpt - delta_row[:, cs])).astype(BF16) for pt, dpt, cs in zip(pts, dpts, cols)]
                pts = [pt.astype(BF16) for pt in pts]
                dqs = [_dot(ktb, dst) for dst in dsts]
                for cs, dq in zip(cols, dqs):
                    dqt_acc[:, cs] += dq
                pt_all = jnp.concatenate(pts, axis=1) if len(chains) > 1 else pts[0]
                dst_all = jnp.concatenate(dsts, axis=1) if len(chains) > 1 else dsts[0]
                dk_ref[pl.ds(ks, tk), :] += _dot(dst_all, q_all[rows, :])
                dv_acc[pl.ds(ks, tk), :] += _dot(pt_all, do_all[rows, :])
                return carry
            return step

        below = qi * (tqs // tk)
        lax.fori_loop(0, below, make_step([(c, False) for c in range(n_chains)]), 0)
        for t in range(tqs // tk):
            make_step(_diagonal_chains(t, tq, n_chains, tk))(below + t, 0)
        dq_ref[...] = jnp.transpose(dqt_acc[...])

        @pl.when(qi == s_dim // tqs - 1)
        def _():
            dv_ref[...] = dv_acc[...].astype(dv_ref.dtype)

    q_spec = pl.BlockSpec((tqs, QK_HEAD), lambda h, qi: (qi, h))
    o_spec = pl.BlockSpec((tqs, HEAD), lambda h, qi: (qi, h))
    k_spec = pl.BlockSpec((s_dim, QK_HEAD), lambda h, qi: (0, h))
    v_spec = pl.BlockSpec((s_dim, HEAD), lambda h, qi: (0, h))
    wide2 = jax.ShapeDtypeStruct((s_dim, N_HEADS * QK_HEAD), F32)
    return pl.pallas_call(
        body,
        out_shape=[wide2, wide2, jax.ShapeDtypeStruct((s_dim, N_HEADS * HEAD), BF16)],
        grid=(N_HEADS, s_dim // tqs),
        in_specs=[q_spec, k_spec, pl.BlockSpec((QK_HEAD, s_dim), lambda h, qi: (h, 0)), v_spec, o_spec, o_spec,
                  pl.BlockSpec((None, 1, tqs), lambda h, qi: (h, 0, qi))],
        out_specs=[q_spec, k_spec, v_spec],
        scratch_shapes=[pltpu.VMEM((QK_HEAD, tqs), F32), pltpu.VMEM((s_dim, HEAD), F32)],
        compiler_params=pltpu.CompilerParams(dimension_semantics=("parallel", "arbitrary")),
        name="attn_bwd",
    )(qc, kc, kct, v, o, d_o, lse)


def _mix_proj_ln1(y_dn, y_mla, proj_g, x, w_o, g, b):
    s_dim = x.shape[0]
    tm = min(512, s_dim)

    def body(yd_ref, ym_ref, g_ref, x_ref, w_ref, lg_ref, lb_ref, mixed_ref, a1_ref, h1_ref, h1b_ref):
        gates = g_ref[...].astype(F32)
        mixed = (_sig(gates[:, :D_MODEL]) * yd_ref[...].astype(F32)
                 + _sig(gates[:, D_MODEL:]) * ym_ref[...].astype(F32)).astype(BF16)
        a1 = _dot(mixed, w_ref[...])
        xh, _ = _ln_stats(ALPHA * x_ref[...] + a1)
        y = xh * lg_ref[...] + lb_ref[...]
        mixed_ref[...] = mixed
        a1_ref[...] = a1
        h1_ref[...] = y
        h1b_ref[...] = y.astype(BF16)

    row = lambda width: pl.BlockSpec((tm, width), lambda i: (i, 0))
    whole = lambda a: pl.BlockSpec(a.shape, lambda i: (0, 0))
    sds = lambda dt: jax.ShapeDtypeStruct((s_dim, D_MODEL), dt)
    return pl.pallas_call(
        body,
        out_shape=[sds(BF16), sds(F32), sds(F32), sds(BF16)],
        grid=(s_dim // tm,),
        in_specs=[row(D_MODEL), row(D_MODEL), row(2 * D_MODEL), row(D_MODEL), whole(w_o), whole(g), whole(b)],
        out_specs=[row(D_MODEL)] * 4,
        compiler_params=pltpu.CompilerParams(dimension_semantics=("parallel",)),
        name="mix_proj_ln1",
    )(y_dn, y_mla, proj_g, x, w_o, g, b)


def _ln1_mix_bwd(x, a1, d_h1, d_pg, y_dn, y_mla, proj_g, g, w_o, w_pg):
    def fn(r, c):
        x_, a1_, dy, dpg, yd, ym, gates = r
        g_, w_o_, w_pg_ = c
        dy = dy + _dot(dpg, w_pg_, NT)
        xh, rr = _ln_stats(ALPHA * x_ + a1_)
        dz = _ln_bwd(dy, xh, rr, g_)
        dz_b = dz.astype(BF16)
        dm = _dot(dz_b, w_o_, NT)
        sd, sm = _sig(gates[:, :D_MODEL]), _sig(gates[:, D_MODEL:])
        d_g = jnp.concatenate([dm * yd * sd * (1.0 - sd), dm * ym * sm * (1.0 - sm)], axis=1)
        return [dz_b, ALPHA * dz, d_g, dm * sd, dm * sm], [_colsum(dy * xh), _colsum(dy)]

    return _rowwise(fn, [x, a1, d_h1, d_pg, y_dn, y_mla, proj_g], [g, w_o, w_pg],
                    [(D_MODEL, BF16), (D_MODEL, F32), (2 * D_MODEL, BF16), (D_MODEL, BF16), (D_MODEL, BF16)],
                    accs=[(1, D_MODEL), (1, D_MODEL)], tm=FUSED_ROWS, name="ln1_mix_bwd")


def _ln_stats(z):
    mu = _rowmean(z)
    zc = z - mu
    r = lax.rsqrt(_rowmean(zc * zc) + EPS_LN)
    return zc * r, r


def _ln_bwd(dy, xh, r, g):
    dxh = dy * g
    return r * (dxh - _rowmean(dxh) - xh * _rowmean(dxh * xh))


def _ffn_in_act(h1b, w_t):
    s_dim, k_dim = h1b.shape
    hidden = w_t.shape[0] // 2
    tm, tn = min(512, s_dim), _pick_wide(hidden)
    nt = hidden // tn

    def body(a_ref, bg_ref, bu_ref, gt_ref, up_ref, act_ref):
        a = a_ref[...]
        gt, up = _dot(a, bg_ref[...], NT), _dot(a, bu_ref[...], NT)
        gt_ref[...] = gt.astype(BF16)
        up_ref[...] = up.astype(BF16)
        act_ref[...] = (gt * _sig(gt) * up).astype(BF16)

    o_spec = pl.BlockSpec((tm, tn), lambda j, i: (i, j))
    sds = jax.ShapeDtypeStruct((s_dim, hidden), BF16)
    return pl.pallas_call(
        body,
        out_shape=[sds, sds, sds],
        grid=(nt, s_dim // tm),
        in_specs=[pl.BlockSpec((tm, k_dim), lambda j, i: (i, 0)), pl.BlockSpec((tn, k_dim), lambda j, i: (j, 0)),
                  pl.BlockSpec((tn, k_dim), lambda j, i: (j + nt, 0))],
        out_specs=[o_spec, o_spec, o_spec],
        compiler_params=pltpu.CompilerParams(dimension_semantics=("parallel", "parallel")),
        name="ffn_in_act",
    )(h1b, w_t, w_t)


def _act_bwd(gt, up, d_act):
    def fn(r, c):
        gt_, up_, da = r
        return [jnp.concatenate([da * up_ * _silu_grad(gt_), da * gt_ * _sig(gt_)], axis=1)], []

    return _rowwise(fn, [gt, up, d_act], [], [(2 * FFN_HIDDEN, BF16)], name="act_bwd")[0]


def _tail(h1, ffn, p, tgt, g, b, w_pg, w_ple_t):
    def fn(r, c):
        h1_, ffn_, p_, t_ = r
        pg_ = _dot(h1_, c[2])
        pp_ = _dot(p_, c[3], NT)
        sp = _sig(pg_)
        xh, rr = _ln_stats(ALPHA * h1_ + ffn_ + sp * pp_)
        y = xh * c[0] + c[1]
        err = y - t_
        dy = err * (1.0 / D_MODEL)
        dz = _ln_bwd(dy, xh, rr, c[0])
        loss = jnp.sum(0.5 * _rowmean(err * err), axis=0, keepdims=True)
        return ([dz, dz * pp_ * sp * (1.0 - sp), dz * sp, ALPHA * dz],
                [_colsum(dy * xh), _colsum(dy), jnp.broadcast_to(loss, (1, LANES))])

    return _rowwise(fn, [h1, ffn, p, tgt], [g, b, w_pg, w_ple_t], [(D_MODEL, BF16)] * 3 + [(D_MODEL, F32)],
                    accs=[(1, D_MODEL), (1, D_MODEL), (1, LANES)], tm=FUSED_ROWS, name="tail")


def _local_step(x, p, pos, tgt, w, late_weights, emit):
    w = dict(w)
    s_dim = x.shape[0]
    xb, pb = x.astype(BF16), p.astype(BF16)
    proj_a = _mm(xb, w["w_in_t"], tb=True, b_rows=(0, 4 * D_MODEL), name="f_proj_a")
    proj_g = _mm(xb, w["wg_t"], tb=True, out_dtype=BF16, name="f_proj_g")
    proj_b = _mm(xb, w["wb_t"], tb=True, name="f_proj_b")
    qkvn = _conv_fwd(proj_a, w["conv"])
    beta, gc = _gates_fwd(proj_b, w["alog"], w["dtb"])
    gc_t = jnp.transpose(gc[:, :N_HEADS])
    u, w_, qd, kt, a_mat, t_fold = _gdr_prep_fwd(qkvn, beta, gc, gc_t)
    o_dn, states = _gdr_scan_fwd(u, w_, qd, kt, a_mat, gc)
    w.update(late_weights("mix", o_dn))
    og, y_dn = _gdr_out_fwd(o_dn, proj_a, w["dnw"], w["br_dn"])
    c_q, c_kv, qc, kc, vv, kct, vt = _mla_prep_fwd(proj_b, pos, w["qnw"], w["kvnw"], w["uq"], w["uk"], w["uv"])
    o_mla, lse = _attn_fwd(qc, kc, vt)
    y_mla = _mm(o_mla, w["br_mla"], out_dtype=BF16, name="f_y_mla")
    mixed, a1, h1, h1b = _mix_proj_ln1(y_dn, y_mla, proj_g, x, w["wo"], w["ln1g"], w["ln1b"])
    w.update(late_weights("ffn", a1))
    gt, up, act = _ffn_in_act(h1b, w["ffn_in_t"])
    ffn = _mm(act, w["ffn_out"], name="f_ffn")
    g = {}
    dz2, d_pg, d_pp, dh1a, g["ln2g"], g["ln2b"], loss = _tail(h1, ffn, pb, tgt, w["ln2g"], w["ln2b"],
                                                            w["ple_gate"], w["ple_t"])
    g["ple_t"] = _mm(d_pp, pb, ta=True, out_dtype=BF16, name="b_w_ple")
    g["ple_gate"] = _mm(h1b, d_pg, ta=True, out_dtype=BF16, name="b_w_ple_gate")
    g["ffn_out"] = _mm(act, dz2, ta=True, out_dtype=BF16, name="b_w_ffn_out")
    d_act = _mm(dz2, w["ffn_out"], tb=True, out_dtype=BF16, name="b_act")
    d_gu = _act_bwd(gt, up, d_act)
    g["ffn_in_t"] = _mm(d_gu, h1b, ta=True, out_dtype=BF16, name="b_w_ffn_in")
    d_gu = emit("ffn", g, d_gu)
    d_h1 = _mm(d_gu, w["ffn_in_t"], add=(dh1a,), name="b_h1_ffn")
    dz1, dxa, d_proj_g, d_y_dn, d_y_mla, g["ln1g"], g["ln1b"] = _ln1_mix_bwd(
        x, a1, d_h1, d_pg, y_dn, y_mla, proj_g, w["ln1g"], w["wo"], w["ple_gate"])
    g["wo"] = _mm(mixed, dz1, ta=True, out_dtype=BF16, name="b_w_o")
    g["br_mla"] = _mm(o_mla, d_y_mla, ta=True, out_dtype=BF16, name="b_w_br_mla")
    d_o_mla = _mm(d_y_mla, w["br_mla"], tb=True, out_dtype=BF16, name="b_o_mla")
    d_qc, d_kc, d_v = _attn_bwd(qc, kc, kct, vv, o_mla, d_o_mla, lse)
    d_q_full, d_kn, d_cq, d_ckv, d_kr, g["qnw"], g["kvnw"] = _mla_prep_bwd(
        proj_b, pos, d_qc, d_kc, d_v, w["qnw"], w["kvnw"], w["uq"], w["uk"], w["uv"])
    g["uq"] = _mm(c_q, d_q_full, ta=True, out_dtype=BF16, name="b_w_uq")
    g["uk"] = _mm(c_kv, d_kn, ta=True, out_dtype=BF16, name="b_w_uk")
    g["uv"] = _mm(c_kv, d_v, ta=True, out_dtype=BF16, name="b_w_uv")
    g["br_dn"] = _mm(og, d_y_dn, ta=True, out_dtype=BF16, name="b_w_br_dn")
    d_y_dn = emit("mix", g, d_y_dn)
    d_o_dn, d_z, g["dnw"] = _gdr_out_bwd(o_dn, proj_a, d_y_dn, w["dnw"], w["br_dn"])
    du, dw, dqd, dkt, d_a, d_egl = _gdr_scan_bwd(u, w_, qd, kt, a_mat, gc, states, d_o_dn)
    dq, dk, dv, d_beta, d_gc = _gdr_prep_bwd(qkvn, beta, gc, gc_t, t_fold, u, w_, du, dw, dqd, dkt, d_a)
    d_egl_rows = jnp.pad(d_egl[:, None, :, 0], ((0, 0), (CHUNK - 1, 0), (0, LANES - N_HEADS))).reshape(s_dim, LANES)
    d_ba, g["alog"], g["dtb"] = _gates_bwd(proj_b, w["alog"], w["dtb"], gc, d_beta, d_gc, d_egl_rows)
    d_qkv, g["conv"] = _conv_bwd(proj_a, w["conv"], dq, dk, dv)
    zeros = jnp.zeros((s_dim, WB_CKV - Q_LORA), BF16)
    d_proj_b = jnp.concatenate([d_cq, zeros, d_ckv, d_kr, d_ba], axis=1)
    g["wa_qkv_t"] = _mm(d_qkv, xb, ta=True, name="b_w_qkv")
    g["wa_z_t"] = _mm(d_z, xb, ta=True, name="b_w_z")
    g["wg_t"] = _mm(d_proj_g, xb, ta=True, name="b_w_g")
    g["wb_t"] = _mm(d_proj_b, xb, ta=True, name="b_w_b")
    d_qkv = emit("small", dict(g, loss=loss), emit("w_in", g, d_qkv))
    dx = _input_grad(d_qkv, d_z, d_proj_g, d_proj_b, w["w_in_t"], w["wg_t"], w["wb_t"], dxa)
    return loss, dx, g


DX_ROWS = 256


def _input_grad(d_qkv, d_z, d_g, d_b, w_in_t, wg_t, wb_t, add):
    s_dim = d_qkv.shape[0]
    n_qkv, n_a = d_qkv.shape[1], d_qkv.shape[1] + d_z.shape[1]

    def body(q_ref, z_ref, g_ref, b_ref, wa_ref, wg_ref, wb_ref, add_ref, o_ref):
        r = add_ref[...] + _dot(q_ref[...], wa_ref[0:n_qkv])
        r = r + _dot(z_ref[...], wa_ref[n_qkv:n_a])
        r = r + _dot(g_ref[...], wg_ref[...])
        o_ref[...] = r + _dot(b_ref[...], wb_ref[...])

    rows = lambda a: pl.BlockSpec((DX_ROWS, a.shape[1]), lambda i: (i, 0))
    whole = lambda shape: pl.BlockSpec(shape, lambda i: (0, 0))
    return pl.pallas_call(
        body,
        out_shape=jax.ShapeDtypeStruct((s_dim, D_MODEL), F32),
        grid=(s_dim // DX_ROWS,),
        in_specs=[rows(d_qkv), rows(d_z), rows(d_g), rows(d_b), whole((n_a, D_MODEL)), whole(wg_t.shape),
                  whole(wb_t.shape), rows(add)],
        out_specs=pl.BlockSpec((DX_ROWS, D_MODEL), lambda i: (i, 0)),
        compiler_params=pltpu.CompilerParams(dimension_semantics=("parallel",)),
        name="b_x",
    )(d_qkv, d_z, d_g, d_b, w_in_t, wg_t, wb_t, add)


_BIG = (("w_in", 1), ("w_uq", 0), ("w_uk", 0), ("w_uv", 0), ("w_br_dn", 0), ("w_br_mla", 0),
        ("w_o", 0), ("w_ffn_in", 1), ("w_ffn_out", 0), ("w_ple", 1), ("w_ple_gate", 0))
_BIG_AXIS = dict(_BIG)
_SMALL = ("ln1_g", "ln1_b", "ln2_g", "ln2_b", "q_norm_w", "kv_norm_w", "dn_norm_w", "dn_a_log", "dn_dt_bias")
_ORDER = ("w_in", "conv_w", "dn_a_log", "dn_dt_bias", "dn_norm_w", "q_norm_w", "w_uq", "kv_norm_w", "w_uk", "w_uv",
          "w_br_dn", "w_br_mla", "w_o", "ln1_g", "ln1_b", "w_ffn_in", "w_ffn_out", "w_ple", "w_ple_gate", "ln2_g",
          "ln2_b")


def _stored_shape(name, shard_shape):
    axis = _BIG_AXIS[name]
    lead = shard_shape[axis]
    return lead, int(np.prod(shard_shape)) // lead


def _to_stored(name, shard):
    return jnp.moveaxis(shard, _BIG_AXIS[name], 0).reshape(_stored_shape(name, shard.shape))


def _from_stored(name, stored, shard_shape):
    axis = _BIG_AXIS[name]
    moved = (shard_shape[axis],) + shard_shape[:axis] + shard_shape[axis + 1:]
    return jnp.moveaxis(stored.reshape(moved), 0, axis)


_W_IN_ROWS = np.cumsum([0, 3072, 1024, 8, 8, Q_LORA, KV_LORA, ROPE, D_MODEL, D_MODEL])


def _first_weights(w_in_t, conv_full, small):
    r = _W_IN_ROWS
    zr = lambda n: jnp.zeros((n, D_MODEL), w_in_t.dtype)
    w = {}
    w["w_in_t"] = w_in_t
    w["wg_t"] = w_in_t[r[7]:r[9]]
    w["wb_t"] = jnp.concatenate([w_in_t[r[4]:r[5]], zr(WB_CKV - Q_LORA), w_in_t[r[5]:r[7]], zr(LANES - ROPE),
                                 w_in_t[r[2]:r[4]], zr(LANES - 2 * N_HEADS)], axis=0)
    w["conv"] = conv_full
    pad_l = lambda v: jnp.pad(v, ((0, 0), (0, LANES - v.shape[1])))
    w["alog"], w["dtb"] = pad_l(small["dn_a_log"]), pad_l(small["dn_dt_bias"])
    w["dnw"], w["qnw"], w["kvnw"] = small["dn_norm_w"], small["q_norm_w"], small["kv_norm_w"]
    w["ln1g"], w["ln1b"], w["ln2g"], w["ln2b"] = small["ln1_g"], small["ln1_b"], small["ln2_g"], small["ln2_b"]
    return w


def _late_weights(group, fw):
    w = {}
    if group == "mix":
        uq = fw["w_uq"].reshape(Q_LORA, N_HEADS, HEAD + ROPE)
        uq_r = jnp.pad(uq[:, :, HEAD:], ((0, 0), (0, 0), (0, HEAD - ROPE)))
        w["uq"] = jnp.concatenate([uq[:, :, :HEAD].reshape(Q_LORA, -1), uq_r.reshape(Q_LORA, -1)], axis=1)
        w["uk"], w["uv"] = fw["w_uk"], fw["w_uv"]
        w["br_dn"], w["br_mla"], w["wo"] = fw["w_br_dn"], fw["w_br_mla"], fw["w_o"]
    else:
        w["ffn_in_t"], w["ffn_out"] = fw["w_ffn_in"], fw["w_ffn_out"]
        w["ple_t"], w["ple_gate"] = fw["w_ple"], fw["w_ple_gate"]
    return w


_GROUP_GRADS = {"ffn": (("w_ple", "ple_t"), ("w_ple_gate", "ple_gate"), ("w_ffn_out", "ffn_out"),
                        ("w_ffn_in", "ffn_in_t")),
                "mix": (("w_o", "wo"), ("w_br_mla", "br_mla"), ("w_uq", "uq"), ("w_uk", "uk"), ("w_uv", "uv"),
                        ("w_br_dn", "br_dn"))}


def _group_grads(group, g):
    out = {}
    for name, key in _GROUP_GRADS[group]:
        t = g[key]
        if name == "w_uq":
            uq_n = t[:, :D_MODEL].reshape(Q_LORA, N_HEADS, HEAD)
            uq_r = t[:, D_MODEL:].reshape(Q_LORA, N_HEADS, HEAD)[:, :, :ROPE]
            t = jnp.concatenate([uq_n, uq_r], axis=2).reshape(Q_LORA, -1)
        out[name] = t
    return out


PACK_ROWS = 512
SUBLANES = 8


def _pack_exchange(parts, name):
    arrays = []
    for a, _, _ in parts:
        if not any(a is b for b in arrays):
            arrays.append(a)
    index = lambda a: next(i for i, b in enumerate(arrays) if a is b)
    chunks, dst = [], 0
    for a, first, rows in parts:
        assert first % SUBLANES == 0 and rows % SUBLANES == 0
        chunks += [(index(a), first + o, dst + o, min(PACK_ROWS, rows - o)) for o in range(0, rows, PACK_ROWS)]
        dst += rows
    c, n, last = arrays[0].shape[1], len(arrays), len(chunks) - 1
    slab = dst // N_DEV
    assert slab * N_DEV == dst

    def body(*refs):
        src_refs, out_ref, recv_ref = refs[:n], refs[n], refs[n + 1]
        buf, sem_in, sem_out, send_sems, recv_sems = refs[n + 2:]
        x, y, core = lax.axis_index("x"), lax.axis_index("y"), lax.axis_index("c")

        def to_sibling(q):
            return pltpu.make_async_remote_copy(
                src_ref=out_ref.at[pl.ds((2 * q + 1 - core) * slab, slab)], dst_ref=recv_ref.at[q],
                send_sem=send_sems.at[q], recv_sem=recv_sems.at[q], device_id=(x, y, 1 - core),
                device_id_type=_MESH_ID)

        sent = [0]

        def send_packed(rows_done):
            while sent[0] < N_DEV // 2 and (2 * sent[0] + 2) * slab <= rows_done:
                to_sibling(sent[0]).start()
                sent[0] += 1

        def load(k):
            i, first, _, rows = chunks[k]
            return pltpu.make_async_copy(src_refs[i].at[pl.ds(first, rows)], buf.at[k % 2, pl.ds(0, rows)],
                                         sem_in.at[k % 2])

        def store(k):
            _, _, first, rows = chunks[k]
            return pltpu.make_async_copy(buf.at[k % 2, pl.ds(0, rows)], out_ref.at[pl.ds(first, rows), 0, :],
                                         sem_out.at[k % 2])

        load(0).start()
        for k in range(last + 1):
            load(k).wait()
            store(k).start()
            if k >= 1:
                store(k - 1).wait()
                send_packed(chunks[k][2])
            if k < last:
                load(k + 1).start()
        store(last).wait()
        send_packed(dst)
        for q in range(N_DEV // 2):
            to_sibling(q).wait_recv()
        for q in range(N_DEV // 2):
            to_sibling(q).wait_send()

    return pl.pallas_call(
        body,
        out_shape=[jax.ShapeDtypeStruct((dst, 1, c), F32), jax.ShapeDtypeStruct((N_DEV // 2, slab, 1, c), F32)],
        in_specs=[_ANY] * n,
        out_specs=[_ANY, _ANY],
        scratch_shapes=[pltpu.VMEM((2, PACK_ROWS, c), F32), pltpu.SemaphoreType.DMA((2,)),
                        pltpu.SemaphoreType.DMA((2,)), pltpu.SemaphoreType.DMA((N_DEV // 2,)),
                        pltpu.SemaphoreType.DMA((N_DEV // 2,))],
        name=name,
    )(*arrays)


def _w_in_grad_parts(g):
    wb = g["wb_t"]
    return [(g["wa_qkv_t"], 0, 3 * D_MODEL), (g["wa_z_t"], 0, D_MODEL), (wb, WB_BA, 2 * N_HEADS),
            (wb, WB_CQ, Q_LORA), (wb, WB_CKV, KV_LORA), (wb, WB_KR, ROPE), (g["wg_t"], 0, 2 * D_MODEL)]


def _small_grads(g):
    return {"ln1_g": g["ln1g"], "ln1_b": g["ln1b"], "ln2_g": g["ln2g"], "ln2_b": g["ln2b"], "q_norm_w": g["qnw"],
            "kv_norm_w": g["kvnw"], "dn_norm_w": g["dnw"], "dn_a_log": g["alog"], "dn_dt_bias": g["dtb"],
            "conv_w": g["conv"]}


_SMALL_SLOTS = {"ln1_g": (0, 0, 1024), "ln1_b": (1, 0, 1024), "ln2_g": (2, 0, 1024), "ln2_b": (3, 0, 1024),
                "q_norm_w": (4, 0, 384), "kv_norm_w": (4, 384, 256), "dn_norm_w": (4, 640, 128),
                "dn_a_log": (4, 768, 8), "dn_dt_bias": (4, 896, 8)}
_SMALL_ROWS, _LOSS_ROW, _CONV_ROW0, _CONV_ROWS = 24, 5, 8, 12


def _pack_small_grads(small_g, loss):
    zeros = lambda r, c: jnp.zeros((r, c), F32)
    row4 = jnp.concatenate([small_g["q_norm_w"], small_g["kv_norm_w"], small_g["dn_norm_w"], small_g["dn_a_log"],
                            small_g["dn_dt_bias"]], axis=1)
    row5 = jnp.concatenate([loss, zeros(1, FLAT_COLS - LANES)], axis=1)
    head = jnp.concatenate([small_g["ln1_g"], small_g["ln1_b"], small_g["ln2_g"], small_g["ln2_b"], row4, row5,
                            zeros(2, FLAT_COLS)], axis=0)
    conv = small_g["conv_w"].reshape(_CONV_ROWS, FLAT_COLS)
    return jnp.concatenate([head, conv, zeros(_SMALL_ROWS - _CONV_ROW0 - _CONV_ROWS, FLAT_COLS)], axis=0)


_MESH_ID = pl.DeviceIdType.MESH
_ANY = pl.BlockSpec(memory_space=pl.ANY)


def _all_gather(blocks, name):
    n = len(blocks)

    def body(*refs):
        x_refs, out_refs = refs[:n], refs[n:2 * n]
        send_sems, recv_sems, local_sems = refs[2 * n:]
        x, y, c = lax.axis_index("x"), lax.axis_index("y"), lax.axis_index("c")
        me, sibling = (x, y, c), (x, y, 1 - c)
        chips = [(1 - x, y), (x, 1 - y), (1 - x, 1 - y)]

        def slot(i, px, py, pc):
            return out_refs[i].at[4 * px + 2 * py + pc]

        def copy(i, k, origin, to, src=None):
            return pltpu.make_async_remote_copy(
                src_ref=slot(i, *origin) if src is None else src, dst_ref=slot(i, *origin),
                send_sem=send_sems.at[7 * i + k], recv_sem=recv_sems.at[7 * i + k], device_id=to,
                device_id_type=_MESH_ID)

        mine = [pltpu.make_async_copy(x_refs[i], slot(i, *me), local_sems.at[i]) for i in range(n)]
        first, passed = [], []
        for i in range(n):
            mine[i].start()
            first.append(copy(i, 0, me, sibling, src=x_refs[i]))
            first += [copy(i, 1 + j, me, (*chip, c), src=x_refs[i]) for j, chip in enumerate(chips)]
        for cp in first:
            cp.start()
        for i in range(n):
            for j, chip in enumerate(chips):
                copy(i, 1 + j, (*chip, c), me).wait_recv()
                passed.append(copy(i, 4 + j, (*chip, c), sibling))
                passed[-1].start()
        for i in range(n):
            copy(i, 0, sibling, me).wait_recv()
            for j, chip in enumerate(chips):
                copy(i, 4 + j, (*chip, 1 - c), me).wait_recv()
        for cp in first + passed:
            cp.wait_send()
        for cp in mine:
            cp.wait()

    return pl.pallas_call(
        body,
        out_shape=[jax.ShapeDtypeStruct((N_DEV,) + b.shape, b.dtype) for b in blocks],
        in_specs=[_ANY] * n,
        out_specs=[_ANY] * n,
        scratch_shapes=[pltpu.SemaphoreType.DMA((7 * n,)), pltpu.SemaphoreType.DMA((7 * n,)),
                        pltpu.SemaphoreType.DMA((n,))],
        name=name,
    )(*blocks)


def _col_tile(c):
    return c if c <= 256 else 256


def _chip_sum(src, recv, parity, name):
    _, r, _, c = src.shape
    tc = _col_tile(c)

    def body(par_ref, a_ref, b_ref, o_ref, ob_ref):
        s = a_ref[...] + b_ref[...]
        o_ref[...] = s
        ob_ref[...] = s.astype(BF16)

    rows = lambda f: pl.BlockSpec((None, r, None, tc), f)
    blk = pl.BlockSpec((None, r, tc), lambda q, j, par: (q, 0, j))
    return pl.pallas_call(
        body,
        out_shape=[jax.ShapeDtypeStruct((4, r, c), F32), jax.ShapeDtypeStruct((4, r, c), BF16)],
        grid_spec=pltpu.PrefetchScalarGridSpec(
            num_scalar_prefetch=1, grid=(4, c // tc),
            in_specs=[rows(lambda q, j, par: (2 * q + par[0], 0, 0, j)), rows(lambda q, j, par: (q, 0, 0, j))],
            out_specs=[blk, blk]),
        compiler_params=pltpu.CompilerParams(dimension_semantics=("parallel", "parallel")),
        name=name,
    )(parity, src, recv)


_HBM = pl.BlockSpec(memory_space=pltpu.HBM)
_SEM = pl.BlockSpec(memory_space=pltpu.SEMAPHORE)
_DATAFLOW = pltpu.SideEffectType.DATAFLOW_SIDE_EFFECTING
N_PEERS = N_DEV - 1


def _ring_peer(j):
    me = 4 * lax.axis_index("x") + 2 * lax.axis_index("y") + lax.axis_index("c")
    k = (me + j) % N_DEV
    return me, k, (k // 4, (k // 2) % 2, k % 2)


def _spread_copy(i, j, src_refs, land_refs, send_sems, recv_sems, scatter):
    me, k, peer = _ring_peer(j)
    return pltpu.make_async_remote_copy(
        src_ref=src_refs[i].at[k] if scatter else src_refs[i], dst_ref=land_refs[i].at[me],
        send_sem=send_sems.at[N_PEERS * i + j - 1], recv_sem=recv_sems.at[N_PEERS * i + j - 1], device_id=peer,
        device_id_type=_MESH_ID)


def _spread_start(srcs, carry, scatter, name):
    n = len(srcs)
    lands = [lax.empty(((N_DEV,) + s.shape[-2:]), s.dtype) for s in srcs]

    def body(*refs):
        src_refs, land_refs = refs[:n], refs[n:2 * n]
        send_sems, recv_sems, local_sems = refs[2 * n + 1:2 * n + 4]
        for i in range(n):
            for j in range(1, N_DEV):
                _spread_copy(i, j, src_refs, land_refs, send_sems, recv_sems, scatter).start()
        for i in range(n):
            _own_copy(i, src_refs, land_refs, local_sems, scatter).start()

    hbm = lambda a: pltpu.HBM(a.shape, a.dtype)
    sems = pltpu.SemaphoreType.DMA((N_PEERS * n,))
    pinned = [pltpu.with_memory_space_constraint(a, pltpu.HBM) for a in list(srcs) + lands + [carry]]
    res = pl.pallas_call(
        body, name=name,
        out_shape=(sems, sems, pltpu.SemaphoreType.DMA((n,)), *[hbm(a) for a in pinned]),
        in_specs=[_HBM] * (2 * n + 1),
        out_specs=(_SEM, _SEM, _SEM, *[_HBM] * (2 * n + 1)),
        input_output_aliases={i: 3 + i for i in range(2 * n + 1)},
        compiler_params=pltpu.CompilerParams(has_side_effects=_DATAFLOW),
    )(*pinned)
    return res[:3], list(res[3:3 + n]), list(res[3 + n:3 + 2 * n]), res[3 + 2 * n]


def _own_copy(i, src_refs, land_refs, local_sems, scatter):
    me = _ring_peer(0)[0]
    return pltpu.make_async_copy(src_refs[i].at[me] if scatter else src_refs[i], land_refs[i].at[me],
                                 local_sems.at[i])


def _spread_wait(started, after, scatter, name):
    sems, srcs, lands, _ = started
    n = len(srcs)

    def body(*refs):
        src_refs, land_refs = refs[:n], refs[n:2 * n]
        send_s, recv_s, local_s = refs[2 * n:2 * n + 3]
        for i in range(n):
            for j in range(1, N_DEV):
                cp = _spread_copy(i, j, src_refs, land_refs, send_s, recv_s, scatter)
                cp.wait_send()
                cp.wait_recv()
        for i in range(n):
            _own_copy(i, src_refs, land_refs, local_s, scatter).wait()

    hbm = lambda a: pltpu.HBM(a.shape, a.dtype)
    res = pl.pallas_call(
        body, name=name,
        out_shape=tuple(hbm(a) for a in srcs + lands),
        in_specs=[_HBM] * (2 * n) + [_SEM, _SEM, _SEM, pl.BlockSpec(memory_space=pl.ANY)],
        out_specs=tuple([_HBM] * (2 * n)),
        input_output_aliases={i: i for i in range(2 * n)},
        compiler_params=pltpu.CompilerParams(has_side_effects=_DATAFLOW),
    )(*srcs, *lands, *sems, after)
    return list(res[n:])


def _chips_copy(i, j, src_refs, land_refs, send_sems, recv_sems):
    x, y, c = lax.axis_index("x"), lax.axis_index("y"), lax.axis_index("c")
    tx, ty = [(1 - x, y), (x, 1 - y), (1 - x, 1 - y)][j]
    return pltpu.make_async_remote_copy(
        src_ref=src_refs[i].at[2 * tx + ty], dst_ref=land_refs[i].at[j], send_sem=send_sems.at[3 * i + j],
        recv_sem=recv_sems.at[3 * i + j], device_id=(tx, ty, c), device_id_type=_MESH_ID)


def _chips_start(srcs, carry, name):
    n = len(srcs)
    lands = [lax.empty((3,) + s.shape[1:], s.dtype) for s in srcs]

    def body(*refs):
        src_refs, land_refs = refs[:n], refs[n:2 * n]
        send_sems, recv_sems = refs[2 * n + 1:2 * n + 3]
        for i in range(n):
            for j in range(3):
                _chips_copy(i, j, src_refs, land_refs, send_sems, recv_sems).start()

    hbm = lambda a: pltpu.HBM(a.shape, a.dtype)
    sems = pltpu.SemaphoreType.DMA((3 * n,))
    pinned = [pltpu.with_memory_space_constraint(a, pltpu.HBM) for a in list(srcs) + lands + [carry]]
    res = pl.pallas_call(
        body, name=name,
        out_shape=(sems, sems, *[hbm(a) for a in pinned]),
        in_specs=[_HBM] * (2 * n + 1),
        out_specs=(_SEM, _SEM, *[_HBM] * (2 * n + 1)),
        input_output_aliases={i: 2 + i for i in range(2 * n + 1)},
        compiler_params=pltpu.CompilerParams(has_side_effects=_DATAFLOW),
    )(*pinned)
    return res[:2], list(res[2:2 + n]), list(res[2 + n:2 + 2 * n]), res[2 + 2 * n]


def _chips_wait(started, after, name):
    sems, srcs, lands, _ = started
    n = len(srcs)

    def body(*refs):
        src_refs, land_refs = refs[:n], refs[n:2 * n]
        send_s, recv_s = refs[2 * n:2 * n + 2]
        for i in range(n):
            for j in range(3):
                cp = _chips_copy(i, j, src_refs, land_refs, send_s, recv_s)
                cp.wait_send()
                cp.wait_recv()

    hbm = lambda a: pltpu.HBM(a.shape, a.dtype)
    res = pl.pallas_call(
        body, name=name,
        out_shape=tuple(hbm(a) for a in srcs + lands),
        in_specs=[_HBM] * (2 * n) + [_SEM, _SEM, pl.BlockSpec(memory_space=pl.ANY)],
        out_specs=tuple([_HBM] * (2 * n)),
        input_output_aliases={i: i for i in range(2 * n)},
        compiler_params=pltpu.CompilerParams(has_side_effects=_DATAFLOW),
    )(*srcs, *lands, *sems, after)
    return list(res[n:])


def _sum8(landing, name):
    _, r, c = landing.shape
    tc = _col_tile(c)

    def body(a_ref, o_ref):
        tot = a_ref[0].astype(F32)
        for k in range(1, N_DEV):
            tot = tot + a_ref[k].astype(F32)
        o_ref[...] = tot

    return pl.pallas_call(
        body,
        out_shape=jax.ShapeDtypeStruct((r, c), F32),
        grid=(c // tc,),
        in_specs=[pl.BlockSpec((N_DEV, r, tc), lambda j: (0, 0, j))],
        out_specs=pl.BlockSpec((r, tc), lambda j: (0, j)),
        compiler_params=pltpu.CompilerParams(dimension_semantics=("parallel",)),
        name=name,
    )(landing)


def _adamw_math(w, g, m, v):
    m = ADAM_B1 * m + (1.0 - ADAM_B1) * g
    v = ADAM_B2 * v + (1.0 - ADAM_B2) * (g * g)
    m_hat = m / (1.0 - ADAM_B1 ** ADAM_STEP)
    v_hat = v / (1.0 - ADAM_B2 ** ADAM_STEP)
    delta = -ADAM_LR * (m_hat / (jnp.sqrt(v_hat) + ADAM_EPS) + ADAM_WD * w)
    return delta, m, v


def _adamw(w, m, v, g, name):
    r, c = w.shape

    def fn(rows, consts):
        return list(_adamw_math(*rows)), []

    return _rowwise(fn, [w, g, m, v], [], [(c, F32)] * 3, tm=r if r <= 512 else 256, name=name)


def _adamw_sum8(w, m, v, landing, name):
    r, c = w.shape
    tc = _col_tile(c)

    def body(w_ref, m_ref, v_ref, a_ref, g_ref, d_ref, m2_ref, v2_ref):
        g = a_ref[0].astype(F32)
        for k in range(1, N_DEV):
            g = g + a_ref[k].astype(F32)
        delta, m2, v2 = _adamw_math(w_ref[...], g, m_ref[...], v_ref[...])
        g_ref[...] = g
        d_ref[...] = delta
        m2_ref[...] = m2
        v2_ref[...] = v2

    blk = pl.BlockSpec((r, tc), lambda j: (0, j))
    return pl.pallas_call(
        body,
        out_shape=[jax.ShapeDtypeStruct((r, c), F32)] * 4,
        grid=(c // tc,),
        in_specs=[blk, blk, blk, pl.BlockSpec((N_DEV, r, tc), lambda j: (0, 0, j))],
        out_specs=[blk] * 4,
        compiler_params=pltpu.CompilerParams(dimension_semantics=("parallel",)),
        name=name,
    )(w, m, v, landing)


def _adamw_parts(w, m, v, own, others, chip, name):
    r, _, c = w.shape
    tc = _col_tile(c)

    def body(q_ref, w_ref, m_ref, v_ref, a_ref, b_ref, g_ref, d_ref, m2_ref, v2_ref):
        g = ((a_ref[...] + b_ref[0].astype(F32)) + b_ref[1].astype(F32)) + b_ref[2].astype(F32)
        delta, m2, v2 = _adamw_math(w_ref[...], g, m_ref[...], v_ref[...])
        g_ref[...] = g
        d_ref[...] = delta
        m2_ref[...] = m2
        v2_ref[...] = v2

    row = pl.BlockSpec((r, None, tc), lambda j, q: (0, 0, j))
    return pl.pallas_call(
        body,
        out_shape=[jax.ShapeDtypeStruct((r, 1, c), F32)] * 4,
        grid_spec=pltpu.PrefetchScalarGridSpec(
            num_scalar_prefetch=1, grid=(c // tc,),
            in_specs=[row, row, row, pl.BlockSpec((None, r, tc), lambda j, q: (q[0], 0, j)),
                      pl.BlockSpec((3, r, tc), lambda j, q: (0, 0, j))],
            out_specs=[row] * 4),
        compiler_params=pltpu.CompilerParams(dimension_semantics=("parallel",)),
        name=name,
    )(chip, w, m, v, own, others)


def _adamw_small(gathered, params):
    ns = len(_SMALL)

    def body(*refs):
        g_ref, p_refs, o_refs = refs[0], refs[1:1 + 3 * ns], refs[1 + 3 * ns:]
        tot = g_ref[0]
        for k in range(1, N_DEV):
            tot = tot + g_ref[k]
        for i, name in enumerate(_SMALL):
            row, lane0, lanes = _SMALL_SLOTS[name]
            g = tot[row:row + 1, lane0:lane0 + lanes]
            w_, m_, v_ = (p_refs[3 * i + j][...] for j in range(3))
            delta, m2, v2 = _adamw_math(w_, g, m_, v_)
            for j, val in enumerate((g, delta, m2, v2)):
                o_refs[4 * i + j][...] = val
        o_refs[4 * ns][...] = tot[_LOSS_ROW:_LOSS_ROW + 1, 0:LANES]
        o_refs[4 * ns + 1][...] = tot[_CONV_ROW0:_CONV_ROW0 + _CONV_ROWS, :]

    out_shape = [jax.ShapeDtypeStruct(w.shape, F32) for (w, _, _) in params for _ in range(4)]
    out_shape += [jax.ShapeDtypeStruct((1, LANES), F32), jax.ShapeDtypeStruct((_CONV_ROWS, FLAT_COLS), F32)]
    flat = [a for wmv in params for a in wmv]
    return pl.pallas_call(body, out_shape=out_shape, name="adamw_small")(gathered, *flat)


def kernel(x, p, positions, w_in, conv_w, dn_a_log, dn_dt_bias, dn_norm_w, q_norm_w, w_uq, kv_norm_w, w_uk, w_uv, w_br_dn, w_br_mla, w_o, ln1_g, ln1_b, w_ffn_in, w_ffn_out, w_ple, w_ple_gate, ln2_g, ln2_b, loss_target, m_w_in, m_conv_w, m_dn_a_log, m_dn_dt_bias, m_dn_norm_w, m_q_norm_w, m_w_uq, m_kv_norm_w, m_w_uk, m_w_uv, m_w_br_dn, m_w_br_mla, m_w_o, m_ln1_g, m_ln1_b, m_w_ffn_in, m_w_ffn_out, m_w_ple, m_w_ple_gate, m_ln2_g, m_ln2_b, v_w_in, v_conv_w, v_dn_a_log, v_dn_dt_bias, v_dn_norm_w, v_q_norm_w, v_w_uq, v_kv_norm_w, v_w_uk, v_w_uv, v_w_br_dn, v_w_br_mla, v_w_o, v_ln1_g, v_ln1_b, v_w_ffn_in, v_w_ffn_out, v_w_ple, v_w_ple_gate, v_ln2_g, v_ln2_b):
    args = dict(locals())
    wts = {n: args[n] for n in _ORDER}
    mom1 = {n: args["m_" + n] for n in _ORDER}
    mom2 = {n: args["v_" + n] for n in _ORDER}
    big_names = [n for n, _ in _BIG]
    shard_shapes = {n: wts[n].shape[1:] for n in big_names}
    c_idx = lax.axis_index("c")
    q_idx = 2 * lax.axis_index("x") + lax.axis_index("y")
    parity, chip = c_idx.reshape(1).astype(jnp.int32), q_idx.reshape(1).astype(jnp.int32)

    stored = {n: _to_stored(n, wts[n][0]).astype(BF16) for n in big_names}
    first = _all_gather([stored["w_in"], conv_w[0]], "ag_first")
    group_names = {grp: [n for n, _ in pairs] for grp, pairs in _GROUP_GRADS.items()}
    carry, gathers = first[0], {}
    for grp in ("mix", "ffn"):
        gathers[grp] = _spread_start([stored[n] for n in group_names[grp]], carry, False, "ag_start_" + grp)
        carry = gathers[grp][3]
    conv_full = jnp.moveaxis(first[1], 0, 1).reshape(conv_w.shape[1], -1)
    small_w = {n: wts[n].astype(F32) for n in _SMALL}
    w = _first_weights(carry.reshape(-1, D_MODEL), conv_full, small_w)

    def late_weights(grp, after):
        got = _spread_wait(gathers[grp], after, False, "ag_wait_" + grp)
        return _late_weights(grp, {n: t.reshape(-1, t.shape[-1]) for n, t in zip(group_names[grp], got)})

    started = {}

    def emit(group, g, carry):
        if group == "w_in":
            rows, cols = _stored_shape("w_in", shard_shapes["w_in"])
            packed, from_sibling = _pack_exchange(_w_in_grad_parts(g), "rs_pack_sibling")
            own, own_bf = _chip_sum(packed.reshape(N_DEV, rows, 1, cols), from_sibling, parity, "rs_sum_w_in")
            started["w_in"] = (own, _chips_start([own_bf], carry, "rs_chips_start"))
            return started["w_in"][1][3]
        if group == "small":
            block = _pack_small_grads(_small_grads(g), g["loss"])
            started["small"] = _spread_start([block], carry, False, "ag_start_small")
            return started["small"][3]
        grads = _group_grads(group, g)
        srcs = [grads[n].reshape((N_DEV,) + _stored_shape(n, shard_shapes[n])) for n in grads]
        started[group] = (list(grads), _spread_start(srcs, carry, True, "rs_start_" + group))
        return started[group][1][3]

    s_dim = x.shape[1]
    loss, dx, g = _local_step(x[0], p[0, 0], positions.reshape(s_dim, 1).astype(F32), loss_target[0], w,
                              late_weights, emit)
    own, chips_started = started.pop("w_in")
    small_started = started.pop("small")

    out_g, out_d, out_m, out_v = {}, {}, {}, {}

    def update(n, grad, shp):
        flat2 = (shp[0], int(np.prod(shp[1:])))
        d, m2, v2 = _adamw(wts[n][0].reshape(flat2), mom1[n][0].reshape(flat2), mom2[n][0].reshape(flat2),
                           grad.reshape(flat2), "adamw_" + n)
        out_g[n], out_d[n], out_m[n], out_v[n] = grad, d.reshape(shp), m2.reshape(shp), v2.reshape(shp)

    for group, (names, st) in started.items():
        for n, landing in zip(names, _spread_wait(st, dx, True, "rs_wait_" + group)):
            shp = shard_shapes[n]
            if _BIG_AXIS[n] == 0 or shp[-1] % LANES:
                res = _adamw_sum8(_to_stored(n, wts[n][0]), _to_stored(n, mom1[n][0]), _to_stored(n, mom2[n][0]),
                                  landing, "adamw_" + n)
                out_g[n], out_d[n], out_m[n], out_v[n] = (_from_stored(n, t, shp) for t in res)
                last = res[3]
            else:
                update(n, _from_stored(n, _sum8(landing, "rs_total_" + n), shp), shp)

    from_chips = _chips_wait(chips_started, last, "rs_chips_wait")[0]
    g_small = _spread_wait(small_started, last, False, "ag_wait_small")[0]
    rows_first = lambda a: jnp.transpose(a, (2, 0, 1))
    res = _adamw_parts(rows_first(wts["w_in"]), rows_first(mom1["w_in"]), rows_first(mom2["w_in"]), own, from_chips,
                       chip, "adamw_w_in")
    out_g["w_in"], out_d["w_in"], out_m["w_in"], out_v["w_in"] = (jnp.transpose(t, (1, 2, 0))[0] for t in res)

    res = _adamw_small(g_small, [(wts[n], mom1[n], mom2[n]) for n in _SMALL])
    for i, n in enumerate(_SMALL):
        out_g[n], out_d[n], out_m[n], out_v[n] = res[4 * i:4 * i + 4]
    loss_out = res[4 * len(_SMALL)][0, 0]
    conv_shape = conv_w.shape[1:]
    conv_g = lax.dynamic_slice(res[-1].reshape(conv_shape[0], -1), (0, (2 * q_idx + c_idx) * conv_shape[1]),
                               conv_shape)
    update("conv_w", conv_g, conv_shape)

    expand = lambda d, n: d[n] if n in _SMALL else d[n][None]
    return (loss_out, dx[None], *[expand(out_g, n) for n in _ORDER], *[expand(out_d, n) for n in _ORDER],
            *[expand(out_m, n) for n in _ORDER], *[expand(out_v, n) for n in _ORDER])
```

```python
import functools

import numpy as np
import jax
import jax.numpy as jnp
from jax import lax
from jax.experimental import pallas as pl
from jax.experimental.pallas import tpu as pltpu

F32 = jnp.float32
BF16 = jnp.bfloat16

D_MODEL = 1024
N_HEADS = 8
HEAD = 128
CHUNK = 64
GROUP = 256
ROPE = 64
Q_LORA = 384
KV_LORA = 256
FFN_HIDDEN = 2816
PLE_DIM = 256
ROPE_BASE = 10000.0
ALPHA = 2.0 ** 0.25
SCALE = float((HEAD + ROPE) ** -0.5)
NEG_BIG = -1e30
EPS_RMS = 1e-6
EPS_LN = 1e-5

ADAM_LR = 0.001
ADAM_B1 = 0.9
ADAM_B2 = 0.999
ADAM_EPS = 1e-08
ADAM_WD = 0.01
ADAM_STEP = 10

N_DEV = 8
LANES = 128
FLAT_COLS = 1024

WB_CQ, WB_CKV, WB_KR, WB_BA, WB_COLS = 0, 512, 768, 896, 1024

HIGHEST = lax.Precision.HIGHEST

NN = (((1,), (0,)), ((), ()))
TN = (((0,), (0,)), ((), ()))
NT = (((1,), (1,)), ((), ()))


def _dot(a, b, dims=NN):
    return lax.dot_general(a.astype(BF16), b.astype(BF16), dims, preferred_element_type=F32)


def _dot32(a, b, dims=NN):
    return lax.dot_general(a, b, dims, precision=HIGHEST, preferred_element_type=F32)


def _sig(x):
    return 1.0 / (1.0 + jnp.exp(-x))


MM_TILE = 1536


def _pick_wide(n):
    if n <= MM_TILE:
        return n
    return max(t for t in range(LANES, MM_TILE + 1, LANES) if n % t == 0)


def _split_bf16(a):
    hi = a.astype(BF16)
    return hi, (a - hi.astype(F32)).astype(BF16)


def _dot3(a, b, dims=NN):
    ah, al = a if isinstance(a, tuple) else _split_bf16(a)
    bh, bl = b if isinstance(b, tuple) else _split_bf16(b)
    d = lambda p, q: lax.dot_general(p, q, dims, preferred_element_type=F32)
    return d(ah, bh) + (d(ah, bl) + d(al, bh))


def _mm(a, b, *, ta=False, tb=False, add=(), out_dtype=F32, b_rows=None, name):
    if ta:
        k_dim, m_dim = a.shape
    else:
        m_dim, k_dim = a.shape
    b_first, b_len = b_rows if b_rows else (0, b.shape[0])
    if tb:
        n_dim, k2 = b_len, b.shape[1]
    else:
        k2, n_dim = b_len, b.shape[1]
    assert k_dim == k2, (a.shape, b.shape, ta, tb)
    tm = _pick_wide(m_dim)
    tn = _pick_wide(n_dim)
    tk = _pick_wide(k_dim)
    nk = k_dim // tk
    n_add = len(add)
    dims = TN if ta else (NT if tb else NN)
    assert not (ta and tb)

    def body(a_ref, b_ref, *rest):
        add_refs = rest[:n_add]
        o_ref = rest[n_add]
        acc = rest[n_add + 1]
        k = pl.program_id(2)

        @pl.when(k == 0)
        def _():
            acc[...] = jnp.zeros_like(acc)

        acc[...] += _dot(a_ref[...], b_ref[...], dims)

        @pl.when(k == nk - 1)
        def _():
            r = acc[...]
            for ar in add_refs:
                r = r + ar[...].astype(F32)
            o_ref[...] = r.astype(o_ref.dtype)

    a_spec = pl.BlockSpec((tk, tm), lambda i, j, k: (k, i)) if ta else pl.BlockSpec((tm, tk), lambda i, j, k: (i, k))
    b_tile = tn if tb else tk
    assert b_first % b_tile == 0
    b0 = b_first // b_tile
    b_spec = (pl.BlockSpec((tn, tk), lambda i, j, k: (b0 + j, k)) if tb
              else pl.BlockSpec((tk, tn), lambda i, j, k: (b0 + k, j)))
    o_spec = pl.BlockSpec((tm, tn), lambda i, j, k: (i, j))
    return pl.pallas_call(
        body,
        out_shape=jax.ShapeDtypeStruct((m_dim, n_dim), out_dtype),
        grid=(m_dim // tm, n_dim // tn, nk),
        in_specs=[a_spec, b_spec] + [o_spec] * n_add,
        out_specs=o_spec,
        scratch_shapes=[pltpu.VMEM((tm, tn), F32)],
        compiler_params=pltpu.CompilerParams(dimension_semantics=("parallel", "parallel", "arbitrary")),
        name=name,
    )(a, b, *add)


def _mm_resident(a, b, *, add=(), name):
    m_dim, k_dim = a.shape
    n_dim = b.shape[1]
    assert b.shape[0] == k_dim
    tm = min(DX_ROWS, m_dim)

    def body(a_ref, b_ref, *rest):
        r = _dot(a_ref[...], b_ref[...])
        for ar in rest[:-1]:
            r = r + ar[...]
        rest[-1][...] = r

    o_spec = pl.BlockSpec((tm, n_dim), lambda i: (i, 0))
    return pl.pallas_call(
        body,
        out_shape=jax.ShapeDtypeStruct((m_dim, n_dim), F32),
        grid=(m_dim // tm,),
        in_specs=[pl.BlockSpec((tm, k_dim), lambda i: (i, 0)),
                  pl.BlockSpec((k_dim, n_dim), lambda i: (0, 0), pipeline_mode=pl.Buffered(1))] + [o_spec] * len(add),
        out_specs=o_spec,
        compiler_params=pltpu.CompilerParams(dimension_semantics=("parallel",)),
        name=name,
    )(a, b, *add)


def _rowwise(fn, rows, consts, outs, accs=(), *, tm=256, name):
    rows = [r if isinstance(r, tuple) else (r, 0, r.shape[1]) for r in rows]
    s_dim = rows[0][0].shape[0]
    tm = min(tm, s_dim)
    assert s_dim % tm == 0 and all(arr.shape[0] == s_dim for arr, _, _ in rows)
    specs = [pl.BlockSpec((tm, width), functools.partial(lambda i, cb: (i, cb), cb=cb)) for _, cb, width in rows]
    args = [arr for arr, _, _ in rows]
    for c in consts:
        specs.append(pl.BlockSpec(c.shape, lambda i: (0, 0)))
        args.append(c)
    nr, nc, no = len(rows), len(consts), len(outs)
    flipped = [len(o) == 3 for o in outs]
    out_shape = [jax.ShapeDtypeStruct((o[0], s_dim) if t else (s_dim, o[0]), o[1]) for o, t in zip(outs, flipped)]
    out_specs = [pl.BlockSpec((o[0], tm), lambda i: (0, i)) if t else pl.BlockSpec((tm, o[0]), lambda i: (i, 0))
                 for o, t in zip(outs, flipped)]
    out_shape += [jax.ShapeDtypeStruct(sh, F32) for sh in accs]
    out_specs += [pl.BlockSpec(sh, lambda i: (0, 0)) for sh in accs]

    def body(*refs):
        r = [x[...].astype(F32) if x.dtype == BF16 else x[...] for x in refs[:nr]]
        c = [x[...] for x in refs[nr:nr + nc]]
        o_refs = refs[nr + nc:nr + nc + no]
        a_refs = refs[nr + nc + no:]
        o_vals, a_vals = fn(r, c)
        for ref, v, t in zip(o_refs, o_vals, flipped, strict=True):
            ref[...] = (jnp.transpose(v.astype(F32)) if t else v).astype(ref.dtype)
        if a_refs:
            @pl.when(pl.program_id(0) == 0)
            def _():
                for ref in a_refs:
                    ref[...] = jnp.zeros_like(ref)

            for ref, v in zip(a_refs, a_vals, strict=True):
                ref[...] += v

    res = pl.pallas_call(
        body,
        out_shape=out_shape,
        grid=(s_dim // tm,),
        in_specs=specs,
        out_specs=out_specs,
        compiler_params=pltpu.CompilerParams(dimension_semantics=("arbitrary" if accs else "parallel",)),
        name=name,
    )(*args)
    return res


def _colsum(v):
    return jnp.sum(v, axis=0, keepdims=True)


def _rowsum(v):
    return jnp.sum(v, axis=1, keepdims=True)


def _rowmean(v):
    return jnp.mean(v, axis=1, keepdims=True)


def _silu_grad(x):
    s = _sig(x)
    return s * (1.0 + x * (1.0 - s))


def _conv_taps(x, w, width=4):
    row = lax.broadcasted_iota(jnp.int32, x.shape, 0)
    c = x * w[width - 1:width, :]
    for s in range(1, width):
        c = c + jnp.where(row >= s, pltpu.roll(x, s, 0), 0.0) * w[width - 1 - s:width - s, :]
    return c


def _conv_fwd(proj_a, conv_w):
    s_dim = proj_a.shape[0]
    n_blk = 3 * N_HEADS

    def body(x_ref, w_ref, o_ref):
        j = pl.program_id(0)
        c = _conv_taps(x_ref[...], w_ref[...])
        y = c * _sig(c)
        r = lax.rsqrt(_rowsum(y * y) + EPS_RMS)
        fac = jnp.where(j < N_HEADS, r * (HEAD ** -0.5), jnp.where(j < 2 * N_HEADS, r, 1.0))
        o_ref[...] = y * fac

    return pl.pallas_call(
        body,
        out_shape=jax.ShapeDtypeStruct((s_dim, n_blk * HEAD), F32),
        grid=(n_blk,),
        in_specs=[pl.BlockSpec((s_dim, HEAD), lambda j: (0, j)), pl.BlockSpec((4, HEAD), lambda j: (0, j))],
        out_specs=pl.BlockSpec((s_dim, HEAD), lambda j: (0, j)),
        compiler_params=pltpu.CompilerParams(dimension_semantics=("parallel",)),
        name="conv_fwd",
    )(proj_a, conv_w)


def _conv_bwd(proj_a, conv_w, dq, dk, dv):
    s_dim = proj_a.shape[0]
    n_blk = 3 * N_HEADS

    def body(x_ref, w_ref, dq_ref, dk_ref, dv_ref, dx_ref, dw_ref):
        j = pl.program_id(0)
        x = x_ref[...]
        w = w_ref[...]
        do = jnp.where(j < N_HEADS, dq_ref[...], jnp.where(j < 2 * N_HEADS, dk_ref[...], dv_ref[...]))
        c = _conv_taps(x, w)
        sg = _sig(c)
        y = c * sg
        r = lax.rsqrt(_rowsum(y * y) + EPS_RMS)
        sc = jnp.where(j < N_HEADS, HEAD ** -0.5, 1.0)
        dy_n = sc * (r * do - y * (r * r * r) * _rowsum(do * y))
        dy = jnp.where(j < 2 * N_HEADS, dy_n, do)
        dc = dy * (sg * (1.0 + c * (1.0 - sg)))
        row = lax.broadcasted_iota(jnp.int32, x.shape, 0)
        dx = dc * w[3:4, :]
        dw_ref[3:4, :] = _colsum(dc * x)
        for s in range(1, 4):
            dx = dx + jnp.where(row < s_dim - s, pltpu.roll(dc, s_dim - s, 0), 0.0) * w[3 - s:4 - s, :]
            xs = jnp.where(row >= s, pltpu.roll(x, s, 0), 0.0)
            dw_ref[3 - s:4 - s, :] = _colsum(dc * xs)
        dx_ref[...] = dx.astype(dx_ref.dtype)

    hd = N_HEADS - 1
    return pl.pallas_call(
        body,
        out_shape=[jax.ShapeDtypeStruct((s_dim, n_blk * HEAD), BF16), jax.ShapeDtypeStruct((4, n_blk * HEAD), F32)],
        grid=(n_blk,),
        in_specs=[
            pl.BlockSpec((s_dim, HEAD), lambda j: (0, j)),
            pl.BlockSpec((4, HEAD), lambda j: (0, j)),
            pl.BlockSpec((s_dim, HEAD), lambda j: (0, jnp.minimum(j, hd))),
            pl.BlockSpec((s_dim, HEAD), lambda j: (0, jnp.clip(j - N_HEADS, 0, hd))),
            pl.BlockSpec((s_dim, HEAD), lambda j: (0, jnp.clip(j - 2 * N_HEADS, 0, hd))),
        ],
        out_specs=[pl.BlockSpec((s_dim, HEAD), lambda j: (0, j)), pl.BlockSpec((4, HEAD), lambda j: (0, j))],
        compiler_params=pltpu.CompilerParams(dimension_semantics=("parallel",)),
        name="conv_bwd",
    )(proj_a, conv_w, dq, dk, dv)


def _chunk_tri(n):
    r = np.arange(n)
    m = ((r[:, None] // CHUNK) == (r[None, :] // CHUNK)) & (r[:, None] >= r[None, :])
    m = m.astype(np.float32)
    return jnp.asarray(m), jnp.asarray(m.T)


def _softplus(z):
    return jnp.maximum(z, 0.0) + jnp.log(1.0 + jnp.exp(-jnp.abs(z)))


def _gates_fwd(proj_b, alog, dtb):
    tm = min(GROUP, proj_b.shape[0])
    tri, _ = _chunk_tri(tm)

    def fn(r, c):
        b = r[0]
        a = pltpu.roll(b, LANES - N_HEADS, 1)
        alog_, dtb_, tri_ = c
        g = -jnp.exp(alog_) * _softplus(a + dtb_)
        return [_sig(b), _dot32(tri_, g)], []

    return _rowwise(fn, [(proj_b, WB_BA // LANES, LANES)], [alog, dtb, tri],
                    [(LANES, F32), (LANES, F32)], tm=tm, name="gates_fwd")


def _gates_bwd(proj_b, alog, dtb, gc, d_beta, d_gc, d_egl_rows):
    tm = min(GROUP, proj_b.shape[0])
    _, tri_t = _chunk_tri(tm)

    def fn(r, c):
        b, gc_, d_beta_, d_gc_, d_egl_ = r
        a = pltpu.roll(b, LANES - N_HEADS, 1)
        alog_, dtb_, tri_t_ = c
        z = a + dtb_
        ea = jnp.exp(alog_)
        g = -ea * _softplus(z)
        dg = _dot32(tri_t_, d_gc_ + d_egl_ * jnp.exp(gc_))
        d_a = dg * (-ea) * _sig(z)
        beta = _sig(b)
        d_ba = d_beta_ * beta * (1.0 - beta) + pltpu.roll(d_a, N_HEADS, 1)
        return [d_ba], [_colsum(dg * g), _colsum(d_a)]

    return _rowwise(fn, [(proj_b, WB_BA // LANES, LANES), gc, d_beta, d_gc, d_egl_rows],
                    [alog, dtb, tri_t], [(LANES, BF16)], accs=[(1, LANES), (1, LANES)], tm=tm,
                    name="gates_bwd")


def _group_masks(n):
    r = lax.broadcasted_iota(jnp.int32, (n, n), 0)
    c = lax.broadcasted_iota(jnp.int32, (n, n), 1)
    same = (r // CHUNK) == (c // CHUNK)
    below, s = [], 2
    while s < CHUNK:
        below.append(jnp.logical_and((r // (2 * s)) == (c // (2 * s)),
                                     jnp.logical_and((r // s) % 2 == 1, (c // s) % 2 == 0)))
        s *= 2
    return dict(same=same, tril=jnp.logical_and(same, r >= c), strict=jnp.logical_and(same, r > c),
                last=c == (r // CHUNK) * CHUNK + (CHUNK - 1), eye=r == c, pair=(r // 2) == (c // 2), below=below)


def _inv_unit_lower(l_mats, mk):
    eye_f = mk["eye"].astype(F32)
    ts = [eye_f - jnp.where(mk["pair"], l_mat, 0.0) for l_mat in l_mats]
    for below in mk["below"]:
        halves = [_split_bf16(t) for t in ts]
        mids = [_dot3(h, jnp.where(below, l_mat, 0.0)) for h, l_mat in zip(halves, l_mats)]
        ts = [t - _dot3(m, h) for t, m, h in zip(ts, mids, halves)]
    return ts


def _unfold_blocks(folded, mask):
    n = folded.shape[0]
    return jnp.where(mask, jnp.concatenate([folded] * (n // CHUNK), axis=1), 0.0)


def _head_cols(beta, gc, gc_t, h):
    lane = lax.broadcasted_iota(jnp.int32, beta.shape, 1)
    sub = lax.broadcasted_iota(jnp.int32, gc_t.shape, 0)
    bcol = _rowsum(jnp.where(lane == h, beta, 0.0))
    gcol = _rowsum(jnp.where(lane == h, gc, 0.0))
    grow = _colsum(jnp.where(sub == h, gc_t, 0.0))
    return bcol, gcol, grow


def _prep_common(q, k, bcol, gcol, grow, mk, t_folded=None):
    n = q.shape[0]
    tril = mk["tril"]
    decay = jnp.where(tril, jnp.exp(jnp.where(tril, gcol - grow, 0.0)), 0.0)
    glast = _rowsum(jnp.where(mk["last"], jnp.broadcast_to(grow, (n, n)), 0.0))
    e = jnp.exp(gcol)
    ekt = jnp.exp(glast - gcol)
    kb = k * bcol
    kk = _dot(kb, k, NT)
    qk = _dot(q, k, NT)
    p = dict(decay=decay, e=e, ekt=ekt, kb=kb, kk=kk, qk=qk)
    if t_folded is not None:
        p["t"] = _unfold_blocks(t_folded, mk["same"])
    return p


GROUPS_PER_STEP = 4
SCAN_CHUNKS_PER_STEP = 4


def _fold_blocks(m):
    n = m.shape[0]
    out = m[:, 0:CHUNK]
    for b in range(1, n // CHUNK):
        out = out + m[:, b * CHUNK:(b + 1) * CHUNK]
    return out


def _gdr_prep_fwd(qkvn, beta, gc, gc_t):
    s_dim = qkvn.shape[0]
    tg = min(GROUP, s_dim)
    n_sub = min(GROUPS_PER_STEP, s_dim // tg)
    tb = tg * n_sub

    def body(q_ref, k_ref, v_ref, b_ref, g_ref, gt_ref, u_ref, w_ref, qd_ref, kt_ref, a_ref, t_ref):
        h = pl.program_id(0)
        mk = _group_masks(tg)
        parts = []
        for s in range(n_sub):
            rows = slice(s * tg, (s + 1) * tg)
            q, k, v = q_ref[rows, :], k_ref[rows, :], v_ref[rows, :]
            bcol, gcol, grow = _head_cols(b_ref[rows, :], g_ref[rows, :], gt_ref[:, rows], h)
            p = _prep_common(q, k, bcol, gcol, grow, mk)
            qd_ref[rows, :] = q * p["e"]
            kt_ref[rows, :] = k * p["ekt"]
            a_ref[rows, :] = _fold_blocks(jnp.where(mk["tril"], p["qk"] * p["decay"], 0.0))
            parts.append((rows, v * bcol, p["kb"] * p["e"], jnp.where(mk["strict"], p["kk"] * p["decay"], 0.0)))
        t_mats = _inv_unit_lower([part[3] for part in parts], mk)
        for (rows, vb, kbe, _), t_mat in zip(parts, t_mats):
            u_ref[rows, :] = _dot(t_mat, vb)
            w_ref[rows, :] = _dot(t_mat, kbe)
            t_ref[rows, :] = _fold_blocks(t_mat)

    row = lambda off: pl.BlockSpec((tb, HEAD), functools.partial(lambda h, m, off: (m, h + off), off=off))
    full = pl.BlockSpec((tb, LANES), lambda h, m: (m, 0))
    o_spec = pl.BlockSpec((tb, HEAD), lambda h, m: (m, h))
    a_spec = pl.BlockSpec((None, tb, CHUNK), lambda h, m: (h, m, 0))
    wide = jax.ShapeDtypeStruct((s_dim, N_HEADS * HEAD), F32)
    folded = jax.ShapeDtypeStruct((N_HEADS, s_dim, CHUNK), F32)
    return pl.pallas_call(
        body,
        out_shape=[wide, wide, wide, wide, folded, folded],
        grid=(N_HEADS, s_dim // tb),
        in_specs=[row(0), row(N_HEADS), row(2 * N_HEADS), full, full, pl.BlockSpec((8, tb), lambda h, m: (0, m))],
        out_specs=[o_spec, o_spec, o_spec, o_spec, a_spec, a_spec],
        compiler_params=pltpu.CompilerParams(dimension_semantics=("parallel", "parallel")),
        name="gdr_prep_fwd",
    )(qkvn, qkvn, qkvn, beta, gc, gc_t)


def _gdr_prep_bwd(qkvn, beta, gc, gc_t, t_fold, u, w, du, dw, dqd, dkt, d_a):
    s_dim = qkvn.shape[0]
    tg = min(GROUP, s_dim)
    n_sub = min(GROUPS_PER_STEP, s_dim // tg)
    tb = tg * n_sub

    def body(q_ref, k_ref, v_ref, b_ref, g_ref, gt_ref, t_ref, u_ref, w_ref, du_ref, dw_ref, dqd_ref, dkt_ref,
             da_ref, dq_ref, dk_ref, dv_ref, db_ref, dg_ref):
        h = pl.program_id(1)

        @pl.when(h == 0)
        def _():
            db_ref[...] = jnp.zeros_like(db_ref)
            dg_ref[...] = jnp.zeros_like(dg_ref)

        mk = _group_masks(tg)
        lane = lax.broadcasted_iota(jnp.int32, (tg, LANES), 1)
        for s in range(n_sub):
            rows = slice(s * tg, (s + 1) * tg)
            q, k, v = q_ref[rows, :], k_ref[rows, :], v_ref[rows, :]
            bcol, gcol, grow = _head_cols(b_ref[rows, :], g_ref[rows, :], gt_ref[:, rows], h)
            p = _prep_common(q, k, bcol, gcol, grow, mk, t_ref[rows, :])
            t_mat, decay, e, ekt, kb = p["t"], p["decay"], p["e"], p["ekt"], p["kb"]
            du_, dw_, dqd_, dkt_ = du_ref[rows, :], dw_ref[rows, :], dqd_ref[rows, :], dkt_ref[rows, :]
            dvb = _dot(t_mat, du_, TN)
            dkbe = _dot(t_mat, dw_, TN)
            d_l = -(_dot(dvb, u_ref[rows, :], NT) + _dot(dkbe, w_ref[rows, :], NT))
            m1 = jnp.where(mk["strict"], d_l, 0.0)
            m2 = _unfold_blocks(da_ref[rows, :], mk["tril"])
            d_kk = m1 * decay
            d_qk = m2 * decay
            d_decay = m1 * p["kk"] + m2 * p["qk"]
            dkb = _dot(d_kk, k) + dkbe * e
            dk = _dot(d_kk, kb, TN) + _dot(d_qk, q, TN) + dkt_ * ekt + dkb * bcol
            dq = _dot(d_qk, k) + dqd_ * e
            d_beta = _rowsum(dkb * k) + _rowsum(dvb * v)
            d_e = _rowsum(dkbe * kb) + _rowsum(dqd_ * q)
            d_ekt = _rowsum(dkt_ * k) * ekt
            d_diff = d_decay * decay
            d_grow = -_colsum(d_diff) + _colsum(jnp.where(mk["last"], jnp.broadcast_to(d_ekt, (tg, tg)), 0.0))
            d_gcol = d_e * e - d_ekt + _rowsum(d_diff)
            d_gcol = d_gcol + _rowsum(jnp.where(mk["eye"], jnp.broadcast_to(d_grow, (tg, tg)), 0.0))
            dq_ref[rows, :] = dq
            dk_ref[rows, :] = dk
            dv_ref[rows, :] = dvb * bcol
            db_ref[rows, :] = jnp.where(lane == h, d_beta, db_ref[rows, :])
            dg_ref[rows, :] = jnp.where(lane == h, d_gcol, dg_ref[rows, :])

    row = lambda off: pl.BlockSpec((tb, HEAD), functools.partial(lambda m, h, off: (m, h + off), off=off))
    full = pl.BlockSpec((tb, LANES), lambda m, h: (m, 0))
    o_spec = pl.BlockSpec((tb, HEAD), lambda m, h: (m, h))
    a_spec = pl.BlockSpec((None, tb, CHUNK), lambda m, h: (h, m, 0))
    wide = jax.ShapeDtypeStruct((s_dim, N_HEADS * HEAD), F32)
    lanes = jax.ShapeDtypeStruct((s_dim, LANES), F32)
    return pl.pallas_call(
        body,
        out_shape=[wide, wide, wide, lanes, lanes],
        grid=(s_dim // tb, N_HEADS),
        in_specs=[row(0), row(N_HEADS), row(2 * N_HEADS), full, full, pl.BlockSpec((8, tb), lambda m, h: (0, m)),
                  a_spec, o_spec, o_spec, o_spec, o_spec, o_spec, o_spec, a_spec],
        out_specs=[o_spec, o_spec, o_spec, full, full],
        compiler_params=pltpu.CompilerParams(dimension_semantics=("parallel", "arbitrary")),
        name="gdr_prep_bwd",
    )(qkvn, qkvn, qkvn, beta, gc, gc_t, t_fold, u, w, du, dw, dqd, dkt, d_a)


def _gdr_scan_fwd(u, w, qd, kt, a_mat, gc):
    s_dim = u.shape[0]
    n_chunks = s_dim // CHUNK
    per = min(SCAN_CHUNKS_PER_STEP, n_chunks)
    tb = per * CHUNK

    def body(u_ref, w_ref, qd_ref, kt_ref, a_ref, g_ref, o_ref, st_ref, state):
        @pl.when(pl.program_id(0) == 0)
        def _():
            state[...] = jnp.zeros_like(state)

        heads = range(N_HEADS)
        cols = [slice(h * HEAD, (h + 1) * HEAD) for h in heads]
        for i in range(per):
            rows = slice(i * CHUNK, (i + 1) * CHUNK)
            egl = jnp.exp(g_ref[(i + 1) * CHUNK - 1:(i + 1) * CHUNK, :])
            s_b = [state[h].astype(BF16) for h in heads]
            for h in heads:
                st_ref[i, h] = state[h]
            ws = [_dot(w_ref[rows, cs], s) for cs, s in zip(cols, s_b)]
            qs = [_dot(qd_ref[rows, cs], s) for cs, s in zip(cols, s_b)]
            vns = [(u_ref[rows, cs] - ws_h).astype(BF16) for cs, ws_h in zip(cols, ws)]
            avs = [_dot(a_ref[h, rows, :], vn) for h, vn in zip(heads, vns)]
            kvs = [_dot(kt_ref[rows, cs], vn, TN) for cs, vn in zip(cols, vns)]
            for h, cs in zip(heads, cols):
                o_ref[rows, cs] = qs[h] + avs[h]
                state[h] = state[h] * egl[:, h:h + 1] + kvs[h]

    wide = pl.BlockSpec((tb, N_HEADS * HEAD), lambda n: (n, 0))
    return pl.pallas_call(
        body,
        out_shape=[jax.ShapeDtypeStruct((s_dim, N_HEADS * HEAD), F32),
                   jax.ShapeDtypeStruct((n_chunks, N_HEADS, HEAD, HEAD), F32)],
        grid=(n_chunks // per,),
        in_specs=[wide, wide, wide, wide, pl.BlockSpec((N_HEADS, tb, CHUNK), lambda n: (0, n, 0)),
                  pl.BlockSpec((tb, LANES), lambda n: (n, 0))],
        out_specs=[wide, pl.BlockSpec((per, N_HEADS, HEAD, HEAD), lambda n: (n, 0, 0, 0))],
        scratch_shapes=[pltpu.VMEM((N_HEADS, HEAD, HEAD), F32)],
        compiler_params=pltpu.CompilerParams(dimension_semantics=("arbitrary",)),
        name="gdr_scan_fwd",
    )(u, w, qd, kt, a_mat, gc)


def _gdr_scan_bwd(u, w, qd, kt, a_mat, gc, states, d_o):
    s_dim = u.shape[0]
    n_chunks = s_dim // CHUNK
    per = min(SCAN_CHUNKS_PER_STEP, n_chunks)
    tb = per * CHUNK
    last = n_chunks // per - 1

    def body(u_ref, w_ref, qd_ref, kt_ref, a_ref, g_ref, st_ref, do_ref,
             du_ref, dw_ref, dqd_ref, dkt_ref, da_ref, de_ref, d_state):
        @pl.when(pl.program_id(0) == 0)
        def _():
            d_state[...] = jnp.zeros_like(d_state)

        heads = range(N_HEADS)
        cols = [slice(h * HEAD, (h + 1) * HEAD) for h in heads]
        for i in reversed(range(per)):
            rows = slice(i * CHUNK, (i + 1) * CHUNK)
            egl = jnp.exp(g_ref[(i + 1) * CHUNK - 1:(i + 1) * CHUNK, :])
            s_b = [st_ref[i, h].astype(BF16) for h in heads]
            ds_b = [d_state[h].astype(BF16) for h in heads]
            dos = [do_ref[rows, cs].astype(BF16) for cs in cols]
            w_b = [w_ref[rows, cs].astype(BF16) for cs in cols]
            ws = [_dot(w_h, s) for w_h, s in zip(w_b, s_b)]
            ados = [_dot(a_ref[h, rows, :], do, TN) for h, do in zip(heads, dos)]
            kds = [_dot(kt_ref[rows, cs], ds) for cs, ds in zip(cols, ds_b)]
            dqds = [_dot(do, s, NT) for do, s in zip(dos, s_b)]
            qdos = [_dot(qd_ref[rows, cs], do, TN) for cs, do in zip(cols, dos)]
            vns = [(u_ref[rows, cs] - ws_h).astype(BF16) for cs, ws_h in zip(cols, ws)]
            dvns = [a + k_ for a, k_ in zip(ados, kds)]
            dvn_b = [d.astype(BF16) for d in dvns]
            das = [_dot(do, vn, NT) for do, vn in zip(dos, vns)]
            dkts = [_dot(vn, ds, NT) for vn, ds in zip(vns, ds_b)]
            dws = [_dot(d, s, NT) for d, s in zip(dvn_b, s_b)]
            wds = [_dot(w_h, d, TN) for w_h, d in zip(w_b, dvn_b)]
            for h, cs in zip(heads, cols):
                ds_n = d_state[h]
                de = jnp.sum(_rowsum(ds_n * st_ref[i, h]), axis=0, keepdims=True)
                de_ref[i, h:h + 1, :] = jnp.broadcast_to(de, (1, LANES))
                dqd_ref[rows, cs] = dqds[h]
                da_ref[h, rows, :] = das[h]
                dkt_ref[rows, cs] = dkts[h]
                du_ref[rows, cs] = dvns[h]
                dw_ref[rows, cs] = -dws[h]
                d_state[h] = ds_n * egl[:, h:h + 1] + qdos[h] - wds[h]

    wide = pl.BlockSpec((tb, N_HEADS * HEAD), lambda n: (last - n, 0))
    a_spec = pl.BlockSpec((N_HEADS, tb, CHUNK), lambda n: (0, last - n, 0))
    wide_shape = jax.ShapeDtypeStruct((s_dim, N_HEADS * HEAD), F32)
    return pl.pallas_call(
        body,
        out_shape=[wide_shape, wide_shape, wide_shape, wide_shape,
                   jax.ShapeDtypeStruct((N_HEADS, s_dim, CHUNK), F32),
                   jax.ShapeDtypeStruct((n_chunks, N_HEADS, LANES), F32)],
        grid=(n_chunks // per,),
        in_specs=[wide, wide, wide, wide, a_spec, pl.BlockSpec((tb, LANES), lambda n: (last - n, 0)),
                  pl.BlockSpec((per, N_HEADS, HEAD, HEAD), lambda n: (last - n, 0, 0, 0)), wide],
        out_specs=[wide, wide, wide, wide, a_spec, pl.BlockSpec((per, N_HEADS, LANES), lambda n: (last - n, 0, 0))],
        scratch_shapes=[pltpu.VMEM((N_HEADS, HEAD, HEAD), F32)],
        compiler_params=pltpu.CompilerParams(dimension_semantics=("arbitrary",)),
        name="gdr_scan_bwd",
    )(u, w, qd, kt, a_mat, gc, states, d_o)


FUSED_ROWS = 512


def _gdr_out_fwd(o_dn, proj_a, dn_w, w_br):
    def fn(r, c):
        o, z = r
        w_, w_br_ = c
        outs = []
        for h in range(N_HEADS):
            cs = slice(h * HEAD, (h + 1) * HEAD)
            oh, zh = o[:, cs], z[:, cs]
            rr = lax.rsqrt(_rowmean(oh * oh) + EPS_RMS)
            outs.append(oh * rr * w_ * (zh * _sig(zh)))
        og = jnp.concatenate(outs, axis=1).astype(BF16)
        return [og, _dot(og, w_br_)], []

    return _rowwise(fn, [o_dn, (proj_a, 3, D_MODEL)], [dn_w, w_br], [(D_MODEL, BF16), (D_MODEL, BF16)],
                    tm=FUSED_ROWS, name="gdr_out_fwd")


def _gdr_out_bwd(o_dn, proj_a, d_y_dn, dn_w, w_br):
    def fn(r, c):
        o, z, dy = r
        w_, w_br_ = c
        dg = _dot(dy, w_br_, NT)
        d_o, d_z = [], []
        d_w = jnp.zeros((1, HEAD), F32)
        for h in range(N_HEADS):
            cs = slice(h * HEAD, (h + 1) * HEAD)
            oh, zh, dgh = o[:, cs], z[:, cs], dg[:, cs]
            rr = lax.rsqrt(_rowmean(oh * oh) + EPS_RMS)
            sz = zh * _sig(zh)
            d_n = dgh * sz
            d_z.append(dgh * (oh * rr * w_) * _silu_grad(zh))
            d_w = d_w + _colsum(d_n * oh * rr)
            gw = d_n * w_
            d_o.append(rr * gw - oh * (rr * rr * rr) * _rowmean(gw * oh))
        return [jnp.concatenate(d_o, axis=1), jnp.concatenate(d_z, axis=1)], [d_w]

    return _rowwise(fn, [o_dn, (proj_a, 3, D_MODEL), d_y_dn], [dn_w, w_br], [(D_MODEL, F32), (D_MODEL, BF16)],
                    accs=[(1, HEAD)], tm=FUSED_ROWS, name="gdr_out_bwd")


def _rms_fwd(x, w):
    r = lax.rsqrt(_rowmean(x * x) + EPS_RMS)
    return x * r * w


def _rms_bwd(x, w, dy):
    r = lax.rsqrt(_rowmean(x * x) + EPS_RMS)
    gw = dy * w
    return r * gw - x * (r * r * r) * _rowmean(gw * x), _colsum(dy * x * r)


def _rope_consts():
    inv = ROPE_BASE ** (-np.arange(0, ROPE, 2, dtype=np.float32) / ROPE)
    t = np.zeros((4, LANES), np.float32)
    t[0, :32] = inv
    t[0, 32:64] = inv
    t[1, :64] = 1.0
    t[2, 32:64] = 1.0
    t[3, :32] = -1.0
    return jnp.asarray(t)


def _rope_tables(pos, consts, width):
    ang = pos * consts[0:1, :]
    cosv, sinv = jnp.cos(ang), jnp.sin(ang)
    reps = width // LANES
    tile = (lambda t: jnp.concatenate([t] * reps, axis=1)) if reps > 1 else (lambda t: t)
    return tile(cosv * consts[1:2, :]), tile(sinv * consts[2:3, :]), tile(sinv * consts[3:4, :])


def _rope_apply(t, tabs):
    cos_t, sin_a, sin_b = tabs
    width = t.shape[1]
    return t * cos_t + pltpu.roll(t, 32, 1) * sin_a + pltpu.roll(t, width - 32, 1) * sin_b


def _rope_transpose(d, tabs):
    cos_t, sin_a, sin_b = tabs
    width = d.shape[1]
    return d * cos_t + pltpu.roll(d * sin_a, width - 32, 1) + pltpu.roll(d * sin_b, 32, 1)


QK_HEAD = 2 * HEAD


def _interleave_heads(a, b):
    parts = []
    for h in range(N_HEADS):
        parts.append(a[:, h * HEAD:(h + 1) * HEAD])
        parts.append(b if b.shape[1] == LANES else b[:, h * LANES:(h + 1) * LANES])
    return jnp.concatenate(parts, axis=1)


def _mla_rows(proj_b):
    return [(proj_b, WB_CQ // Q_LORA, Q_LORA), (proj_b, WB_CKV // KV_LORA, KV_LORA), (proj_b, WB_KR // LANES, LANES)]


def _mla_prep_fwd(proj_b, pos, qn_w, kvn_w, uq, uk, uv):
    def fn(r, c):
        cq, ckv, kr, pos_ = r
        qn_w_, kvn_w_, uq_, uk_, uv_, rope = c
        c_q = _rms_fwd(cq, qn_w_).astype(BF16)
        c_kv = _rms_fwd(ckv, kvn_w_).astype(BF16)
        qf = _dot(c_q, uq_)
        qr = _rope_apply(qf[:, D_MODEL:], _rope_tables(pos_, rope, D_MODEL))
        kr = _rope_apply(kr, _rope_tables(pos_, rope, LANES))
        kc = _interleave_heads(_dot(c_kv, uk_), kr)
        v = _dot(c_kv, uv_)
        return [c_q, c_kv, _interleave_heads(qf[:, :D_MODEL], qr) * SCALE, kc, v, kc, v], []

    wide2 = N_HEADS * QK_HEAD
    return _rowwise(fn, _mla_rows(proj_b) + [pos], [qn_w, kvn_w, uq, uk, uv, _rope_consts()],
                    [(Q_LORA, BF16), (KV_LORA, BF16), (wide2, BF16), (wide2, BF16), (D_MODEL, BF16),
                     (wide2, BF16, "T"), (D_MODEL, BF16, "T")], tm=FUSED_ROWS, name="mla_prep_fwd")


def _mla_prep_bwd(proj_b, pos, d_qc, d_kc, d_v, qn_w, kvn_w, uq, uk, uv):
    def fn(r, c):
        cq, ckv, _, pos_, dq, dk, dv = r
        qn_w_, kvn_w_, uq_, uk_, uv_, rope = c
        even = lambda t: jnp.concatenate([t[:, (2 * h) * LANES:(2 * h + 1) * LANES] for h in range(N_HEADS)], axis=1)
        odd = lambda t: jnp.concatenate([t[:, (2 * h + 1) * LANES:(2 * h + 2) * LANES] for h in range(N_HEADS)], axis=1)
        d_qr_raw = _rope_transpose(odd(dq), _rope_tables(pos_, rope, D_MODEL)) * SCALE
        d_qf = jnp.concatenate([even(dq) * SCALE, d_qr_raw], axis=1).astype(BF16)
        d_kn = even(dk).astype(BF16)
        dkr = dk[:, LANES:2 * LANES]
        for h in range(1, N_HEADS):
            dkr = dkr + dk[:, (2 * h + 1) * LANES:(2 * h + 2) * LANES]
        d_cq, d_qnw = _rms_bwd(cq, qn_w_, _dot(d_qf, uq_, NT))
        d_ckv, d_kvnw = _rms_bwd(ckv, kvn_w_, _dot(d_kn, uk_, NT) + _dot(dv, uv_, NT))
        return [d_qf, d_kn, d_cq, d_ckv, _rope_transpose(dkr, _rope_tables(pos_, rope, LANES))], [d_qnw, d_kvnw]

    return _rowwise(fn, _mla_rows(proj_b) + [pos, d_qc, d_kc, d_v], [qn_w, kvn_w, uq, uk, uv, _rope_consts()],
                    [(2 * D_MODEL, BF16), (D_MODEL, BF16), (Q_LORA, BF16), (KV_LORA, BF16), (LANES, BF16)],
                    accs=[(1, Q_LORA), (1, KV_LORA)], tm=FUSED_ROWS, name="mla_prep_bwd")


def _causal_mask_t(st, key0, query0):
    key = lax.broadcasted_iota(jnp.int32, st.shape, 0) + key0
    query = lax.broadcasted_iota(jnp.int32, st.shape, 1) + query0
    return jnp.where(key <= query, st, NEG_BIG)


def _attn_tiles(s_dim):
    tq = min(512, s_dim)
    n_chains = 2 if s_dim >= 2 * tq else 1
    return tq, n_chains, min(512, s_dim)


def _diagonal_chains(t, tq, n_chains, tk):
    return [(c, (t + 1) * tk - 1 > c * tq) for c in range(n_chains) if t * tk < (c + 1) * tq]


def _attn_fwd(qc, kc, vt):
    s_dim = qc.shape[0]
    tq, n_chains, tk = _attn_tiles(s_dim)
    tqs = tq * n_chains

    def body(q_ref, k_ref, vt_ref, o_ref, lse_ref, m_s, l_s, acc):
        qi = pl.program_id(1)
        m_s[...] = jnp.full_like(m_s, NEG_BIG)
        l_s[...] = jnp.zeros_like(l_s)
        acc[...] = jnp.zeros_like(acc)

        def make_step(chains):
            def step(j, carry):
                ks = pl.multiple_of(j * tk, tk)
                kb, vtb = k_ref[pl.ds(ks, tk), :], vt_ref[:, pl.ds(ks, tk)]
                cols = [slice(c * tq, (c + 1) * tq) for c, _ in chains]
                sts = [_dot(kb, q_ref[cs, :], NT) for cs in cols]
                sts = [_causal_mask_t(st, j * tk, qi * tqs + c * tq) if masked else st
                       for st, (c, masked) in zip(sts, chains)]
                m_prevs = [m_s[:, cs] for cs in cols]
                m_news = [jnp.maximum(mp, jnp.max(st, axis=0, keepdims=True)) for mp, st in zip(m_prevs, sts)]
                alphas = [jnp.exp(mp - mn) for mp, mn in zip(m_prevs, m_news)]
                pts = [jnp.exp(st - mn) for st, mn in zip(sts, m_news)]
                pvs = [_dot(vtb, pt) for pt in pts]
                for cs, mn, al, pt, pv in zip(cols, m_news, alphas, pts, pvs):
                    l_s[:, cs] = al * l_s[:, cs] + _colsum(pt)
                    m_s[:, cs] = mn
                    acc[:, cs] = acc[:, cs] * al + pv
                return carry
            return step

        below = qi * (tqs // tk)
        lax.fori_loop(0, below, make_step([(c, False) for c in range(n_chains)]), 0)
        for t in range(tqs // tk):
            make_step(_diagonal_chains(t, tq, n_chains, tk))(below + t, 0)
        l = l_s[...]
        o_ref[...] = jnp.transpose(acc[...] / l)
        lse_ref[...] = m_s[...] + jnp.log(l)

    return pl.pallas_call(
        body,
        out_shape=[jax.ShapeDtypeStruct((s_dim, N_HEADS * HEAD), F32), jax.ShapeDtypeStruct((N_HEADS, 1, s_dim), F32)],
        grid=(N_HEADS, s_dim // tqs),
        in_specs=[pl.BlockSpec((tqs, QK_HEAD), lambda h, qi: (qi, h)),
                  pl.BlockSpec((s_dim, QK_HEAD), lambda h, qi: (0, h)),
                  pl.BlockSpec((HEAD, s_dim), lambda h, qi: (h, 0))],
        out_specs=[pl.BlockSpec((tqs, HEAD), lambda h, qi: (qi, h)),
                   pl.BlockSpec((None, 1, tqs), lambda h, qi: (h, 0, qi))],
        scratch_shapes=[pltpu.VMEM((1, tqs), F32), pltpu.VMEM((1, tqs), F32), pltpu.VMEM((HEAD, tqs), F32)],
        compiler_params=pltpu.CompilerParams(dimension_semantics=("parallel", "parallel")),
        name="attn_fwd",
    )(qc, kc, vt)


def _attn_bwd(qc, kc, kct, v, o, d_o, lse):
    s_dim = qc.shape[0]
    tq, n_chains, tk = _attn_tiles(s_dim)
    tqs = tq * n_chains

    def body(q_ref, k_ref, kt_ref, v_ref, o_ref, do_ref, lse_ref, dq_ref, dk_ref, dv_ref, dqt_acc, dv_acc):
        qi = pl.program_id(1)

        @pl.when(qi == 0)
        def _():
            dk_ref[...] = jnp.zeros_like(dk_ref)
            dv_acc[...] = jnp.zeros_like(dv_acc)

        dqt_acc[...] = jnp.zeros_like(dqt_acc)
        do_f = do_ref[...]
        do_all = do_f.astype(BF16)
        q_all = q_ref[...]
        lse_row = lse_ref[...]
        delta_row = _dot3(jnp.ones((8, HEAD), F32), o_ref[...] * do_f, NT)[0:1, :]

        def make_step(chains):
            rows = slice(chains[0][0] * tq, (chains[-1][0] + 1) * tq)

            def step(j, carry):
                ks = pl.multiple_of(j * tk, tk)
                kb, vb, ktb = k_ref[pl.ds(ks, tk), :], v_ref[pl.ds(ks, tk), :], kt_ref[:, pl.ds(ks, tk)]
                cols = [slice(c * tq, (c + 1) * tq) for c, _ in chains]
                sts = [_dot(kb, q_all[cs, :], NT) for cs in cols]
                sts = [_causal_mask_t(st, j * tk, qi * tqs + c * tq) if masked else st
                       for st, (c, masked) in zip(sts, chains)]
                dpts = [_dot(vb, do_all[cs, :], NT) for cs in cols]
                pts = [jnp.exp(st - lse_row[:, cs]) for st, cs in zip(sts, cols)]
                dsts = [(pt * (dpt - delta_row[:, cs])).astype(BF16) for pt, dpt, cs in zip(pts, dpts, cols)]
                pts = [pt.astype(BF16) for pt in pts]
                dqs = [_dot(ktb, dst) for dst in dsts]
                for cs, dq in zip(cols, dqs):
                    dqt_acc[:, cs] += dq
                pt_all = jnp.concatenate(pts, axis=1) if len(chains) > 1 else pts[0]
                dst_all = jnp.concatenate(dsts, axis=1) if len(chains) > 1 else dsts[0]
                dk_ref[pl.ds(ks, tk), :] += _dot(dst_all, q_all[rows, :])
                dv_acc[pl.ds(ks, tk), :] += _dot(pt_all, do_all[rows, :])
                return carry
            return step

        below = qi * (tqs // tk)
        lax.fori_loop(0, below, make_step([(c, False) for c in range(n_chains)]), 0)
        for t in range(tqs // tk):
            make_step(_diagonal_chains(t, tq, n_chains, tk))(below + t, 0)
        dq_ref[...] = jnp.transpose(dqt_acc[...])

        @pl.when(qi == s_dim // tqs - 1)
        def _():
            dv_ref[...] = dv_acc[...].astype(dv_ref.dtype)

    q_spec = pl.BlockSpec((tqs, QK_HEAD), lambda h, qi: (qi, h))
    o_spec = pl.BlockSpec((tqs, HEAD), lambda h, qi: (qi, h))
    k_spec = pl.BlockSpec((s_dim, QK_HEAD), lambda h, qi: (0, h))
    v_spec = pl.BlockSpec((s_dim, HEAD), lambda h, qi: (0, h))
    wide2 = jax.ShapeDtypeStruct((s_dim, N_HEADS * QK_HEAD), F32)
    return pl.pallas_call(
        body,
        out_shape=[wide2, wide2, jax.ShapeDtypeStruct((s_dim, N_HEADS * HEAD), BF16)],
        grid=(N_HEADS, s_dim // tqs),
        in_specs=[q_spec, k_spec, pl.BlockSpec((QK_HEAD, s_dim), lambda h, qi: (h, 0)), v_spec, o_spec, o_spec,
                  pl.BlockSpec((None, 1, tqs), lambda h, qi: (h, 0, qi))],
        out_specs=[q_spec, k_spec, v_spec],
        scratch_shapes=[pltpu.VMEM((QK_HEAD, tqs), F32), pltpu.VMEM((s_dim, HEAD), F32)],
        compiler_params=pltpu.CompilerParams(dimension_semantics=("parallel", "arbitrary")),
        name="attn_bwd",
    )(qc, kc, kct, v, o, d_o, lse)


def _mix_proj_ln1(y_dn, y_mla, proj_g, x, w_o, g, b):
    s_dim = x.shape[0]
    tm = min(512, s_dim)

    def body(yd_ref, ym_ref, g_ref, x_ref, w_ref, lg_ref, lb_ref, mixed_ref, a1_ref, h1_ref, h1b_ref):
        gates = g_ref[...].astype(F32)
        mixed = (_sig(gates[:, :D_MODEL]) * yd_ref[...].astype(F32)
                 + _sig(gates[:, D_MODEL:]) * ym_ref[...].astype(F32)).astype(BF16)
        a1 = _dot(mixed, w_ref[...])
        xh, _ = _ln_stats(ALPHA * x_ref[...] + a1)
        y = xh * lg_ref[...] + lb_ref[...]
        mixed_ref[...] = mixed
        a1_ref[...] = a1
        h1_ref[...] = y
        h1b_ref[...] = y.astype(BF16)

    row = lambda width: pl.BlockSpec((tm, width), lambda i: (i, 0))
    whole = lambda a: pl.BlockSpec(a.shape, lambda i: (0, 0))
    sds = lambda dt: jax.ShapeDtypeStruct((s_dim, D_MODEL), dt)
    return pl.pallas_call(
        body,
        out_shape=[sds(BF16), sds(F32), sds(F32), sds(BF16)],
        grid=(s_dim // tm,),
        in_specs=[row(D_MODEL), row(D_MODEL), row(2 * D_MODEL), row(D_MODEL), whole(w_o), whole(g), whole(b)],
        out_specs=[row(D_MODEL)] * 4,
        compiler_params=pltpu.CompilerParams(dimension_semantics=("parallel",)),
        name="mix_proj_ln1",
    )(y_dn, y_mla, proj_g, x, w_o, g, b)


def _ln1_mix_bwd(x, a1, d_h1, d_pg, y_dn, y_mla, proj_g, g, w_o, w_pg):
    def fn(r, c):
        x_, a1_, dy, dpg, yd, ym, gates = r
        g_, w_o_, w_pg_ = c
        dy = dy + _dot(dpg, w_pg_, NT)
        xh, rr = _ln_stats(ALPHA * x_ + a1_)
        dz = _ln_bwd(dy, xh, rr, g_)
        dz_b = dz.astype(BF16)
        dm = _dot(dz_b, w_o_, NT)
        sd, sm = _sig(gates[:, :D_MODEL]), _sig(gates[:, D_MODEL:])
        d_g = jnp.concatenate([dm * yd * sd * (1.0 - sd), dm * ym * sm * (1.0 - sm)], axis=1)
        return [dz_b, ALPHA * dz, d_g, dm * sd, dm * sm], [_colsum(dy * xh), _colsum(dy)]

    return _rowwise(fn, [x, a1, d_h1, d_pg, y_dn, y_mla, proj_g], [g, w_o, w_pg],
                    [(D_MODEL, BF16), (D_MODEL, F32), (2 * D_MODEL, BF16), (D_MODEL, BF16), (D_MODEL, BF16)],
                    accs=[(1, D_MODEL), (1, D_MODEL)], tm=FUSED_ROWS, name="ln1_mix_bwd")


def _ln_stats(z):
    mu = _rowmean(z)
    zc = z - mu
    r = lax.rsqrt(_rowmean(zc * zc) + EPS_LN)
    return zc * r, r


def _ln_bwd(dy, xh, r, g):
    dxh = dy * g
    return r * (dxh - _rowmean(dxh) - xh * _rowmean(dxh * xh))


def _ffn_in_act(h1b, w_t):
    s_dim, k_dim = h1b.shape
    hidden = w_t.shape[0] // 2
    tm, tn = min(512, s_dim), _pick_wide(hidden)
    nt = hidden // tn

    def body(a_ref, bg_ref, bu_ref, gt_ref, up_ref, act_ref):
        a = a_ref[...]
        gt, up = _dot(a, bg_ref[...], NT), _dot(a, bu_ref[...], NT)
        gt_ref[...] = gt.astype(BF16)
        up_ref[...] = up.astype(BF16)
        act_ref[...] = (gt * _sig(gt) * up).astype(BF16)

    o_spec = pl.BlockSpec((tm, tn), lambda j, i: (i, j))
    sds = jax.ShapeDtypeStruct((s_dim, hidden), BF16)
    return pl.pallas_call(
        body,
        out_shape=[sds, sds, sds],
        grid=(nt, s_dim // tm),
        in_specs=[pl.BlockSpec((tm, k_dim), lambda j, i: (i, 0)), pl.BlockSpec((tn, k_dim), lambda j, i: (j, 0)),
                  pl.BlockSpec((tn, k_dim), lambda j, i: (j + nt, 0))],
        out_specs=[o_spec, o_spec, o_spec],
        compiler_params=pltpu.CompilerParams(dimension_semantics=("parallel", "parallel")),
        name="ffn_in_act",
    )(h1b, w_t, w_t)


def _act_bwd(gt, up, d_act):
    def fn(r, c):
        gt_, up_, da = r
        return [jnp.concatenate([da * up_ * _silu_grad(gt_), da * gt_ * _sig(gt_)], axis=1)], []

    return _rowwise(fn, [gt, up, d_act], [], [(2 * FFN_HIDDEN, BF16)], name="act_bwd")[0]


def _tail(h1, ffn, p, tgt, g, b, w_pg, w_ple_t):
    def fn(r, c):
        h1_, ffn_, p_, t_ = r
        pg_ = _dot(h1_, c[2])
        pp_ = _dot(p_, c[3], NT)
        sp = _sig(pg_)
        xh, rr = _ln_stats(ALPHA * h1_ + ffn_ + sp * pp_)
        y = xh * c[0] + c[1]
        err = y - t_
        dy = err * (1.0 / D_MODEL)
        dz = _ln_bwd(dy, xh, rr, c[0])
        loss = jnp.sum(0.5 * _rowmean(err * err), axis=0, keepdims=True)
        return ([dz, dz * pp_ * sp * (1.0 - sp), dz * sp, ALPHA * dz],
                [_colsum(dy * xh), _colsum(dy), jnp.broadcast_to(loss, (1, LANES))])

    return _rowwise(fn, [h1, ffn, p, tgt], [g, b, w_pg, w_ple_t], [(D_MODEL, BF16)] * 3 + [(D_MODEL, F32)],
                    accs=[(1, D_MODEL), (1, D_MODEL), (1, LANES)], tm=FUSED_ROWS, name="tail")


def _local_step(x, p, pos, tgt, w, late_weights, emit):
    w = dict(w)
    s_dim = x.shape[0]
    xb, pb = x.astype(BF16), p.astype(BF16)
    proj_a = _mm(xb, w["w_in_t"], tb=True, b_rows=(0, 4 * D_MODEL), name="f_proj_a")
    proj_g = _mm(xb, w["wg_t"], tb=True, out_dtype=BF16, name="f_proj_g")
    proj_b = _mm(xb, w["wb_t"], tb=True, name="f_proj_b")
    qkvn = _conv_fwd(proj_a, w["conv"])
    beta, gc = _gates_fwd(proj_b, w["alog"], w["dtb"])
    gc_t = jnp.transpose(gc[:, :N_HEADS])
    u, w_, qd, kt, a_mat, t_fold = _gdr_prep_fwd(qkvn, beta, gc, gc_t)
    o_dn, states = _gdr_scan_fwd(u, w_, qd, kt, a_mat, gc)
    w.update(late_weights("mix", o_dn))
    og, y_dn = _gdr_out_fwd(o_dn, proj_a, w["dnw"], w["br_dn"])
    c_q, c_kv, qc, kc, vv, kct, vt = _mla_prep_fwd(proj_b, pos, w["qnw"], w["kvnw"], w["uq"], w["uk"], w["uv"])
    o_mla, lse = _attn_fwd(qc, kc, vt)
    y_mla = _mm(o_mla, w["br_mla"], out_dtype=BF16, name="f_y_mla")
    mixed, a1, h1, h1b = _mix_proj_ln1(y_dn, y_mla, proj_g, x, w["wo"], w["ln1g"], w["ln1b"])
    w.update(late_weights("ffn", a1))
    gt, up, act = _ffn_in_act(h1b, w["ffn_in_t"])
    ffn = _mm_resident(act, w["ffn_out"], name="f_ffn")
    g = {}
    dz2, d_pg, d_pp, dh1a, g["ln2g"], g["ln2b"], loss = _tail(h1, ffn, pb, tgt, w["ln2g"], w["ln2b"],
                                                            w["ple_gate"], w["ple_t"])
    g["ple_t"] = _mm(d_pp, pb, ta=True, out_dtype=BF16, name="b_w_ple")
    g["ple_gate"] = _mm(h1b, d_pg, ta=True, out_dtype=BF16, name="b_w_ple_gate")
    g["ffn_out"] = _mm(act, dz2, ta=True, out_dtype=BF16, name="b_w_ffn_out")
    d_act = _mm(dz2, w["ffn_out"], tb=True, out_dtype=BF16, name="b_act")
    d_gu = _act_bwd(gt, up, d_act)
    g["ffn_in_t"] = _mm(d_gu, h1b, ta=True, out_dtype=BF16, name="b_w_ffn_in")
    d_gu = emit("ffn", g, d_gu)
    d_h1 = _mm_resident(d_gu, w["ffn_in_t"], add=(dh1a,), name="b_h1_ffn")
    dz1, dxa, d_proj_g, d_y_dn, d_y_mla, g["ln1g"], g["ln1b"] = _ln1_mix_bwd(
        x, a1, d_h1, d_pg, y_dn, y_mla, proj_g, w["ln1g"], w["wo"], w["ple_gate"])
    g["wo"] = _mm(mixed, dz1, ta=True, out_dtype=BF16, name="b_w_o")
    g["br_mla"] = _mm(o_mla, d_y_mla, ta=True, out_dtype=BF16, name="b_w_br_mla")
    d_o_mla = _mm(d_y_mla, w["br_mla"], tb=True, out_dtype=BF16, name="b_o_mla")
    d_qc, d_kc, d_v = _attn_bwd(qc, kc, kct, vv, o_mla, d_o_mla, lse)
    d_q_full, d_kn, d_cq, d_ckv, d_kr, g["qnw"], g["kvnw"] = _mla_prep_bwd(
        proj_b, pos, d_qc, d_kc, d_v, w["qnw"], w["kvnw"], w["uq"], w["uk"], w["uv"])
    g["uq"] = _mm(c_q, d_q_full, ta=True, out_dtype=BF16, name="b_w_uq")
    g["uk"] = _mm(c_kv, d_kn, ta=True, out_dtype=BF16, name="b_w_uk")
    g["uv"] = _mm(c_kv, d_v, ta=True, out_dtype=BF16, name="b_w_uv")
    g["br_dn"] = _mm(og, d_y_dn, ta=True, out_dtype=BF16, name="b_w_br_dn")
    d_y_dn = emit("mix", g, d_y_dn)
    d_o_dn, d_z, g["dnw"] = _gdr_out_bwd(o_dn, proj_a, d_y_dn, w["dnw"], w["br_dn"])
    du, dw, dqd, dkt, d_a, d_egl = _gdr_scan_bwd(u, w_, qd, kt, a_mat, gc, states, d_o_dn)
    dq, dk, dv, d_beta, d_gc = _gdr_prep_bwd(qkvn, beta, gc, gc_t, t_fold, u, w_, du, dw, dqd, dkt, d_a)
    d_egl_rows = jnp.pad(d_egl[:, None, :, 0], ((0, 0), (CHUNK - 1, 0), (0, LANES - N_HEADS))).reshape(s_dim, LANES)
    d_ba, g["alog"], g["dtb"] = _gates_bwd(proj_b, w["alog"], w["dtb"], gc, d_beta, d_gc, d_egl_rows)
    d_qkv, g["conv"] = _conv_bwd(proj_a, w["conv"], dq, dk, dv)
    zeros = jnp.zeros((s_dim, WB_CKV - Q_LORA), BF16)
    d_proj_b = jnp.concatenate([d_cq, zeros, d_ckv, d_kr, d_ba], axis=1)
    g["wa_qkv_t"] = _mm(d_qkv, xb, ta=True, name="b_w_qkv")
    g["wa_z_t"] = _mm(d_z, xb, ta=True, name="b_w_z")
    g["wg_t"] = _mm(d_proj_g, xb, ta=True, name="b_w_g")
    g["wb_t"] = _mm(d_proj_b, xb, ta=True, name="b_w_b")
    d_qkv = emit("small", dict(g, loss=loss), emit("w_in", g, d_qkv))
    dx = _input_grad(d_qkv, d_z, d_proj_g, d_proj_b, w["w_in_t"], w["wg_t"], w["wb_t"], dxa)
    return loss, dx, g


DX_ROWS = 512


def _input_grad(d_qkv, d_z, d_g, d_b, w_in_t, wg_t, wb_t, add):
    s_dim = d_qkv.shape[0]
    n_qkv, n_a = d_qkv.shape[1], d_qkv.shape[1] + d_z.shape[1]

    def body(q_ref, z_ref, g_ref, b_ref, wa_ref, wg_ref, wb_ref, add_ref, o_ref):
        r = add_ref[...] + _dot(q_ref[...], wa_ref[0:n_qkv])
        r = r + _dot(z_ref[...], wa_ref[n_qkv:n_a])
        r = r + _dot(g_ref[...], wg_ref[...])
        o_ref[...] = r + _dot(b_ref[...], wb_ref[...])

    rows = lambda a: pl.BlockSpec((DX_ROWS, a.shape[1]), lambda i: (i, 0))
    whole = lambda shape: pl.BlockSpec(shape, lambda i: (0, 0), pipeline_mode=pl.Buffered(1))
    return pl.pallas_call(
        body,
        out_shape=jax.ShapeDtypeStruct((s_dim, D_MODEL), F32),
        grid=(s_dim // DX_ROWS,),
        in_specs=[rows(d_qkv), rows(d_z), rows(d_g), rows(d_b), whole((n_a, D_MODEL)), whole(wg_t.shape),
                  whole(wb_t.shape), rows(add)],
        out_specs=pl.BlockSpec((DX_ROWS, D_MODEL), lambda i: (i, 0)),
        compiler_params=pltpu.CompilerParams(dimension_semantics=("parallel",)),
        name="b_x",
    )(d_qkv, d_z, d_g, d_b, w_in_t, wg_t, wb_t, add)


_BIG = (("w_in", 1), ("w_uq", 0), ("w_uk", 0), ("w_uv", 0), ("w_br_dn", 0), ("w_br_mla", 0),
        ("w_o", 0), ("w_ffn_in", 1), ("w_ffn_out", 0), ("w_ple", 1), ("w_ple_gate", 0))
_BIG_AXIS = dict(_BIG)
_SMALL = ("ln1_g", "ln1_b", "ln2_g", "ln2_b", "q_norm_w", "kv_norm_w", "dn_norm_w", "dn_a_log", "dn_dt_bias")
_ORDER = ("w_in", "conv_w", "dn_a_log", "dn_dt_bias", "dn_norm_w", "q_norm_w", "w_uq", "kv_norm_w", "w_uk", "w_uv",
          "w_br_dn", "w_br_mla", "w_o", "ln1_g", "ln1_b", "w_ffn_in", "w_ffn_out", "w_ple", "w_ple_gate", "ln2_g",
          "ln2_b")


def _stored_shape(name, shard_shape):
    axis = _BIG_AXIS[name]
    lead = shard_shape[axis]
    return lead, int(np.prod(shard_shape)) // lead


def _to_stored(name, shard):
    return jnp.moveaxis(shard, _BIG_AXIS[name], 0).reshape(_stored_shape(name, shard.shape))


def _from_stored(name, stored, shard_shape):
    axis = _BIG_AXIS[name]
    moved = (shard_shape[axis],) + shard_shape[:axis] + shard_shape[axis + 1:]
    return jnp.moveaxis(stored.reshape(moved), 0, axis)


_W_IN_ROWS = np.cumsum([0, 3072, 1024, 8, 8, Q_LORA, KV_LORA, ROPE, D_MODEL, D_MODEL])


def _first_weights(w_in_t, conv_full, small):
    r = _W_IN_ROWS
    zr = lambda n: jnp.zeros((n, D_MODEL), w_in_t.dtype)
    w = {}
    w["w_in_t"] = w_in_t
    w["wg_t"] = w_in_t[r[7]:r[9]]
    w["wb_t"] = jnp.concatenate([w_in_t[r[4]:r[5]], zr(WB_CKV - Q_LORA), w_in_t[r[5]:r[7]], zr(LANES - ROPE),
                                 w_in_t[r[2]:r[4]], zr(LANES - 2 * N_HEADS)], axis=0)
    w["conv"] = conv_full
    pad_l = lambda v: jnp.pad(v, ((0, 0), (0, LANES - v.shape[1])))
    w["alog"], w["dtb"] = pad_l(small["dn_a_log"]), pad_l(small["dn_dt_bias"])
    w["dnw"], w["qnw"], w["kvnw"] = small["dn_norm_w"], small["q_norm_w"], small["kv_norm_w"]
    w["ln1g"], w["ln1b"], w["ln2g"], w["ln2b"] = small["ln1_g"], small["ln1_b"], small["ln2_g"], small["ln2_b"]
    return w


def _late_weights(group, fw):
    w = {}
    if group == "mix":
        uq = fw["w_uq"].reshape(Q_LORA, N_HEADS, HEAD + ROPE)
        uq_r = jnp.pad(uq[:, :, HEAD:], ((0, 0), (0, 0), (0, HEAD - ROPE)))
        w["uq"] = jnp.concatenate([uq[:, :, :HEAD].reshape(Q_LORA, -1), uq_r.reshape(Q_LORA, -1)], axis=1)
        w["uk"], w["uv"] = fw["w_uk"], fw["w_uv"]
        w["br_dn"], w["br_mla"], w["wo"] = fw["w_br_dn"], fw["w_br_mla"], fw["w_o"]
    else:
        w["ffn_in_t"], w["ffn_out"] = fw["w_ffn_in"], fw["w_ffn_out"]
        w["ple_t"], w["ple_gate"] = fw["w_ple"], fw["w_ple_gate"]
    return w


_GROUP_GRADS = {"ffn": (("w_ple", "ple_t"), ("w_ple_gate", "ple_gate"), ("w_ffn_out", "ffn_out"),
                        ("w_ffn_in", "ffn_in_t")),
                "mix": (("w_o", "wo"), ("w_br_mla", "br_mla"), ("w_uq", "uq"), ("w_uk", "uk"), ("w_uv", "uv"),
                        ("w_br_dn", "br_dn"))}


def _group_grads(group, g):
    out = {}
    for name, key in _GROUP_GRADS[group]:
        t = g[key]
        if name == "w_uq":
            uq_n = t[:, :D_MODEL].reshape(Q_LORA, N_HEADS, HEAD)
            uq_r = t[:, D_MODEL:].reshape(Q_LORA, N_HEADS, HEAD)[:, :, :ROPE]
            t = jnp.concatenate([uq_n, uq_r], axis=2).reshape(Q_LORA, -1)
        out[name] = t
    return out


PACK_ROWS = 512
SUBLANES = 8


def _pack_exchange(parts, name):
    arrays = []
    for a, _, _ in parts:
        if not any(a is b for b in arrays):
            arrays.append(a)
    index = lambda a: next(i for i, b in enumerate(arrays) if a is b)
    chunks, dst = [], 0
    for a, first, rows in parts:
        assert first % SUBLANES == 0 and rows % SUBLANES == 0
        chunks += [(index(a), first + o, dst + o, min(PACK_ROWS, rows - o)) for o in range(0, rows, PACK_ROWS)]
        dst += rows
    c, n, last = arrays[0].shape[1], len(arrays), len(chunks) - 1
    slab = dst // N_DEV
    assert slab * N_DEV == dst

    def body(*refs):
        src_refs, out_ref, recv_ref = refs[:n], refs[n], refs[n + 1]
        buf, sem_in, sem_out, send_sems, recv_sems = refs[n + 2:]
        x, y, core = lax.axis_index("x"), lax.axis_index("y"), lax.axis_index("c")

        def to_sibling(q):
            return pltpu.make_async_remote_copy(
                src_ref=out_ref.at[pl.ds((2 * q + 1 - core) * slab, slab)], dst_ref=recv_ref.at[q],
                send_sem=send_sems.at[q], recv_sem=recv_sems.at[q], device_id=(x, y, 1 - core),
                device_id_type=_MESH_ID)

        sent = [0]

        def send_packed(rows_done):
            while sent[0] < N_DEV // 2 and (2 * sent[0] + 2) * slab <= rows_done:
                to_sibling(sent[0]).start()
                sent[0] += 1

        def load(k):
            i, first, _, rows = chunks[k]
            return pltpu.make_async_copy(src_refs[i].at[pl.ds(first, rows)], buf.at[k % 2, pl.ds(0, rows)],
                                         sem_in.at[k % 2])

        def store(k):
            _, _, first, rows = chunks[k]
            return pltpu.make_async_copy(buf.at[k % 2, pl.ds(0, rows)], out_ref.at[pl.ds(first, rows), 0, :],
                                         sem_out.at[k % 2])

        load(0).start()
        for k in range(last + 1):
            load(k).wait()
            store(k).start()
            if k >= 1:
                store(k - 1).wait()
                send_packed(chunks[k][2])
            if k < last:
                load(k + 1).start()
        store(last).wait()
        send_packed(dst)
        for q in range(N_DEV // 2):
            to_sibling(q).wait_recv()
        for q in range(N_DEV // 2):
            to_sibling(q).wait_send()

    return pl.pallas_call(
        body,
        out_shape=[jax.ShapeDtypeStruct((dst, 1, c), F32), jax.ShapeDtypeStruct((N_DEV // 2, slab, 1, c), F32)],
        in_specs=[_ANY] * n,
        out_specs=[_ANY, _ANY],
        scratch_shapes=[pltpu.VMEM((2, PACK_ROWS, c), F32), pltpu.SemaphoreType.DMA((2,)),
                        pltpu.SemaphoreType.DMA((2,)), pltpu.SemaphoreType.DMA((N_DEV // 2,)),
                        pltpu.SemaphoreType.DMA((N_DEV // 2,))],
        name=name,
    )(*arrays)


def _w_in_grad_parts(g):
    wb = g["wb_t"]
    return [(g["wa_qkv_t"], 0, 3 * D_MODEL), (g["wa_z_t"], 0, D_MODEL), (wb, WB_BA, 2 * N_HEADS),
            (wb, WB_CQ, Q_LORA), (wb, WB_CKV, KV_LORA), (wb, WB_KR, ROPE), (g["wg_t"], 0, 2 * D_MODEL)]


def _small_grads(g):
    return {"ln1_g": g["ln1g"], "ln1_b": g["ln1b"], "ln2_g": g["ln2g"], "ln2_b": g["ln2b"], "q_norm_w": g["qnw"],
            "kv_norm_w": g["kvnw"], "dn_norm_w": g["dnw"], "dn_a_log": g["alog"], "dn_dt_bias": g["dtb"],
            "conv_w": g["conv"]}


_SMALL_SLOTS = {"ln1_g": (0, 0, 1024), "ln1_b": (1, 0, 1024), "ln2_g": (2, 0, 1024), "ln2_b": (3, 0, 1024),
                "q_norm_w": (4, 0, 384), "kv_norm_w": (4, 384, 256), "dn_norm_w": (4, 640, 128),
                "dn_a_log": (4, 768, 8), "dn_dt_bias": (4, 896, 8)}
_SMALL_ROWS, _LOSS_ROW, _CONV_ROW0, _CONV_ROWS = 24, 5, 8, 12


def _pack_small_grads(small_g, loss):
    zeros = lambda r, c: jnp.zeros((r, c), F32)
    row4 = jnp.concatenate([small_g["q_norm_w"], small_g["kv_norm_w"], small_g["dn_norm_w"], small_g["dn_a_log"],
                            small_g["dn_dt_bias"]], axis=1)
    row5 = jnp.concatenate([loss, zeros(1, FLAT_COLS - LANES)], axis=1)
    head = jnp.concatenate([small_g["ln1_g"], small_g["ln1_b"], small_g["ln2_g"], small_g["ln2_b"], row4, row5,
                            zeros(2, FLAT_COLS)], axis=0)
    conv = small_g["conv_w"].reshape(_CONV_ROWS, FLAT_COLS)
    return jnp.concatenate([head, conv, zeros(_SMALL_ROWS - _CONV_ROW0 - _CONV_ROWS, FLAT_COLS)], axis=0)


_MESH_ID = pl.DeviceIdType.MESH
_ANY = pl.BlockSpec(memory_space=pl.ANY)


def _all_gather(blocks, name):
    n = len(blocks)

    def body(*refs):
        x_refs, out_refs = refs[:n], refs[n:2 * n]
        send_sems, recv_sems, local_sems = refs[2 * n:]
        x, y, c = lax.axis_index("x"), lax.axis_index("y"), lax.axis_index("c")
        me, sibling = (x, y, c), (x, y, 1 - c)
        chips = [(1 - x, y), (x, 1 - y), (1 - x, 1 - y)]

        def slot(i, px, py, pc):
            return out_refs[i].at[4 * px + 2 * py + pc]

        def copy(i, k, origin, to, src=None):
            return pltpu.make_async_remote_copy(
                src_ref=slot(i, *origin) if src is None else src, dst_ref=slot(i, *origin),
                send_sem=send_sems.at[7 * i + k], recv_sem=recv_sems.at[7 * i + k], device_id=to,
                device_id_type=_MESH_ID)

        mine = [pltpu.make_async_copy(x_refs[i], slot(i, *me), local_sems.at[i]) for i in range(n)]
        first, passed = [], []
        for i in range(n):
            mine[i].start()
            first.append(copy(i, 0, me, sibling, src=x_refs[i]))
            first += [copy(i, 1 + j, me, (*chip, c), src=x_refs[i]) for j, chip in enumerate(chips)]
        for cp in first:
            cp.start()
        for i in range(n):
            for j, chip in enumerate(chips):
                copy(i, 1 + j, (*chip, c), me).wait_recv()
                passed.append(copy(i, 4 + j, (*chip, c), sibling))
                passed[-1].start()
        for i in range(n):
            copy(i, 0, sibling, me).wait_recv()
            for j, chip in enumerate(chips):
                copy(i, 4 + j, (*chip, 1 - c), me).wait_recv()
        for cp in first + passed:
            cp.wait_send()
        for cp in mine:
            cp.wait()

    return pl.pallas_call(
        body,
        out_shape=[jax.ShapeDtypeStruct((N_DEV,) + b.shape, b.dtype) for b in blocks],
        in_specs=[_ANY] * n,
        out_specs=[_ANY] * n,
        scratch_shapes=[pltpu.SemaphoreType.DMA((7 * n,)), pltpu.SemaphoreType.DMA((7 * n,)),
                        pltpu.SemaphoreType.DMA((n,))],
        name=name,
    )(*blocks)


def _col_tile(c):
    return c if c <= 256 else 256


def _chip_sum(src, recv, parity, name):
    _, r, _, c = src.shape
    tc = _col_tile(c)

    def body(par_ref, a_ref, b_ref, o_ref, ob_ref):
        s = a_ref[...] + b_ref[...]
        o_ref[...] = s
        ob_ref[...] = s.astype(BF16)

    rows = lambda f: pl.BlockSpec((None, r, None, tc), f)
    blk = pl.BlockSpec((None, r, tc), lambda q, j, par: (q, 0, j))
    return pl.pallas_call(
        body,
        out_shape=[jax.ShapeDtypeStruct((4, r, c), F32), jax.ShapeDtypeStruct((4, r, c), BF16)],
        grid_spec=pltpu.PrefetchScalarGridSpec(
            num_scalar_prefetch=1, grid=(4, c // tc),
            in_specs=[rows(lambda q, j, par: (2 * q + par[0], 0, 0, j)), rows(lambda q, j, par: (q, 0, 0, j))],
            out_specs=[blk, blk]),
        compiler_params=pltpu.CompilerParams(dimension_semantics=("parallel", "parallel")),
        name=name,
    )(parity, src, recv)


_HBM = pl.BlockSpec(memory_space=pltpu.HBM)
_SEM = pl.BlockSpec(memory_space=pltpu.SEMAPHORE)
_DATAFLOW = pltpu.SideEffectType.DATAFLOW_SIDE_EFFECTING
N_PEERS = N_DEV - 1


def _ring_peer(j):
    me = 4 * lax.axis_index("x") + 2 * lax.axis_index("y") + lax.axis_index("c")
    k = (me + j) % N_DEV
    return me, k, (k // 4, (k // 2) % 2, k % 2)


def _spread_copy(i, j, src_refs, land_refs, send_sems, recv_sems, scatter):
    me, k, peer = _ring_peer(j)
    return pltpu.make_async_remote_copy(
        src_ref=src_refs[i].at[k] if scatter else src_refs[i], dst_ref=land_refs[i].at[me],
        send_sem=send_sems.at[N_PEERS * i + j - 1], recv_sem=recv_sems.at[N_PEERS * i + j - 1], device_id=peer,
        device_id_type=_MESH_ID)


def _spread_start(srcs, carry, scatter, name):
    n = len(srcs)
    lands = [lax.empty(((N_DEV,) + s.shape[-2:]), s.dtype) for s in srcs]

    def body(*refs):
        src_refs, land_refs = refs[:n], refs[n:2 * n]
        send_sems, recv_sems, local_sems = refs[2 * n + 1:2 * n + 4]
        for i in range(n):
            for j in range(1, N_DEV):
                _spread_copy(i, j, src_refs, land_refs, send_sems, recv_sems, scatter).start()
        for i in range(n):
            _own_copy(i, src_refs, land_refs, local_sems, scatter).start()

    hbm = lambda a: pltpu.HBM(a.shape, a.dtype)
    sems = pltpu.SemaphoreType.DMA((N_PEERS * n,))
    pinned = [pltpu.with_memory_space_constraint(a, pltpu.HBM) for a in list(srcs) + lands + [carry]]
    res = pl.pallas_call(
        body, name=name,
        out_shape=(sems, sems, pltpu.SemaphoreType.DMA((n,)), *[hbm(a) for a in pinned]),
        in_specs=[_HBM] * (2 * n + 1),
        out_specs=(_SEM, _SEM, _SEM, *[_HBM] * (2 * n + 1)),
        input_output_aliases={i: 3 + i for i in range(2 * n + 1)},
        compiler_params=pltpu.CompilerParams(has_side_effects=_DATAFLOW),
    )(*pinned)
    return res[:3], list(res[3:3 + n]), list(res[3 + n:3 + 2 * n]), res[3 + 2 * n]


def _own_copy(i, src_refs, land_refs, local_sems, scatter):
    me = _ring_peer(0)[0]
    return pltpu.make_async_copy(src_refs[i].at[me] if scatter else src_refs[i], land_refs[i].at[me],
                                 local_sems.at[i])


def _spread_wait(started, after, scatter, name):
    sems, srcs, lands, _ = started
    n = len(srcs)

    def body(*refs):
        src_refs, land_refs = refs[:n], refs[n:2 * n]
        send_s, recv_s, local_s = refs[2 * n:2 * n + 3]
        for i in range(n):
            for j in range(1, N_DEV):
                cp = _spread_copy(i, j, src_refs, land_refs, send_s, recv_s, scatter)
                cp.wait_send()
                cp.wait_recv()
        for i in range(n):
            _own_copy(i, src_refs, land_refs, local_s, scatter).wait()

    hbm = lambda a: pltpu.HBM(a.shape, a.dtype)
    res = pl.pallas_call(
        body, name=name,
        out_shape=tuple(hbm(a) for a in srcs + lands),
        in_specs=[_HBM] * (2 * n) + [_SEM, _SEM, _SEM, pl.BlockSpec(memory_space=pl.ANY)],
        out_specs=tuple([_HBM] * (2 * n)),
        input_output_aliases={i: i for i in range(2 * n)},
        compiler_params=pltpu.CompilerParams(has_side_effects=_DATAFLOW),
    )(*srcs, *lands, *sems, after)
    return list(res[n:])


def _chips_copy(i, j, src_refs, land_refs, send_sems, recv_sems):
    x, y, c = lax.axis_index("x"), lax.axis_index("y"), lax.axis_index("c")
    tx, ty = [(1 - x, y), (x, 1 - y), (1 - x, 1 - y)][j]
    return pltpu.make_async_remote_copy(
        src_ref=src_refs[i].at[2 * tx + ty], dst_ref=land_refs[i].at[j], send_sem=send_sems.at[3 * i + j],
        recv_sem=recv_sems.at[3 * i + j], device_id=(tx, ty, c), device_id_type=_MESH_ID)


def _chips_start(srcs, carry, name):
    n = len(srcs)
    lands = [lax.empty((3,) + s.shape[1:], s.dtype) for s in srcs]

    def body(*refs):
        src_refs, land_refs = refs[:n], refs[n:2 * n]
        send_sems, recv_sems = refs[2 * n + 1:2 * n + 3]
        for i in range(n):
            for j in range(3):
                _chips_copy(i, j, src_refs, land_refs, send_sems, recv_sems).start()

    hbm = lambda a: pltpu.HBM(a.shape, a.dtype)
    sems = pltpu.SemaphoreType.DMA((3 * n,))
    pinned = [pltpu.with_memory_space_constraint(a, pltpu.HBM) for a in list(srcs) + lands + [carry]]
    res = pl.pallas_call(
        body, name=name,
        out_shape=(sems, sems, *[hbm(a) for a in pinned]),
        in_specs=[_HBM] * (2 * n + 1),
        out_specs=(_SEM, _SEM, *[_HBM] * (2 * n + 1)),
        input_output_aliases={i: 2 + i for i in range(2 * n + 1)},
        compiler_params=pltpu.CompilerParams(has_side_effects=_DATAFLOW),
    )(*pinned)
    return res[:2], list(res[2:2 + n]), list(res[2 + n:2 + 2 * n]), res[2 + 2 * n]


def _chips_wait(started, after, name):
    sems, srcs, lands, _ = started
    n = len(srcs)

    def body(*refs):
        src_refs, land_refs = refs[:n], refs[n:2 * n]
        send_s, recv_s = refs[2 * n:2 * n + 2]
        for i in range(n):
            for j in range(3):
                cp = _chips_copy(i, j, src_refs, land_refs, send_s, recv_s)
                cp.wait_send()
                cp.wait_recv()

    hbm = lambda a: pltpu.HBM(a.shape, a.dtype)
    res = pl.pallas_call(
        body, name=name,
        out_shape=tuple(hbm(a) for a in srcs + lands),
        in_specs=[_HBM] * (2 * n) + [_SEM, _SEM, pl.BlockSpec(memory_space=pl.ANY)],
        out_specs=tuple([_HBM] * (2 * n)),
        input_output_aliases={i: i for i in range(2 * n)},
        compiler_params=pltpu.CompilerParams(has_side_effects=_DATAFLOW),
    )(*srcs, *lands, *sems, after)
    return list(res[n:])


def _sum8(landing, name):
    _, r, c = landing.shape
    tc = _col_tile(c)

    def body(a_ref, o_ref):
        tot = a_ref[0].astype(F32)
        for k in range(1, N_DEV):
            tot = tot + a_ref[k].astype(F32)
        o_ref[...] = tot

    return pl.pallas_call(
        body,
        out_shape=jax.ShapeDtypeStruct((r, c), F32),
        grid=(c // tc,),
        in_specs=[pl.BlockSpec((N_DEV, r, tc), lambda j: (0, 0, j))],
        out_specs=pl.BlockSpec((r, tc), lambda j: (0, j)),
        compiler_params=pltpu.CompilerParams(dimension_semantics=("parallel",)),
        name=name,
    )(landing)


def _adamw_math(w, g, m, v):
    m = ADAM_B1 * m + (1.0 - ADAM_B1) * g
    v = ADAM_B2 * v + (1.0 - ADAM_B2) * (g * g)
    m_hat = m / (1.0 - ADAM_B1 ** ADAM_STEP)
    v_hat = v / (1.0 - ADAM_B2 ** ADAM_STEP)
    delta = -ADAM_LR * (m_hat / (jnp.sqrt(v_hat) + ADAM_EPS) + ADAM_WD * w)
    return delta, m, v


def _adamw(w, m, v, g, name):
    r, c = w.shape

    def fn(rows, consts):
        return list(_adamw_math(*rows)), []

    return _rowwise(fn, [w, g, m, v], [], [(c, F32)] * 3, tm=r if r <= 512 else 256, name=name)


def _adamw_sum8(w, m, v, landing, name):
    r, c = w.shape
    tc = _col_tile(c)

    def body(w_ref, m_ref, v_ref, a_ref, g_ref, d_ref, m2_ref, v2_ref):
        g = a_ref[0].astype(F32)
        for k in range(1, N_DEV):
            g = g + a_ref[k].astype(F32)
        delta, m2, v2 = _adamw_math(w_ref[...], g, m_ref[...], v_ref[...])
        g_ref[...] = g
        d_ref[...] = delta
        m2_ref[...] = m2
        v2_ref[...] = v2

    blk = pl.BlockSpec((r, tc), lambda j: (0, j))
    return pl.pallas_call(
        body,
        out_shape=[jax.ShapeDtypeStruct((r, c), F32)] * 4,
        grid=(c // tc,),
        in_specs=[blk, blk, blk, pl.BlockSpec((N_DEV, r, tc), lambda j: (0, 0, j))],
        out_specs=[blk] * 4,
        compiler_params=pltpu.CompilerParams(dimension_semantics=("parallel",)),
        name=name,
    )(w, m, v, landing)


def _adamw_parts(w, m, v, own, others, chip, name):
    r, _, c = w.shape
    tc = _col_tile(c)

    def body(q_ref, w_ref, m_ref, v_ref, a_ref, b_ref, g_ref, d_ref, m2_ref, v2_ref):
        g = ((a_ref[...] + b_ref[0].astype(F32)) + b_ref[1].astype(F32)) + b_ref[2].astype(F32)
        delta, m2, v2 = _adamw_math(w_ref[...], g, m_ref[...], v_ref[...])
        g_ref[...] = g
        d_ref[...] = delta
        m2_ref[...] = m2
        v2_ref[...] = v2

    row = pl.BlockSpec((r, None, tc), lambda j, q: (0, 0, j))
    return pl.pallas_call(
        body,
        out_shape=[jax.ShapeDtypeStruct((r, 1, c), F32)] * 4,
        grid_spec=pltpu.PrefetchScalarGridSpec(
            num_scalar_prefetch=1, grid=(c // tc,),
            in_specs=[row, row, row, pl.BlockSpec((None, r, tc), lambda j, q: (q[0], 0, j)),
                      pl.BlockSpec((3, r, tc), lambda j, q: (0, 0, j))],
            out_specs=[row] * 4),
        compiler_params=pltpu.CompilerParams(dimension_semantics=("parallel",)),
        name=name,
    )(chip, w, m, v, own, others)


def _adamw_small(gathered, params):
    ns = len(_SMALL)

    def body(*refs):
        g_ref, p_refs, o_refs = refs[0], refs[1:1 + 3 * ns], refs[1 + 3 * ns:]
        tot = g_ref[0]
        for k in range(1, N_DEV):
            tot = tot + g_ref[k]
        for i, name in enumerate(_SMALL):
            row, lane0, lanes = _SMALL_SLOTS[name]
            g = tot[row:row + 1, lane0:lane0 + lanes]
            w_, m_, v_ = (p_refs[3 * i + j][...] for j in range(3))
            delta, m2, v2 = _adamw_math(w_, g, m_, v_)
            for j, val in enumerate((g, delta, m2, v2)):
                o_refs[4 * i + j][...] = val
        o_refs[4 * ns][...] = tot[_LOSS_ROW:_LOSS_ROW + 1, 0:LANES]
        o_refs[4 * ns + 1][...] = tot[_CONV_ROW0:_CONV_ROW0 + _CONV_ROWS, :]

    out_shape = [jax.ShapeDtypeStruct(w.shape, F32) for (w, _, _) in params for _ in range(4)]
    out_shape += [jax.ShapeDtypeStruct((1, LANES), F32), jax.ShapeDtypeStruct((_CONV_ROWS, FLAT_COLS), F32)]
    flat = [a for wmv in params for a in wmv]
    return pl.pallas_call(body, out_shape=out_shape, name="adamw_small")(gathered, *flat)


def kernel(x, p, positions, w_in, conv_w, dn_a_log, dn_dt_bias, dn_norm_w, q_norm_w, w_uq, kv_norm_w, w_uk, w_uv, w_br_dn, w_br_mla, w_o, ln1_g, ln1_b, w_ffn_in, w_ffn_out, w_ple, w_ple_gate, ln2_g, ln2_b, loss_target, m_w_in, m_conv_w, m_dn_a_log, m_dn_dt_bias, m_dn_norm_w, m_q_norm_w, m_w_uq, m_kv_norm_w, m_w_uk, m_w_uv, m_w_br_dn, m_w_br_mla, m_w_o, m_ln1_g, m_ln1_b, m_w_ffn_in, m_w_ffn_out, m_w_ple, m_w_ple_gate, m_ln2_g, m_ln2_b, v_w_in, v_conv_w, v_dn_a_log, v_dn_dt_bias, v_dn_norm_w, v_q_norm_w, v_w_uq, v_kv_norm_w, v_w_uk, v_w_uv, v_w_br_dn, v_w_br_mla, v_w_o, v_ln1_g, v_ln1_b, v_w_ffn_in, v_w_ffn_out, v_w_ple, v_w_ple_gate, v_ln2_g, v_ln2_b):
    args = dict(locals())
    wts = {n: args[n] for n in _ORDER}
    mom1 = {n: args["m_" + n] for n in _ORDER}
    mom2 = {n: args["v_" + n] for n in _ORDER}
    big_names = [n for n, _ in _BIG]
    shard_shapes = {n: wts[n].shape[1:] for n in big_names}
    c_idx = lax.axis_index("c")
    q_idx = 2 * lax.axis_index("x") + lax.axis_index("y")
    parity, chip = c_idx.reshape(1).astype(jnp.int32), q_idx.reshape(1).astype(jnp.int32)

    stored = {n: _to_stored(n, wts[n][0]).astype(BF16) for n in big_names}
    first = _all_gather([stored["w_in"], conv_w[0]], "ag_first")
    group_names = {grp: [n for n, _ in pairs] for grp, pairs in _GROUP_GRADS.items()}
    carry, gathers = first[0], {}
    for grp in ("mix", "ffn"):
        gathers[grp] = _spread_start([stored[n] for n in group_names[grp]], carry, False, "ag_start_" + grp)
        carry = gathers[grp][3]
    conv_full = jnp.moveaxis(first[1], 0, 1).reshape(conv_w.shape[1], -1)
    small_w = {n: wts[n].astype(F32) for n in _SMALL}
    w = _first_weights(carry.reshape(-1, D_MODEL), conv_full, small_w)

    def late_weights(grp, after):
        got = _spread_wait(gathers[grp], after, False, "ag_wait_" + grp)
        return _late_weights(grp, {n: t.reshape(-1, t.shape[-1]) for n, t in zip(group_names[grp], got)})

    started = {}

    def emit(group, g, carry):
        if group == "w_in":
            rows, cols = _stored_shape("w_in", shard_shapes["w_in"])
            packed, from_sibling = _pack_exchange(_w_in_grad_parts(g), "rs_pack_sibling")
            own, own_bf = _chip_sum(packed.reshape(N_DEV, rows, 1, cols), from_sibling, parity, "rs_sum_w_in")
            started["w_in"] = (own, _chips_start([own_bf], carry, "rs_chips_start"))
            return started["w_in"][1][3]
        if group == "small":
            block = _pack_small_grads(_small_grads(g), g["loss"])
            started["small"] = _spread_start([block], carry, False, "ag_start_small")
            return started["small"][3]
        grads = _group_grads(group, g)
        srcs = [grads[n].reshape((N_DEV,) + _stored_shape(n, shard_shapes[n])) for n in grads]
        started[group] = (list(grads), _spread_start(srcs, carry, True, "rs_start_" + group))
        return started[group][1][3]

    s_dim = x.shape[1]
    loss, dx, g = _local_step(x[0], p[0, 0], positions.reshape(s_dim, 1).astype(F32), loss_target[0], w,
                              late_weights, emit)
    own, chips_started = started.pop("w_in")
    small_started = started.pop("small")

    out_g, out_d, out_m, out_v = {}, {}, {}, {}

    def update(n, grad, shp):
        flat2 = (shp[0], int(np.prod(shp[1:])))
        d, m2, v2 = _adamw(wts[n][0].reshape(flat2), mom1[n][0].reshape(flat2), mom2[n][0].reshape(flat2),
                           grad.reshape(flat2), "adamw_" + n)
        out_g[n], out_d[n], out_m[n], out_v[n] = grad, d.reshape(shp), m2.reshape(shp), v2.reshape(shp)

    for group, (names, st) in started.items():
        for n, landing in zip(names, _spread_wait(st, dx, True, "rs_wait_" + group)):
            shp = shard_shapes[n]
            if _BIG_AXIS[n] == 0 or shp[-1] % LANES:
                res = _adamw_sum8(_to_stored(n, wts[n][0]), _to_stored(n, mom1[n][0]), _to_stored(n, mom2[n][0]),
                                  landing, "adamw_" + n)
                out_g[n], out_d[n], out_m[n], out_v[n] = (_from_stored(n, t, shp) for t in res)
                last = res[3]
            else:
                update(n, _from_stored(n, _sum8(landing, "rs_total_" + n), shp), shp)

    from_chips = _chips_wait(chips_started, last, "rs_chips_wait")[0]
    g_small = _spread_wait(small_started, last, False, "ag_wait_small")[0]
    rows_first = lambda a: jnp.transpose(a, (2, 0, 1))
    res = _adamw_parts(rows_first(wts["w_in"]), rows_first(mom1["w_in"]), rows_first(mom2["w_in"]), own, from_chips,
                       chip, "adamw_w_in")
    out_g["w_in"], out_d["w_in"], out_m["w_in"], out_v["w_in"] = (jnp.transpose(t, (1, 2, 0))[0] for t in res)

    res = _adamw_small(g_small, [(wts[n], mom1[n], mom2[n]) for n in _SMALL])
    for i, n in enumerate(_SMALL):
        out_g[n], out_d[n], out_m[n], out_v[n] = res[4 * i:4 * i + 4]
    loss_out = res[4 * len(_SMALL)][0, 0]
    conv_shape = conv_w.shape[1:]
    conv_g = lax.dynamic_slice(res[-1].reshape(conv_shape[0], -1), (0, (2 * q_idx + c_idx) * conv_shape[1]),
                               conv_shape)
    update("conv_w", conv_g, conv_shape)

    expand = lambda d, n: d[n] if n in _SMALL else d[n][None]
    return (loss_out, dx[None], *[expand(out_g, n) for n in _ORDER], *[expand(out_d, n) for n in _ORDER],
            *[expand(out_m, n) for n in _ORDER], *[expand(out_v, n) for n in _ORDER])
```

```python
import functools

import numpy as np
import jax
import jax.numpy as jnp
from jax import lax
from jax.experimental import pallas as pl
from jax.experimental.pallas import tpu as pltpu

F32 = jnp.float32
BF16 = jnp.bfloat16

D_MODEL = 1024
N_HEADS = 8
HEAD = 128
CHUNK = 64
GROUP = 256
ROPE = 64
Q_LORA = 384
KV_LORA = 256
FFN_HIDDEN = 2816
PLE_DIM = 256
ROPE_BASE = 10000.0
ALPHA = 2.0 ** 0.25
SCALE = float((HEAD + ROPE) ** -0.5)
NEG_BIG = -1e30
EPS_RMS = 1e-6
EPS_LN = 1e-5

ADAM_LR = 0.001
ADAM_B1 = 0.9
ADAM_B2 = 0.999
ADAM_EPS = 1e-08
ADAM_WD = 0.01
ADAM_STEP = 10

N_DEV = 8
LANES = 128
FLAT_COLS = 1024

WB_CQ, WB_CKV, WB_KR, WB_BA, WB_COLS = 0, 512, 768, 896, 1024

HIGHEST = lax.Precision.HIGHEST

NN = (((1,), (0,)), ((), ()))
TN = (((0,), (0,)), ((), ()))
NT = (((1,), (1,)), ((), ()))


def _dot(a, b, dims=NN):
    return lax.dot_general(a.astype(BF16), b.astype(BF16), dims, preferred_element_type=F32)


def _dot32(a, b, dims=NN):
    return lax.dot_general(a, b, dims, precision=HIGHEST, preferred_element_type=F32)


def _sig(x):
    return 1.0 / (1.0 + jnp.exp(-x))


MM_TILE = 1536


def _pick_wide(n):
    if n <= MM_TILE:
        return n
    return max(t for t in range(LANES, MM_TILE + 1, LANES) if n % t == 0)


def _split_bf16(a):
    hi = a.astype(BF16)
    return hi, (a - hi.astype(F32)).astype(BF16)


def _dot3(a, b, dims=NN):
    ah, al = a if isinstance(a, tuple) else _split_bf16(a)
    bh, bl = b if isinstance(b, tuple) else _split_bf16(b)
    d = lambda p, q: lax.dot_general(p, q, dims, preferred_element_type=F32)
    return d(ah, bh) + (d(ah, bl) + d(al, bh))


def _mm(a, b, *, ta=False, tb=False, add=(), out_dtype=F32, name):
    if ta:
        k_dim, m_dim = a.shape
    else:
        m_dim, k_dim = a.shape
    if tb:
        n_dim, k2 = b.shape
    else:
        k2, n_dim = b.shape
    assert k_dim == k2, (a.shape, b.shape, ta, tb)
    tm = _pick_wide(m_dim)
    tn = _pick_wide(n_dim)
    tk = _pick_wide(k_dim)
    nk = k_dim // tk
    n_add = len(add)
    dims = TN if ta else (NT if tb else NN)
    assert not (ta and tb)

    def body(a_ref, b_ref, *rest):
        add_refs = rest[:n_add]
        o_ref = rest[n_add]
        acc = rest[n_add + 1]
        k = pl.program_id(2)

        @pl.when(k == 0)
        def _():
            acc[...] = jnp.zeros_like(acc)

        acc[...] += _dot(a_ref[...], b_ref[...], dims)

        @pl.when(k == nk - 1)
        def _():
            r = acc[...]
            for ar in add_refs:
                r = r + ar[...].astype(F32)
            o_ref[...] = r.astype(o_ref.dtype)

    a_spec = pl.BlockSpec((tk, tm), lambda i, j, k: (k, i)) if ta else pl.BlockSpec((tm, tk), lambda i, j, k: (i, k))
    b_spec = pl.BlockSpec((tn, tk), lambda i, j, k: (j, k)) if tb else pl.BlockSpec((tk, tn), lambda i, j, k: (k, j))
    o_spec = pl.BlockSpec((tm, tn), lambda i, j, k: (i, j))
    return pl.pallas_call(
        body,
        out_shape=jax.ShapeDtypeStruct((m_dim, n_dim), out_dtype),
        grid=(m_dim // tm, n_dim // tn, nk),
        in_specs=[a_spec, b_spec] + [o_spec] * n_add,
        out_specs=o_spec,
        scratch_shapes=[pltpu.VMEM((tm, tn), F32)],
        compiler_params=pltpu.CompilerParams(dimension_semantics=("parallel", "parallel", "arbitrary")),
        name=name,
    )(a, b, *add)


def _mm_resident(a, b, *, add=(), name):
    m_dim, k_dim = a.shape
    n_dim = b.shape[1]
    assert b.shape[0] == k_dim
    tm = min(DX_ROWS, m_dim)

    def body(a_ref, b_ref, *rest):
        r = _dot(a_ref[...], b_ref[...])
        for ar in rest[:-1]:
            r = r + ar[...]
        rest[-1][...] = r

    o_spec = pl.BlockSpec((tm, n_dim), lambda i: (i, 0))
    return pl.pallas_call(
        body,
        out_shape=jax.ShapeDtypeStruct((m_dim, n_dim), F32),
        grid=(m_dim // tm,),
        in_specs=[pl.BlockSpec((tm, k_dim), lambda i: (i, 0)),
                  pl.BlockSpec((k_dim, n_dim), lambda i: (0, 0), pipeline_mode=pl.Buffered(1))] + [o_spec] * len(add),
        out_specs=o_spec,
        compiler_params=pltpu.CompilerParams(dimension_semantics=("parallel",)),
        name=name,
    )(a, b, *add)


def _rowwise(fn, rows, consts, outs, accs=(), *, tm=256, name):
    rows = [r if isinstance(r, tuple) else (r, 0, r.shape[1]) for r in rows]
    s_dim = rows[0][0].shape[0]
    tm = min(tm, s_dim)
    assert s_dim % tm == 0 and all(arr.shape[0] == s_dim for arr, _, _ in rows)
    specs = [pl.BlockSpec((tm, width), functools.partial(lambda i, cb: (i, cb), cb=cb)) for _, cb, width in rows]
    args = [arr for arr, _, _ in rows]
    for c in consts:
        specs.append(pl.BlockSpec(c.shape, lambda i: (0, 0)))
        args.append(c)
    nr, nc, no = len(rows), len(consts), len(outs)
    flipped = [len(o) == 3 for o in outs]
    out_shape = [jax.ShapeDtypeStruct((o[0], s_dim) if t else (s_dim, o[0]), o[1]) for o, t in zip(outs, flipped)]
    out_specs = [pl.BlockSpec((o[0], tm), lambda i: (0, i)) if t else pl.BlockSpec((tm, o[0]), lambda i: (i, 0))
                 for o, t in zip(outs, flipped)]
    out_shape += [jax.ShapeDtypeStruct(sh, F32) for sh in accs]
    out_specs += [pl.BlockSpec(sh, lambda i: (0, 0)) for sh in accs]

    def body(*refs):
        r = [x[...].astype(F32) if x.dtype == BF16 else x[...] for x in refs[:nr]]
        c = [x[...] for x in refs[nr:nr + nc]]
        o_refs = refs[nr + nc:nr + nc + no]
        a_refs = refs[nr + nc + no:]
        o_vals, a_vals = fn(r, c)
        for ref, v, t in zip(o_refs, o_vals, flipped, strict=True):
            ref[...] = (jnp.transpose(v.astype(F32)) if t else v).astype(ref.dtype)
        if a_refs:
            @pl.when(pl.program_id(0) == 0)
            def _():
                for ref in a_refs:
                    ref[...] = jnp.zeros_like(ref)

            for ref, v in zip(a_refs, a_vals, strict=True):
                ref[...] += v

    res = pl.pallas_call(
        body,
        out_shape=out_shape,
        grid=(s_dim // tm,),
        in_specs=specs,
        out_specs=out_specs,
        compiler_params=pltpu.CompilerParams(dimension_semantics=("arbitrary" if accs else "parallel",)),
        name=name,
    )(*args)
    return res


def _colsum(v):
    return jnp.sum(v, axis=0, keepdims=True)


def _rowsum(v):
    return jnp.sum(v, axis=1, keepdims=True)


def _rowmean(v):
    return jnp.mean(v, axis=1, keepdims=True)


def _silu_grad(x):
    s = _sig(x)
    return s * (1.0 + x * (1.0 - s))


def _conv_taps(x, w, width=4):
    row = lax.broadcasted_iota(jnp.int32, x.shape, 0)
    c = x * w[width - 1:width, :]
    for s in range(1, width):
        c = c + jnp.where(row >= s, pltpu.roll(x, s, 0), 0.0) * w[width - 1 - s:width - s, :]
    return c


def _conv_fwd(proj_a, conv_w):
    s_dim = proj_a.shape[0]
    n_blk = 3 * N_HEADS

    def body(x_ref, w_ref, o_ref):
        j = pl.program_id(0)
        c = _conv_taps(x_ref[...], w_ref[...])
        y = c * _sig(c)
        r = lax.rsqrt(_rowsum(y * y) + EPS_RMS)
        fac = jnp.where(j < N_HEADS, r * (HEAD ** -0.5), jnp.where(j < 2 * N_HEADS, r, 1.0))
        o_ref[...] = y * fac

    return pl.pallas_call(
        body,
        out_shape=jax.ShapeDtypeStruct((s_dim, n_blk * HEAD), F32),
        grid=(n_blk,),
        in_specs=[pl.BlockSpec((s_dim, HEAD), lambda j: (0, j)), pl.BlockSpec((4, HEAD), lambda j: (0, j))],
        out_specs=pl.BlockSpec((s_dim, HEAD), lambda j: (0, j)),
        compiler_params=pltpu.CompilerParams(dimension_semantics=("parallel",)),
        name="conv_fwd",
    )(proj_a, conv_w)


def _conv_bwd(proj_a, conv_w, dq, dk, dv):
    s_dim = proj_a.shape[0]
    n_blk = 3 * N_HEADS

    def body(x_ref, w_ref, dq_ref, dk_ref, dv_ref, dx_ref, dw_ref):
        j = pl.program_id(0)
        x = x_ref[...]
        w = w_ref[...]
        do = jnp.where(j < N_HEADS, dq_ref[...], jnp.where(j < 2 * N_HEADS, dk_ref[...], dv_ref[...]))
        c = _conv_taps(x, w)
        sg = _sig(c)
        y = c * sg
        r = lax.rsqrt(_rowsum(y * y) + EPS_RMS)
        sc = jnp.where(j < N_HEADS, HEAD ** -0.5, 1.0)
        dy_n = sc * (r * do - y * (r * r * r) * _rowsum(do * y))
        dy = jnp.where(j < 2 * N_HEADS, dy_n, do)
        dc = dy * (sg * (1.0 + c * (1.0 - sg)))
        row = lax.broadcasted_iota(jnp.int32, x.shape, 0)
        dx = dc * w[3:4, :]
        dw_ref[3:4, :] = _colsum(dc * x)
        for s in range(1, 4):
            dx = dx + jnp.where(row < s_dim - s, pltpu.roll(dc, s_dim - s, 0), 0.0) * w[3 - s:4 - s, :]
            xs = jnp.where(row >= s, pltpu.roll(x, s, 0), 0.0)
            dw_ref[3 - s:4 - s, :] = _colsum(dc * xs)
        dx_ref[...] = dx.astype(dx_ref.dtype)

    hd = N_HEADS - 1
    return pl.pallas_call(
        body,
        out_shape=[jax.ShapeDtypeStruct((s_dim, n_blk * HEAD), BF16), jax.ShapeDtypeStruct((4, n_blk * HEAD), F32)],
        grid=(n_blk,),
        in_specs=[
            pl.BlockSpec((s_dim, HEAD), lambda j: (0, j)),
            pl.BlockSpec((4, HEAD), lambda j: (0, j)),
            pl.BlockSpec((s_dim, HEAD), lambda j: (0, jnp.minimum(j, hd))),
            pl.BlockSpec((s_dim, HEAD), lambda j: (0, jnp.clip(j - N_HEADS, 0, hd))),
            pl.BlockSpec((s_dim, HEAD), lambda j: (0, jnp.clip(j - 2 * N_HEADS, 0, hd))),
        ],
        out_specs=[pl.BlockSpec((s_dim, HEAD), lambda j: (0, j)), pl.BlockSpec((4, HEAD), lambda j: (0, j))],
        compiler_params=pltpu.CompilerParams(dimension_semantics=("parallel",)),
        name="conv_bwd",
    )(proj_a, conv_w, dq, dk, dv)


def _chunk_tri(n):
    r = np.arange(n)
    m = ((r[:, None] // CHUNK) == (r[None, :] // CHUNK)) & (r[:, None] >= r[None, :])
    m = m.astype(np.float32)
    return jnp.asarray(m), jnp.asarray(m.T)


def _softplus(z):
    return jnp.maximum(z, 0.0) + jnp.log(1.0 + jnp.exp(-jnp.abs(z)))


def _gates_fwd(proj_b, alog, dtb):
    tm = min(GROUP, proj_b.shape[0])
    tri, _ = _chunk_tri(tm)

    def fn(r, c):
        b = r[0]
        a = pltpu.roll(b, LANES - N_HEADS, 1)
        alog_, dtb_, tri_ = c
        g = -jnp.exp(alog_) * _softplus(a + dtb_)
        return [_sig(b), _dot32(tri_, g)], []

    return _rowwise(fn, [(proj_b, WB_BA // LANES, LANES)], [alog, dtb, tri],
                    [(LANES, F32), (LANES, F32)], tm=tm, name="gates_fwd")


def _gates_bwd(proj_b, alog, dtb, gc, d_beta, d_gc, d_egl_rows):
    tm = min(GROUP, proj_b.shape[0])
    _, tri_t = _chunk_tri(tm)

    def fn(r, c):
        b, gc_, d_beta_, d_gc_, d_egl_ = r
        a = pltpu.roll(b, LANES - N_HEADS, 1)
        alog_, dtb_, tri_t_ = c
        z = a + dtb_
        ea = jnp.exp(alog_)
        g = -ea * _softplus(z)
        dg = _dot32(tri_t_, d_gc_ + d_egl_ * jnp.exp(gc_))
        d_a = dg * (-ea) * _sig(z)
        beta = _sig(b)
        d_ba = d_beta_ * beta * (1.0 - beta) + pltpu.roll(d_a, N_HEADS, 1)
        return [d_ba], [_colsum(dg * g), _colsum(d_a)]

    return _rowwise(fn, [(proj_b, WB_BA // LANES, LANES), gc, d_beta, d_gc, d_egl_rows],
                    [alog, dtb, tri_t], [(LANES, BF16)], accs=[(1, LANES), (1, LANES)], tm=tm,
                    name="gates_bwd")


def _group_masks(n):
    r = lax.broadcasted_iota(jnp.int32, (n, n), 0)
    c = lax.broadcasted_iota(jnp.int32, (n, n), 1)
    same = (r // CHUNK) == (c // CHUNK)
    below, s = [], 2
    while s < CHUNK:
        below.append(jnp.logical_and((r // (2 * s)) == (c // (2 * s)),
                                     jnp.logical_and((r // s) % 2 == 1, (c // s) % 2 == 0)))
        s *= 2
    return dict(same=same, tril=jnp.logical_and(same, r >= c), strict=jnp.logical_and(same, r > c),
                last=c == (r // CHUNK) * CHUNK + (CHUNK - 1), eye=r == c, pair=(r // 2) == (c // 2), below=below)


def _inv_unit_lower(l_mats, mk):
    eye_f = mk["eye"].astype(F32)
    ts = [eye_f - jnp.where(mk["pair"], l_mat, 0.0) for l_mat in l_mats]
    for below in mk["below"]:
        halves = [_split_bf16(t) for t in ts]
        mids = [_dot3(h, jnp.where(below, l_mat, 0.0)) for h, l_mat in zip(halves, l_mats)]
        ts = [t - _dot3(m, h) for t, m, h in zip(ts, mids, halves)]
    return ts


def _unfold_blocks(folded, mask):
    n = folded.shape[0]
    return jnp.where(mask, jnp.concatenate([folded] * (n // CHUNK), axis=1), 0.0)


def _head_cols(beta, gc, gc_t, h):
    lane = lax.broadcasted_iota(jnp.int32, beta.shape, 1)
    sub = lax.broadcasted_iota(jnp.int32, gc_t.shape, 0)
    bcol = _rowsum(jnp.where(lane == h, beta, 0.0))
    gcol = _rowsum(jnp.where(lane == h, gc, 0.0))
    grow = _colsum(jnp.where(sub == h, gc_t, 0.0))
    return bcol, gcol, grow


def _prep_common(q, k, bcol, gcol, grow, mk, t_folded=None):
    n = q.shape[0]
    tril = mk["tril"]
    decay = jnp.where(tril, jnp.exp(jnp.where(tril, gcol - grow, 0.0)), 0.0)
    glast = _rowsum(jnp.where(mk["last"], jnp.broadcast_to(grow, (n, n)), 0.0))
    e = jnp.exp(gcol)
    ekt = jnp.exp(glast - gcol)
    kb = k * bcol
    kk = _dot(kb, k, NT)
    qk = _dot(q, k, NT)
    p = dict(decay=decay, e=e, ekt=ekt, kb=kb, kk=kk, qk=qk)
    if t_folded is not None:
        p["t"] = _unfold_blocks(t_folded, mk["same"])
    return p


GROUPS_PER_STEP = 4
SCAN_CHUNKS_PER_STEP = 4


def _fold_blocks(m):
    n = m.shape[0]
    out = m[:, 0:CHUNK]
    for b in range(1, n // CHUNK):
        out = out + m[:, b * CHUNK:(b + 1) * CHUNK]
    return out


def _gdr_prep_fwd(qkvn, beta, gc, gc_t):
    s_dim = qkvn.shape[0]
    tg = min(GROUP, s_dim)
    n_sub = min(GROUPS_PER_STEP, s_dim // tg)
    tb = tg * n_sub

    def body(q_ref, k_ref, v_ref, b_ref, g_ref, gt_ref, u_ref, w_ref, qd_ref, kt_ref, a_ref, t_ref):
        h = pl.program_id(0)
        mk = _group_masks(tg)
        parts = []
        for s in range(n_sub):
            rows = slice(s * tg, (s + 1) * tg)
            q, k, v = q_ref[rows, :], k_ref[rows, :], v_ref[rows, :]
            bcol, gcol, grow = _head_cols(b_ref[rows, :], g_ref[rows, :], gt_ref[:, rows], h)
            p = _prep_common(q, k, bcol, gcol, grow, mk)
            qd_ref[rows, :] = q * p["e"]
            kt_ref[rows, :] = k * p["ekt"]
            a_ref[rows, :] = _fold_blocks(jnp.where(mk["tril"], p["qk"] * p["decay"], 0.0))
            parts.append((rows, v * bcol, p["kb"] * p["e"], jnp.where(mk["strict"], p["kk"] * p["decay"], 0.0)))
        t_mats = _inv_unit_lower([part[3] for part in parts], mk)
        for (rows, vb, kbe, _), t_mat in zip(parts, t_mats):
            u_ref[rows, :] = _dot(t_mat, vb)
            w_ref[rows, :] = _dot(t_mat, kbe)
            t_ref[rows, :] = _fold_blocks(t_mat)

    row = lambda off: pl.BlockSpec((tb, HEAD), functools.partial(lambda h, m, off: (m, h + off), off=off))
    full = pl.BlockSpec((tb, LANES), lambda h, m: (m, 0))
    o_spec = pl.BlockSpec((tb, HEAD), lambda h, m: (m, h))
    a_spec = pl.BlockSpec((None, tb, CHUNK), lambda h, m: (h, m, 0))
    wide = jax.ShapeDtypeStruct((s_dim, N_HEADS * HEAD), F32)
    folded = jax.ShapeDtypeStruct((N_HEADS, s_dim, CHUNK), F32)
    return pl.pallas_call(
        body,
        out_shape=[wide, wide, wide, wide, folded, folded],
        grid=(N_HEADS, s_dim // tb),
        in_specs=[row(0), row(N_HEADS), row(2 * N_HEADS), full, full, pl.BlockSpec((8, tb), lambda h, m: (0, m))],
        out_specs=[o_spec, o_spec, o_spec, o_spec, a_spec, a_spec],
        compiler_params=pltpu.CompilerParams(dimension_semantics=("parallel", "parallel")),
        name="gdr_prep_fwd",
    )(qkvn, qkvn, qkvn, beta, gc, gc_t)


def _gdr_prep_bwd(qkvn, beta, gc, gc_t, t_fold, u, w, du, dw, dqd, dkt, d_a):
    s_dim = qkvn.shape[0]
    tg = min(GROUP, s_dim)
    n_sub = min(GROUPS_PER_STEP, s_dim // tg)
    tb = tg * n_sub

    def body(q_ref, k_ref, v_ref, b_ref, g_ref, gt_ref, t_ref, u_ref, w_ref, du_ref, dw_ref, dqd_ref, dkt_ref,
             da_ref, dq_ref, dk_ref, dv_ref, db_ref, dg_ref):
        h = pl.program_id(1)

        @pl.when(h == 0)
        def _():
            db_ref[...] = jnp.zeros_like(db_ref)
            dg_ref[...] = jnp.zeros_like(dg_ref)

        mk = _group_masks(tg)
        lane = lax.broadcasted_iota(jnp.int32, (tg, LANES), 1)
        for s in range(n_sub):
            rows = slice(s * tg, (s + 1) * tg)
            q, k, v = q_ref[rows, :], k_ref[rows, :], v_ref[rows, :]
            bcol, gcol, grow = _head_cols(b_ref[rows, :], g_ref[rows, :], gt_ref[:, rows], h)
            p = _prep_common(q, k, bcol, gcol, grow, mk, t_ref[rows, :])
            t_mat, decay, e, ekt, kb = p["t"], p["decay"], p["e"], p["ekt"], p["kb"]
            du_, dw_, dqd_, dkt_ = du_ref[rows, :], dw_ref[rows, :], dqd_ref[rows, :], dkt_ref[rows, :]
            dvb = _dot(t_mat, du_, TN)
            dkbe = _dot(t_mat, dw_, TN)
            d_l = -(_dot(dvb, u_ref[rows, :], NT) + _dot(dkbe, w_ref[rows, :], NT))
            m1 = jnp.where(mk["strict"], d_l, 0.0)
            m2 = _unfold_blocks(da_ref[rows, :], mk["tril"])
            d_kk = m1 * decay
            d_qk = m2 * decay
            d_decay = m1 * p["kk"] + m2 * p["qk"]
            dkb = _dot(d_kk, k) + dkbe * e
            dk = _dot(d_kk, kb, TN) + _dot(d_qk, q, TN) + dkt_ * ekt + dkb * bcol
            dq = _dot(d_qk, k) + dqd_ * e
            d_beta = _rowsum(dkb * k) + _rowsum(dvb * v)
            d_e = _rowsum(dkbe * kb) + _rowsum(dqd_ * q)
            d_ekt = _rowsum(dkt_ * k) * ekt
            d_diff = d_decay * decay
            d_grow = -_colsum(d_diff) + _colsum(jnp.where(mk["last"], jnp.broadcast_to(d_ekt, (tg, tg)), 0.0))
            d_gcol = d_e * e - d_ekt + _rowsum(d_diff)
            d_gcol = d_gcol + _rowsum(jnp.where(mk["eye"], jnp.broadcast_to(d_grow, (tg, tg)), 0.0))
            dq_ref[rows, :] = dq
            dk_ref[rows, :] = dk
            dv_ref[rows, :] = dvb * bcol
            db_ref[rows, :] = jnp.where(lane == h, d_beta, db_ref[rows, :])
            dg_ref[rows, :] = jnp.where(lane == h, d_gcol, dg_ref[rows, :])

    row = lambda off: pl.BlockSpec((tb, HEAD), functools.partial(lambda m, h, off: (m, h + off), off=off))
    full = pl.BlockSpec((tb, LANES), lambda m, h: (m, 0))
    o_spec = pl.BlockSpec((tb, HEAD), lambda m, h: (m, h))
    a_spec = pl.BlockSpec((None, tb, CHUNK), lambda m, h: (h, m, 0))
    wide = jax.ShapeDtypeStruct((s_dim, N_HEADS * HEAD), F32)
    lanes = jax.ShapeDtypeStruct((s_dim, LANES), F32)
    return pl.pallas_call(
        body,
        out_shape=[wide, wide, wide, lanes, lanes],
        grid=(s_dim // tb, N_HEADS),
        in_specs=[row(0), row(N_HEADS), row(2 * N_HEADS), full, full, pl.BlockSpec((8, tb), lambda m, h: (0, m)),
                  a_spec, o_spec, o_spec, o_spec, o_spec, o_spec, o_spec, a_spec],
        out_specs=[o_spec, o_spec, o_spec, full, full],
        compiler_params=pltpu.CompilerParams(dimension_semantics=("parallel", "arbitrary")),
        name="gdr_prep_bwd",
    )(qkvn, qkvn, qkvn, beta, gc, gc_t, t_fold, u, w, du, dw, dqd, dkt, d_a)


def _gdr_scan_fwd(u, w, qd, kt, a_mat, gc):
    s_dim = u.shape[0]
    n_chunks = s_dim // CHUNK
    per = min(SCAN_CHUNKS_PER_STEP, n_chunks)
    tb = per * CHUNK

    def body(u_ref, w_ref, qd_ref, kt_ref, a_ref, g_ref, o_ref, st_ref, state):
        @pl.when(pl.program_id(0) == 0)
        def _():
            state[...] = jnp.zeros_like(state)

        heads = range(N_HEADS)
        cols = [slice(h * HEAD, (h + 1) * HEAD) for h in heads]
        for i in range(per):
            rows = slice(i * CHUNK, (i + 1) * CHUNK)
            egl = jnp.exp(g_ref[(i + 1) * CHUNK - 1:(i + 1) * CHUNK, :])
            s_b = [state[h].astype(BF16) for h in heads]
            for h in heads:
                st_ref[i, h] = state[h]
            ws = [_dot(w_ref[rows, cs], s) for cs, s in zip(cols, s_b)]
            qs = [_dot(qd_ref[rows, cs], s) for cs, s in zip(cols, s_b)]
            vns = [(u_ref[rows, cs] - ws_h).astype(BF16) for cs, ws_h in zip(cols, ws)]
            avs = [_dot(a_ref[h, rows, :], vn) for h, vn in zip(heads, vns)]
            kvs = [_dot(kt_ref[rows, cs], vn, TN) for cs, vn in zip(cols, vns)]
            for h, cs in zip(heads, cols):
                o_ref[rows, cs] = qs[h] + avs[h]
                state[h] = state[h] * egl[:, h:h + 1] + kvs[h]

    wide = pl.BlockSpec((tb, N_HEADS * HEAD), lambda n: (n, 0))
    return pl.pallas_call(
        body,
        out_shape=[jax.ShapeDtypeStruct((s_dim, N_HEADS * HEAD), F32),
                   jax.ShapeDtypeStruct((n_chunks, N_HEADS, HEAD, HEAD), F32)],
        grid=(n_chunks // per,),
        in_specs=[wide, wide, wide, wide, pl.BlockSpec((N_HEADS, tb, CHUNK), lambda n: (0, n, 0)),
                  pl.BlockSpec((tb, LANES), lambda n: (n, 0))],
        out_specs=[wide, pl.BlockSpec((per, N_HEADS, HEAD, HEAD), lambda n: (n, 0, 0, 0))],
        scratch_shapes=[pltpu.VMEM((N_HEADS, HEAD, HEAD), F32)],
        compiler_params=pltpu.CompilerParams(dimension_semantics=("arbitrary",)),
        name="gdr_scan_fwd",
    )(u, w, qd, kt, a_mat, gc)


def _gdr_scan_bwd(u, w, qd, kt, a_mat, gc, states, d_o):
    s_dim = u.shape[0]
    n_chunks = s_dim // CHUNK
    per = min(SCAN_CHUNKS_PER_STEP, n_chunks)
    tb = per * CHUNK
    last = n_chunks // per - 1

    def body(u_ref, w_ref, qd_ref, kt_ref, a_ref, g_ref, st_ref, do_ref,
             du_ref, dw_ref, dqd_ref, dkt_ref, da_ref, de_ref, d_state):
        @pl.when(pl.program_id(0) == 0)
        def _():
            d_state[...] = jnp.zeros_like(d_state)

        heads = range(N_HEADS)
        cols = [slice(h * HEAD, (h + 1) * HEAD) for h in heads]
        for i in reversed(range(per)):
            rows = slice(i * CHUNK, (i + 1) * CHUNK)
            egl = jnp.exp(g_ref[(i + 1) * CHUNK - 1:(i + 1) * CHUNK, :])
            s_b = [st_ref[i, h].astype(BF16) for h in heads]
            ds_b = [d_state[h].astype(BF16) for h in heads]
            dos = [do_ref[rows, cs].astype(BF16) for cs in cols]
            w_b = [w_ref[rows, cs].astype(BF16) for cs in cols]
            ws = [_dot(w_h, s) for w_h, s in zip(w_b, s_b)]
            ados = [_dot(a_ref[h, rows, :], do, TN) for h, do in zip(heads, dos)]
            kds = [_dot(kt_ref[rows, cs], ds) for cs, ds in zip(cols, ds_b)]
            dqds = [_dot(do, s, NT) for do, s in zip(dos, s_b)]
            qdos = [_dot(qd_ref[rows, cs], do, TN) for cs, do in zip(cols, dos)]
            vns = [(u_ref[rows, cs] - ws_h).astype(BF16) for cs, ws_h in zip(cols, ws)]
            dvns = [a + k_ for a, k_ in zip(ados, kds)]
            dvn_b = [d.astype(BF16) for d in dvns]
            das = [_dot(do, vn, NT) for do, vn in zip(dos, vns)]
            dkts = [_dot(vn, ds, NT) for vn, ds in zip(vns, ds_b)]
            dws = [_dot(d, s, NT) for d, s in zip(dvn_b, s_b)]
            wds = [_dot(w_h, d, TN) for w_h, d in zip(w_b, dvn_b)]
            for h, cs in zip(heads, cols):
                ds_n = d_state[h]
                de = jnp.sum(_rowsum(ds_n * st_ref[i, h]), axis=0, keepdims=True)
                de_ref[i, h:h + 1, :] = jnp.broadcast_to(de, (1, LANES))
                dqd_ref[rows, cs] = dqds[h]
                da_ref[h, rows, :] = das[h]
                dkt_ref[rows, cs] = dkts[h]
                du_ref[rows, cs] = dvns[h]
                dw_ref[rows, cs] = -dws[h]
                d_state[h] = ds_n * egl[:, h:h + 1] + qdos[h] - wds[h]

    wide = pl.BlockSpec((tb, N_HEADS * HEAD), lambda n: (last - n, 0))
    a_spec = pl.BlockSpec((N_HEADS, tb, CHUNK), lambda n: (0, last - n, 0))
    wide_shape = jax.ShapeDtypeStruct((s_dim, N_HEADS * HEAD), F32)
    return pl.pallas_call(
        body,
        out_shape=[wide_shape, wide_shape, wide_shape, wide_shape,
                   jax.ShapeDtypeStruct((N_HEADS, s_dim, CHUNK), F32),
                   jax.ShapeDtypeStruct((n_chunks, N_HEADS, LANES), F32)],
        grid=(n_chunks // per,),
        in_specs=[wide, wide, wide, wide, a_spec, pl.BlockSpec((tb, LANES), lambda n: (last - n, 0)),
                  pl.BlockSpec((per, N_HEADS, HEAD, HEAD), lambda n: (last - n, 0, 0, 0)), wide],
        out_specs=[wide, wide, wide, wide, a_spec, pl.BlockSpec((per, N_HEADS, LANES), lambda n: (last - n, 0, 0))],
        scratch_shapes=[pltpu.VMEM((N_HEADS, HEAD, HEAD), F32)],
        compiler_params=pltpu.CompilerParams(dimension_semantics=("arbitrary",)),
        name="gdr_scan_bwd",
    )(u, w, qd, kt, a_mat, gc, states, d_o)


FUSED_ROWS = 512


def _gdr_out_fwd(o_dn, proj_a, dn_w, w_br):
    def fn(r, c):
        o, z = r
        w_, w_br_ = c
        outs = []
        for h in range(N_HEADS):
            cs = slice(h * HEAD, (h + 1) * HEAD)
            oh, zh = o[:, cs], z[:, cs]
            rr = lax.rsqrt(_rowmean(oh * oh) + EPS_RMS)
            outs.append(oh * rr * w_ * (zh * _sig(zh)))
        og = jnp.concatenate(outs, axis=1).astype(BF16)
        return [og, _dot(og, w_br_)], []

    return _rowwise(fn, [o_dn, (proj_a, 3, D_MODEL)], [dn_w, w_br], [(D_MODEL, BF16), (D_MODEL, BF16)],
                    tm=FUSED_ROWS, name="gdr_out_fwd")


def _gdr_out_bwd(o_dn, proj_a, d_y_dn, dn_w, w_br):
    def fn(r, c):
        o, z, dy = r
        w_, w_br_ = c
        dg = _dot(dy, w_br_, NT)
        d_o, d_z = [], []
        d_w = jnp.zeros((1, HEAD), F32)
        for h in range(N_HEADS):
            cs = slice(h * HEAD, (h + 1) * HEAD)
            oh, zh, dgh = o[:, cs], z[:, cs], dg[:, cs]
            rr = lax.rsqrt(_rowmean(oh * oh) + EPS_RMS)
            sz = zh * _sig(zh)
            d_n = dgh * sz
            d_z.append(dgh * (oh * rr * w_) * _silu_grad(zh))
            d_w = d_w + _colsum(d_n * oh * rr)
            gw = d_n * w_
            d_o.append(rr * gw - oh * (rr * rr * rr) * _rowmean(gw * oh))
        return [jnp.concatenate(d_o, axis=1), jnp.concatenate(d_z, axis=1)], [d_w]

    return _rowwise(fn, [o_dn, (proj_a, 3, D_MODEL), d_y_dn], [dn_w, w_br], [(D_MODEL, F32), (D_MODEL, BF16)],
                    accs=[(1, HEAD)], tm=FUSED_ROWS, name="gdr_out_bwd")


def _rms_fwd(x, w):
    r = lax.rsqrt(_rowmean(x * x) + EPS_RMS)
    return x * r * w


def _rms_bwd(x, w, dy):
    r = lax.rsqrt(_rowmean(x * x) + EPS_RMS)
    gw = dy * w
    return r * gw - x * (r * r * r) * _rowmean(gw * x), _colsum(dy * x * r)


def _rope_consts():
    inv = ROPE_BASE ** (-np.arange(0, ROPE, 2, dtype=np.float32) / ROPE)
    t = np.zeros((4, LANES), np.float32)
    t[0, :32] = inv
    t[0, 32:64] = inv
    t[1, :64] = 1.0
    t[2, 32:64] = 1.0
    t[3, :32] = -1.0
    return jnp.asarray(t)


def _rope_tables(pos, consts, width):
    ang = pos * consts[0:1, :]
    cosv, sinv = jnp.cos(ang), jnp.sin(ang)
    reps = width // LANES
    tile = (lambda t: jnp.concatenate([t] * reps, axis=1)) if reps > 1 else (lambda t: t)
    return tile(cosv * consts[1:2, :]), tile(sinv * consts[2:3, :]), tile(sinv * consts[3:4, :])


def _rope_apply(t, tabs):
    cos_t, sin_a, sin_b = tabs
    width = t.shape[1]
    return t * cos_t + pltpu.roll(t, 32, 1) * sin_a + pltpu.roll(t, width - 32, 1) * sin_b


def _rope_transpose(d, tabs):
    cos_t, sin_a, sin_b = tabs
    width = d.shape[1]
    return d * cos_t + pltpu.roll(d * sin_a, width - 32, 1) + pltpu.roll(d * sin_b, 32, 1)


QK_HEAD = 2 * HEAD


def _interleave_heads(a, b):
    parts = []
    for h in range(N_HEADS):
        parts.append(a[:, h * HEAD:(h + 1) * HEAD])
        parts.append(b if b.shape[1] == LANES else b[:, h * LANES:(h + 1) * LANES])
    return jnp.concatenate(parts, axis=1)


def _mla_rows(proj_b):
    return [(proj_b, WB_CQ // Q_LORA, Q_LORA), (proj_b, WB_CKV // KV_LORA, KV_LORA), (proj_b, WB_KR // LANES, LANES)]


def _mla_prep_fwd(proj_b, pos, qn_w, kvn_w, uq, uk, uv):
    def fn(r, c):
        cq, ckv, kr, pos_ = r
        qn_w_, kvn_w_, uq_, uk_, uv_, rope = c
        c_q = _rms_fwd(cq, qn_w_).astype(BF16)
        c_kv = _rms_fwd(ckv, kvn_w_).astype(BF16)
        qf = _dot(c_q, uq_)
        qr = _rope_apply(qf[:, D_MODEL:], _rope_tables(pos_, rope, D_MODEL))
        kr = _rope_apply(kr, _rope_tables(pos_, rope, LANES))
        kc = _interleave_heads(_dot(c_kv, uk_), kr)
        v = _dot(c_kv, uv_)
        return [c_q, c_kv, _interleave_heads(qf[:, :D_MODEL], qr) * SCALE, kc, v, kc, v], []

    wide2 = N_HEADS * QK_HEAD
    return _rowwise(fn, _mla_rows(proj_b) + [pos], [qn_w, kvn_w, uq, uk, uv, _rope_consts()],
                    [(Q_LORA, BF16), (KV_LORA, BF16), (wide2, BF16), (wide2, BF16), (D_MODEL, BF16),
                     (wide2, BF16, "T"), (D_MODEL, BF16, "T")], tm=FUSED_ROWS, name="mla_prep_fwd")


def _mla_prep_bwd(proj_b, pos, d_qc, d_kc, d_v, qn_w, kvn_w, uq, uk, uv):
    def fn(r, c):
        cq, ckv, _, pos_, dq, dk, dv = r
        qn_w_, kvn_w_, uq_, uk_, uv_, rope = c
        even = lambda t: jnp.concatenate([t[:, (2 * h) * LANES:(2 * h + 1) * LANES] for h in range(N_HEADS)], axis=1)
        odd = lambda t: jnp.concatenate([t[:, (2 * h + 1) * LANES:(2 * h + 2) * LANES] for h in range(N_HEADS)], axis=1)
        d_qr_raw = _rope_transpose(odd(dq), _rope_tables(pos_, rope, D_MODEL)) * SCALE
        d_qf = jnp.concatenate([even(dq) * SCALE, d_qr_raw], axis=1).astype(BF16)
        d_kn = even(dk).astype(BF16)
        dkr = dk[:, LANES:2 * LANES]
        for h in range(1, N_HEADS):
            dkr = dkr + dk[:, (2 * h + 1) * LANES:(2 * h + 2) * LANES]
        d_cq, d_qnw = _rms_bwd(cq, qn_w_, _dot(d_qf, uq_, NT))
        d_ckv, d_kvnw = _rms_bwd(ckv, kvn_w_, _dot(d_kn, uk_, NT) + _dot(dv, uv_, NT))
        return [d_qf, d_kn, d_cq, d_ckv, _rope_transpose(dkr, _rope_tables(pos_, rope, LANES))], [d_qnw, d_kvnw]

    return _rowwise(fn, _mla_rows(proj_b) + [pos, d_qc, d_kc, d_v], [qn_w, kvn_w, uq, uk, uv, _rope_consts()],
                    [(2 * D_MODEL, BF16), (D_MODEL, BF16), (Q_LORA, BF16), (KV_LORA, BF16), (LANES, BF16)],
                    accs=[(1, Q_LORA), (1, KV_LORA)], tm=FUSED_ROWS, name="mla_prep_bwd")


def _causal_mask_t(st, key0, query0):
    key = lax.broadcasted_iota(jnp.int32, st.shape, 0) + key0
    query = lax.broadcasted_iota(jnp.int32, st.shape, 1) + query0
    return jnp.where(key <= query, st, NEG_BIG)


def _attn_tiles(s_dim):
    tq = min(512, s_dim)
    n_chains = 2 if s_dim >= 2 * tq else 1
    return tq, n_chains, min(512, s_dim)


def _diagonal_chains(t, tq, n_chains, tk):
    return [(c, (t + 1) * tk - 1 > c * tq) for c in range(n_chains) if t * tk < (c + 1) * tq]


def _attn_fwd(qc, kc, vt):
    s_dim = qc.shape[0]
    tq, n_chains, tk = _attn_tiles(s_dim)
    tqs = tq * n_chains

    def body(q_ref, k_ref, vt_ref, o_ref, lse_ref, m_s, l_s, acc):
        qi = pl.program_id(1)
        m_s[...] = jnp.full_like(m_s, NEG_BIG)
        l_s[...] = jnp.zeros_like(l_s)
        acc[...] = jnp.zeros_like(acc)

        def make_step(chains):
            def step(j, carry):
                ks = pl.multiple_of(j * tk, tk)
                kb, vtb = k_ref[pl.ds(ks, tk), :], vt_ref[:, pl.ds(ks, tk)]
                cols = [slice(c * tq, (c + 1) * tq) for c, _ in chains]
                sts = [_dot(kb, q_ref[cs, :], NT) for cs in cols]
                sts = [_causal_mask_t(st, j * tk, qi * tqs + c * tq) if masked else st
                       for st, (c, masked) in zip(sts, chains)]
                m_prevs = [m_s[:, cs] for cs in cols]
                m_news = [jnp.maximum(mp, jnp.max(st, axis=0, keepdims=True)) for mp, st in zip(m_prevs, sts)]
                alphas = [jnp.exp(mp - mn) for mp, mn in zip(m_prevs, m_news)]
                pts = [jnp.exp(st - mn) for st, mn in zip(sts, m_news)]
                pvs = [_dot(vtb, pt) for pt in pts]
                for cs, mn, al, pt, pv in zip(cols, m_news, alphas, pts, pvs):
                    l_s[:, cs] = al * l_s[:, cs] + _colsum(pt)
                    m_s[:, cs] = mn
                    acc[:, cs] = acc[:, cs] * al + pv
                return carry
            return step

        below = qi * (tqs // tk)
        lax.fori_loop(0, below, make_step([(c, False) for c in range(n_chains)]), 0)
        for t in range(tqs // tk):
            make_step(_diagonal_chains(t, tq, n_chains, tk))(below + t, 0)
        l = l_s[...]
        o_ref[...] = jnp.transpose(acc[...] / l)
        lse_ref[...] = m_s[...] + jnp.log(l)

    return pl.pallas_call(
        body,
        out_shape=[jax.ShapeDtypeStruct((s_dim, N_HEADS * HEAD), F32), jax.ShapeDtypeStruct((N_HEADS, 1, s_dim), F32)],
        grid=(N_HEADS, s_dim // tqs),
        in_specs=[pl.BlockSpec((tqs, QK_HEAD), lambda h, qi: (qi, h)),
                  pl.BlockSpec((s_dim, QK_HEAD), lambda h, qi: (0, h)),
                  pl.BlockSpec((HEAD, s_dim), lambda h, qi: (h, 0))],
        out_specs=[pl.BlockSpec((tqs, HEAD), lambda h, qi: (qi, h)),
                   pl.BlockSpec((None, 1, tqs), lambda h, qi: (h, 0, qi))],
        scratch_shapes=[pltpu.VMEM((1, tqs), F32), pltpu.VMEM((1, tqs), F32), pltpu.VMEM((HEAD, tqs), F32)],
        compiler_params=pltpu.CompilerParams(dimension_semantics=("parallel", "parallel")),
        name="attn_fwd",
    )(qc, kc, vt)


def _attn_bwd(qc, kc, kct, v, o, d_o, lse):
    s_dim = qc.shape[0]
    tq, n_chains, tk = _attn_tiles(s_dim)
    tqs = tq * n_chains

    def body(q_ref, k_ref, kt_ref, v_ref, o_ref, do_ref, lse_ref, dq_ref, dk_ref, dv_ref, dqt_acc, dv_acc):
        qi = pl.program_id(1)

        @pl.when(qi == 0)
        def _():
            dk_ref[...] = jnp.zeros_like(dk_ref)
            dv_acc[...] = jnp.zeros_like(dv_acc)

        dqt_acc[...] = jnp.zeros_like(dqt_acc)
        do_f = do_ref[...]
        do_all = do_f.astype(BF16)
        q_all = q_ref[...]
        lse_row = lse_ref[...]
        delta_row = _dot3(jnp.ones((8, HEAD), F32), o_ref[...] * do_f, NT)[0:1, :]

        def make_step(chains):
            rows = slice(chains[0][0] * tq, (chains[-1][0] + 1) * tq)

            def step(j, carry):
                ks = pl.multiple_of(j * tk, tk)
                kb, vb, ktb = k_ref[pl.ds(ks, tk), :], v_ref[pl.ds(ks, tk), :], kt_ref[:, pl.ds(ks, tk)]
                cols = [slice(c * tq, (c + 1) * tq) for c, _ in chains]
                sts = [_dot(kb, q_all[cs, :], NT) for cs in cols]
                sts = [_causal_mask_t(st, j * tk, qi * tqs + c * tq) if masked else st
                       for st, (c, masked) in zip(sts, chains)]
                dpts = [_dot(vb, do_all[cs, :], NT) for cs in cols]
                pts = [jnp.exp(st - lse_row[:, cs]) for st, cs in zip(sts, cols)]
                dsts = [(pt * (dpt - delta_row[:, cs])).astype(BF16) for pt, dpt, cs in zip(pts, dpts, cols)]
                pts = [pt.astype(BF16) for pt in pts]
                dqs = [_dot(ktb, dst) for dst in dsts]
                for cs, dq in zip(cols, dqs):
                    dqt_acc[:, cs] += dq
                pt_all = jnp.concatenate(pts, axis=1) if len(chains) > 1 else pts[0]
                dst_all = jnp.concatenate(dsts, axis=1) if len(chains) > 1 else dsts[0]
                dk_ref[pl.ds(ks, tk), :] += _dot(dst_all, q_all[rows, :])
                dv_acc[pl.ds(ks, tk), :] += _dot(pt_all, do_all[rows, :])
                return carry
            return step

        below = qi * (tqs // tk)
        lax.fori_loop(0, below, make_step([(c, False) for c in range(n_chains)]), 0)
        for t in range(tqs // tk):
            make_step(_diagonal_chains(t, tq, n_chains, tk))(below + t, 0)
        dq_ref[...] = jnp.transpose(dqt_acc[...])

        @pl.when(qi == s_dim // tqs - 1)
        def _():
            dv_ref[...] = dv_acc[...].astype(dv_ref.dtype)

    q_spec = pl.BlockSpec((tqs, QK_HEAD), lambda h, qi: (qi, h))
    o_spec = pl.BlockSpec((tqs, HEAD), lambda h, qi: (qi, h))
    k_spec = pl.BlockSpec((s_dim, QK_HEAD), lambda h, qi: (0, h))
    v_spec = pl.BlockSpec((s_dim, HEAD), lambda h, qi: (0, h))
    wide2 = jax.ShapeDtypeStruct((s_dim, N_HEADS * QK_HEAD), F32)
    return pl.pallas_call(
        body,
        out_shape=[wide2, wide2, jax.ShapeDtypeStruct((s_dim, N_HEADS * HEAD), BF16)],
        grid=(N_HEADS, s_dim // tqs),
        in_specs=[q_spec, k_spec, pl.BlockSpec((QK_HEAD, s_dim), lambda h, qi: (h, 0)), v_spec, o_spec, o_spec,
                  pl.BlockSpec((None, 1, tqs), lambda h, qi: (h, 0, qi))],
        out_specs=[q_spec, k_spec, v_spec],
        scratch_shapes=[pltpu.VMEM((QK_HEAD, tqs), F32), pltpu.VMEM((s_dim, HEAD), F32)],
        compiler_params=pltpu.CompilerParams(dimension_semantics=("parallel", "arbitrary")),
        name="attn_bwd",
    )(qc, kc, kct, v, o, d_o, lse)


def _mix_proj_ln1(y_dn, y_mla, proj_g, x, w_o, g, b):
    s_dim = x.shape[0]
    tm = min(512, s_dim)

    def body(yd_ref, ym_ref, g_ref, x_ref, w_ref, lg_ref, lb_ref, mixed_ref, a1_ref, h1_ref, h1b_ref):
        gates = g_ref[...].astype(F32)
        mixed = (_sig(gates[:, :D_MODEL]) * yd_ref[...].astype(F32)
                 + _sig(gates[:, D_MODEL:]) * ym_ref[...].astype(F32)).astype(BF16)
        a1 = _dot(mixed, w_ref[...])
        xh, _ = _ln_stats(ALPHA * x_ref[...] + a1)
        y = xh * lg_ref[...] + lb_ref[...]
        mixed_ref[...] = mixed
        a1_ref[...] = a1
        h1_ref[...] = y
        h1b_ref[...] = y.astype(BF16)

    row = lambda width: pl.BlockSpec((tm, width), lambda i: (i, 0))
    whole = lambda a: pl.BlockSpec(a.shape, lambda i: (0, 0))
    sds = lambda dt: jax.ShapeDtypeStruct((s_dim, D_MODEL), dt)
    return pl.pallas_call(
        body,
        out_shape=[sds(BF16), sds(F32), sds(F32), sds(BF16)],
        grid=(s_dim // tm,),
        in_specs=[row(D_MODEL), row(D_MODEL), row(2 * D_MODEL), row(D_MODEL), whole(w_o), whole(g), whole(b)],
        out_specs=[row(D_MODEL)] * 4,
        compiler_params=pltpu.CompilerParams(dimension_semantics=("parallel",)),
        name="mix_proj_ln1",
    )(y_dn, y_mla, proj_g, x, w_o, g, b)


def _ln1_mix_bwd(x, a1, d_h1, d_pg, y_dn, y_mla, proj_g, g, w_o, w_pg):
    def fn(r, c):
        x_, a1_, dy, dpg, yd, ym, gates = r
        g_, w_o_, w_pg_ = c
        dy = dy + _dot(dpg, w_pg_, NT)
        xh, rr = _ln_stats(ALPHA * x_ + a1_)
        dz = _ln_bwd(dy, xh, rr, g_)
        dz_b = dz.astype(BF16)
        dm = _dot(dz_b, w_o_, NT)
        sd, sm = _sig(gates[:, :D_MODEL]), _sig(gates[:, D_MODEL:])
        d_g = jnp.concatenate([dm * yd * sd * (1.0 - sd), dm * ym * sm * (1.0 - sm)], axis=1)
        return [dz_b, ALPHA * dz, d_g, dm * sd, dm * sm], [_colsum(dy * xh), _colsum(dy)]

    return _rowwise(fn, [x, a1, d_h1, d_pg, y_dn, y_mla, proj_g], [g, w_o, w_pg],
                    [(D_MODEL, BF16), (D_MODEL, F32), (2 * D_MODEL, BF16), (D_MODEL, BF16), (D_MODEL, BF16)],
                    accs=[(1, D_MODEL), (1, D_MODEL)], tm=FUSED_ROWS, name="ln1_mix_bwd")


def _ln_stats(z):
    mu = _rowmean(z)
    zc = z - mu
    r = lax.rsqrt(_rowmean(zc * zc) + EPS_LN)
    return zc * r, r


def _ln_bwd(dy, xh, r, g):
    dxh = dy * g
    return r * (dxh - _rowmean(dxh) - xh * _rowmean(dxh * xh))


def _ffn_in_act(h1b, w_t):
    s_dim, k_dim = h1b.shape
    hidden = w_t.shape[0] // 2
    tm, tn = min(512, s_dim), _pick_wide(hidden)
    nt = hidden // tn

    def body(a_ref, bg_ref, bu_ref, gt_ref, up_ref, act_ref):
        a = a_ref[...]
        gt, up = _dot(a, bg_ref[...], NT), _dot(a, bu_ref[...], NT)
        gt_ref[...] = gt.astype(BF16)
        up_ref[...] = up.astype(BF16)
        act_ref[...] = (gt * _sig(gt) * up).astype(BF16)

    o_spec = pl.BlockSpec((tm, tn), lambda j, i: (i, j))
    sds = jax.ShapeDtypeStruct((s_dim, hidden), BF16)
    return pl.pallas_call(
        body,
        out_shape=[sds, sds, sds],
        grid=(nt, s_dim // tm),
        in_specs=[pl.BlockSpec((tm, k_dim), lambda j, i: (i, 0)), pl.BlockSpec((tn, k_dim), lambda j, i: (j, 0)),
                  pl.BlockSpec((tn, k_dim), lambda j, i: (j + nt, 0))],
        out_specs=[o_spec, o_spec, o_spec],
        compiler_params=pltpu.CompilerParams(dimension_semantics=("parallel", "parallel")),
        name="ffn_in_act",
    )(h1b, w_t, w_t)


def _act_bwd(gt, up, d_act):
    def fn(r, c):
        gt_, up_, da = r
        return [jnp.concatenate([da * up_ * _silu_grad(gt_), da * gt_ * _sig(gt_)], axis=1)], []

    return _rowwise(fn, [gt, up, d_act], [], [(2 * FFN_HIDDEN, BF16)], name="act_bwd")[0]


def _tail(h1, ffn, p, tgt, g, b, w_pg, w_ple_t):
    def fn(r, c):
        h1_, ffn_, p_, t_ = r
        pg_ = _dot(h1_, c[2])
        pp_ = _dot(p_, c[3], NT)
        sp = _sig(pg_)
        xh, rr = _ln_stats(ALPHA * h1_ + ffn_ + sp * pp_)
        y = xh * c[0] + c[1]
        err = y - t_
        dy = err * (1.0 / D_MODEL)
        dz = _ln_bwd(dy, xh, rr, c[0])
        loss = jnp.sum(0.5 * _rowmean(err * err), axis=0, keepdims=True)
        return ([dz, dz * pp_ * sp * (1.0 - sp), dz * sp, ALPHA * dz],
                [_colsum(dy * xh), _colsum(dy), jnp.broadcast_to(loss, (1, LANES))])

    return _rowwise(fn, [h1, ffn, p, tgt], [g, b, w_pg, w_ple_t], [(D_MODEL, BF16)] * 3 + [(D_MODEL, F32)],
                    accs=[(1, D_MODEL), (1, D_MODEL), (1, LANES)], tm=FUSED_ROWS, name="tail")


def _local_step(x, p, pos, tgt, w, late_weights, emit):
    w = dict(w)
    s_dim = x.shape[0]
    xb, pb = x.astype(BF16), p.astype(BF16)
    proj_a, proj_g, proj_b = _input_proj(xb, w["w_in_t"], w["wg_t"], w["wb_t"])
    qkvn = _conv_fwd(proj_a, w["conv"])
    beta, gc = _gates_fwd(proj_b, w["alog"], w["dtb"])
    gc_t = jnp.transpose(gc[:, :N_HEADS])
    u, w_, qd, kt, a_mat, t_fold = _gdr_prep_fwd(qkvn, beta, gc, gc_t)
    o_dn, states = _gdr_scan_fwd(u, w_, qd, kt, a_mat, gc)
    w.update(late_weights("mix", o_dn))
    og, y_dn = _gdr_out_fwd(o_dn, proj_a, w["dnw"], w["br_dn"])
    c_q, c_kv, qc, kc, vv, kct, vt = _mla_prep_fwd(proj_b, pos, w["qnw"], w["kvnw"], w["uq"], w["uk"], w["uv"])
    o_mla, lse = _attn_fwd(qc, kc, vt)
    y_mla = _mm(o_mla, w["br_mla"], out_dtype=BF16, name="f_y_mla")
    mixed, a1, h1, h1b = _mix_proj_ln1(y_dn, y_mla, proj_g, x, w["wo"], w["ln1g"], w["ln1b"])
    w.update(late_weights("ffn", a1))
    gt, up, act = _ffn_in_act(h1b, w["ffn_in_t"])
    ffn = _mm_resident(act, w["ffn_out"], name="f_ffn")
    g = {}
    dz2, d_pg, d_pp, dh1a, g["ln2g"], g["ln2b"], loss = _tail(h1, ffn, pb, tgt, w["ln2g"], w["ln2b"],
                                                            w["ple_gate"], w["ple_t"])
    g["ple_t"] = _mm(d_pp, pb, ta=True, out_dtype=BF16, name="b_w_ple")
    g["ple_gate"] = _mm(h1b, d_pg, ta=True, out_dtype=BF16, name="b_w_ple_gate")
    g["ffn_out"] = _mm(act, dz2, ta=True, out_dtype=BF16, name="b_w_ffn_out")
    d_act = _mm(dz2, w["ffn_out"], tb=True, out_dtype=BF16, name="b_act")
    d_gu = _act_bwd(gt, up, d_act)
    g["ffn_in_t"] = _mm(d_gu, h1b, ta=True, out_dtype=BF16, name="b_w_ffn_in")
    d_gu = emit("ffn", g, d_gu)
    d_h1 = _mm_resident(d_gu, w["ffn_in_t"], add=(dh1a,), name="b_h1_ffn")
    dz1, dxa, d_proj_g, d_y_dn, d_y_mla, g["ln1g"], g["ln1b"] = _ln1_mix_bwd(
        x, a1, d_h1, d_pg, y_dn, y_mla, proj_g, w["ln1g"], w["wo"], w["ple_gate"])
    g["wo"] = _mm(mixed, dz1, ta=True, out_dtype=BF16, name="b_w_o")
    g["br_mla"] = _mm(o_mla, d_y_mla, ta=True, out_dtype=BF16, name="b_w_br_mla")
    d_o_mla = _mm(d_y_mla, w["br_mla"], tb=True, out_dtype=BF16, name="b_o_mla")
    d_qc, d_kc, d_v = _attn_bwd(qc, kc, kct, vv, o_mla, d_o_mla, lse)
    d_q_full, d_kn, d_cq, d_ckv, d_kr, g["qnw"], g["kvnw"] = _mla_prep_bwd(
        proj_b, pos, d_qc, d_kc, d_v, w["qnw"], w["kvnw"], w["uq"], w["uk"], w["uv"])
    g["uq"] = _mm(c_q, d_q_full, ta=True, out_dtype=BF16, name="b_w_uq")
    g["uk"] = _mm(c_kv, d_kn, ta=True, out_dtype=BF16, name="b_w_uk")
    g["uv"] = _mm(c_kv, d_v, ta=True, out_dtype=BF16, name="b_w_uv")
    g["br_dn"] = _mm(og, d_y_dn, ta=True, out_dtype=BF16, name="b_w_br_dn")
    d_y_dn = emit("mix", g, d_y_dn)
    d_o_dn, d_z, g["dnw"] = _gdr_out_bwd(o_dn, proj_a, d_y_dn, w["dnw"], w["br_dn"])
    du, dw, dqd, dkt, d_a, d_egl = _gdr_scan_bwd(u, w_, qd, kt, a_mat, gc, states, d_o_dn)
    dq, dk, dv, d_beta, d_gc = _gdr_prep_bwd(qkvn, beta, gc, gc_t, t_fold, u, w_, du, dw, dqd, dkt, d_a)
    d_egl_rows = jnp.pad(d_egl[:, None, :, 0], ((0, 0), (CHUNK - 1, 0), (0, LANES - N_HEADS))).reshape(s_dim, LANES)
    d_ba, g["alog"], g["dtb"] = _gates_bwd(proj_b, w["alog"], w["dtb"], gc, d_beta, d_gc, d_egl_rows)
    d_qkv, g["conv"] = _conv_bwd(proj_a, w["conv"], dq, dk, dv)
    zeros = jnp.zeros((s_dim, WB_CKV - Q_LORA), BF16)
    d_proj_b = jnp.concatenate([d_cq, zeros, d_ckv, d_kr, d_ba], axis=1)
    g["wa_qkv_t"] = _mm(d_qkv, xb, ta=True, name="b_w_qkv")
    g["wa_z_t"] = _mm(d_z, xb, ta=True, name="b_w_z")
    g["wg_t"] = _mm(d_proj_g, xb, ta=True, name="b_w_g")
    g["wb_t"] = _mm(d_proj_b, xb, ta=True, name="b_w_b")
    d_qkv = emit("small", dict(g, loss=loss), emit("w_in", g, d_qkv))
    dx = _input_grad(d_qkv, d_z, d_proj_g, d_proj_b, w["w_in_t"], w["wg_t"], w["wb_t"], dxa)
    return loss, dx, g


DX_ROWS = 512


def _input_proj(xb, w_in_t, wg_t, wb_t):
    s_dim = xb.shape[0]
    n_a = 4 * D_MODEL

    def body(x_ref, wa_ref, wg_ref, wb_ref, a_ref, g_ref, b_ref):
        xv = x_ref[...]
        a_ref[...] = _dot(xv, wa_ref[...], NT)
        g_ref[...] = _dot(xv, wg_ref[...], NT).astype(BF16)
        b_ref[...] = _dot(xv, wb_ref[...], NT)

    rows = lambda width: pl.BlockSpec((DX_ROWS, width), lambda i: (i, 0))
    whole = lambda shape: pl.BlockSpec(shape, lambda i: (0, 0), pipeline_mode=pl.Buffered(1))
    return pl.pallas_call(
        body,
        out_shape=[jax.ShapeDtypeStruct((s_dim, n_a), F32), jax.ShapeDtypeStruct((s_dim, wg_t.shape[0]), BF16),
                   jax.ShapeDtypeStruct((s_dim, wb_t.shape[0]), F32)],
        grid=(s_dim // DX_ROWS,),
        in_specs=[rows(D_MODEL), whole((n_a, D_MODEL)), whole(wg_t.shape), whole(wb_t.shape)],
        out_specs=[rows(n_a), rows(wg_t.shape[0]), rows(wb_t.shape[0])],
        compiler_params=pltpu.CompilerParams(dimension_semantics=("parallel",)),
        name="f_proj",
    )(xb, w_in_t, wg_t, wb_t)


def _input_grad(d_qkv, d_z, d_g, d_b, w_in_t, wg_t, wb_t, add):
    s_dim = d_qkv.shape[0]
    n_qkv, n_a = d_qkv.shape[1], d_qkv.shape[1] + d_z.shape[1]

    def body(q_ref, z_ref, g_ref, b_ref, wa_ref, wg_ref, wb_ref, add_ref, o_ref):
        r = add_ref[...] + _dot(q_ref[...], wa_ref[0:n_qkv])
        r = r + _dot(z_ref[...], wa_ref[n_qkv:n_a])
        r = r + _dot(g_ref[...], wg_ref[...])
        o_ref[...] = r + _dot(b_ref[...], wb_ref[...])

    rows = lambda a: pl.BlockSpec((DX_ROWS, a.shape[1]), lambda i: (i, 0))
    whole = lambda shape: pl.BlockSpec(shape, lambda i: (0, 0), pipeline_mode=pl.Buffered(1))
    return pl.pallas_call(
        body,
        out_shape=jax.ShapeDtypeStruct((s_dim, D_MODEL), F32),
        grid=(s_dim // DX_ROWS,),
        in_specs=[rows(d_qkv), rows(d_z), rows(d_g), rows(d_b), whole((n_a, D_MODEL)), whole(wg_t.shape),
                  whole(wb_t.shape), rows(add)],
        out_specs=pl.BlockSpec((DX_ROWS, D_MODEL), lambda i: (i, 0)),
        compiler_params=pltpu.CompilerParams(dimension_semantics=("parallel",)),
        name="b_x",
    )(d_qkv, d_z, d_g, d_b, w_in_t, wg_t, wb_t, add)


_BIG = (("w_in", 1), ("w_uq", 0), ("w_uk", 0), ("w_uv", 0), ("w_br_dn", 0), ("w_br_mla", 0),
        ("w_o", 0), ("w_ffn_in", 1), ("w_ffn_out", 0), ("w_ple", 1), ("w_ple_gate", 0))
_BIG_AXIS = dict(_BIG)
_SMALL = ("ln1_g", "ln1_b", "ln2_g", "ln2_b", "q_norm_w", "kv_norm_w", "dn_norm_w", "dn_a_log", "dn_dt_bias")
_ORDER = ("w_in", "conv_w", "dn_a_log", "dn_dt_bias", "dn_norm_w", "q_norm_w", "w_uq", "kv_norm_w", "w_uk", "w_uv",
          "w_br_dn", "w_br_mla", "w_o", "ln1_g", "ln1_b", "w_ffn_in", "w_ffn_out", "w_ple", "w_ple_gate", "ln2_g",
          "ln2_b")


def _stored_shape(name, shard_shape):
    axis = _BIG_AXIS[name]
    lead = shard_shape[axis]
    return lead, int(np.prod(shard_shape)) // lead


def _to_stored(name, shard):
    return jnp.moveaxis(shard, _BIG_AXIS[name], 0).reshape(_stored_shape(name, shard.shape))


def _from_stored(name, stored, shard_shape):
    axis = _BIG_AXIS[name]
    moved = (shard_shape[axis],) + shard_shape[:axis] + shard_shape[axis + 1:]
    return jnp.moveaxis(stored.reshape(moved), 0, axis)


_W_IN_ROWS = np.cumsum([0, 3072, 1024, 8, 8, Q_LORA, KV_LORA, ROPE, D_MODEL, D_MODEL])


def _first_weights(w_in_t, conv_full, small):
    r = _W_IN_ROWS
    zr = lambda n: jnp.zeros((n, D_MODEL), w_in_t.dtype)
    w = {}
    w["w_in_t"] = w_in_t
    w["wg_t"] = w_in_t[r[7]:r[9]]
    w["wb_t"] = jnp.concatenate([w_in_t[r[4]:r[5]], zr(WB_CKV - Q_LORA), w_in_t[r[5]:r[7]], zr(LANES - ROPE),
                                 w_in_t[r[2]:r[4]], zr(LANES - 2 * N_HEADS)], axis=0)
    w["conv"] = conv_full
    pad_l = lambda v: jnp.pad(v, ((0, 0), (0, LANES - v.shape[1])))
    w["alog"], w["dtb"] = pad_l(small["dn_a_log"]), pad_l(small["dn_dt_bias"])
    w["dnw"], w["qnw"], w["kvnw"] = small["dn_norm_w"], small["q_norm_w"], small["kv_norm_w"]
    w["ln1g"], w["ln1b"], w["ln2g"], w["ln2b"] = small["ln1_g"], small["ln1_b"], small["ln2_g"], small["ln2_b"]
    return w


def _late_weights(group, fw):
    w = {}
    if group == "mix":
        uq = fw["w_uq"].reshape(Q_LORA, N_HEADS, HEAD + ROPE)
        uq_r = jnp.pad(uq[:, :, HEAD:], ((0, 0), (0, 0), (0, HEAD - ROPE)))
        w["uq"] = jnp.concatenate([uq[:, :, :HEAD].reshape(Q_LORA, -1), uq_r.reshape(Q_LORA, -1)], axis=1)
        w["uk"], w["uv"] = fw["w_uk"], fw["w_uv"]
        w["br_dn"], w["br_mla"], w["wo"] = fw["w_br_dn"], fw["w_br_mla"], fw["w_o"]
    else:
        w["ffn_in_t"], w["ffn_out"] = fw["w_ffn_in"], fw["w_ffn_out"]
        w["ple_t"], w["ple_gate"] = fw["w_ple"], fw["w_ple_gate"]
    return w


_GROUP_GRADS = {"ffn": (("w_ple", "ple_t"), ("w_ple_gate", "ple_gate"), ("w_ffn_out", "ffn_out"),
                        ("w_ffn_in", "ffn_in_t")),
                "mix": (("w_o", "wo"), ("w_br_mla", "br_mla"), ("w_uq", "uq"), ("w_uk", "uk"), ("w_uv", "uv"),
                        ("w_br_dn", "br_dn"))}


def _group_grads(group, g):
    out = {}
    for name, key in _GROUP_GRADS[group]:
        t = g[key]
        if name == "w_uq":
            uq_n = t[:, :D_MODEL].reshape(Q_LORA, N_HEADS, HEAD)
            uq_r = t[:, D_MODEL:].reshape(Q_LORA, N_HEADS, HEAD)[:, :, :ROPE]
            t = jnp.concatenate([uq_n, uq_r], axis=2).reshape(Q_LORA, -1)
        out[name] = t
    return out


PACK_ROWS = 512
SUBLANES = 8


def _pack_exchange(parts, name):
    arrays = []
    for a, _, _ in parts:
        if not any(a is b for b in arrays):
            arrays.append(a)
    index = lambda a: next(i for i, b in enumerate(arrays) if a is b)
    chunks, dst = [], 0
    for a, first, rows in parts:
        assert first % SUBLANES == 0 and rows % SUBLANES == 0
        chunks += [(index(a), first + o, dst + o, min(PACK_ROWS, rows - o)) for o in range(0, rows, PACK_ROWS)]
        dst += rows
    c, n, last = arrays[0].shape[1], len(arrays), len(chunks) - 1
    slab = dst // N_DEV
    assert slab * N_DEV == dst

    def body(*refs):
        src_refs, out_ref, recv_ref = refs[:n], refs[n], refs[n + 1]
        buf, sem_in, sem_out, send_sems, recv_sems = refs[n + 2:]
        x, y, core = lax.axis_index("x"), lax.axis_index("y"), lax.axis_index("c")

        def to_sibling(q):
            return pltpu.make_async_remote_copy(
                src_ref=out_ref.at[pl.ds((2 * q + 1 - core) * slab, slab)], dst_ref=recv_ref.at[q],
                send_sem=send_sems.at[q], recv_sem=recv_sems.at[q], device_id=(x, y, 1 - core),
                device_id_type=_MESH_ID)

        sent = [0]

        def send_packed(rows_done):
            while sent[0] < N_DEV // 2 and (2 * sent[0] + 2) * slab <= rows_done:
                to_sibling(sent[0]).start()
                sent[0] += 1

        def load(k):
            i, first, _, rows = chunks[k]
            return pltpu.make_async_copy(src_refs[i].at[pl.ds(first, rows)], buf.at[k % 2, pl.ds(0, rows)],
                                         sem_in.at[k % 2])

        def store(k):
            _, _, first, rows = chunks[k]
            return pltpu.make_async_copy(buf.at[k % 2, pl.ds(0, rows)], out_ref.at[pl.ds(first, rows), 0, :],
                                         sem_out.at[k % 2])

        load(0).start()
        for k in range(last + 1):
            load(k).wait()
            store(k).start()
            if k >= 1:
                store(k - 1).wait()
                send_packed(chunks[k][2])
            if k < last:
                load(k + 1).start()
        store(last).wait()
        send_packed(dst)
        for q in range(N_DEV // 2):
            to_sibling(q).wait_recv()
        for q in range(N_DEV // 2):
            to_sibling(q).wait_send()

    return pl.pallas_call(
        body,
        out_shape=[jax.ShapeDtypeStruct((dst, 1, c), F32), jax.ShapeDtypeStruct((N_DEV // 2, slab, 1, c), F32)],
        in_specs=[_ANY] * n,
        out_specs=[_ANY, _ANY],
        scratch_shapes=[pltpu.VMEM((2, PACK_ROWS, c), F32), pltpu.SemaphoreType.DMA((2,)),
                        pltpu.SemaphoreType.DMA((2,)), pltpu.SemaphoreType.DMA((N_DEV // 2,)),
                        pltpu.SemaphoreType.DMA((N_DEV // 2,))],
        name=name,
    )(*arrays)


def _w_in_grad_parts(g):
    wb = g["wb_t"]
    return [(g["wa_qkv_t"], 0, 3 * D_MODEL), (g["wa_z_t"], 0, D_MODEL), (wb, WB_BA, 2 * N_HEADS),
            (wb, WB_CQ, Q_LORA), (wb, WB_CKV, KV_LORA), (wb, WB_KR, ROPE), (g["wg_t"], 0, 2 * D_MODEL)]


def _small_grads(g):
    return {"ln1_g": g["ln1g"], "ln1_b": g["ln1b"], "ln2_g": g["ln2g"], "ln2_b": g["ln2b"], "q_norm_w": g["qnw"],
            "kv_norm_w": g["kvnw"], "dn_norm_w": g["dnw"], "dn_a_log": g["alog"], "dn_dt_bias": g["dtb"],
            "conv_w": g["conv"]}


_SMALL_SLOTS = {"ln1_g": (0, 0, 1024), "ln1_b": (1, 0, 1024), "ln2_g": (2, 0, 1024), "ln2_b": (3, 0, 1024),
                "q_norm_w": (4, 0, 384), "kv_norm_w": (4, 384, 256), "dn_norm_w": (4, 640, 128),
                "dn_a_log": (4, 768, 8), "dn_dt_bias": (4, 896, 8)}
_SMALL_ROWS, _LOSS_ROW, _CONV_ROW0, _CONV_ROWS = 24, 5, 8, 12


def _pack_small_grads(small_g, loss):
    zeros = lambda r, c: jnp.zeros((r, c), F32)
    row4 = jnp.concatenate([small_g["q_norm_w"], small_g["kv_norm_w"], small_g["dn_norm_w"], small_g["dn_a_log"],
                            small_g["dn_dt_bias"]], axis=1)
    row5 = jnp.concatenate([loss, zeros(1, FLAT_COLS - LANES)], axis=1)
    head = jnp.concatenate([small_g["ln1_g"], small_g["ln1_b"], small_g["ln2_g"], small_g["ln2_b"], row4, row5,
                            zeros(2, FLAT_COLS)], axis=0)
    conv = small_g["conv_w"].reshape(_CONV_ROWS, FLAT_COLS)
    return jnp.concatenate([head, conv, zeros(_SMALL_ROWS - _CONV_ROW0 - _CONV_ROWS, FLAT_COLS)], axis=0)


_MESH_ID = pl.DeviceIdType.MESH
_ANY = pl.BlockSpec(memory_space=pl.ANY)


def _all_gather(blocks, name):
    n = len(blocks)

    def body(*refs):
        x_refs, out_refs = refs[:n], refs[n:2 * n]
        send_sems, recv_sems, local_sems = refs[2 * n:]
        x, y, c = lax.axis_index("x"), lax.axis_index("y"), lax.axis_index("c")
        me, sibling = (x, y, c), (x, y, 1 - c)
        chips = [(1 - x, y), (x, 1 - y), (1 - x, 1 - y)]

        def slot(i, px, py, pc):
            return out_refs[i].at[4 * px + 2 * py + pc]

        def copy(i, k, origin, to, src=None):
            return pltpu.make_async_remote_copy(
                src_ref=slot(i, *origin) if src is None else src, dst_ref=slot(i, *origin),
                send_sem=send_sems.at[7 * i + k], recv_sem=recv_sems.at[7 * i + k], device_id=to,
                device_id_type=_MESH_ID)

        mine = [pltpu.make_async_copy(x_refs[i], slot(i, *me), local_sems.at[i]) for i in range(n)]
        first, passed = [], []
        for i in range(n):
            mine[i].start()
            first.append(copy(i, 0, me, sibling, src=x_refs[i]))
            first += [copy(i, 1 + j, me, (*chip, c), src=x_refs[i]) for j, chip in enumerate(chips)]
        for cp in first:
            cp.start()
        for i in range(n):
            for j, chip in enumerate(chips):
                copy(i, 1 + j, (*chip, c), me).wait_recv()
                passed.append(copy(i, 4 + j, (*chip, c), sibling))
                passed[-1].start()
        for i in range(n):
            copy(i, 0, sibling, me).wait_recv()
            for j, chip in enumerate(chips):
                copy(i, 4 + j, (*chip, 1 - c), me).wait_recv()
        for cp in first + passed:
            cp.wait_send()
        for cp in mine:
            cp.wait()

    return pl.pallas_call(
        body,
        out_shape=[jax.ShapeDtypeStruct((N_DEV,) + b.shape, b.dtype) for b in blocks],
        in_specs=[_ANY] * n,
        out_specs=[_ANY] * n,
        scratch_shapes=[pltpu.SemaphoreType.DMA((7 * n,)), pltpu.SemaphoreType.DMA((7 * n,)),
                        pltpu.SemaphoreType.DMA((n,))],
        name=name,
    )(*blocks)


def _col_tile(c):
    return c if c <= 256 else 256


def _chip_sum(src, recv, parity, name):
    _, r, _, c = src.shape
    tc = _col_tile(c)

    def body(par_ref, a_ref, b_ref, o_ref, ob_ref):
        s = a_ref[...] + b_ref[...]
        o_ref[...] = s
        ob_ref[...] = s.astype(BF16)

    rows = lambda f: pl.BlockSpec((None, r, None, tc), f)
    blk = pl.BlockSpec((None, r, tc), lambda q, j, par: (q, 0, j))
    return pl.pallas_call(
        body,
        out_shape=[jax.ShapeDtypeStruct((4, r, c), F32), jax.ShapeDtypeStruct((4, r, c), BF16)],
        grid_spec=pltpu.PrefetchScalarGridSpec(
            num_scalar_prefetch=1, grid=(4, c // tc),
            in_specs=[rows(lambda q, j, par: (2 * q + par[0], 0, 0, j)), rows(lambda q, j, par: (q, 0, 0, j))],
            out_specs=[blk, blk]),
        compiler_params=pltpu.CompilerParams(dimension_semantics=("parallel", "parallel")),
        name=name,
    )(parity, src, recv)


_HBM = pl.BlockSpec(memory_space=pltpu.HBM)
_SEM = pl.BlockSpec(memory_space=pltpu.SEMAPHORE)
_DATAFLOW = pltpu.SideEffectType.DATAFLOW_SIDE_EFFECTING
N_PEERS = N_DEV - 1


def _ring_peer(j):
    me = 4 * lax.axis_index("x") + 2 * lax.axis_index("y") + lax.axis_index("c")
    k = (me + j) % N_DEV
    return me, k, (k // 4, (k // 2) % 2, k % 2)


def _spread_copy(i, j, src_refs, land_refs, send_sems, recv_sems, scatter):
    me, k, peer = _ring_peer(j)
    return pltpu.make_async_remote_copy(
        src_ref=src_refs[i].at[k] if scatter else src_refs[i], dst_ref=land_refs[i].at[me],
        send_sem=send_sems.at[N_PEERS * i + j - 1], recv_sem=recv_sems.at[N_PEERS * i + j - 1], device_id=peer,
        device_id_type=_MESH_ID)


def _spread_start(srcs, carry, scatter, name):
    n = len(srcs)
    lands = [lax.empty(((N_DEV,) + s.shape[-2:]), s.dtype) for s in srcs]

    def body(*refs):
        src_refs, land_refs = refs[:n], refs[n:2 * n]
        send_sems, recv_sems, local_sems = refs[2 * n + 1:2 * n + 4]
        for i in range(n):
            for j in range(1, N_DEV):
                _spread_copy(i, j, src_refs, land_refs, send_sems, recv_sems, scatter).start()
        for i in range(n):
            _own_copy(i, src_refs, land_refs, local_sems, scatter).start()

    hbm = lambda a: pltpu.HBM(a.shape, a.dtype)
    sems = pltpu.SemaphoreType.DMA((N_PEERS * n,))
    pinned = [pltpu.with_memory_space_constraint(a, pltpu.HBM) for a in list(srcs) + lands + [carry]]
    res = pl.pallas_call(
        body, name=name,
        out_shape=(sems, sems, pltpu.SemaphoreType.DMA((n,)), *[hbm(a) for a in pinned]),
        in_specs=[_HBM] * (2 * n + 1),
        out_specs=(_SEM, _SEM, _SEM, *[_HBM] * (2 * n + 1)),
        input_output_aliases={i: 3 + i for i in range(2 * n + 1)},
        compiler_params=pltpu.CompilerParams(has_side_effects=_DATAFLOW),
    )(*pinned)
    return res[:3], list(res[3:3 + n]), list(res[3 + n:3 + 2 * n]), res[3 + 2 * n]


def _own_copy(i, src_refs, land_refs, local_sems, scatter):
    me = _ring_peer(0)[0]
    return pltpu.make_async_copy(src_refs[i].at[me] if scatter else src_refs[i], land_refs[i].at[me],
                                 local_sems.at[i])


def _spread_wait(started, after, scatter, name):
    sems, srcs, lands, _ = started
    n = len(srcs)

    def body(*refs):
        src_refs, land_refs = refs[:n], refs[n:2 * n]
        send_s, recv_s, local_s = refs[2 * n:2 * n + 3]
        for i in range(n):
            for j in range(1, N_DEV):
                cp = _spread_copy(i, j, src_refs, land_refs, send_s, recv_s, scatter)
                cp.wait_send()
                cp.wait_recv()
        for i in range(n):
            _own_copy(i, src_refs, land_refs, local_s, scatter).wait()

    hbm = lambda a: pltpu.HBM(a.shape, a.dtype)
    res = pl.pallas_call(
        body, name=name,
        out_shape=tuple(hbm(a) for a in srcs + lands),
        in_specs=[_HBM] * (2 * n) + [_SEM, _SEM, _SEM, pl.BlockSpec(memory_space=pl.ANY)],
        out_specs=tuple([_HBM] * (2 * n)),
        input_output_aliases={i: i for i in range(2 * n)},
        compiler_params=pltpu.CompilerParams(has_side_effects=_DATAFLOW),
    )(*srcs, *lands, *sems, after)
    return list(res[n:])


def _chips_copy(i, j, src_refs, land_refs, send_sems, recv_sems):
    x, y, c = lax.axis_index("x"), lax.axis_index("y"), lax.axis_index("c")
    tx, ty = [(1 - x, y), (x, 1 - y), (1 - x, 1 - y)][j]
    return pltpu.make_async_remote_copy(
        src_ref=src_refs[i].at[2 * tx + ty], dst_ref=land_refs[i].at[j], send_sem=send_sems.at[3 * i + j],
        recv_sem=recv_sems.at[3 * i + j], device_id=(tx, ty, c), device_id_type=_MESH_ID)


def _chips_start(srcs, carry, name):
    n = len(srcs)
    lands = [lax.empty((3,) + s.shape[1:], s.dtype) for s in srcs]

    def body(*refs):
        src_refs, land_refs = refs[:n], refs[n:2 * n]
        send_sems, recv_sems = refs[2 * n + 1:2 * n + 3]
        for i in range(n):
            for j in range(3):
                _chips_copy(i, j, src_refs, land_refs, send_sems, recv_sems).start()

    hbm = lambda a: pltpu.HBM(a.shape, a.dtype)
    sems = pltpu.SemaphoreType.DMA((3 * n,))
    pinned = [pltpu.with_memory_space_constraint(a, pltpu.HBM) for a in list(srcs) + lands + [carry]]
    res = pl.pallas_call(
        body, name=name,
        out_shape=(sems, sems, *[hbm(a) for a in pinned]),
        in_specs=[_HBM] * (2 * n + 1),
        out_specs=(_SEM, _SEM, *[_HBM] * (2 * n + 1)),
        input_output_aliases={i: 2 + i for i in range(2 * n + 1)},
        compiler_params=pltpu.CompilerParams(has_side_effects=_DATAFLOW),
    )(*pinned)
    return res[:2], list(res[2:2 + n]), list(res[2 + n:2 + 2 * n]), res[2 + 2 * n]


def _chips_wait(started, after, name):
    sems, srcs, lands, _ = started
    n = len(srcs)

    def body(*refs):
        src_refs, land_refs = refs[:n], refs[n:2 * n]
        send_s, recv_s = refs[2 * n:2 * n + 2]
        for i in range(n):
            for j in range(3):
                cp = _chips_copy(i, j, src_refs, land_refs, send_s, recv_s)
                cp.wait_send()
                cp.wait_recv()

    hbm = lambda a: pltpu.HBM(a.shape, a.dtype)
    res = pl.pallas_call(
        body, name=name,
        out_shape=tuple(hbm(a) for a in srcs + lands),
        in_specs=[_HBM] * (2 * n) + [_SEM, _SEM, pl.BlockSpec(memory_space=pl.ANY)],
        out_specs=tuple([_HBM] * (2 * n)),
        input_output_aliases={i: i for i in range(2 * n)},
        compiler_params=pltpu.CompilerParams(has_side_effects=_DATAFLOW),
    )(*srcs, *lands, *sems, after)
    return list(res[n:])


def _sum8(landing, name):
    _, r, c = landing.shape
    tc = _col_tile(c)

    def body(a_ref, o_ref):
        tot = a_ref[0].astype(F32)
        for k in range(1, N_DEV):
            tot = tot + a_ref[k].astype(F32)
        o_ref[...] = tot

    return pl.pallas_call(
        body,
        out_shape=jax.ShapeDtypeStruct((r, c), F32),
        grid=(c // tc,),
        in_specs=[pl.BlockSpec((N_DEV, r, tc), lambda j: (0, 0, j))],
        out_specs=pl.BlockSpec((r, tc), lambda j: (0, j)),
        compiler_params=pltpu.CompilerParams(dimension_semantics=("parallel",)),
        name=name,
    )(landing)


def _adamw_math(w, g, m, v):
    m = ADAM_B1 * m + (1.0 - ADAM_B1) * g
    v = ADAM_B2 * v + (1.0 - ADAM_B2) * (g * g)
    m_hat = m / (1.0 - ADAM_B1 ** ADAM_STEP)
    v_hat = v / (1.0 - ADAM_B2 ** ADAM_STEP)
    delta = -ADAM_LR * (m_hat / (jnp.sqrt(v_hat) + ADAM_EPS) + ADAM_WD * w)
    return delta, m, v


def _adamw(w, m, v, g, name):
    r, c = w.shape

    def fn(rows, consts):
        return list(_adamw_math(*rows)), []

    return _rowwise(fn, [w, g, m, v], [], [(c, F32)] * 3, tm=r if r <= 512 else 256, name=name)


def _adamw_sum8(w, m, v, landing, name):
    r, c = w.shape
    tc = _col_tile(c)

    def body(w_ref, m_ref, v_ref, a_ref, g_ref, d_ref, m2_ref, v2_ref):
        g = a_ref[0].astype(F32)
        for k in range(1, N_DEV):
            g = g + a_ref[k].astype(F32)
        delta, m2, v2 = _adamw_math(w_ref[...], g, m_ref[...], v_ref[...])
        g_ref[...] = g
        d_ref[...] = delta
        m2_ref[...] = m2
        v2_ref[...] = v2

    blk = pl.BlockSpec((r, tc), lambda j: (0, j))
    return pl.pallas_call(
        body,
        out_shape=[jax.ShapeDtypeStruct((r, c), F32)] * 4,
        grid=(c // tc,),
        in_specs=[blk, blk, blk, pl.BlockSpec((N_DEV, r, tc), lambda j: (0, 0, j))],
        out_specs=[blk] * 4,
        compiler_params=pltpu.CompilerParams(dimension_semantics=("parallel",)),
        name=name,
    )(w, m, v, landing)


def _adamw_parts(w, m, v, own, others, chip, name):
    r, _, c = w.shape
    tc = _col_tile(c)

    def body(q_ref, w_ref, m_ref, v_ref, a_ref, b_ref, g_ref, d_ref, m2_ref, v2_ref):
        g = ((a_ref[...] + b_ref[0].astype(F32)) + b_ref[1].astype(F32)) + b_ref[2].astype(F32)
        delta, m2, v2 = _adamw_math(w_ref[...], g, m_ref[...], v_ref[...])
        g_ref[...] = g
        d_ref[...] = delta
        m2_ref[...] = m2
        v2_ref[...] = v2

    row = pl.BlockSpec((r, None, tc), lambda j, q: (0, 0, j))
    return pl.pallas_call(
        body,
        out_shape=[jax.ShapeDtypeStruct((r, 1, c), F32)] * 4,
        grid_spec=pltpu.PrefetchScalarGridSpec(
            num_scalar_prefetch=1, grid=(c // tc,),
            in_specs=[row, row, row, pl.BlockSpec((None, r, tc), lambda j, q: (q[0], 0, j)),
                      pl.BlockSpec((3, r, tc), lambda j, q: (0, 0, j))],
            out_specs=[row] * 4),
        compiler_params=pltpu.CompilerParams(dimension_semantics=("parallel",)),
        name=name,
    )(chip, w, m, v, own, others)


def _adamw_small(gathered, params):
    ns = len(_SMALL)

    def body(*refs):
        g_ref, p_refs, o_refs = refs[0], refs[1:1 + 3 * ns], refs[1 + 3 * ns:]
        tot = g_ref[0]
        for k in range(1, N_DEV):
            tot = tot + g_ref[k]
        for i, name in enumerate(_SMALL):
            row, lane0, lanes = _SMALL_SLOTS[name]
            g = tot[row:row + 1, lane0:lane0 + lanes]
            w_, m_, v_ = (p_refs[3 * i + j][...] for j in range(3))
            delta, m2, v2 = _adamw_math(w_, g, m_, v_)
            for j, val in enumerate((g, delta, m2, v2)):
                o_refs[4 * i + j][...] = val
        o_refs[4 * ns][...] = tot[_LOSS_ROW:_LOSS_ROW + 1, 0:LANES]
        o_refs[4 * ns + 1][...] = tot[_CONV_ROW0:_CONV_ROW0 + _CONV_ROWS, :]

    out_shape = [jax.ShapeDtypeStruct(w.shape, F32) for (w, _, _) in params for _ in range(4)]
    out_shape += [jax.ShapeDtypeStruct((1, LANES), F32), jax.ShapeDtypeStruct((_CONV_ROWS, FLAT_COLS), F32)]
    flat = [a for wmv in params for a in wmv]
    return pl.pallas_call(body, out_shape=out_shape, name="adamw_small")(gathered, *flat)


def kernel(x, p, positions, w_in, conv_w, dn_a_log, dn_dt_bias, dn_norm_w, q_norm_w, w_uq, kv_norm_w, w_uk, w_uv, w_br_dn, w_br_mla, w_o, ln1_g, ln1_b, w_ffn_in, w_ffn_out, w_ple, w_ple_gate, ln2_g, ln2_b, loss_target, m_w_in, m_conv_w, m_dn_a_log, m_dn_dt_bias, m_dn_norm_w, m_q_norm_w, m_w_uq, m_kv_norm_w, m_w_uk, m_w_uv, m_w_br_dn, m_w_br_mla, m_w_o, m_ln1_g, m_ln1_b, m_w_ffn_in, m_w_ffn_out, m_w_ple, m_w_ple_gate, m_ln2_g, m_ln2_b, v_w_in, v_conv_w, v_dn_a_log, v_dn_dt_bias, v_dn_norm_w, v_q_norm_w, v_w_uq, v_kv_norm_w, v_w_uk, v_w_uv, v_w_br_dn, v_w_br_mla, v_w_o, v_ln1_g, v_ln1_b, v_w_ffn_in, v_w_ffn_out, v_w_ple, v_w_ple_gate, v_ln2_g, v_ln2_b):
    args = dict(locals())
    wts = {n: args[n] for n in _ORDER}
    mom1 = {n: args["m_" + n] for n in _ORDER}
    mom2 = {n: args["v_" + n] for n in _ORDER}
    big_names = [n for n, _ in _BIG]
    shard_shapes = {n: wts[n].shape[1:] for n in big_names}
    c_idx = lax.axis_index("c")
    q_idx = 2 * lax.axis_index("x") + lax.axis_index("y")
    parity, chip = c_idx.reshape(1).astype(jnp.int32), q_idx.reshape(1).astype(jnp.int32)

    stored = {n: _to_stored(n, wts[n][0]).astype(BF16) for n in big_names}
    first = _all_gather([stored["w_in"], conv_w[0]], "ag_first")
    group_names = {grp: [n for n, _ in pairs] for grp, pairs in _GROUP_GRADS.items()}
    carry, gathers = first[0], {}
    for grp in ("mix", "ffn"):
        gathers[grp] = _spread_start([stored[n] for n in group_names[grp]], carry, False, "ag_start_" + grp)
        carry = gathers[grp][3]
    conv_full = jnp.moveaxis(first[1], 0, 1).reshape(conv_w.shape[1], -1)
    small_w = {n: wts[n].astype(F32) for n in _SMALL}
    w = _first_weights(carry.reshape(-1, D_MODEL), conv_full, small_w)

    def late_weights(grp, after):
        got = _spread_wait(gathers[grp], after, False, "ag_wait_" + grp)
        return _late_weights(grp, {n: t.reshape(-1, t.shape[-1]) for n, t in zip(group_names[grp], got)})

    started = {}

    def emit(group, g, carry):
        if group == "w_in":
            rows, cols = _stored_shape("w_in", shard_shapes["w_in"])
            packed, from_sibling = _pack_exchange(_w_in_grad_parts(g), "rs_pack_sibling")
            own, own_bf = _chip_sum(packed.reshape(N_DEV, rows, 1, cols), from_sibling, parity, "rs_sum_w_in")
            started["w_in"] = (own, _chips_start([own_bf], carry, "rs_chips_start"))
            return started["w_in"][1][3]
        if group == "small":
            block = _pack_small_grads(_small_grads(g), g["loss"])
            started["small"] = _spread_start([block], carry, False, "ag_start_small")
            return started["small"][3]
        grads = _group_grads(group, g)
        srcs = [grads[n].reshape((N_DEV,) + _stored_shape(n, shard_shapes[n])) for n in grads]
        started[group] = (list(grads), _spread_start(srcs, carry, True, "rs_start_" + group))
        return started[group][1][3]

    s_dim = x.shape[1]
    loss, dx, g = _local_step(x[0], p[0, 0], positions.reshape(s_dim, 1).astype(F32), loss_target[0], w,
                              late_weights, emit)
    own, chips_started = started.pop("w_in")
    small_started = started.pop("small")

    out_g, out_d, out_m, out_v = {}, {}, {}, {}

    def update(n, grad, shp):
        flat2 = (shp[0], int(np.prod(shp[1:])))
        d, m2, v2 = _adamw(wts[n][0].reshape(flat2), mom1[n][0].reshape(flat2), mom2[n][0].reshape(flat2),
                           grad.reshape(flat2), "adamw_" + n)
        out_g[n], out_d[n], out_m[n], out_v[n] = grad, d.reshape(shp), m2.reshape(shp), v2.reshape(shp)

    for group, (names, st) in started.items():
        for n, landing in zip(names, _spread_wait(st, dx, True, "rs_wait_" + group)):
            shp = shard_shapes[n]
            if _BIG_AXIS[n] == 0 or shp[-1] % LANES:
                res = _adamw_sum8(_to_stored(n, wts[n][0]), _to_stored(n, mom1[n][0]), _to_stored(n, mom2[n][0]),
                                  landing, "adamw_" + n)
                out_g[n], out_d[n], out_m[n], out_v[n] = (_from_stored(n, t, shp) for t in res)
                last = res[3]
            else:
                update(n, _from_stored(n, _sum8(landing, "rs_total_" + n), shp), shp)

    from_chips = _chips_wait(chips_started, last, "rs_chips_wait")[0]
    g_small = _spread_wait(small_started, last, False, "ag_wait_small")[0]
    rows_first = lambda a: jnp.transpose(a, (2, 0, 1))
    res = _adamw_parts(rows_first(wts["w_in"]), rows_first(mom1["w_in"]), rows_first(mom2["w_in"]), own, from_chips,
                       chip, "adamw_w_in")
    out_g["w_in"], out_d["w_in"], out_m["w_in"], out_v["w_in"] = (jnp.transpose(t, (1, 2, 0))[0] for t in res)

    res = _adamw_small(g_small, [(wts[n], mom1[n], mom2[n]) for n in _SMALL])
    for i, n in enumerate(_SMALL):
        out_g[n], out_d[n], out_m[n], out_v[n] = res[4 * i:4 * i + 4]
    loss_out = res[4 * len(_SMALL)][0, 0]
    conv_shape = conv_w.shape[1:]
    conv_g = lax.dynamic_slice(res[-1].reshape(conv_shape[0], -1), (0, (2 * q_idx + c_idx) * conv_shape[1]),
                               conv_shape)
    update("conv_w", conv_g, conv_shape)

    expand = lambda d, n: d[n] if n in _SMALL else d[n][None]
    return (loss_out, dx[None], *[expand(out_g, n) for n in _ORDER], *[expand(out_d, n) for n in _ORDER],
            *[expand(out_m, n) for n in _ORDER], *[expand(out_v, n) for n in _ORDER])
```

```python
import functools

import numpy as np
import jax
import jax.numpy as jnp
from jax import lax
from jax.experimental import pallas as pl
from jax.experimental.pallas import tpu as pltpu

F32 = jnp.float32
BF16 = jnp.bfloat16

D_MODEL = 1024
N_HEADS = 8
HEAD = 128
CHUNK = 64
GROUP = 256
ROPE = 64
Q_LORA = 384
KV_LORA = 256
FFN_HIDDEN = 2816
PLE_DIM = 256
ROPE_BASE = 10000.0
ALPHA = 2.0 ** 0.25
SCALE = float((HEAD + ROPE) ** -0.5)
NEG_BIG = -1e30
EPS_RMS = 1e-6
EPS_LN = 1e-5

ADAM_LR = 0.001
ADAM_B1 = 0.9
ADAM_B2 = 0.999
ADAM_EPS = 1e-08
ADAM_WD = 0.01
ADAM_STEP = 10

N_DEV = 8
LANES = 128
FLAT_COLS = 1024

WB_CQ, WB_CKV, WB_KR, WB_BA, WB_COLS = 0, 512, 768, 896, 1024

HIGHEST = lax.Precision.HIGHEST

NN = (((1,), (0,)), ((), ()))
TN = (((0,), (0,)), ((), ()))
NT = (((1,), (1,)), ((), ()))


def _dot(a, b, dims=NN):
    return lax.dot_general(a.astype(BF16), b.astype(BF16), dims, preferred_element_type=F32)


def _dot32(a, b, dims=NN):
    return lax.dot_general(a, b, dims, precision=HIGHEST, preferred_element_type=F32)


def _sig(x):
    return 1.0 / (1.0 + jnp.exp(-x))


MM_TILE = 1536


def _pick_wide(n):
    if n <= MM_TILE:
        return n
    return max(t for t in range(LANES, MM_TILE + 1, LANES) if n % t == 0)


def _split_bf16(a):
    hi = a.astype(BF16)
    return hi, (a - hi.astype(F32)).astype(BF16)


def _dot3(a, b, dims=NN):
    ah, al = a if isinstance(a, tuple) else _split_bf16(a)
    bh, bl = b if isinstance(b, tuple) else _split_bf16(b)
    d = lambda p, q: lax.dot_general(p, q, dims, preferred_element_type=F32)
    return d(ah, bh) + (d(ah, bl) + d(al, bh))


def _mm(a, b, *, ta=False, tb=False, add=(), out_dtype=F32, name):
    if ta:
        k_dim, m_dim = a.shape
    else:
        m_dim, k_dim = a.shape
    if tb:
        n_dim, k2 = b.shape
    else:
        k2, n_dim = b.shape
    assert k_dim == k2, (a.shape, b.shape, ta, tb)
    tm = _pick_wide(m_dim)
    tn = _pick_wide(n_dim)
    tk = _pick_wide(k_dim)
    nk = k_dim // tk
    n_add = len(add)
    dims = TN if ta else (NT if tb else NN)
    assert not (ta and tb)

    def body(a_ref, b_ref, *rest):
        add_refs = rest[:n_add]
        o_ref = rest[n_add]
        acc = rest[n_add + 1]
        k = pl.program_id(2)

        @pl.when(k == 0)
        def _():
            acc[...] = jnp.zeros_like(acc)

        acc[...] += _dot(a_ref[...], b_ref[...], dims)

        @pl.when(k == nk - 1)
        def _():
            r = acc[...]
            for ar in add_refs:
                r = r + ar[...].astype(F32)
            o_ref[...] = r.astype(o_ref.dtype)

    a_spec = pl.BlockSpec((tk, tm), lambda i, j, k: (k, i)) if ta else pl.BlockSpec((tm, tk), lambda i, j, k: (i, k))
    b_spec = pl.BlockSpec((tn, tk), lambda i, j, k: (j, k)) if tb else pl.BlockSpec((tk, tn), lambda i, j, k: (k, j))
    o_spec = pl.BlockSpec((tm, tn), lambda i, j, k: (i, j))
    return pl.pallas_call(
        body,
        out_shape=jax.ShapeDtypeStruct((m_dim, n_dim), out_dtype),
        grid=(m_dim // tm, n_dim // tn, nk),
        in_specs=[a_spec, b_spec] + [o_spec] * n_add,
        out_specs=o_spec,
        scratch_shapes=[pltpu.VMEM((tm, tn), F32)],
        compiler_params=pltpu.CompilerParams(dimension_semantics=("parallel", "parallel", "arbitrary")),
        name=name,
    )(a, b, *add)


def _mm_resident(a, b, *, add=(), name):
    m_dim, k_dim = a.shape
    n_dim = b.shape[1]
    assert b.shape[0] == k_dim
    tm = min(DX_ROWS, m_dim)

    def body(a_ref, b_ref, *rest):
        r = _dot(a_ref[...], b_ref[...])
        for ar in rest[:-1]:
            r = r + ar[...]
        rest[-1][...] = r

    o_spec = pl.BlockSpec((tm, n_dim), lambda i: (i, 0))
    return pl.pallas_call(
        body,
        out_shape=jax.ShapeDtypeStruct((m_dim, n_dim), F32),
        grid=(m_dim // tm,),
        in_specs=[pl.BlockSpec((tm, k_dim), lambda i: (i, 0)),
                  pl.BlockSpec((k_dim, n_dim), lambda i: (0, 0), pipeline_mode=pl.Buffered(1))] + [o_spec] * len(add),
        out_specs=o_spec,
        compiler_params=pltpu.CompilerParams(dimension_semantics=("parallel",)),
        name=name,
    )(a, b, *add)


def _rowwise(fn, rows, consts, outs, accs=(), *, tm=256, name):
    rows = [r if isinstance(r, tuple) else (r, 0, r.shape[1]) for r in rows]
    s_dim = rows[0][0].shape[0]
    tm = min(tm, s_dim)
    assert s_dim % tm == 0 and all(arr.shape[0] == s_dim for arr, _, _ in rows)
    specs = [pl.BlockSpec((tm, width), functools.partial(lambda i, cb: (i, cb), cb=cb)) for _, cb, width in rows]
    args = [arr for arr, _, _ in rows]
    for c in consts:
        specs.append(pl.BlockSpec(c.shape, lambda i: (0, 0)))
        args.append(c)
    nr, nc, no = len(rows), len(consts), len(outs)
    flipped = [len(o) == 3 for o in outs]
    out_shape = [jax.ShapeDtypeStruct((o[0], s_dim) if t else (s_dim, o[0]), o[1]) for o, t in zip(outs, flipped)]
    out_specs = [pl.BlockSpec((o[0], tm), lambda i: (0, i)) if t else pl.BlockSpec((tm, o[0]), lambda i: (i, 0))
                 for o, t in zip(outs, flipped)]
    out_shape += [jax.ShapeDtypeStruct(sh, F32) for sh in accs]
    out_specs += [pl.BlockSpec(sh, lambda i: (0, 0)) for sh in accs]

    def body(*refs):
        r = [x[...].astype(F32) if x.dtype == BF16 else x[...] for x in refs[:nr]]
        c = [x[...] for x in refs[nr:nr + nc]]
        o_refs = refs[nr + nc:nr + nc + no]
        a_refs = refs[nr + nc + no:]
        o_vals, a_vals = fn(r, c)
        for ref, v, t in zip(o_refs, o_vals, flipped, strict=True):
            ref[...] = (jnp.transpose(v.astype(F32)) if t else v).astype(ref.dtype)
        if a_refs:
            @pl.when(pl.program_id(0) == 0)
            def _():
                for ref in a_refs:
                    ref[...] = jnp.zeros_like(ref)

            for ref, v in zip(a_refs, a_vals, strict=True):
                ref[...] += v

    res = pl.pallas_call(
        body,
        out_shape=out_shape,
        grid=(s_dim // tm,),
        in_specs=specs,
        out_specs=out_specs,
        compiler_params=pltpu.CompilerParams(dimension_semantics=("arbitrary" if accs else "parallel",)),
        name=name,
    )(*args)
    return res


def _colsum(v):
    return jnp.sum(v, axis=0, keepdims=True)


def _rowsum(v):
    return jnp.sum(v, axis=1, keepdims=True)


def _rowmean(v):
    return jnp.mean(v, axis=1, keepdims=True)


def _silu_grad(x):
    s = _sig(x)
    return s * (1.0 + x * (1.0 - s))


def _conv_taps(x, w, width=4):
    row = lax.broadcasted_iota(jnp.int32, x.shape, 0)
    c = x * w[width - 1:width, :]
    for s in range(1, width):
        c = c + jnp.where(row >= s, pltpu.roll(x, s, 0), 0.0) * w[width - 1 - s:width - s, :]
    return c


def _conv_fwd(proj_a, conv_w):
    s_dim = proj_a.shape[0]
    n_blk = 3 * N_HEADS

    def body(x_ref, w_ref, o_ref):
        j = pl.program_id(0)
        c = _conv_taps(x_ref[...], w_ref[...])
        y = c * _sig(c)
        r = lax.rsqrt(_rowsum(y * y) + EPS_RMS)
        fac = jnp.where(j < N_HEADS, r * (HEAD ** -0.5), jnp.where(j < 2 * N_HEADS, r, 1.0))
        o_ref[...] = y * fac

    return pl.pallas_call(
        body,
        out_shape=jax.ShapeDtypeStruct((s_dim, n_blk * HEAD), F32),
        grid=(n_blk,),
        in_specs=[pl.BlockSpec((s_dim, HEAD), lambda j: (0, j)), pl.BlockSpec((4, HEAD), lambda j: (0, j))],
        out_specs=pl.BlockSpec((s_dim, HEAD), lambda j: (0, j)),
        compiler_params=pltpu.CompilerParams(dimension_semantics=("parallel",)),
        name="conv_fwd",
    )(proj_a, conv_w)


def _conv_bwd(proj_a, conv_w, dq, dk, dv):
    s_dim = proj_a.shape[0]
    n_blk = 3 * N_HEADS

    def body(x_ref, w_ref, dq_ref, dk_ref, dv_ref, dx_ref, dw_ref):
        j = pl.program_id(0)
        x = x_ref[...]
        w = w_ref[...]
        do = jnp.where(j < N_HEADS, dq_ref[...], jnp.where(j < 2 * N_HEADS, dk_ref[...], dv_ref[...]))
        c = _conv_taps(x, w)
        sg = _sig(c)
        y = c * sg
        r = lax.rsqrt(_rowsum(y * y) + EPS_RMS)
        sc = jnp.where(j < N_HEADS, HEAD ** -0.5, 1.0)
        dy_n = sc * (r * do - y * (r * r * r) * _rowsum(do * y))
        dy = jnp.where(j < 2 * N_HEADS, dy_n, do)
        dc = dy * (sg * (1.0 + c * (1.0 - sg)))
        row = lax.broadcasted_iota(jnp.int32, x.shape, 0)
        dx = dc * w[3:4, :]
        dw_ref[3:4, :] = _colsum(dc * x)
        for s in range(1, 4):
            dx = dx + jnp.where(row < s_dim - s, pltpu.roll(dc, s_dim - s, 0), 0.0) * w[3 - s:4 - s, :]
            xs = jnp.where(row >= s, pltpu.roll(x, s, 0), 0.0)
            dw_ref[3 - s:4 - s, :] = _colsum(dc * xs)
        dx_ref[...] = dx.astype(dx_ref.dtype)

    hd = N_HEADS - 1
    return pl.pallas_call(
        body,
        out_shape=[jax.ShapeDtypeStruct((s_dim, n_blk * HEAD), BF16), jax.ShapeDtypeStruct((4, n_blk * HEAD), F32)],
        grid=(n_blk,),
        in_specs=[
            pl.BlockSpec((s_dim, HEAD), lambda j: (0, j)),
            pl.BlockSpec((4, HEAD), lambda j: (0, j)),
            pl.BlockSpec((s_dim, HEAD), lambda j: (0, jnp.minimum(j, hd))),
            pl.BlockSpec((s_dim, HEAD), lambda j: (0, jnp.clip(j - N_HEADS, 0, hd))),
            pl.BlockSpec((s_dim, HEAD), lambda j: (0, jnp.clip(j - 2 * N_HEADS, 0, hd))),
        ],
        out_specs=[pl.BlockSpec((s_dim, HEAD), lambda j: (0, j)), pl.BlockSpec((4, HEAD), lambda j: (0, j))],
        compiler_params=pltpu.CompilerParams(dimension_semantics=("parallel",)),
        name="conv_bwd",
    )(proj_a, conv_w, dq, dk, dv)


def _chunk_tri(n):
    r = np.arange(n)
    m = ((r[:, None] // CHUNK) == (r[None, :] // CHUNK)) & (r[:, None] >= r[None, :])
    m = m.astype(np.float32)
    return jnp.asarray(m), jnp.asarray(m.T)


def _softplus(z):
    return jnp.maximum(z, 0.0) + jnp.log(1.0 + jnp.exp(-jnp.abs(z)))


def _gates_fwd(proj_b, alog, dtb):
    tm = min(GROUP, proj_b.shape[0])
    tri, _ = _chunk_tri(tm)

    def fn(r, c):
        b = r[0]
        a = pltpu.roll(b, LANES - N_HEADS, 1)
        alog_, dtb_, tri_ = c
        g = -jnp.exp(alog_) * _softplus(a + dtb_)
        return [_sig(b), _dot32(tri_, g)], []

    return _rowwise(fn, [(proj_b, WB_BA // LANES, LANES)], [alog, dtb, tri],
                    [(LANES, F32), (LANES, F32)], tm=tm, name="gates_fwd")


def _gates_bwd(proj_b, alog, dtb, gc, d_beta, d_gc, d_egl_rows):
    tm = min(GROUP, proj_b.shape[0])
    _, tri_t = _chunk_tri(tm)

    def fn(r, c):
        b, gc_, d_beta_, d_gc_, d_egl_ = r
        a = pltpu.roll(b, LANES - N_HEADS, 1)
        alog_, dtb_, tri_t_ = c
        z = a + dtb_
        ea = jnp.exp(alog_)
        g = -ea * _softplus(z)
        dg = _dot32(tri_t_, d_gc_ + d_egl_ * jnp.exp(gc_))
        d_a = dg * (-ea) * _sig(z)
        beta = _sig(b)
        d_ba = d_beta_ * beta * (1.0 - beta) + pltpu.roll(d_a, N_HEADS, 1)
        return [d_ba], [_colsum(dg * g), _colsum(d_a)]

    return _rowwise(fn, [(proj_b, WB_BA // LANES, LANES), gc, d_beta, d_gc, d_egl_rows],
                    [alog, dtb, tri_t], [(LANES, BF16)], accs=[(1, LANES), (1, LANES)], tm=tm,
                    name="gates_bwd")


def _group_masks(n):
    r = lax.broadcasted_iota(jnp.int32, (n, n), 0)
    c = lax.broadcasted_iota(jnp.int32, (n, n), 1)
    same = (r // CHUNK) == (c // CHUNK)
    below, s = [], 2
    while s < CHUNK:
        below.append(jnp.logical_and((r // (2 * s)) == (c // (2 * s)),
                                     jnp.logical_and((r // s) % 2 == 1, (c // s) % 2 == 0)))
        s *= 2
    return dict(same=same, tril=jnp.logical_and(same, r >= c), strict=jnp.logical_and(same, r > c),
                last=c == (r // CHUNK) * CHUNK + (CHUNK - 1), eye=r == c, pair=(r // 2) == (c // 2), below=below)


def _inv_unit_lower(l_mats, mk):
    eye_f = mk["eye"].astype(F32)
    ts = [eye_f - jnp.where(mk["pair"], l_mat, 0.0) for l_mat in l_mats]
    for below in mk["below"]:
        halves = [_split_bf16(t) for t in ts]
        mids = [_dot3(h, jnp.where(below, l_mat, 0.0)) for h, l_mat in zip(halves, l_mats)]
        ts = [t - _dot3(m, h) for t, m, h in zip(ts, mids, halves)]
    return ts


def _unfold_blocks(folded, mask):
    n = folded.shape[0]
    return jnp.where(mask, jnp.concatenate([folded] * (n // CHUNK), axis=1), 0.0)


def _head_cols(beta, gc, gc_t, h):
    lane = lax.broadcasted_iota(jnp.int32, beta.shape, 1)
    sub = lax.broadcasted_iota(jnp.int32, gc_t.shape, 0)
    bcol = _rowsum(jnp.where(lane == h, beta, 0.0))
    gcol = _rowsum(jnp.where(lane == h, gc, 0.0))
    grow = _colsum(jnp.where(sub == h, gc_t, 0.0))
    return bcol, gcol, grow


def _prep_common(q, k, bcol, gcol, grow, mk, t_folded=None):
    n = q.shape[0]
    tril = mk["tril"]
    decay = jnp.where(tril, jnp.exp(jnp.where(tril, gcol - grow, 0.0)), 0.0)
    glast = _rowsum(jnp.where(mk["last"], jnp.broadcast_to(grow, (n, n)), 0.0))
    e = jnp.exp(gcol)
    ekt = jnp.exp(glast - gcol)
    kb = k * bcol
    kk = _dot(kb, k, NT)
    qk = _dot(q, k, NT)
    p = dict(decay=decay, e=e, ekt=ekt, kb=kb, kk=kk, qk=qk)
    if t_folded is not None:
        p["t"] = _unfold_blocks(t_folded, mk["same"])
    return p


GROUPS_PER_STEP = 4
SCAN_CHUNKS_PER_STEP = 4


def _fold_blocks(m):
    n = m.shape[0]
    out = m[:, 0:CHUNK]
    for b in range(1, n // CHUNK):
        out = out + m[:, b * CHUNK:(b + 1) * CHUNK]
    return out


def _gdr_prep_fwd(qkvn, beta, gc, gc_t):
    s_dim = qkvn.shape[0]
    tg = min(GROUP, s_dim)
    n_sub = min(GROUPS_PER_STEP, s_dim // tg)
    tb = tg * n_sub

    def body(q_ref, k_ref, v_ref, b_ref, g_ref, gt_ref, u_ref, w_ref, qd_ref, kt_ref, a_ref, t_ref):
        h = pl.program_id(0)
        mk = _group_masks(tg)
        parts = []
        for s in range(n_sub):
            rows = slice(s * tg, (s + 1) * tg)
            q, k, v = q_ref[rows, :], k_ref[rows, :], v_ref[rows, :]
            bcol, gcol, grow = _head_cols(b_ref[rows, :], g_ref[rows, :], gt_ref[:, rows], h)
            p = _prep_common(q, k, bcol, gcol, grow, mk)
            qd_ref[rows, :] = q * p["e"]
            kt_ref[rows, :] = k * p["ekt"]
            a_ref[rows, :] = _fold_blocks(jnp.where(mk["tril"], p["qk"] * p["decay"], 0.0))
            parts.append((rows, v * bcol, p["kb"] * p["e"], jnp.where(mk["strict"], p["kk"] * p["decay"], 0.0)))
        t_mats = _inv_unit_lower([part[3] for part in parts], mk)
        for (rows, vb, kbe, _), t_mat in zip(parts, t_mats):
            u_ref[rows, :] = _dot(t_mat, vb)
            w_ref[rows, :] = _dot(t_mat, kbe)
            t_ref[rows, :] = _fold_blocks(t_mat)

    row = lambda off: pl.BlockSpec((tb, HEAD), functools.partial(lambda h, m, off: (m, h + off), off=off))
    full = pl.BlockSpec((tb, LANES), lambda h, m: (m, 0))
    o_spec = pl.BlockSpec((tb, HEAD), lambda h, m: (m, h))
    a_spec = pl.BlockSpec((None, tb, CHUNK), lambda h, m: (h, m, 0))
    wide = jax.ShapeDtypeStruct((s_dim, N_HEADS * HEAD), F32)
    folded = jax.ShapeDtypeStruct((N_HEADS, s_dim, CHUNK), F32)
    return pl.pallas_call(
        body,
        out_shape=[wide, wide, wide, wide, folded, folded],
        grid=(N_HEADS, s_dim // tb),
        in_specs=[row(0), row(N_HEADS), row(2 * N_HEADS), full, full, pl.BlockSpec((8, tb), lambda h, m: (0, m))],
        out_specs=[o_spec, o_spec, o_spec, o_spec, a_spec, a_spec],
        compiler_params=pltpu.CompilerParams(dimension_semantics=("parallel", "parallel")),
        name="gdr_prep_fwd",
    )(qkvn, qkvn, qkvn, beta, gc, gc_t)


def _gdr_prep_bwd(qkvn, beta, gc, gc_t, t_fold, u, w, du, dw, dqd, dkt, d_a):
    s_dim = qkvn.shape[0]
    tg = min(GROUP, s_dim)
    n_sub = min(GROUPS_PER_STEP, s_dim // tg)
    tb = tg * n_sub

    def body(q_ref, k_ref, v_ref, b_ref, g_ref, gt_ref, t_ref, u_ref, w_ref, du_ref, dw_ref, dqd_ref, dkt_ref,
             da_ref, dq_ref, dk_ref, dv_ref, db_ref, dg_ref):
        h = pl.program_id(1)

        @pl.when(h == 0)
        def _():
            db_ref[...] = jnp.zeros_like(db_ref)
            dg_ref[...] = jnp.zeros_like(dg_ref)

        mk = _group_masks(tg)
        lane = lax.broadcasted_iota(jnp.int32, (tg, LANES), 1)
        for s in range(n_sub):
            rows = slice(s * tg, (s + 1) * tg)
            q, k, v = q_ref[rows, :], k_ref[rows, :], v_ref[rows, :]
            bcol, gcol, grow = _head_cols(b_ref[rows, :], g_ref[rows, :], gt_ref[:, rows], h)
            p = _prep_common(q, k, bcol, gcol, grow, mk, t_ref[rows, :])
            t_mat, decay, e, ekt, kb = p["t"], p["decay"], p["e"], p["ekt"], p["kb"]
            du_, dw_, dqd_, dkt_ = du_ref[rows, :], dw_ref[rows, :], dqd_ref[rows, :], dkt_ref[rows, :]
            dvb = _dot(t_mat, du_, TN)
            dkbe = _dot(t_mat, dw_, TN)
            d_l = -(_dot(dvb, u_ref[rows, :], NT) + _dot(dkbe, w_ref[rows, :], NT))
            m1 = jnp.where(mk["strict"], d_l, 0.0)
            m2 = _unfold_blocks(da_ref[rows, :], mk["tril"])
            d_kk = m1 * decay
            d_qk = m2 * decay
            d_decay = m1 * p["kk"] + m2 * p["qk"]
            dkb = _dot(d_kk, k) + dkbe * e
            dk = _dot(d_kk, kb, TN) + _dot(d_qk, q, TN) + dkt_ * ekt + dkb * bcol
            dq = _dot(d_qk, k) + dqd_ * e
            d_beta = _rowsum(dkb * k) + _rowsum(dvb * v)
            d_e = _rowsum(dkbe * kb) + _rowsum(dqd_ * q)
            d_ekt = _rowsum(dkt_ * k) * ekt
            d_diff = d_decay * decay
            d_grow = -_colsum(d_diff) + _colsum(jnp.where(mk["last"], jnp.broadcast_to(d_ekt, (tg, tg)), 0.0))
            d_gcol = d_e * e - d_ekt + _rowsum(d_diff)
            d_gcol = d_gcol + _rowsum(jnp.where(mk["eye"], jnp.broadcast_to(d_grow, (tg, tg)), 0.0))
            dq_ref[rows, :] = dq
            dk_ref[rows, :] = dk
            dv_ref[rows, :] = dvb * bcol
            db_ref[rows, :] = jnp.where(lane == h, d_beta, db_ref[rows, :])
            dg_ref[rows, :] = jnp.where(lane == h, d_gcol, dg_ref[rows, :])

    row = lambda off: pl.BlockSpec((tb, HEAD), functools.partial(lambda m, h, off: (m, h + off), off=off))
    full = pl.BlockSpec((tb, LANES), lambda m, h: (m, 0))
    o_spec = pl.BlockSpec((tb, HEAD), lambda m, h: (m, h))
    a_spec = pl.BlockSpec((None, tb, CHUNK), lambda m, h: (h, m, 0))
    wide = jax.ShapeDtypeStruct((s_dim, N_HEADS * HEAD), F32)
    lanes = jax.ShapeDtypeStruct((s_dim, LANES), F32)
    return pl.pallas_call(
        body,
        out_shape=[wide, wide, wide, lanes, lanes],
        grid=(s_dim // tb, N_HEADS),
        in_specs=[row(0), row(N_HEADS), row(2 * N_HEADS), full, full, pl.BlockSpec((8, tb), lambda m, h: (0, m)),
                  a_spec, o_spec, o_spec, o_spec, o_spec, o_spec, o_spec, a_spec],
        out_specs=[o_spec, o_spec, o_spec, full, full],
        compiler_params=pltpu.CompilerParams(dimension_semantics=("parallel", "arbitrary")),
        name="gdr_prep_bwd",
    )(qkvn, qkvn, qkvn, beta, gc, gc_t, t_fold, u, w, du, dw, dqd, dkt, d_a)


def _gdr_scan_fwd(u, w, qd, kt, a_mat, gc):
    s_dim = u.shape[0]
    n_chunks = s_dim // CHUNK
    per = min(SCAN_CHUNKS_PER_STEP, n_chunks)
    tb = per * CHUNK

    def body(u_ref, w_ref, qd_ref, kt_ref, a_ref, g_ref, o_ref, st_ref, state):
        @pl.when(pl.program_id(0) == 0)
        def _():
            state[...] = jnp.zeros_like(state)

        heads = range(N_HEADS)
        cols = [slice(h * HEAD, (h + 1) * HEAD) for h in heads]
        for i in range(per):
            rows = slice(i * CHUNK, (i + 1) * CHUNK)
            egl = jnp.exp(g_ref[(i + 1) * CHUNK - 1:(i + 1) * CHUNK, :])
            s_b = [state[h].astype(BF16) for h in heads]
            for h in heads:
                st_ref[i, h] = state[h]
            ws = [_dot(w_ref[rows, cs], s) for cs, s in zip(cols, s_b)]
            qs = [_dot(qd_ref[rows, cs], s) for cs, s in zip(cols, s_b)]
            vns = [(u_ref[rows, cs] - ws_h).astype(BF16) for cs, ws_h in zip(cols, ws)]
            avs = [_dot(a_ref[h, rows, :], vn) for h, vn in zip(heads, vns)]
            kvs = [_dot(kt_ref[rows, cs], vn, TN) for cs, vn in zip(cols, vns)]
            for h, cs in zip(heads, cols):
                o_ref[rows, cs] = qs[h] + avs[h]
                state[h] = state[h] * egl[:, h:h + 1] + kvs[h]

    wide = pl.BlockSpec((tb, N_HEADS * HEAD), lambda n: (n, 0))
    return pl.pallas_call(
        body,
        out_shape=[jax.ShapeDtypeStruct((s_dim, N_HEADS * HEAD), F32),
                   jax.ShapeDtypeStruct((n_chunks, N_HEADS, HEAD, HEAD), F32)],
        grid=(n_chunks // per,),
        in_specs=[wide, wide, wide, wide, pl.BlockSpec((N_HEADS, tb, CHUNK), lambda n: (0, n, 0)),
                  pl.BlockSpec((tb, LANES), lambda n: (n, 0))],
        out_specs=[wide, pl.BlockSpec((per, N_HEADS, HEAD, HEAD), lambda n: (n, 0, 0, 0))],
        scratch_shapes=[pltpu.VMEM((N_HEADS, HEAD, HEAD), F32)],
        compiler_params=pltpu.CompilerParams(dimension_semantics=("arbitrary",)),
        name="gdr_scan_fwd",
    )(u, w, qd, kt, a_mat, gc)


def _gdr_scan_bwd(u, w, qd, kt, a_mat, gc, states, d_o):
    s_dim = u.shape[0]
    n_chunks = s_dim // CHUNK
    per = min(SCAN_CHUNKS_PER_STEP, n_chunks)
    tb = per * CHUNK
    last = n_chunks // per - 1

    def body(u_ref, w_ref, qd_ref, kt_ref, a_ref, g_ref, st_ref, do_ref,
             du_ref, dw_ref, dqd_ref, dkt_ref, da_ref, de_ref, d_state):
        @pl.when(pl.program_id(0) == 0)
        def _():
            d_state[...] = jnp.zeros_like(d_state)

        heads = range(N_HEADS)
        cols = [slice(h * HEAD, (h + 1) * HEAD) for h in heads]
        for i in reversed(range(per)):
            rows = slice(i * CHUNK, (i + 1) * CHUNK)
            egl = jnp.exp(g_ref[(i + 1) * CHUNK - 1:(i + 1) * CHUNK, :])
            s_b = [st_ref[i, h].astype(BF16) for h in heads]
            ds_b = [d_state[h].astype(BF16) for h in heads]
            dos = [do_ref[rows, cs].astype(BF16) for cs in cols]
            w_b = [w_ref[rows, cs].astype(BF16) for cs in cols]
            ws = [_dot(w_h, s) for w_h, s in zip(w_b, s_b)]
            ados = [_dot(a_ref[h, rows, :], do, TN) for h, do in zip(heads, dos)]
            kds = [_dot(kt_ref[rows, cs], ds) for cs, ds in zip(cols, ds_b)]
            dqds = [_dot(do, s, NT) for do, s in zip(dos, s_b)]
            qdos = [_dot(qd_ref[rows, cs], do, TN) for cs, do in zip(cols, dos)]
            vns = [(u_ref[rows, cs] - ws_h).astype(BF16) for cs, ws_h in zip(cols, ws)]
            dvns = [a + k_ for a, k_ in zip(ados, kds)]
            dvn_b = [d.astype(BF16) for d in dvns]
            das = [_dot(do, vn, NT) for do, vn in zip(dos, vns)]
            dkts = [_dot(vn, ds, NT) for vn, ds in zip(vns, ds_b)]
            dws = [_dot(d, s, NT) for d, s in zip(dvn_b, s_b)]
            wds = [_dot(w_h, d, TN) for w_h, d in zip(w_b, dvn_b)]
            for h, cs in zip(heads, cols):
                ds_n = d_state[h]
                de = jnp.sum(_rowsum(ds_n * st_ref[i, h]), axis=0, keepdims=True)
                de_ref[i, h:h + 1, :] = jnp.broadcast_to(de, (1, LANES))
                dqd_ref[rows, cs] = dqds[h]
                da_ref[h, rows, :] = das[h]
                dkt_ref[rows, cs] = dkts[h]
                du_ref[rows, cs] = dvns[h]
                dw_ref[rows, cs] = -dws[h]
                d_state[h] = ds_n * egl[:, h:h + 1] + qdos[h] - wds[h]

    wide = pl.BlockSpec((tb, N_HEADS * HEAD), lambda n: (last - n, 0))
    a_spec = pl.BlockSpec((N_HEADS, tb, CHUNK), lambda n: (0, last - n, 0))
    wide_shape = jax.ShapeDtypeStruct((s_dim, N_HEADS * HEAD), F32)
    return pl.pallas_call(
        body,
        out_shape=[wide_shape, wide_shape, wide_shape, wide_shape,
                   jax.ShapeDtypeStruct((N_HEADS, s_dim, CHUNK), F32),
                   jax.ShapeDtypeStruct((n_chunks, N_HEADS, LANES), F32)],
        grid=(n_chunks // per,),
        in_specs=[wide, wide, wide, wide, a_spec, pl.BlockSpec((tb, LANES), lambda n: (last - n, 0)),
                  pl.BlockSpec((per, N_HEADS, HEAD, HEAD), lambda n: (last - n, 0, 0, 0)), wide],
        out_specs=[wide, wide, wide, wide, a_spec, pl.BlockSpec((per, N_HEADS, LANES), lambda n: (last - n, 0, 0))],
        scratch_shapes=[pltpu.VMEM((N_HEADS, HEAD, HEAD), F32)],
        compiler_params=pltpu.CompilerParams(dimension_semantics=("arbitrary",)),
        name="gdr_scan_bwd",
    )(u, w, qd, kt, a_mat, gc, states, d_o)


FUSED_ROWS = 512


def _gdr_out_fwd(o_dn, proj_a, dn_w, w_br):
    def fn(r, c):
        o, z = r
        w_, w_br_ = c
        outs = []
        for h in range(N_HEADS):
            cs = slice(h * HEAD, (h + 1) * HEAD)
            oh, zh = o[:, cs], z[:, cs]
            rr = lax.rsqrt(_rowmean(oh * oh) + EPS_RMS)
            outs.append(oh * rr * w_ * (zh * _sig(zh)))
        og = jnp.concatenate(outs, axis=1).astype(BF16)
        return [og, _dot(og, w_br_)], []

    return _rowwise(fn, [o_dn, (proj_a, 3, D_MODEL)], [dn_w, w_br], [(D_MODEL, BF16), (D_MODEL, BF16)],
                    tm=FUSED_ROWS, name="gdr_out_fwd")


def _gdr_out_bwd(o_dn, proj_a, d_y_dn, dn_w, w_br):
    def fn(r, c):
        o, z, dy = r
        w_, w_br_ = c
        dg = _dot(dy, w_br_, NT)
        d_o, d_z = [], []
        d_w = jnp.zeros((1, HEAD), F32)
        for h in range(N_HEADS):
            cs = slice(h * HEAD, (h + 1) * HEAD)
            oh, zh, dgh = o[:, cs], z[:, cs], dg[:, cs]
            rr = lax.rsqrt(_rowmean(oh * oh) + EPS_RMS)
            sz = zh * _sig(zh)
            d_n = dgh * sz
            d_z.append(dgh * (oh * rr * w_) * _silu_grad(zh))
            d_w = d_w + _colsum(d_n * oh * rr)
            gw = d_n * w_
            d_o.append(rr * gw - oh * (rr * rr * rr) * _rowmean(gw * oh))
        return [jnp.concatenate(d_o, axis=1), jnp.concatenate(d_z, axis=1)], [d_w]

    return _rowwise(fn, [o_dn, (proj_a, 3, D_MODEL), d_y_dn], [dn_w, w_br], [(D_MODEL, F32), (D_MODEL, BF16)],
                    accs=[(1, HEAD)], tm=FUSED_ROWS, name="gdr_out_bwd")


def _rms_fwd(x, w):
    r = lax.rsqrt(_rowmean(x * x) + EPS_RMS)
    return x * r * w


def _rms_bwd(x, w, dy):
    r = lax.rsqrt(_rowmean(x * x) + EPS_RMS)
    gw = dy * w
    return r * gw - x * (r * r * r) * _rowmean(gw * x), _colsum(dy * x * r)


def _rope_consts():
    inv = ROPE_BASE ** (-np.arange(0, ROPE, 2, dtype=np.float32) / ROPE)
    t = np.zeros((4, LANES), np.float32)
    t[0, :32] = inv
    t[0, 32:64] = inv
    t[1, :64] = 1.0
    t[2, 32:64] = 1.0
    t[3, :32] = -1.0
    return jnp.asarray(t)


def _rope_tables(pos, consts, width):
    ang = pos * consts[0:1, :]
    cosv, sinv = jnp.cos(ang), jnp.sin(ang)
    reps = width // LANES
    tile = (lambda t: jnp.concatenate([t] * reps, axis=1)) if reps > 1 else (lambda t: t)
    return tile(cosv * consts[1:2, :]), tile(sinv * consts[2:3, :]), tile(sinv * consts[3:4, :])


def _rope_apply(t, tabs):
    cos_t, sin_a, sin_b = tabs
    width = t.shape[1]
    return t * cos_t + pltpu.roll(t, 32, 1) * sin_a + pltpu.roll(t, width - 32, 1) * sin_b


def _rope_transpose(d, tabs):
    cos_t, sin_a, sin_b = tabs
    width = d.shape[1]
    return d * cos_t + pltpu.roll(d * sin_a, width - 32, 1) + pltpu.roll(d * sin_b, 32, 1)


QK_HEAD = 2 * HEAD


def _interleave_heads(a, b):
    parts = []
    for h in range(N_HEADS):
        parts.append(a[:, h * HEAD:(h + 1) * HEAD])
        parts.append(b if b.shape[1] == LANES else b[:, h * LANES:(h + 1) * LANES])
    return jnp.concatenate(parts, axis=1)


def _mla_rows(proj_b):
    return [(proj_b, WB_CQ // Q_LORA, Q_LORA), (proj_b, WB_CKV // KV_LORA, KV_LORA), (proj_b, WB_KR // LANES, LANES)]


def _mla_prep_fwd(proj_b, pos, qn_w, kvn_w, uq, uk, uv):
    def fn(r, c):
        cq, ckv, kr, pos_ = r
        qn_w_, kvn_w_, uq_, uk_, uv_, rope = c
        c_q = _rms_fwd(cq, qn_w_).astype(BF16)
        c_kv = _rms_fwd(ckv, kvn_w_).astype(BF16)
        qf = _dot(c_q, uq_)
        qr = _rope_apply(qf[:, D_MODEL:], _rope_tables(pos_, rope, D_MODEL))
        kr = _rope_apply(kr, _rope_tables(pos_, rope, LANES))
        kc = _interleave_heads(_dot(c_kv, uk_), kr)
        v = _dot(c_kv, uv_)
        return [c_q, c_kv, _interleave_heads(qf[:, :D_MODEL], qr) * SCALE, kc, v, kc, v], []

    wide2 = N_HEADS * QK_HEAD
    return _rowwise(fn, _mla_rows(proj_b) + [pos], [qn_w, kvn_w, uq, uk, uv, _rope_consts()],
                    [(Q_LORA, BF16), (KV_LORA, BF16), (wide2, BF16), (wide2, BF16), (D_MODEL, BF16),
                     (wide2, BF16, "T"), (D_MODEL, BF16, "T")], tm=FUSED_ROWS, name="mla_prep_fwd")


def _mla_prep_bwd(proj_b, pos, d_qc, d_kc, d_v, qn_w, kvn_w, uq, uk, uv):
    def fn(r, c):
        cq, ckv, _, pos_, dq, dk, dv = r
        qn_w_, kvn_w_, uq_, uk_, uv_, rope = c
        even = lambda t: jnp.concatenate([t[:, (2 * h) * LANES:(2 * h + 1) * LANES] for h in range(N_HEADS)], axis=1)
        odd = lambda t: jnp.concatenate([t[:, (2 * h + 1) * LANES:(2 * h + 2) * LANES] for h in range(N_HEADS)], axis=1)
        d_qr_raw = _rope_transpose(odd(dq), _rope_tables(pos_, rope, D_MODEL)) * SCALE
        d_qf = jnp.concatenate([even(dq) * SCALE, d_qr_raw], axis=1).astype(BF16)
        d_kn = even(dk).astype(BF16)
        dkr = dk[:, LANES:2 * LANES]
        for h in range(1, N_HEADS):
            dkr = dkr + dk[:, (2 * h + 1) * LANES:(2 * h + 2) * LANES]
        d_cq, d_qnw = _rms_bwd(cq, qn_w_, _dot(d_qf, uq_, NT))
        d_ckv, d_kvnw = _rms_bwd(ckv, kvn_w_, _dot(d_kn, uk_, NT) + _dot(dv, uv_, NT))
        return [d_qf, d_kn, d_cq, d_ckv, _rope_transpose(dkr, _rope_tables(pos_, rope, LANES))], [d_qnw, d_kvnw]

    return _rowwise(fn, _mla_rows(proj_b) + [pos, d_qc, d_kc, d_v], [qn_w, kvn_w, uq, uk, uv, _rope_consts()],
                    [(2 * D_MODEL, BF16), (D_MODEL, BF16), (Q_LORA, BF16), (KV_LORA, BF16), (LANES, BF16)],
                    accs=[(1, Q_LORA), (1, KV_LORA)], tm=FUSED_ROWS, name="mla_prep_bwd")


def _causal_mask_t(st, key0, query0):
    key = lax.broadcasted_iota(jnp.int32, st.shape, 0) + key0
    query = lax.broadcasted_iota(jnp.int32, st.shape, 1) + query0
    return jnp.where(key <= query, st, NEG_BIG)


def _attn_tiles(s_dim):
    tq = min(512, s_dim)
    n_chains = 2 if s_dim >= 2 * tq else 1
    return tq, n_chains, min(512, s_dim)


def _diagonal_chains(t, tq, n_chains, tk):
    return [(c, (t + 1) * tk - 1 > c * tq) for c in range(n_chains) if t * tk < (c + 1) * tq]


def _attn_fwd(qc, kc, vt):
    s_dim = qc.shape[0]
    tq, n_chains, tk = _attn_tiles(s_dim)
    tqs = tq * n_chains

    def body(q_ref, k_ref, vt_ref, o_ref, lse_ref, m_s, l_s, acc):
        qi = pl.program_id(1)
        m_s[...] = jnp.full_like(m_s, NEG_BIG)
        l_s[...] = jnp.zeros_like(l_s)
        acc[...] = jnp.zeros_like(acc)

        def make_step(chains):
            def step(j, carry):
                ks = pl.multiple_of(j * tk, tk)
                kb, vtb = k_ref[pl.ds(ks, tk), :], vt_ref[:, pl.ds(ks, tk)]
                cols = [slice(c * tq, (c + 1) * tq) for c, _ in chains]
                sts = [_dot(kb, q_ref[cs, :], NT) for cs in cols]
                sts = [_causal_mask_t(st, j * tk, qi * tqs + c * tq) if masked else st
                       for st, (c, masked) in zip(sts, chains)]
                m_prevs = [m_s[:, cs] for cs in cols]
                m_news = [jnp.maximum(mp, jnp.max(st, axis=0, keepdims=True)) for mp, st in zip(m_prevs, sts)]
                alphas = [jnp.exp(mp - mn) for mp, mn in zip(m_prevs, m_news)]
                pts = [jnp.exp(st - mn) for st, mn in zip(sts, m_news)]
                pvs = [_dot(vtb, pt) for pt in pts]
                for cs, mn, al, pt, pv in zip(cols, m_news, alphas, pts, pvs):
                    l_s[:, cs] = al * l_s[:, cs] + _colsum(pt)
                    m_s[:, cs] = mn
                    acc[:, cs] = acc[:, cs] * al + pv
                return carry
            return step

        below = qi * (tqs // tk)
        lax.fori_loop(0, below, make_step([(c, False) for c in range(n_chains)]), 0)
        for t in range(tqs // tk):
            make_step(_diagonal_chains(t, tq, n_chains, tk))(below + t, 0)
        l = l_s[...]
        o_ref[...] = jnp.transpose(acc[...] / l)
        lse_ref[...] = m_s[...] + jnp.log(l)

    return pl.pallas_call(
        body,
        out_shape=[jax.ShapeDtypeStruct((s_dim, N_HEADS * HEAD), F32), jax.ShapeDtypeStruct((N_HEADS, 1, s_dim), F32)],
        grid=(N_HEADS, s_dim // tqs),
        in_specs=[pl.BlockSpec((tqs, QK_HEAD), lambda h, qi: (qi, h)),
                  pl.BlockSpec((s_dim, QK_HEAD), lambda h, qi: (0, h)),
                  pl.BlockSpec((HEAD, s_dim), lambda h, qi: (h, 0))],
        out_specs=[pl.BlockSpec((tqs, HEAD), lambda h, qi: (qi, h)),
                   pl.BlockSpec((None, 1, tqs), lambda h, qi: (h, 0, qi))],
        scratch_shapes=[pltpu.VMEM((1, tqs), F32), pltpu.VMEM((1, tqs), F32), pltpu.VMEM((HEAD, tqs), F32)],
        compiler_params=pltpu.CompilerParams(dimension_semantics=("parallel", "parallel")),
        name="attn_fwd",
    )(qc, kc, vt)


def _attn_bwd(qc, kc, kct, v, o, d_o, lse):
    s_dim = qc.shape[0]
    tq, n_chains, tk = _attn_tiles(s_dim)
    tqs = tq * n_chains

    def body(q_ref, k_ref, kt_ref, v_ref, o_ref, do_ref, lse_ref, dq_ref, dk_ref, dv_ref, dqt_acc, dv_acc):
        qi = pl.program_id(1)

        @pl.when(qi == 0)
        def _():
            dk_ref[...] = jnp.zeros_like(dk_ref)
            dv_acc[...] = jnp.zeros_like(dv_acc)

        dqt_acc[...] = jnp.zeros_like(dqt_acc)
        do_f = do_ref[...]
        do_all = do_f.astype(BF16)
        q_all = q_ref[...]
        lse_row = lse_ref[...]
        delta_row = _dot3(jnp.ones((8, HEAD), F32), o_ref[...] * do_f, NT)[0:1, :]

        def make_step(chains):
            rows = slice(chains[0][0] * tq, (chains[-1][0] + 1) * tq)

            def step(j, carry):
                ks = pl.multiple_of(j * tk, tk)
                kb, vb, ktb = k_ref[pl.ds(ks, tk), :], v_ref[pl.ds(ks, tk), :], kt_ref[:, pl.ds(ks, tk)]
                cols = [slice(c * tq, (c + 1) * tq) for c, _ in chains]
                sts = [_dot(kb, q_all[cs, :], NT) for cs in cols]
                sts = [_causal_mask_t(st, j * tk, qi * tqs + c * tq) if masked else st
                       for st, (c, masked) in zip(sts, chains)]
                dpts = [_dot(vb, do_all[cs, :], NT) for cs in cols]
                pts = [jnp.exp(st - lse_row[:, cs]) for st, cs in zip(sts, cols)]
                dsts = [(pt * (dpt - delta_row[:, cs])).astype(BF16) for pt, dpt, cs in zip(pts, dpts, cols)]
                pts = [pt.astype(BF16) for pt in pts]
                dqs = [_dot(ktb, dst) for dst in dsts]
                for cs, dq in zip(cols, dqs):
                    dqt_acc[:, cs] += dq
                pt_all = jnp.concatenate(pts, axis=1) if len(chains) > 1 else pts[0]
                dst_all = jnp.concatenate(dsts, axis=1) if len(chains) > 1 else dsts[0]
                dk_ref[pl.ds(ks, tk), :] += _dot(dst_all, q_all[rows, :])
                dv_acc[pl.ds(ks, tk), :] += _dot(pt_all, do_all[rows, :])
                return carry
            return step

        below = qi * (tqs // tk)
        lax.fori_loop(0, below, make_step([(c, False) for c in range(n_chains)]), 0)
        for t in range(tqs // tk):
            make_step(_diagonal_chains(t, tq, n_chains, tk))(below + t, 0)
        dq_ref[...] = jnp.transpose(dqt_acc[...])

        @pl.when(qi == s_dim // tqs - 1)
        def _():
            dv_ref[...] = dv_acc[...].astype(dv_ref.dtype)

    q_spec = pl.BlockSpec((tqs, QK_HEAD), lambda h, qi: (qi, h))
    o_spec = pl.BlockSpec((tqs, HEAD), lambda h, qi: (qi, h))
    k_spec = pl.BlockSpec((s_dim, QK_HEAD), lambda h, qi: (0, h))
    v_spec = pl.BlockSpec((s_dim, HEAD), lambda h, qi: (0, h))
    wide2 = jax.ShapeDtypeStruct((s_dim, N_HEADS * QK_HEAD), F32)
    return pl.pallas_call(
        body,
        out_shape=[wide2, wide2, jax.ShapeDtypeStruct((s_dim, N_HEADS * HEAD), BF16)],
        grid=(N_HEADS, s_dim // tqs),
        in_specs=[q_spec, k_spec, pl.BlockSpec((QK_HEAD, s_dim), lambda h, qi: (h, 0)), v_spec, o_spec, o_spec,
                  pl.BlockSpec((None, 1, tqs), lambda h, qi: (h, 0, qi))],
        out_specs=[q_spec, k_spec, v_spec],
        scratch_shapes=[pltpu.VMEM((QK_HEAD, tqs), F32), pltpu.VMEM((s_dim, HEAD), F32)],
        compiler_params=pltpu.CompilerParams(dimension_semantics=("parallel", "arbitrary")),
        name="attn_bwd",
    )(qc, kc, kct, v, o, d_o, lse)


def _mix_proj_ln1(y_dn, y_mla, proj_g, x, w_o, g, b):
    s_dim = x.shape[0]
    tm = min(512, s_dim)

    def body(yd_ref, ym_ref, g_ref, x_ref, w_ref, lg_ref, lb_ref, mixed_ref, a1_ref, h1_ref, h1b_ref):
        gates = g_ref[...].astype(F32)
        mixed = (_sig(gates[:, :D_MODEL]) * yd_ref[...].astype(F32)
                 + _sig(gates[:, D_MODEL:]) * ym_ref[...].astype(F32)).astype(BF16)
        a1 = _dot(mixed, w_ref[...])
        xh, _ = _ln_stats(ALPHA * x_ref[...] + a1)
        y = xh * lg_ref[...] + lb_ref[...]
        mixed_ref[...] = mixed
        a1_ref[...] = a1
        h1_ref[...] = y
        h1b_ref[...] = y.astype(BF16)

    row = lambda width: pl.BlockSpec((tm, width), lambda i: (i, 0))
    whole = lambda a: pl.BlockSpec(a.shape, lambda i: (0, 0))
    sds = lambda dt: jax.ShapeDtypeStruct((s_dim, D_MODEL), dt)
    return pl.pallas_call(
        body,
        out_shape=[sds(BF16), sds(F32), sds(F32), sds(BF16)],
        grid=(s_dim // tm,),
        in_specs=[row(D_MODEL), row(D_MODEL), row(2 * D_MODEL), row(D_MODEL), whole(w_o), whole(g), whole(b)],
        out_specs=[row(D_MODEL)] * 4,
        compiler_params=pltpu.CompilerParams(dimension_semantics=("parallel",)),
        name="mix_proj_ln1",
    )(y_dn, y_mla, proj_g, x, w_o, g, b)


def _ln1_mix_bwd(x, a1, d_h1, d_pg, y_dn, y_mla, proj_g, g, w_o, w_pg):
    def fn(r, c):
        x_, a1_, dy, dpg, yd, ym, gates = r
        g_, w_o_, w_pg_ = c
        dy = dy + _dot(dpg, w_pg_, NT)
        xh, rr = _ln_stats(ALPHA * x_ + a1_)
        dz = _ln_bwd(dy, xh, rr, g_)
        dz_b = dz.astype(BF16)
        dm = _dot(dz_b, w_o_, NT)
        sd, sm = _sig(gates[:, :D_MODEL]), _sig(gates[:, D_MODEL:])
        d_g = jnp.concatenate([dm * yd * sd * (1.0 - sd), dm * ym * sm * (1.0 - sm)], axis=1)
        return [dz_b, ALPHA * dz, d_g, dm * sd, dm * sm], [_colsum(dy * xh), _colsum(dy)]

    return _rowwise(fn, [x, a1, d_h1, d_pg, y_dn, y_mla, proj_g], [g, w_o, w_pg],
                    [(D_MODEL, BF16), (D_MODEL, F32), (2 * D_MODEL, BF16), (D_MODEL, BF16), (D_MODEL, BF16)],
                    accs=[(1, D_MODEL), (1, D_MODEL)], tm=FUSED_ROWS, name="ln1_mix_bwd")


def _ln_stats(z):
    mu = _rowmean(z)
    zc = z - mu
    r = lax.rsqrt(_rowmean(zc * zc) + EPS_LN)
    return zc * r, r


def _ln_bwd(dy, xh, r, g):
    dxh = dy * g
    return r * (dxh - _rowmean(dxh) - xh * _rowmean(dxh * xh))


def _ffn_in_act(h1b, w_t):
    s_dim, k_dim = h1b.shape
    hidden = w_t.shape[0] // 2
    tm, tn = min(512, s_dim), _pick_wide(hidden)
    nt = hidden // tn

    def body(a_ref, bg_ref, bu_ref, gt_ref, up_ref, act_ref):
        a = a_ref[...]
        gt, up = _dot(a, bg_ref[...], NT), _dot(a, bu_ref[...], NT)
        gt_ref[...] = gt.astype(BF16)
        up_ref[...] = up.astype(BF16)
        act_ref[...] = (gt * _sig(gt) * up).astype(BF16)

    o_spec = pl.BlockSpec((tm, tn), lambda j, i: (i, j))
    sds = jax.ShapeDtypeStruct((s_dim, hidden), BF16)
    return pl.pallas_call(
        body,
        out_shape=[sds, sds, sds],
        grid=(nt, s_dim // tm),
        in_specs=[pl.BlockSpec((tm, k_dim), lambda j, i: (i, 0)), pl.BlockSpec((tn, k_dim), lambda j, i: (j, 0)),
                  pl.BlockSpec((tn, k_dim), lambda j, i: (j + nt, 0))],
        out_specs=[o_spec, o_spec, o_spec],
        compiler_params=pltpu.CompilerParams(dimension_semantics=("parallel", "parallel")),
        name="ffn_in_act",
    )(h1b, w_t, w_t)


def _act_bwd(gt, up, d_act):
    def fn(r, c):
        gt_, up_, da = r
        return [jnp.concatenate([da * up_ * _silu_grad(gt_), da * gt_ * _sig(gt_)], axis=1)], []

    return _rowwise(fn, [gt, up, d_act], [], [(2 * FFN_HIDDEN, BF16)], name="act_bwd")[0]


def _tail(h1, ffn, p, tgt, g, b, w_pg, w_ple_t):
    def fn(r, c):
        h1_, ffn_, p_, t_ = r
        pg_ = _dot(h1_, c[2])
        pp_ = _dot(p_, c[3], NT)
        sp = _sig(pg_)
        xh, rr = _ln_stats(ALPHA * h1_ + ffn_ + sp * pp_)
        y = xh * c[0] + c[1]
        err = y - t_
        dy = err * (1.0 / D_MODEL)
        dz = _ln_bwd(dy, xh, rr, c[0])
        loss = jnp.sum(0.5 * _rowmean(err * err), axis=0, keepdims=True)
        return ([dz, dz * pp_ * sp * (1.0 - sp), dz * sp, ALPHA * dz],
                [_colsum(dy * xh), _colsum(dy), jnp.broadcast_to(loss, (1, LANES))])

    return _rowwise(fn, [h1, ffn, p, tgt], [g, b, w_pg, w_ple_t], [(D_MODEL, BF16)] * 3 + [(D_MODEL, F32)],
                    accs=[(1, D_MODEL), (1, D_MODEL), (1, LANES)], tm=FUSED_ROWS, name="tail")


def _local_step(x, p, pos, tgt, w, late_weights, emit):
    w = dict(w)
    s_dim = x.shape[0]
    pb = p.astype(BF16)
    proj_a, proj_g, proj_b, xb = _input_proj(x, w["w_in_t"], w["wg_t"], w["wb_t"])
    qkvn = _conv_fwd(proj_a, w["conv"])
    beta, gc = _gates_fwd(proj_b, w["alog"], w["dtb"])
    gc_t = jnp.transpose(gc[:, :N_HEADS])
    u, w_, qd, kt, a_mat, t_fold = _gdr_prep_fwd(qkvn, beta, gc, gc_t)
    o_dn, states = _gdr_scan_fwd(u, w_, qd, kt, a_mat, gc)
    w.update(late_weights("mix", o_dn))
    og, y_dn = _gdr_out_fwd(o_dn, proj_a, w["dnw"], w["br_dn"])
    c_q, c_kv, qc, kc, vv, kct, vt = _mla_prep_fwd(proj_b, pos, w["qnw"], w["kvnw"], w["uq"], w["uk"], w["uv"])
    o_mla, lse = _attn_fwd(qc, kc, vt)
    y_mla = _mm(o_mla, w["br_mla"], out_dtype=BF16, name="f_y_mla")
    mixed, a1, h1, h1b = _mix_proj_ln1(y_dn, y_mla, proj_g, x, w["wo"], w["ln1g"], w["ln1b"])
    w.update(late_weights("ffn", a1))
    gt, up, act = _ffn_in_act(h1b, w["ffn_in_t"])
    ffn = _mm_resident(act, w["ffn_out"], name="f_ffn")
    g = {}
    dz2, d_pg, d_pp, dh1a, g["ln2g"], g["ln2b"], loss = _tail(h1, ffn, pb, tgt, w["ln2g"], w["ln2b"],
                                                            w["ple_gate"], w["ple_t"])
    g["ple_t"] = _mm(d_pp, pb, ta=True, out_dtype=BF16, name="b_w_ple")
    g["ple_gate"] = _mm(h1b, d_pg, ta=True, out_dtype=BF16, name="b_w_ple_gate")
    g["ffn_out"] = _mm(act, dz2, ta=True, out_dtype=BF16, name="b_w_ffn_out")
    d_act = _mm(dz2, w["ffn_out"], tb=True, out_dtype=BF16, name="b_act")
    d_gu = _act_bwd(gt, up, d_act)
    g["ffn_in_t"] = _mm(d_gu, h1b, ta=True, out_dtype=BF16, name="b_w_ffn_in")
    d_gu = emit("ffn", g, d_gu)
    d_h1 = _mm_resident(d_gu, w["ffn_in_t"], add=(dh1a,), name="b_h1_ffn")
    dz1, dxa, d_proj_g, d_y_dn, d_y_mla, g["ln1g"], g["ln1b"] = _ln1_mix_bwd(
        x, a1, d_h1, d_pg, y_dn, y_mla, proj_g, w["ln1g"], w["wo"], w["ple_gate"])
    g["wo"] = _mm(mixed, dz1, ta=True, out_dtype=BF16, name="b_w_o")
    g["br_mla"] = _mm(o_mla, d_y_mla, ta=True, out_dtype=BF16, name="b_w_br_mla")
    d_o_mla = _mm(d_y_mla, w["br_mla"], tb=True, out_dtype=BF16, name="b_o_mla")
    d_qc, d_kc, d_v = _attn_bwd(qc, kc, kct, vv, o_mla, d_o_mla, lse)
    d_q_full, d_kn, d_cq, d_ckv, d_kr, g["qnw"], g["kvnw"] = _mla_prep_bwd(
        proj_b, pos, d_qc, d_kc, d_v, w["qnw"], w["kvnw"], w["uq"], w["uk"], w["uv"])
    g["uq"] = _mm(c_q, d_q_full, ta=True, out_dtype=BF16, name="b_w_uq")
    g["uk"] = _mm(c_kv, d_kn, ta=True, out_dtype=BF16, name="b_w_uk")
    g["uv"] = _mm(c_kv, d_v, ta=True, out_dtype=BF16, name="b_w_uv")
    g["br_dn"] = _mm(og, d_y_dn, ta=True, out_dtype=BF16, name="b_w_br_dn")
    d_y_dn = emit("mix", g, d_y_dn)
    d_o_dn, d_z, g["dnw"] = _gdr_out_bwd(o_dn, proj_a, d_y_dn, w["dnw"], w["br_dn"])
    du, dw, dqd, dkt, d_a, d_egl = _gdr_scan_bwd(u, w_, qd, kt, a_mat, gc, states, d_o_dn)
    dq, dk, dv, d_beta, d_gc = _gdr_prep_bwd(qkvn, beta, gc, gc_t, t_fold, u, w_, du, dw, dqd, dkt, d_a)
    d_egl_rows = jnp.pad(d_egl[:, None, :, 0], ((0, 0), (CHUNK - 1, 0), (0, LANES - N_HEADS))).reshape(s_dim, LANES)
    d_ba, g["alog"], g["dtb"] = _gates_bwd(proj_b, w["alog"], w["dtb"], gc, d_beta, d_gc, d_egl_rows)
    d_qkv, g["conv"] = _conv_bwd(proj_a, w["conv"], dq, dk, dv)
    zeros = jnp.zeros((s_dim, WB_CKV - Q_LORA), BF16)
    d_proj_b = jnp.concatenate([d_cq, zeros, d_ckv, d_kr, d_ba], axis=1)
    g["wa_qkv_t"] = _mm(d_qkv, xb, ta=True, name="b_w_qkv")
    g["wa_z_t"] = _mm(d_z, xb, ta=True, name="b_w_z")
    g["wg_t"] = _mm(d_proj_g, xb, ta=True, name="b_w_g")
    g["wb_t"] = _mm(d_proj_b, xb, ta=True, name="b_w_b")
    d_qkv = emit("small", dict(g, loss=loss), emit("w_in", g, d_qkv))
    dx = _input_grad(d_qkv, d_z, d_proj_g, d_proj_b, w["w_in_t"], w["wg_t"], w["wb_t"], dxa)
    return loss, dx, g


DX_ROWS = 512


def _input_proj(x, w_in_t, wg_t, wb_t):
    s_dim = x.shape[0]
    n_a = 4 * D_MODEL

    def body(x_ref, wa_ref, wg_ref, wb_ref, a_ref, g_ref, b_ref, xb_ref):
        xv = x_ref[...].astype(BF16)
        xb_ref[...] = xv
        a_ref[...] = _dot(xv, wa_ref[...], NT)
        g_ref[...] = _dot(xv, wg_ref[...], NT).astype(BF16)
        b_ref[...] = _dot(xv, wb_ref[...], NT)

    rows = lambda width: pl.BlockSpec((DX_ROWS, width), lambda i: (i, 0))
    whole = lambda shape: pl.BlockSpec(shape, lambda i: (0, 0), pipeline_mode=pl.Buffered(1))
    return pl.pallas_call(
        body,
        out_shape=[jax.ShapeDtypeStruct((s_dim, n_a), F32), jax.ShapeDtypeStruct((s_dim, wg_t.shape[0]), BF16),
                   jax.ShapeDtypeStruct((s_dim, wb_t.shape[0]), F32), jax.ShapeDtypeStruct((s_dim, D_MODEL), BF16)],
        grid=(s_dim // DX_ROWS,),
        in_specs=[rows(D_MODEL), whole((n_a, D_MODEL)), whole(wg_t.shape), whole(wb_t.shape)],
        out_specs=[rows(n_a), rows(wg_t.shape[0]), rows(wb_t.shape[0]), rows(D_MODEL)],
        compiler_params=pltpu.CompilerParams(dimension_semantics=("parallel",)),
        name="f_proj",
    )(x, w_in_t, wg_t, wb_t)


def _input_grad(d_qkv, d_z, d_g, d_b, w_in_t, wg_t, wb_t, add):
    s_dim = d_qkv.shape[0]
    n_qkv, n_a = d_qkv.shape[1], d_qkv.shape[1] + d_z.shape[1]

    def body(q_ref, z_ref, g_ref, b_ref, wa_ref, wg_ref, wb_ref, add_ref, o_ref):
        r = add_ref[...] + _dot(q_ref[...], wa_ref[0:n_qkv])
        r = r + _dot(z_ref[...], wa_ref[n_qkv:n_a])
        r = r + _dot(g_ref[...], wg_ref[...])
        o_ref[...] = r + _dot(b_ref[...], wb_ref[...])

    rows = lambda a: pl.BlockSpec((DX_ROWS, a.shape[1]), lambda i: (i, 0))
    whole = lambda shape: pl.BlockSpec(shape, lambda i: (0, 0), pipeline_mode=pl.Buffered(1))
    return pl.pallas_call(
        body,
        out_shape=jax.ShapeDtypeStruct((s_dim, D_MODEL), F32),
        grid=(s_dim // DX_ROWS,),
        in_specs=[rows(d_qkv), rows(d_z), rows(d_g), rows(d_b), whole((n_a, D_MODEL)), whole(wg_t.shape),
                  whole(wb_t.shape), rows(add)],
        out_specs=pl.BlockSpec((DX_ROWS, D_MODEL), lambda i: (i, 0)),
        compiler_params=pltpu.CompilerParams(dimension_semantics=("parallel",)),
        name="b_x",
    )(d_qkv, d_z, d_g, d_b, w_in_t, wg_t, wb_t, add)


_BIG = (("w_in", 1), ("w_uq", 0), ("w_uk", 0), ("w_uv", 0), ("w_br_dn", 0), ("w_br_mla", 0),
        ("w_o", 0), ("w_ffn_in", 1), ("w_ffn_out", 0), ("w_ple", 1), ("w_ple_gate", 0))
_BIG_AXIS = dict(_BIG)
_SMALL = ("ln1_g", "ln1_b", "ln2_g", "ln2_b", "q_norm_w", "kv_norm_w", "dn_norm_w", "dn_a_log", "dn_dt_bias")
_ORDER = ("w_in", "conv_w", "dn_a_log", "dn_dt_bias", "dn_norm_w", "q_norm_w", "w_uq", "kv_norm_w", "w_uk", "w_uv",
          "w_br_dn", "w_br_mla", "w_o", "ln1_g", "ln1_b", "w_ffn_in", "w_ffn_out", "w_ple", "w_ple_gate", "ln2_g",
          "ln2_b")


def _stored_shape(name, shard_shape):
    axis = _BIG_AXIS[name]
    lead = shard_shape[axis]
    return lead, int(np.prod(shard_shape)) // lead


def _to_stored(name, shard):
    return jnp.moveaxis(shard, _BIG_AXIS[name], 0).reshape(_stored_shape(name, shard.shape))


def _from_stored(name, stored, shard_shape):
    axis = _BIG_AXIS[name]
    moved = (shard_shape[axis],) + shard_shape[:axis] + shard_shape[axis + 1:]
    return jnp.moveaxis(stored.reshape(moved), 0, axis)


_W_IN_ROWS = np.cumsum([0, 3072, 1024, 8, 8, Q_LORA, KV_LORA, ROPE, D_MODEL, D_MODEL])


def _first_weights(w_in_t, conv_full, small):
    r = _W_IN_ROWS
    zr = lambda n: jnp.zeros((n, D_MODEL), w_in_t.dtype)
    w = {}
    w["w_in_t"] = w_in_t
    w["wg_t"] = w_in_t[r[7]:r[9]]
    w["wb_t"] = jnp.concatenate([w_in_t[r[4]:r[5]], zr(WB_CKV - Q_LORA), w_in_t[r[5]:r[7]], zr(LANES - ROPE),
                                 w_in_t[r[2]:r[4]], zr(LANES - 2 * N_HEADS)], axis=0)
    w["conv"] = conv_full
    pad_l = lambda v: jnp.pad(v, ((0, 0), (0, LANES - v.shape[1])))
    w["alog"], w["dtb"] = pad_l(small["dn_a_log"]), pad_l(small["dn_dt_bias"])
    w["dnw"], w["qnw"], w["kvnw"] = small["dn_norm_w"], small["q_norm_w"], small["kv_norm_w"]
    w["ln1g"], w["ln1b"], w["ln2g"], w["ln2b"] = small["ln1_g"], small["ln1_b"], small["ln2_g"], small["ln2_b"]
    return w


def _late_weights(group, fw):
    w = {}
    if group == "mix":
        uq = fw["w_uq"].reshape(Q_LORA, N_HEADS, HEAD + ROPE)
        uq_r = jnp.pad(uq[:, :, HEAD:], ((0, 0), (0, 0), (0, HEAD - ROPE)))
        w["uq"] = jnp.concatenate([uq[:, :, :HEAD].reshape(Q_LORA, -1), uq_r.reshape(Q_LORA, -1)], axis=1)
        w["uk"], w["uv"] = fw["w_uk"], fw["w_uv"]
        w["br_dn"], w["br_mla"], w["wo"] = fw["w_br_dn"], fw["w_br_mla"], fw["w_o"]
    else:
        w["ffn_in_t"], w["ffn_out"] = fw["w_ffn_in"], fw["w_ffn_out"]
        w["ple_t"], w["ple_gate"] = fw["w_ple"], fw["w_ple_gate"]
    return w


_GROUP_GRADS = {"ffn": (("w_ple", "ple_t"), ("w_ple_gate", "ple_gate"), ("w_ffn_out", "ffn_out"),
                        ("w_ffn_in", "ffn_in_t")),
                "mix": (("w_o", "wo"), ("w_br_mla", "br_mla"), ("w_uq", "uq"), ("w_uk", "uk"), ("w_uv", "uv"),
                        ("w_br_dn", "br_dn"))}


def _group_grads(group, g):
    out = {}
    for name, key in _GROUP_GRADS[group]:
        t = g[key]
        if name == "w_uq":
            uq_n = t[:, :D_MODEL].reshape(Q_LORA, N_HEADS, HEAD)
            uq_r = t[:, D_MODEL:].reshape(Q_LORA, N_HEADS, HEAD)[:, :, :ROPE]
            t = jnp.concatenate([uq_n, uq_r], axis=2).reshape(Q_LORA, -1)
        out[name] = t
    return out


PACK_ROWS = 512
SUBLANES = 8


def _pack_exchange(parts, name):
    arrays = []
    for a, _, _ in parts:
        if not any(a is b for b in arrays):
            arrays.append(a)
    index = lambda a: next(i for i, b in enumerate(arrays) if a is b)
    chunks, dst = [], 0
    for a, first, rows in parts:
        assert first % SUBLANES == 0 and rows % SUBLANES == 0
        chunks += [(index(a), first + o, dst + o, min(PACK_ROWS, rows - o)) for o in range(0, rows, PACK_ROWS)]
        dst += rows
    c, n, last = arrays[0].shape[1], len(arrays), len(chunks) - 1
    slab = dst // N_DEV
    assert slab * N_DEV == dst

    def body(*refs):
        src_refs, out_ref, recv_ref = refs[:n], refs[n], refs[n + 1]
        buf, sem_in, sem_out, send_sems, recv_sems = refs[n + 2:]
        x, y, core = lax.axis_index("x"), lax.axis_index("y"), lax.axis_index("c")

        def to_sibling(q):
            return pltpu.make_async_remote_copy(
                src_ref=out_ref.at[pl.ds((2 * q + 1 - core) * slab, slab)], dst_ref=recv_ref.at[q],
                send_sem=send_sems.at[q], recv_sem=recv_sems.at[q], device_id=(x, y, 1 - core),
                device_id_type=_MESH_ID)

        sent = [0]

        def send_packed(rows_done):
            while sent[0] < N_DEV // 2 and (2 * sent[0] + 2) * slab <= rows_done:
                to_sibling(sent[0]).start()
                sent[0] += 1

        def load(k):
            i, first, _, rows = chunks[k]
            return pltpu.make_async_copy(src_refs[i].at[pl.ds(first, rows)], buf.at[k % 2, pl.ds(0, rows)],
                                         sem_in.at[k % 2])

        def store(k):
            _, _, first, rows = chunks[k]
            return pltpu.make_async_copy(buf.at[k % 2, pl.ds(0, rows)], out_ref.at[pl.ds(first, rows), 0, :],
                                         sem_out.at[k % 2])

        load(0).start()
        for k in range(last + 1):
            load(k).wait()
            store(k).start()
            if k >= 1:
                store(k - 1).wait()
                send_packed(chunks[k][2])
            if k < last:
                load(k + 1).start()
        store(last).wait()
        send_packed(dst)
        for q in range(N_DEV // 2):
            to_sibling(q).wait_recv()
        for q in range(N_DEV // 2):
            to_sibling(q).wait_send()

    return pl.pallas_call(
        body,
        out_shape=[jax.ShapeDtypeStruct((dst, 1, c), F32), jax.ShapeDtypeStruct((N_DEV // 2, slab, 1, c), F32)],
        in_specs=[_ANY] * n,
        out_specs=[_ANY, _ANY],
        scratch_shapes=[pltpu.VMEM((2, PACK_ROWS, c), F32), pltpu.SemaphoreType.DMA((2,)),
                        pltpu.SemaphoreType.DMA((2,)), pltpu.SemaphoreType.DMA((N_DEV // 2,)),
                        pltpu.SemaphoreType.DMA((N_DEV // 2,))],
        name=name,
    )(*arrays)


def _w_in_grad_parts(g):
    wb = g["wb_t"]
    return [(g["wa_qkv_t"], 0, 3 * D_MODEL), (g["wa_z_t"], 0, D_MODEL), (wb, WB_BA, 2 * N_HEADS),
            (wb, WB_CQ, Q_LORA), (wb, WB_CKV, KV_LORA), (wb, WB_KR, ROPE), (g["wg_t"], 0, 2 * D_MODEL)]


def _small_grads(g):
    return {"ln1_g": g["ln1g"], "ln1_b": g["ln1b"], "ln2_g": g["ln2g"], "ln2_b": g["ln2b"], "q_norm_w": g["qnw"],
            "kv_norm_w": g["kvnw"], "dn_norm_w": g["dnw"], "dn_a_log": g["alog"], "dn_dt_bias": g["dtb"],
            "conv_w": g["conv"]}


_SMALL_SLOTS = {"ln1_g": (0, 0, 1024), "ln1_b": (1, 0, 1024), "ln2_g": (2, 0, 1024), "ln2_b": (3, 0, 1024),
                "q_norm_w": (4, 0, 384), "kv_norm_w": (4, 384, 256), "dn_norm_w": (4, 640, 128),
                "dn_a_log": (4, 768, 8), "dn_dt_bias": (4, 896, 8)}
_SMALL_ROWS, _LOSS_ROW, _CONV_ROW0, _CONV_ROWS = 24, 5, 8, 12


def _pack_small_grads(small_g, loss):
    zeros = lambda r, c: jnp.zeros((r, c), F32)
    row4 = jnp.concatenate([small_g["q_norm_w"], small_g["kv_norm_w"], small_g["dn_norm_w"], small_g["dn_a_log"],
                            small_g["dn_dt_bias"]], axis=1)
    row5 = jnp.concatenate([loss, zeros(1, FLAT_COLS - LANES)], axis=1)
    head = jnp.concatenate([small_g["ln1_g"], small_g["ln1_b"], small_g["ln2_g"], small_g["ln2_b"], row4, row5,
                            zeros(2, FLAT_COLS)], axis=0)
    conv = small_g["conv_w"].reshape(_CONV_ROWS, FLAT_COLS)
    return jnp.concatenate([head, conv, zeros(_SMALL_ROWS - _CONV_ROW0 - _CONV_ROWS, FLAT_COLS)], axis=0)


_MESH_ID = pl.DeviceIdType.MESH
_ANY = pl.BlockSpec(memory_space=pl.ANY)


def _all_gather(blocks, name):
    n = len(blocks)

    def body(*refs):
        x_refs, out_refs = refs[:n], refs[n:2 * n]
        send_sems, recv_sems, local_sems = refs[2 * n:]
        x, y, c = lax.axis_index("x"), lax.axis_index("y"), lax.axis_index("c")
        me, sibling = (x, y, c), (x, y, 1 - c)
        chips = [(1 - x, y), (x, 1 - y), (1 - x, 1 - y)]

        def slot(i, px, py, pc):
            return out_refs[i].at[4 * px + 2 * py + pc]

        def copy(i, k, origin, to, src=None):
            return pltpu.make_async_remote_copy(
                src_ref=slot(i, *origin) if src is None else src, dst_ref=slot(i, *origin),
                send_sem=send_sems.at[7 * i + k], recv_sem=recv_sems.at[7 * i + k], device_id=to,
                device_id_type=_MESH_ID)

        mine = [pltpu.make_async_copy(x_refs[i], slot(i, *me), local_sems.at[i]) for i in range(n)]
        first, passed = [], []
        for i in range(n):
            mine[i].start()
            first.append(copy(i, 0, me, sibling, src=x_refs[i]))
            first += [copy(i, 1 + j, me, (*chip, c), src=x_refs[i]) for j, chip in enumerate(chips)]
        for cp in first:
            cp.start()
        for i in range(n):
            for j, chip in enumerate(chips):
                copy(i, 1 + j, (*chip, c), me).wait_recv()
                passed.append(copy(i, 4 + j, (*chip, c), sibling))
                passed[-1].start()
        for i in range(n):
            copy(i, 0, sibling, me).wait_recv()
            for j, chip in enumerate(chips):
                copy(i, 4 + j, (*chip, 1 - c), me).wait_recv()
        for cp in first + passed:
            cp.wait_send()
        for cp in mine:
            cp.wait()

    return pl.pallas_call(
        body,
        out_shape=[jax.ShapeDtypeStruct((N_DEV,) + b.shape, b.dtype) for b in blocks],
        in_specs=[_ANY] * n,
        out_specs=[_ANY] * n,
        scratch_shapes=[pltpu.SemaphoreType.DMA((7 * n,)), pltpu.SemaphoreType.DMA((7 * n,)),
                        pltpu.SemaphoreType.DMA((n,))],
        name=name,
    )(*blocks)


def _col_tile(c):
    return c if c <= 256 else 256


def _chip_sum(src, recv, parity, name):
    _, r, _, c = src.shape
    tc = _col_tile(c)

    def body(par_ref, a_ref, b_ref, o_ref, ob_ref):
        s = a_ref[...] + b_ref[...]
        o_ref[...] = s
        ob_ref[...] = s.astype(BF16)

    rows = lambda f: pl.BlockSpec((None, r, None, tc), f)
    blk = pl.BlockSpec((None, r, tc), lambda q, j, par: (q, 0, j))
    return pl.pallas_call(
        body,
        out_shape=[jax.ShapeDtypeStruct((4, r, c), F32), jax.ShapeDtypeStruct((4, r, c), BF16)],
        grid_spec=pltpu.PrefetchScalarGridSpec(
            num_scalar_prefetch=1, grid=(4, c // tc),
            in_specs=[rows(lambda q, j, par: (2 * q + par[0], 0, 0, j)), rows(lambda q, j, par: (q, 0, 0, j))],
            out_specs=[blk, blk]),
        compiler_params=pltpu.CompilerParams(dimension_semantics=("parallel", "parallel")),
        name=name,
    )(parity, src, recv)


_HBM = pl.BlockSpec(memory_space=pltpu.HBM)
_SEM = pl.BlockSpec(memory_space=pltpu.SEMAPHORE)
_DATAFLOW = pltpu.SideEffectType.DATAFLOW_SIDE_EFFECTING
N_PEERS = N_DEV - 1


def _ring_peer(j):
    me = 4 * lax.axis_index("x") + 2 * lax.axis_index("y") + lax.axis_index("c")
    k = (me + j) % N_DEV
    return me, k, (k // 4, (k // 2) % 2, k % 2)


def _spread_copy(i, j, src_refs, land_refs, send_sems, recv_sems, scatter):
    me, k, peer = _ring_peer(j)
    return pltpu.make_async_remote_copy(
        src_ref=src_refs[i].at[k] if scatter else src_refs[i], dst_ref=land_refs[i].at[me],
        send_sem=send_sems.at[N_PEERS * i + j - 1], recv_sem=recv_sems.at[N_PEERS * i + j - 1], device_id=peer,
        device_id_type=_MESH_ID)


def _spread_start(srcs, carry, scatter, name):
    n = len(srcs)
    lands = [lax.empty(((N_DEV,) + s.shape[-2:]), s.dtype) for s in srcs]

    def body(*refs):
        src_refs, land_refs = refs[:n], refs[n:2 * n]
        send_sems, recv_sems, local_sems = refs[2 * n + 1:2 * n + 4]
        for i in range(n):
            for j in range(1, N_DEV):
                _spread_copy(i, j, src_refs, land_refs, send_sems, recv_sems, scatter).start()
        for i in range(n):
            _own_copy(i, src_refs, land_refs, local_sems, scatter).start()

    hbm = lambda a: pltpu.HBM(a.shape, a.dtype)
    sems = pltpu.SemaphoreType.DMA((N_PEERS * n,))
    pinned = [pltpu.with_memory_space_constraint(a, pltpu.HBM) for a in list(srcs) + lands + [carry]]
    res = pl.pallas_call(
        body, name=name,
        out_shape=(sems, sems, pltpu.SemaphoreType.DMA((n,)), *[hbm(a) for a in pinned]),
        in_specs=[_HBM] * (2 * n + 1),
        out_specs=(_SEM, _SEM, _SEM, *[_HBM] * (2 * n + 1)),
        input_output_aliases={i: 3 + i for i in range(2 * n + 1)},
        compiler_params=pltpu.CompilerParams(has_side_effects=_DATAFLOW),
    )(*pinned)
    return res[:3], list(res[3:3 + n]), list(res[3 + n:3 + 2 * n]), res[3 + 2 * n]


def _own_copy(i, src_refs, land_refs, local_sems, scatter):
    me = _ring_peer(0)[0]
    return pltpu.make_async_copy(src_refs[i].at[me] if scatter else src_refs[i], land_refs[i].at[me],
                                 local_sems.at[i])


def _spread_wait(started, after, scatter, name):
    sems, srcs, lands, _ = started
    n = len(srcs)

    def body(*refs):
        src_refs, land_refs = refs[:n], refs[n:2 * n]
        send_s, recv_s, local_s = refs[2 * n:2 * n + 3]
        for i in range(n):
            for j in range(1, N_DEV):
                cp = _spread_copy(i, j, src_refs, land_refs, send_s, recv_s, scatter)
                cp.wait_send()
                cp.wait_recv()
        for i in range(n):
            _own_copy(i, src_refs, land_refs, local_s, scatter).wait()

    hbm = lambda a: pltpu.HBM(a.shape, a.dtype)
    res = pl.pallas_call(
        body, name=name,
        out_shape=tuple(hbm(a) for a in srcs + lands),
        in_specs=[_HBM] * (2 * n) + [_SEM, _SEM, _SEM, pl.BlockSpec(memory_space=pl.ANY)],
        out_specs=tuple([_HBM] * (2 * n)),
        input_output_aliases={i: i for i in range(2 * n)},
        compiler_params=pltpu.CompilerParams(has_side_effects=_DATAFLOW),
    )(*srcs, *lands, *sems, after)
    return list(res[n:])


def _chips_copy(i, j, src_refs, land_refs, send_sems, recv_sems):
    x, y, c = lax.axis_index("x"), lax.axis_index("y"), lax.axis_index("c")
    tx, ty = [(1 - x, y), (x, 1 - y), (1 - x, 1 - y)][j]
    return pltpu.make_async_remote_copy(
        src_ref=src_refs[i].at[2 * tx + ty], dst_ref=land_refs[i].at[j], send_sem=send_sems.at[3 * i + j],
        recv_sem=recv_sems.at[3 * i + j], device_id=(tx, ty, c), device_id_type=_MESH_ID)


def _chips_start(srcs, carry, name):
    n = len(srcs)
    lands = [lax.empty((3,) + s.shape[1:], s.dtype) for s in srcs]

    def body(*refs):
        src_refs, land_refs = refs[:n], refs[n:2 * n]
        send_sems, recv_sems = refs[2 * n + 1:2 * n + 3]
        for i in range(n):
            for j in range(3):
                _chips_copy(i, j, src_refs, land_refs, send_sems, recv_sems).start()

    hbm = lambda a: pltpu.HBM(a.shape, a.dtype)
    sems = pltpu.SemaphoreType.DMA((3 * n,))
    pinned = [pltpu.with_memory_space_constraint(a, pltpu.HBM) for a in list(srcs) + lands + [carry]]
    res = pl.pallas_call(
        body, name=name,
        out_shape=(sems, sems, *[hbm(a) for a in pinned]),
        in_specs=[_HBM] * (2 * n + 1),
        out_specs=(_SEM, _SEM, *[_HBM] * (2 * n + 1)),
        input_output_aliases={i: 2 + i for i in range(2 * n + 1)},
        compiler_params=pltpu.CompilerParams(has_side_effects=_DATAFLOW),
    )(*pinned)
    return res[:2], list(res[2:2 + n]), list(res[2 + n:2 + 2 * n]), res[2 + 2 * n]


def _chips_wait(started, after, name):
    sems, srcs, lands, _ = started
    n = len(srcs)

    def body(*refs):
        src_refs, land_refs = refs[:n], refs[n:2 * n]
        send_s, recv_s = refs[2 * n:2 * n + 2]
        for i in range(n):
            for j in range(3):
                cp = _chips_copy(i, j, src_refs, land_refs, send_s, recv_s)
                cp.wait_send()
                cp.wait_recv()

    hbm = lambda a: pltpu.HBM(a.shape, a.dtype)
    res = pl.pallas_call(
        body, name=name,
        out_shape=tuple(hbm(a) for a in srcs + lands),
        in_specs=[_HBM] * (2 * n) + [_SEM, _SEM, pl.BlockSpec(memory_space=pl.ANY)],
        out_specs=tuple([_HBM] * (2 * n)),
        input_output_aliases={i: i for i in range(2 * n)},
        compiler_params=pltpu.CompilerParams(has_side_effects=_DATAFLOW),
    )(*srcs, *lands, *sems, after)
    return list(res[n:])


def _sum8(landing, name):
    _, r, c = landing.shape
    tc = _col_tile(c)

    def body(a_ref, o_ref):
        tot = a_ref[0].astype(F32)
        for k in range(1, N_DEV):
            tot = tot + a_ref[k].astype(F32)
        o_ref[...] = tot

    return pl.pallas_call(
        body,
        out_shape=jax.ShapeDtypeStruct((r, c), F32),
        grid=(c // tc,),
        in_specs=[pl.BlockSpec((N_DEV, r, tc), lambda j: (0, 0, j))],
        out_specs=pl.BlockSpec((r, tc), lambda j: (0, j)),
        compiler_params=pltpu.CompilerParams(dimension_semantics=("parallel",)),
        name=name,
    )(landing)


def _adamw_math(w, g, m, v):
    m = ADAM_B1 * m + (1.0 - ADAM_B1) * g
    v = ADAM_B2 * v + (1.0 - ADAM_B2) * (g * g)
    m_hat = m / (1.0 - ADAM_B1 ** ADAM_STEP)
    v_hat = v / (1.0 - ADAM_B2 ** ADAM_STEP)
    delta = -ADAM_LR * (m_hat / (jnp.sqrt(v_hat) + ADAM_EPS) + ADAM_WD * w)
    return delta, m, v


def _adamw(w, m, v, g, name):
    r, c = w.shape

    def fn(rows, consts):
        return list(_adamw_math(*rows)), []

    return _rowwise(fn, [w, g, m, v], [], [(c, F32)] * 3, tm=r if r <= 512 else 256, name=name)


def _adamw_sum8(w, m, v, landing, name):
    r, c = w.shape
    tc = _col_tile(c)

    def body(w_ref, m_ref, v_ref, a_ref, g_ref, d_ref, m2_ref, v2_ref):
        g = a_ref[0].astype(F32)
        for k in range(1, N_DEV):
            g = g + a_ref[k].astype(F32)
        delta, m2, v2 = _adamw_math(w_ref[...], g, m_ref[...], v_ref[...])
        g_ref[...] = g
        d_ref[...] = delta
        m2_ref[...] = m2
        v2_ref[...] = v2

    blk = pl.BlockSpec((r, tc), lambda j: (0, j))
    return pl.pallas_call(
        body,
        out_shape=[jax.ShapeDtypeStruct((r, c), F32)] * 4,
        grid=(c // tc,),
        in_specs=[blk, blk, blk, pl.BlockSpec((N_DEV, r, tc), lambda j: (0, 0, j))],
        out_specs=[blk] * 4,
        compiler_params=pltpu.CompilerParams(dimension_semantics=("parallel",)),
        name=name,
    )(w, m, v, landing)


def _adamw_parts(w, m, v, own, others, chip, name):
    r, _, c = w.shape
    tc = _col_tile(c)

    def body(q_ref, w_ref, m_ref, v_ref, a_ref, b_ref, g_ref, d_ref, m2_ref, v2_ref):
        g = ((a_ref[...] + b_ref[0].astype(F32)) + b_ref[1].astype(F32)) + b_ref[2].astype(F32)
        delta, m2, v2 = _adamw_math(w_ref[...], g, m_ref[...], v_ref[...])
        g_ref[...] = g
        d_ref[...] = delta
        m2_ref[...] = m2
        v2_ref[...] = v2

    row = pl.BlockSpec((r, None, tc), lambda j, q: (0, 0, j))
    return pl.pallas_call(
        body,
        out_shape=[jax.ShapeDtypeStruct((r, 1, c), F32)] * 4,
        grid_spec=pltpu.PrefetchScalarGridSpec(
            num_scalar_prefetch=1, grid=(c // tc,),
            in_specs=[row, row, row, pl.BlockSpec((None, r, tc), lambda j, q: (q[0], 0, j)),
                      pl.BlockSpec((3, r, tc), lambda j, q: (0, 0, j))],
            out_specs=[row] * 4),
        compiler_params=pltpu.CompilerParams(dimension_semantics=("parallel",)),
        name=name,
    )(chip, w, m, v, own, others)


def _adamw_small(gathered, params):
    ns = len(_SMALL)

    def body(*refs):
        g_ref, p_refs, o_refs = refs[0], refs[1:1 + 3 * ns], refs[1 + 3 * ns:]
        tot = g_ref[0]
        for k in range(1, N_DEV):
            tot = tot + g_ref[k]
        for i, name in enumerate(_SMALL):
            row, lane0, lanes = _SMALL_SLOTS[name]
            g = tot[row:row + 1, lane0:lane0 + lanes]
            w_, m_, v_ = (p_refs[3 * i + j][...] for j in range(3))
            delta, m2, v2 = _adamw_math(w_, g, m_, v_)
            for j, val in enumerate((g, delta, m2, v2)):
                o_refs[4 * i + j][...] = val
        o_refs[4 * ns][...] = tot[_LOSS_ROW:_LOSS_ROW + 1, 0:LANES]
        o_refs[4 * ns + 1][...] = tot[_CONV_ROW0:_CONV_ROW0 + _CONV_ROWS, :]

    out_shape = [jax.ShapeDtypeStruct(w.shape, F32) for (w, _, _) in params for _ in range(4)]
    out_shape += [jax.ShapeDtypeStruct((1, LANES), F32), jax.ShapeDtypeStruct((_CONV_ROWS, FLAT_COLS), F32)]
    flat = [a for wmv in params for a in wmv]
    return pl.pallas_call(body, out_shape=out_shape, name="adamw_small")(gathered, *flat)


def kernel(x, p, positions, w_in, conv_w, dn_a_log, dn_dt_bias, dn_norm_w, q_norm_w, w_uq, kv_norm_w, w_uk, w_uv, w_br_dn, w_br_mla, w_o, ln1_g, ln1_b, w_ffn_in, w_ffn_out, w_ple, w_ple_gate, ln2_g, ln2_b, loss_target, m_w_in, m_conv_w, m_dn_a_log, m_dn_dt_bias, m_dn_norm_w, m_q_norm_w, m_w_uq, m_kv_norm_w, m_w_uk, m_w_uv, m_w_br_dn, m_w_br_mla, m_w_o, m_ln1_g, m_ln1_b, m_w_ffn_in, m_w_ffn_out, m_w_ple, m_w_ple_gate, m_ln2_g, m_ln2_b, v_w_in, v_conv_w, v_dn_a_log, v_dn_dt_bias, v_dn_norm_w, v_q_norm_w, v_w_uq, v_kv_norm_w, v_w_uk, v_w_uv, v_w_br_dn, v_w_br_mla, v_w_o, v_ln1_g, v_ln1_b, v_w_ffn_in, v_w_ffn_out, v_w_ple, v_w_ple_gate, v_ln2_g, v_ln2_b):
    args = dict(locals())
    wts = {n: args[n] for n in _ORDER}
    mom1 = {n: args["m_" + n] for n in _ORDER}
    mom2 = {n: args["v_" + n] for n in _ORDER}
    big_names = [n for n, _ in _BIG]
    shard_shapes = {n: wts[n].shape[1:] for n in big_names}
    c_idx = lax.axis_index("c")
    q_idx = 2 * lax.axis_index("x") + lax.axis_index("y")
    parity, chip = c_idx.reshape(1).astype(jnp.int32), q_idx.reshape(1).astype(jnp.int32)

    stored = {n: _to_stored(n, wts[n][0]).astype(BF16) for n in big_names}
    first = _all_gather([stored["w_in"], conv_w[0]], "ag_first")
    group_names = {grp: [n for n, _ in pairs] for grp, pairs in _GROUP_GRADS.items()}
    carry, gathers = first[0], {}
    for grp in ("mix", "ffn"):
        gathers[grp] = _spread_start([stored[n] for n in group_names[grp]], carry, False, "ag_start_" + grp)
        carry = gathers[grp][3]
    conv_full = jnp.moveaxis(first[1], 0, 1).reshape(conv_w.shape[1], -1)
    small_w = {n: wts[n].astype(F32) for n in _SMALL}
    w = _first_weights(carry.reshape(-1, D_MODEL), conv_full, small_w)

    def late_weights(grp, after):
        got = _spread_wait(gathers[grp], after, False, "ag_wait_" + grp)
        return _late_weights(grp, {n: t.reshape(-1, t.shape[-1]) for n, t in zip(group_names[grp], got)})

    started = {}

    def emit(group, g, carry):
        if group == "w_in":
            rows, cols = _stored_shape("w_in", shard_shapes["w_in"])
            packed, from_sibling = _pack_exchange(_w_in_grad_parts(g), "rs_pack_sibling")
            own, own_bf = _chip_sum(packed.reshape(N_DEV, rows, 1, cols), from_sibling, parity, "rs_sum_w_in")
            started["w_in"] = (own, _chips_start([own_bf], carry, "rs_chips_start"))
            return started["w_in"][1][3]
        if group == "small":
            block = _pack_small_grads(_small_grads(g), g["loss"])
            started["small"] = _spread_start([block], carry, False, "ag_start_small")
            return started["small"][3]
        grads = _group_grads(group, g)
        srcs = [grads[n].reshape((N_DEV,) + _stored_shape(n, shard_shapes[n])) for n in grads]
        started[group] = (list(grads), _spread_start(srcs, carry, True, "rs_start_" + group))
        return started[group][1][3]

    s_dim = x.shape[1]
    loss, dx, g = _local_step(x[0], p[0, 0], positions.reshape(s_dim, 1).astype(F32), loss_target[0], w,
                              late_weights, emit)
    own, chips_started = started.pop("w_in")
    small_started = started.pop("small")

    out_g, out_d, out_m, out_v = {}, {}, {}, {}

    def update(n, grad, shp):
        flat2 = (shp[0], int(np.prod(shp[1:])))
        d, m2, v2 = _adamw(wts[n][0].reshape(flat2), mom1[n][0].reshape(flat2), mom2[n][0].reshape(flat2),
                           grad.reshape(flat2), "adamw_" + n)
        out_g[n], out_d[n], out_m[n], out_v[n] = grad, d.reshape(shp), m2.reshape(shp), v2.reshape(shp)

    for group, (names, st) in started.items():
        for n, landing in zip(names, _spread_wait(st, dx, True, "rs_wait_" + group)):
            shp = shard_shapes[n]
            if _BIG_AXIS[n] == 0 or shp[-1] % LANES:
                res = _adamw_sum8(_to_stored(n, wts[n][0]), _to_stored(n, mom1[n][0]), _to_stored(n, mom2[n][0]),
                                  landing, "adamw_" + n)
                out_g[n], out_d[n], out_m[n], out_v[n] = (_from_stored(n, t, shp) for t in res)
                last = res[3]
            else:
                update(n, _from_stored(n, _sum8(landing, "rs_total_" + n), shp), shp)

    from_chips = _chips_wait(chips_started, last, "rs_chips_wait")[0]
    g_small = _spread_wait(small_started, last, False, "ag_wait_small")[0]
    rows_first = lambda a: jnp.transpose(a, (2, 0, 1))
    res = _adamw_parts(rows_first(wts["w_in"]), rows_first(mom1["w_in"]), rows_first(mom2["w_in"]), own, from_chips,
                       chip, "adamw_w_in")
    out_g["w_in"], out_d["w_in"], out_m["w_in"], out_v["w_in"] = (jnp.transpose(t, (1, 2, 0))[0] for t in res)

    res = _adamw_small(g_small, [(wts[n], mom1[n], mom2[n]) for n in _SMALL])
    for i, n in enumerate(_SMALL):
        out_g[n], out_d[n], out_m[n], out_v[n] = res[4 * i:4 * i + 4]
    loss_out = res[4 * len(_SMALL)][0, 0]
    conv_shape = conv_w.shape[1:]
    conv_g = lax.dynamic_slice(res[-1].reshape(conv_shape[0], -1), (0, (2 * q_idx + c_idx) * conv_shape[1]),
                               conv_shape)
    update("conv_w", conv_g, conv_shape)

    expand = lambda d, n: d[n] if n in _SMALL else d[n][None]
    return (loss_out, dx[None], *[expand(out_g, n) for n in _ORDER], *[expand(out_d, n) for n in _ORDER],
            *[expand(out_m, n) for n in _ORDER], *[expand(out_v, n) for n in _ORDER])
```

```python
import functools

import numpy as np
import jax
import jax.numpy as jnp
from jax import lax
from jax.experimental import pallas as pl
from jax.experimental.pallas import tpu as pltpu

F32 = jnp.float32
BF16 = jnp.bfloat16

D_MODEL = 1024
N_HEADS = 8
HEAD = 128
CHUNK = 64
GROUP = 256
ROPE = 64
Q_LORA = 384
KV_LORA = 256
FFN_HIDDEN = 2816
PLE_DIM = 256
ROPE_BASE = 10000.0
ALPHA = 2.0 ** 0.25
SCALE = float((HEAD + ROPE) ** -0.5)
NEG_BIG = -1e30
EPS_RMS = 1e-6
EPS_LN = 1e-5

ADAM_LR = 0.001
ADAM_B1 = 0.9
ADAM_B2 = 0.999
ADAM_EPS = 1e-08
ADAM_WD = 0.01
ADAM_STEP = 10

N_DEV = 8
LANES = 128
FLAT_COLS = 1024

WB_CQ, WB_CKV, WB_KR, WB_BA, WB_COLS = 0, 512, 768, 896, 1024

HIGHEST = lax.Precision.HIGHEST

NN = (((1,), (0,)), ((), ()))
TN = (((0,), (0,)), ((), ()))
NT = (((1,), (1,)), ((), ()))


def _dot(a, b, dims=NN):
    return lax.dot_general(a.astype(BF16), b.astype(BF16), dims, preferred_element_type=F32)


def _dot32(a, b, dims=NN):
    return lax.dot_general(a, b, dims, precision=HIGHEST, preferred_element_type=F32)


def _sig(x):
    return 1.0 / (1.0 + jnp.exp(-x))


MM_TILE = 1536


def _pick_wide(n):
    if n <= MM_TILE:
        return n
    return max(t for t in range(LANES, MM_TILE + 1, LANES) if n % t == 0)


def _split_bf16(a):
    hi = a.astype(BF16)
    return hi, (a - hi.astype(F32)).astype(BF16)


def _dot3(a, b, dims=NN):
    ah, al = a if isinstance(a, tuple) else _split_bf16(a)
    bh, bl = b if isinstance(b, tuple) else _split_bf16(b)
    d = lambda p, q: lax.dot_general(p, q, dims, preferred_element_type=F32)
    return d(ah, bh) + (d(ah, bl) + d(al, bh))


def _mm(a, b, *, ta=False, tb=False, add=(), out_dtype=F32, name):
    if ta:
        k_dim, m_dim = a.shape
    else:
        m_dim, k_dim = a.shape
    if tb:
        n_dim, k2 = b.shape
    else:
        k2, n_dim = b.shape
    assert k_dim == k2, (a.shape, b.shape, ta, tb)
    tm = _pick_wide(m_dim)
    tn = _pick_wide(n_dim)
    tk = _pick_wide(k_dim)
    nk = k_dim // tk
    n_add = len(add)
    dims = TN if ta else (NT if tb else NN)
    assert not (ta and tb)

    def body(a_ref, b_ref, *rest):
        add_refs = rest[:n_add]
        o_ref = rest[n_add]
        acc = rest[n_add + 1]
        k = pl.program_id(2)

        @pl.when(k == 0)
        def _():
            acc[...] = jnp.zeros_like(acc)

        acc[...] += _dot(a_ref[...], b_ref[...], dims)

        @pl.when(k == nk - 1)
        def _():
            r = acc[...]
            for ar in add_refs:
                r = r + ar[...].astype(F32)
            o_ref[...] = r.astype(o_ref.dtype)

    a_spec = pl.BlockSpec((tk, tm), lambda i, j, k: (k, i)) if ta else pl.BlockSpec((tm, tk), lambda i, j, k: (i, k))
    b_spec = pl.BlockSpec((tn, tk), lambda i, j, k: (j, k)) if tb else pl.BlockSpec((tk, tn), lambda i, j, k: (k, j))
    o_spec = pl.BlockSpec((tm, tn), lambda i, j, k: (i, j))
    return pl.pallas_call(
        body,
        out_shape=jax.ShapeDtypeStruct((m_dim, n_dim), out_dtype),
        grid=(m_dim // tm, n_dim // tn, nk),
        in_specs=[a_spec, b_spec] + [o_spec] * n_add,
        out_specs=o_spec,
        scratch_shapes=[pltpu.VMEM((tm, tn), F32)],
        compiler_params=pltpu.CompilerParams(dimension_semantics=("parallel", "parallel", "arbitrary")),
        name=name,
    )(a, b, *add)


def _mm_resident(a, b, *, tb=False, add=(), out_dtype=F32, name):
    m_dim, k_dim = a.shape
    n_dim, k2 = b.shape if tb else b.shape[::-1]
    assert k2 == k_dim
    tm = min(DX_ROWS, m_dim)
    dims = NT if tb else NN

    def body(a_ref, b_ref, *rest):
        r = _dot(a_ref[...], b_ref[...], dims)
        for ar in rest[:-1]:
            r = r + ar[...]
        rest[-1][...] = r.astype(out_dtype)

    o_spec = pl.BlockSpec((tm, n_dim), lambda i: (i, 0))
    return pl.pallas_call(
        body,
        out_shape=jax.ShapeDtypeStruct((m_dim, n_dim), out_dtype),
        grid=(m_dim // tm,),
        in_specs=[pl.BlockSpec((tm, k_dim), lambda i: (i, 0)),
                  pl.BlockSpec(b.shape, lambda i: (0, 0), pipeline_mode=pl.Buffered(1))] + [o_spec] * len(add),
        out_specs=o_spec,
        compiler_params=pltpu.CompilerParams(dimension_semantics=("parallel",)),
        name=name,
    )(a, b, *add)


def _rowwise(fn, rows, consts, outs, accs=(), *, tm=256, name):
    rows = [r if isinstance(r, tuple) else (r, 0, r.shape[1]) for r in rows]
    s_dim = rows[0][0].shape[0]
    tm = min(tm, s_dim)
    assert s_dim % tm == 0 and all(arr.shape[0] == s_dim for arr, _, _ in rows)
    specs = [pl.BlockSpec((tm, width), functools.partial(lambda i, cb: (i, cb), cb=cb)) for _, cb, width in rows]
    args = [arr for arr, _, _ in rows]
    for c in consts:
        specs.append(pl.BlockSpec(c.shape, lambda i: (0, 0)))
        args.append(c)
    nr, nc, no = len(rows), len(consts), len(outs)
    flipped = [len(o) == 3 for o in outs]
    out_shape = [jax.ShapeDtypeStruct((o[0], s_dim) if t else (s_dim, o[0]), o[1]) for o, t in zip(outs, flipped)]
    out_specs = [pl.BlockSpec((o[0], tm), lambda i: (0, i)) if t else pl.BlockSpec((tm, o[0]), lambda i: (i, 0))
                 for o, t in zip(outs, flipped)]
    out_shape += [jax.ShapeDtypeStruct(sh, F32) for sh in accs]
    out_specs += [pl.BlockSpec(sh, lambda i: (0, 0)) for sh in accs]

    def body(*refs):
        r = [x[...].astype(F32) if x.dtype == BF16 else x[...] for x in refs[:nr]]
        c = [x[...] for x in refs[nr:nr + nc]]
        o_refs = refs[nr + nc:nr + nc + no]
        a_refs = refs[nr + nc + no:]
        o_vals, a_vals = fn(r, c)
        for ref, v, t in zip(o_refs, o_vals, flipped, strict=True):
            ref[...] = (jnp.transpose(v.astype(F32)) if t else v).astype(ref.dtype)
        if a_refs:
            @pl.when(pl.program_id(0) == 0)
            def _():
                for ref in a_refs:
                    ref[...] = jnp.zeros_like(ref)

            for ref, v in zip(a_refs, a_vals, strict=True):
                ref[...] += v

    res = pl.pallas_call(
        body,
        out_shape=out_shape,
        grid=(s_dim // tm,),
        in_specs=specs,
        out_specs=out_specs,
        compiler_params=pltpu.CompilerParams(dimension_semantics=("arbitrary" if accs else "parallel",)),
        name=name,
    )(*args)
    return res


def _colsum(v):
    return jnp.sum(v, axis=0, keepdims=True)


def _rowsum(v):
    return jnp.sum(v, axis=1, keepdims=True)


def _rowmean(v):
    return jnp.mean(v, axis=1, keepdims=True)


def _silu_grad(x):
    s = _sig(x)
    return s * (1.0 + x * (1.0 - s))


def _conv_taps(x, w, width=4):
    row = lax.broadcasted_iota(jnp.int32, x.shape, 0)
    c = x * w[width - 1:width, :]
    for s in range(1, width):
        c = c + jnp.where(row >= s, pltpu.roll(x, s, 0), 0.0) * w[width - 1 - s:width - s, :]
    return c


def _conv_fwd(proj_a, conv_w):
    s_dim = proj_a.shape[0]
    n_blk = 3 * N_HEADS

    def body(x_ref, w_ref, o_ref):
        j = pl.program_id(0)
        c = _conv_taps(x_ref[...], w_ref[...])
        y = c * _sig(c)
        r = lax.rsqrt(_rowsum(y * y) + EPS_RMS)
        fac = jnp.where(j < N_HEADS, r * (HEAD ** -0.5), jnp.where(j < 2 * N_HEADS, r, 1.0))
        o_ref[...] = y * fac

    return pl.pallas_call(
        body,
        out_shape=jax.ShapeDtypeStruct((s_dim, n_blk * HEAD), F32),
        grid=(n_blk,),
        in_specs=[pl.BlockSpec((s_dim, HEAD), lambda j: (0, j)), pl.BlockSpec((4, HEAD), lambda j: (0, j))],
        out_specs=pl.BlockSpec((s_dim, HEAD), lambda j: (0, j)),
        compiler_params=pltpu.CompilerParams(dimension_semantics=("parallel",)),
        name="conv_fwd",
    )(proj_a, conv_w)


def _conv_bwd(proj_a, conv_w, dq, dk, dv):
    s_dim = proj_a.shape[0]
    n_blk = 3 * N_HEADS

    def body(x_ref, w_ref, dq_ref, dk_ref, dv_ref, dx_ref, dw_ref):
        j = pl.program_id(0)
        x = x_ref[...]
        w = w_ref[...]
        do = jnp.where(j < N_HEADS, dq_ref[...], jnp.where(j < 2 * N_HEADS, dk_ref[...], dv_ref[...]))
        c = _conv_taps(x, w)
        sg = _sig(c)
        y = c * sg
        r = lax.rsqrt(_rowsum(y * y) + EPS_RMS)
        sc = jnp.where(j < N_HEADS, HEAD ** -0.5, 1.0)
        dy_n = sc * (r * do - y * (r * r * r) * _rowsum(do * y))
        dy = jnp.where(j < 2 * N_HEADS, dy_n, do)
        dc = dy * (sg * (1.0 + c * (1.0 - sg)))
        row = lax.broadcasted_iota(jnp.int32, x.shape, 0)
        dx = dc * w[3:4, :]
        dw_ref[3:4, :] = _colsum(dc * x)
        for s in range(1, 4):
            dx = dx + jnp.where(row < s_dim - s, pltpu.roll(dc, s_dim - s, 0), 0.0) * w[3 - s:4 - s, :]
            xs = jnp.where(row >= s, pltpu.roll(x, s, 0), 0.0)
            dw_ref[3 - s:4 - s, :] = _colsum(dc * xs)
        dx_ref[...] = dx.astype(dx_ref.dtype)

    hd = N_HEADS - 1
    return pl.pallas_call(
        body,
        out_shape=[jax.ShapeDtypeStruct((s_dim, n_blk * HEAD), BF16), jax.ShapeDtypeStruct((4, n_blk * HEAD), F32)],
        grid=(n_blk,),
        in_specs=[
            pl.BlockSpec((s_dim, HEAD), lambda j: (0, j)),
            pl.BlockSpec((4, HEAD), lambda j: (0, j)),
            pl.BlockSpec((s_dim, HEAD), lambda j: (0, jnp.minimum(j, hd))),
            pl.BlockSpec((s_dim, HEAD), lambda j: (0, jnp.clip(j - N_HEADS, 0, hd))),
            pl.BlockSpec((s_dim, HEAD), lambda j: (0, jnp.clip(j - 2 * N_HEADS, 0, hd))),
        ],
        out_specs=[pl.BlockSpec((s_dim, HEAD), lambda j: (0, j)), pl.BlockSpec((4, HEAD), lambda j: (0, j))],
        compiler_params=pltpu.CompilerParams(dimension_semantics=("parallel",)),
        name="conv_bwd",
    )(proj_a, conv_w, dq, dk, dv)


def _chunk_tri(n):
    r = np.arange(n)
    m = ((r[:, None] // CHUNK) == (r[None, :] // CHUNK)) & (r[:, None] >= r[None, :])
    m = m.astype(np.float32)
    return jnp.asarray(m), jnp.asarray(m.T)


def _softplus(z):
    return jnp.maximum(z, 0.0) + jnp.log(1.0 + jnp.exp(-jnp.abs(z)))


def _gates_fwd(proj_b, alog, dtb):
    tm = min(GROUP, proj_b.shape[0])
    tri, _ = _chunk_tri(tm)

    def fn(r, c):
        b = r[0]
        a = pltpu.roll(b, LANES - N_HEADS, 1)
        alog_, dtb_, tri_ = c
        g = -jnp.exp(alog_) * _softplus(a + dtb_)
        return [_sig(b), _dot32(tri_, g)], []

    return _rowwise(fn, [(proj_b, WB_BA // LANES, LANES)], [alog, dtb, tri],
                    [(LANES, F32), (LANES, F32)], tm=tm, name="gates_fwd")


def _gates_bwd(proj_b, alog, dtb, gc, d_beta, d_gc, d_egl_rows):
    tm = min(GROUP, proj_b.shape[0])
    _, tri_t = _chunk_tri(tm)

    def fn(r, c):
        b, gc_, d_beta_, d_gc_, d_egl_ = r
        a = pltpu.roll(b, LANES - N_HEADS, 1)
        alog_, dtb_, tri_t_ = c
        z = a + dtb_
        ea = jnp.exp(alog_)
        g = -ea * _softplus(z)
        dg = _dot32(tri_t_, d_gc_ + d_egl_ * jnp.exp(gc_))
        d_a = dg * (-ea) * _sig(z)
        beta = _sig(b)
        d_ba = d_beta_ * beta * (1.0 - beta) + pltpu.roll(d_a, N_HEADS, 1)
        return [d_ba], [_colsum(dg * g), _colsum(d_a)]

    return _rowwise(fn, [(proj_b, WB_BA // LANES, LANES), gc, d_beta, d_gc, d_egl_rows],
                    [alog, dtb, tri_t], [(LANES, BF16)], accs=[(1, LANES), (1, LANES)], tm=tm,
                    name="gates_bwd")


def _group_masks(n):
    r = lax.broadcasted_iota(jnp.int32, (n, n), 0)
    c = lax.broadcasted_iota(jnp.int32, (n, n), 1)
    same = (r // CHUNK) == (c // CHUNK)
    below, s = [], 2
    while s < CHUNK:
        below.append(jnp.logical_and((r // (2 * s)) == (c // (2 * s)),
                                     jnp.logical_and((r // s) % 2 == 1, (c // s) % 2 == 0)))
        s *= 2
    return dict(same=same, tril=jnp.logical_and(same, r >= c), strict=jnp.logical_and(same, r > c),
                last=c == (r // CHUNK) * CHUNK + (CHUNK - 1), eye=r == c, pair=(r // 2) == (c // 2), below=below)


def _inv_unit_lower(l_mats, mk):
    eye_f = mk["eye"].astype(F32)
    ts = [eye_f - jnp.where(mk["pair"], l_mat, 0.0) for l_mat in l_mats]
    for below in mk["below"]:
        halves = [_split_bf16(t) for t in ts]
        mids = [_dot3(h, jnp.where(below, l_mat, 0.0)) for h, l_mat in zip(halves, l_mats)]
        ts = [t - _dot3(m, h) for t, m, h in zip(ts, mids, halves)]
    return ts


def _unfold_blocks(folded, mask):
    n = folded.shape[0]
    return jnp.where(mask, jnp.concatenate([folded] * (n // CHUNK), axis=1), 0.0)


def _head_cols(beta, gc, gc_t, h):
    lane = lax.broadcasted_iota(jnp.int32, beta.shape, 1)
    sub = lax.broadcasted_iota(jnp.int32, gc_t.shape, 0)
    bcol = _rowsum(jnp.where(lane == h, beta, 0.0))
    gcol = _rowsum(jnp.where(lane == h, gc, 0.0))
    grow = _colsum(jnp.where(sub == h, gc_t, 0.0))
    return bcol, gcol, grow


def _prep_common(q, k, bcol, gcol, grow, mk, t_folded=None):
    n = q.shape[0]
    tril = mk["tril"]
    decay = jnp.where(tril, jnp.exp(jnp.where(tril, gcol - grow, 0.0)), 0.0)
    glast = _rowsum(jnp.where(mk["last"], jnp.broadcast_to(grow, (n, n)), 0.0))
    e = jnp.exp(gcol)
    ekt = jnp.exp(glast - gcol)
    kb = k * bcol
    kk = _dot(kb, k, NT)
    qk = _dot(q, k, NT)
    p = dict(decay=decay, e=e, ekt=ekt, kb=kb, kk=kk, qk=qk)
    if t_folded is not None:
        p["t"] = _unfold_blocks(t_folded, mk["same"])
    return p


GROUPS_PER_STEP = 4
SCAN_CHUNKS_PER_STEP = 4


def _fold_blocks(m):
    n = m.shape[0]
    out = m[:, 0:CHUNK]
    for b in range(1, n // CHUNK):
        out = out + m[:, b * CHUNK:(b + 1) * CHUNK]
    return out


def _gdr_prep_fwd(qkvn, beta, gc, gc_t):
    s_dim = qkvn.shape[0]
    tg = min(GROUP, s_dim)
    n_sub = min(GROUPS_PER_STEP, s_dim // tg)
    tb = tg * n_sub

    def body(q_ref, k_ref, v_ref, b_ref, g_ref, gt_ref, u_ref, w_ref, qd_ref, kt_ref, a_ref, t_ref):
        h = pl.program_id(0)
        mk = _group_masks(tg)
        parts = []
        for s in range(n_sub):
            rows = slice(s * tg, (s + 1) * tg)
            q, k, v = q_ref[rows, :], k_ref[rows, :], v_ref[rows, :]
            bcol, gcol, grow = _head_cols(b_ref[rows, :], g_ref[rows, :], gt_ref[:, rows], h)
            p = _prep_common(q, k, bcol, gcol, grow, mk)
            qd_ref[rows, :] = q * p["e"]
            kt_ref[rows, :] = k * p["ekt"]
            a_ref[rows, :] = _fold_blocks(jnp.where(mk["tril"], p["qk"] * p["decay"], 0.0))
            parts.append((rows, v * bcol, p["kb"] * p["e"], jnp.where(mk["strict"], p["kk"] * p["decay"], 0.0)))
        t_mats = _inv_unit_lower([part[3] for part in parts], mk)
        for (rows, vb, kbe, _), t_mat in zip(parts, t_mats):
            u_ref[rows, :] = _dot(t_mat, vb)
            w_ref[rows, :] = _dot(t_mat, kbe)
            t_ref[rows, :] = _fold_blocks(t_mat)

    row = lambda off: pl.BlockSpec((tb, HEAD), functools.partial(lambda h, m, off: (m, h + off), off=off))
    full = pl.BlockSpec((tb, LANES), lambda h, m: (m, 0))
    o_spec = pl.BlockSpec((tb, HEAD), lambda h, m: (m, h))
    a_spec = pl.BlockSpec((None, tb, CHUNK), lambda h, m: (h, m, 0))
    wide = jax.ShapeDtypeStruct((s_dim, N_HEADS * HEAD), F32)
    folded = jax.ShapeDtypeStruct((N_HEADS, s_dim, CHUNK), F32)
    return pl.pallas_call(
        body,
        out_shape=[wide, wide, wide, wide, folded, folded],
        grid=(N_HEADS, s_dim // tb),
        in_specs=[row(0), row(N_HEADS), row(2 * N_HEADS), full, full, pl.BlockSpec((8, tb), lambda h, m: (0, m))],
        out_specs=[o_spec, o_spec, o_spec, o_spec, a_spec, a_spec],
        compiler_params=pltpu.CompilerParams(dimension_semantics=("parallel", "parallel")),
        name="gdr_prep_fwd",
    )(qkvn, qkvn, qkvn, beta, gc, gc_t)


def _gdr_prep_bwd(qkvn, beta, gc, gc_t, t_fold, u, w, du, dw, dqd, dkt, d_a):
    s_dim = qkvn.shape[0]
    tg = min(GROUP, s_dim)
    n_sub = min(GROUPS_PER_STEP, s_dim // tg)
    tb = tg * n_sub

    def body(q_ref, k_ref, v_ref, b_ref, g_ref, gt_ref, t_ref, u_ref, w_ref, du_ref, dw_ref, dqd_ref, dkt_ref,
             da_ref, dq_ref, dk_ref, dv_ref, db_ref, dg_ref):
        h = pl.program_id(1)

        @pl.when(h == 0)
        def _():
            db_ref[...] = jnp.zeros_like(db_ref)
            dg_ref[...] = jnp.zeros_like(dg_ref)

        mk = _group_masks(tg)
        lane = lax.broadcasted_iota(jnp.int32, (tg, LANES), 1)
        for s in range(n_sub):
            rows = slice(s * tg, (s + 1) * tg)
            q, k, v = q_ref[rows, :], k_ref[rows, :], v_ref[rows, :]
            bcol, gcol, grow = _head_cols(b_ref[rows, :], g_ref[rows, :], gt_ref[:, rows], h)
            p = _prep_common(q, k, bcol, gcol, grow, mk, t_ref[rows, :])
            t_mat, decay, e, ekt, kb = p["t"], p["decay"], p["e"], p["ekt"], p["kb"]
            du_, dw_, dqd_, dkt_ = du_ref[rows, :], dw_ref[rows, :], dqd_ref[rows, :], dkt_ref[rows, :]
            dvb = _dot(t_mat, du_, TN)
            dkbe = _dot(t_mat, dw_, TN)
            d_l = -(_dot(dvb, u_ref[rows, :], NT) + _dot(dkbe, w_ref[rows, :], NT))
            m1 = jnp.where(mk["strict"], d_l, 0.0)
            m2 = _unfold_blocks(da_ref[rows, :], mk["tril"])
            d_kk = m1 * decay
            d_qk = m2 * decay
            d_decay = m1 * p["kk"] + m2 * p["qk"]
            dkb = _dot(d_kk, k) + dkbe * e
            dk = _dot(d_kk, kb, TN) + _dot(d_qk, q, TN) + dkt_ * ekt + dkb * bcol
            dq = _dot(d_qk, k) + dqd_ * e
            d_beta = _rowsum(dkb * k) + _rowsum(dvb * v)
            d_e = _rowsum(dkbe * kb) + _rowsum(dqd_ * q)
            d_ekt = _rowsum(dkt_ * k) * ekt
            d_diff = d_decay * decay
            d_grow = -_colsum(d_diff) + _colsum(jnp.where(mk["last"], jnp.broadcast_to(d_ekt, (tg, tg)), 0.0))
            d_gcol = d_e * e - d_ekt + _rowsum(d_diff)
            d_gcol = d_gcol + _rowsum(jnp.where(mk["eye"], jnp.broadcast_to(d_grow, (tg, tg)), 0.0))
            dq_ref[rows, :] = dq
            dk_ref[rows, :] = dk
            dv_ref[rows, :] = dvb * bcol
            db_ref[rows, :] = jnp.where(lane == h, d_beta, db_ref[rows, :])
            dg_ref[rows, :] = jnp.where(lane == h, d_gcol, dg_ref[rows, :])

    row = lambda off: pl.BlockSpec((tb, HEAD), functools.partial(lambda m, h, off: (m, h + off), off=off))
    full = pl.BlockSpec((tb, LANES), lambda m, h: (m, 0))
    o_spec = pl.BlockSpec((tb, HEAD), lambda m, h: (m, h))
    a_spec = pl.BlockSpec((None, tb, CHUNK), lambda m, h: (h, m, 0))
    wide = jax.ShapeDtypeStruct((s_dim, N_HEADS * HEAD), F32)
    lanes = jax.ShapeDtypeStruct((s_dim, LANES), F32)
    return pl.pallas_call(
        body,
        out_shape=[wide, wide, wide, lanes, lanes],
        grid=(s_dim // tb, N_HEADS),
        in_specs=[row(0), row(N_HEADS), row(2 * N_HEADS), full, full, pl.BlockSpec((8, tb), lambda m, h: (0, m)),
                  a_spec, o_spec, o_spec, o_spec, o_spec, o_spec, o_spec, a_spec],
        out_specs=[o_spec, o_spec, o_spec, full, full],
        compiler_params=pltpu.CompilerParams(dimension_semantics=("parallel", "arbitrary")),
        name="gdr_prep_bwd",
    )(qkvn, qkvn, qkvn, beta, gc, gc_t, t_fold, u, w, du, dw, dqd, dkt, d_a)


def _gdr_scan_fwd(u, w, qd, kt, a_mat, gc):
    s_dim = u.shape[0]
    n_chunks = s_dim // CHUNK
    per = min(SCAN_CHUNKS_PER_STEP, n_chunks)
    tb = per * CHUNK

    def body(u_ref, w_ref, qd_ref, kt_ref, a_ref, g_ref, o_ref, st_ref, state):
        @pl.when(pl.program_id(0) == 0)
        def _():
            state[...] = jnp.zeros_like(state)

        heads = range(N_HEADS)
        cols = [slice(h * HEAD, (h + 1) * HEAD) for h in heads]
        for i in range(per):
            rows = slice(i * CHUNK, (i + 1) * CHUNK)
            egl = jnp.exp(g_ref[(i + 1) * CHUNK - 1:(i + 1) * CHUNK, :])
            s_b = [state[h].astype(BF16) for h in heads]
            for h in heads:
                st_ref[i, h] = state[h]
            ws = [_dot(w_ref[rows, cs], s) for cs, s in zip(cols, s_b)]
            qs = [_dot(qd_ref[rows, cs], s) for cs, s in zip(cols, s_b)]
            vns = [(u_ref[rows, cs] - ws_h).astype(BF16) for cs, ws_h in zip(cols, ws)]
            avs = [_dot(a_ref[h, rows, :], vn) for h, vn in zip(heads, vns)]
            kvs = [_dot(kt_ref[rows, cs], vn, TN) for cs, vn in zip(cols, vns)]
            for h, cs in zip(heads, cols):
                o_ref[rows, cs] = qs[h] + avs[h]
                state[h] = state[h] * egl[:, h:h + 1] + kvs[h]

    wide = pl.BlockSpec((tb, N_HEADS * HEAD), lambda n: (n, 0))
    return pl.pallas_call(
        body,
        out_shape=[jax.ShapeDtypeStruct((s_dim, N_HEADS * HEAD), F32),
                   jax.ShapeDtypeStruct((n_chunks, N_HEADS, HEAD, HEAD), F32)],
        grid=(n_chunks // per,),
        in_specs=[wide, wide, wide, wide, pl.BlockSpec((N_HEADS, tb, CHUNK), lambda n: (0, n, 0)),
                  pl.BlockSpec((tb, LANES), lambda n: (n, 0))],
        out_specs=[wide, pl.BlockSpec((per, N_HEADS, HEAD, HEAD), lambda n: (n, 0, 0, 0))],
        scratch_shapes=[pltpu.VMEM((N_HEADS, HEAD, HEAD), F32)],
        compiler_params=pltpu.CompilerParams(dimension_semantics=("arbitrary",)),
        name="gdr_scan_fwd",
    )(u, w, qd, kt, a_mat, gc)


def _gdr_scan_bwd(u, w, qd, kt, a_mat, gc, states, d_o):
    s_dim = u.shape[0]
    n_chunks = s_dim // CHUNK
    per = min(SCAN_CHUNKS_PER_STEP, n_chunks)
    tb = per * CHUNK
    last = n_chunks // per - 1

    def body(u_ref, w_ref, qd_ref, kt_ref, a_ref, g_ref, st_ref, do_ref,
             du_ref, dw_ref, dqd_ref, dkt_ref, da_ref, de_ref, d_state):
        @pl.when(pl.program_id(0) == 0)
        def _():
            d_state[...] = jnp.zeros_like(d_state)

        heads = range(N_HEADS)
        cols = [slice(h * HEAD, (h + 1) * HEAD) for h in heads]
        for i in reversed(range(per)):
            rows = slice(i * CHUNK, (i + 1) * CHUNK)
            egl = jnp.exp(g_ref[(i + 1) * CHUNK - 1:(i + 1) * CHUNK, :])
            s_b = [st_ref[i, h].astype(BF16) for h in heads]
            ds_b = [d_state[h].astype(BF16) for h in heads]
            dos = [do_ref[rows, cs].astype(BF16) for cs in cols]
            w_b = [w_ref[rows, cs].astype(BF16) for cs in cols]
            ws = [_dot(w_h, s) for w_h, s in zip(w_b, s_b)]
            ados = [_dot(a_ref[h, rows, :], do, TN) for h, do in zip(heads, dos)]
            kds = [_dot(kt_ref[rows, cs], ds) for cs, ds in zip(cols, ds_b)]
            dqds = [_dot(do, s, NT) for do, s in zip(dos, s_b)]
            qdos = [_dot(qd_ref[rows, cs], do, TN) for cs, do in zip(cols, dos)]
            vns = [(u_ref[rows, cs] - ws_h).astype(BF16) for cs, ws_h in zip(cols, ws)]
            dvns = [a + k_ for a, k_ in zip(ados, kds)]
            dvn_b = [d.astype(BF16) for d in dvns]
            das = [_dot(do, vn, NT) for do, vn in zip(dos, vns)]
            dkts = [_dot(vn, ds, NT) for vn, ds in zip(vns, ds_b)]
            dws = [_dot(d, s, NT) for d, s in zip(dvn_b, s_b)]
            wds = [_dot(w_h, d, TN) for w_h, d in zip(w_b, dvn_b)]
            for h, cs in zip(heads, cols):
                ds_n = d_state[h]
                de = jnp.sum(_rowsum(ds_n * st_ref[i, h]), axis=0, keepdims=True)
                de_ref[i, h:h + 1, :] = jnp.broadcast_to(de, (1, LANES))
                dqd_ref[rows, cs] = dqds[h]
                da_ref[h, rows, :] = das[h]
                dkt_ref[rows, cs] = dkts[h]
                du_ref[rows, cs] = dvns[h]
                dw_ref[rows, cs] = -dws[h]
                d_state[h] = ds_n * egl[:, h:h + 1] + qdos[h] - wds[h]

    wide = pl.BlockSpec((tb, N_HEADS * HEAD), lambda n: (last - n, 0))
    a_spec = pl.BlockSpec((N_HEADS, tb, CHUNK), lambda n: (0, last - n, 0))
    wide_shape = jax.ShapeDtypeStruct((s_dim, N_HEADS * HEAD), F32)
    return pl.pallas_call(
        body,
        out_shape=[wide_shape, wide_shape, wide_shape, wide_shape,
                   jax.ShapeDtypeStruct((N_HEADS, s_dim, CHUNK), F32),
                   jax.ShapeDtypeStruct((n_chunks, N_HEADS, LANES), F32)],
        grid=(n_chunks // per,),
        in_specs=[wide, wide, wide, wide, a_spec, pl.BlockSpec((tb, LANES), lambda n: (last - n, 0)),
                  pl.BlockSpec((per, N_HEADS, HEAD, HEAD), lambda n: (last - n, 0, 0, 0)), wide],
        out_specs=[wide, wide, wide, wide, a_spec, pl.BlockSpec((per, N_HEADS, LANES), lambda n: (last - n, 0, 0))],
        scratch_shapes=[pltpu.VMEM((N_HEADS, HEAD, HEAD), F32)],
        compiler_params=pltpu.CompilerParams(dimension_semantics=("arbitrary",)),
        name="gdr_scan_bwd",
    )(u, w, qd, kt, a_mat, gc, states, d_o)


FUSED_ROWS = 512


def _gdr_out_fwd(o_dn, proj_a, dn_w, w_br):
    def fn(r, c):
        o, z = r
        w_, w_br_ = c
        outs = []
        for h in range(N_HEADS):
            cs = slice(h * HEAD, (h + 1) * HEAD)
            oh, zh = o[:, cs], z[:, cs]
            rr = lax.rsqrt(_rowmean(oh * oh) + EPS_RMS)
            outs.append(oh * rr * w_ * (zh * _sig(zh)))
        og = jnp.concatenate(outs, axis=1).astype(BF16)
        return [og, _dot(og, w_br_)], []

    return _rowwise(fn, [o_dn, (proj_a, 3, D_MODEL)], [dn_w, w_br], [(D_MODEL, BF16), (D_MODEL, BF16)],
                    tm=FUSED_ROWS, name="gdr_out_fwd")


def _gdr_out_bwd(o_dn, proj_a, d_y_dn, dn_w, w_br):
    def fn(r, c):
        o, z, dy = r
        w_, w_br_ = c
        dg = _dot(dy, w_br_, NT)
        d_o, d_z = [], []
        d_w = jnp.zeros((1, HEAD), F32)
        for h in range(N_HEADS):
            cs = slice(h * HEAD, (h + 1) * HEAD)
            oh, zh, dgh = o[:, cs], z[:, cs], dg[:, cs]
            rr = lax.rsqrt(_rowmean(oh * oh) + EPS_RMS)
            sz = zh * _sig(zh)
            d_n = dgh * sz
            d_z.append(dgh * (oh * rr * w_) * _silu_grad(zh))
            d_w = d_w + _colsum(d_n * oh * rr)
            gw = d_n * w_
            d_o.append(rr * gw - oh * (rr * rr * rr) * _rowmean(gw * oh))
        return [jnp.concatenate(d_o, axis=1), jnp.concatenate(d_z, axis=1)], [d_w]

    return _rowwise(fn, [o_dn, (proj_a, 3, D_MODEL), d_y_dn], [dn_w, w_br], [(D_MODEL, F32), (D_MODEL, BF16)],
                    accs=[(1, HEAD)], tm=FUSED_ROWS, name="gdr_out_bwd")


def _rms_fwd(x, w):
    r = lax.rsqrt(_rowmean(x * x) + EPS_RMS)
    return x * r * w


def _rms_bwd(x, w, dy):
    r = lax.rsqrt(_rowmean(x * x) + EPS_RMS)
    gw = dy * w
    return r * gw - x * (r * r * r) * _rowmean(gw * x), _colsum(dy * x * r)


def _rope_consts():
    inv = ROPE_BASE ** (-np.arange(0, ROPE, 2, dtype=np.float32) / ROPE)
    t = np.zeros((4, LANES), np.float32)
    t[0, :32] = inv
    t[0, 32:64] = inv
    t[1, :64] = 1.0
    t[2, 32:64] = 1.0
    t[3, :32] = -1.0
    return jnp.asarray(t)


def _rope_tables(pos, consts, width):
    ang = pos * consts[0:1, :]
    cosv, sinv = jnp.cos(ang), jnp.sin(ang)
    reps = width // LANES
    tile = (lambda t: jnp.concatenate([t] * reps, axis=1)) if reps > 1 else (lambda t: t)
    return tile(cosv * consts[1:2, :]), tile(sinv * consts[2:3, :]), tile(sinv * consts[3:4, :])


def _rope_apply(t, tabs):
    cos_t, sin_a, sin_b = tabs
    width = t.shape[1]
    return t * cos_t + pltpu.roll(t, 32, 1) * sin_a + pltpu.roll(t, width - 32, 1) * sin_b


def _rope_transpose(d, tabs):
    cos_t, sin_a, sin_b = tabs
    width = d.shape[1]
    return d * cos_t + pltpu.roll(d * sin_a, width - 32, 1) + pltpu.roll(d * sin_b, 32, 1)


QK_HEAD = 2 * HEAD


def _interleave_heads(a, b):
    parts = []
    for h in range(N_HEADS):
        parts.append(a[:, h * HEAD:(h + 1) * HEAD])
        parts.append(b if b.shape[1] == LANES else b[:, h * LANES:(h + 1) * LANES])
    return jnp.concatenate(parts, axis=1)


def _mla_rows(proj_b):
    return [(proj_b, WB_CQ // Q_LORA, Q_LORA), (proj_b, WB_CKV // KV_LORA, KV_LORA), (proj_b, WB_KR // LANES, LANES)]


def _mla_prep_fwd(proj_b, pos, qn_w, kvn_w, uq, uk, uv):
    def fn(r, c):
        cq, ckv, kr, pos_ = r
        qn_w_, kvn_w_, uq_, uk_, uv_, rope = c
        c_q = _rms_fwd(cq, qn_w_).astype(BF16)
        c_kv = _rms_fwd(ckv, kvn_w_).astype(BF16)
        qf = _dot(c_q, uq_)
        qr = _rope_apply(qf[:, D_MODEL:], _rope_tables(pos_, rope, D_MODEL))
        kr = _rope_apply(kr, _rope_tables(pos_, rope, LANES))
        kc = _interleave_heads(_dot(c_kv, uk_), kr)
        v = _dot(c_kv, uv_)
        return [c_q, c_kv, _interleave_heads(qf[:, :D_MODEL], qr) * SCALE, kc, v, kc, v], []

    wide2 = N_HEADS * QK_HEAD
    return _rowwise(fn, _mla_rows(proj_b) + [pos], [qn_w, kvn_w, uq, uk, uv, _rope_consts()],
                    [(Q_LORA, BF16), (KV_LORA, BF16), (wide2, BF16), (wide2, BF16), (D_MODEL, BF16),
                     (wide2, BF16, "T"), (D_MODEL, BF16, "T")], tm=FUSED_ROWS, name="mla_prep_fwd")


def _mla_prep_bwd(proj_b, pos, d_qc, d_kc, d_v, qn_w, kvn_w, uq, uk, uv):
    def fn(r, c):
        cq, ckv, _, pos_, dq, dk, dv = r
        qn_w_, kvn_w_, uq_, uk_, uv_, rope = c
        even = lambda t: jnp.concatenate([t[:, (2 * h) * LANES:(2 * h + 1) * LANES] for h in range(N_HEADS)], axis=1)
        odd = lambda t: jnp.concatenate([t[:, (2 * h + 1) * LANES:(2 * h + 2) * LANES] for h in range(N_HEADS)], axis=1)
        d_qr_raw = _rope_transpose(odd(dq), _rope_tables(pos_, rope, D_MODEL)) * SCALE
        d_qf = jnp.concatenate([even(dq) * SCALE, d_qr_raw], axis=1).astype(BF16)
        d_kn = even(dk).astype(BF16)
        dkr = dk[:, LANES:2 * LANES]
        for h in range(1, N_HEADS):
            dkr = dkr + dk[:, (2 * h + 1) * LANES:(2 * h + 2) * LANES]
        d_cq, d_qnw = _rms_bwd(cq, qn_w_, _dot(d_qf, uq_, NT))
        d_ckv, d_kvnw = _rms_bwd(ckv, kvn_w_, _dot(d_kn, uk_, NT) + _dot(dv, uv_, NT))
        return [d_qf, d_kn, d_cq, d_ckv, _rope_transpose(dkr, _rope_tables(pos_, rope, LANES))], [d_qnw, d_kvnw]

    return _rowwise(fn, _mla_rows(proj_b) + [pos, d_qc, d_kc, d_v], [qn_w, kvn_w, uq, uk, uv, _rope_consts()],
                    [(2 * D_MODEL, BF16), (D_MODEL, BF16), (Q_LORA, BF16), (KV_LORA, BF16), (LANES, BF16)],
                    accs=[(1, Q_LORA), (1, KV_LORA)], tm=FUSED_ROWS, name="mla_prep_bwd")


def _causal_mask_t(st, key0, query0):
    key = lax.broadcasted_iota(jnp.int32, st.shape, 0) + key0
    query = lax.broadcasted_iota(jnp.int32, st.shape, 1) + query0
    return jnp.where(key <= query, st, NEG_BIG)


def _attn_tiles(s_dim):
    tq = min(512, s_dim)
    n_chains = 2 if s_dim >= 2 * tq else 1
    return tq, n_chains, min(512, s_dim)


def _diagonal_chains(t, tq, n_chains, tk):
    return [(c, (t + 1) * tk - 1 > c * tq) for c in range(n_chains) if t * tk < (c + 1) * tq]


def _attn_fwd(qc, kc, vt):
    s_dim = qc.shape[0]
    tq, n_chains, tk = _attn_tiles(s_dim)
    tqs = tq * n_chains

    def body(q_ref, k_ref, vt_ref, o_ref, lse_ref, m_s, l_s, acc):
        qi = pl.program_id(1)
        m_s[...] = jnp.full_like(m_s, NEG_BIG)
        l_s[...] = jnp.zeros_like(l_s)
        acc[...] = jnp.zeros_like(acc)

        def make_step(chains):
            def step(j, carry):
                ks = pl.multiple_of(j * tk, tk)
                kb, vtb = k_ref[pl.ds(ks, tk), :], vt_ref[:, pl.ds(ks, tk)]
                cols = [slice(c * tq, (c + 1) * tq) for c, _ in chains]
                sts = [_dot(kb, q_ref[cs, :], NT) for cs in cols]
                sts = [_causal_mask_t(st, j * tk, qi * tqs + c * tq) if masked else st
                       for st, (c, masked) in zip(sts, chains)]
                m_prevs = [m_s[:, cs] for cs in cols]
                m_news = [jnp.maximum(mp, jnp.max(st, axis=0, keepdims=True)) for mp, st in zip(m_prevs, sts)]
                alphas = [jnp.exp(mp - mn) for mp, mn in zip(m_prevs, m_news)]
                pts = [jnp.exp(st - mn) for st, mn in zip(sts, m_news)]
                pvs = [_dot(vtb, pt) for pt in pts]
                for cs, mn, al, pt, pv in zip(cols, m_news, alphas, pts, pvs):
                    l_s[:, cs] = al * l_s[:, cs] + _colsum(pt)
                    m_s[:, cs] = mn
                    acc[:, cs] = acc[:, cs] * al + pv
                return carry
            return step

        below = qi * (tqs // tk)
        lax.fori_loop(0, below, make_step([(c, False) for c in range(n_chains)]), 0)
        for t in range(tqs // tk):
            make_step(_diagonal_chains(t, tq, n_chains, tk))(below + t, 0)
        l = l_s[...]
        o_ref[...] = jnp.transpose(acc[...] / l)
        lse_ref[...] = m_s[...] + jnp.log(l)

    return pl.pallas_call(
        body,
        out_shape=[jax.ShapeDtypeStruct((s_dim, N_HEADS * HEAD), F32), jax.ShapeDtypeStruct((N_HEADS, 1, s_dim), F32)],
        grid=(N_HEADS, s_dim // tqs),
        in_specs=[pl.BlockSpec((tqs, QK_HEAD), lambda h, qi: (qi, h)),
                  pl.BlockSpec((s_dim, QK_HEAD), lambda h, qi: (0, h)),
                  pl.BlockSpec((HEAD, s_dim), lambda h, qi: (h, 0))],
        out_specs=[pl.BlockSpec((tqs, HEAD), lambda h, qi: (qi, h)),
                   pl.BlockSpec((None, 1, tqs), lambda h, qi: (h, 0, qi))],
        scratch_shapes=[pltpu.VMEM((1, tqs), F32), pltpu.VMEM((1, tqs), F32), pltpu.VMEM((HEAD, tqs), F32)],
        compiler_params=pltpu.CompilerParams(dimension_semantics=("parallel", "parallel")),
        name="attn_fwd",
    )(qc, kc, vt)


def _attn_bwd(qc, kc, kct, v, o, d_o, lse):
    s_dim = qc.shape[0]
    tq, n_chains, tk = _attn_tiles(s_dim)
    tqs = tq * n_chains

    def body(q_ref, k_ref, kt_ref, v_ref, o_ref, do_ref, lse_ref, dq_ref, dk_ref, dv_ref, dqt_acc, dv_acc):
        qi = pl.program_id(1)

        @pl.when(qi == 0)
        def _():
            dk_ref[...] = jnp.zeros_like(dk_ref)
            dv_acc[...] = jnp.zeros_like(dv_acc)

        dqt_acc[...] = jnp.zeros_like(dqt_acc)
        do_f = do_ref[...]
        do_all = do_f.astype(BF16)
        q_all = q_ref[...]
        lse_row = lse_ref[...]
        delta_row = _dot3(jnp.ones((8, HEAD), F32), o_ref[...] * do_f, NT)[0:1, :]

        def make_step(chains):
            rows = slice(chains[0][0] * tq, (chains[-1][0] + 1) * tq)

            def step(j, carry):
                ks = pl.multiple_of(j * tk, tk)
                kb, vb, ktb = k_ref[pl.ds(ks, tk), :], v_ref[pl.ds(ks, tk), :], kt_ref[:, pl.ds(ks, tk)]
                cols = [slice(c * tq, (c + 1) * tq) for c, _ in chains]
                sts = [_dot(kb, q_all[cs, :], NT) for cs in cols]
                sts = [_causal_mask_t(st, j * tk, qi * tqs + c * tq) if masked else st
                       for st, (c, masked) in zip(sts, chains)]
                dpts = [_dot(vb, do_all[cs, :], NT) for cs in cols]
                pts = [jnp.exp(st - lse_row[:, cs]) for st, cs in zip(sts, cols)]
                dsts = [(pt * (dpt - delta_row[:, cs])).astype(BF16) for pt, dpt, cs in zip(pts, dpts, cols)]
                pts = [pt.astype(BF16) for pt in pts]
                dqs = [_dot(ktb, dst) for dst in dsts]
                for cs, dq in zip(cols, dqs):
                    dqt_acc[:, cs] += dq
                pt_all = jnp.concatenate(pts, axis=1) if len(chains) > 1 else pts[0]
                dst_all = jnp.concatenate(dsts, axis=1) if len(chains) > 1 else dsts[0]
                dk_ref[pl.ds(ks, tk), :] += _dot(dst_all, q_all[rows, :])
                dv_acc[pl.ds(ks, tk), :] += _dot(pt_all, do_all[rows, :])
                return carry
            return step

        below = qi * (tqs // tk)
        lax.fori_loop(0, below, make_step([(c, False) for c in range(n_chains)]), 0)
        for t in range(tqs // tk):
            make_step(_diagonal_chains(t, tq, n_chains, tk))(below + t, 0)
        dq_ref[...] = jnp.transpose(dqt_acc[...])

        @pl.when(qi == s_dim // tqs - 1)
        def _():
            dv_ref[...] = dv_acc[...].astype(dv_ref.dtype)

    q_spec = pl.BlockSpec((tqs, QK_HEAD), lambda h, qi: (qi, h))
    o_spec = pl.BlockSpec((tqs, HEAD), lambda h, qi: (qi, h))
    k_spec = pl.BlockSpec((s_dim, QK_HEAD), lambda h, qi: (0, h))
    v_spec = pl.BlockSpec((s_dim, HEAD), lambda h, qi: (0, h))
    wide2 = jax.ShapeDtypeStruct((s_dim, N_HEADS * QK_HEAD), F32)
    return pl.pallas_call(
        body,
        out_shape=[wide2, wide2, jax.ShapeDtypeStruct((s_dim, N_HEADS * HEAD), BF16)],
        grid=(N_HEADS, s_dim // tqs),
        in_specs=[q_spec, k_spec, pl.BlockSpec((QK_HEAD, s_dim), lambda h, qi: (h, 0)), v_spec, o_spec, o_spec,
                  pl.BlockSpec((None, 1, tqs), lambda h, qi: (h, 0, qi))],
        out_specs=[q_spec, k_spec, v_spec],
        scratch_shapes=[pltpu.VMEM((QK_HEAD, tqs), F32), pltpu.VMEM((s_dim, HEAD), F32)],
        compiler_params=pltpu.CompilerParams(dimension_semantics=("parallel", "arbitrary")),
        name="attn_bwd",
    )(qc, kc, kct, v, o, d_o, lse)


def _mix_proj_ln1(y_dn, y_mla, proj_g, x, w_o, g, b):
    s_dim = x.shape[0]
    tm = min(512, s_dim)

    def body(yd_ref, ym_ref, g_ref, x_ref, w_ref, lg_ref, lb_ref, mixed_ref, a1_ref, h1_ref, h1b_ref):
        gates = g_ref[...].astype(F32)
        mixed = (_sig(gates[:, :D_MODEL]) * yd_ref[...].astype(F32)
                 + _sig(gates[:, D_MODEL:]) * ym_ref[...].astype(F32)).astype(BF16)
        a1 = _dot(mixed, w_ref[...])
        xh, _ = _ln_stats(ALPHA * x_ref[...] + a1)
        y = xh * lg_ref[...] + lb_ref[...]
        mixed_ref[...] = mixed
        a1_ref[...] = a1
        h1_ref[...] = y
        h1b_ref[...] = y.astype(BF16)

    row = lambda width: pl.BlockSpec((tm, width), lambda i: (i, 0))
    whole = lambda a: pl.BlockSpec(a.shape, lambda i: (0, 0))
    sds = lambda dt: jax.ShapeDtypeStruct((s_dim, D_MODEL), dt)
    return pl.pallas_call(
        body,
        out_shape=[sds(BF16), sds(F32), sds(F32), sds(BF16)],
        grid=(s_dim // tm,),
        in_specs=[row(D_MODEL), row(D_MODEL), row(2 * D_MODEL), row(D_MODEL), whole(w_o), whole(g), whole(b)],
        out_specs=[row(D_MODEL)] * 4,
        compiler_params=pltpu.CompilerParams(dimension_semantics=("parallel",)),
        name="mix_proj_ln1",
    )(y_dn, y_mla, proj_g, x, w_o, g, b)


def _ln1_mix_bwd(x, a1, d_h1, d_pg, y_dn, y_mla, proj_g, g, w_o, w_pg):
    def fn(r, c):
        x_, a1_, dy, dpg, yd, ym, gates = r
        g_, w_o_, w_pg_ = c
        dy = dy + _dot(dpg, w_pg_, NT)
        xh, rr = _ln_stats(ALPHA * x_ + a1_)
        dz = _ln_bwd(dy, xh, rr, g_)
        dz_b = dz.astype(BF16)
        dm = _dot(dz_b, w_o_, NT)
        sd, sm = _sig(gates[:, :D_MODEL]), _sig(gates[:, D_MODEL:])
        d_g = jnp.concatenate([dm * yd * sd * (1.0 - sd), dm * ym * sm * (1.0 - sm)], axis=1)
        return [dz_b, ALPHA * dz, d_g, dm * sd, dm * sm], [_colsum(dy * xh), _colsum(dy)]

    return _rowwise(fn, [x, a1, d_h1, d_pg, y_dn, y_mla, proj_g], [g, w_o, w_pg],
                    [(D_MODEL, BF16), (D_MODEL, F32), (2 * D_MODEL, BF16), (D_MODEL, BF16), (D_MODEL, BF16)],
                    accs=[(1, D_MODEL), (1, D_MODEL)], tm=FUSED_ROWS, name="ln1_mix_bwd")


def _ln_stats(z):
    mu = _rowmean(z)
    zc = z - mu
    r = lax.rsqrt(_rowmean(zc * zc) + EPS_LN)
    return zc * r, r


def _ln_bwd(dy, xh, r, g):
    dxh = dy * g
    return r * (dxh - _rowmean(dxh) - xh * _rowmean(dxh * xh))


def _ffn_in_act(h1b, w_t):
    s_dim, k_dim = h1b.shape
    hidden = w_t.shape[0] // 2
    tm, tn = min(512, s_dim), _pick_wide(hidden)
    nt = hidden // tn

    def body(a_ref, bg_ref, bu_ref, gt_ref, up_ref, act_ref):
        a = a_ref[...]
        gt, up = _dot(a, bg_ref[...], NT), _dot(a, bu_ref[...], NT)
        gt_ref[...] = gt.astype(BF16)
        up_ref[...] = up.astype(BF16)
        act_ref[...] = (gt * _sig(gt) * up).astype(BF16)

    o_spec = pl.BlockSpec((tm, tn), lambda j, i: (i, j))
    sds = jax.ShapeDtypeStruct((s_dim, hidden), BF16)
    return pl.pallas_call(
        body,
        out_shape=[sds, sds, sds],
        grid=(nt, s_dim // tm),
        in_specs=[pl.BlockSpec((tm, k_dim), lambda j, i: (i, 0)), pl.BlockSpec((tn, k_dim), lambda j, i: (j, 0)),
                  pl.BlockSpec((tn, k_dim), lambda j, i: (j + nt, 0))],
        out_specs=[o_spec, o_spec, o_spec],
        compiler_params=pltpu.CompilerParams(dimension_semantics=("parallel", "parallel")),
        name="ffn_in_act",
    )(h1b, w_t, w_t)


def _act_bwd(gt, up, d_act):
    def fn(r, c):
        gt_, up_, da = r
        return [jnp.concatenate([da * up_ * _silu_grad(gt_), da * gt_ * _sig(gt_)], axis=1)], []

    return _rowwise(fn, [gt, up, d_act], [], [(2 * FFN_HIDDEN, BF16)], name="act_bwd")[0]


def _tail(h1, ffn, p, tgt, g, b, w_pg, w_ple_t):
    def fn(r, c):
        h1_, ffn_, p_, t_ = r
        pg_ = _dot(h1_, c[2])
        pp_ = _dot(p_, c[3], NT)
        sp = _sig(pg_)
        xh, rr = _ln_stats(ALPHA * h1_ + ffn_ + sp * pp_)
        y = xh * c[0] + c[1]
        err = y - t_
        dy = err * (1.0 / D_MODEL)
        dz = _ln_bwd(dy, xh, rr, c[0])
        loss = jnp.sum(0.5 * _rowmean(err * err), axis=0, keepdims=True)
        return ([dz, dz * pp_ * sp * (1.0 - sp), dz * sp, ALPHA * dz],
                [_colsum(dy * xh), _colsum(dy), jnp.broadcast_to(loss, (1, LANES))])

    return _rowwise(fn, [h1, ffn, p, tgt], [g, b, w_pg, w_ple_t], [(D_MODEL, BF16)] * 3 + [(D_MODEL, F32)],
                    accs=[(1, D_MODEL), (1, D_MODEL), (1, LANES)], tm=FUSED_ROWS, name="tail")


def _local_step(x, p, pos, tgt, w, late_weights, emit):
    w = dict(w)
    s_dim = x.shape[0]
    pb = p.astype(BF16)
    proj_a, proj_g, proj_b, xb = _input_proj(x, w["w_in_t"], w["wg_t"], w["wb_t"])
    qkvn = _conv_fwd(proj_a, w["conv"])
    beta, gc = _gates_fwd(proj_b, w["alog"], w["dtb"])
    gc_t = jnp.transpose(gc[:, :N_HEADS])
    u, w_, qd, kt, a_mat, t_fold = _gdr_prep_fwd(qkvn, beta, gc, gc_t)
    o_dn, states = _gdr_scan_fwd(u, w_, qd, kt, a_mat, gc)
    w.update(late_weights("mix", o_dn))
    og, y_dn = _gdr_out_fwd(o_dn, proj_a, w["dnw"], w["br_dn"])
    c_q, c_kv, qc, kc, vv, kct, vt = _mla_prep_fwd(proj_b, pos, w["qnw"], w["kvnw"], w["uq"], w["uk"], w["uv"])
    o_mla, lse = _attn_fwd(qc, kc, vt)
    y_mla = _mm_resident(o_mla, w["br_mla"], out_dtype=BF16, name="f_y_mla")
    mixed, a1, h1, h1b = _mix_proj_ln1(y_dn, y_mla, proj_g, x, w["wo"], w["ln1g"], w["ln1b"])
    w.update(late_weights("ffn", a1))
    gt, up, act = _ffn_in_act(h1b, w["ffn_in_t"])
    ffn = _mm_resident(act, w["ffn_out"], name="f_ffn")
    g = {}
    dz2, d_pg, d_pp, dh1a, g["ln2g"], g["ln2b"], loss = _tail(h1, ffn, pb, tgt, w["ln2g"], w["ln2b"],
                                                            w["ple_gate"], w["ple_t"])
    g["ple_t"] = _mm(d_pp, pb, ta=True, out_dtype=BF16, name="b_w_ple")
    g["ple_gate"] = _mm(h1b, d_pg, ta=True, out_dtype=BF16, name="b_w_ple_gate")
    g["ffn_out"] = _mm(act, dz2, ta=True, out_dtype=BF16, name="b_w_ffn_out")
    d_act = _mm_resident(dz2, w["ffn_out"], tb=True, out_dtype=BF16, name="b_act")
    d_gu = _act_bwd(gt, up, d_act)
    g["ffn_in_t"] = _mm(d_gu, h1b, ta=True, out_dtype=BF16, name="b_w_ffn_in")
    d_gu = emit("ffn", g, d_gu)
    d_h1 = _mm_resident(d_gu, w["ffn_in_t"], add=(dh1a,), name="b_h1_ffn")
    dz1, dxa, d_proj_g, d_y_dn, d_y_mla, g["ln1g"], g["ln1b"] = _ln1_mix_bwd(
        x, a1, d_h1, d_pg, y_dn, y_mla, proj_g, w["ln1g"], w["wo"], w["ple_gate"])
    g["wo"] = _mm(mixed, dz1, ta=True, out_dtype=BF16, name="b_w_o")
    g["br_mla"] = _mm(o_mla, d_y_mla, ta=True, out_dtype=BF16, name="b_w_br_mla")
    d_o_mla = _mm_resident(d_y_mla, w["br_mla"], tb=True, out_dtype=BF16, name="b_o_mla")
    d_qc, d_kc, d_v = _attn_bwd(qc, kc, kct, vv, o_mla, d_o_mla, lse)
    d_q_full, d_kn, d_cq, d_ckv, d_kr, g["qnw"], g["kvnw"] = _mla_prep_bwd(
        proj_b, pos, d_qc, d_kc, d_v, w["qnw"], w["kvnw"], w["uq"], w["uk"], w["uv"])
    g["uq"] = _mm(c_q, d_q_full, ta=True, out_dtype=BF16, name="b_w_uq")
    g["uk"] = _mm(c_kv, d_kn, ta=True, out_dtype=BF16, name="b_w_uk")
    g["uv"] = _mm(c_kv, d_v, ta=True, out_dtype=BF16, name="b_w_uv")
    g["br_dn"] = _mm(og, d_y_dn, ta=True, out_dtype=BF16, name="b_w_br_dn")
    d_y_dn = emit("mix", g, d_y_dn)
    d_o_dn, d_z, g["dnw"] = _gdr_out_bwd(o_dn, proj_a, d_y_dn, w["dnw"], w["br_dn"])
    du, dw, dqd, dkt, d_a, d_egl = _gdr_scan_bwd(u, w_, qd, kt, a_mat, gc, states, d_o_dn)
    dq, dk, dv, d_beta, d_gc = _gdr_prep_bwd(qkvn, beta, gc, gc_t, t_fold, u, w_, du, dw, dqd, dkt, d_a)
    d_egl_rows = jnp.pad(d_egl[:, None, :, 0], ((0, 0), (CHUNK - 1, 0), (0, LANES - N_HEADS))).reshape(s_dim, LANES)
    d_ba, g["alog"], g["dtb"] = _gates_bwd(proj_b, w["alog"], w["dtb"], gc, d_beta, d_gc, d_egl_rows)
    d_qkv, g["conv"] = _conv_bwd(proj_a, w["conv"], dq, dk, dv)
    zeros = jnp.zeros((s_dim, WB_CKV - Q_LORA), BF16)
    d_proj_b = jnp.concatenate([d_cq, zeros, d_ckv, d_kr, d_ba], axis=1)
    g["wa_qkv_t"] = _mm(d_qkv, xb, ta=True, name="b_w_qkv")
    g["wa_z_t"] = _mm(d_z, xb, ta=True, name="b_w_z")
    g["wg_t"] = _mm(d_proj_g, xb, ta=True, name="b_w_g")
    g["wb_t"] = _mm(d_proj_b, xb, ta=True, name="b_w_b")
    d_qkv = emit("small", dict(g, loss=loss), emit("w_in", g, d_qkv))
    dx = _input_grad(d_qkv, d_z, d_proj_g, d_proj_b, w["w_in_t"], w["wg_t"], w["wb_t"], dxa)
    return loss, dx, g


DX_ROWS = 512


def _input_proj(x, w_in_t, wg_t, wb_t):
    s_dim = x.shape[0]
    n_a = 4 * D_MODEL

    def body(x_ref, wa_ref, wg_ref, wb_ref, a_ref, g_ref, b_ref, xb_ref):
        xv = x_ref[...].astype(BF16)
        xb_ref[...] = xv
        a_ref[...] = _dot(xv, wa_ref[...], NT)
        g_ref[...] = _dot(xv, wg_ref[...], NT).astype(BF16)
        b_ref[...] = _dot(xv, wb_ref[...], NT)

    rows = lambda width: pl.BlockSpec((DX_ROWS, width), lambda i: (i, 0))
    whole = lambda shape: pl.BlockSpec(shape, lambda i: (0, 0), pipeline_mode=pl.Buffered(1))
    return pl.pallas_call(
        body,
        out_shape=[jax.ShapeDtypeStruct((s_dim, n_a), F32), jax.ShapeDtypeStruct((s_dim, wg_t.shape[0]), BF16),
                   jax.ShapeDtypeStruct((s_dim, wb_t.shape[0]), F32), jax.ShapeDtypeStruct((s_dim, D_MODEL), BF16)],
        grid=(s_dim // DX_ROWS,),
        in_specs=[rows(D_MODEL), whole((n_a, D_MODEL)), whole(wg_t.shape), whole(wb_t.shape)],
        out_specs=[rows(n_a), rows(wg_t.shape[0]), rows(wb_t.shape[0]), rows(D_MODEL)],
        compiler_params=pltpu.CompilerParams(dimension_semantics=("parallel",)),
        name="f_proj",
    )(x, w_in_t, wg_t, wb_t)


def _input_grad(d_qkv, d_z, d_g, d_b, w_in_t, wg_t, wb_t, add):
    s_dim = d_qkv.shape[0]
    n_qkv, n_a = d_qkv.shape[1], d_qkv.shape[1] + d_z.shape[1]

    def body(q_ref, z_ref, g_ref, b_ref, wa_ref, wg_ref, wb_ref, add_ref, o_ref):
        r = add_ref[...] + _dot(q_ref[...], wa_ref[0:n_qkv])
        r = r + _dot(z_ref[...], wa_ref[n_qkv:n_a])
        r = r + _dot(g_ref[...], wg_ref[...])
        o_ref[...] = r + _dot(b_ref[...], wb_ref[...])

    rows = lambda a: pl.BlockSpec((DX_ROWS, a.shape[1]), lambda i: (i, 0))
    whole = lambda shape: pl.BlockSpec(shape, lambda i: (0, 0), pipeline_mode=pl.Buffered(1))
    return pl.pallas_call(
        body,
        out_shape=jax.ShapeDtypeStruct((s_dim, D_MODEL), F32),
        grid=(s_dim // DX_ROWS,),
        in_specs=[rows(d_qkv), rows(d_z), rows(d_g), rows(d_b), whole((n_a, D_MODEL)), whole(wg_t.shape),
                  whole(wb_t.shape), rows(add)],
        out_specs=pl.BlockSpec((DX_ROWS, D_MODEL), lambda i: (i, 0)),
        compiler_params=pltpu.CompilerParams(dimension_semantics=("parallel",)),
        name="b_x",
    )(d_qkv, d_z, d_g, d_b, w_in_t, wg_t, wb_t, add)


_BIG = (("w_in", 1), ("w_uq", 0), ("w_uk", 0), ("w_uv", 0), ("w_br_dn", 0), ("w_br_mla", 0),
        ("w_o", 0), ("w_ffn_in", 1), ("w_ffn_out", 0), ("w_ple", 1), ("w_ple_gate", 0))
_BIG_AXIS = dict(_BIG)
_SMALL = ("ln1_g", "ln1_b", "ln2_g", "ln2_b", "q_norm_w", "kv_norm_w", "dn_norm_w", "dn_a_log", "dn_dt_bias")
_ORDER = ("w_in", "conv_w", "dn_a_log", "dn_dt_bias", "dn_norm_w", "q_norm_w", "w_uq", "kv_norm_w", "w_uk", "w_uv",
          "w_br_dn", "w_br_mla", "w_o", "ln1_g", "ln1_b", "w_ffn_in", "w_ffn_out", "w_ple", "w_ple_gate", "ln2_g",
          "ln2_b")


def _stored_shape(name, shard_shape):
    axis = _BIG_AXIS[name]
    lead = shard_shape[axis]
    return lead, int(np.prod(shard_shape)) // lead


def _to_stored(name, shard):
    return jnp.moveaxis(shard, _BIG_AXIS[name], 0).reshape(_stored_shape(name, shard.shape))


def _from_stored(name, stored, shard_shape):
    axis = _BIG_AXIS[name]
    moved = (shard_shape[axis],) + shard_shape[:axis] + shard_shape[axis + 1:]
    return jnp.moveaxis(stored.reshape(moved), 0, axis)


_W_IN_ROWS = np.cumsum([0, 3072, 1024, 8, 8, Q_LORA, KV_LORA, ROPE, D_MODEL, D_MODEL])


def _first_weights(w_in_t, conv_full, small):
    r = _W_IN_ROWS
    zr = lambda n: jnp.zeros((n, D_MODEL), w_in_t.dtype)
    w = {}
    w["w_in_t"] = w_in_t
    w["wg_t"] = w_in_t[r[7]:r[9]]
    w["wb_t"] = jnp.concatenate([w_in_t[r[4]:r[5]], zr(WB_CKV - Q_LORA), w_in_t[r[5]:r[7]], zr(LANES - ROPE),
                                 w_in_t[r[2]:r[4]], zr(LANES - 2 * N_HEADS)], axis=0)
    w["conv"] = conv_full
    pad_l = lambda v: jnp.pad(v, ((0, 0), (0, LANES - v.shape[1])))
    w["alog"], w["dtb"] = pad_l(small["dn_a_log"]), pad_l(small["dn_dt_bias"])
    w["dnw"], w["qnw"], w["kvnw"] = small["dn_norm_w"], small["q_norm_w"], small["kv_norm_w"]
    w["ln1g"], w["ln1b"], w["ln2g"], w["ln2b"] = small["ln1_g"], small["ln1_b"], small["ln2_g"], small["ln2_b"]
    return w


def _late_weights(group, fw):
    w = {}
    if group == "mix":
        uq = fw["w_uq"].reshape(Q_LORA, N_HEADS, HEAD + ROPE)
        uq_r = jnp.pad(uq[:, :, HEAD:], ((0, 0), (0, 0), (0, HEAD - ROPE)))
        w["uq"] = jnp.concatenate([uq[:, :, :HEAD].reshape(Q_LORA, -1), uq_r.reshape(Q_LORA, -1)], axis=1)
        w["uk"], w["uv"] = fw["w_uk"], fw["w_uv"]
        w["br_dn"], w["br_mla"], w["wo"] = fw["w_br_dn"], fw["w_br_mla"], fw["w_o"]
    else:
        w["ffn_in_t"], w["ffn_out"] = fw["w_ffn_in"], fw["w_ffn_out"]
        w["ple_t"], w["ple_gate"] = fw["w_ple"], fw["w_ple_gate"]
    return w


_GROUP_GRADS = {"ffn": (("w_ple", "ple_t"), ("w_ple_gate", "ple_gate"), ("w_ffn_out", "ffn_out"),
                        ("w_ffn_in", "ffn_in_t")),
                "mix": (("w_o", "wo"), ("w_br_mla", "br_mla"), ("w_uq", "uq"), ("w_uk", "uk"), ("w_uv", "uv"),
                        ("w_br_dn", "br_dn"))}


def _group_grads(group, g):
    out = {}
    for name, key in _GROUP_GRADS[group]:
        t = g[key]
        if name == "w_uq":
            uq_n = t[:, :D_MODEL].reshape(Q_LORA, N_HEADS, HEAD)
            uq_r = t[:, D_MODEL:].reshape(Q_LORA, N_HEADS, HEAD)[:, :, :ROPE]
            t = jnp.concatenate([uq_n, uq_r], axis=2).reshape(Q_LORA, -1)
        out[name] = t
    return out


PACK_ROWS = 512
SUBLANES = 8


def _pack_exchange(parts, name):
    arrays = []
    for a, _, _ in parts:
        if not any(a is b for b in arrays):
            arrays.append(a)
    index = lambda a: next(i for i, b in enumerate(arrays) if a is b)
    chunks, dst = [], 0
    for a, first, rows in parts:
        assert first % SUBLANES == 0 and rows % SUBLANES == 0
        chunks += [(index(a), first + o, dst + o, min(PACK_ROWS, rows - o)) for o in range(0, rows, PACK_ROWS)]
        dst += rows
    c, n, last = arrays[0].shape[1], len(arrays), len(chunks) - 1
    slab = dst // N_DEV
    assert slab * N_DEV == dst

    def body(*refs):
        src_refs, out_ref, recv_ref = refs[:n], refs[n], refs[n + 1]
        buf, sem_in, sem_out, send_sems, recv_sems = refs[n + 2:]
        x, y, core = lax.axis_index("x"), lax.axis_index("y"), lax.axis_index("c")

        def to_sibling(q):
            return pltpu.make_async_remote_copy(
                src_ref=out_ref.at[pl.ds((2 * q + 1 - core) * slab, slab)], dst_ref=recv_ref.at[q],
                send_sem=send_sems.at[q], recv_sem=recv_sems.at[q], device_id=(x, y, 1 - core),
                device_id_type=_MESH_ID)

        sent = [0]

        def send_packed(rows_done):
            while sent[0] < N_DEV // 2 and (2 * sent[0] + 2) * slab <= rows_done:
                to_sibling(sent[0]).start()
                sent[0] += 1

        def load(k):
            i, first, _, rows = chunks[k]
            return pltpu.make_async_copy(src_refs[i].at[pl.ds(first, rows)], buf.at[k % 2, pl.ds(0, rows)],
                                         sem_in.at[k % 2])

        def store(k):
            _, _, first, rows = chunks[k]
            return pltpu.make_async_copy(buf.at[k % 2, pl.ds(0, rows)], out_ref.at[pl.ds(first, rows), 0, :],
                                         sem_out.at[k % 2])

        load(0).start()
        for k in range(last + 1):
            load(k).wait()
            store(k).start()
            if k >= 1:
                store(k - 1).wait()
                send_packed(chunks[k][2])
            if k < last:
                load(k + 1).start()
        store(last).wait()
        send_packed(dst)
        for q in range(N_DEV // 2):
            to_sibling(q).wait_recv()
        for q in range(N_DEV // 2):
            to_sibling(q).wait_send()

    return pl.pallas_call(
        body,
        out_shape=[jax.ShapeDtypeStruct((dst, 1, c), F32), jax.ShapeDtypeStruct((N_DEV // 2, slab, 1, c), F32)],
        in_specs=[_ANY] * n,
        out_specs=[_ANY, _ANY],
        scratch_shapes=[pltpu.VMEM((2, PACK_ROWS, c), F32), pltpu.SemaphoreType.DMA((2,)),
                        pltpu.SemaphoreType.DMA((2,)), pltpu.SemaphoreType.DMA((N_DEV // 2,)),
                        pltpu.SemaphoreType.DMA((N_DEV // 2,))],
        name=name,
    )(*arrays)


def _w_in_grad_parts(g):
    wb = g["wb_t"]
    return [(g["wa_qkv_t"], 0, 3 * D_MODEL), (g["wa_z_t"], 0, D_MODEL), (wb, WB_BA, 2 * N_HEADS),
            (wb, WB_CQ, Q_LORA), (wb, WB_CKV, KV_LORA), (wb, WB_KR, ROPE), (g["wg_t"], 0, 2 * D_MODEL)]


def _small_grads(g):
    return {"ln1_g": g["ln1g"], "ln1_b": g["ln1b"], "ln2_g": g["ln2g"], "ln2_b": g["ln2b"], "q_norm_w": g["qnw"],
            "kv_norm_w": g["kvnw"], "dn_norm_w": g["dnw"], "dn_a_log": g["alog"], "dn_dt_bias": g["dtb"],
            "conv_w": g["conv"]}


_SMALL_SLOTS = {"ln1_g": (0, 0, 1024), "ln1_b": (1, 0, 1024), "ln2_g": (2, 0, 1024), "ln2_b": (3, 0, 1024),
                "q_norm_w": (4, 0, 384), "kv_norm_w": (4, 384, 256), "dn_norm_w": (4, 640, 128),
                "dn_a_log": (4, 768, 8), "dn_dt_bias": (4, 896, 8)}
_SMALL_ROWS, _LOSS_ROW, _CONV_ROW0, _CONV_ROWS = 24, 5, 8, 12


def _pack_small_grads(small_g, loss):
    zeros = lambda r, c: jnp.zeros((r, c), F32)
    row4 = jnp.concatenate([small_g["q_norm_w"], small_g["kv_norm_w"], small_g["dn_norm_w"], small_g["dn_a_log"],
                            small_g["dn_dt_bias"]], axis=1)
    row5 = jnp.concatenate([loss, zeros(1, FLAT_COLS - LANES)], axis=1)
    head = jnp.concatenate([small_g["ln1_g"], small_g["ln1_b"], small_g["ln2_g"], small_g["ln2_b"], row4, row5,
                            zeros(2, FLAT_COLS)], axis=0)
    conv = small_g["conv_w"].reshape(_CONV_ROWS, FLAT_COLS)
    return jnp.concatenate([head, conv, zeros(_SMALL_ROWS - _CONV_ROW0 - _CONV_ROWS, FLAT_COLS)], axis=0)


_MESH_ID = pl.DeviceIdType.MESH
_ANY = pl.BlockSpec(memory_space=pl.ANY)


def _all_gather(blocks, name):
    n = len(blocks)

    def body(*refs):
        x_refs, out_refs = refs[:n], refs[n:2 * n]
        send_sems, recv_sems, local_sems = refs[2 * n:]
        x, y, c = lax.axis_index("x"), lax.axis_index("y"), lax.axis_index("c")
        me, sibling = (x, y, c), (x, y, 1 - c)
        chips = [(1 - x, y), (x, 1 - y), (1 - x, 1 - y)]

        def slot(i, px, py, pc):
            return out_refs[i].at[4 * px + 2 * py + pc]

        def copy(i, k, origin, to, src=None):
            return pltpu.make_async_remote_copy(
                src_ref=slot(i, *origin) if src is None else src, dst_ref=slot(i, *origin),
                send_sem=send_sems.at[7 * i + k], recv_sem=recv_sems.at[7 * i + k], device_id=to,
                device_id_type=_MESH_ID)

        mine = [pltpu.make_async_copy(x_refs[i], slot(i, *me), local_sems.at[i]) for i in range(n)]
        first, passed = [], []
        for i in range(n):
            mine[i].start()
            first.append(copy(i, 0, me, sibling, src=x_refs[i]))
            first += [copy(i, 1 + j, me, (*chip, c), src=x_refs[i]) for j, chip in enumerate(chips)]
        for cp in first:
            cp.start()
        for i in range(n):
            for j, chip in enumerate(chips):
                copy(i, 1 + j, (*chip, c), me).wait_recv()
                passed.append(copy(i, 4 + j, (*chip, c), sibling))
                passed[-1].start()
        for i in range(n):
            copy(i, 0, sibling, me).wait_recv()
            for j, chip in enumerate(chips):
                copy(i, 4 + j, (*chip, 1 - c), me).wait_recv()
        for cp in first + passed:
            cp.wait_send()
        for cp in mine:
            cp.wait()

    return pl.pallas_call(
        body,
        out_shape=[jax.ShapeDtypeStruct((N_DEV,) + b.shape, b.dtype) for b in blocks],
        in_specs=[_ANY] * n,
        out_specs=[_ANY] * n,
        scratch_shapes=[pltpu.SemaphoreType.DMA((7 * n,)), pltpu.SemaphoreType.DMA((7 * n,)),
                        pltpu.SemaphoreType.DMA((n,))],
        name=name,
    )(*blocks)


def _col_tile(c):
    return c if c <= 256 else 256


def _chip_sum(src, recv, parity, name):
    _, r, _, c = src.shape
    tc = _col_tile(c)

    def body(par_ref, a_ref, b_ref, o_ref, ob_ref):
        s = a_ref[...] + b_ref[...]
        o_ref[...] = s
        ob_ref[...] = s.astype(BF16)

    rows = lambda f: pl.BlockSpec((None, r, None, tc), f)
    blk = pl.BlockSpec((None, r, tc), lambda q, j, par: (q, 0, j))
    return pl.pallas_call(
        body,
        out_shape=[jax.ShapeDtypeStruct((4, r, c), F32), jax.ShapeDtypeStruct((4, r, c), BF16)],
        grid_spec=pltpu.PrefetchScalarGridSpec(
            num_scalar_prefetch=1, grid=(4, c // tc),
            in_specs=[rows(lambda q, j, par: (2 * q + par[0], 0, 0, j)), rows(lambda q, j, par: (q, 0, 0, j))],
            out_specs=[blk, blk]),
        compiler_params=pltpu.CompilerParams(dimension_semantics=("parallel", "parallel")),
        name=name,
    )(parity, src, recv)


_HBM = pl.BlockSpec(memory_space=pltpu.HBM)
_SEM = pl.BlockSpec(memory_space=pltpu.SEMAPHORE)
_DATAFLOW = pltpu.SideEffectType.DATAFLOW_SIDE_EFFECTING
N_PEERS = N_DEV - 1


def _ring_peer(j):
    me = 4 * lax.axis_index("x") + 2 * lax.axis_index("y") + lax.axis_index("c")
    k = (me + j) % N_DEV
    return me, k, (k // 4, (k // 2) % 2, k % 2)


def _spread_copy(i, j, src_refs, land_refs, send_sems, recv_sems, scatter):
    me, k, peer = _ring_peer(j)
    return pltpu.make_async_remote_copy(
        src_ref=src_refs[i].at[k] if scatter else src_refs[i], dst_ref=land_refs[i].at[me],
        send_sem=send_sems.at[N_PEERS * i + j - 1], recv_sem=recv_sems.at[N_PEERS * i + j - 1], device_id=peer,
        device_id_type=_MESH_ID)


def _spread_start(srcs, carry, scatter, name):
    n = len(srcs)
    lands = [lax.empty(((N_DEV,) + s.shape[-2:]), s.dtype) for s in srcs]

    def body(*refs):
        src_refs, land_refs = refs[:n], refs[n:2 * n]
        send_sems, recv_sems, local_sems = refs[2 * n + 1:2 * n + 4]
        for i in range(n):
            for j in range(1, N_DEV):
                _spread_copy(i, j, src_refs, land_refs, send_sems, recv_sems, scatter).start()
        for i in range(n):
            _own_copy(i, src_refs, land_refs, local_sems, scatter).start()

    hbm = lambda a: pltpu.HBM(a.shape, a.dtype)
    sems = pltpu.SemaphoreType.DMA((N_PEERS * n,))
    pinned = [pltpu.with_memory_space_constraint(a, pltpu.HBM) for a in list(srcs) + lands + [carry]]
    res = pl.pallas_call(
        body, name=name,
        out_shape=(sems, sems, pltpu.SemaphoreType.DMA((n,)), *[hbm(a) for a in pinned]),
        in_specs=[_HBM] * (2 * n + 1),
        out_specs=(_SEM, _SEM, _SEM, *[_HBM] * (2 * n + 1)),
        input_output_aliases={i: 3 + i for i in range(2 * n + 1)},
        compiler_params=pltpu.CompilerParams(has_side_effects=_DATAFLOW),
    )(*pinned)
    return res[:3], list(res[3:3 + n]), list(res[3 + n:3 + 2 * n]), res[3 + 2 * n]


def _own_copy(i, src_refs, land_refs, local_sems, scatter):
    me = _ring_peer(0)[0]
    return pltpu.make_async_copy(src_refs[i].at[me] if scatter else src_refs[i], land_refs[i].at[me],
                                 local_sems.at[i])


def _spread_wait(started, after, scatter, name):
    sems, srcs, lands, _ = started
    n = len(srcs)

    def body(*refs):
        src_refs, land_refs = refs[:n], refs[n:2 * n]
        send_s, recv_s, local_s = refs[2 * n:2 * n + 3]
        for i in range(n):
            for j in range(1, N_DEV):
                cp = _spread_copy(i, j, src_refs, land_refs, send_s, recv_s, scatter)
                cp.wait_send()
                cp.wait_recv()
        for i in range(n):
            _own_copy(i, src_refs, land_refs, local_s, scatter).wait()

    hbm = lambda a: pltpu.HBM(a.shape, a.dtype)
    res = pl.pallas_call(
        body, name=name,
        out_shape=tuple(hbm(a) for a in srcs + lands),
        in_specs=[_HBM] * (2 * n) + [_SEM, _SEM, _SEM, pl.BlockSpec(memory_space=pl.ANY)],
        out_specs=tuple([_HBM] * (2 * n)),
        input_output_aliases={i: i for i in range(2 * n)},
        compiler_params=pltpu.CompilerParams(has_side_effects=_DATAFLOW),
    )(*srcs, *lands, *sems, after)
    return list(res[n:])


def _chips_copy(i, j, src_refs, land_refs, send_sems, recv_sems):
    x, y, c = lax.axis_index("x"), lax.axis_index("y"), lax.axis_index("c")
    tx, ty = [(1 - x, y), (x, 1 - y), (1 - x, 1 - y)][j]
    return pltpu.make_async_remote_copy(
        src_ref=src_refs[i].at[2 * tx + ty], dst_ref=land_refs[i].at[j], send_sem=send_sems.at[3 * i + j],
        recv_sem=recv_sems.at[3 * i + j], device_id=(tx, ty, c), device_id_type=_MESH_ID)


def _chips_start(srcs, carry, name):
    n = len(srcs)
    lands = [lax.empty((3,) + s.shape[1:], s.dtype) for s in srcs]

    def body(*refs):
        src_refs, land_refs = refs[:n], refs[n:2 * n]
        send_sems, recv_sems = refs[2 * n + 1:2 * n + 3]
        for i in range(n):
            for j in range(3):
                _chips_copy(i, j, src_refs, land_refs, send_sems, recv_sems).start()

    hbm = lambda a: pltpu.HBM(a.shape, a.dtype)
    sems = pltpu.SemaphoreType.DMA((3 * n,))
    pinned = [pltpu.with_memory_space_constraint(a, pltpu.HBM) for a in list(srcs) + lands + [carry]]
    res = pl.pallas_call(
        body, name=name,
        out_shape=(sems, sems, *[hbm(a) for a in pinned]),
        in_specs=[_HBM] * (2 * n + 1),
        out_specs=(_SEM, _SEM, *[_HBM] * (2 * n + 1)),
        input_output_aliases={i: 2 + i for i in range(2 * n + 1)},
        compiler_params=pltpu.CompilerParams(has_side_effects=_DATAFLOW),
    )(*pinned)
    return res[:2], list(res[2:2 + n]), list(res[2 + n:2 + 2 * n]), res[2 + 2 * n]


def _chips_wait(started, after, name):
    sems, srcs, lands, _ = started
    n = len(srcs)

    def body(*refs):
        src_refs, land_refs = refs[:n], refs[n:2 * n]
        send_s, recv_s = refs[2 * n:2 * n + 2]
        for i in range(n):
            for j in range(3):
                cp = _chips_copy(i, j, src_refs, land_refs, send_s, recv_s)
                cp.wait_send()
                cp.wait_recv()

    hbm = lambda a: pltpu.HBM(a.shape, a.dtype)
    res = pl.pallas_call(
        body, name=name,
        out_shape=tuple(hbm(a) for a in srcs + lands),
        in_specs=[_HBM] * (2 * n) + [_SEM, _SEM, pl.BlockSpec(memory_space=pl.ANY)],
        out_specs=tuple([_HBM] * (2 * n)),
        input_output_aliases={i: i for i in range(2 * n)},
        compiler_params=pltpu.CompilerParams(has_side_effects=_DATAFLOW),
    )(*srcs, *lands, *sems, after)
    return list(res[n:])


def _sum8(landing, name):
    _, r, c = landing.shape
    tc = _col_tile(c)

    def body(a_ref, o_ref):
        tot = a_ref[0].astype(F32)
        for k in range(1, N_DEV):
            tot = tot + a_ref[k].astype(F32)
        o_ref[...] = tot

    return pl.pallas_call(
        body,
        out_shape=jax.ShapeDtypeStruct((r, c), F32),
        grid=(c // tc,),
        in_specs=[pl.BlockSpec((N_DEV, r, tc), lambda j: (0, 0, j))],
        out_specs=pl.BlockSpec((r, tc), lambda j: (0, j)),
        compiler_params=pltpu.CompilerParams(dimension_semantics=("parallel",)),
        name=name,
    )(landing)


def _adamw_math(w, g, m, v):
    m = ADAM_B1 * m + (1.0 - ADAM_B1) * g
    v = ADAM_B2 * v + (1.0 - ADAM_B2) * (g * g)
    m_hat = m / (1.0 - ADAM_B1 ** ADAM_STEP)
    v_hat = v / (1.0 - ADAM_B2 ** ADAM_STEP)
    delta = -ADAM_LR * (m_hat / (jnp.sqrt(v_hat) + ADAM_EPS) + ADAM_WD * w)
    return delta, m, v


def _adamw(w, m, v, g, name):
    r, c = w.shape

    def fn(rows, consts):
        return list(_adamw_math(*rows)), []

    return _rowwise(fn, [w, g, m, v], [], [(c, F32)] * 3, tm=r if r <= 512 else 256, name=name)


def _adamw_sum8(w, m, v, landing, name):
    r, c = w.shape
    tc = _col_tile(c)

    def body(w_ref, m_ref, v_ref, a_ref, g_ref, d_ref, m2_ref, v2_ref):
        g = a_ref[0].astype(F32)
        for k in range(1, N_DEV):
            g = g + a_ref[k].astype(F32)
        delta, m2, v2 = _adamw_math(w_ref[...], g, m_ref[...], v_ref[...])
        g_ref[...] = g
        d_ref[...] = delta
        m2_ref[...] = m2
        v2_ref[...] = v2

    blk = pl.BlockSpec((r, tc), lambda j: (0, j))
    return pl.pallas_call(
        body,
        out_shape=[jax.ShapeDtypeStruct((r, c), F32)] * 4,
        grid=(c // tc,),
        in_specs=[blk, blk, blk, pl.BlockSpec((N_DEV, r, tc), lambda j: (0, 0, j))],
        out_specs=[blk] * 4,
        compiler_params=pltpu.CompilerParams(dimension_semantics=("parallel",)),
        name=name,
    )(w, m, v, landing)


def _adamw_parts(w, m, v, own, others, chip, name):
    r, _, c = w.shape
    tc = _col_tile(c)

    def body(q_ref, w_ref, m_ref, v_ref, a_ref, b_ref, g_ref, d_ref, m2_ref, v2_ref):
        g = ((a_ref[...] + b_ref[0].astype(F32)) + b_ref[1].astype(F32)) + b_ref[2].astype(F32)
        delta, m2, v2 = _adamw_math(w_ref[...], g, m_ref[...], v_ref[...])
        g_ref[...] = g
        d_ref[...] = delta
        m2_ref[...] = m2
        v2_ref[...] = v2

    row = pl.BlockSpec((r, None, tc), lambda j, q: (0, 0, j))
    return pl.pallas_call(
        body,
        out_shape=[jax.ShapeDtypeStruct((r, 1, c), F32)] * 4,
        grid_spec=pltpu.PrefetchScalarGridSpec(
            num_scalar_prefetch=1, grid=(c // tc,),
            in_specs=[row, row, row, pl.BlockSpec((None, r, tc), lambda j, q: (q[0], 0, j)),
                      pl.BlockSpec((3, r, tc), lambda j, q: (0, 0, j))],
            out_specs=[row] * 4),
        compiler_params=pltpu.CompilerParams(dimension_semantics=("parallel",)),
        name=name,
    )(chip, w, m, v, own, others)


def _adamw_small(gathered, params):
    ns = len(_SMALL)

    def body(*refs):
        g_ref, p_refs, o_refs = refs[0], refs[1:1 + 3 * ns], refs[1 + 3 * ns:]
        tot = g_ref[0]
        for k in range(1, N_DEV):
            tot = tot + g_ref[k]
        for i, name in enumerate(_SMALL):
            row, lane0, lanes = _SMALL_SLOTS[name]
            g = tot[row:row + 1, lane0:lane0 + lanes]
            w_, m_, v_ = (p_refs[3 * i + j][...] for j in range(3))
            delta, m2, v2 = _adamw_math(w_, g, m_, v_)
            for j, val in enumerate((g, delta, m2, v2)):
                o_refs[4 * i + j][...] = val
        o_refs[4 * ns][...] = tot[_LOSS_ROW:_LOSS_ROW + 1, 0:LANES]
        o_refs[4 * ns + 1][...] = tot[_CONV_ROW0:_CONV_ROW0 + _CONV_ROWS, :]

    out_shape = [jax.ShapeDtypeStruct(w.shape, F32) for (w, _, _) in params for _ in range(4)]
    out_shape += [jax.ShapeDtypeStruct((1, LANES), F32), jax.ShapeDtypeStruct((_CONV_ROWS, FLAT_COLS), F32)]
    flat = [a for wmv in params for a in wmv]
    return pl.pallas_call(body, out_shape=out_shape, name="adamw_small")(gathered, *flat)


def kernel(x, p, positions, w_in, conv_w, dn_a_log, dn_dt_bias, dn_norm_w, q_norm_w, w_uq, kv_norm_w, w_uk, w_uv, w_br_dn, w_br_mla, w_o, ln1_g, ln1_b, w_ffn_in, w_ffn_out, w_ple, w_ple_gate, ln2_g, ln2_b, loss_target, m_w_in, m_conv_w, m_dn_a_log, m_dn_dt_bias, m_dn_norm_w, m_q_norm_w, m_w_uq, m_kv_norm_w, m_w_uk, m_w_uv, m_w_br_dn, m_w_br_mla, m_w_o, m_ln1_g, m_ln1_b, m_w_ffn_in, m_w_ffn_out, m_w_ple, m_w_ple_gate, m_ln2_g, m_ln2_b, v_w_in, v_conv_w, v_dn_a_log, v_dn_dt_bias, v_dn_norm_w, v_q_norm_w, v_w_uq, v_kv_norm_w, v_w_uk, v_w_uv, v_w_br_dn, v_w_br_mla, v_w_o, v_ln1_g, v_ln1_b, v_w_ffn_in, v_w_ffn_out, v_w_ple, v_w_ple_gate, v_ln2_g, v_ln2_b):
    args = dict(locals())
    wts = {n: args[n] for n in _ORDER}
    mom1 = {n: args["m_" + n] for n in _ORDER}
    mom2 = {n: args["v_" + n] for n in _ORDER}
    big_names = [n for n, _ in _BIG]
    shard_shapes = {n: wts[n].shape[1:] for n in big_names}
    c_idx = lax.axis_index("c")
    q_idx = 2 * lax.axis_index("x") + lax.axis_index("y")
    parity, chip = c_idx.reshape(1).astype(jnp.int32), q_idx.reshape(1).astype(jnp.int32)

    stored = {n: _to_stored(n, wts[n][0]).astype(BF16) for n in big_names}
    first = _all_gather([stored["w_in"], conv_w[0]], "ag_first")
    group_names = {grp: [n for n, _ in pairs] for grp, pairs in _GROUP_GRADS.items()}
    carry, gathers = first[0], {}
    for grp in ("mix", "ffn"):
        gathers[grp] = _spread_start([stored[n] for n in group_names[grp]], carry, False, "ag_start_" + grp)
        carry = gathers[grp][3]
    conv_full = jnp.moveaxis(first[1], 0, 1).reshape(conv_w.shape[1], -1)
    small_w = {n: wts[n].astype(F32) for n in _SMALL}
    w = _first_weights(carry.reshape(-1, D_MODEL), conv_full, small_w)

    def late_weights(grp, after):
        got = _spread_wait(gathers[grp], after, False, "ag_wait_" + grp)
        return _late_weights(grp, {n: t.reshape(-1, t.shape[-1]) for n, t in zip(group_names[grp], got)})

    started = {}

    def emit(group, g, carry):
        if group == "w_in":
            rows, cols = _stored_shape("w_in", shard_shapes["w_in"])
            packed, from_sibling = _pack_exchange(_w_in_grad_parts(g), "rs_pack_sibling")
            own, own_bf = _chip_sum(packed.reshape(N_DEV, rows, 1, cols), from_sibling, parity, "rs_sum_w_in")
            started["w_in"] = (own, _chips_start([own_bf], carry, "rs_chips_start"))
            return started["w_in"][1][3]
        if group == "small":
            block = _pack_small_grads(_small_grads(g), g["loss"])
            started["small"] = _spread_start([block], carry, False, "ag_start_small")
            return started["small"][3]
        grads = _group_grads(group, g)
        srcs = [grads[n].reshape((N_DEV,) + _stored_shape(n, shard_shapes[n])) for n in grads]
        started[group] = (list(grads), _spread_start(srcs, carry, True, "rs_start_" + group))
        return started[group][1][3]

    s_dim = x.shape[1]
    loss, dx, g = _local_step(x[0], p[0, 0], positions.reshape(s_dim, 1).astype(F32), loss_target[0], w,
                              late_weights, emit)
    own, chips_started = started.pop("w_in")
    small_started = started.pop("small")

    out_g, out_d, out_m, out_v = {}, {}, {}, {}

    def update(n, grad, shp):
        flat2 = (shp[0], int(np.prod(shp[1:])))
        d, m2, v2 = _adamw(wts[n][0].reshape(flat2), mom1[n][0].reshape(flat2), mom2[n][0].reshape(flat2),
                           grad.reshape(flat2), "adamw_" + n)
        out_g[n], out_d[n], out_m[n], out_v[n] = grad, d.reshape(shp), m2.reshape(shp), v2.reshape(shp)

    for group, (names, st) in started.items():
        for n, landing in zip(names, _spread_wait(st, dx, True, "rs_wait_" + group)):
            shp = shard_shapes[n]
            if _BIG_AXIS[n] == 0 or shp[-1] % LANES:
                res = _adamw_sum8(_to_stored(n, wts[n][0]), _to_stored(n, mom1[n][0]), _to_stored(n, mom2[n][0]),
                                  landing, "adamw_" + n)
                out_g[n], out_d[n], out_m[n], out_v[n] = (_from_stored(n, t, shp) for t in res)
                last = res[3]
            else:
                update(n, _from_stored(n, _sum8(landing, "rs_total_" + n), shp), shp)

    from_chips = _chips_wait(chips_started, last, "rs_chips_wait")[0]
    g_small = _spread_wait(small_started, last, False, "ag_wait_small")[0]
    rows_first = lambda a: jnp.transpose(a, (2, 0, 1))
    res = _adamw_parts(rows_first(wts["w_in"]), rows_first(mom1["w_in"]), rows_first(mom2["w_in"]), own, from_chips,
                       chip, "adamw_w_in")
    out_g["w_in"], out_d["w_in"], out_m["w_in"], out_v["w_in"] = (jnp.transpose(t, (1, 2, 0))[0] for t in res)

    res = _adamw_small(g_small, [(wts[n], mom1[n], mom2[n]) for n in _SMALL])
    for i, n in enumerate(_SMALL):
        out_g[n], out_d[n], out_m[n], out_v[n] = res[4 * i:4 * i + 4]
    loss_out = res[4 * len(_SMALL)][0, 0]
    conv_shape = conv_w.shape[1:]
    conv_g = lax.dynamic_slice(res[-1].reshape(conv_shape[0], -1), (0, (2 * q_idx + c_idx) * conv_shape[1]),
                               conv_shape)
    update("conv_w", conv_g, conv_shape)

    expand = lambda d, n: d[n] if n in _SMALL else d[n][None]
    return (loss_out, dx[None], *[expand(out_g, n) for n in _ORDER], *[expand(out_d, n) for n in _ORDER],
            *[expand(out_m, n) for n in _ORDER], *[expand(out_v, n) for n in _ORDER])
```

```python
import functools

import numpy as np
import jax
import jax.numpy as jnp
from jax import lax
from jax.experimental import pallas as pl
from jax.experimental.pallas import tpu as pltpu

F32 = jnp.float32
BF16 = jnp.bfloat16

D_MODEL = 1024
N_HEADS = 8
HEAD = 128
CHUNK = 64
GROUP = 256
ROPE = 64
Q_LORA = 384
KV_LORA = 256
FFN_HIDDEN = 2816
PLE_DIM = 256
ROPE_BASE = 10000.0
ALPHA = 2.0 ** 0.25
SCALE = float((HEAD + ROPE) ** -0.5)
NEG_BIG = -1e30
EPS_RMS = 1e-6
EPS_LN = 1e-5

ADAM_LR = 0.001
ADAM_B1 = 0.9
ADAM_B2 = 0.999
ADAM_EPS = 1e-08
ADAM_WD = 0.01
ADAM_STEP = 10

N_DEV = 8
LANES = 128
FLAT_COLS = 1024

WB_CQ, WB_CKV, WB_KR, WB_BA, WB_COLS = 0, 512, 768, 896, 1024

HIGHEST = lax.Precision.HIGHEST

NN = (((1,), (0,)), ((), ()))
TN = (((0,), (0,)), ((), ()))
NT = (((1,), (1,)), ((), ()))


def _dot(a, b, dims=NN):
    return lax.dot_general(a.astype(BF16), b.astype(BF16), dims, preferred_element_type=F32)


def _dot32(a, b, dims=NN):
    return lax.dot_general(a, b, dims, precision=HIGHEST, preferred_element_type=F32)


def _sig(x):
    return 1.0 / (1.0 + jnp.exp(-x))


MM_TILE = 1536


def _pick_wide(n):
    if n <= MM_TILE:
        return n
    return max(t for t in range(LANES, MM_TILE + 1, LANES) if n % t == 0)


def _split_bf16(a):
    hi = a.astype(BF16)
    return hi, (a - hi.astype(F32)).astype(BF16)


def _dot3(a, b, dims=NN):
    ah, al = a if isinstance(a, tuple) else _split_bf16(a)
    bh, bl = b if isinstance(b, tuple) else _split_bf16(b)
    d = lambda p, q: lax.dot_general(p, q, dims, preferred_element_type=F32)
    return d(ah, bh) + (d(ah, bl) + d(al, bh))


def _mm(a, b, *, ta=False, tb=False, add=(), out_dtype=F32, name):
    if ta:
        k_dim, m_dim = a.shape
    else:
        m_dim, k_dim = a.shape
    if tb:
        n_dim, k2 = b.shape
    else:
        k2, n_dim = b.shape
    assert k_dim == k2, (a.shape, b.shape, ta, tb)
    tm = _pick_wide(m_dim)
    tn = _pick_wide(n_dim)
    tk = _pick_wide(k_dim)
    nk = k_dim // tk
    n_add = len(add)
    dims = TN if ta else (NT if tb else NN)
    assert not (ta and tb)

    def body(a_ref, b_ref, *rest):
        add_refs = rest[:n_add]
        o_ref = rest[n_add]
        acc = rest[n_add + 1]
        k = pl.program_id(2)

        @pl.when(k == 0)
        def _():
            acc[...] = jnp.zeros_like(acc)

        acc[...] += _dot(a_ref[...], b_ref[...], dims)

        @pl.when(k == nk - 1)
        def _():
            r = acc[...]
            for ar in add_refs:
                r = r + ar[...].astype(F32)
            o_ref[...] = r.astype(o_ref.dtype)

    a_spec = pl.BlockSpec((tk, tm), lambda i, j, k: (k, i)) if ta else pl.BlockSpec((tm, tk), lambda i, j, k: (i, k))
    b_spec = pl.BlockSpec((tn, tk), lambda i, j, k: (j, k)) if tb else pl.BlockSpec((tk, tn), lambda i, j, k: (k, j))
    o_spec = pl.BlockSpec((tm, tn), lambda i, j, k: (i, j))
    return pl.pallas_call(
        body,
        out_shape=jax.ShapeDtypeStruct((m_dim, n_dim), out_dtype),
        grid=(m_dim // tm, n_dim // tn, nk),
        in_specs=[a_spec, b_spec] + [o_spec] * n_add,
        out_specs=o_spec,
        scratch_shapes=[pltpu.VMEM((tm, tn), F32)],
        compiler_params=pltpu.CompilerParams(dimension_semantics=("parallel", "parallel", "arbitrary")),
        name=name,
    )(a, b, *add)


def _mm_resident(a, b, *, tb=False, add=(), out_dtype=F32, name):
    m_dim, k_dim = a.shape
    n_dim, k2 = b.shape if tb else b.shape[::-1]
    assert k2 == k_dim
    tm = min(DX_ROWS, m_dim)
    dims = NT if tb else NN

    def body(a_ref, b_ref, *rest):
        r = _dot(a_ref[...], b_ref[...], dims)
        for ar in rest[:-1]:
            r = r + ar[...]
        rest[-1][...] = r.astype(out_dtype)

    o_spec = pl.BlockSpec((tm, n_dim), lambda i: (i, 0))
    return pl.pallas_call(
        body,
        out_shape=jax.ShapeDtypeStruct((m_dim, n_dim), out_dtype),
        grid=(m_dim // tm,),
        in_specs=[pl.BlockSpec((tm, k_dim), lambda i: (i, 0)),
                  pl.BlockSpec(b.shape, lambda i: (0, 0), pipeline_mode=pl.Buffered(1))] + [o_spec] * len(add),
        out_specs=o_spec,
        compiler_params=pltpu.CompilerParams(dimension_semantics=("parallel",)),
        name=name,
    )(a, b, *add)


def _rowwise(fn, rows, consts, outs, accs=(), *, tm=256, name):
    rows = [r if isinstance(r, tuple) else (r, 0, r.shape[1]) for r in rows]
    s_dim = rows[0][0].shape[0]
    tm = min(tm, s_dim)
    assert s_dim % tm == 0 and all(arr.shape[0] == s_dim for arr, _, _ in rows)
    specs = [pl.BlockSpec((tm, width), functools.partial(lambda i, cb: (i, cb), cb=cb)) for _, cb, width in rows]
    args = [arr for arr, _, _ in rows]
    for c in consts:
        specs.append(pl.BlockSpec(c.shape, lambda i: (0, 0)))
        args.append(c)
    nr, nc, no = len(rows), len(consts), len(outs)
    flipped = [len(o) == 3 for o in outs]
    out_shape = [jax.ShapeDtypeStruct((o[0], s_dim) if t else (s_dim, o[0]), o[1]) for o, t in zip(outs, flipped)]
    out_specs = [pl.BlockSpec((o[0], tm), lambda i: (0, i)) if t else pl.BlockSpec((tm, o[0]), lambda i: (i, 0))
                 for o, t in zip(outs, flipped)]
    out_shape += [jax.ShapeDtypeStruct(sh, F32) for sh in accs]
    out_specs += [pl.BlockSpec(sh, lambda i: (0, 0)) for sh in accs]

    def body(*refs):
        r = [x[...].astype(F32) if x.dtype == BF16 else x[...] for x in refs[:nr]]
        c = [x[...] for x in refs[nr:nr + nc]]
        o_refs = refs[nr + nc:nr + nc + no]
        a_refs = refs[nr + nc + no:]
        o_vals, a_vals = fn(r, c)
        for ref, v, t in zip(o_refs, o_vals, flipped, strict=True):
            ref[...] = (jnp.transpose(v.astype(F32)) if t else v).astype(ref.dtype)
        if a_refs:
            @pl.when(pl.program_id(0) == 0)
            def _():
                for ref in a_refs:
                    ref[...] = jnp.zeros_like(ref)

            for ref, v in zip(a_refs, a_vals, strict=True):
                ref[...] += v

    res = pl.pallas_call(
        body,
        out_shape=out_shape,
        grid=(s_dim // tm,),
        in_specs=specs,
        out_specs=out_specs,
        compiler_params=pltpu.CompilerParams(dimension_semantics=("arbitrary" if accs else "parallel",)),
        name=name,
    )(*args)
    return res


def _colsum(v):
    return jnp.sum(v, axis=0, keepdims=True)


def _rowsum(v):
    return jnp.sum(v, axis=1, keepdims=True)


def _rowmean(v):
    return jnp.mean(v, axis=1, keepdims=True)


def _silu_grad(x):
    s = _sig(x)
    return s * (1.0 + x * (1.0 - s))


def _conv_taps(x, w, width=4):
    row = lax.broadcasted_iota(jnp.int32, x.shape, 0)
    c = x * w[width - 1:width, :]
    for s in range(1, width):
        c = c + jnp.where(row >= s, pltpu.roll(x, s, 0), 0.0) * w[width - 1 - s:width - s, :]
    return c


def _conv_fwd(proj_a, conv_w):
    s_dim = proj_a.shape[0]
    n_blk = 3 * N_HEADS

    def body(x_ref, w_ref, o_ref):
        j = pl.program_id(0)
        c = _conv_taps(x_ref[...], w_ref[...])
        y = c * _sig(c)
        r = lax.rsqrt(_rowsum(y * y) + EPS_RMS)
        fac = jnp.where(j < N_HEADS, r * (HEAD ** -0.5), jnp.where(j < 2 * N_HEADS, r, 1.0))
        o_ref[...] = y * fac

    return pl.pallas_call(
        body,
        out_shape=jax.ShapeDtypeStruct((s_dim, n_blk * HEAD), F32),
        grid=(n_blk,),
        in_specs=[pl.BlockSpec((s_dim, HEAD), lambda j: (0, j)), pl.BlockSpec((4, HEAD), lambda j: (0, j))],
        out_specs=pl.BlockSpec((s_dim, HEAD), lambda j: (0, j)),
        compiler_params=pltpu.CompilerParams(dimension_semantics=("parallel",)),
        name="conv_fwd",
    )(proj_a, conv_w)


def _conv_bwd(proj_a, conv_w, dq, dk, dv):
    s_dim = proj_a.shape[0]
    n_blk = 3 * N_HEADS

    def body(x_ref, w_ref, dq_ref, dk_ref, dv_ref, dx_ref, dw_ref):
        j = pl.program_id(0)
        x = x_ref[...]
        w = w_ref[...]
        do = jnp.where(j < N_HEADS, dq_ref[...], jnp.where(j < 2 * N_HEADS, dk_ref[...], dv_ref[...]))
        c = _conv_taps(x, w)
        sg = _sig(c)
        y = c * sg
        r = lax.rsqrt(_rowsum(y * y) + EPS_RMS)
        sc = jnp.where(j < N_HEADS, HEAD ** -0.5, 1.0)
        dy_n = sc * (r * do - y * (r * r * r) * _rowsum(do * y))
        dy = jnp.where(j < 2 * N_HEADS, dy_n, do)
        dc = dy * (sg * (1.0 + c * (1.0 - sg)))
        row = lax.broadcasted_iota(jnp.int32, x.shape, 0)
        dx = dc * w[3:4, :]
        dw_ref[3:4, :] = _colsum(dc * x)
        for s in range(1, 4):
            dx = dx + jnp.where(row < s_dim - s, pltpu.roll(dc, s_dim - s, 0), 0.0) * w[3 - s:4 - s, :]
            xs = jnp.where(row >= s, pltpu.roll(x, s, 0), 0.0)
            dw_ref[3 - s:4 - s, :] = _colsum(dc * xs)
        dx_ref[...] = dx.astype(dx_ref.dtype)

    hd = N_HEADS - 1
    return pl.pallas_call(
        body,
        out_shape=[jax.ShapeDtypeStruct((s_dim, n_blk * HEAD), BF16), jax.ShapeDtypeStruct((4, n_blk * HEAD), F32)],
        grid=(n_blk,),
        in_specs=[
            pl.BlockSpec((s_dim, HEAD), lambda j: (0, j)),
            pl.BlockSpec((4, HEAD), lambda j: (0, j)),
            pl.BlockSpec((s_dim, HEAD), lambda j: (0, jnp.minimum(j, hd))),
            pl.BlockSpec((s_dim, HEAD), lambda j: (0, jnp.clip(j - N_HEADS, 0, hd))),
            pl.BlockSpec((s_dim, HEAD), lambda j: (0, jnp.clip(j - 2 * N_HEADS, 0, hd))),
        ],
        out_specs=[pl.BlockSpec((s_dim, HEAD), lambda j: (0, j)), pl.BlockSpec((4, HEAD), lambda j: (0, j))],
        compiler_params=pltpu.CompilerParams(dimension_semantics=("parallel",)),
        name="conv_bwd",
    )(proj_a, conv_w, dq, dk, dv)


def _chunk_tri(n):
    r = np.arange(n)
    m = ((r[:, None] // CHUNK) == (r[None, :] // CHUNK)) & (r[:, None] >= r[None, :])
    m = m.astype(np.float32)
    return jnp.asarray(m), jnp.asarray(m.T)


def _softplus(z):
    return jnp.maximum(z, 0.0) + jnp.log(1.0 + jnp.exp(-jnp.abs(z)))


def _gates_fwd(proj_b, alog, dtb):
    tm = min(GROUP, proj_b.shape[0])
    tri, _ = _chunk_tri(tm)

    def fn(r, c):
        b = r[0]
        a = pltpu.roll(b, LANES - N_HEADS, 1)
        alog_, dtb_, tri_ = c
        g = -jnp.exp(alog_) * _softplus(a + dtb_)
        return [_sig(b), _dot32(tri_, g)], []

    return _rowwise(fn, [(proj_b, WB_BA // LANES, LANES)], [alog, dtb, tri],
                    [(LANES, F32), (LANES, F32)], tm=tm, name="gates_fwd")


def _gates_bwd(proj_b, alog, dtb, gc, d_beta, d_gc, d_egl_rows):
    tm = min(GROUP, proj_b.shape[0])
    _, tri_t = _chunk_tri(tm)

    def fn(r, c):
        b, gc_, d_beta_, d_gc_, d_egl_ = r
        a = pltpu.roll(b, LANES - N_HEADS, 1)
        alog_, dtb_, tri_t_ = c
        z = a + dtb_
        ea = jnp.exp(alog_)
        g = -ea * _softplus(z)
        dg = _dot32(tri_t_, d_gc_ + d_egl_ * jnp.exp(gc_))
        d_a = dg * (-ea) * _sig(z)
        beta = _sig(b)
        d_ba = d_beta_ * beta * (1.0 - beta) + pltpu.roll(d_a, N_HEADS, 1)
        return [d_ba], [_colsum(dg * g), _colsum(d_a)]

    return _rowwise(fn, [(proj_b, WB_BA // LANES, LANES), gc, d_beta, d_gc, d_egl_rows],
                    [alog, dtb, tri_t], [(LANES, BF16)], accs=[(1, LANES), (1, LANES)], tm=tm,
                    name="gates_bwd")


def _group_masks(n):
    r = lax.broadcasted_iota(jnp.int32, (n, n), 0)
    c = lax.broadcasted_iota(jnp.int32, (n, n), 1)
    same = (r // CHUNK) == (c // CHUNK)
    below, s = [], 2
    while s < CHUNK:
        below.append(jnp.logical_and((r // (2 * s)) == (c // (2 * s)),
                                     jnp.logical_and((r // s) % 2 == 1, (c // s) % 2 == 0)))
        s *= 2
    return dict(same=same, tril=jnp.logical_and(same, r >= c), strict=jnp.logical_and(same, r > c),
                last=c == (r // CHUNK) * CHUNK + (CHUNK - 1), eye=r == c, pair=(r // 2) == (c // 2), below=below)


def _inv_unit_lower(l_mats, mk):
    eye_f = mk["eye"].astype(F32)
    ts = [eye_f - jnp.where(mk["pair"], l_mat, 0.0) for l_mat in l_mats]
    for below in mk["below"]:
        halves = [_split_bf16(t) for t in ts]
        mids = [_dot3(h, jnp.where(below, l_mat, 0.0)) for h, l_mat in zip(halves, l_mats)]
        ts = [t - _dot3(m, h) for t, m, h in zip(ts, mids, halves)]
    return ts


def _unfold_blocks(folded, mask):
    n = folded.shape[0]
    return jnp.where(mask, jnp.concatenate([folded] * (n // CHUNK), axis=1), 0.0)


def _head_cols(beta, gc, gc_t, h):
    lane = lax.broadcasted_iota(jnp.int32, beta.shape, 1)
    sub = lax.broadcasted_iota(jnp.int32, gc_t.shape, 0)
    bcol = _rowsum(jnp.where(lane == h, beta, 0.0))
    gcol = _rowsum(jnp.where(lane == h, gc, 0.0))
    grow = _colsum(jnp.where(sub == h, gc_t, 0.0))
    return bcol, gcol, grow


def _prep_common(q, k, bcol, gcol, grow, mk, t_folded=None):
    n = q.shape[0]
    tril = mk["tril"]
    decay = jnp.where(tril, jnp.exp(jnp.where(tril, gcol - grow, 0.0)), 0.0)
    glast = _rowsum(jnp.where(mk["last"], jnp.broadcast_to(grow, (n, n)), 0.0))
    e = jnp.exp(gcol)
    ekt = jnp.exp(glast - gcol)
    kb = k * bcol
    kk = _dot(kb, k, NT)
    qk = _dot(q, k, NT)
    p = dict(decay=decay, e=e, ekt=ekt, kb=kb, kk=kk, qk=qk)
    if t_folded is not None:
        p["t"] = _unfold_blocks(t_folded, mk["same"])
    return p


GROUPS_PER_STEP = 4
SCAN_CHUNKS_PER_STEP = 4


def _fold_blocks(m):
    n = m.shape[0]
    out = m[:, 0:CHUNK]
    for b in range(1, n // CHUNK):
        out = out + m[:, b * CHUNK:(b + 1) * CHUNK]
    return out


def _gdr_prep_fwd(qkvn, beta, gc, gc_t):
    s_dim = qkvn.shape[0]
    tg = min(GROUP, s_dim)
    n_sub = min(GROUPS_PER_STEP, s_dim // tg)
    tb = tg * n_sub

    def body(q_ref, k_ref, v_ref, b_ref, g_ref, gt_ref, u_ref, w_ref, qd_ref, kt_ref, a_ref, t_ref):
        h = pl.program_id(0)
        mk = _group_masks(tg)
        parts = []
        for s in range(n_sub):
            rows = slice(s * tg, (s + 1) * tg)
            q, k, v = q_ref[rows, :], k_ref[rows, :], v_ref[rows, :]
            bcol, gcol, grow = _head_cols(b_ref[rows, :], g_ref[rows, :], gt_ref[:, rows], h)
            p = _prep_common(q, k, bcol, gcol, grow, mk)
            qd_ref[rows, :] = q * p["e"]
            kt_ref[rows, :] = k * p["ekt"]
            a_ref[rows, :] = _fold_blocks(jnp.where(mk["tril"], p["qk"] * p["decay"], 0.0))
            parts.append((rows, v * bcol, p["kb"] * p["e"], jnp.where(mk["strict"], p["kk"] * p["decay"], 0.0)))
        t_mats = _inv_unit_lower([part[3] for part in parts], mk)
        for (rows, vb, kbe, _), t_mat in zip(parts, t_mats):
            u_ref[rows, :] = _dot(t_mat, vb)
            w_ref[rows, :] = _dot(t_mat, kbe)
            t_ref[rows, :] = _fold_blocks(t_mat)

    row = lambda off: pl.BlockSpec((tb, HEAD), functools.partial(lambda h, m, off: (m, h + off), off=off))
    full = pl.BlockSpec((tb, LANES), lambda h, m: (m, 0))
    o_spec = pl.BlockSpec((tb, HEAD), lambda h, m: (m, h))
    a_spec = pl.BlockSpec((None, tb, CHUNK), lambda h, m: (h, m, 0))
    wide = jax.ShapeDtypeStruct((s_dim, N_HEADS * HEAD), F32)
    folded = jax.ShapeDtypeStruct((N_HEADS, s_dim, CHUNK), F32)
    return pl.pallas_call(
        body,
        out_shape=[wide, wide, wide, wide, folded, folded],
        grid=(N_HEADS, s_dim // tb),
        in_specs=[row(0), row(N_HEADS), row(2 * N_HEADS), full, full, pl.BlockSpec((8, tb), lambda h, m: (0, m))],
        out_specs=[o_spec, o_spec, o_spec, o_spec, a_spec, a_spec],
        compiler_params=pltpu.CompilerParams(dimension_semantics=("parallel", "parallel")),
        name="gdr_prep_fwd",
    )(qkvn, qkvn, qkvn, beta, gc, gc_t)


def _gdr_prep_bwd(qkvn, beta, gc, gc_t, t_fold, u, w, du, dw, dqd, dkt, d_a):
    s_dim = qkvn.shape[0]
    tg = min(GROUP, s_dim)
    n_sub = min(GROUPS_PER_STEP, s_dim // tg)
    tb = tg * n_sub

    def body(q_ref, k_ref, v_ref, b_ref, g_ref, gt_ref, t_ref, u_ref, w_ref, du_ref, dw_ref, dqd_ref, dkt_ref,
             da_ref, dq_ref, dk_ref, dv_ref, db_ref, dg_ref):
        h = pl.program_id(1)

        @pl.when(h == 0)
        def _():
            db_ref[...] = jnp.zeros_like(db_ref)
            dg_ref[...] = jnp.zeros_like(dg_ref)

        mk = _group_masks(tg)
        lane = lax.broadcasted_iota(jnp.int32, (tg, LANES), 1)
        for s in range(n_sub):
            rows = slice(s * tg, (s + 1) * tg)
            q, k, v = q_ref[rows, :], k_ref[rows, :], v_ref[rows, :]
            bcol, gcol, grow = _head_cols(b_ref[rows, :], g_ref[rows, :], gt_ref[:, rows], h)
            p = _prep_common(q, k, bcol, gcol, grow, mk, t_ref[rows, :])
            t_mat, decay, e, ekt, kb = p["t"], p["decay"], p["e"], p["ekt"], p["kb"]
            du_, dw_, dqd_, dkt_ = du_ref[rows, :], dw_ref[rows, :], dqd_ref[rows, :], dkt_ref[rows, :]
            dvb = _dot(t_mat, du_, TN)
            dkbe = _dot(t_mat, dw_, TN)
            d_l = -(_dot(dvb, u_ref[rows, :], NT) + _dot(dkbe, w_ref[rows, :], NT))
            m1 = jnp.where(mk["strict"], d_l, 0.0)
            m2 = _unfold_blocks(da_ref[rows, :], mk["tril"])
            d_kk = m1 * decay
            d_qk = m2 * decay
            d_decay = m1 * p["kk"] + m2 * p["qk"]
            dkb = _dot(d_kk, k) + dkbe * e
            dk = _dot(d_kk, kb, TN) + _dot(d_qk, q, TN) + dkt_ * ekt + dkb * bcol
            dq = _dot(d_qk, k) + dqd_ * e
            d_beta = _rowsum(dkb * k) + _rowsum(dvb * v)
            d_e = _rowsum(dkbe * kb) + _rowsum(dqd_ * q)
            d_ekt = _rowsum(dkt_ * k) * ekt
            d_diff = d_decay * decay
            d_grow = -_colsum(d_diff) + _colsum(jnp.where(mk["last"], jnp.broadcast_to(d_ekt, (tg, tg)), 0.0))
            d_gcol = d_e * e - d_ekt + _rowsum(d_diff)
            d_gcol = d_gcol + _rowsum(jnp.where(mk["eye"], jnp.broadcast_to(d_grow, (tg, tg)), 0.0))
            dq_ref[rows, :] = dq
            dk_ref[rows, :] = dk
            dv_ref[rows, :] = dvb * bcol
            db_ref[rows, :] = jnp.where(lane == h, d_beta, db_ref[rows, :])
            dg_ref[rows, :] = jnp.where(lane == h, d_gcol, dg_ref[rows, :])

    row = lambda off: pl.BlockSpec((tb, HEAD), functools.partial(lambda m, h, off: (m, h + off), off=off))
    full = pl.BlockSpec((tb, LANES), lambda m, h: (m, 0))
    o_spec = pl.BlockSpec((tb, HEAD), lambda m, h: (m, h))
    a_spec = pl.BlockSpec((None, tb, CHUNK), lambda m, h: (h, m, 0))
    wide = jax.ShapeDtypeStruct((s_dim, N_HEADS * HEAD), F32)
    lanes = jax.ShapeDtypeStruct((s_dim, LANES), F32)
    return pl.pallas_call(
        body,
        out_shape=[wide, wide, wide, lanes, lanes],
        grid=(s_dim // tb, N_HEADS),
        in_specs=[row(0), row(N_HEADS), row(2 * N_HEADS), full, full, pl.BlockSpec((8, tb), lambda m, h: (0, m)),
                  a_spec, o_spec, o_spec, o_spec, o_spec, o_spec, o_spec, a_spec],
        out_specs=[o_spec, o_spec, o_spec, full, full],
        compiler_params=pltpu.CompilerParams(dimension_semantics=("parallel", "arbitrary")),
        name="gdr_prep_bwd",
    )(qkvn, qkvn, qkvn, beta, gc, gc_t, t_fold, u, w, du, dw, dqd, dkt, d_a)


def _gdr_scan_fwd(u, w, qd, kt, a_mat, gc):
    s_dim = u.shape[0]
    n_chunks = s_dim // CHUNK
    per = min(SCAN_CHUNKS_PER_STEP, n_chunks)
    tb = per * CHUNK

    def body(u_ref, w_ref, qd_ref, kt_ref, a_ref, g_ref, o_ref, st_ref, state):
        @pl.when(pl.program_id(0) == 0)
        def _():
            state[...] = jnp.zeros_like(state)

        heads = range(N_HEADS)
        cols = [slice(h * HEAD, (h + 1) * HEAD) for h in heads]
        for i in range(per):
            rows = slice(i * CHUNK, (i + 1) * CHUNK)
            egl = jnp.exp(g_ref[(i + 1) * CHUNK - 1:(i + 1) * CHUNK, :])
            s_b = [state[h].astype(BF16) for h in heads]
            for h in heads:
                st_ref[i, h] = state[h]
            ws = [_dot(w_ref[rows, cs], s) for cs, s in zip(cols, s_b)]
            qs = [_dot(qd_ref[rows, cs], s) for cs, s in zip(cols, s_b)]
            vns = [(u_ref[rows, cs] - ws_h).astype(BF16) for cs, ws_h in zip(cols, ws)]
            avs = [_dot(a_ref[h, rows, :], vn) for h, vn in zip(heads, vns)]
            kvs = [_dot(kt_ref[rows, cs], vn, TN) for cs, vn in zip(cols, vns)]
            for h, cs in zip(heads, cols):
                o_ref[rows, cs] = qs[h] + avs[h]
                state[h] = state[h] * egl[:, h:h + 1] + kvs[h]

    wide = pl.BlockSpec((tb, N_HEADS * HEAD), lambda n: (n, 0))
    return pl.pallas_call(
        body,
        out_shape=[jax.ShapeDtypeStruct((s_dim, N_HEADS * HEAD), F32),
                   jax.ShapeDtypeStruct((n_chunks, N_HEADS, HEAD, HEAD), F32)],
        grid=(n_chunks // per,),
        in_specs=[wide, wide, wide, wide, pl.BlockSpec((N_HEADS, tb, CHUNK), lambda n: (0, n, 0)),
                  pl.BlockSpec((tb, LANES), lambda n: (n, 0))],
        out_specs=[wide, pl.BlockSpec((per, N_HEADS, HEAD, HEAD), lambda n: (n, 0, 0, 0))],
        scratch_shapes=[pltpu.VMEM((N_HEADS, HEAD, HEAD), F32)],
        compiler_params=pltpu.CompilerParams(dimension_semantics=("arbitrary",)),
        name="gdr_scan_fwd",
    )(u, w, qd, kt, a_mat, gc)


def _gdr_scan_bwd(u, w, qd, kt, a_mat, gc, states, d_o):
    s_dim = u.shape[0]
    n_chunks = s_dim // CHUNK
    per = min(SCAN_CHUNKS_PER_STEP, n_chunks)
    tb = per * CHUNK
    last = n_chunks // per - 1

    def body(u_ref, w_ref, qd_ref, kt_ref, a_ref, g_ref, st_ref, do_ref,
             du_ref, dw_ref, dqd_ref, dkt_ref, da_ref, de_ref, d_state):
        @pl.when(pl.program_id(0) == 0)
        def _():
            d_state[...] = jnp.zeros_like(d_state)

        heads = range(N_HEADS)
        cols = [slice(h * HEAD, (h + 1) * HEAD) for h in heads]
        for i in reversed(range(per)):
            rows = slice(i * CHUNK, (i + 1) * CHUNK)
            egl = jnp.exp(g_ref[(i + 1) * CHUNK - 1:(i + 1) * CHUNK, :])
            s_b = [st_ref[i, h].astype(BF16) for h in heads]
            ds_b = [d_state[h].astype(BF16) for h in heads]
            dos = [do_ref[rows, cs].astype(BF16) for cs in cols]
            w_b = [w_ref[rows, cs].astype(BF16) for cs in cols]
            ws = [_dot(w_h, s) for w_h, s in zip(w_b, s_b)]
            ados = [_dot(a_ref[h, rows, :], do, TN) for h, do in zip(heads, dos)]
            kds = [_dot(kt_ref[rows, cs], ds) for cs, ds in zip(cols, ds_b)]
            dqds = [_dot(do, s, NT) for do, s in zip(dos, s_b)]
            qdos = [_dot(qd_ref[rows, cs], do, TN) for cs, do in zip(cols, dos)]
            vns = [(u_ref[rows, cs] - ws_h).astype(BF16) for cs, ws_h in zip(cols, ws)]
            dvns = [a + k_ for a, k_ in zip(ados, kds)]
            dvn_b = [d.astype(BF16) for d in dvns]
            das = [_dot(do, vn, NT) for do, vn in zip(dos, vns)]
            dkts = [_dot(vn, ds, NT) for vn, ds in zip(vns, ds_b)]
            dws = [_dot(d, s, NT) for d, s in zip(dvn_b, s_b)]
            wds = [_dot(w_h, d, TN) for w_h, d in zip(w_b, dvn_b)]
            for h, cs in zip(heads, cols):
                ds_n = d_state[h]
                de = jnp.sum(_rowsum(ds_n * st_ref[i, h]), axis=0, keepdims=True)
                de_ref[i, h:h + 1, :] = jnp.broadcast_to(de, (1, LANES))
                dqd_ref[rows, cs] = dqds[h]
                da_ref[h, rows, :] = das[h]
                dkt_ref[rows, cs] = dkts[h]
                du_ref[rows, cs] = dvns[h]
                dw_ref[rows, cs] = -dws[h]
                d_state[h] = ds_n * egl[:, h:h + 1] + qdos[h] - wds[h]

    wide = pl.BlockSpec((tb, N_HEADS * HEAD), lambda n: (last - n, 0))
    a_spec = pl.BlockSpec((N_HEADS, tb, CHUNK), lambda n: (0, last - n, 0))
    wide_shape = jax.ShapeDtypeStruct((s_dim, N_HEADS * HEAD), F32)
    return pl.pallas_call(
        body,
        out_shape=[wide_shape, wide_shape, wide_shape, wide_shape,
                   jax.ShapeDtypeStruct((N_HEADS, s_dim, CHUNK), F32),
                   jax.ShapeDtypeStruct((n_chunks, N_HEADS, LANES), F32)],
        grid=(n_chunks // per,),
        in_specs=[wide, wide, wide, wide, a_spec, pl.BlockSpec((tb, LANES), lambda n: (last - n, 0)),
                  pl.BlockSpec((per, N_HEADS, HEAD, HEAD), lambda n: (last - n, 0, 0, 0)), wide],
        out_specs=[wide, wide, wide, wide, a_spec, pl.BlockSpec((per, N_HEADS, LANES), lambda n: (last - n, 0, 0))],
        scratch_shapes=[pltpu.VMEM((N_HEADS, HEAD, HEAD), F32)],
        compiler_params=pltpu.CompilerParams(dimension_semantics=("arbitrary",)),
        name="gdr_scan_bwd",
    )(u, w, qd, kt, a_mat, gc, states, d_o)


FUSED_ROWS = 512


def _gdr_out_fwd(o_dn, proj_a, dn_w, w_br):
    def fn(r, c):
        o, z = r
        w_, w_br_ = c
        outs = []
        for h in range(N_HEADS):
            cs = slice(h * HEAD, (h + 1) * HEAD)
            oh, zh = o[:, cs], z[:, cs]
            rr = lax.rsqrt(_rowmean(oh * oh) + EPS_RMS)
            outs.append(oh * rr * w_ * (zh * _sig(zh)))
        og = jnp.concatenate(outs, axis=1).astype(BF16)
        return [og, _dot(og, w_br_)], []

    return _rowwise(fn, [o_dn, (proj_a, 3, D_MODEL)], [dn_w, w_br], [(D_MODEL, BF16), (D_MODEL, BF16)],
                    tm=FUSED_ROWS, name="gdr_out_fwd")


def _gdr_out_bwd(o_dn, proj_a, d_y_dn, dn_w, w_br):
    def fn(r, c):
        o, z, dy = r
        w_, w_br_ = c
        dg = _dot(dy, w_br_, NT)
        d_o, d_z = [], []
        d_w = jnp.zeros((1, HEAD), F32)
        for h in range(N_HEADS):
            cs = slice(h * HEAD, (h + 1) * HEAD)
            oh, zh, dgh = o[:, cs], z[:, cs], dg[:, cs]
            rr = lax.rsqrt(_rowmean(oh * oh) + EPS_RMS)
            sz = zh * _sig(zh)
            d_n = dgh * sz
            d_z.append(dgh * (oh * rr * w_) * _silu_grad(zh))
            d_w = d_w + _colsum(d_n * oh * rr)
            gw = d_n * w_
            d_o.append(rr * gw - oh * (rr * rr * rr) * _rowmean(gw * oh))
        return [jnp.concatenate(d_o, axis=1), jnp.concatenate(d_z, axis=1)], [d_w]

    return _rowwise(fn, [o_dn, (proj_a, 3, D_MODEL), d_y_dn], [dn_w, w_br], [(D_MODEL, F32), (D_MODEL, BF16)],
                    accs=[(1, HEAD)], tm=FUSED_ROWS, name="gdr_out_bwd")


def _rms_fwd(x, w):
    r = lax.rsqrt(_rowmean(x * x) + EPS_RMS)
    return x * r * w


def _rms_bwd(x, w, dy):
    r = lax.rsqrt(_rowmean(x * x) + EPS_RMS)
    gw = dy * w
    return r * gw - x * (r * r * r) * _rowmean(gw * x), _colsum(dy * x * r)


def _rope_consts():
    inv = ROPE_BASE ** (-np.arange(0, ROPE, 2, dtype=np.float32) / ROPE)
    t = np.zeros((4, LANES), np.float32)
    t[0, :32] = inv
    t[0, 32:64] = inv
    t[1, :64] = 1.0
    t[2, 32:64] = 1.0
    t[3, :32] = -1.0
    return jnp.asarray(t)


def _rope_tables(pos, consts, width):
    ang = pos * consts[0:1, :]
    cosv, sinv = jnp.cos(ang), jnp.sin(ang)
    reps = width // LANES
    tile = (lambda t: jnp.concatenate([t] * reps, axis=1)) if reps > 1 else (lambda t: t)
    return tile(cosv * consts[1:2, :]), tile(sinv * consts[2:3, :]), tile(sinv * consts[3:4, :])


def _rope_apply(t, tabs):
    cos_t, sin_a, sin_b = tabs
    width = t.shape[1]
    return t * cos_t + pltpu.roll(t, 32, 1) * sin_a + pltpu.roll(t, width - 32, 1) * sin_b


def _rope_transpose(d, tabs):
    cos_t, sin_a, sin_b = tabs
    width = d.shape[1]
    return d * cos_t + pltpu.roll(d * sin_a, width - 32, 1) + pltpu.roll(d * sin_b, 32, 1)


QK_HEAD = 2 * HEAD


def _interleave_heads(a, b):
    parts = []
    for h in range(N_HEADS):
        parts.append(a[:, h * HEAD:(h + 1) * HEAD])
        parts.append(b if b.shape[1] == LANES else b[:, h * LANES:(h + 1) * LANES])
    return jnp.concatenate(parts, axis=1)


def _mla_rows(proj_b):
    return [(proj_b, WB_CQ // Q_LORA, Q_LORA), (proj_b, WB_CKV // KV_LORA, KV_LORA), (proj_b, WB_KR // LANES, LANES)]


def _mla_prep_fwd(proj_b, pos, qn_w, kvn_w, uq, uk, uv):
    def fn(r, c):
        cq, ckv, kr, pos_ = r
        qn_w_, kvn_w_, uq_, uk_, uv_, rope = c
        c_q = _rms_fwd(cq, qn_w_).astype(BF16)
        c_kv = _rms_fwd(ckv, kvn_w_).astype(BF16)
        qf = _dot(c_q, uq_)
        qr = _rope_apply(qf[:, D_MODEL:], _rope_tables(pos_, rope, D_MODEL))
        kr = _rope_apply(kr, _rope_tables(pos_, rope, LANES))
        kc = _interleave_heads(_dot(c_kv, uk_), kr)
        v = _dot(c_kv, uv_)
        return [c_q, c_kv, _interleave_heads(qf[:, :D_MODEL], qr) * SCALE, kc, v, kc, v], []

    wide2 = N_HEADS * QK_HEAD
    return _rowwise(fn, _mla_rows(proj_b) + [pos], [qn_w, kvn_w, uq, uk, uv, _rope_consts()],
                    [(Q_LORA, BF16), (KV_LORA, BF16), (wide2, BF16), (wide2, BF16), (D_MODEL, BF16),
                     (wide2, BF16, "T"), (D_MODEL, BF16, "T")], tm=FUSED_ROWS, name="mla_prep_fwd")


def _mla_prep_bwd(proj_b, pos, d_qc, d_kc, d_v, qn_w, kvn_w, uq, uk, uv):
    def fn(r, c):
        cq, ckv, _, pos_, dq, dk, dv = r
        qn_w_, kvn_w_, uq_, uk_, uv_, rope = c
        even = lambda t: jnp.concatenate([t[:, (2 * h) * LANES:(2 * h + 1) * LANES] for h in range(N_HEADS)], axis=1)
        odd = lambda t: jnp.concatenate([t[:, (2 * h + 1) * LANES:(2 * h + 2) * LANES] for h in range(N_HEADS)], axis=1)
        d_qr_raw = _rope_transpose(odd(dq), _rope_tables(pos_, rope, D_MODEL)) * SCALE
        d_qf = jnp.concatenate([even(dq) * SCALE, d_qr_raw], axis=1).astype(BF16)
        d_kn = even(dk).astype(BF16)
        dkr = dk[:, LANES:2 * LANES]
        for h in range(1, N_HEADS):
            dkr = dkr + dk[:, (2 * h + 1) * LANES:(2 * h + 2) * LANES]
        d_cq, d_qnw = _rms_bwd(cq, qn_w_, _dot(d_qf, uq_, NT))
        d_ckv, d_kvnw = _rms_bwd(ckv, kvn_w_, _dot(d_kn, uk_, NT) + _dot(dv, uv_, NT))
        return [d_qf, d_kn, d_cq, d_ckv, _rope_transpose(dkr, _rope_tables(pos_, rope, LANES))], [d_qnw, d_kvnw]

    return _rowwise(fn, _mla_rows(proj_b) + [pos, d_qc, d_kc, d_v], [qn_w, kvn_w, uq, uk, uv, _rope_consts()],
                    [(2 * D_MODEL, BF16), (D_MODEL, BF16), (Q_LORA, BF16), (KV_LORA, BF16), (LANES, BF16)],
                    accs=[(1, Q_LORA), (1, KV_LORA)], tm=FUSED_ROWS, name="mla_prep_bwd")


def _causal_mask_t(st, key0, query0):
    key = lax.broadcasted_iota(jnp.int32, st.shape, 0) + key0
    query = lax.broadcasted_iota(jnp.int32, st.shape, 1) + query0
    return jnp.where(key <= query, st, NEG_BIG)


def _attn_tiles(s_dim):
    tq = min(512, s_dim)
    n_chains = 2 if s_dim >= 2 * tq else 1
    return tq, n_chains, min(512, s_dim)


def _diagonal_chains(t, tq, n_chains, tk):
    return [(c, (t + 1) * tk - 1 > c * tq) for c in range(n_chains) if t * tk < (c + 1) * tq]


def _attn_fwd(qc, kc, vt):
    s_dim = qc.shape[0]
    tq, n_chains, tk = _attn_tiles(s_dim)
    tqs = tq * n_chains

    def body(q_ref, k_ref, vt_ref, o_ref, lse_ref, m_s, l_s, acc):
        qi = pl.program_id(1)
        m_s[...] = jnp.full_like(m_s, NEG_BIG)
        l_s[...] = jnp.zeros_like(l_s)
        acc[...] = jnp.zeros_like(acc)

        def make_step(chains):
            def step(j, carry):
                ks = pl.multiple_of(j * tk, tk)
                kb, vtb = k_ref[pl.ds(ks, tk), :], vt_ref[:, pl.ds(ks, tk)]
                cols = [slice(c * tq, (c + 1) * tq) for c, _ in chains]
                sts = [_dot(kb, q_ref[cs, :], NT) for cs in cols]
                sts = [_causal_mask_t(st, j * tk, qi * tqs + c * tq) if masked else st
                       for st, (c, masked) in zip(sts, chains)]
                m_prevs = [m_s[:, cs] for cs in cols]
                m_news = [jnp.maximum(mp, jnp.max(st, axis=0, keepdims=True)) for mp, st in zip(m_prevs, sts)]
                alphas = [jnp.exp(mp - mn) for mp, mn in zip(m_prevs, m_news)]
                pts = [jnp.exp(st - mn) for st, mn in zip(sts, m_news)]
                pvs = [_dot(vtb, pt) for pt in pts]
                for cs, mn, al, pt, pv in zip(cols, m_news, alphas, pts, pvs):
                    l_s[:, cs] = al * l_s[:, cs] + _colsum(pt)
                    m_s[:, cs] = mn
                    acc[:, cs] = acc[:, cs] * al + pv
                return carry
            return step

        below = qi * (tqs // tk)
        lax.fori_loop(0, below, make_step([(c, False) for c in range(n_chains)]), 0)
        for t in range(tqs // tk):
            make_step(_diagonal_chains(t, tq, n_chains, tk))(below + t, 0)
        l = l_s[...]
        o_ref[...] = jnp.transpose(acc[...] / l)
        lse_ref[...] = m_s[...] + jnp.log(l)

    return pl.pallas_call(
        body,
        out_shape=[jax.ShapeDtypeStruct((s_dim, N_HEADS * HEAD), F32), jax.ShapeDtypeStruct((N_HEADS, 1, s_dim), F32)],
        grid=(N_HEADS, s_dim // tqs),
        in_specs=[pl.BlockSpec((tqs, QK_HEAD), lambda h, qi: (qi, h)),
                  pl.BlockSpec((s_dim, QK_HEAD), lambda h, qi: (0, h)),
                  pl.BlockSpec((HEAD, s_dim), lambda h, qi: (h, 0))],
        out_specs=[pl.BlockSpec((tqs, HEAD), lambda h, qi: (qi, h)),
                   pl.BlockSpec((None, 1, tqs), lambda h, qi: (h, 0, qi))],
        scratch_shapes=[pltpu.VMEM((1, tqs), F32), pltpu.VMEM((1, tqs), F32), pltpu.VMEM((HEAD, tqs), F32)],
        compiler_params=pltpu.CompilerParams(dimension_semantics=("parallel", "parallel")),
        name="attn_fwd",
    )(qc, kc, vt)


def _attn_bwd(qc, kc, kct, v, o, d_o, lse):
    s_dim = qc.shape[0]
    tq, n_chains, tk = _attn_tiles(s_dim)
    tqs = tq * n_chains

    def body(q_ref, k_ref, kt_ref, v_ref, o_ref, do_ref, lse_ref, dq_ref, dk_ref, dv_ref, dqt_acc, dv_acc):
        qi = pl.program_id(1)

        @pl.when(qi == 0)
        def _():
            dk_ref[...] = jnp.zeros_like(dk_ref)
            dv_acc[...] = jnp.zeros_like(dv_acc)

        dqt_acc[...] = jnp.zeros_like(dqt_acc)
        do_f = do_ref[...]
        do_all = do_f.astype(BF16)
        q_all = q_ref[...]
        lse_row = lse_ref[...]
        delta_row = _dot3(jnp.ones((8, HEAD), F32), o_ref[...] * do_f, NT)[0:1, :]

        def make_step(chains):
            rows = slice(chains[0][0] * tq, (chains[-1][0] + 1) * tq)

            def step(j, carry):
                ks = pl.multiple_of(j * tk, tk)
                kb, vb, ktb = k_ref[pl.ds(ks, tk), :], v_ref[pl.ds(ks, tk), :], kt_ref[:, pl.ds(ks, tk)]
                cols = [slice(c * tq, (c + 1) * tq) for c, _ in chains]
                sts = [_dot(kb, q_all[cs, :], NT) for cs in cols]
                sts = [_causal_mask_t(st, j * tk, qi * tqs + c * tq) if masked else st
                       for st, (c, masked) in zip(sts, chains)]
                dpts = [_dot(vb, do_all[cs, :], NT) for cs in cols]
                pts = [jnp.exp(st - lse_row[:, cs]) for st, cs in zip(sts, cols)]
                dsts = [(pt * (dpt - delta_row[:, cs])).astype(BF16) for pt, dpt, cs in zip(pts, dpts, cols)]
                pts = [pt.astype(BF16) for pt in pts]
                dqs = [_dot(ktb, dst) for dst in dsts]
                for cs, dq in zip(cols, dqs):
                    dqt_acc[:, cs] += dq
                pt_all = jnp.concatenate(pts, axis=1) if len(chains) > 1 else pts[0]
                dst_all = jnp.concatenate(dsts, axis=1) if len(chains) > 1 else dsts[0]
                dk_ref[pl.ds(ks, tk), :] += _dot(dst_all, q_all[rows, :])
                dv_acc[pl.ds(ks, tk), :] += _dot(pt_all, do_all[rows, :])
                return carry
            return step

        below = qi * (tqs // tk)
        lax.fori_loop(0, below, make_step([(c, False) for c in range(n_chains)]), 0)
        for t in range(tqs // tk):
            make_step(_diagonal_chains(t, tq, n_chains, tk))(below + t, 0)
        dq_ref[...] = jnp.transpose(dqt_acc[...])

        @pl.when(qi == s_dim // tqs - 1)
        def _():
            dv_ref[...] = dv_acc[...].astype(dv_ref.dtype)

    q_spec = pl.BlockSpec((tqs, QK_HEAD), lambda h, qi: (qi, h))
    o_spec = pl.BlockSpec((tqs, HEAD), lambda h, qi: (qi, h))
    k_spec = pl.BlockSpec((s_dim, QK_HEAD), lambda h, qi: (0, h))
    v_spec = pl.BlockSpec((s_dim, HEAD), lambda h, qi: (0, h))
    wide2 = jax.ShapeDtypeStruct((s_dim, N_HEADS * QK_HEAD), F32)
    return pl.pallas_call(
        body,
        out_shape=[wide2, wide2, jax.ShapeDtypeStruct((s_dim, N_HEADS * HEAD), BF16)],
        grid=(N_HEADS, s_dim // tqs),
        in_specs=[q_spec, k_spec, pl.BlockSpec((QK_HEAD, s_dim), lambda h, qi: (h, 0)), v_spec, o_spec, o_spec,
                  pl.BlockSpec((None, 1, tqs), lambda h, qi: (h, 0, qi))],
        out_specs=[q_spec, k_spec, v_spec],
        scratch_shapes=[pltpu.VMEM((QK_HEAD, tqs), F32), pltpu.VMEM((s_dim, HEAD), F32)],
        compiler_params=pltpu.CompilerParams(dimension_semantics=("parallel", "arbitrary")),
        name="attn_bwd",
    )(qc, kc, kct, v, o, d_o, lse)


def _mix_proj_ln1(y_dn, y_mla, proj_g, x, w_o, g, b):
    s_dim = x.shape[0]
    tm = min(512, s_dim)

    def body(yd_ref, ym_ref, g_ref, x_ref, w_ref, lg_ref, lb_ref, mixed_ref, a1_ref, h1_ref, h1b_ref):
        gates = g_ref[...].astype(F32)
        mixed = (_sig(gates[:, :D_MODEL]) * yd_ref[...].astype(F32)
                 + _sig(gates[:, D_MODEL:]) * ym_ref[...].astype(F32)).astype(BF16)
        a1 = _dot(mixed, w_ref[...])
        xh, _ = _ln_stats(ALPHA * x_ref[...] + a1)
        y = xh * lg_ref[...] + lb_ref[...]
        mixed_ref[...] = mixed
        a1_ref[...] = a1
        h1_ref[...] = y
        h1b_ref[...] = y.astype(BF16)

    row = lambda width: pl.BlockSpec((tm, width), lambda i: (i, 0))
    whole = lambda a: pl.BlockSpec(a.shape, lambda i: (0, 0))
    sds = lambda dt: jax.ShapeDtypeStruct((s_dim, D_MODEL), dt)
    return pl.pallas_call(
        body,
        out_shape=[sds(BF16), sds(F32), sds(F32), sds(BF16)],
        grid=(s_dim // tm,),
        in_specs=[row(D_MODEL), row(D_MODEL), row(2 * D_MODEL), row(D_MODEL), whole(w_o), whole(g), whole(b)],
        out_specs=[row(D_MODEL)] * 4,
        compiler_params=pltpu.CompilerParams(dimension_semantics=("parallel",)),
        name="mix_proj_ln1",
    )(y_dn, y_mla, proj_g, x, w_o, g, b)


def _ln1_mix_bwd(x, a1, d_h1, d_pg, y_dn, y_mla, proj_g, g, w_o, w_pg):
    def fn(r, c):
        x_, a1_, dy, dpg, yd, ym, gates = r
        g_, w_o_, w_pg_ = c
        dy = dy + _dot(dpg, w_pg_, NT)
        xh, rr = _ln_stats(ALPHA * x_ + a1_)
        dz = _ln_bwd(dy, xh, rr, g_)
        dz_b = dz.astype(BF16)
        dm = _dot(dz_b, w_o_, NT)
        sd, sm = _sig(gates[:, :D_MODEL]), _sig(gates[:, D_MODEL:])
        d_g = jnp.concatenate([dm * yd * sd * (1.0 - sd), dm * ym * sm * (1.0 - sm)], axis=1)
        return [dz_b, ALPHA * dz, d_g, dm * sd, dm * sm], [_colsum(dy * xh), _colsum(dy)]

    return _rowwise(fn, [x, a1, d_h1, d_pg, y_dn, y_mla, proj_g], [g, w_o, w_pg],
                    [(D_MODEL, BF16), (D_MODEL, F32), (2 * D_MODEL, BF16), (D_MODEL, BF16), (D_MODEL, BF16)],
                    accs=[(1, D_MODEL), (1, D_MODEL)], tm=FUSED_ROWS, name="ln1_mix_bwd")


def _ln_stats(z):
    mu = _rowmean(z)
    zc = z - mu
    r = lax.rsqrt(_rowmean(zc * zc) + EPS_LN)
    return zc * r, r


def _ln_bwd(dy, xh, r, g):
    dxh = dy * g
    return r * (dxh - _rowmean(dxh) - xh * _rowmean(dxh * xh))


def _ffn_in_act(h1b, w_t):
    s_dim, k_dim = h1b.shape
    hidden = w_t.shape[0] // 2
    tm, tn = min(512, s_dim), _pick_wide(hidden)
    nt = hidden // tn

    def body(a_ref, bg_ref, bu_ref, gt_ref, up_ref, act_ref):
        a = a_ref[...]
        gt, up = _dot(a, bg_ref[...], NT), _dot(a, bu_ref[...], NT)
        gt_ref[...] = gt.astype(BF16)
        up_ref[...] = up.astype(BF16)
        act_ref[...] = (gt * _sig(gt) * up).astype(BF16)

    o_spec = pl.BlockSpec((tm, tn), lambda j, i: (i, j))
    sds = jax.ShapeDtypeStruct((s_dim, hidden), BF16)
    return pl.pallas_call(
        body,
        out_shape=[sds, sds, sds],
        grid=(nt, s_dim // tm),
        in_specs=[pl.BlockSpec((tm, k_dim), lambda j, i: (i, 0)), pl.BlockSpec((tn, k_dim), lambda j, i: (j, 0)),
                  pl.BlockSpec((tn, k_dim), lambda j, i: (j + nt, 0))],
        out_specs=[o_spec, o_spec, o_spec],
        compiler_params=pltpu.CompilerParams(dimension_semantics=("parallel", "parallel")),
        name="ffn_in_act",
    )(h1b, w_t, w_t)


def _act_bwd(gt, up, d_act):
    def fn(r, c):
        gt_, up_, da = r
        return [jnp.concatenate([da * up_ * _silu_grad(gt_), da * gt_ * _sig(gt_)], axis=1)], []

    return _rowwise(fn, [gt, up, d_act], [], [(2 * FFN_HIDDEN, BF16)], name="act_bwd")[0]


def _tail(h1, ffn, p, tgt, g, b, w_pg, w_ple_t):
    def fn(r, c):
        h1_, ffn_, p_, t_ = r
        pg_ = _dot(h1_, c[2])
        pp_ = _dot(p_, c[3], NT)
        sp = _sig(pg_)
        xh, rr = _ln_stats(ALPHA * h1_ + ffn_ + sp * pp_)
        y = xh * c[0] + c[1]
        err = y - t_
        dy = err * (1.0 / D_MODEL)
        dz = _ln_bwd(dy, xh, rr, c[0])
        loss = jnp.sum(0.5 * _rowmean(err * err), axis=0, keepdims=True)
        return ([dz, dz * pp_ * sp * (1.0 - sp), dz * sp, ALPHA * dz],
                [_colsum(dy * xh), _colsum(dy), jnp.broadcast_to(loss, (1, LANES))])

    return _rowwise(fn, [h1, ffn, p, tgt], [g, b, w_pg, w_ple_t], [(D_MODEL, BF16)] * 3 + [(D_MODEL, F32)],
                    accs=[(1, D_MODEL), (1, D_MODEL), (1, LANES)], tm=FUSED_ROWS, name="tail")


def _local_step(x, p, pos, tgt, w, late_weights, emit):
    w = dict(w)
    s_dim = x.shape[0]
    pb = p.astype(BF16)
    proj_a, proj_g, proj_b, xb = _input_proj(x, w["w_in_t"], w["wg_t"], w["wb_t"])
    qkvn = _conv_fwd(proj_a, w["conv"])
    beta, gc = _gates_fwd(proj_b, w["alog"], w["dtb"])
    gc_t = jnp.transpose(gc[:, :N_HEADS])
    u, w_, qd, kt, a_mat, t_fold = _gdr_prep_fwd(qkvn, beta, gc, gc_t)
    o_dn, states = _gdr_scan_fwd(u, w_, qd, kt, a_mat, gc)
    w.update(late_weights("mix", o_dn))
    og, y_dn = _gdr_out_fwd(o_dn, proj_a, w["dnw"], w["br_dn"])
    c_q, c_kv, qc, kc, vv, kct, vt = _mla_prep_fwd(proj_b, pos, w["qnw"], w["kvnw"], w["uq"], w["uk"], w["uv"])
    o_mla, lse = _attn_fwd(qc, kc, vt)
    y_mla = _mm(o_mla, w["br_mla"], out_dtype=BF16, name="f_y_mla")
    mixed, a1, h1, h1b = _mix_proj_ln1(y_dn, y_mla, proj_g, x, w["wo"], w["ln1g"], w["ln1b"])
    w.update(late_weights("ffn", a1))
    gt, up, act = _ffn_in_act(h1b, w["ffn_in_t"])
    ffn = _mm_resident(act, w["ffn_out"], name="f_ffn")
    g = {}
    dz2, d_pg, d_pp, dh1a, g["ln2g"], g["ln2b"], loss = _tail(h1, ffn, pb, tgt, w["ln2g"], w["ln2b"],
                                                            w["ple_gate"], w["ple_t"])
    g["ple_t"] = _mm(d_pp, pb, ta=True, out_dtype=BF16, name="b_w_ple")
    g["ple_gate"] = _mm(h1b, d_pg, ta=True, out_dtype=BF16, name="b_w_ple_gate")
    g["ffn_out"] = _mm(act, dz2, ta=True, out_dtype=BF16, name="b_w_ffn_out")
    d_act = _mm_resident(dz2, w["ffn_out"], tb=True, out_dtype=BF16, name="b_act")
    d_gu = _act_bwd(gt, up, d_act)
    g["ffn_in_t"] = _mm(d_gu, h1b, ta=True, out_dtype=BF16, name="b_w_ffn_in")
    d_gu = emit("ffn", g, d_gu)
    d_h1 = _mm_resident(d_gu, w["ffn_in_t"], add=(dh1a,), name="b_h1_ffn")
    dz1, dxa, d_proj_g, d_y_dn, d_y_mla, g["ln1g"], g["ln1b"] = _ln1_mix_bwd(
        x, a1, d_h1, d_pg, y_dn, y_mla, proj_g, w["ln1g"], w["wo"], w["ple_gate"])
    g["wo"] = _mm(mixed, dz1, ta=True, out_dtype=BF16, name="b_w_o")
    g["br_mla"] = _mm(o_mla, d_y_mla, ta=True, out_dtype=BF16, name="b_w_br_mla")
    d_o_mla = _mm(d_y_mla, w["br_mla"], tb=True, out_dtype=BF16, name="b_o_mla")
    d_qc, d_kc, d_v = _attn_bwd(qc, kc, kct, vv, o_mla, d_o_mla, lse)
    d_q_full, d_kn, d_cq, d_ckv, d_kr, g["qnw"], g["kvnw"] = _mla_prep_bwd(
        proj_b, pos, d_qc, d_kc, d_v, w["qnw"], w["kvnw"], w["uq"], w["uk"], w["uv"])
    g["uq"] = _mm(c_q, d_q_full, ta=True, out_dtype=BF16, name="b_w_uq")
    g["uk"] = _mm(c_kv, d_kn, ta=True, out_dtype=BF16, name="b_w_uk")
    g["uv"] = _mm(c_kv, d_v, ta=True, out_dtype=BF16, name="b_w_uv")
    g["br_dn"] = _mm(og, d_y_dn, ta=True, out_dtype=BF16, name="b_w_br_dn")
    d_y_dn = emit("mix", g, d_y_dn)
    d_o_dn, d_z, g["dnw"] = _gdr_out_bwd(o_dn, proj_a, d_y_dn, w["dnw"], w["br_dn"])
    du, dw, dqd, dkt, d_a, d_egl = _gdr_scan_bwd(u, w_, qd, kt, a_mat, gc, states, d_o_dn)
    dq, dk, dv, d_beta, d_gc = _gdr_prep_bwd(qkvn, beta, gc, gc_t, t_fold, u, w_, du, dw, dqd, dkt, d_a)
    d_egl_rows = jnp.pad(d_egl[:, None, :, 0], ((0, 0), (CHUNK - 1, 0), (0, LANES - N_HEADS))).reshape(s_dim, LANES)
    d_ba, g["alog"], g["dtb"] = _gates_bwd(proj_b, w["alog"], w["dtb"], gc, d_beta, d_gc, d_egl_rows)
    d_qkv, g["conv"] = _conv_bwd(proj_a, w["conv"], dq, dk, dv)
    zeros = jnp.zeros((s_dim, WB_CKV - Q_LORA), BF16)
    d_proj_b = jnp.concatenate([d_cq, zeros, d_ckv, d_kr, d_ba], axis=1)
    g["wa_qkv_t"] = _mm(d_qkv, xb, ta=True, name="b_w_qkv")
    g["wa_z_t"] = _mm(d_z, xb, ta=True, name="b_w_z")
    g["wg_t"] = _mm(d_proj_g, xb, ta=True, name="b_w_g")
    g["wb_t"] = _mm(d_proj_b, xb, ta=True, name="b_w_b")
    d_qkv = emit("small", dict(g, loss=loss), emit("w_in", g, d_qkv))
    dx = _input_grad(d_qkv, d_z, d_proj_g, d_proj_b, w["w_in_t"], w["wg_t"], w["wb_t"], dxa)
    return loss, dx, g


DX_ROWS = 512


def _input_proj(x, w_in_t, wg_t, wb_t):
    s_dim = x.shape[0]
    n_a = 4 * D_MODEL

    def body(x_ref, wa_ref, wg_ref, wb_ref, a_ref, g_ref, b_ref, xb_ref):
        xv = x_ref[...].astype(BF16)
        xb_ref[...] = xv
        a_ref[...] = _dot(xv, wa_ref[...], NT)
        g_ref[...] = _dot(xv, wg_ref[...], NT).astype(BF16)
        b_ref[...] = _dot(xv, wb_ref[...], NT)

    rows = lambda width: pl.BlockSpec((DX_ROWS, width), lambda i: (i, 0))
    whole = lambda shape: pl.BlockSpec(shape, lambda i: (0, 0), pipeline_mode=pl.Buffered(1))
    return pl.pallas_call(
        body,
        out_shape=[jax.ShapeDtypeStruct((s_dim, n_a), F32), jax.ShapeDtypeStruct((s_dim, wg_t.shape[0]), BF16),
                   jax.ShapeDtypeStruct((s_dim, wb_t.shape[0]), F32), jax.ShapeDtypeStruct((s_dim, D_MODEL), BF16)],
        grid=(s_dim // DX_ROWS,),
        in_specs=[rows(D_MODEL), whole((n_a, D_MODEL)), whole(wg_t.shape), whole(wb_t.shape)],
        out_specs=[rows(n_a), rows(wg_t.shape[0]), rows(wb_t.shape[0]), rows(D_MODEL)],
        compiler_params=pltpu.CompilerParams(dimension_semantics=("parallel",)),
        name="f_proj",
    )(x, w_in_t, wg_t, wb_t)


def _input_grad(d_qkv, d_z, d_g, d_b, w_in_t, wg_t, wb_t, add):
    s_dim = d_qkv.shape[0]
    n_qkv, n_a = d_qkv.shape[1], d_qkv.shape[1] + d_z.shape[1]

    def body(q_ref, z_ref, g_ref, b_ref, wa_ref, wg_ref, wb_ref, add_ref, o_ref):
        r = add_ref[...] + _dot(q_ref[...], wa_ref[0:n_qkv])
        r = r + _dot(z_ref[...], wa_ref[n_qkv:n_a])
        r = r + _dot(g_ref[...], wg_ref[...])
        o_ref[...] = r + _dot(b_ref[...], wb_ref[...])

    rows = lambda a: pl.BlockSpec((DX_ROWS, a.shape[1]), lambda i: (i, 0))
    whole = lambda shape: pl.BlockSpec(shape, lambda i: (0, 0), pipeline_mode=pl.Buffered(1))
    return pl.pallas_call(
        body,
        out_shape=jax.ShapeDtypeStruct((s_dim, D_MODEL), F32),
        grid=(s_dim // DX_ROWS,),
        in_specs=[rows(d_qkv), rows(d_z), rows(d_g), rows(d_b), whole((n_a, D_MODEL)), whole(wg_t.shape),
                  whole(wb_t.shape), rows(add)],
        out_specs=pl.BlockSpec((DX_ROWS, D_MODEL), lambda i: (i, 0)),
        compiler_params=pltpu.CompilerParams(dimension_semantics=("parallel",)),
        name="b_x",
    )(d_qkv, d_z, d_g, d_b, w_in_t, wg_t, wb_t, add)


_BIG = (("w_in", 1), ("w_uq", 0), ("w_uk", 0), ("w_uv", 0), ("w_br_dn", 0), ("w_br_mla", 0),
        ("w_o", 0), ("w_ffn_in", 1), ("w_ffn_out", 0), ("w_ple", 1), ("w_ple_gate", 0))
_BIG_AXIS = dict(_BIG)
_SMALL = ("ln1_g", "ln1_b", "ln2_g", "ln2_b", "q_norm_w", "kv_norm_w", "dn_norm_w", "dn_a_log", "dn_dt_bias")
_ORDER = ("w_in", "conv_w", "dn_a_log", "dn_dt_bias", "dn_norm_w", "q_norm_w", "w_uq", "kv_norm_w", "w_uk", "w_uv",
          "w_br_dn", "w_br_mla", "w_o", "ln1_g", "ln1_b", "w_ffn_in", "w_ffn_out", "w_ple", "w_ple_gate", "ln2_g",
          "ln2_b")


def _stored_shape(name, shard_shape):
    axis = _BIG_AXIS[name]
    lead = shard_shape[axis]
    return lead, int(np.prod(shard_shape)) // lead


def _to_stored(name, shard):
    return jnp.moveaxis(shard, _BIG_AXIS[name], 0).reshape(_stored_shape(name, shard.shape))


def _from_stored(name, stored, shard_shape):
    axis = _BIG_AXIS[name]
    moved = (shard_shape[axis],) + shard_shape[:axis] + shard_shape[axis + 1:]
    return jnp.moveaxis(stored.reshape(moved), 0, axis)


_W_IN_ROWS = np.cumsum([0, 3072, 1024, 8, 8, Q_LORA, KV_LORA, ROPE, D_MODEL, D_MODEL])


def _first_weights(w_in_t, conv_full, small):
    r = _W_IN_ROWS
    zr = lambda n: jnp.zeros((n, D_MODEL), w_in_t.dtype)
    w = {}
    w["w_in_t"] = w_in_t
    w["wg_t"] = w_in_t[r[7]:r[9]]
    w["wb_t"] = jnp.concatenate([w_in_t[r[4]:r[5]], zr(WB_CKV - Q_LORA), w_in_t[r[5]:r[7]], zr(LANES - ROPE),
                                 w_in_t[r[2]:r[4]], zr(LANES - 2 * N_HEADS)], axis=0)
    w["conv"] = conv_full
    pad_l = lambda v: jnp.pad(v, ((0, 0), (0, LANES - v.shape[1])))
    w["alog"], w["dtb"] = pad_l(small["dn_a_log"]), pad_l(small["dn_dt_bias"])
    w["dnw"], w["qnw"], w["kvnw"] = small["dn_norm_w"], small["q_norm_w"], small["kv_norm_w"]
    w["ln1g"], w["ln1b"], w["ln2g"], w["ln2b"] = small["ln1_g"], small["ln1_b"], small["ln2_g"], small["ln2_b"]
    return w


def _late_weights(group, fw):
    w = {}
    if group == "mix":
        uq = fw["w_uq"].reshape(Q_LORA, N_HEADS, HEAD + ROPE)
        uq_r = jnp.pad(uq[:, :, HEAD:], ((0, 0), (0, 0), (0, HEAD - ROPE)))
        w["uq"] = jnp.concatenate([uq[:, :, :HEAD].reshape(Q_LORA, -1), uq_r.reshape(Q_LORA, -1)], axis=1)
        w["uk"], w["uv"] = fw["w_uk"], fw["w_uv"]
        w["br_dn"], w["br_mla"], w["wo"] = fw["w_br_dn"], fw["w_br_mla"], fw["w_o"]
    else:
        w["ffn_in_t"], w["ffn_out"] = fw["w_ffn_in"], fw["w_ffn_out"]
        w["ple_t"], w["ple_gate"] = fw["w_ple"], fw["w_ple_gate"]
    return w


_GROUP_GRADS = {"ffn": (("w_ple", "ple_t"), ("w_ple_gate", "ple_gate"), ("w_ffn_out", "ffn_out"),
                        ("w_ffn_in", "ffn_in_t")),
                "mix": (("w_o", "wo"), ("w_br_mla", "br_mla"), ("w_uq", "uq"), ("w_uk", "uk"), ("w_uv", "uv"),
                        ("w_br_dn", "br_dn"))}


def _group_grads(group, g):
    out = {}
    for name, key in _GROUP_GRADS[group]:
        t = g[key]
        if name == "w_uq":
            uq_n = t[:, :D_MODEL].reshape(Q_LORA, N_HEADS, HEAD)
            uq_r = t[:, D_MODEL:].reshape(Q_LORA, N_HEADS, HEAD)[:, :, :ROPE]
            t = jnp.concatenate([uq_n, uq_r], axis=2).reshape(Q_LORA, -1)
        out[name] = t
    return out


PACK_ROWS = 512
SUBLANES = 8


def _pack_exchange(parts, name):
    arrays = []
    for a, _, _ in parts:
        if not any(a is b for b in arrays):
            arrays.append(a)
    index = lambda a: next(i for i, b in enumerate(arrays) if a is b)
    chunks, dst = [], 0
    for a, first, rows in parts:
        assert first % SUBLANES == 0 and rows % SUBLANES == 0
        chunks += [(index(a), first + o, dst + o, min(PACK_ROWS, rows - o)) for o in range(0, rows, PACK_ROWS)]
        dst += rows
    c, n, last = arrays[0].shape[1], len(arrays), len(chunks) - 1
    slab = dst // N_DEV
    assert slab * N_DEV == dst

    def body(*refs):
        src_refs, out_ref, recv_ref = refs[:n], refs[n], refs[n + 1]
        buf, sem_in, sem_out, send_sems, recv_sems = refs[n + 2:]
        x, y, core = lax.axis_index("x"), lax.axis_index("y"), lax.axis_index("c")

        def to_sibling(q):
            return pltpu.make_async_remote_copy(
                src_ref=out_ref.at[pl.ds((2 * q + 1 - core) * slab, slab)], dst_ref=recv_ref.at[q],
                send_sem=send_sems.at[q], recv_sem=recv_sems.at[q], device_id=(x, y, 1 - core),
                device_id_type=_MESH_ID)

        sent = [0]

        def send_packed(rows_done):
            while sent[0] < N_DEV // 2 and (2 * sent[0] + 2) * slab <= rows_done:
                to_sibling(sent[0]).start()
                sent[0] += 1

        def load(k):
            i, first, _, rows = chunks[k]
            return pltpu.make_async_copy(src_refs[i].at[pl.ds(first, rows)], buf.at[k % 2, pl.ds(0, rows)],
                                         sem_in.at[k % 2])

        def store(k):
            _, _, first, rows = chunks[k]
            return pltpu.make_async_copy(buf.at[k % 2, pl.ds(0, rows)], out_ref.at[pl.ds(first, rows), 0, :],
                                         sem_out.at[k % 2])

        load(0).start()
        for k in range(last + 1):
            load(k).wait()
            store(k).start()
            if k >= 1:
                store(k - 1).wait()
                send_packed(chunks[k][2])
            if k < last:
                load(k + 1).start()
        store(last).wait()
        send_packed(dst)
        for q in range(N_DEV // 2):
            to_sibling(q).wait_recv()
        for q in range(N_DEV // 2):
            to_sibling(q).wait_send()

    return pl.pallas_call(
        body,
        out_shape=[jax.ShapeDtypeStruct((dst, 1, c), F32), jax.ShapeDtypeStruct((N_DEV // 2, slab, 1, c), F32)],
        in_specs=[_ANY] * n,
        out_specs=[_ANY, _ANY],
        scratch_shapes=[pltpu.VMEM((2, PACK_ROWS, c), F32), pltpu.SemaphoreType.DMA((2,)),
                        pltpu.SemaphoreType.DMA((2,)), pltpu.SemaphoreType.DMA((N_DEV // 2,)),
                        pltpu.SemaphoreType.DMA((N_DEV // 2,))],
        name=name,
    )(*arrays)


def _w_in_grad_parts(g):
    wb = g["wb_t"]
    return [(g["wa_qkv_t"], 0, 3 * D_MODEL), (g["wa_z_t"], 0, D_MODEL), (wb, WB_BA, 2 * N_HEADS),
            (wb, WB_CQ, Q_LORA), (wb, WB_CKV, KV_LORA), (wb, WB_KR, ROPE), (g["wg_t"], 0, 2 * D_MODEL)]


def _small_grads(g):
    return {"ln1_g": g["ln1g"], "ln1_b": g["ln1b"], "ln2_g": g["ln2g"], "ln2_b": g["ln2b"], "q_norm_w": g["qnw"],
            "kv_norm_w": g["kvnw"], "dn_norm_w": g["dnw"], "dn_a_log": g["alog"], "dn_dt_bias": g["dtb"],
            "conv_w": g["conv"]}


_SMALL_SLOTS = {"ln1_g": (0, 0, 1024), "ln1_b": (1, 0, 1024), "ln2_g": (2, 0, 1024), "ln2_b": (3, 0, 1024),
                "q_norm_w": (4, 0, 384), "kv_norm_w": (4, 384, 256), "dn_norm_w": (4, 640, 128),
                "dn_a_log": (4, 768, 8), "dn_dt_bias": (4, 896, 8)}
_SMALL_ROWS, _LOSS_ROW, _CONV_ROW0, _CONV_ROWS = 24, 5, 8, 12


def _pack_small_grads(small_g, loss):
    zeros = lambda r, c: jnp.zeros((r, c), F32)
    row4 = jnp.concatenate([small_g["q_norm_w"], small_g["kv_norm_w"], small_g["dn_norm_w"], small_g["dn_a_log"],
                            small_g["dn_dt_bias"]], axis=1)
    row5 = jnp.concatenate([loss, zeros(1, FLAT_COLS - LANES)], axis=1)
    head = jnp.concatenate([small_g["ln1_g"], small_g["ln1_b"], small_g["ln2_g"], small_g["ln2_b"], row4, row5,
                            zeros(2, FLAT_COLS)], axis=0)
    conv = small_g["conv_w"].reshape(_CONV_ROWS, FLAT_COLS)
    return jnp.concatenate([head, conv, zeros(_SMALL_ROWS - _CONV_ROW0 - _CONV_ROWS, FLAT_COLS)], axis=0)


_MESH_ID = pl.DeviceIdType.MESH
_ANY = pl.BlockSpec(memory_space=pl.ANY)


def _all_gather(blocks, name):
    n = len(blocks)

    def body(*refs):
        x_refs, out_refs = refs[:n], refs[n:2 * n]
        send_sems, recv_sems, local_sems = refs[2 * n:]
        x, y, c = lax.axis_index("x"), lax.axis_index("y"), lax.axis_index("c")
        me, sibling = (x, y, c), (x, y, 1 - c)
        chips = [(1 - x, y), (x, 1 - y), (1 - x, 1 - y)]

        def slot(i, px, py, pc):
            return out_refs[i].at[4 * px + 2 * py + pc]

        def copy(i, k, origin, to, src=None):
            return pltpu.make_async_remote_copy(
                src_ref=slot(i, *origin) if src is None else src, dst_ref=slot(i, *origin),
                send_sem=send_sems.at[7 * i + k], recv_sem=recv_sems.at[7 * i + k], device_id=to,
                device_id_type=_MESH_ID)

        mine = [pltpu.make_async_copy(x_refs[i], slot(i, *me), local_sems.at[i]) for i in range(n)]
        first, passed = [], []
        for i in range(n):
            mine[i].start()
            first.append(copy(i, 0, me, sibling, src=x_refs[i]))
            first += [copy(i, 1 + j, me, (*chip, c), src=x_refs[i]) for j, chip in enumerate(chips)]
        for cp in first:
            cp.start()
        for i in range(n):
            for j, chip in enumerate(chips):
                copy(i, 1 + j, (*chip, c), me).wait_recv()
                passed.append(copy(i, 4 + j, (*chip, c), sibling))
                passed[-1].start()
        for i in range(n):
            copy(i, 0, sibling, me).wait_recv()
            for j, chip in enumerate(chips):
                copy(i, 4 + j, (*chip, 1 - c), me).wait_recv()
        for cp in first + passed:
            cp.wait_send()
        for cp in mine:
            cp.wait()

    return pl.pallas_call(
        body,
        out_shape=[jax.ShapeDtypeStruct((N_DEV,) + b.shape, b.dtype) for b in blocks],
        in_specs=[_ANY] * n,
        out_specs=[_ANY] * n,
        scratch_shapes=[pltpu.SemaphoreType.DMA((7 * n,)), pltpu.SemaphoreType.DMA((7 * n,)),
                        pltpu.SemaphoreType.DMA((n,))],
        name=name,
    )(*blocks)


def _col_tile(c):
    return c if c <= 256 else 256


def _chip_sum(src, recv, parity, name):
    _, r, _, c = src.shape
    tc = _col_tile(c)

    def body(par_ref, a_ref, b_ref, o_ref, ob_ref):
        s = a_ref[...] + b_ref[...]
        o_ref[...] = s
        ob_ref[...] = s.astype(BF16)

    rows = lambda f: pl.BlockSpec((None, r, None, tc), f)
    blk = pl.BlockSpec((None, r, tc), lambda q, j, par: (q, 0, j))
    return pl.pallas_call(
        body,
        out_shape=[jax.ShapeDtypeStruct((4, r, c), F32), jax.ShapeDtypeStruct((4, r, c), BF16)],
        grid_spec=pltpu.PrefetchScalarGridSpec(
            num_scalar_prefetch=1, grid=(4, c // tc),
            in_specs=[rows(lambda q, j, par: (2 * q + par[0], 0, 0, j)), rows(lambda q, j, par: (q, 0, 0, j))],
            out_specs=[blk, blk]),
        compiler_params=pltpu.CompilerParams(dimension_semantics=("parallel", "parallel")),
        name=name,
    )(parity, src, recv)


_HBM = pl.BlockSpec(memory_space=pltpu.HBM)
_SEM = pl.BlockSpec(memory_space=pltpu.SEMAPHORE)
_DATAFLOW = pltpu.SideEffectType.DATAFLOW_SIDE_EFFECTING
N_PEERS = N_DEV - 1


def _ring_peer(j):
    me = 4 * lax.axis_index("x") + 2 * lax.axis_index("y") + lax.axis_index("c")
    k = (me + j) % N_DEV
    return me, k, (k // 4, (k // 2) % 2, k % 2)


def _spread_copy(i, j, src_refs, land_refs, send_sems, recv_sems, scatter):
    me, k, peer = _ring_peer(j)
    return pltpu.make_async_remote_copy(
        src_ref=src_refs[i].at[k] if scatter else src_refs[i], dst_ref=land_refs[i].at[me],
        send_sem=send_sems.at[N_PEERS * i + j - 1], recv_sem=recv_sems.at[N_PEERS * i + j - 1], device_id=peer,
        device_id_type=_MESH_ID)


def _spread_start(srcs, carry, scatter, name):
    n = len(srcs)
    lands = [lax.empty(((N_DEV,) + s.shape[-2:]), s.dtype) for s in srcs]

    def body(*refs):
        src_refs, land_refs = refs[:n], refs[n:2 * n]
        send_sems, recv_sems, local_sems = refs[2 * n + 1:2 * n + 4]
        for i in range(n):
            for j in range(1, N_DEV):
                _spread_copy(i, j, src_refs, land_refs, send_sems, recv_sems, scatter).start()
        for i in range(n):
            _own_copy(i, src_refs, land_refs, local_sems, scatter).start()

    hbm = lambda a: pltpu.HBM(a.shape, a.dtype)
    sems = pltpu.SemaphoreType.DMA((N_PEERS * n,))
    pinned = [pltpu.with_memory_space_constraint(a, pltpu.HBM) for a in list(srcs) + lands + [carry]]
    res = pl.pallas_call(
        body, name=name,
        out_shape=(sems, sems, pltpu.SemaphoreType.DMA((n,)), *[hbm(a) for a in pinned]),
        in_specs=[_HBM] * (2 * n + 1),
        out_specs=(_SEM, _SEM, _SEM, *[_HBM] * (2 * n + 1)),
        input_output_aliases={i: 3 + i for i in range(2 * n + 1)},
        compiler_params=pltpu.CompilerParams(has_side_effects=_DATAFLOW),
    )(*pinned)
    return res[:3], list(res[3:3 + n]), list(res[3 + n:3 + 2 * n]), res[3 + 2 * n]


def _own_copy(i, src_refs, land_refs, local_sems, scatter):
    me = _ring_peer(0)[0]
    return pltpu.make_async_copy(src_refs[i].at[me] if scatter else src_refs[i], land_refs[i].at[me],
                                 local_sems.at[i])


def _spread_wait(started, after, scatter, name):
    sems, srcs, lands, _ = started
    n = len(srcs)

    def body(*refs):
        src_refs, land_refs = refs[:n], refs[n:2 * n]
        send_s, recv_s, local_s = refs[2 * n:2 * n + 3]
        for i in range(n):
            for j in range(1, N_DEV):
                cp = _spread_copy(i, j, src_refs, land_refs, send_s, recv_s, scatter)
                cp.wait_send()
                cp.wait_recv()
        for i in range(n):
            _own_copy(i, src_refs, land_refs, local_s, scatter).wait()

    hbm = lambda a: pltpu.HBM(a.shape, a.dtype)
    res = pl.pallas_call(
        body, name=name,
        out_shape=tuple(hbm(a) for a in srcs + lands),
        in_specs=[_HBM] * (2 * n) + [_SEM, _SEM, _SEM, pl.BlockSpec(memory_space=pl.ANY)],
        out_specs=tuple([_HBM] * (2 * n)),
        input_output_aliases={i: i for i in range(2 * n)},
        compiler_params=pltpu.CompilerParams(has_side_effects=_DATAFLOW),
    )(*srcs, *lands, *sems, after)
    return list(res[n:])


def _chips_copy(i, j, src_refs, land_refs, send_sems, recv_sems):
    x, y, c = lax.axis_index("x"), lax.axis_index("y"), lax.axis_index("c")
    tx, ty = [(1 - x, y), (x, 1 - y), (1 - x, 1 - y)][j]
    return pltpu.make_async_remote_copy(
        src_ref=src_refs[i].at[2 * tx + ty], dst_ref=land_refs[i].at[j], send_sem=send_sems.at[3 * i + j],
        recv_sem=recv_sems.at[3 * i + j], device_id=(tx, ty, c), device_id_type=_MESH_ID)


def _chips_start(srcs, carry, name):
    n = len(srcs)
    lands = [lax.empty((3,) + s.shape[1:], s.dtype) for s in srcs]

    def body(*refs):
        src_refs, land_refs = refs[:n], refs[n:2 * n]
        send_sems, recv_sems = refs[2 * n + 1:2 * n + 3]
        for i in range(n):
            for j in range(3):
                _chips_copy(i, j, src_refs, land_refs, send_sems, recv_sems).start()

    hbm = lambda a: pltpu.HBM(a.shape, a.dtype)
    sems = pltpu.SemaphoreType.DMA((3 * n,))
    pinned = [pltpu.with_memory_space_constraint(a, pltpu.HBM) for a in list(srcs) + lands + [carry]]
    res = pl.pallas_call(
        body, name=name,
        out_shape=(sems, sems, *[hbm(a) for a in pinned]),
        in_specs=[_HBM] * (2 * n + 1),
        out_specs=(_SEM, _SEM, *[_HBM] * (2 * n + 1)),
        input_output_aliases={i: 2 + i for i in range(2 * n + 1)},
        compiler_params=pltpu.CompilerParams(has_side_effects=_DATAFLOW),
    )(*pinned)
    return res[:2], list(res[2:2 + n]), list(res[2 + n:2 + 2 * n]), res[2 + 2 * n]


def _chips_wait(started, after, name):
    sems, srcs, lands, _ = started
    n = len(srcs)

    def body(*refs):
        src_refs, land_refs = refs[:n], refs[n:2 * n]
        send_s, recv_s = refs[2 * n:2 * n + 2]
        for i in range(n):
            for j in range(3):
                cp = _chips_copy(i, j, src_refs, land_refs, send_s, recv_s)
                cp.wait_send()
                cp.wait_recv()

    hbm = lambda a: pltpu.HBM(a.shape, a.dtype)
    res = pl.pallas_call(
        body, name=name,
        out_shape=tuple(hbm(a) for a in srcs + lands),
        in_specs=[_HBM] * (2 * n) + [_SEM, _SEM, pl.BlockSpec(memory_space=pl.ANY)],
        out_specs=tuple([_HBM] * (2 * n)),
        input_output_aliases={i: i for i in range(2 * n)},
        compiler_params=pltpu.CompilerParams(has_side_effects=_DATAFLOW),
    )(*srcs, *lands, *sems, after)
    return list(res[n:])


def _sum8(landing, name):
    _, r, c = landing.shape
    tc = _col_tile(c)

    def body(a_ref, o_ref):
        tot = a_ref[0].astype(F32)
        for k in range(1, N_DEV):
            tot = tot + a_ref[k].astype(F32)
        o_ref[...] = tot

    return pl.pallas_call(
        body,
        out_shape=jax.ShapeDtypeStruct((r, c), F32),
        grid=(c // tc,),
        in_specs=[pl.BlockSpec((N_DEV, r, tc), lambda j: (0, 0, j))],
        out_specs=pl.BlockSpec((r, tc), lambda j: (0, j)),
        compiler_params=pltpu.CompilerParams(dimension_semantics=("parallel",)),
        name=name,
    )(landing)


def _adamw_math(w, g, m, v):
    m = ADAM_B1 * m + (1.0 - ADAM_B1) * g
    v = ADAM_B2 * v + (1.0 - ADAM_B2) * (g * g)
    m_hat = m / (1.0 - ADAM_B1 ** ADAM_STEP)
    v_hat = v / (1.0 - ADAM_B2 ** ADAM_STEP)
    delta = -ADAM_LR * (m_hat / (jnp.sqrt(v_hat) + ADAM_EPS) + ADAM_WD * w)
    return delta, m, v


def _adamw(w, m, v, g, name):
    r, c = w.shape

    def fn(rows, consts):
        return list(_adamw_math(*rows)), []

    return _rowwise(fn, [w, g, m, v], [], [(c, F32)] * 3, tm=r if r <= 512 else 256, name=name)


def _adamw_sum8(w, m, v, landing, name):
    r, c = w.shape
    tc = _col_tile(c)

    def body(w_ref, m_ref, v_ref, a_ref, g_ref, d_ref, m2_ref, v2_ref):
        g = a_ref[0].astype(F32)
        for k in range(1, N_DEV):
            g = g + a_ref[k].astype(F32)
        delta, m2, v2 = _adamw_math(w_ref[...], g, m_ref[...], v_ref[...])
        g_ref[...] = g
        d_ref[...] = delta
        m2_ref[...] = m2
        v2_ref[...] = v2

    blk = pl.BlockSpec((r, tc), lambda j: (0, j))
    return pl.pallas_call(
        body,
        out_shape=[jax.ShapeDtypeStruct((r, c), F32)] * 4,
        grid=(c // tc,),
        in_specs=[blk, blk, blk, pl.BlockSpec((N_DEV, r, tc), lambda j: (0, 0, j))],
        out_specs=[blk] * 4,
        compiler_params=pltpu.CompilerParams(dimension_semantics=("parallel",)),
        name=name,
    )(w, m, v, landing)


def _adamw_parts(w, m, v, own, others, chip, name):
    r, _, c = w.shape
    tc = _col_tile(c)

    def body(q_ref, w_ref, m_ref, v_ref, a_ref, b_ref, g_ref, d_ref, m2_ref, v2_ref):
        g = ((a_ref[...] + b_ref[0].astype(F32)) + b_ref[1].astype(F32)) + b_ref[2].astype(F32)
        delta, m2, v2 = _adamw_math(w_ref[...], g, m_ref[...], v_ref[...])
        g_ref[...] = g
        d_ref[...] = delta
        m2_ref[...] = m2
        v2_ref[...] = v2

    row = pl.BlockSpec((r, None, tc), lambda j, q: (0, 0, j))
    return pl.pallas_call(
        body,
        out_shape=[jax.ShapeDtypeStruct((r, 1, c), F32)] * 4,
        grid_spec=pltpu.PrefetchScalarGridSpec(
            num_scalar_prefetch=1, grid=(c // tc,),
            in_specs=[row, row, row, pl.BlockSpec((None, r, tc), lambda j, q: (q[0], 0, j)),
                      pl.BlockSpec((3, r, tc), lambda j, q: (0, 0, j))],
            out_specs=[row] * 4),
        compiler_params=pltpu.CompilerParams(dimension_semantics=("parallel",)),
        name=name,
    )(chip, w, m, v, own, others)


def _adamw_small(gathered, params):
    ns = len(_SMALL)

    def body(*refs):
        g_ref, p_refs, o_refs = refs[0], refs[1:1 + 3 * ns], refs[1 + 3 * ns:]
        tot = g_ref[0]
        for k in range(1, N_DEV):
            tot = tot + g_ref[k]
        for i, name in enumerate(_SMALL):
            row, lane0, lanes = _SMALL_SLOTS[name]
            g = tot[row:row + 1, lane0:lane0 + lanes]
            w_, m_, v_ = (p_refs[3 * i + j][...] for j in range(3))
            delta, m2, v2 = _adamw_math(w_, g, m_, v_)
            for j, val in enumerate((g, delta, m2, v2)):
                o_refs[4 * i + j][...] = val
        o_refs[4 * ns][...] = tot[_LOSS_ROW:_LOSS_ROW + 1, 0:LANES]
        o_refs[4 * ns + 1][...] = tot[_CONV_ROW0:_CONV_ROW0 + _CONV_ROWS, :]

    out_shape = [jax.ShapeDtypeStruct(w.shape, F32) for (w, _, _) in params for _ in range(4)]
    out_shape += [jax.ShapeDtypeStruct((1, LANES), F32), jax.ShapeDtypeStruct((_CONV_ROWS, FLAT_COLS), F32)]
    flat = [a for wmv in params for a in wmv]
    return pl.pallas_call(body, out_shape=out_shape, name="adamw_small")(gathered, *flat)


def kernel(x, p, positions, w_in, conv_w, dn_a_log, dn_dt_bias, dn_norm_w, q_norm_w, w_uq, kv_norm_w, w_uk, w_uv, w_br_dn, w_br_mla, w_o, ln1_g, ln1_b, w_ffn_in, w_ffn_out, w_ple, w_ple_gate, ln2_g, ln2_b, loss_target, m_w_in, m_conv_w, m_dn_a_log, m_dn_dt_bias, m_dn_norm_w, m_q_norm_w, m_w_uq, m_kv_norm_w, m_w_uk, m_w_uv, m_w_br_dn, m_w_br_mla, m_w_o, m_ln1_g, m_ln1_b, m_w_ffn_in, m_w_ffn_out, m_w_ple, m_w_ple_gate, m_ln2_g, m_ln2_b, v_w_in, v_conv_w, v_dn_a_log, v_dn_dt_bias, v_dn_norm_w, v_q_norm_w, v_w_uq, v_kv_norm_w, v_w_uk, v_w_uv, v_w_br_dn, v_w_br_mla, v_w_o, v_ln1_g, v_ln1_b, v_w_ffn_in, v_w_ffn_out, v_w_ple, v_w_ple_gate, v_ln2_g, v_ln2_b):
    args = dict(locals())
    wts = {n: args[n] for n in _ORDER}
    mom1 = {n: args["m_" + n] for n in _ORDER}
    mom2 = {n: args["v_" + n] for n in _ORDER}
    big_names = [n for n, _ in _BIG]
    shard_shapes = {n: wts[n].shape[1:] for n in big_names}
    c_idx = lax.axis_index("c")
    q_idx = 2 * lax.axis_index("x") + lax.axis_index("y")
    parity, chip = c_idx.reshape(1).astype(jnp.int32), q_idx.reshape(1).astype(jnp.int32)

    stored = {n: _to_stored(n, wts[n][0]).astype(BF16) for n in big_names}
    first = _all_gather([stored["w_in"], conv_w[0]], "ag_first")
    group_names = {grp: [n for n, _ in pairs] for grp, pairs in _GROUP_GRADS.items()}
    carry, gathers = first[0], {}
    for grp in ("mix", "ffn"):
        gathers[grp] = _spread_start([stored[n] for n in group_names[grp]], carry, False, "ag_start_" + grp)
        carry = gathers[grp][3]
    conv_full = jnp.moveaxis(first[1], 0, 1).reshape(conv_w.shape[1], -1)
    small_w = {n: wts[n].astype(F32) for n in _SMALL}
    w = _first_weights(carry.reshape(-1, D_MODEL), conv_full, small_w)

    def late_weights(grp, after):
        got = _spread_wait(gathers[grp], after, False, "ag_wait_" + grp)
        return _late_weights(grp, {n: t.reshape(-1, t.shape[-1]) for n, t in zip(group_names[grp], got)})

    started = {}

    def emit(group, g, carry):
        if group == "w_in":
            rows, cols = _stored_shape("w_in", shard_shapes["w_in"])
            packed, from_sibling = _pack_exchange(_w_in_grad_parts(g), "rs_pack_sibling")
            own, own_bf = _chip_sum(packed.reshape(N_DEV, rows, 1, cols), from_sibling, parity, "rs_sum_w_in")
            started["w_in"] = (own, _chips_start([own_bf], carry, "rs_chips_start"))
            return started["w_in"][1][3]
        if group == "small":
            block = _pack_small_grads(_small_grads(g), g["loss"])
            started["small"] = _spread_start([block], carry, False, "ag_start_small")
            return started["small"][3]
        grads = _group_grads(group, g)
        srcs = [grads[n].reshape((N_DEV,) + _stored_shape(n, shard_shapes[n])) for n in grads]
        started[group] = (list(grads), _spread_start(srcs, carry, True, "rs_start_" + group))
        return started[group][1][3]

    s_dim = x.shape[1]
    loss, dx, g = _local_step(x[0], p[0, 0], positions.reshape(s_dim, 1).astype(F32), loss_target[0], w,
                              late_weights, emit)
    own, chips_started = started.pop("w_in")
    small_started = started.pop("small")

    out_g, out_d, out_m, out_v = {}, {}, {}, {}

    def update(n, grad, shp):
        flat2 = (shp[0], int(np.prod(shp[1:])))
        d, m2, v2 = _adamw(wts[n][0].reshape(flat2), mom1[n][0].reshape(flat2), mom2[n][0].reshape(flat2),
                           grad.reshape(flat2), "adamw_" + n)
        out_g[n], out_d[n], out_m[n], out_v[n] = grad, d.reshape(shp), m2.reshape(shp), v2.reshape(shp)

    for group, (names, st) in started.items():
        for n, landing in zip(names, _spread_wait(st, dx, True, "rs_wait_" + group)):
            shp = shard_shapes[n]
            if _BIG_AXIS[n] == 0 or shp[-1] % LANES:
                res = _adamw_sum8(_to_stored(n, wts[n][0]), _to_stored(n, mom1[n][0]), _to_stored(n, mom2[n][0]),
                                  landing, "adamw_" + n)
                out_g[n], out_d[n], out_m[n], out_v[n] = (_from_stored(n, t, shp) for t in res)
                last = res[3]
            else:
                update(n, _from_stored(n, _sum8(landing, "rs_total_" + n), shp), shp)

    from_chips = _chips_wait(chips_started, last, "rs_chips_wait")[0]
    g_small = _spread_wait(small_started, last, False, "ag_wait_small")[0]
    rows_first = lambda a: jnp.transpose(a, (2, 0, 1))
    res = _adamw_parts(rows_first(wts["w_in"]), rows_first(mom1["w_in"]), rows_first(mom2["w_in"]), own, from_chips,
                       chip, "adamw_w_in")
    out_g["w_in"], out_d["w_in"], out_m["w_in"], out_v["w_in"] = (jnp.transpose(t, (1, 2, 0))[0] for t in res)

    res = _adamw_small(g_small, [(wts[n], mom1[n], mom2[n]) for n in _SMALL])
    for i, n in enumerate(_SMALL):
        out_g[n], out_d[n], out_m[n], out_v[n] = res[4 * i:4 * i + 4]
    loss_out = res[4 * len(_SMALL)][0, 0]
    conv_shape = conv_w.shape[1:]
    conv_g = lax.dynamic_slice(res[-1].reshape(conv_shape[0], -1), (0, (2 * q_idx + c_idx) * conv_shape[1]),
                               conv_shape)
    update("conv_w", conv_g, conv_shape)

    expand = lambda d, n: d[n] if n in _SMALL else d[n][None]
    return (loss_out, dx[None], *[expand(out_g, n) for n in _ORDER], *[expand(out_d, n) for n in _ORDER],
            *[expand(out_m, n) for n in _ORDER], *[expand(out_v, n) for n in _ORDER])
```

```python
import functools

import numpy as np
import jax
import jax.numpy as jnp
from jax import lax
from jax.experimental import pallas as pl
from jax.experimental.pallas import tpu as pltpu

F32 = jnp.float32
BF16 = jnp.bfloat16

D_MODEL = 1024
N_HEADS = 8
HEAD = 128
CHUNK = 64
GROUP = 256
ROPE = 64
Q_LORA = 384
KV_LORA = 256
FFN_HIDDEN = 2816
PLE_DIM = 256
ROPE_BASE = 10000.0
ALPHA = 2.0 ** 0.25
SCALE = float((HEAD + ROPE) ** -0.5)
NEG_BIG = -1e30
EPS_RMS = 1e-6
EPS_LN = 1e-5

ADAM_LR = 0.001
ADAM_B1 = 0.9
ADAM_B2 = 0.999
ADAM_EPS = 1e-08
ADAM_WD = 0.01
ADAM_STEP = 10

N_DEV = 8
LANES = 128
FLAT_COLS = 1024

WB_CQ, WB_CKV, WB_KR, WB_BA, WB_COLS = 0, 512, 768, 896, 1024

HIGHEST = lax.Precision.HIGHEST

NN = (((1,), (0,)), ((), ()))
TN = (((0,), (0,)), ((), ()))
NT = (((1,), (1,)), ((), ()))


def _dot(a, b, dims=NN):
    return lax.dot_general(a.astype(BF16), b.astype(BF16), dims, preferred_element_type=F32)


def _dot32(a, b, dims=NN):
    return lax.dot_general(a, b, dims, precision=HIGHEST, preferred_element_type=F32)


def _sig(x):
    return 1.0 / (1.0 + jnp.exp(-x))


MM_TILE = 1536


def _pick_wide(n):
    if n <= MM_TILE:
        return n
    return max(t for t in range(LANES, MM_TILE + 1, LANES) if n % t == 0)


def _split_bf16(a):
    hi = a.astype(BF16)
    return hi, (a - hi.astype(F32)).astype(BF16)


def _dot3(a, b, dims=NN):
    ah, al = a if isinstance(a, tuple) else _split_bf16(a)
    bh, bl = b if isinstance(b, tuple) else _split_bf16(b)
    d = lambda p, q: lax.dot_general(p, q, dims, preferred_element_type=F32)
    return d(ah, bh) + (d(ah, bl) + d(al, bh))


def _mm(a, b, *, ta=False, tb=False, add=(), out_dtype=F32, name):
    if ta:
        k_dim, m_dim = a.shape
    else:
        m_dim, k_dim = a.shape
    if tb:
        n_dim, k2 = b.shape
    else:
        k2, n_dim = b.shape
    assert k_dim == k2, (a.shape, b.shape, ta, tb)
    tm = _pick_wide(m_dim)
    tn = _pick_wide(n_dim)
    tk = _pick_wide(k_dim)
    nk = k_dim // tk
    n_add = len(add)
    dims = TN if ta else (NT if tb else NN)
    assert not (ta and tb)

    def body(a_ref, b_ref, *rest):
        add_refs = rest[:n_add]
        o_ref = rest[n_add]
        acc = rest[n_add + 1]
        k = pl.program_id(2)

        @pl.when(k == 0)
        def _():
            acc[...] = jnp.zeros_like(acc)

        acc[...] += _dot(a_ref[...], b_ref[...], dims)

        @pl.when(k == nk - 1)
        def _():
            r = acc[...]
            for ar in add_refs:
                r = r + ar[...].astype(F32)
            o_ref[...] = r.astype(o_ref.dtype)

    a_spec = pl.BlockSpec((tk, tm), lambda i, j, k: (k, i)) if ta else pl.BlockSpec((tm, tk), lambda i, j, k: (i, k))
    b_spec = pl.BlockSpec((tn, tk), lambda i, j, k: (j, k)) if tb else pl.BlockSpec((tk, tn), lambda i, j, k: (k, j))
    o_spec = pl.BlockSpec((tm, tn), lambda i, j, k: (i, j))
    return pl.pallas_call(
        body,
        out_shape=jax.ShapeDtypeStruct((m_dim, n_dim), out_dtype),
        grid=(m_dim // tm, n_dim // tn, nk),
        in_specs=[a_spec, b_spec] + [o_spec] * n_add,
        out_specs=o_spec,
        scratch_shapes=[pltpu.VMEM((tm, tn), F32)],
        compiler_params=pltpu.CompilerParams(dimension_semantics=("parallel", "parallel", "arbitrary")),
        name=name,
    )(a, b, *add)


def _mm_resident(a, b, *, tb=False, add=(), out_dtype=F32, name):
    m_dim, k_dim = a.shape
    n_dim, k2 = b.shape if tb else b.shape[::-1]
    assert k2 == k_dim
    tm = min(DX_ROWS, m_dim)
    dims = NT if tb else NN

    def body(a_ref, b_ref, *rest):
        r = _dot(a_ref[...], b_ref[...], dims)
        for ar in rest[:-1]:
            r = r + ar[...]
        rest[-1][...] = r.astype(out_dtype)

    o_spec = pl.BlockSpec((tm, n_dim), lambda i: (i, 0))
    return pl.pallas_call(
        body,
        out_shape=jax.ShapeDtypeStruct((m_dim, n_dim), out_dtype),
        grid=(m_dim // tm,),
        in_specs=[pl.BlockSpec((tm, k_dim), lambda i: (i, 0)),
                  pl.BlockSpec(b.shape, lambda i: (0, 0), pipeline_mode=pl.Buffered(1))] + [o_spec] * len(add),
        out_specs=o_spec,
        compiler_params=pltpu.CompilerParams(dimension_semantics=("parallel",)),
        name=name,
    )(a, b, *add)


def _rowwise(fn, rows, consts, outs, accs=(), *, tm=256, name):
    rows = [r if isinstance(r, tuple) else (r, 0, r.shape[1]) for r in rows]
    s_dim = rows[0][0].shape[0]
    tm = min(tm, s_dim)
    assert s_dim % tm == 0 and all(arr.shape[0] == s_dim for arr, _, _ in rows)
    specs = [pl.BlockSpec((tm, width), functools.partial(lambda i, cb: (i, cb), cb=cb)) for _, cb, width in rows]
    args = [arr for arr, _, _ in rows]
    for c in consts:
        specs.append(pl.BlockSpec(c.shape, lambda i: (0, 0)))
        args.append(c)
    nr, nc, no = len(rows), len(consts), len(outs)
    flipped = [len(o) == 3 for o in outs]
    out_shape = [jax.ShapeDtypeStruct((o[0], s_dim) if t else (s_dim, o[0]), o[1]) for o, t in zip(outs, flipped)]
    out_specs = [pl.BlockSpec((o[0], tm), lambda i: (0, i)) if t else pl.BlockSpec((tm, o[0]), lambda i: (i, 0))
                 for o, t in zip(outs, flipped)]
    out_shape += [jax.ShapeDtypeStruct(sh, F32) for sh in accs]
    out_specs += [pl.BlockSpec(sh, lambda i: (0, 0)) for sh in accs]

    def body(*refs):
        r = [x[...].astype(F32) if x.dtype == BF16 else x[...] for x in refs[:nr]]
        c = [x[...] for x in refs[nr:nr + nc]]
        o_refs = refs[nr + nc:nr + nc + no]
        a_refs = refs[nr + nc + no:]
        o_vals, a_vals = fn(r, c)
        for ref, v, t in zip(o_refs, o_vals, flipped, strict=True):
            ref[...] = (jnp.transpose(v.astype(F32)) if t else v).astype(ref.dtype)
        if a_refs:
            @pl.when(pl.program_id(0) == 0)
            def _():
                for ref in a_refs:
                    ref[...] = jnp.zeros_like(ref)

            for ref, v in zip(a_refs, a_vals, strict=True):
                ref[...] += v

    res = pl.pallas_call(
        body,
        out_shape=out_shape,
        grid=(s_dim // tm,),
        in_specs=specs,
        out_specs=out_specs,
        compiler_params=pltpu.CompilerParams(dimension_semantics=("arbitrary" if accs else "parallel",)),
        name=name,
    )(*args)
    return res


def _colsum(v):
    return jnp.sum(v, axis=0, keepdims=True)


def _rowsum(v):
    return jnp.sum(v, axis=1, keepdims=True)


def _rowmean(v):
    return jnp.mean(v, axis=1, keepdims=True)


def _silu_grad(x):
    s = _sig(x)
    return s * (1.0 + x * (1.0 - s))


def _conv_taps(x, w, width=4):
    row = lax.broadcasted_iota(jnp.int32, x.shape, 0)
    c = x * w[width - 1:width, :]
    for s in range(1, width):
        c = c + jnp.where(row >= s, pltpu.roll(x, s, 0), 0.0) * w[width - 1 - s:width - s, :]
    return c


def _conv_fwd(proj_a, conv_w):
    s_dim = proj_a.shape[0]
    n_blk = 3 * N_HEADS

    def body(x_ref, w_ref, o_ref):
        j = pl.program_id(0)
        c = _conv_taps(x_ref[...], w_ref[...])
        y = c * _sig(c)
        r = lax.rsqrt(_rowsum(y * y) + EPS_RMS)
        fac = jnp.where(j < N_HEADS, r * (HEAD ** -0.5), jnp.where(j < 2 * N_HEADS, r, 1.0))
        o_ref[...] = y * fac

    return pl.pallas_call(
        body,
        out_shape=jax.ShapeDtypeStruct((s_dim, n_blk * HEAD), F32),
        grid=(n_blk,),
        in_specs=[pl.BlockSpec((s_dim, HEAD), lambda j: (0, j)), pl.BlockSpec((4, HEAD), lambda j: (0, j))],
        out_specs=pl.BlockSpec((s_dim, HEAD), lambda j: (0, j)),
        compiler_params=pltpu.CompilerParams(dimension_semantics=("parallel",)),
        name="conv_fwd",
    )(proj_a, conv_w)


def _conv_bwd(proj_a, conv_w, dq, dk, dv):
    s_dim = proj_a.shape[0]
    n_blk = 3 * N_HEADS

    def body(x_ref, w_ref, dq_ref, dk_ref, dv_ref, dx_ref, dw_ref):
        j = pl.program_id(0)
        x = x_ref[...]
        w = w_ref[...]
        do = jnp.where(j < N_HEADS, dq_ref[...], jnp.where(j < 2 * N_HEADS, dk_ref[...], dv_ref[...]))
        c = _conv_taps(x, w)
        sg = _sig(c)
        y = c * sg
        r = lax.rsqrt(_rowsum(y * y) + EPS_RMS)
        sc = jnp.where(j < N_HEADS, HEAD ** -0.5, 1.0)
        dy_n = sc * (r * do - y * (r * r * r) * _rowsum(do * y))
        dy = jnp.where(j < 2 * N_HEADS, dy_n, do)
        dc = dy * (sg * (1.0 + c * (1.0 - sg)))
        row = lax.broadcasted_iota(jnp.int32, x.shape, 0)
        dx = dc * w[3:4, :]
        dw_ref[3:4, :] = _colsum(dc * x)
        for s in range(1, 4):
            dx = dx + jnp.where(row < s_dim - s, pltpu.roll(dc, s_dim - s, 0), 0.0) * w[3 - s:4 - s, :]
            xs = jnp.where(row >= s, pltpu.roll(x, s, 0), 0.0)
            dw_ref[3 - s:4 - s, :] = _colsum(dc * xs)
        dx_ref[...] = dx.astype(dx_ref.dtype)

    hd = N_HEADS - 1
    return pl.pallas_call(
        body,
        out_shape=[jax.ShapeDtypeStruct((s_dim, n_blk * HEAD), BF16), jax.ShapeDtypeStruct((4, n_blk * HEAD), F32)],
        grid=(n_blk,),
        in_specs=[
            pl.BlockSpec((s_dim, HEAD), lambda j: (0, j)),
            pl.BlockSpec((4, HEAD), lambda j: (0, j)),
            pl.BlockSpec((s_dim, HEAD), lambda j: (0, jnp.minimum(j, hd))),
            pl.BlockSpec((s_dim, HEAD), lambda j: (0, jnp.clip(j - N_HEADS, 0, hd))),
            pl.BlockSpec((s_dim, HEAD), lambda j: (0, jnp.clip(j - 2 * N_HEADS, 0, hd))),
        ],
        out_specs=[pl.BlockSpec((s_dim, HEAD), lambda j: (0, j)), pl.BlockSpec((4, HEAD), lambda j: (0, j))],
        compiler_params=pltpu.CompilerParams(dimension_semantics=("parallel",)),
        name="conv_bwd",
    )(proj_a, conv_w, dq, dk, dv)


def _chunk_tri(n):
    r = np.arange(n)
    m = ((r[:, None] // CHUNK) == (r[None, :] // CHUNK)) & (r[:, None] >= r[None, :])
    m = m.astype(np.float32)
    return jnp.asarray(m), jnp.asarray(m.T)


def _softplus(z):
    return jnp.maximum(z, 0.0) + jnp.log(1.0 + jnp.exp(-jnp.abs(z)))


def _gates_fwd(proj_b, alog, dtb):
    tm = min(GROUP, proj_b.shape[0])
    tri, _ = _chunk_tri(tm)

    def fn(r, c):
        b = r[0]
        a = pltpu.roll(b, LANES - N_HEADS, 1)
        alog_, dtb_, tri_ = c
        g = -jnp.exp(alog_) * _softplus(a + dtb_)
        return [_sig(b), _dot32(tri_, g)], []

    return _rowwise(fn, [(proj_b, WB_BA // LANES, LANES)], [alog, dtb, tri],
                    [(LANES, F32), (LANES, F32)], tm=tm, name="gates_fwd")


def _gates_bwd(proj_b, alog, dtb, gc, d_beta, d_gc, d_egl_rows):
    tm = min(GROUP, proj_b.shape[0])
    _, tri_t = _chunk_tri(tm)

    def fn(r, c):
        b, gc_, d_beta_, d_gc_, d_egl_ = r
        a = pltpu.roll(b, LANES - N_HEADS, 1)
        alog_, dtb_, tri_t_ = c
        z = a + dtb_
        ea = jnp.exp(alog_)
        g = -ea * _softplus(z)
        dg = _dot32(tri_t_, d_gc_ + d_egl_ * jnp.exp(gc_))
        d_a = dg * (-ea) * _sig(z)
        beta = _sig(b)
        d_ba = d_beta_ * beta * (1.0 - beta) + pltpu.roll(d_a, N_HEADS, 1)
        return [d_ba], [_colsum(dg * g), _colsum(d_a)]

    return _rowwise(fn, [(proj_b, WB_BA // LANES, LANES), gc, d_beta, d_gc, d_egl_rows],
                    [alog, dtb, tri_t], [(LANES, BF16)], accs=[(1, LANES), (1, LANES)], tm=tm,
                    name="gates_bwd")


def _group_masks(n):
    r = lax.broadcasted_iota(jnp.int32, (n, n), 0)
    c = lax.broadcasted_iota(jnp.int32, (n, n), 1)
    same = (r // CHUNK) == (c // CHUNK)
    below, s = [], 2
    while s < CHUNK:
        below.append(jnp.logical_and((r // (2 * s)) == (c // (2 * s)),
                                     jnp.logical_and((r // s) % 2 == 1, (c // s) % 2 == 0)))
        s *= 2
    return dict(same=same, tril=jnp.logical_and(same, r >= c), strict=jnp.logical_and(same, r > c),
                last=c == (r // CHUNK) * CHUNK + (CHUNK - 1), eye=r == c, pair=(r // 2) == (c // 2), below=below)


def _inv_unit_lower(l_mats, mk):
    eye_f = mk["eye"].astype(F32)
    ts = [eye_f - jnp.where(mk["pair"], l_mat, 0.0) for l_mat in l_mats]
    for below in mk["below"]:
        halves = [_split_bf16(t) for t in ts]
        mids = [_dot3(h, jnp.where(below, l_mat, 0.0)) for h, l_mat in zip(halves, l_mats)]
        ts = [t - _dot3(m, h) for t, m, h in zip(ts, mids, halves)]
    return ts


def _unfold_blocks(folded, mask):
    n = folded.shape[0]
    return jnp.where(mask, jnp.concatenate([folded] * (n // CHUNK), axis=1), 0.0)


def _head_cols(beta, gc, gc_t, h):
    lane = lax.broadcasted_iota(jnp.int32, beta.shape, 1)
    sub = lax.broadcasted_iota(jnp.int32, gc_t.shape, 0)
    bcol = _rowsum(jnp.where(lane == h, beta, 0.0))
    gcol = _rowsum(jnp.where(lane == h, gc, 0.0))
    grow = _colsum(jnp.where(sub == h, gc_t, 0.0))
    return bcol, gcol, grow


def _prep_common(q, k, bcol, gcol, grow, mk, t_folded=None):
    n = q.shape[0]
    tril = mk["tril"]
    decay = jnp.where(tril, jnp.exp(jnp.where(tril, gcol - grow, 0.0)), 0.0)
    glast = _rowsum(jnp.where(mk["last"], jnp.broadcast_to(grow, (n, n)), 0.0))
    e = jnp.exp(gcol)
    ekt = jnp.exp(glast - gcol)
    kb = k * bcol
    kk = _dot(kb, k, NT)
    qk = _dot(q, k, NT)
    p = dict(decay=decay, e=e, ekt=ekt, kb=kb, kk=kk, qk=qk)
    if t_folded is not None:
        p["t"] = _unfold_blocks(t_folded, mk["same"])
    return p


GROUPS_PER_STEP = 4
SCAN_CHUNKS_PER_STEP = 4


def _fold_blocks(m):
    n = m.shape[0]
    out = m[:, 0:CHUNK]
    for b in range(1, n // CHUNK):
        out = out + m[:, b * CHUNK:(b + 1) * CHUNK]
    return out


def _gdr_prep_fwd(qkvn, beta, gc, gc_t):
    s_dim = qkvn.shape[0]
    tg = min(GROUP, s_dim)
    n_sub = min(GROUPS_PER_STEP, s_dim // tg)
    tb = tg * n_sub

    def body(q_ref, k_ref, v_ref, b_ref, g_ref, gt_ref, u_ref, w_ref, qd_ref, kt_ref, a_ref, t_ref):
        h = pl.program_id(0)
        mk = _group_masks(tg)
        parts = []
        for s in range(n_sub):
            rows = slice(s * tg, (s + 1) * tg)
            q, k, v = q_ref[rows, :], k_ref[rows, :], v_ref[rows, :]
            bcol, gcol, grow = _head_cols(b_ref[rows, :], g_ref[rows, :], gt_ref[:, rows], h)
            p = _prep_common(q, k, bcol, gcol, grow, mk)
            qd_ref[rows, :] = q * p["e"]
            kt_ref[rows, :] = k * p["ekt"]
            a_ref[rows, :] = _fold_blocks(jnp.where(mk["tril"], p["qk"] * p["decay"], 0.0))
            parts.append((rows, v * bcol, p["kb"] * p["e"], jnp.where(mk["strict"], p["kk"] * p["decay"], 0.0)))
        t_mats = _inv_unit_lower([part[3] for part in parts], mk)
        for (rows, vb, kbe, _), t_mat in zip(parts, t_mats):
            u_ref[rows, :] = _dot(t_mat, vb)
            w_ref[rows, :] = _dot(t_mat, kbe)
            t_ref[rows, :] = _fold_blocks(t_mat)

    row = lambda off: pl.BlockSpec((tb, HEAD), functools.partial(lambda h, m, off: (m, h + off), off=off))
    full = pl.BlockSpec((tb, LANES), lambda h, m: (m, 0))
    o_spec = pl.BlockSpec((tb, HEAD), lambda h, m: (m, h))
    a_spec = pl.BlockSpec((None, tb, CHUNK), lambda h, m: (h, m, 0))
    wide = jax.ShapeDtypeStruct((s_dim, N_HEADS * HEAD), F32)
    folded = jax.ShapeDtypeStruct((N_HEADS, s_dim, CHUNK), F32)
    return pl.pallas_call(
        body,
        out_shape=[wide, wide, wide, wide, folded, folded],
        grid=(N_HEADS, s_dim // tb),
        in_specs=[row(0), row(N_HEADS), row(2 * N_HEADS), full, full, pl.BlockSpec((8, tb), lambda h, m: (0, m))],
        out_specs=[o_spec, o_spec, o_spec, o_spec, a_spec, a_spec],
        compiler_params=pltpu.CompilerParams(dimension_semantics=("parallel", "parallel")),
        name="gdr_prep_fwd",
    )(qkvn, qkvn, qkvn, beta, gc, gc_t)


def _gdr_prep_bwd(qkvn, beta, gc, gc_t, t_fold, u, w, du, dw, dqd, dkt, d_a):
    s_dim = qkvn.shape[0]
    tg = min(GROUP, s_dim)
    n_sub = min(GROUPS_PER_STEP, s_dim // tg)
    tb = tg * n_sub

    def body(q_ref, k_ref, v_ref, b_ref, g_ref, gt_ref, t_ref, u_ref, w_ref, du_ref, dw_ref, dqd_ref, dkt_ref,
             da_ref, dq_ref, dk_ref, dv_ref, db_ref, dg_ref):
        h = pl.program_id(1)

        @pl.when(h == 0)
        def _():
            db_ref[...] = jnp.zeros_like(db_ref)
            dg_ref[...] = jnp.zeros_like(dg_ref)

        mk = _group_masks(tg)
        lane = lax.broadcasted_iota(jnp.int32, (tg, LANES), 1)
        for s in range(n_sub):
            rows = slice(s * tg, (s + 1) * tg)
            q, k, v = q_ref[rows, :], k_ref[rows, :], v_ref[rows, :]
            bcol, gcol, grow = _head_cols(b_ref[rows, :], g_ref[rows, :], gt_ref[:, rows], h)
            p = _prep_common(q, k, bcol, gcol, grow, mk, t_ref[rows, :])
            t_mat, decay, e, ekt, kb = p["t"], p["decay"], p["e"], p["ekt"], p["kb"]
            du_, dw_, dqd_, dkt_ = du_ref[rows, :], dw_ref[rows, :], dqd_ref[rows, :], dkt_ref[rows, :]
            dvb = _dot(t_mat, du_, TN)
            dkbe = _dot(t_mat, dw_, TN)
            d_l = -(_dot(dvb, u_ref[rows, :], NT) + _dot(dkbe, w_ref[rows, :], NT))
            m1 = jnp.where(mk["strict"], d_l, 0.0)
            m2 = _unfold_blocks(da_ref[rows, :], mk["tril"])
            d_kk = m1 * decay
            d_qk = m2 * decay
            d_decay = m1 * p["kk"] + m2 * p["qk"]
            dkb = _dot(d_kk, k) + dkbe * e
            dk = _dot(d_kk, kb, TN) + _dot(d_qk, q, TN) + dkt_ * ekt + dkb * bcol
            dq = _dot(d_qk, k) + dqd_ * e
            d_beta = _rowsum(dkb * k) + _rowsum(dvb * v)
            d_e = _rowsum(dkbe * kb) + _rowsum(dqd_ * q)
            d_ekt = _rowsum(dkt_ * k) * ekt
            d_diff = d_decay * decay
            d_grow = -_colsum(d_diff) + _colsum(jnp.where(mk["last"], jnp.broadcast_to(d_ekt, (tg, tg)), 0.0))
            d_gcol = d_e * e - d_ekt + _rowsum(d_diff)
            d_gcol = d_gcol + _rowsum(jnp.where(mk["eye"], jnp.broadcast_to(d_grow, (tg, tg)), 0.0))
            dq_ref[rows, :] = dq
            dk_ref[rows, :] = dk
            dv_ref[rows, :] = dvb * bcol
            db_ref[rows, :] = jnp.where(lane == h, d_beta, db_ref[rows, :])
            dg_ref[rows, :] = jnp.where(lane == h, d_gcol, dg_ref[rows, :])

    row = lambda off: pl.BlockSpec((tb, HEAD), functools.partial(lambda m, h, off: (m, h + off), off=off))
    full = pl.BlockSpec((tb, LANES), lambda m, h: (m, 0))
    o_spec = pl.BlockSpec((tb, HEAD), lambda m, h: (m, h))
    a_spec = pl.BlockSpec((None, tb, CHUNK), lambda m, h: (h, m, 0))
    wide = jax.ShapeDtypeStruct((s_dim, N_HEADS * HEAD), F32)
    lanes = jax.ShapeDtypeStruct((s_dim, LANES), F32)
    return pl.pallas_call(
        body,
        out_shape=[wide, wide, wide, lanes, lanes],
        grid=(s_dim // tb, N_HEADS),
        in_specs=[row(0), row(N_HEADS), row(2 * N_HEADS), full, full, pl.BlockSpec((8, tb), lambda m, h: (0, m)),
                  a_spec, o_spec, o_spec, o_spec, o_spec, o_spec, o_spec, a_spec],
        out_specs=[o_spec, o_spec, o_spec, full, full],
        compiler_params=pltpu.CompilerParams(dimension_semantics=("parallel", "arbitrary")),
        name="gdr_prep_bwd",
    )(qkvn, qkvn, qkvn, beta, gc, gc_t, t_fold, u, w, du, dw, dqd, dkt, d_a)


def _gdr_scan_fwd(u, w, qd, kt, a_mat, gc):
    s_dim = u.shape[0]
    n_chunks = s_dim // CHUNK
    per = min(SCAN_CHUNKS_PER_STEP, n_chunks)
    tb = per * CHUNK

    def body(u_ref, w_ref, qd_ref, kt_ref, a_ref, g_ref, o_ref, st_ref, state):
        @pl.when(pl.program_id(0) == 0)
        def _():
            state[...] = jnp.zeros_like(state)

        heads = range(N_HEADS)
        cols = [slice(h * HEAD, (h + 1) * HEAD) for h in heads]
        for i in range(per):
            rows = slice(i * CHUNK, (i + 1) * CHUNK)
            egl = jnp.exp(g_ref[(i + 1) * CHUNK - 1:(i + 1) * CHUNK, :])
            s_b = [state[h].astype(BF16) for h in heads]
            for h in heads:
                st_ref[i, h] = state[h]
            ws = [_dot(w_ref[rows, cs], s) for cs, s in zip(cols, s_b)]
            qs = [_dot(qd_ref[rows, cs], s) for cs, s in zip(cols, s_b)]
            vns = [(u_ref[rows, cs] - ws_h).astype(BF16) for cs, ws_h in zip(cols, ws)]
            avs = [_dot(a_ref[h, rows, :], vn) for h, vn in zip(heads, vns)]
            kvs = [_dot(kt_ref[rows, cs], vn, TN) for cs, vn in zip(cols, vns)]
            for h, cs in zip(heads, cols):
                o_ref[rows, cs] = qs[h] + avs[h]
                state[h] = state[h] * egl[:, h:h + 1] + kvs[h]

    wide = pl.BlockSpec((tb, N_HEADS * HEAD), lambda n: (n, 0))
    return pl.pallas_call(
        body,
        out_shape=[jax.ShapeDtypeStruct((s_dim, N_HEADS * HEAD), F32),
                   jax.ShapeDtypeStruct((n_chunks, N_HEADS, HEAD, HEAD), F32)],
        grid=(n_chunks // per,),
        in_specs=[wide, wide, wide, wide, pl.BlockSpec((N_HEADS, tb, CHUNK), lambda n: (0, n, 0)),
                  pl.BlockSpec((tb, LANES), lambda n: (n, 0))],
        out_specs=[wide, pl.BlockSpec((per, N_HEADS, HEAD, HEAD), lambda n: (n, 0, 0, 0))],
        scratch_shapes=[pltpu.VMEM((N_HEADS, HEAD, HEAD), F32)],
        compiler_params=pltpu.CompilerParams(dimension_semantics=("arbitrary",)),
        name="gdr_scan_fwd",
    )(u, w, qd, kt, a_mat, gc)


def _gdr_scan_bwd(u, w, qd, kt, a_mat, gc, states, d_o):
    s_dim = u.shape[0]
    n_chunks = s_dim // CHUNK
    per = min(SCAN_CHUNKS_PER_STEP, n_chunks)
    tb = per * CHUNK
    last = n_chunks // per - 1

    def body(u_ref, w_ref, qd_ref, kt_ref, a_ref, g_ref, st_ref, do_ref,
             du_ref, dw_ref, dqd_ref, dkt_ref, da_ref, de_ref, d_state):
        @pl.when(pl.program_id(0) == 0)
        def _():
            d_state[...] = jnp.zeros_like(d_state)

        heads = range(N_HEADS)
        cols = [slice(h * HEAD, (h + 1) * HEAD) for h in heads]
        for i in reversed(range(per)):
            rows = slice(i * CHUNK, (i + 1) * CHUNK)
            egl = jnp.exp(g_ref[(i + 1) * CHUNK - 1:(i + 1) * CHUNK, :])
            s_b = [st_ref[i, h].astype(BF16) for h in heads]
            ds_b = [d_state[h].astype(BF16) for h in heads]
            dos = [do_ref[rows, cs].astype(BF16) for cs in cols]
            w_b = [w_ref[rows, cs].astype(BF16) for cs in cols]
            ws = [_dot(w_h, s) for w_h, s in zip(w_b, s_b)]
            ados = [_dot(a_ref[h, rows, :], do, TN) for h, do in zip(heads, dos)]
            kds = [_dot(kt_ref[rows, cs], ds) for cs, ds in zip(cols, ds_b)]
            dqds = [_dot(do, s, NT) for do, s in zip(dos, s_b)]
            qdos = [_dot(qd_ref[rows, cs], do, TN) for cs, do in zip(cols, dos)]
            vns = [(u_ref[rows, cs] - ws_h).astype(BF16) for cs, ws_h in zip(cols, ws)]
            dvns = [a + k_ for a, k_ in zip(ados, kds)]
            dvn_b = [d.astype(BF16) for d in dvns]
            das = [_dot(do, vn, NT) for do, vn in zip(dos, vns)]
            dkts = [_dot(vn, ds, NT) for vn, ds in zip(vns, ds_b)]
            dws = [_dot(d, s, NT) for d, s in zip(dvn_b, s_b)]
            wds = [_dot(w_h, d, TN) for w_h, d in zip(w_b, dvn_b)]
            for h, cs in zip(heads, cols):
                ds_n = d_state[h]
                de = jnp.sum(_rowsum(ds_n * st_ref[i, h]), axis=0, keepdims=True)
                de_ref[i, h:h + 1, :] = jnp.broadcast_to(de, (1, LANES))
                dqd_ref[rows, cs] = dqds[h]
                da_ref[h, rows, :] = das[h]
                dkt_ref[rows, cs] = dkts[h]
                du_ref[rows, cs] = dvns[h]
                dw_ref[rows, cs] = -dws[h]
                d_state[h] = ds_n * egl[:, h:h + 1] + qdos[h] - wds[h]

    wide = pl.BlockSpec((tb, N_HEADS * HEAD), lambda n: (last - n, 0))
    a_spec = pl.BlockSpec((N_HEADS, tb, CHUNK), lambda n: (0, last - n, 0))
    wide_shape = jax.ShapeDtypeStruct((s_dim, N_HEADS * HEAD), F32)
    return pl.pallas_call(
        body,
        out_shape=[wide_shape, wide_shape, wide_shape, wide_shape,
                   jax.ShapeDtypeStruct((N_HEADS, s_dim, CHUNK), F32),
                   jax.ShapeDtypeStruct((n_chunks, N_HEADS, LANES), F32)],
        grid=(n_chunks // per,),
        in_specs=[wide, wide, wide, wide, a_spec, pl.BlockSpec((tb, LANES), lambda n: (last - n, 0)),
                  pl.BlockSpec((per, N_HEADS, HEAD, HEAD), lambda n: (last - n, 0, 0, 0)), wide],
        out_specs=[wide, wide, wide, wide, a_spec, pl.BlockSpec((per, N_HEADS, LANES), lambda n: (last - n, 0, 0))],
        scratch_shapes=[pltpu.VMEM((N_HEADS, HEAD, HEAD), F32)],
        compiler_params=pltpu.CompilerParams(dimension_semantics=("arbitrary",)),
        name="gdr_scan_bwd",
    )(u, w, qd, kt, a_mat, gc, states, d_o)


FUSED_ROWS = 512


def _gdr_out_fwd(o_dn, proj_a, dn_w, w_br):
    def fn(r, c):
        o, z = r
        w_, w_br_ = c
        outs = []
        for h in range(N_HEADS):
            cs = slice(h * HEAD, (h + 1) * HEAD)
            oh, zh = o[:, cs], z[:, cs]
            rr = lax.rsqrt(_rowmean(oh * oh) + EPS_RMS)
            outs.append(oh * rr * w_ * (zh * _sig(zh)))
        og = jnp.concatenate(outs, axis=1).astype(BF16)
        return [og, _dot(og, w_br_)], []

    return _rowwise(fn, [o_dn, (proj_a, 3, D_MODEL)], [dn_w, w_br], [(D_MODEL, BF16), (D_MODEL, BF16)],
                    tm=FUSED_ROWS, name="gdr_out_fwd")


def _gdr_out_bwd(o_dn, proj_a, d_y_dn, dn_w, w_br):
    def fn(r, c):
        o, z, dy = r
        w_, w_br_ = c
        dg = _dot(dy, w_br_, NT)
        d_o, d_z = [], []
        d_w = jnp.zeros((1, HEAD), F32)
        for h in range(N_HEADS):
            cs = slice(h * HEAD, (h + 1) * HEAD)
            oh, zh, dgh = o[:, cs], z[:, cs], dg[:, cs]
            rr = lax.rsqrt(_rowmean(oh * oh) + EPS_RMS)
            sz = zh * _sig(zh)
            d_n = dgh * sz
            d_z.append(dgh * (oh * rr * w_) * _silu_grad(zh))
            d_w = d_w + _colsum(d_n * oh * rr)
            gw = d_n * w_
            d_o.append(rr * gw - oh * (rr * rr * rr) * _rowmean(gw * oh))
        return [jnp.concatenate(d_o, axis=1), jnp.concatenate(d_z, axis=1)], [d_w]

    return _rowwise(fn, [o_dn, (proj_a, 3, D_MODEL), d_y_dn], [dn_w, w_br], [(D_MODEL, F32), (D_MODEL, BF16)],
                    accs=[(1, HEAD)], tm=FUSED_ROWS, name="gdr_out_bwd")


def _rms_fwd(x, w):
    r = lax.rsqrt(_rowmean(x * x) + EPS_RMS)
    return x * r * w


def _rms_bwd(x, w, dy):
    r = lax.rsqrt(_rowmean(x * x) + EPS_RMS)
    gw = dy * w
    return r * gw - x * (r * r * r) * _rowmean(gw * x), _colsum(dy * x * r)


def _rope_consts():
    inv = ROPE_BASE ** (-np.arange(0, ROPE, 2, dtype=np.float32) / ROPE)
    t = np.zeros((4, LANES), np.float32)
    t[0, :32] = inv
    t[0, 32:64] = inv
    t[1, :64] = 1.0
    t[2, 32:64] = 1.0
    t[3, :32] = -1.0
    return jnp.asarray(t)


def _rope_tables(pos, consts, width):
    ang = pos * consts[0:1, :]
    cosv, sinv = jnp.cos(ang), jnp.sin(ang)
    reps = width // LANES
    tile = (lambda t: jnp.concatenate([t] * reps, axis=1)) if reps > 1 else (lambda t: t)
    return tile(cosv * consts[1:2, :]), tile(sinv * consts[2:3, :]), tile(sinv * consts[3:4, :])


def _rope_apply(t, tabs):
    cos_t, sin_a, sin_b = tabs
    width = t.shape[1]
    return t * cos_t + pltpu.roll(t, 32, 1) * sin_a + pltpu.roll(t, width - 32, 1) * sin_b


def _rope_transpose(d, tabs):
    cos_t, sin_a, sin_b = tabs
    width = d.shape[1]
    return d * cos_t + pltpu.roll(d * sin_a, width - 32, 1) + pltpu.roll(d * sin_b, 32, 1)


QK_HEAD = 2 * HEAD


def _interleave_heads(a, b):
    parts = []
    for h in range(N_HEADS):
        parts.append(a[:, h * HEAD:(h + 1) * HEAD])
        parts.append(b if b.shape[1] == LANES else b[:, h * LANES:(h + 1) * LANES])
    return jnp.concatenate(parts, axis=1)


def _mla_rows(proj_b):
    return [(proj_b, WB_CQ // Q_LORA, Q_LORA), (proj_b, WB_CKV // KV_LORA, KV_LORA), (proj_b, WB_KR // LANES, LANES)]


def _mla_prep_fwd(proj_b, pos, qn_w, kvn_w, uq, uk, uv):
    def fn(r, c):
        cq, ckv, kr, pos_ = r
        qn_w_, kvn_w_, uq_, uk_, uv_, rope = c
        c_q = _rms_fwd(cq, qn_w_).astype(BF16)
        c_kv = _rms_fwd(ckv, kvn_w_).astype(BF16)
        qf = _dot(c_q, uq_)
        qr = _rope_apply(qf[:, D_MODEL:], _rope_tables(pos_, rope, D_MODEL))
        kr = _rope_apply(kr, _rope_tables(pos_, rope, LANES))
        kc = _interleave_heads(_dot(c_kv, uk_), kr)
        v = _dot(c_kv, uv_)
        return [c_q, c_kv, _interleave_heads(qf[:, :D_MODEL], qr) * SCALE, kc, v, kc, v], []

    wide2 = N_HEADS * QK_HEAD
    return _rowwise(fn, _mla_rows(proj_b) + [pos], [qn_w, kvn_w, uq, uk, uv, _rope_consts()],
                    [(Q_LORA, BF16), (KV_LORA, BF16), (wide2, BF16), (wide2, BF16), (D_MODEL, BF16),
                     (wide2, BF16, "T"), (D_MODEL, BF16, "T")], tm=FUSED_ROWS, name="mla_prep_fwd")


def _mla_prep_bwd(proj_b, pos, d_qc, d_kc, d_v, qn_w, kvn_w, uq, uk, uv):
    def fn(r, c):
        cq, ckv, _, pos_, dq, dk, dv = r
        qn_w_, kvn_w_, uq_, uk_, uv_, rope = c
        even = lambda t: jnp.concatenate([t[:, (2 * h) * LANES:(2 * h + 1) * LANES] for h in range(N_HEADS)], axis=1)
        odd = lambda t: jnp.concatenate([t[:, (2 * h + 1) * LANES:(2 * h + 2) * LANES] for h in range(N_HEADS)], axis=1)
        d_qr_raw = _rope_transpose(odd(dq), _rope_tables(pos_, rope, D_MODEL)) * SCALE
        d_qf = jnp.concatenate([even(dq) * SCALE, d_qr_raw], axis=1).astype(BF16)
        d_kn = even(dk).astype(BF16)
        dkr = dk[:, LANES:2 * LANES]
        for h in range(1, N_HEADS):
            dkr = dkr + dk[:, (2 * h + 1) * LANES:(2 * h + 2) * LANES]
        d_cq, d_qnw = _rms_bwd(cq, qn_w_, _dot(d_qf, uq_, NT))
        d_ckv, d_kvnw = _rms_bwd(ckv, kvn_w_, _dot(d_kn, uk_, NT) + _dot(dv, uv_, NT))
        return [d_qf, d_kn, d_cq, d_ckv, _rope_transpose(dkr, _rope_tables(pos_, rope, LANES))], [d_qnw, d_kvnw]

    return _rowwise(fn, _mla_rows(proj_b) + [pos, d_qc, d_kc, d_v], [qn_w, kvn_w, uq, uk, uv, _rope_consts()],
                    [(2 * D_MODEL, BF16), (D_MODEL, BF16), (Q_LORA, BF16), (KV_LORA, BF16), (LANES, BF16)],
                    accs=[(1, Q_LORA), (1, KV_LORA)], tm=FUSED_ROWS, name="mla_prep_bwd")


def _causal_mask_t(st, key0, query0):
    key = lax.broadcasted_iota(jnp.int32, st.shape, 0) + key0
    query = lax.broadcasted_iota(jnp.int32, st.shape, 1) + query0
    return jnp.where(key <= query, st, NEG_BIG)


def _attn_tiles(s_dim):
    tq = min(512, s_dim)
    n_chains = 2 if s_dim >= 2 * tq else 1
    return tq, n_chains, min(512, s_dim)


def _diagonal_chains(t, tq, n_chains, tk):
    return [(c, (t + 1) * tk - 1 > c * tq) for c in range(n_chains) if t * tk < (c + 1) * tq]


def _attn_fwd(qc, kc, vt):
    s_dim = qc.shape[0]
    tq, n_chains, tk = _attn_tiles(s_dim)
    tqs = tq * n_chains

    def body(q_ref, k_ref, vt_ref, o_ref, lse_ref, m_s, l_s, acc):
        qi = pl.program_id(1)
        m_s[...] = jnp.full_like(m_s, NEG_BIG)
        l_s[...] = jnp.zeros_like(l_s)
        acc[...] = jnp.zeros_like(acc)

        def make_step(chains):
            def step(j, carry):
                ks = pl.multiple_of(j * tk, tk)
                kb, vtb = k_ref[pl.ds(ks, tk), :], vt_ref[:, pl.ds(ks, tk)]
                cols = [slice(c * tq, (c + 1) * tq) for c, _ in chains]
                sts = [_dot(kb, q_ref[cs, :], NT) for cs in cols]
                sts = [_causal_mask_t(st, j * tk, qi * tqs + c * tq) if masked else st
                       for st, (c, masked) in zip(sts, chains)]
                m_prevs = [m_s[:, cs] for cs in cols]
                m_news = [jnp.maximum(mp, jnp.max(st, axis=0, keepdims=True)) for mp, st in zip(m_prevs, sts)]
                alphas = [jnp.exp(mp - mn) for mp, mn in zip(m_prevs, m_news)]
                pts = [jnp.exp(st - mn) for st, mn in zip(sts, m_news)]
                pvs = [_dot(vtb, pt) for pt in pts]
                for cs, mn, al, pt, pv in zip(cols, m_news, alphas, pts, pvs):
                    l_s[:, cs] = al * l_s[:, cs] + _colsum(pt)
                    m_s[:, cs] = mn
                    acc[:, cs] = acc[:, cs] * al + pv
                return carry
            return step

        below = qi * (tqs // tk)
        lax.fori_loop(0, below, make_step([(c, False) for c in range(n_chains)]), 0)
        for t in range(tqs // tk):
            make_step(_diagonal_chains(t, tq, n_chains, tk))(below + t, 0)
        l = l_s[...]
        o_ref[...] = jnp.transpose(acc[...] / l)
        lse_ref[...] = m_s[...] + jnp.log(l)

    return pl.pallas_call(
        body,
        out_shape=[jax.ShapeDtypeStruct((s_dim, N_HEADS * HEAD), F32), jax.ShapeDtypeStruct((N_HEADS, 1, s_dim), F32)],
        grid=(N_HEADS, s_dim // tqs),
        in_specs=[pl.BlockSpec((tqs, QK_HEAD), lambda h, qi: (qi, h)),
                  pl.BlockSpec((s_dim, QK_HEAD), lambda h, qi: (0, h)),
                  pl.BlockSpec((HEAD, s_dim), lambda h, qi: (h, 0))],
        out_specs=[pl.BlockSpec((tqs, HEAD), lambda h, qi: (qi, h)),
                   pl.BlockSpec((None, 1, tqs), lambda h, qi: (h, 0, qi))],
        scratch_shapes=[pltpu.VMEM((1, tqs), F32), pltpu.VMEM((1, tqs), F32), pltpu.VMEM((HEAD, tqs), F32)],
        compiler_params=pltpu.CompilerParams(dimension_semantics=("parallel", "parallel")),
        name="attn_fwd",
    )(qc, kc, vt)


def _attn_bwd(qc, kc, kct, v, o, d_o, lse):
    s_dim = qc.shape[0]
    tq, n_chains, tk = _attn_tiles(s_dim)
    tqs = tq * n_chains

    def body(q_ref, k_ref, kt_ref, v_ref, o_ref, do_ref, lse_ref, dq_ref, dk_ref, dv_ref, dqt_acc, dv_acc):
        qi = pl.program_id(1)

        @pl.when(qi == 0)
        def _():
            dk_ref[...] = jnp.zeros_like(dk_ref)
            dv_acc[...] = jnp.zeros_like(dv_acc)

        dqt_acc[...] = jnp.zeros_like(dqt_acc)
        do_f = do_ref[...]
        do_all = do_f.astype(BF16)
        q_all = q_ref[...]
        lse_row = lse_ref[...]
        delta_row = _dot3(jnp.ones((8, HEAD), F32), o_ref[...] * do_f, NT)[0:1, :]

        def make_step(chains):
            rows = slice(chains[0][0] * tq, (chains[-1][0] + 1) * tq)

            def step(j, carry):
                ks = pl.multiple_of(j * tk, tk)
                kb, vb, ktb = k_ref[pl.ds(ks, tk), :], v_ref[pl.ds(ks, tk), :], kt_ref[:, pl.ds(ks, tk)]
                cols = [slice(c * tq, (c + 1) * tq) for c, _ in chains]
                sts = [_dot(kb, q_all[cs, :], NT) for cs in cols]
                sts = [_causal_mask_t(st, j * tk, qi * tqs + c * tq) if masked else st
                       for st, (c, masked) in zip(sts, chains)]
                dpts = [_dot(vb, do_all[cs, :], NT) for cs in cols]
                pts = [jnp.exp(st - lse_row[:, cs]) for st, cs in zip(sts, cols)]
                dsts = [(pt * (dpt - delta_row[:, cs])).astype(BF16) for pt, dpt, cs in zip(pts, dpts, cols)]
                pts = [pt.astype(BF16) for pt in pts]
                dqs = [_dot(ktb, dst) for dst in dsts]
                for cs, dq in zip(cols, dqs):
                    dqt_acc[:, cs] += dq
                pt_all = jnp.concatenate(pts, axis=1) if len(chains) > 1 else pts[0]
                dst_all = jnp.concatenate(dsts, axis=1) if len(chains) > 1 else dsts[0]
                dk_ref[pl.ds(ks, tk), :] += _dot(dst_all, q_all[rows, :])
                dv_acc[pl.ds(ks, tk), :] += _dot(pt_all, do_all[rows, :])
                return carry
            return step

        below = qi * (tqs // tk)
        lax.fori_loop(0, below, make_step([(c, False) for c in range(n_chains)]), 0)
        for t in range(tqs // tk):
            make_step(_diagonal_chains(t, tq, n_chains, tk))(below + t, 0)
        dq_ref[...] = jnp.transpose(dqt_acc[...])

        @pl.when(qi == s_dim // tqs - 1)
        def _():
            dv_ref[...] = dv_acc[...].astype(dv_ref.dtype)

    q_spec = pl.BlockSpec((tqs, QK_HEAD), lambda h, qi: (qi, h))
    o_spec = pl.BlockSpec((tqs, HEAD), lambda h, qi: (qi, h))
    k_spec = pl.BlockSpec((s_dim, QK_HEAD), lambda h, qi: (0, h))
    v_spec = pl.BlockSpec((s_dim, HEAD), lambda h, qi: (0, h))
    wide2 = jax.ShapeDtypeStruct((s_dim, N_HEADS * QK_HEAD), F32)
    return pl.pallas_call(
        body,
        out_shape=[wide2, wide2, jax.ShapeDtypeStruct((s_dim, N_HEADS * HEAD), BF16)],
        grid=(N_HEADS, s_dim // tqs),
        in_specs=[q_spec, k_spec, pl.BlockSpec((QK_HEAD, s_dim), lambda h, qi: (h, 0)), v_spec, o_spec, o_spec,
                  pl.BlockSpec((None, 1, tqs), lambda h, qi: (h, 0, qi))],
        out_specs=[q_spec, k_spec, v_spec],
        scratch_shapes=[pltpu.VMEM((QK_HEAD, tqs), F32), pltpu.VMEM((s_dim, HEAD), F32)],
        compiler_params=pltpu.CompilerParams(dimension_semantics=("parallel", "arbitrary")),
        name="attn_bwd",
    )(qc, kc, kct, v, o, d_o, lse)


def _mix_proj_ln1(y_dn, y_mla, proj_g, x, w_o, g, b):
    s_dim = x.shape[0]
    tm = min(512, s_dim)

    def body(yd_ref, ym_ref, g_ref, x_ref, w_ref, lg_ref, lb_ref, mixed_ref, a1_ref, h1_ref, h1b_ref):
        gates = g_ref[...].astype(F32)
        mixed = (_sig(gates[:, :D_MODEL]) * yd_ref[...].astype(F32)
                 + _sig(gates[:, D_MODEL:]) * ym_ref[...].astype(F32)).astype(BF16)
        a1 = _dot(mixed, w_ref[...])
        xh, _ = _ln_stats(ALPHA * x_ref[...] + a1)
        y = xh * lg_ref[...] + lb_ref[...]
        mixed_ref[...] = mixed
        a1_ref[...] = a1
        h1_ref[...] = y
        h1b_ref[...] = y.astype(BF16)

    row = lambda width: pl.BlockSpec((tm, width), lambda i: (i, 0))
    whole = lambda a: pl.BlockSpec(a.shape, lambda i: (0, 0))
    sds = lambda dt: jax.ShapeDtypeStruct((s_dim, D_MODEL), dt)
    return pl.pallas_call(
        body,
        out_shape=[sds(BF16), sds(F32), sds(F32), sds(BF16)],
        grid=(s_dim // tm,),
        in_specs=[row(D_MODEL), row(D_MODEL), row(2 * D_MODEL), row(D_MODEL), whole(w_o), whole(g), whole(b)],
        out_specs=[row(D_MODEL)] * 4,
        compiler_params=pltpu.CompilerParams(dimension_semantics=("parallel",)),
        name="mix_proj_ln1",
    )(y_dn, y_mla, proj_g, x, w_o, g, b)


def _ln1_mix_bwd(x, a1, d_h1, d_pg, y_dn, y_mla, proj_g, g, w_o, w_pg):
    def fn(r, c):
        x_, a1_, dy, dpg, yd, ym, gates = r
        g_, w_o_, w_pg_ = c
        dy = dy + _dot(dpg, w_pg_, NT)
        xh, rr = _ln_stats(ALPHA * x_ + a1_)
        dz = _ln_bwd(dy, xh, rr, g_)
        dz_b = dz.astype(BF16)
        dm = _dot(dz_b, w_o_, NT)
        sd, sm = _sig(gates[:, :D_MODEL]), _sig(gates[:, D_MODEL:])
        d_g = jnp.concatenate([dm * yd * sd * (1.0 - sd), dm * ym * sm * (1.0 - sm)], axis=1)
        return [dz_b, ALPHA * dz, d_g, dm * sd, dm * sm], [_colsum(dy * xh), _colsum(dy)]

    return _rowwise(fn, [x, a1, d_h1, d_pg, y_dn, y_mla, proj_g], [g, w_o, w_pg],
                    [(D_MODEL, BF16), (D_MODEL, F32), (2 * D_MODEL, BF16), (D_MODEL, BF16), (D_MODEL, BF16)],
                    accs=[(1, D_MODEL), (1, D_MODEL)], tm=FUSED_ROWS, name="ln1_mix_bwd")


def _ln_stats(z):
    mu = _rowmean(z)
    zc = z - mu
    r = lax.rsqrt(_rowmean(zc * zc) + EPS_LN)
    return zc * r, r


def _ln_bwd(dy, xh, r, g):
    dxh = dy * g
    return r * (dxh - _rowmean(dxh) - xh * _rowmean(dxh * xh))


def _ffn_in_act(h1b, w_t):
    s_dim, k_dim = h1b.shape
    hidden = w_t.shape[0] // 2
    tm, tn = min(512, s_dim), _pick_wide(hidden)
    nt = hidden // tn

    def body(a_ref, bg_ref, bu_ref, gt_ref, up_ref, act_ref):
        a = a_ref[...]
        gt, up = _dot(a, bg_ref[...], NT), _dot(a, bu_ref[...], NT)
        gt_ref[...] = gt.astype(BF16)
        up_ref[...] = up.astype(BF16)
        act_ref[...] = (gt * _sig(gt) * up).astype(BF16)

    o_spec = pl.BlockSpec((tm, tn), lambda j, i: (i, j))
    sds = jax.ShapeDtypeStruct((s_dim, hidden), BF16)
    return pl.pallas_call(
        body,
        out_shape=[sds, sds, sds],
        grid=(nt, s_dim // tm),
        in_specs=[pl.BlockSpec((tm, k_dim), lambda j, i: (i, 0)), pl.BlockSpec((tn, k_dim), lambda j, i: (j, 0)),
                  pl.BlockSpec((tn, k_dim), lambda j, i: (j + nt, 0))],
        out_specs=[o_spec, o_spec, o_spec],
        compiler_params=pltpu.CompilerParams(dimension_semantics=("parallel", "parallel")),
        name="ffn_in_act",
    )(h1b, w_t, w_t)


def _act_bwd(gt, up, d_act):
    def fn(r, c):
        gt_, up_, da = r
        return [jnp.concatenate([da * up_ * _silu_grad(gt_), da * gt_ * _sig(gt_)], axis=1)], []

    return _rowwise(fn, [gt, up, d_act], [], [(2 * FFN_HIDDEN, BF16)], name="act_bwd")[0]


def _tail(h1, ffn, p, tgt, g, b, w_pg, w_ple_t):
    def fn(r, c):
        h1_, ffn_, p_, t_ = r
        pg_ = _dot(h1_, c[2])
        pp_ = _dot(p_, c[3], NT)
        sp = _sig(pg_)
        xh, rr = _ln_stats(ALPHA * h1_ + ffn_ + sp * pp_)
        y = xh * c[0] + c[1]
        err = y - t_
        dy = err * (1.0 / D_MODEL)
        dz = _ln_bwd(dy, xh, rr, c[0])
        loss = jnp.sum(0.5 * _rowmean(err * err), axis=0, keepdims=True)
        return ([dz, dz * pp_ * sp * (1.0 - sp), dz * sp, ALPHA * dz],
                [_colsum(dy * xh), _colsum(dy), jnp.broadcast_to(loss, (1, LANES))])

    return _rowwise(fn, [h1, ffn, p, tgt], [g, b, w_pg, w_ple_t], [(D_MODEL, BF16)] * 3 + [(D_MODEL, F32)],
                    accs=[(1, D_MODEL), (1, D_MODEL), (1, LANES)], tm=FUSED_ROWS, name="tail")


def _local_step(x, p, pos, tgt, w, late_weights, emit):
    w = dict(w)
    s_dim = x.shape[0]
    pb = p.astype(BF16)
    proj_a, proj_g, proj_b, xb = _input_proj(x, w["w_in_t"], w["wg_t"], w["wb_t"])
    qkvn = _conv_fwd(proj_a, w["conv"])
    beta, gc = _gates_fwd(proj_b, w["alog"], w["dtb"])
    gc_t = jnp.transpose(gc[:, :N_HEADS])
    u, w_, qd, kt, a_mat, t_fold = _gdr_prep_fwd(qkvn, beta, gc, gc_t)
    o_dn, states = _gdr_scan_fwd(u, w_, qd, kt, a_mat, gc)
    w.update(late_weights("mix", o_dn))
    og, y_dn = _gdr_out_fwd(o_dn, proj_a, w["dnw"], w["br_dn"])
    c_q, c_kv, qc, kc, vv, kct, vt = _mla_prep_fwd(proj_b, pos, w["qnw"], w["kvnw"], w["uq"], w["uk"], w["uv"])
    o_mla, lse = _attn_fwd(qc, kc, vt)
    y_mla = _mm_resident(o_mla, w["br_mla"], out_dtype=BF16, name="f_y_mla")
    mixed, a1, h1, h1b = _mix_proj_ln1(y_dn, y_mla, proj_g, x, w["wo"], w["ln1g"], w["ln1b"])
    w.update(late_weights("ffn", a1))
    gt, up, act = _ffn_in_act(h1b, w["ffn_in_t"])
    ffn = _mm_resident(act, w["ffn_out"], name="f_ffn")
    g = {}
    dz2, d_pg, d_pp, dh1a, g["ln2g"], g["ln2b"], loss = _tail(h1, ffn, pb, tgt, w["ln2g"], w["ln2b"],
                                                            w["ple_gate"], w["ple_t"])
    g["ple_t"] = _mm(d_pp, pb, ta=True, out_dtype=BF16, name="b_w_ple")
    g["ple_gate"] = _mm(h1b, d_pg, ta=True, out_dtype=BF16, name="b_w_ple_gate")
    g["ffn_out"] = _mm(act, dz2, ta=True, out_dtype=BF16, name="b_w_ffn_out")
    d_act = _mm_resident(dz2, w["ffn_out"], tb=True, out_dtype=BF16, name="b_act")
    d_gu = _act_bwd(gt, up, d_act)
    g["ffn_in_t"] = _mm(d_gu, h1b, ta=True, out_dtype=BF16, name="b_w_ffn_in")
    d_gu = emit("ffn", g, d_gu)
    d_h1 = _mm_resident(d_gu, w["ffn_in_t"], add=(dh1a,), name="b_h1_ffn")
    dz1, dxa, d_proj_g, d_y_dn, d_y_mla, g["ln1g"], g["ln1b"] = _ln1_mix_bwd(
        x, a1, d_h1, d_pg, y_dn, y_mla, proj_g, w["ln1g"], w["wo"], w["ple_gate"])
    g["wo"] = _mm(mixed, dz1, ta=True, out_dtype=BF16, name="b_w_o")
    g["br_mla"] = _mm(o_mla, d_y_mla, ta=True, out_dtype=BF16, name="b_w_br_mla")
    d_o_mla = _mm_resident(d_y_mla, w["br_mla"], tb=True, out_dtype=BF16, name="b_o_mla")
    d_qc, d_kc, d_v = _attn_bwd(qc, kc, kct, vv, o_mla, d_o_mla, lse)
    d_q_full, d_kn, d_cq, d_ckv, d_kr, g["qnw"], g["kvnw"] = _mla_prep_bwd(
        proj_b, pos, d_qc, d_kc, d_v, w["qnw"], w["kvnw"], w["uq"], w["uk"], w["uv"])
    g["uq"] = _mm(c_q, d_q_full, ta=True, out_dtype=BF16, name="b_w_uq")
    g["uk"] = _mm(c_kv, d_kn, ta=True, out_dtype=BF16, name="b_w_uk")
    g["uv"] = _mm(c_kv, d_v, ta=True, out_dtype=BF16, name="b_w_uv")
    g["br_dn"] = _mm(og, d_y_dn, ta=True, out_dtype=BF16, name="b_w_br_dn")
    d_y_dn = emit("mix", g, d_y_dn)
    d_o_dn, d_z, g["dnw"] = _gdr_out_bwd(o_dn, proj_a, d_y_dn, w["dnw"], w["br_dn"])
    du, dw, dqd, dkt, d_a, d_egl = _gdr_scan_bwd(u, w_, qd, kt, a_mat, gc, states, d_o_dn)
    dq, dk, dv, d_beta, d_gc = _gdr_prep_bwd(qkvn, beta, gc, gc_t, t_fold, u, w_, du, dw, dqd, dkt, d_a)
    d_egl_rows = jnp.pad(d_egl[:, None, :, 0], ((0, 0), (CHUNK - 1, 0), (0, LANES - N_HEADS))).reshape(s_dim, LANES)
    d_ba, g["alog"], g["dtb"] = _gates_bwd(proj_b, w["alog"], w["dtb"], gc, d_beta, d_gc, d_egl_rows)
    d_qkv, g["conv"] = _conv_bwd(proj_a, w["conv"], dq, dk, dv)
    zeros = jnp.zeros((s_dim, WB_CKV - Q_LORA), BF16)
    d_proj_b = jnp.concatenate([d_cq, zeros, d_ckv, d_kr, d_ba], axis=1)
    g["wa_qkv_t"] = _mm(d_qkv, xb, ta=True, name="b_w_qkv")
    g["wa_z_t"] = _mm(d_z, xb, ta=True, name="b_w_z")
    g["wg_t"] = _mm(d_proj_g, xb, ta=True, name="b_w_g")
    g["wb_t"] = _mm(d_proj_b, xb, ta=True, name="b_w_b")
    d_qkv = emit("small", dict(g, loss=loss), emit("w_in", g, d_qkv))
    dx = _input_grad(d_qkv, d_z, d_proj_g, d_proj_b, w["w_in_t"], w["wg_t"], w["wb_t"], dxa)
    return loss, dx, g


DX_ROWS = 512


def _input_proj(x, w_in_t, wg_t, wb_t):
    s_dim = x.shape[0]
    n_a = 4 * D_MODEL

    def body(x_ref, wa_ref, wg_ref, wb_ref, a_ref, g_ref, b_ref, xb_ref):
        xv = x_ref[...].astype(BF16)
        xb_ref[...] = xv
        a_ref[...] = _dot(xv, wa_ref[...], NT)
        g_ref[...] = _dot(xv, wg_ref[...], NT).astype(BF16)
        b_ref[...] = _dot(xv, wb_ref[...], NT)

    rows = lambda width: pl.BlockSpec((DX_ROWS, width), lambda i: (i, 0))
    whole = lambda shape: pl.BlockSpec(shape, lambda i: (0, 0), pipeline_mode=pl.Buffered(1))
    return pl.pallas_call(
        body,
        out_shape=[jax.ShapeDtypeStruct((s_dim, n_a), F32), jax.ShapeDtypeStruct((s_dim, wg_t.shape[0]), BF16),
                   jax.ShapeDtypeStruct((s_dim, wb_t.shape[0]), F32), jax.ShapeDtypeStruct((s_dim, D_MODEL), BF16)],
        grid=(s_dim // DX_ROWS,),
        in_specs=[rows(D_MODEL), whole((n_a, D_MODEL)), whole(wg_t.shape), whole(wb_t.shape)],
        out_specs=[rows(n_a), rows(wg_t.shape[0]), rows(wb_t.shape[0]), rows(D_MODEL)],
        compiler_params=pltpu.CompilerParams(dimension_semantics=("parallel",)),
        name="f_proj",
    )(x, w_in_t, wg_t, wb_t)


def _input_grad(d_qkv, d_z, d_g, d_b, w_in_t, wg_t, wb_t, add):
    s_dim = d_qkv.shape[0]
    n_qkv, n_a = d_qkv.shape[1], d_qkv.shape[1] + d_z.shape[1]

    def body(q_ref, z_ref, g_ref, b_ref, wa_ref, wg_ref, wb_ref, add_ref, o_ref):
        r = add_ref[...] + _dot(q_ref[...], wa_ref[0:n_qkv])
        r = r + _dot(z_ref[...], wa_ref[n_qkv:n_a])
        r = r + _dot(g_ref[...], wg_ref[...])
        o_ref[...] = r + _dot(b_ref[...], wb_ref[...])

    rows = lambda a: pl.BlockSpec((DX_ROWS, a.shape[1]), lambda i: (i, 0))
    whole = lambda shape: pl.BlockSpec(shape, lambda i: (0, 0), pipeline_mode=pl.Buffered(1))
    return pl.pallas_call(
        body,
        out_shape=jax.ShapeDtypeStruct((s_dim, D_MODEL), F32),
        grid=(s_dim // DX_ROWS,),
        in_specs=[rows(d_qkv), rows(d_z), rows(d_g), rows(d_b), whole((n_a, D_MODEL)), whole(wg_t.shape),
                  whole(wb_t.shape), rows(add)],
        out_specs=pl.BlockSpec((DX_ROWS, D_MODEL), lambda i: (i, 0)),
        compiler_params=pltpu.CompilerParams(dimension_semantics=("parallel",)),
        name="b_x",
    )(d_qkv, d_z, d_g, d_b, w_in_t, wg_t, wb_t, add)


_BIG = (("w_in", 1), ("w_uq", 0), ("w_uk", 0), ("w_uv", 0), ("w_br_dn", 0), ("w_br_mla", 0),
        ("w_o", 0), ("w_ffn_in", 1), ("w_ffn_out", 0), ("w_ple", 1), ("w_ple_gate", 0))
_BIG_AXIS = dict(_BIG)
_SMALL = ("ln1_g", "ln1_b", "ln2_g", "ln2_b", "q_norm_w", "kv_norm_w", "dn_norm_w", "dn_a_log", "dn_dt_bias")
_ORDER = ("w_in", "conv_w", "dn_a_log", "dn_dt_bias", "dn_norm_w", "q_norm_w", "w_uq", "kv_norm_w", "w_uk", "w_uv",
          "w_br_dn", "w_br_mla", "w_o", "ln1_g", "ln1_b", "w_ffn_in", "w_ffn_out", "w_ple", "w_ple_gate", "ln2_g",
          "ln2_b")


def _stored_shape(name, shard_shape):
    axis = _BIG_AXIS[name]
    lead = shard_shape[axis]
    return lead, int(np.prod(shard_shape)) // lead


def _to_stored(name, shard):
    return jnp.moveaxis(shard, _BIG_AXIS[name], 0).reshape(_stored_shape(name, shard.shape))


def _from_stored(name, stored, shard_shape):
    axis = _BIG_AXIS[name]
    moved = (shard_shape[axis],) + shard_shape[:axis] + shard_shape[axis + 1:]
    return jnp.moveaxis(stored.reshape(moved), 0, axis)


_W_IN_ROWS = np.cumsum([0, 3072, 1024, 8, 8, Q_LORA, KV_LORA, ROPE, D_MODEL, D_MODEL])


def _first_weights(w_in_t, conv_full, small):
    r = _W_IN_ROWS
    zr = lambda n: jnp.zeros((n, D_MODEL), w_in_t.dtype)
    w = {}
    w["w_in_t"] = w_in_t
    w["wg_t"] = w_in_t[r[7]:r[9]]
    w["wb_t"] = jnp.concatenate([w_in_t[r[4]:r[5]], zr(WB_CKV - Q_LORA), w_in_t[r[5]:r[7]], zr(LANES - ROPE),
                                 w_in_t[r[2]:r[4]], zr(LANES - 2 * N_HEADS)], axis=0)
    w["conv"] = conv_full
    pad_l = lambda v: jnp.pad(v, ((0, 0), (0, LANES - v.shape[1])))
    w["alog"], w["dtb"] = pad_l(small["dn_a_log"]), pad_l(small["dn_dt_bias"])
    w["dnw"], w["qnw"], w["kvnw"] = small["dn_norm_w"], small["q_norm_w"], small["kv_norm_w"]
    w["ln1g"], w["ln1b"], w["ln2g"], w["ln2b"] = small["ln1_g"], small["ln1_b"], small["ln2_g"], small["ln2_b"]
    return w


def _late_weights(group, fw):
    w = {}
    if group == "mix":
        uq = fw["w_uq"].reshape(Q_LORA, N_HEADS, HEAD + ROPE)
        uq_r = jnp.pad(uq[:, :, HEAD:], ((0, 0), (0, 0), (0, HEAD - ROPE)))
        w["uq"] = jnp.concatenate([uq[:, :, :HEAD].reshape(Q_LORA, -1), uq_r.reshape(Q_LORA, -1)], axis=1)
        w["uk"], w["uv"] = fw["w_uk"], fw["w_uv"]
        w["br_dn"], w["br_mla"], w["wo"] = fw["w_br_dn"], fw["w_br_mla"], fw["w_o"]
    else:
        w["ffn_in_t"], w["ffn_out"] = fw["w_ffn_in"], fw["w_ffn_out"]
        w["ple_t"], w["ple_gate"] = fw["w_ple"], fw["w_ple_gate"]
    return w


_GROUP_GRADS = {"ffn": (("w_ple", "ple_t"), ("w_ple_gate", "ple_gate"), ("w_ffn_out", "ffn_out"),
                        ("w_ffn_in", "ffn_in_t")),
                "mix": (("w_o", "wo"), ("w_br_mla", "br_mla"), ("w_uq", "uq"), ("w_uk", "uk"), ("w_uv", "uv"),
                        ("w_br_dn", "br_dn"))}


def _group_grads(group, g):
    out = {}
    for name, key in _GROUP_GRADS[group]:
        t = g[key]
        if name == "w_uq":
            uq_n = t[:, :D_MODEL].reshape(Q_LORA, N_HEADS, HEAD)
            uq_r = t[:, D_MODEL:].reshape(Q_LORA, N_HEADS, HEAD)[:, :, :ROPE]
            t = jnp.concatenate([uq_n, uq_r], axis=2).reshape(Q_LORA, -1)
        out[name] = t
    return out


PACK_ROWS = 512
PACK_BUFFERS, PACK_AHEAD = 4, 2
SUBLANES = 8


def _pack_exchange(parts, name):
    arrays = []
    for a, _, _ in parts:
        if not any(a is b for b in arrays):
            arrays.append(a)
    index = lambda a: next(i for i, b in enumerate(arrays) if a is b)
    chunks, dst = [], 0
    for a, first, rows in parts:
        assert first % SUBLANES == 0 and rows % SUBLANES == 0
        chunks += [(index(a), first + o, dst + o, min(PACK_ROWS, rows - o)) for o in range(0, rows, PACK_ROWS)]
        dst += rows
    c, n, last = arrays[0].shape[1], len(arrays), len(chunks) - 1
    slab = dst // N_DEV
    assert slab * N_DEV == dst

    def body(*refs):
        src_refs, out_ref, recv_ref = refs[:n], refs[n], refs[n + 1]
        buf, sem_in, sem_out, send_sems, recv_sems = refs[n + 2:]
        x, y, core = lax.axis_index("x"), lax.axis_index("y"), lax.axis_index("c")

        def to_sibling(q):
            return pltpu.make_async_remote_copy(
                src_ref=out_ref.at[pl.ds((2 * q + 1 - core) * slab, slab)], dst_ref=recv_ref.at[q],
                send_sem=send_sems.at[q], recv_sem=recv_sems.at[q], device_id=(x, y, 1 - core),
                device_id_type=_MESH_ID)

        sent = [0]

        def send_packed(rows_done):
            while sent[0] < N_DEV // 2 and (2 * sent[0] + 2) * slab <= rows_done:
                to_sibling(sent[0]).start()
                sent[0] += 1

        def load(k):
            i, first, _, rows = chunks[k]
            return pltpu.make_async_copy(src_refs[i].at[pl.ds(first, rows)],
                                         buf.at[k % PACK_BUFFERS, pl.ds(0, rows)], sem_in.at[k % PACK_BUFFERS])

        def store(k):
            _, _, first, rows = chunks[k]
            return pltpu.make_async_copy(buf.at[k % PACK_BUFFERS, pl.ds(0, rows)],
                                         out_ref.at[pl.ds(first, rows), 0, :], sem_out.at[k % PACK_BUFFERS])

        def stored(k):
            store(k).wait()
            send_packed(chunks[k][2] + chunks[k][3])

        for k in range(min(PACK_AHEAD, last + 1)):
            load(k).start()
        for k in range(last + 1):
            load(k).wait()
            store(k).start()
            ahead = k + PACK_AHEAD
            if ahead <= last:
                if ahead >= PACK_BUFFERS:
                    stored(ahead - PACK_BUFFERS)
                load(ahead).start()
        for k in range(max(0, last + 1 - PACK_BUFFERS), last + 1):
            stored(k)
        for q in range(N_DEV // 2):
            to_sibling(q).wait_recv()
        for q in range(N_DEV // 2):
            to_sibling(q).wait_send()

    return pl.pallas_call(
        body,
        out_shape=[jax.ShapeDtypeStruct((dst, 1, c), F32), jax.ShapeDtypeStruct((N_DEV // 2, slab, 1, c), F32)],
        in_specs=[_ANY] * n,
        out_specs=[_ANY, _ANY],
        scratch_shapes=[pltpu.VMEM((PACK_BUFFERS, PACK_ROWS, c), F32), pltpu.SemaphoreType.DMA((PACK_BUFFERS,)),
                        pltpu.SemaphoreType.DMA((PACK_BUFFERS,)), pltpu.SemaphoreType.DMA((N_DEV // 2,)),
                        pltpu.SemaphoreType.DMA((N_DEV // 2,))],
        name=name,
    )(*arrays)


def _w_in_grad_parts(g):
    wb = g["wb_t"]
    return [(g["wa_qkv_t"], 0, 3 * D_MODEL), (g["wa_z_t"], 0, D_MODEL), (wb, WB_BA, 2 * N_HEADS),
            (wb, WB_CQ, Q_LORA), (wb, WB_CKV, KV_LORA), (wb, WB_KR, ROPE), (g["wg_t"], 0, 2 * D_MODEL)]


def _small_grads(g):
    return {"ln1_g": g["ln1g"], "ln1_b": g["ln1b"], "ln2_g": g["ln2g"], "ln2_b": g["ln2b"], "q_norm_w": g["qnw"],
            "kv_norm_w": g["kvnw"], "dn_norm_w": g["dnw"], "dn_a_log": g["alog"], "dn_dt_bias": g["dtb"],
            "conv_w": g["conv"]}


_SMALL_SLOTS = {"ln1_g": (0, 0, 1024), "ln1_b": (1, 0, 1024), "ln2_g": (2, 0, 1024), "ln2_b": (3, 0, 1024),
                "q_norm_w": (4, 0, 384), "kv_norm_w": (4, 384, 256), "dn_norm_w": (4, 640, 128),
                "dn_a_log": (4, 768, 8), "dn_dt_bias": (4, 896, 8)}
_SMALL_ROWS, _LOSS_ROW, _CONV_ROW0, _CONV_ROWS = 24, 5, 8, 12


def _pack_small_grads(small_g, loss):
    zeros = lambda r, c: jnp.zeros((r, c), F32)
    row4 = jnp.concatenate([small_g["q_norm_w"], small_g["kv_norm_w"], small_g["dn_norm_w"], small_g["dn_a_log"],
                            small_g["dn_dt_bias"]], axis=1)
    row5 = jnp.concatenate([loss, zeros(1, FLAT_COLS - LANES)], axis=1)
    head = jnp.concatenate([small_g["ln1_g"], small_g["ln1_b"], small_g["ln2_g"], small_g["ln2_b"], row4, row5,
                            zeros(2, FLAT_COLS)], axis=0)
    conv = small_g["conv_w"].reshape(_CONV_ROWS, FLAT_COLS)
    return jnp.concatenate([head, conv, zeros(_SMALL_ROWS - _CONV_ROW0 - _CONV_ROWS, FLAT_COLS)], axis=0)


_MESH_ID = pl.DeviceIdType.MESH
_ANY = pl.BlockSpec(memory_space=pl.ANY)


def _all_gather(blocks, name):
    n = len(blocks)

    def body(*refs):
        x_refs, out_refs = refs[:n], refs[n:2 * n]
        send_sems, recv_sems, local_sems = refs[2 * n:]
        x, y, c = lax.axis_index("x"), lax.axis_index("y"), lax.axis_index("c")
        me, sibling = (x, y, c), (x, y, 1 - c)
        chips = [(1 - x, y), (x, 1 - y), (1 - x, 1 - y)]

        def slot(i, px, py, pc):
            return out_refs[i].at[4 * px + 2 * py + pc]

        def copy(i, k, origin, to, src=None):
            return pltpu.make_async_remote_copy(
                src_ref=slot(i, *origin) if src is None else src, dst_ref=slot(i, *origin),
                send_sem=send_sems.at[7 * i + k], recv_sem=recv_sems.at[7 * i + k], device_id=to,
                device_id_type=_MESH_ID)

        mine = [pltpu.make_async_copy(x_refs[i], slot(i, *me), local_sems.at[i]) for i in range(n)]
        first, passed = [], []
        for i in range(n):
            mine[i].start()
            first.append(copy(i, 0, me, sibling, src=x_refs[i]))
            first += [copy(i, 1 + j, me, (*chip, c), src=x_refs[i]) for j, chip in enumerate(chips)]
        for cp in first:
            cp.start()
        for i in range(n):
            for j, chip in enumerate(chips):
                copy(i, 1 + j, (*chip, c), me).wait_recv()
                passed.append(copy(i, 4 + j, (*chip, c), sibling))
                passed[-1].start()
        for i in range(n):
            copy(i, 0, sibling, me).wait_recv()
            for j, chip in enumerate(chips):
                copy(i, 4 + j, (*chip, 1 - c), me).wait_recv()
        for cp in first + passed:
            cp.wait_send()
        for cp in mine:
            cp.wait()

    return pl.pallas_call(
        body,
        out_shape=[jax.ShapeDtypeStruct((N_DEV,) + b.shape, b.dtype) for b in blocks],
        in_specs=[_ANY] * n,
        out_specs=[_ANY] * n,
        scratch_shapes=[pltpu.SemaphoreType.DMA((7 * n,)), pltpu.SemaphoreType.DMA((7 * n,)),
                        pltpu.SemaphoreType.DMA((n,))],
        name=name,
    )(*blocks)


def _col_tile(c):
    return c if c <= 256 else 256


def _chip_sum(src, recv, parity, name):
    _, r, _, c = src.shape
    tc = _col_tile(c)

    def body(par_ref, a_ref, b_ref, o_ref, ob_ref):
        s = a_ref[...] + b_ref[...]
        o_ref[...] = s
        ob_ref[...] = s.astype(BF16)

    rows = lambda f: pl.BlockSpec((None, r, None, tc), f)
    blk = pl.BlockSpec((None, r, tc), lambda q, j, par: (q, 0, j))
    return pl.pallas_call(
        body,
        out_shape=[jax.ShapeDtypeStruct((4, r, c), F32), jax.ShapeDtypeStruct((4, r, c), BF16)],
        grid_spec=pltpu.PrefetchScalarGridSpec(
            num_scalar_prefetch=1, grid=(4, c // tc),
            in_specs=[rows(lambda q, j, par: (2 * q + par[0], 0, 0, j)), rows(lambda q, j, par: (q, 0, 0, j))],
            out_specs=[blk, blk]),
        compiler_params=pltpu.CompilerParams(dimension_semantics=("parallel", "parallel")),
        name=name,
    )(parity, src, recv)


_HBM = pl.BlockSpec(memory_space=pltpu.HBM)
_SEM = pl.BlockSpec(memory_space=pltpu.SEMAPHORE)
_DATAFLOW = pltpu.SideEffectType.DATAFLOW_SIDE_EFFECTING
N_PEERS = N_DEV - 1


def _ring_peer(j):
    me = 4 * lax.axis_index("x") + 2 * lax.axis_index("y") + lax.axis_index("c")
    k = (me + j) % N_DEV
    return me, k, (k // 4, (k // 2) % 2, k % 2)


def _spread_copy(i, j, src_refs, land_refs, send_sems, recv_sems, scatter):
    me, k, peer = _ring_peer(j)
    return pltpu.make_async_remote_copy(
        src_ref=src_refs[i].at[k] if scatter else src_refs[i], dst_ref=land_refs[i].at[me],
        send_sem=send_sems.at[N_PEERS * i + j - 1], recv_sem=recv_sems.at[N_PEERS * i + j - 1], device_id=peer,
        device_id_type=_MESH_ID)


def _spread_start(srcs, carry, scatter, name):
    n = len(srcs)
    lands = [lax.empty(((N_DEV,) + s.shape[-2:]), s.dtype) for s in srcs]

    def body(*refs):
        src_refs, land_refs = refs[:n], refs[n:2 * n]
        send_sems, recv_sems, local_sems = refs[2 * n + 1:2 * n + 4]
        for i in range(n):
            for j in range(1, N_DEV):
                _spread_copy(i, j, src_refs, land_refs, send_sems, recv_sems, scatter).start()
        for i in range(n):
            _own_copy(i, src_refs, land_refs, local_sems, scatter).start()

    hbm = lambda a: pltpu.HBM(a.shape, a.dtype)
    sems = pltpu.SemaphoreType.DMA((N_PEERS * n,))
    pinned = [pltpu.with_memory_space_constraint(a, pltpu.HBM) for a in list(srcs) + lands + [carry]]
    res = pl.pallas_call(
        body, name=name,
        out_shape=(sems, sems, pltpu.SemaphoreType.DMA((n,)), *[hbm(a) for a in pinned]),
        in_specs=[_HBM] * (2 * n + 1),
        out_specs=(_SEM, _SEM, _SEM, *[_HBM] * (2 * n + 1)),
        input_output_aliases={i: 3 + i for i in range(2 * n + 1)},
        compiler_params=pltpu.CompilerParams(has_side_effects=_DATAFLOW),
    )(*pinned)
    return res[:3], list(res[3:3 + n]), list(res[3 + n:3 + 2 * n]), res[3 + 2 * n]


def _own_copy(i, src_refs, land_refs, local_sems, scatter):
    me = _ring_peer(0)[0]
    return pltpu.make_async_copy(src_refs[i].at[me] if scatter else src_refs[i], land_refs[i].at[me],
                                 local_sems.at[i])


def _spread_wait(started, after, scatter, name):
    sems, srcs, lands, _ = started
    n = len(srcs)

    def body(*refs):
        src_refs, land_refs = refs[:n], refs[n:2 * n]
        send_s, recv_s, local_s = refs[2 * n:2 * n + 3]
        for i in range(n):
            for j in range(1, N_DEV):
                cp = _spread_copy(i, j, src_refs, land_refs, send_s, recv_s, scatter)
                cp.wait_send()
                cp.wait_recv()
        for i in range(n):
            _own_copy(i, src_refs, land_refs, local_s, scatter).wait()

    hbm = lambda a: pltpu.HBM(a.shape, a.dtype)
    res = pl.pallas_call(
        body, name=name,
        out_shape=tuple(hbm(a) for a in srcs + lands),
        in_specs=[_HBM] * (2 * n) + [_SEM, _SEM, _SEM, pl.BlockSpec(memory_space=pl.ANY)],
        out_specs=tuple([_HBM] * (2 * n)),
        input_output_aliases={i: i for i in range(2 * n)},
        compiler_params=pltpu.CompilerParams(has_side_effects=_DATAFLOW),
    )(*srcs, *lands, *sems, after)
    return list(res[n:])


def _chips_copy(i, j, src_refs, land_refs, send_sems, recv_sems):
    x, y, c = lax.axis_index("x"), lax.axis_index("y"), lax.axis_index("c")
    tx, ty = [(1 - x, y), (x, 1 - y), (1 - x, 1 - y)][j]
    return pltpu.make_async_remote_copy(
        src_ref=src_refs[i].at[2 * tx + ty], dst_ref=land_refs[i].at[j], send_sem=send_sems.at[3 * i + j],
        recv_sem=recv_sems.at[3 * i + j], device_id=(tx, ty, c), device_id_type=_MESH_ID)


def _chips_start(srcs, carry, name):
    n = len(srcs)
    lands = [lax.empty((3,) + s.shape[1:], s.dtype) for s in srcs]

    def body(*refs):
        src_refs, land_refs = refs[:n], refs[n:2 * n]
        send_sems, recv_sems = refs[2 * n + 1:2 * n + 3]
        for i in range(n):
            for j in range(3):
                _chips_copy(i, j, src_refs, land_refs, send_sems, recv_sems).start()

    hbm = lambda a: pltpu.HBM(a.shape, a.dtype)
    sems = pltpu.SemaphoreType.DMA((3 * n,))
    pinned = [pltpu.with_memory_space_constraint(a, pltpu.HBM) for a in list(srcs) + lands + [carry]]
    res = pl.pallas_call(
        body, name=name,
        out_shape=(sems, sems, *[hbm(a) for a in pinned]),
        in_specs=[_HBM] * (2 * n + 1),
        out_specs=(_SEM, _SEM, *[_HBM] * (2 * n + 1)),
        input_output_aliases={i: 2 + i for i in range(2 * n + 1)},
        compiler_params=pltpu.CompilerParams(has_side_effects=_DATAFLOW),
    )(*pinned)
    return res[:2], list(res[2:2 + n]), list(res[2 + n:2 + 2 * n]), res[2 + 2 * n]


def _chips_wait(started, after, name):
    sems, srcs, lands, _ = started
    n = len(srcs)

    def body(*refs):
        src_refs, land_refs = refs[:n], refs[n:2 * n]
        send_s, recv_s = refs[2 * n:2 * n + 2]
        for i in range(n):
            for j in range(3):
                cp = _chips_copy(i, j, src_refs, land_refs, send_s, recv_s)
                cp.wait_send()
                cp.wait_recv()

    hbm = lambda a: pltpu.HBM(a.shape, a.dtype)
    res = pl.pallas_call(
        body, name=name,
        out_shape=tuple(hbm(a) for a in srcs + lands),
        in_specs=[_HBM] * (2 * n) + [_SEM, _SEM, pl.BlockSpec(memory_space=pl.ANY)],
        out_specs=tuple([_HBM] * (2 * n)),
        input_output_aliases={i: i for i in range(2 * n)},
        compiler_params=pltpu.CompilerParams(has_side_effects=_DATAFLOW),
    )(*srcs, *lands, *sems, after)
    return list(res[n:])


def _sum8(landing, name):
    _, r, c = landing.shape
    tc = _col_tile(c)

    def body(a_ref, o_ref):
        tot = a_ref[0].astype(F32)
        for k in range(1, N_DEV):
            tot = tot + a_ref[k].astype(F32)
        o_ref[...] = tot

    return pl.pallas_call(
        body,
        out_shape=jax.ShapeDtypeStruct((r, c), F32),
        grid=(c // tc,),
        in_specs=[pl.BlockSpec((N_DEV, r, tc), lambda j: (0, 0, j))],
        out_specs=pl.BlockSpec((r, tc), lambda j: (0, j)),
        compiler_params=pltpu.CompilerParams(dimension_semantics=("parallel",)),
        name=name,
    )(landing)


def _adamw_math(w, g, m, v):
    m = ADAM_B1 * m + (1.0 - ADAM_B1) * g
    v = ADAM_B2 * v + (1.0 - ADAM_B2) * (g * g)
    m_hat = m / (1.0 - ADAM_B1 ** ADAM_STEP)
    v_hat = v / (1.0 - ADAM_B2 ** ADAM_STEP)
    delta = -ADAM_LR * (m_hat / (jnp.sqrt(v_hat) + ADAM_EPS) + ADAM_WD * w)
    return delta, m, v


def _adamw(w, m, v, g, name):
    r, c = w.shape

    def fn(rows, consts):
        return list(_adamw_math(*rows)), []

    return _rowwise(fn, [w, g, m, v], [], [(c, F32)] * 3, tm=r if r <= 512 else 256, name=name)


def _adamw_sum8(w, m, v, landing, name):
    r, c = w.shape
    tc = _col_tile(c)

    def body(w_ref, m_ref, v_ref, a_ref, g_ref, d_ref, m2_ref, v2_ref):
        g = a_ref[0].astype(F32)
        for k in range(1, N_DEV):
            g = g + a_ref[k].astype(F32)
        delta, m2, v2 = _adamw_math(w_ref[...], g, m_ref[...], v_ref[...])
        g_ref[...] = g
        d_ref[...] = delta
        m2_ref[...] = m2
        v2_ref[...] = v2

    blk = pl.BlockSpec((r, tc), lambda j: (0, j))
    return pl.pallas_call(
        body,
        out_shape=[jax.ShapeDtypeStruct((r, c), F32)] * 4,
        grid=(c // tc,),
        in_specs=[blk, blk, blk, pl.BlockSpec((N_DEV, r, tc), lambda j: (0, 0, j))],
        out_specs=[blk] * 4,
        compiler_params=pltpu.CompilerParams(dimension_semantics=("parallel",)),
        name=name,
    )(w, m, v, landing)


def _adamw_parts(w, m, v, own, others, chip, name):
    r, _, c = w.shape
    tc = _col_tile(c)

    def body(q_ref, w_ref, m_ref, v_ref, a_ref, b_ref, g_ref, d_ref, m2_ref, v2_ref):
        g = ((a_ref[...] + b_ref[0].astype(F32)) + b_ref[1].astype(F32)) + b_ref[2].astype(F32)
        delta, m2, v2 = _adamw_math(w_ref[...], g, m_ref[...], v_ref[...])
        g_ref[...] = g
        d_ref[...] = delta
        m2_ref[...] = m2
        v2_ref[...] = v2

    row = pl.BlockSpec((r, None, tc), lambda j, q: (0, 0, j))
    return pl.pallas_call(
        body,
        out_shape=[jax.ShapeDtypeStruct((r, 1, c), F32)] * 4,
        grid_spec=pltpu.PrefetchScalarGridSpec(
            num_scalar_prefetch=1, grid=(c // tc,),
            in_specs=[row, row, row, pl.BlockSpec((None, r, tc), lambda j, q: (q[0], 0, j)),
                      pl.BlockSpec((3, r, tc), lambda j, q: (0, 0, j))],
            out_specs=[row] * 4),
        compiler_params=pltpu.CompilerParams(dimension_semantics=("parallel",)),
        name=name,
    )(chip, w, m, v, own, others)


def _adamw_small(gathered, params):
    ns = len(_SMALL)

    def body(*refs):
        g_ref, p_refs, o_refs = refs[0], refs[1:1 + 3 * ns], refs[1 + 3 * ns:]
        tot = g_ref[0]
        for k in range(1, N_DEV):
            tot = tot + g_ref[k]
        for i, name in enumerate(_SMALL):
            row, lane0, lanes = _SMALL_SLOTS[name]
            g = tot[row:row + 1, lane0:lane0 + lanes]
            w_, m_, v_ = (p_refs[3 * i + j][...] for j in range(3))
            delta, m2, v2 = _adamw_math(w_, g, m_, v_)
            for j, val in enumerate((g, delta, m2, v2)):
                o_refs[4 * i + j][...] = val
        o_refs[4 * ns][...] = tot[_LOSS_ROW:_LOSS_ROW + 1, 0:LANES]
        o_refs[4 * ns + 1][...] = tot[_CONV_ROW0:_CONV_ROW0 + _CONV_ROWS, :]

    out_shape = [jax.ShapeDtypeStruct(w.shape, F32) for (w, _, _) in params for _ in range(4)]
    out_shape += [jax.ShapeDtypeStruct((1, LANES), F32), jax.ShapeDtypeStruct((_CONV_ROWS, FLAT_COLS), F32)]
    flat = [a for wmv in params for a in wmv]
    return pl.pallas_call(body, out_shape=out_shape, name="adamw_small")(gathered, *flat)


def kernel(x, p, positions, w_in, conv_w, dn_a_log, dn_dt_bias, dn_norm_w, q_norm_w, w_uq, kv_norm_w, w_uk, w_uv, w_br_dn, w_br_mla, w_o, ln1_g, ln1_b, w_ffn_in, w_ffn_out, w_ple, w_ple_gate, ln2_g, ln2_b, loss_target, m_w_in, m_conv_w, m_dn_a_log, m_dn_dt_bias, m_dn_norm_w, m_q_norm_w, m_w_uq, m_kv_norm_w, m_w_uk, m_w_uv, m_w_br_dn, m_w_br_mla, m_w_o, m_ln1_g, m_ln1_b, m_w_ffn_in, m_w_ffn_out, m_w_ple, m_w_ple_gate, m_ln2_g, m_ln2_b, v_w_in, v_conv_w, v_dn_a_log, v_dn_dt_bias, v_dn_norm_w, v_q_norm_w, v_w_uq, v_kv_norm_w, v_w_uk, v_w_uv, v_w_br_dn, v_w_br_mla, v_w_o, v_ln1_g, v_ln1_b, v_w_ffn_in, v_w_ffn_out, v_w_ple, v_w_ple_gate, v_ln2_g, v_ln2_b):
    args = dict(locals())
    wts = {n: args[n] for n in _ORDER}
    mom1 = {n: args["m_" + n] for n in _ORDER}
    mom2 = {n: args["v_" + n] for n in _ORDER}
    big_names = [n for n, _ in _BIG]
    shard_shapes = {n: wts[n].shape[1:] for n in big_names}
    c_idx = lax.axis_index("c")
    q_idx = 2 * lax.axis_index("x") + lax.axis_index("y")
    parity, chip = c_idx.reshape(1).astype(jnp.int32), q_idx.reshape(1).astype(jnp.int32)

    stored = {n: _to_stored(n, wts[n][0]).astype(BF16) for n in big_names}
    first = _all_gather([stored["w_in"], conv_w[0]], "ag_first")
    group_names = {grp: [n for n, _ in pairs] for grp, pairs in _GROUP_GRADS.items()}
    carry, gathers = first[0], {}
    for grp in ("mix", "ffn"):
        gathers[grp] = _spread_start([stored[n] for n in group_names[grp]], carry, False, "ag_start_" + grp)
        carry = gathers[grp][3]
    conv_full = jnp.moveaxis(first[1], 0, 1).reshape(conv_w.shape[1], -1)
    small_w = {n: wts[n].astype(F32) for n in _SMALL}
    w = _first_weights(carry.reshape(-1, D_MODEL), conv_full, small_w)

    def late_weights(grp, after):
        got = _spread_wait(gathers[grp], after, False, "ag_wait_" + grp)
        return _late_weights(grp, {n: t.reshape(-1, t.shape[-1]) for n, t in zip(group_names[grp], got)})

    started = {}

    def emit(group, g, carry):
        if group == "w_in":
            rows, cols = _stored_shape("w_in", shard_shapes["w_in"])
            packed, from_sibling = _pack_exchange(_w_in_grad_parts(g), "rs_pack_sibling")
            own, own_bf = _chip_sum(packed.reshape(N_DEV, rows, 1, cols), from_sibling, parity, "rs_sum_w_in")
            started["w_in"] = (own, _chips_start([own_bf], carry, "rs_chips_start"))
            return started["w_in"][1][3]
        if group == "small":
            block = _pack_small_grads(_small_grads(g), g["loss"])
            started["small"] = _spread_start([block], carry, False, "ag_start_small")
            return started["small"][3]
        grads = _group_grads(group, g)
        srcs = [grads[n].reshape((N_DEV,) + _stored_shape(n, shard_shapes[n])) for n in grads]
        started[group] = (list(grads), _spread_start(srcs, carry, True, "rs_start_" + group))
        return started[group][1][3]

    s_dim = x.shape[1]
    loss, dx, g = _local_step(x[0], p[0, 0], positions.reshape(s_dim, 1).astype(F32), loss_target[0], w,
                              late_weights, emit)
    own, chips_started = started.pop("w_in")
    small_started = started.pop("small")

    out_g, out_d, out_m, out_v = {}, {}, {}, {}

    def update(n, grad, shp):
        flat2 = (shp[0], int(np.prod(shp[1:])))
        d, m2, v2 = _adamw(wts[n][0].reshape(flat2), mom1[n][0].reshape(flat2), mom2[n][0].reshape(flat2),
                           grad.reshape(flat2), "adamw_" + n)
        out_g[n], out_d[n], out_m[n], out_v[n] = grad, d.reshape(shp), m2.reshape(shp), v2.reshape(shp)

    for group, (names, st) in started.items():
        for n, landing in zip(names, _spread_wait(st, dx, True, "rs_wait_" + group)):
            shp = shard_shapes[n]
            if _BIG_AXIS[n] == 0 or shp[-1] % LANES:
                res = _adamw_sum8(_to_stored(n, wts[n][0]), _to_stored(n, mom1[n][0]), _to_stored(n, mom2[n][0]),
                                  landing, "adamw_" + n)
                out_g[n], out_d[n], out_m[n], out_v[n] = (_from_stored(n, t, shp) for t in res)
                last = res[3]
            else:
                update(n, _from_stored(n, _sum8(landing, "rs_total_" + n), shp), shp)

    from_chips = _chips_wait(chips_started, last, "rs_chips_wait")[0]
    g_small = _spread_wait(small_started, last, False, "ag_wait_small")[0]
    rows_first = lambda a: jnp.transpose(a, (2, 0, 1))
    res = _adamw_parts(rows_first(wts["w_in"]), rows_first(mom1["w_in"]), rows_first(mom2["w_in"]), own, from_chips,
                       chip, "adamw_w_in")
    out_g["w_in"], out_d["w_in"], out_m["w_in"], out_v["w_in"] = (jnp.transpose(t, (1, 2, 0))[0] for t in res)

    res = _adamw_small(g_small, [(wts[n], mom1[n], mom2[n]) for n in _SMALL])
    for i, n in enumerate(_SMALL):
        out_g[n], out_d[n], out_m[n], out_v[n] = res[4 * i:4 * i + 4]
    loss_out = res[4 * len(_SMALL)][0, 0]
    conv_shape = conv_w.shape[1:]
    conv_g = lax.dynamic_slice(res[-1].reshape(conv_shape[0], -1), (0, (2 * q_idx + c_idx) * conv_shape[1]),
                               conv_shape)
    update("conv_w", conv_g, conv_shape)

    expand = lambda d, n: d[n] if n in _SMALL else d[n][None]
    return (loss_out, dx[None], *[expand(out_g, n) for n in _ORDER], *[expand(out_d, n) for n in _ORDER],
            *[expand(out_m, n) for n in _ORDER], *[expand(out_v, n) for n in _ORDER])
```

```python
import functools

import numpy as np
import jax
import jax.numpy as jnp
from jax import lax
from jax.experimental import pallas as pl
from jax.experimental.pallas import tpu as pltpu

F32 = jnp.float32
BF16 = jnp.bfloat16

D_MODEL = 1024
N_HEADS = 8
HEAD = 128
CHUNK = 64
GROUP = 256
ROPE = 64
Q_LORA = 384
KV_LORA = 256
FFN_HIDDEN = 2816
PLE_DIM = 256
ROPE_BASE = 10000.0
ALPHA = 2.0 ** 0.25
SCALE = float((HEAD + ROPE) ** -0.5)
NEG_BIG = -1e30
EPS_RMS = 1e-6
EPS_LN = 1e-5

ADAM_LR = 0.001
ADAM_B1 = 0.9
ADAM_B2 = 0.999
ADAM_EPS = 1e-08
ADAM_WD = 0.01
ADAM_STEP = 10

N_DEV = 8
LANES = 128
FLAT_COLS = 1024

WB_CQ, WB_CKV, WB_KR, WB_BA, WB_COLS = 0, 512, 768, 896, 1024

HIGHEST = lax.Precision.HIGHEST

NN = (((1,), (0,)), ((), ()))
TN = (((0,), (0,)), ((), ()))
NT = (((1,), (1,)), ((), ()))


def _dot(a, b, dims=NN):
    return lax.dot_general(a.astype(BF16), b.astype(BF16), dims, preferred_element_type=F32)


def _dot32(a, b, dims=NN):
    return lax.dot_general(a, b, dims, precision=HIGHEST, preferred_element_type=F32)


def _sig(x):
    return 1.0 / (1.0 + jnp.exp(-x))


MM_TILE = 1536


def _pick_wide(n):
    if n <= MM_TILE:
        return n
    return max(t for t in range(LANES, MM_TILE + 1, LANES) if n % t == 0)


def _split_bf16(a):
    hi = a.astype(BF16)
    return hi, (a - hi.astype(F32)).astype(BF16)


def _dot3(a, b, dims=NN):
    ah, al = a if isinstance(a, tuple) else _split_bf16(a)
    bh, bl = b if isinstance(b, tuple) else _split_bf16(b)
    d = lambda p, q: lax.dot_general(p, q, dims, preferred_element_type=F32)
    return d(ah, bh) + (d(ah, bl) + d(al, bh))


def _mm(a, b, *, ta=False, tb=False, add=(), out_dtype=F32, name):
    if ta:
        k_dim, m_dim = a.shape
    else:
        m_dim, k_dim = a.shape
    if tb:
        n_dim, k2 = b.shape
    else:
        k2, n_dim = b.shape
    assert k_dim == k2, (a.shape, b.shape, ta, tb)
    tm = _pick_wide(m_dim)
    tn = _pick_wide(n_dim)
    tk = _pick_wide(k_dim)
    nk = k_dim // tk
    n_add = len(add)
    dims = TN if ta else (NT if tb else NN)
    assert not (ta and tb)

    def body(a_ref, b_ref, *rest):
        add_refs = rest[:n_add]
        o_ref = rest[n_add]
        acc = rest[n_add + 1]
        k = pl.program_id(2)

        @pl.when(k == 0)
        def _():
            acc[...] = jnp.zeros_like(acc)

        acc[...] += _dot(a_ref[...], b_ref[...], dims)

        @pl.when(k == nk - 1)
        def _():
            r = acc[...]
            for ar in add_refs:
                r = r + ar[...].astype(F32)
            o_ref[...] = r.astype(o_ref.dtype)

    a_spec = pl.BlockSpec((tk, tm), lambda i, j, k: (k, i)) if ta else pl.BlockSpec((tm, tk), lambda i, j, k: (i, k))
    b_spec = pl.BlockSpec((tn, tk), lambda i, j, k: (j, k)) if tb else pl.BlockSpec((tk, tn), lambda i, j, k: (k, j))
    o_spec = pl.BlockSpec((tm, tn), lambda i, j, k: (i, j))
    return pl.pallas_call(
        body,
        out_shape=jax.ShapeDtypeStruct((m_dim, n_dim), out_dtype),
        grid=(m_dim // tm, n_dim // tn, nk),
        in_specs=[a_spec, b_spec] + [o_spec] * n_add,
        out_specs=o_spec,
        scratch_shapes=[pltpu.VMEM((tm, tn), F32)],
        compiler_params=pltpu.CompilerParams(dimension_semantics=("parallel", "parallel", "arbitrary")),
        name=name,
    )(a, b, *add)


def _mm_resident(a, b, *, tb=False, add=(), out_dtype=F32, name):
    m_dim, k_dim = a.shape
    n_dim, k2 = b.shape if tb else b.shape[::-1]
    assert k2 == k_dim
    tm = min(DX_ROWS, m_dim)
    dims = NT if tb else NN

    def body(a_ref, b_ref, *rest):
        r = _dot(a_ref[...], b_ref[...], dims)
        for ar in rest[:-1]:
            r = r + ar[...]
        rest[-1][...] = r.astype(out_dtype)

    o_spec = pl.BlockSpec((tm, n_dim), lambda i: (i, 0))
    return pl.pallas_call(
        body,
        out_shape=jax.ShapeDtypeStruct((m_dim, n_dim), out_dtype),
        grid=(m_dim // tm,),
        in_specs=[pl.BlockSpec((tm, k_dim), lambda i: (i, 0)),
                  pl.BlockSpec(b.shape, lambda i: (0, 0), pipeline_mode=pl.Buffered(1))] + [o_spec] * len(add),
        out_specs=o_spec,
        compiler_params=pltpu.CompilerParams(dimension_semantics=("parallel",)),
        name=name,
    )(a, b, *add)


def _rowwise(fn, rows, consts, outs, accs=(), *, tm=256, name):
    rows = [r if isinstance(r, tuple) else (r, 0, r.shape[1]) for r in rows]
    s_dim = rows[0][0].shape[0]
    tm = min(tm, s_dim)
    assert s_dim % tm == 0 and all(arr.shape[0] == s_dim for arr, _, _ in rows)
    specs = [pl.BlockSpec((tm, width), functools.partial(lambda i, cb: (i, cb), cb=cb)) for _, cb, width in rows]
    args = [arr for arr, _, _ in rows]
    for c in consts:
        specs.append(pl.BlockSpec(c.shape, lambda i: (0, 0)))
        args.append(c)
    nr, nc, no = len(rows), len(consts), len(outs)
    flipped = [len(o) == 3 for o in outs]
    out_shape = [jax.ShapeDtypeStruct((o[0], s_dim) if t else (s_dim, o[0]), o[1]) for o, t in zip(outs, flipped)]
    out_specs = [pl.BlockSpec((o[0], tm), lambda i: (0, i)) if t else pl.BlockSpec((tm, o[0]), lambda i: (i, 0))
                 for o, t in zip(outs, flipped)]
    out_shape += [jax.ShapeDtypeStruct(sh, F32) for sh in accs]
    out_specs += [pl.BlockSpec(sh, lambda i: (0, 0)) for sh in accs]

    def body(*refs):
        r = [x[...].astype(F32) if x.dtype == BF16 else x[...] for x in refs[:nr]]
        c = [x[...] for x in refs[nr:nr + nc]]
        o_refs = refs[nr + nc:nr + nc + no]
        a_refs = refs[nr + nc + no:]
        o_vals, a_vals = fn(r, c)
        for ref, v, t in zip(o_refs, o_vals, flipped, strict=True):
            ref[...] = (jnp.transpose(v.astype(F32)) if t else v).astype(ref.dtype)
        if a_refs:
            @pl.when(pl.program_id(0) == 0)
            def _():
                for ref in a_refs:
                    ref[...] = jnp.zeros_like(ref)

            for ref, v in zip(a_refs, a_vals, strict=True):
                ref[...] += v

    res = pl.pallas_call(
        body,
        out_shape=out_shape,
        grid=(s_dim // tm,),
        in_specs=specs,
        out_specs=out_specs,
        compiler_params=pltpu.CompilerParams(dimension_semantics=("arbitrary" if accs else "parallel",)),
        name=name,
    )(*args)
    return res


def _colsum(v):
    return jnp.sum(v, axis=0, keepdims=True)


def _rowsum(v):
    return jnp.sum(v, axis=1, keepdims=True)


def _rowmean(v):
    return jnp.mean(v, axis=1, keepdims=True)


def _silu_grad(x):
    s = _sig(x)
    return s * (1.0 + x * (1.0 - s))


def _conv_taps(x, w, width=4):
    row = lax.broadcasted_iota(jnp.int32, x.shape, 0)
    c = x * w[width - 1:width, :]
    for s in range(1, width):
        c = c + jnp.where(row >= s, pltpu.roll(x, s, 0), 0.0) * w[width - 1 - s:width - s, :]
    return c


def _conv_fwd(proj_a, conv_w):
    s_dim = proj_a.shape[0]
    n_blk = 3 * N_HEADS

    def body(x_ref, w_ref, o_ref):
        j = pl.program_id(0)
        c = _conv_taps(x_ref[...], w_ref[...])
        y = c * _sig(c)
        r = lax.rsqrt(_rowsum(y * y) + EPS_RMS)
        fac = jnp.where(j < N_HEADS, r * (HEAD ** -0.5), jnp.where(j < 2 * N_HEADS, r, 1.0))
        o_ref[...] = y * fac

    return pl.pallas_call(
        body,
        out_shape=jax.ShapeDtypeStruct((s_dim, n_blk * HEAD), F32),
        grid=(n_blk,),
        in_specs=[pl.BlockSpec((s_dim, HEAD), lambda j: (0, j)), pl.BlockSpec((4, HEAD), lambda j: (0, j))],
        out_specs=pl.BlockSpec((s_dim, HEAD), lambda j: (0, j)),
        compiler_params=pltpu.CompilerParams(dimension_semantics=("parallel",)),
        name="conv_fwd",
    )(proj_a, conv_w)


def _conv_bwd(proj_a, conv_w, dq, dk, dv):
    s_dim = proj_a.shape[0]
    n_blk = 3 * N_HEADS

    def body(x_ref, w_ref, dq_ref, dk_ref, dv_ref, dx_ref, dw_ref):
        j = pl.program_id(0)
        x = x_ref[...]
        w = w_ref[...]
        do = jnp.where(j < N_HEADS, dq_ref[...], jnp.where(j < 2 * N_HEADS, dk_ref[...], dv_ref[...]))
        c = _conv_taps(x, w)
        sg = _sig(c)
        y = c * sg
        r = lax.rsqrt(_rowsum(y * y) + EPS_RMS)
        sc = jnp.where(j < N_HEADS, HEAD ** -0.5, 1.0)
        dy_n = sc * (r * do - y * (r * r * r) * _rowsum(do * y))
        dy = jnp.where(j < 2 * N_HEADS, dy_n, do)
        dc = dy * (sg * (1.0 + c * (1.0 - sg)))
        row = lax.broadcasted_iota(jnp.int32, x.shape, 0)
        dx = dc * w[3:4, :]
        dw_ref[3:4, :] = _colsum(dc * x)
        for s in range(1, 4):
            dx = dx + jnp.where(row < s_dim - s, pltpu.roll(dc, s_dim - s, 0), 0.0) * w[3 - s:4 - s, :]
            xs = jnp.where(row >= s, pltpu.roll(x, s, 0), 0.0)
            dw_ref[3 - s:4 - s, :] = _colsum(dc * xs)
        dx_ref[...] = dx.astype(dx_ref.dtype)

    hd = N_HEADS - 1
    return pl.pallas_call(
        body,
        out_shape=[jax.ShapeDtypeStruct((s_dim, n_blk * HEAD), BF16), jax.ShapeDtypeStruct((4, n_blk * HEAD), F32)],
        grid=(n_blk,),
        in_specs=[
            pl.BlockSpec((s_dim, HEAD), lambda j: (0, j)),
            pl.BlockSpec((4, HEAD), lambda j: (0, j)),
            pl.BlockSpec((s_dim, HEAD), lambda j: (0, jnp.minimum(j, hd))),
            pl.BlockSpec((s_dim, HEAD), lambda j: (0, jnp.clip(j - N_HEADS, 0, hd))),
            pl.BlockSpec((s_dim, HEAD), lambda j: (0, jnp.clip(j - 2 * N_HEADS, 0, hd))),
        ],
        out_specs=[pl.BlockSpec((s_dim, HEAD), lambda j: (0, j)), pl.BlockSpec((4, HEAD), lambda j: (0, j))],
        compiler_params=pltpu.CompilerParams(dimension_semantics=("parallel",)),
        name="conv_bwd",
    )(proj_a, conv_w, dq, dk, dv)


def _chunk_tri(n):
    r = np.arange(n)
    m = ((r[:, None] // CHUNK) == (r[None, :] // CHUNK)) & (r[:, None] >= r[None, :])
    m = m.astype(np.float32)
    return jnp.asarray(m), jnp.asarray(m.T)


def _softplus(z):
    return jnp.maximum(z, 0.0) + jnp.log(1.0 + jnp.exp(-jnp.abs(z)))


def _gates_fwd(proj_b, alog, dtb):
    tm = min(GROUP, proj_b.shape[0])
    tri, _ = _chunk_tri(tm)

    def fn(r, c):
        b = r[0]
        a = pltpu.roll(b, LANES - N_HEADS, 1)
        alog_, dtb_, tri_ = c
        g = -jnp.exp(alog_) * _softplus(a + dtb_)
        return [_sig(b), _dot32(tri_, g)], []

    return _rowwise(fn, [(proj_b, WB_BA // LANES, LANES)], [alog, dtb, tri],
                    [(LANES, F32), (LANES, F32)], tm=tm, name="gates_fwd")


def _gates_bwd(proj_b, alog, dtb, gc, d_beta, d_gc, d_egl_rows):
    tm = min(GROUP, proj_b.shape[0])
    _, tri_t = _chunk_tri(tm)

    def fn(r, c):
        b, gc_, d_beta_, d_gc_, d_egl_ = r
        a = pltpu.roll(b, LANES - N_HEADS, 1)
        alog_, dtb_, tri_t_ = c
        z = a + dtb_
        ea = jnp.exp(alog_)
        g = -ea * _softplus(z)
        dg = _dot32(tri_t_, d_gc_ + d_egl_ * jnp.exp(gc_))
        d_a = dg * (-ea) * _sig(z)
        beta = _sig(b)
        d_ba = d_beta_ * beta * (1.0 - beta) + pltpu.roll(d_a, N_HEADS, 1)
        return [d_ba], [_colsum(dg * g), _colsum(d_a)]

    return _rowwise(fn, [(proj_b, WB_BA // LANES, LANES), gc, d_beta, d_gc, d_egl_rows],
                    [alog, dtb, tri_t], [(LANES, BF16)], accs=[(1, LANES), (1, LANES)], tm=tm,
                    name="gates_bwd")


def _group_masks(n):
    r = lax.broadcasted_iota(jnp.int32, (n, n), 0)
    c = lax.broadcasted_iota(jnp.int32, (n, n), 1)
    same = (r // CHUNK) == (c // CHUNK)
    below, s = [], 2
    while s < CHUNK:
        below.append(jnp.logical_and((r // (2 * s)) == (c // (2 * s)),
                                     jnp.logical_and((r // s) % 2 == 1, (c // s) % 2 == 0)))
        s *= 2
    return dict(same=same, tril=jnp.logical_and(same, r >= c), strict=jnp.logical_and(same, r > c),
                last=c == (r // CHUNK) * CHUNK + (CHUNK - 1), eye=r == c, pair=(r // 2) == (c // 2), below=below)


def _inv_unit_lower(l_mats, mk):
    eye_f = mk["eye"].astype(F32)
    ts = [eye_f - jnp.where(mk["pair"], l_mat, 0.0) for l_mat in l_mats]
    for below in mk["below"]:
        halves = [_split_bf16(t) for t in ts]
        mids = [_dot3(h, jnp.where(below, l_mat, 0.0)) for h, l_mat in zip(halves, l_mats)]
        ts = [t - _dot3(m, h) for t, m, h in zip(ts, mids, halves)]
    return ts


def _unfold_blocks(folded, mask):
    n = folded.shape[0]
    return jnp.where(mask, jnp.concatenate([folded] * (n // CHUNK), axis=1), 0.0)


def _head_cols(beta, gc, gc_t, h):
    lane = lax.broadcasted_iota(jnp.int32, beta.shape, 1)
    sub = lax.broadcasted_iota(jnp.int32, gc_t.shape, 0)
    bcol = _rowsum(jnp.where(lane == h, beta, 0.0))
    gcol = _rowsum(jnp.where(lane == h, gc, 0.0))
    grow = _colsum(jnp.where(sub == h, gc_t, 0.0))
    return bcol, gcol, grow


def _prep_common(q, k, bcol, gcol, grow, mk, t_folded=None):
    n = q.shape[0]
    tril = mk["tril"]
    decay = jnp.where(tril, jnp.exp(jnp.where(tril, gcol - grow, 0.0)), 0.0)
    glast = _rowsum(jnp.where(mk["last"], jnp.broadcast_to(grow, (n, n)), 0.0))
    e = jnp.exp(gcol)
    ekt = jnp.exp(glast - gcol)
    kb = k * bcol
    kk = _dot(kb, k, NT)
    qk = _dot(q, k, NT)
    p = dict(decay=decay, e=e, ekt=ekt, kb=kb, kk=kk, qk=qk)
    if t_folded is not None:
        p["t"] = _unfold_blocks(t_folded, mk["same"])
    return p


GROUPS_PER_STEP = 4
SCAN_CHUNKS_PER_STEP = 4


def _fold_blocks(m):
    n = m.shape[0]
    out = m[:, 0:CHUNK]
    for b in range(1, n // CHUNK):
        out = out + m[:, b * CHUNK:(b + 1) * CHUNK]
    return out


def _gdr_prep_fwd(qkvn, beta, gc, gc_t):
    s_dim = qkvn.shape[0]
    tg = min(GROUP, s_dim)
    n_sub = min(GROUPS_PER_STEP, s_dim // tg)
    tb = tg * n_sub

    def body(q_ref, k_ref, v_ref, b_ref, g_ref, gt_ref, u_ref, w_ref, qd_ref, kt_ref, a_ref, t_ref):
        h = pl.program_id(0)
        mk = _group_masks(tg)
        parts = []
        for s in range(n_sub):
            rows = slice(s * tg, (s + 1) * tg)
            q, k, v = q_ref[rows, :], k_ref[rows, :], v_ref[rows, :]
            bcol, gcol, grow = _head_cols(b_ref[rows, :], g_ref[rows, :], gt_ref[:, rows], h)
            p = _prep_common(q, k, bcol, gcol, grow, mk)
            qd_ref[rows, :] = q * p["e"]
            kt_ref[rows, :] = k * p["ekt"]
            a_ref[rows, :] = _fold_blocks(jnp.where(mk["tril"], p["qk"] * p["decay"], 0.0))
            parts.append((rows, v * bcol, p["kb"] * p["e"], jnp.where(mk["strict"], p["kk"] * p["decay"], 0.0)))
        t_mats = _inv_unit_lower([part[3] for part in parts], mk)
        for (rows, vb, kbe, _), t_mat in zip(parts, t_mats):
            u_ref[rows, :] = _dot(t_mat, vb)
            w_ref[rows, :] = _dot(t_mat, kbe)
            t_ref[rows, :] = _fold_blocks(t_mat)

    row = lambda off: pl.BlockSpec((tb, HEAD), functools.partial(lambda h, m, off: (m, h + off), off=off))
    full = pl.BlockSpec((tb, LANES), lambda h, m: (m, 0))
    o_spec = pl.BlockSpec((tb, HEAD), lambda h, m: (m, h))
    a_spec = pl.BlockSpec((None, tb, CHUNK), lambda h, m: (h, m, 0))
    wide = jax.ShapeDtypeStruct((s_dim, N_HEADS * HEAD), F32)
    folded = jax.ShapeDtypeStruct((N_HEADS, s_dim, CHUNK), F32)
    return pl.pallas_call(
        body,
        out_shape=[wide, wide, wide, wide, folded, folded],
        grid=(N_HEADS, s_dim // tb),
        in_specs=[row(0), row(N_HEADS), row(2 * N_HEADS), full, full, pl.BlockSpec((8, tb), lambda h, m: (0, m))],
        out_specs=[o_spec, o_spec, o_spec, o_spec, a_spec, a_spec],
        compiler_params=pltpu.CompilerParams(dimension_semantics=("parallel", "parallel")),
        name="gdr_prep_fwd",
    )(qkvn, qkvn, qkvn, beta, gc, gc_t)


def _gdr_prep_bwd(qkvn, beta, gc, gc_t, t_fold, u, w, du, dw, dqd, dkt, d_a):
    s_dim = qkvn.shape[0]
    tg = min(GROUP, s_dim)
    n_sub = min(GROUPS_PER_STEP, s_dim // tg)
    tb = tg * n_sub

    def body(q_ref, k_ref, v_ref, b_ref, g_ref, gt_ref, t_ref, u_ref, w_ref, du_ref, dw_ref, dqd_ref, dkt_ref,
             da_ref, dq_ref, dk_ref, dv_ref, db_ref, dg_ref):
        h = pl.program_id(1)

        @pl.when(h == 0)
        def _():
            db_ref[...] = jnp.zeros_like(db_ref)
            dg_ref[...] = jnp.zeros_like(dg_ref)

        mk = _group_masks(tg)
        lane = lax.broadcasted_iota(jnp.int32, (tg, LANES), 1)
        for s in range(n_sub):
            rows = slice(s * tg, (s + 1) * tg)
            q, k, v = q_ref[rows, :], k_ref[rows, :], v_ref[rows, :]
            bcol, gcol, grow = _head_cols(b_ref[rows, :], g_ref[rows, :], gt_ref[:, rows], h)
            p = _prep_common(q, k, bcol, gcol, grow, mk, t_ref[rows, :])
            t_mat, decay, e, ekt, kb = p["t"], p["decay"], p["e"], p["ekt"], p["kb"]
            du_, dw_, dqd_, dkt_ = du_ref[rows, :], dw_ref[rows, :], dqd_ref[rows, :], dkt_ref[rows, :]
            dvb = _dot(t_mat, du_, TN)
            dkbe = _dot(t_mat, dw_, TN)
            d_l = -(_dot(dvb, u_ref[rows, :], NT) + _dot(dkbe, w_ref[rows, :], NT))
            m1 = jnp.where(mk["strict"], d_l, 0.0)
            m2 = _unfold_blocks(da_ref[rows, :], mk["tril"])
            d_kk = m1 * decay
            d_qk = m2 * decay
            d_decay = m1 * p["kk"] + m2 * p["qk"]
            dkb = _dot(d_kk, k) + dkbe * e
            dk = _dot(d_kk, kb, TN) + _dot(d_qk, q, TN) + dkt_ * ekt + dkb * bcol
            dq = _dot(d_qk, k) + dqd_ * e
            d_beta = _rowsum(dkb * k) + _rowsum(dvb * v)
            d_e = _rowsum(dkbe * kb) + _rowsum(dqd_ * q)
            d_ekt = _rowsum(dkt_ * k) * ekt
            d_diff = d_decay * decay
            d_grow = -_colsum(d_diff) + _colsum(jnp.where(mk["last"], jnp.broadcast_to(d_ekt, (tg, tg)), 0.0))
            d_gcol = d_e * e - d_ekt + _rowsum(d_diff)
            d_gcol = d_gcol + _rowsum(jnp.where(mk["eye"], jnp.broadcast_to(d_grow, (tg, tg)), 0.0))
            dq_ref[rows, :] = dq
            dk_ref[rows, :] = dk
            dv_ref[rows, :] = dvb * bcol
            db_ref[rows, :] = jnp.where(lane == h, d_beta, db_ref[rows, :])
            dg_ref[rows, :] = jnp.where(lane == h, d_gcol, dg_ref[rows, :])

    row = lambda off: pl.BlockSpec((tb, HEAD), functools.partial(lambda m, h, off: (m, h + off), off=off))
    full = pl.BlockSpec((tb, LANES), lambda m, h: (m, 0))
    o_spec = pl.BlockSpec((tb, HEAD), lambda m, h: (m, h))
    a_spec = pl.BlockSpec((None, tb, CHUNK), lambda m, h: (h, m, 0))
    wide = jax.ShapeDtypeStruct((s_dim, N_HEADS * HEAD), F32)
    lanes = jax.ShapeDtypeStruct((s_dim, LANES), F32)
    return pl.pallas_call(
        body,
        out_shape=[wide, wide, wide, lanes, lanes],
        grid=(s_dim // tb, N_HEADS),
        in_specs=[row(0), row(N_HEADS), row(2 * N_HEADS), full, full, pl.BlockSpec((8, tb), lambda m, h: (0, m)),
                  a_spec, o_spec, o_spec, o_spec, o_spec, o_spec, o_spec, a_spec],
        out_specs=[o_spec, o_spec, o_spec, full, full],
        compiler_params=pltpu.CompilerParams(dimension_semantics=("parallel", "arbitrary")),
        name="gdr_prep_bwd",
    )(qkvn, qkvn, qkvn, beta, gc, gc_t, t_fold, u, w, du, dw, dqd, dkt, d_a)


def _gdr_scan_fwd(u, w, qd, kt, a_mat, gc):
    s_dim = u.shape[0]
    n_chunks = s_dim // CHUNK
    per = min(SCAN_CHUNKS_PER_STEP, n_chunks)
    tb = per * CHUNK

    def body(u_ref, w_ref, qd_ref, kt_ref, a_ref, g_ref, o_ref, st_ref, state):
        @pl.when(pl.program_id(0) == 0)
        def _():
            state[...] = jnp.zeros_like(state)

        heads = range(N_HEADS)
        cols = [slice(h * HEAD, (h + 1) * HEAD) for h in heads]
        for i in range(per):
            rows = slice(i * CHUNK, (i + 1) * CHUNK)
            egl = jnp.exp(g_ref[(i + 1) * CHUNK - 1:(i + 1) * CHUNK, :])
            s_b = [state[h].astype(BF16) for h in heads]
            for h in heads:
                st_ref[i, h] = state[h]
            ws = [_dot(w_ref[rows, cs], s) for cs, s in zip(cols, s_b)]
            qs = [_dot(qd_ref[rows, cs], s) for cs, s in zip(cols, s_b)]
            vns = [(u_ref[rows, cs] - ws_h).astype(BF16) for cs, ws_h in zip(cols, ws)]
            avs = [_dot(a_ref[h, rows, :], vn) for h, vn in zip(heads, vns)]
            kvs = [_dot(kt_ref[rows, cs], vn, TN) for cs, vn in zip(cols, vns)]
            for h, cs in zip(heads, cols):
                o_ref[rows, cs] = qs[h] + avs[h]
                state[h] = state[h] * egl[:, h:h + 1] + kvs[h]

    wide = pl.BlockSpec((tb, N_HEADS * HEAD), lambda n: (n, 0))
    return pl.pallas_call(
        body,
        out_shape=[jax.ShapeDtypeStruct((s_dim, N_HEADS * HEAD), F32),
                   jax.ShapeDtypeStruct((n_chunks, N_HEADS, HEAD, HEAD), F32)],
        grid=(n_chunks // per,),
        in_specs=[wide, wide, wide, wide, pl.BlockSpec((N_HEADS, tb, CHUNK), lambda n: (0, n, 0)),
                  pl.BlockSpec((tb, LANES), lambda n: (n, 0))],
        out_specs=[wide, pl.BlockSpec((per, N_HEADS, HEAD, HEAD), lambda n: (n, 0, 0, 0))],
        scratch_shapes=[pltpu.VMEM((N_HEADS, HEAD, HEAD), F32)],
        compiler_params=pltpu.CompilerParams(dimension_semantics=("arbitrary",)),
        name="gdr_scan_fwd",
    )(u, w, qd, kt, a_mat, gc)


def _gdr_scan_bwd(u, w, qd, kt, a_mat, gc, states, d_o):
    s_dim = u.shape[0]
    n_chunks = s_dim // CHUNK
    per = min(SCAN_CHUNKS_PER_STEP, n_chunks)
    tb = per * CHUNK
    last = n_chunks // per - 1

    def body(u_ref, w_ref, qd_ref, kt_ref, a_ref, g_ref, st_ref, do_ref,
             du_ref, dw_ref, dqd_ref, dkt_ref, da_ref, de_ref, d_state):
        @pl.when(pl.program_id(0) == 0)
        def _():
            d_state[...] = jnp.zeros_like(d_state)

        heads = range(N_HEADS)
        cols = [slice(h * HEAD, (h + 1) * HEAD) for h in heads]
        for i in reversed(range(per)):
            rows = slice(i * CHUNK, (i + 1) * CHUNK)
            egl = jnp.exp(g_ref[(i + 1) * CHUNK - 1:(i + 1) * CHUNK, :])
            s_b = [st_ref[i, h].astype(BF16) for h in heads]
            ds_b = [d_state[h].astype(BF16) for h in heads]
            dos = [do_ref[rows, cs].astype(BF16) for cs in cols]
            w_b = [w_ref[rows, cs].astype(BF16) for cs in cols]
            ws = [_dot(w_h, s) for w_h, s in zip(w_b, s_b)]
            ados = [_dot(a_ref[h, rows, :], do, TN) for h, do in zip(heads, dos)]
            kds = [_dot(kt_ref[rows, cs], ds) for cs, ds in zip(cols, ds_b)]
            dqds = [_dot(do, s, NT) for do, s in zip(dos, s_b)]
            qdos = [_dot(qd_ref[rows, cs], do, TN) for cs, do in zip(cols, dos)]
            vns = [(u_ref[rows, cs] - ws_h).astype(BF16) for cs, ws_h in zip(cols, ws)]
            dvns = [a + k_ for a, k_ in zip(ados, kds)]
            dvn_b = [d.astype(BF16) for d in dvns]
            das = [_dot(do, vn, NT) for do, vn in zip(dos, vns)]
            dkts = [_dot(vn, ds, NT) for vn, ds in zip(vns, ds_b)]
            dws = [_dot(d, s, NT) for d, s in zip(dvn_b, s_b)]
            wds = [_dot(w_h, d, TN) for w_h, d in zip(w_b, dvn_b)]
            for h, cs in zip(heads, cols):
                ds_n = d_state[h]
                de = jnp.sum(_rowsum(ds_n * st_ref[i, h]), axis=0, keepdims=True)
                de_ref[i, h:h + 1, :] = jnp.broadcast_to(de, (1, LANES))
                dqd_ref[rows, cs] = dqds[h]
                da_ref[h, rows, :] = das[h]
                dkt_ref[rows, cs] = dkts[h]
                du_ref[rows, cs] = dvns[h]
                dw_ref[rows, cs] = -dws[h]
                d_state[h] = ds_n * egl[:, h:h + 1] + qdos[h] - wds[h]

    wide = pl.BlockSpec((tb, N_HEADS * HEAD), lambda n: (last - n, 0))
    a_spec = pl.BlockSpec((N_HEADS, tb, CHUNK), lambda n: (0, last - n, 0))
    wide_shape = jax.ShapeDtypeStruct((s_dim, N_HEADS * HEAD), F32)
    return pl.pallas_call(
        body,
        out_shape=[wide_shape, wide_shape, wide_shape, wide_shape,
                   jax.ShapeDtypeStruct((N_HEADS, s_dim, CHUNK), F32),
                   jax.ShapeDtypeStruct((n_chunks, N_HEADS, LANES), F32)],
        grid=(n_chunks // per,),
        in_specs=[wide, wide, wide, wide, a_spec, pl.BlockSpec((tb, LANES), lambda n: (last - n, 0)),
                  pl.BlockSpec((per, N_HEADS, HEAD, HEAD), lambda n: (last - n, 0, 0, 0)), wide],
        out_specs=[wide, wide, wide, wide, a_spec, pl.BlockSpec((per, N_HEADS, LANES), lambda n: (last - n, 0, 0))],
        scratch_shapes=[pltpu.VMEM((N_HEADS, HEAD, HEAD), F32)],
        compiler_params=pltpu.CompilerParams(dimension_semantics=("arbitrary",)),
        name="gdr_scan_bwd",
    )(u, w, qd, kt, a_mat, gc, states, d_o)


FUSED_ROWS = 512


def _gdr_out_fwd(o_dn, proj_a, dn_w, w_br):
    def fn(r, c):
        o, z = r
        w_, w_br_ = c
        outs = []
        for h in range(N_HEADS):
            cs = slice(h * HEAD, (h + 1) * HEAD)
            oh, zh = o[:, cs], z[:, cs]
            rr = lax.rsqrt(_rowmean(oh * oh) + EPS_RMS)
            outs.append(oh * rr * w_ * (zh * _sig(zh)))
        og = jnp.concatenate(outs, axis=1).astype(BF16)
        return [og, _dot(og, w_br_)], []

    return _rowwise(fn, [o_dn, (proj_a, 3, D_MODEL)], [dn_w, w_br], [(D_MODEL, BF16), (D_MODEL, BF16)],
                    tm=FUSED_ROWS, name="gdr_out_fwd")


def _gdr_out_bwd(o_dn, proj_a, d_y_dn, dn_w, w_br):
    def fn(r, c):
        o, z, dy = r
        w_, w_br_ = c
        dg = _dot(dy, w_br_, NT)
        d_o, d_z = [], []
        d_w = jnp.zeros((1, HEAD), F32)
        for h in range(N_HEADS):
            cs = slice(h * HEAD, (h + 1) * HEAD)
            oh, zh, dgh = o[:, cs], z[:, cs], dg[:, cs]
            rr = lax.rsqrt(_rowmean(oh * oh) + EPS_RMS)
            sz = zh * _sig(zh)
            d_n = dgh * sz
            d_z.append(dgh * (oh * rr * w_) * _silu_grad(zh))
            d_w = d_w + _colsum(d_n * oh * rr)
            gw = d_n * w_
            d_o.append(rr * gw - oh * (rr * rr * rr) * _rowmean(gw * oh))
        return [jnp.concatenate(d_o, axis=1), jnp.concatenate(d_z, axis=1)], [d_w]

    return _rowwise(fn, [o_dn, (proj_a, 3, D_MODEL), d_y_dn], [dn_w, w_br], [(D_MODEL, F32), (D_MODEL, BF16)],
                    accs=[(1, HEAD)], tm=FUSED_ROWS, name="gdr_out_bwd")


def _rms_fwd(x, w):
    r = lax.rsqrt(_rowmean(x * x) + EPS_RMS)
    return x * r * w


def _rms_bwd(x, w, dy):
    r = lax.rsqrt(_rowmean(x * x) + EPS_RMS)
    gw = dy * w
    return r * gw - x * (r * r * r) * _rowmean(gw * x), _colsum(dy * x * r)


def _rope_consts():
    inv = ROPE_BASE ** (-np.arange(0, ROPE, 2, dtype=np.float32) / ROPE)
    t = np.zeros((4, LANES), np.float32)
    t[0, :32] = inv
    t[0, 32:64] = inv
    t[1, :64] = 1.0
    t[2, 32:64] = 1.0
    t[3, :32] = -1.0
    return jnp.asarray(t)


def _rope_tables(pos, consts, width):
    ang = pos * consts[0:1, :]
    cosv, sinv = jnp.cos(ang), jnp.sin(ang)
    reps = width // LANES
    tile = (lambda t: jnp.concatenate([t] * reps, axis=1)) if reps > 1 else (lambda t: t)
    return tile(cosv * consts[1:2, :]), tile(sinv * consts[2:3, :]), tile(sinv * consts[3:4, :])


def _rope_apply(t, tabs):
    cos_t, sin_a, sin_b = tabs
    width = t.shape[1]
    return t * cos_t + pltpu.roll(t, 32, 1) * sin_a + pltpu.roll(t, width - 32, 1) * sin_b


def _rope_transpose(d, tabs):
    cos_t, sin_a, sin_b = tabs
    width = d.shape[1]
    return d * cos_t + pltpu.roll(d * sin_a, width - 32, 1) + pltpu.roll(d * sin_b, 32, 1)


QK_HEAD = 2 * HEAD


def _interleave_heads(a, b):
    parts = []
    for h in range(N_HEADS):
        parts.append(a[:, h * HEAD:(h + 1) * HEAD])
        parts.append(b if b.shape[1] == LANES else b[:, h * LANES:(h + 1) * LANES])
    return jnp.concatenate(parts, axis=1)


def _mla_rows(proj_b):
    return [(proj_b, WB_CQ // Q_LORA, Q_LORA), (proj_b, WB_CKV // KV_LORA, KV_LORA), (proj_b, WB_KR // LANES, LANES)]


def _mla_prep_fwd(proj_b, pos, qn_w, kvn_w, uq, uk, uv):
    def fn(r, c):
        cq, ckv, kr, pos_ = r
        qn_w_, kvn_w_, uq_, uk_, uv_, rope = c
        c_q = _rms_fwd(cq, qn_w_).astype(BF16)
        c_kv = _rms_fwd(ckv, kvn_w_).astype(BF16)
        qf = _dot(c_q, uq_)
        qr = _rope_apply(qf[:, D_MODEL:], _rope_tables(pos_, rope, D_MODEL))
        kr = _rope_apply(kr, _rope_tables(pos_, rope, LANES))
        kc = _interleave_heads(_dot(c_kv, uk_), kr)
        v = _dot(c_kv, uv_)
        return [c_q, c_kv, _interleave_heads(qf[:, :D_MODEL], qr) * SCALE, kc, v, kc, v], []

    wide2 = N_HEADS * QK_HEAD
    return _rowwise(fn, _mla_rows(proj_b) + [pos], [qn_w, kvn_w, uq, uk, uv, _rope_consts()],
                    [(Q_LORA, BF16), (KV_LORA, BF16), (wide2, BF16), (wide2, BF16), (D_MODEL, BF16),
                     (wide2, BF16, "T"), (D_MODEL, BF16, "T")], tm=FUSED_ROWS, name="mla_prep_fwd")


def _mla_prep_bwd(proj_b, pos, d_qc, d_kc, d_v, qn_w, kvn_w, uq, uk, uv):
    def fn(r, c):
        cq, ckv, _, pos_, dq, dk, dv = r
        qn_w_, kvn_w_, uq_, uk_, uv_, rope = c
        even = lambda t: jnp.concatenate([t[:, (2 * h) * LANES:(2 * h + 1) * LANES] for h in range(N_HEADS)], axis=1)
        odd = lambda t: jnp.concatenate([t[:, (2 * h + 1) * LANES:(2 * h + 2) * LANES] for h in range(N_HEADS)], axis=1)
        d_qr_raw = _rope_transpose(odd(dq), _rope_tables(pos_, rope, D_MODEL)) * SCALE
        d_qf = jnp.concatenate([even(dq) * SCALE, d_qr_raw], axis=1).astype(BF16)
        d_kn = even(dk).astype(BF16)
        dkr = dk[:, LANES:2 * LANES]
        for h in range(1, N_HEADS):
            dkr = dkr + dk[:, (2 * h + 1) * LANES:(2 * h + 2) * LANES]
        d_cq, d_qnw = _rms_bwd(cq, qn_w_, _dot(d_qf, uq_, NT))
        d_ckv, d_kvnw = _rms_bwd(ckv, kvn_w_, _dot(d_kn, uk_, NT) + _dot(dv, uv_, NT))
        return [d_qf, d_kn, d_cq, d_ckv, _rope_transpose(dkr, _rope_tables(pos_, rope, LANES))], [d_qnw, d_kvnw]

    return _rowwise(fn, _mla_rows(proj_b) + [pos, d_qc, d_kc, d_v], [qn_w, kvn_w, uq, uk, uv, _rope_consts()],
                    [(2 * D_MODEL, BF16), (D_MODEL, BF16), (Q_LORA, BF16), (KV_LORA, BF16), (LANES, BF16)],
                    accs=[(1, Q_LORA), (1, KV_LORA)], tm=FUSED_ROWS, name="mla_prep_bwd")


def _causal_mask_t(st, key0, query0):
    key = lax.broadcasted_iota(jnp.int32, st.shape, 0) + key0
    query = lax.broadcasted_iota(jnp.int32, st.shape, 1) + query0
    return jnp.where(key <= query, st, NEG_BIG)


def _attn_tiles(s_dim):
    tq = min(512, s_dim)
    n_chains = 2 if s_dim >= 2 * tq else 1
    return tq, n_chains, min(512, s_dim)


def _diagonal_chains(t, tq, n_chains, tk):
    return [(c, (t + 1) * tk - 1 > c * tq) for c in range(n_chains) if t * tk < (c + 1) * tq]


def _attn_fwd(qc, kc, vt):
    s_dim = qc.shape[0]
    tq, n_chains, tk = _attn_tiles(s_dim)
    tqs = tq * n_chains

    def body(q_ref, k_ref, vt_ref, o_ref, lse_ref, m_s, l_s, acc):
        qi = pl.program_id(1)
        m_s[...] = jnp.full_like(m_s, NEG_BIG)
        l_s[...] = jnp.zeros_like(l_s)
        acc[...] = jnp.zeros_like(acc)

        def make_step(chains):
            def step(j, carry):
                ks = pl.multiple_of(j * tk, tk)
                kb, vtb = k_ref[pl.ds(ks, tk), :], vt_ref[:, pl.ds(ks, tk)]
                cols = [slice(c * tq, (c + 1) * tq) for c, _ in chains]
                sts = [_dot(kb, q_ref[cs, :], NT) for cs in cols]
                sts = [_causal_mask_t(st, j * tk, qi * tqs + c * tq) if masked else st
                       for st, (c, masked) in zip(sts, chains)]
                m_prevs = [m_s[:, cs] for cs in cols]
                m_news = [jnp.maximum(mp, jnp.max(st, axis=0, keepdims=True)) for mp, st in zip(m_prevs, sts)]
                alphas = [jnp.exp(mp - mn) for mp, mn in zip(m_prevs, m_news)]
                pts = [jnp.exp(st - mn) for st, mn in zip(sts, m_news)]
                pvs = [_dot(vtb, pt) for pt in pts]
                for cs, mn, al, pt, pv in zip(cols, m_news, alphas, pts, pvs):
                    l_s[:, cs] = al * l_s[:, cs] + _colsum(pt)
                    m_s[:, cs] = mn
                    acc[:, cs] = acc[:, cs] * al + pv
                return carry
            return step

        below = qi * (tqs // tk)
        lax.fori_loop(0, below, make_step([(c, False) for c in range(n_chains)]), 0)
        for t in range(tqs // tk):
            make_step(_diagonal_chains(t, tq, n_chains, tk))(below + t, 0)
        l = l_s[...]
        o_ref[...] = jnp.transpose(acc[...] / l)
        lse_ref[...] = m_s[...] + jnp.log(l)

    return pl.pallas_call(
        body,
        out_shape=[jax.ShapeDtypeStruct((s_dim, N_HEADS * HEAD), F32), jax.ShapeDtypeStruct((N_HEADS, 1, s_dim), F32)],
        grid=(N_HEADS, s_dim // tqs),
        in_specs=[pl.BlockSpec((tqs, QK_HEAD), lambda h, qi: (qi, h)),
                  pl.BlockSpec((s_dim, QK_HEAD), lambda h, qi: (0, h)),
                  pl.BlockSpec((HEAD, s_dim), lambda h, qi: (h, 0))],
        out_specs=[pl.BlockSpec((tqs, HEAD), lambda h, qi: (qi, h)),
                   pl.BlockSpec((None, 1, tqs), lambda h, qi: (h, 0, qi))],
        scratch_shapes=[pltpu.VMEM((1, tqs), F32), pltpu.VMEM((1, tqs), F32), pltpu.VMEM((HEAD, tqs), F32)],
        compiler_params=pltpu.CompilerParams(dimension_semantics=("parallel", "parallel")),
        name="attn_fwd",
    )(qc, kc, vt)


def _attn_bwd(qc, kc, kct, v, o, d_o, lse):
    s_dim = qc.shape[0]
    tq, n_chains, tk = _attn_tiles(s_dim)
    tqs = tq * n_chains

    def body(q_ref, k_ref, kt_ref, v_ref, o_ref, do_ref, lse_ref, dq_ref, dk_ref, dv_ref, dqt_acc, dv_acc):
        qi = pl.program_id(1)

        @pl.when(qi == 0)
        def _():
            dk_ref[...] = jnp.zeros_like(dk_ref)
            dv_acc[...] = jnp.zeros_like(dv_acc)

        dqt_acc[...] = jnp.zeros_like(dqt_acc)
        do_f = do_ref[...]
        do_all = do_f.astype(BF16)
        q_all = q_ref[...]
        lse_row = lse_ref[...]
        delta_row = _dot3(jnp.ones((8, HEAD), F32), o_ref[...] * do_f, NT)[0:1, :]

        def make_step(chains):
            rows = slice(chains[0][0] * tq, (chains[-1][0] + 1) * tq)

            def step(j, carry):
                ks = pl.multiple_of(j * tk, tk)
                kb, vb, ktb = k_ref[pl.ds(ks, tk), :], v_ref[pl.ds(ks, tk), :], kt_ref[:, pl.ds(ks, tk)]
                cols = [slice(c * tq, (c + 1) * tq) for c, _ in chains]
                sts = [_dot(kb, q_all[cs, :], NT) for cs in cols]
                sts = [_causal_mask_t(st, j * tk, qi * tqs + c * tq) if masked else st
                       for st, (c, masked) in zip(sts, chains)]
                dpts = [_dot(vb, do_all[cs, :], NT) for cs in cols]
                pts = [jnp.exp(st - lse_row[:, cs]) for st, cs in zip(sts, cols)]
                dsts = [(pt * (dpt - delta_row[:, cs])).astype(BF16) for pt, dpt, cs in zip(pts, dpts, cols)]
                pts = [pt.astype(BF16) for pt in pts]
                dqs = [_dot(ktb, dst) for dst in dsts]
                for cs, dq in zip(cols, dqs):
                    dqt_acc[:, cs] += dq
                pt_all = jnp.concatenate(pts, axis=1) if len(chains) > 1 else pts[0]
                dst_all = jnp.concatenate(dsts, axis=1) if len(chains) > 1 else dsts[0]
                dk_ref[pl.ds(ks, tk), :] += _dot(dst_all, q_all[rows, :])
                dv_acc[pl.ds(ks, tk), :] += _dot(pt_all, do_all[rows, :])
                return carry
            return step

        below = qi * (tqs // tk)
        lax.fori_loop(0, below, make_step([(c, False) for c in range(n_chains)]), 0)
        for t in range(tqs // tk):
            make_step(_diagonal_chains(t, tq, n_chains, tk))(below + t, 0)
        dq_ref[...] = jnp.transpose(dqt_acc[...])

        @pl.when(qi == s_dim // tqs - 1)
        def _():
            dv_ref[...] = dv_acc[...].astype(dv_ref.dtype)

    q_spec = pl.BlockSpec((tqs, QK_HEAD), lambda h, qi: (qi, h))
    o_spec = pl.BlockSpec((tqs, HEAD), lambda h, qi: (qi, h))
    k_spec = pl.BlockSpec((s_dim, QK_HEAD), lambda h, qi: (0, h))
    v_spec = pl.BlockSpec((s_dim, HEAD), lambda h, qi: (0, h))
    wide2 = jax.ShapeDtypeStruct((s_dim, N_HEADS * QK_HEAD), F32)
    return pl.pallas_call(
        body,
        out_shape=[wide2, wide2, jax.ShapeDtypeStruct((s_dim, N_HEADS * HEAD), BF16)],
        grid=(N_HEADS, s_dim // tqs),
        in_specs=[q_spec, k_spec, pl.BlockSpec((QK_HEAD, s_dim), lambda h, qi: (h, 0)), v_spec, o_spec, o_spec,
                  pl.BlockSpec((None, 1, tqs), lambda h, qi: (h, 0, qi))],
        out_specs=[q_spec, k_spec, v_spec],
        scratch_shapes=[pltpu.VMEM((QK_HEAD, tqs), F32), pltpu.VMEM((s_dim, HEAD), F32)],
        compiler_params=pltpu.CompilerParams(dimension_semantics=("parallel", "arbitrary")),
        name="attn_bwd",
    )(qc, kc, kct, v, o, d_o, lse)


def _mix_proj_ln1(y_dn, y_mla, proj_g, x, w_o, g, b):
    s_dim = x.shape[0]
    tm = min(512, s_dim)

    def body(yd_ref, ym_ref, g_ref, x_ref, w_ref, lg_ref, lb_ref, mixed_ref, a1_ref, h1_ref, h1b_ref):
        gates = g_ref[...].astype(F32)
        mixed = (_sig(gates[:, :D_MODEL]) * yd_ref[...].astype(F32)
                 + _sig(gates[:, D_MODEL:]) * ym_ref[...].astype(F32)).astype(BF16)
        a1 = _dot(mixed, w_ref[...])
        xh, _ = _ln_stats(ALPHA * x_ref[...] + a1)
        y = xh * lg_ref[...] + lb_ref[...]
        mixed_ref[...] = mixed
        a1_ref[...] = a1
        h1_ref[...] = y
        h1b_ref[...] = y.astype(BF16)

    row = lambda width: pl.BlockSpec((tm, width), lambda i: (i, 0))
    whole = lambda a: pl.BlockSpec(a.shape, lambda i: (0, 0))
    sds = lambda dt: jax.ShapeDtypeStruct((s_dim, D_MODEL), dt)
    return pl.pallas_call(
        body,
        out_shape=[sds(BF16), sds(F32), sds(F32), sds(BF16)],
        grid=(s_dim // tm,),
        in_specs=[row(D_MODEL), row(D_MODEL), row(2 * D_MODEL), row(D_MODEL), whole(w_o), whole(g), whole(b)],
        out_specs=[row(D_MODEL)] * 4,
        compiler_params=pltpu.CompilerParams(dimension_semantics=("parallel",)),
        name="mix_proj_ln1",
    )(y_dn, y_mla, proj_g, x, w_o, g, b)


def _ln1_mix_bwd(x, a1, d_h1, d_pg, y_dn, y_mla, proj_g, g, w_o, w_pg):
    def fn(r, c):
        x_, a1_, dy, dpg, yd, ym, gates = r
        g_, w_o_, w_pg_ = c
        dy = dy + _dot(dpg, w_pg_, NT)
        xh, rr = _ln_stats(ALPHA * x_ + a1_)
        dz = _ln_bwd(dy, xh, rr, g_)
        dz_b = dz.astype(BF16)
        dm = _dot(dz_b, w_o_, NT)
        sd, sm = _sig(gates[:, :D_MODEL]), _sig(gates[:, D_MODEL:])
        d_g = jnp.concatenate([dm * yd * sd * (1.0 - sd), dm * ym * sm * (1.0 - sm)], axis=1)
        return [dz_b, ALPHA * dz, d_g, dm * sd, dm * sm], [_colsum(dy * xh), _colsum(dy)]

    return _rowwise(fn, [x, a1, d_h1, d_pg, y_dn, y_mla, proj_g], [g, w_o, w_pg],
                    [(D_MODEL, BF16), (D_MODEL, F32), (2 * D_MODEL, BF16), (D_MODEL, BF16), (D_MODEL, BF16)],
                    accs=[(1, D_MODEL), (1, D_MODEL)], tm=FUSED_ROWS, name="ln1_mix_bwd")


def _ln_stats(z):
    mu = _rowmean(z)
    zc = z - mu
    r = lax.rsqrt(_rowmean(zc * zc) + EPS_LN)
    return zc * r, r


def _ln_bwd(dy, xh, r, g):
    dxh = dy * g
    return r * (dxh - _rowmean(dxh) - xh * _rowmean(dxh * xh))


def _ffn_in_act(h1b, w_t):
    s_dim, k_dim = h1b.shape
    hidden = w_t.shape[0] // 2
    tm, tn = min(512, s_dim), _pick_wide(hidden)
    nt = hidden // tn

    def body(a_ref, bg_ref, bu_ref, gt_ref, up_ref, act_ref):
        a = a_ref[...]
        gt, up = _dot(a, bg_ref[...], NT), _dot(a, bu_ref[...], NT)
        gt_ref[...] = gt.astype(BF16)
        up_ref[...] = up.astype(BF16)
        act_ref[...] = (gt * _sig(gt) * up).astype(BF16)

    o_spec = pl.BlockSpec((tm, tn), lambda j, i: (i, j))
    sds = jax.ShapeDtypeStruct((s_dim, hidden), BF16)
    return pl.pallas_call(
        body,
        out_shape=[sds, sds, sds],
        grid=(nt, s_dim // tm),
        in_specs=[pl.BlockSpec((tm, k_dim), lambda j, i: (i, 0)), pl.BlockSpec((tn, k_dim), lambda j, i: (j, 0)),
                  pl.BlockSpec((tn, k_dim), lambda j, i: (j + nt, 0))],
        out_specs=[o_spec, o_spec, o_spec],
        compiler_params=pltpu.CompilerParams(dimension_semantics=("parallel", "parallel")),
        name="ffn_in_act",
    )(h1b, w_t, w_t)


def _act_bwd(gt, up, d_act):
    def fn(r, c):
        gt_, up_, da = r
        return [jnp.concatenate([da * up_ * _silu_grad(gt_), da * gt_ * _sig(gt_)], axis=1)], []

    return _rowwise(fn, [gt, up, d_act], [], [(2 * FFN_HIDDEN, BF16)], name="act_bwd")[0]


def _tail(h1, ffn, p, tgt, g, b, w_pg, w_ple_t):
    def fn(r, c):
        h1_, ffn_, p_, t_ = r
        pg_ = _dot(h1_, c[2])
        pp_ = _dot(p_, c[3], NT)
        sp = _sig(pg_)
        xh, rr = _ln_stats(ALPHA * h1_ + ffn_ + sp * pp_)
        y = xh * c[0] + c[1]
        err = y - t_
        dy = err * (1.0 / D_MODEL)
        dz = _ln_bwd(dy, xh, rr, c[0])
        loss = jnp.sum(0.5 * _rowmean(err * err), axis=0, keepdims=True)
        return ([dz, dz * pp_ * sp * (1.0 - sp), dz * sp, ALPHA * dz],
                [_colsum(dy * xh), _colsum(dy), jnp.broadcast_to(loss, (1, LANES))])

    return _rowwise(fn, [h1, ffn, p, tgt], [g, b, w_pg, w_ple_t], [(D_MODEL, BF16)] * 3 + [(D_MODEL, F32)],
                    accs=[(1, D_MODEL), (1, D_MODEL), (1, LANES)], tm=FUSED_ROWS, name="tail")


def _local_step(x, p, pos, tgt, w, late_weights, emit):
    w = dict(w)
    s_dim = x.shape[0]
    pb = p.astype(BF16)
    proj_a, proj_g, proj_b, xb = _input_proj(x, w["w_in_t"], w["wg_t"], w["wb_t"])
    qkvn = _conv_fwd(proj_a, w["conv"])
    beta, gc = _gates_fwd(proj_b, w["alog"], w["dtb"])
    gc_t = jnp.transpose(gc[:, :N_HEADS])
    u, w_, qd, kt, a_mat, t_fold = _gdr_prep_fwd(qkvn, beta, gc, gc_t)
    o_dn, states = _gdr_scan_fwd(u, w_, qd, kt, a_mat, gc)
    w.update(late_weights("mix", o_dn))
    og, y_dn = _gdr_out_fwd(o_dn, proj_a, w["dnw"], w["br_dn"])
    c_q, c_kv, qc, kc, vv, kct, vt = _mla_prep_fwd(proj_b, pos, w["qnw"], w["kvnw"], w["uq"], w["uk"], w["uv"])
    o_mla, lse = _attn_fwd(qc, kc, vt)
    y_mla = _mm_resident(o_mla, w["br_mla"], out_dtype=BF16, name="f_y_mla")
    mixed, a1, h1, h1b = _mix_proj_ln1(y_dn, y_mla, proj_g, x, w["wo"], w["ln1g"], w["ln1b"])
    w.update(late_weights("ffn", a1))
    gt, up, act = _ffn_in_act(h1b, w["ffn_in_t"])
    ffn = _mm_resident(act, w["ffn_out"], name="f_ffn")
    g = {}
    dz2, d_pg, d_pp, dh1a, g["ln2g"], g["ln2b"], loss = _tail(h1, ffn, pb, tgt, w["ln2g"], w["ln2b"],
                                                            w["ple_gate"], w["ple_t"])
    g["ple_t"] = _mm(d_pp, pb, ta=True, out_dtype=BF16, name="b_w_ple")
    g["ple_gate"] = _mm(h1b, d_pg, ta=True, out_dtype=BF16, name="b_w_ple_gate")
    g["ffn_out"] = _mm(act, dz2, ta=True, out_dtype=BF16, name="b_w_ffn_out")
    d_act = _mm_resident(dz2, w["ffn_out"], tb=True, out_dtype=BF16, name="b_act")
    d_gu = _act_bwd(gt, up, d_act)
    g["ffn_in_t"] = _mm(d_gu, h1b, ta=True, out_dtype=BF16, name="b_w_ffn_in")
    d_gu = emit("ffn", g, d_gu)
    d_h1 = _mm_resident(d_gu, w["ffn_in_t"], add=(dh1a,), name="b_h1_ffn")
    dz1, dxa, d_proj_g, d_y_dn, d_y_mla, g["ln1g"], g["ln1b"] = _ln1_mix_bwd(
        x, a1, d_h1, d_pg, y_dn, y_mla, proj_g, w["ln1g"], w["wo"], w["ple_gate"])
    g["wo"] = _mm(mixed, dz1, ta=True, out_dtype=BF16, name="b_w_o")
    g["br_mla"] = _mm(o_mla, d_y_mla, ta=True, out_dtype=BF16, name="b_w_br_mla")
    d_o_mla = _mm_resident(d_y_mla, w["br_mla"], tb=True, out_dtype=BF16, name="b_o_mla")
    d_qc, d_kc, d_v = _attn_bwd(qc, kc, kct, vv, o_mla, d_o_mla, lse)
    d_q_full, d_kn, d_cq, d_ckv, d_kr, g["qnw"], g["kvnw"] = _mla_prep_bwd(
        proj_b, pos, d_qc, d_kc, d_v, w["qnw"], w["kvnw"], w["uq"], w["uk"], w["uv"])
    g["uq"] = _mm(c_q, d_q_full, ta=True, out_dtype=BF16, name="b_w_uq")
    g["uk"] = _mm(c_kv, d_kn, ta=True, out_dtype=BF16, name="b_w_uk")
    g["uv"] = _mm(c_kv, d_v, ta=True, out_dtype=BF16, name="b_w_uv")
    g["br_dn"] = _mm(og, d_y_dn, ta=True, out_dtype=BF16, name="b_w_br_dn")
    d_y_dn = emit("mix", g, d_y_dn)
    d_o_dn, d_z, g["dnw"] = _gdr_out_bwd(o_dn, proj_a, d_y_dn, w["dnw"], w["br_dn"])
    du, dw, dqd, dkt, d_a, d_egl = _gdr_scan_bwd(u, w_, qd, kt, a_mat, gc, states, d_o_dn)
    dq, dk, dv, d_beta, d_gc = _gdr_prep_bwd(qkvn, beta, gc, gc_t, t_fold, u, w_, du, dw, dqd, dkt, d_a)
    d_egl_rows = jnp.pad(d_egl[:, None, :, 0], ((0, 0), (CHUNK - 1, 0), (0, LANES - N_HEADS))).reshape(s_dim, LANES)
    d_ba, g["alog"], g["dtb"] = _gates_bwd(proj_b, w["alog"], w["dtb"], gc, d_beta, d_gc, d_egl_rows)
    d_qkv, g["conv"] = _conv_bwd(proj_a, w["conv"], dq, dk, dv)
    zeros = jnp.zeros((s_dim, WB_CKV - Q_LORA), BF16)
    d_proj_b = jnp.concatenate([d_cq, zeros, d_ckv, d_kr, d_ba], axis=1)
    g["wa_qkv_t"] = _mm(d_qkv, xb, ta=True, name="b_w_qkv")
    g["wa_z_t"] = _mm(d_z, xb, ta=True, name="b_w_z")
    g["wg_t"] = _mm(d_proj_g, xb, ta=True, name="b_w_g")
    g["wb_t"] = _mm(d_proj_b, xb, ta=True, name="b_w_b")
    d_qkv = emit("small", dict(g, loss=loss), emit("w_in", g, d_qkv))
    dx = _input_grad(d_qkv, d_z, d_proj_g, d_proj_b, w["w_in_t"], w["wg_t"], w["wb_t"], dxa)
    return loss, dx, g


DX_ROWS = 512


def _input_proj(x, w_in_t, wg_t, wb_t):
    s_dim = x.shape[0]
    n_a = 4 * D_MODEL

    def body(x_ref, wa_ref, wg_ref, wb_ref, a_ref, g_ref, b_ref, xb_ref):
        xv = x_ref[...].astype(BF16)
        xb_ref[...] = xv
        a_ref[...] = _dot(xv, wa_ref[...], NT)
        g_ref[...] = _dot(xv, wg_ref[...], NT).astype(BF16)
        b_ref[...] = _dot(xv, wb_ref[...], NT)

    rows = lambda width: pl.BlockSpec((DX_ROWS, width), lambda i: (i, 0))
    whole = lambda shape: pl.BlockSpec(shape, lambda i: (0, 0), pipeline_mode=pl.Buffered(1))
    return pl.pallas_call(
        body,
        out_shape=[jax.ShapeDtypeStruct((s_dim, n_a), F32), jax.ShapeDtypeStruct((s_dim, wg_t.shape[0]), BF16),
                   jax.ShapeDtypeStruct((s_dim, wb_t.shape[0]), F32), jax.ShapeDtypeStruct((s_dim, D_MODEL), BF16)],
        grid=(s_dim // DX_ROWS,),
        in_specs=[rows(D_MODEL), whole((n_a, D_MODEL)), whole(wg_t.shape), whole(wb_t.shape)],
        out_specs=[rows(n_a), rows(wg_t.shape[0]), rows(wb_t.shape[0]), rows(D_MODEL)],
        compiler_params=pltpu.CompilerParams(dimension_semantics=("parallel",)),
        name="f_proj",
    )(x, w_in_t, wg_t, wb_t)


def _input_grad(d_qkv, d_z, d_g, d_b, w_in_t, wg_t, wb_t, add):
    s_dim = d_qkv.shape[0]
    n_qkv, n_a = d_qkv.shape[1], d_qkv.shape[1] + d_z.shape[1]

    def body(q_ref, z_ref, g_ref, b_ref, wa_ref, wg_ref, wb_ref, add_ref, o_ref):
        r = add_ref[...] + _dot(q_ref[...], wa_ref[0:n_qkv])
        r = r + _dot(z_ref[...], wa_ref[n_qkv:n_a])
        r = r + _dot(g_ref[...], wg_ref[...])
        o_ref[...] = r + _dot(b_ref[...], wb_ref[...])

    rows = lambda a: pl.BlockSpec((DX_ROWS, a.shape[1]), lambda i: (i, 0))
    whole = lambda shape: pl.BlockSpec(shape, lambda i: (0, 0), pipeline_mode=pl.Buffered(1))
    return pl.pallas_call(
        body,
        out_shape=jax.ShapeDtypeStruct((s_dim, D_MODEL), F32),
        grid=(s_dim // DX_ROWS,),
        in_specs=[rows(d_qkv), rows(d_z), rows(d_g), rows(d_b), whole((n_a, D_MODEL)), whole(wg_t.shape),
                  whole(wb_t.shape), rows(add)],
        out_specs=pl.BlockSpec((DX_ROWS, D_MODEL), lambda i: (i, 0)),
        compiler_params=pltpu.CompilerParams(dimension_semantics=("parallel",)),
        name="b_x",
    )(d_qkv, d_z, d_g, d_b, w_in_t, wg_t, wb_t, add)


_BIG = (("w_in", 1), ("w_uq", 0), ("w_uk", 0), ("w_uv", 0), ("w_br_dn", 0), ("w_br_mla", 0),
        ("w_o", 0), ("w_ffn_in", 1), ("w_ffn_out", 0), ("w_ple", 1), ("w_ple_gate", 0))
_BIG_AXIS = dict(_BIG)
_SMALL = ("ln1_g", "ln1_b", "ln2_g", "ln2_b", "q_norm_w", "kv_norm_w", "dn_norm_w", "dn_a_log", "dn_dt_bias")
_ORDER = ("w_in", "conv_w", "dn_a_log", "dn_dt_bias", "dn_norm_w", "q_norm_w", "w_uq", "kv_norm_w", "w_uk", "w_uv",
          "w_br_dn", "w_br_mla", "w_o", "ln1_g", "ln1_b", "w_ffn_in", "w_ffn_out", "w_ple", "w_ple_gate", "ln2_g",
          "ln2_b")


def _stored_shape(name, shard_shape):
    axis = _BIG_AXIS[name]
    lead = shard_shape[axis]
    return lead, int(np.prod(shard_shape)) // lead


def _to_stored(name, shard):
    return jnp.moveaxis(shard, _BIG_AXIS[name], 0).reshape(_stored_shape(name, shard.shape))


def _from_stored(name, stored, shard_shape):
    axis = _BIG_AXIS[name]
    moved = (shard_shape[axis],) + shard_shape[:axis] + shard_shape[axis + 1:]
    return jnp.moveaxis(stored.reshape(moved), 0, axis)


_W_IN_ROWS = np.cumsum([0, 3072, 1024, 8, 8, Q_LORA, KV_LORA, ROPE, D_MODEL, D_MODEL])


def _first_weights(w_in_t, conv_full, small):
    r = _W_IN_ROWS
    zr = lambda n: jnp.zeros((n, D_MODEL), w_in_t.dtype)
    w = {}
    w["w_in_t"] = w_in_t
    w["wg_t"] = w_in_t[r[7]:r[9]]
    w["wb_t"] = jnp.concatenate([w_in_t[r[4]:r[5]], zr(WB_CKV - Q_LORA), w_in_t[r[5]:r[7]], zr(LANES - ROPE),
                                 w_in_t[r[2]:r[4]], zr(LANES - 2 * N_HEADS)], axis=0)
    w["conv"] = conv_full
    pad_l = lambda v: jnp.pad(v, ((0, 0), (0, LANES - v.shape[1])))
    w["alog"], w["dtb"] = pad_l(small["dn_a_log"]), pad_l(small["dn_dt_bias"])
    w["dnw"], w["qnw"], w["kvnw"] = small["dn_norm_w"], small["q_norm_w"], small["kv_norm_w"]
    w["ln1g"], w["ln1b"], w["ln2g"], w["ln2b"] = small["ln1_g"], small["ln1_b"], small["ln2_g"], small["ln2_b"]
    return w


def _late_weights(group, fw):
    w = {}
    if group == "mix":
        uq = fw["w_uq"].reshape(Q_LORA, N_HEADS, HEAD + ROPE)
        uq_r = jnp.pad(uq[:, :, HEAD:], ((0, 0), (0, 0), (0, HEAD - ROPE)))
        w["uq"] = jnp.concatenate([uq[:, :, :HEAD].reshape(Q_LORA, -1), uq_r.reshape(Q_LORA, -1)], axis=1)
        w["uk"], w["uv"] = fw["w_uk"], fw["w_uv"]
        w["br_dn"], w["br_mla"], w["wo"] = fw["w_br_dn"], fw["w_br_mla"], fw["w_o"]
    else:
        w["ffn_in_t"], w["ffn_out"] = fw["w_ffn_in"], fw["w_ffn_out"]
        w["ple_t"], w["ple_gate"] = fw["w_ple"], fw["w_ple_gate"]
    return w


_GROUP_GRADS = {"ffn": (("w_ple", "ple_t"), ("w_ple_gate", "ple_gate"), ("w_ffn_out", "ffn_out"),
                        ("w_ffn_in", "ffn_in_t")),
                "mix": (("w_o", "wo"), ("w_br_mla", "br_mla"), ("w_uq", "uq"), ("w_uk", "uk"), ("w_uv", "uv"),
                        ("w_br_dn", "br_dn"))}


def _group_grads(group, g):
    out = {}
    for name, key in _GROUP_GRADS[group]:
        t = g[key]
        if name == "w_uq":
            uq_n = t[:, :D_MODEL].reshape(Q_LORA, N_HEADS, HEAD)
            uq_r = t[:, D_MODEL:].reshape(Q_LORA, N_HEADS, HEAD)[:, :, :ROPE]
            t = jnp.concatenate([uq_n, uq_r], axis=2).reshape(Q_LORA, -1)
        out[name] = t
    return out


PACK_ROWS = 512
PACK_BUFFERS, PACK_AHEAD = 4, 2
SUBLANES = 8


def _pack_exchange(parts, name):
    arrays = []
    for a, _, _ in parts:
        if not any(a is b for b in arrays):
            arrays.append(a)
    index = lambda a: next(i for i, b in enumerate(arrays) if a is b)
    chunks, dst = [], 0
    for a, first, rows in parts:
        assert first % SUBLANES == 0 and rows % SUBLANES == 0
        chunks += [(index(a), first + o, dst + o, min(PACK_ROWS, rows - o)) for o in range(0, rows, PACK_ROWS)]
        dst += rows
    c, n, last = arrays[0].shape[1], len(arrays), len(chunks) - 1
    slab = dst // N_DEV
    assert slab * N_DEV == dst

    def body(*refs):
        src_refs, out_ref, recv_ref = refs[:n], refs[n], refs[n + 1]
        buf, sem_in, sem_out, send_sems, recv_sems = refs[n + 2:]
        x, y, core = lax.axis_index("x"), lax.axis_index("y"), lax.axis_index("c")

        def to_sibling(q):
            return pltpu.make_async_remote_copy(
                src_ref=out_ref.at[pl.ds((2 * q + 1 - core) * slab, slab)], dst_ref=recv_ref.at[q],
                send_sem=send_sems.at[q], recv_sem=recv_sems.at[q], device_id=(x, y, 1 - core),
                device_id_type=_MESH_ID)

        sent = [0]

        def send_packed(rows_done):
            while sent[0] < N_DEV // 2 and (2 * sent[0] + 2) * slab <= rows_done:
                to_sibling(sent[0]).start()
                sent[0] += 1

        def load(k):
            i, first, _, rows = chunks[k]
            return pltpu.make_async_copy(src_refs[i].at[pl.ds(first, rows)],
                                         buf.at[k % PACK_BUFFERS, pl.ds(0, rows)], sem_in.at[k % PACK_BUFFERS])

        def store(k):
            _, _, first, rows = chunks[k]
            return pltpu.make_async_copy(buf.at[k % PACK_BUFFERS, pl.ds(0, rows)],
                                         out_ref.at[pl.ds(first, rows), 0, :], sem_out.at[k % PACK_BUFFERS])

        def stored(k):
            store(k).wait()
            send_packed(chunks[k][2] + chunks[k][3])

        for k in range(min(PACK_AHEAD, last + 1)):
            load(k).start()
        for k in range(last + 1):
            load(k).wait()
            store(k).start(priority=1)
            ahead = k + PACK_AHEAD
            if ahead <= last:
                if ahead >= PACK_BUFFERS:
                    stored(ahead - PACK_BUFFERS)
                load(ahead).start()
        for k in range(max(0, last + 1 - PACK_BUFFERS), last + 1):
            stored(k)
        for q in range(N_DEV // 2):
            to_sibling(q).wait_recv()
        for q in range(N_DEV // 2):
            to_sibling(q).wait_send()

    return pl.pallas_call(
        body,
        out_shape=[jax.ShapeDtypeStruct((dst, 1, c), F32), jax.ShapeDtypeStruct((N_DEV // 2, slab, 1, c), F32)],
        in_specs=[_ANY] * n,
        out_specs=[_ANY, _ANY],
        scratch_shapes=[pltpu.VMEM((PACK_BUFFERS, PACK_ROWS, c), F32), pltpu.SemaphoreType.DMA((PACK_BUFFERS,)),
                        pltpu.SemaphoreType.DMA((PACK_BUFFERS,)), pltpu.SemaphoreType.DMA((N_DEV // 2,)),
                        pltpu.SemaphoreType.DMA((N_DEV // 2,))],
        name=name,
    )(*arrays)


def _w_in_grad_parts(g):
    wb = g["wb_t"]
    return [(g["wa_qkv_t"], 0, 3 * D_MODEL), (g["wa_z_t"], 0, D_MODEL), (wb, WB_BA, 2 * N_HEADS),
            (wb, WB_CQ, Q_LORA), (wb, WB_CKV, KV_LORA), (wb, WB_KR, ROPE), (g["wg_t"], 0, 2 * D_MODEL)]


def _small_grads(g):
    return {"ln1_g": g["ln1g"], "ln1_b": g["ln1b"], "ln2_g": g["ln2g"], "ln2_b": g["ln2b"], "q_norm_w": g["qnw"],
            "kv_norm_w": g["kvnw"], "dn_norm_w": g["dnw"], "dn_a_log": g["alog"], "dn_dt_bias": g["dtb"],
            "conv_w": g["conv"]}


_SMALL_SLOTS = {"ln1_g": (0, 0, 1024), "ln1_b": (1, 0, 1024), "ln2_g": (2, 0, 1024), "ln2_b": (3, 0, 1024),
                "q_norm_w": (4, 0, 384), "kv_norm_w": (4, 384, 256), "dn_norm_w": (4, 640, 128),
                "dn_a_log": (4, 768, 8), "dn_dt_bias": (4, 896, 8)}
_SMALL_ROWS, _LOSS_ROW, _CONV_ROW0, _CONV_ROWS = 24, 5, 8, 12


def _pack_small_grads(small_g, loss):
    zeros = lambda r, c: jnp.zeros((r, c), F32)
    row4 = jnp.concatenate([small_g["q_norm_w"], small_g["kv_norm_w"], small_g["dn_norm_w"], small_g["dn_a_log"],
                            small_g["dn_dt_bias"]], axis=1)
    row5 = jnp.concatenate([loss, zeros(1, FLAT_COLS - LANES)], axis=1)
    head = jnp.concatenate([small_g["ln1_g"], small_g["ln1_b"], small_g["ln2_g"], small_g["ln2_b"], row4, row5,
                            zeros(2, FLAT_COLS)], axis=0)
    conv = small_g["conv_w"].reshape(_CONV_ROWS, FLAT_COLS)
    return jnp.concatenate([head, conv, zeros(_SMALL_ROWS - _CONV_ROW0 - _CONV_ROWS, FLAT_COLS)], axis=0)


_MESH_ID = pl.DeviceIdType.MESH
_ANY = pl.BlockSpec(memory_space=pl.ANY)


def _all_gather(blocks, name):
    n = len(blocks)

    def body(*refs):
        x_refs, out_refs = refs[:n], refs[n:2 * n]
        send_sems, recv_sems, local_sems = refs[2 * n:]
        x, y, c = lax.axis_index("x"), lax.axis_index("y"), lax.axis_index("c")
        me, sibling = (x, y, c), (x, y, 1 - c)
        chips = [(1 - x, y), (x, 1 - y), (1 - x, 1 - y)]

        def slot(i, px, py, pc):
            return out_refs[i].at[4 * px + 2 * py + pc]

        def copy(i, k, origin, to, src=None):
            return pltpu.make_async_remote_copy(
                src_ref=slot(i, *origin) if src is None else src, dst_ref=slot(i, *origin),
                send_sem=send_sems.at[7 * i + k], recv_sem=recv_sems.at[7 * i + k], device_id=to,
                device_id_type=_MESH_ID)

        mine = [pltpu.make_async_copy(x_refs[i], slot(i, *me), local_sems.at[i]) for i in range(n)]
        first, passed = [], []
        for i in range(n):
            mine[i].start()
            first.append(copy(i, 0, me, sibling, src=x_refs[i]))
            first += [copy(i, 1 + j, me, (*chip, c), src=x_refs[i]) for j, chip in enumerate(chips)]
        for cp in first:
            cp.start()
        for i in range(n):
            for j, chip in enumerate(chips):
                copy(i, 1 + j, (*chip, c), me).wait_recv()
                passed.append(copy(i, 4 + j, (*chip, c), sibling))
                passed[-1].start()
        for i in range(n):
            copy(i, 0, sibling, me).wait_recv()
            for j, chip in enumerate(chips):
                copy(i, 4 + j, (*chip, 1 - c), me).wait_recv()
        for cp in first + passed:
            cp.wait_send()
        for cp in mine:
            cp.wait()

    return pl.pallas_call(
        body,
        out_shape=[jax.ShapeDtypeStruct((N_DEV,) + b.shape, b.dtype) for b in blocks],
        in_specs=[_ANY] * n,
        out_specs=[_ANY] * n,
        scratch_shapes=[pltpu.SemaphoreType.DMA((7 * n,)), pltpu.SemaphoreType.DMA((7 * n,)),
                        pltpu.SemaphoreType.DMA((n,))],
        name=name,
    )(*blocks)


def _col_tile(c):
    return c if c <= 256 else 256


def _chip_sum(src, recv, parity, name):
    _, r, _, c = src.shape
    tc = _col_tile(c)

    def body(par_ref, a_ref, b_ref, o_ref, ob_ref):
        s = a_ref[...] + b_ref[...]
        o_ref[...] = s
        ob_ref[...] = s.astype(BF16)

    rows = lambda f: pl.BlockSpec((None, r, None, tc), f)
    blk = pl.BlockSpec((None, r, tc), lambda q, j, par: (q, 0, j))
    return pl.pallas_call(
        body,
        out_shape=[jax.ShapeDtypeStruct((4, r, c), F32), jax.ShapeDtypeStruct((4, r, c), BF16)],
        grid_spec=pltpu.PrefetchScalarGridSpec(
            num_scalar_prefetch=1, grid=(4, c // tc),
            in_specs=[rows(lambda q, j, par: (2 * q + par[0], 0, 0, j)), rows(lambda q, j, par: (q, 0, 0, j))],
            out_specs=[blk, blk]),
        compiler_params=pltpu.CompilerParams(dimension_semantics=("parallel", "parallel")),
        name=name,
    )(parity, src, recv)


_HBM = pl.BlockSpec(memory_space=pltpu.HBM)
_SEM = pl.BlockSpec(memory_space=pltpu.SEMAPHORE)
_DATAFLOW = pltpu.SideEffectType.DATAFLOW_SIDE_EFFECTING
N_PEERS = N_DEV - 1


def _ring_peer(j):
    me = 4 * lax.axis_index("x") + 2 * lax.axis_index("y") + lax.axis_index("c")
    k = (me + j) % N_DEV
    return me, k, (k // 4, (k // 2) % 2, k % 2)


def _spread_copy(i, j, src_refs, land_refs, send_sems, recv_sems, scatter):
    me, k, peer = _ring_peer(j)
    return pltpu.make_async_remote_copy(
        src_ref=src_refs[i].at[k] if scatter else src_refs[i], dst_ref=land_refs[i].at[me],
        send_sem=send_sems.at[N_PEERS * i + j - 1], recv_sem=recv_sems.at[N_PEERS * i + j - 1], device_id=peer,
        device_id_type=_MESH_ID)


def _spread_start(srcs, carry, scatter, name):
    n = len(srcs)
    lands = [lax.empty(((N_DEV,) + s.shape[-2:]), s.dtype) for s in srcs]

    def body(*refs):
        src_refs, land_refs = refs[:n], refs[n:2 * n]
        send_sems, recv_sems, local_sems = refs[2 * n + 1:2 * n + 4]
        for i in range(n):
            for j in range(1, N_DEV):
                _spread_copy(i, j, src_refs, land_refs, send_sems, recv_sems, scatter).start()
        for i in range(n):
            _own_copy(i, src_refs, land_refs, local_sems, scatter).start()

    hbm = lambda a: pltpu.HBM(a.shape, a.dtype)
    sems = pltpu.SemaphoreType.DMA((N_PEERS * n,))
    pinned = [pltpu.with_memory_space_constraint(a, pltpu.HBM) for a in list(srcs) + lands + [carry]]
    res = pl.pallas_call(
        body, name=name,
        out_shape=(sems, sems, pltpu.SemaphoreType.DMA((n,)), *[hbm(a) for a in pinned]),
        in_specs=[_HBM] * (2 * n + 1),
        out_specs=(_SEM, _SEM, _SEM, *[_HBM] * (2 * n + 1)),
        input_output_aliases={i: 3 + i for i in range(2 * n + 1)},
        compiler_params=pltpu.CompilerParams(has_side_effects=_DATAFLOW),
    )(*pinned)
    return res[:3], list(res[3:3 + n]), list(res[3 + n:3 + 2 * n]), res[3 + 2 * n]


def _own_copy(i, src_refs, land_refs, local_sems, scatter):
    me = _ring_peer(0)[0]
    return pltpu.make_async_copy(src_refs[i].at[me] if scatter else src_refs[i], land_refs[i].at[me],
                                 local_sems.at[i])


def _spread_wait(started, after, scatter, name):
    sems, srcs, lands, _ = started
    n = len(srcs)

    def body(*refs):
        src_refs, land_refs = refs[:n], refs[n:2 * n]
        send_s, recv_s, local_s = refs[2 * n:2 * n + 3]
        for i in range(n):
            for j in range(1, N_DEV):
                cp = _spread_copy(i, j, src_refs, land_refs, send_s, recv_s, scatter)
                cp.wait_send()
                cp.wait_recv()
        for i in range(n):
            _own_copy(i, src_refs, land_refs, local_s, scatter).wait()

    hbm = lambda a: pltpu.HBM(a.shape, a.dtype)
    res = pl.pallas_call(
        body, name=name,
        out_shape=tuple(hbm(a) for a in srcs + lands),
        in_specs=[_HBM] * (2 * n) + [_SEM, _SEM, _SEM, pl.BlockSpec(memory_space=pl.ANY)],
        out_specs=tuple([_HBM] * (2 * n)),
        input_output_aliases={i: i for i in range(2 * n)},
        compiler_params=pltpu.CompilerParams(has_side_effects=_DATAFLOW),
    )(*srcs, *lands, *sems, after)
    return list(res[n:])


def _chips_copy(i, j, src_refs, land_refs, send_sems, recv_sems):
    x, y, c = lax.axis_index("x"), lax.axis_index("y"), lax.axis_index("c")
    tx, ty = [(1 - x, y), (x, 1 - y), (1 - x, 1 - y)][j]
    return pltpu.make_async_remote_copy(
        src_ref=src_refs[i].at[2 * tx + ty], dst_ref=land_refs[i].at[j], send_sem=send_sems.at[3 * i + j],
        recv_sem=recv_sems.at[3 * i + j], device_id=(tx, ty, c), device_id_type=_MESH_ID)


def _chips_start(srcs, carry, name):
    n = len(srcs)
    lands = [lax.empty((3,) + s.shape[1:], s.dtype) for s in srcs]

    def body(*refs):
        src_refs, land_refs = refs[:n], refs[n:2 * n]
        send_sems, recv_sems = refs[2 * n + 1:2 * n + 3]
        for i in range(n):
            for j in range(3):
                _chips_copy(i, j, src_refs, land_refs, send_sems, recv_sems).start()

    hbm = lambda a: pltpu.HBM(a.shape, a.dtype)
    sems = pltpu.SemaphoreType.DMA((3 * n,))
    pinned = [pltpu.with_memory_space_constraint(a, pltpu.HBM) for a in list(srcs) + lands + [carry]]
    res = pl.pallas_call(
        body, name=name,
        out_shape=(sems, sems, *[hbm(a) for a in pinned]),
        in_specs=[_HBM] * (2 * n + 1),
        out_specs=(_SEM, _SEM, *[_HBM] * (2 * n + 1)),
        input_output_aliases={i: 2 + i for i in range(2 * n + 1)},
        compiler_params=pltpu.CompilerParams(has_side_effects=_DATAFLOW),
    )(*pinned)
    return res[:2], list(res[2:2 + n]), list(res[2 + n:2 + 2 * n]), res[2 + 2 * n]


def _chips_wait(started, after, name):
    sems, srcs, lands, _ = started
    n = len(srcs)

    def body(*refs):
        src_refs, land_refs = refs[:n], refs[n:2 * n]
        send_s, recv_s = refs[2 * n:2 * n + 2]
        for i in range(n):
            for j in range(3):
                cp = _chips_copy(i, j, src_refs, land_refs, send_s, recv_s)
                cp.wait_send()
                cp.wait_recv()

    hbm = lambda a: pltpu.HBM(a.shape, a.dtype)
    res = pl.pallas_call(
        body, name=name,
        out_shape=tuple(hbm(a) for a in srcs + lands),
        in_specs=[_HBM] * (2 * n) + [_SEM, _SEM, pl.BlockSpec(memory_space=pl.ANY)],
        out_specs=tuple([_HBM] * (2 * n)),
        input_output_aliases={i: i for i in range(2 * n)},
        compiler_params=pltpu.CompilerParams(has_side_effects=_DATAFLOW),
    )(*srcs, *lands, *sems, after)
    return list(res[n:])


def _sum8(landing, name):
    _, r, c = landing.shape
    tc = _col_tile(c)

    def body(a_ref, o_ref):
        tot = a_ref[0].astype(F32)
        for k in range(1, N_DEV):
            tot = tot + a_ref[k].astype(F32)
        o_ref[...] = tot

    return pl.pallas_call(
        body,
        out_shape=jax.ShapeDtypeStruct((r, c), F32),
        grid=(c // tc,),
        in_specs=[pl.BlockSpec((N_DEV, r, tc), lambda j: (0, 0, j))],
        out_specs=pl.BlockSpec((r, tc), lambda j: (0, j)),
        compiler_params=pltpu.CompilerParams(dimension_semantics=("parallel",)),
        name=name,
    )(landing)


def _adamw_math(w, g, m, v):
    m = ADAM_B1 * m + (1.0 - ADAM_B1) * g
    v = ADAM_B2 * v + (1.0 - ADAM_B2) * (g * g)
    m_hat = m / (1.0 - ADAM_B1 ** ADAM_STEP)
    v_hat = v / (1.0 - ADAM_B2 ** ADAM_STEP)
    delta = -ADAM_LR * (m_hat / (jnp.sqrt(v_hat) + ADAM_EPS) + ADAM_WD * w)
    return delta, m, v


def _adamw(w, m, v, g, name):
    r, c = w.shape

    def fn(rows, consts):
        return list(_adamw_math(*rows)), []

    return _rowwise(fn, [w, g, m, v], [], [(c, F32)] * 3, tm=r if r <= 512 else 256, name=name)


def _adamw_sum8(w, m, v, landing, name):
    r, c = w.shape
    tc = _col_tile(c)

    def body(w_ref, m_ref, v_ref, a_ref, g_ref, d_ref, m2_ref, v2_ref):
        g = a_ref[0].astype(F32)
        for k in range(1, N_DEV):
            g = g + a_ref[k].astype(F32)
        delta, m2, v2 = _adamw_math(w_ref[...], g, m_ref[...], v_ref[...])
        g_ref[...] = g
        d_ref[...] = delta
        m2_ref[...] = m2
        v2_ref[...] = v2

    blk = pl.BlockSpec((r, tc), lambda j: (0, j))
    return pl.pallas_call(
        body,
        out_shape=[jax.ShapeDtypeStruct((r, c), F32)] * 4,
        grid=(c // tc,),
        in_specs=[blk, blk, blk, pl.BlockSpec((N_DEV, r, tc), lambda j: (0, 0, j))],
        out_specs=[blk] * 4,
        compiler_params=pltpu.CompilerParams(dimension_semantics=("parallel",)),
        name=name,
    )(w, m, v, landing)


def _adamw_parts(w, m, v, own, others, chip, name):
    r, _, c = w.shape
    tc = _col_tile(c)

    def body(q_ref, w_ref, m_ref, v_ref, a_ref, b_ref, g_ref, d_ref, m2_ref, v2_ref):
        g = ((a_ref[...] + b_ref[0].astype(F32)) + b_ref[1].astype(F32)) + b_ref[2].astype(F32)
        delta, m2, v2 = _adamw_math(w_ref[...], g, m_ref[...], v_ref[...])
        g_ref[...] = g
        d_ref[...] = delta
        m2_ref[...] = m2
        v2_ref[...] = v2

    row = pl.BlockSpec((r, None, tc), lambda j, q: (0, 0, j))
    return pl.pallas_call(
        body,
        out_shape=[jax.ShapeDtypeStruct((r, 1, c), F32)] * 4,
        grid_spec=pltpu.PrefetchScalarGridSpec(
            num_scalar_prefetch=1, grid=(c // tc,),
            in_specs=[row, row, row, pl.BlockSpec((None, r, tc), lambda j, q: (q[0], 0, j)),
                      pl.BlockSpec((3, r, tc), lambda j, q: (0, 0, j))],
            out_specs=[row] * 4),
        compiler_params=pltpu.CompilerParams(dimension_semantics=("parallel",)),
        name=name,
    )(chip, w, m, v, own, others)


def _adamw_small(gathered, params):
    ns = len(_SMALL)

    def body(*refs):
        g_ref, p_refs, o_refs = refs[0], refs[1:1 + 3 * ns], refs[1 + 3 * ns:]
        tot = g_ref[0]
        for k in range(1, N_DEV):
            tot = tot + g_ref[k]
        for i, name in enumerate(_SMALL):
            row, lane0, lanes = _SMALL_SLOTS[name]
            g = tot[row:row + 1, lane0:lane0 + lanes]
            w_, m_, v_ = (p_refs[3 * i + j][...] for j in range(3))
            delta, m2, v2 = _adamw_math(w_, g, m_, v_)
            for j, val in enumerate((g, delta, m2, v2)):
                o_refs[4 * i + j][...] = val
        o_refs[4 * ns][...] = tot[_LOSS_ROW:_LOSS_ROW + 1, 0:LANES]
        o_refs[4 * ns + 1][...] = tot[_CONV_ROW0:_CONV_ROW0 + _CONV_ROWS, :]

    out_shape = [jax.ShapeDtypeStruct(w.shape, F32) for (w, _, _) in params for _ in range(4)]
    out_shape += [jax.ShapeDtypeStruct((1, LANES), F32), jax.ShapeDtypeStruct((_CONV_ROWS, FLAT_COLS), F32)]
    flat = [a for wmv in params for a in wmv]
    return pl.pallas_call(body, out_shape=out_shape, name="adamw_small")(gathered, *flat)


def kernel(x, p, positions, w_in, conv_w, dn_a_log, dn_dt_bias, dn_norm_w, q_norm_w, w_uq, kv_norm_w, w_uk, w_uv, w_br_dn, w_br_mla, w_o, ln1_g, ln1_b, w_ffn_in, w_ffn_out, w_ple, w_ple_gate, ln2_g, ln2_b, loss_target, m_w_in, m_conv_w, m_dn_a_log, m_dn_dt_bias, m_dn_norm_w, m_q_norm_w, m_w_uq, m_kv_norm_w, m_w_uk, m_w_uv, m_w_br_dn, m_w_br_mla, m_w_o, m_ln1_g, m_ln1_b, m_w_ffn_in, m_w_ffn_out, m_w_ple, m_w_ple_gate, m_ln2_g, m_ln2_b, v_w_in, v_conv_w, v_dn_a_log, v_dn_dt_bias, v_dn_norm_w, v_q_norm_w, v_w_uq, v_kv_norm_w, v_w_uk, v_w_uv, v_w_br_dn, v_w_br_mla, v_w_o, v_ln1_g, v_ln1_b, v_w_ffn_in, v_w_ffn_out, v_w_ple, v_w_ple_gate, v_ln2_g, v_ln2_b):
    args = dict(locals())
    wts = {n: args[n] for n in _ORDER}
    mom1 = {n: args["m_" + n] for n in _ORDER}
    mom2 = {n: args["v_" + n] for n in _ORDER}
    big_names = [n for n, _ in _BIG]
    shard_shapes = {n: wts[n].shape[1:] for n in big_names}
    c_idx = lax.axis_index("c")
    q_idx = 2 * lax.axis_index("x") + lax.axis_index("y")
    parity, chip = c_idx.reshape(1).astype(jnp.int32), q_idx.reshape(1).astype(jnp.int32)

    stored = {n: _to_stored(n, wts[n][0]).astype(BF16) for n in big_names}
    first = _all_gather([stored["w_in"], conv_w[0]], "ag_first")
    group_names = {grp: [n for n, _ in pairs] for grp, pairs in _GROUP_GRADS.items()}
    carry, gathers = first[0], {}
    for grp in ("mix", "ffn"):
        gathers[grp] = _spread_start([stored[n] for n in group_names[grp]], carry, False, "ag_start_" + grp)
        carry = gathers[grp][3]
    conv_full = jnp.moveaxis(first[1], 0, 1).reshape(conv_w.shape[1], -1)
    small_w = {n: wts[n].astype(F32) for n in _SMALL}
    w = _first_weights(carry.reshape(-1, D_MODEL), conv_full, small_w)

    def late_weights(grp, after):
        got = _spread_wait(gathers[grp], after, False, "ag_wait_" + grp)
        return _late_weights(grp, {n: t.reshape(-1, t.shape[-1]) for n, t in zip(group_names[grp], got)})

    started = {}

    def emit(group, g, carry):
        if group == "w_in":
            rows, cols = _stored_shape("w_in", shard_shapes["w_in"])
            packed, from_sibling = _pack_exchange(_w_in_grad_parts(g), "rs_pack_sibling")
            own, own_bf = _chip_sum(packed.reshape(N_DEV, rows, 1, cols), from_sibling, parity, "rs_sum_w_in")
            started["w_in"] = (own, _chips_start([own_bf], carry, "rs_chips_start"))
            return started["w_in"][1][3]
        if group == "small":
            block = _pack_small_grads(_small_grads(g), g["loss"])
            started["small"] = _spread_start([block], carry, False, "ag_start_small")
            return started["small"][3]
        grads = _group_grads(group, g)
        srcs = [grads[n].reshape((N_DEV,) + _stored_shape(n, shard_shapes[n])) for n in grads]
        started[group] = (list(grads), _spread_start(srcs, carry, True, "rs_start_" + group))
        return started[group][1][3]

    s_dim = x.shape[1]
    loss, dx, g = _local_step(x[0], p[0, 0], positions.reshape(s_dim, 1).astype(F32), loss_target[0], w,
                              late_weights, emit)
    own, chips_started = started.pop("w_in")
    small_started = started.pop("small")

    out_g, out_d, out_m, out_v = {}, {}, {}, {}

    def update(n, grad, shp):
        flat2 = (shp[0], int(np.prod(shp[1:])))
        d, m2, v2 = _adamw(wts[n][0].reshape(flat2), mom1[n][0].reshape(flat2), mom2[n][0].reshape(flat2),
                           grad.reshape(flat2), "adamw_" + n)
        out_g[n], out_d[n], out_m[n], out_v[n] = grad, d.reshape(shp), m2.reshape(shp), v2.reshape(shp)

    for group, (names, st) in started.items():
        for n, landing in zip(names, _spread_wait(st, dx, True, "rs_wait_" + group)):
            shp = shard_shapes[n]
            if _BIG_AXIS[n] == 0 or shp[-1] % LANES:
                res = _adamw_sum8(_to_stored(n, wts[n][0]), _to_stored(n, mom1[n][0]), _to_stored(n, mom2[n][0]),
                                  landing, "adamw_" + n)
                out_g[n], out_d[n], out_m[n], out_v[n] = (_from_stored(n, t, shp) for t in res)
                last = res[3]
            else:
                update(n, _from_stored(n, _sum8(landing, "rs_total_" + n), shp), shp)

    from_chips = _chips_wait(chips_started, last, "rs_chips_wait")[0]
    g_small = _spread_wait(small_started, last, False, "ag_wait_small")[0]
    rows_first = lambda a: jnp.transpose(a, (2, 0, 1))
    res = _adamw_parts(rows_first(wts["w_in"]), rows_first(mom1["w_in"]), rows_first(mom2["w_in"]), own, from_chips,
                       chip, "adamw_w_in")
    out_g["w_in"], out_d["w_in"], out_m["w_in"], out_v["w_in"] = (jnp.transpose(t, (1, 2, 0))[0] for t in res)

    res = _adamw_small(g_small, [(wts[n], mom1[n], mom2[n]) for n in _SMALL])
    for i, n in enumerate(_SMALL):
        out_g[n], out_d[n], out_m[n], out_v[n] = res[4 * i:4 * i + 4]
    loss_out = res[4 * len(_SMALL)][0, 0]
    conv_shape = conv_w.shape[1:]
    conv_g = lax.dynamic_slice(res[-1].reshape(conv_shape[0], -1), (0, (2 * q_idx + c_idx) * conv_shape[1]),
                               conv_shape)
    update("conv_w", conv_g, conv_shape)

    expand = lambda d, n: d[n] if n in _SMALL else d[n][None]
    return (loss_out, dx[None], *[expand(out_g, n) for n in _ORDER], *[expand(out_d, n) for n in _ORDER],
            *[expand(out_m, n) for n in _ORDER], *[expand(out_v, n) for n in _ORDER])
```

```python
import functools

import numpy as np
import jax
import jax.numpy as jnp
from jax import lax
from jax.experimental import pallas as pl
from jax.experimental.pallas import tpu as pltpu

F32 = jnp.float32
BF16 = jnp.bfloat16

D_MODEL = 1024
N_HEADS = 8
HEAD = 128
CHUNK = 64
GROUP = 256
ROPE = 64
Q_LORA = 384
KV_LORA = 256
FFN_HIDDEN = 2816
PLE_DIM = 256
ROPE_BASE = 10000.0
ALPHA = 2.0 ** 0.25
SCALE = float((HEAD + ROPE) ** -0.5)
NEG_BIG = -1e30
EPS_RMS = 1e-6
EPS_LN = 1e-5

ADAM_LR = 0.001
ADAM_B1 = 0.9
ADAM_B2 = 0.999
ADAM_EPS = 1e-08
ADAM_WD = 0.01
ADAM_STEP = 10

N_DEV = 8
LANES = 128
FLAT_COLS = 1024

WB_CQ, WB_CKV, WB_KR, WB_BA, WB_COLS = 0, 512, 768, 896, 1024

HIGHEST = lax.Precision.HIGHEST

NN = (((1,), (0,)), ((), ()))
TN = (((0,), (0,)), ((), ()))
NT = (((1,), (1,)), ((), ()))


def _dot(a, b, dims=NN):
    return lax.dot_general(a.astype(BF16), b.astype(BF16), dims, preferred_element_type=F32)


def _dot32(a, b, dims=NN):
    return lax.dot_general(a, b, dims, precision=HIGHEST, preferred_element_type=F32)


def _sig(x):
    return 1.0 / (1.0 + jnp.exp(-x))


MM_TILE = 1536


def _pick_wide(n):
    if n <= MM_TILE:
        return n
    return max(t for t in range(LANES, MM_TILE + 1, LANES) if n % t == 0)


def _split_bf16(a):
    hi = a.astype(BF16)
    return hi, (a - hi.astype(F32)).astype(BF16)


def _dot3(a, b, dims=NN):
    ah, al = a if isinstance(a, tuple) else _split_bf16(a)
    bh, bl = b if isinstance(b, tuple) else _split_bf16(b)
    d = lambda p, q: lax.dot_general(p, q, dims, preferred_element_type=F32)
    return d(ah, bh) + (d(ah, bl) + d(al, bh))


def _mm(a, b, *, ta=False, tb=False, add=(), out_dtype=F32, name):
    if ta:
        k_dim, m_dim = a.shape
    else:
        m_dim, k_dim = a.shape
    if tb:
        n_dim, k2 = b.shape
    else:
        k2, n_dim = b.shape
    assert k_dim == k2, (a.shape, b.shape, ta, tb)
    tm = _pick_wide(m_dim)
    tn = _pick_wide(n_dim)
    tk = _pick_wide(k_dim)
    nk = k_dim // tk
    n_add = len(add)
    dims = TN if ta else (NT if tb else NN)
    assert not (ta and tb)

    def body(a_ref, b_ref, *rest):
        add_refs = rest[:n_add]
        o_ref = rest[n_add]
        acc = rest[n_add + 1]
        k = pl.program_id(2)

        @pl.when(k == 0)
        def _():
            acc[...] = jnp.zeros_like(acc)

        acc[...] += _dot(a_ref[...], b_ref[...], dims)

        @pl.when(k == nk - 1)
        def _():
            r = acc[...]
            for ar in add_refs:
                r = r + ar[...].astype(F32)
            o_ref[...] = r.astype(o_ref.dtype)

    a_spec = pl.BlockSpec((tk, tm), lambda i, j, k: (k, i)) if ta else pl.BlockSpec((tm, tk), lambda i, j, k: (i, k))
    b_spec = pl.BlockSpec((tn, tk), lambda i, j, k: (j, k)) if tb else pl.BlockSpec((tk, tn), lambda i, j, k: (k, j))
    o_spec = pl.BlockSpec((tm, tn), lambda i, j, k: (i, j))
    return pl.pallas_call(
        body,
        out_shape=jax.ShapeDtypeStruct((m_dim, n_dim), out_dtype),
        grid=(m_dim // tm, n_dim // tn, nk),
        in_specs=[a_spec, b_spec] + [o_spec] * n_add,
        out_specs=o_spec,
        scratch_shapes=[pltpu.VMEM((tm, tn), F32)],
        compiler_params=pltpu.CompilerParams(dimension_semantics=("parallel", "parallel", "arbitrary")),
        name=name,
    )(a, b, *add)


def _mm_resident(a, b, *, tb=False, add=(), out_dtype=F32, name):
    m_dim, k_dim = a.shape
    n_dim, k2 = b.shape if tb else b.shape[::-1]
    assert k2 == k_dim
    tm = min(DX_ROWS, m_dim)
    dims = NT if tb else NN

    def body(a_ref, b_ref, *rest):
        r = _dot(a_ref[...], b_ref[...], dims)
        for ar in rest[:-1]:
            r = r + ar[...]
        rest[-1][...] = r.astype(out_dtype)

    o_spec = pl.BlockSpec((tm, n_dim), lambda i: (i, 0))
    return pl.pallas_call(
        body,
        out_shape=jax.ShapeDtypeStruct((m_dim, n_dim), out_dtype),
        grid=(m_dim // tm,),
        in_specs=[pl.BlockSpec((tm, k_dim), lambda i: (i, 0)),
                  pl.BlockSpec(b.shape, lambda i: (0, 0), pipeline_mode=pl.Buffered(1))] + [o_spec] * len(add),
        out_specs=o_spec,
        compiler_params=pltpu.CompilerParams(dimension_semantics=("parallel",)),
        name=name,
    )(a, b, *add)


def _rowwise(fn, rows, consts, outs, accs=(), *, tm=256, name):
    rows = [r if isinstance(r, tuple) else (r, 0, r.shape[1]) for r in rows]
    s_dim = rows[0][0].shape[0]
    tm = min(tm, s_dim)
    assert s_dim % tm == 0 and all(arr.shape[0] == s_dim for arr, _, _ in rows)
    specs = [pl.BlockSpec((tm, width), functools.partial(lambda i, cb: (i, cb), cb=cb)) for _, cb, width in rows]
    args = [arr for arr, _, _ in rows]
    for c in consts:
        specs.append(pl.BlockSpec(c.shape, lambda i: (0, 0)))
        args.append(c)
    nr, nc, no = len(rows), len(consts), len(outs)
    flipped = [len(o) == 3 for o in outs]
    out_shape = [jax.ShapeDtypeStruct((o[0], s_dim) if t else (s_dim, o[0]), o[1]) for o, t in zip(outs, flipped)]
    out_specs = [pl.BlockSpec((o[0], tm), lambda i: (0, i)) if t else pl.BlockSpec((tm, o[0]), lambda i: (i, 0))
                 for o, t in zip(outs, flipped)]
    out_shape += [jax.ShapeDtypeStruct(sh, F32) for sh in accs]
    out_specs += [pl.BlockSpec(sh, lambda i: (0, 0)) for sh in accs]

    def body(*refs):
        r = [x[...].astype(F32) if x.dtype == BF16 else x[...] for x in refs[:nr]]
        c = [x[...] for x in refs[nr:nr + nc]]
        o_refs = refs[nr + nc:nr + nc + no]
        a_refs = refs[nr + nc + no:]
        o_vals, a_vals = fn(r, c)
        for ref, v, t in zip(o_refs, o_vals, flipped, strict=True):
            ref[...] = (jnp.transpose(v.astype(F32)) if t else v).astype(ref.dtype)
        if a_refs:
            @pl.when(pl.program_id(0) == 0)
            def _():
                for ref in a_refs:
                    ref[...] = jnp.zeros_like(ref)

            for ref, v in zip(a_refs, a_vals, strict=True):
                ref[...] += v

    res = pl.pallas_call(
        body,
        out_shape=out_shape,
        grid=(s_dim // tm,),
        in_specs=specs,
        out_specs=out_specs,
        compiler_params=pltpu.CompilerParams(dimension_semantics=("arbitrary" if accs else "parallel",)),
        name=name,
    )(*args)
    return res


def _colsum(v):
    return jnp.sum(v, axis=0, keepdims=True)


def _rowsum(v):
    return jnp.sum(v, axis=1, keepdims=True)


def _rowmean(v):
    return jnp.mean(v, axis=1, keepdims=True)


def _silu_grad(x):
    s = _sig(x)
    return s * (1.0 + x * (1.0 - s))


def _conv_taps(x, w, width=4):
    row = lax.broadcasted_iota(jnp.int32, x.shape, 0)
    c = x * w[width - 1:width, :]
    for s in range(1, width):
        c = c + jnp.where(row >= s, pltpu.roll(x, s, 0), 0.0) * w[width - 1 - s:width - s, :]
    return c


def _conv_fwd(proj_a, conv_w):
    s_dim = proj_a.shape[0]
    n_blk = 3 * N_HEADS

    def body(x_ref, w_ref, o_ref):
        j = pl.program_id(0)
        c = _conv_taps(x_ref[...], w_ref[...])
        y = c * _sig(c)
        r = lax.rsqrt(_rowsum(y * y) + EPS_RMS)
        fac = jnp.where(j < N_HEADS, r * (HEAD ** -0.5), jnp.where(j < 2 * N_HEADS, r, 1.0))
        o_ref[...] = y * fac

    return pl.pallas_call(
        body,
        out_shape=jax.ShapeDtypeStruct((s_dim, n_blk * HEAD), F32),
        grid=(n_blk,),
        in_specs=[pl.BlockSpec((s_dim, HEAD), lambda j: (0, j)), pl.BlockSpec((4, HEAD), lambda j: (0, j))],
        out_specs=pl.BlockSpec((s_dim, HEAD), lambda j: (0, j)),
        compiler_params=pltpu.CompilerParams(dimension_semantics=("parallel",)),
        name="conv_fwd",
    )(proj_a, conv_w)


def _conv_bwd(proj_a, conv_w, dq, dk, dv):
    s_dim = proj_a.shape[0]
    n_blk = 3 * N_HEADS

    def body(x_ref, w_ref, dq_ref, dk_ref, dv_ref, dx_ref, dw_ref):
        j = pl.program_id(0)
        x = x_ref[...]
        w = w_ref[...]
        do = jnp.where(j < N_HEADS, dq_ref[...], jnp.where(j < 2 * N_HEADS, dk_ref[...], dv_ref[...]))
        c = _conv_taps(x, w)
        sg = _sig(c)
        y = c * sg
        r = lax.rsqrt(_rowsum(y * y) + EPS_RMS)
        sc = jnp.where(j < N_HEADS, HEAD ** -0.5, 1.0)
        dy_n = sc * (r * do - y * (r * r * r) * _rowsum(do * y))
        dy = jnp.where(j < 2 * N_HEADS, dy_n, do)
        dc = dy * (sg * (1.0 + c * (1.0 - sg)))
        row = lax.broadcasted_iota(jnp.int32, x.shape, 0)
        dx = dc * w[3:4, :]
        dw_ref[3:4, :] = _colsum(dc * x)
        for s in range(1, 4):
            dx = dx + jnp.where(row < s_dim - s, pltpu.roll(dc, s_dim - s, 0), 0.0) * w[3 - s:4 - s, :]
            xs = jnp.where(row >= s, pltpu.roll(x, s, 0), 0.0)
            dw_ref[3 - s:4 - s, :] = _colsum(dc * xs)
        dx_ref[...] = dx.astype(dx_ref.dtype)

    hd = N_HEADS - 1
    return pl.pallas_call(
        body,
        out_shape=[jax.ShapeDtypeStruct((s_dim, n_blk * HEAD), BF16), jax.ShapeDtypeStruct((4, n_blk * HEAD), F32)],
        grid=(n_blk,),
        in_specs=[
            pl.BlockSpec((s_dim, HEAD), lambda j: (0, j)),
            pl.BlockSpec((4, HEAD), lambda j: (0, j)),
            pl.BlockSpec((s_dim, HEAD), lambda j: (0, jnp.minimum(j, hd))),
            pl.BlockSpec((s_dim, HEAD), lambda j: (0, jnp.clip(j - N_HEADS, 0, hd))),
            pl.BlockSpec((s_dim, HEAD), lambda j: (0, jnp.clip(j - 2 * N_HEADS, 0, hd))),
        ],
        out_specs=[pl.BlockSpec((s_dim, HEAD), lambda j: (0, j)), pl.BlockSpec((4, HEAD), lambda j: (0, j))],
        compiler_params=pltpu.CompilerParams(dimension_semantics=("parallel",)),
        name="conv_bwd",
    )(proj_a, conv_w, dq, dk, dv)


def _chunk_tri(n):
    r = np.arange(n)
    m = ((r[:, None] // CHUNK) == (r[None, :] // CHUNK)) & (r[:, None] >= r[None, :])
    m = m.astype(np.float32)
    return jnp.asarray(m), jnp.asarray(m.T)


def _softplus(z):
    return jnp.maximum(z, 0.0) + jnp.log(1.0 + jnp.exp(-jnp.abs(z)))


def _gates_fwd(proj_b, alog, dtb):
    tm = min(GROUP, proj_b.shape[0])
    tri, _ = _chunk_tri(tm)

    def fn(r, c):
        b = r[0]
        a = pltpu.roll(b, LANES - N_HEADS, 1)
        alog_, dtb_, tri_ = c
        g = -jnp.exp(alog_) * _softplus(a + dtb_)
        return [_sig(b), _dot32(tri_, g)], []

    return _rowwise(fn, [(proj_b, WB_BA // LANES, LANES)], [alog, dtb, tri],
                    [(LANES, F32), (LANES, F32)], tm=tm, name="gates_fwd")


def _gates_bwd(proj_b, alog, dtb, gc, d_beta, d_gc, d_egl_rows):
    tm = min(GROUP, proj_b.shape[0])
    _, tri_t = _chunk_tri(tm)

    def fn(r, c):
        b, gc_, d_beta_, d_gc_, d_egl_ = r
        a = pltpu.roll(b, LANES - N_HEADS, 1)
        alog_, dtb_, tri_t_ = c
        z = a + dtb_
        ea = jnp.exp(alog_)
        g = -ea * _softplus(z)
        dg = _dot32(tri_t_, d_gc_ + d_egl_ * jnp.exp(gc_))
        d_a = dg * (-ea) * _sig(z)
        beta = _sig(b)
        d_ba = d_beta_ * beta * (1.0 - beta) + pltpu.roll(d_a, N_HEADS, 1)
        return [d_ba], [_colsum(dg * g), _colsum(d_a)]

    return _rowwise(fn, [(proj_b, WB_BA // LANES, LANES), gc, d_beta, d_gc, d_egl_rows],
                    [alog, dtb, tri_t], [(LANES, BF16)], accs=[(1, LANES), (1, LANES)], tm=tm,
                    name="gates_bwd")


def _group_masks(n):
    r = lax.broadcasted_iota(jnp.int32, (n, n), 0)
    c = lax.broadcasted_iota(jnp.int32, (n, n), 1)
    same = (r // CHUNK) == (c // CHUNK)
    below, s = [], 2
    while s < CHUNK:
        below.append(jnp.logical_and((r // (2 * s)) == (c // (2 * s)),
                                     jnp.logical_and((r // s) % 2 == 1, (c // s) % 2 == 0)))
        s *= 2
    return dict(same=same, tril=jnp.logical_and(same, r >= c), strict=jnp.logical_and(same, r > c),
                last=c == (r // CHUNK) * CHUNK + (CHUNK - 1), eye=r == c, pair=(r // 2) == (c // 2), below=below)


def _inv_unit_lower(l_mats, mk):
    eye_f = mk["eye"].astype(F32)
    ts = [eye_f - jnp.where(mk["pair"], l_mat, 0.0) for l_mat in l_mats]
    for below in mk["below"]:
        halves = [_split_bf16(t) for t in ts]
        mids = [_dot3(h, jnp.where(below, l_mat, 0.0)) for h, l_mat in zip(halves, l_mats)]
        ts = [t - _dot3(m, h) for t, m, h in zip(ts, mids, halves)]
    return ts


def _unfold_blocks(folded, mask):
    n = folded.shape[0]
    return jnp.where(mask, jnp.concatenate([folded] * (n // CHUNK), axis=1), 0.0)


def _head_cols(beta, gc, gc_t, h):
    lane = lax.broadcasted_iota(jnp.int32, beta.shape, 1)
    sub = lax.broadcasted_iota(jnp.int32, gc_t.shape, 0)
    bcol = _rowsum(jnp.where(lane == h, beta, 0.0))
    gcol = _rowsum(jnp.where(lane == h, gc, 0.0))
    grow = _colsum(jnp.where(sub == h, gc_t, 0.0))
    return bcol, gcol, grow


def _prep_common(q, k, bcol, gcol, grow, mk, t_folded=None):
    n = q.shape[0]
    tril = mk["tril"]
    decay = jnp.where(tril, jnp.exp(jnp.where(tril, gcol - grow, 0.0)), 0.0)
    glast = _rowsum(jnp.where(mk["last"], jnp.broadcast_to(grow, (n, n)), 0.0))
    e = jnp.exp(gcol)
    ekt = jnp.exp(glast - gcol)
    kb = k * bcol
    kk = _dot(kb, k, NT)
    qk = _dot(q, k, NT)
    p = dict(decay=decay, e=e, ekt=ekt, kb=kb, kk=kk, qk=qk)
    if t_folded is not None:
        p["t"] = _unfold_blocks(t_folded, mk["same"])
    return p


GROUPS_PER_STEP = 4
SCAN_CHUNKS_PER_STEP = 4


def _fold_blocks(m):
    n = m.shape[0]
    out = m[:, 0:CHUNK]
    for b in range(1, n // CHUNK):
        out = out + m[:, b * CHUNK:(b + 1) * CHUNK]
    return out


def _gdr_prep_fwd(qkvn, beta, gc, gc_t):
    s_dim = qkvn.shape[0]
    tg = min(GROUP, s_dim)
    n_sub = min(GROUPS_PER_STEP, s_dim // tg)
    tb = tg * n_sub

    def body(q_ref, k_ref, v_ref, b_ref, g_ref, gt_ref, u_ref, w_ref, qd_ref, kt_ref, a_ref, t_ref):
        h = pl.program_id(0)
        mk = _group_masks(tg)
        parts = []
        for s in range(n_sub):
            rows = slice(s * tg, (s + 1) * tg)
            q, k, v = q_ref[rows, :], k_ref[rows, :], v_ref[rows, :]
            bcol, gcol, grow = _head_cols(b_ref[rows, :], g_ref[rows, :], gt_ref[:, rows], h)
            p = _prep_common(q, k, bcol, gcol, grow, mk)
            qd_ref[rows, :] = q * p["e"]
            kt_ref[rows, :] = k * p["ekt"]
            a_ref[rows, :] = _fold_blocks(jnp.where(mk["tril"], p["qk"] * p["decay"], 0.0))
            parts.append((rows, v * bcol, p["kb"] * p["e"], jnp.where(mk["strict"], p["kk"] * p["decay"], 0.0)))
        t_mats = _inv_unit_lower([part[3] for part in parts], mk)
        for (rows, vb, kbe, _), t_mat in zip(parts, t_mats):
            u_ref[rows, :] = _dot(t_mat, vb)
            w_ref[rows, :] = _dot(t_mat, kbe)
            t_ref[rows, :] = _fold_blocks(t_mat)

    row = lambda off: pl.BlockSpec((tb, HEAD), functools.partial(lambda h, m, off: (m, h + off), off=off))
    full = pl.BlockSpec((tb, LANES), lambda h, m: (m, 0))
    o_spec = pl.BlockSpec((tb, HEAD), lambda h, m: (m, h))
    a_spec = pl.BlockSpec((None, tb, CHUNK), lambda h, m: (h, m, 0))
    wide = jax.ShapeDtypeStruct((s_dim, N_HEADS * HEAD), F32)
    folded = jax.ShapeDtypeStruct((N_HEADS, s_dim, CHUNK), F32)
    return pl.pallas_call(
        body,
        out_shape=[wide, wide, wide, wide, folded, folded],
        grid=(N_HEADS, s_dim // tb),
        in_specs=[row(0), row(N_HEADS), row(2 * N_HEADS), full, full, pl.BlockSpec((8, tb), lambda h, m: (0, m))],
        out_specs=[o_spec, o_spec, o_spec, o_spec, a_spec, a_spec],
        compiler_params=pltpu.CompilerParams(dimension_semantics=("parallel", "parallel")),
        name="gdr_prep_fwd",
    )(qkvn, qkvn, qkvn, beta, gc, gc_t)


def _gdr_prep_bwd(qkvn, beta, gc, gc_t, t_fold, u, w, du, dw, dqd, dkt, d_a):
    s_dim = qkvn.shape[0]
    tg = min(GROUP, s_dim)
    n_sub = min(GROUPS_PER_STEP, s_dim // tg)
    tb = tg * n_sub

    def body(q_ref, k_ref, v_ref, b_ref, g_ref, gt_ref, t_ref, u_ref, w_ref, du_ref, dw_ref, dqd_ref, dkt_ref,
             da_ref, dq_ref, dk_ref, dv_ref, db_ref, dg_ref):
        h = pl.program_id(1)

        @pl.when(h == 0)
        def _():
            db_ref[...] = jnp.zeros_like(db_ref)
            dg_ref[...] = jnp.zeros_like(dg_ref)

        mk = _group_masks(tg)
        lane = lax.broadcasted_iota(jnp.int32, (tg, LANES), 1)
        for s in range(n_sub):
            rows = slice(s * tg, (s + 1) * tg)
            q, k, v = q_ref[rows, :], k_ref[rows, :], v_ref[rows, :]
            bcol, gcol, grow = _head_cols(b_ref[rows, :], g_ref[rows, :], gt_ref[:, rows], h)
            p = _prep_common(q, k, bcol, gcol, grow, mk, t_ref[rows, :])
            t_mat, decay, e, ekt, kb = p["t"], p["decay"], p["e"], p["ekt"], p["kb"]
            du_, dw_, dqd_, dkt_ = du_ref[rows, :], dw_ref[rows, :], dqd_ref[rows, :], dkt_ref[rows, :]
            dvb = _dot(t_mat, du_, TN)
            dkbe = _dot(t_mat, dw_, TN)
            d_l = -(_dot(dvb, u_ref[rows, :], NT) + _dot(dkbe, w_ref[rows, :], NT))
            m1 = jnp.where(mk["strict"], d_l, 0.0)
            m2 = _unfold_blocks(da_ref[rows, :], mk["tril"])
            d_kk = m1 * decay
            d_qk = m2 * decay
            d_decay = m1 * p["kk"] + m2 * p["qk"]
            dkb = _dot(d_kk, k) + dkbe * e
            dk = _dot(d_kk, kb, TN) + _dot(d_qk, q, TN) + dkt_ * ekt + dkb * bcol
            dq = _dot(d_qk, k) + dqd_ * e
            d_beta = _rowsum(dkb * k) + _rowsum(dvb * v)
            d_e = _rowsum(dkbe * kb) + _rowsum(dqd_ * q)
            d_ekt = _rowsum(dkt_ * k) * ekt
            d_diff = d_decay * decay
            d_grow = -_colsum(d_diff) + _colsum(jnp.where(mk["last"], jnp.broadcast_to(d_ekt, (tg, tg)), 0.0))
            d_gcol = d_e * e - d_ekt + _rowsum(d_diff)
            d_gcol = d_gcol + _rowsum(jnp.where(mk["eye"], jnp.broadcast_to(d_grow, (tg, tg)), 0.0))
            dq_ref[rows, :] = dq
            dk_ref[rows, :] = dk
            dv_ref[rows, :] = dvb * bcol
            db_ref[rows, :] = jnp.where(lane == h, d_beta, db_ref[rows, :])
            dg_ref[rows, :] = jnp.where(lane == h, d_gcol, dg_ref[rows, :])

    row = lambda off: pl.BlockSpec((tb, HEAD), functools.partial(lambda m, h, off: (m, h + off), off=off))
    full = pl.BlockSpec((tb, LANES), lambda m, h: (m, 0))
    o_spec = pl.BlockSpec((tb, HEAD), lambda m, h: (m, h))
    a_spec = pl.BlockSpec((None, tb, CHUNK), lambda m, h: (h, m, 0))
    wide = jax.ShapeDtypeStruct((s_dim, N_HEADS * HEAD), F32)
    lanes = jax.ShapeDtypeStruct((s_dim, LANES), F32)
    return pl.pallas_call(
        body,
        out_shape=[wide, wide, wide, lanes, lanes],
        grid=(s_dim // tb, N_HEADS),
        in_specs=[row(0), row(N_HEADS), row(2 * N_HEADS), full, full, pl.BlockSpec((8, tb), lambda m, h: (0, m)),
                  a_spec, o_spec, o_spec, o_spec, o_spec, o_spec, o_spec, a_spec],
        out_specs=[o_spec, o_spec, o_spec, full, full],
        compiler_params=pltpu.CompilerParams(dimension_semantics=("parallel", "arbitrary")),
        name="gdr_prep_bwd",
    )(qkvn, qkvn, qkvn, beta, gc, gc_t, t_fold, u, w, du, dw, dqd, dkt, d_a)


def _gdr_scan_fwd(u, w, qd, kt, a_mat, gc):
    s_dim = u.shape[0]
    n_chunks = s_dim // CHUNK
    per = min(SCAN_CHUNKS_PER_STEP, n_chunks)
    tb = per * CHUNK

    def body(u_ref, w_ref, qd_ref, kt_ref, a_ref, g_ref, o_ref, st_ref, state):
        @pl.when(pl.program_id(0) == 0)
        def _():
            state[...] = jnp.zeros_like(state)

        heads = range(N_HEADS)
        cols = [slice(h * HEAD, (h + 1) * HEAD) for h in heads]
        for i in range(per):
            rows = slice(i * CHUNK, (i + 1) * CHUNK)
            egl = jnp.exp(g_ref[(i + 1) * CHUNK - 1:(i + 1) * CHUNK, :])
            s_b = [state[h].astype(BF16) for h in heads]
            for h in heads:
                st_ref[i, h] = state[h]
            ws = [_dot(w_ref[rows, cs], s) for cs, s in zip(cols, s_b)]
            qs = [_dot(qd_ref[rows, cs], s) for cs, s in zip(cols, s_b)]
            vns = [(u_ref[rows, cs] - ws_h).astype(BF16) for cs, ws_h in zip(cols, ws)]
            avs = [_dot(a_ref[h, rows, :], vn) for h, vn in zip(heads, vns)]
            kvs = [_dot(kt_ref[rows, cs], vn, TN) for cs, vn in zip(cols, vns)]
            for h, cs in zip(heads, cols):
                o_ref[rows, cs] = qs[h] + avs[h]
                state[h] = state[h] * egl[:, h:h + 1] + kvs[h]

    wide = pl.BlockSpec((tb, N_HEADS * HEAD), lambda n: (n, 0))
    return pl.pallas_call(
        body,
        out_shape=[jax.ShapeDtypeStruct((s_dim, N_HEADS * HEAD), F32),
                   jax.ShapeDtypeStruct((n_chunks, N_HEADS, HEAD, HEAD), F32)],
        grid=(n_chunks // per,),
        in_specs=[wide, wide, wide, wide, pl.BlockSpec((N_HEADS, tb, CHUNK), lambda n: (0, n, 0)),
                  pl.BlockSpec((tb, LANES), lambda n: (n, 0))],
        out_specs=[wide, pl.BlockSpec((per, N_HEADS, HEAD, HEAD), lambda n: (n, 0, 0, 0))],
        scratch_shapes=[pltpu.VMEM((N_HEADS, HEAD, HEAD), F32)],
        compiler_params=pltpu.CompilerParams(dimension_semantics=("arbitrary",)),
        name="gdr_scan_fwd",
    )(u, w, qd, kt, a_mat, gc)


def _gdr_scan_bwd(u, w, qd, kt, a_mat, gc, states, d_o):
    s_dim = u.shape[0]
    n_chunks = s_dim // CHUNK
    per = min(SCAN_CHUNKS_PER_STEP, n_chunks)
    tb = per * CHUNK
    last = n_chunks // per - 1

    def body(u_ref, w_ref, qd_ref, kt_ref, a_ref, g_ref, st_ref, do_ref,
             du_ref, dw_ref, dqd_ref, dkt_ref, da_ref, de_ref, d_state):
        @pl.when(pl.program_id(0) == 0)
        def _():
            d_state[...] = jnp.zeros_like(d_state)

        heads = range(N_HEADS)
        cols = [slice(h * HEAD, (h + 1) * HEAD) for h in heads]
        for i in reversed(range(per)):
            rows = slice(i * CHUNK, (i + 1) * CHUNK)
            egl = jnp.exp(g_ref[(i + 1) * CHUNK - 1:(i + 1) * CHUNK, :])
            s_b = [st_ref[i, h].astype(BF16) for h in heads]
            ds_b = [d_state[h].astype(BF16) for h in heads]
            dos = [do_ref[rows, cs].astype(BF16) for cs in cols]
            w_b = [w_ref[rows, cs].astype(BF16) for cs in cols]
            ws = [_dot(w_h, s) for w_h, s in zip(w_b, s_b)]
            ados = [_dot(a_ref[h, rows, :], do, TN) for h, do in zip(heads, dos)]
            kds = [_dot(kt_ref[rows, cs], ds) for cs, ds in zip(cols, ds_b)]
            dqds = [_dot(do, s, NT) for do, s in zip(dos, s_b)]
            qdos = [_dot(qd_ref[rows, cs], do, TN) for cs, do in zip(cols, dos)]
            vns = [(u_ref[rows, cs] - ws_h).astype(BF16) for cs, ws_h in zip(cols, ws)]
            dvns = [a + k_ for a, k_ in zip(ados, kds)]
            dvn_b = [d.astype(BF16) for d in dvns]
            das = [_dot(do, vn, NT) for do, vn in zip(dos, vns)]
            dkts = [_dot(vn, ds, NT) for vn, ds in zip(vns, ds_b)]
            dws = [_dot(d, s, NT) for d, s in zip(dvn_b, s_b)]
            wds = [_dot(w_h, d, TN) for w_h, d in zip(w_b, dvn_b)]
            for h, cs in zip(heads, cols):
                ds_n = d_state[h]
                de = jnp.sum(_rowsum(ds_n * st_ref[i, h]), axis=0, keepdims=True)
                de_ref[i, h:h + 1, :] = jnp.broadcast_to(de, (1, LANES))
                dqd_ref[rows, cs] = dqds[h]
                da_ref[h, rows, :] = das[h]
                dkt_ref[rows, cs] = dkts[h]
                du_ref[rows, cs] = dvns[h]
                dw_ref[rows, cs] = -dws[h]
                d_state[h] = ds_n * egl[:, h:h + 1] + qdos[h] - wds[h]

    wide = pl.BlockSpec((tb, N_HEADS * HEAD), lambda n: (last - n, 0))
    a_spec = pl.BlockSpec((N_HEADS, tb, CHUNK), lambda n: (0, last - n, 0))
    wide_shape = jax.ShapeDtypeStruct((s_dim, N_HEADS * HEAD), F32)
    return pl.pallas_call(
        body,
        out_shape=[wide_shape, wide_shape, wide_shape, wide_shape,
                   jax.ShapeDtypeStruct((N_HEADS, s_dim, CHUNK), F32),
                   jax.ShapeDtypeStruct((n_chunks, N_HEADS, LANES), F32)],
        grid=(n_chunks // per,),
        in_specs=[wide, wide, wide, wide, a_spec, pl.BlockSpec((tb, LANES), lambda n: (last - n, 0)),
                  pl.BlockSpec((per, N_HEADS, HEAD, HEAD), lambda n: (last - n, 0, 0, 0)), wide],
        out_specs=[wide, wide, wide, wide, a_spec, pl.BlockSpec((per, N_HEADS, LANES), lambda n: (last - n, 0, 0))],
        scratch_shapes=[pltpu.VMEM((N_HEADS, HEAD, HEAD), F32)],
        compiler_params=pltpu.CompilerParams(dimension_semantics=("arbitrary",)),
        name="gdr_scan_bwd",
    )(u, w, qd, kt, a_mat, gc, states, d_o)


FUSED_ROWS = 512


def _gdr_out_fwd(o_dn, proj_a, dn_w, w_br):
    def fn(r, c):
        o, z = r
        w_, w_br_ = c
        outs = []
        for h in range(N_HEADS):
            cs = slice(h * HEAD, (h + 1) * HEAD)
            oh, zh = o[:, cs], z[:, cs]
            rr = lax.rsqrt(_rowmean(oh * oh) + EPS_RMS)
            outs.append(oh * rr * w_ * (zh * _sig(zh)))
        og = jnp.concatenate(outs, axis=1).astype(BF16)
        return [og, _dot(og, w_br_)], []

    return _rowwise(fn, [o_dn, (proj_a, 3, D_MODEL)], [dn_w, w_br], [(D_MODEL, BF16), (D_MODEL, BF16)],
                    tm=FUSED_ROWS, name="gdr_out_fwd")


def _gdr_out_bwd(o_dn, proj_a, d_y_dn, dn_w, w_br):
    def fn(r, c):
        o, z, dy = r
        w_, w_br_ = c
        dg = _dot(dy, w_br_, NT)
        d_o, d_z = [], []
        d_w = jnp.zeros((1, HEAD), F32)
        for h in range(N_HEADS):
            cs = slice(h * HEAD, (h + 1) * HEAD)
            oh, zh, dgh = o[:, cs], z[:, cs], dg[:, cs]
            rr = lax.rsqrt(_rowmean(oh * oh) + EPS_RMS)
            sz = zh * _sig(zh)
            d_n = dgh * sz
            d_z.append(dgh * (oh * rr * w_) * _silu_grad(zh))
            d_w = d_w + _colsum(d_n * oh * rr)
            gw = d_n * w_
            d_o.append(rr * gw - oh * (rr * rr * rr) * _rowmean(gw * oh))
        return [jnp.concatenate(d_o, axis=1), jnp.concatenate(d_z, axis=1)], [d_w]

    return _rowwise(fn, [o_dn, (proj_a, 3, D_MODEL), d_y_dn], [dn_w, w_br], [(D_MODEL, F32), (D_MODEL, BF16)],
                    accs=[(1, HEAD)], tm=FUSED_ROWS, name="gdr_out_bwd")


def _rms_fwd(x, w):
    r = lax.rsqrt(_rowmean(x * x) + EPS_RMS)
    return x * r * w


def _rms_bwd(x, w, dy):
    r = lax.rsqrt(_rowmean(x * x) + EPS_RMS)
    gw = dy * w
    return r * gw - x * (r * r * r) * _rowmean(gw * x), _colsum(dy * x * r)


def _rope_consts():
    inv = ROPE_BASE ** (-np.arange(0, ROPE, 2, dtype=np.float32) / ROPE)
    t = np.zeros((4, LANES), np.float32)
    t[0, :32] = inv
    t[0, 32:64] = inv
    t[1, :64] = 1.0
    t[2, 32:64] = 1.0
    t[3, :32] = -1.0
    return jnp.asarray(t)


def _rope_tables(pos, consts, width):
    ang = pos * consts[0:1, :]
    cosv, sinv = jnp.cos(ang), jnp.sin(ang)
    reps = width // LANES
    tile = (lambda t: jnp.concatenate([t] * reps, axis=1)) if reps > 1 else (lambda t: t)
    return tile(cosv * consts[1:2, :]), tile(sinv * consts[2:3, :]), tile(sinv * consts[3:4, :])


def _rope_apply(t, tabs):
    cos_t, sin_a, sin_b = tabs
    width = t.shape[1]
    return t * cos_t + pltpu.roll(t, 32, 1) * sin_a + pltpu.roll(t, width - 32, 1) * sin_b


def _rope_transpose(d, tabs):
    cos_t, sin_a, sin_b = tabs
    width = d.shape[1]
    return d * cos_t + pltpu.roll(d * sin_a, width - 32, 1) + pltpu.roll(d * sin_b, 32, 1)


QK_HEAD = 2 * HEAD


def _interleave_heads(a, b):
    parts = []
    for h in range(N_HEADS):
        parts.append(a[:, h * HEAD:(h + 1) * HEAD])
        parts.append(b if b.shape[1] == LANES else b[:, h * LANES:(h + 1) * LANES])
    return jnp.concatenate(parts, axis=1)


def _mla_rows(proj_b):
    return [(proj_b, WB_CQ // Q_LORA, Q_LORA), (proj_b, WB_CKV // KV_LORA, KV_LORA), (proj_b, WB_KR // LANES, LANES)]


def _mla_prep_fwd(proj_b, pos, qn_w, kvn_w, uq, uk, uv):
    def fn(r, c):
        cq, ckv, kr, pos_ = r
        qn_w_, kvn_w_, uq_, uk_, uv_, rope = c
        c_q = _rms_fwd(cq, qn_w_).astype(BF16)
        c_kv = _rms_fwd(ckv, kvn_w_).astype(BF16)
        qf = _dot(c_q, uq_)
        qr = _rope_apply(qf[:, D_MODEL:], _rope_tables(pos_, rope, D_MODEL))
        kr = _rope_apply(kr, _rope_tables(pos_, rope, LANES))
        kc = _interleave_heads(_dot(c_kv, uk_), kr)
        v = _dot(c_kv, uv_)
        return [c_q, c_kv, _interleave_heads(qf[:, :D_MODEL], qr) * SCALE, kc, v, kc, v], []

    wide2 = N_HEADS * QK_HEAD
    return _rowwise(fn, _mla_rows(proj_b) + [pos], [qn_w, kvn_w, uq, uk, uv, _rope_consts()],
                    [(Q_LORA, BF16), (KV_LORA, BF16), (wide2, BF16), (wide2, BF16), (D_MODEL, BF16),
                     (wide2, BF16, "T"), (D_MODEL, BF16, "T")], tm=FUSED_ROWS, name="mla_prep_fwd")


def _mla_prep_bwd(proj_b, pos, d_qc, d_kc, d_v, qn_w, kvn_w, uq, uk, uv):
    def fn(r, c):
        cq, ckv, _, pos_, dq, dk, dv = r
        qn_w_, kvn_w_, uq_, uk_, uv_, rope = c
        even = lambda t: jnp.concatenate([t[:, (2 * h) * LANES:(2 * h + 1) * LANES] for h in range(N_HEADS)], axis=1)
        odd = lambda t: jnp.concatenate([t[:, (2 * h + 1) * LANES:(2 * h + 2) * LANES] for h in range(N_HEADS)], axis=1)
        d_qr_raw = _rope_transpose(odd(dq), _rope_tables(pos_, rope, D_MODEL)) * SCALE
        d_qf = jnp.concatenate([even(dq) * SCALE, d_qr_raw], axis=1).astype(BF16)
        d_kn = even(dk).astype(BF16)
        dkr = dk[:, LANES:2 * LANES]
        for h in range(1, N_HEADS):
            dkr = dkr + dk[:, (2 * h + 1) * LANES:(2 * h + 2) * LANES]
        d_cq, d_qnw = _rms_bwd(cq, qn_w_, _dot(d_qf, uq_, NT))
        d_ckv, d_kvnw = _rms_bwd(ckv, kvn_w_, _dot(d_kn, uk_, NT) + _dot(dv, uv_, NT))
        return [d_qf, d_kn, d_cq, d_ckv, _rope_transpose(dkr, _rope_tables(pos_, rope, LANES))], [d_qnw, d_kvnw]

    return _rowwise(fn, _mla_rows(proj_b) + [pos, d_qc, d_kc, d_v], [qn_w, kvn_w, uq, uk, uv, _rope_consts()],
                    [(2 * D_MODEL, BF16), (D_MODEL, BF16), (Q_LORA, BF16), (KV_LORA, BF16), (LANES, BF16)],
                    accs=[(1, Q_LORA), (1, KV_LORA)], tm=FUSED_ROWS, name="mla_prep_bwd")


def _causal_mask_t(st, key0, query0):
    key = lax.broadcasted_iota(jnp.int32, st.shape, 0) + key0
    query = lax.broadcasted_iota(jnp.int32, st.shape, 1) + query0
    return jnp.where(key <= query, st, NEG_BIG)


def _attn_tiles(s_dim):
    tq = min(512, s_dim)
    n_chains = 2 if s_dim >= 2 * tq else 1
    return tq, n_chains, min(512, s_dim)


def _diagonal_chains(t, tq, n_chains, tk):
    return [(c, (t + 1) * tk - 1 > c * tq) for c in range(n_chains) if t * tk < (c + 1) * tq]


def _attn_fwd(qc, kc, vt):
    s_dim = qc.shape[0]
    tq, n_chains, tk = _attn_tiles(s_dim)
    tqs = tq * n_chains

    def body(q_ref, k_ref, vt_ref, o_ref, lse_ref, m_s, l_s, acc):
        qi = pl.program_id(1)
        m_s[...] = jnp.full_like(m_s, NEG_BIG)
        l_s[...] = jnp.zeros_like(l_s)
        acc[...] = jnp.zeros_like(acc)

        def make_step(chains):
            def step(j, carry):
                ks = pl.multiple_of(j * tk, tk)
                kb, vtb = k_ref[pl.ds(ks, tk), :], vt_ref[:, pl.ds(ks, tk)]
                cols = [slice(c * tq, (c + 1) * tq) for c, _ in chains]
                sts = [_dot(kb, q_ref[cs, :], NT) for cs in cols]
                sts = [_causal_mask_t(st, j * tk, qi * tqs + c * tq) if masked else st
                       for st, (c, masked) in zip(sts, chains)]
                m_prevs = [m_s[:, cs] for cs in cols]
                m_news = [jnp.maximum(mp, jnp.max(st, axis=0, keepdims=True)) for mp, st in zip(m_prevs, sts)]
                alphas = [jnp.exp(mp - mn) for mp, mn in zip(m_prevs, m_news)]
                pts = [jnp.exp(st - mn) for st, mn in zip(sts, m_news)]
                pvs = [_dot(vtb, pt) for pt in pts]
                for cs, mn, al, pt, pv in zip(cols, m_news, alphas, pts, pvs):
                    l_s[:, cs] = al * l_s[:, cs] + _colsum(pt)
                    m_s[:, cs] = mn
                    acc[:, cs] = acc[:, cs] * al + pv
                return carry
            return step

        below = qi * (tqs // tk)
        lax.fori_loop(0, below, make_step([(c, False) for c in range(n_chains)]), 0)
        for t in range(tqs // tk):
            make_step(_diagonal_chains(t, tq, n_chains, tk))(below + t, 0)
        l = l_s[...]
        o_ref[...] = jnp.transpose(acc[...] / l)
        lse_ref[...] = m_s[...] + jnp.log(l)

    return pl.pallas_call(
        body,
        out_shape=[jax.ShapeDtypeStruct((s_dim, N_HEADS * HEAD), F32), jax.ShapeDtypeStruct((N_HEADS, 1, s_dim), F32)],
        grid=(N_HEADS, s_dim // tqs),
        in_specs=[pl.BlockSpec((tqs, QK_HEAD), lambda h, qi: (qi, h)),
                  pl.BlockSpec((s_dim, QK_HEAD), lambda h, qi: (0, h)),
                  pl.BlockSpec((HEAD, s_dim), lambda h, qi: (h, 0))],
        out_specs=[pl.BlockSpec((tqs, HEAD), lambda h, qi: (qi, h)),
                   pl.BlockSpec((None, 1, tqs), lambda h, qi: (h, 0, qi))],
        scratch_shapes=[pltpu.VMEM((1, tqs), F32), pltpu.VMEM((1, tqs), F32), pltpu.VMEM((HEAD, tqs), F32)],
        compiler_params=pltpu.CompilerParams(dimension_semantics=("parallel", "parallel")),
        name="attn_fwd",
    )(qc, kc, vt)


def _attn_bwd(qc, kc, kct, v, o, d_o, lse):
    s_dim = qc.shape[0]
    tq, n_chains, tk = _attn_tiles(s_dim)
    tqs = tq * n_chains

    def body(q_ref, k_ref, kt_ref, v_ref, o_ref, do_ref, lse_ref, dq_ref, dk_ref, dv_ref, dqt_acc, dv_acc):
        qi = pl.program_id(1)

        @pl.when(qi == 0)
        def _():
            dk_ref[...] = jnp.zeros_like(dk_ref)
            dv_acc[...] = jnp.zeros_like(dv_acc)

        dqt_acc[...] = jnp.zeros_like(dqt_acc)
        do_f = do_ref[...]
        do_all = do_f.astype(BF16)
        q_all = q_ref[...]
        lse_row = lse_ref[...]
        delta_row = _dot3(jnp.ones((8, HEAD), F32), o_ref[...] * do_f, NT)[0:1, :]

        def make_step(chains):
            rows = slice(chains[0][0] * tq, (chains[-1][0] + 1) * tq)

            def step(j, carry):
                ks = pl.multiple_of(j * tk, tk)
                kb, vb, ktb = k_ref[pl.ds(ks, tk), :], v_ref[pl.ds(ks, tk), :], kt_ref[:, pl.ds(ks, tk)]
                cols = [slice(c * tq, (c + 1) * tq) for c, _ in chains]
                sts = [_dot(kb, q_all[cs, :], NT) for cs in cols]
                sts = [_causal_mask_t(st, j * tk, qi * tqs + c * tq) if masked else st
                       for st, (c, masked) in zip(sts, chains)]
                dpts = [_dot(vb, do_all[cs, :], NT) for cs in cols]
                pts = [jnp.exp(st - lse_row[:, cs]) for st, cs in zip(sts, cols)]
                dsts = [(pt * (dpt - delta_row[:, cs])).astype(BF16) for pt, dpt, cs in zip(pts, dpts, cols)]
                pts = [pt.astype(BF16) for pt in pts]
                dqs = [_dot(ktb, dst) for dst in dsts]
                for cs, dq in zip(cols, dqs):
                    dqt_acc[:, cs] += dq
                pt_all = jnp.concatenate(pts, axis=1) if len(chains) > 1 else pts[0]
                dst_all = jnp.concatenate(dsts, axis=1) if len(chains) > 1 else dsts[0]
                dk_ref[pl.ds(ks, tk), :] += _dot(dst_all, q_all[rows, :])
                dv_acc[pl.ds(ks, tk), :] += _dot(pt_all, do_all[rows, :])
                return carry
            return step

        below = qi * (tqs // tk)
        lax.fori_loop(0, below, make_step([(c, False) for c in range(n_chains)]), 0)
        for t in range(tqs // tk):
            make_step(_diagonal_chains(t, tq, n_chains, tk))(below + t, 0)
        dq_ref[...] = jnp.transpose(dqt_acc[...])

        @pl.when(qi == s_dim // tqs - 1)
        def _():
            dv_ref[...] = dv_acc[...].astype(dv_ref.dtype)

    q_spec = pl.BlockSpec((tqs, QK_HEAD), lambda h, qi: (qi, h))
    o_spec = pl.BlockSpec((tqs, HEAD), lambda h, qi: (qi, h))
    k_spec = pl.BlockSpec((s_dim, QK_HEAD), lambda h, qi: (0, h))
    v_spec = pl.BlockSpec((s_dim, HEAD), lambda h, qi: (0, h))
    wide2 = jax.ShapeDtypeStruct((s_dim, N_HEADS * QK_HEAD), F32)
    return pl.pallas_call(
        body,
        out_shape=[wide2, wide2, jax.ShapeDtypeStruct((s_dim, N_HEADS * HEAD), BF16)],
        grid=(N_HEADS, s_dim // tqs),
        in_specs=[q_spec, k_spec, pl.BlockSpec((QK_HEAD, s_dim), lambda h, qi: (h, 0)), v_spec, o_spec, o_spec,
                  pl.BlockSpec((None, 1, tqs), lambda h, qi: (h, 0, qi))],
        out_specs=[q_spec, k_spec, v_spec],
        scratch_shapes=[pltpu.VMEM((QK_HEAD, tqs), F32), pltpu.VMEM((s_dim, HEAD), F32)],
        compiler_params=pltpu.CompilerParams(dimension_semantics=("parallel", "arbitrary")),
        name="attn_bwd",
    )(qc, kc, kct, v, o, d_o, lse)


def _mix_proj_ln1(y_dn, y_mla, proj_g, x, w_o, g, b):
    s_dim = x.shape[0]
    tm = min(512, s_dim)

    def body(yd_ref, ym_ref, g_ref, x_ref, w_ref, lg_ref, lb_ref, mixed_ref, a1_ref, h1_ref, h1b_ref):
        gates = g_ref[...].astype(F32)
        mixed = (_sig(gates[:, :D_MODEL]) * yd_ref[...].astype(F32)
                 + _sig(gates[:, D_MODEL:]) * ym_ref[...].astype(F32)).astype(BF16)
        a1 = _dot(mixed, w_ref[...])
        xh, _ = _ln_stats(ALPHA * x_ref[...] + a1)
        y = xh * lg_ref[...] + lb_ref[...]
        mixed_ref[...] = mixed
        a1_ref[...] = a1
        h1_ref[...] = y
        h1b_ref[...] = y.astype(BF16)

    row = lambda width: pl.BlockSpec((tm, width), lambda i: (i, 0))
    whole = lambda a: pl.BlockSpec(a.shape, lambda i: (0, 0))
    sds = lambda dt: jax.ShapeDtypeStruct((s_dim, D_MODEL), dt)
    return pl.pallas_call(
        body,
        out_shape=[sds(BF16), sds(F32), sds(F32), sds(BF16)],
        grid=(s_dim // tm,),
        in_specs=[row(D_MODEL), row(D_MODEL), row(2 * D_MODEL), row(D_MODEL), whole(w_o), whole(g), whole(b)],
        out_specs=[row(D_MODEL)] * 4,
        compiler_params=pltpu.CompilerParams(dimension_semantics=("parallel",)),
        name="mix_proj_ln1",
    )(y_dn, y_mla, proj_g, x, w_o, g, b)


def _ln1_mix_bwd(x, a1, d_h1, d_pg, y_dn, y_mla, proj_g, g, w_o, w_pg):
    def fn(r, c):
        x_, a1_, dy, dpg, yd, ym, gates = r
        g_, w_o_, w_pg_ = c
        dy = dy + _dot(dpg, w_pg_, NT)
        xh, rr = _ln_stats(ALPHA * x_ + a1_)
        dz = _ln_bwd(dy, xh, rr, g_)
        dz_b = dz.astype(BF16)
        dm = _dot(dz_b, w_o_, NT)
        sd, sm = _sig(gates[:, :D_MODEL]), _sig(gates[:, D_MODEL:])
        d_g = jnp.concatenate([dm * yd * sd * (1.0 - sd), dm * ym * sm * (1.0 - sm)], axis=1)
        return [dz_b, ALPHA * dz, d_g, dm * sd, dm * sm], [_colsum(dy * xh), _colsum(dy)]

    return _rowwise(fn, [x, a1, d_h1, d_pg, y_dn, y_mla, proj_g], [g, w_o, w_pg],
                    [(D_MODEL, BF16), (D_MODEL, F32), (2 * D_MODEL, BF16), (D_MODEL, BF16), (D_MODEL, BF16)],
                    accs=[(1, D_MODEL), (1, D_MODEL)], tm=FUSED_ROWS, name="ln1_mix_bwd")


def _ln_stats(z):
    mu = _rowmean(z)
    zc = z - mu
    r = lax.rsqrt(_rowmean(zc * zc) + EPS_LN)
    return zc * r, r


def _ln_bwd(dy, xh, r, g):
    dxh = dy * g
    return r * (dxh - _rowmean(dxh) - xh * _rowmean(dxh * xh))


def _ffn_in_act(h1b, w_t):
    s_dim, k_dim = h1b.shape
    hidden = w_t.shape[0] // 2
    tm, tn = min(512, s_dim), _pick_wide(hidden)
    nt = hidden // tn

    def body(a_ref, bg_ref, bu_ref, gt_ref, up_ref, act_ref):
        a = a_ref[...]
        gt, up = _dot(a, bg_ref[...], NT), _dot(a, bu_ref[...], NT)
        gt_ref[...] = gt.astype(BF16)
        up_ref[...] = up.astype(BF16)
        act_ref[...] = (gt * _sig(gt) * up).astype(BF16)

    o_spec = pl.BlockSpec((tm, tn), lambda j, i: (i, j))
    sds = jax.ShapeDtypeStruct((s_dim, hidden), BF16)
    return pl.pallas_call(
        body,
        out_shape=[sds, sds, sds],
        grid=(nt, s_dim // tm),
        in_specs=[pl.BlockSpec((tm, k_dim), lambda j, i: (i, 0)), pl.BlockSpec((tn, k_dim), lambda j, i: (j, 0)),
                  pl.BlockSpec((tn, k_dim), lambda j, i: (j + nt, 0))],
        out_specs=[o_spec, o_spec, o_spec],
        compiler_params=pltpu.CompilerParams(dimension_semantics=("parallel", "parallel")),
        name="ffn_in_act",
    )(h1b, w_t, w_t)


def _act_bwd(gt, up, d_act):
    def fn(r, c):
        gt_, up_, da = r
        return [jnp.concatenate([da * up_ * _silu_grad(gt_), da * gt_ * _sig(gt_)], axis=1)], []

    return _rowwise(fn, [gt, up, d_act], [], [(2 * FFN_HIDDEN, BF16)], name="act_bwd")[0]


def _tail(h1, ffn, p, tgt, g, b, w_pg, w_ple_t):
    def fn(r, c):
        h1_, ffn_, p_, t_ = r
        pg_ = _dot(h1_, c[2])
        pp_ = _dot(p_, c[3], NT)
        sp = _sig(pg_)
        xh, rr = _ln_stats(ALPHA * h1_ + ffn_ + sp * pp_)
        y = xh * c[0] + c[1]
        err = y - t_
        dy = err * (1.0 / D_MODEL)
        dz = _ln_bwd(dy, xh, rr, c[0])
        loss = jnp.sum(0.5 * _rowmean(err * err), axis=0, keepdims=True)
        return ([dz, dz * pp_ * sp * (1.0 - sp), dz * sp, ALPHA * dz],
                [_colsum(dy * xh), _colsum(dy), jnp.broadcast_to(loss, (1, LANES))])

    return _rowwise(fn, [h1, ffn, p, tgt], [g, b, w_pg, w_ple_t], [(D_MODEL, BF16)] * 3 + [(D_MODEL, F32)],
                    accs=[(1, D_MODEL), (1, D_MODEL), (1, LANES)], tm=FUSED_ROWS, name="tail")


def _local_step(x, p, pos, tgt, w, late_weights, emit):
    w = dict(w)
    s_dim = x.shape[0]
    pb = p.astype(BF16)
    proj_a, proj_g, proj_b, xb = _input_proj(x, w["w_in_t"], w["wg_t"], w["wb_t"])
    qkvn = _conv_fwd(proj_a, w["conv"])
    beta, gc = _gates_fwd(proj_b, w["alog"], w["dtb"])
    gc_t = jnp.transpose(gc[:, :N_HEADS])
    u, w_, qd, kt, a_mat, t_fold = _gdr_prep_fwd(qkvn, beta, gc, gc_t)
    o_dn, states = _gdr_scan_fwd(u, w_, qd, kt, a_mat, gc)
    w.update(late_weights("mix", o_dn))
    og, y_dn = _gdr_out_fwd(o_dn, proj_a, w["dnw"], w["br_dn"])
    c_q, c_kv, qc, kc, vv, kct, vt = _mla_prep_fwd(proj_b, pos, w["qnw"], w["kvnw"], w["uq"], w["uk"], w["uv"])
    o_mla, lse = _attn_fwd(qc, kc, vt)
    y_mla = _mm_resident(o_mla, w["br_mla"], out_dtype=BF16, name="f_y_mla")
    mixed, a1, h1, h1b = _mix_proj_ln1(y_dn, y_mla, proj_g, x, w["wo"], w["ln1g"], w["ln1b"])
    w.update(late_weights("ffn", a1))
    gt, up, act = _ffn_in_act(h1b, w["ffn_in_t"])
    ffn = _mm_resident(act, w["ffn_out"], name="f_ffn")
    g = {}
    dz2, d_pg, d_pp, dh1a, g["ln2g"], g["ln2b"], loss = _tail(h1, ffn, pb, tgt, w["ln2g"], w["ln2b"],
                                                            w["ple_gate"], w["ple_t"])
    g["ple_t"] = _mm(d_pp, pb, ta=True, out_dtype=BF16, name="b_w_ple")
    g["ple_gate"] = _mm(h1b, d_pg, ta=True, out_dtype=BF16, name="b_w_ple_gate")
    g["ffn_out"] = _mm(act, dz2, ta=True, out_dtype=BF16, name="b_w_ffn_out")
    d_act = _mm_resident(dz2, w["ffn_out"], tb=True, out_dtype=BF16, name="b_act")
    d_gu = _act_bwd(gt, up, d_act)
    g["ffn_in_t"] = _mm(d_gu, h1b, ta=True, out_dtype=BF16, name="b_w_ffn_in")
    d_gu = emit("ffn", g, d_gu)
    d_h1 = _mm_resident(d_gu, w["ffn_in_t"], add=(dh1a,), name="b_h1_ffn")
    dz1, dxa, d_proj_g, d_y_dn, d_y_mla, g["ln1g"], g["ln1b"] = _ln1_mix_bwd(
        x, a1, d_h1, d_pg, y_dn, y_mla, proj_g, w["ln1g"], w["wo"], w["ple_gate"])
    g["wo"] = _mm(mixed, dz1, ta=True, out_dtype=BF16, name="b_w_o")
    g["br_mla"] = _mm(o_mla, d_y_mla, ta=True, out_dtype=BF16, name="b_w_br_mla")
    d_o_mla = _mm_resident(d_y_mla, w["br_mla"], tb=True, out_dtype=BF16, name="b_o_mla")
    d_qc, d_kc, d_v = _attn_bwd(qc, kc, kct, vv, o_mla, d_o_mla, lse)
    d_q_full, d_kn, d_cq, d_ckv, d_kr, g["qnw"], g["kvnw"] = _mla_prep_bwd(
        proj_b, pos, d_qc, d_kc, d_v, w["qnw"], w["kvnw"], w["uq"], w["uk"], w["uv"])
    g["uq"] = _mm(c_q, d_q_full, ta=True, out_dtype=BF16, name="b_w_uq")
    g["uk"] = _mm(c_kv, d_kn, ta=True, out_dtype=BF16, name="b_w_uk")
    g["uv"] = _mm(c_kv, d_v, ta=True, out_dtype=BF16, name="b_w_uv")
    g["br_dn"] = _mm(og, d_y_dn, ta=True, out_dtype=BF16, name="b_w_br_dn")
    d_y_dn = emit("mix", g, d_y_dn)
    d_o_dn, d_z, g["dnw"] = _gdr_out_bwd(o_dn, proj_a, d_y_dn, w["dnw"], w["br_dn"])
    du, dw, dqd, dkt, d_a, d_egl = _gdr_scan_bwd(u, w_, qd, kt, a_mat, gc, states, d_o_dn)
    dq, dk, dv, d_beta, d_gc = _gdr_prep_bwd(qkvn, beta, gc, gc_t, t_fold, u, w_, du, dw, dqd, dkt, d_a)
    d_egl_rows = jnp.pad(d_egl[:, None, :, 0], ((0, 0), (CHUNK - 1, 0), (0, LANES - N_HEADS))).reshape(s_dim, LANES)
    d_ba, g["alog"], g["dtb"] = _gates_bwd(proj_b, w["alog"], w["dtb"], gc, d_beta, d_gc, d_egl_rows)
    d_qkv, g["conv"] = _conv_bwd(proj_a, w["conv"], dq, dk, dv)
    zeros = jnp.zeros((s_dim, WB_CKV - Q_LORA), BF16)
    d_proj_b = jnp.concatenate([d_cq, zeros, d_ckv, d_kr, d_ba], axis=1)
    g["wa_qkv_t"] = _mm(d_qkv, xb, ta=True, name="b_w_qkv")
    g["wa_z_t"] = _mm(d_z, xb, ta=True, name="b_w_z")
    g["wg_t"] = _mm(d_proj_g, xb, ta=True, name="b_w_g")
    g["wb_t"] = _mm(d_proj_b, xb, ta=True, name="b_w_b")
    d_qkv = emit("small", dict(g, loss=loss), emit("w_in", g, d_qkv))
    dx = _input_grad(d_qkv, d_z, d_proj_g, d_proj_b, w["w_in_t"], w["wg_t"], w["wb_t"], dxa)
    return loss, dx, g


DX_ROWS = 512


def _input_proj(x, w_in_t, wg_t, wb_t):
    s_dim = x.shape[0]
    n_a = 4 * D_MODEL

    def body(x_ref, wa_ref, wg_ref, wb_ref, a_ref, g_ref, b_ref, xb_ref):
        xv = x_ref[...].astype(BF16)
        xb_ref[...] = xv
        a_ref[...] = _dot(xv, wa_ref[...], NT)
        g_ref[...] = _dot(xv, wg_ref[...], NT).astype(BF16)
        b_ref[...] = _dot(xv, wb_ref[...], NT)

    rows = lambda width: pl.BlockSpec((DX_ROWS, width), lambda i: (i, 0))
    whole = lambda shape: pl.BlockSpec(shape, lambda i: (0, 0), pipeline_mode=pl.Buffered(1))
    return pl.pallas_call(
        body,
        out_shape=[jax.ShapeDtypeStruct((s_dim, n_a), F32), jax.ShapeDtypeStruct((s_dim, wg_t.shape[0]), BF16),
                   jax.ShapeDtypeStruct((s_dim, wb_t.shape[0]), F32), jax.ShapeDtypeStruct((s_dim, D_MODEL), BF16)],
        grid=(s_dim // DX_ROWS,),
        in_specs=[rows(D_MODEL), whole((n_a, D_MODEL)), whole(wg_t.shape), whole(wb_t.shape)],
        out_specs=[rows(n_a), rows(wg_t.shape[0]), rows(wb_t.shape[0]), rows(D_MODEL)],
        compiler_params=pltpu.CompilerParams(dimension_semantics=("parallel",)),
        name="f_proj",
    )(x, w_in_t, wg_t, wb_t)


def _input_grad(d_qkv, d_z, d_g, d_b, w_in_t, wg_t, wb_t, add):
    s_dim = d_qkv.shape[0]
    n_qkv, n_a = d_qkv.shape[1], d_qkv.shape[1] + d_z.shape[1]

    def body(q_ref, z_ref, g_ref, b_ref, wa_ref, wg_ref, wb_ref, add_ref, o_ref):
        r = add_ref[...] + _dot(q_ref[...], wa_ref[0:n_qkv])
        r = r + _dot(z_ref[...], wa_ref[n_qkv:n_a])
        r = r + _dot(g_ref[...], wg_ref[...])
        o_ref[...] = r + _dot(b_ref[...], wb_ref[...])

    rows = lambda a: pl.BlockSpec((DX_ROWS, a.shape[1]), lambda i: (i, 0))
    whole = lambda shape: pl.BlockSpec(shape, lambda i: (0, 0), pipeline_mode=pl.Buffered(1))
    return pl.pallas_call(
        body,
        out_shape=jax.ShapeDtypeStruct((s_dim, D_MODEL), F32),
        grid=(s_dim // DX_ROWS,),
        in_specs=[rows(d_qkv), rows(d_z), rows(d_g), rows(d_b), whole((n_a, D_MODEL)), whole(wg_t.shape),
                  whole(wb_t.shape), rows(add)],
        out_specs=pl.BlockSpec((DX_ROWS, D_MODEL), lambda i: (i, 0)),
        compiler_params=pltpu.CompilerParams(dimension_semantics=("parallel",)),
        name="b_x",
    )(d_qkv, d_z, d_g, d_b, w_in_t, wg_t, wb_t, add)


_BIG = (("w_in", 1), ("w_uq", 0), ("w_uk", 0), ("w_uv", 0), ("w_br_dn", 0), ("w_br_mla", 0),
        ("w_o", 0), ("w_ffn_in", 1), ("w_ffn_out", 0), ("w_ple", 1), ("w_ple_gate", 0))
_BIG_AXIS = dict(_BIG)
_SMALL = ("ln1_g", "ln1_b", "ln2_g", "ln2_b", "q_norm_w", "kv_norm_w", "dn_norm_w", "dn_a_log", "dn_dt_bias")
_ORDER = ("w_in", "conv_w", "dn_a_log", "dn_dt_bias", "dn_norm_w", "q_norm_w", "w_uq", "kv_norm_w", "w_uk", "w_uv",
          "w_br_dn", "w_br_mla", "w_o", "ln1_g", "ln1_b", "w_ffn_in", "w_ffn_out", "w_ple", "w_ple_gate", "ln2_g",
          "ln2_b")


def _stored_shape(name, shard_shape):
    axis = _BIG_AXIS[name]
    lead = shard_shape[axis]
    return lead, int(np.prod(shard_shape)) // lead


def _to_stored(name, shard):
    return jnp.moveaxis(shard, _BIG_AXIS[name], 0).reshape(_stored_shape(name, shard.shape))


def _from_stored(name, stored, shard_shape):
    axis = _BIG_AXIS[name]
    moved = (shard_shape[axis],) + shard_shape[:axis] + shard_shape[axis + 1:]
    return jnp.moveaxis(stored.reshape(moved), 0, axis)


_W_IN_ROWS = np.cumsum([0, 3072, 1024, 8, 8, Q_LORA, KV_LORA, ROPE, D_MODEL, D_MODEL])


def _first_weights(w_in_t, conv_full, small):
    r = _W_IN_ROWS
    zr = lambda n: jnp.zeros((n, D_MODEL), w_in_t.dtype)
    w = {}
    w["w_in_t"] = w_in_t
    w["wg_t"] = w_in_t[r[7]:r[9]]
    w["wb_t"] = jnp.concatenate([w_in_t[r[4]:r[5]], zr(WB_CKV - Q_LORA), w_in_t[r[5]:r[7]], zr(LANES - ROPE),
                                 w_in_t[r[2]:r[4]], zr(LANES - 2 * N_HEADS)], axis=0)
    w["conv"] = conv_full
    pad_l = lambda v: jnp.pad(v, ((0, 0), (0, LANES - v.shape[1])))
    w["alog"], w["dtb"] = pad_l(small["dn_a_log"]), pad_l(small["dn_dt_bias"])
    w["dnw"], w["qnw"], w["kvnw"] = small["dn_norm_w"], small["q_norm_w"], small["kv_norm_w"]
    w["ln1g"], w["ln1b"], w["ln2g"], w["ln2b"] = small["ln1_g"], small["ln1_b"], small["ln2_g"], small["ln2_b"]
    return w


def _late_weights(group, fw):
    w = {}
    if group == "mix":
        uq = fw["w_uq"].reshape(Q_LORA, N_HEADS, HEAD + ROPE)
        uq_r = jnp.pad(uq[:, :, HEAD:], ((0, 0), (0, 0), (0, HEAD - ROPE)))
        w["uq"] = jnp.concatenate([uq[:, :, :HEAD].reshape(Q_LORA, -1), uq_r.reshape(Q_LORA, -1)], axis=1)
        w["uk"], w["uv"] = fw["w_uk"], fw["w_uv"]
        w["br_dn"], w["br_mla"], w["wo"] = fw["w_br_dn"], fw["w_br_mla"], fw["w_o"]
    else:
        w["ffn_in_t"], w["ffn_out"] = fw["w_ffn_in"], fw["w_ffn_out"]
        w["ple_t"], w["ple_gate"] = fw["w_ple"], fw["w_ple_gate"]
    return w


_GROUP_GRADS = {"ffn": (("w_ple", "ple_t"), ("w_ple_gate", "ple_gate"), ("w_ffn_out", "ffn_out"),
                        ("w_ffn_in", "ffn_in_t")),
                "mix": (("w_o", "wo"), ("w_br_mla", "br_mla"), ("w_uq", "uq"), ("w_uk", "uk"), ("w_uv", "uv"),
                        ("w_br_dn", "br_dn"))}


def _group_grads(group, g):
    out = {}
    for name, key in _GROUP_GRADS[group]:
        t = g[key]
        if name == "w_uq":
            uq_n = t[:, :D_MODEL].reshape(Q_LORA, N_HEADS, HEAD)
            uq_r = t[:, D_MODEL:].reshape(Q_LORA, N_HEADS, HEAD)[:, :, :ROPE]
            t = jnp.concatenate([uq_n, uq_r], axis=2).reshape(Q_LORA, -1)
        out[name] = t
    return out


PACK_ROWS = 512
PACK_BUFFERS, PACK_AHEAD = 4, 2
SUBLANES = 8


def _pack_exchange(parts, name):
    arrays = []
    for a, _, _ in parts:
        if not any(a is b for b in arrays):
            arrays.append(a)
    index = lambda a: next(i for i, b in enumerate(arrays) if a is b)
    chunks, dst = [], 0
    for a, first, rows in parts:
        assert first % SUBLANES == 0 and rows % SUBLANES == 0
        chunks += [(index(a), first + o, dst + o, min(PACK_ROWS, rows - o)) for o in range(0, rows, PACK_ROWS)]
        dst += rows
    c, n, last = arrays[0].shape[1], len(arrays), len(chunks) - 1
    slab = dst // N_DEV
    assert slab * N_DEV == dst

    def body(*refs):
        src_refs, out_ref, recv_ref = refs[:n], refs[n], refs[n + 1]
        buf, sem_in, sem_out, send_sems, recv_sems = refs[n + 2:]
        x, y, core = lax.axis_index("x"), lax.axis_index("y"), lax.axis_index("c")

        def to_sibling(q):
            return pltpu.make_async_remote_copy(
                src_ref=out_ref.at[pl.ds((2 * q + 1 - core) * slab, slab)], dst_ref=recv_ref.at[q],
                send_sem=send_sems.at[q], recv_sem=recv_sems.at[q], device_id=(x, y, 1 - core),
                device_id_type=_MESH_ID)

        sent = [0]

        def send_packed(rows_done):
            while sent[0] < N_DEV // 2 and (2 * sent[0] + 2) * slab <= rows_done:
                to_sibling(sent[0]).start()
                sent[0] += 1

        def load(k):
            i, first, _, rows = chunks[k]
            return pltpu.make_async_copy(src_refs[i].at[pl.ds(first, rows)],
                                         buf.at[k % PACK_BUFFERS, pl.ds(0, rows)], sem_in.at[k % PACK_BUFFERS])

        def store(k):
            _, _, first, rows = chunks[k]
            return pltpu.make_async_copy(buf.at[k % PACK_BUFFERS, pl.ds(0, rows)],
                                         out_ref.at[pl.ds(first, rows), 0, :], sem_out.at[k % PACK_BUFFERS])

        def stored(k):
            store(k).wait()
            send_packed(chunks[k][2] + chunks[k][3])

        for k in range(min(PACK_AHEAD, last + 1)):
            load(k).start()
        for k in range(last + 1):
            load(k).wait()
            store(k).start(priority=1)
            ahead = k + PACK_AHEAD
            if ahead <= last:
                if ahead >= PACK_BUFFERS:
                    stored(ahead - PACK_BUFFERS)
                load(ahead).start()
        for k in range(max(0, last + 1 - PACK_BUFFERS), last + 1):
            stored(k)
        for q in range(N_DEV // 2):
            to_sibling(q).wait_recv()
        for q in range(N_DEV // 2):
            to_sibling(q).wait_send()

    return pl.pallas_call(
        body,
        out_shape=[jax.ShapeDtypeStruct((dst, 1, c), F32), jax.ShapeDtypeStruct((N_DEV // 2, slab, 1, c), F32)],
        in_specs=[_ANY] * n,
        out_specs=[_ANY, _ANY],
        scratch_shapes=[pltpu.VMEM((PACK_BUFFERS, PACK_ROWS, c), F32), pltpu.SemaphoreType.DMA((PACK_BUFFERS,)),
                        pltpu.SemaphoreType.DMA((PACK_BUFFERS,)), pltpu.SemaphoreType.DMA((N_DEV // 2,)),
                        pltpu.SemaphoreType.DMA((N_DEV // 2,))],
        name=name,
    )(*arrays)


def _w_in_grad_parts(g):
    wb = g["wb_t"]
    return [(g["wa_qkv_t"], 0, 3 * D_MODEL), (g["wa_z_t"], 0, D_MODEL), (wb, WB_BA, 2 * N_HEADS),
            (wb, WB_CQ, Q_LORA), (wb, WB_CKV, KV_LORA), (wb, WB_KR, ROPE), (g["wg_t"], 0, 2 * D_MODEL)]


def _small_grads(g):
    return {"ln1_g": g["ln1g"], "ln1_b": g["ln1b"], "ln2_g": g["ln2g"], "ln2_b": g["ln2b"], "q_norm_w": g["qnw"],
            "kv_norm_w": g["kvnw"], "dn_norm_w": g["dnw"], "dn_a_log": g["alog"], "dn_dt_bias": g["dtb"],
            "conv_w": g["conv"]}


_SMALL_SLOTS = {"ln1_g": (0, 0, 1024), "ln1_b": (1, 0, 1024), "ln2_g": (2, 0, 1024), "ln2_b": (3, 0, 1024),
                "q_norm_w": (4, 0, 384), "kv_norm_w": (4, 384, 256), "dn_norm_w": (4, 640, 128),
                "dn_a_log": (4, 768, 8), "dn_dt_bias": (4, 896, 8)}
_SMALL_ROWS, _LOSS_ROW, _CONV_ROW0, _CONV_ROWS = 24, 5, 8, 12


def _pack_small_grads(small_g, loss):
    zeros = lambda r, c: jnp.zeros((r, c), F32)
    row4 = jnp.concatenate([small_g["q_norm_w"], small_g["kv_norm_w"], small_g["dn_norm_w"], small_g["dn_a_log"],
                            small_g["dn_dt_bias"]], axis=1)
    row5 = jnp.concatenate([loss, zeros(1, FLAT_COLS - LANES)], axis=1)
    head = jnp.concatenate([small_g["ln1_g"], small_g["ln1_b"], small_g["ln2_g"], small_g["ln2_b"], row4, row5,
                            zeros(2, FLAT_COLS)], axis=0)
    conv = small_g["conv_w"].reshape(_CONV_ROWS, FLAT_COLS)
    return jnp.concatenate([head, conv, zeros(_SMALL_ROWS - _CONV_ROW0 - _CONV_ROWS, FLAT_COLS)], axis=0)


_MESH_ID = pl.DeviceIdType.MESH
_ANY = pl.BlockSpec(memory_space=pl.ANY)


def _all_gather(blocks, name):
    n = len(blocks)

    def body(*refs):
        x_refs, out_refs = refs[:n], refs[n:2 * n]
        send_sems, recv_sems, local_sems = refs[2 * n:]
        x, y, c = lax.axis_index("x"), lax.axis_index("y"), lax.axis_index("c")
        me, sibling = (x, y, c), (x, y, 1 - c)
        chips = [(1 - x, y), (x, 1 - y), (1 - x, 1 - y)]

        def slot(i, px, py, pc):
            return out_refs[i].at[4 * px + 2 * py + pc]

        def copy(i, k, origin, to, src=None):
            return pltpu.make_async_remote_copy(
                src_ref=slot(i, *origin) if src is None else src, dst_ref=slot(i, *origin),
                send_sem=send_sems.at[7 * i + k], recv_sem=recv_sems.at[7 * i + k], device_id=to,
                device_id_type=_MESH_ID)

        mine = [pltpu.make_async_copy(x_refs[i], slot(i, *me), local_sems.at[i]) for i in range(n)]
        first, passed = [], []
        for i in range(n):
            mine[i].start()
            first.append(copy(i, 0, me, sibling, src=x_refs[i]))
            first += [copy(i, 1 + j, me, (*chip, c), src=x_refs[i]) for j, chip in enumerate(chips)]
        for cp in first:
            cp.start()
        for i in range(n):
            for j, chip in enumerate(chips):
                copy(i, 1 + j, (*chip, c), me).wait_recv()
                passed.append(copy(i, 4 + j, (*chip, c), sibling))
                passed[-1].start()
        for i in range(n):
            copy(i, 0, sibling, me).wait_recv()
            for j, chip in enumerate(chips):
                copy(i, 4 + j, (*chip, 1 - c), me).wait_recv()
        for cp in first + passed:
            cp.wait_send()
        for cp in mine:
            cp.wait()

    return pl.pallas_call(
        body,
        out_shape=[jax.ShapeDtypeStruct((N_DEV,) + b.shape, b.dtype) for b in blocks],
        in_specs=[_ANY] * n,
        out_specs=[_ANY] * n,
        scratch_shapes=[pltpu.SemaphoreType.DMA((7 * n,)), pltpu.SemaphoreType.DMA((7 * n,)),
                        pltpu.SemaphoreType.DMA((n,))],
        name=name,
    )(*blocks)


def _col_tile(c):
    return c if c <= 256 else 256


def _chip_sum(src, recv, where, name):
    _, r, _, c = src.shape
    tc = _col_tile(c)

    def body(w_ref, a_ref, b_ref, o_ref, ob_ref):
        s = a_ref[...] + b_ref[...]
        ob_ref[...] = s.astype(BF16)

        @pl.when(pl.program_id(1) == w_ref[1])
        def _():
            o_ref[...] = s

    rows = lambda f: pl.BlockSpec((None, r, None, tc), f)
    return pl.pallas_call(
        body,
        out_shape=[jax.ShapeDtypeStruct((r, c), F32), jax.ShapeDtypeStruct((4, r, c), BF16)],
        grid_spec=pltpu.PrefetchScalarGridSpec(
            num_scalar_prefetch=1, grid=(c // tc, 4),
            in_specs=[rows(lambda j, q, w: (2 * q + w[0], 0, 0, j)), rows(lambda j, q, w: (q, 0, 0, j))],
            out_specs=[pl.BlockSpec((r, tc), lambda j, q, w: (0, j)),
                       pl.BlockSpec((None, r, tc), lambda j, q, w: (q, 0, j))]),
        compiler_params=pltpu.CompilerParams(dimension_semantics=("parallel", "arbitrary")),
        name=name,
    )(where, src, recv)


_HBM = pl.BlockSpec(memory_space=pltpu.HBM)
_SEM = pl.BlockSpec(memory_space=pltpu.SEMAPHORE)
_DATAFLOW = pltpu.SideEffectType.DATAFLOW_SIDE_EFFECTING
N_PEERS = N_DEV - 1


def _ring_peer(j):
    me = 4 * lax.axis_index("x") + 2 * lax.axis_index("y") + lax.axis_index("c")
    k = (me + j) % N_DEV
    return me, k, (k // 4, (k // 2) % 2, k % 2)


def _spread_copy(i, j, src_refs, land_refs, send_sems, recv_sems, scatter):
    me, k, peer = _ring_peer(j)
    return pltpu.make_async_remote_copy(
        src_ref=src_refs[i].at[k] if scatter else src_refs[i], dst_ref=land_refs[i].at[me],
        send_sem=send_sems.at[N_PEERS * i + j - 1], recv_sem=recv_sems.at[N_PEERS * i + j - 1], device_id=peer,
        device_id_type=_MESH_ID)


def _spread_start(srcs, carry, scatter, name):
    n = len(srcs)
    lands = [lax.empty(((N_DEV,) + s.shape[-2:]), s.dtype) for s in srcs]

    def body(*refs):
        src_refs, land_refs = refs[:n], refs[n:2 * n]
        send_sems, recv_sems, local_sems = refs[2 * n + 1:2 * n + 4]
        for i in range(n):
            for j in range(1, N_DEV):
                _spread_copy(i, j, src_refs, land_refs, send_sems, recv_sems, scatter).start()
        for i in range(n):
            _own_copy(i, src_refs, land_refs, local_sems, scatter).start()

    hbm = lambda a: pltpu.HBM(a.shape, a.dtype)
    sems = pltpu.SemaphoreType.DMA((N_PEERS * n,))
    pinned = [pltpu.with_memory_space_constraint(a, pltpu.HBM) for a in list(srcs) + lands + [carry]]
    res = pl.pallas_call(
        body, name=name,
        out_shape=(sems, sems, pltpu.SemaphoreType.DMA((n,)), *[hbm(a) for a in pinned]),
        in_specs=[_HBM] * (2 * n + 1),
        out_specs=(_SEM, _SEM, _SEM, *[_HBM] * (2 * n + 1)),
        input_output_aliases={i: 3 + i for i in range(2 * n + 1)},
        compiler_params=pltpu.CompilerParams(has_side_effects=_DATAFLOW),
    )(*pinned)
    return res[:3], list(res[3:3 + n]), list(res[3 + n:3 + 2 * n]), res[3 + 2 * n]


def _own_copy(i, src_refs, land_refs, local_sems, scatter):
    me = _ring_peer(0)[0]
    return pltpu.make_async_copy(src_refs[i].at[me] if scatter else src_refs[i], land_refs[i].at[me],
                                 local_sems.at[i])


def _spread_wait(started, after, scatter, name):
    sems, srcs, lands, _ = started
    n = len(srcs)

    def body(*refs):
        src_refs, land_refs = refs[:n], refs[n:2 * n]
        send_s, recv_s, local_s = refs[2 * n:2 * n + 3]
        for i in range(n):
            for j in range(1, N_DEV):
                cp = _spread_copy(i, j, src_refs, land_refs, send_s, recv_s, scatter)
                cp.wait_send()
                cp.wait_recv()
        for i in range(n):
            _own_copy(i, src_refs, land_refs, local_s, scatter).wait()

    hbm = lambda a: pltpu.HBM(a.shape, a.dtype)
    res = pl.pallas_call(
        body, name=name,
        out_shape=tuple(hbm(a) for a in srcs + lands),
        in_specs=[_HBM] * (2 * n) + [_SEM, _SEM, _SEM, pl.BlockSpec(memory_space=pl.ANY)],
        out_specs=tuple([_HBM] * (2 * n)),
        input_output_aliases={i: i for i in range(2 * n)},
        compiler_params=pltpu.CompilerParams(has_side_effects=_DATAFLOW),
    )(*srcs, *lands, *sems, after)
    return list(res[n:])


def _chips_copy(i, j, src_refs, land_refs, send_sems, recv_sems):
    x, y, c = lax.axis_index("x"), lax.axis_index("y"), lax.axis_index("c")
    tx, ty = [(1 - x, y), (x, 1 - y), (1 - x, 1 - y)][j]
    return pltpu.make_async_remote_copy(
        src_ref=src_refs[i].at[2 * tx + ty], dst_ref=land_refs[i].at[j], send_sem=send_sems.at[3 * i + j],
        recv_sem=recv_sems.at[3 * i + j], device_id=(tx, ty, c), device_id_type=_MESH_ID)


def _chips_start(srcs, carry, name):
    n = len(srcs)
    lands = [lax.empty((3,) + s.shape[1:], s.dtype) for s in srcs]

    def body(*refs):
        src_refs, land_refs = refs[:n], refs[n:2 * n]
        send_sems, recv_sems = refs[2 * n + 1:2 * n + 3]
        for i in range(n):
            for j in range(3):
                _chips_copy(i, j, src_refs, land_refs, send_sems, recv_sems).start()

    hbm = lambda a: pltpu.HBM(a.shape, a.dtype)
    sems = pltpu.SemaphoreType.DMA((3 * n,))
    pinned = [pltpu.with_memory_space_constraint(a, pltpu.HBM) for a in list(srcs) + lands + [carry]]
    res = pl.pallas_call(
        body, name=name,
        out_shape=(sems, sems, *[hbm(a) for a in pinned]),
        in_specs=[_HBM] * (2 * n + 1),
        out_specs=(_SEM, _SEM, *[_HBM] * (2 * n + 1)),
        input_output_aliases={i: 2 + i for i in range(2 * n + 1)},
        compiler_params=pltpu.CompilerParams(has_side_effects=_DATAFLOW),
    )(*pinned)
    return res[:2], list(res[2:2 + n]), list(res[2 + n:2 + 2 * n]), res[2 + 2 * n]


def _chips_wait(started, after, name):
    sems, srcs, lands, _ = started
    n = len(srcs)

    def body(*refs):
        src_refs, land_refs = refs[:n], refs[n:2 * n]
        send_s, recv_s = refs[2 * n:2 * n + 2]
        for i in range(n):
            for j in range(3):
                cp = _chips_copy(i, j, src_refs, land_refs, send_s, recv_s)
                cp.wait_send()
                cp.wait_recv()

    hbm = lambda a: pltpu.HBM(a.shape, a.dtype)
    res = pl.pallas_call(
        body, name=name,
        out_shape=tuple(hbm(a) for a in srcs + lands),
        in_specs=[_HBM] * (2 * n) + [_SEM, _SEM, pl.BlockSpec(memory_space=pl.ANY)],
        out_specs=tuple([_HBM] * (2 * n)),
        input_output_aliases={i: i for i in range(2 * n)},
        compiler_params=pltpu.CompilerParams(has_side_effects=_DATAFLOW),
    )(*srcs, *lands, *sems, after)
    return list(res[n:])


def _sum8(landing, name):
    _, r, c = landing.shape
    tc = _col_tile(c)

    def body(a_ref, o_ref):
        tot = a_ref[0].astype(F32)
        for k in range(1, N_DEV):
            tot = tot + a_ref[k].astype(F32)
        o_ref[...] = tot

    return pl.pallas_call(
        body,
        out_shape=jax.ShapeDtypeStruct((r, c), F32),
        grid=(c // tc,),
        in_specs=[pl.BlockSpec((N_DEV, r, tc), lambda j: (0, 0, j))],
        out_specs=pl.BlockSpec((r, tc), lambda j: (0, j)),
        compiler_params=pltpu.CompilerParams(dimension_semantics=("parallel",)),
        name=name,
    )(landing)


def _adamw_math(w, g, m, v):
    m = ADAM_B1 * m + (1.0 - ADAM_B1) * g
    v = ADAM_B2 * v + (1.0 - ADAM_B2) * (g * g)
    m_hat = m / (1.0 - ADAM_B1 ** ADAM_STEP)
    v_hat = v / (1.0 - ADAM_B2 ** ADAM_STEP)
    delta = -ADAM_LR * (m_hat / (jnp.sqrt(v_hat) + ADAM_EPS) + ADAM_WD * w)
    return delta, m, v


def _adamw(w, m, v, g, name):
    r, c = w.shape

    def fn(rows, consts):
        return list(_adamw_math(*rows)), []

    return _rowwise(fn, [w, g, m, v], [], [(c, F32)] * 3, tm=r if r <= 512 else 256, name=name)


def _adamw_sum8(w, m, v, landing, name):
    r, c = w.shape
    tc = _col_tile(c)

    def body(w_ref, m_ref, v_ref, a_ref, g_ref, d_ref, m2_ref, v2_ref):
        g = a_ref[0].astype(F32)
        for k in range(1, N_DEV):
            g = g + a_ref[k].astype(F32)
        delta, m2, v2 = _adamw_math(w_ref[...], g, m_ref[...], v_ref[...])
        g_ref[...] = g
        d_ref[...] = delta
        m2_ref[...] = m2
        v2_ref[...] = v2

    blk = pl.BlockSpec((r, tc), lambda j: (0, j))
    return pl.pallas_call(
        body,
        out_shape=[jax.ShapeDtypeStruct((r, c), F32)] * 4,
        grid=(c // tc,),
        in_specs=[blk, blk, blk, pl.BlockSpec((N_DEV, r, tc), lambda j: (0, 0, j))],
        out_specs=[blk] * 4,
        compiler_params=pltpu.CompilerParams(dimension_semantics=("parallel",)),
        name=name,
    )(w, m, v, landing)


def _adamw_parts(w, m, v, own, others, name):
    r, _, c = w.shape
    tc = _col_tile(c)

    def body(w_ref, m_ref, v_ref, a_ref, b_ref, g_ref, d_ref, m2_ref, v2_ref):
        g = ((a_ref[...] + b_ref[0].astype(F32)) + b_ref[1].astype(F32)) + b_ref[2].astype(F32)
        delta, m2, v2 = _adamw_math(w_ref[...], g, m_ref[...], v_ref[...])
        g_ref[...] = g
        d_ref[...] = delta
        m2_ref[...] = m2
        v2_ref[...] = v2

    row = pl.BlockSpec((r, None, tc), lambda j: (0, 0, j))
    return pl.pallas_call(
        body,
        out_shape=[jax.ShapeDtypeStruct((r, 1, c), F32)] * 4,
        grid=(c // tc,),
        in_specs=[row, row, row, pl.BlockSpec((r, tc), lambda j: (0, j)), pl.BlockSpec((3, r, tc), lambda j: (0, 0, j))],
        out_specs=[row] * 4,
        compiler_params=pltpu.CompilerParams(dimension_semantics=("parallel",)),
        name=name,
    )(w, m, v, own, others)


def _adamw_small(gathered, params):
    ns = len(_SMALL)

    def body(*refs):
        g_ref, p_refs, o_refs = refs[0], refs[1:1 + 3 * ns], refs[1 + 3 * ns:]
        tot = g_ref[0]
        for k in range(1, N_DEV):
            tot = tot + g_ref[k]
        for i, name in enumerate(_SMALL):
            row, lane0, lanes = _SMALL_SLOTS[name]
            g = tot[row:row + 1, lane0:lane0 + lanes]
            w_, m_, v_ = (p_refs[3 * i + j][...] for j in range(3))
            delta, m2, v2 = _adamw_math(w_, g, m_, v_)
            for j, val in enumerate((g, delta, m2, v2)):
                o_refs[4 * i + j][...] = val
        o_refs[4 * ns][...] = tot[_LOSS_ROW:_LOSS_ROW + 1, 0:LANES]
        o_refs[4 * ns + 1][...] = tot[_CONV_ROW0:_CONV_ROW0 + _CONV_ROWS, :]

    out_shape = [jax.ShapeDtypeStruct(w.shape, F32) for (w, _, _) in params for _ in range(4)]
    out_shape += [jax.ShapeDtypeStruct((1, LANES), F32), jax.ShapeDtypeStruct((_CONV_ROWS, FLAT_COLS), F32)]
    flat = [a for wmv in params for a in wmv]
    return pl.pallas_call(body, out_shape=out_shape, name="adamw_small")(gathered, *flat)


def kernel(x, p, positions, w_in, conv_w, dn_a_log, dn_dt_bias, dn_norm_w, q_norm_w, w_uq, kv_norm_w, w_uk, w_uv, w_br_dn, w_br_mla, w_o, ln1_g, ln1_b, w_ffn_in, w_ffn_out, w_ple, w_ple_gate, ln2_g, ln2_b, loss_target, m_w_in, m_conv_w, m_dn_a_log, m_dn_dt_bias, m_dn_norm_w, m_q_norm_w, m_w_uq, m_kv_norm_w, m_w_uk, m_w_uv, m_w_br_dn, m_w_br_mla, m_w_o, m_ln1_g, m_ln1_b, m_w_ffn_in, m_w_ffn_out, m_w_ple, m_w_ple_gate, m_ln2_g, m_ln2_b, v_w_in, v_conv_w, v_dn_a_log, v_dn_dt_bias, v_dn_norm_w, v_q_norm_w, v_w_uq, v_kv_norm_w, v_w_uk, v_w_uv, v_w_br_dn, v_w_br_mla, v_w_o, v_ln1_g, v_ln1_b, v_w_ffn_in, v_w_ffn_out, v_w_ple, v_w_ple_gate, v_ln2_g, v_ln2_b):
    args = dict(locals())
    wts = {n: args[n] for n in _ORDER}
    mom1 = {n: args["m_" + n] for n in _ORDER}
    mom2 = {n: args["v_" + n] for n in _ORDER}
    big_names = [n for n, _ in _BIG]
    shard_shapes = {n: wts[n].shape[1:] for n in big_names}
    c_idx = lax.axis_index("c")
    q_idx = 2 * lax.axis_index("x") + lax.axis_index("y")
    where = jnp.stack([c_idx, q_idx]).astype(jnp.int32)

    stored = {n: _to_stored(n, wts[n][0]).astype(BF16) for n in big_names}
    first = _all_gather([stored["w_in"], conv_w[0]], "ag_first")
    group_names = {grp: [n for n, _ in pairs] for grp, pairs in _GROUP_GRADS.items()}
    carry, gathers = first[0], {}
    for grp in ("mix", "ffn"):
        gathers[grp] = _spread_start([stored[n] for n in group_names[grp]], carry, False, "ag_start_" + grp)
        carry = gathers[grp][3]
    conv_full = jnp.moveaxis(first[1], 0, 1).reshape(conv_w.shape[1], -1)
    small_w = {n: wts[n].astype(F32) for n in _SMALL}
    w = _first_weights(carry.reshape(-1, D_MODEL), conv_full, small_w)

    def late_weights(grp, after):
        got = _spread_wait(gathers[grp], after, False, "ag_wait_" + grp)
        return _late_weights(grp, {n: t.reshape(-1, t.shape[-1]) for n, t in zip(group_names[grp], got)})

    started = {}

    def emit(group, g, carry):
        if group == "w_in":
            rows, cols = _stored_shape("w_in", shard_shapes["w_in"])
            packed, from_sibling = _pack_exchange(_w_in_grad_parts(g), "rs_pack_sibling")
            own, own_bf = _chip_sum(packed.reshape(N_DEV, rows, 1, cols), from_sibling, where, "rs_sum_w_in")
            started["w_in"] = (own, _chips_start([own_bf], carry, "rs_chips_start"))
            return started["w_in"][1][3]
        if group == "small":
            block = _pack_small_grads(_small_grads(g), g["loss"])
            started["small"] = _spread_start([block], carry, False, "ag_start_small")
            return started["small"][3]
        grads = _group_grads(group, g)
        srcs = [grads[n].reshape((N_DEV,) + _stored_shape(n, shard_shapes[n])) for n in grads]
        started[group] = (list(grads), _spread_start(srcs, carry, True, "rs_start_" + group))
        return started[group][1][3]

    s_dim = x.shape[1]
    loss, dx, g = _local_step(x[0], p[0, 0], positions.reshape(s_dim, 1).astype(F32), loss_target[0], w,
                              late_weights, emit)
    own, chips_started = started.pop("w_in")
    small_started = started.pop("small")

    out_g, out_d, out_m, out_v = {}, {}, {}, {}

    def update(n, grad, shp):
        flat2 = (shp[0], int(np.prod(shp[1:])))
        d, m2, v2 = _adamw(wts[n][0].reshape(flat2), mom1[n][0].reshape(flat2), mom2[n][0].reshape(flat2),
                           grad.reshape(flat2), "adamw_" + n)
        out_g[n], out_d[n], out_m[n], out_v[n] = grad, d.reshape(shp), m2.reshape(shp), v2.reshape(shp)

    for group, (names, st) in started.items():
        for n, landing in zip(names, _spread_wait(st, dx, True, "rs_wait_" + group)):
            shp = shard_shapes[n]
            if _BIG_AXIS[n] == 0 or shp[-1] % LANES:
                res = _adamw_sum8(_to_stored(n, wts[n][0]), _to_stored(n, mom1[n][0]), _to_stored(n, mom2[n][0]),
                                  landing, "adamw_" + n)
                out_g[n], out_d[n], out_m[n], out_v[n] = (_from_stored(n, t, shp) for t in res)
                last = res[3]
            else:
                update(n, _from_stored(n, _sum8(landing, "rs_total_" + n), shp), shp)

    from_chips = _chips_wait(chips_started, last, "rs_chips_wait")[0]
    g_small = _spread_wait(small_started, last, False, "ag_wait_small")[0]
    rows_first = lambda a: jnp.transpose(a, (2, 0, 1))
    res = _adamw_parts(rows_first(wts["w_in"]), rows_first(mom1["w_in"]), rows_first(mom2["w_in"]), own, from_chips,
                       "adamw_w_in")
    out_g["w_in"], out_d["w_in"], out_m["w_in"], out_v["w_in"] = (jnp.transpose(t, (1, 2, 0))[0] for t in res)

    res = _adamw_small(g_small, [(wts[n], mom1[n], mom2[n]) for n in _SMALL])
    for i, n in enumerate(_SMALL):
        out_g[n], out_d[n], out_m[n], out_v[n] = res[4 * i:4 * i + 4]
    loss_out = res[4 * len(_SMALL)][0, 0]
    conv_shape = conv_w.shape[1:]
    conv_g = lax.dynamic_slice(res[-1].reshape(conv_shape[0], -1), (0, (2 * q_idx + c_idx) * conv_shape[1]),
                               conv_shape)
    update("conv_w", conv_g, conv_shape)

    expand = lambda d, n: d[n] if n in _SMALL else d[n][None]
    return (loss_out, dx[None], *[expand(out_g, n) for n in _ORDER], *[expand(out_d, n) for n in _ORDER],
            *[expand(out_m, n) for n in _ORDER], *[expand(out_v, n) for n in _ORDER])
```

```python
import functools

import numpy as np
import jax
import jax.numpy as jnp
from jax import lax
from jax.experimental import pallas as pl
from jax.experimental.pallas import tpu as pltpu

F32 = jnp.float32
BF16 = jnp.bfloat16

D_MODEL = 1024
N_HEADS = 8
HEAD = 128
CHUNK = 64
GROUP = 256
ROPE = 64
Q_LORA = 384
KV_LORA = 256
FFN_HIDDEN = 2816
PLE_DIM = 256
ROPE_BASE = 10000.0
ALPHA = 2.0 ** 0.25
SCALE = float((HEAD + ROPE) ** -0.5)
NEG_BIG = -1e30
EPS_RMS = 1e-6
EPS_LN = 1e-5

ADAM_LR = 0.001
ADAM_B1 = 0.9
ADAM_B2 = 0.999
ADAM_EPS = 1e-08
ADAM_WD = 0.01
ADAM_STEP = 10

N_DEV = 8
LANES = 128
FLAT_COLS = 1024

WB_CQ, WB_CKV, WB_KR, WB_BA, WB_COLS = 0, 512, 768, 896, 1024

HIGHEST = lax.Precision.HIGHEST

NN = (((1,), (0,)), ((), ()))
TN = (((0,), (0,)), ((), ()))
NT = (((1,), (1,)), ((), ()))


def _dot(a, b, dims=NN):
    return lax.dot_general(a.astype(BF16), b.astype(BF16), dims, preferred_element_type=F32)


def _dot32(a, b, dims=NN):
    return lax.dot_general(a, b, dims, precision=HIGHEST, preferred_element_type=F32)


def _sig(x):
    return 1.0 / (1.0 + jnp.exp(-x))


MM_TILE = 1536


def _pick_wide(n):
    if n <= MM_TILE:
        return n
    return max(t for t in range(LANES, MM_TILE + 1, LANES) if n % t == 0)


def _split_bf16(a):
    hi = a.astype(BF16)
    return hi, (a - hi.astype(F32)).astype(BF16)


def _dot3(a, b, dims=NN):
    ah, al = a if isinstance(a, tuple) else _split_bf16(a)
    bh, bl = b if isinstance(b, tuple) else _split_bf16(b)
    d = lambda p, q: lax.dot_general(p, q, dims, preferred_element_type=F32)
    return d(ah, bh) + (d(ah, bl) + d(al, bh))


def _mm(a, b, *, ta=False, tb=False, add=(), out_dtype=F32, name):
    if ta:
        k_dim, m_dim = a.shape
    else:
        m_dim, k_dim = a.shape
    if tb:
        n_dim, k2 = b.shape
    else:
        k2, n_dim = b.shape
    assert k_dim == k2, (a.shape, b.shape, ta, tb)
    tm = _pick_wide(m_dim)
    tn = _pick_wide(n_dim)
    tk = _pick_wide(k_dim)
    nk = k_dim // tk
    n_add = len(add)
    dims = TN if ta else (NT if tb else NN)
    assert not (ta and tb)

    def body(a_ref, b_ref, *rest):
        add_refs = rest[:n_add]
        o_ref = rest[n_add]
        acc = rest[n_add + 1]
        k = pl.program_id(2)

        @pl.when(k == 0)
        def _():
            acc[...] = jnp.zeros_like(acc)

        acc[...] += _dot(a_ref[...], b_ref[...], dims)

        @pl.when(k == nk - 1)
        def _():
            r = acc[...]
            for ar in add_refs:
                r = r + ar[...].astype(F32)
            o_ref[...] = r.astype(o_ref.dtype)

    a_spec = pl.BlockSpec((tk, tm), lambda i, j, k: (k, i)) if ta else pl.BlockSpec((tm, tk), lambda i, j, k: (i, k))
    b_spec = pl.BlockSpec((tn, tk), lambda i, j, k: (j, k)) if tb else pl.BlockSpec((tk, tn), lambda i, j, k: (k, j))
    o_spec = pl.BlockSpec((tm, tn), lambda i, j, k: (i, j))
    return pl.pallas_call(
        body,
        out_shape=jax.ShapeDtypeStruct((m_dim, n_dim), out_dtype),
        grid=(m_dim // tm, n_dim // tn, nk),
        in_specs=[a_spec, b_spec] + [o_spec] * n_add,
        out_specs=o_spec,
        scratch_shapes=[pltpu.VMEM((tm, tn), F32)],
        compiler_params=pltpu.CompilerParams(dimension_semantics=("parallel", "parallel", "arbitrary")),
        name=name,
    )(a, b, *add)


def _mm_resident(a, b, *, tb=False, add=(), out_dtype=F32, name):
    m_dim, k_dim = a.shape
    n_dim, k2 = b.shape if tb else b.shape[::-1]
    assert k2 == k_dim
    tm = min(DX_ROWS, m_dim)
    dims = NT if tb else NN

    def body(a_ref, b_ref, *rest):
        r = _dot(a_ref[...], b_ref[...], dims)
        for ar in rest[:-1]:
            r = r + ar[...]
        rest[-1][...] = r.astype(out_dtype)

    o_spec = pl.BlockSpec((tm, n_dim), lambda i: (i, 0))
    return pl.pallas_call(
        body,
        out_shape=jax.ShapeDtypeStruct((m_dim, n_dim), out_dtype),
        grid=(m_dim // tm,),
        in_specs=[pl.BlockSpec((tm, k_dim), lambda i: (i, 0)),
                  pl.BlockSpec(b.shape, lambda i: (0, 0), pipeline_mode=pl.Buffered(1))] + [o_spec] * len(add),
        out_specs=o_spec,
        compiler_params=pltpu.CompilerParams(dimension_semantics=("parallel",)),
        name=name,
    )(a, b, *add)


def _rowwise(fn, rows, consts, outs, accs=(), *, tm=256, name):
    rows = [r if isinstance(r, tuple) else (r, 0, r.shape[1]) for r in rows]
    s_dim = rows[0][0].shape[0]
    tm = min(tm, s_dim)
    assert s_dim % tm == 0 and all(arr.shape[0] == s_dim for arr, _, _ in rows)
    specs = [pl.BlockSpec((tm, width), functools.partial(lambda i, cb: (i, cb), cb=cb)) for _, cb, width in rows]
    args = [arr for arr, _, _ in rows]
    for c in consts:
        specs.append(pl.BlockSpec(c.shape, lambda i: (0, 0)))
        args.append(c)
    nr, nc, no = len(rows), len(consts), len(outs)
    flipped = [len(o) == 3 for o in outs]
    out_shape = [jax.ShapeDtypeStruct((o[0], s_dim) if t else (s_dim, o[0]), o[1]) for o, t in zip(outs, flipped)]
    out_specs = [pl.BlockSpec((o[0], tm), lambda i: (0, i)) if t else pl.BlockSpec((tm, o[0]), lambda i: (i, 0))
                 for o, t in zip(outs, flipped)]
    out_shape += [jax.ShapeDtypeStruct(sh, F32) for sh in accs]
    out_specs += [pl.BlockSpec(sh, lambda i: (0, 0)) for sh in accs]

    def body(*refs):
        r = [x[...].astype(F32) if x.dtype == BF16 else x[...] for x in refs[:nr]]
        c = [x[...] for x in refs[nr:nr + nc]]
        o_refs = refs[nr + nc:nr + nc + no]
        a_refs = refs[nr + nc + no:]
        o_vals, a_vals = fn(r, c)
        for ref, v, t in zip(o_refs, o_vals, flipped, strict=True):
            ref[...] = (jnp.transpose(v.astype(F32)) if t else v).astype(ref.dtype)
        if a_refs:
            @pl.when(pl.program_id(0) == 0)
            def _():
                for ref in a_refs:
                    ref[...] = jnp.zeros_like(ref)

            for ref, v in zip(a_refs, a_vals, strict=True):
                ref[...] += v

    res = pl.pallas_call(
        body,
        out_shape=out_shape,
        grid=(s_dim // tm,),
        in_specs=specs,
        out_specs=out_specs,
        compiler_params=pltpu.CompilerParams(dimension_semantics=("arbitrary" if accs else "parallel",)),
        name=name,
    )(*args)
    return res


def _colsum(v):
    return jnp.sum(v, axis=0, keepdims=True)


def _rowsum(v):
    return jnp.sum(v, axis=1, keepdims=True)


def _rowmean(v):
    return jnp.mean(v, axis=1, keepdims=True)


def _silu_grad(x):
    s = _sig(x)
    return s * (1.0 + x * (1.0 - s))


def _conv_taps(x, w, width=4):
    row = lax.broadcasted_iota(jnp.int32, x.shape, 0)
    c = x * w[width - 1:width, :]
    for s in range(1, width):
        c = c + jnp.where(row >= s, pltpu.roll(x, s, 0), 0.0) * w[width - 1 - s:width - s, :]
    return c


def _conv_fwd(proj_a, conv_w):
    s_dim = proj_a.shape[0]
    n_blk = 3 * N_HEADS

    def body(x_ref, w_ref, o_ref):
        j = pl.program_id(0)
        c = _conv_taps(x_ref[...], w_ref[...])
        y = c * _sig(c)
        r = lax.rsqrt(_rowsum(y * y) + EPS_RMS)
        fac = jnp.where(j < N_HEADS, r * (HEAD ** -0.5), jnp.where(j < 2 * N_HEADS, r, 1.0))
        o_ref[...] = y * fac

    return pl.pallas_call(
        body,
        out_shape=jax.ShapeDtypeStruct((s_dim, n_blk * HEAD), F32),
        grid=(n_blk,),
        in_specs=[pl.BlockSpec((s_dim, HEAD), lambda j: (0, j)), pl.BlockSpec((4, HEAD), lambda j: (0, j))],
        out_specs=pl.BlockSpec((s_dim, HEAD), lambda j: (0, j)),
        compiler_params=pltpu.CompilerParams(dimension_semantics=("parallel",)),
        name="conv_fwd",
    )(proj_a, conv_w)


def _conv_bwd(proj_a, conv_w, dq, dk, dv):
    s_dim = proj_a.shape[0]
    n_blk = 3 * N_HEADS

    def body(x_ref, w_ref, dq_ref, dk_ref, dv_ref, dx_ref, dw_ref):
        j = pl.program_id(0)
        x = x_ref[...]
        w = w_ref[...]
        do = jnp.where(j < N_HEADS, dq_ref[...], jnp.where(j < 2 * N_HEADS, dk_ref[...], dv_ref[...]))
        c = _conv_taps(x, w)
        sg = _sig(c)
        y = c * sg
        r = lax.rsqrt(_rowsum(y * y) + EPS_RMS)
        sc = jnp.where(j < N_HEADS, HEAD ** -0.5, 1.0)
        dy_n = sc * (r * do - y * (r * r * r) * _rowsum(do * y))
        dy = jnp.where(j < 2 * N_HEADS, dy_n, do)
        dc = dy * (sg * (1.0 + c * (1.0 - sg)))
        row = lax.broadcasted_iota(jnp.int32, x.shape, 0)
        dx = dc * w[3:4, :]
        dw_ref[3:4, :] = _colsum(dc * x)
        for s in range(1, 4):
            dx = dx + jnp.where(row < s_dim - s, pltpu.roll(dc, s_dim - s, 0), 0.0) * w[3 - s:4 - s, :]
            xs = jnp.where(row >= s, pltpu.roll(x, s, 0), 0.0)
            dw_ref[3 - s:4 - s, :] = _colsum(dc * xs)
        dx_ref[...] = dx.astype(dx_ref.dtype)

    hd = N_HEADS - 1
    return pl.pallas_call(
        body,
        out_shape=[jax.ShapeDtypeStruct((s_dim, n_blk * HEAD), BF16), jax.ShapeDtypeStruct((4, n_blk * HEAD), F32)],
        grid=(n_blk,),
        in_specs=[
            pl.BlockSpec((s_dim, HEAD), lambda j: (0, j)),
            pl.BlockSpec((4, HEAD), lambda j: (0, j)),
            pl.BlockSpec((s_dim, HEAD), lambda j: (0, jnp.minimum(j, hd))),
            pl.BlockSpec((s_dim, HEAD), lambda j: (0, jnp.clip(j - N_HEADS, 0, hd))),
            pl.BlockSpec((s_dim, HEAD), lambda j: (0, jnp.clip(j - 2 * N_HEADS, 0, hd))),
        ],
        out_specs=[pl.BlockSpec((s_dim, HEAD), lambda j: (0, j)), pl.BlockSpec((4, HEAD), lambda j: (0, j))],
        compiler_params=pltpu.CompilerParams(dimension_semantics=("parallel",)),
        name="conv_bwd",
    )(proj_a, conv_w, dq, dk, dv)


def _chunk_tri(n):
    r = np.arange(n)
    m = ((r[:, None] // CHUNK) == (r[None, :] // CHUNK)) & (r[:, None] >= r[None, :])
    m = m.astype(np.float32)
    return jnp.asarray(m), jnp.asarray(m.T)


def _softplus(z):
    return jnp.maximum(z, 0.0) + jnp.log(1.0 + jnp.exp(-jnp.abs(z)))


def _gates_fwd(proj_b, alog, dtb):
    tm = min(GROUP, proj_b.shape[0])
    tri, _ = _chunk_tri(tm)

    def fn(r, c):
        b = r[0]
        a = pltpu.roll(b, LANES - N_HEADS, 1)
        alog_, dtb_, tri_ = c
        g = -jnp.exp(alog_) * _softplus(a + dtb_)
        return [_sig(b), _dot32(tri_, g)], []

    return _rowwise(fn, [(proj_b, WB_BA // LANES, LANES)], [alog, dtb, tri],
                    [(LANES, F32), (LANES, F32)], tm=tm, name="gates_fwd")


def _gates_bwd(proj_b, alog, dtb, gc, d_beta, d_gc, d_egl_rows):
    tm = min(GROUP, proj_b.shape[0])
    _, tri_t = _chunk_tri(tm)

    def fn(r, c):
        b, gc_, d_beta_, d_gc_, d_egl_ = r
        a = pltpu.roll(b, LANES - N_HEADS, 1)
        alog_, dtb_, tri_t_ = c
        z = a + dtb_
        ea = jnp.exp(alog_)
        g = -ea * _softplus(z)
        dg = _dot32(tri_t_, d_gc_ + d_egl_ * jnp.exp(gc_))
        d_a = dg * (-ea) * _sig(z)
        beta = _sig(b)
        d_ba = d_beta_ * beta * (1.0 - beta) + pltpu.roll(d_a, N_HEADS, 1)
        return [d_ba], [_colsum(dg * g), _colsum(d_a)]

    return _rowwise(fn, [(proj_b, WB_BA // LANES, LANES), gc, d_beta, d_gc, d_egl_rows],
                    [alog, dtb, tri_t], [(LANES, BF16)], accs=[(1, LANES), (1, LANES)], tm=tm,
                    name="gates_bwd")


def _group_masks(n):
    r = lax.broadcasted_iota(jnp.int32, (n, n), 0)
    c = lax.broadcasted_iota(jnp.int32, (n, n), 1)
    same = (r // CHUNK) == (c // CHUNK)
    below, s = [], 2
    while s < CHUNK:
        below.append(jnp.logical_and((r // (2 * s)) == (c // (2 * s)),
                                     jnp.logical_and((r // s) % 2 == 1, (c // s) % 2 == 0)))
        s *= 2
    return dict(same=same, tril=jnp.logical_and(same, r >= c), strict=jnp.logical_and(same, r > c),
                last=c == (r // CHUNK) * CHUNK + (CHUNK - 1), eye=r == c, pair=(r // 2) == (c // 2), below=below)


def _inv_unit_lower(l_mats, mk):
    eye_f = mk["eye"].astype(F32)
    ts = [eye_f - jnp.where(mk["pair"], l_mat, 0.0) for l_mat in l_mats]
    for below in mk["below"]:
        halves = [_split_bf16(t) for t in ts]
        mids = [_dot3(h, jnp.where(below, l_mat, 0.0)) for h, l_mat in zip(halves, l_mats)]
        ts = [t - _dot3(m, h) for t, m, h in zip(ts, mids, halves)]
    return ts


def _unfold_blocks(folded, mask):
    n = folded.shape[0]
    return jnp.where(mask, jnp.concatenate([folded] * (n // CHUNK), axis=1), 0.0)


def _head_cols(beta, gc, gc_t, h):
    lane = lax.broadcasted_iota(jnp.int32, beta.shape, 1)
    sub = lax.broadcasted_iota(jnp.int32, gc_t.shape, 0)
    bcol = _rowsum(jnp.where(lane == h, beta, 0.0))
    gcol = _rowsum(jnp.where(lane == h, gc, 0.0))
    grow = _colsum(jnp.where(sub == h, gc_t, 0.0))
    return bcol, gcol, grow


def _prep_common(q, k, bcol, gcol, grow, mk, t_folded=None):
    n = q.shape[0]
    tril = mk["tril"]
    decay = jnp.where(tril, jnp.exp(jnp.where(tril, gcol - grow, 0.0)), 0.0)
    glast = _rowsum(jnp.where(mk["last"], jnp.broadcast_to(grow, (n, n)), 0.0))
    e = jnp.exp(gcol)
    ekt = jnp.exp(glast - gcol)
    kb = k * bcol
    kk = _dot(kb, k, NT)
    qk = _dot(q, k, NT)
    p = dict(decay=decay, e=e, ekt=ekt, kb=kb, kk=kk, qk=qk)
    if t_folded is not None:
        p["t"] = _unfold_blocks(t_folded, mk["same"])
    return p


GROUPS_PER_STEP = 4
SCAN_CHUNKS_PER_STEP = 4


def _fold_blocks(m):
    n = m.shape[0]
    out = m[:, 0:CHUNK]
    for b in range(1, n // CHUNK):
        out = out + m[:, b * CHUNK:(b + 1) * CHUNK]
    return out


def _gdr_prep_fwd(qkvn, beta, gc, gc_t):
    s_dim = qkvn.shape[0]
    tg = min(GROUP, s_dim)
    n_sub = min(GROUPS_PER_STEP, s_dim // tg)
    tb = tg * n_sub

    def body(q_ref, k_ref, v_ref, b_ref, g_ref, gt_ref, u_ref, w_ref, qd_ref, kt_ref, a_ref, t_ref):
        h = pl.program_id(0)
        mk = _group_masks(tg)
        parts = []
        for s in range(n_sub):
            rows = slice(s * tg, (s + 1) * tg)
            q, k, v = q_ref[rows, :], k_ref[rows, :], v_ref[rows, :]
            bcol, gcol, grow = _head_cols(b_ref[rows, :], g_ref[rows, :], gt_ref[:, rows], h)
            p = _prep_common(q, k, bcol, gcol, grow, mk)
            qd_ref[rows, :] = q * p["e"]
            kt_ref[rows, :] = k * p["ekt"]
            a_ref[rows, :] = _fold_blocks(jnp.where(mk["tril"], p["qk"] * p["decay"], 0.0))
            parts.append((rows, v * bcol, p["kb"] * p["e"], jnp.where(mk["strict"], p["kk"] * p["decay"], 0.0)))
        t_mats = _inv_unit_lower([part[3] for part in parts], mk)
        for (rows, vb, kbe, _), t_mat in zip(parts, t_mats):
            u_ref[rows, :] = _dot(t_mat, vb)
            w_ref[rows, :] = _dot(t_mat, kbe)
            t_ref[rows, :] = _fold_blocks(t_mat)

    row = lambda off: pl.BlockSpec((tb, HEAD), functools.partial(lambda h, m, off: (m, h + off), off=off))
    full = pl.BlockSpec((tb, LANES), lambda h, m: (m, 0))
    o_spec = pl.BlockSpec((tb, HEAD), lambda h, m: (m, h))
    a_spec = pl.BlockSpec((None, tb, CHUNK), lambda h, m: (h, m, 0))
    wide = jax.ShapeDtypeStruct((s_dim, N_HEADS * HEAD), F32)
    folded = jax.ShapeDtypeStruct((N_HEADS, s_dim, CHUNK), F32)
    return pl.pallas_call(
        body,
        out_shape=[wide, wide, wide, wide, folded, folded],
        grid=(N_HEADS, s_dim // tb),
        in_specs=[row(0), row(N_HEADS), row(2 * N_HEADS), full, full, pl.BlockSpec((8, tb), lambda h, m: (0, m))],
        out_specs=[o_spec, o_spec, o_spec, o_spec, a_spec, a_spec],
        compiler_params=pltpu.CompilerParams(dimension_semantics=("parallel", "parallel")),
        name="gdr_prep_fwd",
    )(qkvn, qkvn, qkvn, beta, gc, gc_t)


def _gdr_prep_bwd(qkvn, beta, gc, gc_t, t_fold, u, w, du, dw, dqd, dkt, d_a):
    s_dim = qkvn.shape[0]
    tg = min(GROUP, s_dim)
    n_sub = min(GROUPS_PER_STEP, s_dim // tg)
    tb = tg * n_sub

    def body(q_ref, k_ref, v_ref, b_ref, g_ref, gt_ref, t_ref, u_ref, w_ref, du_ref, dw_ref, dqd_ref, dkt_ref,
             da_ref, dq_ref, dk_ref, dv_ref, db_ref, dg_ref):
        h = pl.program_id(1)

        @pl.when(h == 0)
        def _():
            db_ref[...] = jnp.zeros_like(db_ref)
            dg_ref[...] = jnp.zeros_like(dg_ref)

        mk = _group_masks(tg)
        lane = lax.broadcasted_iota(jnp.int32, (tg, LANES), 1)
        for s in range(n_sub):
            rows = slice(s * tg, (s + 1) * tg)
            q, k, v = q_ref[rows, :], k_ref[rows, :], v_ref[rows, :]
            bcol, gcol, grow = _head_cols(b_ref[rows, :], g_ref[rows, :], gt_ref[:, rows], h)
            p = _prep_common(q, k, bcol, gcol, grow, mk, t_ref[rows, :])
            t_mat, decay, e, ekt, kb = p["t"], p["decay"], p["e"], p["ekt"], p["kb"]
            du_, dw_, dqd_, dkt_ = du_ref[rows, :], dw_ref[rows, :], dqd_ref[rows, :], dkt_ref[rows, :]
            dvb = _dot(t_mat, du_, TN)
            dkbe = _dot(t_mat, dw_, TN)
            d_l = -(_dot(dvb, u_ref[rows, :], NT) + _dot(dkbe, w_ref[rows, :], NT))
            m1 = jnp.where(mk["strict"], d_l, 0.0)
            m2 = _unfold_blocks(da_ref[rows, :], mk["tril"])
            d_kk = m1 * decay
            d_qk = m2 * decay
            d_decay = m1 * p["kk"] + m2 * p["qk"]
            dkb = _dot(d_kk, k) + dkbe * e
            dk = _dot(d_kk, kb, TN) + _dot(d_qk, q, TN) + dkt_ * ekt + dkb * bcol
            dq = _dot(d_qk, k) + dqd_ * e
            d_beta = _rowsum(dkb * k) + _rowsum(dvb * v)
            d_e = _rowsum(dkbe * kb) + _rowsum(dqd_ * q)
            d_ekt = _rowsum(dkt_ * k) * ekt
            d_diff = d_decay * decay
            d_grow = -_colsum(d_diff) + _colsum(jnp.where(mk["last"], jnp.broadcast_to(d_ekt, (tg, tg)), 0.0))
            d_gcol = d_e * e - d_ekt + _rowsum(d_diff)
            d_gcol = d_gcol + _rowsum(jnp.where(mk["eye"], jnp.broadcast_to(d_grow, (tg, tg)), 0.0))
            dq_ref[rows, :] = dq
            dk_ref[rows, :] = dk
            dv_ref[rows, :] = dvb * bcol
            db_ref[rows, :] = jnp.where(lane == h, d_beta, db_ref[rows, :])
            dg_ref[rows, :] = jnp.where(lane == h, d_gcol, dg_ref[rows, :])

    row = lambda off: pl.BlockSpec((tb, HEAD), functools.partial(lambda m, h, off: (m, h + off), off=off))
    full = pl.BlockSpec((tb, LANES), lambda m, h: (m, 0))
    o_spec = pl.BlockSpec((tb, HEAD), lambda m, h: (m, h))
    a_spec = pl.BlockSpec((None, tb, CHUNK), lambda m, h: (h, m, 0))
    wide = jax.ShapeDtypeStruct((s_dim, N_HEADS * HEAD), F32)
    lanes = jax.ShapeDtypeStruct((s_dim, LANES), F32)
    return pl.pallas_call(
        body,
        out_shape=[wide, wide, wide, lanes, lanes],
        grid=(s_dim // tb, N_HEADS),
        in_specs=[row(0), row(N_HEADS), row(2 * N_HEADS), full, full, pl.BlockSpec((8, tb), lambda m, h: (0, m)),
                  a_spec, o_spec, o_spec, o_spec, o_spec, o_spec, o_spec, a_spec],
        out_specs=[o_spec, o_spec, o_spec, full, full],
        compiler_params=pltpu.CompilerParams(dimension_semantics=("parallel", "arbitrary")),
        name="gdr_prep_bwd",
    )(qkvn, qkvn, qkvn, beta, gc, gc_t, t_fold, u, w, du, dw, dqd, dkt, d_a)


def _gdr_scan_fwd(u, w, qd, kt, a_mat, gc):
    s_dim = u.shape[0]
    n_chunks = s_dim // CHUNK
    per = min(SCAN_CHUNKS_PER_STEP, n_chunks)
    tb = per * CHUNK

    def body(u_ref, w_ref, qd_ref, kt_ref, a_ref, g_ref, o_ref, st_ref, state):
        @pl.when(pl.program_id(0) == 0)
        def _():
            state[...] = jnp.zeros_like(state)

        heads = range(N_HEADS)
        cols = [slice(h * HEAD, (h + 1) * HEAD) for h in heads]
        for i in range(per):
            rows = slice(i * CHUNK, (i + 1) * CHUNK)
            egl = jnp.exp(g_ref[(i + 1) * CHUNK - 1:(i + 1) * CHUNK, :])
            s_b = [state[h].astype(BF16) for h in heads]
            for h in heads:
                st_ref[i, h] = state[h]
            ws = [_dot(w_ref[rows, cs], s) for cs, s in zip(cols, s_b)]
            qs = [_dot(qd_ref[rows, cs], s) for cs, s in zip(cols, s_b)]
            vns = [(u_ref[rows, cs] - ws_h).astype(BF16) for cs, ws_h in zip(cols, ws)]
            avs = [_dot(a_ref[h, rows, :], vn) for h, vn in zip(heads, vns)]
            kvs = [_dot(kt_ref[rows, cs], vn, TN) for cs, vn in zip(cols, vns)]
            for h, cs in zip(heads, cols):
                o_ref[rows, cs] = qs[h] + avs[h]
                state[h] = state[h] * egl[:, h:h + 1] + kvs[h]

    wide = pl.BlockSpec((tb, N_HEADS * HEAD), lambda n: (n, 0))
    return pl.pallas_call(
        body,
        out_shape=[jax.ShapeDtypeStruct((s_dim, N_HEADS * HEAD), F32),
                   jax.ShapeDtypeStruct((n_chunks, N_HEADS, HEAD, HEAD), F32)],
        grid=(n_chunks // per,),
        in_specs=[wide, wide, wide, wide, pl.BlockSpec((N_HEADS, tb, CHUNK), lambda n: (0, n, 0)),
                  pl.BlockSpec((tb, LANES), lambda n: (n, 0))],
        out_specs=[wide, pl.BlockSpec((per, N_HEADS, HEAD, HEAD), lambda n: (n, 0, 0, 0))],
        scratch_shapes=[pltpu.VMEM((N_HEADS, HEAD, HEAD), F32)],
        compiler_params=pltpu.CompilerParams(dimension_semantics=("arbitrary",)),
        name="gdr_scan_fwd",
    )(u, w, qd, kt, a_mat, gc)


def _gdr_scan_bwd(u, w, qd, kt, a_mat, gc, states, d_o):
    s_dim = u.shape[0]
    n_chunks = s_dim // CHUNK
    per = min(SCAN_CHUNKS_PER_STEP, n_chunks)
    tb = per * CHUNK
    last = n_chunks // per - 1

    def body(u_ref, w_ref, qd_ref, kt_ref, a_ref, g_ref, st_ref, do_ref,
             du_ref, dw_ref, dqd_ref, dkt_ref, da_ref, de_ref, d_state):
        @pl.when(pl.program_id(0) == 0)
        def _():
            d_state[...] = jnp.zeros_like(d_state)

        heads = range(N_HEADS)
        cols = [slice(h * HEAD, (h + 1) * HEAD) for h in heads]
        for i in reversed(range(per)):
            rows = slice(i * CHUNK, (i + 1) * CHUNK)
            egl = jnp.exp(g_ref[(i + 1) * CHUNK - 1:(i + 1) * CHUNK, :])
            s_b = [st_ref[i, h].astype(BF16) for h in heads]
            ds_b = [d_state[h].astype(BF16) for h in heads]
            dos = [do_ref[rows, cs].astype(BF16) for cs in cols]
            w_b = [w_ref[rows, cs].astype(BF16) for cs in cols]
            ws = [_dot(w_h, s) for w_h, s in zip(w_b, s_b)]
            ados = [_dot(a_ref[h, rows, :], do, TN) for h, do in zip(heads, dos)]
            kds = [_dot(kt_ref[rows, cs], ds) for cs, ds in zip(cols, ds_b)]
            dqds = [_dot(do, s, NT) for do, s in zip(dos, s_b)]
            qdos = [_dot(qd_ref[rows, cs], do, TN) for cs, do in zip(cols, dos)]
            vns = [(u_ref[rows, cs] - ws_h).astype(BF16) for cs, ws_h in zip(cols, ws)]
            dvns = [a + k_ for a, k_ in zip(ados, kds)]
            dvn_b = [d.astype(BF16) for d in dvns]
            das = [_dot(do, vn, NT) for do, vn in zip(dos, vns)]
            dkts = [_dot(vn, ds, NT) for vn, ds in zip(vns, ds_b)]
            dws = [_dot(d, s, NT) for d, s in zip(dvn_b, s_b)]
            wds = [_dot(w_h, d, TN) for w_h, d in zip(w_b, dvn_b)]
            for h, cs in zip(heads, cols):
                ds_n = d_state[h]
                de = jnp.sum(_rowsum(ds_n * st_ref[i, h]), axis=0, keepdims=True)
                de_ref[i, h:h + 1, :] = jnp.broadcast_to(de, (1, LANES))
                dqd_ref[rows, cs] = dqds[h]
                da_ref[h, rows, :] = das[h]
                dkt_ref[rows, cs] = dkts[h]
                du_ref[rows, cs] = dvns[h]
                dw_ref[rows, cs] = -dws[h]
                d_state[h] = ds_n * egl[:, h:h + 1] + qdos[h] - wds[h]

    wide = pl.BlockSpec((tb, N_HEADS * HEAD), lambda n: (last - n, 0))
    a_spec = pl.BlockSpec((N_HEADS, tb, CHUNK), lambda n: (0, last - n, 0))
    wide_shape = jax.ShapeDtypeStruct((s_dim, N_HEADS * HEAD), F32)
    return pl.pallas_call(
        body,
        out_shape=[wide_shape, wide_shape, wide_shape, wide_shape,
                   jax.ShapeDtypeStruct((N_HEADS, s_dim, CHUNK), F32),
                   jax.ShapeDtypeStruct((n_chunks, N_HEADS, LANES), F32)],
        grid=(n_chunks // per,),
        in_specs=[wide, wide, wide, wide, a_spec, pl.BlockSpec((tb, LANES), lambda n: (last - n, 0)),
                  pl.BlockSpec((per, N_HEADS, HEAD, HEAD), lambda n: (last - n, 0, 0, 0)), wide],
        out_specs=[wide, wide, wide, wide, a_spec, pl.BlockSpec((per, N_HEADS, LANES), lambda n: (last - n, 0, 0))],
        scratch_shapes=[pltpu.VMEM((N_HEADS, HEAD, HEAD), F32)],
        compiler_params=pltpu.CompilerParams(dimension_semantics=("arbitrary",)),
        name="gdr_scan_bwd",
    )(u, w, qd, kt, a_mat, gc, states, d_o)


FUSED_ROWS = 512


def _gdr_out_fwd(o_dn, proj_a, dn_w, w_br):
    def fn(r, c):
        o, z = r
        w_, w_br_ = c
        outs = []
        for h in range(N_HEADS):
            cs = slice(h * HEAD, (h + 1) * HEAD)
            oh, zh = o[:, cs], z[:, cs]
            rr = lax.rsqrt(_rowmean(oh * oh) + EPS_RMS)
            outs.append(oh * rr * w_ * (zh * _sig(zh)))
        og = jnp.concatenate(outs, axis=1).astype(BF16)
        return [og, _dot(og, w_br_)], []

    return _rowwise(fn, [o_dn, (proj_a, 3, D_MODEL)], [dn_w, w_br], [(D_MODEL, BF16), (D_MODEL, BF16)],
                    tm=FUSED_ROWS, name="gdr_out_fwd")


def _gdr_out_bwd(o_dn, proj_a, d_y_dn, dn_w, w_br):
    def fn(r, c):
        o, z, dy = r
        w_, w_br_ = c
        dg = _dot(dy, w_br_, NT)
        d_o, d_z = [], []
        d_w = jnp.zeros((1, HEAD), F32)
        for h in range(N_HEADS):
            cs = slice(h * HEAD, (h + 1) * HEAD)
            oh, zh, dgh = o[:, cs], z[:, cs], dg[:, cs]
            rr = lax.rsqrt(_rowmean(oh * oh) + EPS_RMS)
            sz = zh * _sig(zh)
            d_n = dgh * sz
            d_z.append(dgh * (oh * rr * w_) * _silu_grad(zh))
            d_w = d_w + _colsum(d_n * oh * rr)
            gw = d_n * w_
            d_o.append(rr * gw - oh * (rr * rr * rr) * _rowmean(gw * oh))
        return [jnp.concatenate(d_o, axis=1), jnp.concatenate(d_z, axis=1)], [d_w]

    return _rowwise(fn, [o_dn, (proj_a, 3, D_MODEL), d_y_dn], [dn_w, w_br], [(D_MODEL, F32), (D_MODEL, BF16)],
                    accs=[(1, HEAD)], tm=FUSED_ROWS, name="gdr_out_bwd")


def _rms_fwd(x, w):
    r = lax.rsqrt(_rowmean(x * x) + EPS_RMS)
    return x * r * w


def _rms_bwd(x, w, dy):
    r = lax.rsqrt(_rowmean(x * x) + EPS_RMS)
    gw = dy * w
    return r * gw - x * (r * r * r) * _rowmean(gw * x), _colsum(dy * x * r)


def _rope_consts():
    inv = ROPE_BASE ** (-np.arange(0, ROPE, 2, dtype=np.float32) / ROPE)
    t = np.zeros((4, LANES), np.float32)
    t[0, :32] = inv
    t[0, 32:64] = inv
    t[1, :64] = 1.0
    t[2, 32:64] = 1.0
    t[3, :32] = -1.0
    return jnp.asarray(t)


def _rope_tables(pos, consts, width):
    ang = pos * consts[0:1, :]
    cosv, sinv = jnp.cos(ang), jnp.sin(ang)
    reps = width // LANES
    tile = (lambda t: jnp.concatenate([t] * reps, axis=1)) if reps > 1 else (lambda t: t)
    return tile(cosv * consts[1:2, :]), tile(sinv * consts[2:3, :]), tile(sinv * consts[3:4, :])


def _rope_apply(t, tabs):
    cos_t, sin_a, sin_b = tabs
    width = t.shape[1]
    return t * cos_t + pltpu.roll(t, 32, 1) * sin_a + pltpu.roll(t, width - 32, 1) * sin_b


def _rope_transpose(d, tabs):
    cos_t, sin_a, sin_b = tabs
    width = d.shape[1]
    return d * cos_t + pltpu.roll(d * sin_a, width - 32, 1) + pltpu.roll(d * sin_b, 32, 1)


QK_HEAD = 2 * HEAD


def _interleave_heads(a, b):
    parts = []
    for h in range(N_HEADS):
        parts.append(a[:, h * HEAD:(h + 1) * HEAD])
        parts.append(b if b.shape[1] == LANES else b[:, h * LANES:(h + 1) * LANES])
    return jnp.concatenate(parts, axis=1)


def _mla_rows(proj_b):
    return [(proj_b, WB_CQ // Q_LORA, Q_LORA), (proj_b, WB_CKV // KV_LORA, KV_LORA), (proj_b, WB_KR // LANES, LANES)]


def _mla_prep_fwd(proj_b, pos, qn_w, kvn_w, uq, uk, uv):
    def fn(r, c):
        cq, ckv, kr, pos_ = r
        qn_w_, kvn_w_, uq_, uk_, uv_, rope = c
        c_q = _rms_fwd(cq, qn_w_).astype(BF16)
        c_kv = _rms_fwd(ckv, kvn_w_).astype(BF16)
        qf = _dot(c_q, uq_)
        qr = _rope_apply(qf[:, D_MODEL:], _rope_tables(pos_, rope, D_MODEL))
        kr = _rope_apply(kr, _rope_tables(pos_, rope, LANES))
        kc = _interleave_heads(_dot(c_kv, uk_), kr)
        v = _dot(c_kv, uv_)
        return [c_q, c_kv, _interleave_heads(qf[:, :D_MODEL], qr) * SCALE, kc, v, kc, v], []

    wide2 = N_HEADS * QK_HEAD
    return _rowwise(fn, _mla_rows(proj_b) + [pos], [qn_w, kvn_w, uq, uk, uv, _rope_consts()],
                    [(Q_LORA, BF16), (KV_LORA, BF16), (wide2, BF16), (wide2, BF16), (D_MODEL, BF16),
                     (wide2, BF16, "T"), (D_MODEL, BF16, "T")], tm=FUSED_ROWS, name="mla_prep_fwd")


def _mla_prep_bwd(proj_b, pos, d_qc, d_kc, d_v, qn_w, kvn_w, uq, uk, uv):
    def fn(r, c):
        cq, ckv, _, pos_, dq, dk, dv = r
        qn_w_, kvn_w_, uq_, uk_, uv_, rope = c
        even = lambda t: jnp.concatenate([t[:, (2 * h) * LANES:(2 * h + 1) * LANES] for h in range(N_HEADS)], axis=1)
        odd = lambda t: jnp.concatenate([t[:, (2 * h + 1) * LANES:(2 * h + 2) * LANES] for h in range(N_HEADS)], axis=1)
        d_qr_raw = _rope_transpose(odd(dq), _rope_tables(pos_, rope, D_MODEL)) * SCALE
        d_qf = jnp.concatenate([even(dq) * SCALE, d_qr_raw], axis=1).astype(BF16)
        d_kn = even(dk).astype(BF16)
        dkr = dk[:, LANES:2 * LANES]
        for h in range(1, N_HEADS):
            dkr = dkr + dk[:, (2 * h + 1) * LANES:(2 * h + 2) * LANES]
        d_cq, d_qnw = _rms_bwd(cq, qn_w_, _dot(d_qf, uq_, NT))
        d_ckv, d_kvnw = _rms_bwd(ckv, kvn_w_, _dot(d_kn, uk_, NT) + _dot(dv, uv_, NT))
        return [d_qf, d_kn, d_cq, d_ckv, _rope_transpose(dkr, _rope_tables(pos_, rope, LANES))], [d_qnw, d_kvnw]

    return _rowwise(fn, _mla_rows(proj_b) + [pos, d_qc, d_kc, d_v], [qn_w, kvn_w, uq, uk, uv, _rope_consts()],
                    [(2 * D_MODEL, BF16), (D_MODEL, BF16), (Q_LORA, BF16), (KV_LORA, BF16), (LANES, BF16)],
                    accs=[(1, Q_LORA), (1, KV_LORA)], tm=FUSED_ROWS, name="mla_prep_bwd")


def _causal_mask_t(st, key0, query0):
    key = lax.broadcasted_iota(jnp.int32, st.shape, 0) + key0
    query = lax.broadcasted_iota(jnp.int32, st.shape, 1) + query0
    return jnp.where(key <= query, st, NEG_BIG)


def _attn_tiles(s_dim):
    tq = min(512, s_dim)
    n_chains = 2 if s_dim >= 2 * tq else 1
    return tq, n_chains, min(512, s_dim)


def _diagonal_chains(t, tq, n_chains, tk):
    return [(c, (t + 1) * tk - 1 > c * tq) for c in range(n_chains) if t * tk < (c + 1) * tq]


def _attn_fwd(qc, kc, vt):
    s_dim = qc.shape[0]
    tq, n_chains, tk = _attn_tiles(s_dim)
    tqs = tq * n_chains

    def body(q_ref, k_ref, vt_ref, o_ref, lse_ref, m_s, l_s, acc):
        qi = pl.program_id(1)
        m_s[...] = jnp.full_like(m_s, NEG_BIG)
        l_s[...] = jnp.zeros_like(l_s)
        acc[...] = jnp.zeros_like(acc)

        def make_step(chains):
            def step(j, carry):
                ks = pl.multiple_of(j * tk, tk)
                kb, vtb = k_ref[pl.ds(ks, tk), :], vt_ref[:, pl.ds(ks, tk)]
                cols = [slice(c * tq, (c + 1) * tq) for c, _ in chains]
                sts = [_dot(kb, q_ref[cs, :], NT) for cs in cols]
                sts = [_causal_mask_t(st, j * tk, qi * tqs + c * tq) if masked else st
                       for st, (c, masked) in zip(sts, chains)]
                m_prevs = [m_s[:, cs] for cs in cols]
                m_news = [jnp.maximum(mp, jnp.max(st, axis=0, keepdims=True)) for mp, st in zip(m_prevs, sts)]
                alphas = [jnp.exp(mp - mn) for mp, mn in zip(m_prevs, m_news)]
                pts = [jnp.exp(st - mn) for st, mn in zip(sts, m_news)]
                pvs = [_dot(vtb, pt) for pt in pts]
                for cs, mn, al, pt, pv in zip(cols, m_news, alphas, pts, pvs):
                    l_s[:, cs] = al * l_s[:, cs] + _colsum(pt)
                    m_s[:, cs] = mn
                    acc[:, cs] = acc[:, cs] * al + pv
                return carry
            return step

        below = qi * (tqs // tk)
        lax.fori_loop(0, below, make_step([(c, False) for c in range(n_chains)]), 0)
        for t in range(tqs // tk):
            make_step(_diagonal_chains(t, tq, n_chains, tk))(below + t, 0)
        l = l_s[...]
        o_ref[...] = jnp.transpose(acc[...] / l)
        lse_ref[...] = m_s[...] + jnp.log(l)

    return pl.pallas_call(
        body,
        out_shape=[jax.ShapeDtypeStruct((s_dim, N_HEADS * HEAD), F32), jax.ShapeDtypeStruct((N_HEADS, 1, s_dim), F32)],
        grid=(N_HEADS, s_dim // tqs),
        in_specs=[pl.BlockSpec((tqs, QK_HEAD), lambda h, qi: (qi, h)),
                  pl.BlockSpec((s_dim, QK_HEAD), lambda h, qi: (0, h)),
                  pl.BlockSpec((HEAD, s_dim), lambda h, qi: (h, 0))],
        out_specs=[pl.BlockSpec((tqs, HEAD), lambda h, qi: (qi, h)),
                   pl.BlockSpec((None, 1, tqs), lambda h, qi: (h, 0, qi))],
        scratch_shapes=[pltpu.VMEM((1, tqs), F32), pltpu.VMEM((1, tqs), F32), pltpu.VMEM((HEAD, tqs), F32)],
        compiler_params=pltpu.CompilerParams(dimension_semantics=("parallel", "parallel")),
        name="attn_fwd",
    )(qc, kc, vt)


def _attn_bwd(qc, kc, kct, v, o, d_o, lse):
    s_dim = qc.shape[0]
    tq, n_chains, tk = _attn_tiles(s_dim)
    tqs = tq * n_chains

    def body(q_ref, k_ref, kt_ref, v_ref, o_ref, do_ref, lse_ref, dq_ref, dk_ref, dv_ref, dqt_acc, dv_acc):
        qi = pl.program_id(1)

        @pl.when(qi == 0)
        def _():
            dk_ref[...] = jnp.zeros_like(dk_ref)
            dv_acc[...] = jnp.zeros_like(dv_acc)

        dqt_acc[...] = jnp.zeros_like(dqt_acc)
        do_f = do_ref[...]
        do_all = do_f.astype(BF16)
        q_all = q_ref[...]
        lse_row = lse_ref[...]
        delta_row = _dot3(jnp.ones((8, HEAD), F32), o_ref[...] * do_f, NT)[0:1, :]

        def make_step(chains):
            rows = slice(chains[0][0] * tq, (chains[-1][0] + 1) * tq)

            def step(j, carry):
                ks = pl.multiple_of(j * tk, tk)
                kb, vb, ktb = k_ref[pl.ds(ks, tk), :], v_ref[pl.ds(ks, tk), :], kt_ref[:, pl.ds(ks, tk)]
                cols = [slice(c * tq, (c + 1) * tq) for c, _ in chains]
                sts = [_dot(kb, q_all[cs, :], NT) for cs in cols]
                sts = [_causal_mask_t(st, j * tk, qi * tqs + c * tq) if masked else st
                       for st, (c, masked) in zip(sts, chains)]
                dpts = [_dot(vb, do_all[cs, :], NT) for cs in cols]
                pts = [jnp.exp(st - lse_row[:, cs]) for st, cs in zip(sts, cols)]
                dsts = [(pt * (dpt - delta_row[:, cs])).astype(BF16) for pt, dpt, cs in zip(pts, dpts, cols)]
                pts = [pt.astype(BF16) for pt in pts]
                dqs = [_dot(ktb, dst) for dst in dsts]
                for cs, dq in zip(cols, dqs):
                    dqt_acc[:, cs] += dq
                pt_all = jnp.concatenate(pts, axis=1) if len(chains) > 1 else pts[0]
                dst_all = jnp.concatenate(dsts, axis=1) if len(chains) > 1 else dsts[0]
                dk_ref[pl.ds(ks, tk), :] += _dot(dst_all, q_all[rows, :])
                dv_acc[pl.ds(ks, tk), :] += _dot(pt_all, do_all[rows, :])
                return carry
            return step

        below = qi * (tqs // tk)
        lax.fori_loop(0, below, make_step([(c, False) for c in range(n_chains)]), 0)
        for t in range(tqs // tk):
            make_step(_diagonal_chains(t, tq, n_chains, tk))(below + t, 0)
        dq_ref[...] = jnp.transpose(dqt_acc[...])

        @pl.when(qi == s_dim // tqs - 1)
        def _():
            dv_ref[...] = dv_acc[...].astype(dv_ref.dtype)

    q_spec = pl.BlockSpec((tqs, QK_HEAD), lambda h, qi: (qi, h))
    o_spec = pl.BlockSpec((tqs, HEAD), lambda h, qi: (qi, h))
    k_spec = pl.BlockSpec((s_dim, QK_HEAD), lambda h, qi: (0, h))
    v_spec = pl.BlockSpec((s_dim, HEAD), lambda h, qi: (0, h))
    wide2 = jax.ShapeDtypeStruct((s_dim, N_HEADS * QK_HEAD), F32)
    return pl.pallas_call(
        body,
        out_shape=[wide2, wide2, jax.ShapeDtypeStruct((s_dim, N_HEADS * HEAD), BF16)],
        grid=(N_HEADS, s_dim // tqs),
        in_specs=[q_spec, k_spec, pl.BlockSpec((QK_HEAD, s_dim), lambda h, qi: (h, 0)), v_spec, o_spec, o_spec,
                  pl.BlockSpec((None, 1, tqs), lambda h, qi: (h, 0, qi))],
        out_specs=[q_spec, k_spec, v_spec],
        scratch_shapes=[pltpu.VMEM((QK_HEAD, tqs), F32), pltpu.VMEM((s_dim, HEAD), F32)],
        compiler_params=pltpu.CompilerParams(dimension_semantics=("parallel", "arbitrary")),
        name="attn_bwd",
    )(qc, kc, kct, v, o, d_o, lse)


def _mix_proj_ln1(y_dn, y_mla, proj_g, x, w_o, g, b):
    s_dim = x.shape[0]
    tm = min(512, s_dim)

    def body(yd_ref, ym_ref, g_ref, x_ref, w_ref, lg_ref, lb_ref, mixed_ref, a1_ref, h1_ref, h1b_ref):
        gates = g_ref[...].astype(F32)
        mixed = (_sig(gates[:, :D_MODEL]) * yd_ref[...].astype(F32)
                 + _sig(gates[:, D_MODEL:]) * ym_ref[...].astype(F32)).astype(BF16)
        a1 = _dot(mixed, w_ref[...])
        xh, _ = _ln_stats(ALPHA * x_ref[...] + a1)
        y = xh * lg_ref[...] + lb_ref[...]
        mixed_ref[...] = mixed
        a1_ref[...] = a1
        h1_ref[...] = y
        h1b_ref[...] = y.astype(BF16)

    row = lambda width: pl.BlockSpec((tm, width), lambda i: (i, 0))
    whole = lambda a: pl.BlockSpec(a.shape, lambda i: (0, 0))
    sds = lambda dt: jax.ShapeDtypeStruct((s_dim, D_MODEL), dt)
    return pl.pallas_call(
        body,
        out_shape=[sds(BF16), sds(F32), sds(F32), sds(BF16)],
        grid=(s_dim // tm,),
        in_specs=[row(D_MODEL), row(D_MODEL), row(2 * D_MODEL), row(D_MODEL), whole(w_o), whole(g), whole(b)],
        out_specs=[row(D_MODEL)] * 4,
        compiler_params=pltpu.CompilerParams(dimension_semantics=("parallel",)),
        name="mix_proj_ln1",
    )(y_dn, y_mla, proj_g, x, w_o, g, b)


def _ln1_mix_bwd(x, a1, d_h1, d_pg, y_dn, y_mla, proj_g, g, w_o, w_pg):
    def fn(r, c):
        x_, a1_, dy, dpg, yd, ym, gates = r
        g_, w_o_, w_pg_ = c
        dy = dy + _dot(dpg, w_pg_, NT)
        xh, rr = _ln_stats(ALPHA * x_ + a1_)
        dz = _ln_bwd(dy, xh, rr, g_)
        dz_b = dz.astype(BF16)
        dm = _dot(dz_b, w_o_, NT)
        sd, sm = _sig(gates[:, :D_MODEL]), _sig(gates[:, D_MODEL:])
        d_g = jnp.concatenate([dm * yd * sd * (1.0 - sd), dm * ym * sm * (1.0 - sm)], axis=1)
        return [dz_b, ALPHA * dz, d_g, dm * sd, dm * sm], [_colsum(dy * xh), _colsum(dy)]

    return _rowwise(fn, [x, a1, d_h1, d_pg, y_dn, y_mla, proj_g], [g, w_o, w_pg],
                    [(D_MODEL, BF16), (D_MODEL, F32), (2 * D_MODEL, BF16), (D_MODEL, BF16), (D_MODEL, BF16)],
                    accs=[(1, D_MODEL), (1, D_MODEL)], tm=FUSED_ROWS, name="ln1_mix_bwd")


def _ln_stats(z):
    mu = _rowmean(z)
    zc = z - mu
    r = lax.rsqrt(_rowmean(zc * zc) + EPS_LN)
    return zc * r, r


def _ln_bwd(dy, xh, r, g):
    dxh = dy * g
    return r * (dxh - _rowmean(dxh) - xh * _rowmean(dxh * xh))


def _ffn_in_act(h1b, w_t):
    s_dim, k_dim = h1b.shape
    hidden = w_t.shape[0] // 2
    tm, tn = min(512, s_dim), _pick_wide(hidden)
    nt = hidden // tn

    def body(a_ref, bg_ref, bu_ref, gt_ref, up_ref, act_ref):
        a = a_ref[...]
        gt, up = _dot(a, bg_ref[...], NT), _dot(a, bu_ref[...], NT)
        gt_ref[...] = gt.astype(BF16)
        up_ref[...] = up.astype(BF16)
        act_ref[...] = (gt * _sig(gt) * up).astype(BF16)

    o_spec = pl.BlockSpec((tm, tn), lambda j, i: (i, j))
    sds = jax.ShapeDtypeStruct((s_dim, hidden), BF16)
    return pl.pallas_call(
        body,
        out_shape=[sds, sds, sds],
        grid=(nt, s_dim // tm),
        in_specs=[pl.BlockSpec((tm, k_dim), lambda j, i: (i, 0)), pl.BlockSpec((tn, k_dim), lambda j, i: (j, 0)),
                  pl.BlockSpec((tn, k_dim), lambda j, i: (j + nt, 0))],
        out_specs=[o_spec, o_spec, o_spec],
        compiler_params=pltpu.CompilerParams(dimension_semantics=("parallel", "parallel")),
        name="ffn_in_act",
    )(h1b, w_t, w_t)


def _act_bwd(gt, up, d_act):
    def fn(r, c):
        gt_, up_, da = r
        return [jnp.concatenate([da * up_ * _silu_grad(gt_), da * gt_ * _sig(gt_)], axis=1)], []

    return _rowwise(fn, [gt, up, d_act], [], [(2 * FFN_HIDDEN, BF16)], name="act_bwd")[0]


def _tail(h1, ffn, p, tgt, g, b, w_pg, w_ple_t):
    def fn(r, c):
        h1_, ffn_, p_, t_ = r
        pg_ = _dot(h1_, c[2])
        pp_ = _dot(p_, c[3], NT)
        sp = _sig(pg_)
        xh, rr = _ln_stats(ALPHA * h1_ + ffn_ + sp * pp_)
        y = xh * c[0] + c[1]
        err = y - t_
        dy = err * (1.0 / D_MODEL)
        dz = _ln_bwd(dy, xh, rr, c[0])
        loss = jnp.sum(0.5 * _rowmean(err * err), axis=0, keepdims=True)
        return ([dz, dz * pp_ * sp * (1.0 - sp), dz * sp, ALPHA * dz],
                [_colsum(dy * xh), _colsum(dy), jnp.broadcast_to(loss, (1, LANES))])

    return _rowwise(fn, [h1, ffn, p, tgt], [g, b, w_pg, w_ple_t], [(D_MODEL, BF16)] * 3 + [(D_MODEL, F32)],
                    accs=[(1, D_MODEL), (1, D_MODEL), (1, LANES)], tm=FUSED_ROWS, name="tail")


def _local_step(x, p, pos, tgt, w, late_weights, emit):
    w = dict(w)
    s_dim = x.shape[0]
    pb = p.astype(BF16)
    proj_a, proj_g, proj_b, xb = _input_proj(x, w["w_in_t"], w["wg_t"], w["wb_t"])
    qkvn = _conv_fwd(proj_a, w["conv"])
    beta, gc = _gates_fwd(proj_b, w["alog"], w["dtb"])
    gc_t = jnp.transpose(gc[:, :N_HEADS])
    u, w_, qd, kt, a_mat, t_fold = _gdr_prep_fwd(qkvn, beta, gc, gc_t)
    o_dn, states = _gdr_scan_fwd(u, w_, qd, kt, a_mat, gc)
    w.update(late_weights("mix", o_dn))
    og, y_dn = _gdr_out_fwd(o_dn, proj_a, w["dnw"], w["br_dn"])
    c_q, c_kv, qc, kc, vv, kct, vt = _mla_prep_fwd(proj_b, pos, w["qnw"], w["kvnw"], w["uq"], w["uk"], w["uv"])
    o_mla, lse = _attn_fwd(qc, kc, vt)
    y_mla = _mm_resident(o_mla, w["br_mla"], out_dtype=BF16, name="f_y_mla")
    mixed, a1, h1, h1b = _mix_proj_ln1(y_dn, y_mla, proj_g, x, w["wo"], w["ln1g"], w["ln1b"])
    w.update(late_weights("ffn", a1))
    gt, up, act = _ffn_in_act(h1b, w["ffn_in_t"])
    ffn = _mm_resident(act, w["ffn_out"], name="f_ffn")
    g = {}
    dz2, d_pg, d_pp, dh1a, g["ln2g"], g["ln2b"], loss = _tail(h1, ffn, pb, tgt, w["ln2g"], w["ln2b"],
                                                            w["ple_gate"], w["ple_t"])
    g["ple_t"] = _mm(d_pp, pb, ta=True, out_dtype=BF16, name="b_w_ple")
    g["ple_gate"] = _mm(h1b, d_pg, ta=True, out_dtype=BF16, name="b_w_ple_gate")
    g["ffn_out"] = _mm(act, dz2, ta=True, out_dtype=BF16, name="b_w_ffn_out")
    d_act = _mm_resident(dz2, w["ffn_out"], tb=True, out_dtype=BF16, name="b_act")
    d_gu = _act_bwd(gt, up, d_act)
    g["ffn_in_t"] = _mm(d_gu, h1b, ta=True, out_dtype=BF16, name="b_w_ffn_in")
    d_gu = emit("ffn", g, d_gu)
    d_h1 = _mm_resident(d_gu, w["ffn_in_t"], add=(dh1a,), name="b_h1_ffn")
    dz1, dxa, d_proj_g, d_y_dn, d_y_mla, g["ln1g"], g["ln1b"] = _ln1_mix_bwd(
        x, a1, d_h1, d_pg, y_dn, y_mla, proj_g, w["ln1g"], w["wo"], w["ple_gate"])
    g["wo"] = _mm(mixed, dz1, ta=True, out_dtype=BF16, name="b_w_o")
    g["br_mla"] = _mm(o_mla, d_y_mla, ta=True, out_dtype=BF16, name="b_w_br_mla")
    d_o_mla = _mm_resident(d_y_mla, w["br_mla"], tb=True, out_dtype=BF16, name="b_o_mla")
    d_qc, d_kc, d_v = _attn_bwd(qc, kc, kct, vv, o_mla, d_o_mla, lse)
    d_q_full, d_kn, d_cq, d_ckv, d_kr, g["qnw"], g["kvnw"] = _mla_prep_bwd(
        proj_b, pos, d_qc, d_kc, d_v, w["qnw"], w["kvnw"], w["uq"], w["uk"], w["uv"])
    g["uq"] = _mm(c_q, d_q_full, ta=True, out_dtype=BF16, name="b_w_uq")
    g["uk"] = _mm(c_kv, d_kn, ta=True, out_dtype=BF16, name="b_w_uk")
    g["uv"] = _mm(c_kv, d_v, ta=True, out_dtype=BF16, name="b_w_uv")
    g["br_dn"] = _mm(og, d_y_dn, ta=True, out_dtype=BF16, name="b_w_br_dn")
    d_y_dn = emit("mix", g, d_y_dn)
    d_o_dn, d_z, g["dnw"] = _gdr_out_bwd(o_dn, proj_a, d_y_dn, w["dnw"], w["br_dn"])
    du, dw, dqd, dkt, d_a, d_egl = _gdr_scan_bwd(u, w_, qd, kt, a_mat, gc, states, d_o_dn)
    dq, dk, dv, d_beta, d_gc = _gdr_prep_bwd(qkvn, beta, gc, gc_t, t_fold, u, w_, du, dw, dqd, dkt, d_a)
    d_egl_rows = jnp.pad(d_egl[:, None, :, 0], ((0, 0), (CHUNK - 1, 0), (0, LANES - N_HEADS))).reshape(s_dim, LANES)
    d_ba, g["alog"], g["dtb"] = _gates_bwd(proj_b, w["alog"], w["dtb"], gc, d_beta, d_gc, d_egl_rows)
    d_qkv, g["conv"] = _conv_bwd(proj_a, w["conv"], dq, dk, dv)
    zeros = jnp.zeros((s_dim, WB_CKV - Q_LORA), BF16)
    d_proj_b = jnp.concatenate([d_cq, zeros, d_ckv, d_kr, d_ba], axis=1)
    g["wa_qkv_t"] = _mm(d_qkv, xb, ta=True, name="b_w_qkv")
    g["wa_z_t"] = _mm(d_z, xb, ta=True, name="b_w_z")
    g["wg_t"] = _mm(d_proj_g, xb, ta=True, name="b_w_g")
    g["wb_t"] = _mm(d_proj_b, xb, ta=True, name="b_w_b")
    d_qkv = emit("small", dict(g, loss=loss), emit("w_in", g, d_qkv))
    dx = _input_grad(d_qkv, d_z, d_proj_g, d_proj_b, w["w_in_t"], w["wg_t"], w["wb_t"], dxa)
    return loss, dx, g


DX_ROWS = 512


def _input_proj(x, w_in_t, wg_t, wb_t):
    s_dim = x.shape[0]
    n_a = 4 * D_MODEL

    def body(x_ref, wa_ref, wg_ref, wb_ref, a_ref, g_ref, b_ref, xb_ref):
        xv = x_ref[...].astype(BF16)
        xb_ref[...] = xv
        a_ref[...] = _dot(xv, wa_ref[...], NT)
        g_ref[...] = _dot(xv, wg_ref[...], NT).astype(BF16)
        b_ref[...] = _dot(xv, wb_ref[...], NT)

    rows = lambda width: pl.BlockSpec((DX_ROWS, width), lambda i: (i, 0))
    whole = lambda shape: pl.BlockSpec(shape, lambda i: (0, 0), pipeline_mode=pl.Buffered(1))
    return pl.pallas_call(
        body,
        out_shape=[jax.ShapeDtypeStruct((s_dim, n_a), F32), jax.ShapeDtypeStruct((s_dim, wg_t.shape[0]), BF16),
                   jax.ShapeDtypeStruct((s_dim, wb_t.shape[0]), F32), jax.ShapeDtypeStruct((s_dim, D_MODEL), BF16)],
        grid=(s_dim // DX_ROWS,),
        in_specs=[rows(D_MODEL), whole((n_a, D_MODEL)), whole(wg_t.shape), whole(wb_t.shape)],
        out_specs=[rows(n_a), rows(wg_t.shape[0]), rows(wb_t.shape[0]), rows(D_MODEL)],
        compiler_params=pltpu.CompilerParams(dimension_semantics=("parallel",)),
        name="f_proj",
    )(x, w_in_t, wg_t, wb_t)


def _input_grad(d_qkv, d_z, d_g, d_b, w_in_t, wg_t, wb_t, add):
    s_dim = d_qkv.shape[0]
    n_qkv, n_a = d_qkv.shape[1], d_qkv.shape[1] + d_z.shape[1]

    def body(q_ref, z_ref, g_ref, b_ref, wa_ref, wg_ref, wb_ref, add_ref, o_ref):
        r = add_ref[...] + _dot(q_ref[...], wa_ref[0:n_qkv])
        r = r + _dot(z_ref[...], wa_ref[n_qkv:n_a])
        r = r + _dot(g_ref[...], wg_ref[...])
        o_ref[...] = r + _dot(b_ref[...], wb_ref[...])

    rows = lambda a: pl.BlockSpec((DX_ROWS, a.shape[1]), lambda i: (i, 0))
    whole = lambda shape: pl.BlockSpec(shape, lambda i: (0, 0), pipeline_mode=pl.Buffered(1))
    return pl.pallas_call(
        body,
        out_shape=jax.ShapeDtypeStruct((s_dim, D_MODEL), F32),
        grid=(s_dim // DX_ROWS,),
        in_specs=[rows(d_qkv), rows(d_z), rows(d_g), rows(d_b), whole((n_a, D_MODEL)), whole(wg_t.shape),
                  whole(wb_t.shape), rows(add)],
        out_specs=pl.BlockSpec((DX_ROWS, D_MODEL), lambda i: (i, 0)),
        compiler_params=pltpu.CompilerParams(dimension_semantics=("parallel",)),
        name="b_x",
    )(d_qkv, d_z, d_g, d_b, w_in_t, wg_t, wb_t, add)


_BIG = (("w_in", 1), ("w_uq", 0), ("w_uk", 0), ("w_uv", 0), ("w_br_dn", 0), ("w_br_mla", 0),
        ("w_o", 0), ("w_ffn_in", 1), ("w_ffn_out", 0), ("w_ple", 1), ("w_ple_gate", 0))
_BIG_AXIS = dict(_BIG)
_SMALL = ("ln1_g", "ln1_b", "ln2_g", "ln2_b", "q_norm_w", "kv_norm_w", "dn_norm_w", "dn_a_log", "dn_dt_bias")
_ORDER = ("w_in", "conv_w", "dn_a_log", "dn_dt_bias", "dn_norm_w", "q_norm_w", "w_uq", "kv_norm_w", "w_uk", "w_uv",
          "w_br_dn", "w_br_mla", "w_o", "ln1_g", "ln1_b", "w_ffn_in", "w_ffn_out", "w_ple", "w_ple_gate", "ln2_g",
          "ln2_b")


def _stored_shape(name, shard_shape):
    axis = _BIG_AXIS[name]
    lead = shard_shape[axis]
    return lead, int(np.prod(shard_shape)) // lead


def _to_stored(name, shard):
    return jnp.moveaxis(shard, _BIG_AXIS[name], 0).reshape(_stored_shape(name, shard.shape))


def _from_stored(name, stored, shard_shape):
    axis = _BIG_AXIS[name]
    moved = (shard_shape[axis],) + shard_shape[:axis] + shard_shape[axis + 1:]
    return jnp.moveaxis(stored.reshape(moved), 0, axis)


_W_IN_ROWS = np.cumsum([0, 3072, 1024, 8, 8, Q_LORA, KV_LORA, ROPE, D_MODEL, D_MODEL])


def _first_weights(w_in_t, conv_full, small):
    r = _W_IN_ROWS
    zr = lambda n: jnp.zeros((n, D_MODEL), w_in_t.dtype)
    w = {}
    w["w_in_t"] = w_in_t
    w["wg_t"] = w_in_t[r[7]:r[9]]
    w["wb_t"] = jnp.concatenate([w_in_t[r[4]:r[5]], zr(WB_CKV - Q_LORA), w_in_t[r[5]:r[7]], zr(LANES - ROPE),
                                 w_in_t[r[2]:r[4]], zr(LANES - 2 * N_HEADS)], axis=0)
    w["conv"] = conv_full
    pad_l = lambda v: jnp.pad(v, ((0, 0), (0, LANES - v.shape[1])))
    w["alog"], w["dtb"] = pad_l(small["dn_a_log"]), pad_l(small["dn_dt_bias"])
    w["dnw"], w["qnw"], w["kvnw"] = small["dn_norm_w"], small["q_norm_w"], small["kv_norm_w"]
    w["ln1g"], w["ln1b"], w["ln2g"], w["ln2b"] = small["ln1_g"], small["ln1_b"], small["ln2_g"], small["ln2_b"]
    return w


def _late_weights(group, fw):
    w = {}
    if group == "mix":
        uq = fw["w_uq"].reshape(Q_LORA, N_HEADS, HEAD + ROPE)
        uq_r = jnp.pad(uq[:, :, HEAD:], ((0, 0), (0, 0), (0, HEAD - ROPE)))
        w["uq"] = jnp.concatenate([uq[:, :, :HEAD].reshape(Q_LORA, -1), uq_r.reshape(Q_LORA, -1)], axis=1)
        w["uk"], w["uv"] = fw["w_uk"], fw["w_uv"]
        w["br_dn"], w["br_mla"], w["wo"] = fw["w_br_dn"], fw["w_br_mla"], fw["w_o"]
    else:
        w["ffn_in_t"], w["ffn_out"] = fw["w_ffn_in"], fw["w_ffn_out"]
        w["ple_t"], w["ple_gate"] = fw["w_ple"], fw["w_ple_gate"]
    return w


_GROUP_GRADS = {"ffn": (("w_ple", "ple_t"), ("w_ple_gate", "ple_gate"), ("w_ffn_out", "ffn_out"),
                        ("w_ffn_in", "ffn_in_t")),
                "mix": (("w_o", "wo"), ("w_br_mla", "br_mla"), ("w_uq", "uq"), ("w_uk", "uk"), ("w_uv", "uv"),
                        ("w_br_dn", "br_dn"))}


def _group_grads(group, g):
    out = {}
    for name, key in _GROUP_GRADS[group]:
        t = g[key]
        if name == "w_uq":
            uq_n = t[:, :D_MODEL].reshape(Q_LORA, N_HEADS, HEAD)
            uq_r = t[:, D_MODEL:].reshape(Q_LORA, N_HEADS, HEAD)[:, :, :ROPE]
            t = jnp.concatenate([uq_n, uq_r], axis=2).reshape(Q_LORA, -1)
        out[name] = t
    return out


PACK_ROWS = 512
PACK_BUFFERS, PACK_AHEAD = 6, 3
SUBLANES = 8


def _pack_exchange(parts, name):
    arrays = []
    for a, _, _ in parts:
        if not any(a is b for b in arrays):
            arrays.append(a)
    index = lambda a: next(i for i, b in enumerate(arrays) if a is b)
    chunks, dst = [], 0
    for a, first, rows in parts:
        assert first % SUBLANES == 0 and rows % SUBLANES == 0
        chunks += [(index(a), first + o, dst + o, min(PACK_ROWS, rows - o)) for o in range(0, rows, PACK_ROWS)]
        dst += rows
    c, n, last = arrays[0].shape[1], len(arrays), len(chunks) - 1
    slab = dst // N_DEV
    assert slab * N_DEV == dst

    def body(*refs):
        src_refs, out_ref, recv_ref = refs[:n], refs[n], refs[n + 1]
        buf, sem_in, sem_out, send_sems, recv_sems = refs[n + 2:]
        x, y, core = lax.axis_index("x"), lax.axis_index("y"), lax.axis_index("c")

        def to_sibling(q):
            return pltpu.make_async_remote_copy(
                src_ref=out_ref.at[pl.ds((2 * q + 1 - core) * slab, slab)], dst_ref=recv_ref.at[q],
                send_sem=send_sems.at[q], recv_sem=recv_sems.at[q], device_id=(x, y, 1 - core),
                device_id_type=_MESH_ID)

        sent = [0]

        def send_packed(rows_done):
            while sent[0] < N_DEV // 2 and (2 * sent[0] + 2) * slab <= rows_done:
                to_sibling(sent[0]).start()
                sent[0] += 1

        def load(k):
            i, first, _, rows = chunks[k]
            return pltpu.make_async_copy(src_refs[i].at[pl.ds(first, rows)],
                                         buf.at[k % PACK_BUFFERS, pl.ds(0, rows)], sem_in.at[k % PACK_BUFFERS])

        def store(k):
            _, _, first, rows = chunks[k]
            return pltpu.make_async_copy(buf.at[k % PACK_BUFFERS, pl.ds(0, rows)],
                                         out_ref.at[pl.ds(first, rows), 0, :], sem_out.at[k % PACK_BUFFERS])

        def stored(k):
            store(k).wait()
            send_packed(chunks[k][2] + chunks[k][3])

        for k in range(min(PACK_AHEAD, last + 1)):
            load(k).start()
        for k in range(last + 1):
            load(k).wait()
            store(k).start(priority=1)
            ahead = k + PACK_AHEAD
            if ahead <= last:
                if ahead >= PACK_BUFFERS:
                    stored(ahead - PACK_BUFFERS)
                load(ahead).start()
        for k in range(max(0, last + 1 - PACK_BUFFERS), last + 1):
            stored(k)
        for q in range(N_DEV // 2):
            to_sibling(q).wait_recv()
        for q in range(N_DEV // 2):
            to_sibling(q).wait_send()

    return pl.pallas_call(
        body,
        out_shape=[jax.ShapeDtypeStruct((dst, 1, c), F32), jax.ShapeDtypeStruct((N_DEV // 2, slab, 1, c), F32)],
        in_specs=[_ANY] * n,
        out_specs=[_ANY, _ANY],
        scratch_shapes=[pltpu.VMEM((PACK_BUFFERS, PACK_ROWS, c), F32), pltpu.SemaphoreType.DMA((PACK_BUFFERS,)),
                        pltpu.SemaphoreType.DMA((PACK_BUFFERS,)), pltpu.SemaphoreType.DMA((N_DEV // 2,)),
                        pltpu.SemaphoreType.DMA((N_DEV // 2,))],
        name=name,
    )(*arrays)


def _w_in_grad_parts(g):
    wb = g["wb_t"]
    return [(g["wa_qkv_t"], 0, 3 * D_MODEL), (g["wa_z_t"], 0, D_MODEL), (wb, WB_BA, 2 * N_HEADS),
            (wb, WB_CQ, Q_LORA), (wb, WB_CKV, KV_LORA), (wb, WB_KR, ROPE), (g["wg_t"], 0, 2 * D_MODEL)]


def _small_grads(g):
    return {"ln1_g": g["ln1g"], "ln1_b": g["ln1b"], "ln2_g": g["ln2g"], "ln2_b": g["ln2b"], "q_norm_w": g["qnw"],
            "kv_norm_w": g["kvnw"], "dn_norm_w": g["dnw"], "dn_a_log": g["alog"], "dn_dt_bias": g["dtb"],
            "conv_w": g["conv"]}


_SMALL_SLOTS = {"ln1_g": (0, 0, 1024), "ln1_b": (1, 0, 1024), "ln2_g": (2, 0, 1024), "ln2_b": (3, 0, 1024),
                "q_norm_w": (4, 0, 384), "kv_norm_w": (4, 384, 256), "dn_norm_w": (4, 640, 128),
                "dn_a_log": (4, 768, 8), "dn_dt_bias": (4, 896, 8)}
_SMALL_ROWS, _LOSS_ROW, _CONV_ROW0, _CONV_ROWS = 24, 5, 8, 12


def _pack_small_grads(small_g, loss):
    zeros = lambda r, c: jnp.zeros((r, c), F32)
    row4 = jnp.concatenate([small_g["q_norm_w"], small_g["kv_norm_w"], small_g["dn_norm_w"], small_g["dn_a_log"],
                            small_g["dn_dt_bias"]], axis=1)
    row5 = jnp.concatenate([loss, zeros(1, FLAT_COLS - LANES)], axis=1)
    head = jnp.concatenate([small_g["ln1_g"], small_g["ln1_b"], small_g["ln2_g"], small_g["ln2_b"], row4, row5,
                            zeros(2, FLAT_COLS)], axis=0)
    conv = small_g["conv_w"].reshape(_CONV_ROWS, FLAT_COLS)
    return jnp.concatenate([head, conv, zeros(_SMALL_ROWS - _CONV_ROW0 - _CONV_ROWS, FLAT_COLS)], axis=0)


_MESH_ID = pl.DeviceIdType.MESH
_ANY = pl.BlockSpec(memory_space=pl.ANY)


def _all_gather(blocks, name):
    n = len(blocks)

    def body(*refs):
        x_refs, out_refs = refs[:n], refs[n:2 * n]
        send_sems, recv_sems, local_sems = refs[2 * n:]
        x, y, c = lax.axis_index("x"), lax.axis_index("y"), lax.axis_index("c")
        me, sibling = (x, y, c), (x, y, 1 - c)
        chips = [(1 - x, y), (x, 1 - y), (1 - x, 1 - y)]

        def slot(i, px, py, pc):
            return out_refs[i].at[4 * px + 2 * py + pc]

        def copy(i, k, origin, to, src=None):
            return pltpu.make_async_remote_copy(
                src_ref=slot(i, *origin) if src is None else src, dst_ref=slot(i, *origin),
                send_sem=send_sems.at[7 * i + k], recv_sem=recv_sems.at[7 * i + k], device_id=to,
                device_id_type=_MESH_ID)

        mine = [pltpu.make_async_copy(x_refs[i], slot(i, *me), local_sems.at[i]) for i in range(n)]
        first, passed = [], []
        for i in range(n):
            mine[i].start()
            first.append(copy(i, 0, me, sibling, src=x_refs[i]))
            first += [copy(i, 1 + j, me, (*chip, c), src=x_refs[i]) for j, chip in enumerate(chips)]
        for cp in first:
            cp.start()
        for i in range(n):
            for j, chip in enumerate(chips):
                copy(i, 1 + j, (*chip, c), me).wait_recv()
                passed.append(copy(i, 4 + j, (*chip, c), sibling))
                passed[-1].start()
        for i in range(n):
            copy(i, 0, sibling, me).wait_recv()
            for j, chip in enumerate(chips):
                copy(i, 4 + j, (*chip, 1 - c), me).wait_recv()
        for cp in first + passed:
            cp.wait_send()
        for cp in mine:
            cp.wait()

    return pl.pallas_call(
        body,
        out_shape=[jax.ShapeDtypeStruct((N_DEV,) + b.shape, b.dtype) for b in blocks],
        in_specs=[_ANY] * n,
        out_specs=[_ANY] * n,
        scratch_shapes=[pltpu.SemaphoreType.DMA((7 * n,)), pltpu.SemaphoreType.DMA((7 * n,)),
                        pltpu.SemaphoreType.DMA((n,))],
        name=name,
    )(*blocks)


def _col_tile(c):
    return c if c <= 256 else 256


def _chip_sum(src, recv, where, name):
    _, r, _, c = src.shape
    tc = _col_tile(c)

    def body(w_ref, a_ref, b_ref, o_ref, ob_ref):
        s = a_ref[...] + b_ref[...]
        ob_ref[...] = s.astype(BF16)

        @pl.when(pl.program_id(1) == w_ref[1])
        def _():
            o_ref[...] = s

    rows = lambda f: pl.BlockSpec((None, r, None, tc), f)
    return pl.pallas_call(
        body,
        out_shape=[jax.ShapeDtypeStruct((r, c), F32), jax.ShapeDtypeStruct((4, r, c), BF16)],
        grid_spec=pltpu.PrefetchScalarGridSpec(
            num_scalar_prefetch=1, grid=(c // tc, 4),
            in_specs=[rows(lambda j, q, w: (2 * q + w[0], 0, 0, j)), rows(lambda j, q, w: (q, 0, 0, j))],
            out_specs=[pl.BlockSpec((r, tc), lambda j, q, w: (0, j)),
                       pl.BlockSpec((None, r, tc), lambda j, q, w: (q, 0, j))]),
        compiler_params=pltpu.CompilerParams(dimension_semantics=("parallel", "arbitrary")),
        name=name,
    )(where, src, recv)


_HBM = pl.BlockSpec(memory_space=pltpu.HBM)
_SEM = pl.BlockSpec(memory_space=pltpu.SEMAPHORE)
_DATAFLOW = pltpu.SideEffectType.DATAFLOW_SIDE_EFFECTING
N_PEERS = N_DEV - 1


def _ring_peer(j):
    me = 4 * lax.axis_index("x") + 2 * lax.axis_index("y") + lax.axis_index("c")
    k = (me + j) % N_DEV
    return me, k, (k // 4, (k // 2) % 2, k % 2)


def _spread_copy(i, j, src_refs, land_refs, send_sems, recv_sems, scatter):
    me, k, peer = _ring_peer(j)
    return pltpu.make_async_remote_copy(
        src_ref=src_refs[i].at[k] if scatter else src_refs[i], dst_ref=land_refs[i].at[me],
        send_sem=send_sems.at[N_PEERS * i + j - 1], recv_sem=recv_sems.at[N_PEERS * i + j - 1], device_id=peer,
        device_id_type=_MESH_ID)


def _spread_start(srcs, carry, scatter, name):
    n = len(srcs)
    lands = [lax.empty(((N_DEV,) + s.shape[-2:]), s.dtype) for s in srcs]

    def body(*refs):
        src_refs, land_refs = refs[:n], refs[n:2 * n]
        send_sems, recv_sems, local_sems = refs[2 * n + 1:2 * n + 4]
        for i in range(n):
            for j in range(1, N_DEV):
                _spread_copy(i, j, src_refs, land_refs, send_sems, recv_sems, scatter).start()
        for i in range(n):
            _own_copy(i, src_refs, land_refs, local_sems, scatter).start()

    hbm = lambda a: pltpu.HBM(a.shape, a.dtype)
    sems = pltpu.SemaphoreType.DMA((N_PEERS * n,))
    pinned = [pltpu.with_memory_space_constraint(a, pltpu.HBM) for a in list(srcs) + lands + [carry]]
    res = pl.pallas_call(
        body, name=name,
        out_shape=(sems, sems, pltpu.SemaphoreType.DMA((n,)), *[hbm(a) for a in pinned]),
        in_specs=[_HBM] * (2 * n + 1),
        out_specs=(_SEM, _SEM, _SEM, *[_HBM] * (2 * n + 1)),
        input_output_aliases={i: 3 + i for i in range(2 * n + 1)},
        compiler_params=pltpu.CompilerParams(has_side_effects=_DATAFLOW),
    )(*pinned)
    return res[:3], list(res[3:3 + n]), list(res[3 + n:3 + 2 * n]), res[3 + 2 * n]


def _own_copy(i, src_refs, land_refs, local_sems, scatter):
    me = _ring_peer(0)[0]
    return pltpu.make_async_copy(src_refs[i].at[me] if scatter else src_refs[i], land_refs[i].at[me],
                                 local_sems.at[i])


def _spread_wait(started, after, scatter, name):
    sems, srcs, lands, _ = started
    n = len(srcs)

    def body(*refs):
        src_refs, land_refs = refs[:n], refs[n:2 * n]
        send_s, recv_s, local_s = refs[2 * n:2 * n + 3]
        for i in range(n):
            for j in range(1, N_DEV):
                cp = _spread_copy(i, j, src_refs, land_refs, send_s, recv_s, scatter)
                cp.wait_send()
                cp.wait_recv()
        for i in range(n):
            _own_copy(i, src_refs, land_refs, local_s, scatter).wait()

    hbm = lambda a: pltpu.HBM(a.shape, a.dtype)
    res = pl.pallas_call(
        body, name=name,
        out_shape=tuple(hbm(a) for a in srcs + lands),
        in_specs=[_HBM] * (2 * n) + [_SEM, _SEM, _SEM, pl.BlockSpec(memory_space=pl.ANY)],
        out_specs=tuple([_HBM] * (2 * n)),
        input_output_aliases={i: i for i in range(2 * n)},
        compiler_params=pltpu.CompilerParams(has_side_effects=_DATAFLOW),
    )(*srcs, *lands, *sems, after)
    return list(res[n:])


def _chips_copy(i, j, src_refs, land_refs, send_sems, recv_sems):
    x, y, c = lax.axis_index("x"), lax.axis_index("y"), lax.axis_index("c")
    tx, ty = [(1 - x, y), (x, 1 - y), (1 - x, 1 - y)][j]
    return pltpu.make_async_remote_copy(
        src_ref=src_refs[i].at[2 * tx + ty], dst_ref=land_refs[i].at[j], send_sem=send_sems.at[3 * i + j],
        recv_sem=recv_sems.at[3 * i + j], device_id=(tx, ty, c), device_id_type=_MESH_ID)


def _chips_start(srcs, carry, name):
    n = len(srcs)
    lands = [lax.empty((3,) + s.shape[1:], s.dtype) for s in srcs]

    def body(*refs):
        src_refs, land_refs = refs[:n], refs[n:2 * n]
        send_sems, recv_sems = refs[2 * n + 1:2 * n + 3]
        for i in range(n):
            for j in range(3):
                _chips_copy(i, j, src_refs, land_refs, send_sems, recv_sems).start()

    hbm = lambda a: pltpu.HBM(a.shape, a.dtype)
    sems = pltpu.SemaphoreType.DMA((3 * n,))
    pinned = [pltpu.with_memory_space_constraint(a, pltpu.HBM) for a in list(srcs) + lands + [carry]]
    res = pl.pallas_call(
        body, name=name,
        out_shape=(sems, sems, *[hbm(a) for a in pinned]),
        in_specs=[_HBM] * (2 * n + 1),
        out_specs=(_SEM, _SEM, *[_HBM] * (2 * n + 1)),
        input_output_aliases={i: 2 + i for i in range(2 * n + 1)},
        compiler_params=pltpu.CompilerParams(has_side_effects=_DATAFLOW),
    )(*pinned)
    return res[:2], list(res[2:2 + n]), list(res[2 + n:2 + 2 * n]), res[2 + 2 * n]


def _chips_wait(started, after, name):
    sems, srcs, lands, _ = started
    n = len(srcs)

    def body(*refs):
        src_refs, land_refs = refs[:n], refs[n:2 * n]
        send_s, recv_s = refs[2 * n:2 * n + 2]
        for i in range(n):
            for j in range(3):
                cp = _chips_copy(i, j, src_refs, land_refs, send_s, recv_s)
                cp.wait_send()
                cp.wait_recv()

    hbm = lambda a: pltpu.HBM(a.shape, a.dtype)
    res = pl.pallas_call(
        body, name=name,
        out_shape=tuple(hbm(a) for a in srcs + lands),
        in_specs=[_HBM] * (2 * n) + [_SEM, _SEM, pl.BlockSpec(memory_space=pl.ANY)],
        out_specs=tuple([_HBM] * (2 * n)),
        input_output_aliases={i: i for i in range(2 * n)},
        compiler_params=pltpu.CompilerParams(has_side_effects=_DATAFLOW),
    )(*srcs, *lands, *sems, after)
    return list(res[n:])


def _sum8(landing, name):
    _, r, c = landing.shape
    tc = _col_tile(c)

    def body(a_ref, o_ref):
        tot = a_ref[0].astype(F32)
        for k in range(1, N_DEV):
            tot = tot + a_ref[k].astype(F32)
        o_ref[...] = tot

    return pl.pallas_call(
        body,
        out_shape=jax.ShapeDtypeStruct((r, c), F32),
        grid=(c // tc,),
        in_specs=[pl.BlockSpec((N_DEV, r, tc), lambda j: (0, 0, j))],
        out_specs=pl.BlockSpec((r, tc), lambda j: (0, j)),
        compiler_params=pltpu.CompilerParams(dimension_semantics=("parallel",)),
        name=name,
    )(landing)


def _adamw_math(w, g, m, v):
    m = ADAM_B1 * m + (1.0 - ADAM_B1) * g
    v = ADAM_B2 * v + (1.0 - ADAM_B2) * (g * g)
    m_hat = m / (1.0 - ADAM_B1 ** ADAM_STEP)
    v_hat = v / (1.0 - ADAM_B2 ** ADAM_STEP)
    delta = -ADAM_LR * (m_hat / (jnp.sqrt(v_hat) + ADAM_EPS) + ADAM_WD * w)
    return delta, m, v


def _adamw(w, m, v, g, name):
    r, c = w.shape

    def fn(rows, consts):
        return list(_adamw_math(*rows)), []

    return _rowwise(fn, [w, g, m, v], [], [(c, F32)] * 3, tm=r if r <= 512 else 256, name=name)


def _adamw_sum8(w, m, v, landing, name):
    r, c = w.shape
    tc = _col_tile(c)

    def body(w_ref, m_ref, v_ref, a_ref, g_ref, d_ref, m2_ref, v2_ref):
        g = a_ref[0].astype(F32)
        for k in range(1, N_DEV):
            g = g + a_ref[k].astype(F32)
        delta, m2, v2 = _adamw_math(w_ref[...], g, m_ref[...], v_ref[...])
        g_ref[...] = g
        d_ref[...] = delta
        m2_ref[...] = m2
        v2_ref[...] = v2

    blk = pl.BlockSpec((r, tc), lambda j: (0, j))
    return pl.pallas_call(
        body,
        out_shape=[jax.ShapeDtypeStruct((r, c), F32)] * 4,
        grid=(c // tc,),
        in_specs=[blk, blk, blk, pl.BlockSpec((N_DEV, r, tc), lambda j: (0, 0, j))],
        out_specs=[blk] * 4,
        compiler_params=pltpu.CompilerParams(dimension_semantics=("parallel",)),
        name=name,
    )(w, m, v, landing)


def _adamw_parts(w, m, v, own, others, name):
    r, _, c = w.shape
    tc = _col_tile(c)

    def body(w_ref, m_ref, v_ref, a_ref, b_ref, g_ref, d_ref, m2_ref, v2_ref):
        g = ((a_ref[...] + b_ref[0].astype(F32)) + b_ref[1].astype(F32)) + b_ref[2].astype(F32)
        delta, m2, v2 = _adamw_math(w_ref[...], g, m_ref[...], v_ref[...])
        g_ref[...] = g
        d_ref[...] = delta
        m2_ref[...] = m2
        v2_ref[...] = v2

    row = pl.BlockSpec((r, None, tc), lambda j: (0, 0, j))
    return pl.pallas_call(
        body,
        out_shape=[jax.ShapeDtypeStruct((r, 1, c), F32)] * 4,
        grid=(c // tc,),
        in_specs=[row, row, row, pl.BlockSpec((r, tc), lambda j: (0, j)), pl.BlockSpec((3, r, tc), lambda j: (0, 0, j))],
        out_specs=[row] * 4,
        compiler_params=pltpu.CompilerParams(dimension_semantics=("parallel",)),
        name=name,
    )(w, m, v, own, others)


def _adamw_small(gathered, params):
    ns = len(_SMALL)

    def body(*refs):
        g_ref, p_refs, o_refs = refs[0], refs[1:1 + 3 * ns], refs[1 + 3 * ns:]
        tot = g_ref[0]
        for k in range(1, N_DEV):
            tot = tot + g_ref[k]
        for i, name in enumerate(_SMALL):
            row, lane0, lanes = _SMALL_SLOTS[name]
            g = tot[row:row + 1, lane0:lane0 + lanes]
            w_, m_, v_ = (p_refs[3 * i + j][...] for j in range(3))
            delta, m2, v2 = _adamw_math(w_, g, m_, v_)
            for j, val in enumerate((g, delta, m2, v2)):
                o_refs[4 * i + j][...] = val
        o_refs[4 * ns][...] = tot[_LOSS_ROW:_LOSS_ROW + 1, 0:LANES]
        o_refs[4 * ns + 1][...] = tot[_CONV_ROW0:_CONV_ROW0 + _CONV_ROWS, :]

    out_shape = [jax.ShapeDtypeStruct(w.shape, F32) for (w, _, _) in params for _ in range(4)]
    out_shape += [jax.ShapeDtypeStruct((1, LANES), F32), jax.ShapeDtypeStruct((_CONV_ROWS, FLAT_COLS), F32)]
    flat = [a for wmv in params for a in wmv]
    return pl.pallas_call(body, out_shape=out_shape, name="adamw_small")(gathered, *flat)


def kernel(x, p, positions, w_in, conv_w, dn_a_log, dn_dt_bias, dn_norm_w, q_norm_w, w_uq, kv_norm_w, w_uk, w_uv, w_br_dn, w_br_mla, w_o, ln1_g, ln1_b, w_ffn_in, w_ffn_out, w_ple, w_ple_gate, ln2_g, ln2_b, loss_target, m_w_in, m_conv_w, m_dn_a_log, m_dn_dt_bias, m_dn_norm_w, m_q_norm_w, m_w_uq, m_kv_norm_w, m_w_uk, m_w_uv, m_w_br_dn, m_w_br_mla, m_w_o, m_ln1_g, m_ln1_b, m_w_ffn_in, m_w_ffn_out, m_w_ple, m_w_ple_gate, m_ln2_g, m_ln2_b, v_w_in, v_conv_w, v_dn_a_log, v_dn_dt_bias, v_dn_norm_w, v_q_norm_w, v_w_uq, v_kv_norm_w, v_w_uk, v_w_uv, v_w_br_dn, v_w_br_mla, v_w_o, v_ln1_g, v_ln1_b, v_w_ffn_in, v_w_ffn_out, v_w_ple, v_w_ple_gate, v_ln2_g, v_ln2_b):
    args = dict(locals())
    wts = {n: args[n] for n in _ORDER}
    mom1 = {n: args["m_" + n] for n in _ORDER}
    mom2 = {n: args["v_" + n] for n in _ORDER}
    big_names = [n for n, _ in _BIG]
    shard_shapes = {n: wts[n].shape[1:] for n in big_names}
    c_idx = lax.axis_index("c")
    q_idx = 2 * lax.axis_index("x") + lax.axis_index("y")
    where = jnp.stack([c_idx, q_idx]).astype(jnp.int32)

    stored = {n: _to_stored(n, wts[n][0]).astype(BF16) for n in big_names}
    first = _all_gather([stored["w_in"], conv_w[0]], "ag_first")
    group_names = {grp: [n for n, _ in pairs] for grp, pairs in _GROUP_GRADS.items()}
    carry, gathers = first[0], {}
    for grp in ("mix", "ffn"):
        gathers[grp] = _spread_start([stored[n] for n in group_names[grp]], carry, False, "ag_start_" + grp)
        carry = gathers[grp][3]
    conv_full = jnp.moveaxis(first[1], 0, 1).reshape(conv_w.shape[1], -1)
    small_w = {n: wts[n].astype(F32) for n in _SMALL}
    w = _first_weights(carry.reshape(-1, D_MODEL), conv_full, small_w)

    def late_weights(grp, after):
        got = _spread_wait(gathers[grp], after, False, "ag_wait_" + grp)
        return _late_weights(grp, {n: t.reshape(-1, t.shape[-1]) for n, t in zip(group_names[grp], got)})

    started = {}

    def emit(group, g, carry):
        if group == "w_in":
            rows, cols = _stored_shape("w_in", shard_shapes["w_in"])
            packed, from_sibling = _pack_exchange(_w_in_grad_parts(g), "rs_pack_sibling")
            own, own_bf = _chip_sum(packed.reshape(N_DEV, rows, 1, cols), from_sibling, where, "rs_sum_w_in")
            started["w_in"] = (own, _chips_start([own_bf], carry, "rs_chips_start"))
            return started["w_in"][1][3]
        if group == "small":
            block = _pack_small_grads(_small_grads(g), g["loss"])
            started["small"] = _spread_start([block], carry, False, "ag_start_small")
            return started["small"][3]
        grads = _group_grads(group, g)
        srcs = [grads[n].reshape((N_DEV,) + _stored_shape(n, shard_shapes[n])) for n in grads]
        started[group] = (list(grads), _spread_start(srcs, carry, True, "rs_start_" + group))
        return started[group][1][3]

    s_dim = x.shape[1]
    loss, dx, g = _local_step(x[0], p[0, 0], positions.reshape(s_dim, 1).astype(F32), loss_target[0], w,
                              late_weights, emit)
    own, chips_started = started.pop("w_in")
    small_started = started.pop("small")

    out_g, out_d, out_m, out_v = {}, {}, {}, {}

    def update(n, grad, shp):
        flat2 = (shp[0], int(np.prod(shp[1:])))
        d, m2, v2 = _adamw(wts[n][0].reshape(flat2), mom1[n][0].reshape(flat2), mom2[n][0].reshape(flat2),
                           grad.reshape(flat2), "adamw_" + n)
        out_g[n], out_d[n], out_m[n], out_v[n] = grad, d.reshape(shp), m2.reshape(shp), v2.reshape(shp)

    for group, (names, st) in started.items():
        for n, landing in zip(names, _spread_wait(st, dx, True, "rs_wait_" + group)):
            shp = shard_shapes[n]
            if _BIG_AXIS[n] == 0 or shp[-1] % LANES:
                res = _adamw_sum8(_to_stored(n, wts[n][0]), _to_stored(n, mom1[n][0]), _to_stored(n, mom2[n][0]),
                                  landing, "adamw_" + n)
                out_g[n], out_d[n], out_m[n], out_v[n] = (_from_stored(n, t, shp) for t in res)
                last = res[3]
            else:
                update(n, _from_stored(n, _sum8(landing, "rs_total_" + n), shp), shp)

    from_chips = _chips_wait(chips_started, last, "rs_chips_wait")[0]
    g_small = _spread_wait(small_started, last, False, "ag_wait_small")[0]
    rows_first = lambda a: jnp.transpose(a, (2, 0, 1))
    res = _adamw_parts(rows_first(wts["w_in"]), rows_first(mom1["w_in"]), rows_first(mom2["w_in"]), own, from_chips,
                       "adamw_w_in")
    out_g["w_in"], out_d["w_in"], out_m["w_in"], out_v["w_in"] = (jnp.transpose(t, (1, 2, 0))[0] for t in res)

    res = _adamw_small(g_small, [(wts[n], mom1[n], mom2[n]) for n in _SMALL])
    for i, n in enumerate(_SMALL):
        out_g[n], out_d[n], out_m[n], out_v[n] = res[4 * i:4 * i + 4]
    loss_out = res[4 * len(_SMALL)][0, 0]
    conv_shape = conv_w.shape[1:]
    conv_g = lax.dynamic_slice(res[-1].reshape(conv_shape[0], -1), (0, (2 * q_idx + c_idx) * conv_shape[1]),
                               conv_shape)
    update("conv_w", conv_g, conv_shape)

    expand = lambda d, n: d[n] if n in _SMALL else d[n][None]
    return (loss_out, dx[None], *[expand(out_g, n) for n in _ORDER], *[expand(out_d, n) for n in _ORDER],
            *[expand(out_m, n) for n in _ORDER], *[expand(out_v, n) for n in _ORDER])
```
